```python
import jax, jax.numpy as jnp
from jax import lax
import numpy as np

D_MODEL = 1024
BATCH = 8
SEQ = 2048
DEPTH = 2

PLE_DIM = 256
EPS = 1e-6
A_HEAD_DIM = 128
A_HEADS = (D_MODEL // 2) // A_HEAD_DIM
A_DIM = A_HEADS * A_HEAD_DIM
QKV_CONV_WIDTH = 4
CHUNK = 64
POOL_WINDOWS = (2, 4, 8, 16)
POOL_GROUPS = len(POOL_WINDOWS)
POOL_DIM = D_MODEL // 4
POOL_GROUP_DIM = POOL_DIM // POOL_GROUPS
CONV_HEADS = 4
CONV_DIM = D_MODEL // 4
CONV_WIDTH = 3
D_MIX = A_DIM + POOL_DIM + CONV_DIM
IN_SIZES = (A_DIM, A_DIM, A_DIM, A_DIM, A_HEADS, A_HEADS, POOL_DIM, CONV_DIM, CONV_DIM, CONV_DIM)
D_IN = sum(IN_SIZES)
D_FF = -(-8 * D_MODEL // (3 * 256)) * 256

kernel_name = 'hybrid_parallel_deltanet_pool_shortconv'


def rms_norm(x, g):
    xf = x.astype(jnp.float32)
    y = xf * lax.rsqrt(jnp.mean(xf * xf, axis=-1, keepdims=True) + EPS)
    return (y * g.astype(jnp.float32)).astype(x.dtype)


def causal_dwconv(x, w):
    K, C = w.shape
    return lax.conv_general_dilated(
        x, w[:, None, :].astype(x.dtype), window_strides=(1,), padding=[(K - 1, 0)],
        dimension_numbers=('NWC', 'WIO', 'NWC'), feature_group_count=C)


def l2norm(t):
    return t * lax.rsqrt(jnp.sum(t * t, axis=-1, keepdims=True) + EPS)


def chunk_gated_delta_rule(q, k, v, g, beta):
    Bn, S, H, DK = q.shape
    DV = v.shape[-1]
    N = S // CHUNK

    def to_chunks(t):
        t = t.reshape((Bn, N, CHUNK, H) + t.shape[3:])
        return jnp.moveaxis(t, 3, 1)

    q = to_chunks(q * (DK ** -0.5))
    k = to_chunks(k)
    v = to_chunks(v)
    g = to_chunks(g)
    beta = to_chunks(beta)
    gc = jnp.cumsum(g, axis=-1)
    kb = k * beta[..., None]
    vb = v * beta[..., None]
    causal_incl = jnp.tril(jnp.ones((CHUNK, CHUNK), dtype=bool))
    causal_strict = jnp.tril(jnp.ones((CHUNK, CHUNK), dtype=bool), -1)
    diff = gc[..., :, None] - gc[..., None, :]
    decay = jnp.exp(jnp.where(causal_incl, diff, -jnp.inf))
    lower = jnp.where(causal_strict, jnp.einsum('bhncd,bhnsd->bhncs', kb, k) * decay, 0.0)
    eye = jnp.eye(CHUNK, dtype=jnp.float32)
    tmat = lax.linalg.triangular_solve(eye + lower, jnp.broadcast_to(eye, lower.shape),
                                       left_side=True, lower=True, unit_diagonal=True)
    u = jnp.einsum('bhncs,bhnsv->bhncv', tmat, vb)
    w = jnp.einsum('bhncs,bhnsd->bhncd', tmat, kb * jnp.exp(gc)[..., None])
    attn = jnp.einsum('bhncd,bhnsd->bhncs', q, k) * decay

    def step(state, inp):
        q_i, k_i, u_i, w_i, gc_i, a_i = inp
        v_new = u_i - jnp.einsum('bhck,bhkv->bhcv', w_i, state)
        o_i = (jnp.einsum('bhck,bhkv->bhcv', q_i * jnp.exp(gc_i)[..., None], state)
               + jnp.einsum('bhcs,bhsv->bhcv', a_i, v_new))
        g_last = gc_i[..., -1]
        state = (state * jnp.exp(g_last)[..., None, None]
                 + jnp.einsum('bhck,bhcv->bhkv', k_i * jnp.exp(g_last[..., None] - gc_i)[..., None], v_new))
        return state, o_i

    xs = tuple(jnp.moveaxis(t, 2, 0) for t in (q, k, u, w, gc, attn))
    state0 = jnp.zeros((Bn, H, DK, DV), jnp.float32)
    _, o = lax.scan(step, state0, xs)
    return jnp.transpose(o, (1, 0, 3, 2, 4)).reshape(Bn, S, H, DV)


def gated_deltanet(q, k, v, z, a, b, conv_w, a_log, dt_bias, onorm_g):
    Bn, S, _ = q.shape
    qkv = jax.nn.silu(causal_dwconv(jnp.concatenate([q, k, v], axis=-1), conv_w))
    q, k, v = jnp.split(qkv.astype(jnp.float32), 3, axis=-1)
    hs = (Bn, S, A_HEADS, A_HEAD_DIM)
    q = l2norm(q.reshape(hs))
    k = l2norm(k.reshape(hs))
    v = v.reshape(hs)
    beta = jax.nn.sigmoid(b.astype(jnp.float32))
    g = -jnp.exp(a_log.astype(jnp.float32)) * jax.nn.softplus(a.astype(jnp.float32) + dt_bias.astype(jnp.float32))
    o = chunk_gated_delta_rule(q, k, v, g, beta)
    o = o * lax.rsqrt(jnp.mean(o * o, axis=-1, keepdims=True) + EPS) * onorm_g.astype(jnp.float32)
    o = o * jax.nn.silu(z.astype(jnp.float32).reshape(hs))
    return o.reshape(Bn, S, A_DIM).astype(z.dtype)


def multiscale_pool(h, pool_w, pool_scale):
    Bn, S, _ = h.shape
    hf = h.astype(jnp.float32)
    cs = jnp.cumsum(hf, axis=1)
    count = jnp.arange(1, S + 1, dtype=jnp.float32)[:, None]
    outs = []
    for gi, win in enumerate(POOL_WINDOWS):
        sl = slice(gi * POOL_GROUP_DIM, (gi + 1) * POOL_GROUP_DIM)
        csg = cs[..., sl]
        lag = jnp.pad(csg, ((0, 0), (win, 0), (0, 0)))[:, :S]
        mean = (csg - lag) / jnp.minimum(count, float(win))
        outs.append(mean - hf[..., sl])
    pooled = jnp.stack(outs, axis=2)
    y = jnp.einsum('bsgc,gcd->bsgd', pooled, pool_w.astype(jnp.float32)).reshape(Bn, S, POOL_DIM)
    return (y * pool_scale.astype(jnp.float32)).astype(h.dtype)


def short_gated_conv(gate_b, gate_c, hc, conv_w):
    return gate_b * causal_dwconv(gate_c * hc, conv_w)


def _fwd_setup_inputs(seed: int = 0) -> dict:
    key = jax.random.key(seed)
    ks = jax.random.split(key, 24)
    f32 = jnp.float32
    nrm = lambda k, shape, scale: jax.random.normal(k, shape, f32) * scale
    dt = jnp.exp(jax.random.uniform(ks[5], (DEPTH, A_HEADS), f32, np.log(1e-3), np.log(1e-1)))
    return {
        'x': nrm(ks[0], (BATCH, SEQ, D_MODEL), 1.0),
        'p': nrm(ks[1], (DEPTH, BATCH, SEQ, PLE_DIM), 1.0),
        'norm1_g': 1.0 + nrm(ks[2], (DEPTH, D_MODEL), 0.02),
        'w_in': nrm(ks[3], (DEPTH, D_MODEL, D_IN), D_MODEL ** -0.5),
        'conv_qkv': nrm(ks[4], (DEPTH, QKV_CONV_WIDTH, 3 * A_DIM), QKV_CONV_WIDTH ** -0.5),
        'a_log': jnp.log(jax.random.uniform(ks[6], (DEPTH, A_HEADS), f32, 1.0, 16.0)),
        'dt_bias': jnp.log(jnp.expm1(dt)),
        'onorm_g': 1.0 + nrm(ks[7], (DEPTH, A_HEAD_DIM), 0.02),
        'pool_w': nrm(ks[8], (DEPTH, POOL_GROUPS, POOL_GROUP_DIM, POOL_GROUP_DIM), POOL_GROUP_DIM ** -0.5),
        'pool_scale': 1.0 + nrm(ks[9], (DEPTH, POOL_DIM), 0.02),
        'sconv_w': nrm(ks[10], (DEPTH, CONV_WIDTH, CONV_DIM), CONV_WIDTH ** -0.5),
        'w_out': nrm(ks[11], (DEPTH, D_MIX, D_MODEL), D_MIX ** -0.5),
        'norm2_g': 1.0 + nrm(ks[12], (DEPTH, D_MODEL), 0.02),
        'w_gate': nrm(ks[13], (DEPTH, D_MODEL, D_FF), D_MODEL ** -0.5),
        'w_up': nrm(ks[14], (DEPTH, D_MODEL, D_FF), D_MODEL ** -0.5),
        'w_down': nrm(ks[15], (DEPTH, D_FF, D_MODEL), D_FF ** -0.5),
        'ple_proj': nrm(ks[16], (DEPTH, PLE_DIM, D_MODEL), PLE_DIM ** -0.5),
        'ple_gate': nrm(ks[17], (DEPTH, D_MODEL, D_MODEL), D_MODEL ** -0.5),
        'final_g': 1.0 + nrm(ks[18], (D_MODEL,), 0.02),
    }


def _fwd_reference(x, p, norm1_g, w_in, conv_qkv, a_log, dt_bias, onorm_g, pool_w, pool_scale,
              sconv_w, w_out, norm2_g, w_gate, w_up, w_down, ple_proj, ple_gate, final_g):
    offsets = [0]
    for s in IN_SIZES[:-1]:
        offsets.append(offsets[-1] + s)
    for i in range(DEPTH):
        h = rms_norm(x, norm1_g[i])
        proj = jnp.einsum('bsd,de->bse', h, w_in[i])
        q, k, v, z, a, b, hp, cb, cc, ch = jnp.split(proj, offsets[1:], axis=-1)
        o_a = gated_deltanet(q, k, v, z, a, b, conv_qkv[i], a_log[i], dt_bias[i], onorm_g[i])
        o_b = multiscale_pool(hp, pool_w[i], pool_scale[i])
        o_c = short_gated_conv(cb, cc, ch, sconv_w[i])
        mixed = jnp.concatenate([o_a, o_b, o_c], axis=-1)
        x = x + jnp.einsum('bse,ed->bsd', mixed, w_out[i])
        h = rms_norm(x, norm2_g[i])
        ff = jax.nn.silu(jnp.einsum('bsd,df->bsf', h, w_gate[i])) * jnp.einsum('bsd,df->bsf', h, w_up[i])
        x = x + jnp.einsum('bsf,fd->bsd', ff, w_down[i])
        gate = jax.nn.sigmoid(jnp.einsum('bsd,de->bse', x, ple_gate[i]).astype(jnp.float32)).astype(x.dtype)
        x = x + gate * jnp.einsum('bsq,qd->bsd', p[i], ple_proj[i])
    return rms_norm(x, final_g)


import jax as _jax
import jax.numpy as _jnp

TWIN_FORMAT = 'train_step'
FWD_PARAMS = ['x', 'p', 'norm1_g', 'w_in', 'conv_qkv', 'a_log', 'dt_bias', 'onorm_g', 'pool_w', 'pool_scale', 'sconv_w', 'w_out', 'norm2_g', 'w_gate', 'w_up', 'w_down', 'ple_proj', 'ple_gate', 'final_g']
TWIN_WEIGHTS = ['norm1_g', 'w_in', 'conv_qkv', 'a_log', 'dt_bias', 'onorm_g', 'pool_w', 'pool_scale', 'sconv_w', 'w_out', 'norm2_g', 'w_gate', 'w_up', 'w_down', 'ple_proj', 'ple_gate', 'final_g']
TWIN_DIFF_INPUT = 'x'
TWIN_INPUTS = ['x', 'p', 'norm1_g', 'w_in', 'conv_qkv', 'a_log', 'dt_bias', 'onorm_g', 'pool_w', 'pool_scale', 'sconv_w', 'w_out', 'norm2_g', 'w_gate', 'w_up', 'w_down', 'ple_proj', 'ple_gate', 'final_g', 'loss_target', 'm_norm1_g', 'm_w_in', 'm_conv_qkv', 'm_a_log', 'm_dt_bias', 'm_onorm_g', 'm_pool_w', 'm_pool_scale', 'm_sconv_w', 'm_w_out', 'm_norm2_g', 'm_w_gate', 'm_w_up', 'm_w_down', 'm_ple_proj', 'm_ple_gate', 'm_final_g', 'v_norm1_g', 'v_w_in', 'v_conv_qkv', 'v_a_log', 'v_dt_bias', 'v_onorm_g', 'v_pool_w', 'v_pool_scale', 'v_sconv_w', 'v_w_out', 'v_norm2_g', 'v_w_gate', 'v_w_up', 'v_w_down', 'v_ple_proj', 'v_ple_gate', 'v_final_g']
TWIN_OUTPUTS = ['loss', 'grad_x', 'grad_norm1_g', 'grad_w_in', 'grad_conv_qkv', 'grad_a_log', 'grad_dt_bias', 'grad_onorm_g', 'grad_pool_w', 'grad_pool_scale', 'grad_sconv_w', 'grad_w_out', 'grad_norm2_g', 'grad_w_gate', 'grad_w_up', 'grad_w_down', 'grad_ple_proj', 'grad_ple_gate', 'grad_final_g', 'delta_norm1_g', 'delta_w_in', 'delta_conv_qkv', 'delta_a_log', 'delta_dt_bias', 'delta_onorm_g', 'delta_pool_w', 'delta_pool_scale', 'delta_sconv_w', 'delta_w_out', 'delta_norm2_g', 'delta_w_gate', 'delta_w_up', 'delta_w_down', 'delta_ple_proj', 'delta_ple_gate', 'delta_final_g', 'new_m_norm1_g', 'new_m_w_in', 'new_m_conv_qkv', 'new_m_a_log', 'new_m_dt_bias', 'new_m_onorm_g', 'new_m_pool_w', 'new_m_pool_scale', 'new_m_sconv_w', 'new_m_w_out', 'new_m_norm2_g', 'new_m_w_gate', 'new_m_w_up', 'new_m_w_down', 'new_m_ple_proj', 'new_m_ple_gate', 'new_m_final_g', 'new_v_norm1_g', 'new_v_w_in', 'new_v_conv_qkv', 'new_v_a_log', 'new_v_dt_bias', 'new_v_onorm_g', 'new_v_pool_w', 'new_v_pool_scale', 'new_v_sconv_w', 'new_v_w_out', 'new_v_norm2_g', 'new_v_w_gate', 'new_v_w_up', 'new_v_w_down', 'new_v_ple_proj', 'new_v_ple_gate', 'new_v_final_g']
TWIN_LEAF_KINDS = {'loss': 'loss', 'grad_x': 'grad_x', 'grad_norm1_g': 'grad_w', 'grad_w_in': 'grad_w', 'grad_conv_qkv': 'grad_w', 'grad_a_log': 'grad_w', 'grad_dt_bias': 'grad_w', 'grad_onorm_g': 'grad_w', 'grad_pool_w': 'grad_w', 'grad_pool_scale': 'grad_w', 'grad_sconv_w': 'grad_w', 'grad_w_out': 'grad_w', 'grad_norm2_g': 'grad_w', 'grad_w_gate': 'grad_w', 'grad_w_up': 'grad_w', 'grad_w_down': 'grad_w', 'grad_ple_proj': 'grad_w', 'grad_ple_gate': 'grad_w', 'grad_final_g': 'grad_w', 'delta_norm1_g': 'delta_w', 'delta_w_in': 'delta_w', 'delta_conv_qkv': 'delta_w', 'delta_a_log': 'delta_w', 'delta_dt_bias': 'delta_w', 'delta_onorm_g': 'delta_w', 'delta_pool_w': 'delta_w', 'delta_pool_scale': 'delta_w', 'delta_sconv_w': 'delta_w', 'delta_w_out': 'delta_w', 'delta_norm2_g': 'delta_w', 'delta_w_gate': 'delta_w', 'delta_w_up': 'delta_w', 'delta_w_down': 'delta_w', 'delta_ple_proj': 'delta_w', 'delta_ple_gate': 'delta_w', 'delta_final_g': 'delta_w', 'new_m_norm1_g': 'new_m', 'new_m_w_in': 'new_m', 'new_m_conv_qkv': 'new_m', 'new_m_a_log': 'new_m', 'new_m_dt_bias': 'new_m', 'new_m_onorm_g': 'new_m', 'new_m_pool_w': 'new_m', 'new_m_pool_scale': 'new_m', 'new_m_sconv_w': 'new_m', 'new_m_w_out': 'new_m', 'new_m_norm2_g': 'new_m', 'new_m_w_gate': 'new_m', 'new_m_w_up': 'new_m', 'new_m_w_down': 'new_m', 'new_m_ple_proj': 'new_m', 'new_m_ple_gate': 'new_m', 'new_m_final_g': 'new_m', 'new_v_norm1_g': 'new_v', 'new_v_w_in': 'new_v', 'new_v_conv_qkv': 'new_v', 'new_v_a_log': 'new_v', 'new_v_dt_bias': 'new_v', 'new_v_onorm_g': 'new_v', 'new_v_pool_w': 'new_v', 'new_v_pool_scale': 'new_v', 'new_v_sconv_w': 'new_v', 'new_v_w_out': 'new_v', 'new_v_norm2_g': 'new_v', 'new_v_w_gate': 'new_v', 'new_v_w_up': 'new_v', 'new_v_w_down': 'new_v', 'new_v_ple_proj': 'new_v', 'new_v_ple_gate': 'new_v', 'new_v_final_g': 'new_v'}


def _forward(args):
    return _fwd_reference(*[args[k] for k in FWD_PARAMS])


def _output_shape():
    out = _jax.eval_shape(lambda: _forward(_fwd_setup_inputs(0)))
    return out.shape, out.dtype

N_MICROBATCH = 1
ADAM_LR = 0.001
ADAM_B1 = 0.9
ADAM_B2 = 0.999
ADAM_EPS = 1e-08
ADAM_WD = 0.01
ADAM_STEP = 10
PER_EXAMPLE_BATCH_AXIS = {'x': 0, 'p': 1, 'loss_target': 0}
SHARED_INPUTS = []
_WEIGHT_DTYPES = {'norm1_g': _jnp.float32, 'w_in': _jnp.float32, 'conv_qkv': _jnp.float32, 'a_log': _jnp.float32, 'dt_bias': _jnp.float32, 'onorm_g': _jnp.float32, 'pool_w': _jnp.float32, 'pool_scale': _jnp.float32, 'sconv_w': _jnp.float32, 'w_out': _jnp.float32, 'norm2_g': _jnp.float32, 'w_gate': _jnp.float32, 'w_up': _jnp.float32, 'w_down': _jnp.float32, 'ple_proj': _jnp.float32, 'ple_gate': _jnp.float32, 'final_g': _jnp.float32}
MOMENT_SCALE = {'norm1_g': 1.178152e-01, 'w_in': 6.870535e-02, 'conv_qkv': 4.545107e-02, 'a_log': 2.684387e-01, 'dt_bias': 2.587902e-01, 'onorm_g': 1.125984e-01, 'pool_w': 8.785388e-02, 'pool_scale': 8.643314e-02, 'sconv_w': 1.025280e-01, 'w_out': 7.712862e-02, 'norm2_g': 7.526125e-02, 'w_gate': 3.131271e-02, 'w_up': 3.038957e-02, 'w_down': 5.024849e-02, 'ple_proj': 4.718040e-02, 'ple_gate': 2.363315e-02, 'final_g': 1.602154e+01}


def _to_microbatches(a, axis):
    t = _jnp.moveaxis(a, axis, 0)
    t = t.reshape((N_MICROBATCH, t.shape[0] // N_MICROBATCH) + t.shape[1:])
    return _jnp.moveaxis(t, 1, axis + 1)


def setup_inputs(seed: int = 0) -> dict:
    inp = _fwd_setup_inputs(seed)
    key = _jax.random.fold_in(_jax.random.key(seed), 7919)
    shape, _ = _output_shape()
    out = dict(inp)
    out["loss_target"] = _jax.random.normal(_jax.random.fold_in(key, 0), shape, _jnp.float32)
    for i, name in enumerate(TWIN_WEIGHTS):
        w = inp[name].astype(_jnp.float32)
        if MOMENT_SCALE is None:
            s = _jnp.sqrt(_jnp.mean(_jnp.square(w)) + 1e-30)
        else:
            s = MOMENT_SCALE[name]
        km, kv = _jax.random.split(_jax.random.fold_in(key, i + 1))
        out[name] = w
        out["m_" + name] = s * _jax.random.normal(km, w.shape, _jnp.float32)
        out["v_" + name] = (s * s) * _jax.random.uniform(kv, w.shape, _jnp.float32, 0.5, 1.5)
    if N_MICROBATCH > 1:
        for name, axis in PER_EXAMPLE_BATCH_AXIS.items():
            out[name] = _to_microbatches(out[name], axis)
    return {'x': out['x'], 'p': out['p'], 'norm1_g': out['norm1_g'], 'w_in': out['w_in'], 'conv_qkv': out['conv_qkv'], 'a_log': out['a_log'], 'dt_bias': out['dt_bias'], 'onorm_g': out['onorm_g'], 'pool_w': out['pool_w'], 'pool_scale': out['pool_scale'], 'sconv_w': out['sconv_w'], 'w_out': out['w_out'], 'norm2_g': out['norm2_g'], 'w_gate': out['w_gate'], 'w_up': out['w_up'], 'w_down': out['w_down'], 'ple_proj': out['ple_proj'], 'ple_gate': out['ple_gate'], 'final_g': out['final_g'], 'loss_target': out['loss_target'], 'm_norm1_g': out['m_norm1_g'], 'm_w_in': out['m_w_in'], 'm_conv_qkv': out['m_conv_qkv'], 'm_a_log': out['m_a_log'], 'm_dt_bias': out['m_dt_bias'], 'm_onorm_g': out['m_onorm_g'], 'm_pool_w': out['m_pool_w'], 'm_pool_scale': out['m_pool_scale'], 'm_sconv_w': out['m_sconv_w'], 'm_w_out': out['m_w_out'], 'm_norm2_g': out['m_norm2_g'], 'm_w_gate': out['m_w_gate'], 'm_w_up': out['m_w_up'], 'm_w_down': out['m_w_down'], 'm_ple_proj': out['m_ple_proj'], 'm_ple_gate': out['m_ple_gate'], 'm_final_g': out['m_final_g'], 'v_norm1_g': out['v_norm1_g'], 'v_w_in': out['v_w_in'], 'v_conv_qkv': out['v_conv_qkv'], 'v_a_log': out['v_a_log'], 'v_dt_bias': out['v_dt_bias'], 'v_onorm_g': out['v_onorm_g'], 'v_pool_w': out['v_pool_w'], 'v_pool_scale': out['v_pool_scale'], 'v_sconv_w': out['v_sconv_w'], 'v_w_out': out['v_w_out'], 'v_norm2_g': out['v_norm2_g'], 'v_w_gate': out['v_w_gate'], 'v_w_up': out['v_w_up'], 'v_w_down': out['v_w_down'], 'v_ple_proj': out['v_ple_proj'], 'v_ple_gate': out['v_ple_gate'], 'v_final_g': out['v_final_g']}


def _loss(weights, diff, rest, loss_target):
    with _jax.named_scope("forward"):
        args = {**rest, TWIN_DIFF_INPUT: diff, **{k: w.astype(_WEIGHT_DTYPES[k]) for k, w in weights.items()}}
        y = _forward(args)
    with _jax.named_scope("loss_head"):
        err = _jnp.square(y.astype(_jnp.float32) - loss_target)
        return 0.5 * _jnp.sum(_jnp.mean(err, axis=-1)) if err.ndim else 0.5 * err


def _adamw(w, g, m, v):
    m = ADAM_B1 * m + (1.0 - ADAM_B1) * g
    v = ADAM_B2 * v + (1.0 - ADAM_B2) * _jnp.square(g)
    m_hat = m / (1.0 - ADAM_B1 ** ADAM_STEP)
    v_hat = v / (1.0 - ADAM_B2 ** ADAM_STEP)
    delta = -ADAM_LR * (m_hat / (_jnp.sqrt(v_hat) + ADAM_EPS) + ADAM_WD * w)
    return delta, m, v


def reference(x, p, norm1_g, w_in, conv_qkv, a_log, dt_bias, onorm_g, pool_w, pool_scale, sconv_w, w_out, norm2_g, w_gate, w_up, w_down, ple_proj, ple_gate, final_g, loss_target, m_norm1_g, m_w_in, m_conv_qkv, m_a_log, m_dt_bias, m_onorm_g, m_pool_w, m_pool_scale, m_sconv_w, m_w_out, m_norm2_g, m_w_gate, m_w_up, m_w_down, m_ple_proj, m_ple_gate, m_final_g, v_norm1_g, v_w_in, v_conv_qkv, v_a_log, v_dt_bias, v_onorm_g, v_pool_w, v_pool_scale, v_sconv_w, v_w_out, v_norm2_g, v_w_gate, v_w_up, v_w_down, v_ple_proj, v_ple_gate, v_final_g):
    given = dict(x=x, p=p, norm1_g=norm1_g, w_in=w_in, conv_qkv=conv_qkv, a_log=a_log, dt_bias=dt_bias, onorm_g=onorm_g, pool_w=pool_w, pool_scale=pool_scale, sconv_w=sconv_w, w_out=w_out, norm2_g=norm2_g, w_gate=w_gate, w_up=w_up, w_down=w_down, ple_proj=ple_proj, ple_gate=ple_gate, final_g=final_g, loss_target=loss_target, m_norm1_g=m_norm1_g, m_w_in=m_w_in, m_conv_qkv=m_conv_qkv, m_a_log=m_a_log, m_dt_bias=m_dt_bias, m_onorm_g=m_onorm_g, m_pool_w=m_pool_w, m_pool_scale=m_pool_scale, m_sconv_w=m_sconv_w, m_w_out=m_w_out, m_norm2_g=m_norm2_g, m_w_gate=m_w_gate, m_w_up=m_w_up, m_w_down=m_w_down, m_ple_proj=m_ple_proj, m_ple_gate=m_ple_gate, m_final_g=m_final_g, v_norm1_g=v_norm1_g, v_w_in=v_w_in, v_conv_qkv=v_conv_qkv, v_a_log=v_a_log, v_dt_bias=v_dt_bias, v_onorm_g=v_onorm_g, v_pool_w=v_pool_w, v_pool_scale=v_pool_scale, v_sconv_w=v_sconv_w, v_w_out=v_w_out, v_norm2_g=v_norm2_g, v_w_gate=v_w_gate, v_w_up=v_w_up, v_w_down=v_w_down, v_ple_proj=v_ple_proj, v_ple_gate=v_ple_gate, v_final_g=v_final_g)
    weights = {n: given[n] for n in TWIN_WEIGHTS}
    shared = {n: given[n] for n in SHARED_INPUTS}
    per_example = {n: given[n] for n in ['x', 'p']}
    grad_fn = _jax.value_and_grad(_loss, argnums=(0, 1))

    def one_microbatch(ex, loss_target):
        ex = dict(ex)
        diff = ex.pop(TWIN_DIFF_INPUT)
        return grad_fn(weights, diff, {**shared, **ex}, loss_target)

    if N_MICROBATCH == 1:
        loss, (grad_w, grad_x) = one_microbatch(per_example, given["loss_target"])
    else:
        def body(carry, xs):
            loss_sum, grad_sum = carry
            l_k, (gw_k, gx_k) = one_microbatch(xs[0], xs[1])
            with _jax.named_scope("update"):
                return (loss_sum + l_k, _jax.tree.map(_jnp.add, grad_sum, gw_k)), gx_k

        init = (_jnp.zeros((), _jnp.float32), _jax.tree.map(_jnp.zeros_like, weights))
        (loss, grad_w), grad_x = _jax.lax.scan(body, init, (per_example, given["loss_target"]))
    with _jax.named_scope("update"):
        delta_w, new_m, new_v = {}, {}, {}
        for n in TWIN_WEIGHTS:
            delta_w[n], new_m[n], new_v[n] = _adamw(weights[n], grad_w[n], given["m_" + n], given["v_" + n])
    return (loss, grad_x, *[grad_w[n] for n in TWIN_WEIGHTS], *[delta_w[n] for n in TWIN_WEIGHTS],
            *[new_m[n] for n in TWIN_WEIGHTS], *[new_v[n] for n in TWIN_WEIGHTS])
```

```python
import functools

import jax
import jax.numpy as jnp
from jax import lax
from jax.experimental import pallas as pl
from jax.experimental.pallas import tpu as pltpu

F32 = jnp.float32
BF16 = jnp.bfloat16

D_MODEL = 1024
DEPTH = 2
PLE_DIM = 256
EPS = 1e-6
HEAD_DIM = 128
HEADS = 4
A_DIM = HEADS * HEAD_DIM
QKV_TAPS = 4
CHUNK = 64
POOL_WINDOWS = (2, 4, 8, 16)
POOL_DIM = 256
CONV_DIM = 256
CONV_TAPS = 3
D_FF = 2816
D_IN = 3080
D_IN_PAD = 3200
AB_COL = 2048
N_DEV = 8

ADAM_LR = 0.001
ADAM_B1 = 0.9
ADAM_B2 = 0.999
ADAM_EPS = 1e-08
ADAM_WD = 0.01
ADAM_STEP = 10

LANE = 128
SUBLANE = 8
VMEM_BYTES_V7X = 64 * 1024 * 1024
VMEM_LIMIT = 48 * 1024 * 1024

_HI = lax.Precision.HIGHEST
NN = ((1,), (0,))
NT = ((1,), (1,))
TN = ((0,), (0,))
MESH = pl.DeviceIdType.MESH


def _dot(a, b, dims, hi=False):
    if hi:
        return lax.dot_general(a, b, (dims, ((), ())), precision=_HI, preferred_element_type=F32)
    return lax.dot_general(a.astype(BF16), b.astype(BF16), (dims, ((), ())), preferred_element_type=F32)


def _pcall(body, *, name, out_shape, grid=(), in_specs=None, out_specs=None, scratch_shapes=(), semantics=None,
           vmem_limit=None, **kw):
    params = {}
    if semantics is not None:
        params["dimension_semantics"] = semantics
    if vmem_limit is not None:
        params["vmem_limit_bytes"] = vmem_limit
    return pl.pallas_call(
        body, name=name, out_shape=out_shape, grid=grid, in_specs=in_specs, out_specs=out_specs,
        scratch_shapes=list(scratch_shapes), compiler_params=pltpu.CompilerParams(**params), **kw)


def _sigmoid(x):
    return 1.0 / (1.0 + jnp.exp(-x))


def _softplus(x):
    return jnp.maximum(x, 0.0) + jnp.log(1.0 + jnp.exp(-jnp.abs(x)))


def _tile(n, cap, mult):
    if n <= cap:
        return n
    best = None
    for t in range(mult, cap + 1, mult):
        if n % t == 0:
            best = t
    assert best is not None, (n, cap, mult)
    return best


def _matmul(a, b, mode, *, name, res=None, out_dtype=F32):
    if mode == "nn":
        (m, k), (k2, n) = a.shape, b.shape
    elif mode == "nt":
        (m, k), (n, k2) = a.shape, b.shape
    else:
        (k, m), (k2, n) = a.shape, b.shape
    assert k == k2, (a.shape, b.shape, mode)
    tm = _tile(m, 512, LANE if mode == "tn" else 16)
    tn = _tile(n, 640, LANE)
    tk = _tile(k, 3200, LANE)
    nk = k // tk
    dims = {"nn": NN, "nt": NT, "tn": TN}[mode]
    a_spec = pl.BlockSpec((tk, tm), lambda i, j, kk: (kk, i)) if mode == "tn" else pl.BlockSpec((tm, tk), lambda i, j, kk: (i, kk))
    b_spec = pl.BlockSpec((tn, tk), lambda i, j, kk: (j, kk)) if mode == "nt" else pl.BlockSpec((tk, tn), lambda i, j, kk: (kk, j))
    o_spec = pl.BlockSpec((tm, tn), lambda i, j, kk: (i, j))
    has_res = res is not None

    def body(*refs):
        a_ref, b_ref = refs[0], refs[1]
        res_ref = refs[2] if has_res else None
        o_ref = refs[2 + has_res]
        part = _dot(a_ref[...], b_ref[...], dims)
        if nk == 1:
            if has_res:
                part = part + res_ref[...]
            o_ref[...] = part.astype(o_ref.dtype)
            return
        acc_ref = refs[3 + has_res]
        kk = pl.program_id(2)

        @pl.when(kk == 0)
        def _():
            acc_ref[...] = part

        @pl.when(kk > 0)
        def _():
            acc_ref[...] += part

        @pl.when(kk == nk - 1)
        def _():
            total = acc_ref[...]
            if has_res:
                total = total + res_ref[...]
            o_ref[...] = total.astype(o_ref.dtype)

    ins = [a, b] + ([res] if has_res else [])
    specs = [a_spec, b_spec] + ([o_spec] if has_res else [])
    return _pcall(body, name=name, out_shape=jax.ShapeDtypeStruct((m, n), out_dtype), grid=(m // tm, n // tn, nk),
                  in_specs=specs, out_specs=o_spec, scratch_shapes=[pltpu.VMEM((tm, tn), F32)] if nk > 1 else [],
                  semantics=("parallel", "parallel", "arbitrary"), vmem_limit=VMEM_LIMIT)(*ins)


ROW_TILE = 256


def _rows(t, width, idx=0):
    return pl.BlockSpec((ROW_TILE, width), lambda i: (i, idx))


def _vec(width):
    return pl.BlockSpec((1, width), lambda i: (0, 0))


def _rmsnorm_fwd(x, g, *, name):
    t, d = x.shape

    def body(x_ref, g_ref, h_ref):
        xv = x_ref[...]
        r = lax.rsqrt(jnp.mean(xv * xv, axis=-1, keepdims=True) + EPS)
        h_ref[...] = (xv * r * g_ref[...]).astype(BF16)

    return _pcall(body, name=name, out_shape=jax.ShapeDtypeStruct((t, d), BF16), grid=(t // ROW_TILE,),
                  in_specs=[_rows(t, d), _vec(d)], out_specs=_rows(t, d), semantics=("parallel",))(x, g)


def _rmsnorm_bwd(x, g, dh, dres, *, name):
    t, d = x.shape

    def body(x_ref, g_ref, dh_ref, dres_ref, dx_ref, dg_ref):
        xv = x_ref[...]
        r = lax.rsqrt(jnp.mean(xv * xv, axis=-1, keepdims=True) + EPS)
        xhat = xv * r
        dhv = dh_ref[...].astype(F32)
        dhg = dhv * g_ref[...]
        dx_ref[...] = dres_ref[...] + r * (dhg - xhat * jnp.mean(dhg * xhat, axis=-1, keepdims=True))
        part = jnp.sum(dhv * xhat, axis=0, keepdims=True)

        @pl.when(pl.program_id(0) == 0)
        def _():
            dg_ref[...] = part

        @pl.when(pl.program_id(0) > 0)
        def _():
            dg_ref[...] += part

    return _pcall(body, name=name, out_shape=(jax.ShapeDtypeStruct((t, d), F32), jax.ShapeDtypeStruct((1, d), F32)),
                  grid=(t // ROW_TILE,), in_specs=[_rows(t, d), _vec(d), _rows(t, d), _rows(t, d)],
                  out_specs=(_rows(t, d), _vec(d)), semantics=("arbitrary",))(x, g, dh, dres)


def _swiglu_fwd(gu, *, name):
    t = gu.shape[0]

    def body(gate_ref, up_ref, ff_ref):
        gate = gate_ref[...]
        ff_ref[...] = (gate * _sigmoid(gate) * up_ref[...]).astype(BF16)

    return _pcall(body, name=name, out_shape=jax.ShapeDtypeStruct((t, D_FF), BF16), grid=(t // ROW_TILE,),
                  in_specs=[_rows(t, D_FF, 0), _rows(t, D_FF, 1)], out_specs=_rows(t, D_FF), semantics=("parallel",))(gu, gu)


def _swiglu_bwd(gu, dff, *, name):
    t = gu.shape[0]

    def body(gate_ref, up_ref, dff_ref, dgu_ref):
        gate = gate_ref[...]
        sig = _sigmoid(gate)
        dffv = dff_ref[...]
        dgu_ref[:, :D_FF] = (dffv * up_ref[...] * sig * (1.0 + gate * (1.0 - sig))).astype(BF16)
        dgu_ref[:, D_FF:] = (dffv * gate * sig).astype(BF16)

    return _pcall(body, name=name, out_shape=jax.ShapeDtypeStruct((t, 2 * D_FF), BF16), grid=(t // ROW_TILE,),
                  in_specs=[_rows(t, D_FF, 0), _rows(t, D_FF, 1), _rows(t, D_FF)], out_specs=_rows(t, 2 * D_FF),
                  semantics=("parallel",))(gu, gu, dff)


def _ple_fwd(x2, pgl, pp, *, name):
    t, d = x2.shape

    def body(x_ref, pgl_ref, pp_ref, o_ref):
        o_ref[...] = x_ref[...] + _sigmoid(pgl_ref[...]) * pp_ref[...]

    return _pcall(body, name=name, out_shape=jax.ShapeDtypeStruct((t, d), F32), grid=(t // ROW_TILE,),
                  in_specs=[_rows(t, d)] * 3, out_specs=_rows(t, d), semantics=("parallel",))(x2, pgl, pp)


def _ple_bwd(dx3, pgl, pp, *, name):
    t, d = dx3.shape

    def body(dx_ref, pgl_ref, pp_ref, dpgl_ref, dpp_ref):
        dxv = dx_ref[...]
        sig = _sigmoid(pgl_ref[...])
        dpp_ref[...] = (dxv * sig).astype(BF16)
        dpgl_ref[...] = (dxv * pp_ref[...] * sig * (1.0 - sig)).astype(BF16)

    return _pcall(body, name=name, out_shape=(jax.ShapeDtypeStruct((t, d), BF16),) * 2, grid=(t // ROW_TILE,),
                  in_specs=[_rows(t, d)] * 3, out_specs=(_rows(t, d),) * 2, semantics=("parallel",))(dx3, pgl, pp)


def _loss_head(x3, g, target, *, name):
    t, d = x3.shape

    def body(x_ref, g_ref, t_ref, dx_ref, dg_ref, loss_ref):
        xv = x_ref[...]
        r = lax.rsqrt(jnp.mean(xv * xv, axis=-1, keepdims=True) + EPS)
        xhat = xv * r
        gv = g_ref[...]
        err = xhat * gv - t_ref[...]
        row_loss = jnp.sum(err * err, axis=-1, keepdims=True) * (0.5 / d)
        lpart = jnp.broadcast_to(jnp.sum(row_loss, axis=0, keepdims=True), (1, LANE))
        dy = err * (1.0 / d)
        dyg = dy * gv
        dx_ref[...] = r * (dyg - xhat * jnp.mean(dyg * xhat, axis=-1, keepdims=True))
        gpart = jnp.sum(dy * xhat, axis=0, keepdims=True)

        @pl.when(pl.program_id(0) == 0)
        def _():
            dg_ref[...] = gpart
            loss_ref[...] = lpart

        @pl.when(pl.program_id(0) > 0)
        def _():
            dg_ref[...] += gpart
            loss_ref[...] += lpart

    return _pcall(body, name=name,
                  out_shape=(jax.ShapeDtypeStruct((t, d), F32), jax.ShapeDtypeStruct((1, d), F32), jax.ShapeDtypeStruct((1, LANE), F32)),
                  grid=(t // ROW_TILE,), in_specs=[_rows(t, d), _vec(d), _rows(t, d)],
                  out_specs=(_rows(t, d), _vec(d), _vec(LANE)), semantics=("arbitrary",))(x3, g, target)


def _shift_down(x, d):
    if d == 0:
        return x
    row = lax.broadcasted_iota(jnp.int32, x.shape, 0)
    return jnp.where(row >= d, pltpu.roll(x, d, 0), 0.0)


def _shift_up(x, d):
    if d == 0:
        return x
    t = x.shape[0]
    row = lax.broadcasted_iota(jnp.int32, x.shape, 0)
    return jnp.where(row < t - d, pltpu.roll(x, t - d, 0), 0.0)


def _colsum(x):
    return jnp.sum(x, axis=0, keepdims=True)


def _col(t, idx_fn):
    return pl.BlockSpec((t, LANE), idx_fn)


def _conv_fwd(x, w_ref, taps):
    acc = None
    for j in range(taps):
        term = w_ref[pl.ds(j, 1), :] * _shift_down(x, taps - 1 - j)
        acc = term if acc is None else acc + term
    return acc


def _conv_bwd(x, dy, w_ref, dw_ref, taps):
    dx = None
    for j in range(taps):
        term = w_ref[pl.ds(j, 1), :] * _shift_up(dy, taps - 1 - j)
        dx = term if dx is None else dx + term
        dw_ref[pl.ds(j, 1), :] = _colsum(dy * _shift_down(x, taps - 1 - j))
    return dx


def _qkv_prep_fwd(proj, conv_w, *, name):
    t = proj.shape[0]
    scale = HEAD_DIM ** -0.5

    def body(x_ref, w_ref, o_ref):
        j = pl.program_id(0)
        c = _conv_fwd(x_ref[...], w_ref, QKV_TAPS)
        s = c * _sigmoid(c)
        r = lax.rsqrt(jnp.sum(s * s, axis=-1, keepdims=True) + EPS)
        f = jnp.where(j < 2 * HEADS, r, 1.0) * jnp.where(j < HEADS, scale, 1.0)
        o_ref[0] = s * f

    return _pcall(body, name=name, out_shape=jax.ShapeDtypeStruct((3 * HEADS, t, LANE), F32), grid=(3 * HEADS,),
                  in_specs=[_col(t, lambda j: (0, j)), pl.BlockSpec((QKV_TAPS, LANE), lambda j: (0, j))],
                  out_specs=pl.BlockSpec((1, t, LANE), lambda j: (j, 0, 0)), semantics=("parallel",),
                  vmem_limit=VMEM_LIMIT)(proj, conv_w)


def _qkv_prep_bwd(proj, conv_w, dqkv, *, name):
    t = proj.shape[0]
    scale = HEAD_DIM ** -0.5

    def body(x_ref, w_ref, d_ref, dx_ref, dw_ref):
        j = pl.program_id(0)
        xv = x_ref[...]
        c = _conv_fwd(xv, w_ref, QKV_TAPS)
        sig = _sigmoid(c)
        s = c * sig
        r = lax.rsqrt(jnp.sum(s * s, axis=-1, keepdims=True) + EPS)
        n0 = s * r
        dv = d_ref[0]
        dn0 = dv * jnp.where(j < HEADS, scale, 1.0)
        ds_norm = r * (dn0 - n0 * jnp.sum(dn0 * n0, axis=-1, keepdims=True))
        ds = jnp.where(j < 2 * HEADS, ds_norm, dv)
        dc = ds * sig * (1.0 + c * (1.0 - sig))
        dx_ref[...] = _conv_bwd(xv, dc, w_ref, dw_ref, QKV_TAPS).astype(BF16)

    return _pcall(body, name=name,
                  out_shape=(jax.ShapeDtypeStruct((t, 3 * A_DIM), BF16), jax.ShapeDtypeStruct((QKV_TAPS, 3 * A_DIM), F32)),
                  grid=(3 * HEADS,),
                  in_specs=[_col(t, lambda j: (0, j)), pl.BlockSpec((QKV_TAPS, LANE), lambda j: (0, j)),
                            pl.BlockSpec((1, t, LANE), lambda j: (j, 0, 0))],
                  out_specs=(_col(t, lambda j: (0, j)), pl.BlockSpec((QKV_TAPS, LANE), lambda j: (0, j))),
                  semantics=("parallel",), vmem_limit=VMEM_LIMIT)(proj, conv_w, dqkv)


def _lane_pick(x, lane_idx, lane):
    return jnp.broadcast_to(jnp.sum(jnp.where(lane == lane_idx, x, 0.0), axis=-1, keepdims=True), x.shape)


def _gates_fwd(proj, alog, dtb, *, name):
    t = proj.shape[0]

    def body(x_ref, alog_ref, dtb_ref, g_ref, b_ref):
        xv = x_ref[...]
        lane = lax.broadcasted_iota(jnp.int32, xv.shape, 1)
        gall = -jnp.exp(alog_ref[...]) * _softplus(xv + dtb_ref[...])
        ball = _sigmoid(xv)
        for h in range(HEADS):
            g_ref[h] = _lane_pick(gall, h, lane)
            b_ref[h] = _lane_pick(ball, HEADS + h, lane)

    out = jax.ShapeDtypeStruct((HEADS, t, LANE), F32)
    whole = pl.BlockSpec((HEADS, t, LANE), lambda i: (0, 0, 0))
    return _pcall(body, name=name, out_shape=(out, out), grid=(1,),
                  in_specs=[_col(t, lambda i: (0, AB_COL // LANE)), _vec(LANE), _vec(LANE)], out_specs=(whole, whole),
                  semantics=("arbitrary",), vmem_limit=VMEM_LIMIT)(proj, alog, dtb)


def _gates_bwd(proj, alog, dtb, dg, dbeta, *, name):
    t = proj.shape[0]

    def body(x_ref, alog_ref, dtb_ref, dg_ref, db_ref, dab_ref, dalog_ref, ddtb_ref):
        xv = x_ref[...]
        lane = lax.broadcasted_iota(jnp.int32, xv.shape, 1)
        lane1 = lax.broadcasted_iota(jnp.int32, (1, LANE), 1)
        z = xv + dtb_ref[...]
        nea = -jnp.exp(alog_ref[...])
        da_f = nea * _sigmoid(z)
        g_f = nea * _softplus(z)
        ball = _sigmoid(xv)
        db_f = ball * (1.0 - ball)
        dab = jnp.zeros_like(xv)
        dalog = jnp.zeros((1, LANE), F32)
        for h in range(HEADS):
            dgh = dg_ref[h]
            dab = dab + jnp.where(lane == h, dgh * da_f, 0.0) + jnp.where(lane == HEADS + h, db_ref[h] * db_f, 0.0)
            dalog = dalog + jnp.where(lane1 == h, _colsum(dgh * g_f), 0.0)
        dab_ref[...] = dab.astype(BF16)
        dalog_ref[...] = dalog
        ddtb_ref[...] = jnp.where(lane1 < HEADS, _colsum(dab), 0.0)

    whole = pl.BlockSpec((HEADS, t, LANE), lambda i: (0, 0, 0))
    vec = jax.ShapeDtypeStruct((1, LANE), F32)
    return _pcall(body, name=name, out_shape=(jax.ShapeDtypeStruct((t, LANE), BF16), vec, vec), grid=(1,),
                  in_specs=[_col(t, lambda i: (0, AB_COL // LANE)), _vec(LANE), _vec(LANE), whole, whole],
                  out_specs=(_col(t, lambda i: (0, 0)), _vec(LANE), _vec(LANE)), semantics=("arbitrary",),
                  vmem_limit=VMEM_LIMIT)(proj, alog, dtb, dg, dbeta)


Z_COL = 3 * A_DIM // LANE


def _apost_fwd(o, proj, gn, *, name):
    t = proj.shape[0]

    def body(o_ref, z_ref, gn_ref, y_ref):
        ov = o_ref[0]
        z = z_ref[...]
        r = lax.rsqrt(jnp.mean(ov * ov, axis=-1, keepdims=True) + EPS)
        y_ref[...] = (ov * r * gn_ref[...] * (z * _sigmoid(z))).astype(BF16)

    return _pcall(body, name=name, out_shape=jax.ShapeDtypeStruct((t, A_DIM), BF16), grid=(HEADS,),
                  in_specs=[pl.BlockSpec((1, t, LANE), lambda h: (h, 0, 0)), _col(t, lambda h: (0, Z_COL + h)),
                            pl.BlockSpec((1, LANE), lambda h: (0, 0))],
                  out_specs=_col(t, lambda h: (0, h)), semantics=("parallel",), vmem_limit=VMEM_LIMIT)(o, proj, gn)


def _apost_bwd(o, proj, gn, dmixed, *, name):
    t = proj.shape[0]

    def body(o_ref, z_ref, gn_ref, d_ref, do_ref, dz_ref, dgn_ref):
        ov = o_ref[0]
        z = z_ref[...]
        gnv = gn_ref[...]
        dv = d_ref[...]
        r = lax.rsqrt(jnp.mean(ov * ov, axis=-1, keepdims=True) + EPS)
        ohat = ov * r
        sig = _sigmoid(z)
        dy = dv * (z * sig)
        dz_ref[...] = (dv * ohat * gnv * sig * (1.0 + z * (1.0 - sig))).astype(BF16)
        dyo = dy * gnv
        do_ref[0] = r * (dyo - ohat * jnp.mean(dyo * ohat, axis=-1, keepdims=True))
        part = _colsum(dy * ohat)

        @pl.when(pl.program_id(0) == 0)
        def _():
            dgn_ref[...] = part

        @pl.when(pl.program_id(0) > 0)
        def _():
            dgn_ref[...] += part

    return _pcall(body, name=name,
                  out_shape=(jax.ShapeDtypeStruct((HEADS, t, LANE), F32), jax.ShapeDtypeStruct((t, A_DIM), BF16),
                             jax.ShapeDtypeStruct((1, LANE), F32)),
                  grid=(HEADS,),
                  in_specs=[pl.BlockSpec((1, t, LANE), lambda h: (h, 0, 0)), _col(t, lambda h: (0, Z_COL + h)),
                            pl.BlockSpec((1, LANE), lambda h: (0, 0)), _col(t, lambda h: (0, h))],
                  out_specs=(pl.BlockSpec((1, t, LANE), lambda h: (h, 0, 0)), _col(t, lambda h: (0, h)),
                             pl.BlockSpec((1, LANE), lambda h: (0, 0))),
                  semantics=("arbitrary",), vmem_limit=VMEM_LIMIT)(o, proj, gn, dmixed)


POOL_COL = (AB_COL + LANE) // LANE
CB_COL = POOL_COL + POOL_DIM // LANE
CC_COL = CB_COL + CONV_DIM // LANE
CH_COL = CC_COL + CONV_DIM // LANE
MAX_WIN_LOG2 = 4


def _window_sums(x, shift):
    sums = []
    cur = x
    for k in range(MAX_WIN_LOG2):
        cur = cur + shift(cur, 1 << k)
        sums.append(cur)
    return sums


def _pick_window(sums, win):
    out = sums[-1]
    for k in range(MAX_WIN_LOG2 - 2, -1, -1):
        out = jnp.where(win == float(2 << k), sums[k], out)
    return out


def _pool_counts(shape, win):
    row = lax.broadcasted_iota(jnp.int32, shape, 0).astype(F32)
    return jnp.minimum(row + 1.0, win)


def _pool_fwd(proj, win, wbd, scale, *, name):
    t = proj.shape[0]

    def body(x_ref, win_ref, w_ref, s_ref, y_ref):
        xv = x_ref[...]
        winv = win_ref[...]
        pooled = _pick_window(_window_sums(xv, _shift_down), winv) / _pool_counts(xv.shape, winv) - xv
        y_ref[...] = (_dot(pooled, w_ref[0], NN) * s_ref[...]).astype(BF16)

    nb = POOL_DIM // LANE
    vec = pl.BlockSpec((1, LANE), lambda b: (0, b))
    return _pcall(body, name=name, out_shape=jax.ShapeDtypeStruct((t, POOL_DIM), BF16), grid=(nb,),
                  in_specs=[_col(t, lambda b: (0, POOL_COL + b)), vec, pl.BlockSpec((1, LANE, LANE), lambda b: (b, 0, 0)), vec],
                  out_specs=_col(t, lambda b: (0, b)), semantics=("parallel",), vmem_limit=VMEM_LIMIT)(proj, win, wbd, scale)


def _pool_bwd(proj, win, wbd, scale, dmixed, *, name):
    t = proj.shape[0]

    def body(x_ref, win_ref, w_ref, s_ref, d_ref, dx_ref, dw_ref, ds_ref):
        xv = x_ref[...]
        winv = win_ref[...]
        cnt = _pool_counts(xv.shape, winv)
        pooled = _pick_window(_window_sums(xv, _shift_down), winv) / cnt - xv
        dv = d_ref[...]
        ds_ref[...] = _colsum(dv * _dot(pooled, w_ref[0], NN))
        dy0 = dv * s_ref[...]
        dw_ref[0] = _dot(pooled, dy0, TN)
        dpooled = _dot(dy0, w_ref[0], NT)
        dmean = dpooled / cnt
        dx_ref[...] = (_pick_window(_window_sums(dmean, _shift_up), winv) - dpooled).astype(BF16)

    nb = POOL_DIM // LANE
    vec = pl.BlockSpec((1, LANE), lambda b: (0, b))
    mat = pl.BlockSpec((1, LANE, LANE), lambda b: (b, 0, 0))
    first = A_DIM // LANE
    return _pcall(body, name=name,
                  out_shape=(jax.ShapeDtypeStruct((t, POOL_DIM), BF16), jax.ShapeDtypeStruct((nb, LANE, LANE), F32),
                             jax.ShapeDtypeStruct((1, POOL_DIM), F32)),
                  grid=(nb,),
                  in_specs=[_col(t, lambda b: (0, POOL_COL + b)), vec, mat, vec, _col(t, lambda b: (0, first + b))],
                  out_specs=(_col(t, lambda b: (0, b)), mat, vec), semantics=("parallel",),
                  vmem_limit=VMEM_LIMIT)(proj, win, wbd, scale, dmixed)


def _sconv_fwd(proj, w, *, name):
    t = proj.shape[0]

    def body(cb_ref, cc_ref, ch_ref, w_ref, y_ref):
        y_ref[...] = (cb_ref[...] * _conv_fwd(cc_ref[...] * ch_ref[...], w_ref, CONV_TAPS)).astype(BF16)

    nb = CONV_DIM // LANE
    return _pcall(body, name=name, out_shape=jax.ShapeDtypeStruct((t, CONV_DIM), BF16), grid=(nb,),
                  in_specs=[_col(t, lambda b: (0, CB_COL + b)), _col(t, lambda b: (0, CC_COL + b)),
                            _col(t, lambda b: (0, CH_COL + b)), pl.BlockSpec((CONV_TAPS, LANE), lambda b: (0, b))],
                  out_specs=_col(t, lambda b: (0, b)), semantics=("parallel",), vmem_limit=VMEM_LIMIT)(proj, proj, proj, w)


def _sconv_bwd(proj, w, dmixed, *, name):
    t = proj.shape[0]

    def body(cb_ref, cc_ref, ch_ref, w_ref, d_ref, dcb_ref, dcc_ref, dch_ref, dw_ref):
        cc = cc_ref[...]
        ch = ch_ref[...]
        u = cc * ch
        dv = d_ref[...]
        dcb_ref[...] = (dv * _conv_fwd(u, w_ref, CONV_TAPS)).astype(BF16)
        du = _conv_bwd(u, dv * cb_ref[...], w_ref, dw_ref, CONV_TAPS)
        dcc_ref[...] = (du * ch).astype(BF16)
        dch_ref[...] = (du * cc).astype(BF16)

    nb = CONV_DIM // LANE
    first = (A_DIM + POOL_DIM) // LANE
    act = jax.ShapeDtypeStruct((t, CONV_DIM), BF16)
    wspec = pl.BlockSpec((CONV_TAPS, LANE), lambda b: (0, b))
    ospec = _col(t, lambda b: (0, b))
    return _pcall(body, name=name, out_shape=(act, act, act, jax.ShapeDtypeStruct((CONV_TAPS, CONV_DIM), F32)), grid=(nb,),
                  in_specs=[_col(t, lambda b: (0, CB_COL + b)), _col(t, lambda b: (0, CC_COL + b)),
                            _col(t, lambda b: (0, CH_COL + b)), wspec, _col(t, lambda b: (0, first + b))],
                  out_specs=(ospec, ospec, ospec, wspec), semantics=("parallel",),
                  vmem_limit=VMEM_LIMIT)(proj, proj, proj, w, dmixed)


def _chunk_consts():
    r = lax.broadcasted_iota(jnp.int32, (CHUNK, CHUNK), 0)
    c = lax.broadcasted_iota(jnp.int32, (CHUNK, CHUNK), 1)
    incl = r >= c
    strict = r > c
    tri = jnp.where(incl, 1.0, 0.0).astype(F32)
    eye = jnp.where(r == c, 1.0, 0.0).astype(F32)
    e0 = jnp.where(lax.broadcasted_iota(jnp.int32, (CHUNK, LANE), 1) == 0, 1.0, 0.0).astype(F32)
    return incl, strict, tri, eye, e0


def _tri_inv(low, eye):
    x = eye - low
    p = _dot(low, low, NN, hi=True)
    for i in range(5):
        x = x + _dot(x, p, NN, hi=True)
        if i < 4:
            p = _dot(p, p, NN, hi=True)
    return x


def _chunk_decay(g, consts):
    incl, _, tri, _, e0 = consts
    gcb = _dot(tri, g, NN, hi=True)
    glast = _dot(jnp.ones((LANE, CHUNK), F32), g, NN, hi=True)
    col = _dot(gcb, e0, NT, hi=True)
    row = _dot(e0, gcb, NT, hi=True)
    decay = jnp.exp(jnp.where(incl, col - row, -1e30))
    return gcb, glast, decay


def _deltanet_fwd(qkv, g, beta, *, name):
    t = qkv.shape[1]
    n_chunks = t // CHUNK

    def body(qkv_ref, g_ref, b_ref, o_ref, st_ref, tm_ref, s_ref):
        @pl.when(pl.program_id(0) == 0)
        def _():
            s_ref[...] = jnp.zeros_like(s_ref)

        consts = _chunk_consts()
        incl, strict, _, eye, _ = consts
        for h in range(HEADS):
            q, k, v = qkv_ref[h], qkv_ref[HEADS + h], qkv_ref[2 * HEADS + h]
            bv = b_ref[h]
            s = s_ref[h]
            st_ref[0, h] = s
            gcb, glast, decay = _chunk_decay(g_ref[h], consts)
            kb = k * bv
            low = jnp.where(strict, _dot(kb, k, NT) * decay, 0.0)
            tm = _tri_inv(low, eye)
            tm_ref[0, h] = tm
            egc = jnp.exp(gcb)
            u = _dot(tm, v * bv, NN)
            w = _dot(tm, kb * egc, NN)
            attn = _dot(q, k, NT) * decay
            v_new = u - _dot(w, s, NN)
            o_ref[h] = _dot(q * egc, s, NN) + _dot(attn, v_new, NN)
            kg = k * jnp.exp(glast[:CHUNK] - gcb)
            s_ref[h] = s * jnp.exp(glast) + _dot(kg, v_new, TN)

    return _pcall(
        body, name=name,
        out_shape=(jax.ShapeDtypeStruct((HEADS, t, LANE), F32), jax.ShapeDtypeStruct((n_chunks, HEADS, LANE, LANE), F32),
                   jax.ShapeDtypeStruct((n_chunks, HEADS, CHUNK, CHUNK), F32)),
        grid=(n_chunks,),
        in_specs=[pl.BlockSpec((3 * HEADS, CHUNK, LANE), lambda n: (0, n, 0)), pl.BlockSpec((HEADS, CHUNK, LANE), lambda n: (0, n, 0)),
                  pl.BlockSpec((HEADS, CHUNK, LANE), lambda n: (0, n, 0))],
        out_specs=(pl.BlockSpec((HEADS, CHUNK, LANE), lambda n: (0, n, 0)), pl.BlockSpec((1, HEADS, LANE, LANE), lambda n: (n, 0, 0, 0)),
                   pl.BlockSpec((1, HEADS, CHUNK, CHUNK), lambda n: (n, 0, 0, 0))),
        scratch_shapes=[pltpu.VMEM((HEADS, LANE, LANE), F32)], semantics=("arbitrary",))(qkv, g, beta)


def _deltanet_bwd(qkv, g, beta, states, tmats, do, *, name):
    t = qkv.shape[1]
    n_chunks = t // CHUNK

    def body(qkv_ref, g_ref, b_ref, st_ref, tm_ref, do_ref, dqkv_ref, dg_ref, db_ref, ds_ref):
        @pl.when(pl.program_id(0) == 0)
        def _():
            ds_ref[...] = jnp.zeros_like(ds_ref)

        consts = _chunk_consts()
        incl, strict, tri, _, _ = consts
        ones = jnp.ones((CHUNK, LANE), F32)
        last_row = lax.broadcasted_iota(jnp.int32, (CHUNK, LANE), 0) == CHUNK - 1
        for h in range(HEADS):
            q, k, v = qkv_ref[h], qkv_ref[HEADS + h], qkv_ref[2 * HEADS + h]
            bv = b_ref[h]
            s = st_ref[0, h]
            tm = tm_ref[0, h]
            dov = do_ref[h]
            ds_next = ds_ref[h]
            gcb, glast, decay = _chunk_decay(g_ref[h], consts)
            egc = jnp.exp(gcb)
            eglast = jnp.exp(glast)
            ekg = jnp.exp(glast[:CHUNK] - gcb)
            kb = k * bv
            vb = v * bv
            kbg = kb * egc
            kk = _dot(kb, k, NT)
            qk = _dot(q, k, NT)
            attn = qk * decay
            w = _dot(tm, kbg, NN)
            v_new = _dot(tm, vb, NN) - _dot(w, s, NN)
            qg = q * egc
            kg = k * ekg
            dv_new = _dot(attn, dov, TN) + _dot(kg, ds_next, NN)
            dattn = jnp.where(incl, _dot(dov, v_new, NT), 0.0)
            dqg = _dot(dov, s, NT)
            dkg = _dot(v_new, ds_next, NT)
            ds_ref[h] = _dot(qg, dov, TN) + eglast * ds_next - _dot(w, dv_new, TN)
            dglast = (jnp.sum(jnp.sum(s * ds_next * eglast, axis=1, keepdims=True), axis=0, keepdims=True)
                      + jnp.sum(jnp.sum(dkg * kg, axis=1, keepdims=True), axis=0, keepdims=True))
            dw = -_dot(dv_new, s, NT)
            dtm = _dot(dv_new, vb, NT) + _dot(dw, kbg, NT)
            dvb = _dot(tm, dv_new, TN)
            dkbg = _dot(tm, dw, TN)
            dlow = jnp.where(strict, -_dot(_dot(tm, dtm, TN), tm, NT), 0.0)
            dkk = dlow * decay
            dqk = dattn * decay
            dkb = _dot(dkk, k, NN) + dkbg * egc
            dk = _dot(dkk, kb, TN) + _dot(dqk, q, TN) + dkg * ekg + dkb * bv
            dq = _dot(dqk, k, NN) + dqg * egc
            dqkv_ref[h] = dq
            dqkv_ref[HEADS + h] = dk
            dqkv_ref[2 * HEADS + h] = dvb * bv
            db_ref[h] = jnp.broadcast_to(jnp.sum(dkb * k + dvb * v, axis=1, keepdims=True), (CHUNK, LANE))
            m = (dlow * kk + dattn * qk) * decay
            dgc = (jnp.sum(dqg * qg + dkbg * kbg - dkg * kg, axis=1, keepdims=True)
                   + jnp.sum(m, axis=1, keepdims=True) - _dot(m, ones, TN, hi=True)
                   + jnp.where(last_row, dglast, 0.0))
            dg_ref[h] = _dot(tri, dgc, TN, hi=True)

    rev = lambda n: (0, n_chunks - 1 - n, 0)
    rev4 = lambda n: (n_chunks - 1 - n, 0, 0, 0)
    act = jax.ShapeDtypeStruct((HEADS, t, LANE), F32)
    hspec = pl.BlockSpec((HEADS, CHUNK, LANE), rev)
    return _pcall(
        body, name=name, out_shape=(jax.ShapeDtypeStruct((3 * HEADS, t, LANE), F32), act, act), grid=(n_chunks,),
        in_specs=[pl.BlockSpec((3 * HEADS, CHUNK, LANE), rev), hspec, hspec, pl.BlockSpec((1, HEADS, LANE, LANE), rev4),
                  pl.BlockSpec((1, HEADS, CHUNK, CHUNK), rev4), hspec],
        out_specs=(pl.BlockSpec((3 * HEADS, CHUNK, LANE), rev), hspec, hspec),
        scratch_shapes=[pltpu.VMEM((HEADS, LANE, LANE), F32)], semantics=("arbitrary",))(qkv, g, beta, states, tmats, do)


ANY = pl.BlockSpec(memory_space=pl.ANY)
PEERS = N_DEV - 1


def _all_gather(arrays, *, name):
    n = len(arrays)

    def body(*refs):
        ins, outs = refs[:n], refs[n:2 * n]
        send_sems, recv_sems, local_sems = refs[2 * n:]
        x, y, c = lax.axis_index("x"), lax.axis_index("y"), lax.axis_index("c")
        me, sibling = (x, y, c), (x, y, 1 - c)
        chips = [(1 - x, y), (x, 1 - y), (1 - x, 1 - y)]

        def copy(a, k, block, to, src=None):
            dst = outs[a].at[4 * block[0] + 2 * block[1] + block[2]]
            return pltpu.make_async_remote_copy(src_ref=dst if src is None else src, dst_ref=dst, send_sem=send_sems.at[a * PEERS + k],
                                                recv_sem=recv_sems.at[a * PEERS + k], device_id=to, device_id_type=MESH)

        local = [pltpu.make_async_copy(ins[a], outs[a].at[4 * x + 2 * y + c], local_sems.at[a]) for a in range(n)]
        for cp in local:
            cp.start()
        first = []
        for a in range(n):
            first.append(copy(a, 0, me, sibling, src=ins[a]))
            first += [copy(a, 1 + j, me, (*chip, c), src=ins[a]) for j, chip in enumerate(chips)]
        for cp in first:
            cp.start()
        passed = []
        for a in range(n):
            for j, chip in enumerate(chips):
                copy(a, 1 + j, (*chip, c), me).wait_recv()
                fwd = copy(a, 4 + j, (*chip, c), sibling)
                fwd.start()
                passed.append(fwd)
        for a in range(n):
            copy(a, 0, sibling, me).wait_recv()
            for j, chip in enumerate(chips):
                copy(a, 4 + j, (*chip, 1 - c), me).wait_recv()
        for cp in first + passed:
            cp.wait_send()
        for cp in local:
            cp.wait()

    return _pcall(body, name=name, out_shape=tuple(jax.ShapeDtypeStruct((N_DEV,) + a.shape, a.dtype) for a in arrays),
                  in_specs=[ANY] * n, out_specs=(ANY,) * n,
                  scratch_shapes=[pltpu.SemaphoreType.DMA((n * PEERS,)), pltpu.SemaphoreType.DMA((n * PEERS,)),
                                  pltpu.SemaphoreType.DMA((n,))])(*arrays)


def _exchange_blocks(arrays, *, name):
    n = len(arrays)

    def body(*refs):
        ins, outs = refs[:n], refs[n:2 * n]
        send_sems, recv_sems, local_sems = refs[2 * n:]
        x, y, c = lax.axis_index("x"), lax.axis_index("y"), lax.axis_index("c")
        mine = 4 * x + 2 * y + c
        copies = []
        for a in range(n):
            lc = pltpu.make_async_copy(ins[a].at[mine], outs[a].at[mine], local_sems.at[a])
            lc.start()
            copies.append(lc)
            for k in range(1, N_DEV):
                px = 1 - x if k & 4 else x
                py = 1 - y if k & 2 else y
                pc = 1 - c if k & 1 else c
                cp = pltpu.make_async_remote_copy(src_ref=ins[a].at[4 * px + 2 * py + pc], dst_ref=outs[a].at[mine],
                                                  send_sem=send_sems.at[a * PEERS + k - 1], recv_sem=recv_sems.at[a * PEERS + k - 1],
                                                  device_id=(px, py, pc), device_id_type=MESH)
                cp.start()
                copies.append(cp)
        for cp in copies:
            cp.wait()

    return _pcall(body, name=name, out_shape=tuple(jax.ShapeDtypeStruct(a.shape, a.dtype) for a in arrays),
                  in_specs=[ANY] * n, out_specs=(ANY,) * n,
                  scratch_shapes=[pltpu.SemaphoreType.DMA((n * PEERS,)), pltpu.SemaphoreType.DMA((n * PEERS,)),
                                  pltpu.SemaphoreType.DMA((n,))])(*arrays)


def _adamw_reduce(w, parts, m, v, *, name):
    r, c = w.shape
    tr = _tile(r, 512, 16)
    bc1 = 1.0 - ADAM_B1 ** ADAM_STEP
    bc2 = 1.0 - ADAM_B2 ** ADAM_STEP

    def body(w_ref, p_ref, m_ref, v_ref, g_ref, d_ref, nm_ref, nv_ref):
        g = p_ref[0].astype(F32)
        for s in range(1, N_DEV):
            g = g + p_ref[s].astype(F32)
        nm = ADAM_B1 * m_ref[...] + (1.0 - ADAM_B1) * g
        nv = ADAM_B2 * v_ref[...] + (1.0 - ADAM_B2) * (g * g)
        g_ref[...] = g
        nm_ref[...] = nm
        nv_ref[...] = nv
        d_ref[...] = -ADAM_LR * ((nm / bc1) / (jnp.sqrt(nv / bc2) + ADAM_EPS) + ADAM_WD * w_ref[...])

    spec = pl.BlockSpec((tr, c), lambda i: (i, 0))
    out = jax.ShapeDtypeStruct((r, c), F32)
    return _pcall(body, name=name, out_shape=(out,) * 4, grid=(r // tr,),
                  in_specs=[spec, pl.BlockSpec((N_DEV, tr, c), lambda i: (0, i, 0)), spec, spec], out_specs=(spec,) * 4,
                  semantics=("parallel",), vmem_limit=VMEM_LIMIT)(w, parts, m, v)


def _pool_windows():
    return jnp.repeat(jnp.asarray(POOL_WINDOWS, F32), POOL_DIM // len(POOL_WINDOWS))[None, :]


def _block_diag_pairs(pool_w):
    z = jnp.zeros_like(pool_w[0])
    return jnp.stack([jnp.block([[pool_w[2 * b], z], [z, pool_w[2 * b + 1]]]) for b in range(2)])


def _pad_lanes(vec):
    return jnp.zeros((1, LANE), F32).at[0, :vec.shape[0]].set(vec)


def _layer_fwd(x, p_i, wt, li):
    tag = f"l{li}"
    h1 = _rmsnorm_fwd(x, wt["norm1_g"], name="rmsnorm_fwd")
    proj = _matmul(h1, wt["w_in"], "nn", name="mm_in")
    qkv = _qkv_prep_fwd(proj, wt["conv_qkv"], name="qkv_prep_fwd")
    g, beta = _gates_fwd(proj, wt["a_log"], wt["dt_bias"], name="gates_fwd")
    o, states, tmats = _deltanet_fwd(qkv, g, beta, name="deltanet_fwd")
    o_a = _apost_fwd(o, proj, wt["onorm_g"], name="apost_fwd")
    o_b = _pool_fwd(proj, wt["pool_win"], wt["pool_wbd"], wt["pool_scale"], name="pool_fwd")
    o_c = _sconv_fwd(proj, wt["sconv_w"], name="sconv_fwd")
    mixed = jnp.concatenate([o_a, o_b, o_c], axis=1)
    x1 = _matmul(mixed, wt["w_out"], "nn", res=x, name="mm_out")
    h2 = _rmsnorm_fwd(x1, wt["norm2_g"], name="rmsnorm_fwd")
    gu = _matmul(h2, wt["w_gu"], "nn", name="mm_gu")
    ff = _swiglu_fwd(gu, name="swiglu_fwd")
    x2 = _matmul(ff, wt["w_down"], "nn", res=x1, name="mm_down")
    pgl = _matmul(x2, wt["ple_gate"], "nn", name="mm_pleg")
    pp = _matmul(p_i, wt["ple_proj"], "nn", name="mm_plep")
    x3 = _ple_fwd(x2, pgl, pp, name="ple_fwd")
    del tag
    saved = dict(x=x, h1=h1, proj=proj, qkv=qkv, g=g, beta=beta, o=o, states=states, tmats=tmats, mixed=mixed, x1=x1, h2=h2,
                 gu=gu, ff=ff, x2=x2, pgl=pgl, pp=pp, p=p_i)
    return x3, saved


def _layer_bwd(dx3, sv, wt):
    gr = {}
    dpgl, dpp = _ple_bwd(dx3, sv["pgl"], sv["pp"], name="ple_bwd")
    gr["ple_proj"] = _matmul(sv["p"], dpp, "tn", name="mm_dplep")
    gr["ple_gate"] = _matmul(sv["x2"], dpgl, "tn", name="mm_dpleg")
    dx2 = _matmul(dpgl, wt["ple_gate"], "nt", res=dx3, name="mm_dx2")
    gr["w_down"] = _matmul(sv["ff"], dx2, "tn", name="mm_ddown")
    dff = _matmul(dx2, wt["w_down"], "nt", name="mm_dff")
    dgu = _swiglu_bwd(sv["gu"], dff, name="swiglu_bwd")
    dwgu = _matmul(sv["h2"], dgu, "tn", name="mm_dgu")
    gr["w_gate"], gr["w_up"] = dwgu[:, :D_FF], dwgu[:, D_FF:]
    dh2 = _matmul(dgu, wt["w_gu"], "nt", name="mm_dh2")
    dx1, gr["norm2_g"] = _rmsnorm_bwd(sv["x1"], wt["norm2_g"], dh2, dx2, name="rmsnorm_bwd")
    gr["w_out"] = _matmul(sv["mixed"], dx1, "tn", name="mm_dout")
    dmixed = _matmul(dx1, wt["w_out"], "nt", name="mm_dmixed")
    proj = sv["proj"]
    dcb, dcc, dch, gr["sconv_w"] = _sconv_bwd(proj, wt["sconv_w"], dmixed, name="sconv_bwd")
    dhp, dwbd, gr["pool_scale"] = _pool_bwd(proj, wt["pool_win"], wt["pool_wbd"], wt["pool_scale"], dmixed, name="pool_bwd")
    half = LANE // 2
    gr["pool_w"] = jnp.stack([dwbd[0, :half, :half], dwbd[0, half:, half:], dwbd[1, :half, :half], dwbd[1, half:, half:]])
    do, dz, gr["onorm_g"] = _apost_bwd(sv["o"], proj, wt["onorm_g"], dmixed, name="apost_bwd")
    dqkv_h, dg, dbeta = _deltanet_bwd(sv["qkv"], sv["g"], sv["beta"], sv["states"], sv["tmats"], do, name="deltanet_bwd")
    dab, dalog, ddtb = _gates_bwd(proj, wt["a_log"], wt["dt_bias"], dg, dbeta, name="gates_bwd")
    gr["a_log"], gr["dt_bias"] = dalog[0, :HEADS], ddtb[0, :HEADS]
    dqkv, gr["conv_qkv"] = _qkv_prep_bwd(proj, wt["conv_qkv"], dqkv_h, name="qkv_prep_bwd")
    dproj = jnp.concatenate([dqkv, dz, dab, dhp, dcb, dcc, dch], axis=1)
    dwin = _matmul(sv["h1"], dproj, "tn", name="mm_din")
    gr["w_in"] = jnp.concatenate([dwin[:, :AB_COL + 2 * HEADS], dwin[:, AB_COL + LANE:]], axis=1)
    dh1 = _matmul(dproj, wt["w_in"], "nt", name="mm_dh1")
    dx, gr["norm1_g"] = _rmsnorm_bwd(sv["x"], wt["norm1_g"], dh1, dx1, name="rmsnorm_bwd")
    return dx, gr


def _local_step(x, p, target, full):
    win = _pool_windows()
    layers = []
    for i in range(DEPTH):
        wt = dict(
            norm1_g=full["norm1_g"][i][None], norm2_g=full["norm2_g"][i][None], onorm_g=full["onorm_g"][i][None],
            a_log=_pad_lanes(full["a_log"][i]), dt_bias=_pad_lanes(full["dt_bias"][i]),
            pool_scale=full["pool_scale"][i][None], pool_win=win, pool_wbd=_block_diag_pairs(full["pool_w"][i]),
            conv_qkv=full["conv_qkv"][i], sconv_w=full["sconv_w"][i],
            w_in=jnp.concatenate([full["w_in"][i][:, :AB_COL + 2 * HEADS], jnp.zeros((D_MODEL, LANE - 2 * HEADS), BF16),
                                  full["w_in"][i][:, AB_COL + 2 * HEADS:]], axis=1),
            w_out=full["w_out"][i], w_gu=jnp.concatenate([full["w_gate"][i], full["w_up"][i]], axis=1),
            w_down=full["w_down"][i], ple_proj=full["ple_proj"][i], ple_gate=full["ple_gate"][i])
        layers.append(wt)
    saved = []
    h = x
    for i in range(DEPTH):
        h, sv = _layer_fwd(h, p[i], layers[i], i)
        saved.append(sv)
    dx, dgf, loss = _loss_head(h, full["final_g"][None], target, name="loss_head")
    grads = []
    for i in reversed(range(DEPTH)):
        dx, gr = _layer_bwd(dx, saved[i], layers[i])
        grads.append(gr)
    grads = grads[::-1]
    stacked = {k: jnp.stack([grads[i][k] for i in range(DEPTH)]) for k in grads[0]}
    stacked["norm1_g"] = stacked["norm1_g"][:, 0]
    stacked["norm2_g"] = stacked["norm2_g"][:, 0]
    stacked["onorm_g"] = stacked["onorm_g"][:, 0]
    stacked["pool_scale"] = stacked["pool_scale"][:, 0]
    return loss, dx, stacked, dgf


COL_SHARDED = ("w_in", "w_gate", "w_up", "ple_proj", "conv_qkv", "sconv_w")
ROW_SHARDED = ("w_out", "w_down", "ple_gate")
F32_PAYLOAD = ("conv_qkv", "sconv_w")
SMALL = ("norm1_g", "a_log", "dt_bias", "onorm_g", "pool_w", "pool_scale", "norm2_g", "final_g")
SLAB_COLS = 1024


def _unshard(name, gathered):
    _, depth, a, b = gathered.shape
    if name in COL_SHARDED:
        return jnp.transpose(gathered, (1, 2, 0, 3)).reshape(depth, a, N_DEV * b)
    return jnp.transpose(gathered, (1, 0, 2, 3)).reshape(depth, N_DEV * a, b)


def _to_blocks(name, grad):
    depth, a, b = grad.shape
    if name in COL_SHARDED:
        return jnp.transpose(grad.reshape(depth, a, N_DEV, b // N_DEV), (2, 0, 1, 3))
    return jnp.transpose(grad.reshape(depth, N_DEV, a // N_DEV, b), (1, 0, 2, 3))


def _slab_rows(shape):
    size = 1
    for s in shape:
        size *= s
    return -(-size // SLAB_COLS)


def _pack_slab(parts, extra_rows):
    rows = []
    for name in SMALL:
        flat = parts[name].reshape(-1)
        nrow = _slab_rows(parts[name].shape)
        rows.append(jnp.pad(flat, (0, nrow * SLAB_COLS - flat.shape[0])).reshape(nrow, SLAB_COLS))
    rows += extra_rows
    slab = jnp.concatenate(rows, axis=0)
    pad = (-slab.shape[0]) % 16
    return jnp.pad(slab, ((0, pad), (0, 0)))


def _unpack_slab(slab, shapes):
    out, row = {}, 0
    for name in SMALL:
        size = 1
        for s in shapes[name]:
            size *= s
        out[name] = slab[row:row + _slab_rows(shapes[name])].reshape(-1)[:size].reshape(shapes[name])
        row += _slab_rows(shapes[name])
    return out, row


def kernel(x, p, norm1_g, w_in, conv_qkv, a_log, dt_bias, onorm_g, pool_w, pool_scale, sconv_w, w_out, norm2_g, w_gate, w_up, w_down, ple_proj, ple_gate, final_g, loss_target, m_norm1_g, m_w_in, m_conv_qkv, m_a_log, m_dt_bias, m_onorm_g, m_pool_w, m_pool_scale, m_sconv_w, m_w_out, m_norm2_g, m_w_gate, m_w_up, m_w_down, m_ple_proj, m_ple_gate, m_final_g, v_norm1_g, v_w_in, v_conv_qkv, v_a_log, v_dt_bias, v_onorm_g, v_pool_w, v_pool_scale, v_sconv_w, v_w_out, v_norm2_g, v_w_gate, v_w_up, v_w_down, v_ple_proj, v_ple_gate, v_final_g):
    names = ["norm1_g", "w_in", "conv_qkv", "a_log", "dt_bias", "onorm_g", "pool_w", "pool_scale", "sconv_w", "w_out", "norm2_g",
             "w_gate", "w_up", "w_down", "ple_proj", "ple_gate", "final_g"]
    w = dict(zip(names, [norm1_g, w_in, conv_qkv, a_log, dt_bias, onorm_g, pool_w, pool_scale, sconv_w, w_out, norm2_g, w_gate, w_up,
                         w_down, ple_proj, ple_gate, final_g]))
    m = dict(zip(names, [m_norm1_g, m_w_in, m_conv_qkv, m_a_log, m_dt_bias, m_onorm_g, m_pool_w, m_pool_scale, m_sconv_w, m_w_out,
                         m_norm2_g, m_w_gate, m_w_up, m_w_down, m_ple_proj, m_ple_gate, m_final_g]))
    v = dict(zip(names, [v_norm1_g, v_w_in, v_conv_qkv, v_a_log, v_dt_bias, v_onorm_g, v_pool_w, v_pool_scale, v_sconv_w, v_w_out,
                         v_norm2_g, v_w_gate, v_w_up, v_w_down, v_ple_proj, v_ple_gate, v_final_g]))
    sharded = COL_SHARDED + ROW_SHARDED

    payload = [w[k] if k in F32_PAYLOAD else w[k].astype(BF16) for k in sharded]
    gathered = _all_gather(payload, name="all_gather_weights")
    full = {k: _unshard(k, g) for k, g in zip(sharded, gathered)}
    full.update({k: w[k] for k in SMALL})

    loss_part, dx, grads, dgf = _local_step(x[0], p[:, 0], loss_target[0], full)
    grads["final_g"] = dgf[0]

    blocks = [_to_blocks(k, grads[k]) for k in sharded]
    blocks = [b if k in F32_PAYLOAD else b.astype(BF16) for k, b in zip(sharded, blocks)]
    received = _exchange_blocks(blocks, name="exchange_grad_blocks")
    loss_row = jnp.pad(loss_part, ((0, 0), (0, SLAB_COLS - LANE)))
    (small_parts,) = _all_gather([_pack_slab(grads, [loss_row])], name="all_gather_small_grads")

    out_g, out_d, out_m, out_v = {}, {}, {}, {}
    for k, rec in zip(sharded, received):
        shp = w[k].shape
        r2 = (shp[0] * shp[1], shp[2])
        res = _adamw_reduce(w[k].reshape(r2), rec.reshape((N_DEV,) + r2), m[k].reshape(r2), v[k].reshape(r2), name="adamw_" + k)
        out_g[k], out_d[k], out_m[k], out_v[k] = [t.reshape(shp) for t in res]
    zero_row = jnp.zeros((1, SLAB_COLS), F32)
    slabs = _adamw_reduce(_pack_slab(w, [zero_row]), small_parts, _pack_slab(m, [zero_row]), _pack_slab(v, [zero_row]), name="adamw_small")
    shapes = {k: w[k].shape for k in SMALL}
    for dst, slab in zip((out_g, out_d, out_m, out_v), slabs):
        vals, _ = _unpack_slab(slab, shapes)
        dst.update(vals)
    _, loss_at = _unpack_slab(slabs[0], shapes)
    loss = slabs[0][loss_at, 0]

    return (loss, dx[None], *[out_g[k] for k in names], *[out_d[k] for k in names], *[out_m[k] for k in names],
            *[out_v[k] for k in names])
```

```python
import functools

import jax
import jax.numpy as jnp
from jax import lax
from jax.experimental import pallas as pl
from jax.experimental.pallas import tpu as pltpu

F32 = jnp.float32
BF16 = jnp.bfloat16

D_MODEL = 1024
DEPTH = 2
PLE_DIM = 256
EPS = 1e-6
HEAD_DIM = 128
HEADS = 4
A_DIM = HEADS * HEAD_DIM
QKV_TAPS = 4
CHUNK = 64
POOL_WINDOWS = (2, 4, 8, 16)
POOL_DIM = 256
CONV_DIM = 256
CONV_TAPS = 3
D_FF = 2816
D_IN = 3080
D_IN_PAD = 3200
AB_COL = 2048
N_DEV = 8

ADAM_LR = 0.001
ADAM_B1 = 0.9
ADAM_B2 = 0.999
ADAM_EPS = 1e-08
ADAM_WD = 0.01
ADAM_STEP = 10

LANE = 128
SUBLANE = 8
VMEM_BYTES_V7X = 64 * 1024 * 1024
VMEM_LIMIT = 48 * 1024 * 1024

_HI = lax.Precision.HIGHEST
NN = ((1,), (0,))
NT = ((1,), (1,))
TN = ((0,), (0,))
MESH = pl.DeviceIdType.MESH


def _dot(a, b, dims, hi=False):
    if hi:
        return lax.dot_general(a, b, (dims, ((), ())), precision=_HI, preferred_element_type=F32)
    return lax.dot_general(a.astype(BF16), b.astype(BF16), (dims, ((), ())), preferred_element_type=F32)


def _pcall(body, *, name, out_shape, grid=(), in_specs=None, out_specs=None, scratch_shapes=(), semantics=None,
           vmem_limit=None, **kw):
    params = {}
    if semantics is not None:
        params["dimension_semantics"] = semantics
    if vmem_limit is not None:
        params["vmem_limit_bytes"] = vmem_limit
    return pl.pallas_call(
        body, name=name, out_shape=out_shape, grid=grid, in_specs=in_specs, out_specs=out_specs,
        scratch_shapes=list(scratch_shapes), compiler_params=pltpu.CompilerParams(**params), **kw)


def _sigmoid(x):
    return 1.0 / (1.0 + jnp.exp(-x))


def _softplus(x):
    return jnp.maximum(x, 0.0) + jnp.log(1.0 + jnp.exp(-jnp.abs(x)))


def _tile(n, cap, mult):
    if n <= cap:
        return n
    best = None
    for t in range(mult, cap + 1, mult):
        if n % t == 0:
            best = t
    assert best is not None, (n, cap, mult)
    return best


def _matmul(a, b, mode, *, name, res=None, out_dtype=F32):
    if mode == "nn":
        (m, k), (k2, n) = a.shape, b.shape
    elif mode == "nt":
        (m, k), (n, k2) = a.shape, b.shape
    else:
        (k, m), (k2, n) = a.shape, b.shape
    assert k == k2, (a.shape, b.shape, mode)
    tm = _tile(m, 512, LANE if mode == "tn" else 16)
    tn = _tile(n, 640, LANE)
    tk = _tile(k, 3200, LANE)
    nk = k // tk
    dims = {"nn": NN, "nt": NT, "tn": TN}[mode]
    a_spec = pl.BlockSpec((tk, tm), lambda i, j, kk: (kk, i)) if mode == "tn" else pl.BlockSpec((tm, tk), lambda i, j, kk: (i, kk))
    b_spec = pl.BlockSpec((tn, tk), lambda i, j, kk: (j, kk)) if mode == "nt" else pl.BlockSpec((tk, tn), lambda i, j, kk: (kk, j))
    o_spec = pl.BlockSpec((tm, tn), lambda i, j, kk: (i, j))
    has_res = res is not None

    def body(*refs):
        a_ref, b_ref = refs[0], refs[1]
        res_ref = refs[2] if has_res else None
        o_ref = refs[2 + has_res]
        part = _dot(a_ref[...], b_ref[...], dims)
        if nk == 1:
            if has_res:
                part = part + res_ref[...]
            o_ref[...] = part.astype(o_ref.dtype)
            return
        acc_ref = refs[3 + has_res]
        kk = pl.program_id(2)

        @pl.when(kk == 0)
        def _():
            acc_ref[...] = part

        @pl.when(kk > 0)
        def _():
            acc_ref[...] += part

        @pl.when(kk == nk - 1)
        def _():
            total = acc_ref[...]
            if has_res:
                total = total + res_ref[...]
            o_ref[...] = total.astype(o_ref.dtype)

    ins = [a, b] + ([res] if has_res else [])
    specs = [a_spec, b_spec] + ([o_spec] if has_res else [])
    return _pcall(body, name=name, out_shape=jax.ShapeDtypeStruct((m, n), out_dtype), grid=(m // tm, n // tn, nk),
                  in_specs=specs, out_specs=o_spec, scratch_shapes=[pltpu.VMEM((tm, tn), F32)] if nk > 1 else [],
                  semantics=("parallel", "parallel", "arbitrary"), vmem_limit=VMEM_LIMIT)(*ins)


ROW_TILE = 256


def _rows(t, width, idx=0):
    return pl.BlockSpec((ROW_TILE, width), lambda i: (i, idx))


def _vec(width):
    return pl.BlockSpec((1, width), lambda i: (0, 0))


def _rmsnorm_fwd(x, g, *, name):
    t, d = x.shape

    def body(x_ref, g_ref, h_ref):
        xv = x_ref[...]
        r = lax.rsqrt(jnp.mean(xv * xv, axis=-1, keepdims=True) + EPS)
        h_ref[...] = (xv * r * g_ref[...]).astype(BF16)

    return _pcall(body, name=name, out_shape=jax.ShapeDtypeStruct((t, d), BF16), grid=(t // ROW_TILE,),
                  in_specs=[_rows(t, d), _vec(d)], out_specs=_rows(t, d), semantics=("parallel",))(x, g)


def _rmsnorm_bwd(x, g, dh, dres, *, name):
    t, d = x.shape

    def body(x_ref, g_ref, dh_ref, dres_ref, dx_ref, dg_ref):
        xv = x_ref[...]
        r = lax.rsqrt(jnp.mean(xv * xv, axis=-1, keepdims=True) + EPS)
        xhat = xv * r
        dhv = dh_ref[...].astype(F32)
        dhg = dhv * g_ref[...]
        dx_ref[...] = dres_ref[...] + r * (dhg - xhat * jnp.mean(dhg * xhat, axis=-1, keepdims=True))
        part = jnp.sum(dhv * xhat, axis=0, keepdims=True)

        @pl.when(pl.program_id(0) == 0)
        def _():
            dg_ref[...] = part

        @pl.when(pl.program_id(0) > 0)
        def _():
            dg_ref[...] += part

    return _pcall(body, name=name, out_shape=(jax.ShapeDtypeStruct((t, d), F32), jax.ShapeDtypeStruct((1, d), F32)),
                  grid=(t // ROW_TILE,), in_specs=[_rows(t, d), _vec(d), _rows(t, d), _rows(t, d)],
                  out_specs=(_rows(t, d), _vec(d)), semantics=("arbitrary",))(x, g, dh, dres)


def _swiglu_fwd(gu, *, name):
    t = gu.shape[0]

    def body(gate_ref, up_ref, ff_ref):
        gate = gate_ref[...]
        ff_ref[...] = (gate * _sigmoid(gate) * up_ref[...]).astype(BF16)

    return _pcall(body, name=name, out_shape=jax.ShapeDtypeStruct((t, D_FF), BF16), grid=(t // ROW_TILE,),
                  in_specs=[_rows(t, D_FF, 0), _rows(t, D_FF, 1)], out_specs=_rows(t, D_FF), semantics=("parallel",))(gu, gu)


def _swiglu_bwd(gu, dff, *, name):
    t = gu.shape[0]

    def body(gate_ref, up_ref, dff_ref, dgu_ref):
        gate = gate_ref[...]
        sig = _sigmoid(gate)
        dffv = dff_ref[...]
        dgu_ref[:, :D_FF] = (dffv * up_ref[...] * sig * (1.0 + gate * (1.0 - sig))).astype(BF16)
        dgu_ref[:, D_FF:] = (dffv * gate * sig).astype(BF16)

    return _pcall(body, name=name, out_shape=jax.ShapeDtypeStruct((t, 2 * D_FF), BF16), grid=(t // ROW_TILE,),
                  in_specs=[_rows(t, D_FF, 0), _rows(t, D_FF, 1), _rows(t, D_FF)], out_specs=_rows(t, 2 * D_FF),
                  semantics=("parallel",))(gu, gu, dff)


def _ple_fwd(x2, pgl, pp, *, name):
    t, d = x2.shape

    def body(x_ref, pgl_ref, pp_ref, o_ref):
        o_ref[...] = x_ref[...] + _sigmoid(pgl_ref[...]) * pp_ref[...]

    return _pcall(body, name=name, out_shape=jax.ShapeDtypeStruct((t, d), F32), grid=(t // ROW_TILE,),
                  in_specs=[_rows(t, d)] * 3, out_specs=_rows(t, d), semantics=("parallel",))(x2, pgl, pp)


def _ple_bwd(dx3, pgl, pp, *, name):
    t, d = dx3.shape

    def body(dx_ref, pgl_ref, pp_ref, dpgl_ref, dpp_ref):
        dxv = dx_ref[...]
        sig = _sigmoid(pgl_ref[...])
        dpp_ref[...] = (dxv * sig).astype(BF16)
        dpgl_ref[...] = (dxv * pp_ref[...] * sig * (1.0 - sig)).astype(BF16)

    return _pcall(body, name=name, out_shape=(jax.ShapeDtypeStruct((t, d), BF16),) * 2, grid=(t // ROW_TILE,),
                  in_specs=[_rows(t, d)] * 3, out_specs=(_rows(t, d),) * 2, semantics=("parallel",))(dx3, pgl, pp)


def _loss_head(x3, g, target, *, name):
    t, d = x3.shape

    def body(x_ref, g_ref, t_ref, dx_ref, dg_ref, loss_ref):
        xv = x_ref[...]
        r = lax.rsqrt(jnp.mean(xv * xv, axis=-1, keepdims=True) + EPS)
        xhat = xv * r
        gv = g_ref[...]
        err = xhat * gv - t_ref[...]
        row_loss = jnp.sum(err * err, axis=-1, keepdims=True) * (0.5 / d)
        lpart = jnp.broadcast_to(jnp.sum(row_loss, axis=0, keepdims=True), (1, LANE))
        dy = err * (1.0 / d)
        dyg = dy * gv
        dx_ref[...] = r * (dyg - xhat * jnp.mean(dyg * xhat, axis=-1, keepdims=True))
        gpart = jnp.sum(dy * xhat, axis=0, keepdims=True)

        @pl.when(pl.program_id(0) == 0)
        def _():
            dg_ref[...] = gpart
            loss_ref[...] = lpart

        @pl.when(pl.program_id(0) > 0)
        def _():
            dg_ref[...] += gpart
            loss_ref[...] += lpart

    return _pcall(body, name=name,
                  out_shape=(jax.ShapeDtypeStruct((t, d), F32), jax.ShapeDtypeStruct((1, d), F32), jax.ShapeDtypeStruct((1, LANE), F32)),
                  grid=(t // ROW_TILE,), in_specs=[_rows(t, d), _vec(d), _rows(t, d)],
                  out_specs=(_rows(t, d), _vec(d), _vec(LANE)), semantics=("arbitrary",))(x3, g, target)


def _shift_down(x, d):
    if d == 0:
        return x
    row = lax.broadcasted_iota(jnp.int32, x.shape, 0)
    return jnp.where(row >= d, pltpu.roll(x, d, 0), 0.0)


def _shift_up(x, d):
    if d == 0:
        return x
    t = x.shape[0]
    row = lax.broadcasted_iota(jnp.int32, x.shape, 0)
    return jnp.where(row < t - d, pltpu.roll(x, t - d, 0), 0.0)


def _colsum(x):
    return jnp.sum(x, axis=0, keepdims=True)


def _col(t, idx_fn):
    return pl.BlockSpec((t, LANE), idx_fn)


def _conv_fwd(x, w_ref, taps):
    acc = None
    for j in range(taps):
        term = w_ref[pl.ds(j, 1), :] * _shift_down(x, taps - 1 - j)
        acc = term if acc is None else acc + term
    return acc


def _conv_bwd(x, dy, w_ref, dw_ref, taps):
    dx = None
    for j in range(taps):
        term = w_ref[pl.ds(j, 1), :] * _shift_up(dy, taps - 1 - j)
        dx = term if dx is None else dx + term
        dw_ref[pl.ds(j, 1), :] = _colsum(dy * _shift_down(x, taps - 1 - j))
    return dx


def _qkv_prep_fwd(proj, conv_w, *, name):
    t = proj.shape[0]
    scale = HEAD_DIM ** -0.5

    def body(x_ref, w_ref, o_ref):
        j = pl.program_id(0)
        c = _conv_fwd(x_ref[...], w_ref, QKV_TAPS)
        s = c * _sigmoid(c)
        r = lax.rsqrt(jnp.sum(s * s, axis=-1, keepdims=True) + EPS)
        f = jnp.where(j < 2 * HEADS, r, 1.0) * jnp.where(j < HEADS, scale, 1.0)
        o_ref[0] = s * f

    return _pcall(body, name=name, out_shape=jax.ShapeDtypeStruct((3 * HEADS, t, LANE), F32), grid=(3 * HEADS,),
                  in_specs=[_col(t, lambda j: (0, j)), pl.BlockSpec((QKV_TAPS, LANE), lambda j: (0, j))],
                  out_specs=pl.BlockSpec((1, t, LANE), lambda j: (j, 0, 0)), semantics=("parallel",),
                  vmem_limit=VMEM_LIMIT)(proj, conv_w)


def _qkv_prep_bwd(proj, conv_w, dqkv, *, name):
    t = proj.shape[0]
    scale = HEAD_DIM ** -0.5

    def body(x_ref, w_ref, d_ref, dx_ref, dw_ref):
        j = pl.program_id(0)
        xv = x_ref[...]
        c = _conv_fwd(xv, w_ref, QKV_TAPS)
        sig = _sigmoid(c)
        s = c * sig
        r = lax.rsqrt(jnp.sum(s * s, axis=-1, keepdims=True) + EPS)
        n0 = s * r
        dv = d_ref[0]
        dn0 = dv * jnp.where(j < HEADS, scale, 1.0)
        ds_norm = r * (dn0 - n0 * jnp.sum(dn0 * n0, axis=-1, keepdims=True))
        ds = jnp.where(j < 2 * HEADS, ds_norm, dv)
        dc = ds * sig * (1.0 + c * (1.0 - sig))
        dx_ref[...] = _conv_bwd(xv, dc, w_ref, dw_ref, QKV_TAPS).astype(BF16)

    return _pcall(body, name=name,
                  out_shape=(jax.ShapeDtypeStruct((t, 3 * A_DIM), BF16), jax.ShapeDtypeStruct((QKV_TAPS, 3 * A_DIM), F32)),
                  grid=(3 * HEADS,),
                  in_specs=[_col(t, lambda j: (0, j)), pl.BlockSpec((QKV_TAPS, LANE), lambda j: (0, j)),
                            pl.BlockSpec((1, t, LANE), lambda j: (j, 0, 0))],
                  out_specs=(_col(t, lambda j: (0, j)), pl.BlockSpec((QKV_TAPS, LANE), lambda j: (0, j))),
                  semantics=("parallel",), vmem_limit=VMEM_LIMIT)(proj, conv_w, dqkv)


def _lane_pick(x, lane_idx, lane):
    return jnp.broadcast_to(jnp.sum(jnp.where(lane == lane_idx, x, 0.0), axis=-1, keepdims=True), x.shape)


def _gates_fwd(proj, alog, dtb, *, name):
    t = proj.shape[0]

    def body(x_ref, alog_ref, dtb_ref, g_ref, b_ref):
        xv = x_ref[...]
        lane = lax.broadcasted_iota(jnp.int32, xv.shape, 1)
        gall = -jnp.exp(alog_ref[...]) * _softplus(xv + dtb_ref[...])
        ball = _sigmoid(xv)
        for h in range(HEADS):
            g_ref[h] = _lane_pick(gall, h, lane)
            b_ref[h] = _lane_pick(ball, HEADS + h, lane)

    out = jax.ShapeDtypeStruct((HEADS, t, LANE), F32)
    whole = pl.BlockSpec((HEADS, t, LANE), lambda i: (0, 0, 0))
    return _pcall(body, name=name, out_shape=(out, out), grid=(1,),
                  in_specs=[_col(t, lambda i: (0, AB_COL // LANE)), _vec(LANE), _vec(LANE)], out_specs=(whole, whole),
                  semantics=("arbitrary",), vmem_limit=VMEM_LIMIT)(proj, alog, dtb)


def _gates_bwd(proj, alog, dtb, dg, dbeta, *, name):
    t = proj.shape[0]

    def body(x_ref, alog_ref, dtb_ref, dg_ref, db_ref, dab_ref, dalog_ref, ddtb_ref):
        xv = x_ref[...]
        lane = lax.broadcasted_iota(jnp.int32, xv.shape, 1)
        lane1 = lax.broadcasted_iota(jnp.int32, (1, LANE), 1)
        z = xv + dtb_ref[...]
        nea = -jnp.exp(alog_ref[...])
        da_f = nea * _sigmoid(z)
        g_f = nea * _softplus(z)
        ball = _sigmoid(xv)
        db_f = ball * (1.0 - ball)
        dab = jnp.zeros_like(xv)
        dalog = jnp.zeros((1, LANE), F32)
        for h in range(HEADS):
            dgh = dg_ref[h]
            dab = dab + jnp.where(lane == h, dgh * da_f, 0.0) + jnp.where(lane == HEADS + h, db_ref[h] * db_f, 0.0)
            dalog = dalog + jnp.where(lane1 == h, _colsum(dgh * g_f), 0.0)
        dab_ref[...] = dab.astype(BF16)
        dalog_ref[...] = dalog
        ddtb_ref[...] = jnp.where(lane1 < HEADS, _colsum(dab), 0.0)

    whole = pl.BlockSpec((HEADS, t, LANE), lambda i: (0, 0, 0))
    vec = jax.ShapeDtypeStruct((1, LANE), F32)
    return _pcall(body, name=name, out_shape=(jax.ShapeDtypeStruct((t, LANE), BF16), vec, vec), grid=(1,),
                  in_specs=[_col(t, lambda i: (0, AB_COL // LANE)), _vec(LANE), _vec(LANE), whole, whole],
                  out_specs=(_col(t, lambda i: (0, 0)), _vec(LANE), _vec(LANE)), semantics=("arbitrary",),
                  vmem_limit=VMEM_LIMIT)(proj, alog, dtb, dg, dbeta)


Z_COL = 3 * A_DIM // LANE


def _apost_fwd(o, proj, gn, *, name):
    t = proj.shape[0]

    def body(o_ref, z_ref, gn_ref, y_ref):
        ov = o_ref[0]
        z = z_ref[...]
        r = lax.rsqrt(jnp.mean(ov * ov, axis=-1, keepdims=True) + EPS)
        y_ref[...] = (ov * r * gn_ref[...] * (z * _sigmoid(z))).astype(BF16)

    return _pcall(body, name=name, out_shape=jax.ShapeDtypeStruct((t, A_DIM), BF16), grid=(HEADS,),
                  in_specs=[pl.BlockSpec((1, t, LANE), lambda h: (h, 0, 0)), _col(t, lambda h: (0, Z_COL + h)),
                            pl.BlockSpec((1, LANE), lambda h: (0, 0))],
                  out_specs=_col(t, lambda h: (0, h)), semantics=("parallel",), vmem_limit=VMEM_LIMIT)(o, proj, gn)


def _apost_bwd(o, proj, gn, dmixed, *, name):
    t = proj.shape[0]

    def body(o_ref, z_ref, gn_ref, d_ref, do_ref, dz_ref, dgn_ref):
        ov = o_ref[0]
        z = z_ref[...]
        gnv = gn_ref[...]
        dv = d_ref[...]
        r = lax.rsqrt(jnp.mean(ov * ov, axis=-1, keepdims=True) + EPS)
        ohat = ov * r
        sig = _sigmoid(z)
        dy = dv * (z * sig)
        dz_ref[...] = (dv * ohat * gnv * sig * (1.0 + z * (1.0 - sig))).astype(BF16)
        dyo = dy * gnv
        do_ref[0] = r * (dyo - ohat * jnp.mean(dyo * ohat, axis=-1, keepdims=True))
        part = _colsum(dy * ohat)

        @pl.when(pl.program_id(0) == 0)
        def _():
            dgn_ref[...] = part

        @pl.when(pl.program_id(0) > 0)
        def _():
            dgn_ref[...] += part

    return _pcall(body, name=name,
                  out_shape=(jax.ShapeDtypeStruct((HEADS, t, LANE), F32), jax.ShapeDtypeStruct((t, A_DIM), BF16),
                             jax.ShapeDtypeStruct((1, LANE), F32)),
                  grid=(HEADS,),
                  in_specs=[pl.BlockSpec((1, t, LANE), lambda h: (h, 0, 0)), _col(t, lambda h: (0, Z_COL + h)),
                            pl.BlockSpec((1, LANE), lambda h: (0, 0)), _col(t, lambda h: (0, h))],
                  out_specs=(pl.BlockSpec((1, t, LANE), lambda h: (h, 0, 0)), _col(t, lambda h: (0, h)),
                             pl.BlockSpec((1, LANE), lambda h: (0, 0))),
                  semantics=("arbitrary",), vmem_limit=VMEM_LIMIT)(o, proj, gn, dmixed)


POOL_COL = (AB_COL + LANE) // LANE
CB_COL = POOL_COL + POOL_DIM // LANE
CC_COL = CB_COL + CONV_DIM // LANE
CH_COL = CC_COL + CONV_DIM // LANE
MAX_WIN_LOG2 = 4


def _window_sums(x, shift):
    sums = []
    cur = x
    for k in range(MAX_WIN_LOG2):
        cur = cur + shift(cur, 1 << k)
        sums.append(cur)
    return sums


def _pick_window(sums, win):
    out = sums[-1]
    for k in range(MAX_WIN_LOG2 - 2, -1, -1):
        out = jnp.where(win == float(2 << k), sums[k], out)
    return out


def _pool_counts(shape, win):
    row = lax.broadcasted_iota(jnp.int32, shape, 0).astype(F32)
    return jnp.minimum(row + 1.0, win)


def _pool_fwd(proj, win, wbd, scale, *, name):
    t = proj.shape[0]

    def body(x_ref, win_ref, w_ref, s_ref, y_ref):
        xv = x_ref[...]
        winv = win_ref[...]
        pooled = _pick_window(_window_sums(xv, _shift_down), winv) / _pool_counts(xv.shape, winv) - xv
        y_ref[...] = (_dot(pooled, w_ref[0], NN) * s_ref[...]).astype(BF16)

    nb = POOL_DIM // LANE
    vec = pl.BlockSpec((1, LANE), lambda b: (0, b))
    return _pcall(body, name=name, out_shape=jax.ShapeDtypeStruct((t, POOL_DIM), BF16), grid=(nb,),
                  in_specs=[_col(t, lambda b: (0, POOL_COL + b)), vec, pl.BlockSpec((1, LANE, LANE), lambda b: (b, 0, 0)), vec],
                  out_specs=_col(t, lambda b: (0, b)), semantics=("parallel",), vmem_limit=VMEM_LIMIT)(proj, win, wbd, scale)


def _pool_bwd(proj, win, wbd, scale, dmixed, *, name):
    t = proj.shape[0]

    def body(x_ref, win_ref, w_ref, s_ref, d_ref, dx_ref, dw_ref, ds_ref):
        xv = x_ref[...]
        winv = win_ref[...]
        cnt = _pool_counts(xv.shape, winv)
        pooled = _pick_window(_window_sums(xv, _shift_down), winv) / cnt - xv
        dv = d_ref[...]
        ds_ref[...] = _colsum(dv * _dot(pooled, w_ref[0], NN))
        dy0 = dv * s_ref[...]
        dw_ref[0] = _dot(pooled, dy0, TN)
        dpooled = _dot(dy0, w_ref[0], NT)
        dmean = dpooled / cnt
        dx_ref[...] = (_pick_window(_window_sums(dmean, _shift_up), winv) - dpooled).astype(BF16)

    nb = POOL_DIM // LANE
    vec = pl.BlockSpec((1, LANE), lambda b: (0, b))
    mat = pl.BlockSpec((1, LANE, LANE), lambda b: (b, 0, 0))
    first = A_DIM // LANE
    return _pcall(body, name=name,
                  out_shape=(jax.ShapeDtypeStruct((t, POOL_DIM), BF16), jax.ShapeDtypeStruct((nb, LANE, LANE), F32),
                             jax.ShapeDtypeStruct((1, POOL_DIM), F32)),
                  grid=(nb,),
                  in_specs=[_col(t, lambda b: (0, POOL_COL + b)), vec, mat, vec, _col(t, lambda b: (0, first + b))],
                  out_specs=(_col(t, lambda b: (0, b)), mat, vec), semantics=("parallel",),
                  vmem_limit=VMEM_LIMIT)(proj, win, wbd, scale, dmixed)


def _sconv_fwd(proj, w, *, name):
    t = proj.shape[0]

    def body(cb_ref, cc_ref, ch_ref, w_ref, y_ref):
        y_ref[...] = (cb_ref[...] * _conv_fwd(cc_ref[...] * ch_ref[...], w_ref, CONV_TAPS)).astype(BF16)

    nb = CONV_DIM // LANE
    return _pcall(body, name=name, out_shape=jax.ShapeDtypeStruct((t, CONV_DIM), BF16), grid=(nb,),
                  in_specs=[_col(t, lambda b: (0, CB_COL + b)), _col(t, lambda b: (0, CC_COL + b)),
                            _col(t, lambda b: (0, CH_COL + b)), pl.BlockSpec((CONV_TAPS, LANE), lambda b: (0, b))],
                  out_specs=_col(t, lambda b: (0, b)), semantics=("parallel",), vmem_limit=VMEM_LIMIT)(proj, proj, proj, w)


def _sconv_bwd(proj, w, dmixed, *, name):
    t = proj.shape[0]

    def body(cb_ref, cc_ref, ch_ref, w_ref, d_ref, dcb_ref, dcc_ref, dch_ref, dw_ref):
        cc = cc_ref[...]
        ch = ch_ref[...]
        u = cc * ch
        dv = d_ref[...]
        dcb_ref[...] = (dv * _conv_fwd(u, w_ref, CONV_TAPS)).astype(BF16)
        du = _conv_bwd(u, dv * cb_ref[...], w_ref, dw_ref, CONV_TAPS)
        dcc_ref[...] = (du * ch).astype(BF16)
        dch_ref[...] = (du * cc).astype(BF16)

    nb = CONV_DIM // LANE
    first = (A_DIM + POOL_DIM) // LANE
    act = jax.ShapeDtypeStruct((t, CONV_DIM), BF16)
    wspec = pl.BlockSpec((CONV_TAPS, LANE), lambda b: (0, b))
    ospec = _col(t, lambda b: (0, b))
    return _pcall(body, name=name, out_shape=(act, act, act, jax.ShapeDtypeStruct((CONV_TAPS, CONV_DIM), F32)), grid=(nb,),
                  in_specs=[_col(t, lambda b: (0, CB_COL + b)), _col(t, lambda b: (0, CC_COL + b)),
                            _col(t, lambda b: (0, CH_COL + b)), wspec, _col(t, lambda b: (0, first + b))],
                  out_specs=(ospec, ospec, ospec, wspec), semantics=("parallel",),
                  vmem_limit=VMEM_LIMIT)(proj, proj, proj, w, dmixed)


def _chunk_masks():
    r = lax.broadcasted_iota(jnp.int32, (CHUNK, CHUNK), 0)
    c = lax.broadcasted_iota(jnp.int32, (CHUNK, CHUNK), 1)
    return r >= c, r > c, jnp.where(r == c, 1.0, 0.0).astype(F32)


def _split(a):
    hi = a.astype(BF16)
    return hi, (a - hi.astype(F32)).astype(BF16)


def _dot_split(a, b, dims):
    (ah, al), (bh, bl) = a, b
    return _dot(ah, bh, dims) + _dot(ah, bl, dims) + _dot(al, bh, dims)


def _tri_inv(lows, eye):
    xs = [eye - low for low in lows]
    ps = [_split(low) for low in lows]
    ps = [_split(_dot_split(p, p, NN)) for p in ps]
    for i in range(5):
        xs = [x + _dot_split(_split(x), p, NN) for x, p in zip(xs, ps)]
        if i < 4:
            ps = [_split(_dot_split(p, p, NN)) for p in ps]
    return xs


def _prefix_sum_rows(x):
    for k in range(6):
        x = x + _shift_down(x, 1 << k)
    return x


def _suffix_sum_rows(x):
    for k in range(6):
        x = x + _shift_up(x, 1 << k)
    return x


def _chunk_decay(g, incl):
    gcb = _prefix_sum_rows(g)
    gtot = _colsum(g)
    col = gcb[:, :CHUNK]
    row = gcb.T[:CHUNK, :]
    decay = jnp.exp(jnp.where(incl, col - row, -1e30))
    return gcb, gtot, decay


CHUNKS_PER_STEP = 2


def _heads_of(ref, base, rows):
    return [ref[base + h, rows, :] for h in range(HEADS)]


def _chunk_rows(j):
    return pl.ds(j * CHUNK, CHUNK)


def _deltanet_prep(qkv, g, beta, *, name):
    t = qkv.shape[1]
    n_chunks = t // CHUNK
    per = CHUNKS_PER_STEP
    probs = [(j, h) for j in range(per) for h in range(HEADS)]

    def body(qkv_ref, g_ref, b_ref, u_ref, w_ref, qg_ref, kg_ref, attn_ref, tm_ref):
        incl, strict, eye = _chunk_masks()
        q = [qkv_ref[h, _chunk_rows(j), :] for j, h in probs]
        k = [qkv_ref[HEADS + h, _chunk_rows(j), :] for j, h in probs]
        v = [qkv_ref[2 * HEADS + h, _chunk_rows(j), :] for j, h in probs]
        bv = [b_ref[h, _chunk_rows(j), :] for j, h in probs]
        dec = [_chunk_decay(g_ref[h, _chunk_rows(j), :], incl) for j, h in probs]
        kb = [a * b for a, b in zip(k, bv)]
        low = [jnp.where(strict, _dot(a, b, NT) * d[2], 0.0) for a, b, d in zip(kb, k, dec)]
        tm = _tri_inv(low, eye)
        egc = [jnp.exp(d[0]) for d in dec]
        u = [_dot(m, a * b, NN) for m, a, b in zip(tm, v, bv)]
        w = [_dot(m, a * e, NN) for m, a, e in zip(tm, kb, egc)]
        attn = [_dot(a, b, NT) * d[2] for a, b, d in zip(q, k, dec)]
        for i, (j, h) in enumerate(probs):
            rows = _chunk_rows(j)
            u_ref[h, rows, :] = u[i]
            w_ref[h, rows, :] = w[i].astype(BF16)
            qg_ref[h, rows, :] = (q[i] * egc[i]).astype(BF16)
            kg_ref[h, rows, :] = (k[i] * jnp.exp(dec[i][1] - dec[i][0])).astype(BF16)
            attn_ref[j, h] = attn[i].astype(BF16)
            tm_ref[j, h] = tm[i]

    act = lambda heads: pl.BlockSpec((heads, per * CHUNK, LANE), lambda n: (0, n, 0))
    mat = pl.BlockSpec((per, HEADS, CHUNK, CHUNK), lambda n: (n, 0, 0, 0))
    return _pcall(
        body, name=name,
        out_shape=(jax.ShapeDtypeStruct((HEADS, t, LANE), F32),) + (jax.ShapeDtypeStruct((HEADS, t, LANE), BF16),) * 3
        + (jax.ShapeDtypeStruct((n_chunks, HEADS, CHUNK, CHUNK), BF16), jax.ShapeDtypeStruct((n_chunks, HEADS, CHUNK, CHUNK), F32)),
        grid=(n_chunks // per,), in_specs=[act(3 * HEADS), act(HEADS), act(HEADS)],
        out_specs=(act(HEADS),) * 4 + (mat, mat), semantics=("parallel",), vmem_limit=VMEM_LIMIT)(qkv, g, beta)


SCAN_CHUNKS_PER_STEP = 4


def _deltanet_scan(u, w, qg, kg, attn, g, *, name):
    t = u.shape[1]
    n_chunks = t // CHUNK
    per = SCAN_CHUNKS_PER_STEP

    def body(u_ref, w_ref, qg_ref, kg_ref, attn_ref, g_ref, o_ref, vn_ref, st_ref, s_ref):
        @pl.when(pl.program_id(0) == 0)
        def _():
            s_ref[...] = jnp.zeros_like(s_ref)

        for j in range(per):
            rows = _chunk_rows(j)
            s = [s_ref[h] for h in range(HEADS)]
            vn = [u_ref[h, rows, :] - _dot(w_ref[h, rows, :], s[h], NN) for h in range(HEADS)]
            o = [_dot(qg_ref[h, rows, :], s[h], NN) + _dot(attn_ref[j, h], vn[h], NN) for h in range(HEADS)]
            eg = [jnp.exp(_colsum(g_ref[h, rows, :])) for h in range(HEADS)]
            for h in range(HEADS):
                st_ref[j, h] = s[h]
                s_ref[h] = s[h] * eg[h] + _dot(kg_ref[h, rows, :], vn[h], TN)
                o_ref[h, rows, :] = o[h]
                vn_ref[h, rows, :] = vn[h]

    act = pl.BlockSpec((HEADS, per * CHUNK, LANE), lambda n: (0, n, 0))
    out = jax.ShapeDtypeStruct((HEADS, t, LANE), F32)
    return _pcall(
        body, name=name, out_shape=(out, out, jax.ShapeDtypeStruct((n_chunks, HEADS, LANE, LANE), F32)), grid=(n_chunks // per,),
        in_specs=[act] * 4 + [pl.BlockSpec((per, HEADS, CHUNK, CHUNK), lambda n: (n, 0, 0, 0)), act],
        out_specs=(act, act, pl.BlockSpec((per, HEADS, LANE, LANE), lambda n: (n, 0, 0, 0))),
        scratch_shapes=[pltpu.VMEM((HEADS, LANE, LANE), F32)], semantics=("arbitrary",))(u, w, qg, kg, attn, g)


def _deltanet_bscan(w, qg, kg, attn, g, do, *, name):
    t = w.shape[1]
    n_chunks = t // CHUNK
    per = SCAN_CHUNKS_PER_STEP
    steps = n_chunks // per

    def body(w_ref, qg_ref, kg_ref, attn_ref, g_ref, do_ref, dvn_ref, dsn_ref, ds_ref):
        @pl.when(pl.program_id(0) == 0)
        def _():
            ds_ref[...] = jnp.zeros_like(ds_ref)

        for j in reversed(range(per)):
            rows = _chunk_rows(j)
            dsn = [ds_ref[h] for h in range(HEADS)]
            dov = [do_ref[h, rows, :] for h in range(HEADS)]
            dvn = [_dot(attn_ref[j, h], dov[h], TN) + _dot(kg_ref[h, rows, :], dsn[h], NN) for h in range(HEADS)]
            eg = [jnp.exp(_colsum(g_ref[h, rows, :])) for h in range(HEADS)]
            for h in range(HEADS):
                dsn_ref[j, h] = dsn[h]
                ds_ref[h] = _dot(qg_ref[h, rows, :], dov[h], TN) + eg[h] * dsn[h] - _dot(w_ref[h, rows, :], dvn[h], TN)
                dvn_ref[h, rows, :] = dvn[h]

    act = pl.BlockSpec((HEADS, per * CHUNK, LANE), lambda n: (0, steps - 1 - n, 0))
    return _pcall(
        body, name=name,
        out_shape=(jax.ShapeDtypeStruct((HEADS, t, LANE), F32), jax.ShapeDtypeStruct((n_chunks, HEADS, LANE, LANE), F32)),
        grid=(steps,),
        in_specs=[act] * 3 + [pl.BlockSpec((per, HEADS, CHUNK, CHUNK), lambda n: (steps - 1 - n, 0, 0, 0)), act, act],
        out_specs=(act, pl.BlockSpec((per, HEADS, LANE, LANE), lambda n: (steps - 1 - n, 0, 0, 0))),
        scratch_shapes=[pltpu.VMEM((HEADS, LANE, LANE), F32)], semantics=("arbitrary",))(w, qg, kg, attn, g, do)


def _sum_all(x):
    return jnp.sum(jnp.sum(x, axis=1, keepdims=True), axis=0, keepdims=True)


def _rowsum(x):
    return jnp.sum(x, axis=1, keepdims=True)


def _deltanet_post(qkv, g, beta, tmats, states, dstates, do, dvn, vn, *, name):
    t = qkv.shape[1]
    n_chunks = t // CHUNK
    per = CHUNKS_PER_STEP
    probs = [(j, h) for j in range(per) for h in range(HEADS)]

    def body(qkv_ref, g_ref, b_ref, tm_ref, st_ref, dsn_ref, do_ref, dvn_ref, vn_ref, dqkv_ref, dg_ref, db_ref):
        incl, strict, _ = _chunk_masks()
        ones = jnp.ones((CHUNK, LANE), BF16)
        last_row = lax.broadcasted_iota(jnp.int32, (CHUNK, LANE), 0) == CHUNK - 1
        z = lambda f, *cols: [f(*a) for a in zip(*cols)]
        q = [qkv_ref[h, _chunk_rows(j), :] for j, h in probs]
        k = [qkv_ref[HEADS + h, _chunk_rows(j), :] for j, h in probs]
        v = [qkv_ref[2 * HEADS + h, _chunk_rows(j), :] for j, h in probs]
        bv = [b_ref[h, _chunk_rows(j), :] for j, h in probs]
        dov = [do_ref[h, _chunk_rows(j), :] for j, h in probs]
        dvn_ = [dvn_ref[h, _chunk_rows(j), :] for j, h in probs]
        vn_ = [vn_ref[h, _chunk_rows(j), :] for j, h in probs]
        tm = [tm_ref[j, h] for j, h in probs]
        s = [st_ref[j, h] for j, h in probs]
        dsn = [dsn_ref[j, h] for j, h in probs]
        dec = [_chunk_decay(g_ref[h, _chunk_rows(j), :], incl) for j, h in probs]
        decay = [d[2] for d in dec]
        egc = [jnp.exp(d[0]) for d in dec]
        ekg = [jnp.exp(d[1] - d[0]) for d in dec]
        kb = z(lambda a, b: a * b, k, bv)
        vb = z(lambda a, b: a * b, v, bv)
        kbg = z(lambda a, b: a * b, kb, egc)
        qg = z(lambda a, b: a * b, q, egc)
        kg = z(lambda a, b: a * b, k, ekg)
        kk = z(lambda a, b: _dot(a, b, NT), kb, k)
        qk = z(lambda a, b: _dot(a, b, NT), q, k)
        dattn = z(lambda a, b: jnp.where(incl, _dot(a, b, NT), 0.0), dov, vn_)
        dqg = z(lambda a, b: _dot(a, b, NT), dov, s)
        dkg = z(lambda a, b: _dot(a, b, NT), vn_, dsn)
        dglast = z(lambda a, b, c, d, e: _sum_all(a * b) * jnp.exp(e[1]) + _sum_all(c * d), s, dsn, dkg, kg, dec)
        dw = z(lambda a, b: -_dot(a, b, NT), dvn_, s)
        dtm = z(lambda a, b, c, d: _dot(a, b, NT) + _dot(c, d, NT), dvn_, vb, dw, kbg)
        dvb = z(lambda a, b: _dot(a, b, TN), tm, dvn_)
        dkbg = z(lambda a, b: _dot(a, b, TN), tm, dw)
        dlow = z(lambda a, b: jnp.where(strict, -_dot(_dot(a, b, TN), a, NT), 0.0), tm, dtm)
        dkk = z(lambda a, b: a * b, dlow, decay)
        dqk = z(lambda a, b: a * b, dattn, decay)
        dkb = z(lambda a, b, c, d: _dot(a, b, NN) + c * d, dkk, k, dkbg, egc)
        dk = z(lambda a, b, c, d, e, f, g_, h_: _dot(a, b, TN) + _dot(c, d, TN) + e * f + g_ * h_, dkk, kb, dqk, q, dkg, ekg, dkb, bv)
        dq = z(lambda a, b, c, d: _dot(a, b, NN) + c * d, dqk, k, dqg, egc)
        m = z(lambda a, b, c, d, e: (a * b + c * d) * e, dlow, kk, dattn, qk, decay)
        mcol = [_dot(mh, ones, TN) + _dot(ml, ones, TN) for mh, ml in (_split(a) for a in m)]
        for i, (j, h) in enumerate(probs):
            rows = _chunk_rows(j)
            dqkv_ref[h, rows, :] = dq[i]
            dqkv_ref[HEADS + h, rows, :] = dk[i]
            dqkv_ref[2 * HEADS + h, rows, :] = dvb[i] * bv[i]
            db_ref[h, rows, :] = jnp.broadcast_to(_rowsum(dkb[i] * k[i] + dvb[i] * v[i]), (CHUNK, LANE))
            dgc = (_rowsum(dqg[i] * qg[i] + dkbg[i] * kbg[i] - dkg[i] * kg[i]) + _rowsum(m[i]) - mcol[i]
                   + jnp.where(last_row, dglast[i], 0.0))
            dg_ref[h, rows, :] = _suffix_sum_rows(dgc)

    act = lambda heads: pl.BlockSpec((heads, per * CHUNK, LANE), lambda n: (0, n, 0))
    mat = lambda d: pl.BlockSpec((per, HEADS, d, d), lambda n: (n, 0, 0, 0))
    out = jax.ShapeDtypeStruct((HEADS, t, LANE), F32)
    return _pcall(
        body, name=name, out_shape=(jax.ShapeDtypeStruct((3 * HEADS, t, LANE), F32), out, out), grid=(n_chunks // per,),
        in_specs=[act(3 * HEADS), act(HEADS), act(HEADS), mat(CHUNK), mat(LANE), mat(LANE), act(HEADS), act(HEADS), act(HEADS)],
        out_specs=(act(3 * HEADS), act(HEADS), act(HEADS)), semantics=("parallel",),
        vmem_limit=VMEM_LIMIT)(qkv, g, beta, tmats, states, dstates, do, dvn, vn)


ANY = pl.BlockSpec(memory_space=pl.ANY)
PEERS = N_DEV - 1


def _all_gather(arrays, *, name):
    n = len(arrays)

    def body(*refs):
        ins, outs = refs[:n], refs[n:2 * n]
        send_sems, recv_sems, local_sems = refs[2 * n:]
        x, y, c = lax.axis_index("x"), lax.axis_index("y"), lax.axis_index("c")
        me, sibling = (x, y, c), (x, y, 1 - c)
        chips = [(1 - x, y), (x, 1 - y), (1 - x, 1 - y)]

        def copy(a, k, block, to, src=None):
            dst = outs[a].at[4 * block[0] + 2 * block[1] + block[2]]
            return pltpu.make_async_remote_copy(src_ref=dst if src is None else src, dst_ref=dst, send_sem=send_sems.at[a * PEERS + k],
                                                recv_sem=recv_sems.at[a * PEERS + k], device_id=to, device_id_type=MESH)

        local = [pltpu.make_async_copy(ins[a], outs[a].at[4 * x + 2 * y + c], local_sems.at[a]) for a in range(n)]
        for cp in local:
            cp.start()
        first = []
        for a in range(n):
            first.append(copy(a, 0, me, sibling, src=ins[a]))
            first += [copy(a, 1 + j, me, (*chip, c), src=ins[a]) for j, chip in enumerate(chips)]
        for cp in first:
            cp.start()
        passed = []
        for a in range(n):
            for j, chip in enumerate(chips):
                copy(a, 1 + j, (*chip, c), me).wait_recv()
                fwd = copy(a, 4 + j, (*chip, c), sibling)
                fwd.start()
                passed.append(fwd)
        for a in range(n):
            copy(a, 0, sibling, me).wait_recv()
            for j, chip in enumerate(chips):
                copy(a, 4 + j, (*chip, 1 - c), me).wait_recv()
        for cp in first + passed:
            cp.wait_send()
        for cp in local:
            cp.wait()

    return _pcall(body, name=name, out_shape=tuple(jax.ShapeDtypeStruct((N_DEV,) + a.shape, a.dtype) for a in arrays),
                  in_specs=[ANY] * n, out_specs=(ANY,) * n,
                  scratch_shapes=[pltpu.SemaphoreType.DMA((n * PEERS,)), pltpu.SemaphoreType.DMA((n * PEERS,)),
                                  pltpu.SemaphoreType.DMA((n,))])(*arrays)


def _exchange_blocks(arrays, *, name):
    n = len(arrays)

    def body(*refs):
        ins, outs = refs[:n], refs[n:2 * n]
        send_sems, recv_sems, local_sems = refs[2 * n:]
        x, y, c = lax.axis_index("x"), lax.axis_index("y"), lax.axis_index("c")
        mine = 4 * x + 2 * y + c
        copies = []
        for a in range(n):
            lc = pltpu.make_async_copy(ins[a].at[mine], outs[a].at[mine], local_sems.at[a])
            lc.start()
            copies.append(lc)
            for k in range(1, N_DEV):
                px = 1 - x if k & 4 else x
                py = 1 - y if k & 2 else y
                pc = 1 - c if k & 1 else c
                cp = pltpu.make_async_remote_copy(src_ref=ins[a].at[4 * px + 2 * py + pc], dst_ref=outs[a].at[mine],
                                                  send_sem=send_sems.at[a * PEERS + k - 1], recv_sem=recv_sems.at[a * PEERS + k - 1],
                                                  device_id=(px, py, pc), device_id_type=MESH)
                cp.start()
                copies.append(cp)
        for cp in copies:
            cp.wait()

    return _pcall(body, name=name, out_shape=tuple(jax.ShapeDtypeStruct(a.shape, a.dtype) for a in arrays),
                  in_specs=[ANY] * n, out_specs=(ANY,) * n,
                  scratch_shapes=[pltpu.SemaphoreType.DMA((n * PEERS,)), pltpu.SemaphoreType.DMA((n * PEERS,)),
                                  pltpu.SemaphoreType.DMA((n,))])(*arrays)


def _adamw_reduce(w, parts, m, v, *, name):
    r, c = w.shape
    tr = _tile(r, 512, 16)
    bc1 = 1.0 - ADAM_B1 ** ADAM_STEP
    bc2 = 1.0 - ADAM_B2 ** ADAM_STEP

    def body(w_ref, p_ref, m_ref, v_ref, g_ref, d_ref, nm_ref, nv_ref):
        g = p_ref[0].astype(F32)
        for s in range(1, N_DEV):
            g = g + p_ref[s].astype(F32)
        nm = ADAM_B1 * m_ref[...] + (1.0 - ADAM_B1) * g
        nv = ADAM_B2 * v_ref[...] + (1.0 - ADAM_B2) * (g * g)
        g_ref[...] = g
        nm_ref[...] = nm
        nv_ref[...] = nv
        d_ref[...] = -ADAM_LR * ((nm / bc1) / (jnp.sqrt(nv / bc2) + ADAM_EPS) + ADAM_WD * w_ref[...])

    spec = pl.BlockSpec((tr, c), lambda i: (i, 0))
    out = jax.ShapeDtypeStruct((r, c), F32)
    return _pcall(body, name=name, out_shape=(out,) * 4, grid=(r // tr,),
                  in_specs=[spec, pl.BlockSpec((N_DEV, tr, c), lambda i: (0, i, 0)), spec, spec], out_specs=(spec,) * 4,
                  semantics=("parallel",), vmem_limit=VMEM_LIMIT)(w, parts, m, v)


def _pool_windows():
    return jnp.repeat(jnp.asarray(POOL_WINDOWS, F32), POOL_DIM // len(POOL_WINDOWS))[None, :]


def _block_diag_pairs(pool_w):
    z = jnp.zeros_like(pool_w[0])
    return jnp.stack([jnp.block([[pool_w[2 * b], z], [z, pool_w[2 * b + 1]]]) for b in range(2)])


def _pad_lanes(vec):
    return jnp.zeros((1, LANE), F32).at[0, :vec.shape[0]].set(vec)


def _layer_fwd(x, p_i, wt, li):
    tag = f"l{li}"
    h1 = _rmsnorm_fwd(x, wt["norm1_g"], name="rmsnorm_fwd")
    proj = _matmul(h1, wt["w_in"], "nn", name="mm_in")
    qkv = _qkv_prep_fwd(proj, wt["conv_qkv"], name="qkv_prep_fwd")
    g, beta = _gates_fwd(proj, wt["a_log"], wt["dt_bias"], name="gates_fwd")
    u, w, qg, kg, attn, tmats = _deltanet_prep(qkv, g, beta, name="deltanet_prep")
    o, vn, states = _deltanet_scan(u, w, qg, kg, attn, g, name="deltanet_scan")
    o_a = _apost_fwd(o, proj, wt["onorm_g"], name="apost_fwd")
    o_b = _pool_fwd(proj, wt["pool_win"], wt["pool_wbd"], wt["pool_scale"], name="pool_fwd")
    o_c = _sconv_fwd(proj, wt["sconv_w"], name="sconv_fwd")
    mixed = jnp.concatenate([o_a, o_b, o_c], axis=1)
    x1 = _matmul(mixed, wt["w_out"], "nn", res=x, name="mm_out")
    h2 = _rmsnorm_fwd(x1, wt["norm2_g"], name="rmsnorm_fwd")
    gu = _matmul(h2, wt["w_gu"], "nn", name="mm_gu")
    ff = _swiglu_fwd(gu, name="swiglu_fwd")
    x2 = _matmul(ff, wt["w_down"], "nn", res=x1, name="mm_down")
    pgl = _matmul(x2, wt["ple_gate"], "nn", name="mm_pleg")
    pp = _matmul(p_i, wt["ple_proj"], "nn", name="mm_plep")
    x3 = _ple_fwd(x2, pgl, pp, name="ple_fwd")
    del tag
    saved = dict(x=x, h1=h1, proj=proj, qkv=qkv, g=g, beta=beta, o=o, states=states, tmats=tmats, mixed=mixed, x1=x1, h2=h2,
                 gu=gu, ff=ff, x2=x2, pgl=pgl, pp=pp, p=p_i, w=w, qg=qg, kg=kg, attn=attn, vn=vn)
    return x3, saved


def _layer_bwd(dx3, sv, wt):
    gr = {}
    dpgl, dpp = _ple_bwd(dx3, sv["pgl"], sv["pp"], name="ple_bwd")
    gr["ple_proj"] = _matmul(sv["p"], dpp, "tn", name="mm_dplep")
    gr["ple_gate"] = _matmul(sv["x2"], dpgl, "tn", name="mm_dpleg")
    dx2 = _matmul(dpgl, wt["ple_gate"], "nt", res=dx3, name="mm_dx2")
    gr["w_down"] = _matmul(sv["ff"], dx2, "tn", name="mm_ddown")
    dff = _matmul(dx2, wt["w_down"], "nt", name="mm_dff")
    dgu = _swiglu_bwd(sv["gu"], dff, name="swiglu_bwd")
    dwgu = _matmul(sv["h2"], dgu, "tn", name="mm_dgu")
    gr["w_gate"], gr["w_up"] = dwgu[:, :D_FF], dwgu[:, D_FF:]
    dh2 = _matmul(dgu, wt["w_gu"], "nt", name="mm_dh2")
    dx1, gr["norm2_g"] = _rmsnorm_bwd(sv["x1"], wt["norm2_g"], dh2, dx2, name="rmsnorm_bwd")
    gr["w_out"] = _matmul(sv["mixed"], dx1, "tn", name="mm_dout")
    dmixed = _matmul(dx1, wt["w_out"], "nt", name="mm_dmixed")
    proj = sv["proj"]
    dcb, dcc, dch, gr["sconv_w"] = _sconv_bwd(proj, wt["sconv_w"], dmixed, name="sconv_bwd")
    dhp, dwbd, gr["pool_scale"] = _pool_bwd(proj, wt["pool_win"], wt["pool_wbd"], wt["pool_scale"], dmixed, name="pool_bwd")
    half = LANE // 2
    gr["pool_w"] = jnp.stack([dwbd[0, :half, :half], dwbd[0, half:, half:], dwbd[1, :half, :half], dwbd[1, half:, half:]])
    do, dz, gr["onorm_g"] = _apost_bwd(sv["o"], proj, wt["onorm_g"], dmixed, name="apost_bwd")
    dvn, dstates = _deltanet_bscan(sv["w"], sv["qg"], sv["kg"], sv["attn"], sv["g"], do, name="deltanet_bscan")
    dqkv_h, dg, dbeta = _deltanet_post(sv["qkv"], sv["g"], sv["beta"], sv["tmats"], sv["states"], dstates, do, dvn, sv["vn"],
                                       name="deltanet_post")
    dab, dalog, ddtb = _gates_bwd(proj, wt["a_log"], wt["dt_bias"], dg, dbeta, name="gates_bwd")
    gr["a_log"], gr["dt_bias"] = dalog[0, :HEADS], ddtb[0, :HEADS]
    dqkv, gr["conv_qkv"] = _qkv_prep_bwd(proj, wt["conv_qkv"], dqkv_h, name="qkv_prep_bwd")
    dproj = jnp.concatenate([dqkv, dz, dab, dhp, dcb, dcc, dch], axis=1)
    dwin = _matmul(sv["h1"], dproj, "tn", name="mm_din")
    gr["w_in"] = jnp.concatenate([dwin[:, :AB_COL + 2 * HEADS], dwin[:, AB_COL + LANE:]], axis=1)
    dh1 = _matmul(dproj, wt["w_in"], "nt", name="mm_dh1")
    dx, gr["norm1_g"] = _rmsnorm_bwd(sv["x"], wt["norm1_g"], dh1, dx1, name="rmsnorm_bwd")
    return dx, gr


def _local_step(x, p, target, full):
    win = _pool_windows()
    layers = []
    for i in range(DEPTH):
        wt = dict(
            norm1_g=full["norm1_g"][i][None], norm2_g=full["norm2_g"][i][None], onorm_g=full["onorm_g"][i][None],
            a_log=_pad_lanes(full["a_log"][i]), dt_bias=_pad_lanes(full["dt_bias"][i]),
            pool_scale=full["pool_scale"][i][None], pool_win=win, pool_wbd=_block_diag_pairs(full["pool_w"][i]),
            conv_qkv=full["conv_qkv"][i], sconv_w=full["sconv_w"][i],
            w_in=jnp.concatenate([full["w_in"][i][:, :AB_COL + 2 * HEADS], jnp.zeros((D_MODEL, LANE - 2 * HEADS), BF16),
                                  full["w_in"][i][:, AB_COL + 2 * HEADS:]], axis=1),
            w_out=full["w_out"][i], w_gu=jnp.concatenate([full["w_gate"][i], full["w_up"][i]], axis=1),
            w_down=full["w_down"][i], ple_proj=full["ple_proj"][i], ple_gate=full["ple_gate"][i])
        layers.append(wt)
    saved = []
    h = x
    for i in range(DEPTH):
        h, sv = _layer_fwd(h, p[i], layers[i], i)
        saved.append(sv)
    dx, dgf, loss = _loss_head(h, full["final_g"][None], target, name="loss_head")
    grads = []
    for i in reversed(range(DEPTH)):
        dx, gr = _layer_bwd(dx, saved[i], layers[i])
        grads.append(gr)
    grads = grads[::-1]
    stacked = {k: jnp.stack([grads[i][k] for i in range(DEPTH)]) for k in grads[0]}
    stacked["norm1_g"] = stacked["norm1_g"][:, 0]
    stacked["norm2_g"] = stacked["norm2_g"][:, 0]
    stacked["onorm_g"] = stacked["onorm_g"][:, 0]
    stacked["pool_scale"] = stacked["pool_scale"][:, 0]
    return loss, dx, stacked, dgf


COL_SHARDED = ("w_in", "w_gate", "w_up", "ple_proj", "conv_qkv", "sconv_w")
ROW_SHARDED = ("w_out", "w_down", "ple_gate")
F32_PAYLOAD = ("conv_qkv", "sconv_w")
SMALL = ("norm1_g", "a_log", "dt_bias", "onorm_g", "pool_w", "pool_scale", "norm2_g", "final_g")
SLAB_COLS = 1024


def _unshard(name, gathered):
    _, depth, a, b = gathered.shape
    if name in COL_SHARDED:
        return jnp.transpose(gathered, (1, 2, 0, 3)).reshape(depth, a, N_DEV * b)
    return jnp.transpose(gathered, (1, 0, 2, 3)).reshape(depth, N_DEV * a, b)


def _to_blocks(name, grad):
    depth, a, b = grad.shape
    if name in COL_SHARDED:
        return jnp.transpose(grad.reshape(depth, a, N_DEV, b // N_DEV), (2, 0, 1, 3))
    return jnp.transpose(grad.reshape(depth, N_DEV, a // N_DEV, b), (1, 0, 2, 3))


def _slab_rows(shape):
    size = 1
    for s in shape:
        size *= s
    return SUBLANE * -(-size // (SUBLANE * SLAB_COLS))


def _pack_slab(parts, extra_row):
    rows = []
    for name in SMALL:
        flat = parts[name].reshape(-1)
        nrow = _slab_rows(parts[name].shape)
        rows.append(jnp.pad(flat, (0, nrow * SLAB_COLS - flat.shape[0])).reshape(nrow, SLAB_COLS))
    rows.append(jnp.pad(extra_row, ((0, SUBLANE - 1), (0, 0))))
    return jnp.concatenate(rows, axis=0)


def _unpack_slab(slab, shapes):
    out, row = {}, 0
    for name in SMALL:
        size = 1
        for s in shapes[name]:
            size *= s
        out[name] = slab[row:row + _slab_rows(shapes[name])].reshape(-1)[:size].reshape(shapes[name])
        row += _slab_rows(shapes[name])
    return out, row


def kernel(x, p, norm1_g, w_in, conv_qkv, a_log, dt_bias, onorm_g, pool_w, pool_scale, sconv_w, w_out, norm2_g, w_gate, w_up, w_down, ple_proj, ple_gate, final_g, loss_target, m_norm1_g, m_w_in, m_conv_qkv, m_a_log, m_dt_bias, m_onorm_g, m_pool_w, m_pool_scale, m_sconv_w, m_w_out, m_norm2_g, m_w_gate, m_w_up, m_w_down, m_ple_proj, m_ple_gate, m_final_g, v_norm1_g, v_w_in, v_conv_qkv, v_a_log, v_dt_bias, v_onorm_g, v_pool_w, v_pool_scale, v_sconv_w, v_w_out, v_norm2_g, v_w_gate, v_w_up, v_w_down, v_ple_proj, v_ple_gate, v_final_g):
    names = ["norm1_g", "w_in", "conv_qkv", "a_log", "dt_bias", "onorm_g", "pool_w", "pool_scale", "sconv_w", "w_out", "norm2_g",
             "w_gate", "w_up", "w_down", "ple_proj", "ple_gate", "final_g"]
    w = dict(zip(names, [norm1_g, w_in, conv_qkv, a_log, dt_bias, onorm_g, pool_w, pool_scale, sconv_w, w_out, norm2_g, w_gate, w_up,
                         w_down, ple_proj, ple_gate, final_g]))
    m = dict(zip(names, [m_norm1_g, m_w_in, m_conv_qkv, m_a_log, m_dt_bias, m_onorm_g, m_pool_w, m_pool_scale, m_sconv_w, m_w_out,
                         m_norm2_g, m_w_gate, m_w_up, m_w_down, m_ple_proj, m_ple_gate, m_final_g]))
    v = dict(zip(names, [v_norm1_g, v_w_in, v_conv_qkv, v_a_log, v_dt_bias, v_onorm_g, v_pool_w, v_pool_scale, v_sconv_w, v_w_out,
                         v_norm2_g, v_w_gate, v_w_up, v_w_down, v_ple_proj, v_ple_gate, v_final_g]))
    sharded = COL_SHARDED + ROW_SHARDED

    payload = [w[k] if k in F32_PAYLOAD else w[k].astype(BF16) for k in sharded]
    gathered = _all_gather(payload, name="all_gather_weights")
    full = {k: _unshard(k, g) for k, g in zip(sharded, gathered)}
    full.update({k: w[k] for k in SMALL})

    loss_part, dx, grads, dgf = _local_step(x[0], p[:, 0], loss_target[0], full)
    grads["final_g"] = dgf[0]

    blocks = [_to_blocks(k, grads[k]) for k in sharded]
    blocks = [b if k in F32_PAYLOAD else b.astype(BF16) for k, b in zip(sharded, blocks)]
    received = _exchange_blocks(blocks, name="exchange_grad_blocks")
    loss_row = jnp.pad(loss_part, ((0, 0), (0, SLAB_COLS - LANE)))
    (small_parts,) = _all_gather([_pack_slab(grads, loss_row)], name="all_gather_small_grads")

    out_g, out_d, out_m, out_v = {}, {}, {}, {}
    for k, rec in zip(sharded, received):
        shp = w[k].shape
        r2 = (shp[0] * shp[1], shp[2])
        res = _adamw_reduce(w[k].reshape(r2), rec.reshape((N_DEV,) + r2), m[k].reshape(r2), v[k].reshape(r2), name="adamw_" + k)
        out_g[k], out_d[k], out_m[k], out_v[k] = [t.reshape(shp) for t in res]
    zero_row = jnp.zeros((1, SLAB_COLS), F32)
    slabs = _adamw_reduce(_pack_slab(w, zero_row), small_parts, _pack_slab(m, zero_row), _pack_slab(v, zero_row), name="adamw_small")
    shapes = {k: w[k].shape for k in SMALL}
    for dst, slab in zip((out_g, out_d, out_m, out_v), slabs):
        vals, _ = _unpack_slab(slab, shapes)
        dst.update(vals)
    _, loss_at = _unpack_slab(slabs[0], shapes)
    loss = slabs[0][loss_at, 0]

    return (loss, dx[None], *[out_g[k] for k in names], *[out_d[k] for k in names], *[out_m[k] for k in names],
            *[out_v[k] for k in names])
```

```python
import functools

import jax
import jax.numpy as jnp
from jax import lax
from jax.experimental import pallas as pl
from jax.experimental.pallas import tpu as pltpu

F32 = jnp.float32
BF16 = jnp.bfloat16

D_MODEL = 1024
DEPTH = 2
PLE_DIM = 256
EPS = 1e-6
HEAD_DIM = 128
HEADS = 4
A_DIM = HEADS * HEAD_DIM
QKV_TAPS = 4
CHUNK = 64
POOL_WINDOWS = (2, 4, 8, 16)
POOL_DIM = 256
CONV_DIM = 256
CONV_TAPS = 3
D_FF = 2816
D_IN = 3080
D_IN_PAD = 3200
AB_COL = 2048
N_DEV = 8

ADAM_LR = 0.001
ADAM_B1 = 0.9
ADAM_B2 = 0.999
ADAM_EPS = 1e-08
ADAM_WD = 0.01
ADAM_STEP = 10

LANE = 128
SUBLANE = 8
VMEM_BYTES_V7X = 64 * 1024 * 1024
VMEM_LIMIT = 48 * 1024 * 1024

_HI = lax.Precision.HIGHEST
NN = ((1,), (0,))
NT = ((1,), (1,))
TN = ((0,), (0,))
MESH = pl.DeviceIdType.MESH


def _dot(a, b, dims, hi=False):
    if hi:
        return lax.dot_general(a, b, (dims, ((), ())), precision=_HI, preferred_element_type=F32)
    return lax.dot_general(a.astype(BF16), b.astype(BF16), (dims, ((), ())), preferred_element_type=F32)


def _pcall(body, *, name, out_shape, grid=(), in_specs=None, out_specs=None, scratch_shapes=(), semantics=None,
           vmem_limit=None, **kw):
    params = {}
    if semantics is not None:
        params["dimension_semantics"] = semantics
    if vmem_limit is not None:
        params["vmem_limit_bytes"] = vmem_limit
    return pl.pallas_call(
        body, name=name, out_shape=out_shape, grid=grid, in_specs=in_specs, out_specs=out_specs,
        scratch_shapes=list(scratch_shapes), compiler_params=pltpu.CompilerParams(**params), **kw)


def _sigmoid(x):
    return 1.0 / (1.0 + jnp.exp(-x))


def _softplus(x):
    return jnp.maximum(x, 0.0) + jnp.log(1.0 + jnp.exp(-jnp.abs(x)))


def _tile(n, cap, mult):
    if n <= cap:
        return n
    best = None
    for t in range(mult, cap + 1, mult):
        if n % t == 0:
            best = t
    assert best is not None, (n, cap, mult)
    return best


def _matmul(a, b, mode, *, name, res=None, out_dtype=F32, b_blocked=False, out_blocked=None):
    if mode == "nn":
        (m, k) = a.shape
        (k2, n) = (b.shape[1], b.shape[0] * b.shape[2]) if b_blocked else b.shape
    elif mode == "nt":
        (m, k) = a.shape
        (n, k2) = (b.shape[1], b.shape[0] * b.shape[2]) if b_blocked else b.shape
    else:
        (k, m), (k2, n) = a.shape, b.shape
    assert k == k2, (a.shape, b.shape, mode)
    tm = _tile(m, 512, LANE if mode == "tn" else 16)
    tn = _tile(n, 640, LANE)
    tk = _tile(k, 3200, LANE)
    if b_blocked and mode == "nn":
        tn = b.shape[2]
    if b_blocked and mode == "nt":
        tk = b.shape[2]
    if out_blocked is not None:
        assert mode == "tn" and out_blocked[0] * out_blocked[1] == n
        tn = out_blocked[1]
    nk = k // tk
    dims = {"nn": NN, "nt": NT, "tn": TN}[mode]
    a_spec = pl.BlockSpec((tk, tm), lambda i, j, kk: (kk, i)) if mode == "tn" else pl.BlockSpec((tm, tk), lambda i, j, kk: (i, kk))
    if b_blocked:
        b_spec = (pl.BlockSpec((1, tk, tn), lambda i, j, kk: (j, kk, 0)) if mode == "nn"
                  else pl.BlockSpec((1, tn, tk), lambda i, j, kk: (kk, j, 0)))
    else:
        b_spec = pl.BlockSpec((tn, tk), lambda i, j, kk: (j, kk)) if mode == "nt" else pl.BlockSpec((tk, tn), lambda i, j, kk: (kk, j))
    r_spec = pl.BlockSpec((tm, tn), lambda i, j, kk: (i, j))
    o_spec = r_spec if out_blocked is None else pl.BlockSpec((1, tm, tn), lambda i, j, kk: (j, i, 0))
    o_shape = (m, n) if out_blocked is None else (out_blocked[0], m, out_blocked[1])
    has_res = res is not None

    def body(*refs):
        a_ref, b_ref = refs[0], refs[1]
        res_ref = refs[2] if has_res else None
        o_ref = refs[2 + has_res]
        if out_blocked is not None:
            o_ref = o_ref.at[0]
        part = _dot(a_ref[...], b_ref[0] if b_blocked else b_ref[...], dims)
        if nk == 1:
            if has_res:
                part = part + res_ref[...]
            o_ref[...] = part.astype(o_ref.dtype)
            return
        acc_ref = refs[3 + has_res]
        kk = pl.program_id(2)

        @pl.when(kk == 0)
        def _():
            acc_ref[...] = part

        @pl.when(kk > 0)
        def _():
            acc_ref[...] += part

        @pl.when(kk == nk - 1)
        def _():
            total = acc_ref[...]
            if has_res:
                total = total + res_ref[...]
            o_ref[...] = total.astype(o_ref.dtype)

    ins = [a, b] + ([res] if has_res else [])
    specs = [a_spec, b_spec] + ([r_spec] if has_res else [])
    return _pcall(body, name=name, out_shape=jax.ShapeDtypeStruct(o_shape, out_dtype), grid=(m // tm, n // tn, nk),
                  in_specs=specs, out_specs=o_spec, scratch_shapes=[pltpu.VMEM((tm, tn), F32)] if nk > 1 else [],
                  semantics=("parallel", "parallel", "arbitrary"), vmem_limit=VMEM_LIMIT)(*ins)


ROW_TILE = 256


def _rows(t, width, idx=0):
    return pl.BlockSpec((ROW_TILE, width), lambda i: (i, idx))


def _vec(width):
    return pl.BlockSpec((1, width), lambda i: (0, 0))


def _rmsnorm_fwd(x, g, *, name):
    t, d = x.shape

    def body(x_ref, g_ref, h_ref):
        xv = x_ref[...]
        r = lax.rsqrt(jnp.mean(xv * xv, axis=-1, keepdims=True) + EPS)
        h_ref[...] = (xv * r * g_ref[...]).astype(BF16)

    return _pcall(body, name=name, out_shape=jax.ShapeDtypeStruct((t, d), BF16), grid=(t // ROW_TILE,),
                  in_specs=[_rows(t, d), _vec(d)], out_specs=_rows(t, d), semantics=("parallel",))(x, g)


def _rmsnorm_bwd(x, g, dh, dres, *, name):
    t, d = x.shape

    def body(x_ref, g_ref, dh_ref, dres_ref, dx_ref, dg_ref):
        xv = x_ref[...]
        r = lax.rsqrt(jnp.mean(xv * xv, axis=-1, keepdims=True) + EPS)
        xhat = xv * r
        dhv = dh_ref[...].astype(F32)
        dhg = dhv * g_ref[...]
        dx_ref[...] = dres_ref[...] + r * (dhg - xhat * jnp.mean(dhg * xhat, axis=-1, keepdims=True))
        part = jnp.sum(dhv * xhat, axis=0, keepdims=True)

        @pl.when(pl.program_id(0) == 0)
        def _():
            dg_ref[...] = part

        @pl.when(pl.program_id(0) > 0)
        def _():
            dg_ref[...] += part

    return _pcall(body, name=name, out_shape=(jax.ShapeDtypeStruct((t, d), F32), jax.ShapeDtypeStruct((1, d), F32)),
                  grid=(t // ROW_TILE,), in_specs=[_rows(t, d), _vec(d), _rows(t, d), _rows(t, d)],
                  out_specs=(_rows(t, d), _vec(d)), semantics=("arbitrary",))(x, g, dh, dres)


def _swiglu_fwd(gate, up, *, name):
    t, f = gate.shape

    def body(gate_ref, up_ref, ff_ref):
        gv = gate_ref[...]
        ff_ref[...] = (gv * _sigmoid(gv) * up_ref[...]).astype(BF16)

    return _pcall(body, name=name, out_shape=jax.ShapeDtypeStruct((t, f), BF16), grid=(t // ROW_TILE,),
                  in_specs=[_rows(t, f), _rows(t, f)], out_specs=_rows(t, f), semantics=("parallel",))(gate, up)


def _swiglu_bwd(gate, up, dff, *, name):
    t, f = gate.shape

    def body(gate_ref, up_ref, dff_ref, dgate_ref, dup_ref):
        gv = gate_ref[...]
        sig = _sigmoid(gv)
        dffv = dff_ref[...]
        dgate_ref[...] = (dffv * up_ref[...] * sig * (1.0 + gv * (1.0 - sig))).astype(BF16)
        dup_ref[...] = (dffv * gv * sig).astype(BF16)

    out = jax.ShapeDtypeStruct((t, f), BF16)
    return _pcall(body, name=name, out_shape=(out, out), grid=(t // ROW_TILE,), in_specs=[_rows(t, f)] * 3,
                  out_specs=(_rows(t, f),) * 2, semantics=("parallel",))(gate, up, dff)


def _ple_fwd(x2, pgl, pp, *, name):
    t, d = x2.shape

    def body(x_ref, pgl_ref, pp_ref, o_ref):
        o_ref[...] = x_ref[...] + _sigmoid(pgl_ref[...]) * pp_ref[...]

    return _pcall(body, name=name, out_shape=jax.ShapeDtypeStruct((t, d), F32), grid=(t // ROW_TILE,),
                  in_specs=[_rows(t, d)] * 3, out_specs=_rows(t, d), semantics=("parallel",))(x2, pgl, pp)


def _ple_bwd(dx3, pgl, pp, *, name):
    t, d = dx3.shape

    def body(dx_ref, pgl_ref, pp_ref, dpgl_ref, dpp_ref):
        dxv = dx_ref[...]
        sig = _sigmoid(pgl_ref[...])
        dpp_ref[...] = (dxv * sig).astype(BF16)
        dpgl_ref[...] = (dxv * pp_ref[...] * sig * (1.0 - sig)).astype(BF16)

    return _pcall(body, name=name, out_shape=(jax.ShapeDtypeStruct((t, d), BF16),) * 2, grid=(t // ROW_TILE,),
                  in_specs=[_rows(t, d)] * 3, out_specs=(_rows(t, d),) * 2, semantics=("parallel",))(dx3, pgl, pp)


def _loss_head(x3, g, target, *, name):
    t, d = x3.shape

    def body(x_ref, g_ref, t_ref, dx_ref, dg_ref, loss_ref):
        xv = x_ref[...]
        r = lax.rsqrt(jnp.mean(xv * xv, axis=-1, keepdims=True) + EPS)
        xhat = xv * r
        gv = g_ref[...]
        err = xhat * gv - t_ref[...]
        row_loss = jnp.sum(err * err, axis=-1, keepdims=True) * (0.5 / d)
        lpart = jnp.broadcast_to(jnp.sum(row_loss, axis=0, keepdims=True), (1, LANE))
        dy = err * (1.0 / d)
        dyg = dy * gv
        dx_ref[...] = r * (dyg - xhat * jnp.mean(dyg * xhat, axis=-1, keepdims=True))
        gpart = jnp.sum(dy * xhat, axis=0, keepdims=True)

        @pl.when(pl.program_id(0) == 0)
        def _():
            dg_ref[...] = gpart
            loss_ref[...] = lpart

        @pl.when(pl.program_id(0) > 0)
        def _():
            dg_ref[...] += gpart
            loss_ref[...] += lpart

    return _pcall(body, name=name,
                  out_shape=(jax.ShapeDtypeStruct((t, d), F32), jax.ShapeDtypeStruct((1, d), F32), jax.ShapeDtypeStruct((1, LANE), F32)),
                  grid=(t // ROW_TILE,), in_specs=[_rows(t, d), _vec(d), _rows(t, d)],
                  out_specs=(_rows(t, d), _vec(d), _vec(LANE)), semantics=("arbitrary",))(x3, g, target)


def _shift_down(x, d):
    if d == 0:
        return x
    row = lax.broadcasted_iota(jnp.int32, x.shape, 0)
    return jnp.where(row >= d, pltpu.roll(x, d, 0), 0.0)


def _shift_up(x, d):
    if d == 0:
        return x
    t = x.shape[0]
    row = lax.broadcasted_iota(jnp.int32, x.shape, 0)
    return jnp.where(row < t - d, pltpu.roll(x, t - d, 0), 0.0)


def _colsum(x):
    return jnp.sum(x, axis=0, keepdims=True)


def _col(t, idx_fn):
    return pl.BlockSpec((t, LANE), idx_fn)


def _conv_fwd(x, w_ref, taps):
    acc = None
    for j in range(taps):
        term = w_ref[pl.ds(j, 1), :] * _shift_down(x, taps - 1 - j)
        acc = term if acc is None else acc + term
    return acc


def _conv_bwd(x, dy, w_ref, dw_ref, taps):
    dx = None
    for j in range(taps):
        term = w_ref[pl.ds(j, 1), :] * _shift_up(dy, taps - 1 - j)
        dx = term if dx is None else dx + term
        dw_ref[pl.ds(j, 1), :] = _colsum(dy * _shift_down(x, taps - 1 - j))
    return dx


def _qkv_prep_fwd(proj, conv_w, *, name):
    t = proj.shape[0]
    scale = HEAD_DIM ** -0.5

    def body(x_ref, w_ref, o_ref):
        j = pl.program_id(0)
        c = _conv_fwd(x_ref[...], w_ref, QKV_TAPS)
        s = c * _sigmoid(c)
        r = lax.rsqrt(jnp.sum(s * s, axis=-1, keepdims=True) + EPS)
        f = jnp.where(j < 2 * HEADS, r, 1.0) * jnp.where(j < HEADS, scale, 1.0)
        o_ref[0] = s * f

    return _pcall(body, name=name, out_shape=jax.ShapeDtypeStruct((3 * HEADS, t, LANE), F32), grid=(3 * HEADS,),
                  in_specs=[_col(t, lambda j: (0, j)), pl.BlockSpec((QKV_TAPS, LANE), lambda j: (0, j))],
                  out_specs=pl.BlockSpec((1, t, LANE), lambda j: (j, 0, 0)), semantics=("parallel",),
                  vmem_limit=VMEM_LIMIT)(proj, conv_w)


def _qkv_prep_bwd(proj, conv_w, dqkv, *, name):
    t = proj.shape[0]
    scale = HEAD_DIM ** -0.5

    def body(x_ref, w_ref, d_ref, dx_ref, dw_ref):
        j = pl.program_id(0)
        xv = x_ref[...]
        c = _conv_fwd(xv, w_ref, QKV_TAPS)
        sig = _sigmoid(c)
        s = c * sig
        r = lax.rsqrt(jnp.sum(s * s, axis=-1, keepdims=True) + EPS)
        n0 = s * r
        dv = d_ref[0]
        dn0 = dv * jnp.where(j < HEADS, scale, 1.0)
        ds_norm = r * (dn0 - n0 * jnp.sum(dn0 * n0, axis=-1, keepdims=True))
        ds = jnp.where(j < 2 * HEADS, ds_norm, dv)
        dc = ds * sig * (1.0 + c * (1.0 - sig))
        dx_ref[...] = _conv_bwd(xv, dc, w_ref, dw_ref, QKV_TAPS).astype(BF16)

    return _pcall(body, name=name,
                  out_shape=(jax.ShapeDtypeStruct((t, 3 * A_DIM), BF16), jax.ShapeDtypeStruct((QKV_TAPS, 3 * A_DIM), F32)),
                  grid=(3 * HEADS,),
                  in_specs=[_col(t, lambda j: (0, j)), pl.BlockSpec((QKV_TAPS, LANE), lambda j: (0, j)),
                            pl.BlockSpec((1, t, LANE), lambda j: (j, 0, 0))],
                  out_specs=(_col(t, lambda j: (0, j)), pl.BlockSpec((QKV_TAPS, LANE), lambda j: (0, j))),
                  semantics=("parallel",), vmem_limit=VMEM_LIMIT)(proj, conv_w, dqkv)


def _lane_pick(x, lane_idx, lane):
    return jnp.broadcast_to(jnp.sum(jnp.where(lane == lane_idx, x, 0.0), axis=-1, keepdims=True), x.shape)


def _gates_fwd(proj, alog, dtb, *, name):
    t = proj.shape[0]

    def body(x_ref, alog_ref, dtb_ref, g_ref, b_ref):
        xv = x_ref[...]
        lane = lax.broadcasted_iota(jnp.int32, xv.shape, 1)
        gall = -jnp.exp(alog_ref[...]) * _softplus(xv + dtb_ref[...])
        ball = _sigmoid(xv)
        for h in range(HEADS):
            g_ref[h] = _lane_pick(gall, h, lane)
            b_ref[h] = _lane_pick(ball, HEADS + h, lane)

    out = jax.ShapeDtypeStruct((HEADS, t, LANE), F32)
    whole = pl.BlockSpec((HEADS, t, LANE), lambda i: (0, 0, 0))
    return _pcall(body, name=name, out_shape=(out, out), grid=(1,),
                  in_specs=[_col(t, lambda i: (0, AB_COL // LANE)), _vec(LANE), _vec(LANE)], out_specs=(whole, whole),
                  semantics=("arbitrary",), vmem_limit=VMEM_LIMIT)(proj, alog, dtb)


def _gates_bwd(proj, alog, dtb, dg, dbeta, *, name):
    t = proj.shape[0]

    def body(x_ref, alog_ref, dtb_ref, dg_ref, db_ref, dab_ref, dalog_ref, ddtb_ref):
        xv = x_ref[...]
        lane = lax.broadcasted_iota(jnp.int32, xv.shape, 1)
        lane1 = lax.broadcasted_iota(jnp.int32, (1, LANE), 1)
        z = xv + dtb_ref[...]
        nea = -jnp.exp(alog_ref[...])
        da_f = nea * _sigmoid(z)
        g_f = nea * _softplus(z)
        ball = _sigmoid(xv)
        db_f = ball * (1.0 - ball)
        dab = jnp.zeros_like(xv)
        dalog = jnp.zeros((1, LANE), F32)
        for h in range(HEADS):
            dgh = dg_ref[h]
            dab = dab + jnp.where(lane == h, dgh * da_f, 0.0) + jnp.where(lane == HEADS + h, db_ref[h] * db_f, 0.0)
            dalog = dalog + jnp.where(lane1 == h, _colsum(dgh * g_f), 0.0)
        dab_ref[...] = dab.astype(BF16)
        dalog_ref[...] = dalog
        ddtb_ref[...] = jnp.where(lane1 < HEADS, _colsum(dab), 0.0)

    whole = pl.BlockSpec((HEADS, t, LANE), lambda i: (0, 0, 0))
    vec = jax.ShapeDtypeStruct((1, LANE), F32)
    return _pcall(body, name=name, out_shape=(jax.ShapeDtypeStruct((t, LANE), BF16), vec, vec), grid=(1,),
                  in_specs=[_col(t, lambda i: (0, AB_COL // LANE)), _vec(LANE), _vec(LANE), whole, whole],
                  out_specs=(_col(t, lambda i: (0, 0)), _vec(LANE), _vec(LANE)), semantics=("arbitrary",),
                  vmem_limit=VMEM_LIMIT)(proj, alog, dtb, dg, dbeta)


Z_COL = 3 * A_DIM // LANE


def _apost_fwd(o, proj, gn, *, name):
    t = proj.shape[0]

    def body(o_ref, z_ref, gn_ref, y_ref):
        ov = o_ref[0]
        z = z_ref[...]
        r = lax.rsqrt(jnp.mean(ov * ov, axis=-1, keepdims=True) + EPS)
        y_ref[...] = (ov * r * gn_ref[...] * (z * _sigmoid(z))).astype(BF16)

    return _pcall(body, name=name, out_shape=jax.ShapeDtypeStruct((t, A_DIM), BF16), grid=(HEADS,),
                  in_specs=[pl.BlockSpec((1, t, LANE), lambda h: (h, 0, 0)), _col(t, lambda h: (0, Z_COL + h)),
                            pl.BlockSpec((1, LANE), lambda h: (0, 0))],
                  out_specs=_col(t, lambda h: (0, h)), semantics=("parallel",), vmem_limit=VMEM_LIMIT)(o, proj, gn)


def _apost_bwd(o, proj, gn, dmixed, *, name):
    t = proj.shape[0]

    def body(o_ref, z_ref, gn_ref, d_ref, do_ref, dz_ref, dgn_ref):
        ov = o_ref[0]
        z = z_ref[...]
        gnv = gn_ref[...]
        dv = d_ref[...]
        r = lax.rsqrt(jnp.mean(ov * ov, axis=-1, keepdims=True) + EPS)
        ohat = ov * r
        sig = _sigmoid(z)
        dy = dv * (z * sig)
        dz_ref[...] = (dv * ohat * gnv * sig * (1.0 + z * (1.0 - sig))).astype(BF16)
        dyo = dy * gnv
        do_ref[0] = r * (dyo - ohat * jnp.mean(dyo * ohat, axis=-1, keepdims=True))
        part = _colsum(dy * ohat)

        @pl.when(pl.program_id(0) == 0)
        def _():
            dgn_ref[...] = part

        @pl.when(pl.program_id(0) > 0)
        def _():
            dgn_ref[...] += part

    return _pcall(body, name=name,
                  out_shape=(jax.ShapeDtypeStruct((HEADS, t, LANE), F32), jax.ShapeDtypeStruct((t, A_DIM), BF16),
                             jax.ShapeDtypeStruct((1, LANE), F32)),
                  grid=(HEADS,),
                  in_specs=[pl.BlockSpec((1, t, LANE), lambda h: (h, 0, 0)), _col(t, lambda h: (0, Z_COL + h)),
                            pl.BlockSpec((1, LANE), lambda h: (0, 0)), _col(t, lambda h: (0, h))],
                  out_specs=(pl.BlockSpec((1, t, LANE), lambda h: (h, 0, 0)), _col(t, lambda h: (0, h)),
                             pl.BlockSpec((1, LANE), lambda h: (0, 0))),
                  semantics=("arbitrary",), vmem_limit=VMEM_LIMIT)(o, proj, gn, dmixed)


POOL_COL = (AB_COL + LANE) // LANE
CB_COL = POOL_COL + POOL_DIM // LANE
CC_COL = CB_COL + CONV_DIM // LANE
CH_COL = CC_COL + CONV_DIM // LANE
MAX_WIN_LOG2 = 4


def _window_sums(x, shift):
    sums = []
    cur = x
    for k in range(MAX_WIN_LOG2):
        cur = cur + shift(cur, 1 << k)
        sums.append(cur)
    return sums


def _pick_window(sums, win):
    out = sums[-1]
    for k in range(MAX_WIN_LOG2 - 2, -1, -1):
        out = jnp.where(win == float(2 << k), sums[k], out)
    return out


def _pool_counts(shape, win):
    row = lax.broadcasted_iota(jnp.int32, shape, 0).astype(F32)
    return jnp.minimum(row + 1.0, win)


def _pool_fwd(proj, win, wbd, scale, *, name):
    t = proj.shape[0]

    def body(x_ref, win_ref, w_ref, s_ref, y_ref):
        xv = x_ref[...]
        winv = win_ref[...]
        pooled = _pick_window(_window_sums(xv, _shift_down), winv) / _pool_counts(xv.shape, winv) - xv
        y_ref[...] = (_dot(pooled, w_ref[0], NN) * s_ref[...]).astype(BF16)

    nb = POOL_DIM // LANE
    vec = pl.BlockSpec((1, LANE), lambda b: (0, b))
    return _pcall(body, name=name, out_shape=jax.ShapeDtypeStruct((t, POOL_DIM), BF16), grid=(nb,),
                  in_specs=[_col(t, lambda b: (0, POOL_COL + b)), vec, pl.BlockSpec((1, LANE, LANE), lambda b: (b, 0, 0)), vec],
                  out_specs=_col(t, lambda b: (0, b)), semantics=("parallel",), vmem_limit=VMEM_LIMIT)(proj, win, wbd, scale)


def _pool_bwd(proj, win, wbd, scale, dmixed, *, name):
    t = proj.shape[0]

    def body(x_ref, win_ref, w_ref, s_ref, d_ref, dx_ref, dw_ref, ds_ref):
        xv = x_ref[...]
        winv = win_ref[...]
        cnt = _pool_counts(xv.shape, winv)
        pooled = _pick_window(_window_sums(xv, _shift_down), winv) / cnt - xv
        dv = d_ref[...]
        ds_ref[...] = _colsum(dv * _dot(pooled, w_ref[0], NN))
        dy0 = dv * s_ref[...]
        dw_ref[0] = _dot(pooled, dy0, TN)
        dpooled = _dot(dy0, w_ref[0], NT)
        dmean = dpooled / cnt
        dx_ref[...] = (_pick_window(_window_sums(dmean, _shift_up), winv) - dpooled).astype(BF16)

    nb = POOL_DIM // LANE
    vec = pl.BlockSpec((1, LANE), lambda b: (0, b))
    mat = pl.BlockSpec((1, LANE, LANE), lambda b: (b, 0, 0))
    first = A_DIM // LANE
    return _pcall(body, name=name,
                  out_shape=(jax.ShapeDtypeStruct((t, POOL_DIM), BF16), jax.ShapeDtypeStruct((nb, LANE, LANE), F32),
                             jax.ShapeDtypeStruct((1, POOL_DIM), F32)),
                  grid=(nb,),
                  in_specs=[_col(t, lambda b: (0, POOL_COL + b)), vec, mat, vec, _col(t, lambda b: (0, first + b))],
                  out_specs=(_col(t, lambda b: (0, b)), mat, vec), semantics=("parallel",),
                  vmem_limit=VMEM_LIMIT)(proj, win, wbd, scale, dmixed)


def _sconv_fwd(proj, w, *, name):
    t = proj.shape[0]

    def body(cb_ref, cc_ref, ch_ref, w_ref, y_ref):
        y_ref[...] = (cb_ref[...] * _conv_fwd(cc_ref[...] * ch_ref[...], w_ref, CONV_TAPS)).astype(BF16)

    nb = CONV_DIM // LANE
    return _pcall(body, name=name, out_shape=jax.ShapeDtypeStruct((t, CONV_DIM), BF16), grid=(nb,),
                  in_specs=[_col(t, lambda b: (0, CB_COL + b)), _col(t, lambda b: (0, CC_COL + b)),
                            _col(t, lambda b: (0, CH_COL + b)), pl.BlockSpec((CONV_TAPS, LANE), lambda b: (0, b))],
                  out_specs=_col(t, lambda b: (0, b)), semantics=("parallel",), vmem_limit=VMEM_LIMIT)(proj, proj, proj, w)


def _sconv_bwd(proj, w, dmixed, *, name):
    t = proj.shape[0]

    def body(cb_ref, cc_ref, ch_ref, w_ref, d_ref, dcb_ref, dcc_ref, dch_ref, dw_ref):
        cc = cc_ref[...]
        ch = ch_ref[...]
        u = cc * ch
        dv = d_ref[...]
        dcb_ref[...] = (dv * _conv_fwd(u, w_ref, CONV_TAPS)).astype(BF16)
        du = _conv_bwd(u, dv * cb_ref[...], w_ref, dw_ref, CONV_TAPS)
        dcc_ref[...] = (du * ch).astype(BF16)
        dch_ref[...] = (du * cc).astype(BF16)

    nb = CONV_DIM // LANE
    first = (A_DIM + POOL_DIM) // LANE
    act = jax.ShapeDtypeStruct((t, CONV_DIM), BF16)
    wspec = pl.BlockSpec((CONV_TAPS, LANE), lambda b: (0, b))
    ospec = _col(t, lambda b: (0, b))
    return _pcall(body, name=name, out_shape=(act, act, act, jax.ShapeDtypeStruct((CONV_TAPS, CONV_DIM), F32)), grid=(nb,),
                  in_specs=[_col(t, lambda b: (0, CB_COL + b)), _col(t, lambda b: (0, CC_COL + b)),
                            _col(t, lambda b: (0, CH_COL + b)), wspec, _col(t, lambda b: (0, first + b))],
                  out_specs=(ospec, ospec, ospec, wspec), semantics=("parallel",),
                  vmem_limit=VMEM_LIMIT)(proj, proj, proj, w, dmixed)


def _chunk_masks():
    r = lax.broadcasted_iota(jnp.int32, (CHUNK, CHUNK), 0)
    c = lax.broadcasted_iota(jnp.int32, (CHUNK, CHUNK), 1)
    return r >= c, r > c, jnp.where(r == c, 1.0, 0.0).astype(F32)


def _split(a):
    hi = a.astype(BF16)
    return hi, (a - hi.astype(F32)).astype(BF16)


def _dot_split(a, b, dims):
    (ah, al), (bh, bl) = a, b
    return _dot(ah, bh, dims) + _dot(ah, bl, dims) + _dot(al, bh, dims)


def _tri_inv(lows, eye):
    xs = [eye - low for low in lows]
    ps = [_split(low) for low in lows]
    ps = [_split(_dot_split(p, p, NN)) for p in ps]
    for i in range(5):
        xs = [x + _dot_split(_split(x), p, NN) for x, p in zip(xs, ps)]
        if i < 4:
            ps = [_split(_dot_split(p, p, NN)) for p in ps]
    return xs


def _prefix_sum_rows(x):
    for k in range(6):
        x = x + _shift_down(x, 1 << k)
    return x


def _suffix_sum_rows(x):
    for k in range(6):
        x = x + _shift_up(x, 1 << k)
    return x


def _chunk_decay(g, incl):
    gcb = _prefix_sum_rows(g)
    gtot = _colsum(g)
    col = gcb[:, :CHUNK]
    row = gcb.T[:CHUNK, :]
    decay = jnp.exp(jnp.where(incl, col - row, -1e30))
    return gcb, gtot, decay


CHUNKS_PER_STEP = 2


def _heads_of(ref, base, rows):
    return [ref[base + h, rows, :] for h in range(HEADS)]


def _chunk_rows(j):
    return pl.ds(j * CHUNK, CHUNK)


def _deltanet_prep(qkv, g, beta, *, name):
    t = qkv.shape[1]
    n_chunks = t // CHUNK
    per = CHUNKS_PER_STEP
    probs = [(j, h) for j in range(per) for h in range(HEADS)]

    def body(qkv_ref, g_ref, b_ref, u_ref, w_ref, qg_ref, kg_ref, attn_ref, tm_ref):
        incl, strict, eye = _chunk_masks()
        q = [qkv_ref[h, _chunk_rows(j), :] for j, h in probs]
        k = [qkv_ref[HEADS + h, _chunk_rows(j), :] for j, h in probs]
        v = [qkv_ref[2 * HEADS + h, _chunk_rows(j), :] for j, h in probs]
        bv = [b_ref[h, _chunk_rows(j), :] for j, h in probs]
        dec = [_chunk_decay(g_ref[h, _chunk_rows(j), :], incl) for j, h in probs]
        kb = [a * b for a, b in zip(k, bv)]
        low = [jnp.where(strict, _dot(a, b, NT) * d[2], 0.0) for a, b, d in zip(kb, k, dec)]
        tm = _tri_inv(low, eye)
        egc = [jnp.exp(d[0]) for d in dec]
        u = [_dot(m, a * b, NN) for m, a, b in zip(tm, v, bv)]
        w = [_dot(m, a * e, NN) for m, a, e in zip(tm, kb, egc)]
        attn = [_dot(a, b, NT) * d[2] for a, b, d in zip(q, k, dec)]
        for i, (j, h) in enumerate(probs):
            rows = _chunk_rows(j)
            u_ref[h, rows, :] = u[i]
            w_ref[h, rows, :] = w[i].astype(BF16)
            qg_ref[h, rows, :] = (q[i] * egc[i]).astype(BF16)
            kg_ref[h, rows, :] = (k[i] * jnp.exp(dec[i][1] - dec[i][0])).astype(BF16)
            attn_ref[j, h] = attn[i].astype(BF16)
            tm_ref[j, h] = tm[i]

    act = lambda heads: pl.BlockSpec((heads, per * CHUNK, LANE), lambda n: (0, n, 0))
    mat = pl.BlockSpec((per, HEADS, CHUNK, CHUNK), lambda n: (n, 0, 0, 0))
    return _pcall(
        body, name=name,
        out_shape=(jax.ShapeDtypeStruct((HEADS, t, LANE), F32),) + (jax.ShapeDtypeStruct((HEADS, t, LANE), BF16),) * 3
        + (jax.ShapeDtypeStruct((n_chunks, HEADS, CHUNK, CHUNK), BF16), jax.ShapeDtypeStruct((n_chunks, HEADS, CHUNK, CHUNK), F32)),
        grid=(n_chunks // per,), in_specs=[act(3 * HEADS), act(HEADS), act(HEADS)],
        out_specs=(act(HEADS),) * 4 + (mat, mat), semantics=("parallel",), vmem_limit=VMEM_LIMIT)(qkv, g, beta)


SCAN_CHUNKS_PER_STEP = 4


def _deltanet_scan(u, w, qg, kg, attn, g, *, name):
    t = u.shape[1]
    n_chunks = t // CHUNK
    per = SCAN_CHUNKS_PER_STEP

    def body(u_ref, w_ref, qg_ref, kg_ref, attn_ref, g_ref, o_ref, vn_ref, st_ref, s_ref):
        @pl.when(pl.program_id(0) == 0)
        def _():
            s_ref[...] = jnp.zeros_like(s_ref)

        for j in range(per):
            rows = _chunk_rows(j)
            s = [s_ref[h] for h in range(HEADS)]
            vn = [u_ref[h, rows, :] - _dot(w_ref[h, rows, :], s[h], NN) for h in range(HEADS)]
            o = [_dot(qg_ref[h, rows, :], s[h], NN) + _dot(attn_ref[j, h], vn[h], NN) for h in range(HEADS)]
            eg = [jnp.exp(_colsum(g_ref[h, rows, :])) for h in range(HEADS)]
            for h in range(HEADS):
                st_ref[j, h] = s[h]
                s_ref[h] = s[h] * eg[h] + _dot(kg_ref[h, rows, :], vn[h], TN)
                o_ref[h, rows, :] = o[h]
                vn_ref[h, rows, :] = vn[h]

    act = pl.BlockSpec((HEADS, per * CHUNK, LANE), lambda n: (0, n, 0))
    out = jax.ShapeDtypeStruct((HEADS, t, LANE), F32)
    return _pcall(
        body, name=name, out_shape=(out, out, jax.ShapeDtypeStruct((n_chunks, HEADS, LANE, LANE), F32)), grid=(n_chunks // per,),
        in_specs=[act] * 4 + [pl.BlockSpec((per, HEADS, CHUNK, CHUNK), lambda n: (n, 0, 0, 0)), act],
        out_specs=(act, act, pl.BlockSpec((per, HEADS, LANE, LANE), lambda n: (n, 0, 0, 0))),
        scratch_shapes=[pltpu.VMEM((HEADS, LANE, LANE), F32)], semantics=("arbitrary",))(u, w, qg, kg, attn, g)


def _deltanet_bscan(w, qg, kg, attn, g, do, *, name):
    t = w.shape[1]
    n_chunks = t // CHUNK
    per = SCAN_CHUNKS_PER_STEP
    steps = n_chunks // per

    def body(w_ref, qg_ref, kg_ref, attn_ref, g_ref, do_ref, dvn_ref, dsn_ref, ds_ref):
        @pl.when(pl.program_id(0) == 0)
        def _():
            ds_ref[...] = jnp.zeros_like(ds_ref)

        for j in reversed(range(per)):
            rows = _chunk_rows(j)
            dsn = [ds_ref[h] for h in range(HEADS)]
            dov = [do_ref[h, rows, :] for h in range(HEADS)]
            dvn = [_dot(attn_ref[j, h], dov[h], TN) + _dot(kg_ref[h, rows, :], dsn[h], NN) for h in range(HEADS)]
            eg = [jnp.exp(_colsum(g_ref[h, rows, :])) for h in range(HEADS)]
            for h in range(HEADS):
                dsn_ref[j, h] = dsn[h]
                ds_ref[h] = _dot(qg_ref[h, rows, :], dov[h], TN) + eg[h] * dsn[h] - _dot(w_ref[h, rows, :], dvn[h], TN)
                dvn_ref[h, rows, :] = dvn[h]

    act = pl.BlockSpec((HEADS, per * CHUNK, LANE), lambda n: (0, steps - 1 - n, 0))
    return _pcall(
        body, name=name,
        out_shape=(jax.ShapeDtypeStruct((HEADS, t, LANE), F32), jax.ShapeDtypeStruct((n_chunks, HEADS, LANE, LANE), F32)),
        grid=(steps,),
        in_specs=[act] * 3 + [pl.BlockSpec((per, HEADS, CHUNK, CHUNK), lambda n: (steps - 1 - n, 0, 0, 0)), act, act],
        out_specs=(act, pl.BlockSpec((per, HEADS, LANE, LANE), lambda n: (steps - 1 - n, 0, 0, 0))),
        scratch_shapes=[pltpu.VMEM((HEADS, LANE, LANE), F32)], semantics=("arbitrary",))(w, qg, kg, attn, g, do)


def _sum_all(x):
    return jnp.sum(jnp.sum(x, axis=1, keepdims=True), axis=0, keepdims=True)


def _rowsum(x):
    return jnp.sum(x, axis=1, keepdims=True)


def _deltanet_post(qkv, g, beta, tmats, states, dstates, do, dvn, vn, *, name):
    t = qkv.shape[1]
    n_chunks = t // CHUNK
    per = CHUNKS_PER_STEP
    probs = [(j, h) for j in range(per) for h in range(HEADS)]

    def body(qkv_ref, g_ref, b_ref, tm_ref, st_ref, dsn_ref, do_ref, dvn_ref, vn_ref, dqkv_ref, dg_ref, db_ref):
        incl, strict, _ = _chunk_masks()
        ones = jnp.ones((CHUNK, LANE), BF16)
        last_row = lax.broadcasted_iota(jnp.int32, (CHUNK, LANE), 0) == CHUNK - 1
        z = lambda f, *cols: [f(*a) for a in zip(*cols)]
        q = [qkv_ref[h, _chunk_rows(j), :] for j, h in probs]
        k = [qkv_ref[HEADS + h, _chunk_rows(j), :] for j, h in probs]
        v = [qkv_ref[2 * HEADS + h, _chunk_rows(j), :] for j, h in probs]
        bv = [b_ref[h, _chunk_rows(j), :] for j, h in probs]
        dov = [do_ref[h, _chunk_rows(j), :] for j, h in probs]
        dvn_ = [dvn_ref[h, _chunk_rows(j), :] for j, h in probs]
        vn_ = [vn_ref[h, _chunk_rows(j), :] for j, h in probs]
        tm = [tm_ref[j, h] for j, h in probs]
        s = [st_ref[j, h] for j, h in probs]
        dsn = [dsn_ref[j, h] for j, h in probs]
        dec = [_chunk_decay(g_ref[h, _chunk_rows(j), :], incl) for j, h in probs]
        decay = [d[2] for d in dec]
        egc = [jnp.exp(d[0]) for d in dec]
        ekg = [jnp.exp(d[1] - d[0]) for d in dec]
        kb = z(lambda a, b: a * b, k, bv)
        vb = z(lambda a, b: a * b, v, bv)
        kbg = z(lambda a, b: a * b, kb, egc)
        qg = z(lambda a, b: a * b, q, egc)
        kg = z(lambda a, b: a * b, k, ekg)
        kk = z(lambda a, b: _dot(a, b, NT), kb, k)
        qk = z(lambda a, b: _dot(a, b, NT), q, k)
        dattn = z(lambda a, b: jnp.where(incl, _dot(a, b, NT), 0.0), dov, vn_)
        dqg = z(lambda a, b: _dot(a, b, NT), dov, s)
        dkg = z(lambda a, b: _dot(a, b, NT), vn_, dsn)
        dglast = z(lambda a, b, c, d, e: _sum_all(a * b) * jnp.exp(e[1]) + _sum_all(c * d), s, dsn, dkg, kg, dec)
        dw = z(lambda a, b: -_dot(a, b, NT), dvn_, s)
        dtm = z(lambda a, b, c, d: _dot(a, b, NT) + _dot(c, d, NT), dvn_, vb, dw, kbg)
        dvb = z(lambda a, b: _dot(a, b, TN), tm, dvn_)
        dkbg = z(lambda a, b: _dot(a, b, TN), tm, dw)
        dlow = z(lambda a, b: jnp.where(strict, -_dot(_dot(a, b, TN), a, NT), 0.0), tm, dtm)
        dkk = z(lambda a, b: a * b, dlow, decay)
        dqk = z(lambda a, b: a * b, dattn, decay)
        dkb = z(lambda a, b, c, d: _dot(a, b, NN) + c * d, dkk, k, dkbg, egc)
        dk = z(lambda a, b, c, d, e, f, g_, h_: _dot(a, b, TN) + _dot(c, d, TN) + e * f + g_ * h_, dkk, kb, dqk, q, dkg, ekg, dkb, bv)
        dq = z(lambda a, b, c, d: _dot(a, b, NN) + c * d, dqk, k, dqg, egc)
        m = z(lambda a, b, c, d, e: (a * b + c * d) * e, dlow, kk, dattn, qk, decay)
        mcol = [_dot(mh, ones, TN) + _dot(ml, ones, TN) for mh, ml in (_split(a) for a in m)]
        for i, (j, h) in enumerate(probs):
            rows = _chunk_rows(j)
            dqkv_ref[h, rows, :] = dq[i]
            dqkv_ref[HEADS + h, rows, :] = dk[i]
            dqkv_ref[2 * HEADS + h, rows, :] = dvb[i] * bv[i]
            db_ref[h, rows, :] = jnp.broadcast_to(_rowsum(dkb[i] * k[i] + dvb[i] * v[i]), (CHUNK, LANE))
            dgc = (_rowsum(dqg[i] * qg[i] + dkbg[i] * kbg[i] - dkg[i] * kg[i]) + _rowsum(m[i]) - mcol[i]
                   + jnp.where(last_row, dglast[i], 0.0))
            dg_ref[h, rows, :] = _suffix_sum_rows(dgc)

    act = lambda heads: pl.BlockSpec((heads, per * CHUNK, LANE), lambda n: (0, n, 0))
    mat = lambda d: pl.BlockSpec((per, HEADS, d, d), lambda n: (n, 0, 0, 0))
    out = jax.ShapeDtypeStruct((HEADS, t, LANE), F32)
    return _pcall(
        body, name=name, out_shape=(jax.ShapeDtypeStruct((3 * HEADS, t, LANE), F32), out, out), grid=(n_chunks // per,),
        in_specs=[act(3 * HEADS), act(HEADS), act(HEADS), mat(CHUNK), mat(LANE), mat(LANE), act(HEADS), act(HEADS), act(HEADS)],
        out_specs=(act(3 * HEADS), act(HEADS), act(HEADS)), semantics=("parallel",),
        vmem_limit=VMEM_LIMIT)(qkv, g, beta, tmats, states, dstates, do, dvn, vn)


ANY = pl.BlockSpec(memory_space=pl.ANY)
PEERS = N_DEV - 1


def _all_gather(arrays, *, name):
    n = len(arrays)

    def body(*refs):
        ins, outs = refs[:n], refs[n:2 * n]
        send_sems, recv_sems, local_sems = refs[2 * n:]
        x, y, c = lax.axis_index("x"), lax.axis_index("y"), lax.axis_index("c")
        me, sibling = (x, y, c), (x, y, 1 - c)
        chips = [(1 - x, y), (x, 1 - y), (1 - x, 1 - y)]

        def copy(a, k, block, to, src=None):
            dst = outs[a].at[4 * block[0] + 2 * block[1] + block[2]]
            return pltpu.make_async_remote_copy(src_ref=dst if src is None else src, dst_ref=dst, send_sem=send_sems.at[a * PEERS + k],
                                                recv_sem=recv_sems.at[a * PEERS + k], device_id=to, device_id_type=MESH)

        local = [pltpu.make_async_copy(ins[a], outs[a].at[4 * x + 2 * y + c], local_sems.at[a]) for a in range(n)]
        for cp in local:
            cp.start()
        first = []
        for a in range(n):
            first.append(copy(a, 0, me, sibling, src=ins[a]))
            first += [copy(a, 1 + j, me, (*chip, c), src=ins[a]) for j, chip in enumerate(chips)]
        for cp in first:
            cp.start()
        passed = []
        for a in range(n):
            for j, chip in enumerate(chips):
                copy(a, 1 + j, (*chip, c), me).wait_recv()
                fwd = copy(a, 4 + j, (*chip, c), sibling)
                fwd.start()
                passed.append(fwd)
        for a in range(n):
            copy(a, 0, sibling, me).wait_recv()
            for j, chip in enumerate(chips):
                copy(a, 4 + j, (*chip, 1 - c), me).wait_recv()
        for cp in first + passed:
            cp.wait_send()
        for cp in local:
            cp.wait()

    return _pcall(body, name=name, out_shape=tuple(jax.ShapeDtypeStruct((N_DEV,) + a.shape, a.dtype) for a in arrays),
                  in_specs=[ANY] * n, out_specs=(ANY,) * n,
                  scratch_shapes=[pltpu.SemaphoreType.DMA((n * PEERS,)), pltpu.SemaphoreType.DMA((n * PEERS,)),
                                  pltpu.SemaphoreType.DMA((n,))])(*arrays)


def _exchange_blocks(arrays, *, name):
    n = len(arrays)

    def body(*refs):
        ins, outs = refs[:n], refs[n:2 * n]
        send_sems, recv_sems, local_sems = refs[2 * n:]
        x, y, c = lax.axis_index("x"), lax.axis_index("y"), lax.axis_index("c")
        mine = 4 * x + 2 * y + c
        copies = []
        for a in range(n):
            lc = pltpu.make_async_copy(ins[a].at[mine], outs[a].at[mine], local_sems.at[a])
            lc.start()
            copies.append(lc)
            for k in range(1, N_DEV):
                px = 1 - x if k & 4 else x
                py = 1 - y if k & 2 else y
                pc = 1 - c if k & 1 else c
                cp = pltpu.make_async_remote_copy(src_ref=ins[a].at[4 * px + 2 * py + pc], dst_ref=outs[a].at[mine],
                                                  send_sem=send_sems.at[a * PEERS + k - 1], recv_sem=recv_sems.at[a * PEERS + k - 1],
                                                  device_id=(px, py, pc), device_id_type=MESH)
                cp.start()
                copies.append(cp)
        for cp in copies:
            cp.wait()

    return _pcall(body, name=name, out_shape=tuple(jax.ShapeDtypeStruct(a.shape, a.dtype) for a in arrays),
                  in_specs=[ANY] * n, out_specs=(ANY,) * n,
                  scratch_shapes=[pltpu.SemaphoreType.DMA((n * PEERS,)), pltpu.SemaphoreType.DMA((n * PEERS,)),
                                  pltpu.SemaphoreType.DMA((n,))])(*arrays)


def _adamw_reduce(w, parts, m, v, *, name):
    layers, r, c = w.shape
    assert len(parts) == layers
    tr = _tile(r, 512, 16)
    tiles = r // tr
    bc1 = 1.0 - ADAM_B1 ** ADAM_STEP
    bc2 = 1.0 - ADAM_B2 ** ADAM_STEP

    def body(w_ref, *rest):
        p_refs = rest[:layers]
        m_ref, v_ref, g_ref, d_ref, nm_ref, nv_ref = rest[layers:]

        def update(p_ref):
            g = p_ref[0, :, pl.ds(0, c)].astype(F32)
            for s in range(1, N_DEV):
                g = g + p_ref[s, :, pl.ds(0, c)].astype(F32)
            nm = ADAM_B1 * m_ref[0] + (1.0 - ADAM_B1) * g
            nv = ADAM_B2 * v_ref[0] + (1.0 - ADAM_B2) * (g * g)
            g_ref[0] = g
            nm_ref[0] = nm
            nv_ref[0] = nv
            d_ref[0] = -ADAM_LR * ((nm / bc1) / (jnp.sqrt(nv / bc2) + ADAM_EPS) + ADAM_WD * w_ref[0])

        for layer in range(layers):
            pl.when(pl.program_id(0) == layer)(functools.partial(update, p_refs[layer]))

    def part_spec(layer, shape):
        rest = 0 if layer > 0 else tiles - 1
        return pl.BlockSpec((N_DEV, tr, shape[2]), lambda l, i: (0, jnp.where(l == layer, i, rest), 0))

    spec = pl.BlockSpec((1, tr, c), lambda l, i: (l, i, 0))
    out = jax.ShapeDtypeStruct((layers, r, c), F32)
    return _pcall(body, name=name, out_shape=(out,) * 4, grid=(layers, tiles),
                  in_specs=[spec] + [part_spec(layer, p.shape) for layer, p in enumerate(parts)] + [spec, spec],
                  out_specs=(spec,) * 4, semantics=("arbitrary", "arbitrary"), vmem_limit=VMEM_LIMIT)(w, *parts, m, v)


def _pool_windows():
    return jnp.repeat(jnp.asarray(POOL_WINDOWS, F32), POOL_DIM // len(POOL_WINDOWS))[None, :]


def _block_diag_pairs(pool_w):
    z = jnp.zeros_like(pool_w[0])
    return jnp.stack([jnp.block([[pool_w[2 * b], z], [z, pool_w[2 * b + 1]]]) for b in range(2)])


def _pad_lanes(vec):
    return jnp.zeros((1, LANE), F32).at[0, :vec.shape[0]].set(vec)


FF_SHARD = D_FF // N_DEV
FF_BLOCK = 384
D_FF_PAD = N_DEV * FF_BLOCK


def _layer_fwd(x, p_i, wt):
    h1 = _rmsnorm_fwd(x, wt["norm1_g"], name="rmsnorm_fwd")
    proj = _matmul(h1, wt["w_in"], "nn", name="mm_in")
    qkv = _qkv_prep_fwd(proj, wt["conv_qkv"], name="qkv_prep_fwd")
    g, beta = _gates_fwd(proj, wt["a_log"], wt["dt_bias"], name="gates_fwd")
    u, w, qg, kg, attn, tmats = _deltanet_prep(qkv, g, beta, name="deltanet_prep")
    o, vn, states = _deltanet_scan(u, w, qg, kg, attn, g, name="deltanet_scan")
    o_a = _apost_fwd(o, proj, wt["onorm_g"], name="apost_fwd")
    o_b = _pool_fwd(proj, wt["pool_win"], wt["pool_wbd"], wt["pool_scale"], name="pool_fwd")
    o_c = _sconv_fwd(proj, wt["sconv_w"], name="sconv_fwd")
    mixed = jnp.concatenate([o_a, o_b, o_c], axis=1)
    x1 = _matmul(mixed, wt["w_out"], "nn", res=x, name="mm_out")
    h2 = _rmsnorm_fwd(x1, wt["norm2_g"], name="rmsnorm_fwd")
    gate = _matmul(h2, wt["w_gate"], "nn", b_blocked=True, name="mm_gate")
    up = _matmul(h2, wt["w_up"], "nn", b_blocked=True, name="mm_up")
    ff = _swiglu_fwd(gate, up, name="swiglu_fwd")
    x2 = _matmul(ff, wt["w_down"], "nn", res=x1, name="mm_down")
    pgl = _matmul(x2, wt["ple_gate"], "nn", name="mm_pleg")
    pp = _matmul(p_i, wt["ple_proj"], "nn", b_blocked=True, name="mm_plep")
    x3 = _ple_fwd(x2, pgl, pp, name="ple_fwd")
    saved = dict(x=x, h1=h1, proj=proj, qkv=qkv, g=g, beta=beta, o=o, states=states, tmats=tmats, mixed=mixed, x1=x1, h2=h2,
                 gate=gate, up=up, ff=ff, x2=x2, pgl=pgl, pp=pp, p=p_i, w=w, qg=qg, kg=kg, attn=attn, vn=vn)
    return x3, saved


def _col_blocks(g):
    a = g.shape[0]
    return jnp.transpose(g.reshape(a, N_DEV, -1), (1, 0, 2))


def _cols_joined(blocks):
    return jnp.transpose(blocks, (1, 0, 2)).reshape(blocks.shape[1], -1)


def _layer_bwd(dx3, sv, wt):
    gr, big = {}, {}
    rows = D_MODEL // N_DEV
    dpgl, dpp = _ple_bwd(dx3, sv["pgl"], sv["pp"], name="ple_bwd")
    big["ple_proj"] = _matmul(sv["p"], dpp, "tn", out_blocked=(N_DEV, rows), out_dtype=BF16, name="mm_dplep")
    big["ple_gate"] = _matmul(sv["x2"], dpgl, "tn", out_dtype=BF16, name="mm_dpleg").reshape(N_DEV, rows, D_MODEL)
    dx2 = _matmul(dpgl, wt["ple_gate"], "nt", res=dx3, name="mm_dx2")
    big["w_down"] = _matmul(sv["ff"], dx2, "tn", out_dtype=BF16, name="mm_ddown").reshape(N_DEV, FF_BLOCK, D_MODEL)
    dff = _matmul(dx2, wt["w_down"], "nt", name="mm_dff")
    dgate, dup = _swiglu_bwd(sv["gate"], sv["up"], dff, name="swiglu_bwd")
    big["w_gate"] = _matmul(sv["h2"], dgate, "tn", out_blocked=(N_DEV, FF_BLOCK), out_dtype=BF16, name="mm_dgate")
    big["w_up"] = _matmul(sv["h2"], dup, "tn", out_blocked=(N_DEV, FF_BLOCK), out_dtype=BF16, name="mm_dup")
    dh2 = _matmul(dgate, wt["w_gate"], "nt", b_blocked=True, name="mm_dh2_gate")
    dh2 = _matmul(dup, wt["w_up"], "nt", b_blocked=True, res=dh2, name="mm_dh2_up")
    dx1, gr["norm2_g"] = _rmsnorm_bwd(sv["x1"], wt["norm2_g"], dh2, dx2, name="rmsnorm_bwd")
    big["w_out"] = _matmul(sv["mixed"], dx1, "tn", out_dtype=BF16, name="mm_dout").reshape(N_DEV, rows, D_MODEL)
    dmixed = _matmul(dx1, wt["w_out"], "nt", name="mm_dmixed")
    proj = sv["proj"]
    dcb, dcc, dch, dsconv = _sconv_bwd(proj, wt["sconv_w"], dmixed, name="sconv_bwd")
    big["sconv_w"] = _col_blocks(dsconv)
    dhp, dwbd, gr["pool_scale"] = _pool_bwd(proj, wt["pool_win"], wt["pool_wbd"], wt["pool_scale"], dmixed, name="pool_bwd")
    half = LANE // 2
    gr["pool_w"] = jnp.stack([dwbd[0, :half, :half], dwbd[0, half:, half:], dwbd[1, :half, :half], dwbd[1, half:, half:]])
    do, dz, gr["onorm_g"] = _apost_bwd(sv["o"], proj, wt["onorm_g"], dmixed, name="apost_bwd")
    dvn, dstates = _deltanet_bscan(sv["w"], sv["qg"], sv["kg"], sv["attn"], sv["g"], do, name="deltanet_bscan")
    dqkv_h, dg, dbeta = _deltanet_post(sv["qkv"], sv["g"], sv["beta"], sv["tmats"], sv["states"], dstates, do, dvn, sv["vn"],
                                       name="deltanet_post")
    dab, dalog, ddtb = _gates_bwd(proj, wt["a_log"], wt["dt_bias"], dg, dbeta, name="gates_bwd")
    gr["a_log"], gr["dt_bias"] = dalog[0, :HEADS], ddtb[0, :HEADS]
    dqkv, dconv = _qkv_prep_bwd(proj, wt["conv_qkv"], dqkv_h, name="qkv_prep_bwd")
    big["conv_qkv"] = _col_blocks(dconv)
    dproj = jnp.concatenate([dqkv, dz, dab, dhp, dcb, dcc, dch], axis=1)
    dwin = _matmul(sv["h1"], dproj, "tn", out_dtype=BF16, name="mm_din")
    big["w_in"] = _col_blocks(jnp.concatenate([dwin[:, :AB_COL + 2 * HEADS], dwin[:, AB_COL + LANE:]], axis=1))
    dh1 = _matmul(dproj, wt["w_in"], "nt", name="mm_dh1")
    dx, gr["norm1_g"] = _rmsnorm_bwd(sv["x"], wt["norm1_g"], dh1, dx1, name="rmsnorm_bwd")
    return dx, big, gr


def _layer_weights(gathered, w, i):
    w_in = _cols_joined(gathered["w_in"])
    return dict(
        norm1_g=w["norm1_g"][i][None], norm2_g=w["norm2_g"][i][None], onorm_g=w["onorm_g"][i][None],
        a_log=_pad_lanes(w["a_log"][i]), dt_bias=_pad_lanes(w["dt_bias"][i]),
        pool_scale=w["pool_scale"][i][None], pool_win=_pool_windows(), pool_wbd=_block_diag_pairs(w["pool_w"][i]),
        conv_qkv=_cols_joined(gathered["conv_qkv"]), sconv_w=_cols_joined(gathered["sconv_w"]),
        w_in=jnp.concatenate([w_in[:, :AB_COL + 2 * HEADS], jnp.zeros((D_MODEL, LANE - 2 * HEADS), BF16),
                              w_in[:, AB_COL + 2 * HEADS:]], axis=1),
        w_gate=gathered["w_gate"], w_up=gathered["w_up"], w_down=gathered["w_down"].reshape(D_FF_PAD, D_MODEL),
        w_out=gathered["w_out"].reshape(D_MODEL, D_MODEL), ple_gate=gathered["ple_gate"].reshape(D_MODEL, D_MODEL),
        ple_proj=gathered["ple_proj"])


def _local_step(x, p, target, layers, final_g):
    saved = []
    h = x
    for i in range(DEPTH):
        h, sv = _layer_fwd(h, p[i], layers[i])
        saved.append(sv)
    dx, dgf, loss = _loss_head(h, final_g, target, name="loss_head")
    big, small = [None] * DEPTH, [None] * DEPTH
    for i in reversed(range(DEPTH)):
        dx, big[i], small[i] = _layer_bwd(dx, saved[i], layers[i])
    return loss, dx, big, small, dgf


SHARDED = ("w_in", "w_gate", "w_up", "w_down", "w_out", "ple_gate", "ple_proj", "conv_qkv", "sconv_w")
SMALL = ("norm1_g", "a_log", "dt_bias", "onorm_g", "pool_w", "pool_scale", "norm2_g", "final_g")
SLAB_COLS = 1024


def _payload(name, shard):
    if name in ("conv_qkv", "sconv_w"):
        return shard
    out = shard.astype(BF16)
    if name in ("w_gate", "w_up"):
        out = jnp.pad(out, ((0, 0), (0, FF_BLOCK - FF_SHARD)))
    if name == "w_down":
        out = jnp.pad(out, ((0, FF_BLOCK - FF_SHARD), (0, 0)))
    return out


def _slab_rows(shape):
    size = 1
    for s in shape:
        size *= s
    return SUBLANE * -(-size // (SUBLANE * SLAB_COLS))


def _pack_slab(parts, extra_row):
    rows = []
    for name in SMALL:
        flat = parts[name].reshape(-1)
        nrow = _slab_rows(parts[name].shape)
        rows.append(jnp.pad(flat, (0, nrow * SLAB_COLS - flat.shape[0])).reshape(nrow, SLAB_COLS))
    rows.append(jnp.pad(extra_row, ((0, SUBLANE - 1), (0, 0))))
    return jnp.concatenate(rows, axis=0)


def _unpack_slab(slab, shapes):
    out, row = {}, 0
    for name in SMALL:
        size = 1
        for s in shapes[name]:
            size *= s
        out[name] = slab[row:row + _slab_rows(shapes[name])].reshape(-1)[:size].reshape(shapes[name])
        row += _slab_rows(shapes[name])
    return out, row


def kernel(x, p, norm1_g, w_in, conv_qkv, a_log, dt_bias, onorm_g, pool_w, pool_scale, sconv_w, w_out, norm2_g, w_gate, w_up, w_down, ple_proj, ple_gate, final_g, loss_target, m_norm1_g, m_w_in, m_conv_qkv, m_a_log, m_dt_bias, m_onorm_g, m_pool_w, m_pool_scale, m_sconv_w, m_w_out, m_norm2_g, m_w_gate, m_w_up, m_w_down, m_ple_proj, m_ple_gate, m_final_g, v_norm1_g, v_w_in, v_conv_qkv, v_a_log, v_dt_bias, v_onorm_g, v_pool_w, v_pool_scale, v_sconv_w, v_w_out, v_norm2_g, v_w_gate, v_w_up, v_w_down, v_ple_proj, v_ple_gate, v_final_g):
    names = ["norm1_g", "w_in", "conv_qkv", "a_log", "dt_bias", "onorm_g", "pool_w", "pool_scale", "sconv_w", "w_out", "norm2_g",
             "w_gate", "w_up", "w_down", "ple_proj", "ple_gate", "final_g"]
    w = dict(zip(names, [norm1_g, w_in, conv_qkv, a_log, dt_bias, onorm_g, pool_w, pool_scale, sconv_w, w_out, norm2_g, w_gate, w_up,
                         w_down, ple_proj, ple_gate, final_g]))
    m = dict(zip(names, [m_norm1_g, m_w_in, m_conv_qkv, m_a_log, m_dt_bias, m_onorm_g, m_pool_w, m_pool_scale, m_sconv_w, m_w_out,
                         m_norm2_g, m_w_gate, m_w_up, m_w_down, m_ple_proj, m_ple_gate, m_final_g]))
    v = dict(zip(names, [v_norm1_g, v_w_in, v_conv_qkv, v_a_log, v_dt_bias, v_onorm_g, v_pool_w, v_pool_scale, v_sconv_w, v_w_out,
                         v_norm2_g, v_w_gate, v_w_up, v_w_down, v_ple_proj, v_ple_gate, v_final_g]))

    layers = []
    for i in range(DEPTH):
        gathered = _all_gather([_payload(k, w[k][i]) for k in SHARDED], name="all_gather_weights")
        layers.append(_layer_weights(dict(zip(SHARDED, gathered)), w, i))

    loss_part, dx, big, small, dgf = _local_step(x[0], p[:, 0], loss_target[0], layers, final_g[None])

    received = [_exchange_blocks([big[i][k] for k in SHARDED], name="exchange_grad_blocks") for i in range(DEPTH)]
    grads = {k: jnp.stack([small[i][k] for i in range(DEPTH)]) for k in small[0]}
    grads = {k: g[:, 0] if k in ("norm1_g", "norm2_g", "onorm_g", "pool_scale") else g for k, g in grads.items()}
    grads["final_g"] = dgf[0]
    loss_row = jnp.pad(loss_part, ((0, 0), (0, SLAB_COLS - LANE)))
    (small_parts,) = _all_gather([_pack_slab(grads, loss_row)], name="all_gather_small_grads")

    out_g, out_d, out_m, out_v = {}, {}, {}, {}
    for j, k in enumerate(SHARDED):
        out_g[k], out_d[k], out_m[k], out_v[k] = _adamw_reduce(w[k], [received[i][j] for i in range(DEPTH)], m[k], v[k],
                                                                name="adamw_" + k)
    zero_row = jnp.zeros((1, SLAB_COLS), F32)
    slabs = _adamw_reduce(_pack_slab(w, zero_row)[None], [small_parts], _pack_slab(m, zero_row)[None],
                          _pack_slab(v, zero_row)[None], name="adamw_small")
    slabs = [s[0] for s in slabs]
    shapes = {k: w[k].shape for k in SMALL}
    for dst, slab in zip((out_g, out_d, out_m, out_v), slabs):
        vals, _ = _unpack_slab(slab, shapes)
        dst.update(vals)
    _, loss_at = _unpack_slab(slabs[0], shapes)
    loss = slabs[0][loss_at, 0]

    return (loss, dx[None], *[out_g[k] for k in names], *[out_d[k] for k in names], *[out_m[k] for k in names],
            *[out_v[k] for k in names])
```

```python
import functools

import jax
import jax.numpy as jnp
from jax import lax
from jax.experimental import pallas as pl
from jax.experimental.pallas import tpu as pltpu

F32 = jnp.float32
BF16 = jnp.bfloat16

D_MODEL = 1024
DEPTH = 2
PLE_DIM = 256
EPS = 1e-6
HEAD_DIM = 128
HEADS = 4
A_DIM = HEADS * HEAD_DIM
QKV_TAPS = 4
CHUNK = 64
POOL_WINDOWS = (2, 4, 8, 16)
POOL_DIM = 256
CONV_DIM = 256
CONV_TAPS = 3
D_FF = 2816
D_IN = 3080
D_IN_PAD = 3200
AB_COL = 2048
N_DEV = 8

ADAM_LR = 0.001
ADAM_B1 = 0.9
ADAM_B2 = 0.999
ADAM_EPS = 1e-08
ADAM_WD = 0.01
ADAM_STEP = 10

LANE = 128
SUBLANE = 8
VMEM_BYTES_V7X = 64 * 1024 * 1024
VMEM_LIMIT = 48 * 1024 * 1024

_HI = lax.Precision.HIGHEST
NN = ((1,), (0,))
NT = ((1,), (1,))
TN = ((0,), (0,))
MESH = pl.DeviceIdType.MESH


def _dot(a, b, dims, hi=False):
    if hi:
        return lax.dot_general(a, b, (dims, ((), ())), precision=_HI, preferred_element_type=F32)
    return lax.dot_general(a.astype(BF16), b.astype(BF16), (dims, ((), ())), preferred_element_type=F32)


def _pcall(body, *, name, out_shape, grid=(), in_specs=None, out_specs=None, scratch_shapes=(), semantics=None,
           vmem_limit=None, **kw):
    params = {}
    if semantics is not None:
        params["dimension_semantics"] = semantics
    if vmem_limit is not None:
        params["vmem_limit_bytes"] = vmem_limit
    return pl.pallas_call(
        body, name=name, out_shape=out_shape, grid=grid, in_specs=in_specs, out_specs=out_specs,
        scratch_shapes=list(scratch_shapes), compiler_params=pltpu.CompilerParams(**params), **kw)


def _sigmoid(x):
    return 1.0 / (1.0 + jnp.exp(-x))


def _softplus(x):
    return jnp.maximum(x, 0.0) + jnp.log(1.0 + jnp.exp(-jnp.abs(x)))


def _tile(n, cap, mult):
    if n <= cap:
        return n
    best = None
    for t in range(mult, cap + 1, mult):
        if n % t == 0:
            best = t
    assert best is not None, (n, cap, mult)
    return best


ROWS_PER_STEP = 512
COLS_PER_DOT = 640


def _matmul_rows(a, b, mode, *, name, res=None, out_dtype=F32, b_blocked=False):
    m, k = a.shape
    if b_blocked:
        nb, _, bw = b.shape
        n = nb * bw if mode == "nn" else b.shape[1]
    else:
        n = b.shape[1] if mode == "nn" else b.shape[0]
    tm = _tile(m, ROWS_PER_STEP, 16)
    cn = bw if (b_blocked and mode == "nn") else _tile(n, COLS_PER_DOT, LANE)
    has_res = res is not None

    def body(*refs):
        a_ref, b_ref = refs[0], refs[1]
        res_ref = refs[2] if has_res else None
        o_ref = refs[2 + has_res]
        if not (b_blocked and mode == "nt"):
            av = a_ref[...].astype(BF16)
        for j in range(n // cn):
            cols = pl.ds(j * cn, cn)
            if mode == "nn":
                part = _dot(av, b_ref[j] if b_blocked else b_ref[:, cols], NN)
            elif not b_blocked:
                part = _dot(av, b_ref[cols, :], NT)
            else:
                part = None
                for s in range(nb):
                    term = _dot(a_ref[:, pl.ds(s * bw, bw)], b_ref[s, cols, :], NT)
                    part = term if part is None else part + term
            if has_res:
                part = part + res_ref[:, cols]
            o_ref[:, cols] = part.astype(o_ref.dtype)

    row = lambda width: pl.BlockSpec((tm, width), lambda i: (i, 0))
    whole = pl.BlockSpec(b.shape, lambda i: (0,) * b.ndim)
    ins = [a, b] + ([res] if has_res else [])
    specs = [row(k), whole] + ([row(n)] if has_res else [])
    return _pcall(body, name=name, out_shape=jax.ShapeDtypeStruct((m, n), out_dtype), grid=(m // tm,), in_specs=specs,
                  out_specs=row(n), semantics=("parallel",), vmem_limit=VMEM_LIMIT)(*ins)


def _matmul(a, b, mode, *, name, res=None, out_dtype=F32, b_blocked=False, out_blocked=None):
    if mode != "tn":
        return _matmul_rows(a, b, mode, name=name, res=res, out_dtype=out_dtype, b_blocked=b_blocked)
    assert res is None and not b_blocked
    (t, m), (t2, n) = a.shape, b.shape
    assert t == t2, (a.shape, b.shape)
    tm = _tile(m, 1024, LANE)
    tn = _tile(n, COLS_PER_DOT, LANE)
    if out_blocked is not None:
        assert out_blocked[0] * out_blocked[1] == n
        tn = out_blocked[1]

    def body(a_ref, b_ref, o_ref):
        part = _dot(a_ref[...], b_ref[...], TN).astype(o_ref.dtype)
        if out_blocked is None:
            o_ref[...] = part
        else:
            o_ref[0] = part

    o_spec = (pl.BlockSpec((tm, tn), lambda i, j: (i, j)) if out_blocked is None
              else pl.BlockSpec((1, tm, tn), lambda i, j: (j, i, 0)))
    o_shape = (m, n) if out_blocked is None else (out_blocked[0], m, out_blocked[1])
    return _pcall(body, name=name, out_shape=jax.ShapeDtypeStruct(o_shape, out_dtype), grid=(m // tm, n // tn),
                  in_specs=[pl.BlockSpec((t, tm), lambda i, j: (0, i)), pl.BlockSpec((t, tn), lambda i, j: (0, j))],
                  out_specs=o_spec, semantics=("parallel", "parallel"), vmem_limit=VMEM_LIMIT)(a, b)


ROW_TILE = 256


def _rows(t, width, idx=0):
    return pl.BlockSpec((ROW_TILE, width), lambda i: (i, idx))


def _vec(width):
    return pl.BlockSpec((1, width), lambda i: (0, 0))


def _rmsnorm_fwd(x, g, *, name):
    t, d = x.shape

    def body(x_ref, g_ref, h_ref):
        xv = x_ref[...]
        r = lax.rsqrt(jnp.mean(xv * xv, axis=-1, keepdims=True) + EPS)
        h_ref[...] = (xv * r * g_ref[...]).astype(BF16)

    return _pcall(body, name=name, out_shape=jax.ShapeDtypeStruct((t, d), BF16), grid=(t // ROW_TILE,),
                  in_specs=[_rows(t, d), _vec(d)], out_specs=_rows(t, d), semantics=("parallel",))(x, g)


def _rmsnorm_bwd(x, g, dh, dres, *, name):
    t, d = x.shape

    def body(x_ref, g_ref, dh_ref, dres_ref, dx_ref, dg_ref):
        xv = x_ref[...]
        r = lax.rsqrt(jnp.mean(xv * xv, axis=-1, keepdims=True) + EPS)
        xhat = xv * r
        dhv = dh_ref[...].astype(F32)
        dhg = dhv * g_ref[...]
        dx_ref[...] = dres_ref[...] + r * (dhg - xhat * jnp.mean(dhg * xhat, axis=-1, keepdims=True))
        part = jnp.sum(dhv * xhat, axis=0, keepdims=True)

        @pl.when(pl.program_id(0) == 0)
        def _():
            dg_ref[...] = part

        @pl.when(pl.program_id(0) > 0)
        def _():
            dg_ref[...] += part

    return _pcall(body, name=name, out_shape=(jax.ShapeDtypeStruct((t, d), F32), jax.ShapeDtypeStruct((1, d), F32)),
                  grid=(t // ROW_TILE,), in_specs=[_rows(t, d), _vec(d), _rows(t, d), _rows(t, d)],
                  out_specs=(_rows(t, d), _vec(d)), semantics=("arbitrary",))(x, g, dh, dres)


def _swiglu_fwd(gate, up, *, name):
    t, f = gate.shape

    def body(gate_ref, up_ref, ff_ref):
        gv = gate_ref[...]
        ff_ref[...] = (gv * _sigmoid(gv) * up_ref[...]).astype(BF16)

    return _pcall(body, name=name, out_shape=jax.ShapeDtypeStruct((t, f), BF16), grid=(t // ROW_TILE,),
                  in_specs=[_rows(t, f), _rows(t, f)], out_specs=_rows(t, f), semantics=("parallel",))(gate, up)


def _swiglu_bwd(gate, up, dff, *, name):
    t, f = gate.shape

    def body(gate_ref, up_ref, dff_ref, dgate_ref, dup_ref):
        gv = gate_ref[...]
        sig = _sigmoid(gv)
        dffv = dff_ref[...]
        dgate_ref[...] = (dffv * up_ref[...] * sig * (1.0 + gv * (1.0 - sig))).astype(BF16)
        dup_ref[...] = (dffv * gv * sig).astype(BF16)

    out = jax.ShapeDtypeStruct((t, f), BF16)
    return _pcall(body, name=name, out_shape=(out, out), grid=(t // ROW_TILE,), in_specs=[_rows(t, f)] * 3,
                  out_specs=(_rows(t, f),) * 2, semantics=("parallel",))(gate, up, dff)


def _ple_fwd(x2, pgl, pp, *, name):
    t, d = x2.shape

    def body(x_ref, pgl_ref, pp_ref, o_ref):
        o_ref[...] = x_ref[...] + _sigmoid(pgl_ref[...]) * pp_ref[...]

    return _pcall(body, name=name, out_shape=jax.ShapeDtypeStruct((t, d), F32), grid=(t // ROW_TILE,),
                  in_specs=[_rows(t, d)] * 3, out_specs=_rows(t, d), semantics=("parallel",))(x2, pgl, pp)


def _ple_bwd(dx3, pgl, pp, *, name):
    t, d = dx3.shape

    def body(dx_ref, pgl_ref, pp_ref, dpgl_ref, dpp_ref):
        dxv = dx_ref[...]
        sig = _sigmoid(pgl_ref[...])
        dpp_ref[...] = (dxv * sig).astype(BF16)
        dpgl_ref[...] = (dxv * pp_ref[...] * sig * (1.0 - sig)).astype(BF16)

    return _pcall(body, name=name, out_shape=(jax.ShapeDtypeStruct((t, d), BF16),) * 2, grid=(t // ROW_TILE,),
                  in_specs=[_rows(t, d)] * 3, out_specs=(_rows(t, d),) * 2, semantics=("parallel",))(dx3, pgl, pp)


def _loss_head(x3, g, target, *, name):
    t, d = x3.shape

    def body(x_ref, g_ref, t_ref, dx_ref, dg_ref, loss_ref):
        xv = x_ref[...]
        r = lax.rsqrt(jnp.mean(xv * xv, axis=-1, keepdims=True) + EPS)
        xhat = xv * r
        gv = g_ref[...]
        err = xhat * gv - t_ref[...]
        row_loss = jnp.sum(err * err, axis=-1, keepdims=True) * (0.5 / d)
        lpart = jnp.broadcast_to(jnp.sum(row_loss, axis=0, keepdims=True), (1, LANE))
        dy = err * (1.0 / d)
        dyg = dy * gv
        dx_ref[...] = r * (dyg - xhat * jnp.mean(dyg * xhat, axis=-1, keepdims=True))
        gpart = jnp.sum(dy * xhat, axis=0, keepdims=True)

        @pl.when(pl.program_id(0) == 0)
        def _():
            dg_ref[...] = gpart
            loss_ref[...] = lpart

        @pl.when(pl.program_id(0) > 0)
        def _():
            dg_ref[...] += gpart
            loss_ref[...] += lpart

    return _pcall(body, name=name,
                  out_shape=(jax.ShapeDtypeStruct((t, d), F32), jax.ShapeDtypeStruct((1, d), F32), jax.ShapeDtypeStruct((1, LANE), F32)),
                  grid=(t // ROW_TILE,), in_specs=[_rows(t, d), _vec(d), _rows(t, d)],
                  out_specs=(_rows(t, d), _vec(d), _vec(LANE)), semantics=("arbitrary",))(x3, g, target)


def _shift_down(x, d):
    if d == 0:
        return x
    row = lax.broadcasted_iota(jnp.int32, x.shape, 0)
    return jnp.where(row >= d, pltpu.roll(x, d, 0), 0.0)


def _shift_up(x, d):
    if d == 0:
        return x
    t = x.shape[0]
    row = lax.broadcasted_iota(jnp.int32, x.shape, 0)
    return jnp.where(row < t - d, pltpu.roll(x, t - d, 0), 0.0)


def _colsum(x):
    return jnp.sum(x, axis=0, keepdims=True)


def _col(t, idx_fn):
    return pl.BlockSpec((t, LANE), idx_fn)


def _conv_fwd(x, w_ref, taps):
    acc = None
    for j in range(taps):
        term = w_ref[pl.ds(j, 1), :] * _shift_down(x, taps - 1 - j)
        acc = term if acc is None else acc + term
    return acc


def _conv_bwd(x, dy, w_ref, dw_ref, taps):
    dx = None
    for j in range(taps):
        term = w_ref[pl.ds(j, 1), :] * _shift_up(dy, taps - 1 - j)
        dx = term if dx is None else dx + term
        dw_ref[pl.ds(j, 1), :] = _colsum(dy * _shift_down(x, taps - 1 - j))
    return dx


def _qkv_prep_fwd(proj, conv_w, *, name):
    t = proj.shape[0]
    scale = HEAD_DIM ** -0.5

    def body(x_ref, w_ref, o_ref):
        j = pl.program_id(0)
        c = _conv_fwd(x_ref[...], w_ref, QKV_TAPS)
        s = c * _sigmoid(c)
        r = lax.rsqrt(jnp.sum(s * s, axis=-1, keepdims=True) + EPS)
        f = jnp.where(j < 2 * HEADS, r, 1.0) * jnp.where(j < HEADS, scale, 1.0)
        o_ref[0] = s * f

    return _pcall(body, name=name, out_shape=jax.ShapeDtypeStruct((3 * HEADS, t, LANE), F32), grid=(3 * HEADS,),
                  in_specs=[_col(t, lambda j: (0, j)), pl.BlockSpec((QKV_TAPS, LANE), lambda j: (0, j))],
                  out_specs=pl.BlockSpec((1, t, LANE), lambda j: (j, 0, 0)), semantics=("parallel",),
                  vmem_limit=VMEM_LIMIT)(proj, conv_w)


def _qkv_prep_bwd(proj, conv_w, dqkv, *, name):
    t = proj.shape[0]
    scale = HEAD_DIM ** -0.5

    def body(x_ref, w_ref, d_ref, dx_ref, dw_ref):
        j = pl.program_id(0)
        xv = x_ref[...]
        c = _conv_fwd(xv, w_ref, QKV_TAPS)
        sig = _sigmoid(c)
        s = c * sig
        r = lax.rsqrt(jnp.sum(s * s, axis=-1, keepdims=True) + EPS)
        n0 = s * r
        dv = d_ref[0]
        dn0 = dv * jnp.where(j < HEADS, scale, 1.0)
        ds_norm = r * (dn0 - n0 * jnp.sum(dn0 * n0, axis=-1, keepdims=True))
        ds = jnp.where(j < 2 * HEADS, ds_norm, dv)
        dc = ds * sig * (1.0 + c * (1.0 - sig))
        dx_ref[...] = _conv_bwd(xv, dc, w_ref, dw_ref, QKV_TAPS).astype(BF16)

    return _pcall(body, name=name,
                  out_shape=(jax.ShapeDtypeStruct((t, 3 * A_DIM), BF16), jax.ShapeDtypeStruct((QKV_TAPS, 3 * A_DIM), F32)),
                  grid=(3 * HEADS,),
                  in_specs=[_col(t, lambda j: (0, j)), pl.BlockSpec((QKV_TAPS, LANE), lambda j: (0, j)),
                            pl.BlockSpec((1, t, LANE), lambda j: (j, 0, 0))],
                  out_specs=(_col(t, lambda j: (0, j)), pl.BlockSpec((QKV_TAPS, LANE), lambda j: (0, j))),
                  semantics=("parallel",), vmem_limit=VMEM_LIMIT)(proj, conv_w, dqkv)


def _lane_pick(x, lane_idx, lane):
    return jnp.broadcast_to(jnp.sum(jnp.where(lane == lane_idx, x, 0.0), axis=-1, keepdims=True), x.shape)


def _gates_fwd(proj, alog, dtb, *, name):
    t = proj.shape[0]

    def body(x_ref, alog_ref, dtb_ref, g_ref, b_ref):
        xv = x_ref[...]
        lane = lax.broadcasted_iota(jnp.int32, xv.shape, 1)
        gall = -jnp.exp(alog_ref[...]) * _softplus(xv + dtb_ref[...])
        ball = _sigmoid(xv)
        for h in range(HEADS):
            g_ref[h] = _lane_pick(gall, h, lane)
            b_ref[h] = _lane_pick(ball, HEADS + h, lane)

    out = jax.ShapeDtypeStruct((HEADS, t, LANE), F32)
    whole = pl.BlockSpec((HEADS, t, LANE), lambda i: (0, 0, 0))
    return _pcall(body, name=name, out_shape=(out, out), grid=(1,),
                  in_specs=[_col(t, lambda i: (0, AB_COL // LANE)), _vec(LANE), _vec(LANE)], out_specs=(whole, whole),
                  semantics=("arbitrary",), vmem_limit=VMEM_LIMIT)(proj, alog, dtb)


def _gates_bwd(proj, alog, dtb, dg, dbeta, *, name):
    t = proj.shape[0]

    def body(x_ref, alog_ref, dtb_ref, dg_ref, db_ref, dab_ref, dalog_ref, ddtb_ref):
        xv = x_ref[...]
        lane = lax.broadcasted_iota(jnp.int32, xv.shape, 1)
        lane1 = lax.broadcasted_iota(jnp.int32, (1, LANE), 1)
        z = xv + dtb_ref[...]
        nea = -jnp.exp(alog_ref[...])
        da_f = nea * _sigmoid(z)
        g_f = nea * _softplus(z)
        ball = _sigmoid(xv)
        db_f = ball * (1.0 - ball)
        dab = jnp.zeros_like(xv)
        dalog = jnp.zeros((1, LANE), F32)
        for h in range(HEADS):
            dgh = dg_ref[h]
            dab = dab + jnp.where(lane == h, dgh * da_f, 0.0) + jnp.where(lane == HEADS + h, db_ref[h] * db_f, 0.0)
            dalog = dalog + jnp.where(lane1 == h, _colsum(dgh * g_f), 0.0)
        dab_ref[...] = dab.astype(BF16)
        dalog_ref[...] = dalog
        ddtb_ref[...] = jnp.where(lane1 < HEADS, _colsum(dab), 0.0)

    whole = pl.BlockSpec((HEADS, t, LANE), lambda i: (0, 0, 0))
    vec = jax.ShapeDtypeStruct((1, LANE), F32)
    return _pcall(body, name=name, out_shape=(jax.ShapeDtypeStruct((t, LANE), BF16), vec, vec), grid=(1,),
                  in_specs=[_col(t, lambda i: (0, AB_COL // LANE)), _vec(LANE), _vec(LANE), whole, whole],
                  out_specs=(_col(t, lambda i: (0, 0)), _vec(LANE), _vec(LANE)), semantics=("arbitrary",),
                  vmem_limit=VMEM_LIMIT)(proj, alog, dtb, dg, dbeta)


Z_COL = 3 * A_DIM // LANE


def _apost_fwd(o, proj, gn, *, name):
    t = proj.shape[0]

    def body(o_ref, z_ref, gn_ref, y_ref):
        ov = o_ref[0]
        z = z_ref[...]
        r = lax.rsqrt(jnp.mean(ov * ov, axis=-1, keepdims=True) + EPS)
        y_ref[...] = (ov * r * gn_ref[...] * (z * _sigmoid(z))).astype(BF16)

    return _pcall(body, name=name, out_shape=jax.ShapeDtypeStruct((t, A_DIM), BF16), grid=(HEADS,),
                  in_specs=[pl.BlockSpec((1, t, LANE), lambda h: (h, 0, 0)), _col(t, lambda h: (0, Z_COL + h)),
                            pl.BlockSpec((1, LANE), lambda h: (0, 0))],
                  out_specs=_col(t, lambda h: (0, h)), semantics=("parallel",), vmem_limit=VMEM_LIMIT)(o, proj, gn)


def _apost_bwd(o, proj, gn, dmixed, *, name):
    t = proj.shape[0]

    def body(o_ref, z_ref, gn_ref, d_ref, do_ref, dz_ref, dgn_ref):
        ov = o_ref[0]
        z = z_ref[...]
        gnv = gn_ref[...]
        dv = d_ref[...]
        r = lax.rsqrt(jnp.mean(ov * ov, axis=-1, keepdims=True) + EPS)
        ohat = ov * r
        sig = _sigmoid(z)
        dy = dv * (z * sig)
        dz_ref[...] = (dv * ohat * gnv * sig * (1.0 + z * (1.0 - sig))).astype(BF16)
        dyo = dy * gnv
        do_ref[0] = r * (dyo - ohat * jnp.mean(dyo * ohat, axis=-1, keepdims=True))
        part = _colsum(dy * ohat)

        @pl.when(pl.program_id(0) == 0)
        def _():
            dgn_ref[...] = part

        @pl.when(pl.program_id(0) > 0)
        def _():
            dgn_ref[...] += part

    return _pcall(body, name=name,
                  out_shape=(jax.ShapeDtypeStruct((HEADS, t, LANE), F32), jax.ShapeDtypeStruct((t, A_DIM), BF16),
                             jax.ShapeDtypeStruct((1, LANE), F32)),
                  grid=(HEADS,),
                  in_specs=[pl.BlockSpec((1, t, LANE), lambda h: (h, 0, 0)), _col(t, lambda h: (0, Z_COL + h)),
                            pl.BlockSpec((1, LANE), lambda h: (0, 0)), _col(t, lambda h: (0, h))],
                  out_specs=(pl.BlockSpec((1, t, LANE), lambda h: (h, 0, 0)), _col(t, lambda h: (0, h)),
                             pl.BlockSpec((1, LANE), lambda h: (0, 0))),
                  semantics=("arbitrary",), vmem_limit=VMEM_LIMIT)(o, proj, gn, dmixed)


POOL_COL = (AB_COL + LANE) // LANE
CB_COL = POOL_COL + POOL_DIM // LANE
CC_COL = CB_COL + CONV_DIM // LANE
CH_COL = CC_COL + CONV_DIM // LANE
MAX_WIN_LOG2 = 4


def _window_sums(x, shift):
    sums = []
    cur = x
    for k in range(MAX_WIN_LOG2):
        cur = cur + shift(cur, 1 << k)
        sums.append(cur)
    return sums


def _pick_window(sums, win):
    out = sums[-1]
    for k in range(MAX_WIN_LOG2 - 2, -1, -1):
        out = jnp.where(win == float(2 << k), sums[k], out)
    return out


def _pool_counts(shape, win):
    row = lax.broadcasted_iota(jnp.int32, shape, 0).astype(F32)
    return jnp.minimum(row + 1.0, win)


def _pool_fwd(proj, win, wbd, scale, *, name):
    t = proj.shape[0]

    def body(x_ref, win_ref, w_ref, s_ref, y_ref):
        xv = x_ref[...]
        winv = win_ref[...]
        pooled = _pick_window(_window_sums(xv, _shift_down), winv) / _pool_counts(xv.shape, winv) - xv
        y_ref[...] = (_dot(pooled, w_ref[0], NN) * s_ref[...]).astype(BF16)

    nb = POOL_DIM // LANE
    vec = pl.BlockSpec((1, LANE), lambda b: (0, b))
    return _pcall(body, name=name, out_shape=jax.ShapeDtypeStruct((t, POOL_DIM), BF16), grid=(nb,),
                  in_specs=[_col(t, lambda b: (0, POOL_COL + b)), vec, pl.BlockSpec((1, LANE, LANE), lambda b: (b, 0, 0)), vec],
                  out_specs=_col(t, lambda b: (0, b)), semantics=("parallel",), vmem_limit=VMEM_LIMIT)(proj, win, wbd, scale)


def _pool_bwd(proj, win, wbd, scale, dmixed, *, name):
    t = proj.shape[0]

    def body(x_ref, win_ref, w_ref, s_ref, d_ref, dx_ref, dw_ref, ds_ref):
        xv = x_ref[...]
        winv = win_ref[...]
        cnt = _pool_counts(xv.shape, winv)
        pooled = _pick_window(_window_sums(xv, _shift_down), winv) / cnt - xv
        dv = d_ref[...]
        ds_ref[...] = _colsum(dv * _dot(pooled, w_ref[0], NN))
        dy0 = dv * s_ref[...]
        dw_ref[0] = _dot(pooled, dy0, TN)
        dpooled = _dot(dy0, w_ref[0], NT)
        dmean = dpooled / cnt
        dx_ref[...] = (_pick_window(_window_sums(dmean, _shift_up), winv) - dpooled).astype(BF16)

    nb = POOL_DIM // LANE
    vec = pl.BlockSpec((1, LANE), lambda b: (0, b))
    mat = pl.BlockSpec((1, LANE, LANE), lambda b: (b, 0, 0))
    first = A_DIM // LANE
    return _pcall(body, name=name,
                  out_shape=(jax.ShapeDtypeStruct((t, POOL_DIM), BF16), jax.ShapeDtypeStruct((nb, LANE, LANE), F32),
                             jax.ShapeDtypeStruct((1, POOL_DIM), F32)),
                  grid=(nb,),
                  in_specs=[_col(t, lambda b: (0, POOL_COL + b)), vec, mat, vec, _col(t, lambda b: (0, first + b))],
                  out_specs=(_col(t, lambda b: (0, b)), mat, vec), semantics=("parallel",),
                  vmem_limit=VMEM_LIMIT)(proj, win, wbd, scale, dmixed)


def _sconv_fwd(proj, w, *, name):
    t = proj.shape[0]

    def body(cb_ref, cc_ref, ch_ref, w_ref, y_ref):
        y_ref[...] = (cb_ref[...] * _conv_fwd(cc_ref[...] * ch_ref[...], w_ref, CONV_TAPS)).astype(BF16)

    nb = CONV_DIM // LANE
    return _pcall(body, name=name, out_shape=jax.ShapeDtypeStruct((t, CONV_DIM), BF16), grid=(nb,),
                  in_specs=[_col(t, lambda b: (0, CB_COL + b)), _col(t, lambda b: (0, CC_COL + b)),
                            _col(t, lambda b: (0, CH_COL + b)), pl.BlockSpec((CONV_TAPS, LANE), lambda b: (0, b))],
                  out_specs=_col(t, lambda b: (0, b)), semantics=("parallel",), vmem_limit=VMEM_LIMIT)(proj, proj, proj, w)


def _sconv_bwd(proj, w, dmixed, *, name):
    t = proj.shape[0]

    def body(cb_ref, cc_ref, ch_ref, w_ref, d_ref, dcb_ref, dcc_ref, dch_ref, dw_ref):
        cc = cc_ref[...]
        ch = ch_ref[...]
        u = cc * ch
        dv = d_ref[...]
        dcb_ref[...] = (dv * _conv_fwd(u, w_ref, CONV_TAPS)).astype(BF16)
        du = _conv_bwd(u, dv * cb_ref[...], w_ref, dw_ref, CONV_TAPS)
        dcc_ref[...] = (du * ch).astype(BF16)
        dch_ref[...] = (du * cc).astype(BF16)

    nb = CONV_DIM // LANE
    first = (A_DIM + POOL_DIM) // LANE
    act = jax.ShapeDtypeStruct((t, CONV_DIM), BF16)
    wspec = pl.BlockSpec((CONV_TAPS, LANE), lambda b: (0, b))
    ospec = _col(t, lambda b: (0, b))
    return _pcall(body, name=name, out_shape=(act, act, act, jax.ShapeDtypeStruct((CONV_TAPS, CONV_DIM), F32)), grid=(nb,),
                  in_specs=[_col(t, lambda b: (0, CB_COL + b)), _col(t, lambda b: (0, CC_COL + b)),
                            _col(t, lambda b: (0, CH_COL + b)), wspec, _col(t, lambda b: (0, first + b))],
                  out_specs=(ospec, ospec, ospec, wspec), semantics=("parallel",),
                  vmem_limit=VMEM_LIMIT)(proj, proj, proj, w, dmixed)


def _chunk_masks():
    r = lax.broadcasted_iota(jnp.int32, (CHUNK, CHUNK), 0)
    c = lax.broadcasted_iota(jnp.int32, (CHUNK, CHUNK), 1)
    return r >= c, r > c, jnp.where(r == c, 1.0, 0.0).astype(F32)


def _split(a):
    hi = a.astype(BF16)
    return hi, (a - hi.astype(F32)).astype(BF16)


def _dot_split(a, b, dims):
    (ah, al), (bh, bl) = a, b
    return _dot(ah, bh, dims) + _dot(ah, bl, dims) + _dot(al, bh, dims)


def _tri_inv(lows, eye):
    xs = [eye - low for low in lows]
    ps = [_split(low) for low in lows]
    ps = [_split(_dot_split(p, p, NN)) for p in ps]
    for i in range(5):
        xs = [x + _dot_split(_split(x), p, NN) for x, p in zip(xs, ps)]
        if i < 4:
            ps = [_split(_dot_split(p, p, NN)) for p in ps]
    return xs


def _prefix_sum_rows(x):
    for k in range(6):
        x = x + _shift_down(x, 1 << k)
    return x


def _suffix_sum_rows(x):
    for k in range(6):
        x = x + _shift_up(x, 1 << k)
    return x


def _chunk_decay(g, incl):
    gcb = _prefix_sum_rows(g)
    gtot = _colsum(g)
    col = gcb[:, :CHUNK]
    row = gcb.T[:CHUNK, :]
    decay = jnp.exp(jnp.where(incl, col - row, -1e30))
    return gcb, gtot, decay


CHUNKS_PER_STEP = 2


def _heads_of(ref, base, rows):
    return [ref[base + h, rows, :] for h in range(HEADS)]


def _chunk_rows(j):
    return pl.ds(j * CHUNK, CHUNK)


def _deltanet_prep(qkv, g, beta, *, name):
    t = qkv.shape[1]
    n_chunks = t // CHUNK
    per = CHUNKS_PER_STEP
    probs = [(j, h) for j in range(per) for h in range(HEADS)]

    def body(qkv_ref, g_ref, b_ref, u_ref, w_ref, qg_ref, kg_ref, attn_ref, tm_ref):
        incl, strict, eye = _chunk_masks()
        q = [qkv_ref[h, _chunk_rows(j), :] for j, h in probs]
        k = [qkv_ref[HEADS + h, _chunk_rows(j), :] for j, h in probs]
        v = [qkv_ref[2 * HEADS + h, _chunk_rows(j), :] for j, h in probs]
        bv = [b_ref[h, _chunk_rows(j), :] for j, h in probs]
        dec = [_chunk_decay(g_ref[h, _chunk_rows(j), :], incl) for j, h in probs]
        kb = [a * b for a, b in zip(k, bv)]
        low = [jnp.where(strict, _dot(a, b, NT) * d[2], 0.0) for a, b, d in zip(kb, k, dec)]
        tm = _tri_inv(low, eye)
        egc = [jnp.exp(d[0]) for d in dec]
        u = [_dot(m, a * b, NN) for m, a, b in zip(tm, v, bv)]
        w = [_dot(m, a * e, NN) for m, a, e in zip(tm, kb, egc)]
        attn = [_dot(a, b, NT) * d[2] for a, b, d in zip(q, k, dec)]
        for i, (j, h) in enumerate(probs):
            rows = _chunk_rows(j)
            u_ref[h, rows, :] = u[i]
            w_ref[h, rows, :] = w[i].astype(BF16)
            qg_ref[h, rows, :] = (q[i] * egc[i]).astype(BF16)
            kg_ref[h, rows, :] = (k[i] * jnp.exp(dec[i][1] - dec[i][0])).astype(BF16)
            attn_ref[j, h] = attn[i].astype(BF16)
            tm_ref[j, h] = tm[i]

    act = lambda heads: pl.BlockSpec((heads, per * CHUNK, LANE), lambda n: (0, n, 0))
    mat = pl.BlockSpec((per, HEADS, CHUNK, CHUNK), lambda n: (n, 0, 0, 0))
    return _pcall(
        body, name=name,
        out_shape=(jax.ShapeDtypeStruct((HEADS, t, LANE), F32),) + (jax.ShapeDtypeStruct((HEADS, t, LANE), BF16),) * 3
        + (jax.ShapeDtypeStruct((n_chunks, HEADS, CHUNK, CHUNK), BF16), jax.ShapeDtypeStruct((n_chunks, HEADS, CHUNK, CHUNK), F32)),
        grid=(n_chunks // per,), in_specs=[act(3 * HEADS), act(HEADS), act(HEADS)],
        out_specs=(act(HEADS),) * 4 + (mat, mat), semantics=("parallel",), vmem_limit=VMEM_LIMIT)(qkv, g, beta)


SCAN_CHUNKS_PER_STEP = 4


def _deltanet_scan(u, w, qg, kg, attn, g, *, name):
    t = u.shape[1]
    n_chunks = t // CHUNK
    per = SCAN_CHUNKS_PER_STEP

    def body(u_ref, w_ref, qg_ref, kg_ref, attn_ref, g_ref, o_ref, vn_ref, st_ref, s_ref):
        @pl.when(pl.program_id(0) == 0)
        def _():
            s_ref[...] = jnp.zeros_like(s_ref)

        for j in range(per):
            rows = _chunk_rows(j)
            s = [s_ref[h] for h in range(HEADS)]
            vn = [u_ref[h, rows, :] - _dot(w_ref[h, rows, :], s[h], NN) for h in range(HEADS)]
            o = [_dot(qg_ref[h, rows, :], s[h], NN) + _dot(attn_ref[j, h], vn[h], NN) for h in range(HEADS)]
            eg = [jnp.exp(_colsum(g_ref[h, rows, :])) for h in range(HEADS)]
            for h in range(HEADS):
                st_ref[j, h] = s[h]
                s_ref[h] = s[h] * eg[h] + _dot(kg_ref[h, rows, :], vn[h], TN)
                o_ref[h, rows, :] = o[h]
                vn_ref[h, rows, :] = vn[h]

    act = pl.BlockSpec((HEADS, per * CHUNK, LANE), lambda n: (0, n, 0))
    out = jax.ShapeDtypeStruct((HEADS, t, LANE), F32)
    return _pcall(
        body, name=name, out_shape=(out, out, jax.ShapeDtypeStruct((n_chunks, HEADS, LANE, LANE), F32)), grid=(n_chunks // per,),
        in_specs=[act] * 4 + [pl.BlockSpec((per, HEADS, CHUNK, CHUNK), lambda n: (n, 0, 0, 0)), act],
        out_specs=(act, act, pl.BlockSpec((per, HEADS, LANE, LANE), lambda n: (n, 0, 0, 0))),
        scratch_shapes=[pltpu.VMEM((HEADS, LANE, LANE), F32)], semantics=("arbitrary",))(u, w, qg, kg, attn, g)


def _deltanet_bscan(w, qg, kg, attn, g, do, *, name):
    t = w.shape[1]
    n_chunks = t // CHUNK
    per = SCAN_CHUNKS_PER_STEP
    steps = n_chunks // per

    def body(w_ref, qg_ref, kg_ref, attn_ref, g_ref, do_ref, dvn_ref, dsn_ref, ds_ref):
        @pl.when(pl.program_id(0) == 0)
        def _():
            ds_ref[...] = jnp.zeros_like(ds_ref)

        for j in reversed(range(per)):
            rows = _chunk_rows(j)
            dsn = [ds_ref[h] for h in range(HEADS)]
            dov = [do_ref[h, rows, :] for h in range(HEADS)]
            dvn = [_dot(attn_ref[j, h], dov[h], TN) + _dot(kg_ref[h, rows, :], dsn[h], NN) for h in range(HEADS)]
            eg = [jnp.exp(_colsum(g_ref[h, rows, :])) for h in range(HEADS)]
            for h in range(HEADS):
                dsn_ref[j, h] = dsn[h]
                ds_ref[h] = _dot(qg_ref[h, rows, :], dov[h], TN) + eg[h] * dsn[h] - _dot(w_ref[h, rows, :], dvn[h], TN)
                dvn_ref[h, rows, :] = dvn[h]

    act = pl.BlockSpec((HEADS, per * CHUNK, LANE), lambda n: (0, steps - 1 - n, 0))
    return _pcall(
        body, name=name,
        out_shape=(jax.ShapeDtypeStruct((HEADS, t, LANE), F32), jax.ShapeDtypeStruct((n_chunks, HEADS, LANE, LANE), F32)),
        grid=(steps,),
        in_specs=[act] * 3 + [pl.BlockSpec((per, HEADS, CHUNK, CHUNK), lambda n: (steps - 1 - n, 0, 0, 0)), act, act],
        out_specs=(act, pl.BlockSpec((per, HEADS, LANE, LANE), lambda n: (steps - 1 - n, 0, 0, 0))),
        scratch_shapes=[pltpu.VMEM((HEADS, LANE, LANE), F32)], semantics=("arbitrary",))(w, qg, kg, attn, g, do)


def _sum_all(x):
    return jnp.sum(jnp.sum(x, axis=1, keepdims=True), axis=0, keepdims=True)


def _rowsum(x):
    return jnp.sum(x, axis=1, keepdims=True)


def _deltanet_post(qkv, g, beta, tmats, states, dstates, do, dvn, vn, *, name):
    t = qkv.shape[1]
    n_chunks = t // CHUNK
    per = CHUNKS_PER_STEP
    probs = [(j, h) for j in range(per) for h in range(HEADS)]

    def body(qkv_ref, g_ref, b_ref, tm_ref, st_ref, dsn_ref, do_ref, dvn_ref, vn_ref, dqkv_ref, dg_ref, db_ref):
        incl, strict, _ = _chunk_masks()
        ones = jnp.ones((CHUNK, LANE), BF16)
        last_row = lax.broadcasted_iota(jnp.int32, (CHUNK, LANE), 0) == CHUNK - 1
        z = lambda f, *cols: [f(*a) for a in zip(*cols)]
        q = [qkv_ref[h, _chunk_rows(j), :] for j, h in probs]
        k = [qkv_ref[HEADS + h, _chunk_rows(j), :] for j, h in probs]
        v = [qkv_ref[2 * HEADS + h, _chunk_rows(j), :] for j, h in probs]
        bv = [b_ref[h, _chunk_rows(j), :] for j, h in probs]
        dov = [do_ref[h, _chunk_rows(j), :] for j, h in probs]
        dvn_ = [dvn_ref[h, _chunk_rows(j), :] for j, h in probs]
        vn_ = [vn_ref[h, _chunk_rows(j), :] for j, h in probs]
        tm = [tm_ref[j, h] for j, h in probs]
        s = [st_ref[j, h] for j, h in probs]
        dsn = [dsn_ref[j, h] for j, h in probs]
        dec = [_chunk_decay(g_ref[h, _chunk_rows(j), :], incl) for j, h in probs]
        decay = [d[2] for d in dec]
        egc = [jnp.exp(d[0]) for d in dec]
        ekg = [jnp.exp(d[1] - d[0]) for d in dec]
        kb = z(lambda a, b: a * b, k, bv)
        vb = z(lambda a, b: a * b, v, bv)
        kbg = z(lambda a, b: a * b, kb, egc)
        qg = z(lambda a, b: a * b, q, egc)
        kg = z(lambda a, b: a * b, k, ekg)
        kk = z(lambda a, b: _dot(a, b, NT), kb, k)
        qk = z(lambda a, b: _dot(a, b, NT), q, k)
        dattn = z(lambda a, b: jnp.where(incl, _dot(a, b, NT), 0.0), dov, vn_)
        dqg = z(lambda a, b: _dot(a, b, NT), dov, s)
        dkg = z(lambda a, b: _dot(a, b, NT), vn_, dsn)
        dglast = z(lambda a, b, c, d, e: _sum_all(a * b) * jnp.exp(e[1]) + _sum_all(c * d), s, dsn, dkg, kg, dec)
        dw = z(lambda a, b: -_dot(a, b, NT), dvn_, s)
        dtm = z(lambda a, b, c, d: _dot(a, b, NT) + _dot(c, d, NT), dvn_, vb, dw, kbg)
        dvb = z(lambda a, b: _dot(a, b, TN), tm, dvn_)
        dkbg = z(lambda a, b: _dot(a, b, TN), tm, dw)
        dlow = z(lambda a, b: jnp.where(strict, -_dot(_dot(a, b, TN), a, NT), 0.0), tm, dtm)
        dkk = z(lambda a, b: a * b, dlow, decay)
        dqk = z(lambda a, b: a * b, dattn, decay)
        dkb = z(lambda a, b, c, d: _dot(a, b, NN) + c * d, dkk, k, dkbg, egc)
        dk = z(lambda a, b, c, d, e, f, g_, h_: _dot(a, b, TN) + _dot(c, d, TN) + e * f + g_ * h_, dkk, kb, dqk, q, dkg, ekg, dkb, bv)
        dq = z(lambda a, b, c, d: _dot(a, b, NN) + c * d, dqk, k, dqg, egc)
        m = z(lambda a, b, c, d, e: (a * b + c * d) * e, dlow, kk, dattn, qk, decay)
        mcol = [_dot(mh, ones, TN) + _dot(ml, ones, TN) for mh, ml in (_split(a) for a in m)]
        for i, (j, h) in enumerate(probs):
            rows = _chunk_rows(j)
            dqkv_ref[h, rows, :] = dq[i]
            dqkv_ref[HEADS + h, rows, :] = dk[i]
            dqkv_ref[2 * HEADS + h, rows, :] = dvb[i] * bv[i]
            db_ref[h, rows, :] = jnp.broadcast_to(_rowsum(dkb[i] * k[i] + dvb[i] * v[i]), (CHUNK, LANE))
            dgc = (_rowsum(dqg[i] * qg[i] + dkbg[i] * kbg[i] - dkg[i] * kg[i]) + _rowsum(m[i]) - mcol[i]
                   + jnp.where(last_row, dglast[i], 0.0))
            dg_ref[h, rows, :] = _suffix_sum_rows(dgc)

    act = lambda heads: pl.BlockSpec((heads, per * CHUNK, LANE), lambda n: (0, n, 0))
    mat = lambda d: pl.BlockSpec((per, HEADS, d, d), lambda n: (n, 0, 0, 0))
    out = jax.ShapeDtypeStruct((HEADS, t, LANE), F32)
    return _pcall(
        body, name=name, out_shape=(jax.ShapeDtypeStruct((3 * HEADS, t, LANE), F32), out, out), grid=(n_chunks // per,),
        in_specs=[act(3 * HEADS), act(HEADS), act(HEADS), mat(CHUNK), mat(LANE), mat(LANE), act(HEADS), act(HEADS), act(HEADS)],
        out_specs=(act(3 * HEADS), act(HEADS), act(HEADS)), semantics=("parallel",),
        vmem_limit=VMEM_LIMIT)(qkv, g, beta, tmats, states, dstates, do, dvn, vn)


ANY = pl.BlockSpec(memory_space=pl.ANY)
PEERS = N_DEV - 1


def _all_gather(arrays, *, name):
    n = len(arrays)

    def body(*refs):
        ins, outs = refs[:n], refs[n:2 * n]
        send_sems, recv_sems, local_sems = refs[2 * n:]
        x, y, c = lax.axis_index("x"), lax.axis_index("y"), lax.axis_index("c")
        me, sibling = (x, y, c), (x, y, 1 - c)
        chips = [(1 - x, y), (x, 1 - y), (1 - x, 1 - y)]

        def copy(a, k, block, to, src=None):
            dst = outs[a].at[4 * block[0] + 2 * block[1] + block[2]]
            return pltpu.make_async_remote_copy(src_ref=dst if src is None else src, dst_ref=dst, send_sem=send_sems.at[a * PEERS + k],
                                                recv_sem=recv_sems.at[a * PEERS + k], device_id=to, device_id_type=MESH)

        local = [pltpu.make_async_copy(ins[a], outs[a].at[4 * x + 2 * y + c], local_sems.at[a]) for a in range(n)]
        for cp in local:
            cp.start()
        first = []
        for a in range(n):
            first.append(copy(a, 0, me, sibling, src=ins[a]))
            first += [copy(a, 1 + j, me, (*chip, c), src=ins[a]) for j, chip in enumerate(chips)]
        for cp in first:
            cp.start()
        passed = []
        for a in range(n):
            for j, chip in enumerate(chips):
                copy(a, 1 + j, (*chip, c), me).wait_recv()
                fwd = copy(a, 4 + j, (*chip, c), sibling)
                fwd.start()
                passed.append(fwd)
        for a in range(n):
            copy(a, 0, sibling, me).wait_recv()
            for j, chip in enumerate(chips):
                copy(a, 4 + j, (*chip, 1 - c), me).wait_recv()
        for cp in first + passed:
            cp.wait_send()
        for cp in local:
            cp.wait()

    return _pcall(body, name=name, out_shape=tuple(jax.ShapeDtypeStruct((N_DEV,) + a.shape, a.dtype) for a in arrays),
                  in_specs=[ANY] * n, out_specs=(ANY,) * n,
                  scratch_shapes=[pltpu.SemaphoreType.DMA((n * PEERS,)), pltpu.SemaphoreType.DMA((n * PEERS,)),
                                  pltpu.SemaphoreType.DMA((n,))])(*arrays)


def _exchange_blocks(arrays, *, name):
    n = len(arrays)

    def body(*refs):
        ins, outs = refs[:n], refs[n:2 * n]
        send_sems, recv_sems, local_sems = refs[2 * n:]
        x, y, c = lax.axis_index("x"), lax.axis_index("y"), lax.axis_index("c")
        mine = 4 * x + 2 * y + c
        copies = []
        for a in range(n):
            lc = pltpu.make_async_copy(ins[a].at[mine], outs[a].at[mine], local_sems.at[a])
            lc.start()
            copies.append(lc)
            for k in range(1, N_DEV):
                px = 1 - x if k & 4 else x
                py = 1 - y if k & 2 else y
                pc = 1 - c if k & 1 else c
                cp = pltpu.make_async_remote_copy(src_ref=ins[a].at[4 * px + 2 * py + pc], dst_ref=outs[a].at[mine],
                                                  send_sem=send_sems.at[a * PEERS + k - 1], recv_sem=recv_sems.at[a * PEERS + k - 1],
                                                  device_id=(px, py, pc), device_id_type=MESH)
                cp.start()
                copies.append(cp)
        for cp in copies:
            cp.wait()

    return _pcall(body, name=name, out_shape=tuple(jax.ShapeDtypeStruct(a.shape, a.dtype) for a in arrays),
                  in_specs=[ANY] * n, out_specs=(ANY,) * n,
                  scratch_shapes=[pltpu.SemaphoreType.DMA((n * PEERS,)), pltpu.SemaphoreType.DMA((n * PEERS,)),
                                  pltpu.SemaphoreType.DMA((n,))])(*arrays)


def _adamw_reduce(w, parts, m, v, *, name):
    layers, r, c = w.shape
    assert len(parts) == layers
    tr = _tile(r, 512, 16)
    tiles = r // tr
    bc1 = 1.0 - ADAM_B1 ** ADAM_STEP
    bc2 = 1.0 - ADAM_B2 ** ADAM_STEP

    def body(w_ref, *rest):
        p_refs = rest[:layers]
        m_ref, v_ref, g_ref, d_ref, nm_ref, nv_ref = rest[layers:]

        def update(p_ref):
            g = p_ref[0, :, pl.ds(0, c)].astype(F32)
            for s in range(1, N_DEV):
                g = g + p_ref[s, :, pl.ds(0, c)].astype(F32)
            nm = ADAM_B1 * m_ref[0] + (1.0 - ADAM_B1) * g
            nv = ADAM_B2 * v_ref[0] + (1.0 - ADAM_B2) * (g * g)
            g_ref[0] = g
            nm_ref[0] = nm
            nv_ref[0] = nv
            d_ref[0] = -ADAM_LR * ((nm / bc1) / (jnp.sqrt(nv / bc2) + ADAM_EPS) + ADAM_WD * w_ref[0])

        for layer in range(layers):
            pl.when(pl.program_id(0) == layer)(functools.partial(update, p_refs[layer]))

    def part_spec(layer, shape):
        rest = 0 if layer > 0 else tiles - 1
        return pl.BlockSpec((N_DEV, tr, shape[2]), lambda l, i: (0, jnp.where(l == layer, i, rest), 0))

    spec = pl.BlockSpec((1, tr, c), lambda l, i: (l, i, 0))
    out = jax.ShapeDtypeStruct((layers, r, c), F32)
    return _pcall(body, name=name, out_shape=(out,) * 4, grid=(layers, tiles),
                  in_specs=[spec] + [part_spec(layer, p.shape) for layer, p in enumerate(parts)] + [spec, spec],
                  out_specs=(spec,) * 4, semantics=("arbitrary", "arbitrary"), vmem_limit=VMEM_LIMIT)(w, *parts, m, v)


def _pool_windows():
    return jnp.repeat(jnp.asarray(POOL_WINDOWS, F32), POOL_DIM // len(POOL_WINDOWS))[None, :]


def _block_diag_pairs(pool_w):
    z = jnp.zeros_like(pool_w[0])
    return jnp.stack([jnp.block([[pool_w[2 * b], z], [z, pool_w[2 * b + 1]]]) for b in range(2)])


def _pad_lanes(vec):
    return jnp.zeros((1, LANE), F32).at[0, :vec.shape[0]].set(vec)


FF_SHARD = D_FF // N_DEV
FF_BLOCK = 384
D_FF_PAD = N_DEV * FF_BLOCK


def _layer_fwd(x, p_i, wt):
    h1 = _rmsnorm_fwd(x, wt["norm1_g"], name="rmsnorm_fwd")
    proj = _matmul(h1, wt["w_in"], "nn", name="mm_in")
    qkv = _qkv_prep_fwd(proj, wt["conv_qkv"], name="qkv_prep_fwd")
    g, beta = _gates_fwd(proj, wt["a_log"], wt["dt_bias"], name="gates_fwd")
    u, w, qg, kg, attn, tmats = _deltanet_prep(qkv, g, beta, name="deltanet_prep")
    o, vn, states = _deltanet_scan(u, w, qg, kg, attn, g, name="deltanet_scan")
    o_a = _apost_fwd(o, proj, wt["onorm_g"], name="apost_fwd")
    o_b = _pool_fwd(proj, wt["pool_win"], wt["pool_wbd"], wt["pool_scale"], name="pool_fwd")
    o_c = _sconv_fwd(proj, wt["sconv_w"], name="sconv_fwd")
    mixed = jnp.concatenate([o_a, o_b, o_c], axis=1)
    x1 = _matmul(mixed, wt["w_out"], "nn", res=x, name="mm_out")
    h2 = _rmsnorm_fwd(x1, wt["norm2_g"], name="rmsnorm_fwd")
    gate = _matmul(h2, wt["w_gate"], "nn", b_blocked=True, name="mm_gate")
    up = _matmul(h2, wt["w_up"], "nn", b_blocked=True, name="mm_up")
    ff = _swiglu_fwd(gate, up, name="swiglu_fwd")
    x2 = _matmul(ff, wt["w_down"], "nn", res=x1, name="mm_down")
    pgl = _matmul(x2, wt["ple_gate"], "nn", name="mm_pleg")
    pp = _matmul(p_i, wt["ple_proj"], "nn", b_blocked=True, name="mm_plep")
    x3 = _ple_fwd(x2, pgl, pp, name="ple_fwd")
    saved = dict(x=x, h1=h1, proj=proj, qkv=qkv, g=g, beta=beta, o=o, states=states, tmats=tmats, mixed=mixed, x1=x1, h2=h2,
                 gate=gate, up=up, ff=ff, x2=x2, pgl=pgl, pp=pp, p=p_i, w=w, qg=qg, kg=kg, attn=attn, vn=vn)
    return x3, saved


def _col_blocks(g):
    a = g.shape[0]
    return jnp.transpose(g.reshape(a, N_DEV, -1), (1, 0, 2))


def _cols_joined(blocks):
    return jnp.transpose(blocks, (1, 0, 2)).reshape(blocks.shape[1], -1)


def _layer_bwd(dx3, sv, wt):
    gr, big = {}, {}
    rows = D_MODEL // N_DEV
    dpgl, dpp = _ple_bwd(dx3, sv["pgl"], sv["pp"], name="ple_bwd")
    big["ple_proj"] = _matmul(sv["p"], dpp, "tn", out_blocked=(N_DEV, rows), out_dtype=BF16, name="mm_dplep")
    big["ple_gate"] = _matmul(sv["x2"], dpgl, "tn", out_dtype=BF16, name="mm_dpleg").reshape(N_DEV, rows, D_MODEL)
    dx2 = _matmul(dpgl, wt["ple_gate"], "nt", res=dx3, name="mm_dx2")
    big["w_down"] = _matmul(sv["ff"], dx2, "tn", out_dtype=BF16, name="mm_ddown").reshape(N_DEV, FF_BLOCK, D_MODEL)
    dff = _matmul(dx2, wt["w_down"], "nt", name="mm_dff")
    dgate, dup = _swiglu_bwd(sv["gate"], sv["up"], dff, name="swiglu_bwd")
    big["w_gate"] = _matmul(sv["h2"], dgate, "tn", out_blocked=(N_DEV, FF_BLOCK), out_dtype=BF16, name="mm_dgate")
    big["w_up"] = _matmul(sv["h2"], dup, "tn", out_blocked=(N_DEV, FF_BLOCK), out_dtype=BF16, name="mm_dup")
    dh2 = _matmul(dgate, wt["w_gate"], "nt", b_blocked=True, name="mm_dh2_gate")
    dh2 = _matmul(dup, wt["w_up"], "nt", b_blocked=True, res=dh2, name="mm_dh2_up")
    dx1, gr["norm2_g"] = _rmsnorm_bwd(sv["x1"], wt["norm2_g"], dh2, dx2, name="rmsnorm_bwd")
    big["w_out"] = _matmul(sv["mixed"], dx1, "tn", out_dtype=BF16, name="mm_dout").reshape(N_DEV, rows, D_MODEL)
    dmixed = _matmul(dx1, wt["w_out"], "nt", name="mm_dmixed")
    proj = sv["proj"]
    dcb, dcc, dch, dsconv = _sconv_bwd(proj, wt["sconv_w"], dmixed, name="sconv_bwd")
    big["sconv_w"] = _col_blocks(dsconv)
    dhp, dwbd, gr["pool_scale"] = _pool_bwd(proj, wt["pool_win"], wt["pool_wbd"], wt["pool_scale"], dmixed, name="pool_bwd")
    half = LANE // 2
    gr["pool_w"] = jnp.stack([dwbd[0, :half, :half], dwbd[0, half:, half:], dwbd[1, :half, :half], dwbd[1, half:, half:]])
    do, dz, gr["onorm_g"] = _apost_bwd(sv["o"], proj, wt["onorm_g"], dmixed, name="apost_bwd")
    dvn, dstates = _deltanet_bscan(sv["w"], sv["qg"], sv["kg"], sv["attn"], sv["g"], do, name="deltanet_bscan")
    dqkv_h, dg, dbeta = _deltanet_post(sv["qkv"], sv["g"], sv["beta"], sv["tmats"], sv["states"], dstates, do, dvn, sv["vn"],
                                       name="deltanet_post")
    dab, dalog, ddtb = _gates_bwd(proj, wt["a_log"], wt["dt_bias"], dg, dbeta, name="gates_bwd")
    gr["a_log"], gr["dt_bias"] = dalog[0, :HEADS], ddtb[0, :HEADS]
    dqkv, dconv = _qkv_prep_bwd(proj, wt["conv_qkv"], dqkv_h, name="qkv_prep_bwd")
    big["conv_qkv"] = _col_blocks(dconv)
    dproj = jnp.concatenate([dqkv, dz, dab, dhp, dcb, dcc, dch], axis=1)
    dwin = _matmul(sv["h1"], dproj, "tn", out_dtype=BF16, name="mm_din")
    big["w_in"] = _col_blocks(jnp.concatenate([dwin[:, :AB_COL + 2 * HEADS], dwin[:, AB_COL + LANE:]], axis=1))
    dh1 = _matmul(dproj, wt["w_in"], "nt", name="mm_dh1")
    dx, gr["norm1_g"] = _rmsnorm_bwd(sv["x"], wt["norm1_g"], dh1, dx1, name="rmsnorm_bwd")
    return dx, big, gr


def _layer_weights(gathered, w, i):
    w_in = _cols_joined(gathered["w_in"])
    return dict(
        norm1_g=w["norm1_g"][i][None], norm2_g=w["norm2_g"][i][None], onorm_g=w["onorm_g"][i][None],
        a_log=_pad_lanes(w["a_log"][i]), dt_bias=_pad_lanes(w["dt_bias"][i]),
        pool_scale=w["pool_scale"][i][None], pool_win=_pool_windows(), pool_wbd=_block_diag_pairs(w["pool_w"][i]),
        conv_qkv=_cols_joined(gathered["conv_qkv"]), sconv_w=_cols_joined(gathered["sconv_w"]),
        w_in=jnp.concatenate([w_in[:, :AB_COL + 2 * HEADS], jnp.zeros((D_MODEL, LANE - 2 * HEADS), BF16),
                              w_in[:, AB_COL + 2 * HEADS:]], axis=1),
        w_gate=gathered["w_gate"], w_up=gathered["w_up"], w_down=gathered["w_down"].reshape(D_FF_PAD, D_MODEL),
        w_out=gathered["w_out"].reshape(D_MODEL, D_MODEL), ple_gate=gathered["ple_gate"].reshape(D_MODEL, D_MODEL),
        ple_proj=gathered["ple_proj"])


def _local_step(x, p, target, layers, final_g):
    saved = []
    h = x
    for i in range(DEPTH):
        h, sv = _layer_fwd(h, p[i], layers[i])
        saved.append(sv)
    dx, dgf, loss = _loss_head(h, final_g, target, name="loss_head")
    big, small = [None] * DEPTH, [None] * DEPTH
    for i in reversed(range(DEPTH)):
        dx, big[i], small[i] = _layer_bwd(dx, saved[i], layers[i])
    return loss, dx, big, small, dgf


SHARDED = ("w_in", "w_gate", "w_up", "w_down", "w_out", "ple_gate", "ple_proj", "conv_qkv", "sconv_w")
SMALL = ("norm1_g", "a_log", "dt_bias", "onorm_g", "pool_w", "pool_scale", "norm2_g", "final_g")
SLAB_COLS = 1024


def _payload(name, shard):
    if name in ("conv_qkv", "sconv_w"):
        return shard
    out = shard.astype(BF16)
    if name in ("w_gate", "w_up"):
        out = jnp.pad(out, ((0, 0), (0, FF_BLOCK - FF_SHARD)))
    if name == "w_down":
        out = jnp.pad(out, ((0, FF_BLOCK - FF_SHARD), (0, 0)))
    return out


def _slab_rows(shape):
    size = 1
    for s in shape:
        size *= s
    return SUBLANE * -(-size // (SUBLANE * SLAB_COLS))


def _pack_slab(parts, extra_row):
    rows = []
    for name in SMALL:
        flat = parts[name].reshape(-1)
        nrow = _slab_rows(parts[name].shape)
        rows.append(jnp.pad(flat, (0, nrow * SLAB_COLS - flat.shape[0])).reshape(nrow, SLAB_COLS))
    rows.append(jnp.pad(extra_row, ((0, SUBLANE - 1), (0, 0))))
    return jnp.concatenate(rows, axis=0)


def _unpack_slab(slab, shapes):
    out, row = {}, 0
    for name in SMALL:
        size = 1
        for s in shapes[name]:
            size *= s
        out[name] = slab[row:row + _slab_rows(shapes[name])].reshape(-1)[:size].reshape(shapes[name])
        row += _slab_rows(shapes[name])
    return out, row


def kernel(x, p, norm1_g, w_in, conv_qkv, a_log, dt_bias, onorm_g, pool_w, pool_scale, sconv_w, w_out, norm2_g, w_gate, w_up, w_down, ple_proj, ple_gate, final_g, loss_target, m_norm1_g, m_w_in, m_conv_qkv, m_a_log, m_dt_bias, m_onorm_g, m_pool_w, m_pool_scale, m_sconv_w, m_w_out, m_norm2_g, m_w_gate, m_w_up, m_w_down, m_ple_proj, m_ple_gate, m_final_g, v_norm1_g, v_w_in, v_conv_qkv, v_a_log, v_dt_bias, v_onorm_g, v_pool_w, v_pool_scale, v_sconv_w, v_w_out, v_norm2_g, v_w_gate, v_w_up, v_w_down, v_ple_proj, v_ple_gate, v_final_g):
    names = ["norm1_g", "w_in", "conv_qkv", "a_log", "dt_bias", "onorm_g", "pool_w", "pool_scale", "sconv_w", "w_out", "norm2_g",
             "w_gate", "w_up", "w_down", "ple_proj", "ple_gate", "final_g"]
    w = dict(zip(names, [norm1_g, w_in, conv_qkv, a_log, dt_bias, onorm_g, pool_w, pool_scale, sconv_w, w_out, norm2_g, w_gate, w_up,
                         w_down, ple_proj, ple_gate, final_g]))
    m = dict(zip(names, [m_norm1_g, m_w_in, m_conv_qkv, m_a_log, m_dt_bias, m_onorm_g, m_pool_w, m_pool_scale, m_sconv_w, m_w_out,
                         m_norm2_g, m_w_gate, m_w_up, m_w_down, m_ple_proj, m_ple_gate, m_final_g]))
    v = dict(zip(names, [v_norm1_g, v_w_in, v_conv_qkv, v_a_log, v_dt_bias, v_onorm_g, v_pool_w, v_pool_scale, v_sconv_w, v_w_out,
                         v_norm2_g, v_w_gate, v_w_up, v_w_down, v_ple_proj, v_ple_gate, v_final_g]))

    layers = []
    for i in range(DEPTH):
        gathered = _all_gather([_payload(k, w[k][i]) for k in SHARDED], name="all_gather_weights")
        layers.append(_layer_weights(dict(zip(SHARDED, gathered)), w, i))

    loss_part, dx, big, small, dgf = _local_step(x[0], p[:, 0], loss_target[0], layers, final_g[None])

    received = [_exchange_blocks([big[i][k] for k in SHARDED], name="exchange_grad_blocks") for i in range(DEPTH)]
    grads = {k: jnp.stack([small[i][k] for i in range(DEPTH)]) for k in small[0]}
    grads = {k: g[:, 0] if k in ("norm1_g", "norm2_g", "onorm_g", "pool_scale") else g for k, g in grads.items()}
    grads["final_g"] = dgf[0]
    loss_row = jnp.pad(loss_part, ((0, 0), (0, SLAB_COLS - LANE)))
    (small_parts,) = _all_gather([_pack_slab(grads, loss_row)], name="all_gather_small_grads")

    out_g, out_d, out_m, out_v = {}, {}, {}, {}
    for j, k in enumerate(SHARDED):
        out_g[k], out_d[k], out_m[k], out_v[k] = _adamw_reduce(w[k], [received[i][j] for i in range(DEPTH)], m[k], v[k],
                                                                name="adamw_" + k)
    zero_row = jnp.zeros((1, SLAB_COLS), F32)
    slabs = _adamw_reduce(_pack_slab(w, zero_row)[None], [small_parts], _pack_slab(m, zero_row)[None],
                          _pack_slab(v, zero_row)[None], name="adamw_small")
    slabs = [s[0] for s in slabs]
    shapes = {k: w[k].shape for k in SMALL}
    for dst, slab in zip((out_g, out_d, out_m, out_v), slabs):
        vals, _ = _unpack_slab(slab, shapes)
        dst.update(vals)
    _, loss_at = _unpack_slab(slabs[0], shapes)
    loss = slabs[0][loss_at, 0]

    return (loss, dx[None], *[out_g[k] for k in names], *[out_d[k] for k in names], *[out_m[k] for k in names],
            *[out_v[k] for k in names])
```

```python
import functools

import jax
import jax.numpy as jnp
from jax import lax
from jax.experimental import pallas as pl
from jax.experimental.pallas import tpu as pltpu

F32 = jnp.float32
BF16 = jnp.bfloat16

D_MODEL = 1024
DEPTH = 2
PLE_DIM = 256
EPS = 1e-6
HEAD_DIM = 128
HEADS = 4
A_DIM = HEADS * HEAD_DIM
QKV_TAPS = 4
CHUNK = 64
POOL_WINDOWS = (2, 4, 8, 16)
POOL_DIM = 256
CONV_DIM = 256
CONV_TAPS = 3
D_FF = 2816
D_IN = 3080
D_IN_PAD = 3200
AB_COL = 2048
N_DEV = 8

ADAM_LR = 0.001
ADAM_B1 = 0.9
ADAM_B2 = 0.999
ADAM_EPS = 1e-08
ADAM_WD = 0.01
ADAM_STEP = 10

LANE = 128
SUBLANE = 8
VMEM_BYTES_V7X = 64 * 1024 * 1024
VMEM_LIMIT = 48 * 1024 * 1024

_HI = lax.Precision.HIGHEST
NN = ((1,), (0,))
NT = ((1,), (1,))
TN = ((0,), (0,))
MESH = pl.DeviceIdType.MESH


def _dot(a, b, dims, hi=False):
    if hi:
        return lax.dot_general(a, b, (dims, ((), ())), precision=_HI, preferred_element_type=F32)
    return lax.dot_general(a.astype(BF16), b.astype(BF16), (dims, ((), ())), preferred_element_type=F32)


def _pcall(body, *, name, out_shape, grid=(), in_specs=None, out_specs=None, scratch_shapes=(), semantics=None,
           vmem_limit=None, **kw):
    params = {}
    if semantics is not None:
        params["dimension_semantics"] = semantics
    if vmem_limit is not None:
        params["vmem_limit_bytes"] = vmem_limit
    return pl.pallas_call(
        body, name=name, out_shape=out_shape, grid=grid, in_specs=in_specs, out_specs=out_specs,
        scratch_shapes=list(scratch_shapes), compiler_params=pltpu.CompilerParams(**params), **kw)


def _sigmoid(x):
    return 1.0 / (1.0 + jnp.exp(-x))


def _softplus(x):
    return jnp.maximum(x, 0.0) + jnp.log(1.0 + jnp.exp(-jnp.abs(x)))


def _tile(n, cap, mult):
    if n <= cap:
        return n
    best = None
    for t in range(mult, cap + 1, mult):
        if n % t == 0:
            best = t
    assert best is not None, (n, cap, mult)
    return best


ROWS_PER_STEP = 512
COLS_PER_DOT = 640


def _matmul_rows(a, b, mode, *, name, res=None, out_dtype=F32, b_blocked=False):
    m, k = a.shape
    if b_blocked:
        nb, _, bw = b.shape
        n = nb * bw if mode == "nn" else b.shape[1]
    else:
        n = b.shape[1] if mode == "nn" else b.shape[0]
    tm = _tile(m, ROWS_PER_STEP, 16)
    cn = bw if (b_blocked and mode == "nn") else _tile(n, COLS_PER_DOT, LANE)
    has_res = res is not None

    def body(*refs):
        a_ref, b_ref = refs[0], refs[1]
        res_ref = refs[2] if has_res else None
        o_ref = refs[2 + has_res]
        if not (b_blocked and mode == "nt"):
            av = a_ref[...].astype(BF16)
        for j in range(n // cn):
            cols = pl.ds(j * cn, cn)
            if mode == "nn":
                part = _dot(av, b_ref[j] if b_blocked else b_ref[:, cols], NN)
            elif not b_blocked:
                part = _dot(av, b_ref[cols, :], NT)
            else:
                part = None
                for s in range(nb):
                    term = _dot(a_ref[:, pl.ds(s * bw, bw)], b_ref[s, cols, :], NT)
                    part = term if part is None else part + term
            if has_res:
                part = part + res_ref[:, cols]
            o_ref[:, cols] = part.astype(o_ref.dtype)

    row = lambda width: pl.BlockSpec((tm, width), lambda i: (i, 0))
    whole = pl.BlockSpec(b.shape, lambda i: (0,) * b.ndim)
    ins = [a, b] + ([res] if has_res else [])
    specs = [row(k), whole] + ([row(n)] if has_res else [])
    return _pcall(body, name=name, out_shape=jax.ShapeDtypeStruct((m, n), out_dtype), grid=(m // tm,), in_specs=specs,
                  out_specs=row(n), semantics=("parallel",), vmem_limit=VMEM_LIMIT)(*ins)


def _matmul(a, b, mode, *, name, res=None, out_dtype=F32, b_blocked=False, out_blocked=None):
    if mode != "tn":
        return _matmul_rows(a, b, mode, name=name, res=res, out_dtype=out_dtype, b_blocked=b_blocked)
    assert res is None and not b_blocked
    (t, m), (t2, n) = a.shape, b.shape
    assert t == t2, (a.shape, b.shape)
    tm = _tile(m, 1024, LANE)
    tn = _tile(n, COLS_PER_DOT, LANE)
    if out_blocked is not None:
        assert out_blocked[0] * out_blocked[1] == n
        tn = out_blocked[1]

    def body(a_ref, b_ref, o_ref):
        part = _dot(a_ref[...], b_ref[...], TN).astype(o_ref.dtype)
        if out_blocked is None:
            o_ref[...] = part
        else:
            o_ref[0] = part

    o_spec = (pl.BlockSpec((tm, tn), lambda i, j: (i, j)) if out_blocked is None
              else pl.BlockSpec((1, tm, tn), lambda i, j: (j, i, 0)))
    o_shape = (m, n) if out_blocked is None else (out_blocked[0], m, out_blocked[1])
    return _pcall(body, name=name, out_shape=jax.ShapeDtypeStruct(o_shape, out_dtype), grid=(m // tm, n // tn),
                  in_specs=[pl.BlockSpec((t, tm), lambda i, j: (0, i)), pl.BlockSpec((t, tn), lambda i, j: (0, j))],
                  out_specs=o_spec, semantics=("parallel", "parallel"), vmem_limit=VMEM_LIMIT)(a, b)


ROW_TILE = 256


def _rows(t, width, idx=0):
    return pl.BlockSpec((ROW_TILE, width), lambda i: (i, idx))


def _vec(width):
    return pl.BlockSpec((1, width), lambda i: (0, 0))


def _rmsnorm_fwd(x, g, *, name):
    t, d = x.shape

    def body(x_ref, g_ref, h_ref):
        xv = x_ref[...]
        r = lax.rsqrt(jnp.mean(xv * xv, axis=-1, keepdims=True) + EPS)
        h_ref[...] = (xv * r * g_ref[...]).astype(BF16)

    return _pcall(body, name=name, out_shape=jax.ShapeDtypeStruct((t, d), BF16), grid=(t // ROW_TILE,),
                  in_specs=[_rows(t, d), _vec(d)], out_specs=_rows(t, d), semantics=("parallel",))(x, g)


def _rmsnorm_bwd(x, g, dh, dres, *, name):
    t, d = x.shape

    def body(x_ref, g_ref, dh_ref, dres_ref, dx_ref, dg_ref):
        xv = x_ref[...]
        r = lax.rsqrt(jnp.mean(xv * xv, axis=-1, keepdims=True) + EPS)
        xhat = xv * r
        dhv = dh_ref[...].astype(F32)
        dhg = dhv * g_ref[...]
        dx_ref[...] = dres_ref[...] + r * (dhg - xhat * jnp.mean(dhg * xhat, axis=-1, keepdims=True))
        part = jnp.sum(dhv * xhat, axis=0, keepdims=True)

        @pl.when(pl.program_id(0) == 0)
        def _():
            dg_ref[...] = part

        @pl.when(pl.program_id(0) > 0)
        def _():
            dg_ref[...] += part

    return _pcall(body, name=name, out_shape=(jax.ShapeDtypeStruct((t, d), F32), jax.ShapeDtypeStruct((1, d), F32)),
                  grid=(t // ROW_TILE,), in_specs=[_rows(t, d), _vec(d), _rows(t, d), _rows(t, d)],
                  out_specs=(_rows(t, d), _vec(d)), semantics=("arbitrary",))(x, g, dh, dres)


def _swiglu_fwd(gate, up, *, name):
    t, f = gate.shape

    def body(gate_ref, up_ref, ff_ref):
        gv = gate_ref[...]
        ff_ref[...] = (gv * _sigmoid(gv) * up_ref[...]).astype(BF16)

    return _pcall(body, name=name, out_shape=jax.ShapeDtypeStruct((t, f), BF16), grid=(t // ROW_TILE,),
                  in_specs=[_rows(t, f), _rows(t, f)], out_specs=_rows(t, f), semantics=("parallel",))(gate, up)


def _swiglu_bwd(gate, up, dff, *, name):
    t, f = gate.shape

    def body(gate_ref, up_ref, dff_ref, dgate_ref, dup_ref):
        gv = gate_ref[...]
        sig = _sigmoid(gv)
        dffv = dff_ref[...]
        dgate_ref[...] = (dffv * up_ref[...] * sig * (1.0 + gv * (1.0 - sig))).astype(BF16)
        dup_ref[...] = (dffv * gv * sig).astype(BF16)

    out = jax.ShapeDtypeStruct((t, f), BF16)
    return _pcall(body, name=name, out_shape=(out, out), grid=(t // ROW_TILE,), in_specs=[_rows(t, f)] * 3,
                  out_specs=(_rows(t, f),) * 2, semantics=("parallel",))(gate, up, dff)


def _ple_fwd(x2, pgl, pp, *, name):
    t, d = x2.shape

    def body(x_ref, pgl_ref, pp_ref, o_ref):
        o_ref[...] = x_ref[...] + _sigmoid(pgl_ref[...]) * pp_ref[...]

    return _pcall(body, name=name, out_shape=jax.ShapeDtypeStruct((t, d), F32), grid=(t // ROW_TILE,),
                  in_specs=[_rows(t, d)] * 3, out_specs=_rows(t, d), semantics=("parallel",))(x2, pgl, pp)


def _ple_bwd(dx3, pgl, pp, *, name):
    t, d = dx3.shape

    def body(dx_ref, pgl_ref, pp_ref, dpgl_ref, dpp_ref):
        dxv = dx_ref[...]
        sig = _sigmoid(pgl_ref[...])
        dpp_ref[...] = (dxv * sig).astype(BF16)
        dpgl_ref[...] = (dxv * pp_ref[...] * sig * (1.0 - sig)).astype(BF16)

    return _pcall(body, name=name, out_shape=(jax.ShapeDtypeStruct((t, d), BF16),) * 2, grid=(t // ROW_TILE,),
                  in_specs=[_rows(t, d)] * 3, out_specs=(_rows(t, d),) * 2, semantics=("parallel",))(dx3, pgl, pp)


def _loss_head(x3, g, target, *, name):
    t, d = x3.shape

    def body(x_ref, g_ref, t_ref, dx_ref, dg_ref, loss_ref):
        xv = x_ref[...]
        r = lax.rsqrt(jnp.mean(xv * xv, axis=-1, keepdims=True) + EPS)
        xhat = xv * r
        gv = g_ref[...]
        err = xhat * gv - t_ref[...]
        row_loss = jnp.sum(err * err, axis=-1, keepdims=True) * (0.5 / d)
        lpart = jnp.broadcast_to(jnp.sum(row_loss, axis=0, keepdims=True), (1, LANE))
        dy = err * (1.0 / d)
        dyg = dy * gv
        dx_ref[...] = r * (dyg - xhat * jnp.mean(dyg * xhat, axis=-1, keepdims=True))
        gpart = jnp.sum(dy * xhat, axis=0, keepdims=True)

        @pl.when(pl.program_id(0) == 0)
        def _():
            dg_ref[...] = gpart
            loss_ref[...] = lpart

        @pl.when(pl.program_id(0) > 0)
        def _():
            dg_ref[...] += gpart
            loss_ref[...] += lpart

    return _pcall(body, name=name,
                  out_shape=(jax.ShapeDtypeStruct((t, d), F32), jax.ShapeDtypeStruct((1, d), F32), jax.ShapeDtypeStruct((1, LANE), F32)),
                  grid=(t // ROW_TILE,), in_specs=[_rows(t, d), _vec(d), _rows(t, d)],
                  out_specs=(_rows(t, d), _vec(d), _vec(LANE)), semantics=("arbitrary",))(x3, g, target)


def _shift_down(x, d):
    if d == 0:
        return x
    row = lax.broadcasted_iota(jnp.int32, x.shape, 0)
    return jnp.where(row >= d, pltpu.roll(x, d, 0), 0.0)


def _shift_up(x, d):
    if d == 0:
        return x
    t = x.shape[0]
    row = lax.broadcasted_iota(jnp.int32, x.shape, 0)
    return jnp.where(row < t - d, pltpu.roll(x, t - d, 0), 0.0)


def _colsum(x):
    return jnp.sum(x, axis=0, keepdims=True)


def _col(t, idx_fn):
    return pl.BlockSpec((t, LANE), idx_fn)


def _conv_fwd(x, w_ref, taps):
    acc = None
    for j in range(taps):
        term = w_ref[pl.ds(j, 1), :] * _shift_down(x, taps - 1 - j)
        acc = term if acc is None else acc + term
    return acc


def _conv_bwd(x, dy, w_ref, dw_ref, taps):
    dx = None
    for j in range(taps):
        term = w_ref[pl.ds(j, 1), :] * _shift_up(dy, taps - 1 - j)
        dx = term if dx is None else dx + term
        dw_ref[pl.ds(j, 1), :] = _colsum(dy * _shift_down(x, taps - 1 - j))
    return dx


def _qkv_prep_fwd(proj, conv_w, *, name):
    t = proj.shape[0]
    scale = HEAD_DIM ** -0.5

    def body(x_ref, w_ref, o_ref):
        j = pl.program_id(0)
        c = _conv_fwd(x_ref[...], w_ref, QKV_TAPS)
        s = c * _sigmoid(c)
        r = lax.rsqrt(jnp.sum(s * s, axis=-1, keepdims=True) + EPS)
        f = jnp.where(j < 2 * HEADS, r, 1.0) * jnp.where(j < HEADS, scale, 1.0)
        o_ref[0] = s * f

    return _pcall(body, name=name, out_shape=jax.ShapeDtypeStruct((3 * HEADS, t, LANE), F32), grid=(3 * HEADS,),
                  in_specs=[_col(t, lambda j: (0, j)), pl.BlockSpec((QKV_TAPS, LANE), lambda j: (0, j))],
                  out_specs=pl.BlockSpec((1, t, LANE), lambda j: (j, 0, 0)), semantics=("parallel",),
                  vmem_limit=VMEM_LIMIT)(proj, conv_w)


def _qkv_prep_bwd(proj, conv_w, dqkv, *, name):
    t = proj.shape[0]
    scale = HEAD_DIM ** -0.5

    def body(x_ref, w_ref, d_ref, dx_ref, dw_ref):
        j = pl.program_id(0)
        xv = x_ref[...]
        c = _conv_fwd(xv, w_ref, QKV_TAPS)
        sig = _sigmoid(c)
        s = c * sig
        r = lax.rsqrt(jnp.sum(s * s, axis=-1, keepdims=True) + EPS)
        n0 = s * r
        dv = d_ref[0]
        dn0 = dv * jnp.where(j < HEADS, scale, 1.0)
        ds_norm = r * (dn0 - n0 * jnp.sum(dn0 * n0, axis=-1, keepdims=True))
        ds = jnp.where(j < 2 * HEADS, ds_norm, dv)
        dc = ds * sig * (1.0 + c * (1.0 - sig))
        dx_ref[...] = _conv_bwd(xv, dc, w_ref, dw_ref, QKV_TAPS).astype(BF16)

    return _pcall(body, name=name,
                  out_shape=(jax.ShapeDtypeStruct((t, 3 * A_DIM), BF16), jax.ShapeDtypeStruct((QKV_TAPS, 3 * A_DIM), F32)),
                  grid=(3 * HEADS,),
                  in_specs=[_col(t, lambda j: (0, j)), pl.BlockSpec((QKV_TAPS, LANE), lambda j: (0, j)),
                            pl.BlockSpec((1, t, LANE), lambda j: (j, 0, 0))],
                  out_specs=(_col(t, lambda j: (0, j)), pl.BlockSpec((QKV_TAPS, LANE), lambda j: (0, j))),
                  semantics=("parallel",), vmem_limit=VMEM_LIMIT)(proj, conv_w, dqkv)


def _lane_pick(x, lane_idx, lane):
    return jnp.broadcast_to(jnp.sum(jnp.where(lane == lane_idx, x, 0.0), axis=-1, keepdims=True), x.shape)


def _gates_fwd(proj, alog, dtb, *, name):
    t = proj.shape[0]

    def body(x_ref, alog_ref, dtb_ref, g_ref, b_ref):
        xv = x_ref[...]
        lane = lax.broadcasted_iota(jnp.int32, xv.shape, 1)
        gall = -jnp.exp(alog_ref[...]) * _softplus(xv + dtb_ref[...])
        ball = _sigmoid(xv)
        for h in range(HEADS):
            g_ref[h] = _lane_pick(gall, h, lane)
            b_ref[h] = _lane_pick(ball, HEADS + h, lane)

    out = jax.ShapeDtypeStruct((HEADS, t, LANE), F32)
    whole = pl.BlockSpec((HEADS, t, LANE), lambda i: (0, 0, 0))
    return _pcall(body, name=name, out_shape=(out, out), grid=(1,),
                  in_specs=[_col(t, lambda i: (0, AB_COL // LANE)), _vec(LANE), _vec(LANE)], out_specs=(whole, whole),
                  semantics=("arbitrary",), vmem_limit=VMEM_LIMIT)(proj, alog, dtb)


def _gates_bwd(proj, alog, dtb, dg, dbeta, *, name):
    t = proj.shape[0]

    def body(x_ref, alog_ref, dtb_ref, dg_ref, db_ref, dab_ref, dalog_ref, ddtb_ref):
        xv = x_ref[...]
        lane = lax.broadcasted_iota(jnp.int32, xv.shape, 1)
        lane1 = lax.broadcasted_iota(jnp.int32, (1, LANE), 1)
        z = xv + dtb_ref[...]
        nea = -jnp.exp(alog_ref[...])
        da_f = nea * _sigmoid(z)
        g_f = nea * _softplus(z)
        ball = _sigmoid(xv)
        db_f = ball * (1.0 - ball)
        dab = jnp.zeros_like(xv)
        dalog = jnp.zeros((1, LANE), F32)
        for h in range(HEADS):
            dgh = dg_ref[h]
            dab = dab + jnp.where(lane == h, dgh * da_f, 0.0) + jnp.where(lane == HEADS + h, db_ref[h] * db_f, 0.0)
            dalog = dalog + jnp.where(lane1 == h, _colsum(dgh * g_f), 0.0)
        dab_ref[...] = dab.astype(BF16)
        dalog_ref[...] = dalog
        ddtb_ref[...] = jnp.where(lane1 < HEADS, _colsum(dab), 0.0)

    whole = pl.BlockSpec((HEADS, t, LANE), lambda i: (0, 0, 0))
    vec = jax.ShapeDtypeStruct((1, LANE), F32)
    return _pcall(body, name=name, out_shape=(jax.ShapeDtypeStruct((t, LANE), BF16), vec, vec), grid=(1,),
                  in_specs=[_col(t, lambda i: (0, AB_COL // LANE)), _vec(LANE), _vec(LANE), whole, whole],
                  out_specs=(_col(t, lambda i: (0, 0)), _vec(LANE), _vec(LANE)), semantics=("arbitrary",),
                  vmem_limit=VMEM_LIMIT)(proj, alog, dtb, dg, dbeta)


Z_COL = 3 * A_DIM // LANE


def _apost_fwd(o, proj, gn, *, name):
    t = proj.shape[0]

    def body(o_ref, z_ref, gn_ref, y_ref):
        ov = o_ref[0]
        z = z_ref[...]
        r = lax.rsqrt(jnp.mean(ov * ov, axis=-1, keepdims=True) + EPS)
        y_ref[...] = (ov * r * gn_ref[...] * (z * _sigmoid(z))).astype(BF16)

    return _pcall(body, name=name, out_shape=jax.ShapeDtypeStruct((t, A_DIM), BF16), grid=(HEADS,),
                  in_specs=[pl.BlockSpec((1, t, LANE), lambda h: (h, 0, 0)), _col(t, lambda h: (0, Z_COL + h)),
                            pl.BlockSpec((1, LANE), lambda h: (0, 0))],
                  out_specs=_col(t, lambda h: (0, h)), semantics=("parallel",), vmem_limit=VMEM_LIMIT)(o, proj, gn)


def _apost_bwd(o, proj, gn, dmixed, *, name):
    t = proj.shape[0]

    def body(o_ref, z_ref, gn_ref, d_ref, do_ref, dz_ref, dgn_ref):
        ov = o_ref[0]
        z = z_ref[...]
        gnv = gn_ref[...]
        dv = d_ref[...]
        r = lax.rsqrt(jnp.mean(ov * ov, axis=-1, keepdims=True) + EPS)
        ohat = ov * r
        sig = _sigmoid(z)
        dy = dv * (z * sig)
        dz_ref[...] = (dv * ohat * gnv * sig * (1.0 + z * (1.0 - sig))).astype(BF16)
        dyo = dy * gnv
        do_ref[0] = r * (dyo - ohat * jnp.mean(dyo * ohat, axis=-1, keepdims=True))
        part = _colsum(dy * ohat)

        @pl.when(pl.program_id(0) == 0)
        def _():
            dgn_ref[...] = part

        @pl.when(pl.program_id(0) > 0)
        def _():
            dgn_ref[...] += part

    return _pcall(body, name=name,
                  out_shape=(jax.ShapeDtypeStruct((HEADS, t, LANE), F32), jax.ShapeDtypeStruct((t, A_DIM), BF16),
                             jax.ShapeDtypeStruct((1, LANE), F32)),
                  grid=(HEADS,),
                  in_specs=[pl.BlockSpec((1, t, LANE), lambda h: (h, 0, 0)), _col(t, lambda h: (0, Z_COL + h)),
                            pl.BlockSpec((1, LANE), lambda h: (0, 0)), _col(t, lambda h: (0, h))],
                  out_specs=(pl.BlockSpec((1, t, LANE), lambda h: (h, 0, 0)), _col(t, lambda h: (0, h)),
                             pl.BlockSpec((1, LANE), lambda h: (0, 0))),
                  semantics=("arbitrary",), vmem_limit=VMEM_LIMIT)(o, proj, gn, dmixed)


POOL_COL = (AB_COL + LANE) // LANE
CB_COL = POOL_COL + POOL_DIM // LANE
CC_COL = CB_COL + CONV_DIM // LANE
CH_COL = CC_COL + CONV_DIM // LANE
MAX_WIN_LOG2 = 4


def _window_sums(x, shift):
    sums = []
    cur = x
    for k in range(MAX_WIN_LOG2):
        cur = cur + shift(cur, 1 << k)
        sums.append(cur)
    return sums


def _pick_window(sums, win):
    out = sums[-1]
    for k in range(MAX_WIN_LOG2 - 2, -1, -1):
        out = jnp.where(win == float(2 << k), sums[k], out)
    return out


def _pool_counts(shape, win):
    row = lax.broadcasted_iota(jnp.int32, shape, 0).astype(F32)
    return jnp.minimum(row + 1.0, win)


def _pool_fwd(proj, win, wbd, scale, *, name):
    t = proj.shape[0]

    def body(x_ref, win_ref, w_ref, s_ref, y_ref):
        xv = x_ref[...]
        winv = win_ref[...]
        pooled = _pick_window(_window_sums(xv, _shift_down), winv) / _pool_counts(xv.shape, winv) - xv
        y_ref[...] = (_dot(pooled, w_ref[0], NN) * s_ref[...]).astype(BF16)

    nb = POOL_DIM // LANE
    vec = pl.BlockSpec((1, LANE), lambda b: (0, b))
    return _pcall(body, name=name, out_shape=jax.ShapeDtypeStruct((t, POOL_DIM), BF16), grid=(nb,),
                  in_specs=[_col(t, lambda b: (0, POOL_COL + b)), vec, pl.BlockSpec((1, LANE, LANE), lambda b: (b, 0, 0)), vec],
                  out_specs=_col(t, lambda b: (0, b)), semantics=("parallel",), vmem_limit=VMEM_LIMIT)(proj, win, wbd, scale)


def _pool_bwd(proj, win, wbd, scale, dmixed, *, name):
    t = proj.shape[0]

    def body(x_ref, win_ref, w_ref, s_ref, d_ref, dx_ref, dw_ref, ds_ref):
        xv = x_ref[...]
        winv = win_ref[...]
        cnt = _pool_counts(xv.shape, winv)
        pooled = _pick_window(_window_sums(xv, _shift_down), winv) / cnt - xv
        dv = d_ref[...]
        ds_ref[...] = _colsum(dv * _dot(pooled, w_ref[0], NN))
        dy0 = dv * s_ref[...]
        dw_ref[0] = _dot(pooled, dy0, TN)
        dpooled = _dot(dy0, w_ref[0], NT)
        dmean = dpooled / cnt
        dx_ref[...] = (_pick_window(_window_sums(dmean, _shift_up), winv) - dpooled).astype(BF16)

    nb = POOL_DIM // LANE
    vec = pl.BlockSpec((1, LANE), lambda b: (0, b))
    mat = pl.BlockSpec((1, LANE, LANE), lambda b: (b, 0, 0))
    first = A_DIM // LANE
    return _pcall(body, name=name,
                  out_shape=(jax.ShapeDtypeStruct((t, POOL_DIM), BF16), jax.ShapeDtypeStruct((nb, LANE, LANE), F32),
                             jax.ShapeDtypeStruct((1, POOL_DIM), F32)),
                  grid=(nb,),
                  in_specs=[_col(t, lambda b: (0, POOL_COL + b)), vec, mat, vec, _col(t, lambda b: (0, first + b))],
                  out_specs=(_col(t, lambda b: (0, b)), mat, vec), semantics=("parallel",),
                  vmem_limit=VMEM_LIMIT)(proj, win, wbd, scale, dmixed)


def _sconv_fwd(proj, w, *, name):
    t = proj.shape[0]

    def body(cb_ref, cc_ref, ch_ref, w_ref, y_ref):
        y_ref[...] = (cb_ref[...] * _conv_fwd(cc_ref[...] * ch_ref[...], w_ref, CONV_TAPS)).astype(BF16)

    nb = CONV_DIM // LANE
    return _pcall(body, name=name, out_shape=jax.ShapeDtypeStruct((t, CONV_DIM), BF16), grid=(nb,),
                  in_specs=[_col(t, lambda b: (0, CB_COL + b)), _col(t, lambda b: (0, CC_COL + b)),
                            _col(t, lambda b: (0, CH_COL + b)), pl.BlockSpec((CONV_TAPS, LANE), lambda b: (0, b))],
                  out_specs=_col(t, lambda b: (0, b)), semantics=("parallel",), vmem_limit=VMEM_LIMIT)(proj, proj, proj, w)


def _sconv_bwd(proj, w, dmixed, *, name):
    t = proj.shape[0]

    def body(cb_ref, cc_ref, ch_ref, w_ref, d_ref, dcb_ref, dcc_ref, dch_ref, dw_ref):
        cc = cc_ref[...]
        ch = ch_ref[...]
        u = cc * ch
        dv = d_ref[...]
        dcb_ref[...] = (dv * _conv_fwd(u, w_ref, CONV_TAPS)).astype(BF16)
        du = _conv_bwd(u, dv * cb_ref[...], w_ref, dw_ref, CONV_TAPS)
        dcc_ref[...] = (du * ch).astype(BF16)
        dch_ref[...] = (du * cc).astype(BF16)

    nb = CONV_DIM // LANE
    first = (A_DIM + POOL_DIM) // LANE
    act = jax.ShapeDtypeStruct((t, CONV_DIM), BF16)
    wspec = pl.BlockSpec((CONV_TAPS, LANE), lambda b: (0, b))
    ospec = _col(t, lambda b: (0, b))
    return _pcall(body, name=name, out_shape=(act, act, act, jax.ShapeDtypeStruct((CONV_TAPS, CONV_DIM), F32)), grid=(nb,),
                  in_specs=[_col(t, lambda b: (0, CB_COL + b)), _col(t, lambda b: (0, CC_COL + b)),
                            _col(t, lambda b: (0, CH_COL + b)), wspec, _col(t, lambda b: (0, first + b))],
                  out_specs=(ospec, ospec, ospec, wspec), semantics=("parallel",),
                  vmem_limit=VMEM_LIMIT)(proj, proj, proj, w, dmixed)


def _chunk_masks():
    r = lax.broadcasted_iota(jnp.int32, (CHUNK, CHUNK), 0)
    c = lax.broadcasted_iota(jnp.int32, (CHUNK, CHUNK), 1)
    return r >= c, r > c, jnp.where(r == c, 1.0, 0.0).astype(F32)


def _split(a):
    hi = a.astype(BF16)
    return hi, (a - hi.astype(F32)).astype(BF16)


def _dot_split(a, b, dims):
    (ah, al), (bh, bl) = a, b
    return _dot(ah, bh, dims) + _dot(ah, bl, dims) + _dot(al, bh, dims)


def _tri_inv(lows, eye):
    xs = [eye - low for low in lows]
    ps = [_split(low) for low in lows]
    ps = [_split(_dot_split(p, p, NN)) for p in ps]
    for i in range(5):
        xs = [x + _dot_split(_split(x), p, NN) for x, p in zip(xs, ps)]
        if i < 4:
            ps = [_split(_dot_split(p, p, NN)) for p in ps]
    return xs


def _prefix_sum_rows(x):
    for k in range(6):
        x = x + _shift_down(x, 1 << k)
    return x


def _suffix_sum_rows(x):
    for k in range(6):
        x = x + _shift_up(x, 1 << k)
    return x


def _chunk_decay(g, incl):
    gcb = _prefix_sum_rows(g)
    gtot = _colsum(g)
    col = gcb[:, :CHUNK]
    row = gcb.T[:CHUNK, :]
    decay = jnp.exp(jnp.where(incl, col - row, -1e30))
    return gcb, gtot, decay


CHUNKS_PER_STEP = 2


def _heads_of(ref, base, rows):
    return [ref[base + h, rows, :] for h in range(HEADS)]


def _chunk_rows(j):
    return pl.ds(j * CHUNK, CHUNK)


def _deltanet_prep(qkv, g, beta, *, name):
    t = qkv.shape[1]
    n_chunks = t // CHUNK
    per = CHUNKS_PER_STEP
    probs = [(j, h) for j in range(per) for h in range(HEADS)]

    def body(qkv_ref, g_ref, b_ref, u_ref, w_ref, qg_ref, kg_ref, attn_ref, tm_ref):
        incl, strict, eye = _chunk_masks()
        q = [qkv_ref[h, _chunk_rows(j), :] for j, h in probs]
        k = [qkv_ref[HEADS + h, _chunk_rows(j), :] for j, h in probs]
        v = [qkv_ref[2 * HEADS + h, _chunk_rows(j), :] for j, h in probs]
        bv = [b_ref[h, _chunk_rows(j), :] for j, h in probs]
        dec = [_chunk_decay(g_ref[h, _chunk_rows(j), :], incl) for j, h in probs]
        kb = [a * b for a, b in zip(k, bv)]
        low = [jnp.where(strict, _dot(a, b, NT) * d[2], 0.0) for a, b, d in zip(kb, k, dec)]
        tm = _tri_inv(low, eye)
        egc = [jnp.exp(d[0]) for d in dec]
        u = [_dot(m, a * b, NN) for m, a, b in zip(tm, v, bv)]
        w = [_dot(m, a * e, NN) for m, a, e in zip(tm, kb, egc)]
        attn = [_dot(a, b, NT) * d[2] for a, b, d in zip(q, k, dec)]
        for i, (j, h) in enumerate(probs):
            rows = _chunk_rows(j)
            u_ref[h, rows, :] = u[i]
            w_ref[h, rows, :] = w[i].astype(BF16)
            qg_ref[h, rows, :] = (q[i] * egc[i]).astype(BF16)
            kg_ref[h, rows, :] = (k[i] * jnp.exp(dec[i][1] - dec[i][0])).astype(BF16)
            attn_ref[j, h] = attn[i].astype(BF16)
            tm_ref[j, h] = tm[i]

    act = lambda heads: pl.BlockSpec((heads, per * CHUNK, LANE), lambda n: (0, n, 0))
    mat = pl.BlockSpec((per, HEADS, CHUNK, CHUNK), lambda n: (n, 0, 0, 0))
    return _pcall(
        body, name=name,
        out_shape=(jax.ShapeDtypeStruct((HEADS, t, LANE), F32),) + (jax.ShapeDtypeStruct((HEADS, t, LANE), BF16),) * 3
        + (jax.ShapeDtypeStruct((n_chunks, HEADS, CHUNK, CHUNK), BF16), jax.ShapeDtypeStruct((n_chunks, HEADS, CHUNK, CHUNK), F32)),
        grid=(n_chunks // per,), in_specs=[act(3 * HEADS), act(HEADS), act(HEADS)],
        out_specs=(act(HEADS),) * 4 + (mat, mat), semantics=("parallel",), vmem_limit=VMEM_LIMIT)(qkv, g, beta)


SCAN_CHUNKS_PER_STEP = 4


def _deltanet_scan(u, w, qg, kg, attn, g, *, name):
    t = u.shape[1]
    n_chunks = t // CHUNK
    per = SCAN_CHUNKS_PER_STEP

    def body(u_ref, w_ref, qg_ref, kg_ref, attn_ref, g_ref, o_ref, vn_ref, st_ref, s_ref):
        @pl.when(pl.program_id(0) == 0)
        def _():
            s_ref[...] = jnp.zeros_like(s_ref)

        for j in range(per):
            rows = _chunk_rows(j)
            s = [s_ref[h] for h in range(HEADS)]
            vn = [u_ref[h, rows, :] - _dot(w_ref[h, rows, :], s[h], NN) for h in range(HEADS)]
            o = [_dot(qg_ref[h, rows, :], s[h], NN) + _dot(attn_ref[j, h], vn[h], NN) for h in range(HEADS)]
            eg = [jnp.exp(_colsum(g_ref[h, rows, :])) for h in range(HEADS)]
            for h in range(HEADS):
                st_ref[j, h] = s[h]
                s_ref[h] = s[h] * eg[h] + _dot(kg_ref[h, rows, :], vn[h], TN)
                o_ref[h, rows, :] = o[h]
                vn_ref[h, rows, :] = vn[h]

    act = pl.BlockSpec((HEADS, per * CHUNK, LANE), lambda n: (0, n, 0))
    out = jax.ShapeDtypeStruct((HEADS, t, LANE), F32)
    return _pcall(
        body, name=name, out_shape=(out, out, jax.ShapeDtypeStruct((n_chunks, HEADS, LANE, LANE), F32)), grid=(n_chunks // per,),
        in_specs=[act] * 4 + [pl.BlockSpec((per, HEADS, CHUNK, CHUNK), lambda n: (n, 0, 0, 0)), act],
        out_specs=(act, act, pl.BlockSpec((per, HEADS, LANE, LANE), lambda n: (n, 0, 0, 0))),
        scratch_shapes=[pltpu.VMEM((HEADS, LANE, LANE), F32)], semantics=("arbitrary",))(u, w, qg, kg, attn, g)


def _deltanet_bscan(w, qg, kg, attn, g, do, *, name):
    t = w.shape[1]
    n_chunks = t // CHUNK
    per = SCAN_CHUNKS_PER_STEP
    steps = n_chunks // per

    def body(w_ref, qg_ref, kg_ref, attn_ref, g_ref, do_ref, dvn_ref, dsn_ref, ds_ref):
        @pl.when(pl.program_id(0) == 0)
        def _():
            ds_ref[...] = jnp.zeros_like(ds_ref)

        for j in reversed(range(per)):
            rows = _chunk_rows(j)
            dsn = [ds_ref[h] for h in range(HEADS)]
            dov = [do_ref[h, rows, :] for h in range(HEADS)]
            dvn = [_dot(attn_ref[j, h], dov[h], TN) + _dot(kg_ref[h, rows, :], dsn[h], NN) for h in range(HEADS)]
            eg = [jnp.exp(_colsum(g_ref[h, rows, :])) for h in range(HEADS)]
            for h in range(HEADS):
                dsn_ref[j, h] = dsn[h]
                ds_ref[h] = _dot(qg_ref[h, rows, :], dov[h], TN) + eg[h] * dsn[h] - _dot(w_ref[h, rows, :], dvn[h], TN)
                dvn_ref[h, rows, :] = dvn[h]

    act = pl.BlockSpec((HEADS, per * CHUNK, LANE), lambda n: (0, steps - 1 - n, 0))
    return _pcall(
        body, name=name,
        out_shape=(jax.ShapeDtypeStruct((HEADS, t, LANE), F32), jax.ShapeDtypeStruct((n_chunks, HEADS, LANE, LANE), F32)),
        grid=(steps,),
        in_specs=[act] * 3 + [pl.BlockSpec((per, HEADS, CHUNK, CHUNK), lambda n: (steps - 1 - n, 0, 0, 0)), act, act],
        out_specs=(act, pl.BlockSpec((per, HEADS, LANE, LANE), lambda n: (steps - 1 - n, 0, 0, 0))),
        scratch_shapes=[pltpu.VMEM((HEADS, LANE, LANE), F32)], semantics=("arbitrary",))(w, qg, kg, attn, g, do)


def _sum_all(x):
    return jnp.sum(jnp.sum(x, axis=1, keepdims=True), axis=0, keepdims=True)


def _rowsum(x):
    return jnp.sum(x, axis=1, keepdims=True)


def _deltanet_post(qkv, g, beta, tmats, states, dstates, do, dvn, vn, *, name):
    t = qkv.shape[1]
    n_chunks = t // CHUNK
    per = CHUNKS_PER_STEP
    probs = [(j, h) for j in range(per) for h in range(HEADS)]

    def body(qkv_ref, g_ref, b_ref, tm_ref, st_ref, dsn_ref, do_ref, dvn_ref, vn_ref, dqkv_ref, dg_ref, db_ref):
        incl, strict, _ = _chunk_masks()
        ones = jnp.ones((CHUNK, LANE), BF16)
        last_row = lax.broadcasted_iota(jnp.int32, (CHUNK, LANE), 0) == CHUNK - 1
        z = lambda f, *cols: [f(*a) for a in zip(*cols)]
        q = [qkv_ref[h, _chunk_rows(j), :] for j, h in probs]
        k = [qkv_ref[HEADS + h, _chunk_rows(j), :] for j, h in probs]
        v = [qkv_ref[2 * HEADS + h, _chunk_rows(j), :] for j, h in probs]
        bv = [b_ref[h, _chunk_rows(j), :] for j, h in probs]
        dov = [do_ref[h, _chunk_rows(j), :] for j, h in probs]
        dvn_ = [dvn_ref[h, _chunk_rows(j), :] for j, h in probs]
        vn_ = [vn_ref[h, _chunk_rows(j), :] for j, h in probs]
        tm = [tm_ref[j, h] for j, h in probs]
        s = [st_ref[j, h] for j, h in probs]
        dsn = [dsn_ref[j, h] for j, h in probs]
        dec = [_chunk_decay(g_ref[h, _chunk_rows(j), :], incl) for j, h in probs]
        decay = [d[2] for d in dec]
        egc = [jnp.exp(d[0]) for d in dec]
        ekg = [jnp.exp(d[1] - d[0]) for d in dec]
        kb = z(lambda a, b: a * b, k, bv)
        vb = z(lambda a, b: a * b, v, bv)
        kbg = z(lambda a, b: a * b, kb, egc)
        qg = z(lambda a, b: a * b, q, egc)
        kg = z(lambda a, b: a * b, k, ekg)
        kk = z(lambda a, b: _dot(a, b, NT), kb, k)
        qk = z(lambda a, b: _dot(a, b, NT), q, k)
        dattn = z(lambda a, b: jnp.where(incl, _dot(a, b, NT), 0.0), dov, vn_)
        dqg = z(lambda a, b: _dot(a, b, NT), dov, s)
        dkg = z(lambda a, b: _dot(a, b, NT), vn_, dsn)
        dglast = z(lambda a, b, c, d, e: _sum_all(a * b) * jnp.exp(e[1]) + _sum_all(c * d), s, dsn, dkg, kg, dec)
        dw = z(lambda a, b: -_dot(a, b, NT), dvn_, s)
        dtm = z(lambda a, b, c, d: _dot(a, b, NT) + _dot(c, d, NT), dvn_, vb, dw, kbg)
        dvb = z(lambda a, b: _dot(a, b, TN), tm, dvn_)
        dkbg = z(lambda a, b: _dot(a, b, TN), tm, dw)
        dlow = z(lambda a, b: jnp.where(strict, -_dot(_dot(a, b, TN), a, NT), 0.0), tm, dtm)
        dkk = z(lambda a, b: a * b, dlow, decay)
        dqk = z(lambda a, b: a * b, dattn, decay)
        dkb = z(lambda a, b, c, d: _dot(a, b, NN) + c * d, dkk, k, dkbg, egc)
        dk = z(lambda a, b, c, d, e, f, g_, h_: _dot(a, b, TN) + _dot(c, d, TN) + e * f + g_ * h_, dkk, kb, dqk, q, dkg, ekg, dkb, bv)
        dq = z(lambda a, b, c, d: _dot(a, b, NN) + c * d, dqk, k, dqg, egc)
        m = z(lambda a, b, c, d, e: (a * b + c * d) * e, dlow, kk, dattn, qk, decay)
        mcol = [_dot(mh, ones, TN) + _dot(ml, ones, TN) for mh, ml in (_split(a) for a in m)]
        for i, (j, h) in enumerate(probs):
            rows = _chunk_rows(j)
            dqkv_ref[h, rows, :] = dq[i]
            dqkv_ref[HEADS + h, rows, :] = dk[i]
            dqkv_ref[2 * HEADS + h, rows, :] = dvb[i] * bv[i]
            db_ref[h, rows, :] = jnp.broadcast_to(_rowsum(dkb[i] * k[i] + dvb[i] * v[i]), (CHUNK, LANE))
            dgc = (_rowsum(dqg[i] * qg[i] + dkbg[i] * kbg[i] - dkg[i] * kg[i]) + _rowsum(m[i]) - mcol[i]
                   + jnp.where(last_row, dglast[i], 0.0))
            dg_ref[h, rows, :] = _suffix_sum_rows(dgc)

    act = lambda heads: pl.BlockSpec((heads, per * CHUNK, LANE), lambda n: (0, n, 0))
    mat = lambda d: pl.BlockSpec((per, HEADS, d, d), lambda n: (n, 0, 0, 0))
    out = jax.ShapeDtypeStruct((HEADS, t, LANE), F32)
    return _pcall(
        body, name=name, out_shape=(jax.ShapeDtypeStruct((3 * HEADS, t, LANE), F32), out, out), grid=(n_chunks // per,),
        in_specs=[act(3 * HEADS), act(HEADS), act(HEADS), mat(CHUNK), mat(LANE), mat(LANE), act(HEADS), act(HEADS), act(HEADS)],
        out_specs=(act(3 * HEADS), act(HEADS), act(HEADS)), semantics=("parallel",),
        vmem_limit=VMEM_LIMIT)(qkv, g, beta, tmats, states, dstates, do, dvn, vn)


ANY = pl.BlockSpec(memory_space=pl.ANY)
PEERS = N_DEV - 1


def _all_gather(arrays, *, name):
    n = len(arrays)

    def body(*refs):
        ins, outs = refs[:n], refs[n:2 * n]
        send_sems, recv_sems, local_sems = refs[2 * n:]
        x, y, c = lax.axis_index("x"), lax.axis_index("y"), lax.axis_index("c")
        me, sibling = (x, y, c), (x, y, 1 - c)
        chips = [(1 - x, y), (x, 1 - y), (1 - x, 1 - y)]

        def copy(a, k, block, to, src=None):
            dst = outs[a].at[4 * block[0] + 2 * block[1] + block[2]]
            return pltpu.make_async_remote_copy(src_ref=dst if src is None else src, dst_ref=dst, send_sem=send_sems.at[a * PEERS + k],
                                                recv_sem=recv_sems.at[a * PEERS + k], device_id=to, device_id_type=MESH)

        local = [pltpu.make_async_copy(ins[a], outs[a].at[4 * x + 2 * y + c], local_sems.at[a]) for a in range(n)]
        for cp in local:
            cp.start()
        first = []
        for a in range(n):
            first.append(copy(a, 0, me, sibling, src=ins[a]))
            first += [copy(a, 1 + j, me, (*chip, c), src=ins[a]) for j, chip in enumerate(chips)]
        for cp in first:
            cp.start()
        passed = []
        for a in range(n):
            for j, chip in enumerate(chips):
                copy(a, 1 + j, (*chip, c), me).wait_recv()
                fwd = copy(a, 4 + j, (*chip, c), sibling)
                fwd.start()
                passed.append(fwd)
        for a in range(n):
            copy(a, 0, sibling, me).wait_recv()
            for j, chip in enumerate(chips):
                copy(a, 4 + j, (*chip, 1 - c), me).wait_recv()
        for cp in first + passed:
            cp.wait_send()
        for cp in local:
            cp.wait()

    return _pcall(body, name=name, out_shape=tuple(jax.ShapeDtypeStruct((N_DEV,) + a.shape, a.dtype) for a in arrays),
                  in_specs=[ANY] * n, out_specs=(ANY,) * n,
                  scratch_shapes=[pltpu.SemaphoreType.DMA((n * PEERS,)), pltpu.SemaphoreType.DMA((n * PEERS,)),
                                  pltpu.SemaphoreType.DMA((n,))])(*arrays)


def _exchange_blocks(arrays, *, name):
    n = len(arrays)

    def body(*refs):
        ins, outs = refs[:n], refs[n:2 * n]
        send_sems, recv_sems, local_sems = refs[2 * n:]
        x, y, c = lax.axis_index("x"), lax.axis_index("y"), lax.axis_index("c")
        mine = 4 * x + 2 * y + c
        copies = []
        for a in range(n):
            lc = pltpu.make_async_copy(ins[a].at[mine], outs[a].at[mine], local_sems.at[a])
            lc.start()
            copies.append(lc)
            for k in range(1, N_DEV):
                px = 1 - x if k & 4 else x
                py = 1 - y if k & 2 else y
                pc = 1 - c if k & 1 else c
                cp = pltpu.make_async_remote_copy(src_ref=ins[a].at[4 * px + 2 * py + pc], dst_ref=outs[a].at[mine],
                                                  send_sem=send_sems.at[a * PEERS + k - 1], recv_sem=recv_sems.at[a * PEERS + k - 1],
                                                  device_id=(px, py, pc), device_id_type=MESH)
                cp.start()
                copies.append(cp)
        for cp in copies:
            cp.wait()

    return _pcall(body, name=name, out_shape=tuple(jax.ShapeDtypeStruct(a.shape, a.dtype) for a in arrays),
                  in_specs=[ANY] * n, out_specs=(ANY,) * n,
                  scratch_shapes=[pltpu.SemaphoreType.DMA((n * PEERS,)), pltpu.SemaphoreType.DMA((n * PEERS,)),
                                  pltpu.SemaphoreType.DMA((n,))])(*arrays)


HBM = pl.BlockSpec(memory_space=pltpu.HBM)
SEM = pl.BlockSpec(memory_space=pltpu.SEMAPHORE)
EFFECT = pltpu.SideEffectType.DATAFLOW_SIDE_EFFECTING


def _direct_copies(srcs, lands, send_sems, recv_sems, local_sems, scatter):
    x, y, c = lax.axis_index("x"), lax.axis_index("y"), lax.axis_index("c")
    mine = 4 * x + 2 * y + c
    copies = []
    for a, (src, land) in enumerate(zip(srcs, lands)):
        copies.append(pltpu.make_async_copy(src.at[mine] if scatter else src, land.at[mine], local_sems.at[a]))
        for k in range(1, N_DEV):
            px = 1 - x if k & 4 else x
            py = 1 - y if k & 2 else y
            pc = 1 - c if k & 1 else c
            copies.append(pltpu.make_async_remote_copy(
                src_ref=src.at[4 * px + 2 * py + pc] if scatter else src, dst_ref=land.at[mine],
                send_sem=send_sems.at[a * PEERS + k - 1], recv_sem=recv_sems.at[a * PEERS + k - 1],
                device_id=(px, py, pc), device_id_type=MESH))
    return copies


def _exchange_start(srcs, scatter, *, name):
    n = len(srcs)
    land_shapes = [s.shape if scatter else (N_DEV,) + s.shape for s in srcs]

    def body(*refs):
        srcs_, lands = refs[:n], refs[n:2 * n]
        send_sems, recv_sems, local_sems = refs[2 * n:2 * n + 3]
        token = refs[-1]
        for cp in _direct_copies(srcs_, lands, send_sems, recv_sems, local_sems, scatter):
            cp.start()
        token[...] = jnp.zeros_like(token)

    sems = (pltpu.SemaphoreType.DMA((n * PEERS,)), pltpu.SemaphoreType.DMA((n * PEERS,)), pltpu.SemaphoreType.DMA((n,)))
    thru = tuple(pltpu.HBM(s.shape, s.dtype) for s in srcs) + tuple(pltpu.HBM(shp, s.dtype) for shp, s in zip(land_shapes, srcs))
    ins = [pltpu.with_memory_space_constraint(s, pltpu.HBM) for s in srcs]
    ins += [pltpu.with_memory_space_constraint(lax.empty(shp, s.dtype), pltpu.HBM) for shp, s in zip(land_shapes, srcs)]
    out = pl.pallas_call(
        body, name=name, out_shape=sems + thru + (jax.ShapeDtypeStruct((SUBLANE, LANE), F32),), in_specs=[HBM] * (2 * n),
        out_specs=(SEM,) * 3 + (HBM,) * (2 * n) + (pl.BlockSpec(memory_space=pltpu.VMEM),),
        input_output_aliases={i: 3 + i for i in range(2 * n)},
        compiler_params=pltpu.CompilerParams(has_side_effects=EFFECT))(*ins)
    return out[:-1], out[-1]


def _exchange_wait(started, after, scatter, *, name):
    n = (len(started) - 3) // 2
    sems, arrays = started[:3], started[3:]

    def body(*refs):
        srcs_, lands = refs[:n], refs[n:2 * n]
        send_sems, recv_sems, local_sems = refs[2 * n:2 * n + 3]
        for cp in _direct_copies(srcs_, lands, send_sems, recv_sems, local_sems, scatter):
            cp.wait()

    out = pl.pallas_call(
        body, name=name, out_shape=tuple(pltpu.HBM(a.shape, a.dtype) for a in arrays),
        in_specs=[HBM] * (2 * n) + [SEM] * 3 + [ANY], out_specs=(HBM,) * (2 * n),
        input_output_aliases={i: i for i in range(2 * n)},
        compiler_params=pltpu.CompilerParams(has_side_effects=EFFECT))(*arrays, *sems, after)
    return out[n:]


def _adamw_reduce(w, parts, m, v, *, name):
    layers, r, c = w.shape
    assert len(parts) == layers
    tr = _tile(r, 512, 16)
    tiles = r // tr
    bc1 = 1.0 - ADAM_B1 ** ADAM_STEP
    bc2 = 1.0 - ADAM_B2 ** ADAM_STEP

    def body(w_ref, *rest):
        p_refs = rest[:layers]
        m_ref, v_ref, g_ref, d_ref, nm_ref, nv_ref = rest[layers:]

        def update(p_ref):
            g = p_ref[0, :, pl.ds(0, c)].astype(F32)
            for s in range(1, N_DEV):
                g = g + p_ref[s, :, pl.ds(0, c)].astype(F32)
            nm = ADAM_B1 * m_ref[0] + (1.0 - ADAM_B1) * g
            nv = ADAM_B2 * v_ref[0] + (1.0 - ADAM_B2) * (g * g)
            g_ref[0] = g
            nm_ref[0] = nm
            nv_ref[0] = nv
            d_ref[0] = -ADAM_LR * ((nm / bc1) / (jnp.sqrt(nv / bc2) + ADAM_EPS) + ADAM_WD * w_ref[0])

        for layer in range(layers):
            pl.when(pl.program_id(0) == layer)(functools.partial(update, p_refs[layer]))

    def part_spec(layer, shape):
        rest = 0 if layer > 0 else tiles - 1
        return pl.BlockSpec((N_DEV, tr, shape[2]), lambda l, i: (0, jnp.where(l == layer, i, rest), 0))

    spec = pl.BlockSpec((1, tr, c), lambda l, i: (l, i, 0))
    out = jax.ShapeDtypeStruct((layers, r, c), F32)
    return _pcall(body, name=name, out_shape=(out,) * 4, grid=(layers, tiles),
                  in_specs=[spec] + [part_spec(layer, p.shape) for layer, p in enumerate(parts)] + [spec, spec],
                  out_specs=(spec,) * 4, semantics=("arbitrary", "arbitrary"), vmem_limit=VMEM_LIMIT)(w, *parts, m, v)


def _pool_windows():
    return jnp.repeat(jnp.asarray(POOL_WINDOWS, F32), POOL_DIM // len(POOL_WINDOWS))[None, :]


def _block_diag_pairs(pool_w):
    z = jnp.zeros_like(pool_w[0])
    return jnp.stack([jnp.block([[pool_w[2 * b], z], [z, pool_w[2 * b + 1]]]) for b in range(2)])


def _pad_lanes(vec):
    return jnp.zeros((1, LANE), F32).at[0, :vec.shape[0]].set(vec)


FF_SHARD = D_FF // N_DEV
FF_BLOCK = 384
D_FF_PAD = N_DEV * FF_BLOCK


def _layer_fwd(x, p_i, wt):
    h1 = _rmsnorm_fwd(x, wt["norm1_g"], name="rmsnorm_fwd")
    proj = _matmul(h1, wt["w_in"], "nn", name="mm_in")
    qkv = _qkv_prep_fwd(proj, wt["conv_qkv"], name="qkv_prep_fwd")
    g, beta = _gates_fwd(proj, wt["a_log"], wt["dt_bias"], name="gates_fwd")
    u, w, qg, kg, attn, tmats = _deltanet_prep(qkv, g, beta, name="deltanet_prep")
    o, vn, states = _deltanet_scan(u, w, qg, kg, attn, g, name="deltanet_scan")
    o_a = _apost_fwd(o, proj, wt["onorm_g"], name="apost_fwd")
    o_b = _pool_fwd(proj, wt["pool_win"], wt["pool_wbd"], wt["pool_scale"], name="pool_fwd")
    o_c = _sconv_fwd(proj, wt["sconv_w"], name="sconv_fwd")
    mixed = jnp.concatenate([o_a, o_b, o_c], axis=1)
    x1 = _matmul(mixed, wt["w_out"], "nn", res=x, name="mm_out")
    h2 = _rmsnorm_fwd(x1, wt["norm2_g"], name="rmsnorm_fwd")
    gate = _matmul(h2, wt["w_gate"], "nn", b_blocked=True, name="mm_gate")
    up = _matmul(h2, wt["w_up"], "nn", b_blocked=True, name="mm_up")
    ff = _swiglu_fwd(gate, up, name="swiglu_fwd")
    x2 = _matmul(ff, wt["w_down"], "nn", res=x1, name="mm_down")
    pgl = _matmul(x2, wt["ple_gate"], "nn", name="mm_pleg")
    pp = _matmul(p_i, wt["ple_proj"], "nn", b_blocked=True, name="mm_plep")
    x3 = _ple_fwd(x2, pgl, pp, name="ple_fwd")
    saved = dict(x=x, h1=h1, proj=proj, qkv=qkv, g=g, beta=beta, o=o, states=states, tmats=tmats, mixed=mixed, x1=x1, h2=h2,
                 gate=gate, up=up, ff=ff, x2=x2, pgl=pgl, pp=pp, p=p_i, w=w, qg=qg, kg=kg, attn=attn, vn=vn)
    return x3, saved


def _col_blocks(g):
    a = g.shape[0]
    return jnp.transpose(g.reshape(a, N_DEV, -1), (1, 0, 2))


def _cols_joined(blocks):
    return jnp.transpose(blocks, (1, 0, 2)).reshape(blocks.shape[1], -1)


def _layer_bwd(dx3, sv, wt):
    gr, big = {}, {}
    rows = D_MODEL // N_DEV
    dpgl, dpp = _ple_bwd(dx3, sv["pgl"], sv["pp"], name="ple_bwd")
    big["ple_proj"] = _matmul(sv["p"], dpp, "tn", out_blocked=(N_DEV, rows), out_dtype=BF16, name="mm_dplep")
    big["ple_gate"] = _matmul(sv["x2"], dpgl, "tn", out_dtype=BF16, name="mm_dpleg").reshape(N_DEV, rows, D_MODEL)
    dx2 = _matmul(dpgl, wt["ple_gate"], "nt", res=dx3, name="mm_dx2")
    big["w_down"] = _matmul(sv["ff"], dx2, "tn", out_dtype=BF16, name="mm_ddown").reshape(N_DEV, FF_BLOCK, D_MODEL)
    dff = _matmul(dx2, wt["w_down"], "nt", name="mm_dff")
    dgate, dup = _swiglu_bwd(sv["gate"], sv["up"], dff, name="swiglu_bwd")
    big["w_gate"] = _matmul(sv["h2"], dgate, "tn", out_blocked=(N_DEV, FF_BLOCK), out_dtype=BF16, name="mm_dgate")
    big["w_up"] = _matmul(sv["h2"], dup, "tn", out_blocked=(N_DEV, FF_BLOCK), out_dtype=BF16, name="mm_dup")
    dh2 = _matmul(dgate, wt["w_gate"], "nt", b_blocked=True, name="mm_dh2_gate")
    dh2 = _matmul(dup, wt["w_up"], "nt", b_blocked=True, res=dh2, name="mm_dh2_up")
    dx1, gr["norm2_g"] = _rmsnorm_bwd(sv["x1"], wt["norm2_g"], dh2, dx2, name="rmsnorm_bwd")
    big["w_out"] = _matmul(sv["mixed"], dx1, "tn", out_dtype=BF16, name="mm_dout").reshape(N_DEV, rows, D_MODEL)
    dmixed = _matmul(dx1, wt["w_out"], "nt", name="mm_dmixed")
    proj = sv["proj"]
    dcb, dcc, dch, dsconv = _sconv_bwd(proj, wt["sconv_w"], dmixed, name="sconv_bwd")
    big["sconv_w"] = _col_blocks(dsconv)
    dhp, dwbd, gr["pool_scale"] = _pool_bwd(proj, wt["pool_win"], wt["pool_wbd"], wt["pool_scale"], dmixed, name="pool_bwd")
    half = LANE // 2
    gr["pool_w"] = jnp.stack([dwbd[0, :half, :half], dwbd[0, half:, half:], dwbd[1, :half, :half], dwbd[1, half:, half:]])
    do, dz, gr["onorm_g"] = _apost_bwd(sv["o"], proj, wt["onorm_g"], dmixed, name="apost_bwd")
    dvn, dstates = _deltanet_bscan(sv["w"], sv["qg"], sv["kg"], sv["attn"], sv["g"], do, name="deltanet_bscan")
    dqkv_h, dg, dbeta = _deltanet_post(sv["qkv"], sv["g"], sv["beta"], sv["tmats"], sv["states"], dstates, do, dvn, sv["vn"],
                                       name="deltanet_post")
    dab, dalog, ddtb = _gates_bwd(proj, wt["a_log"], wt["dt_bias"], dg, dbeta, name="gates_bwd")
    gr["a_log"], gr["dt_bias"] = dalog[0, :HEADS], ddtb[0, :HEADS]
    dqkv, dconv = _qkv_prep_bwd(proj, wt["conv_qkv"], dqkv_h, name="qkv_prep_bwd")
    big["conv_qkv"] = _col_blocks(dconv)
    dproj = jnp.concatenate([dqkv, dz, dab, dhp, dcb, dcc, dch], axis=1)
    dwin = _matmul(sv["h1"], dproj, "tn", out_dtype=BF16, name="mm_din")
    big["w_in"] = _col_blocks(jnp.concatenate([dwin[:, :AB_COL + 2 * HEADS], dwin[:, AB_COL + LANE:]], axis=1))
    dh1 = _matmul(dproj, wt["w_in"], "nt", name="mm_dh1")
    dx, gr["norm1_g"] = _rmsnorm_bwd(sv["x"], wt["norm1_g"], dh1, dx1, name="rmsnorm_bwd")
    return dx, big, gr


def _layer_weights(gathered, w, i):
    w_in = _cols_joined(gathered["w_in"])
    return dict(
        norm1_g=w["norm1_g"][i][None], norm2_g=w["norm2_g"][i][None], onorm_g=w["onorm_g"][i][None],
        a_log=_pad_lanes(w["a_log"][i]), dt_bias=_pad_lanes(w["dt_bias"][i]),
        pool_scale=w["pool_scale"][i][None], pool_win=_pool_windows(), pool_wbd=_block_diag_pairs(w["pool_w"][i]),
        conv_qkv=_cols_joined(gathered["conv_qkv"]), sconv_w=_cols_joined(gathered["sconv_w"]),
        w_in=jnp.concatenate([w_in[:, :AB_COL + 2 * HEADS], jnp.zeros((D_MODEL, LANE - 2 * HEADS), BF16),
                              w_in[:, AB_COL + 2 * HEADS:]], axis=1),
        w_gate=gathered["w_gate"], w_up=gathered["w_up"], w_down=gathered["w_down"].reshape(D_FF_PAD, D_MODEL),
        w_out=gathered["w_out"].reshape(D_MODEL, D_MODEL), ple_gate=gathered["ple_gate"].reshape(D_MODEL, D_MODEL),
        ple_proj=gathered["ple_proj"])


def _local_step(x, p, target, layers, final_g):
    saved = []
    h = x
    for i in range(DEPTH):
        h, sv = _layer_fwd(h, p[i], layers[i])
        saved.append(sv)
    dx, dgf, loss = _loss_head(h, final_g, target, name="loss_head")
    big, small = [None] * DEPTH, [None] * DEPTH
    for i in reversed(range(DEPTH)):
        dx, big[i], small[i] = _layer_bwd(dx, saved[i], layers[i])
    return loss, dx, big, small, dgf


SHARDED = ("w_in", "w_gate", "w_up", "w_down", "w_out", "ple_gate", "ple_proj", "conv_qkv", "sconv_w")
SMALL = ("norm1_g", "a_log", "dt_bias", "onorm_g", "pool_w", "pool_scale", "norm2_g", "final_g")
SLAB_COLS = 1024


def _payload(name, shard):
    if name in ("conv_qkv", "sconv_w"):
        return shard
    out = shard.astype(BF16)
    if name in ("w_gate", "w_up"):
        out = jnp.pad(out, ((0, 0), (0, FF_BLOCK - FF_SHARD)))
    if name == "w_down":
        out = jnp.pad(out, ((0, FF_BLOCK - FF_SHARD), (0, 0)))
    return out


def _slab_rows(shape):
    size = 1
    for s in shape:
        size *= s
    return SUBLANE * -(-size // (SUBLANE * SLAB_COLS))


def _pack_slab(parts, extra_row):
    rows = []
    for name in SMALL:
        flat = parts[name].reshape(-1)
        nrow = _slab_rows(parts[name].shape)
        rows.append(jnp.pad(flat, (0, nrow * SLAB_COLS - flat.shape[0])).reshape(nrow, SLAB_COLS))
    rows.append(jnp.pad(extra_row, ((0, SUBLANE - 1), (0, 0))))
    return jnp.concatenate(rows, axis=0)


def _unpack_slab(slab, shapes):
    out, row = {}, 0
    for name in SMALL:
        size = 1
        for s in shapes[name]:
            size *= s
        out[name] = slab[row:row + _slab_rows(shapes[name])].reshape(-1)[:size].reshape(shapes[name])
        row += _slab_rows(shapes[name])
    return out, row


def kernel(x, p, norm1_g, w_in, conv_qkv, a_log, dt_bias, onorm_g, pool_w, pool_scale, sconv_w, w_out, norm2_g, w_gate, w_up, w_down, ple_proj, ple_gate, final_g, loss_target, m_norm1_g, m_w_in, m_conv_qkv, m_a_log, m_dt_bias, m_onorm_g, m_pool_w, m_pool_scale, m_sconv_w, m_w_out, m_norm2_g, m_w_gate, m_w_up, m_w_down, m_ple_proj, m_ple_gate, m_final_g, v_norm1_g, v_w_in, v_conv_qkv, v_a_log, v_dt_bias, v_onorm_g, v_pool_w, v_pool_scale, v_sconv_w, v_w_out, v_norm2_g, v_w_gate, v_w_up, v_w_down, v_ple_proj, v_ple_gate, v_final_g):
    names = ["norm1_g", "w_in", "conv_qkv", "a_log", "dt_bias", "onorm_g", "pool_w", "pool_scale", "sconv_w", "w_out", "norm2_g",
             "w_gate", "w_up", "w_down", "ple_proj", "ple_gate", "final_g"]
    w = dict(zip(names, [norm1_g, w_in, conv_qkv, a_log, dt_bias, onorm_g, pool_w, pool_scale, sconv_w, w_out, norm2_g, w_gate, w_up,
                         w_down, ple_proj, ple_gate, final_g]))
    m = dict(zip(names, [m_norm1_g, m_w_in, m_conv_qkv, m_a_log, m_dt_bias, m_onorm_g, m_pool_w, m_pool_scale, m_sconv_w, m_w_out,
                         m_norm2_g, m_w_gate, m_w_up, m_w_down, m_ple_proj, m_ple_gate, m_final_g]))
    v = dict(zip(names, [v_norm1_g, v_w_in, v_conv_qkv, v_a_log, v_dt_bias, v_onorm_g, v_pool_w, v_pool_scale, v_sconv_w, v_w_out,
                         v_norm2_g, v_w_gate, v_w_up, v_w_down, v_ple_proj, v_ple_gate, v_final_g]))

    gathered = _all_gather([_payload(k, w[k][0]) for k in SHARDED], name="all_gather_weights")
    in_flight, token = _exchange_start([_payload(k, w[k][1]) for k in SHARDED], False, name="gather_start")
    first = _layer_weights(dict(zip(SHARDED, gathered)), w, 0)
    first["norm1_g"] = first["norm1_g"] + token[0, 0]

    h, saved0 = _layer_fwd(x[0], p[0, 0], first)
    gathered = _exchange_wait(in_flight, h, False, name="gather_wait")
    second = _layer_weights(dict(zip(SHARDED, gathered)), w, 1)
    h, saved1 = _layer_fwd(h, p[1, 0], second)
    dx, dgf, loss_part = _loss_head(h, final_g[None], loss_target[0], name="loss_head")
    big, small = [None] * DEPTH, [None] * DEPTH
    dx, big[1], small[1] = _layer_bwd(dx, saved1, second)
    in_flight, token = _exchange_start([big[1][k] for k in SHARDED], True, name="exchange_start")
    dx, big[0], small[0] = _layer_bwd(dx + token[0, 0], saved0, first)

    received = [_exchange_blocks([big[0][k] for k in SHARDED], name="exchange_grad_blocks"),
                _exchange_wait(in_flight, dx, True, name="exchange_wait")]
    grads = {k: jnp.stack([small[i][k] for i in range(DEPTH)]) for k in small[0]}
    grads = {k: g[:, 0] if k in ("norm1_g", "norm2_g", "onorm_g", "pool_scale") else g for k, g in grads.items()}
    grads["final_g"] = dgf[0]
    loss_row = jnp.pad(loss_part, ((0, 0), (0, SLAB_COLS - LANE)))
    (small_parts,) = _all_gather([_pack_slab(grads, loss_row)], name="all_gather_small_grads")

    out_g, out_d, out_m, out_v = {}, {}, {}, {}
    for j, k in enumerate(SHARDED):
        out_g[k], out_d[k], out_m[k], out_v[k] = _adamw_reduce(w[k], [received[i][j] for i in range(DEPTH)], m[k], v[k],
                                                                name="adamw_" + k)
    zero_row = jnp.zeros((1, SLAB_COLS), F32)
    slabs = _adamw_reduce(_pack_slab(w, zero_row)[None], [small_parts], _pack_slab(m, zero_row)[None],
                          _pack_slab(v, zero_row)[None], name="adamw_small")
    slabs = [s[0] for s in slabs]
    shapes = {k: w[k].shape for k in SMALL}
    for dst, slab in zip((out_g, out_d, out_m, out_v), slabs):
        vals, _ = _unpack_slab(slab, shapes)
        dst.update(vals)
    _, loss_at = _unpack_slab(slabs[0], shapes)
    loss = slabs[0][loss_at, 0]

    return (loss, dx[None], *[out_g[k] for k in names], *[out_d[k] for k in names], *[out_m[k] for k in names],
            *[out_v[k] for k in names])
```

```python
import functools

import jax
import jax.numpy as jnp
from jax import lax
from jax.experimental import pallas as pl
from jax.experimental.pallas import tpu as pltpu

F32 = jnp.float32
BF16 = jnp.bfloat16

D_MODEL = 1024
DEPTH = 2
PLE_DIM = 256
EPS = 1e-6
HEAD_DIM = 128
HEADS = 4
A_DIM = HEADS * HEAD_DIM
QKV_TAPS = 4
CHUNK = 64
POOL_WINDOWS = (2, 4, 8, 16)
POOL_DIM = 256
CONV_DIM = 256
CONV_TAPS = 3
D_FF = 2816
D_IN = 3080
D_IN_PAD = 3200
AB_COL = 2048
N_DEV = 8

ADAM_LR = 0.001
ADAM_B1 = 0.9
ADAM_B2 = 0.999
ADAM_EPS = 1e-08
ADAM_WD = 0.01
ADAM_STEP = 10

LANE = 128
SUBLANE = 8
VMEM_BYTES_V7X = 64 * 1024 * 1024
VMEM_LIMIT = 48 * 1024 * 1024

_HI = lax.Precision.HIGHEST
NN = ((1,), (0,))
NT = ((1,), (1,))
TN = ((0,), (0,))
MESH = pl.DeviceIdType.MESH


def _dot(a, b, dims, hi=False):
    if hi:
        return lax.dot_general(a, b, (dims, ((), ())), precision=_HI, preferred_element_type=F32)
    return lax.dot_general(a.astype(BF16), b.astype(BF16), (dims, ((), ())), preferred_element_type=F32)


def _pcall(body, *, name, out_shape, grid=(), in_specs=None, out_specs=None, scratch_shapes=(), semantics=None,
           vmem_limit=None, **kw):
    params = {}
    if semantics is not None:
        params["dimension_semantics"] = semantics
    if vmem_limit is not None:
        params["vmem_limit_bytes"] = vmem_limit
    return pl.pallas_call(
        body, name=name, out_shape=out_shape, grid=grid, in_specs=in_specs, out_specs=out_specs,
        scratch_shapes=list(scratch_shapes), compiler_params=pltpu.CompilerParams(**params), **kw)


def _sigmoid(x):
    return 1.0 / (1.0 + jnp.exp(-x))


def _softplus(x):
    return jnp.maximum(x, 0.0) + jnp.log(1.0 + jnp.exp(-jnp.abs(x)))


def _tile(n, cap, mult):
    if n <= cap:
        return n
    best = None
    for t in range(mult, cap + 1, mult):
        if n % t == 0:
            best = t
    assert best is not None, (n, cap, mult)
    return best


ROWS_PER_STEP = 512
COLS_PER_DOT = 640


def _matmul_rows(a, b, mode, *, name, res=None, out_dtype=F32, b_blocked=False):
    m, k = a.shape
    if b_blocked:
        nb, _, bw = b.shape
        n = nb * bw if mode == "nn" else b.shape[1]
    else:
        n = b.shape[1] if mode == "nn" else b.shape[0]
    tm = _tile(m, ROWS_PER_STEP, 16)
    cn = bw if (b_blocked and mode == "nn") else _tile(n, COLS_PER_DOT, LANE)
    has_res = res is not None

    def body(*refs):
        a_ref, b_ref = refs[0], refs[1]
        res_ref = refs[2] if has_res else None
        o_ref = refs[2 + has_res]
        if not (b_blocked and mode == "nt"):
            av = a_ref[...].astype(BF16)
        for j in range(n // cn):
            cols = pl.ds(j * cn, cn)
            if mode == "nn":
                part = _dot(av, b_ref[j] if b_blocked else b_ref[:, cols], NN)
            elif not b_blocked:
                part = _dot(av, b_ref[cols, :], NT)
            else:
                part = None
                for s in range(nb):
                    term = _dot(a_ref[:, pl.ds(s * bw, bw)], b_ref[s, cols, :], NT)
                    part = term if part is None else part + term
            if has_res:
                part = part + res_ref[:, cols]
            o_ref[:, cols] = part.astype(o_ref.dtype)

    row = lambda width: pl.BlockSpec((tm, width), lambda i: (i, 0))
    whole = pl.BlockSpec(b.shape, lambda i: (0,) * b.ndim)
    ins = [a, b] + ([res] if has_res else [])
    specs = [row(k), whole] + ([row(n)] if has_res else [])
    return _pcall(body, name=name, out_shape=jax.ShapeDtypeStruct((m, n), out_dtype), grid=(m // tm,), in_specs=specs,
                  out_specs=row(n), semantics=("parallel",), vmem_limit=VMEM_LIMIT)(*ins)


def _matmul(a, b, mode, *, name, res=None, out_dtype=F32, b_blocked=False, out_blocked=None):
    if mode != "tn":
        return _matmul_rows(a, b, mode, name=name, res=res, out_dtype=out_dtype, b_blocked=b_blocked)
    assert res is None and not b_blocked
    (t, m), (t2, n) = a.shape, b.shape
    assert t == t2, (a.shape, b.shape)
    tm = _tile(m, 1024, LANE)
    tn = _tile(n, COLS_PER_DOT, LANE)
    if out_blocked is not None:
        assert out_blocked[0] * out_blocked[1] == n
        tn = out_blocked[1]

    def body(a_ref, b_ref, o_ref):
        part = _dot(a_ref[...], b_ref[...], TN).astype(o_ref.dtype)
        if out_blocked is None:
            o_ref[...] = part
        else:
            o_ref[0] = part

    o_spec = (pl.BlockSpec((tm, tn), lambda i, j: (i, j)) if out_blocked is None
              else pl.BlockSpec((1, tm, tn), lambda i, j: (j, i, 0)))
    o_shape = (m, n) if out_blocked is None else (out_blocked[0], m, out_blocked[1])
    return _pcall(body, name=name, out_shape=jax.ShapeDtypeStruct(o_shape, out_dtype), grid=(m // tm, n // tn),
                  in_specs=[pl.BlockSpec((t, tm), lambda i, j: (0, i)), pl.BlockSpec((t, tn), lambda i, j: (0, j))],
                  out_specs=o_spec, semantics=("parallel", "parallel"), vmem_limit=VMEM_LIMIT)(a, b)


ROW_TILE = 256


def _rows(t, width, idx=0):
    return pl.BlockSpec((ROW_TILE, width), lambda i: (i, idx))


def _vec(width):
    return pl.BlockSpec((1, width), lambda i: (0, 0))


def _rmsnorm_fwd(x, g, *, name):
    t, d = x.shape

    def body(x_ref, g_ref, h_ref):
        xv = x_ref[...]
        r = lax.rsqrt(jnp.mean(xv * xv, axis=-1, keepdims=True) + EPS)
        h_ref[...] = (xv * r * g_ref[...]).astype(BF16)

    return _pcall(body, name=name, out_shape=jax.ShapeDtypeStruct((t, d), BF16), grid=(t // ROW_TILE,),
                  in_specs=[_rows(t, d), _vec(d)], out_specs=_rows(t, d), semantics=("parallel",))(x, g)


def _rmsnorm_bwd(x, g, dh, dres, *, name):
    t, d = x.shape

    def body(x_ref, g_ref, dh_ref, dres_ref, dx_ref, dg_ref):
        xv = x_ref[...]
        r = lax.rsqrt(jnp.mean(xv * xv, axis=-1, keepdims=True) + EPS)
        xhat = xv * r
        dhv = dh_ref[...].astype(F32)
        dhg = dhv * g_ref[...]
        dx_ref[...] = dres_ref[...] + r * (dhg - xhat * jnp.mean(dhg * xhat, axis=-1, keepdims=True))
        part = jnp.sum(dhv * xhat, axis=0, keepdims=True)

        @pl.when(pl.program_id(0) == 0)
        def _():
            dg_ref[...] = part

        @pl.when(pl.program_id(0) > 0)
        def _():
            dg_ref[...] += part

    return _pcall(body, name=name, out_shape=(jax.ShapeDtypeStruct((t, d), F32), jax.ShapeDtypeStruct((1, d), F32)),
                  grid=(t // ROW_TILE,), in_specs=[_rows(t, d), _vec(d), _rows(t, d), _rows(t, d)],
                  out_specs=(_rows(t, d), _vec(d)), semantics=("arbitrary",))(x, g, dh, dres)


def _swiglu_fwd(gate, up, *, name):
    t, f = gate.shape

    def body(gate_ref, up_ref, ff_ref):
        gv = gate_ref[...]
        ff_ref[...] = (gv * _sigmoid(gv) * up_ref[...]).astype(BF16)

    return _pcall(body, name=name, out_shape=jax.ShapeDtypeStruct((t, f), BF16), grid=(t // ROW_TILE,),
                  in_specs=[_rows(t, f), _rows(t, f)], out_specs=_rows(t, f), semantics=("parallel",))(gate, up)


def _swiglu_bwd(gate, up, dff, *, name):
    t, f = gate.shape

    def body(gate_ref, up_ref, dff_ref, dgate_ref, dup_ref):
        gv = gate_ref[...]
        sig = _sigmoid(gv)
        dffv = dff_ref[...]
        dgate_ref[...] = (dffv * up_ref[...] * sig * (1.0 + gv * (1.0 - sig))).astype(BF16)
        dup_ref[...] = (dffv * gv * sig).astype(BF16)

    out = jax.ShapeDtypeStruct((t, f), BF16)
    return _pcall(body, name=name, out_shape=(out, out), grid=(t // ROW_TILE,), in_specs=[_rows(t, f)] * 3,
                  out_specs=(_rows(t, f),) * 2, semantics=("parallel",))(gate, up, dff)


def _ple_fwd(x2, pgl, pp, *, name):
    t, d = x2.shape

    def body(x_ref, pgl_ref, pp_ref, o_ref):
        o_ref[...] = x_ref[...] + _sigmoid(pgl_ref[...]) * pp_ref[...]

    return _pcall(body, name=name, out_shape=jax.ShapeDtypeStruct((t, d), F32), grid=(t // ROW_TILE,),
                  in_specs=[_rows(t, d)] * 3, out_specs=_rows(t, d), semantics=("parallel",))(x2, pgl, pp)


def _ple_bwd(dx3, pgl, pp, *, name):
    t, d = dx3.shape

    def body(dx_ref, pgl_ref, pp_ref, dpgl_ref, dpp_ref):
        dxv = dx_ref[...]
        sig = _sigmoid(pgl_ref[...])
        dpp_ref[...] = (dxv * sig).astype(BF16)
        dpgl_ref[...] = (dxv * pp_ref[...] * sig * (1.0 - sig)).astype(BF16)

    return _pcall(body, name=name, out_shape=(jax.ShapeDtypeStruct((t, d), BF16),) * 2, grid=(t // ROW_TILE,),
                  in_specs=[_rows(t, d)] * 3, out_specs=(_rows(t, d),) * 2, semantics=("parallel",))(dx3, pgl, pp)


def _loss_head(x3, g, target, *, name):
    t, d = x3.shape

    def body(x_ref, g_ref, t_ref, dx_ref, dg_ref, loss_ref):
        xv = x_ref[...]
        r = lax.rsqrt(jnp.mean(xv * xv, axis=-1, keepdims=True) + EPS)
        xhat = xv * r
        gv = g_ref[...]
        err = xhat * gv - t_ref[...]
        row_loss = jnp.sum(err * err, axis=-1, keepdims=True) * (0.5 / d)
        lpart = jnp.broadcast_to(jnp.sum(row_loss, axis=0, keepdims=True), (1, LANE))
        dy = err * (1.0 / d)
        dyg = dy * gv
        dx_ref[...] = r * (dyg - xhat * jnp.mean(dyg * xhat, axis=-1, keepdims=True))
        gpart = jnp.sum(dy * xhat, axis=0, keepdims=True)

        @pl.when(pl.program_id(0) == 0)
        def _():
            dg_ref[...] = gpart
            loss_ref[...] = lpart

        @pl.when(pl.program_id(0) > 0)
        def _():
            dg_ref[...] += gpart
            loss_ref[...] += lpart

    return _pcall(body, name=name,
                  out_shape=(jax.ShapeDtypeStruct((t, d), F32), jax.ShapeDtypeStruct((1, d), F32), jax.ShapeDtypeStruct((1, LANE), F32)),
                  grid=(t // ROW_TILE,), in_specs=[_rows(t, d), _vec(d), _rows(t, d)],
                  out_specs=(_rows(t, d), _vec(d), _vec(LANE)), semantics=("arbitrary",))(x3, g, target)


def _shift_down(x, d):
    if d == 0:
        return x
    row = lax.broadcasted_iota(jnp.int32, x.shape, 0)
    return jnp.where(row >= d, pltpu.roll(x, d, 0), 0.0)


def _shift_up(x, d):
    if d == 0:
        return x
    t = x.shape[0]
    row = lax.broadcasted_iota(jnp.int32, x.shape, 0)
    return jnp.where(row < t - d, pltpu.roll(x, t - d, 0), 0.0)


def _colsum(x):
    return jnp.sum(x, axis=0, keepdims=True)


def _col(t, idx_fn):
    return pl.BlockSpec((t, LANE), idx_fn)


def _conv_fwd(x, w_ref, taps):
    acc = None
    for j in range(taps):
        term = w_ref[pl.ds(j, 1), :] * _shift_down(x, taps - 1 - j)
        acc = term if acc is None else acc + term
    return acc


def _conv_bwd(x, dy, w_ref, dw_ref, taps):
    dx = None
    for j in range(taps):
        term = w_ref[pl.ds(j, 1), :] * _shift_up(dy, taps - 1 - j)
        dx = term if dx is None else dx + term
        dw_ref[pl.ds(j, 1), :] = _colsum(dy * _shift_down(x, taps - 1 - j))
    return dx


def _qkv_prep_fwd(proj, conv_w, *, name):
    t = proj.shape[0]
    scale = HEAD_DIM ** -0.5

    def body(x_ref, w_ref, o_ref):
        j = pl.program_id(0)
        c = _conv_fwd(x_ref[...], w_ref, QKV_TAPS)
        s = c * _sigmoid(c)
        r = lax.rsqrt(jnp.sum(s * s, axis=-1, keepdims=True) + EPS)
        f = jnp.where(j < 2 * HEADS, r, 1.0) * jnp.where(j < HEADS, scale, 1.0)
        o_ref[0] = s * f

    return _pcall(body, name=name, out_shape=jax.ShapeDtypeStruct((3 * HEADS, t, LANE), F32), grid=(3 * HEADS,),
                  in_specs=[_col(t, lambda j: (0, j)), pl.BlockSpec((QKV_TAPS, LANE), lambda j: (0, j))],
                  out_specs=pl.BlockSpec((1, t, LANE), lambda j: (j, 0, 0)), semantics=("parallel",),
                  vmem_limit=VMEM_LIMIT)(proj, conv_w)


def _qkv_prep_bwd(proj, conv_w, dqkv, *, name):
    t = proj.shape[0]
    scale = HEAD_DIM ** -0.5

    def body(x_ref, w_ref, d_ref, dx_ref, dw_ref):
        j = pl.program_id(0)
        xv = x_ref[...]
        c = _conv_fwd(xv, w_ref, QKV_TAPS)
        sig = _sigmoid(c)
        s = c * sig
        r = lax.rsqrt(jnp.sum(s * s, axis=-1, keepdims=True) + EPS)
        n0 = s * r
        dv = d_ref[0]
        dn0 = dv * jnp.where(j < HEADS, scale, 1.0)
        ds_norm = r * (dn0 - n0 * jnp.sum(dn0 * n0, axis=-1, keepdims=True))
        ds = jnp.where(j < 2 * HEADS, ds_norm, dv)
        dc = ds * sig * (1.0 + c * (1.0 - sig))
        dx_ref[...] = _conv_bwd(xv, dc, w_ref, dw_ref, QKV_TAPS).astype(BF16)

    return _pcall(body, name=name,
                  out_shape=(jax.ShapeDtypeStruct((t, 3 * A_DIM), BF16), jax.ShapeDtypeStruct((QKV_TAPS, 3 * A_DIM), F32)),
                  grid=(3 * HEADS,),
                  in_specs=[_col(t, lambda j: (0, j)), pl.BlockSpec((QKV_TAPS, LANE), lambda j: (0, j)),
                            pl.BlockSpec((1, t, LANE), lambda j: (j, 0, 0))],
                  out_specs=(_col(t, lambda j: (0, j)), pl.BlockSpec((QKV_TAPS, LANE), lambda j: (0, j))),
                  semantics=("parallel",), vmem_limit=VMEM_LIMIT)(proj, conv_w, dqkv)


def _lane_pick(x, lane_idx, lane):
    return jnp.broadcast_to(jnp.sum(jnp.where(lane == lane_idx, x, 0.0), axis=-1, keepdims=True), x.shape)


def _gates_fwd(proj, alog, dtb, *, name):
    t = proj.shape[0]

    def body(x_ref, alog_ref, dtb_ref, g_ref, b_ref):
        xv = x_ref[...]
        lane = lax.broadcasted_iota(jnp.int32, xv.shape, 1)
        gall = -jnp.exp(alog_ref[...]) * _softplus(xv + dtb_ref[...])
        ball = _sigmoid(xv)
        for h in range(HEADS):
            g_ref[h] = _lane_pick(gall, h, lane)
            b_ref[h] = _lane_pick(ball, HEADS + h, lane)

    out = jax.ShapeDtypeStruct((HEADS, t, LANE), F32)
    whole = pl.BlockSpec((HEADS, t, LANE), lambda i: (0, 0, 0))
    return _pcall(body, name=name, out_shape=(out, out), grid=(1,),
                  in_specs=[_col(t, lambda i: (0, AB_COL // LANE)), _vec(LANE), _vec(LANE)], out_specs=(whole, whole),
                  semantics=("arbitrary",), vmem_limit=VMEM_LIMIT)(proj, alog, dtb)


def _gates_bwd(proj, alog, dtb, dg, dbeta, *, name):
    t = proj.shape[0]

    def body(x_ref, alog_ref, dtb_ref, dg_ref, db_ref, dab_ref, dalog_ref, ddtb_ref):
        xv = x_ref[...]
        lane = lax.broadcasted_iota(jnp.int32, xv.shape, 1)
        lane1 = lax.broadcasted_iota(jnp.int32, (1, LANE), 1)
        z = xv + dtb_ref[...]
        nea = -jnp.exp(alog_ref[...])
        da_f = nea * _sigmoid(z)
        g_f = nea * _softplus(z)
        ball = _sigmoid(xv)
        db_f = ball * (1.0 - ball)
        dab = jnp.zeros_like(xv)
        dalog = jnp.zeros((1, LANE), F32)
        for h in range(HEADS):
            dgh = dg_ref[h]
            dab = dab + jnp.where(lane == h, dgh * da_f, 0.0) + jnp.where(lane == HEADS + h, db_ref[h] * db_f, 0.0)
            dalog = dalog + jnp.where(lane1 == h, _colsum(dgh * g_f), 0.0)
        dab_ref[...] = dab.astype(BF16)
        dalog_ref[...] = dalog
        ddtb_ref[...] = jnp.where(lane1 < HEADS, _colsum(dab), 0.0)

    whole = pl.BlockSpec((HEADS, t, LANE), lambda i: (0, 0, 0))
    vec = jax.ShapeDtypeStruct((1, LANE), F32)
    return _pcall(body, name=name, out_shape=(jax.ShapeDtypeStruct((t, LANE), BF16), vec, vec), grid=(1,),
                  in_specs=[_col(t, lambda i: (0, AB_COL // LANE)), _vec(LANE), _vec(LANE), whole, whole],
                  out_specs=(_col(t, lambda i: (0, 0)), _vec(LANE), _vec(LANE)), semantics=("arbitrary",),
                  vmem_limit=VMEM_LIMIT)(proj, alog, dtb, dg, dbeta)


Z_COL = 3 * A_DIM // LANE


def _apost_fwd(o, proj, gn, *, name):
    t = proj.shape[0]

    def body(o_ref, z_ref, gn_ref, y_ref):
        ov = o_ref[0]
        z = z_ref[...]
        r = lax.rsqrt(jnp.mean(ov * ov, axis=-1, keepdims=True) + EPS)
        y_ref[...] = (ov * r * gn_ref[...] * (z * _sigmoid(z))).astype(BF16)

    return _pcall(body, name=name, out_shape=jax.ShapeDtypeStruct((t, A_DIM), BF16), grid=(HEADS,),
                  in_specs=[pl.BlockSpec((1, t, LANE), lambda h: (h, 0, 0)), _col(t, lambda h: (0, Z_COL + h)),
                            pl.BlockSpec((1, LANE), lambda h: (0, 0))],
                  out_specs=_col(t, lambda h: (0, h)), semantics=("parallel",), vmem_limit=VMEM_LIMIT)(o, proj, gn)


def _apost_bwd(o, proj, gn, dmixed, *, name):
    t = proj.shape[0]

    def body(o_ref, z_ref, gn_ref, d_ref, do_ref, dz_ref, dgn_ref):
        ov = o_ref[0]
        z = z_ref[...]
        gnv = gn_ref[...]
        dv = d_ref[...]
        r = lax.rsqrt(jnp.mean(ov * ov, axis=-1, keepdims=True) + EPS)
        ohat = ov * r
        sig = _sigmoid(z)
        dy = dv * (z * sig)
        dz_ref[...] = (dv * ohat * gnv * sig * (1.0 + z * (1.0 - sig))).astype(BF16)
        dyo = dy * gnv
        do_ref[0] = r * (dyo - ohat * jnp.mean(dyo * ohat, axis=-1, keepdims=True))
        part = _colsum(dy * ohat)

        @pl.when(pl.program_id(0) == 0)
        def _():
            dgn_ref[...] = part

        @pl.when(pl.program_id(0) > 0)
        def _():
            dgn_ref[...] += part

    return _pcall(body, name=name,
                  out_shape=(jax.ShapeDtypeStruct((HEADS, t, LANE), F32), jax.ShapeDtypeStruct((t, A_DIM), BF16),
                             jax.ShapeDtypeStruct((1, LANE), F32)),
                  grid=(HEADS,),
                  in_specs=[pl.BlockSpec((1, t, LANE), lambda h: (h, 0, 0)), _col(t, lambda h: (0, Z_COL + h)),
                            pl.BlockSpec((1, LANE), lambda h: (0, 0)), _col(t, lambda h: (0, h))],
                  out_specs=(pl.BlockSpec((1, t, LANE), lambda h: (h, 0, 0)), _col(t, lambda h: (0, h)),
                             pl.BlockSpec((1, LANE), lambda h: (0, 0))),
                  semantics=("arbitrary",), vmem_limit=VMEM_LIMIT)(o, proj, gn, dmixed)


POOL_COL = (AB_COL + LANE) // LANE
CB_COL = POOL_COL + POOL_DIM // LANE
CC_COL = CB_COL + CONV_DIM // LANE
CH_COL = CC_COL + CONV_DIM // LANE
MAX_WIN_LOG2 = 4


def _window_sums(x, shift):
    sums = []
    cur = x
    for k in range(MAX_WIN_LOG2):
        cur = cur + shift(cur, 1 << k)
        sums.append(cur)
    return sums


def _pick_window(sums, win):
    out = sums[-1]
    for k in range(MAX_WIN_LOG2 - 2, -1, -1):
        out = jnp.where(win == float(2 << k), sums[k], out)
    return out


def _pool_counts(shape, win):
    row = lax.broadcasted_iota(jnp.int32, shape, 0).astype(F32)
    return jnp.minimum(row + 1.0, win)


def _pool_fwd(proj, win, wbd, scale, *, name):
    t = proj.shape[0]

    def body(x_ref, win_ref, w_ref, s_ref, y_ref):
        xv = x_ref[...]
        winv = win_ref[...]
        pooled = _pick_window(_window_sums(xv, _shift_down), winv) / _pool_counts(xv.shape, winv) - xv
        y_ref[...] = (_dot(pooled, w_ref[0], NN) * s_ref[...]).astype(BF16)

    nb = POOL_DIM // LANE
    vec = pl.BlockSpec((1, LANE), lambda b: (0, b))
    return _pcall(body, name=name, out_shape=jax.ShapeDtypeStruct((t, POOL_DIM), BF16), grid=(nb,),
                  in_specs=[_col(t, lambda b: (0, POOL_COL + b)), vec, pl.BlockSpec((1, LANE, LANE), lambda b: (b, 0, 0)), vec],
                  out_specs=_col(t, lambda b: (0, b)), semantics=("parallel",), vmem_limit=VMEM_LIMIT)(proj, win, wbd, scale)


def _pool_bwd(proj, win, wbd, scale, dmixed, *, name):
    t = proj.shape[0]

    def body(x_ref, win_ref, w_ref, s_ref, d_ref, dx_ref, dw_ref, ds_ref):
        xv = x_ref[...]
        winv = win_ref[...]
        cnt = _pool_counts(xv.shape, winv)
        pooled = _pick_window(_window_sums(xv, _shift_down), winv) / cnt - xv
        dv = d_ref[...]
        ds_ref[...] = _colsum(dv * _dot(pooled, w_ref[0], NN))
        dy0 = dv * s_ref[...]
        dw_ref[0] = _dot(pooled, dy0, TN)
        dpooled = _dot(dy0, w_ref[0], NT)
        dmean = dpooled / cnt
        dx_ref[...] = (_pick_window(_window_sums(dmean, _shift_up), winv) - dpooled).astype(BF16)

    nb = POOL_DIM // LANE
    vec = pl.BlockSpec((1, LANE), lambda b: (0, b))
    mat = pl.BlockSpec((1, LANE, LANE), lambda b: (b, 0, 0))
    first = A_DIM // LANE
    return _pcall(body, name=name,
                  out_shape=(jax.ShapeDtypeStruct((t, POOL_DIM), BF16), jax.ShapeDtypeStruct((nb, LANE, LANE), F32),
                             jax.ShapeDtypeStruct((1, POOL_DIM), F32)),
                  grid=(nb,),
                  in_specs=[_col(t, lambda b: (0, POOL_COL + b)), vec, mat, vec, _col(t, lambda b: (0, first + b))],
                  out_specs=(_col(t, lambda b: (0, b)), mat, vec), semantics=("parallel",),
                  vmem_limit=VMEM_LIMIT)(proj, win, wbd, scale, dmixed)


def _sconv_fwd(proj, w, *, name):
    t = proj.shape[0]

    def body(cb_ref, cc_ref, ch_ref, w_ref, y_ref):
        y_ref[...] = (cb_ref[...] * _conv_fwd(cc_ref[...] * ch_ref[...], w_ref, CONV_TAPS)).astype(BF16)

    nb = CONV_DIM // LANE
    return _pcall(body, name=name, out_shape=jax.ShapeDtypeStruct((t, CONV_DIM), BF16), grid=(nb,),
                  in_specs=[_col(t, lambda b: (0, CB_COL + b)), _col(t, lambda b: (0, CC_COL + b)),
                            _col(t, lambda b: (0, CH_COL + b)), pl.BlockSpec((CONV_TAPS, LANE), lambda b: (0, b))],
                  out_specs=_col(t, lambda b: (0, b)), semantics=("parallel",), vmem_limit=VMEM_LIMIT)(proj, proj, proj, w)


def _sconv_bwd(proj, w, dmixed, *, name):
    t = proj.shape[0]

    def body(cb_ref, cc_ref, ch_ref, w_ref, d_ref, dcb_ref, dcc_ref, dch_ref, dw_ref):
        cc = cc_ref[...]
        ch = ch_ref[...]
        u = cc * ch
        dv = d_ref[...]
        dcb_ref[...] = (dv * _conv_fwd(u, w_ref, CONV_TAPS)).astype(BF16)
        du = _conv_bwd(u, dv * cb_ref[...], w_ref, dw_ref, CONV_TAPS)
        dcc_ref[...] = (du * ch).astype(BF16)
        dch_ref[...] = (du * cc).astype(BF16)

    nb = CONV_DIM // LANE
    first = (A_DIM + POOL_DIM) // LANE
    act = jax.ShapeDtypeStruct((t, CONV_DIM), BF16)
    wspec = pl.BlockSpec((CONV_TAPS, LANE), lambda b: (0, b))
    ospec = _col(t, lambda b: (0, b))
    return _pcall(body, name=name, out_shape=(act, act, act, jax.ShapeDtypeStruct((CONV_TAPS, CONV_DIM), F32)), grid=(nb,),
                  in_specs=[_col(t, lambda b: (0, CB_COL + b)), _col(t, lambda b: (0, CC_COL + b)),
                            _col(t, lambda b: (0, CH_COL + b)), wspec, _col(t, lambda b: (0, first + b))],
                  out_specs=(ospec, ospec, ospec, wspec), semantics=("parallel",),
                  vmem_limit=VMEM_LIMIT)(proj, proj, proj, w, dmixed)


def _chunk_masks():
    r = lax.broadcasted_iota(jnp.int32, (CHUNK, CHUNK), 0)
    c = lax.broadcasted_iota(jnp.int32, (CHUNK, CHUNK), 1)
    return r >= c, r > c, jnp.where(r == c, 1.0, 0.0).astype(F32)


def _split(a):
    hi = a.astype(BF16)
    return hi, (a - hi.astype(F32)).astype(BF16)


def _dot_split(a, b, dims):
    (ah, al), (bh, bl) = a, b
    return _dot(ah, bh, dims) + _dot(ah, bl, dims) + _dot(al, bh, dims)


def _tri_inv(lows, eye):
    xs = [eye - low for low in lows]
    ps = [_split(low) for low in lows]
    ps = [_split(_dot_split(p, p, NN)) for p in ps]
    for i in range(5):
        xs = [x + _dot_split(_split(x), p, NN) for x, p in zip(xs, ps)]
        if i < 4:
            ps = [_split(_dot_split(p, p, NN)) for p in ps]
    return xs


def _prefix_sum_rows(x):
    for k in range(6):
        x = x + _shift_down(x, 1 << k)
    return x


def _suffix_sum_rows(x):
    for k in range(6):
        x = x + _shift_up(x, 1 << k)
    return x


def _chunk_decay(g, incl):
    gcb = _prefix_sum_rows(g)
    gtot = _colsum(g)
    col = gcb[:, :CHUNK]
    row = gcb.T[:CHUNK, :]
    decay = jnp.exp(jnp.where(incl, col - row, -1e30))
    return gcb, gtot, decay


CHUNKS_PER_STEP = 2


def _heads_of(ref, base, rows):
    return [ref[base + h, rows, :] for h in range(HEADS)]


def _chunk_rows(j):
    return pl.ds(j * CHUNK, CHUNK)


def _deltanet_prep(qkv, g, beta, *, name):
    t = qkv.shape[1]
    n_chunks = t // CHUNK
    per = CHUNKS_PER_STEP
    probs = [(j, h) for j in range(per) for h in range(HEADS)]

    def body(qkv_ref, g_ref, b_ref, u_ref, w_ref, qg_ref, kg_ref, attn_ref, tm_ref):
        incl, strict, eye = _chunk_masks()
        q = [qkv_ref[h, _chunk_rows(j), :] for j, h in probs]
        k = [qkv_ref[HEADS + h, _chunk_rows(j), :] for j, h in probs]
        v = [qkv_ref[2 * HEADS + h, _chunk_rows(j), :] for j, h in probs]
        bv = [b_ref[h, _chunk_rows(j), :] for j, h in probs]
        dec = [_chunk_decay(g_ref[h, _chunk_rows(j), :], incl) for j, h in probs]
        kb = [a * b for a, b in zip(k, bv)]
        low = [jnp.where(strict, _dot(a, b, NT) * d[2], 0.0) for a, b, d in zip(kb, k, dec)]
        tm = _tri_inv(low, eye)
        egc = [jnp.exp(d[0]) for d in dec]
        u = [_dot(m, a * b, NN) for m, a, b in zip(tm, v, bv)]
        w = [_dot(m, a * e, NN) for m, a, e in zip(tm, kb, egc)]
        attn = [_dot(a, b, NT) * d[2] for a, b, d in zip(q, k, dec)]
        for i, (j, h) in enumerate(probs):
            rows = _chunk_rows(j)
            u_ref[h, rows, :] = u[i]
            w_ref[h, rows, :] = w[i].astype(BF16)
            qg_ref[h, rows, :] = (q[i] * egc[i]).astype(BF16)
            kg_ref[h, rows, :] = (k[i] * jnp.exp(dec[i][1] - dec[i][0])).astype(BF16)
            attn_ref[j, h] = attn[i].astype(BF16)
            tm_ref[j, h] = tm[i]

    act = lambda heads: pl.BlockSpec((heads, per * CHUNK, LANE), lambda n: (0, n, 0))
    mat = pl.BlockSpec((per, HEADS, CHUNK, CHUNK), lambda n: (n, 0, 0, 0))
    return _pcall(
        body, name=name,
        out_shape=(jax.ShapeDtypeStruct((HEADS, t, LANE), F32),) + (jax.ShapeDtypeStruct((HEADS, t, LANE), BF16),) * 3
        + (jax.ShapeDtypeStruct((n_chunks, HEADS, CHUNK, CHUNK), BF16), jax.ShapeDtypeStruct((n_chunks, HEADS, CHUNK, CHUNK), F32)),
        grid=(n_chunks // per,), in_specs=[act(3 * HEADS), act(HEADS), act(HEADS)],
        out_specs=(act(HEADS),) * 4 + (mat, mat), semantics=("parallel",), vmem_limit=VMEM_LIMIT)(qkv, g, beta)


SCAN_CHUNKS_PER_STEP = 4


def _deltanet_scan(u, w, qg, kg, attn, g, *, name):
    t = u.shape[1]
    n_chunks = t // CHUNK
    per = SCAN_CHUNKS_PER_STEP

    def body(u_ref, w_ref, qg_ref, kg_ref, attn_ref, g_ref, o_ref, vn_ref, st_ref, s_ref):
        @pl.when(pl.program_id(0) == 0)
        def _():
            s_ref[...] = jnp.zeros_like(s_ref)

        for j in range(per):
            rows = _chunk_rows(j)
            s = [s_ref[h] for h in range(HEADS)]
            vn = [u_ref[h, rows, :] - _dot(w_ref[h, rows, :], s[h], NN) for h in range(HEADS)]
            o = [_dot(qg_ref[h, rows, :], s[h], NN) + _dot(attn_ref[j, h], vn[h], NN) for h in range(HEADS)]
            eg = [jnp.exp(_colsum(g_ref[h, rows, :])) for h in range(HEADS)]
            for h in range(HEADS):
                st_ref[j, h] = s[h]
                s_ref[h] = s[h] * eg[h] + _dot(kg_ref[h, rows, :], vn[h], TN)
                o_ref[h, rows, :] = o[h]
                vn_ref[h, rows, :] = vn[h]

    act = pl.BlockSpec((HEADS, per * CHUNK, LANE), lambda n: (0, n, 0))
    out = jax.ShapeDtypeStruct((HEADS, t, LANE), F32)
    return _pcall(
        body, name=name, out_shape=(out, out, jax.ShapeDtypeStruct((n_chunks, HEADS, LANE, LANE), F32)), grid=(n_chunks // per,),
        in_specs=[act] * 4 + [pl.BlockSpec((per, HEADS, CHUNK, CHUNK), lambda n: (n, 0, 0, 0)), act],
        out_specs=(act, act, pl.BlockSpec((per, HEADS, LANE, LANE), lambda n: (n, 0, 0, 0))),
        scratch_shapes=[pltpu.VMEM((HEADS, LANE, LANE), F32)], semantics=("arbitrary",))(u, w, qg, kg, attn, g)


def _deltanet_bscan(w, qg, kg, attn, g, do, *, name):
    t = w.shape[1]
    n_chunks = t // CHUNK
    per = SCAN_CHUNKS_PER_STEP
    steps = n_chunks // per

    def body(w_ref, qg_ref, kg_ref, attn_ref, g_ref, do_ref, dvn_ref, dsn_ref, ds_ref):
        @pl.when(pl.program_id(0) == 0)
        def _():
            ds_ref[...] = jnp.zeros_like(ds_ref)

        for j in reversed(range(per)):
            rows = _chunk_rows(j)
            dsn = [ds_ref[h] for h in range(HEADS)]
            dov = [do_ref[h, rows, :] for h in range(HEADS)]
            dvn = [_dot(attn_ref[j, h], dov[h], TN) + _dot(kg_ref[h, rows, :], dsn[h], NN) for h in range(HEADS)]
            eg = [jnp.exp(_colsum(g_ref[h, rows, :])) for h in range(HEADS)]
            for h in range(HEADS):
                dsn_ref[j, h] = dsn[h]
                ds_ref[h] = _dot(qg_ref[h, rows, :], dov[h], TN) + eg[h] * dsn[h] - _dot(w_ref[h, rows, :], dvn[h], TN)
                dvn_ref[h, rows, :] = dvn[h]

    act = pl.BlockSpec((HEADS, per * CHUNK, LANE), lambda n: (0, steps - 1 - n, 0))
    return _pcall(
        body, name=name,
        out_shape=(jax.ShapeDtypeStruct((HEADS, t, LANE), F32), jax.ShapeDtypeStruct((n_chunks, HEADS, LANE, LANE), F32)),
        grid=(steps,),
        in_specs=[act] * 3 + [pl.BlockSpec((per, HEADS, CHUNK, CHUNK), lambda n: (steps - 1 - n, 0, 0, 0)), act, act],
        out_specs=(act, pl.BlockSpec((per, HEADS, LANE, LANE), lambda n: (steps - 1 - n, 0, 0, 0))),
        scratch_shapes=[pltpu.VMEM((HEADS, LANE, LANE), F32)], semantics=("arbitrary",))(w, qg, kg, attn, g, do)


def _sum_all(x):
    return jnp.sum(jnp.sum(x, axis=1, keepdims=True), axis=0, keepdims=True)


def _rowsum(x):
    return jnp.sum(x, axis=1, keepdims=True)


def _deltanet_post(qkv, g, beta, tmats, states, dstates, do, dvn, vn, *, name):
    t = qkv.shape[1]
    n_chunks = t // CHUNK
    per = CHUNKS_PER_STEP
    probs = [(j, h) for j in range(per) for h in range(HEADS)]

    def body(qkv_ref, g_ref, b_ref, tm_ref, st_ref, dsn_ref, do_ref, dvn_ref, vn_ref, dqkv_ref, dg_ref, db_ref):
        incl, strict, _ = _chunk_masks()
        ones = jnp.ones((CHUNK, LANE), BF16)
        last_row = lax.broadcasted_iota(jnp.int32, (CHUNK, LANE), 0) == CHUNK - 1
        z = lambda f, *cols: [f(*a) for a in zip(*cols)]
        q = [qkv_ref[h, _chunk_rows(j), :] for j, h in probs]
        k = [qkv_ref[HEADS + h, _chunk_rows(j), :] for j, h in probs]
        v = [qkv_ref[2 * HEADS + h, _chunk_rows(j), :] for j, h in probs]
        bv = [b_ref[h, _chunk_rows(j), :] for j, h in probs]
        dov = [do_ref[h, _chunk_rows(j), :] for j, h in probs]
        dvn_ = [dvn_ref[h, _chunk_rows(j), :] for j, h in probs]
        vn_ = [vn_ref[h, _chunk_rows(j), :] for j, h in probs]
        tm = [tm_ref[j, h] for j, h in probs]
        s = [st_ref[j, h] for j, h in probs]
        dsn = [dsn_ref[j, h] for j, h in probs]
        dec = [_chunk_decay(g_ref[h, _chunk_rows(j), :], incl) for j, h in probs]
        decay = [d[2] for d in dec]
        egc = [jnp.exp(d[0]) for d in dec]
        ekg = [jnp.exp(d[1] - d[0]) for d in dec]
        kb = z(lambda a, b: a * b, k, bv)
        vb = z(lambda a, b: a * b, v, bv)
        kbg = z(lambda a, b: a * b, kb, egc)
        qg = z(lambda a, b: a * b, q, egc)
        kg = z(lambda a, b: a * b, k, ekg)
        kk = z(lambda a, b: _dot(a, b, NT), kb, k)
        qk = z(lambda a, b: _dot(a, b, NT), q, k)
        dattn = z(lambda a, b: jnp.where(incl, _dot(a, b, NT), 0.0), dov, vn_)
        dqg = z(lambda a, b: _dot(a, b, NT), dov, s)
        dkg = z(lambda a, b: _dot(a, b, NT), vn_, dsn)
        dglast = z(lambda a, b, c, d, e: _sum_all(a * b) * jnp.exp(e[1]) + _sum_all(c * d), s, dsn, dkg, kg, dec)
        dw = z(lambda a, b: -_dot(a, b, NT), dvn_, s)
        dtm = z(lambda a, b, c, d: _dot(a, b, NT) + _dot(c, d, NT), dvn_, vb, dw, kbg)
        dvb = z(lambda a, b: _dot(a, b, TN), tm, dvn_)
        dkbg = z(lambda a, b: _dot(a, b, TN), tm, dw)
        dlow = z(lambda a, b: jnp.where(strict, -_dot(_dot(a, b, TN), a, NT), 0.0), tm, dtm)
        dkk = z(lambda a, b: a * b, dlow, decay)
        dqk = z(lambda a, b: a * b, dattn, decay)
        dkb = z(lambda a, b, c, d: _dot(a, b, NN) + c * d, dkk, k, dkbg, egc)
        dk = z(lambda a, b, c, d, e, f, g_, h_: _dot(a, b, TN) + _dot(c, d, TN) + e * f + g_ * h_, dkk, kb, dqk, q, dkg, ekg, dkb, bv)
        dq = z(lambda a, b, c, d: _dot(a, b, NN) + c * d, dqk, k, dqg, egc)
        m = z(lambda a, b, c, d, e: (a * b + c * d) * e, dlow, kk, dattn, qk, decay)
        mcol = [_dot(mh, ones, TN) + _dot(ml, ones, TN) for mh, ml in (_split(a) for a in m)]
        for i, (j, h) in enumerate(probs):
            rows = _chunk_rows(j)
            dqkv_ref[h, rows, :] = dq[i]
            dqkv_ref[HEADS + h, rows, :] = dk[i]
            dqkv_ref[2 * HEADS + h, rows, :] = dvb[i] * bv[i]
            db_ref[h, rows, :] = jnp.broadcast_to(_rowsum(dkb[i] * k[i] + dvb[i] * v[i]), (CHUNK, LANE))
            dgc = (_rowsum(dqg[i] * qg[i] + dkbg[i] * kbg[i] - dkg[i] * kg[i]) + _rowsum(m[i]) - mcol[i]
                   + jnp.where(last_row, dglast[i], 0.0))
            dg_ref[h, rows, :] = _suffix_sum_rows(dgc)

    act = lambda heads: pl.BlockSpec((heads, per * CHUNK, LANE), lambda n: (0, n, 0))
    mat = lambda d: pl.BlockSpec((per, HEADS, d, d), lambda n: (n, 0, 0, 0))
    out = jax.ShapeDtypeStruct((HEADS, t, LANE), F32)
    return _pcall(
        body, name=name, out_shape=(jax.ShapeDtypeStruct((3 * HEADS, t, LANE), F32), out, out), grid=(n_chunks // per,),
        in_specs=[act(3 * HEADS), act(HEADS), act(HEADS), mat(CHUNK), mat(LANE), mat(LANE), act(HEADS), act(HEADS), act(HEADS)],
        out_specs=(act(3 * HEADS), act(HEADS), act(HEADS)), semantics=("parallel",),
        vmem_limit=VMEM_LIMIT)(qkv, g, beta, tmats, states, dstates, do, dvn, vn)


ANY = pl.BlockSpec(memory_space=pl.ANY)
PEERS = N_DEV - 1


def _all_gather(arrays, *, name):
    n = len(arrays)

    def body(*refs):
        ins, outs = refs[:n], refs[n:2 * n]
        send_sems, recv_sems, local_sems = refs[2 * n:]
        x, y, c = lax.axis_index("x"), lax.axis_index("y"), lax.axis_index("c")
        me, sibling = (x, y, c), (x, y, 1 - c)
        chips = [(1 - x, y), (x, 1 - y), (1 - x, 1 - y)]

        def copy(a, k, block, to, src=None):
            dst = outs[a].at[4 * block[0] + 2 * block[1] + block[2]]
            return pltpu.make_async_remote_copy(src_ref=dst if src is None else src, dst_ref=dst, send_sem=send_sems.at[a * PEERS + k],
                                                recv_sem=recv_sems.at[a * PEERS + k], device_id=to, device_id_type=MESH)

        local = [pltpu.make_async_copy(ins[a], outs[a].at[4 * x + 2 * y + c], local_sems.at[a]) for a in range(n)]
        for cp in local:
            cp.start()
        first = []
        for a in range(n):
            first.append(copy(a, 0, me, sibling, src=ins[a]))
            first += [copy(a, 1 + j, me, (*chip, c), src=ins[a]) for j, chip in enumerate(chips)]
        for cp in first:
            cp.start()
        passed = []
        for a in range(n):
            for j, chip in enumerate(chips):
                copy(a, 1 + j, (*chip, c), me).wait_recv()
                fwd = copy(a, 4 + j, (*chip, c), sibling)
                fwd.start()
                passed.append(fwd)
        for a in range(n):
            copy(a, 0, sibling, me).wait_recv()
            for j, chip in enumerate(chips):
                copy(a, 4 + j, (*chip, 1 - c), me).wait_recv()
        for cp in first + passed:
            cp.wait_send()
        for cp in local:
            cp.wait()

    return _pcall(body, name=name, out_shape=tuple(jax.ShapeDtypeStruct((N_DEV,) + a.shape, a.dtype) for a in arrays),
                  in_specs=[ANY] * n, out_specs=(ANY,) * n,
                  scratch_shapes=[pltpu.SemaphoreType.DMA((n * PEERS,)), pltpu.SemaphoreType.DMA((n * PEERS,)),
                                  pltpu.SemaphoreType.DMA((n,))])(*arrays)


def _exchange_blocks(arrays, *, name):
    n = len(arrays)

    def body(*refs):
        ins, outs = refs[:n], refs[n:2 * n]
        send_sems, recv_sems, local_sems = refs[2 * n:]
        x, y, c = lax.axis_index("x"), lax.axis_index("y"), lax.axis_index("c")
        mine = 4 * x + 2 * y + c
        copies = []
        for a in range(n):
            lc = pltpu.make_async_copy(ins[a].at[mine], outs[a].at[mine], local_sems.at[a])
            lc.start()
            copies.append(lc)
            for k in range(1, N_DEV):
                px = 1 - x if k & 4 else x
                py = 1 - y if k & 2 else y
                pc = 1 - c if k & 1 else c
                cp = pltpu.make_async_remote_copy(src_ref=ins[a].at[4 * px + 2 * py + pc], dst_ref=outs[a].at[mine],
                                                  send_sem=send_sems.at[a * PEERS + k - 1], recv_sem=recv_sems.at[a * PEERS + k - 1],
                                                  device_id=(px, py, pc), device_id_type=MESH)
                cp.start()
                copies.append(cp)
        for cp in copies:
            cp.wait()

    return _pcall(body, name=name, out_shape=tuple(jax.ShapeDtypeStruct(a.shape, a.dtype) for a in arrays),
                  in_specs=[ANY] * n, out_specs=(ANY,) * n,
                  scratch_shapes=[pltpu.SemaphoreType.DMA((n * PEERS,)), pltpu.SemaphoreType.DMA((n * PEERS,)),
                                  pltpu.SemaphoreType.DMA((n,))])(*arrays)


HBM = pl.BlockSpec(memory_space=pltpu.HBM)
SEM = pl.BlockSpec(memory_space=pltpu.SEMAPHORE)
EFFECT = pltpu.SideEffectType.DATAFLOW_SIDE_EFFECTING


def _direct_copies(srcs, lands, send_sems, recv_sems, local_sems, scatter):
    x, y, c = lax.axis_index("x"), lax.axis_index("y"), lax.axis_index("c")
    mine = 4 * x + 2 * y + c
    copies = []
    for a, (src, land) in enumerate(zip(srcs, lands)):
        copies.append(pltpu.make_async_copy(src.at[mine] if scatter else src, land.at[mine], local_sems.at[a]))
        for k in range(1, N_DEV):
            px = 1 - x if k & 4 else x
            py = 1 - y if k & 2 else y
            pc = 1 - c if k & 1 else c
            copies.append(pltpu.make_async_remote_copy(
                src_ref=src.at[4 * px + 2 * py + pc] if scatter else src, dst_ref=land.at[mine],
                send_sem=send_sems.at[a * PEERS + k - 1], recv_sem=recv_sems.at[a * PEERS + k - 1],
                device_id=(px, py, pc), device_id_type=MESH))
    return copies


def _exchange_start(groups, scatter, *, name):
    srcs = [s for group in groups for s in group]
    n = len(srcs)
    sizes = [len(group) for group in groups]
    starts = [sum(sizes[:g]) for g in range(len(groups))]
    land_shapes = [s.shape if scatter else (N_DEV,) + s.shape for s in srcs]

    def body(*refs):
        srcs_, lands = refs[:n], refs[n:2 * n]
        token = refs[-1]
        for g, (at, size) in enumerate(zip(starts, sizes)):
            send_sems, recv_sems, local_sems = refs[2 * n + 3 * g:2 * n + 3 * g + 3]
            for cp in _direct_copies(srcs_[at:at + size], lands[at:at + size], send_sems, recv_sems, local_sems, scatter):
                cp.start()
        token[...] = jnp.zeros_like(token)

    sems = tuple(t for size in sizes for t in (pltpu.SemaphoreType.DMA((size * PEERS,)), pltpu.SemaphoreType.DMA((size * PEERS,)),
                                               pltpu.SemaphoreType.DMA((size,))))
    thru = tuple(pltpu.HBM(s.shape, s.dtype) for s in srcs) + tuple(pltpu.HBM(shp, s.dtype) for shp, s in zip(land_shapes, srcs))
    ins = [pltpu.with_memory_space_constraint(s, pltpu.HBM) for s in srcs]
    ins += [pltpu.with_memory_space_constraint(lax.empty(shp, s.dtype), pltpu.HBM) for shp, s in zip(land_shapes, srcs)]
    out = pl.pallas_call(
        body, name=name, out_shape=sems + thru + (jax.ShapeDtypeStruct((SUBLANE, LANE), F32),), in_specs=[HBM] * (2 * n),
        out_specs=(SEM,) * len(sems) + (HBM,) * (2 * n) + (pl.BlockSpec(memory_space=pltpu.VMEM),),
        input_output_aliases={i: len(sems) + i for i in range(2 * n)},
        compiler_params=pltpu.CompilerParams(has_side_effects=EFFECT))(*ins)
    arrays = out[len(sems):-1]
    started = [tuple(out[3 * g:3 * g + 3]) + tuple(arrays[at:at + size]) + tuple(arrays[n + at:n + at + size])
               for g, (at, size) in enumerate(zip(starts, sizes))]
    return started, out[-1]


def _exchange_wait(started, after, scatter, *, name):
    n = (len(started) - 3) // 2
    sems, arrays = started[:3], started[3:]

    def body(*refs):
        srcs_, lands = refs[:n], refs[n:2 * n]
        send_sems, recv_sems, local_sems = refs[2 * n:2 * n + 3]
        for cp in _direct_copies(srcs_, lands, send_sems, recv_sems, local_sems, scatter):
            cp.wait()

    out = pl.pallas_call(
        body, name=name, out_shape=tuple(pltpu.HBM(a.shape, a.dtype) for a in arrays),
        in_specs=[HBM] * (2 * n) + [SEM] * 3 + [ANY], out_specs=(HBM,) * (2 * n),
        input_output_aliases={i: i for i in range(2 * n)},
        compiler_params=pltpu.CompilerParams(has_side_effects=EFFECT))(*arrays, *sems, after)
    return out[n:]


def _adamw_reduce(w, parts, m, v, *, name):
    layers, r, c = w.shape
    assert len(parts) == layers
    tr = _tile(r, 512, 16)
    tiles = r // tr
    bc1 = 1.0 - ADAM_B1 ** ADAM_STEP
    bc2 = 1.0 - ADAM_B2 ** ADAM_STEP

    def body(w_ref, *rest):
        p_refs = rest[:layers]
        m_ref, v_ref, g_ref, d_ref, nm_ref, nv_ref = rest[layers:]

        def update(p_ref):
            g = p_ref[0, :, pl.ds(0, c)].astype(F32)
            for s in range(1, N_DEV):
                g = g + p_ref[s, :, pl.ds(0, c)].astype(F32)
            nm = ADAM_B1 * m_ref[0] + (1.0 - ADAM_B1) * g
            nv = ADAM_B2 * v_ref[0] + (1.0 - ADAM_B2) * (g * g)
            g_ref[0] = g
            nm_ref[0] = nm
            nv_ref[0] = nv
            d_ref[0] = -ADAM_LR * ((nm / bc1) / (jnp.sqrt(nv / bc2) + ADAM_EPS) + ADAM_WD * w_ref[0])

        for layer in range(layers):
            pl.when(pl.program_id(0) == layer)(functools.partial(update, p_refs[layer]))

    def part_spec(layer, shape):
        rest = 0 if layer > 0 else tiles - 1
        return pl.BlockSpec((N_DEV, tr, shape[2]), lambda l, i: (0, jnp.where(l == layer, i, rest), 0))

    spec = pl.BlockSpec((1, tr, c), lambda l, i: (l, i, 0))
    out = jax.ShapeDtypeStruct((layers, r, c), F32)
    return _pcall(body, name=name, out_shape=(out,) * 4, grid=(layers, tiles),
                  in_specs=[spec] + [part_spec(layer, p.shape) for layer, p in enumerate(parts)] + [spec, spec],
                  out_specs=(spec,) * 4, semantics=("arbitrary", "arbitrary"), vmem_limit=VMEM_LIMIT)(w, *parts, m, v)


def _pool_windows():
    return jnp.repeat(jnp.asarray(POOL_WINDOWS, F32), POOL_DIM // len(POOL_WINDOWS))[None, :]


def _block_diag_pairs(pool_w):
    z = jnp.zeros_like(pool_w[0])
    return jnp.stack([jnp.block([[pool_w[2 * b], z], [z, pool_w[2 * b + 1]]]) for b in range(2)])


def _pad_lanes(vec):
    return jnp.zeros((1, LANE), F32).at[0, :vec.shape[0]].set(vec)


FF_SHARD = D_FF // N_DEV
FF_BLOCK = 384
D_FF_PAD = N_DEV * FF_BLOCK


def _layer_fwd(x, p_i, wt, fetch):
    wt = {**wt, **fetch(0, x)}
    h1 = _rmsnorm_fwd(x, wt["norm1_g"], name="rmsnorm_fwd")
    proj = _matmul(h1, wt["w_in"], "nn", name="mm_in")
    qkv = _qkv_prep_fwd(proj, wt["conv_qkv"], name="qkv_prep_fwd")
    g, beta = _gates_fwd(proj, wt["a_log"], wt["dt_bias"], name="gates_fwd")
    u, w, qg, kg, attn, tmats = _deltanet_prep(qkv, g, beta, name="deltanet_prep")
    o, vn, states = _deltanet_scan(u, w, qg, kg, attn, g, name="deltanet_scan")
    o_a = _apost_fwd(o, proj, wt["onorm_g"], name="apost_fwd")
    o_b = _pool_fwd(proj, wt["pool_win"], wt["pool_wbd"], wt["pool_scale"], name="pool_fwd")
    o_c = _sconv_fwd(proj, wt["sconv_w"], name="sconv_fwd")
    mixed = jnp.concatenate([o_a, o_b, o_c], axis=1)
    wt.update(fetch(1, mixed))
    x1 = _matmul(mixed, wt["w_out"], "nn", res=x, name="mm_out")
    h2 = _rmsnorm_fwd(x1, wt["norm2_g"], name="rmsnorm_fwd")
    wt.update(fetch(2, h2))
    gate = _matmul(h2, wt["w_gate"], "nn", b_blocked=True, name="mm_gate")
    up = _matmul(h2, wt["w_up"], "nn", b_blocked=True, name="mm_up")
    ff = _swiglu_fwd(gate, up, name="swiglu_fwd")
    wt.update(fetch(3, ff))
    x2 = _matmul(ff, wt["w_down"], "nn", res=x1, name="mm_down")
    wt.update(fetch(4, x2))
    pgl = _matmul(x2, wt["ple_gate"], "nn", name="mm_pleg")
    pp = _matmul(p_i, wt["ple_proj"], "nn", b_blocked=True, name="mm_plep")
    x3 = _ple_fwd(x2, pgl, pp, name="ple_fwd")
    saved = dict(x=x, h1=h1, proj=proj, qkv=qkv, g=g, beta=beta, o=o, states=states, tmats=tmats, mixed=mixed, x1=x1, h2=h2,
                 gate=gate, up=up, ff=ff, x2=x2, pgl=pgl, pp=pp, p=p_i, w=w, qg=qg, kg=kg, attn=attn, vn=vn, wt=wt)
    return x3, saved


def _col_blocks(g):
    a = g.shape[0]
    return jnp.transpose(g.reshape(a, N_DEV, -1), (1, 0, 2))


def _cols_joined(blocks):
    return jnp.transpose(blocks, (1, 0, 2)).reshape(blocks.shape[1], -1)


def _layer_bwd(dx3, sv, emit):
    gr, big = {}, {}
    wt = sv["wt"]
    rows = D_MODEL // N_DEV
    dpgl, dpp = _ple_bwd(dx3, sv["pgl"], sv["pp"], name="ple_bwd")
    big["ple_proj"] = _matmul(sv["p"], dpp, "tn", out_blocked=(N_DEV, rows), out_dtype=BF16, name="mm_dplep")
    big["ple_gate"] = _matmul(sv["x2"], dpgl, "tn", out_dtype=BF16, name="mm_dpleg").reshape(N_DEV, rows, D_MODEL)
    dx2 = _matmul(dpgl, wt["ple_gate"], "nt", res=dx3, name="mm_dx2")
    big["w_down"] = _matmul(sv["ff"], dx2, "tn", out_dtype=BF16, name="mm_ddown").reshape(N_DEV, FF_BLOCK, D_MODEL)
    emit(0, big)
    dff = _matmul(dx2, wt["w_down"], "nt", name="mm_dff")
    dgate, dup = _swiglu_bwd(sv["gate"], sv["up"], dff, name="swiglu_bwd")
    big["w_gate"] = _matmul(sv["h2"], dgate, "tn", out_blocked=(N_DEV, FF_BLOCK), out_dtype=BF16, name="mm_dgate")
    big["w_up"] = _matmul(sv["h2"], dup, "tn", out_blocked=(N_DEV, FF_BLOCK), out_dtype=BF16, name="mm_dup")
    dh2 = _matmul(dgate, wt["w_gate"], "nt", b_blocked=True, name="mm_dh2_gate")
    dh2 = _matmul(dup, wt["w_up"], "nt", b_blocked=True, res=dh2, name="mm_dh2_up")
    dx1, gr["norm2_g"] = _rmsnorm_bwd(sv["x1"], wt["norm2_g"], dh2, dx2, name="rmsnorm_bwd")
    big["w_out"] = _matmul(sv["mixed"], dx1, "tn", out_dtype=BF16, name="mm_dout").reshape(N_DEV, rows, D_MODEL)
    emit(1, big)
    dmixed = _matmul(dx1, wt["w_out"], "nt", name="mm_dmixed")
    proj = sv["proj"]
    dcb, dcc, dch, dsconv = _sconv_bwd(proj, wt["sconv_w"], dmixed, name="sconv_bwd")
    big["sconv_w"] = _col_blocks(dsconv)
    dhp, dwbd, gr["pool_scale"] = _pool_bwd(proj, wt["pool_win"], wt["pool_wbd"], wt["pool_scale"], dmixed, name="pool_bwd")
    half = LANE // 2
    gr["pool_w"] = jnp.stack([dwbd[0, :half, :half], dwbd[0, half:, half:], dwbd[1, :half, :half], dwbd[1, half:, half:]])
    do, dz, gr["onorm_g"] = _apost_bwd(sv["o"], proj, wt["onorm_g"], dmixed, name="apost_bwd")
    dvn, dstates = _deltanet_bscan(sv["w"], sv["qg"], sv["kg"], sv["attn"], sv["g"], do, name="deltanet_bscan")
    dqkv_h, dg, dbeta = _deltanet_post(sv["qkv"], sv["g"], sv["beta"], sv["tmats"], sv["states"], dstates, do, dvn, sv["vn"],
                                       name="deltanet_post")
    dab, dalog, ddtb = _gates_bwd(proj, wt["a_log"], wt["dt_bias"], dg, dbeta, name="gates_bwd")
    gr["a_log"], gr["dt_bias"] = dalog[0, :HEADS], ddtb[0, :HEADS]
    dqkv, dconv = _qkv_prep_bwd(proj, wt["conv_qkv"], dqkv_h, name="qkv_prep_bwd")
    big["conv_qkv"] = _col_blocks(dconv)
    dproj = jnp.concatenate([dqkv, dz, dab, dhp, dcb, dcc, dch], axis=1)
    dwin = _matmul(sv["h1"], dproj, "tn", out_dtype=BF16, name="mm_din")
    big["w_in"] = _col_blocks(jnp.concatenate([dwin[:, :AB_COL + 2 * HEADS], dwin[:, AB_COL + LANE:]], axis=1))
    emit(2, big)
    dh1 = _matmul(dproj, wt["w_in"], "nt", name="mm_dh1")
    dx, gr["norm1_g"] = _rmsnorm_bwd(sv["x"], wt["norm1_g"], dh1, dx1, name="rmsnorm_bwd")
    return dx, gr


FETCH_GROUPS = (("w_in", "conv_qkv", "sconv_w"), ("w_out",), ("w_gate", "w_up"), ("w_down",), ("ple_gate", "ple_proj"))
EMIT_GROUPS = (("ple_proj", "ple_gate", "w_down"), ("w_gate", "w_up", "w_out"), ("w_in", "conv_qkv", "sconv_w"))


def _small_weights(w, i):
    return dict(
        norm1_g=w["norm1_g"][i][None], norm2_g=w["norm2_g"][i][None], onorm_g=w["onorm_g"][i][None],
        a_log=_pad_lanes(w["a_log"][i]), dt_bias=_pad_lanes(w["dt_bias"][i]),
        pool_scale=w["pool_scale"][i][None], pool_win=_pool_windows(), pool_wbd=_block_diag_pairs(w["pool_w"][i]))


def _as_read(name, gathered):
    if name == "w_in":
        w_in = _cols_joined(gathered)
        return jnp.concatenate([w_in[:, :AB_COL + 2 * HEADS], jnp.zeros((D_MODEL, LANE - 2 * HEADS), BF16),
                                w_in[:, AB_COL + 2 * HEADS:]], axis=1)
    if name in ("conv_qkv", "sconv_w"):
        return _cols_joined(gathered)
    if name in ("w_gate", "w_up", "ple_proj"):
        return gathered
    return gathered.reshape(-1, D_MODEL)


def _layer_weights(gathered, w, i):
    return {**_small_weights(w, i), **{k: _as_read(k, g) for k, g in gathered.items()}}


def _local_step(x, p, target, layers, final_g):
    saved = []
    h = x
    for i in range(DEPTH):
        replicated = {k: v for k, v in layers[i].items() if k not in SHARDED}
        h, sv = _layer_fwd(h, p[i], replicated, lambda group, after, i=i: {k: layers[i][k] for k in FETCH_GROUPS[group]})
        saved.append(sv)
    dx, dgf, loss = _loss_head(h, final_g, target, name="loss_head")
    big, small = [{} for _ in range(DEPTH)], [None] * DEPTH
    for i in reversed(range(DEPTH)):
        dx, small[i] = _layer_bwd(dx, saved[i], lambda group, blocks, i=i: big[i].update({k: blocks[k] for k in EMIT_GROUPS[group]}))
    return loss, dx, big, small, dgf


SHARDED = ("w_in", "w_gate", "w_up", "w_down", "w_out", "ple_gate", "ple_proj", "conv_qkv", "sconv_w")
SMALL = ("norm1_g", "a_log", "dt_bias", "onorm_g", "pool_w", "pool_scale", "norm2_g", "final_g")
SLAB_COLS = 1024


def _payload(name, shard):
    if name in ("conv_qkv", "sconv_w"):
        return shard
    out = shard.astype(BF16)
    if name in ("w_gate", "w_up"):
        out = jnp.pad(out, ((0, 0), (0, FF_BLOCK - FF_SHARD)))
    if name == "w_down":
        out = jnp.pad(out, ((0, FF_BLOCK - FF_SHARD), (0, 0)))
    return out


def _slab_rows(shape):
    size = 1
    for s in shape:
        size *= s
    return SUBLANE * -(-size // (SUBLANE * SLAB_COLS))


def _pack_slab(parts, extra_row):
    rows = []
    for name in SMALL:
        flat = parts[name].reshape(-1)
        nrow = _slab_rows(parts[name].shape)
        rows.append(jnp.pad(flat, (0, nrow * SLAB_COLS - flat.shape[0])).reshape(nrow, SLAB_COLS))
    rows.append(jnp.pad(extra_row, ((0, SUBLANE - 1), (0, 0))))
    return jnp.concatenate(rows, axis=0)


def _unpack_slab(slab, shapes):
    out, row = {}, 0
    for name in SMALL:
        size = 1
        for s in shapes[name]:
            size *= s
        out[name] = slab[row:row + _slab_rows(shapes[name])].reshape(-1)[:size].reshape(shapes[name])
        row += _slab_rows(shapes[name])
    return out, row


def kernel(x, p, norm1_g, w_in, conv_qkv, a_log, dt_bias, onorm_g, pool_w, pool_scale, sconv_w, w_out, norm2_g, w_gate, w_up, w_down, ple_proj, ple_gate, final_g, loss_target, m_norm1_g, m_w_in, m_conv_qkv, m_a_log, m_dt_bias, m_onorm_g, m_pool_w, m_pool_scale, m_sconv_w, m_w_out, m_norm2_g, m_w_gate, m_w_up, m_w_down, m_ple_proj, m_ple_gate, m_final_g, v_norm1_g, v_w_in, v_conv_qkv, v_a_log, v_dt_bias, v_onorm_g, v_pool_w, v_pool_scale, v_sconv_w, v_w_out, v_norm2_g, v_w_gate, v_w_up, v_w_down, v_ple_proj, v_ple_gate, v_final_g):
    names = ["norm1_g", "w_in", "conv_qkv", "a_log", "dt_bias", "onorm_g", "pool_w", "pool_scale", "sconv_w", "w_out", "norm2_g",
             "w_gate", "w_up", "w_down", "ple_proj", "ple_gate", "final_g"]
    w = dict(zip(names, [norm1_g, w_in, conv_qkv, a_log, dt_bias, onorm_g, pool_w, pool_scale, sconv_w, w_out, norm2_g, w_gate, w_up,
                         w_down, ple_proj, ple_gate, final_g]))
    m = dict(zip(names, [m_norm1_g, m_w_in, m_conv_qkv, m_a_log, m_dt_bias, m_onorm_g, m_pool_w, m_pool_scale, m_sconv_w, m_w_out,
                         m_norm2_g, m_w_gate, m_w_up, m_w_down, m_ple_proj, m_ple_gate, m_final_g]))
    v = dict(zip(names, [v_norm1_g, v_w_in, v_conv_qkv, v_a_log, v_dt_bias, v_onorm_g, v_pool_w, v_pool_scale, v_sconv_w, v_w_out,
                         v_norm2_g, v_w_gate, v_w_up, v_w_down, v_ple_proj, v_ple_gate, v_final_g]))

    payload = lambda i, group: [_payload(k, w[k][i]) for k in FETCH_GROUPS[group]]
    groups = range(len(FETCH_GROUPS))
    gathered_now = _all_gather(payload(0, 0), name="all_gather_weights")
    flying0, token0 = _exchange_start([payload(0, g) for g in groups[1:]], False, name="gather_start_0")
    flying1, token1 = _exchange_start([payload(1, g) for g in groups], False, name="gather_start_1")
    replicated = [_small_weights(w, i) for i in range(DEPTH)]
    replicated[0]["norm1_g"] = replicated[0]["norm1_g"] + token0[0, 0] + token1[0, 0]

    def fetch(i, group, after):
        if i == 0 and group == 0:
            arrays = gathered_now
        else:
            arrays = _exchange_wait((flying0[group - 1] if i == 0 else flying1[group]), after, False, name=f"gather_wait_{i}_{group}")
        return {k: _as_read(k, a) for k, a in zip(FETCH_GROUPS[group], arrays)}

    h, saved0 = _layer_fwd(x[0], p[0, 0], replicated[0], functools.partial(fetch, 0))
    h, saved1 = _layer_fwd(h, p[1, 0], replicated[1], functools.partial(fetch, 1))
    dx, dgf, loss_part = _loss_head(h, final_g[None], loss_target[0], name="loss_head")
    small, big1, flying = [None] * DEPTH, {}, {}
    dx, small[1] = _layer_bwd(dx, saved1, lambda group, blocks: big1.update({k: blocks[k] for k in EMIT_GROUPS[group]}))
    flying[1], token = _exchange_start([[big1[k] for k in names] for names in EMIT_GROUPS], True, name="exchange_start_1")
    flying[0], landed_last = [], []

    def emit(group, blocks):
        arrays = [blocks[k] for k in EMIT_GROUPS[group]]
        if group < len(EMIT_GROUPS) - 1:
            flying[0].append(_exchange_start([arrays], True, name=f"exchange_start_0_{group}")[0][0])
        else:
            landed_last.extend(_exchange_blocks(arrays, name="exchange_grad_blocks"))

    dx, small[0] = _layer_bwd(dx + token[0, 0], saved0, emit)
    received = [{}, {}]
    for group, members in enumerate(EMIT_GROUPS):
        for i in range(DEPTH):
            if i == 0 and group == len(EMIT_GROUPS) - 1:
                landed = landed_last
            else:
                landed = _exchange_wait(flying[i][group], dx, True, name=f"exchange_wait_{i}_{group}")
            received[i].update(zip(members, landed))

    grads = {k: jnp.stack([small[i][k] for i in range(DEPTH)]) for k in small[0]}
    grads = {k: g[:, 0] if k in ("norm1_g", "norm2_g", "onorm_g", "pool_scale") else g for k, g in grads.items()}
    grads["final_g"] = dgf[0]
    loss_row = jnp.pad(loss_part, ((0, 0), (0, SLAB_COLS - LANE)))
    (small_parts,) = _all_gather([_pack_slab(grads, loss_row)], name="all_gather_small_grads")

    out_g, out_d, out_m, out_v = {}, {}, {}, {}
    for k in SHARDED:
        out_g[k], out_d[k], out_m[k], out_v[k] = _adamw_reduce(w[k], [received[i][k] for i in range(DEPTH)], m[k], v[k],
                                                                name="adamw_" + k)
    zero_row = jnp.zeros((1, SLAB_COLS), F32)
    slabs = _adamw_reduce(_pack_slab(w, zero_row)[None], [small_parts], _pack_slab(m, zero_row)[None],
                          _pack_slab(v, zero_row)[None], name="adamw_small")
    slabs = [s[0] for s in slabs]
    shapes = {k: w[k].shape for k in SMALL}
    for dst, slab in zip((out_g, out_d, out_m, out_v), slabs):
        vals, _ = _unpack_slab(slab, shapes)
        dst.update(vals)
    _, loss_at = _unpack_slab(slabs[0], shapes)
    loss = slabs[0][loss_at, 0]

    return (loss, dx[None], *[out_g[k] for k in names], *[out_d[k] for k in names], *[out_m[k] for k in names],
            *[out_v[k] for k in names])
```

```python
import functools

import jax
import jax.numpy as jnp
from jax import lax
from jax.experimental import pallas as pl
from jax.experimental.pallas import tpu as pltpu

F32 = jnp.float32
BF16 = jnp.bfloat16

D_MODEL = 1024
DEPTH = 2
PLE_DIM = 256
EPS = 1e-6
HEAD_DIM = 128
HEADS = 4
A_DIM = HEADS * HEAD_DIM
QKV_TAPS = 4
CHUNK = 64
POOL_WINDOWS = (2, 4, 8, 16)
POOL_DIM = 256
CONV_DIM = 256
CONV_TAPS = 3
D_FF = 2816
D_IN = 3080
D_IN_PAD = 3200
AB_COL = 2048
N_DEV = 8

ADAM_LR = 0.001
ADAM_B1 = 0.9
ADAM_B2 = 0.999
ADAM_EPS = 1e-08
ADAM_WD = 0.01
ADAM_STEP = 10

LANE = 128
SUBLANE = 8
VMEM_BYTES_V7X = 64 * 1024 * 1024
VMEM_LIMIT = 48 * 1024 * 1024

_HI = lax.Precision.HIGHEST
NN = ((1,), (0,))
NT = ((1,), (1,))
TN = ((0,), (0,))
MESH = pl.DeviceIdType.MESH


def _dot(a, b, dims, hi=False):
    if hi:
        return lax.dot_general(a, b, (dims, ((), ())), precision=_HI, preferred_element_type=F32)
    return lax.dot_general(a.astype(BF16), b.astype(BF16), (dims, ((), ())), preferred_element_type=F32)


def _pcall(body, *, name, out_shape, grid=(), in_specs=None, out_specs=None, scratch_shapes=(), semantics=None,
           vmem_limit=None, **kw):
    params = {}
    if semantics is not None:
        params["dimension_semantics"] = semantics
    if vmem_limit is not None:
        params["vmem_limit_bytes"] = vmem_limit
    return pl.pallas_call(
        body, name=name, out_shape=out_shape, grid=grid, in_specs=in_specs, out_specs=out_specs,
        scratch_shapes=list(scratch_shapes), compiler_params=pltpu.CompilerParams(**params), **kw)


def _sigmoid(x):
    return 1.0 / (1.0 + jnp.exp(-x))


def _softplus(x):
    return jnp.maximum(x, 0.0) + jnp.log(1.0 + jnp.exp(-jnp.abs(x)))


def _tile(n, cap, mult):
    if n <= cap:
        return n
    best = None
    for t in range(mult, cap + 1, mult):
        if n % t == 0:
            best = t
    assert best is not None, (n, cap, mult)
    return best


ROWS_PER_STEP = 512
COLS_PER_DOT = 640


def _matmul_rows(a, b, mode, *, name, res=None, out_dtype=F32, b_blocked=False):
    m, k = a.shape
    if b_blocked:
        nb, _, bw = b.shape
        n = nb * bw if mode == "nn" else b.shape[1]
    else:
        n = b.shape[1] if mode == "nn" else b.shape[0]
    tm = _tile(m, ROWS_PER_STEP, 16)
    cn = bw if (b_blocked and mode == "nn") else _tile(n, COLS_PER_DOT, LANE)
    has_res = res is not None

    def body(*refs):
        a_ref, b_ref = refs[0], refs[1]
        res_ref = refs[2] if has_res else None
        o_ref = refs[2 + has_res]
        if not (b_blocked and mode == "nt"):
            av = a_ref[...].astype(BF16)
        for j in range(n // cn):
            cols = pl.ds(j * cn, cn)
            if mode == "nn":
                part = _dot(av, b_ref[j] if b_blocked else b_ref[:, cols], NN)
            elif not b_blocked:
                part = _dot(av, b_ref[cols, :], NT)
            else:
                part = None
                for s in range(nb):
                    term = _dot(a_ref[:, pl.ds(s * bw, bw)], b_ref[s, cols, :], NT)
                    part = term if part is None else part + term
            if has_res:
                part = part + res_ref[:, cols]
            o_ref[:, cols] = part.astype(o_ref.dtype)

    row = lambda width: pl.BlockSpec((tm, width), lambda i: (i, 0))
    whole = pl.BlockSpec(b.shape, lambda i: (0,) * b.ndim)
    ins = [a, b] + ([res] if has_res else [])
    specs = [row(k), whole] + ([row(n)] if has_res else [])
    return _pcall(body, name=name, out_shape=jax.ShapeDtypeStruct((m, n), out_dtype), grid=(m // tm,), in_specs=specs,
                  out_specs=row(n), semantics=("parallel",), vmem_limit=VMEM_LIMIT)(*ins)


def _matmul(a, b, mode, *, name, res=None, out_dtype=F32, b_blocked=False, out_blocked=None):
    if mode != "tn":
        return _matmul_rows(a, b, mode, name=name, res=res, out_dtype=out_dtype, b_blocked=b_blocked)
    assert res is None and not b_blocked
    (t, m), (t2, n) = a.shape, b.shape
    assert t == t2, (a.shape, b.shape)
    tm = _tile(m, 1024, LANE)
    tn = _tile(n, COLS_PER_DOT, LANE)
    if out_blocked is not None:
        assert out_blocked[0] * out_blocked[1] == n
        tn = out_blocked[1]

    def body(a_ref, b_ref, o_ref):
        part = _dot(a_ref[...], b_ref[...], TN).astype(o_ref.dtype)
        if out_blocked is None:
            o_ref[...] = part
        else:
            o_ref[0] = part

    o_spec = (pl.BlockSpec((tm, tn), lambda i, j: (i, j)) if out_blocked is None
              else pl.BlockSpec((1, tm, tn), lambda i, j: (j, i, 0)))
    o_shape = (m, n) if out_blocked is None else (out_blocked[0], m, out_blocked[1])
    return _pcall(body, name=name, out_shape=jax.ShapeDtypeStruct(o_shape, out_dtype), grid=(m // tm, n // tn),
                  in_specs=[pl.BlockSpec((t, tm), lambda i, j: (0, i)), pl.BlockSpec((t, tn), lambda i, j: (0, j))],
                  out_specs=o_spec, semantics=("parallel", "parallel"), vmem_limit=VMEM_LIMIT)(a, b)


ROW_TILE = 256


def _rows(t, width, idx=0):
    return pl.BlockSpec((ROW_TILE, width), lambda i: (i, idx))


def _vec(width):
    return pl.BlockSpec((1, width), lambda i: (0, 0))


def _rmsnorm_fwd(x, g, *, name):
    t, d = x.shape

    def body(x_ref, g_ref, h_ref):
        xv = x_ref[...]
        r = lax.rsqrt(jnp.mean(xv * xv, axis=-1, keepdims=True) + EPS)
        h_ref[...] = (xv * r * g_ref[...]).astype(BF16)

    return _pcall(body, name=name, out_shape=jax.ShapeDtypeStruct((t, d), BF16), grid=(t // ROW_TILE,),
                  in_specs=[_rows(t, d), _vec(d)], out_specs=_rows(t, d), semantics=("parallel",))(x, g)


def _rmsnorm_bwd(x, g, dh, dres, *, name):
    t, d = x.shape

    def body(x_ref, g_ref, dh_ref, dres_ref, dx_ref, dg_ref):
        xv = x_ref[...]
        r = lax.rsqrt(jnp.mean(xv * xv, axis=-1, keepdims=True) + EPS)
        xhat = xv * r
        dhv = dh_ref[...].astype(F32)
        dhg = dhv * g_ref[...]
        dx_ref[...] = dres_ref[...] + r * (dhg - xhat * jnp.mean(dhg * xhat, axis=-1, keepdims=True))
        part = jnp.sum(dhv * xhat, axis=0, keepdims=True)

        @pl.when(pl.program_id(0) == 0)
        def _():
            dg_ref[...] = part

        @pl.when(pl.program_id(0) > 0)
        def _():
            dg_ref[...] += part

    return _pcall(body, name=name, out_shape=(jax.ShapeDtypeStruct((t, d), F32), jax.ShapeDtypeStruct((1, d), F32)),
                  grid=(t // ROW_TILE,), in_specs=[_rows(t, d), _vec(d), _rows(t, d), _rows(t, d)],
                  out_specs=(_rows(t, d), _vec(d)), semantics=("arbitrary",))(x, g, dh, dres)


def _swiglu_fwd(gate, up, *, name):
    t, f = gate.shape

    def body(gate_ref, up_ref, ff_ref):
        gv = gate_ref[...]
        ff_ref[...] = (gv * _sigmoid(gv) * up_ref[...]).astype(BF16)

    return _pcall(body, name=name, out_shape=jax.ShapeDtypeStruct((t, f), BF16), grid=(t // ROW_TILE,),
                  in_specs=[_rows(t, f), _rows(t, f)], out_specs=_rows(t, f), semantics=("parallel",))(gate, up)


def _swiglu_bwd(gate, up, dff, *, name):
    t, f = gate.shape

    def body(gate_ref, up_ref, dff_ref, dgate_ref, dup_ref):
        gv = gate_ref[...]
        sig = _sigmoid(gv)
        dffv = dff_ref[...]
        dgate_ref[...] = (dffv * up_ref[...] * sig * (1.0 + gv * (1.0 - sig))).astype(BF16)
        dup_ref[...] = (dffv * gv * sig).astype(BF16)

    out = jax.ShapeDtypeStruct((t, f), BF16)
    return _pcall(body, name=name, out_shape=(out, out), grid=(t // ROW_TILE,), in_specs=[_rows(t, f)] * 3,
                  out_specs=(_rows(t, f),) * 2, semantics=("parallel",))(gate, up, dff)


def _ple_fwd(x2, pgl, pp, *, name):
    t, d = x2.shape

    def body(x_ref, pgl_ref, pp_ref, o_ref):
        o_ref[...] = x_ref[...] + _sigmoid(pgl_ref[...]) * pp_ref[...]

    return _pcall(body, name=name, out_shape=jax.ShapeDtypeStruct((t, d), F32), grid=(t // ROW_TILE,),
                  in_specs=[_rows(t, d)] * 3, out_specs=_rows(t, d), semantics=("parallel",))(x2, pgl, pp)


def _ple_bwd(dx3, pgl, pp, *, name):
    t, d = dx3.shape

    def body(dx_ref, pgl_ref, pp_ref, dpgl_ref, dpp_ref):
        dxv = dx_ref[...]
        sig = _sigmoid(pgl_ref[...])
        dpp_ref[...] = (dxv * sig).astype(BF16)
        dpgl_ref[...] = (dxv * pp_ref[...] * sig * (1.0 - sig)).astype(BF16)

    return _pcall(body, name=name, out_shape=(jax.ShapeDtypeStruct((t, d), BF16),) * 2, grid=(t // ROW_TILE,),
                  in_specs=[_rows(t, d)] * 3, out_specs=(_rows(t, d),) * 2, semantics=("parallel",))(dx3, pgl, pp)


def _loss_head(x3, g, target, *, name):
    t, d = x3.shape

    def body(x_ref, g_ref, t_ref, dx_ref, dg_ref, loss_ref):
        xv = x_ref[...]
        r = lax.rsqrt(jnp.mean(xv * xv, axis=-1, keepdims=True) + EPS)
        xhat = xv * r
        gv = g_ref[...]
        err = xhat * gv - t_ref[...]
        row_loss = jnp.sum(err * err, axis=-1, keepdims=True) * (0.5 / d)
        lpart = jnp.broadcast_to(jnp.sum(row_loss, axis=0, keepdims=True), (1, LANE))
        dy = err * (1.0 / d)
        dyg = dy * gv
        dx_ref[...] = r * (dyg - xhat * jnp.mean(dyg * xhat, axis=-1, keepdims=True))
        gpart = jnp.sum(dy * xhat, axis=0, keepdims=True)

        @pl.when(pl.program_id(0) == 0)
        def _():
            dg_ref[...] = gpart
            loss_ref[...] = lpart

        @pl.when(pl.program_id(0) > 0)
        def _():
            dg_ref[...] += gpart
            loss_ref[...] += lpart

    return _pcall(body, name=name,
                  out_shape=(jax.ShapeDtypeStruct((t, d), F32), jax.ShapeDtypeStruct((1, d), F32), jax.ShapeDtypeStruct((1, LANE), F32)),
                  grid=(t // ROW_TILE,), in_specs=[_rows(t, d), _vec(d), _rows(t, d)],
                  out_specs=(_rows(t, d), _vec(d), _vec(LANE)), semantics=("arbitrary",))(x3, g, target)


def _shift_down(x, d):
    if d == 0:
        return x
    row = lax.broadcasted_iota(jnp.int32, x.shape, 0)
    return jnp.where(row >= d, pltpu.roll(x, d, 0), 0.0)


def _shift_up(x, d):
    if d == 0:
        return x
    t = x.shape[0]
    row = lax.broadcasted_iota(jnp.int32, x.shape, 0)
    return jnp.where(row < t - d, pltpu.roll(x, t - d, 0), 0.0)


def _colsum(x):
    return jnp.sum(x, axis=0, keepdims=True)


def _col(t, idx_fn):
    return pl.BlockSpec((t, LANE), idx_fn)


def _conv_fwd(x, w_ref, taps):
    acc = None
    for j in range(taps):
        term = w_ref[pl.ds(j, 1), :] * _shift_down(x, taps - 1 - j)
        acc = term if acc is None else acc + term
    return acc


def _conv_bwd(x, dy, w_ref, dw_ref, taps):
    dx = None
    for j in range(taps):
        term = w_ref[pl.ds(j, 1), :] * _shift_up(dy, taps - 1 - j)
        dx = term if dx is None else dx + term
        dw_ref[pl.ds(j, 1), :] = _colsum(dy * _shift_down(x, taps - 1 - j))
    return dx


def _qkv_prep_fwd(proj, conv_w, *, name):
    t = proj.shape[0]
    scale = HEAD_DIM ** -0.5

    def body(x_ref, w_ref, o_ref):
        j = pl.program_id(0)
        c = _conv_fwd(x_ref[...], w_ref, QKV_TAPS)
        s = c * _sigmoid(c)
        r = lax.rsqrt(jnp.sum(s * s, axis=-1, keepdims=True) + EPS)
        f = jnp.where(j < 2 * HEADS, r, 1.0) * jnp.where(j < HEADS, scale, 1.0)
        o_ref[0] = s * f

    return _pcall(body, name=name, out_shape=jax.ShapeDtypeStruct((3 * HEADS, t, LANE), F32), grid=(3 * HEADS,),
                  in_specs=[_col(t, lambda j: (0, j)), pl.BlockSpec((QKV_TAPS, LANE), lambda j: (0, j))],
                  out_specs=pl.BlockSpec((1, t, LANE), lambda j: (j, 0, 0)), semantics=("parallel",),
                  vmem_limit=VMEM_LIMIT)(proj, conv_w)


def _qkv_prep_bwd(proj, conv_w, dqkv, *, name):
    t = proj.shape[0]
    scale = HEAD_DIM ** -0.5

    def body(x_ref, w_ref, d_ref, dx_ref, dw_ref):
        j = pl.program_id(0)
        xv = x_ref[...]
        c = _conv_fwd(xv, w_ref, QKV_TAPS)
        sig = _sigmoid(c)
        s = c * sig
        r = lax.rsqrt(jnp.sum(s * s, axis=-1, keepdims=True) + EPS)
        n0 = s * r
        dv = d_ref[0]
        dn0 = dv * jnp.where(j < HEADS, scale, 1.0)
        ds_norm = r * (dn0 - n0 * jnp.sum(dn0 * n0, axis=-1, keepdims=True))
        ds = jnp.where(j < 2 * HEADS, ds_norm, dv)
        dc = ds * sig * (1.0 + c * (1.0 - sig))
        dx_ref[...] = _conv_bwd(xv, dc, w_ref, dw_ref, QKV_TAPS).astype(BF16)

    return _pcall(body, name=name,
                  out_shape=(jax.ShapeDtypeStruct((t, 3 * A_DIM), BF16), jax.ShapeDtypeStruct((QKV_TAPS, 3 * A_DIM), F32)),
                  grid=(3 * HEADS,),
                  in_specs=[_col(t, lambda j: (0, j)), pl.BlockSpec((QKV_TAPS, LANE), lambda j: (0, j)),
                            pl.BlockSpec((1, t, LANE), lambda j: (j, 0, 0))],
                  out_specs=(_col(t, lambda j: (0, j)), pl.BlockSpec((QKV_TAPS, LANE), lambda j: (0, j))),
                  semantics=("parallel",), vmem_limit=VMEM_LIMIT)(proj, conv_w, dqkv)


def _lane_pick(x, lane_idx, lane):
    return jnp.broadcast_to(jnp.sum(jnp.where(lane == lane_idx, x, 0.0), axis=-1, keepdims=True), x.shape)


def _gates_fwd(proj, alog, dtb, *, name):
    t = proj.shape[0]

    def body(x_ref, alog_ref, dtb_ref, g_ref, b_ref):
        xv = x_ref[...]
        lane = lax.broadcasted_iota(jnp.int32, xv.shape, 1)
        gall = -jnp.exp(alog_ref[...]) * _softplus(xv + dtb_ref[...])
        ball = _sigmoid(xv)
        for h in range(HEADS):
            g_ref[h] = _lane_pick(gall, h, lane)
            b_ref[h] = _lane_pick(ball, HEADS + h, lane)

    out = jax.ShapeDtypeStruct((HEADS, t, LANE), F32)
    whole = pl.BlockSpec((HEADS, t, LANE), lambda i: (0, 0, 0))
    return _pcall(body, name=name, out_shape=(out, out), grid=(1,),
                  in_specs=[_col(t, lambda i: (0, AB_COL // LANE)), _vec(LANE), _vec(LANE)], out_specs=(whole, whole),
                  semantics=("arbitrary",), vmem_limit=VMEM_LIMIT)(proj, alog, dtb)


def _gates_bwd(proj, alog, dtb, dg, dbeta, *, name):
    t = proj.shape[0]

    def body(x_ref, alog_ref, dtb_ref, dg_ref, db_ref, dab_ref, dalog_ref, ddtb_ref):
        xv = x_ref[...]
        lane = lax.broadcasted_iota(jnp.int32, xv.shape, 1)
        lane1 = lax.broadcasted_iota(jnp.int32, (1, LANE), 1)
        z = xv + dtb_ref[...]
        nea = -jnp.exp(alog_ref[...])
        da_f = nea * _sigmoid(z)
        g_f = nea * _softplus(z)
        ball = _sigmoid(xv)
        db_f = ball * (1.0 - ball)
        dab = jnp.zeros_like(xv)
        dalog = jnp.zeros((1, LANE), F32)
        for h in range(HEADS):
            dgh = dg_ref[h]
            dab = dab + jnp.where(lane == h, dgh * da_f, 0.0) + jnp.where(lane == HEADS + h, db_ref[h] * db_f, 0.0)
            dalog = dalog + jnp.where(lane1 == h, _colsum(dgh * g_f), 0.0)
        dab_ref[...] = dab.astype(BF16)
        dalog_ref[...] = dalog
        ddtb_ref[...] = jnp.where(lane1 < HEADS, _colsum(dab), 0.0)

    whole = pl.BlockSpec((HEADS, t, LANE), lambda i: (0, 0, 0))
    vec = jax.ShapeDtypeStruct((1, LANE), F32)
    return _pcall(body, name=name, out_shape=(jax.ShapeDtypeStruct((t, LANE), BF16), vec, vec), grid=(1,),
                  in_specs=[_col(t, lambda i: (0, AB_COL // LANE)), _vec(LANE), _vec(LANE), whole, whole],
                  out_specs=(_col(t, lambda i: (0, 0)), _vec(LANE), _vec(LANE)), semantics=("arbitrary",),
                  vmem_limit=VMEM_LIMIT)(proj, alog, dtb, dg, dbeta)


Z_COL = 3 * A_DIM // LANE


def _apost_fwd(o, proj, gn, *, name):
    t = proj.shape[0]

    def body(o_ref, z_ref, gn_ref, y_ref):
        ov = o_ref[0]
        z = z_ref[...]
        r = lax.rsqrt(jnp.mean(ov * ov, axis=-1, keepdims=True) + EPS)
        y_ref[...] = (ov * r * gn_ref[...] * (z * _sigmoid(z))).astype(BF16)

    return _pcall(body, name=name, out_shape=jax.ShapeDtypeStruct((t, A_DIM), BF16), grid=(HEADS,),
                  in_specs=[pl.BlockSpec((1, t, LANE), lambda h: (h, 0, 0)), _col(t, lambda h: (0, Z_COL + h)),
                            pl.BlockSpec((1, LANE), lambda h: (0, 0))],
                  out_specs=_col(t, lambda h: (0, h)), semantics=("parallel",), vmem_limit=VMEM_LIMIT)(o, proj, gn)


def _apost_bwd(o, proj, gn, dmixed, *, name):
    t = proj.shape[0]

    def body(o_ref, z_ref, gn_ref, d_ref, do_ref, dz_ref, dgn_ref):
        ov = o_ref[0]
        z = z_ref[...]
        gnv = gn_ref[...]
        dv = d_ref[...]
        r = lax.rsqrt(jnp.mean(ov * ov, axis=-1, keepdims=True) + EPS)
        ohat = ov * r
        sig = _sigmoid(z)
        dy = dv * (z * sig)
        dz_ref[...] = (dv * ohat * gnv * sig * (1.0 + z * (1.0 - sig))).astype(BF16)
        dyo = dy * gnv
        do_ref[0] = r * (dyo - ohat * jnp.mean(dyo * ohat, axis=-1, keepdims=True))
        part = _colsum(dy * ohat)

        @pl.when(pl.program_id(0) == 0)
        def _():
            dgn_ref[...] = part

        @pl.when(pl.program_id(0) > 0)
        def _():
            dgn_ref[...] += part

    return _pcall(body, name=name,
                  out_shape=(jax.ShapeDtypeStruct((HEADS, t, LANE), F32), jax.ShapeDtypeStruct((t, A_DIM), BF16),
                             jax.ShapeDtypeStruct((1, LANE), F32)),
                  grid=(HEADS,),
                  in_specs=[pl.BlockSpec((1, t, LANE), lambda h: (h, 0, 0)), _col(t, lambda h: (0, Z_COL + h)),
                            pl.BlockSpec((1, LANE), lambda h: (0, 0)), _col(t, lambda h: (0, h))],
                  out_specs=(pl.BlockSpec((1, t, LANE), lambda h: (h, 0, 0)), _col(t, lambda h: (0, h)),
                             pl.BlockSpec((1, LANE), lambda h: (0, 0))),
                  semantics=("arbitrary",), vmem_limit=VMEM_LIMIT)(o, proj, gn, dmixed)


POOL_COL = (AB_COL + LANE) // LANE
CB_COL = POOL_COL + POOL_DIM // LANE
CC_COL = CB_COL + CONV_DIM // LANE
CH_COL = CC_COL + CONV_DIM // LANE
MAX_WIN_LOG2 = 4


def _window_sums(x, shift):
    sums = []
    cur = x
    for k in range(MAX_WIN_LOG2):
        cur = cur + shift(cur, 1 << k)
        sums.append(cur)
    return sums


def _pick_window(sums, win):
    out = sums[-1]
    for k in range(MAX_WIN_LOG2 - 2, -1, -1):
        out = jnp.where(win == float(2 << k), sums[k], out)
    return out


def _pool_counts(shape, win):
    row = lax.broadcasted_iota(jnp.int32, shape, 0).astype(F32)
    return jnp.minimum(row + 1.0, win)


def _pool_fwd(proj, win, wbd, scale, *, name):
    t = proj.shape[0]

    def body(x_ref, win_ref, w_ref, s_ref, y_ref):
        xv = x_ref[...]
        winv = win_ref[...]
        pooled = _pick_window(_window_sums(xv, _shift_down), winv) / _pool_counts(xv.shape, winv) - xv
        y_ref[...] = (_dot(pooled, w_ref[0], NN) * s_ref[...]).astype(BF16)

    nb = POOL_DIM // LANE
    vec = pl.BlockSpec((1, LANE), lambda b: (0, b))
    return _pcall(body, name=name, out_shape=jax.ShapeDtypeStruct((t, POOL_DIM), BF16), grid=(nb,),
                  in_specs=[_col(t, lambda b: (0, POOL_COL + b)), vec, pl.BlockSpec((1, LANE, LANE), lambda b: (b, 0, 0)), vec],
                  out_specs=_col(t, lambda b: (0, b)), semantics=("parallel",), vmem_limit=VMEM_LIMIT)(proj, win, wbd, scale)


def _pool_bwd(proj, win, wbd, scale, dmixed, *, name):
    t = proj.shape[0]

    def body(x_ref, win_ref, w_ref, s_ref, d_ref, dx_ref, dw_ref, ds_ref):
        xv = x_ref[...]
        winv = win_ref[...]
        cnt = _pool_counts(xv.shape, winv)
        pooled = _pick_window(_window_sums(xv, _shift_down), winv) / cnt - xv
        dv = d_ref[...]
        ds_ref[...] = _colsum(dv * _dot(pooled, w_ref[0], NN))
        dy0 = dv * s_ref[...]
        dw_ref[0] = _dot(pooled, dy0, TN)
        dpooled = _dot(dy0, w_ref[0], NT)
        dmean = dpooled / cnt
        dx_ref[...] = (_pick_window(_window_sums(dmean, _shift_up), winv) - dpooled).astype(BF16)

    nb = POOL_DIM // LANE
    vec = pl.BlockSpec((1, LANE), lambda b: (0, b))
    mat = pl.BlockSpec((1, LANE, LANE), lambda b: (b, 0, 0))
    first = A_DIM // LANE
    return _pcall(body, name=name,
                  out_shape=(jax.ShapeDtypeStruct((t, POOL_DIM), BF16), jax.ShapeDtypeStruct((nb, LANE, LANE), F32),
                             jax.ShapeDtypeStruct((1, POOL_DIM), F32)),
                  grid=(nb,),
                  in_specs=[_col(t, lambda b: (0, POOL_COL + b)), vec, mat, vec, _col(t, lambda b: (0, first + b))],
                  out_specs=(_col(t, lambda b: (0, b)), mat, vec), semantics=("parallel",),
                  vmem_limit=VMEM_LIMIT)(proj, win, wbd, scale, dmixed)


def _sconv_fwd(proj, w, *, name):
    t = proj.shape[0]

    def body(cb_ref, cc_ref, ch_ref, w_ref, y_ref):
        y_ref[...] = (cb_ref[...] * _conv_fwd(cc_ref[...] * ch_ref[...], w_ref, CONV_TAPS)).astype(BF16)

    nb = CONV_DIM // LANE
    return _pcall(body, name=name, out_shape=jax.ShapeDtypeStruct((t, CONV_DIM), BF16), grid=(nb,),
                  in_specs=[_col(t, lambda b: (0, CB_COL + b)), _col(t, lambda b: (0, CC_COL + b)),
                            _col(t, lambda b: (0, CH_COL + b)), pl.BlockSpec((CONV_TAPS, LANE), lambda b: (0, b))],
                  out_specs=_col(t, lambda b: (0, b)), semantics=("parallel",), vmem_limit=VMEM_LIMIT)(proj, proj, proj, w)


def _sconv_bwd(proj, w, dmixed, *, name):
    t = proj.shape[0]

    def body(cb_ref, cc_ref, ch_ref, w_ref, d_ref, dcb_ref, dcc_ref, dch_ref, dw_ref):
        cc = cc_ref[...]
        ch = ch_ref[...]
        u = cc * ch
        dv = d_ref[...]
        dcb_ref[...] = (dv * _conv_fwd(u, w_ref, CONV_TAPS)).astype(BF16)
        du = _conv_bwd(u, dv * cb_ref[...], w_ref, dw_ref, CONV_TAPS)
        dcc_ref[...] = (du * ch).astype(BF16)
        dch_ref[...] = (du * cc).astype(BF16)

    nb = CONV_DIM // LANE
    first = (A_DIM + POOL_DIM) // LANE
    act = jax.ShapeDtypeStruct((t, CONV_DIM), BF16)
    wspec = pl.BlockSpec((CONV_TAPS, LANE), lambda b: (0, b))
    ospec = _col(t, lambda b: (0, b))
    return _pcall(body, name=name, out_shape=(act, act, act, jax.ShapeDtypeStruct((CONV_TAPS, CONV_DIM), F32)), grid=(nb,),
                  in_specs=[_col(t, lambda b: (0, CB_COL + b)), _col(t, lambda b: (0, CC_COL + b)),
                            _col(t, lambda b: (0, CH_COL + b)), wspec, _col(t, lambda b: (0, first + b))],
                  out_specs=(ospec, ospec, ospec, wspec), semantics=("parallel",),
                  vmem_limit=VMEM_LIMIT)(proj, proj, proj, w, dmixed)


def _chunk_masks():
    r = lax.broadcasted_iota(jnp.int32, (CHUNK, CHUNK), 0)
    c = lax.broadcasted_iota(jnp.int32, (CHUNK, CHUNK), 1)
    return r >= c, r > c, jnp.where(r == c, 1.0, 0.0).astype(F32)


def _split(a):
    hi = a.astype(BF16)
    return hi, (a - hi.astype(F32)).astype(BF16)


def _dot_split(a, b, dims):
    (ah, al), (bh, bl) = a, b
    return _dot(ah, bh, dims) + _dot(ah, bl, dims) + _dot(al, bh, dims)


def _tri_inv(lows, eye):
    xs = [eye - low for low in lows]
    ps = [_split(low) for low in lows]
    ps = [_split(_dot_split(p, p, NN)) for p in ps]
    for i in range(5):
        xs = [x + _dot_split(_split(x), p, NN) for x, p in zip(xs, ps)]
        if i < 4:
            ps = [_split(_dot_split(p, p, NN)) for p in ps]
    return xs


def _prefix_sum_rows(x):
    for k in range(6):
        x = x + _shift_down(x, 1 << k)
    return x


def _suffix_sum_rows(x):
    for k in range(6):
        x = x + _shift_up(x, 1 << k)
    return x


def _chunk_decay(g, incl):
    gcb = _prefix_sum_rows(g)
    gtot = _colsum(g)
    col = gcb[:, :CHUNK]
    row = gcb.T[:CHUNK, :]
    decay = jnp.exp(jnp.where(incl, col - row, -1e30))
    return gcb, gtot, decay


CHUNKS_PER_STEP = 2


def _heads_of(ref, base, rows):
    return [ref[base + h, rows, :] for h in range(HEADS)]


def _chunk_rows(j):
    return pl.ds(j * CHUNK, CHUNK)


def _deltanet_prep(qkv, g, beta, *, name):
    t = qkv.shape[1]
    n_chunks = t // CHUNK
    per = CHUNKS_PER_STEP
    probs = [(j, h) for j in range(per) for h in range(HEADS)]

    def body(qkv_ref, g_ref, b_ref, u_ref, w_ref, qg_ref, kg_ref, attn_ref, tm_ref):
        incl, strict, eye = _chunk_masks()
        q = [qkv_ref[h, _chunk_rows(j), :] for j, h in probs]
        k = [qkv_ref[HEADS + h, _chunk_rows(j), :] for j, h in probs]
        v = [qkv_ref[2 * HEADS + h, _chunk_rows(j), :] for j, h in probs]
        bv = [b_ref[h, _chunk_rows(j), :] for j, h in probs]
        dec = [_chunk_decay(g_ref[h, _chunk_rows(j), :], incl) for j, h in probs]
        kb = [a * b for a, b in zip(k, bv)]
        low = [jnp.where(strict, _dot(a, b, NT) * d[2], 0.0) for a, b, d in zip(kb, k, dec)]
        tm = _tri_inv(low, eye)
        egc = [jnp.exp(d[0]) for d in dec]
        u = [_dot(m, a * b, NN) for m, a, b in zip(tm, v, bv)]
        w = [_dot(m, a * e, NN) for m, a, e in zip(tm, kb, egc)]
        attn = [_dot(a, b, NT) * d[2] for a, b, d in zip(q, k, dec)]
        for i, (j, h) in enumerate(probs):
            rows = _chunk_rows(j)
            u_ref[h, rows, :] = u[i]
            w_ref[h, rows, :] = w[i].astype(BF16)
            qg_ref[h, rows, :] = (q[i] * egc[i]).astype(BF16)
            kg_ref[h, rows, :] = (k[i] * jnp.exp(dec[i][1] - dec[i][0])).astype(BF16)
            attn_ref[j, h] = attn[i].astype(BF16)
            tm_ref[j, h] = tm[i]

    act = lambda heads: pl.BlockSpec((heads, per * CHUNK, LANE), lambda n: (0, n, 0))
    mat = pl.BlockSpec((per, HEADS, CHUNK, CHUNK), lambda n: (n, 0, 0, 0))
    return _pcall(
        body, name=name,
        out_shape=(jax.ShapeDtypeStruct((HEADS, t, LANE), F32),) + (jax.ShapeDtypeStruct((HEADS, t, LANE), BF16),) * 3
        + (jax.ShapeDtypeStruct((n_chunks, HEADS, CHUNK, CHUNK), BF16), jax.ShapeDtypeStruct((n_chunks, HEADS, CHUNK, CHUNK), F32)),
        grid=(n_chunks // per,), in_specs=[act(3 * HEADS), act(HEADS), act(HEADS)],
        out_specs=(act(HEADS),) * 4 + (mat, mat), semantics=("parallel",), vmem_limit=VMEM_LIMIT)(qkv, g, beta)


SCAN_CHUNKS_PER_STEP = 4


def _deltanet_scan(u, w, qg, kg, attn, g, *, name):
    t = u.shape[1]
    n_chunks = t // CHUNK
    per = SCAN_CHUNKS_PER_STEP

    def body(u_ref, w_ref, qg_ref, kg_ref, attn_ref, g_ref, o_ref, vn_ref, st_ref, s_ref):
        @pl.when(pl.program_id(0) == 0)
        def _():
            s_ref[...] = jnp.zeros_like(s_ref)

        for j in range(per):
            rows = _chunk_rows(j)
            s = [s_ref[h] for h in range(HEADS)]
            vn = [u_ref[h, rows, :] - _dot(w_ref[h, rows, :], s[h], NN) for h in range(HEADS)]
            o = [_dot(qg_ref[h, rows, :], s[h], NN) + _dot(attn_ref[j, h], vn[h], NN) for h in range(HEADS)]
            eg = [jnp.exp(_colsum(g_ref[h, rows, :])) for h in range(HEADS)]
            for h in range(HEADS):
                st_ref[j, h] = s[h]
                s_ref[h] = s[h] * eg[h] + _dot(kg_ref[h, rows, :], vn[h], TN)
                o_ref[h, rows, :] = o[h]
                vn_ref[h, rows, :] = vn[h]

    act = pl.BlockSpec((HEADS, per * CHUNK, LANE), lambda n: (0, n, 0))
    out = jax.ShapeDtypeStruct((HEADS, t, LANE), F32)
    return _pcall(
        body, name=name, out_shape=(out, out, jax.ShapeDtypeStruct((n_chunks, HEADS, LANE, LANE), F32)), grid=(n_chunks // per,),
        in_specs=[act] * 4 + [pl.BlockSpec((per, HEADS, CHUNK, CHUNK), lambda n: (n, 0, 0, 0)), act],
        out_specs=(act, act, pl.BlockSpec((per, HEADS, LANE, LANE), lambda n: (n, 0, 0, 0))),
        scratch_shapes=[pltpu.VMEM((HEADS, LANE, LANE), F32)], semantics=("arbitrary",))(u, w, qg, kg, attn, g)


def _deltanet_bscan(w, qg, kg, attn, g, do, *, name):
    t = w.shape[1]
    n_chunks = t // CHUNK
    per = SCAN_CHUNKS_PER_STEP
    steps = n_chunks // per

    def body(w_ref, qg_ref, kg_ref, attn_ref, g_ref, do_ref, dvn_ref, dsn_ref, ds_ref):
        @pl.when(pl.program_id(0) == 0)
        def _():
            ds_ref[...] = jnp.zeros_like(ds_ref)

        for j in reversed(range(per)):
            rows = _chunk_rows(j)
            dsn = [ds_ref[h] for h in range(HEADS)]
            dov = [do_ref[h, rows, :] for h in range(HEADS)]
            dvn = [_dot(attn_ref[j, h], dov[h], TN) + _dot(kg_ref[h, rows, :], dsn[h], NN) for h in range(HEADS)]
            eg = [jnp.exp(_colsum(g_ref[h, rows, :])) for h in range(HEADS)]
            for h in range(HEADS):
                dsn_ref[j, h] = dsn[h]
                ds_ref[h] = _dot(qg_ref[h, rows, :], dov[h], TN) + eg[h] * dsn[h] - _dot(w_ref[h, rows, :], dvn[h], TN)
                dvn_ref[h, rows, :] = dvn[h]

    act = pl.BlockSpec((HEADS, per * CHUNK, LANE), lambda n: (0, steps - 1 - n, 0))
    return _pcall(
        body, name=name,
        out_shape=(jax.ShapeDtypeStruct((HEADS, t, LANE), F32), jax.ShapeDtypeStruct((n_chunks, HEADS, LANE, LANE), F32)),
        grid=(steps,),
        in_specs=[act] * 3 + [pl.BlockSpec((per, HEADS, CHUNK, CHUNK), lambda n: (steps - 1 - n, 0, 0, 0)), act, act],
        out_specs=(act, pl.BlockSpec((per, HEADS, LANE, LANE), lambda n: (steps - 1 - n, 0, 0, 0))),
        scratch_shapes=[pltpu.VMEM((HEADS, LANE, LANE), F32)], semantics=("arbitrary",))(w, qg, kg, attn, g, do)


def _sum_all(x):
    return jnp.sum(jnp.sum(x, axis=1, keepdims=True), axis=0, keepdims=True)


def _rowsum(x):
    return jnp.sum(x, axis=1, keepdims=True)


def _deltanet_post(qkv, g, beta, tmats, states, dstates, do, dvn, vn, *, name):
    t = qkv.shape[1]
    n_chunks = t // CHUNK
    per = CHUNKS_PER_STEP
    probs = [(j, h) for j in range(per) for h in range(HEADS)]

    def body(qkv_ref, g_ref, b_ref, tm_ref, st_ref, dsn_ref, do_ref, dvn_ref, vn_ref, dqkv_ref, dg_ref, db_ref):
        incl, strict, _ = _chunk_masks()
        ones = jnp.ones((CHUNK, LANE), BF16)
        last_row = lax.broadcasted_iota(jnp.int32, (CHUNK, LANE), 0) == CHUNK - 1
        z = lambda f, *cols: [f(*a) for a in zip(*cols)]
        q = [qkv_ref[h, _chunk_rows(j), :] for j, h in probs]
        k = [qkv_ref[HEADS + h, _chunk_rows(j), :] for j, h in probs]
        v = [qkv_ref[2 * HEADS + h, _chunk_rows(j), :] for j, h in probs]
        bv = [b_ref[h, _chunk_rows(j), :] for j, h in probs]
        dov = [do_ref[h, _chunk_rows(j), :] for j, h in probs]
        dvn_ = [dvn_ref[h, _chunk_rows(j), :] for j, h in probs]
        vn_ = [vn_ref[h, _chunk_rows(j), :] for j, h in probs]
        tm = [tm_ref[j, h] for j, h in probs]
        s = [st_ref[j, h] for j, h in probs]
        dsn = [dsn_ref[j, h] for j, h in probs]
        dec = [_chunk_decay(g_ref[h, _chunk_rows(j), :], incl) for j, h in probs]
        decay = [d[2] for d in dec]
        egc = [jnp.exp(d[0]) for d in dec]
        ekg = [jnp.exp(d[1] - d[0]) for d in dec]
        kb = z(lambda a, b: a * b, k, bv)
        vb = z(lambda a, b: a * b, v, bv)
        kbg = z(lambda a, b: a * b, kb, egc)
        qg = z(lambda a, b: a * b, q, egc)
        kg = z(lambda a, b: a * b, k, ekg)
        kk = z(lambda a, b: _dot(a, b, NT), kb, k)
        qk = z(lambda a, b: _dot(a, b, NT), q, k)
        dattn = z(lambda a, b: jnp.where(incl, _dot(a, b, NT), 0.0), dov, vn_)
        dqg = z(lambda a, b: _dot(a, b, NT), dov, s)
        dkg = z(lambda a, b: _dot(a, b, NT), vn_, dsn)
        dglast = z(lambda a, b, c, d, e: _sum_all(a * b) * jnp.exp(e[1]) + _sum_all(c * d), s, dsn, dkg, kg, dec)
        dw = z(lambda a, b: -_dot(a, b, NT), dvn_, s)
        dtm = z(lambda a, b, c, d: _dot(a, b, NT) + _dot(c, d, NT), dvn_, vb, dw, kbg)
        dvb = z(lambda a, b: _dot(a, b, TN), tm, dvn_)
        dkbg = z(lambda a, b: _dot(a, b, TN), tm, dw)
        dlow = z(lambda a, b: jnp.where(strict, -_dot(_dot(a, b, TN), a, NT), 0.0), tm, dtm)
        dkk = z(lambda a, b: a * b, dlow, decay)
        dqk = z(lambda a, b: a * b, dattn, decay)
        dkb = z(lambda a, b, c, d: _dot(a, b, NN) + c * d, dkk, k, dkbg, egc)
        dk = z(lambda a, b, c, d, e, f, g_, h_: _dot(a, b, TN) + _dot(c, d, TN) + e * f + g_ * h_, dkk, kb, dqk, q, dkg, ekg, dkb, bv)
        dq = z(lambda a, b, c, d: _dot(a, b, NN) + c * d, dqk, k, dqg, egc)
        m = z(lambda a, b, c, d, e: (a * b + c * d) * e, dlow, kk, dattn, qk, decay)
        mcol = [_dot(mh, ones, TN) + _dot(ml, ones, TN) for mh, ml in (_split(a) for a in m)]
        for i, (j, h) in enumerate(probs):
            rows = _chunk_rows(j)
            dqkv_ref[h, rows, :] = dq[i]
            dqkv_ref[HEADS + h, rows, :] = dk[i]
            dqkv_ref[2 * HEADS + h, rows, :] = dvb[i] * bv[i]
            db_ref[h, rows, :] = jnp.broadcast_to(_rowsum(dkb[i] * k[i] + dvb[i] * v[i]), (CHUNK, LANE))
            dgc = (_rowsum(dqg[i] * qg[i] + dkbg[i] * kbg[i] - dkg[i] * kg[i]) + _rowsum(m[i]) - mcol[i]
                   + jnp.where(last_row, dglast[i], 0.0))
            dg_ref[h, rows, :] = _suffix_sum_rows(dgc)

    act = lambda heads: pl.BlockSpec((heads, per * CHUNK, LANE), lambda n: (0, n, 0))
    mat = lambda d: pl.BlockSpec((per, HEADS, d, d), lambda n: (n, 0, 0, 0))
    out = jax.ShapeDtypeStruct((HEADS, t, LANE), F32)
    return _pcall(
        body, name=name, out_shape=(jax.ShapeDtypeStruct((3 * HEADS, t, LANE), F32), out, out), grid=(n_chunks // per,),
        in_specs=[act(3 * HEADS), act(HEADS), act(HEADS), mat(CHUNK), mat(LANE), mat(LANE), act(HEADS), act(HEADS), act(HEADS)],
        out_specs=(act(3 * HEADS), act(HEADS), act(HEADS)), semantics=("parallel",),
        vmem_limit=VMEM_LIMIT)(qkv, g, beta, tmats, states, dstates, do, dvn, vn)


ANY = pl.BlockSpec(memory_space=pl.ANY)
PEERS = N_DEV - 1


def _all_gather(arrays, *, name):
    n = len(arrays)

    def body(*refs):
        ins, outs = refs[:n], refs[n:2 * n]
        send_sems, recv_sems, local_sems = refs[2 * n:]
        x, y, c = lax.axis_index("x"), lax.axis_index("y"), lax.axis_index("c")
        me, sibling = (x, y, c), (x, y, 1 - c)
        chips = [(1 - x, y), (x, 1 - y), (1 - x, 1 - y)]

        def copy(a, k, block, to, src=None):
            dst = outs[a].at[4 * block[0] + 2 * block[1] + block[2]]
            return pltpu.make_async_remote_copy(src_ref=dst if src is None else src, dst_ref=dst, send_sem=send_sems.at[a * PEERS + k],
                                                recv_sem=recv_sems.at[a * PEERS + k], device_id=to, device_id_type=MESH)

        local = [pltpu.make_async_copy(ins[a], outs[a].at[4 * x + 2 * y + c], local_sems.at[a]) for a in range(n)]
        for cp in local:
            cp.start()
        first = []
        for a in range(n):
            first.append(copy(a, 0, me, sibling, src=ins[a]))
            first += [copy(a, 1 + j, me, (*chip, c), src=ins[a]) for j, chip in enumerate(chips)]
        for cp in first:
            cp.start()
        passed = []
        for a in range(n):
            for j, chip in enumerate(chips):
                copy(a, 1 + j, (*chip, c), me).wait_recv()
                fwd = copy(a, 4 + j, (*chip, c), sibling)
                fwd.start()
                passed.append(fwd)
        for a in range(n):
            copy(a, 0, sibling, me).wait_recv()
            for j, chip in enumerate(chips):
                copy(a, 4 + j, (*chip, 1 - c), me).wait_recv()
        for cp in first + passed:
            cp.wait_send()
        for cp in local:
            cp.wait()

    return _pcall(body, name=name, out_shape=tuple(jax.ShapeDtypeStruct((N_DEV,) + a.shape, a.dtype) for a in arrays),
                  in_specs=[ANY] * n, out_specs=(ANY,) * n,
                  scratch_shapes=[pltpu.SemaphoreType.DMA((n * PEERS,)), pltpu.SemaphoreType.DMA((n * PEERS,)),
                                  pltpu.SemaphoreType.DMA((n,))])(*arrays)


CHIPS = 4


def _pair_exchange(arrays, *, name):
    n = len(arrays)

    def body(*refs):
        ins, outs = refs[:n], refs[n:2 * n]
        send_sems, recv_sems = refs[2 * n:]
        x, y, c = lax.axis_index("x"), lax.axis_index("y"), lax.axis_index("c")
        copies = []
        for a in range(n):
            for q in range(CHIPS):
                cp = pltpu.make_async_remote_copy(src_ref=ins[a].at[2 * q + 1 - c], dst_ref=outs[a].at[q],
                                                  send_sem=send_sems.at[a * CHIPS + q], recv_sem=recv_sems.at[a * CHIPS + q],
                                                  device_id=(x, y, 1 - c), device_id_type=MESH)
                cp.start()
                copies.append(cp)
        for cp in copies:
            cp.wait()

    return _pcall(body, name=name, out_shape=tuple(jax.ShapeDtypeStruct((CHIPS,) + a.shape[1:], a.dtype) for a in arrays),
                  in_specs=[ANY] * n, out_specs=(ANY,) * n,
                  scratch_shapes=[pltpu.SemaphoreType.DMA((n * CHIPS,)), pltpu.SemaphoreType.DMA((n * CHIPS,))])(*arrays)


def _pair_add(blocks, theirs, *, name):
    _, r, c_ = blocks.shape
    tr = _tile(r, 512, 16)

    def body(mine_ref, theirs_ref, o_ref):
        core = lax.axis_index("c")
        own = jnp.where(core == 0, mine_ref[0, 0].astype(F32), mine_ref[0, 1].astype(F32))
        o_ref[0] = (own + theirs_ref[0].astype(F32)).astype(o_ref.dtype)

    spec = pl.BlockSpec((1, tr, c_), lambda q, i: (q, i, 0))
    return _pcall(body, name=name, out_shape=jax.ShapeDtypeStruct(theirs.shape, theirs.dtype), grid=(CHIPS, r // tr),
                  in_specs=[pl.BlockSpec((1, 2, tr, c_), lambda q, i: (q, 0, i, 0)), spec], out_specs=spec,
                  semantics=("parallel", "parallel"), vmem_limit=VMEM_LIMIT)(blocks.reshape(CHIPS, 2, r, c_), theirs)


HBM = pl.BlockSpec(memory_space=pltpu.HBM)
SEM = pl.BlockSpec(memory_space=pltpu.SEMAPHORE)
EFFECT = pltpu.SideEffectType.DATAFLOW_SIDE_EFFECTING


GATHER, CHIP_SCATTER = "gather", "chip_scatter"
PEERS_OF = {GATHER: N_DEV - 1, CHIP_SCATTER: CHIPS - 1}


def _direct_copies(srcs, lands, send_sems, recv_sems, local_sems, kind):
    x, y, c = lax.axis_index("x"), lax.axis_index("y"), lax.axis_index("c")
    peers = PEERS_OF[kind]
    copies = []
    for a, (src, land) in enumerate(zip(srcs, lands)):
        if kind == GATHER:
            mine = 4 * x + 2 * y + c
            copies.append(pltpu.make_async_copy(src, land.at[mine], local_sems.at[a]))
        else:
            mine = 2 * x + y
            copies.append(pltpu.make_async_copy(src.at[mine], land.at[mine], local_sems.at[a]))
        for k in range(1, peers + 1):
            bits = k if kind == GATHER else 2 * k
            px = 1 - x if bits & 4 else x
            py = 1 - y if bits & 2 else y
            pc = 1 - c if bits & 1 else c
            copies.append(pltpu.make_async_remote_copy(
                src_ref=src if kind == GATHER else src.at[2 * px + py], dst_ref=land.at[mine],
                send_sem=send_sems.at[a * peers + k - 1], recv_sem=recv_sems.at[a * peers + k - 1],
                device_id=(px, py, pc), device_id_type=MESH))
    return copies


def _exchange_start(groups, kind, *, name):
    srcs = [s for group in groups for s in group]
    n = len(srcs)
    sizes = [len(group) for group in groups]
    starts = [sum(sizes[:g]) for g in range(len(groups))]
    land_shapes = [(N_DEV,) + s.shape if kind == GATHER else s.shape for s in srcs]
    peers = PEERS_OF[kind]

    def body(*refs):
        srcs_, lands = refs[:n], refs[n:2 * n]
        token = refs[-1]
        for g, (at, size) in enumerate(zip(starts, sizes)):
            send_sems, recv_sems, local_sems = refs[2 * n + 3 * g:2 * n + 3 * g + 3]
            for cp in _direct_copies(srcs_[at:at + size], lands[at:at + size], send_sems, recv_sems, local_sems, kind):
                cp.start()
        token[...] = jnp.zeros_like(token)

    sems = tuple(t for size in sizes for t in (pltpu.SemaphoreType.DMA((size * peers,)), pltpu.SemaphoreType.DMA((size * peers,)),
                                               pltpu.SemaphoreType.DMA((size,))))
    thru = tuple(pltpu.HBM(s.shape, s.dtype) for s in srcs) + tuple(pltpu.HBM(shp, s.dtype) for shp, s in zip(land_shapes, srcs))
    ins = [pltpu.with_memory_space_constraint(s, pltpu.HBM) for s in srcs]
    ins += [pltpu.with_memory_space_constraint(lax.empty(shp, s.dtype), pltpu.HBM) for shp, s in zip(land_shapes, srcs)]
    out = pl.pallas_call(
        body, name=name, out_shape=sems + thru + (jax.ShapeDtypeStruct((SUBLANE, LANE), F32),), in_specs=[HBM] * (2 * n),
        out_specs=(SEM,) * len(sems) + (HBM,) * (2 * n) + (pl.BlockSpec(memory_space=pltpu.VMEM),),
        input_output_aliases={i: len(sems) + i for i in range(2 * n)},
        compiler_params=pltpu.CompilerParams(has_side_effects=EFFECT))(*ins)
    arrays = out[len(sems):-1]
    started = [tuple(out[3 * g:3 * g + 3]) + tuple(arrays[at:at + size]) + tuple(arrays[n + at:n + at + size])
               for g, (at, size) in enumerate(zip(starts, sizes))]
    return started, out[-1]


def _exchange_wait(started, after, kind, *, name):
    n = (len(started) - 3) // 2
    sems, arrays = started[:3], started[3:]

    def body(*refs):
        srcs_, lands = refs[:n], refs[n:2 * n]
        send_sems, recv_sems, local_sems = refs[2 * n:2 * n + 3]
        for cp in _direct_copies(srcs_, lands, send_sems, recv_sems, local_sems, kind):
            cp.wait()

    out = pl.pallas_call(
        body, name=name, out_shape=tuple(pltpu.HBM(a.shape, a.dtype) for a in arrays),
        in_specs=[HBM] * (2 * n) + [SEM] * 3 + [ANY], out_specs=(HBM,) * (2 * n),
        input_output_aliases={i: i for i in range(2 * n)},
        compiler_params=pltpu.CompilerParams(has_side_effects=EFFECT))(*arrays, *sems, after)
    return out[n:]


def _adamw_reduce(w, parts, m, v, *, name):
    layers, r, c = w.shape
    assert len(parts) == layers
    senders = parts[0].shape[0]
    tr = _tile(r, 512, 16)
    tiles = r // tr
    bc1 = 1.0 - ADAM_B1 ** ADAM_STEP
    bc2 = 1.0 - ADAM_B2 ** ADAM_STEP

    def body(w_ref, *rest):
        p_refs = rest[:layers]
        m_ref, v_ref, g_ref, d_ref, nm_ref, nv_ref = rest[layers:]

        def update(p_ref):
            g = p_ref[0, :, pl.ds(0, c)].astype(F32)
            for s in range(1, senders):
                g = g + p_ref[s, :, pl.ds(0, c)].astype(F32)
            nm = ADAM_B1 * m_ref[0] + (1.0 - ADAM_B1) * g
            nv = ADAM_B2 * v_ref[0] + (1.0 - ADAM_B2) * (g * g)
            g_ref[0] = g
            nm_ref[0] = nm
            nv_ref[0] = nv
            d_ref[0] = -ADAM_LR * ((nm / bc1) / (jnp.sqrt(nv / bc2) + ADAM_EPS) + ADAM_WD * w_ref[0])

        for layer in range(layers):
            pl.when(pl.program_id(0) == layer)(functools.partial(update, p_refs[layer]))

    def part_spec(layer, shape):
        rest = 0 if layer > 0 else tiles - 1
        return pl.BlockSpec((senders, tr, shape[2]), lambda l, i: (0, jnp.where(l == layer, i, rest), 0))

    spec = pl.BlockSpec((1, tr, c), lambda l, i: (l, i, 0))
    out = jax.ShapeDtypeStruct((layers, r, c), F32)
    return _pcall(body, name=name, out_shape=(out,) * 4, grid=(layers, tiles),
                  in_specs=[spec] + [part_spec(layer, p.shape) for layer, p in enumerate(parts)] + [spec, spec],
                  out_specs=(spec,) * 4, semantics=("arbitrary", "arbitrary"), vmem_limit=VMEM_LIMIT)(w, *parts, m, v)


def _pool_windows():
    return jnp.repeat(jnp.asarray(POOL_WINDOWS, F32), POOL_DIM // len(POOL_WINDOWS))[None, :]


def _block_diag_pairs(pool_w):
    z = jnp.zeros_like(pool_w[0])
    return jnp.stack([jnp.block([[pool_w[2 * b], z], [z, pool_w[2 * b + 1]]]) for b in range(2)])


def _pad_lanes(vec):
    return jnp.zeros((1, LANE), F32).at[0, :vec.shape[0]].set(vec)


FF_SHARD = D_FF // N_DEV
FF_BLOCK = 384
D_FF_PAD = N_DEV * FF_BLOCK


def _layer_fwd(x, p_i, wt, fetch):
    wt = {**wt, **fetch(0, x)}
    h1 = _rmsnorm_fwd(x, wt["norm1_g"], name="rmsnorm_fwd")
    proj = _matmul(h1, wt["w_in"], "nn", name="mm_in")
    qkv = _qkv_prep_fwd(proj, wt["conv_qkv"], name="qkv_prep_fwd")
    g, beta = _gates_fwd(proj, wt["a_log"], wt["dt_bias"], name="gates_fwd")
    u, w, qg, kg, attn, tmats = _deltanet_prep(qkv, g, beta, name="deltanet_prep")
    o, vn, states = _deltanet_scan(u, w, qg, kg, attn, g, name="deltanet_scan")
    o_a = _apost_fwd(o, proj, wt["onorm_g"], name="apost_fwd")
    o_b = _pool_fwd(proj, wt["pool_win"], wt["pool_wbd"], wt["pool_scale"], name="pool_fwd")
    o_c = _sconv_fwd(proj, wt["sconv_w"], name="sconv_fwd")
    mixed = jnp.concatenate([o_a, o_b, o_c], axis=1)
    wt.update(fetch(1, mixed))
    x1 = _matmul(mixed, wt["w_out"], "nn", res=x, name="mm_out")
    h2 = _rmsnorm_fwd(x1, wt["norm2_g"], name="rmsnorm_fwd")
    wt.update(fetch(2, h2))
    gate = _matmul(h2, wt["w_gate"], "nn", b_blocked=True, name="mm_gate")
    up = _matmul(h2, wt["w_up"], "nn", b_blocked=True, name="mm_up")
    ff = _swiglu_fwd(gate, up, name="swiglu_fwd")
    wt.update(fetch(3, ff))
    x2 = _matmul(ff, wt["w_down"], "nn", res=x1, name="mm_down")
    wt.update(fetch(4, x2))
    pgl = _matmul(x2, wt["ple_gate"], "nn", name="mm_pleg")
    pp = _matmul(p_i, wt["ple_proj"], "nn", b_blocked=True, name="mm_plep")
    x3 = _ple_fwd(x2, pgl, pp, name="ple_fwd")
    saved = dict(x=x, h1=h1, proj=proj, qkv=qkv, g=g, beta=beta, o=o, states=states, tmats=tmats, mixed=mixed, x1=x1, h2=h2,
                 gate=gate, up=up, ff=ff, x2=x2, pgl=pgl, pp=pp, p=p_i, w=w, qg=qg, kg=kg, attn=attn, vn=vn, wt=wt)
    return x3, saved


def _col_blocks(g):
    a = g.shape[0]
    return jnp.transpose(g.reshape(a, N_DEV, -1), (1, 0, 2))


def _cols_joined(blocks):
    return jnp.transpose(blocks, (1, 0, 2)).reshape(blocks.shape[1], -1)


def _layer_bwd(dx3, sv, emit):
    gr, big = {}, {}
    wt = sv["wt"]
    rows = D_MODEL // N_DEV
    dpgl, dpp = _ple_bwd(dx3, sv["pgl"], sv["pp"], name="ple_bwd")
    big["ple_proj"] = _matmul(sv["p"], dpp, "tn", out_blocked=(N_DEV, rows), out_dtype=BF16, name="mm_dplep")
    big["ple_gate"] = _matmul(sv["x2"], dpgl, "tn", out_dtype=BF16, name="mm_dpleg").reshape(N_DEV, rows, D_MODEL)
    dx2 = _matmul(dpgl, wt["ple_gate"], "nt", res=dx3, name="mm_dx2")
    big["w_down"] = _matmul(sv["ff"], dx2, "tn", out_dtype=BF16, name="mm_ddown").reshape(N_DEV, FF_BLOCK, D_MODEL)
    emit(0, big)
    dff = _matmul(dx2, wt["w_down"], "nt", name="mm_dff")
    dgate, dup = _swiglu_bwd(sv["gate"], sv["up"], dff, name="swiglu_bwd")
    big["w_gate"] = _matmul(sv["h2"], dgate, "tn", out_blocked=(N_DEV, FF_BLOCK), out_dtype=BF16, name="mm_dgate")
    big["w_up"] = _matmul(sv["h2"], dup, "tn", out_blocked=(N_DEV, FF_BLOCK), out_dtype=BF16, name="mm_dup")
    dh2 = _matmul(dgate, wt["w_gate"], "nt", b_blocked=True, name="mm_dh2_gate")
    dh2 = _matmul(dup, wt["w_up"], "nt", b_blocked=True, res=dh2, name="mm_dh2_up")
    dx1, gr["norm2_g"] = _rmsnorm_bwd(sv["x1"], wt["norm2_g"], dh2, dx2, name="rmsnorm_bwd")
    big["w_out"] = _matmul(sv["mixed"], dx1, "tn", out_dtype=BF16, name="mm_dout").reshape(N_DEV, rows, D_MODEL)
    emit(1, big)
    dmixed = _matmul(dx1, wt["w_out"], "nt", name="mm_dmixed")
    proj = sv["proj"]
    dcb, dcc, dch, dsconv = _sconv_bwd(proj, wt["sconv_w"], dmixed, name="sconv_bwd")
    big["sconv_w"] = _col_blocks(dsconv)
    dhp, dwbd, gr["pool_scale"] = _pool_bwd(proj, wt["pool_win"], wt["pool_wbd"], wt["pool_scale"], dmixed, name="pool_bwd")
    half = LANE // 2
    gr["pool_w"] = jnp.stack([dwbd[0, :half, :half], dwbd[0, half:, half:], dwbd[1, :half, :half], dwbd[1, half:, half:]])
    do, dz, gr["onorm_g"] = _apost_bwd(sv["o"], proj, wt["onorm_g"], dmixed, name="apost_bwd")
    dvn, dstates = _deltanet_bscan(sv["w"], sv["qg"], sv["kg"], sv["attn"], sv["g"], do, name="deltanet_bscan")
    dqkv_h, dg, dbeta = _deltanet_post(sv["qkv"], sv["g"], sv["beta"], sv["tmats"], sv["states"], dstates, do, dvn, sv["vn"],
                                       name="deltanet_post")
    dab, dalog, ddtb = _gates_bwd(proj, wt["a_log"], wt["dt_bias"], dg, dbeta, name="gates_bwd")
    gr["a_log"], gr["dt_bias"] = dalog[0, :HEADS], ddtb[0, :HEADS]
    dqkv, dconv = _qkv_prep_bwd(proj, wt["conv_qkv"], dqkv_h, name="qkv_prep_bwd")
    big["conv_qkv"] = _col_blocks(dconv)
    dproj = jnp.concatenate([dqkv, dz, dab, dhp, dcb, dcc, dch], axis=1)
    dwin = _matmul(sv["h1"], dproj, "tn", out_dtype=BF16, name="mm_din")
    big["w_in"] = _col_blocks(jnp.concatenate([dwin[:, :AB_COL + 2 * HEADS], dwin[:, AB_COL + LANE:]], axis=1))
    emit(2, big)
    dh1 = _matmul(dproj, wt["w_in"], "nt", name="mm_dh1")
    dx, gr["norm1_g"] = _rmsnorm_bwd(sv["x"], wt["norm1_g"], dh1, dx1, name="rmsnorm_bwd")
    return dx, gr


FETCH_GROUPS = (("w_in", "conv_qkv", "sconv_w"), ("w_out",), ("w_gate", "w_up"), ("w_down",), ("ple_gate", "ple_proj"))
EMIT_GROUPS = (("ple_proj", "ple_gate", "w_down"), ("w_gate", "w_up", "w_out"), ("w_in", "conv_qkv", "sconv_w"))


def _small_weights(w, i):
    return dict(
        norm1_g=w["norm1_g"][i][None], norm2_g=w["norm2_g"][i][None], onorm_g=w["onorm_g"][i][None],
        a_log=_pad_lanes(w["a_log"][i]), dt_bias=_pad_lanes(w["dt_bias"][i]),
        pool_scale=w["pool_scale"][i][None], pool_win=_pool_windows(), pool_wbd=_block_diag_pairs(w["pool_w"][i]))


def _as_read(name, gathered):
    if name == "w_in":
        w_in = _cols_joined(gathered)
        return jnp.concatenate([w_in[:, :AB_COL + 2 * HEADS], jnp.zeros((D_MODEL, LANE - 2 * HEADS), BF16),
                                w_in[:, AB_COL + 2 * HEADS:]], axis=1)
    if name in ("conv_qkv", "sconv_w"):
        return _cols_joined(gathered)
    if name in ("w_gate", "w_up", "ple_proj"):
        return gathered
    return gathered.reshape(-1, D_MODEL)


def _layer_weights(gathered, w, i):
    return {**_small_weights(w, i), **{k: _as_read(k, g) for k, g in gathered.items()}}


def _local_step(x, p, target, layers, final_g):
    saved = []
    h = x
    for i in range(DEPTH):
        replicated = {k: v for k, v in layers[i].items() if k not in SHARDED}
        h, sv = _layer_fwd(h, p[i], replicated, lambda group, after, i=i: {k: layers[i][k] for k in FETCH_GROUPS[group]})
        saved.append(sv)
    dx, dgf, loss = _loss_head(h, final_g, target, name="loss_head")
    big, small = [{} for _ in range(DEPTH)], [None] * DEPTH
    for i in reversed(range(DEPTH)):
        dx, small[i] = _layer_bwd(dx, saved[i], lambda group, blocks, i=i: big[i].update({k: blocks[k] for k in EMIT_GROUPS[group]}))
    return loss, dx, big, small, dgf


SHARDED = ("w_in", "w_gate", "w_up", "w_down", "w_out", "ple_gate", "ple_proj", "conv_qkv", "sconv_w")
SMALL = ("norm1_g", "a_log", "dt_bias", "onorm_g", "pool_w", "pool_scale", "norm2_g", "final_g")
SLAB_COLS = 1024


def _payload(name, shard):
    if name in ("conv_qkv", "sconv_w"):
        return shard
    out = shard.astype(BF16)
    if name in ("w_gate", "w_up"):
        out = jnp.pad(out, ((0, 0), (0, FF_BLOCK - FF_SHARD)))
    if name == "w_down":
        out = jnp.pad(out, ((0, FF_BLOCK - FF_SHARD), (0, 0)))
    return out


def _slab_rows(shape):
    size = 1
    for s in shape:
        size *= s
    return SUBLANE * -(-size // (SUBLANE * SLAB_COLS))


def _pack_slab(parts, extra_row):
    rows = []
    for name in SMALL:
        flat = parts[name].reshape(-1)
        nrow = _slab_rows(parts[name].shape)
        rows.append(jnp.pad(flat, (0, nrow * SLAB_COLS - flat.shape[0])).reshape(nrow, SLAB_COLS))
    rows.append(jnp.pad(extra_row, ((0, SUBLANE - 1), (0, 0))))
    return jnp.concatenate(rows, axis=0)


def _unpack_slab(slab, shapes):
    out, row = {}, 0
    for name in SMALL:
        size = 1
        for s in shapes[name]:
            size *= s
        out[name] = slab[row:row + _slab_rows(shapes[name])].reshape(-1)[:size].reshape(shapes[name])
        row += _slab_rows(shapes[name])
    return out, row


def kernel(x, p, norm1_g, w_in, conv_qkv, a_log, dt_bias, onorm_g, pool_w, pool_scale, sconv_w, w_out, norm2_g, w_gate, w_up, w_down, ple_proj, ple_gate, final_g, loss_target, m_norm1_g, m_w_in, m_conv_qkv, m_a_log, m_dt_bias, m_onorm_g, m_pool_w, m_pool_scale, m_sconv_w, m_w_out, m_norm2_g, m_w_gate, m_w_up, m_w_down, m_ple_proj, m_ple_gate, m_final_g, v_norm1_g, v_w_in, v_conv_qkv, v_a_log, v_dt_bias, v_onorm_g, v_pool_w, v_pool_scale, v_sconv_w, v_w_out, v_norm2_g, v_w_gate, v_w_up, v_w_down, v_ple_proj, v_ple_gate, v_final_g):
    names = ["norm1_g", "w_in", "conv_qkv", "a_log", "dt_bias", "onorm_g", "pool_w", "pool_scale", "sconv_w", "w_out", "norm2_g",
             "w_gate", "w_up", "w_down", "ple_proj", "ple_gate", "final_g"]
    w = dict(zip(names, [norm1_g, w_in, conv_qkv, a_log, dt_bias, onorm_g, pool_w, pool_scale, sconv_w, w_out, norm2_g, w_gate, w_up,
                         w_down, ple_proj, ple_gate, final_g]))
    m = dict(zip(names, [m_norm1_g, m_w_in, m_conv_qkv, m_a_log, m_dt_bias, m_onorm_g, m_pool_w, m_pool_scale, m_sconv_w, m_w_out,
                         m_norm2_g, m_w_gate, m_w_up, m_w_down, m_ple_proj, m_ple_gate, m_final_g]))
    v = dict(zip(names, [v_norm1_g, v_w_in, v_conv_qkv, v_a_log, v_dt_bias, v_onorm_g, v_pool_w, v_pool_scale, v_sconv_w, v_w_out,
                         v_norm2_g, v_w_gate, v_w_up, v_w_down, v_ple_proj, v_ple_gate, v_final_g]))

    gathered = dict(zip(SHARDED, _all_gather([_payload(k, w[k][0]) for k in SHARDED], name="all_gather_weights")))
    (flying,), token = _exchange_start([[_payload(k, w[k][1]) for k in SHARDED]], GATHER, name="gather_start")
    replicated = [_small_weights(w, i) for i in range(DEPTH)]
    replicated[0]["norm1_g"] = replicated[0]["norm1_g"] + token[0, 0]

    def fetch(i, group, after):
        if i == 1 and group == 0:
            gathered.update(zip(SHARDED, _exchange_wait(flying, after, GATHER, name="gather_wait")))
        return {k: _as_read(k, gathered[k]) for k in FETCH_GROUPS[group]}

    def reduce_scatter_start(members, blocks, tag):
        mine = [blocks[k] for k in members]
        theirs = _pair_exchange(mine, name="pair_exchange")
        sums = [_pair_add(a, b, name="pair_add") for a, b in zip(mine, theirs)]
        (started,), token = _exchange_start([sums], CHIP_SCATTER, name="exchange_start_" + tag)
        return started, token

    h, saved0 = _layer_fwd(x[0], p[0, 0], replicated[0], functools.partial(fetch, 0))
    h, saved1 = _layer_fwd(h, p[1, 0], replicated[1], functools.partial(fetch, 1))
    dx, dgf, loss_part = _loss_head(h, final_g[None], loss_target[0], name="loss_head")
    small, big1, flying0 = [None] * DEPTH, {}, []
    dx, small[1] = _layer_bwd(dx, saved1, lambda group, blocks: big1.update({k: blocks[k] for k in EMIT_GROUPS[group]}))
    flying1, token = reduce_scatter_start(SHARDED, big1, "1")
    dx, small[0] = _layer_bwd(dx + token[0, 0], saved0,
                              lambda group, blocks: flying0.append(reduce_scatter_start(EMIT_GROUPS[group], blocks, f"0_{group}")[0]))
    received = [{}, dict(zip(SHARDED, _exchange_wait(flying1, dx, CHIP_SCATTER, name="exchange_wait_1")))]
    for group, members in enumerate(EMIT_GROUPS):
        received[0].update(zip(members, _exchange_wait(flying0[group], dx, CHIP_SCATTER, name=f"exchange_wait_0_{group}")))

    grads = {k: jnp.stack([small[i][k] for i in range(DEPTH)]) for k in small[0]}
    grads = {k: g[:, 0] if k in ("norm1_g", "norm2_g", "onorm_g", "pool_scale") else g for k, g in grads.items()}
    grads["final_g"] = dgf[0]
    loss_row = jnp.pad(loss_part, ((0, 0), (0, SLAB_COLS - LANE)))
    (small_parts,) = _all_gather([_pack_slab(grads, loss_row)], name="all_gather_small_grads")

    out_g, out_d, out_m, out_v = {}, {}, {}, {}
    for k in SHARDED:
        out_g[k], out_d[k], out_m[k], out_v[k] = _adamw_reduce(w[k], [received[i][k] for i in range(DEPTH)], m[k], v[k],
                                                                name="adamw_" + k)
    zero_row = jnp.zeros((1, SLAB_COLS), F32)
    slabs = _adamw_reduce(_pack_slab(w, zero_row)[None], [small_parts], _pack_slab(m, zero_row)[None],
                          _pack_slab(v, zero_row)[None], name="adamw_small")
    slabs = [s[0] for s in slabs]
    shapes = {k: w[k].shape for k in SMALL}
    for dst, slab in zip((out_g, out_d, out_m, out_v), slabs):
        vals, _ = _unpack_slab(slab, shapes)
        dst.update(vals)
    _, loss_at = _unpack_slab(slabs[0], shapes)
    loss = slabs[0][loss_at, 0]

    return (loss, dx[None], *[out_g[k] for k in names], *[out_d[k] for k in names], *[out_m[k] for k in names],
            *[out_v[k] for k in names])
```

```python
import functools

import jax
import jax.numpy as jnp
from jax import lax
from jax.experimental import pallas as pl
from jax.experimental.pallas import tpu as pltpu

F32 = jnp.float32
BF16 = jnp.bfloat16

D_MODEL = 1024
DEPTH = 2
PLE_DIM = 256
EPS = 1e-6
HEAD_DIM = 128
HEADS = 4
A_DIM = HEADS * HEAD_DIM
QKV_TAPS = 4
CHUNK = 64
POOL_WINDOWS = (2, 4, 8, 16)
POOL_DIM = 256
CONV_DIM = 256
CONV_TAPS = 3
D_FF = 2816
D_IN = 3080
D_IN_PAD = 3200
AB_COL = 2048
N_DEV = 8

ADAM_LR = 0.001
ADAM_B1 = 0.9
ADAM_B2 = 0.999
ADAM_EPS = 1e-08
ADAM_WD = 0.01
ADAM_STEP = 10

LANE = 128
SUBLANE = 8
VMEM_BYTES_V7X = 64 * 1024 * 1024
VMEM_LIMIT = 48 * 1024 * 1024

_HI = lax.Precision.HIGHEST
NN = ((1,), (0,))
NT = ((1,), (1,))
TN = ((0,), (0,))
MESH = pl.DeviceIdType.MESH


def _dot(a, b, dims, hi=False):
    if hi:
        return lax.dot_general(a, b, (dims, ((), ())), precision=_HI, preferred_element_type=F32)
    return lax.dot_general(a.astype(BF16), b.astype(BF16), (dims, ((), ())), preferred_element_type=F32)


def _pcall(body, *, name, out_shape, grid=(), in_specs=None, out_specs=None, scratch_shapes=(), semantics=None,
           vmem_limit=None, after=None, **kw):
    params = {}
    if semantics is not None:
        params["dimension_semantics"] = semantics
    if vmem_limit is not None:
        params["vmem_limit_bytes"] = vmem_limit
    if after is not None:
        n_in, inner = len(in_specs), body
        body = lambda *refs: inner(*refs[:n_in], *refs[n_in + 1:])
        in_specs = list(in_specs) + [pl.BlockSpec(after.shape, lambda *_: (0,) * after.ndim)]
    call = pl.pallas_call(
        body, name=name, out_shape=out_shape, grid=grid, in_specs=in_specs, out_specs=out_specs,
        scratch_shapes=list(scratch_shapes), compiler_params=pltpu.CompilerParams(**params), **kw)
    return call if after is None else (lambda *args: call(*args, after))


def _sigmoid(x):
    return 1.0 / (1.0 + jnp.exp(-x))


def _softplus(x):
    return jnp.maximum(x, 0.0) + jnp.log(1.0 + jnp.exp(-jnp.abs(x)))


def _tile(n, cap, mult):
    if n <= cap:
        return n
    best = None
    for t in range(mult, cap + 1, mult):
        if n % t == 0:
            best = t
    assert best is not None, (n, cap, mult)
    return best


ROWS_PER_STEP = 512
COLS_PER_DOT = 640


def _matmul_rows(a, b, mode, *, name, res=None, out_dtype=F32, b_blocked=False, after=None):
    m, k = a.shape
    if b_blocked:
        nb, _, bw = b.shape
        n = nb * bw if mode == "nn" else b.shape[1]
    else:
        n = b.shape[1] if mode == "nn" else b.shape[0]
    tm = _tile(m, ROWS_PER_STEP, 16)
    cn = bw if (b_blocked and mode == "nn") else _tile(n, COLS_PER_DOT, LANE)
    has_res = res is not None

    def body(*refs):
        a_ref, b_ref = refs[0], refs[1]
        res_ref = refs[2] if has_res else None
        o_ref = refs[2 + has_res]
        if not (b_blocked and mode == "nt"):
            av = a_ref[...].astype(BF16)
        for j in range(n // cn):
            cols = pl.ds(j * cn, cn)
            if mode == "nn":
                part = _dot(av, b_ref[j] if b_blocked else b_ref[:, cols], NN)
            elif not b_blocked:
                part = _dot(av, b_ref[cols, :], NT)
            else:
                part = None
                for s in range(nb):
                    term = _dot(a_ref[:, pl.ds(s * bw, bw)], b_ref[s, cols, :], NT)
                    part = term if part is None else part + term
            if has_res:
                part = part + res_ref[:, cols]
            o_ref[:, cols] = part.astype(o_ref.dtype)

    row = lambda width: pl.BlockSpec((tm, width), lambda i: (i, 0))
    whole = pl.BlockSpec(b.shape, lambda i: (0,) * b.ndim)
    ins = [a, b] + ([res] if has_res else [])
    specs = [row(k), whole] + ([row(n)] if has_res else [])
    return _pcall(body, name=name, out_shape=jax.ShapeDtypeStruct((m, n), out_dtype), grid=(m // tm,), in_specs=specs,
                  out_specs=row(n), semantics=("parallel",), vmem_limit=VMEM_LIMIT, after=after)(*ins)


def _matmul(a, b, mode, *, name, res=None, out_dtype=F32, b_blocked=False, out_blocked=None, after=None):
    if mode != "tn":
        return _matmul_rows(a, b, mode, name=name, res=res, out_dtype=out_dtype, b_blocked=b_blocked, after=after)
    assert res is None and not b_blocked and after is None
    (t, m), (t2, n) = a.shape, b.shape
    assert t == t2, (a.shape, b.shape)
    tm = _tile(m, 1024, LANE)
    tn = _tile(n, COLS_PER_DOT, LANE)
    if out_blocked is not None:
        assert out_blocked[0] * out_blocked[1] == n
        tn = out_blocked[1]

    def body(a_ref, b_ref, o_ref):
        part = _dot(a_ref[...], b_ref[...], TN).astype(o_ref.dtype)
        if out_blocked is None:
            o_ref[...] = part
        else:
            o_ref[0] = part

    o_spec = (pl.BlockSpec((tm, tn), lambda i, j: (i, j)) if out_blocked is None
              else pl.BlockSpec((1, tm, tn), lambda i, j: (j, i, 0)))
    o_shape = (m, n) if out_blocked is None else (out_blocked[0], m, out_blocked[1])
    return _pcall(body, name=name, out_shape=jax.ShapeDtypeStruct(o_shape, out_dtype), grid=(m // tm, n // tn),
                  in_specs=[pl.BlockSpec((t, tm), lambda i, j: (0, i)), pl.BlockSpec((t, tn), lambda i, j: (0, j))],
                  out_specs=o_spec, semantics=("parallel", "parallel"), vmem_limit=VMEM_LIMIT)(a, b)


ROW_TILE = 256


def _rows(t, width, idx=0):
    return pl.BlockSpec((ROW_TILE, width), lambda i: (i, idx))


def _vec(width):
    return pl.BlockSpec((1, width), lambda i: (0, 0))


def _rmsnorm_fwd(x, g, *, name):
    t, d = x.shape

    def body(x_ref, g_ref, h_ref):
        xv = x_ref[...]
        r = lax.rsqrt(jnp.mean(xv * xv, axis=-1, keepdims=True) + EPS)
        h_ref[...] = (xv * r * g_ref[...]).astype(BF16)

    return _pcall(body, name=name, out_shape=jax.ShapeDtypeStruct((t, d), BF16), grid=(t // ROW_TILE,),
                  in_specs=[_rows(t, d), _vec(d)], out_specs=_rows(t, d), semantics=("parallel",))(x, g)


def _rmsnorm_bwd(x, g, dh, dres, *, name):
    t, d = x.shape

    def body(x_ref, g_ref, dh_ref, dres_ref, dx_ref, dg_ref):
        xv = x_ref[...]
        r = lax.rsqrt(jnp.mean(xv * xv, axis=-1, keepdims=True) + EPS)
        xhat = xv * r
        dhv = dh_ref[...].astype(F32)
        dhg = dhv * g_ref[...]
        dx_ref[...] = dres_ref[...] + r * (dhg - xhat * jnp.mean(dhg * xhat, axis=-1, keepdims=True))
        part = jnp.sum(dhv * xhat, axis=0, keepdims=True)

        @pl.when(pl.program_id(0) == 0)
        def _():
            dg_ref[...] = part

        @pl.when(pl.program_id(0) > 0)
        def _():
            dg_ref[...] += part

    return _pcall(body, name=name, out_shape=(jax.ShapeDtypeStruct((t, d), F32), jax.ShapeDtypeStruct((1, d), F32)),
                  grid=(t // ROW_TILE,), in_specs=[_rows(t, d), _vec(d), _rows(t, d), _rows(t, d)],
                  out_specs=(_rows(t, d), _vec(d)), semantics=("arbitrary",))(x, g, dh, dres)


def _swiglu_fwd(gate, up, *, name):
    t, f = gate.shape

    def body(gate_ref, up_ref, ff_ref):
        gv = gate_ref[...]
        ff_ref[...] = (gv * _sigmoid(gv) * up_ref[...]).astype(BF16)

    return _pcall(body, name=name, out_shape=jax.ShapeDtypeStruct((t, f), BF16), grid=(t // ROW_TILE,),
                  in_specs=[_rows(t, f), _rows(t, f)], out_specs=_rows(t, f), semantics=("parallel",))(gate, up)


def _swiglu_bwd(gate, up, dff, *, name):
    t, f = gate.shape

    def body(gate_ref, up_ref, dff_ref, dgate_ref, dup_ref):
        gv = gate_ref[...]
        sig = _sigmoid(gv)
        dffv = dff_ref[...]
        dgate_ref[...] = (dffv * up_ref[...] * sig * (1.0 + gv * (1.0 - sig))).astype(BF16)
        dup_ref[...] = (dffv * gv * sig).astype(BF16)

    out = jax.ShapeDtypeStruct((t, f), BF16)
    return _pcall(body, name=name, out_shape=(out, out), grid=(t // ROW_TILE,), in_specs=[_rows(t, f)] * 3,
                  out_specs=(_rows(t, f),) * 2, semantics=("parallel",))(gate, up, dff)


def _ple_fwd(x2, pgl, pp, *, name):
    t, d = x2.shape

    def body(x_ref, pgl_ref, pp_ref, o_ref):
        o_ref[...] = x_ref[...] + _sigmoid(pgl_ref[...]) * pp_ref[...]

    return _pcall(body, name=name, out_shape=jax.ShapeDtypeStruct((t, d), F32), grid=(t // ROW_TILE,),
                  in_specs=[_rows(t, d)] * 3, out_specs=_rows(t, d), semantics=("parallel",))(x2, pgl, pp)


def _ple_bwd(dx3, pgl, pp, *, name, after=None):
    t, d = dx3.shape

    def body(dx_ref, pgl_ref, pp_ref, dpgl_ref, dpp_ref):
        dxv = dx_ref[...]
        sig = _sigmoid(pgl_ref[...])
        dpp_ref[...] = (dxv * sig).astype(BF16)
        dpgl_ref[...] = (dxv * pp_ref[...] * sig * (1.0 - sig)).astype(BF16)

    return _pcall(body, name=name, out_shape=(jax.ShapeDtypeStruct((t, d), BF16),) * 2, grid=(t // ROW_TILE,),
                  in_specs=[_rows(t, d)] * 3, out_specs=(_rows(t, d),) * 2, semantics=("parallel",), after=after)(dx3, pgl, pp)


def _loss_head(x3, g, target, *, name):
    t, d = x3.shape

    def body(x_ref, g_ref, t_ref, dx_ref, dg_ref, loss_ref):
        xv = x_ref[...]
        r = lax.rsqrt(jnp.mean(xv * xv, axis=-1, keepdims=True) + EPS)
        xhat = xv * r
        gv = g_ref[...]
        err = xhat * gv - t_ref[...]
        row_loss = jnp.sum(err * err, axis=-1, keepdims=True) * (0.5 / d)
        lpart = jnp.broadcast_to(jnp.sum(row_loss, axis=0, keepdims=True), (1, LANE))
        dy = err * (1.0 / d)
        dyg = dy * gv
        dx_ref[...] = r * (dyg - xhat * jnp.mean(dyg * xhat, axis=-1, keepdims=True))
        gpart = jnp.sum(dy * xhat, axis=0, keepdims=True)

        @pl.when(pl.program_id(0) == 0)
        def _():
            dg_ref[...] = gpart
            loss_ref[...] = lpart

        @pl.when(pl.program_id(0) > 0)
        def _():
            dg_ref[...] += gpart
            loss_ref[...] += lpart

    return _pcall(body, name=name,
                  out_shape=(jax.ShapeDtypeStruct((t, d), F32), jax.ShapeDtypeStruct((1, d), F32), jax.ShapeDtypeStruct((1, LANE), F32)),
                  grid=(t // ROW_TILE,), in_specs=[_rows(t, d), _vec(d), _rows(t, d)],
                  out_specs=(_rows(t, d), _vec(d), _vec(LANE)), semantics=("arbitrary",))(x3, g, target)


def _shift_down(x, d):
    if d == 0:
        return x
    row = lax.broadcasted_iota(jnp.int32, x.shape, 0)
    return jnp.where(row >= d, pltpu.roll(x, d, 0), 0.0)


def _shift_up(x, d):
    if d == 0:
        return x
    t = x.shape[0]
    row = lax.broadcasted_iota(jnp.int32, x.shape, 0)
    return jnp.where(row < t - d, pltpu.roll(x, t - d, 0), 0.0)


def _colsum(x):
    return jnp.sum(x, axis=0, keepdims=True)


def _col(t, idx_fn):
    return pl.BlockSpec((t, LANE), idx_fn)


def _conv_fwd(x, w_ref, taps):
    acc = None
    for j in range(taps):
        term = w_ref[pl.ds(j, 1), :] * _shift_down(x, taps - 1 - j)
        acc = term if acc is None else acc + term
    return acc


def _conv_bwd(x, dy, w_ref, dw_ref, taps):
    dx = None
    for j in range(taps):
        term = w_ref[pl.ds(j, 1), :] * _shift_up(dy, taps - 1 - j)
        dx = term if dx is None else dx + term
        dw_ref[pl.ds(j, 1), :] = _colsum(dy * _shift_down(x, taps - 1 - j))
    return dx


def _qkv_prep_fwd(proj, conv_w, *, name):
    t = proj.shape[0]
    scale = HEAD_DIM ** -0.5

    def body(x_ref, w_ref, o_ref):
        j = pl.program_id(0)
        c = _conv_fwd(x_ref[...], w_ref, QKV_TAPS)
        s = c * _sigmoid(c)
        r = lax.rsqrt(jnp.sum(s * s, axis=-1, keepdims=True) + EPS)
        f = jnp.where(j < 2 * HEADS, r, 1.0) * jnp.where(j < HEADS, scale, 1.0)
        o_ref[0] = s * f

    return _pcall(body, name=name, out_shape=jax.ShapeDtypeStruct((3 * HEADS, t, LANE), F32), grid=(3 * HEADS,),
                  in_specs=[_col(t, lambda j: (0, j)), pl.BlockSpec((QKV_TAPS, LANE), lambda j: (0, j))],
                  out_specs=pl.BlockSpec((1, t, LANE), lambda j: (j, 0, 0)), semantics=("parallel",),
                  vmem_limit=VMEM_LIMIT)(proj, conv_w)


def _qkv_prep_bwd(proj, conv_w, dqkv, *, name):
    t = proj.shape[0]
    scale = HEAD_DIM ** -0.5

    def body(x_ref, w_ref, d_ref, dx_ref, dw_ref):
        j = pl.program_id(0)
        xv = x_ref[...]
        c = _conv_fwd(xv, w_ref, QKV_TAPS)
        sig = _sigmoid(c)
        s = c * sig
        r = lax.rsqrt(jnp.sum(s * s, axis=-1, keepdims=True) + EPS)
        n0 = s * r
        dv = d_ref[0]
        dn0 = dv * jnp.where(j < HEADS, scale, 1.0)
        ds_norm = r * (dn0 - n0 * jnp.sum(dn0 * n0, axis=-1, keepdims=True))
        ds = jnp.where(j < 2 * HEADS, ds_norm, dv)
        dc = ds * sig * (1.0 + c * (1.0 - sig))
        dx_ref[...] = _conv_bwd(xv, dc, w_ref, dw_ref, QKV_TAPS).astype(BF16)

    return _pcall(body, name=name,
                  out_shape=(jax.ShapeDtypeStruct((t, 3 * A_DIM), BF16), jax.ShapeDtypeStruct((QKV_TAPS, 3 * A_DIM), F32)),
                  grid=(3 * HEADS,),
                  in_specs=[_col(t, lambda j: (0, j)), pl.BlockSpec((QKV_TAPS, LANE), lambda j: (0, j)),
                            pl.BlockSpec((1, t, LANE), lambda j: (j, 0, 0))],
                  out_specs=(_col(t, lambda j: (0, j)), pl.BlockSpec((QKV_TAPS, LANE), lambda j: (0, j))),
                  semantics=("parallel",), vmem_limit=VMEM_LIMIT)(proj, conv_w, dqkv)


def _lane_pick(x, lane_idx, lane):
    return jnp.broadcast_to(jnp.sum(jnp.where(lane == lane_idx, x, 0.0), axis=-1, keepdims=True), x.shape)


def _gates_fwd(proj, alog, dtb, *, name):
    t = proj.shape[0]

    def body(x_ref, alog_ref, dtb_ref, g_ref, b_ref):
        xv = x_ref[...]
        lane = lax.broadcasted_iota(jnp.int32, xv.shape, 1)
        gall = -jnp.exp(alog_ref[...]) * _softplus(xv + dtb_ref[...])
        ball = _sigmoid(xv)
        for h in range(HEADS):
            g_ref[h] = _lane_pick(gall, h, lane)
            b_ref[h] = _lane_pick(ball, HEADS + h, lane)

    out = jax.ShapeDtypeStruct((HEADS, t, LANE), F32)
    whole = pl.BlockSpec((HEADS, t, LANE), lambda i: (0, 0, 0))
    return _pcall(body, name=name, out_shape=(out, out), grid=(1,),
                  in_specs=[_col(t, lambda i: (0, AB_COL // LANE)), _vec(LANE), _vec(LANE)], out_specs=(whole, whole),
                  semantics=("arbitrary",), vmem_limit=VMEM_LIMIT)(proj, alog, dtb)


def _gates_bwd(proj, alog, dtb, dg, dbeta, *, name):
    t = proj.shape[0]

    def body(x_ref, alog_ref, dtb_ref, dg_ref, db_ref, dab_ref, dalog_ref, ddtb_ref):
        xv = x_ref[...]
        lane = lax.broadcasted_iota(jnp.int32, xv.shape, 1)
        lane1 = lax.broadcasted_iota(jnp.int32, (1, LANE), 1)
        z = xv + dtb_ref[...]
        nea = -jnp.exp(alog_ref[...])
        da_f = nea * _sigmoid(z)
        g_f = nea * _softplus(z)
        ball = _sigmoid(xv)
        db_f = ball * (1.0 - ball)
        dab = jnp.zeros_like(xv)
        dalog = jnp.zeros((1, LANE), F32)
        for h in range(HEADS):
            dgh = dg_ref[h]
            dab = dab + jnp.where(lane == h, dgh * da_f, 0.0) + jnp.where(lane == HEADS + h, db_ref[h] * db_f, 0.0)
            dalog = dalog + jnp.where(lane1 == h, _colsum(dgh * g_f), 0.0)
        dab_ref[...] = dab.astype(BF16)
        dalog_ref[...] = dalog
        ddtb_ref[...] = jnp.where(lane1 < HEADS, _colsum(dab), 0.0)

    whole = pl.BlockSpec((HEADS, t, LANE), lambda i: (0, 0, 0))
    vec = jax.ShapeDtypeStruct((1, LANE), F32)
    return _pcall(body, name=name, out_shape=(jax.ShapeDtypeStruct((t, LANE), BF16), vec, vec), grid=(1,),
                  in_specs=[_col(t, lambda i: (0, AB_COL // LANE)), _vec(LANE), _vec(LANE), whole, whole],
                  out_specs=(_col(t, lambda i: (0, 0)), _vec(LANE), _vec(LANE)), semantics=("arbitrary",),
                  vmem_limit=VMEM_LIMIT)(proj, alog, dtb, dg, dbeta)


Z_COL = 3 * A_DIM // LANE


def _apost_fwd(o, proj, gn, *, name):
    t = proj.shape[0]

    def body(o_ref, z_ref, gn_ref, y_ref):
        ov = o_ref[0]
        z = z_ref[...]
        r = lax.rsqrt(jnp.mean(ov * ov, axis=-1, keepdims=True) + EPS)
        y_ref[...] = (ov * r * gn_ref[...] * (z * _sigmoid(z))).astype(BF16)

    return _pcall(body, name=name, out_shape=jax.ShapeDtypeStruct((t, A_DIM), BF16), grid=(HEADS,),
                  in_specs=[pl.BlockSpec((1, t, LANE), lambda h: (h, 0, 0)), _col(t, lambda h: (0, Z_COL + h)),
                            pl.BlockSpec((1, LANE), lambda h: (0, 0))],
                  out_specs=_col(t, lambda h: (0, h)), semantics=("parallel",), vmem_limit=VMEM_LIMIT)(o, proj, gn)


def _apost_bwd(o, proj, gn, dmixed, *, name):
    t = proj.shape[0]

    def body(o_ref, z_ref, gn_ref, d_ref, do_ref, dz_ref, dgn_ref):
        ov = o_ref[0]
        z = z_ref[...]
        gnv = gn_ref[...]
        dv = d_ref[...]
        r = lax.rsqrt(jnp.mean(ov * ov, axis=-1, keepdims=True) + EPS)
        ohat = ov * r
        sig = _sigmoid(z)
        dy = dv * (z * sig)
        dz_ref[...] = (dv * ohat * gnv * sig * (1.0 + z * (1.0 - sig))).astype(BF16)
        dyo = dy * gnv
        do_ref[0] = r * (dyo - ohat * jnp.mean(dyo * ohat, axis=-1, keepdims=True))
        part = _colsum(dy * ohat)

        @pl.when(pl.program_id(0) == 0)
        def _():
            dgn_ref[...] = part

        @pl.when(pl.program_id(0) > 0)
        def _():
            dgn_ref[...] += part

    return _pcall(body, name=name,
                  out_shape=(jax.ShapeDtypeStruct((HEADS, t, LANE), F32), jax.ShapeDtypeStruct((t, A_DIM), BF16),
                             jax.ShapeDtypeStruct((1, LANE), F32)),
                  grid=(HEADS,),
                  in_specs=[pl.BlockSpec((1, t, LANE), lambda h: (h, 0, 0)), _col(t, lambda h: (0, Z_COL + h)),
                            pl.BlockSpec((1, LANE), lambda h: (0, 0)), _col(t, lambda h: (0, h))],
                  out_specs=(pl.BlockSpec((1, t, LANE), lambda h: (h, 0, 0)), _col(t, lambda h: (0, h)),
                             pl.BlockSpec((1, LANE), lambda h: (0, 0))),
                  semantics=("arbitrary",), vmem_limit=VMEM_LIMIT)(o, proj, gn, dmixed)


POOL_COL = (AB_COL + LANE) // LANE
CB_COL = POOL_COL + POOL_DIM // LANE
CC_COL = CB_COL + CONV_DIM // LANE
CH_COL = CC_COL + CONV_DIM // LANE
MAX_WIN_LOG2 = 4


def _window_sums(x, shift):
    sums = []
    cur = x
    for k in range(MAX_WIN_LOG2):
        cur = cur + shift(cur, 1 << k)
        sums.append(cur)
    return sums


def _pick_window(sums, win):
    out = sums[-1]
    for k in range(MAX_WIN_LOG2 - 2, -1, -1):
        out = jnp.where(win == float(2 << k), sums[k], out)
    return out


def _pool_counts(shape, win):
    row = lax.broadcasted_iota(jnp.int32, shape, 0).astype(F32)
    return jnp.minimum(row + 1.0, win)


def _pool_fwd(proj, win, wbd, scale, *, name):
    t = proj.shape[0]

    def body(x_ref, win_ref, w_ref, s_ref, y_ref):
        xv = x_ref[...]
        winv = win_ref[...]
        pooled = _pick_window(_window_sums(xv, _shift_down), winv) / _pool_counts(xv.shape, winv) - xv
        y_ref[...] = (_dot(pooled, w_ref[0], NN) * s_ref[...]).astype(BF16)

    nb = POOL_DIM // LANE
    vec = pl.BlockSpec((1, LANE), lambda b: (0, b))
    return _pcall(body, name=name, out_shape=jax.ShapeDtypeStruct((t, POOL_DIM), BF16), grid=(nb,),
                  in_specs=[_col(t, lambda b: (0, POOL_COL + b)), vec, pl.BlockSpec((1, LANE, LANE), lambda b: (b, 0, 0)), vec],
                  out_specs=_col(t, lambda b: (0, b)), semantics=("parallel",), vmem_limit=VMEM_LIMIT)(proj, win, wbd, scale)


def _pool_bwd(proj, win, wbd, scale, dmixed, *, name):
    t = proj.shape[0]

    def body(x_ref, win_ref, w_ref, s_ref, d_ref, dx_ref, dw_ref, ds_ref):
        xv = x_ref[...]
        winv = win_ref[...]
        cnt = _pool_counts(xv.shape, winv)
        pooled = _pick_window(_window_sums(xv, _shift_down), winv) / cnt - xv
        dv = d_ref[...]
        ds_ref[...] = _colsum(dv * _dot(pooled, w_ref[0], NN))
        dy0 = dv * s_ref[...]
        dw_ref[0] = _dot(pooled, dy0, TN)
        dpooled = _dot(dy0, w_ref[0], NT)
        dmean = dpooled / cnt
        dx_ref[...] = (_pick_window(_window_sums(dmean, _shift_up), winv) - dpooled).astype(BF16)

    nb = POOL_DIM // LANE
    vec = pl.BlockSpec((1, LANE), lambda b: (0, b))
    mat = pl.BlockSpec((1, LANE, LANE), lambda b: (b, 0, 0))
    first = A_DIM // LANE
    return _pcall(body, name=name,
                  out_shape=(jax.ShapeDtypeStruct((t, POOL_DIM), BF16), jax.ShapeDtypeStruct((nb, LANE, LANE), F32),
                             jax.ShapeDtypeStruct((1, POOL_DIM), F32)),
                  grid=(nb,),
                  in_specs=[_col(t, lambda b: (0, POOL_COL + b)), vec, mat, vec, _col(t, lambda b: (0, first + b))],
                  out_specs=(_col(t, lambda b: (0, b)), mat, vec), semantics=("parallel",),
                  vmem_limit=VMEM_LIMIT)(proj, win, wbd, scale, dmixed)


def _sconv_fwd(proj, w, *, name):
    t = proj.shape[0]

    def body(cb_ref, cc_ref, ch_ref, w_ref, y_ref):
        y_ref[...] = (cb_ref[...] * _conv_fwd(cc_ref[...] * ch_ref[...], w_ref, CONV_TAPS)).astype(BF16)

    nb = CONV_DIM // LANE
    return _pcall(body, name=name, out_shape=jax.ShapeDtypeStruct((t, CONV_DIM), BF16), grid=(nb,),
                  in_specs=[_col(t, lambda b: (0, CB_COL + b)), _col(t, lambda b: (0, CC_COL + b)),
                            _col(t, lambda b: (0, CH_COL + b)), pl.BlockSpec((CONV_TAPS, LANE), lambda b: (0, b))],
                  out_specs=_col(t, lambda b: (0, b)), semantics=("parallel",), vmem_limit=VMEM_LIMIT)(proj, proj, proj, w)


def _sconv_bwd(proj, w, dmixed, *, name):
    t = proj.shape[0]

    def body(cb_ref, cc_ref, ch_ref, w_ref, d_ref, dcb_ref, dcc_ref, dch_ref, dw_ref):
        cc = cc_ref[...]
        ch = ch_ref[...]
        u = cc * ch
        dv = d_ref[...]
        dcb_ref[...] = (dv * _conv_fwd(u, w_ref, CONV_TAPS)).astype(BF16)
        du = _conv_bwd(u, dv * cb_ref[...], w_ref, dw_ref, CONV_TAPS)
        dcc_ref[...] = (du * ch).astype(BF16)
        dch_ref[...] = (du * cc).astype(BF16)

    nb = CONV_DIM // LANE
    first = (A_DIM + POOL_DIM) // LANE
    act = jax.ShapeDtypeStruct((t, CONV_DIM), BF16)
    wspec = pl.BlockSpec((CONV_TAPS, LANE), lambda b: (0, b))
    ospec = _col(t, lambda b: (0, b))
    return _pcall(body, name=name, out_shape=(act, act, act, jax.ShapeDtypeStruct((CONV_TAPS, CONV_DIM), F32)), grid=(nb,),
                  in_specs=[_col(t, lambda b: (0, CB_COL + b)), _col(t, lambda b: (0, CC_COL + b)),
                            _col(t, lambda b: (0, CH_COL + b)), wspec, _col(t, lambda b: (0, first + b))],
                  out_specs=(ospec, ospec, ospec, wspec), semantics=("parallel",),
                  vmem_limit=VMEM_LIMIT)(proj, proj, proj, w, dmixed)


def _chunk_masks():
    r = lax.broadcasted_iota(jnp.int32, (CHUNK, CHUNK), 0)
    c = lax.broadcasted_iota(jnp.int32, (CHUNK, CHUNK), 1)
    return r >= c, r > c, jnp.where(r == c, 1.0, 0.0).astype(F32)


def _split(a):
    hi = a.astype(BF16)
    return hi, (a - hi.astype(F32)).astype(BF16)


def _dot_split(a, b, dims):
    (ah, al), (bh, bl) = a, b
    return _dot(ah, bh, dims) + _dot(ah, bl, dims) + _dot(al, bh, dims)


def _tri_inv(lows, eye):
    xs = [eye - low for low in lows]
    ps = [_split(low) for low in lows]
    ps = [_split(_dot_split(p, p, NN)) for p in ps]
    for i in range(5):
        xs = [x + _dot_split(_split(x), p, NN) for x, p in zip(xs, ps)]
        if i < 4:
            ps = [_split(_dot_split(p, p, NN)) for p in ps]
    return xs


def _prefix_sum_rows(x):
    for k in range(6):
        x = x + _shift_down(x, 1 << k)
    return x


def _suffix_sum_rows(x):
    for k in range(6):
        x = x + _shift_up(x, 1 << k)
    return x


def _chunk_decay(g, incl):
    gcb = _prefix_sum_rows(g)
    gtot = _colsum(g)
    col = gcb[:, :CHUNK]
    row = gcb.T[:CHUNK, :]
    decay = jnp.exp(jnp.where(incl, col - row, -1e30))
    return gcb, gtot, decay


CHUNKS_PER_STEP = 2


def _heads_of(ref, base, rows):
    return [ref[base + h, rows, :] for h in range(HEADS)]


def _chunk_rows(j):
    return pl.ds(j * CHUNK, CHUNK)


def _deltanet_prep(qkv, g, beta, *, name):
    t = qkv.shape[1]
    n_chunks = t // CHUNK
    per = CHUNKS_PER_STEP
    probs = [(j, h) for j in range(per) for h in range(HEADS)]

    def body(qkv_ref, g_ref, b_ref, u_ref, w_ref, qg_ref, kg_ref, attn_ref, tm_ref):
        incl, strict, eye = _chunk_masks()
        q = [qkv_ref[h, _chunk_rows(j), :] for j, h in probs]
        k = [qkv_ref[HEADS + h, _chunk_rows(j), :] for j, h in probs]
        v = [qkv_ref[2 * HEADS + h, _chunk_rows(j), :] for j, h in probs]
        bv = [b_ref[h, _chunk_rows(j), :] for j, h in probs]
        dec = [_chunk_decay(g_ref[h, _chunk_rows(j), :], incl) for j, h in probs]
        kb = [a * b for a, b in zip(k, bv)]
        low = [jnp.where(strict, _dot(a, b, NT) * d[2], 0.0) for a, b, d in zip(kb, k, dec)]
        tm = _tri_inv(low, eye)
        egc = [jnp.exp(d[0]) for d in dec]
        u = [_dot(m, a * b, NN) for m, a, b in zip(tm, v, bv)]
        w = [_dot(m, a * e, NN) for m, a, e in zip(tm, kb, egc)]
        attn = [_dot(a, b, NT) * d[2] for a, b, d in zip(q, k, dec)]
        for i, (j, h) in enumerate(probs):
            rows = _chunk_rows(j)
            u_ref[h, rows, :] = u[i]
            w_ref[h, rows, :] = w[i].astype(BF16)
            qg_ref[h, rows, :] = (q[i] * egc[i]).astype(BF16)
            kg_ref[h, rows, :] = (k[i] * jnp.exp(dec[i][1] - dec[i][0])).astype(BF16)
            attn_ref[j, h] = attn[i].astype(BF16)
            tm_ref[j, h] = tm[i]

    act = lambda heads: pl.BlockSpec((heads, per * CHUNK, LANE), lambda n: (0, n, 0))
    mat = pl.BlockSpec((per, HEADS, CHUNK, CHUNK), lambda n: (n, 0, 0, 0))
    return _pcall(
        body, name=name,
        out_shape=(jax.ShapeDtypeStruct((HEADS, t, LANE), F32),) + (jax.ShapeDtypeStruct((HEADS, t, LANE), BF16),) * 3
        + (jax.ShapeDtypeStruct((n_chunks, HEADS, CHUNK, CHUNK), BF16), jax.ShapeDtypeStruct((n_chunks, HEADS, CHUNK, CHUNK), F32)),
        grid=(n_chunks // per,), in_specs=[act(3 * HEADS), act(HEADS), act(HEADS)],
        out_specs=(act(HEADS),) * 4 + (mat, mat), semantics=("parallel",), vmem_limit=VMEM_LIMIT)(qkv, g, beta)


SCAN_CHUNKS_PER_STEP = 4


def _deltanet_scan(u, w, qg, kg, attn, g, *, name):
    t = u.shape[1]
    n_chunks = t // CHUNK
    per = SCAN_CHUNKS_PER_STEP

    def body(u_ref, w_ref, qg_ref, kg_ref, attn_ref, g_ref, o_ref, vn_ref, st_ref, s_ref):
        @pl.when(pl.program_id(0) == 0)
        def _():
            s_ref[...] = jnp.zeros_like(s_ref)

        for j in range(per):
            rows = _chunk_rows(j)
            s = [s_ref[h] for h in range(HEADS)]
            vn = [u_ref[h, rows, :] - _dot(w_ref[h, rows, :], s[h], NN) for h in range(HEADS)]
            o = [_dot(qg_ref[h, rows, :], s[h], NN) + _dot(attn_ref[j, h], vn[h], NN) for h in range(HEADS)]
            eg = [jnp.exp(_colsum(g_ref[h, rows, :])) for h in range(HEADS)]
            for h in range(HEADS):
                st_ref[j, h] = s[h]
                s_ref[h] = s[h] * eg[h] + _dot(kg_ref[h, rows, :], vn[h], TN)
                o_ref[h, rows, :] = o[h]
                vn_ref[h, rows, :] = vn[h]

    act = pl.BlockSpec((HEADS, per * CHUNK, LANE), lambda n: (0, n, 0))
    out = jax.ShapeDtypeStruct((HEADS, t, LANE), F32)
    return _pcall(
        body, name=name, out_shape=(out, out, jax.ShapeDtypeStruct((n_chunks, HEADS, LANE, LANE), F32)), grid=(n_chunks // per,),
        in_specs=[act] * 4 + [pl.BlockSpec((per, HEADS, CHUNK, CHUNK), lambda n: (n, 0, 0, 0)), act],
        out_specs=(act, act, pl.BlockSpec((per, HEADS, LANE, LANE), lambda n: (n, 0, 0, 0))),
        scratch_shapes=[pltpu.VMEM((HEADS, LANE, LANE), F32)], semantics=("arbitrary",))(u, w, qg, kg, attn, g)


def _deltanet_bscan(w, qg, kg, attn, g, do, *, name):
    t = w.shape[1]
    n_chunks = t // CHUNK
    per = SCAN_CHUNKS_PER_STEP
    steps = n_chunks // per

    def body(w_ref, qg_ref, kg_ref, attn_ref, g_ref, do_ref, dvn_ref, dsn_ref, ds_ref):
        @pl.when(pl.program_id(0) == 0)
        def _():
            ds_ref[...] = jnp.zeros_like(ds_ref)

        for j in reversed(range(per)):
            rows = _chunk_rows(j)
            dsn = [ds_ref[h] for h in range(HEADS)]
            dov = [do_ref[h, rows, :] for h in range(HEADS)]
            dvn = [_dot(attn_ref[j, h], dov[h], TN) + _dot(kg_ref[h, rows, :], dsn[h], NN) for h in range(HEADS)]
            eg = [jnp.exp(_colsum(g_ref[h, rows, :])) for h in range(HEADS)]
            for h in range(HEADS):
                dsn_ref[j, h] = dsn[h]
                ds_ref[h] = _dot(qg_ref[h, rows, :], dov[h], TN) + eg[h] * dsn[h] - _dot(w_ref[h, rows, :], dvn[h], TN)
                dvn_ref[h, rows, :] = dvn[h]

    act = pl.BlockSpec((HEADS, per * CHUNK, LANE), lambda n: (0, steps - 1 - n, 0))
    return _pcall(
        body, name=name,
        out_shape=(jax.ShapeDtypeStruct((HEADS, t, LANE), F32), jax.ShapeDtypeStruct((n_chunks, HEADS, LANE, LANE), F32)),
        grid=(steps,),
        in_specs=[act] * 3 + [pl.BlockSpec((per, HEADS, CHUNK, CHUNK), lambda n: (steps - 1 - n, 0, 0, 0)), act, act],
        out_specs=(act, pl.BlockSpec((per, HEADS, LANE, LANE), lambda n: (steps - 1 - n, 0, 0, 0))),
        scratch_shapes=[pltpu.VMEM((HEADS, LANE, LANE), F32)], semantics=("arbitrary",))(w, qg, kg, attn, g, do)


def _sum_all(x):
    return jnp.sum(jnp.sum(x, axis=1, keepdims=True), axis=0, keepdims=True)


def _rowsum(x):
    return jnp.sum(x, axis=1, keepdims=True)


def _deltanet_post(qkv, g, beta, tmats, states, dstates, do, dvn, vn, *, name):
    t = qkv.shape[1]
    n_chunks = t // CHUNK
    per = CHUNKS_PER_STEP
    probs = [(j, h) for j in range(per) for h in range(HEADS)]

    def body(qkv_ref, g_ref, b_ref, tm_ref, st_ref, dsn_ref, do_ref, dvn_ref, vn_ref, dqkv_ref, dg_ref, db_ref):
        incl, strict, _ = _chunk_masks()
        ones = jnp.ones((CHUNK, LANE), BF16)
        last_row = lax.broadcasted_iota(jnp.int32, (CHUNK, LANE), 0) == CHUNK - 1
        z = lambda f, *cols: [f(*a) for a in zip(*cols)]
        q = [qkv_ref[h, _chunk_rows(j), :] for j, h in probs]
        k = [qkv_ref[HEADS + h, _chunk_rows(j), :] for j, h in probs]
        v = [qkv_ref[2 * HEADS + h, _chunk_rows(j), :] for j, h in probs]
        bv = [b_ref[h, _chunk_rows(j), :] for j, h in probs]
        dov = [do_ref[h, _chunk_rows(j), :] for j, h in probs]
        dvn_ = [dvn_ref[h, _chunk_rows(j), :] for j, h in probs]
        vn_ = [vn_ref[h, _chunk_rows(j), :] for j, h in probs]
        tm = [tm_ref[j, h] for j, h in probs]
        s = [st_ref[j, h] for j, h in probs]
        dsn = [dsn_ref[j, h] for j, h in probs]
        dec = [_chunk_decay(g_ref[h, _chunk_rows(j), :], incl) for j, h in probs]
        decay = [d[2] for d in dec]
        egc = [jnp.exp(d[0]) for d in dec]
        ekg = [jnp.exp(d[1] - d[0]) for d in dec]
        kb = z(lambda a, b: a * b, k, bv)
        vb = z(lambda a, b: a * b, v, bv)
        kbg = z(lambda a, b: a * b, kb, egc)
        qg = z(lambda a, b: a * b, q, egc)
        kg = z(lambda a, b: a * b, k, ekg)
        kk = z(lambda a, b: _dot(a, b, NT), kb, k)
        qk = z(lambda a, b: _dot(a, b, NT), q, k)
        dattn = z(lambda a, b: jnp.where(incl, _dot(a, b, NT), 0.0), dov, vn_)
        dqg = z(lambda a, b: _dot(a, b, NT), dov, s)
        dkg = z(lambda a, b: _dot(a, b, NT), vn_, dsn)
        dglast = z(lambda a, b, c, d, e: _sum_all(a * b) * jnp.exp(e[1]) + _sum_all(c * d), s, dsn, dkg, kg, dec)
        dw = z(lambda a, b: -_dot(a, b, NT), dvn_, s)
        dtm = z(lambda a, b, c, d: _dot(a, b, NT) + _dot(c, d, NT), dvn_, vb, dw, kbg)
        dvb = z(lambda a, b: _dot(a, b, TN), tm, dvn_)
        dkbg = z(lambda a, b: _dot(a, b, TN), tm, dw)
        dlow = z(lambda a, b: jnp.where(strict, -_dot(_dot(a, b, TN), a, NT), 0.0), tm, dtm)
        dkk = z(lambda a, b: a * b, dlow, decay)
        dqk = z(lambda a, b: a * b, dattn, decay)
        dkb = z(lambda a, b, c, d: _dot(a, b, NN) + c * d, dkk, k, dkbg, egc)
        dk = z(lambda a, b, c, d, e, f, g_, h_: _dot(a, b, TN) + _dot(c, d, TN) + e * f + g_ * h_, dkk, kb, dqk, q, dkg, ekg, dkb, bv)
        dq = z(lambda a, b, c, d: _dot(a, b, NN) + c * d, dqk, k, dqg, egc)
        m = z(lambda a, b, c, d, e: (a * b + c * d) * e, dlow, kk, dattn, qk, decay)
        mcol = [_dot(mh, ones, TN) + _dot(ml, ones, TN) for mh, ml in (_split(a) for a in m)]
        for i, (j, h) in enumerate(probs):
            rows = _chunk_rows(j)
            dqkv_ref[h, rows, :] = dq[i]
            dqkv_ref[HEADS + h, rows, :] = dk[i]
            dqkv_ref[2 * HEADS + h, rows, :] = dvb[i] * bv[i]
            db_ref[h, rows, :] = jnp.broadcast_to(_rowsum(dkb[i] * k[i] + dvb[i] * v[i]), (CHUNK, LANE))
            dgc = (_rowsum(dqg[i] * qg[i] + dkbg[i] * kbg[i] - dkg[i] * kg[i]) + _rowsum(m[i]) - mcol[i]
                   + jnp.where(last_row, dglast[i], 0.0))
            dg_ref[h, rows, :] = _suffix_sum_rows(dgc)

    act = lambda heads: pl.BlockSpec((heads, per * CHUNK, LANE), lambda n: (0, n, 0))
    mat = lambda d: pl.BlockSpec((per, HEADS, d, d), lambda n: (n, 0, 0, 0))
    out = jax.ShapeDtypeStruct((HEADS, t, LANE), F32)
    return _pcall(
        body, name=name, out_shape=(jax.ShapeDtypeStruct((3 * HEADS, t, LANE), F32), out, out), grid=(n_chunks // per,),
        in_specs=[act(3 * HEADS), act(HEADS), act(HEADS), mat(CHUNK), mat(LANE), mat(LANE), act(HEADS), act(HEADS), act(HEADS)],
        out_specs=(act(3 * HEADS), act(HEADS), act(HEADS)), semantics=("parallel",),
        vmem_limit=VMEM_LIMIT)(qkv, g, beta, tmats, states, dstates, do, dvn, vn)


ANY = pl.BlockSpec(memory_space=pl.ANY)
PEERS = N_DEV - 1


def _all_gather(arrays, *, name):
    n = len(arrays)

    def body(*refs):
        ins, outs = refs[:n], refs[n:2 * n]
        send_sems, recv_sems, local_sems = refs[2 * n:]
        x, y, c = lax.axis_index("x"), lax.axis_index("y"), lax.axis_index("c")
        me, sibling = (x, y, c), (x, y, 1 - c)
        chips = [(1 - x, y), (x, 1 - y), (1 - x, 1 - y)]

        def copy(a, k, block, to, src=None):
            dst = outs[a].at[4 * block[0] + 2 * block[1] + block[2]]
            return pltpu.make_async_remote_copy(src_ref=dst if src is None else src, dst_ref=dst, send_sem=send_sems.at[a * PEERS + k],
                                                recv_sem=recv_sems.at[a * PEERS + k], device_id=to, device_id_type=MESH)

        local = [pltpu.make_async_copy(ins[a], outs[a].at[4 * x + 2 * y + c], local_sems.at[a]) for a in range(n)]
        for cp in local:
            cp.start()
        first = []
        for a in range(n):
            first.append(copy(a, 0, me, sibling, src=ins[a]))
            first += [copy(a, 1 + j, me, (*chip, c), src=ins[a]) for j, chip in enumerate(chips)]
        for cp in first:
            cp.start()
        passed = []
        for a in range(n):
            for j, chip in enumerate(chips):
                copy(a, 1 + j, (*chip, c), me).wait_recv()
                fwd = copy(a, 4 + j, (*chip, c), sibling)
                fwd.start()
                passed.append(fwd)
        for a in range(n):
            copy(a, 0, sibling, me).wait_recv()
            for j, chip in enumerate(chips):
                copy(a, 4 + j, (*chip, 1 - c), me).wait_recv()
        for cp in first + passed:
            cp.wait_send()
        for cp in local:
            cp.wait()

    return _pcall(body, name=name, out_shape=tuple(jax.ShapeDtypeStruct((N_DEV,) + a.shape, a.dtype) for a in arrays),
                  in_specs=[ANY] * n, out_specs=(ANY,) * n,
                  scratch_shapes=[pltpu.SemaphoreType.DMA((n * PEERS,)), pltpu.SemaphoreType.DMA((n * PEERS,)),
                                  pltpu.SemaphoreType.DMA((n,))])(*arrays)


CHIPS = 4


def _pair_exchange(arrays, *, name):
    n = len(arrays)

    def body(*refs):
        ins, outs = refs[:n], refs[n:2 * n]
        send_sems, recv_sems = refs[2 * n:]
        x, y, c = lax.axis_index("x"), lax.axis_index("y"), lax.axis_index("c")
        copies = []
        for a in range(n):
            for q in range(CHIPS):
                cp = pltpu.make_async_remote_copy(src_ref=ins[a].at[2 * q + 1 - c], dst_ref=outs[a].at[q],
                                                  send_sem=send_sems.at[a * CHIPS + q], recv_sem=recv_sems.at[a * CHIPS + q],
                                                  device_id=(x, y, 1 - c), device_id_type=MESH)
                cp.start()
                copies.append(cp)
        for cp in copies:
            cp.wait()

    return _pcall(body, name=name, out_shape=tuple(jax.ShapeDtypeStruct((CHIPS,) + a.shape[1:], a.dtype) for a in arrays),
                  in_specs=[ANY] * n, out_specs=(ANY,) * n,
                  scratch_shapes=[pltpu.SemaphoreType.DMA((n * CHIPS,)), pltpu.SemaphoreType.DMA((n * CHIPS,))])(*arrays)


def _pair_add(blocks, theirs, *, name):
    _, r, c_ = blocks.shape
    tr = _tile(r, 512, 16)

    def body(mine_ref, theirs_ref, o_ref):
        core = lax.axis_index("c")
        own = jnp.where(core == 0, mine_ref[0, 0].astype(F32), mine_ref[0, 1].astype(F32))
        o_ref[0] = (own + theirs_ref[0].astype(F32)).astype(o_ref.dtype)

    spec = pl.BlockSpec((1, tr, c_), lambda q, i: (q, i, 0))
    return _pcall(body, name=name, out_shape=jax.ShapeDtypeStruct(theirs.shape, theirs.dtype), grid=(CHIPS, r // tr),
                  in_specs=[pl.BlockSpec((1, 2, tr, c_), lambda q, i: (q, 0, i, 0)), spec], out_specs=spec,
                  semantics=("parallel", "parallel"), vmem_limit=VMEM_LIMIT)(blocks.reshape(CHIPS, 2, r, c_), theirs)


HBM = pl.BlockSpec(memory_space=pltpu.HBM)
SEM = pl.BlockSpec(memory_space=pltpu.SEMAPHORE)
EFFECT = pltpu.SideEffectType.DATAFLOW_SIDE_EFFECTING


GATHER, CHIP_SCATTER = "gather", "chip_scatter"
PEERS_OF = {GATHER: N_DEV - 1, CHIP_SCATTER: CHIPS - 1}


def _direct_copies(srcs, lands, send_sems, recv_sems, local_sems, kind):
    x, y, c = lax.axis_index("x"), lax.axis_index("y"), lax.axis_index("c")
    peers = PEERS_OF[kind]
    copies = []
    for a, (src, land) in enumerate(zip(srcs, lands)):
        if kind == GATHER:
            mine = 4 * x + 2 * y + c
            copies.append(pltpu.make_async_copy(src, land.at[mine], local_sems.at[a]))
        else:
            mine = 2 * x + y
            copies.append(pltpu.make_async_copy(src.at[mine], land.at[mine], local_sems.at[a]))
        for k in range(1, peers + 1):
            bits = k if kind == GATHER else 2 * k
            px = 1 - x if bits & 4 else x
            py = 1 - y if bits & 2 else y
            pc = 1 - c if bits & 1 else c
            copies.append(pltpu.make_async_remote_copy(
                src_ref=src if kind == GATHER else src.at[2 * px + py], dst_ref=land.at[mine],
                send_sem=send_sems.at[a * peers + k - 1], recv_sem=recv_sems.at[a * peers + k - 1],
                device_id=(px, py, pc), device_id_type=MESH))
    return copies


def _exchange_start(groups, kind, *, name):
    srcs = [s for group in groups for s in group]
    n = len(srcs)
    sizes = [len(group) for group in groups]
    starts = [sum(sizes[:g]) for g in range(len(groups))]
    land_shapes = [(N_DEV,) + s.shape if kind == GATHER else s.shape for s in srcs]
    peers = PEERS_OF[kind]

    def body(*refs):
        srcs_, lands = refs[:n], refs[n:2 * n]
        token = refs[-1]
        for g, (at, size) in enumerate(zip(starts, sizes)):
            send_sems, recv_sems, local_sems = refs[2 * n + 3 * g:2 * n + 3 * g + 3]
            for cp in _direct_copies(srcs_[at:at + size], lands[at:at + size], send_sems, recv_sems, local_sems, kind):
                cp.start()
        token[...] = jnp.zeros_like(token)

    sems = tuple(t for size in sizes for t in (pltpu.SemaphoreType.DMA((size * peers,)), pltpu.SemaphoreType.DMA((size * peers,)),
                                               pltpu.SemaphoreType.DMA((size,))))
    thru = tuple(pltpu.HBM(s.shape, s.dtype) for s in srcs) + tuple(pltpu.HBM(shp, s.dtype) for shp, s in zip(land_shapes, srcs))
    ins = [pltpu.with_memory_space_constraint(s, pltpu.HBM) for s in srcs]
    ins += [pltpu.with_memory_space_constraint(lax.empty(shp, s.dtype), pltpu.HBM) for shp, s in zip(land_shapes, srcs)]
    out = pl.pallas_call(
        body, name=name, out_shape=sems + thru + (jax.ShapeDtypeStruct((SUBLANE, LANE), F32),), in_specs=[HBM] * (2 * n),
        out_specs=(SEM,) * len(sems) + (HBM,) * (2 * n) + (pl.BlockSpec(memory_space=pltpu.VMEM),),
        input_output_aliases={i: len(sems) + i for i in range(2 * n)},
        compiler_params=pltpu.CompilerParams(has_side_effects=EFFECT))(*ins)
    arrays = out[len(sems):-1]
    started = [tuple(out[3 * g:3 * g + 3]) + tuple(arrays[at:at + size]) + tuple(arrays[n + at:n + at + size])
               for g, (at, size) in enumerate(zip(starts, sizes))]
    return started, out[-1]


def _exchange_wait(started, after, kind, *, name):
    n = (len(started) - 3) // 2
    sems, arrays = started[:3], started[3:]

    def body(*refs):
        srcs_, lands = refs[:n], refs[n:2 * n]
        send_sems, recv_sems, local_sems = refs[2 * n:2 * n + 3]
        for cp in _direct_copies(srcs_, lands, send_sems, recv_sems, local_sems, kind):
            cp.wait()

    out = pl.pallas_call(
        body, name=name, out_shape=tuple(pltpu.HBM(a.shape, a.dtype) for a in arrays),
        in_specs=[HBM] * (2 * n) + [SEM] * 3 + [ANY], out_specs=(HBM,) * (2 * n),
        input_output_aliases={i: i for i in range(2 * n)},
        compiler_params=pltpu.CompilerParams(has_side_effects=EFFECT))(*arrays, *sems, after)
    return out[n:]


def _adamw_reduce(w, parts, m, v, *, name):
    layers, r, c = w.shape
    assert len(parts) == layers
    senders = parts[0].shape[0]
    tr = _tile(r, 512, 16)
    tiles = r // tr
    bc1 = 1.0 - ADAM_B1 ** ADAM_STEP
    bc2 = 1.0 - ADAM_B2 ** ADAM_STEP

    def body(w_ref, *rest):
        p_refs = rest[:layers]
        m_ref, v_ref, g_ref, d_ref, nm_ref, nv_ref = rest[layers:]

        def update(p_ref):
            g = p_ref[0, :, pl.ds(0, c)].astype(F32)
            for s in range(1, senders):
                g = g + p_ref[s, :, pl.ds(0, c)].astype(F32)
            nm = ADAM_B1 * m_ref[0] + (1.0 - ADAM_B1) * g
            nv = ADAM_B2 * v_ref[0] + (1.0 - ADAM_B2) * (g * g)
            g_ref[0] = g
            nm_ref[0] = nm
            nv_ref[0] = nv
            d_ref[0] = -ADAM_LR * ((nm / bc1) / (jnp.sqrt(nv / bc2) + ADAM_EPS) + ADAM_WD * w_ref[0])

        for layer in range(layers):
            pl.when(pl.program_id(0) == layer)(functools.partial(update, p_refs[layer]))

    def part_spec(layer, shape):
        rest = 0 if layer > 0 else tiles - 1
        return pl.BlockSpec((senders, tr, shape[2]), lambda l, i: (0, jnp.where(l == layer, i, rest), 0))

    spec = pl.BlockSpec((1, tr, c), lambda l, i: (l, i, 0))
    out = jax.ShapeDtypeStruct((layers, r, c), F32)
    return _pcall(body, name=name, out_shape=(out,) * 4, grid=(layers, tiles),
                  in_specs=[spec] + [part_spec(layer, p.shape) for layer, p in enumerate(parts)] + [spec, spec],
                  out_specs=(spec,) * 4, semantics=("arbitrary", "arbitrary"), vmem_limit=VMEM_LIMIT)(w, *parts, m, v)


def _pool_windows():
    return jnp.repeat(jnp.asarray(POOL_WINDOWS, F32), POOL_DIM // len(POOL_WINDOWS))[None, :]


def _block_diag_pairs(pool_w):
    z = jnp.zeros_like(pool_w[0])
    return jnp.stack([jnp.block([[pool_w[2 * b], z], [z, pool_w[2 * b + 1]]]) for b in range(2)])


def _pad_lanes(vec):
    return jnp.zeros((1, LANE), F32).at[0, :vec.shape[0]].set(vec)


FF_SHARD = D_FF // N_DEV
FF_BLOCK = 384
D_FF_PAD = N_DEV * FF_BLOCK


def _layer_fwd(x, p_i, wt, fetch):
    wt = {**wt, **fetch(0, x)}
    h1 = _rmsnorm_fwd(x, wt["norm1_g"], name="rmsnorm_fwd")
    proj = _matmul(h1, wt["w_in"], "nn", name="mm_in")
    qkv = _qkv_prep_fwd(proj, wt["conv_qkv"], name="qkv_prep_fwd")
    g, beta = _gates_fwd(proj, wt["a_log"], wt["dt_bias"], name="gates_fwd")
    u, w, qg, kg, attn, tmats = _deltanet_prep(qkv, g, beta, name="deltanet_prep")
    o, vn, states = _deltanet_scan(u, w, qg, kg, attn, g, name="deltanet_scan")
    o_a = _apost_fwd(o, proj, wt["onorm_g"], name="apost_fwd")
    o_b = _pool_fwd(proj, wt["pool_win"], wt["pool_wbd"], wt["pool_scale"], name="pool_fwd")
    o_c = _sconv_fwd(proj, wt["sconv_w"], name="sconv_fwd")
    mixed = jnp.concatenate([o_a, o_b, o_c], axis=1)
    wt.update(fetch(1, mixed))
    x1 = _matmul(mixed, wt["w_out"], "nn", res=x, name="mm_out")
    h2 = _rmsnorm_fwd(x1, wt["norm2_g"], name="rmsnorm_fwd")
    wt.update(fetch(2, h2))
    gate = _matmul(h2, wt["w_gate"], "nn", b_blocked=True, name="mm_gate")
    up = _matmul(h2, wt["w_up"], "nn", b_blocked=True, name="mm_up")
    ff = _swiglu_fwd(gate, up, name="swiglu_fwd")
    wt.update(fetch(3, ff))
    x2 = _matmul(ff, wt["w_down"], "nn", res=x1, name="mm_down")
    wt.update(fetch(4, x2))
    pgl = _matmul(x2, wt["ple_gate"], "nn", name="mm_pleg")
    pp = _matmul(p_i, wt["ple_proj"], "nn", b_blocked=True, name="mm_plep")
    x3 = _ple_fwd(x2, pgl, pp, name="ple_fwd")
    saved = dict(x=x, h1=h1, proj=proj, qkv=qkv, g=g, beta=beta, o=o, states=states, tmats=tmats, mixed=mixed, x1=x1, h2=h2,
                 gate=gate, up=up, ff=ff, x2=x2, pgl=pgl, pp=pp, p=p_i, w=w, qg=qg, kg=kg, attn=attn, vn=vn, wt=wt)
    return x3, saved


def _col_blocks(g):
    a = g.shape[0]
    return jnp.transpose(g.reshape(a, N_DEV, -1), (1, 0, 2))


def _cols_joined(blocks):
    return jnp.transpose(blocks, (1, 0, 2)).reshape(blocks.shape[1], -1)


def _layer_bwd(dx3, sv, emit, after=None):
    gr, big = {}, {}
    wt = sv["wt"]
    rows = D_MODEL // N_DEV
    dpgl, dpp = _ple_bwd(dx3, sv["pgl"], sv["pp"], name="ple_bwd", after=after)
    big["ple_proj"] = _matmul(sv["p"], dpp, "tn", out_blocked=(N_DEV, rows), out_dtype=BF16, name="mm_dplep")
    big["ple_gate"] = _matmul(sv["x2"], dpgl, "tn", out_dtype=BF16, name="mm_dpleg").reshape(N_DEV, rows, D_MODEL)
    dx2 = _matmul(dpgl, wt["ple_gate"], "nt", res=dx3, name="mm_dx2")
    big["w_down"] = _matmul(sv["ff"], dx2, "tn", out_dtype=BF16, name="mm_ddown").reshape(N_DEV, FF_BLOCK, D_MODEL)
    dff = _matmul(dx2, wt["w_down"], "nt", name="mm_dff", after=emit(0, big))
    dgate, dup = _swiglu_bwd(sv["gate"], sv["up"], dff, name="swiglu_bwd")
    big["w_gate"] = _matmul(sv["h2"], dgate, "tn", out_blocked=(N_DEV, FF_BLOCK), out_dtype=BF16, name="mm_dgate")
    big["w_up"] = _matmul(sv["h2"], dup, "tn", out_blocked=(N_DEV, FF_BLOCK), out_dtype=BF16, name="mm_dup")
    dh2 = _matmul(dgate, wt["w_gate"], "nt", b_blocked=True, name="mm_dh2_gate")
    dh2 = _matmul(dup, wt["w_up"], "nt", b_blocked=True, res=dh2, name="mm_dh2_up")
    dx1, gr["norm2_g"] = _rmsnorm_bwd(sv["x1"], wt["norm2_g"], dh2, dx2, name="rmsnorm_bwd")
    big["w_out"] = _matmul(sv["mixed"], dx1, "tn", out_dtype=BF16, name="mm_dout").reshape(N_DEV, rows, D_MODEL)
    dmixed = _matmul(dx1, wt["w_out"], "nt", name="mm_dmixed", after=emit(1, big))
    proj = sv["proj"]
    dcb, dcc, dch, dsconv = _sconv_bwd(proj, wt["sconv_w"], dmixed, name="sconv_bwd")
    big["sconv_w"] = _col_blocks(dsconv)
    dhp, dwbd, gr["pool_scale"] = _pool_bwd(proj, wt["pool_win"], wt["pool_wbd"], wt["pool_scale"], dmixed, name="pool_bwd")
    half = LANE // 2
    gr["pool_w"] = jnp.stack([dwbd[0, :half, :half], dwbd[0, half:, half:], dwbd[1, :half, :half], dwbd[1, half:, half:]])
    do, dz, gr["onorm_g"] = _apost_bwd(sv["o"], proj, wt["onorm_g"], dmixed, name="apost_bwd")
    dvn, dstates = _deltanet_bscan(sv["w"], sv["qg"], sv["kg"], sv["attn"], sv["g"], do, name="deltanet_bscan")
    dqkv_h, dg, dbeta = _deltanet_post(sv["qkv"], sv["g"], sv["beta"], sv["tmats"], sv["states"], dstates, do, dvn, sv["vn"],
                                       name="deltanet_post")
    dab, dalog, ddtb = _gates_bwd(proj, wt["a_log"], wt["dt_bias"], dg, dbeta, name="gates_bwd")
    gr["a_log"], gr["dt_bias"] = dalog[0, :HEADS], ddtb[0, :HEADS]
    dqkv, dconv = _qkv_prep_bwd(proj, wt["conv_qkv"], dqkv_h, name="qkv_prep_bwd")
    big["conv_qkv"] = _col_blocks(dconv)
    dproj = jnp.concatenate([dqkv, dz, dab, dhp, dcb, dcc, dch], axis=1)
    dwin = _matmul(sv["h1"], dproj, "tn", out_dtype=BF16, name="mm_din")
    big["w_in"] = _col_blocks(jnp.concatenate([dwin[:, :AB_COL + 2 * HEADS], dwin[:, AB_COL + LANE:]], axis=1))
    dh1 = _matmul(dproj, wt["w_in"], "nt", name="mm_dh1", after=emit(2, big))
    dx, gr["norm1_g"] = _rmsnorm_bwd(sv["x"], wt["norm1_g"], dh1, dx1, name="rmsnorm_bwd")
    return dx, gr


FETCH_GROUPS = (("w_in", "conv_qkv", "sconv_w"), ("w_out",), ("w_gate", "w_up"), ("w_down",), ("ple_gate", "ple_proj"))
EMIT_GROUPS = (("ple_proj", "ple_gate", "w_down"), ("w_gate", "w_up", "w_out"), ("w_in", "conv_qkv", "sconv_w"))


def _small_weights(w, i):
    return dict(
        norm1_g=w["norm1_g"][i][None], norm2_g=w["norm2_g"][i][None], onorm_g=w["onorm_g"][i][None],
        a_log=_pad_lanes(w["a_log"][i]), dt_bias=_pad_lanes(w["dt_bias"][i]),
        pool_scale=w["pool_scale"][i][None], pool_win=_pool_windows(), pool_wbd=_block_diag_pairs(w["pool_w"][i]))


def _as_read(name, gathered):
    if name == "w_in":
        w_in = _cols_joined(gathered)
        return jnp.concatenate([w_in[:, :AB_COL + 2 * HEADS], jnp.zeros((D_MODEL, LANE - 2 * HEADS), BF16),
                                w_in[:, AB_COL + 2 * HEADS:]], axis=1)
    if name in ("conv_qkv", "sconv_w"):
        return _cols_joined(gathered)
    if name in ("w_gate", "w_up", "ple_proj"):
        return gathered
    return gathered.reshape(-1, D_MODEL)


def _layer_weights(gathered, w, i):
    return {**_small_weights(w, i), **{k: _as_read(k, g) for k, g in gathered.items()}}


def _local_step(x, p, target, layers, final_g):
    saved = []
    h = x
    for i in range(DEPTH):
        replicated = {k: v for k, v in layers[i].items() if k not in SHARDED}
        h, sv = _layer_fwd(h, p[i], replicated, lambda group, after, i=i: {k: layers[i][k] for k in FETCH_GROUPS[group]})
        saved.append(sv)
    dx, dgf, loss = _loss_head(h, final_g, target, name="loss_head")
    big, small = [{} for _ in range(DEPTH)], [None] * DEPTH
    for i in reversed(range(DEPTH)):
        dx, small[i] = _layer_bwd(dx, saved[i], lambda group, blocks, i=i: big[i].update({k: blocks[k] for k in EMIT_GROUPS[group]}))
    return loss, dx, big, small, dgf


SHARDED = ("w_in", "w_gate", "w_up", "w_down", "w_out", "ple_gate", "ple_proj", "conv_qkv", "sconv_w")
SMALL = ("norm1_g", "a_log", "dt_bias", "onorm_g", "pool_w", "pool_scale", "norm2_g", "final_g")
SLAB_COLS = 1024


def _payload(name, shard):
    if name in ("conv_qkv", "sconv_w"):
        return shard
    out = shard.astype(BF16)
    if name in ("w_gate", "w_up"):
        out = jnp.pad(out, ((0, 0), (0, FF_BLOCK - FF_SHARD)))
    if name == "w_down":
        out = jnp.pad(out, ((0, FF_BLOCK - FF_SHARD), (0, 0)))
    return out


def _slab_rows(shape):
    size = 1
    for s in shape:
        size *= s
    return SUBLANE * -(-size // (SUBLANE * SLAB_COLS))


def _pack_slab(parts, extra_row):
    rows = []
    for name in SMALL:
        flat = parts[name].reshape(-1)
        nrow = _slab_rows(parts[name].shape)
        rows.append(jnp.pad(flat, (0, nrow * SLAB_COLS - flat.shape[0])).reshape(nrow, SLAB_COLS))
    rows.append(jnp.pad(extra_row, ((0, SUBLANE - 1), (0, 0))))
    return jnp.concatenate(rows, axis=0)


def _unpack_slab(slab, shapes):
    out, row = {}, 0
    for name in SMALL:
        size = 1
        for s in shapes[name]:
            size *= s
        out[name] = slab[row:row + _slab_rows(shapes[name])].reshape(-1)[:size].reshape(shapes[name])
        row += _slab_rows(shapes[name])
    return out, row


def kernel(x, p, norm1_g, w_in, conv_qkv, a_log, dt_bias, onorm_g, pool_w, pool_scale, sconv_w, w_out, norm2_g, w_gate, w_up, w_down, ple_proj, ple_gate, final_g, loss_target, m_norm1_g, m_w_in, m_conv_qkv, m_a_log, m_dt_bias, m_onorm_g, m_pool_w, m_pool_scale, m_sconv_w, m_w_out, m_norm2_g, m_w_gate, m_w_up, m_w_down, m_ple_proj, m_ple_gate, m_final_g, v_norm1_g, v_w_in, v_conv_qkv, v_a_log, v_dt_bias, v_onorm_g, v_pool_w, v_pool_scale, v_sconv_w, v_w_out, v_norm2_g, v_w_gate, v_w_up, v_w_down, v_ple_proj, v_ple_gate, v_final_g):
    names = ["norm1_g", "w_in", "conv_qkv", "a_log", "dt_bias", "onorm_g", "pool_w", "pool_scale", "sconv_w", "w_out", "norm2_g",
             "w_gate", "w_up", "w_down", "ple_proj", "ple_gate", "final_g"]
    w = dict(zip(names, [norm1_g, w_in, conv_qkv, a_log, dt_bias, onorm_g, pool_w, pool_scale, sconv_w, w_out, norm2_g, w_gate, w_up,
                         w_down, ple_proj, ple_gate, final_g]))
    m = dict(zip(names, [m_norm1_g, m_w_in, m_conv_qkv, m_a_log, m_dt_bias, m_onorm_g, m_pool_w, m_pool_scale, m_sconv_w, m_w_out,
                         m_norm2_g, m_w_gate, m_w_up, m_w_down, m_ple_proj, m_ple_gate, m_final_g]))
    v = dict(zip(names, [v_norm1_g, v_w_in, v_conv_qkv, v_a_log, v_dt_bias, v_onorm_g, v_pool_w, v_pool_scale, v_sconv_w, v_w_out,
                         v_norm2_g, v_w_gate, v_w_up, v_w_down, v_ple_proj, v_ple_gate, v_final_g]))

    gathered = dict(zip(SHARDED, _all_gather([_payload(k, w[k][0]) for k in SHARDED], name="all_gather_weights")))
    (flying,), token = _exchange_start([[_payload(k, w[k][1]) for k in SHARDED]], GATHER, name="gather_start")
    replicated = [_small_weights(w, i) for i in range(DEPTH)]
    replicated[0]["norm1_g"] = replicated[0]["norm1_g"] + token[0, 0]

    def fetch(i, group, after):
        if i == 1 and group == 0:
            gathered.update(zip(SHARDED, _exchange_wait(flying, after, GATHER, name="gather_wait")))
        return {k: _as_read(k, gathered[k]) for k in FETCH_GROUPS[group]}

    def reduce_scatter_start(members, blocks, tag):
        mine = [blocks[k] for k in members]
        theirs = _pair_exchange(mine, name="pair_exchange")
        sums = [_pair_add(a, b, name="pair_add") for a, b in zip(mine, theirs)]
        (started,), token = _exchange_start([sums], CHIP_SCATTER, name="exchange_start_" + tag)
        return started, token

    h, saved0 = _layer_fwd(x[0], p[0, 0], replicated[0], functools.partial(fetch, 0))
    h, saved1 = _layer_fwd(h, p[1, 0], replicated[1], functools.partial(fetch, 1))
    dx, dgf, loss_part = _loss_head(h, final_g[None], loss_target[0], name="loss_head")
    small, big1, flying0 = [None] * DEPTH, {}, []
    dx, small[1] = _layer_bwd(dx, saved1, lambda group, blocks: big1.update({k: blocks[k] for k in EMIT_GROUPS[group]}))
    flying1, token = reduce_scatter_start(SHARDED, big1, "1")

    def emit(group, blocks):
        started, token = reduce_scatter_start(EMIT_GROUPS[group], blocks, f"0_{group}")
        flying0.append(started)
        return token

    dx, small[0] = _layer_bwd(dx, saved0, emit, after=token)
    received = [{}, dict(zip(SHARDED, _exchange_wait(flying1, dx, CHIP_SCATTER, name="exchange_wait_1")))]
    for group, members in enumerate(EMIT_GROUPS):
        received[0].update(zip(members, _exchange_wait(flying0[group], dx, CHIP_SCATTER, name=f"exchange_wait_0_{group}")))

    grads = {k: jnp.stack([small[i][k] for i in range(DEPTH)]) for k in small[0]}
    grads = {k: g[:, 0] if k in ("norm1_g", "norm2_g", "onorm_g", "pool_scale") else g for k, g in grads.items()}
    grads["final_g"] = dgf[0]
    loss_row = jnp.pad(loss_part, ((0, 0), (0, SLAB_COLS - LANE)))
    (small_parts,) = _all_gather([_pack_slab(grads, loss_row)], name="all_gather_small_grads")

    out_g, out_d, out_m, out_v = {}, {}, {}, {}
    for k in SHARDED:
        out_g[k], out_d[k], out_m[k], out_v[k] = _adamw_reduce(w[k], [received[i][k] for i in range(DEPTH)], m[k], v[k],
                                                                name="adamw_" + k)
    zero_row = jnp.zeros((1, SLAB_COLS), F32)
    slabs = _adamw_reduce(_pack_slab(w, zero_row)[None], [small_parts], _pack_slab(m, zero_row)[None],
                          _pack_slab(v, zero_row)[None], name="adamw_small")
    slabs = [s[0] for s in slabs]
    shapes = {k: w[k].shape for k in SMALL}
    for dst, slab in zip((out_g, out_d, out_m, out_v), slabs):
        vals, _ = _unpack_slab(slab, shapes)
        dst.update(vals)
    _, loss_at = _unpack_slab(slabs[0], shapes)
    loss = slabs[0][loss_at, 0]

    return (loss, dx[None], *[out_g[k] for k in names], *[out_d[k] for k in names], *[out_m[k] for k in names],
            *[out_v[k] for k in names])
```

```python
import functools

import jax
import jax.numpy as jnp
from jax import lax
from jax.experimental import pallas as pl
from jax.experimental.pallas import tpu as pltpu

F32 = jnp.float32
BF16 = jnp.bfloat16

D_MODEL = 1024
DEPTH = 2
PLE_DIM = 256
EPS = 1e-6
HEAD_DIM = 128
HEADS = 4
A_DIM = HEADS * HEAD_DIM
QKV_TAPS = 4
CHUNK = 64
POOL_WINDOWS = (2, 4, 8, 16)
POOL_DIM = 256
CONV_DIM = 256
CONV_TAPS = 3
D_FF = 2816
D_IN = 3080
D_IN_PAD = 3200
AB_COL = 2048
N_DEV = 8

ADAM_LR = 0.001
ADAM_B1 = 0.9
ADAM_B2 = 0.999
ADAM_EPS = 1e-08
ADAM_WD = 0.01
ADAM_STEP = 10

LANE = 128
SUBLANE = 8
VMEM_BYTES_V7X = 64 * 1024 * 1024
VMEM_LIMIT = 48 * 1024 * 1024

_HI = lax.Precision.HIGHEST
NN = ((1,), (0,))
NT = ((1,), (1,))
TN = ((0,), (0,))
MESH = pl.DeviceIdType.MESH


def _dot(a, b, dims, hi=False):
    if hi:
        return lax.dot_general(a, b, (dims, ((), ())), precision=_HI, preferred_element_type=F32)
    return lax.dot_general(a.astype(BF16), b.astype(BF16), (dims, ((), ())), preferred_element_type=F32)


def _pcall(body, *, name, out_shape, grid=(), in_specs=None, out_specs=None, scratch_shapes=(), semantics=None,
           vmem_limit=None, after=None, **kw):
    params = {}
    if semantics is not None:
        params["dimension_semantics"] = semantics
    if vmem_limit is not None:
        params["vmem_limit_bytes"] = vmem_limit
    if after is not None:
        n_in, inner = len(in_specs), body
        body = lambda *refs: inner(*refs[:n_in], *refs[n_in + 1:])
        in_specs = list(in_specs) + [pl.BlockSpec(after.shape, lambda *_: (0,) * after.ndim)]
    call = pl.pallas_call(
        body, name=name, out_shape=out_shape, grid=grid, in_specs=in_specs, out_specs=out_specs,
        scratch_shapes=list(scratch_shapes), compiler_params=pltpu.CompilerParams(**params), **kw)
    return call if after is None else (lambda *args: call(*args, after))


def _sigmoid(x):
    return 1.0 / (1.0 + jnp.exp(-x))


def _softplus(x):
    return jnp.maximum(x, 0.0) + jnp.log(1.0 + jnp.exp(-jnp.abs(x)))


def _tile(n, cap, mult):
    if n <= cap:
        return n
    best = None
    for t in range(mult, cap + 1, mult):
        if n % t == 0:
            best = t
    assert best is not None, (n, cap, mult)
    return best


ROWS_PER_STEP = 512
NARROW_RESULT = 1024
COLS_PER_DOT = 640


def _resident(weight):
    return pl.BlockSpec(weight.shape, lambda i: (0,) * weight.ndim, pipeline_mode=pl.Buffered(1))


def _matmul_rows(a, b, mode, *, name, res=None, out_dtype=F32, b_blocked=False, after=None):
    m, k = a.shape
    if b_blocked:
        nb, _, bw = b.shape
        n = nb * bw if mode == "nn" else b.shape[1]
    else:
        n = b.shape[1] if mode == "nn" else b.shape[0]
    tm = _tile(m, ROWS_PER_STEP if n > NARROW_RESULT else 2 * ROWS_PER_STEP, 16)
    cn = bw if (b_blocked and mode == "nn") else _tile(n, COLS_PER_DOT, LANE)
    has_res = res is not None

    def body(*refs):
        a_ref, b_ref = refs[0], refs[1]
        res_ref = refs[2] if has_res else None
        o_ref = refs[2 + has_res]
        if not (b_blocked and mode == "nt"):
            av = a_ref[...].astype(BF16)
        for j in range(n // cn):
            cols = pl.ds(j * cn, cn)
            if mode == "nn":
                part = _dot(av, b_ref[j] if b_blocked else b_ref[:, cols], NN)
            elif not b_blocked:
                part = _dot(av, b_ref[cols, :], NT)
            else:
                part = None
                for s in range(nb):
                    term = _dot(a_ref[:, pl.ds(s * bw, bw)], b_ref[s, cols, :], NT)
                    part = term if part is None else part + term
            if has_res:
                part = part + res_ref[:, cols]
            o_ref[:, cols] = part.astype(o_ref.dtype)

    row = lambda width: pl.BlockSpec((tm, width), lambda i: (i, 0))
    whole = _resident(b)
    ins = [a, b] + ([res] if has_res else [])
    specs = [row(k), whole] + ([row(n)] if has_res else [])
    return _pcall(body, name=name, out_shape=jax.ShapeDtypeStruct((m, n), out_dtype), grid=(m // tm,), in_specs=specs,
                  out_specs=row(n), semantics=("parallel",), vmem_limit=VMEM_LIMIT, after=after)(*ins)


def _matmul(a, b, mode, *, name, res=None, out_dtype=F32, b_blocked=False, out_blocked=None, after=None):
    if mode != "tn":
        return _matmul_rows(a, b, mode, name=name, res=res, out_dtype=out_dtype, b_blocked=b_blocked, after=after)
    assert res is None and not b_blocked and after is None
    (t, m), (t2, n) = a.shape, b.shape
    assert t == t2, (a.shape, b.shape)
    tm = _tile(m, 1024, LANE)
    tn = _tile(n, COLS_PER_DOT, LANE)
    if out_blocked is not None:
        assert out_blocked[0] * out_blocked[1] == n
        tn = out_blocked[1]

    def body(a_ref, b_ref, o_ref):
        part = _dot(a_ref[...], b_ref[...], TN).astype(o_ref.dtype)
        if out_blocked is None:
            o_ref[...] = part
        else:
            o_ref[0] = part

    o_spec = (pl.BlockSpec((tm, tn), lambda i, j: (i, j)) if out_blocked is None
              else pl.BlockSpec((1, tm, tn), lambda i, j: (j, i, 0)))
    o_shape = (m, n) if out_blocked is None else (out_blocked[0], m, out_blocked[1])
    return _pcall(body, name=name, out_shape=jax.ShapeDtypeStruct(o_shape, out_dtype), grid=(m // tm, n // tn),
                  in_specs=[pl.BlockSpec((t, tm), lambda i, j: (0, i)), pl.BlockSpec((t, tn), lambda i, j: (0, j))],
                  out_specs=o_spec, semantics=("parallel", "parallel"), vmem_limit=VMEM_LIMIT)(a, b)


ROW_TILE = 256


def _rows(t, width, idx=0):
    return pl.BlockSpec((ROW_TILE, width), lambda i: (i, idx))


def _vec(width):
    return pl.BlockSpec((1, width), lambda i: (0, 0))


def _rmsnorm_fwd(x, g, *, name):
    t, d = x.shape

    def body(x_ref, g_ref, h_ref):
        xv = x_ref[...]
        r = lax.rsqrt(jnp.mean(xv * xv, axis=-1, keepdims=True) + EPS)
        h_ref[...] = (xv * r * g_ref[...]).astype(BF16)

    return _pcall(body, name=name, out_shape=jax.ShapeDtypeStruct((t, d), BF16), grid=(t // ROW_TILE,),
                  in_specs=[_rows(t, d), _vec(d)], out_specs=_rows(t, d), semantics=("parallel",))(x, g)


def _rmsnorm_bwd(x, g, dh, dres, *, name):
    t, d = x.shape

    def body(x_ref, g_ref, dh_ref, dres_ref, dx_ref, dg_ref):
        xv = x_ref[...]
        r = lax.rsqrt(jnp.mean(xv * xv, axis=-1, keepdims=True) + EPS)
        xhat = xv * r
        dhv = dh_ref[...].astype(F32)
        dhg = dhv * g_ref[...]
        dx_ref[...] = dres_ref[...] + r * (dhg - xhat * jnp.mean(dhg * xhat, axis=-1, keepdims=True))
        part = jnp.sum(dhv * xhat, axis=0, keepdims=True)

        @pl.when(pl.program_id(0) == 0)
        def _():
            dg_ref[...] = part

        @pl.when(pl.program_id(0) > 0)
        def _():
            dg_ref[...] += part

    return _pcall(body, name=name, out_shape=(jax.ShapeDtypeStruct((t, d), F32), jax.ShapeDtypeStruct((1, d), F32)),
                  grid=(t // ROW_TILE,), in_specs=[_rows(t, d), _vec(d), _rows(t, d), _rows(t, d)],
                  out_specs=(_rows(t, d), _vec(d)), semantics=("arbitrary",))(x, g, dh, dres)


def _swiglu_fwd(h, w_gate, w_up, *, name):
    t, k = h.shape
    nb, _, bw = w_gate.shape
    tm = _tile(t, ROWS_PER_STEP, 16)

    def body(h_ref, wg_ref, wu_ref, ff_ref, gate_ref, up_ref):
        hv = h_ref[...]
        for j in range(nb):
            cols = pl.ds(j * bw, bw)
            gv = _dot(hv, wg_ref[j], NN)
            uv = _dot(hv, wu_ref[j], NN)
            gate_ref[:, cols] = gv.astype(BF16)
            up_ref[:, cols] = uv.astype(BF16)
            ff_ref[:, cols] = (gv * _sigmoid(gv) * uv).astype(BF16)

    row = lambda width: pl.BlockSpec((tm, width), lambda i: (i, 0))
    out = jax.ShapeDtypeStruct((t, nb * bw), BF16)
    return _pcall(body, name=name, out_shape=(out,) * 3, grid=(t // tm,), in_specs=[row(k), _resident(w_gate), _resident(w_up)],
                  out_specs=(row(nb * bw),) * 3, semantics=("parallel",), vmem_limit=VMEM_LIMIT)(h, w_gate, w_up)


def _swiglu_bwd(dx2, w_down, gate, up, *, name, after=None):
    t, d = dx2.shape
    f = w_down.shape[0]
    tm = _tile(t, ROWS_PER_STEP, 16)
    cn = _tile(f, COLS_PER_DOT, LANE)

    def body(dx_ref, w_ref, gate_ref, up_ref, dgate_ref, dup_ref):
        dxv = dx_ref[...].astype(BF16)
        for j in range(f // cn):
            cols = pl.ds(j * cn, cn)
            dffv = _dot(dxv, w_ref[cols, :], NT)
            gv = gate_ref[:, cols].astype(F32)
            sig = _sigmoid(gv)
            dgate_ref[:, cols] = (dffv * up_ref[:, cols].astype(F32) * sig * (1.0 + gv * (1.0 - sig))).astype(BF16)
            dup_ref[:, cols] = (dffv * gv * sig).astype(BF16)

    row = lambda width: pl.BlockSpec((tm, width), lambda i: (i, 0))
    out = jax.ShapeDtypeStruct((t, f), BF16)
    return _pcall(body, name=name, out_shape=(out, out), grid=(t // tm,), in_specs=[row(d), _resident(w_down), row(f), row(f)],
                  out_specs=(row(f), row(f)), semantics=("parallel",), vmem_limit=VMEM_LIMIT, after=after)(dx2, w_down, gate, up)


def _ple_fwd(x2, pgl, pp, *, name):
    t, d = x2.shape

    def body(x_ref, pgl_ref, pp_ref, o_ref):
        o_ref[...] = x_ref[...] + _sigmoid(pgl_ref[...]) * pp_ref[...]

    return _pcall(body, name=name, out_shape=jax.ShapeDtypeStruct((t, d), F32), grid=(t // ROW_TILE,),
                  in_specs=[_rows(t, d)] * 3, out_specs=_rows(t, d), semantics=("parallel",))(x2, pgl, pp)


def _ple_bwd(dx3, pgl, pp, *, name, after=None):
    t, d = dx3.shape

    def body(dx_ref, pgl_ref, pp_ref, dpgl_ref, dpp_ref):
        dxv = dx_ref[...]
        sig = _sigmoid(pgl_ref[...])
        dpp_ref[...] = (dxv * sig).astype(BF16)
        dpgl_ref[...] = (dxv * pp_ref[...] * sig * (1.0 - sig)).astype(BF16)

    return _pcall(body, name=name, out_shape=(jax.ShapeDtypeStruct((t, d), BF16),) * 2, grid=(t // ROW_TILE,),
                  in_specs=[_rows(t, d)] * 3, out_specs=(_rows(t, d),) * 2, semantics=("parallel",), after=after)(dx3, pgl, pp)


def _loss_head(x3, g, target, *, name):
    t, d = x3.shape

    def body(x_ref, g_ref, t_ref, dx_ref, dg_ref, loss_ref):
        xv = x_ref[...]
        r = lax.rsqrt(jnp.mean(xv * xv, axis=-1, keepdims=True) + EPS)
        xhat = xv * r
        gv = g_ref[...]
        err = xhat * gv - t_ref[...]
        row_loss = jnp.sum(err * err, axis=-1, keepdims=True) * (0.5 / d)
        lpart = jnp.broadcast_to(jnp.sum(row_loss, axis=0, keepdims=True), (1, LANE))
        dy = err * (1.0 / d)
        dyg = dy * gv
        dx_ref[...] = r * (dyg - xhat * jnp.mean(dyg * xhat, axis=-1, keepdims=True))
        gpart = jnp.sum(dy * xhat, axis=0, keepdims=True)

        @pl.when(pl.program_id(0) == 0)
        def _():
            dg_ref[...] = gpart
            loss_ref[...] = lpart

        @pl.when(pl.program_id(0) > 0)
        def _():
            dg_ref[...] += gpart
            loss_ref[...] += lpart

    return _pcall(body, name=name,
                  out_shape=(jax.ShapeDtypeStruct((t, d), F32), jax.ShapeDtypeStruct((1, d), F32), jax.ShapeDtypeStruct((1, LANE), F32)),
                  grid=(t // ROW_TILE,), in_specs=[_rows(t, d), _vec(d), _rows(t, d)],
                  out_specs=(_rows(t, d), _vec(d), _vec(LANE)), semantics=("arbitrary",))(x3, g, target)


def _shift_down(x, d):
    if d == 0:
        return x
    row = lax.broadcasted_iota(jnp.int32, x.shape, 0)
    return jnp.where(row >= d, pltpu.roll(x, d, 0), 0.0)


def _shift_up(x, d):
    if d == 0:
        return x
    t = x.shape[0]
    row = lax.broadcasted_iota(jnp.int32, x.shape, 0)
    return jnp.where(row < t - d, pltpu.roll(x, t - d, 0), 0.0)


def _colsum(x):
    return jnp.sum(x, axis=0, keepdims=True)


def _col(t, idx_fn):
    return pl.BlockSpec((t, LANE), idx_fn)


def _conv_fwd(x, w_ref, taps):
    acc = None
    for j in range(taps):
        term = w_ref[pl.ds(j, 1), :] * _shift_down(x, taps - 1 - j)
        acc = term if acc is None else acc + term
    return acc


def _conv_bwd(x, dy, w_ref, dw_ref, taps):
    dx = None
    for j in range(taps):
        term = w_ref[pl.ds(j, 1), :] * _shift_up(dy, taps - 1 - j)
        dx = term if dx is None else dx + term
        dw_ref[pl.ds(j, 1), :] = _colsum(dy * _shift_down(x, taps - 1 - j))
    return dx


def _qkv_prep_fwd(proj, conv_w, *, name):
    t = proj.shape[0]
    scale = HEAD_DIM ** -0.5

    def body(x_ref, w_ref, o_ref):
        j = pl.program_id(0)
        c = _conv_fwd(x_ref[...], w_ref, QKV_TAPS)
        s = c * _sigmoid(c)
        r = lax.rsqrt(jnp.sum(s * s, axis=-1, keepdims=True) + EPS)
        f = jnp.where(j < 2 * HEADS, r, 1.0) * jnp.where(j < HEADS, scale, 1.0)
        o_ref[0] = s * f

    return _pcall(body, name=name, out_shape=jax.ShapeDtypeStruct((3 * HEADS, t, LANE), F32), grid=(3 * HEADS,),
                  in_specs=[_col(t, lambda j: (0, j)), pl.BlockSpec((QKV_TAPS, LANE), lambda j: (0, j))],
                  out_specs=pl.BlockSpec((1, t, LANE), lambda j: (j, 0, 0)), semantics=("parallel",),
                  vmem_limit=VMEM_LIMIT)(proj, conv_w)


def _qkv_prep_bwd(proj, conv_w, dqkv, *, name):
    t = proj.shape[0]
    scale = HEAD_DIM ** -0.5

    def body(x_ref, w_ref, d_ref, dx_ref, dw_ref):
        j = pl.program_id(0)
        xv = x_ref[...]
        c = _conv_fwd(xv, w_ref, QKV_TAPS)
        sig = _sigmoid(c)
        s = c * sig
        r = lax.rsqrt(jnp.sum(s * s, axis=-1, keepdims=True) + EPS)
        n0 = s * r
        dv = d_ref[0]
        dn0 = dv * jnp.where(j < HEADS, scale, 1.0)
        ds_norm = r * (dn0 - n0 * jnp.sum(dn0 * n0, axis=-1, keepdims=True))
        ds = jnp.where(j < 2 * HEADS, ds_norm, dv)
        dc = ds * sig * (1.0 + c * (1.0 - sig))
        dx_ref[...] = _conv_bwd(xv, dc, w_ref, dw_ref, QKV_TAPS).astype(BF16)

    return _pcall(body, name=name,
                  out_shape=(jax.ShapeDtypeStruct((t, 3 * A_DIM), BF16), jax.ShapeDtypeStruct((QKV_TAPS, 3 * A_DIM), F32)),
                  grid=(3 * HEADS,),
                  in_specs=[_col(t, lambda j: (0, j)), pl.BlockSpec((QKV_TAPS, LANE), lambda j: (0, j)),
                            pl.BlockSpec((1, t, LANE), lambda j: (j, 0, 0))],
                  out_specs=(_col(t, lambda j: (0, j)), pl.BlockSpec((QKV_TAPS, LANE), lambda j: (0, j))),
                  semantics=("parallel",), vmem_limit=VMEM_LIMIT)(proj, conv_w, dqkv)


def _lane_pick(x, lane_idx, lane):
    return jnp.broadcast_to(jnp.sum(jnp.where(lane == lane_idx, x, 0.0), axis=-1, keepdims=True), x.shape)


def _gates_fwd(proj, alog, dtb, *, name):
    t = proj.shape[0]

    def body(x_ref, alog_ref, dtb_ref, g_ref, b_ref):
        xv = x_ref[...]
        lane = lax.broadcasted_iota(jnp.int32, xv.shape, 1)
        gall = -jnp.exp(alog_ref[...]) * _softplus(xv + dtb_ref[...])
        ball = _sigmoid(xv)
        for h in range(HEADS):
            g_ref[h] = _lane_pick(gall, h, lane)
            b_ref[h] = _lane_pick(ball, HEADS + h, lane)

    out = jax.ShapeDtypeStruct((HEADS, t, LANE), F32)
    whole = pl.BlockSpec((HEADS, t, LANE), lambda i: (0, 0, 0))
    return _pcall(body, name=name, out_shape=(out, out), grid=(1,),
                  in_specs=[_col(t, lambda i: (0, AB_COL // LANE)), _vec(LANE), _vec(LANE)], out_specs=(whole, whole),
                  semantics=("arbitrary",), vmem_limit=VMEM_LIMIT)(proj, alog, dtb)


def _gates_bwd(proj, alog, dtb, dg, dbeta, *, name):
    t = proj.shape[0]

    def body(x_ref, alog_ref, dtb_ref, dg_ref, db_ref, dab_ref, dalog_ref, ddtb_ref):
        xv = x_ref[...]
        lane = lax.broadcasted_iota(jnp.int32, xv.shape, 1)
        lane1 = lax.broadcasted_iota(jnp.int32, (1, LANE), 1)
        z = xv + dtb_ref[...]
        nea = -jnp.exp(alog_ref[...])
        da_f = nea * _sigmoid(z)
        g_f = nea * _softplus(z)
        ball = _sigmoid(xv)
        db_f = ball * (1.0 - ball)
        dab = jnp.zeros_like(xv)
        dalog = jnp.zeros((1, LANE), F32)
        for h in range(HEADS):
            dgh = dg_ref[h]
            dab = dab + jnp.where(lane == h, dgh * da_f, 0.0) + jnp.where(lane == HEADS + h, db_ref[h] * db_f, 0.0)
            dalog = dalog + jnp.where(lane1 == h, _colsum(dgh * g_f), 0.0)
        dab_ref[...] = dab.astype(BF16)
        dalog_ref[...] = dalog
        ddtb_ref[...] = jnp.where(lane1 < HEADS, _colsum(dab), 0.0)

    whole = pl.BlockSpec((HEADS, t, LANE), lambda i: (0, 0, 0))
    vec = jax.ShapeDtypeStruct((1, LANE), F32)
    return _pcall(body, name=name, out_shape=(jax.ShapeDtypeStruct((t, LANE), BF16), vec, vec), grid=(1,),
                  in_specs=[_col(t, lambda i: (0, AB_COL // LANE)), _vec(LANE), _vec(LANE), whole, whole],
                  out_specs=(_col(t, lambda i: (0, 0)), _vec(LANE), _vec(LANE)), semantics=("arbitrary",),
                  vmem_limit=VMEM_LIMIT)(proj, alog, dtb, dg, dbeta)


Z_COL = 3 * A_DIM // LANE


def _apost_fwd(o, proj, gn, *, name):
    t = proj.shape[0]

    def body(o_ref, z_ref, gn_ref, y_ref):
        ov = o_ref[0]
        z = z_ref[...]
        r = lax.rsqrt(jnp.mean(ov * ov, axis=-1, keepdims=True) + EPS)
        y_ref[...] = (ov * r * gn_ref[...] * (z * _sigmoid(z))).astype(BF16)

    return _pcall(body, name=name, out_shape=jax.ShapeDtypeStruct((t, A_DIM), BF16), grid=(HEADS,),
                  in_specs=[pl.BlockSpec((1, t, LANE), lambda h: (h, 0, 0)), _col(t, lambda h: (0, Z_COL + h)),
                            pl.BlockSpec((1, LANE), lambda h: (0, 0))],
                  out_specs=_col(t, lambda h: (0, h)), semantics=("parallel",), vmem_limit=VMEM_LIMIT)(o, proj, gn)


def _apost_bwd(o, proj, gn, dmixed, *, name):
    t = proj.shape[0]

    def body(o_ref, z_ref, gn_ref, d_ref, do_ref, dz_ref, dgn_ref):
        ov = o_ref[0]
        z = z_ref[...]
        gnv = gn_ref[...]
        dv = d_ref[...]
        r = lax.rsqrt(jnp.mean(ov * ov, axis=-1, keepdims=True) + EPS)
        ohat = ov * r
        sig = _sigmoid(z)
        dy = dv * (z * sig)
        dz_ref[...] = (dv * ohat * gnv * sig * (1.0 + z * (1.0 - sig))).astype(BF16)
        dyo = dy * gnv
        do_ref[0] = r * (dyo - ohat * jnp.mean(dyo * ohat, axis=-1, keepdims=True))
        part = _colsum(dy * ohat)

        @pl.when(pl.program_id(0) == 0)
        def _():
            dgn_ref[...] = part

        @pl.when(pl.program_id(0) > 0)
        def _():
            dgn_ref[...] += part

    return _pcall(body, name=name,
                  out_shape=(jax.ShapeDtypeStruct((HEADS, t, LANE), F32), jax.ShapeDtypeStruct((t, A_DIM), BF16),
                             jax.ShapeDtypeStruct((1, LANE), F32)),
                  grid=(HEADS,),
                  in_specs=[pl.BlockSpec((1, t, LANE), lambda h: (h, 0, 0)), _col(t, lambda h: (0, Z_COL + h)),
                            pl.BlockSpec((1, LANE), lambda h: (0, 0)), _col(t, lambda h: (0, h))],
                  out_specs=(pl.BlockSpec((1, t, LANE), lambda h: (h, 0, 0)), _col(t, lambda h: (0, h)),
                             pl.BlockSpec((1, LANE), lambda h: (0, 0))),
                  semantics=("arbitrary",), vmem_limit=VMEM_LIMIT)(o, proj, gn, dmixed)


POOL_COL = (AB_COL + LANE) // LANE
CB_COL = POOL_COL + POOL_DIM // LANE
CC_COL = CB_COL + CONV_DIM // LANE
CH_COL = CC_COL + CONV_DIM // LANE
MAX_WIN_LOG2 = 4


def _window_sums(x, shift):
    sums = []
    cur = x
    for k in range(MAX_WIN_LOG2):
        cur = cur + shift(cur, 1 << k)
        sums.append(cur)
    return sums


def _pick_window(sums, win):
    out = sums[-1]
    for k in range(MAX_WIN_LOG2 - 2, -1, -1):
        out = jnp.where(win == float(2 << k), sums[k], out)
    return out


def _pool_counts(shape, win):
    row = lax.broadcasted_iota(jnp.int32, shape, 0).astype(F32)
    return jnp.minimum(row + 1.0, win)


def _pool_fwd(proj, win, wbd, scale, *, name):
    t = proj.shape[0]

    def body(x_ref, win_ref, w_ref, s_ref, y_ref):
        xv = x_ref[...]
        winv = win_ref[...]
        pooled = _pick_window(_window_sums(xv, _shift_down), winv) / _pool_counts(xv.shape, winv) - xv
        y_ref[...] = (_dot(pooled, w_ref[0], NN) * s_ref[...]).astype(BF16)

    nb = POOL_DIM // LANE
    vec = pl.BlockSpec((1, LANE), lambda b: (0, b))
    return _pcall(body, name=name, out_shape=jax.ShapeDtypeStruct((t, POOL_DIM), BF16), grid=(nb,),
                  in_specs=[_col(t, lambda b: (0, POOL_COL + b)), vec, pl.BlockSpec((1, LANE, LANE), lambda b: (b, 0, 0)), vec],
                  out_specs=_col(t, lambda b: (0, b)), semantics=("parallel",), vmem_limit=VMEM_LIMIT)(proj, win, wbd, scale)


def _pool_bwd(proj, win, wbd, scale, dmixed, *, name):
    t = proj.shape[0]

    def body(x_ref, win_ref, w_ref, s_ref, d_ref, dx_ref, dw_ref, ds_ref):
        xv = x_ref[...]
        winv = win_ref[...]
        cnt = _pool_counts(xv.shape, winv)
        pooled = _pick_window(_window_sums(xv, _shift_down), winv) / cnt - xv
        dv = d_ref[...]
        ds_ref[...] = _colsum(dv * _dot(pooled, w_ref[0], NN))
        dy0 = dv * s_ref[...]
        dw_ref[0] = _dot(pooled, dy0, TN)
        dpooled = _dot(dy0, w_ref[0], NT)
        dmean = dpooled / cnt
        dx_ref[...] = (_pick_window(_window_sums(dmean, _shift_up), winv) - dpooled).astype(BF16)

    nb = POOL_DIM // LANE
    vec = pl.BlockSpec((1, LANE), lambda b: (0, b))
    mat = pl.BlockSpec((1, LANE, LANE), lambda b: (b, 0, 0))
    first = A_DIM // LANE
    return _pcall(body, name=name,
                  out_shape=(jax.ShapeDtypeStruct((t, POOL_DIM), BF16), jax.ShapeDtypeStruct((nb, LANE, LANE), F32),
                             jax.ShapeDtypeStruct((1, POOL_DIM), F32)),
                  grid=(nb,),
                  in_specs=[_col(t, lambda b: (0, POOL_COL + b)), vec, mat, vec, _col(t, lambda b: (0, first + b))],
                  out_specs=(_col(t, lambda b: (0, b)), mat, vec), semantics=("parallel",),
                  vmem_limit=VMEM_LIMIT)(proj, win, wbd, scale, dmixed)


def _sconv_fwd(proj, w, *, name):
    t = proj.shape[0]

    def body(cb_ref, cc_ref, ch_ref, w_ref, y_ref):
        y_ref[...] = (cb_ref[...] * _conv_fwd(cc_ref[...] * ch_ref[...], w_ref, CONV_TAPS)).astype(BF16)

    nb = CONV_DIM // LANE
    return _pcall(body, name=name, out_shape=jax.ShapeDtypeStruct((t, CONV_DIM), BF16), grid=(nb,),
                  in_specs=[_col(t, lambda b: (0, CB_COL + b)), _col(t, lambda b: (0, CC_COL + b)),
                            _col(t, lambda b: (0, CH_COL + b)), pl.BlockSpec((CONV_TAPS, LANE), lambda b: (0, b))],
                  out_specs=_col(t, lambda b: (0, b)), semantics=("parallel",), vmem_limit=VMEM_LIMIT)(proj, proj, proj, w)


def _sconv_bwd(proj, w, dmixed, *, name):
    t = proj.shape[0]

    def body(cb_ref, cc_ref, ch_ref, w_ref, d_ref, dcb_ref, dcc_ref, dch_ref, dw_ref):
        cc = cc_ref[...]
        ch = ch_ref[...]
        u = cc * ch
        dv = d_ref[...]
        dcb_ref[...] = (dv * _conv_fwd(u, w_ref, CONV_TAPS)).astype(BF16)
        du = _conv_bwd(u, dv * cb_ref[...], w_ref, dw_ref, CONV_TAPS)
        dcc_ref[...] = (du * ch).astype(BF16)
        dch_ref[...] = (du * cc).astype(BF16)

    nb = CONV_DIM // LANE
    first = (A_DIM + POOL_DIM) // LANE
    act = jax.ShapeDtypeStruct((t, CONV_DIM), BF16)
    wspec = pl.BlockSpec((CONV_TAPS, LANE), lambda b: (0, b))
    ospec = _col(t, lambda b: (0, b))
    return _pcall(body, name=name, out_shape=(act, act, act, jax.ShapeDtypeStruct((CONV_TAPS, CONV_DIM), F32)), grid=(nb,),
                  in_specs=[_col(t, lambda b: (0, CB_COL + b)), _col(t, lambda b: (0, CC_COL + b)),
                            _col(t, lambda b: (0, CH_COL + b)), wspec, _col(t, lambda b: (0, first + b))],
                  out_specs=(ospec, ospec, ospec, wspec), semantics=("parallel",),
                  vmem_limit=VMEM_LIMIT)(proj, proj, proj, w, dmixed)


def _chunk_masks():
    r = lax.broadcasted_iota(jnp.int32, (CHUNK, CHUNK), 0)
    c = lax.broadcasted_iota(jnp.int32, (CHUNK, CHUNK), 1)
    return r >= c, r > c, jnp.where(r == c, 1.0, 0.0).astype(F32)


def _split(a):
    hi = a.astype(BF16)
    return hi, (a - hi.astype(F32)).astype(BF16)


def _dot_split(a, b, dims):
    (ah, al), (bh, bl) = a, b
    return _dot(ah, bh, dims) + _dot(ah, bl, dims) + _dot(al, bh, dims)


def _tri_inv(lows, eye):
    xs = [eye - low for low in lows]
    ps = [_split(low) for low in lows]
    ps = [_split(_dot_split(p, p, NN)) for p in ps]
    for i in range(5):
        xs = [x + _dot_split(_split(x), p, NN) for x, p in zip(xs, ps)]
        if i < 4:
            ps = [_split(_dot_split(p, p, NN)) for p in ps]
    return xs


def _prefix_sum_rows(x):
    for k in range(6):
        x = x + _shift_down(x, 1 << k)
    return x


def _suffix_sum_rows(x):
    for k in range(6):
        x = x + _shift_up(x, 1 << k)
    return x


def _chunk_decay(g, incl):
    gcb = _prefix_sum_rows(g)
    gtot = _colsum(g)
    col = gcb[:, :CHUNK]
    row = gcb.T[:CHUNK, :]
    decay = jnp.exp(jnp.where(incl, col - row, -1e30))
    return gcb, gtot, decay


CHUNKS_PER_STEP = 2


def _heads_of(ref, base, rows):
    return [ref[base + h, rows, :] for h in range(HEADS)]


def _chunk_rows(j):
    return pl.ds(j * CHUNK, CHUNK)


def _deltanet_prep(qkv, g, beta, *, name):
    t = qkv.shape[1]
    n_chunks = t // CHUNK
    per = CHUNKS_PER_STEP
    probs = [(j, h) for j in range(per) for h in range(HEADS)]

    def body(qkv_ref, g_ref, b_ref, u_ref, w_ref, qg_ref, kg_ref, attn_ref, tm_ref):
        incl, strict, eye = _chunk_masks()
        q = [qkv_ref[h, _chunk_rows(j), :] for j, h in probs]
        k = [qkv_ref[HEADS + h, _chunk_rows(j), :] for j, h in probs]
        v = [qkv_ref[2 * HEADS + h, _chunk_rows(j), :] for j, h in probs]
        bv = [b_ref[h, _chunk_rows(j), :] for j, h in probs]
        dec = [_chunk_decay(g_ref[h, _chunk_rows(j), :], incl) for j, h in probs]
        kb = [a * b for a, b in zip(k, bv)]
        low = [jnp.where(strict, _dot(a, b, NT) * d[2], 0.0) for a, b, d in zip(kb, k, dec)]
        tm = _tri_inv(low, eye)
        egc = [jnp.exp(d[0]) for d in dec]
        u = [_dot(m, a * b, NN) for m, a, b in zip(tm, v, bv)]
        w = [_dot(m, a * e, NN) for m, a, e in zip(tm, kb, egc)]
        attn = [_dot(a, b, NT) * d[2] for a, b, d in zip(q, k, dec)]
        for i, (j, h) in enumerate(probs):
            rows = _chunk_rows(j)
            u_ref[h, rows, :] = u[i]
            w_ref[h, rows, :] = w[i].astype(BF16)
            qg_ref[h, rows, :] = (q[i] * egc[i]).astype(BF16)
            kg_ref[h, rows, :] = (k[i] * jnp.exp(dec[i][1] - dec[i][0])).astype(BF16)
            attn_ref[j, h] = attn[i].astype(BF16)
            tm_ref[j, h] = tm[i]

    act = lambda heads: pl.BlockSpec((heads, per * CHUNK, LANE), lambda n: (0, n, 0))
    mat = pl.BlockSpec((per, HEADS, CHUNK, CHUNK), lambda n: (n, 0, 0, 0))
    return _pcall(
        body, name=name,
        out_shape=(jax.ShapeDtypeStruct((HEADS, t, LANE), F32),) + (jax.ShapeDtypeStruct((HEADS, t, LANE), BF16),) * 3
        + (jax.ShapeDtypeStruct((n_chunks, HEADS, CHUNK, CHUNK), BF16), jax.ShapeDtypeStruct((n_chunks, HEADS, CHUNK, CHUNK), F32)),
        grid=(n_chunks // per,), in_specs=[act(3 * HEADS), act(HEADS), act(HEADS)],
        out_specs=(act(HEADS),) * 4 + (mat, mat), semantics=("parallel",), vmem_limit=VMEM_LIMIT)(qkv, g, beta)


SCAN_CHUNKS_PER_STEP = 4


def _deltanet_scan(u, w, qg, kg, attn, g, *, name):
    t = u.shape[1]
    n_chunks = t // CHUNK
    per = SCAN_CHUNKS_PER_STEP

    def body(u_ref, w_ref, qg_ref, kg_ref, attn_ref, g_ref, o_ref, vn_ref, st_ref, s_ref):
        @pl.when(pl.program_id(0) == 0)
        def _():
            s_ref[...] = jnp.zeros_like(s_ref)

        for j in range(per):
            rows = _chunk_rows(j)
            s = [s_ref[h] for h in range(HEADS)]
            vn = [u_ref[h, rows, :] - _dot(w_ref[h, rows, :], s[h], NN) for h in range(HEADS)]
            o = [_dot(qg_ref[h, rows, :], s[h], NN) + _dot(attn_ref[j, h], vn[h], NN) for h in range(HEADS)]
            eg = [jnp.exp(_colsum(g_ref[h, rows, :])) for h in range(HEADS)]
            for h in range(HEADS):
                st_ref[j, h] = s[h]
                s_ref[h] = s[h] * eg[h] + _dot(kg_ref[h, rows, :], vn[h], TN)
                o_ref[h, rows, :] = o[h]
                vn_ref[h, rows, :] = vn[h]

    act = pl.BlockSpec((HEADS, per * CHUNK, LANE), lambda n: (0, n, 0))
    out = jax.ShapeDtypeStruct((HEADS, t, LANE), F32)
    return _pcall(
        body, name=name, out_shape=(out, out, jax.ShapeDtypeStruct((n_chunks, HEADS, LANE, LANE), F32)), grid=(n_chunks // per,),
        in_specs=[act] * 4 + [pl.BlockSpec((per, HEADS, CHUNK, CHUNK), lambda n: (n, 0, 0, 0)), act],
        out_specs=(act, act, pl.BlockSpec((per, HEADS, LANE, LANE), lambda n: (n, 0, 0, 0))),
        scratch_shapes=[pltpu.VMEM((HEADS, LANE, LANE), F32)], semantics=("arbitrary",))(u, w, qg, kg, attn, g)


def _deltanet_bscan(w, qg, kg, attn, g, do, *, name):
    t = w.shape[1]
    n_chunks = t // CHUNK
    per = SCAN_CHUNKS_PER_STEP
    steps = n_chunks // per

    def body(w_ref, qg_ref, kg_ref, attn_ref, g_ref, do_ref, dvn_ref, dsn_ref, ds_ref):
        @pl.when(pl.program_id(0) == 0)
        def _():
            ds_ref[...] = jnp.zeros_like(ds_ref)

        for j in reversed(range(per)):
            rows = _chunk_rows(j)
            dsn = [ds_ref[h] for h in range(HEADS)]
            dov = [do_ref[h, rows, :] for h in range(HEADS)]
            dvn = [_dot(attn_ref[j, h], dov[h], TN) + _dot(kg_ref[h, rows, :], dsn[h], NN) for h in range(HEADS)]
            eg = [jnp.exp(_colsum(g_ref[h, rows, :])) for h in range(HEADS)]
            for h in range(HEADS):
                dsn_ref[j, h] = dsn[h]
                ds_ref[h] = _dot(qg_ref[h, rows, :], dov[h], TN) + eg[h] * dsn[h] - _dot(w_ref[h, rows, :], dvn[h], TN)
                dvn_ref[h, rows, :] = dvn[h]

    act = pl.BlockSpec((HEADS, per * CHUNK, LANE), lambda n: (0, steps - 1 - n, 0))
    return _pcall(
        body, name=name,
        out_shape=(jax.ShapeDtypeStruct((HEADS, t, LANE), F32), jax.ShapeDtypeStruct((n_chunks, HEADS, LANE, LANE), F32)),
        grid=(steps,),
        in_specs=[act] * 3 + [pl.BlockSpec((per, HEADS, CHUNK, CHUNK), lambda n: (steps - 1 - n, 0, 0, 0)), act, act],
        out_specs=(act, pl.BlockSpec((per, HEADS, LANE, LANE), lambda n: (steps - 1 - n, 0, 0, 0))),
        scratch_shapes=[pltpu.VMEM((HEADS, LANE, LANE), F32)], semantics=("arbitrary",))(w, qg, kg, attn, g, do)


def _sum_all(x):
    return jnp.sum(jnp.sum(x, axis=1, keepdims=True), axis=0, keepdims=True)


def _rowsum(x):
    return jnp.sum(x, axis=1, keepdims=True)


def _deltanet_post(qkv, g, beta, tmats, states, dstates, do, dvn, vn, *, name):
    t = qkv.shape[1]
    n_chunks = t // CHUNK
    per = CHUNKS_PER_STEP
    probs = [(j, h) for j in range(per) for h in range(HEADS)]

    def body(qkv_ref, g_ref, b_ref, tm_ref, st_ref, dsn_ref, do_ref, dvn_ref, vn_ref, dqkv_ref, dg_ref, db_ref):
        incl, strict, _ = _chunk_masks()
        ones = jnp.ones((CHUNK, LANE), BF16)
        last_row = lax.broadcasted_iota(jnp.int32, (CHUNK, LANE), 0) == CHUNK - 1
        z = lambda f, *cols: [f(*a) for a in zip(*cols)]
        q = [qkv_ref[h, _chunk_rows(j), :] for j, h in probs]
        k = [qkv_ref[HEADS + h, _chunk_rows(j), :] for j, h in probs]
        v = [qkv_ref[2 * HEADS + h, _chunk_rows(j), :] for j, h in probs]
        bv = [b_ref[h, _chunk_rows(j), :] for j, h in probs]
        dov = [do_ref[h, _chunk_rows(j), :] for j, h in probs]
        dvn_ = [dvn_ref[h, _chunk_rows(j), :] for j, h in probs]
        vn_ = [vn_ref[h, _chunk_rows(j), :] for j, h in probs]
        tm = [tm_ref[j, h] for j, h in probs]
        s = [st_ref[j, h] for j, h in probs]
        dsn = [dsn_ref[j, h] for j, h in probs]
        dec = [_chunk_decay(g_ref[h, _chunk_rows(j), :], incl) for j, h in probs]
        decay = [d[2] for d in dec]
        egc = [jnp.exp(d[0]) for d in dec]
        ekg = [jnp.exp(d[1] - d[0]) for d in dec]
        kb = z(lambda a, b: a * b, k, bv)
        vb = z(lambda a, b: a * b, v, bv)
        kbg = z(lambda a, b: a * b, kb, egc)
        qg = z(lambda a, b: a * b, q, egc)
        kg = z(lambda a, b: a * b, k, ekg)
        kk = z(lambda a, b: _dot(a, b, NT), kb, k)
        qk = z(lambda a, b: _dot(a, b, NT), q, k)
        dattn = z(lambda a, b: jnp.where(incl, _dot(a, b, NT), 0.0), dov, vn_)
        dqg = z(lambda a, b: _dot(a, b, NT), dov, s)
        dkg = z(lambda a, b: _dot(a, b, NT), vn_, dsn)
        dglast = z(lambda a, b, c, d, e: _sum_all(a * b) * jnp.exp(e[1]) + _sum_all(c * d), s, dsn, dkg, kg, dec)
        dw = z(lambda a, b: -_dot(a, b, NT), dvn_, s)
        dtm = z(lambda a, b, c, d: _dot(a, b, NT) + _dot(c, d, NT), dvn_, vb, dw, kbg)
        dvb = z(lambda a, b: _dot(a, b, TN), tm, dvn_)
        dkbg = z(lambda a, b: _dot(a, b, TN), tm, dw)
        dlow = z(lambda a, b: jnp.where(strict, -_dot(_dot(a, b, TN), a, NT), 0.0), tm, dtm)
        dkk = z(lambda a, b: a * b, dlow, decay)
        dqk = z(lambda a, b: a * b, dattn, decay)
        dkb = z(lambda a, b, c, d: _dot(a, b, NN) + c * d, dkk, k, dkbg, egc)
        dk = z(lambda a, b, c, d, e, f, g_, h_: _dot(a, b, TN) + _dot(c, d, TN) + e * f + g_ * h_, dkk, kb, dqk, q, dkg, ekg, dkb, bv)
        dq = z(lambda a, b, c, d: _dot(a, b, NN) + c * d, dqk, k, dqg, egc)
        m = z(lambda a, b, c, d, e: (a * b + c * d) * e, dlow, kk, dattn, qk, decay)
        mcol = [_dot(mh, ones, TN) + _dot(ml, ones, TN) for mh, ml in (_split(a) for a in m)]
        for i, (j, h) in enumerate(probs):
            rows = _chunk_rows(j)
            dqkv_ref[h, rows, :] = dq[i]
            dqkv_ref[HEADS + h, rows, :] = dk[i]
            dqkv_ref[2 * HEADS + h, rows, :] = dvb[i] * bv[i]
            db_ref[h, rows, :] = jnp.broadcast_to(_rowsum(dkb[i] * k[i] + dvb[i] * v[i]), (CHUNK, LANE))
            dgc = (_rowsum(dqg[i] * qg[i] + dkbg[i] * kbg[i] - dkg[i] * kg[i]) + _rowsum(m[i]) - mcol[i]
                   + jnp.where(last_row, dglast[i], 0.0))
            dg_ref[h, rows, :] = _suffix_sum_rows(dgc)

    act = lambda heads: pl.BlockSpec((heads, per * CHUNK, LANE), lambda n: (0, n, 0))
    mat = lambda d: pl.BlockSpec((per, HEADS, d, d), lambda n: (n, 0, 0, 0))
    out = jax.ShapeDtypeStruct((HEADS, t, LANE), F32)
    return _pcall(
        body, name=name, out_shape=(jax.ShapeDtypeStruct((3 * HEADS, t, LANE), F32), out, out), grid=(n_chunks // per,),
        in_specs=[act(3 * HEADS), act(HEADS), act(HEADS), mat(CHUNK), mat(LANE), mat(LANE), act(HEADS), act(HEADS), act(HEADS)],
        out_specs=(act(3 * HEADS), act(HEADS), act(HEADS)), semantics=("parallel",),
        vmem_limit=VMEM_LIMIT)(qkv, g, beta, tmats, states, dstates, do, dvn, vn)


ANY = pl.BlockSpec(memory_space=pl.ANY)
PEERS = N_DEV - 1


def _all_gather(arrays, *, name):
    n = len(arrays)

    def body(*refs):
        ins, outs = refs[:n], refs[n:2 * n]
        send_sems, recv_sems, local_sems = refs[2 * n:]
        x, y, c = lax.axis_index("x"), lax.axis_index("y"), lax.axis_index("c")
        me, sibling = (x, y, c), (x, y, 1 - c)
        chips = [(1 - x, y), (x, 1 - y), (1 - x, 1 - y)]

        def copy(a, k, block, to, src=None):
            dst = outs[a].at[4 * block[0] + 2 * block[1] + block[2]]
            return pltpu.make_async_remote_copy(src_ref=dst if src is None else src, dst_ref=dst, send_sem=send_sems.at[a * PEERS + k],
                                                recv_sem=recv_sems.at[a * PEERS + k], device_id=to, device_id_type=MESH)

        local = [pltpu.make_async_copy(ins[a], outs[a].at[4 * x + 2 * y + c], local_sems.at[a]) for a in range(n)]
        for cp in local:
            cp.start()
        first = []
        for a in range(n):
            first.append(copy(a, 0, me, sibling, src=ins[a]))
            first += [copy(a, 1 + j, me, (*chip, c), src=ins[a]) for j, chip in enumerate(chips)]
        for cp in first:
            cp.start()
        passed = []
        for a in range(n):
            for j, chip in enumerate(chips):
                copy(a, 1 + j, (*chip, c), me).wait_recv()
                fwd = copy(a, 4 + j, (*chip, c), sibling)
                fwd.start()
                passed.append(fwd)
        for a in range(n):
            copy(a, 0, sibling, me).wait_recv()
            for j, chip in enumerate(chips):
                copy(a, 4 + j, (*chip, 1 - c), me).wait_recv()
        for cp in first + passed:
            cp.wait_send()
        for cp in local:
            cp.wait()

    return _pcall(body, name=name, out_shape=tuple(jax.ShapeDtypeStruct((N_DEV,) + a.shape, a.dtype) for a in arrays),
                  in_specs=[ANY] * n, out_specs=(ANY,) * n,
                  scratch_shapes=[pltpu.SemaphoreType.DMA((n * PEERS,)), pltpu.SemaphoreType.DMA((n * PEERS,)),
                                  pltpu.SemaphoreType.DMA((n,))])(*arrays)


CHIPS = 4


def _pair_exchange(arrays, *, name):
    n = len(arrays)

    def body(*refs):
        ins, outs = refs[:n], refs[n:2 * n]
        send_sems, recv_sems = refs[2 * n:]
        x, y, c = lax.axis_index("x"), lax.axis_index("y"), lax.axis_index("c")
        copies = []
        for a in range(n):
            for q in range(CHIPS):
                cp = pltpu.make_async_remote_copy(src_ref=ins[a].at[2 * q + 1 - c], dst_ref=outs[a].at[q],
                                                  send_sem=send_sems.at[a * CHIPS + q], recv_sem=recv_sems.at[a * CHIPS + q],
                                                  device_id=(x, y, 1 - c), device_id_type=MESH)
                cp.start()
                copies.append(cp)
        for cp in copies:
            cp.wait()

    return _pcall(body, name=name, out_shape=tuple(jax.ShapeDtypeStruct((CHIPS,) + a.shape[1:], a.dtype) for a in arrays),
                  in_specs=[ANY] * n, out_specs=(ANY,) * n,
                  scratch_shapes=[pltpu.SemaphoreType.DMA((n * CHIPS,)), pltpu.SemaphoreType.DMA((n * CHIPS,))])(*arrays)


def _pair_add(blocks, theirs, *, name):
    _, r, c_ = blocks.shape
    tr = _tile(r, 512, 16)

    def body(mine_ref, theirs_ref, o_ref):
        core = lax.axis_index("c")
        own = jnp.where(core == 0, mine_ref[0, 0].astype(F32), mine_ref[0, 1].astype(F32))
        o_ref[0] = (own + theirs_ref[0].astype(F32)).astype(o_ref.dtype)

    spec = pl.BlockSpec((1, tr, c_), lambda q, i: (q, i, 0))
    return _pcall(body, name=name, out_shape=jax.ShapeDtypeStruct(theirs.shape, theirs.dtype), grid=(CHIPS, r // tr),
                  in_specs=[pl.BlockSpec((1, 2, tr, c_), lambda q, i: (q, 0, i, 0)), spec], out_specs=spec,
                  semantics=("parallel", "parallel"), vmem_limit=VMEM_LIMIT)(blocks.reshape(CHIPS, 2, r, c_), theirs)


HBM = pl.BlockSpec(memory_space=pltpu.HBM)
SEM = pl.BlockSpec(memory_space=pltpu.SEMAPHORE)
EFFECT = pltpu.SideEffectType.DATAFLOW_SIDE_EFFECTING


GATHER, CHIP_SCATTER = "gather", "chip_scatter"
PEERS_OF = {GATHER: N_DEV - 1, CHIP_SCATTER: CHIPS - 1}


def _direct_copies(srcs, lands, send_sems, recv_sems, local_sems, kind):
    x, y, c = lax.axis_index("x"), lax.axis_index("y"), lax.axis_index("c")
    peers = PEERS_OF[kind]
    copies = []
    for a, (src, land) in enumerate(zip(srcs, lands)):
        if kind == GATHER:
            mine = 4 * x + 2 * y + c
            copies.append(pltpu.make_async_copy(src, land.at[mine], local_sems.at[a]))
        else:
            mine = 2 * x + y
            copies.append(pltpu.make_async_copy(src.at[mine], land.at[mine], local_sems.at[a]))
        for k in range(1, peers + 1):
            bits = k if kind == GATHER else 2 * k
            px = 1 - x if bits & 4 else x
            py = 1 - y if bits & 2 else y
            pc = 1 - c if bits & 1 else c
            copies.append(pltpu.make_async_remote_copy(
                src_ref=src if kind == GATHER else src.at[2 * px + py], dst_ref=land.at[mine],
                send_sem=send_sems.at[a * peers + k - 1], recv_sem=recv_sems.at[a * peers + k - 1],
                device_id=(px, py, pc), device_id_type=MESH))
    return copies


def _exchange_start(groups, kind, *, name):
    srcs = [s for group in groups for s in group]
    n = len(srcs)
    sizes = [len(group) for group in groups]
    starts = [sum(sizes[:g]) for g in range(len(groups))]
    land_shapes = [(N_DEV,) + s.shape if kind == GATHER else s.shape for s in srcs]
    peers = PEERS_OF[kind]

    def body(*refs):
        srcs_, lands = refs[:n], refs[n:2 * n]
        token = refs[-1]
        for g, (at, size) in enumerate(zip(starts, sizes)):
            send_sems, recv_sems, local_sems = refs[2 * n + 3 * g:2 * n + 3 * g + 3]
            for cp in _direct_copies(srcs_[at:at + size], lands[at:at + size], send_sems, recv_sems, local_sems, kind):
                cp.start()
        token[...] = jnp.zeros_like(token)

    sems = tuple(t for size in sizes for t in (pltpu.SemaphoreType.DMA((size * peers,)), pltpu.SemaphoreType.DMA((size * peers,)),
                                               pltpu.SemaphoreType.DMA((size,))))
    thru = tuple(pltpu.HBM(s.shape, s.dtype) for s in srcs) + tuple(pltpu.HBM(shp, s.dtype) for shp, s in zip(land_shapes, srcs))
    ins = [pltpu.with_memory_space_constraint(s, pltpu.HBM) for s in srcs]
    ins += [pltpu.with_memory_space_constraint(lax.empty(shp, s.dtype), pltpu.HBM) for shp, s in zip(land_shapes, srcs)]
    out = pl.pallas_call(
        body, name=name, out_shape=sems + thru + (jax.ShapeDtypeStruct((SUBLANE, LANE), F32),), in_specs=[HBM] * (2 * n),
        out_specs=(SEM,) * len(sems) + (HBM,) * (2 * n) + (pl.BlockSpec(memory_space=pltpu.VMEM),),
        input_output_aliases={i: len(sems) + i for i in range(2 * n)},
        compiler_params=pltpu.CompilerParams(has_side_effects=EFFECT))(*ins)
    arrays = out[len(sems):-1]
    started = [tuple(out[3 * g:3 * g + 3]) + tuple(arrays[at:at + size]) + tuple(arrays[n + at:n + at + size])
               for g, (at, size) in enumerate(zip(starts, sizes))]
    return started, out[-1]


def _exchange_wait(started, after, kind, *, name):
    n = (len(started) - 3) // 2
    sems, arrays = started[:3], started[3:]

    def body(*refs):
        srcs_, lands = refs[:n], refs[n:2 * n]
        send_sems, recv_sems, local_sems = refs[2 * n:2 * n + 3]
        for cp in _direct_copies(srcs_, lands, send_sems, recv_sems, local_sems, kind):
            cp.wait()

    out = pl.pallas_call(
        body, name=name, out_shape=tuple(pltpu.HBM(a.shape, a.dtype) for a in arrays),
        in_specs=[HBM] * (2 * n) + [SEM] * 3 + [ANY], out_specs=(HBM,) * (2 * n),
        input_output_aliases={i: i for i in range(2 * n)},
        compiler_params=pltpu.CompilerParams(has_side_effects=EFFECT))(*arrays, *sems, after)
    return out[n:]


def _adamw_reduce(w, parts, m, v, *, name):
    layers, r, c = w.shape
    assert len(parts) == layers
    senders = parts[0].shape[0]
    tr = _tile(r, 512, 16)
    tiles = r // tr
    bc1 = 1.0 - ADAM_B1 ** ADAM_STEP
    bc2 = 1.0 - ADAM_B2 ** ADAM_STEP

    def body(w_ref, *rest):
        p_refs = rest[:layers]
        m_ref, v_ref, g_ref, d_ref, nm_ref, nv_ref = rest[layers:]

        def update(p_ref):
            g = p_ref[0, :, pl.ds(0, c)].astype(F32)
            for s in range(1, senders):
                g = g + p_ref[s, :, pl.ds(0, c)].astype(F32)
            nm = ADAM_B1 * m_ref[0] + (1.0 - ADAM_B1) * g
            nv = ADAM_B2 * v_ref[0] + (1.0 - ADAM_B2) * (g * g)
            g_ref[0] = g
            nm_ref[0] = nm
            nv_ref[0] = nv
            d_ref[0] = -ADAM_LR * ((nm / bc1) / (jnp.sqrt(nv / bc2) + ADAM_EPS) + ADAM_WD * w_ref[0])

        for layer in range(layers):
            pl.when(pl.program_id(0) == layer)(functools.partial(update, p_refs[layer]))

    def part_spec(layer, shape):
        rest = 0 if layer > 0 else tiles - 1
        return pl.BlockSpec((senders, tr, shape[2]), lambda l, i: (0, jnp.where(l == layer, i, rest), 0))

    spec = pl.BlockSpec((1, tr, c), lambda l, i: (l, i, 0))
    out = jax.ShapeDtypeStruct((layers, r, c), F32)
    return _pcall(body, name=name, out_shape=(out,) * 4, grid=(layers, tiles),
                  in_specs=[spec] + [part_spec(layer, p.shape) for layer, p in enumerate(parts)] + [spec, spec],
                  out_specs=(spec,) * 4, semantics=("arbitrary", "arbitrary"), vmem_limit=VMEM_LIMIT)(w, *parts, m, v)


def _pool_windows():
    return jnp.repeat(jnp.asarray(POOL_WINDOWS, F32), POOL_DIM // len(POOL_WINDOWS))[None, :]


def _block_diag_pairs(pool_w):
    z = jnp.zeros_like(pool_w[0])
    return jnp.stack([jnp.block([[pool_w[2 * b], z], [z, pool_w[2 * b + 1]]]) for b in range(2)])


def _pad_lanes(vec):
    return jnp.zeros((1, LANE), F32).at[0, :vec.shape[0]].set(vec)


FF_SHARD = D_FF // N_DEV
FF_BLOCK = 384
D_FF_PAD = N_DEV * FF_BLOCK


def _layer_fwd(x, p_i, wt, fetch):
    wt = {**wt, **fetch(0, x)}
    h1 = _rmsnorm_fwd(x, wt["norm1_g"], name="rmsnorm_fwd")
    proj = _matmul(h1, wt["w_in"], "nn", name="mm_in")
    qkv = _qkv_prep_fwd(proj, wt["conv_qkv"], name="qkv_prep_fwd")
    g, beta = _gates_fwd(proj, wt["a_log"], wt["dt_bias"], name="gates_fwd")
    u, w, qg, kg, attn, tmats = _deltanet_prep(qkv, g, beta, name="deltanet_prep")
    o, vn, states = _deltanet_scan(u, w, qg, kg, attn, g, name="deltanet_scan")
    o_a = _apost_fwd(o, proj, wt["onorm_g"], name="apost_fwd")
    o_b = _pool_fwd(proj, wt["pool_win"], wt["pool_wbd"], wt["pool_scale"], name="pool_fwd")
    o_c = _sconv_fwd(proj, wt["sconv_w"], name="sconv_fwd")
    mixed = jnp.concatenate([o_a, o_b, o_c], axis=1)
    wt.update(fetch(1, mixed))
    x1 = _matmul(mixed, wt["w_out"], "nn", res=x, name="mm_out")
    h2 = _rmsnorm_fwd(x1, wt["norm2_g"], name="rmsnorm_fwd")
    wt.update(fetch(2, h2))
    ff, gate, up = _swiglu_fwd(h2, wt["w_gate"], wt["w_up"], name="swiglu_fwd")
    wt.update(fetch(3, ff))
    x2 = _matmul(ff, wt["w_down"], "nn", res=x1, name="mm_down")
    wt.update(fetch(4, x2))
    pgl = _matmul(x2, wt["ple_gate"], "nn", name="mm_pleg")
    pp = _matmul(p_i, wt["ple_proj"], "nn", b_blocked=True, name="mm_plep")
    x3 = _ple_fwd(x2, pgl, pp, name="ple_fwd")
    saved = dict(x=x, h1=h1, proj=proj, qkv=qkv, g=g, beta=beta, o=o, states=states, tmats=tmats, mixed=mixed, x1=x1, h2=h2,
                 gate=gate, up=up, ff=ff, x2=x2, pgl=pgl, pp=pp, p=p_i, w=w, qg=qg, kg=kg, attn=attn, vn=vn, wt=wt)
    return x3, saved


def _col_blocks(g):
    a = g.shape[0]
    return jnp.transpose(g.reshape(a, N_DEV, -1), (1, 0, 2))


def _cols_joined(blocks):
    return jnp.transpose(blocks, (1, 0, 2)).reshape(blocks.shape[1], -1)


def _layer_bwd(dx3, sv, emit, after=None):
    gr, big = {}, {}
    wt = sv["wt"]
    rows = D_MODEL // N_DEV
    dpgl, dpp = _ple_bwd(dx3, sv["pgl"], sv["pp"], name="ple_bwd", after=after)
    big["ple_proj"] = _matmul(sv["p"], dpp, "tn", out_blocked=(N_DEV, rows), out_dtype=BF16, name="mm_dplep")
    big["ple_gate"] = _matmul(sv["x2"], dpgl, "tn", out_dtype=BF16, name="mm_dpleg").reshape(N_DEV, rows, D_MODEL)
    dx2 = _matmul(dpgl, wt["ple_gate"], "nt", res=dx3, name="mm_dx2")
    big["w_down"] = _matmul(sv["ff"], dx2, "tn", out_dtype=BF16, name="mm_ddown").reshape(N_DEV, FF_BLOCK, D_MODEL)
    dgate, dup = _swiglu_bwd(dx2, wt["w_down"], sv["gate"], sv["up"], name="swiglu_bwd", after=emit(0, big))
    big["w_gate"] = _matmul(sv["h2"], dgate, "tn", out_blocked=(N_DEV, FF_BLOCK), out_dtype=BF16, name="mm_dgate")
    big["w_up"] = _matmul(sv["h2"], dup, "tn", out_blocked=(N_DEV, FF_BLOCK), out_dtype=BF16, name="mm_dup")
    dh2 = _matmul(dgate, wt["w_gate"], "nt", b_blocked=True, name="mm_dh2_gate")
    dh2 = _matmul(dup, wt["w_up"], "nt", b_blocked=True, res=dh2, name="mm_dh2_up")
    dx1, gr["norm2_g"] = _rmsnorm_bwd(sv["x1"], wt["norm2_g"], dh2, dx2, name="rmsnorm_bwd")
    big["w_out"] = _matmul(sv["mixed"], dx1, "tn", out_dtype=BF16, name="mm_dout").reshape(N_DEV, rows, D_MODEL)
    dmixed = _matmul(dx1, wt["w_out"], "nt", name="mm_dmixed", after=emit(1, big))
    proj = sv["proj"]
    dcb, dcc, dch, dsconv = _sconv_bwd(proj, wt["sconv_w"], dmixed, name="sconv_bwd")
    big["sconv_w"] = _col_blocks(dsconv)
    dhp, dwbd, gr["pool_scale"] = _pool_bwd(proj, wt["pool_win"], wt["pool_wbd"], wt["pool_scale"], dmixed, name="pool_bwd")
    half = LANE // 2
    gr["pool_w"] = jnp.stack([dwbd[0, :half, :half], dwbd[0, half:, half:], dwbd[1, :half, :half], dwbd[1, half:, half:]])
    do, dz, gr["onorm_g"] = _apost_bwd(sv["o"], proj, wt["onorm_g"], dmixed, name="apost_bwd")
    dvn, dstates = _deltanet_bscan(sv["w"], sv["qg"], sv["kg"], sv["attn"], sv["g"], do, name="deltanet_bscan")
    dqkv_h, dg, dbeta = _deltanet_post(sv["qkv"], sv["g"], sv["beta"], sv["tmats"], sv["states"], dstates, do, dvn, sv["vn"],
                                       name="deltanet_post")
    dab, dalog, ddtb = _gates_bwd(proj, wt["a_log"], wt["dt_bias"], dg, dbeta, name="gates_bwd")
    gr["a_log"], gr["dt_bias"] = dalog[0, :HEADS], ddtb[0, :HEADS]
    dqkv, dconv = _qkv_prep_bwd(proj, wt["conv_qkv"], dqkv_h, name="qkv_prep_bwd")
    big["conv_qkv"] = _col_blocks(dconv)
    dproj = jnp.concatenate([dqkv, dz, dab, dhp, dcb, dcc, dch], axis=1)
    dwin = _matmul(sv["h1"], dproj, "tn", out_dtype=BF16, name="mm_din")
    big["w_in"] = _col_blocks(jnp.concatenate([dwin[:, :AB_COL + 2 * HEADS], dwin[:, AB_COL + LANE:]], axis=1))
    dh1 = _matmul(dproj, wt["w_in"], "nt", name="mm_dh1", after=emit(2, big))
    dx, gr["norm1_g"] = _rmsnorm_bwd(sv["x"], wt["norm1_g"], dh1, dx1, name="rmsnorm_bwd")
    return dx, gr


FETCH_GROUPS = (("w_in", "conv_qkv", "sconv_w"), ("w_out",), ("w_gate", "w_up"), ("w_down",), ("ple_gate", "ple_proj"))
EMIT_GROUPS = (("ple_proj", "ple_gate", "w_down"), ("w_gate", "w_up", "w_out"), ("w_in", "conv_qkv", "sconv_w"))


def _small_weights(w, i):
    return dict(
        norm1_g=w["norm1_g"][i][None], norm2_g=w["norm2_g"][i][None], onorm_g=w["onorm_g"][i][None],
        a_log=_pad_lanes(w["a_log"][i]), dt_bias=_pad_lanes(w["dt_bias"][i]),
        pool_scale=w["pool_scale"][i][None], pool_win=_pool_windows(), pool_wbd=_block_diag_pairs(w["pool_w"][i]))


def _as_read(name, gathered):
    if name == "w_in":
        w_in = _cols_joined(gathered)
        return jnp.concatenate([w_in[:, :AB_COL + 2 * HEADS], jnp.zeros((D_MODEL, LANE - 2 * HEADS), BF16),
                                w_in[:, AB_COL + 2 * HEADS:]], axis=1)
    if name in ("conv_qkv", "sconv_w"):
        return _cols_joined(gathered)
    if name in ("w_gate", "w_up", "ple_proj"):
        return gathered
    return gathered.reshape(-1, D_MODEL)


def _layer_weights(gathered, w, i):
    return {**_small_weights(w, i), **{k: _as_read(k, g) for k, g in gathered.items()}}


def _local_step(x, p, target, layers, final_g):
    saved = []
    h = x
    for i in range(DEPTH):
        replicated = {k: v for k, v in layers[i].items() if k not in SHARDED}
        h, sv = _layer_fwd(h, p[i], replicated, lambda group, after, i=i: {k: layers[i][k] for k in FETCH_GROUPS[group]})
        saved.append(sv)
    dx, dgf, loss = _loss_head(h, final_g, target, name="loss_head")
    big, small = [{} for _ in range(DEPTH)], [None] * DEPTH
    for i in reversed(range(DEPTH)):
        dx, small[i] = _layer_bwd(dx, saved[i], lambda group, blocks, i=i: big[i].update({k: blocks[k] for k in EMIT_GROUPS[group]}))
    return loss, dx, big, small, dgf


SHARDED = ("w_in", "w_gate", "w_up", "w_down", "w_out", "ple_gate", "ple_proj", "conv_qkv", "sconv_w")
SMALL = ("norm1_g", "a_log", "dt_bias", "onorm_g", "pool_w", "pool_scale", "norm2_g", "final_g")
SLAB_COLS = 1024


def _payload(name, shard):
    if name in ("conv_qkv", "sconv_w"):
        return shard
    out = shard.astype(BF16)
    if name in ("w_gate", "w_up"):
        out = jnp.pad(out, ((0, 0), (0, FF_BLOCK - FF_SHARD)))
    if name == "w_down":
        out = jnp.pad(out, ((0, FF_BLOCK - FF_SHARD), (0, 0)))
    return out


def _slab_rows(shape):
    size = 1
    for s in shape:
        size *= s
    return SUBLANE * -(-size // (SUBLANE * SLAB_COLS))


def _pack_slab(parts, extra_row):
    rows = []
    for name in SMALL:
        flat = parts[name].reshape(-1)
        nrow = _slab_rows(parts[name].shape)
        rows.append(jnp.pad(flat, (0, nrow * SLAB_COLS - flat.shape[0])).reshape(nrow, SLAB_COLS))
    rows.append(jnp.pad(extra_row, ((0, SUBLANE - 1), (0, 0))))
    return jnp.concatenate(rows, axis=0)


def _unpack_slab(slab, shapes):
    out, row = {}, 0
    for name in SMALL:
        size = 1
        for s in shapes[name]:
            size *= s
        out[name] = slab[row:row + _slab_rows(shapes[name])].reshape(-1)[:size].reshape(shapes[name])
        row += _slab_rows(shapes[name])
    return out, row


def kernel(x, p, norm1_g, w_in, conv_qkv, a_log, dt_bias, onorm_g, pool_w, pool_scale, sconv_w, w_out, norm2_g, w_gate, w_up, w_down, ple_proj, ple_gate, final_g, loss_target, m_norm1_g, m_w_in, m_conv_qkv, m_a_log, m_dt_bias, m_onorm_g, m_pool_w, m_pool_scale, m_sconv_w, m_w_out, m_norm2_g, m_w_gate, m_w_up, m_w_down, m_ple_proj, m_ple_gate, m_final_g, v_norm1_g, v_w_in, v_conv_qkv, v_a_log, v_dt_bias, v_onorm_g, v_pool_w, v_pool_scale, v_sconv_w, v_w_out, v_norm2_g, v_w_gate, v_w_up, v_w_down, v_ple_proj, v_ple_gate, v_final_g):
    names = ["norm1_g", "w_in", "conv_qkv", "a_log", "dt_bias", "onorm_g", "pool_w", "pool_scale", "sconv_w", "w_out", "norm2_g",
             "w_gate", "w_up", "w_down", "ple_proj", "ple_gate", "final_g"]
    w = dict(zip(names, [norm1_g, w_in, conv_qkv, a_log, dt_bias, onorm_g, pool_w, pool_scale, sconv_w, w_out, norm2_g, w_gate, w_up,
                         w_down, ple_proj, ple_gate, final_g]))
    m = dict(zip(names, [m_norm1_g, m_w_in, m_conv_qkv, m_a_log, m_dt_bias, m_onorm_g, m_pool_w, m_pool_scale, m_sconv_w, m_w_out,
                         m_norm2_g, m_w_gate, m_w_up, m_w_down, m_ple_proj, m_ple_gate, m_final_g]))
    v = dict(zip(names, [v_norm1_g, v_w_in, v_conv_qkv, v_a_log, v_dt_bias, v_onorm_g, v_pool_w, v_pool_scale, v_sconv_w, v_w_out,
                         v_norm2_g, v_w_gate, v_w_up, v_w_down, v_ple_proj, v_ple_gate, v_final_g]))

    gathered = dict(zip(SHARDED, _all_gather([_payload(k, w[k][0]) for k in SHARDED], name="all_gather_weights")))
    (flying,), token = _exchange_start([[_payload(k, w[k][1]) for k in SHARDED]], GATHER, name="gather_start")
    replicated = [_small_weights(w, i) for i in range(DEPTH)]
    replicated[0]["norm1_g"] = replicated[0]["norm1_g"] + token[0, 0]

    def fetch(i, group, after):
        if i == 1 and group == 0:
            gathered.update(zip(SHARDED, _exchange_wait(flying, after, GATHER, name="gather_wait")))
        return {k: _as_read(k, gathered[k]) for k in FETCH_GROUPS[group]}

    def reduce_scatter_start(members, blocks, tag):
        mine = [blocks[k] for k in members]
        theirs = _pair_exchange(mine, name="pair_exchange")
        sums = [_pair_add(a, b, name="pair_add") for a, b in zip(mine, theirs)]
        (started,), token = _exchange_start([sums], CHIP_SCATTER, name="exchange_start_" + tag)
        return started, token

    h, saved0 = _layer_fwd(x[0], p[0, 0], replicated[0], functools.partial(fetch, 0))
    h, saved1 = _layer_fwd(h, p[1, 0], replicated[1], functools.partial(fetch, 1))
    dx, dgf, loss_part = _loss_head(h, final_g[None], loss_target[0], name="loss_head")
    small, big1, flying0 = [None] * DEPTH, {}, []
    dx, small[1] = _layer_bwd(dx, saved1, lambda group, blocks: big1.update({k: blocks[k] for k in EMIT_GROUPS[group]}))
    flying1, token = reduce_scatter_start(SHARDED, big1, "1")

    def emit(group, blocks):
        started, token = reduce_scatter_start(EMIT_GROUPS[group], blocks, f"0_{group}")
        flying0.append(started)
        return token

    dx, small[0] = _layer_bwd(dx, saved0, emit, after=token)
    received = [{}, dict(zip(SHARDED, _exchange_wait(flying1, dx, CHIP_SCATTER, name="exchange_wait_1")))]
    for group, members in enumerate(EMIT_GROUPS):
        received[0].update(zip(members, _exchange_wait(flying0[group], dx, CHIP_SCATTER, name=f"exchange_wait_0_{group}")))

    grads = {k: jnp.stack([small[i][k] for i in range(DEPTH)]) for k in small[0]}
    grads = {k: g[:, 0] if k in ("norm1_g", "norm2_g", "onorm_g", "pool_scale") else g for k, g in grads.items()}
    grads["final_g"] = dgf[0]
    loss_row = jnp.pad(loss_part, ((0, 0), (0, SLAB_COLS - LANE)))
    (small_parts,) = _all_gather([_pack_slab(grads, loss_row)], name="all_gather_small_grads")

    out_g, out_d, out_m, out_v = {}, {}, {}, {}
    for k in SHARDED:
        out_g[k], out_d[k], out_m[k], out_v[k] = _adamw_reduce(w[k], [received[i][k] for i in range(DEPTH)], m[k], v[k],
                                                                name="adamw_" + k)
    zero_row = jnp.zeros((1, SLAB_COLS), F32)
    slabs = _adamw_reduce(_pack_slab(w, zero_row)[None], [small_parts], _pack_slab(m, zero_row)[None],
                          _pack_slab(v, zero_row)[None], name="adamw_small")
    slabs = [s[0] for s in slabs]
    shapes = {k: w[k].shape for k in SMALL}
    for dst, slab in zip((out_g, out_d, out_m, out_v), slabs):
        vals, _ = _unpack_slab(slab, shapes)
        dst.update(vals)
    _, loss_at = _unpack_slab(slabs[0], shapes)
    loss = slabs[0][loss_at, 0]

    return (loss, dx[None], *[out_g[k] for k in names], *[out_d[k] for k in names], *[out_m[k] for k in names],
            *[out_v[k] for k in names])
```

```python
import functools

import jax
import jax.numpy as jnp
from jax import lax
from jax.experimental import pallas as pl
from jax.experimental.pallas import tpu as pltpu

F32 = jnp.float32
BF16 = jnp.bfloat16

D_MODEL = 1024
DEPTH = 2
PLE_DIM = 256
EPS = 1e-6
HEAD_DIM = 128
HEADS = 4
A_DIM = HEADS * HEAD_DIM
QKV_TAPS = 4
CHUNK = 64
POOL_WINDOWS = (2, 4, 8, 16)
POOL_DIM = 256
CONV_DIM = 256
CONV_TAPS = 3
D_FF = 2816
D_IN = 3080
D_IN_PAD = 3200
AB_COL = 2048
N_DEV = 8

ADAM_LR = 0.001
ADAM_B1 = 0.9
ADAM_B2 = 0.999
ADAM_EPS = 1e-08
ADAM_WD = 0.01
ADAM_STEP = 10

LANE = 128
SUBLANE = 8
VMEM_BYTES_V7X = 64 * 1024 * 1024
VMEM_LIMIT = 48 * 1024 * 1024

_HI = lax.Precision.HIGHEST
NN = ((1,), (0,))
NT = ((1,), (1,))
TN = ((0,), (0,))
MESH = pl.DeviceIdType.MESH


def _dot(a, b, dims, hi=False):
    if hi:
        return lax.dot_general(a, b, (dims, ((), ())), precision=_HI, preferred_element_type=F32)
    return lax.dot_general(a.astype(BF16), b.astype(BF16), (dims, ((), ())), preferred_element_type=F32)


def _pcall(body, *, name, out_shape, grid=(), in_specs=None, out_specs=None, scratch_shapes=(), semantics=None,
           vmem_limit=None, after=None, **kw):
    params = {}
    if semantics is not None:
        params["dimension_semantics"] = semantics
    if vmem_limit is not None:
        params["vmem_limit_bytes"] = vmem_limit
    if after is not None:
        n_in, inner = len(in_specs), body
        body = lambda *refs: inner(*refs[:n_in], *refs[n_in + 1:])
        in_specs = list(in_specs) + [pl.BlockSpec(after.shape, lambda *_: (0,) * after.ndim)]
    call = pl.pallas_call(
        body, name=name, out_shape=out_shape, grid=grid, in_specs=in_specs, out_specs=out_specs,
        scratch_shapes=list(scratch_shapes), compiler_params=pltpu.CompilerParams(**params), **kw)
    return call if after is None else (lambda *args: call(*args, after))


def _sigmoid(x):
    return 1.0 / (1.0 + jnp.exp(-x))


def _softplus(x):
    return jnp.maximum(x, 0.0) + jnp.log(1.0 + jnp.exp(-jnp.abs(x)))


def _tile(n, cap, mult):
    if n <= cap:
        return n
    best = None
    for t in range(mult, cap + 1, mult):
        if n % t == 0:
            best = t
    assert best is not None, (n, cap, mult)
    return best


ROWS_PER_STEP = 512
NARROW_RESULT = 1024
COLS_PER_DOT = 640


def _resident(weight):
    return pl.BlockSpec(weight.shape, lambda i: (0,) * weight.ndim, pipeline_mode=pl.Buffered(1))


def _matmul_rows(a, b, mode, *, name, res=None, out_dtype=F32, b_blocked=False, after=None):
    m, k = a.shape
    if b_blocked:
        nb, _, bw = b.shape
        n = nb * bw if mode == "nn" else b.shape[1]
    else:
        n = b.shape[1] if mode == "nn" else b.shape[0]
    tm = _tile(m, ROWS_PER_STEP if n > NARROW_RESULT else 2 * ROWS_PER_STEP, 16)
    cn = bw if (b_blocked and mode == "nn") else _tile(n, COLS_PER_DOT, LANE)
    has_res = res is not None

    def body(*refs):
        a_ref, b_ref = refs[0], refs[1]
        res_ref = refs[2] if has_res else None
        o_ref = refs[2 + has_res]
        if not (b_blocked and mode == "nt"):
            av = a_ref[...].astype(BF16)
        for j in range(n // cn):
            cols = pl.ds(j * cn, cn)
            if mode == "nn":
                part = _dot(av, b_ref[j] if b_blocked else b_ref[:, cols], NN)
            elif not b_blocked:
                part = _dot(av, b_ref[cols, :], NT)
            else:
                part = None
                for s in range(nb):
                    term = _dot(a_ref[:, pl.ds(s * bw, bw)], b_ref[s, cols, :], NT)
                    part = term if part is None else part + term
            if has_res:
                part = part + res_ref[:, cols]
            o_ref[:, cols] = part.astype(o_ref.dtype)

    row = lambda width: pl.BlockSpec((tm, width), lambda i: (i, 0))
    whole = _resident(b)
    ins = [a, b] + ([res] if has_res else [])
    specs = [row(k), whole] + ([row(n)] if has_res else [])
    return _pcall(body, name=name, out_shape=jax.ShapeDtypeStruct((m, n), out_dtype), grid=(m // tm,), in_specs=specs,
                  out_specs=row(n), semantics=("parallel",), vmem_limit=VMEM_LIMIT, after=after)(*ins)


def _matmul(a, b, mode, *, name, res=None, out_dtype=F32, b_blocked=False, out_blocked=None, after=None):
    if mode != "tn":
        return _matmul_rows(a, b, mode, name=name, res=res, out_dtype=out_dtype, b_blocked=b_blocked, after=after)
    assert res is None and not b_blocked and after is None
    (t, m), (t2, n) = a.shape, b.shape
    assert t == t2, (a.shape, b.shape)
    tm = _tile(m, 1024, LANE)
    tn = _tile(n, COLS_PER_DOT, LANE)
    if out_blocked is not None:
        assert out_blocked[0] * out_blocked[1] == n
        tn = out_blocked[1]

    def body(a_ref, b_ref, o_ref):
        part = _dot(a_ref[...], b_ref[...], TN).astype(o_ref.dtype)
        if out_blocked is None:
            o_ref[...] = part
        else:
            o_ref[0] = part

    o_spec = (pl.BlockSpec((tm, tn), lambda i, j: (i, j)) if out_blocked is None
              else pl.BlockSpec((1, tm, tn), lambda i, j: (j, i, 0)))
    o_shape = (m, n) if out_blocked is None else (out_blocked[0], m, out_blocked[1])
    return _pcall(body, name=name, out_shape=jax.ShapeDtypeStruct(o_shape, out_dtype), grid=(m // tm, n // tn),
                  in_specs=[pl.BlockSpec((t, tm), lambda i, j: (0, i)), pl.BlockSpec((t, tn), lambda i, j: (0, j))],
                  out_specs=o_spec, semantics=("parallel", "parallel"), vmem_limit=VMEM_LIMIT)(a, b)


ROW_TILE = 256


def _rows(t, width, idx=0):
    return pl.BlockSpec((ROW_TILE, width), lambda i: (i, idx))


def _vec(width):
    return pl.BlockSpec((1, width), lambda i: (0, 0))


def _rmsnorm_fwd(x, g, *, name):
    t, d = x.shape

    def body(x_ref, g_ref, h_ref):
        xv = x_ref[...]
        r = lax.rsqrt(jnp.mean(xv * xv, axis=-1, keepdims=True) + EPS)
        h_ref[...] = (xv * r * g_ref[...]).astype(BF16)

    return _pcall(body, name=name, out_shape=jax.ShapeDtypeStruct((t, d), BF16), grid=(t // ROW_TILE,),
                  in_specs=[_rows(t, d), _vec(d)], out_specs=_rows(t, d), semantics=("parallel",))(x, g)


def _rmsnorm_bwd(x, g, dh, dres, *, name):
    t, d = x.shape

    def body(x_ref, g_ref, dh_ref, dres_ref, dx_ref, dg_ref):
        xv = x_ref[...]
        r = lax.rsqrt(jnp.mean(xv * xv, axis=-1, keepdims=True) + EPS)
        xhat = xv * r
        dhv = dh_ref[...].astype(F32)
        dhg = dhv * g_ref[...]
        dx_ref[...] = dres_ref[...] + r * (dhg - xhat * jnp.mean(dhg * xhat, axis=-1, keepdims=True))
        part = jnp.sum(dhv * xhat, axis=0, keepdims=True)

        @pl.when(pl.program_id(0) == 0)
        def _():
            dg_ref[...] = part

        @pl.when(pl.program_id(0) > 0)
        def _():
            dg_ref[...] += part

    return _pcall(body, name=name, out_shape=(jax.ShapeDtypeStruct((t, d), F32), jax.ShapeDtypeStruct((1, d), F32)),
                  grid=(t // ROW_TILE,), in_specs=[_rows(t, d), _vec(d), _rows(t, d), _rows(t, d)],
                  out_specs=(_rows(t, d), _vec(d)), semantics=("arbitrary",))(x, g, dh, dres)


def _swiglu_fwd(h, w_gate, w_up, *, name):
    t, k = h.shape
    nb, _, bw = w_gate.shape
    tm = _tile(t, ROWS_PER_STEP, 16)

    def body(h_ref, wg_ref, wu_ref, ff_ref, gate_ref, up_ref):
        hv = h_ref[...]
        for j in range(nb):
            cols = pl.ds(j * bw, bw)
            gv = _dot(hv, wg_ref[j], NN)
            uv = _dot(hv, wu_ref[j], NN)
            gate_ref[:, cols] = gv.astype(BF16)
            up_ref[:, cols] = uv.astype(BF16)
            ff_ref[:, cols] = (gv * _sigmoid(gv) * uv).astype(BF16)

    row = lambda width: pl.BlockSpec((tm, width), lambda i: (i, 0))
    out = jax.ShapeDtypeStruct((t, nb * bw), BF16)
    return _pcall(body, name=name, out_shape=(out,) * 3, grid=(t // tm,), in_specs=[row(k), _resident(w_gate), _resident(w_up)],
                  out_specs=(row(nb * bw),) * 3, semantics=("parallel",), vmem_limit=VMEM_LIMIT)(h, w_gate, w_up)


def _swiglu_bwd(dx2, w_down, gate, up, *, name, after=None):
    t, d = dx2.shape
    f = w_down.shape[0]
    tm = _tile(t, ROWS_PER_STEP, 16)
    cn = _tile(f, COLS_PER_DOT, LANE)

    def body(dx_ref, w_ref, gate_ref, up_ref, dgate_ref, dup_ref):
        dxv = dx_ref[...].astype(BF16)
        for j in range(f // cn):
            cols = pl.ds(j * cn, cn)
            dffv = _dot(dxv, w_ref[cols, :], NT)
            gv = gate_ref[:, cols].astype(F32)
            sig = _sigmoid(gv)
            dgate_ref[:, cols] = (dffv * up_ref[:, cols].astype(F32) * sig * (1.0 + gv * (1.0 - sig))).astype(BF16)
            dup_ref[:, cols] = (dffv * gv * sig).astype(BF16)

    row = lambda width: pl.BlockSpec((tm, width), lambda i: (i, 0))
    out = jax.ShapeDtypeStruct((t, f), BF16)
    return _pcall(body, name=name, out_shape=(out, out), grid=(t // tm,), in_specs=[row(d), _resident(w_down), row(f), row(f)],
                  out_specs=(row(f), row(f)), semantics=("parallel",), vmem_limit=VMEM_LIMIT, after=after)(dx2, w_down, gate, up)


def _ple_fwd(x2, pgl, pp, *, name):
    t, d = x2.shape

    def body(x_ref, pgl_ref, pp_ref, o_ref):
        o_ref[...] = x_ref[...] + _sigmoid(pgl_ref[...]) * pp_ref[...]

    return _pcall(body, name=name, out_shape=jax.ShapeDtypeStruct((t, d), F32), grid=(t // ROW_TILE,),
                  in_specs=[_rows(t, d)] * 3, out_specs=_rows(t, d), semantics=("parallel",))(x2, pgl, pp)


def _ple_bwd(dx3, pgl, pp, *, name, after=None):
    t, d = dx3.shape

    def body(dx_ref, pgl_ref, pp_ref, dpgl_ref, dpp_ref):
        dxv = dx_ref[...]
        sig = _sigmoid(pgl_ref[...])
        dpp_ref[...] = (dxv * sig).astype(BF16)
        dpgl_ref[...] = (dxv * pp_ref[...] * sig * (1.0 - sig)).astype(BF16)

    return _pcall(body, name=name, out_shape=(jax.ShapeDtypeStruct((t, d), BF16),) * 2, grid=(t // ROW_TILE,),
                  in_specs=[_rows(t, d)] * 3, out_specs=(_rows(t, d),) * 2, semantics=("parallel",), after=after)(dx3, pgl, pp)


def _loss_head(x3, g, target, *, name):
    t, d = x3.shape

    def body(x_ref, g_ref, t_ref, dx_ref, dg_ref, loss_ref):
        xv = x_ref[...]
        r = lax.rsqrt(jnp.mean(xv * xv, axis=-1, keepdims=True) + EPS)
        xhat = xv * r
        gv = g_ref[...]
        err = xhat * gv - t_ref[...]
        row_loss = jnp.sum(err * err, axis=-1, keepdims=True) * (0.5 / d)
        lpart = jnp.broadcast_to(jnp.sum(row_loss, axis=0, keepdims=True), (1, LANE))
        dy = err * (1.0 / d)
        dyg = dy * gv
        dx_ref[...] = r * (dyg - xhat * jnp.mean(dyg * xhat, axis=-1, keepdims=True))
        gpart = jnp.sum(dy * xhat, axis=0, keepdims=True)

        @pl.when(pl.program_id(0) == 0)
        def _():
            dg_ref[...] = gpart
            loss_ref[...] = lpart

        @pl.when(pl.program_id(0) > 0)
        def _():
            dg_ref[...] += gpart
            loss_ref[...] += lpart

    return _pcall(body, name=name,
                  out_shape=(jax.ShapeDtypeStruct((t, d), F32), jax.ShapeDtypeStruct((1, d), F32), jax.ShapeDtypeStruct((1, LANE), F32)),
                  grid=(t // ROW_TILE,), in_specs=[_rows(t, d), _vec(d), _rows(t, d)],
                  out_specs=(_rows(t, d), _vec(d), _vec(LANE)), semantics=("arbitrary",))(x3, g, target)


def _shift_down(x, d):
    if d == 0:
        return x
    row = lax.broadcasted_iota(jnp.int32, x.shape, 0)
    return jnp.where(row >= d, pltpu.roll(x, d, 0), 0.0)


def _shift_up(x, d):
    if d == 0:
        return x
    t = x.shape[0]
    row = lax.broadcasted_iota(jnp.int32, x.shape, 0)
    return jnp.where(row < t - d, pltpu.roll(x, t - d, 0), 0.0)


def _colsum(x):
    return jnp.sum(x, axis=0, keepdims=True)


def _col(t, idx_fn):
    return pl.BlockSpec((t, LANE), idx_fn)


def _conv_fwd(x, w_ref, taps):
    acc = None
    for j in range(taps):
        term = w_ref[pl.ds(j, 1), :] * _shift_down(x, taps - 1 - j)
        acc = term if acc is None else acc + term
    return acc


def _conv_bwd(x, dy, w_ref, dw_ref, taps):
    dx = None
    for j in range(taps):
        term = w_ref[pl.ds(j, 1), :] * _shift_up(dy, taps - 1 - j)
        dx = term if dx is None else dx + term
        dw_ref[pl.ds(j, 1), :] = _colsum(dy * _shift_down(x, taps - 1 - j))
    return dx


def _qkv_prep_fwd(proj, conv_w, *, name):
    t = proj.shape[0]
    scale = HEAD_DIM ** -0.5

    def body(x_ref, w_ref, o_ref):
        j = pl.program_id(0)
        c = _conv_fwd(x_ref[...], w_ref, QKV_TAPS)
        s = c * _sigmoid(c)
        r = lax.rsqrt(jnp.sum(s * s, axis=-1, keepdims=True) + EPS)
        f = jnp.where(j < 2 * HEADS, r, 1.0) * jnp.where(j < HEADS, scale, 1.0)
        o_ref[0] = s * f

    return _pcall(body, name=name, out_shape=jax.ShapeDtypeStruct((3 * HEADS, t, LANE), F32), grid=(3 * HEADS,),
                  in_specs=[_col(t, lambda j: (0, j)), pl.BlockSpec((QKV_TAPS, LANE), lambda j: (0, j))],
                  out_specs=pl.BlockSpec((1, t, LANE), lambda j: (j, 0, 0)), semantics=("parallel",),
                  vmem_limit=VMEM_LIMIT)(proj, conv_w)


def _qkv_prep_bwd(proj, conv_w, dqkv, *, name):
    t = proj.shape[0]
    scale = HEAD_DIM ** -0.5

    def body(x_ref, w_ref, d_ref, dx_ref, dw_ref):
        j = pl.program_id(0)
        xv = x_ref[...]
        c = _conv_fwd(xv, w_ref, QKV_TAPS)
        sig = _sigmoid(c)
        s = c * sig
        r = lax.rsqrt(jnp.sum(s * s, axis=-1, keepdims=True) + EPS)
        n0 = s * r
        dv = d_ref[0]
        dn0 = dv * jnp.where(j < HEADS, scale, 1.0)
        ds_norm = r * (dn0 - n0 * jnp.sum(dn0 * n0, axis=-1, keepdims=True))
        ds = jnp.where(j < 2 * HEADS, ds_norm, dv)
        dc = ds * sig * (1.0 + c * (1.0 - sig))
        dx_ref[...] = _conv_bwd(xv, dc, w_ref, dw_ref, QKV_TAPS).astype(BF16)

    return _pcall(body, name=name,
                  out_shape=(jax.ShapeDtypeStruct((t, 3 * A_DIM), BF16), jax.ShapeDtypeStruct((QKV_TAPS, 3 * A_DIM), F32)),
                  grid=(3 * HEADS,),
                  in_specs=[_col(t, lambda j: (0, j)), pl.BlockSpec((QKV_TAPS, LANE), lambda j: (0, j)),
                            pl.BlockSpec((1, t, LANE), lambda j: (j, 0, 0))],
                  out_specs=(_col(t, lambda j: (0, j)), pl.BlockSpec((QKV_TAPS, LANE), lambda j: (0, j))),
                  semantics=("parallel",), vmem_limit=VMEM_LIMIT)(proj, conv_w, dqkv)


def _lane_pick(x, lane_idx, lane):
    return jnp.broadcast_to(jnp.sum(jnp.where(lane == lane_idx, x, 0.0), axis=-1, keepdims=True), x.shape)


def _gates_fwd(proj, alog, dtb, *, name):
    t = proj.shape[0]

    def body(x_ref, alog_ref, dtb_ref, g_ref, b_ref):
        xv = x_ref[...]
        lane = lax.broadcasted_iota(jnp.int32, xv.shape, 1)
        gall = -jnp.exp(alog_ref[...]) * _softplus(xv + dtb_ref[...])
        ball = _sigmoid(xv)
        for h in range(HEADS):
            g_ref[h] = _lane_pick(gall, h, lane)
            b_ref[h] = _lane_pick(ball, HEADS + h, lane)

    out = jax.ShapeDtypeStruct((HEADS, t, LANE), F32)
    whole = pl.BlockSpec((HEADS, t, LANE), lambda i: (0, 0, 0))
    return _pcall(body, name=name, out_shape=(out, out), grid=(1,),
                  in_specs=[_col(t, lambda i: (0, AB_COL // LANE)), _vec(LANE), _vec(LANE)], out_specs=(whole, whole),
                  semantics=("arbitrary",), vmem_limit=VMEM_LIMIT)(proj, alog, dtb)


def _gates_bwd(proj, alog, dtb, dg, dbeta, *, name):
    t = proj.shape[0]

    def body(x_ref, alog_ref, dtb_ref, dg_ref, db_ref, dab_ref, dalog_ref, ddtb_ref):
        xv = x_ref[...]
        lane = lax.broadcasted_iota(jnp.int32, xv.shape, 1)
        lane1 = lax.broadcasted_iota(jnp.int32, (1, LANE), 1)
        z = xv + dtb_ref[...]
        nea = -jnp.exp(alog_ref[...])
        da_f = nea * _sigmoid(z)
        g_f = nea * _softplus(z)
        ball = _sigmoid(xv)
        db_f = ball * (1.0 - ball)
        dab = jnp.zeros_like(xv)
        dalog = jnp.zeros((1, LANE), F32)
        for h in range(HEADS):
            dgh = dg_ref[h]
            dab = dab + jnp.where(lane == h, dgh * da_f, 0.0) + jnp.where(lane == HEADS + h, db_ref[h] * db_f, 0.0)
            dalog = dalog + jnp.where(lane1 == h, _colsum(dgh * g_f), 0.0)
        dab_ref[...] = dab.astype(BF16)
        dalog_ref[...] = dalog
        ddtb_ref[...] = jnp.where(lane1 < HEADS, _colsum(dab), 0.0)

    whole = pl.BlockSpec((HEADS, t, LANE), lambda i: (0, 0, 0))
    vec = jax.ShapeDtypeStruct((1, LANE), F32)
    return _pcall(body, name=name, out_shape=(jax.ShapeDtypeStruct((t, LANE), BF16), vec, vec), grid=(1,),
                  in_specs=[_col(t, lambda i: (0, AB_COL // LANE)), _vec(LANE), _vec(LANE), whole, whole],
                  out_specs=(_col(t, lambda i: (0, 0)), _vec(LANE), _vec(LANE)), semantics=("arbitrary",),
                  vmem_limit=VMEM_LIMIT)(proj, alog, dtb, dg, dbeta)


Z_COL = 3 * A_DIM // LANE


def _apost_fwd(o, proj, gn, *, name):
    t = proj.shape[0]

    def body(o_ref, z_ref, gn_ref, y_ref):
        ov = o_ref[0]
        z = z_ref[...]
        r = lax.rsqrt(jnp.mean(ov * ov, axis=-1, keepdims=True) + EPS)
        y_ref[...] = (ov * r * gn_ref[...] * (z * _sigmoid(z))).astype(BF16)

    return _pcall(body, name=name, out_shape=jax.ShapeDtypeStruct((t, A_DIM), BF16), grid=(HEADS,),
                  in_specs=[pl.BlockSpec((1, t, LANE), lambda h: (h, 0, 0)), _col(t, lambda h: (0, Z_COL + h)),
                            pl.BlockSpec((1, LANE), lambda h: (0, 0))],
                  out_specs=_col(t, lambda h: (0, h)), semantics=("parallel",), vmem_limit=VMEM_LIMIT)(o, proj, gn)


def _apost_bwd(o, proj, gn, dmixed, *, name):
    t = proj.shape[0]

    def body(o_ref, z_ref, gn_ref, d_ref, do_ref, dz_ref, dgn_ref):
        ov = o_ref[0]
        z = z_ref[...]
        gnv = gn_ref[...]
        dv = d_ref[...]
        r = lax.rsqrt(jnp.mean(ov * ov, axis=-1, keepdims=True) + EPS)
        ohat = ov * r
        sig = _sigmoid(z)
        dy = dv * (z * sig)
        dz_ref[...] = (dv * ohat * gnv * sig * (1.0 + z * (1.0 - sig))).astype(BF16)
        dyo = dy * gnv
        do_ref[0] = r * (dyo - ohat * jnp.mean(dyo * ohat, axis=-1, keepdims=True))
        part = _colsum(dy * ohat)

        @pl.when(pl.program_id(0) == 0)
        def _():
            dgn_ref[...] = part

        @pl.when(pl.program_id(0) > 0)
        def _():
            dgn_ref[...] += part

    return _pcall(body, name=name,
                  out_shape=(jax.ShapeDtypeStruct((HEADS, t, LANE), F32), jax.ShapeDtypeStruct((t, A_DIM), BF16),
                             jax.ShapeDtypeStruct((1, LANE), F32)),
                  grid=(HEADS,),
                  in_specs=[pl.BlockSpec((1, t, LANE), lambda h: (h, 0, 0)), _col(t, lambda h: (0, Z_COL + h)),
                            pl.BlockSpec((1, LANE), lambda h: (0, 0)), _col(t, lambda h: (0, h))],
                  out_specs=(pl.BlockSpec((1, t, LANE), lambda h: (h, 0, 0)), _col(t, lambda h: (0, h)),
                             pl.BlockSpec((1, LANE), lambda h: (0, 0))),
                  semantics=("arbitrary",), vmem_limit=VMEM_LIMIT)(o, proj, gn, dmixed)


POOL_COL = (AB_COL + LANE) // LANE
CB_COL = POOL_COL + POOL_DIM // LANE
CC_COL = CB_COL + CONV_DIM // LANE
CH_COL = CC_COL + CONV_DIM // LANE
MAX_WIN_LOG2 = 4


def _window_sums(x, shift):
    sums = []
    cur = x
    for k in range(MAX_WIN_LOG2):
        cur = cur + shift(cur, 1 << k)
        sums.append(cur)
    return sums


def _pick_window(sums, win):
    out = sums[-1]
    for k in range(MAX_WIN_LOG2 - 2, -1, -1):
        out = jnp.where(win == float(2 << k), sums[k], out)
    return out


def _pool_counts(shape, win):
    row = lax.broadcasted_iota(jnp.int32, shape, 0).astype(F32)
    return jnp.minimum(row + 1.0, win)


def _pool_fwd(proj, win, wbd, scale, *, name):
    t = proj.shape[0]

    def body(x_ref, win_ref, w_ref, s_ref, y_ref):
        xv = x_ref[...]
        winv = win_ref[...]
        pooled = _pick_window(_window_sums(xv, _shift_down), winv) / _pool_counts(xv.shape, winv) - xv
        y_ref[...] = (_dot(pooled, w_ref[0], NN) * s_ref[...]).astype(BF16)

    nb = POOL_DIM // LANE
    vec = pl.BlockSpec((1, LANE), lambda b: (0, b))
    return _pcall(body, name=name, out_shape=jax.ShapeDtypeStruct((t, POOL_DIM), BF16), grid=(nb,),
                  in_specs=[_col(t, lambda b: (0, POOL_COL + b)), vec, pl.BlockSpec((1, LANE, LANE), lambda b: (b, 0, 0)), vec],
                  out_specs=_col(t, lambda b: (0, b)), semantics=("parallel",), vmem_limit=VMEM_LIMIT)(proj, win, wbd, scale)


def _pool_bwd(proj, win, wbd, scale, dmixed, *, name):
    t = proj.shape[0]

    def body(x_ref, win_ref, w_ref, s_ref, d_ref, dx_ref, dw_ref, ds_ref):
        xv = x_ref[...]
        winv = win_ref[...]
        cnt = _pool_counts(xv.shape, winv)
        pooled = _pick_window(_window_sums(xv, _shift_down), winv) / cnt - xv
        dv = d_ref[...]
        ds_ref[...] = _colsum(dv * _dot(pooled, w_ref[0], NN))
        dy0 = dv * s_ref[...]
        dw_ref[0] = _dot(pooled, dy0, TN)
        dpooled = _dot(dy0, w_ref[0], NT)
        dmean = dpooled / cnt
        dx_ref[...] = (_pick_window(_window_sums(dmean, _shift_up), winv) - dpooled).astype(BF16)

    nb = POOL_DIM // LANE
    vec = pl.BlockSpec((1, LANE), lambda b: (0, b))
    mat = pl.BlockSpec((1, LANE, LANE), lambda b: (b, 0, 0))
    first = A_DIM // LANE
    return _pcall(body, name=name,
                  out_shape=(jax.ShapeDtypeStruct((t, POOL_DIM), BF16), jax.ShapeDtypeStruct((nb, LANE, LANE), F32),
                             jax.ShapeDtypeStruct((1, POOL_DIM), F32)),
                  grid=(nb,),
                  in_specs=[_col(t, lambda b: (0, POOL_COL + b)), vec, mat, vec, _col(t, lambda b: (0, first + b))],
                  out_specs=(_col(t, lambda b: (0, b)), mat, vec), semantics=("parallel",),
                  vmem_limit=VMEM_LIMIT)(proj, win, wbd, scale, dmixed)


def _sconv_fwd(proj, w, *, name):
    t = proj.shape[0]

    def body(cb_ref, cc_ref, ch_ref, w_ref, y_ref):
        y_ref[...] = (cb_ref[...] * _conv_fwd(cc_ref[...] * ch_ref[...], w_ref, CONV_TAPS)).astype(BF16)

    nb = CONV_DIM // LANE
    return _pcall(body, name=name, out_shape=jax.ShapeDtypeStruct((t, CONV_DIM), BF16), grid=(nb,),
                  in_specs=[_col(t, lambda b: (0, CB_COL + b)), _col(t, lambda b: (0, CC_COL + b)),
                            _col(t, lambda b: (0, CH_COL + b)), pl.BlockSpec((CONV_TAPS, LANE), lambda b: (0, b))],
                  out_specs=_col(t, lambda b: (0, b)), semantics=("parallel",), vmem_limit=VMEM_LIMIT)(proj, proj, proj, w)


def _sconv_bwd(proj, w, dmixed, *, name):
    t = proj.shape[0]

    def body(cb_ref, cc_ref, ch_ref, w_ref, d_ref, dcb_ref, dcc_ref, dch_ref, dw_ref):
        cc = cc_ref[...]
        ch = ch_ref[...]
        u = cc * ch
        dv = d_ref[...]
        dcb_ref[...] = (dv * _conv_fwd(u, w_ref, CONV_TAPS)).astype(BF16)
        du = _conv_bwd(u, dv * cb_ref[...], w_ref, dw_ref, CONV_TAPS)
        dcc_ref[...] = (du * ch).astype(BF16)
        dch_ref[...] = (du * cc).astype(BF16)

    nb = CONV_DIM // LANE
    first = (A_DIM + POOL_DIM) // LANE
    act = jax.ShapeDtypeStruct((t, CONV_DIM), BF16)
    wspec = pl.BlockSpec((CONV_TAPS, LANE), lambda b: (0, b))
    ospec = _col(t, lambda b: (0, b))
    return _pcall(body, name=name, out_shape=(act, act, act, jax.ShapeDtypeStruct((CONV_TAPS, CONV_DIM), F32)), grid=(nb,),
                  in_specs=[_col(t, lambda b: (0, CB_COL + b)), _col(t, lambda b: (0, CC_COL + b)),
                            _col(t, lambda b: (0, CH_COL + b)), wspec, _col(t, lambda b: (0, first + b))],
                  out_specs=(ospec, ospec, ospec, wspec), semantics=("parallel",),
                  vmem_limit=VMEM_LIMIT)(proj, proj, proj, w, dmixed)


def _chunk_masks():
    r = lax.broadcasted_iota(jnp.int32, (CHUNK, CHUNK), 0)
    c = lax.broadcasted_iota(jnp.int32, (CHUNK, CHUNK), 1)
    return r >= c, r > c, jnp.where(r == c, 1.0, 0.0).astype(F32)


def _split(a):
    hi = a.astype(BF16)
    return hi, (a - hi.astype(F32)).astype(BF16)


def _dot_split(a, b, dims):
    (ah, al), (bh, bl) = a, b
    return _dot(ah, bh, dims) + _dot(ah, bl, dims) + _dot(al, bh, dims)


def _tri_inv(lows, eye):
    xs = [eye - low for low in lows]
    ps = [_split(low) for low in lows]
    ps = [_split(_dot_split(p, p, NN)) for p in ps]
    for i in range(5):
        xs = [x + _dot_split(_split(x), p, NN) for x, p in zip(xs, ps)]
        if i < 4:
            ps = [_split(_dot_split(p, p, NN)) for p in ps]
    return xs


def _prefix_sum_rows(x):
    for k in range(6):
        x = x + _shift_down(x, 1 << k)
    return x


def _suffix_sum_rows(x):
    for k in range(6):
        x = x + _shift_up(x, 1 << k)
    return x


def _chunk_decay(g, incl):
    gcb = _prefix_sum_rows(g)
    gtot = _colsum(g)
    col = gcb[:, :CHUNK]
    row = gcb.T[:CHUNK, :]
    decay = jnp.exp(jnp.where(incl, col - row, -1e30))
    return gcb, gtot, decay


CHUNKS_PER_STEP = 2


def _heads_of(ref, base, rows):
    return [ref[base + h, rows, :] for h in range(HEADS)]


def _chunk_rows(j):
    return pl.ds(j * CHUNK, CHUNK)


def _deltanet_prep(qkv, g, beta, *, name):
    t = qkv.shape[1]
    n_chunks = t // CHUNK
    per = CHUNKS_PER_STEP
    probs = [(j, h) for j in range(per) for h in range(HEADS)]

    def body(qkv_ref, g_ref, b_ref, u_ref, w_ref, qg_ref, kg_ref, attn_ref, tm_ref):
        incl, strict, eye = _chunk_masks()
        q = [qkv_ref[h, _chunk_rows(j), :] for j, h in probs]
        k = [qkv_ref[HEADS + h, _chunk_rows(j), :] for j, h in probs]
        v = [qkv_ref[2 * HEADS + h, _chunk_rows(j), :] for j, h in probs]
        bv = [b_ref[h, _chunk_rows(j), :] for j, h in probs]
        dec = [_chunk_decay(g_ref[h, _chunk_rows(j), :], incl) for j, h in probs]
        kb = [a * b for a, b in zip(k, bv)]
        low = [jnp.where(strict, _dot(a, b, NT) * d[2], 0.0) for a, b, d in zip(kb, k, dec)]
        tm = _tri_inv(low, eye)
        egc = [jnp.exp(d[0]) for d in dec]
        u = [_dot(m, a * b, NN) for m, a, b in zip(tm, v, bv)]
        w = [_dot(m, a * e, NN) for m, a, e in zip(tm, kb, egc)]
        attn = [_dot(a, b, NT) * d[2] for a, b, d in zip(q, k, dec)]
        for i, (j, h) in enumerate(probs):
            rows = _chunk_rows(j)
            u_ref[h, rows, :] = u[i]
            w_ref[h, rows, :] = w[i].astype(BF16)
            qg_ref[h, rows, :] = (q[i] * egc[i]).astype(BF16)
            kg_ref[h, rows, :] = (k[i] * jnp.exp(dec[i][1] - dec[i][0])).astype(BF16)
            attn_ref[j, h] = attn[i].astype(BF16)
            tm_ref[j, h] = tm[i]

    act = lambda heads: pl.BlockSpec((heads, per * CHUNK, LANE), lambda n: (0, n, 0))
    mat = pl.BlockSpec((per, HEADS, CHUNK, CHUNK), lambda n: (n, 0, 0, 0))
    return _pcall(
        body, name=name,
        out_shape=(jax.ShapeDtypeStruct((HEADS, t, LANE), F32),) + (jax.ShapeDtypeStruct((HEADS, t, LANE), BF16),) * 3
        + (jax.ShapeDtypeStruct((n_chunks, HEADS, CHUNK, CHUNK), BF16), jax.ShapeDtypeStruct((n_chunks, HEADS, CHUNK, CHUNK), F32)),
        grid=(n_chunks // per,), in_specs=[act(3 * HEADS), act(HEADS), act(HEADS)],
        out_specs=(act(HEADS),) * 4 + (mat, mat), semantics=("parallel",), vmem_limit=VMEM_LIMIT)(qkv, g, beta)


SCAN_CHUNKS_PER_STEP = 4


def _deltanet_scan(u, w, qg, kg, attn, g, *, name):
    t = u.shape[1]
    n_chunks = t // CHUNK
    per = SCAN_CHUNKS_PER_STEP

    def body(u_ref, w_ref, qg_ref, kg_ref, attn_ref, g_ref, o_ref, vn_ref, st_ref, s_ref):
        @pl.when(pl.program_id(0) == 0)
        def _():
            s_ref[...] = jnp.zeros_like(s_ref)

        for j in range(per):
            rows = _chunk_rows(j)
            s = [s_ref[h] for h in range(HEADS)]
            vn = [u_ref[h, rows, :] - _dot(w_ref[h, rows, :], s[h], NN) for h in range(HEADS)]
            o = [_dot(qg_ref[h, rows, :], s[h], NN) + _dot(attn_ref[j, h], vn[h], NN) for h in range(HEADS)]
            eg = [jnp.exp(_colsum(g_ref[h, rows, :])) for h in range(HEADS)]
            for h in range(HEADS):
                st_ref[j, h] = s[h]
                s_ref[h] = s[h] * eg[h] + _dot(kg_ref[h, rows, :], vn[h], TN)
                o_ref[h, rows, :] = o[h]
                vn_ref[h, rows, :] = vn[h]

    act = pl.BlockSpec((HEADS, per * CHUNK, LANE), lambda n: (0, n, 0))
    out = jax.ShapeDtypeStruct((HEADS, t, LANE), F32)
    return _pcall(
        body, name=name, out_shape=(out, out, jax.ShapeDtypeStruct((n_chunks, HEADS, LANE, LANE), F32)), grid=(n_chunks // per,),
        in_specs=[act] * 4 + [pl.BlockSpec((per, HEADS, CHUNK, CHUNK), lambda n: (n, 0, 0, 0)), act],
        out_specs=(act, act, pl.BlockSpec((per, HEADS, LANE, LANE), lambda n: (n, 0, 0, 0))),
        scratch_shapes=[pltpu.VMEM((HEADS, LANE, LANE), F32)], semantics=("arbitrary",))(u, w, qg, kg, attn, g)


def _deltanet_bscan(w, qg, kg, attn, g, do, *, name):
    t = w.shape[1]
    n_chunks = t // CHUNK
    per = SCAN_CHUNKS_PER_STEP
    steps = n_chunks // per

    def body(w_ref, qg_ref, kg_ref, attn_ref, g_ref, do_ref, dvn_ref, dsn_ref, ds_ref):
        @pl.when(pl.program_id(0) == 0)
        def _():
            ds_ref[...] = jnp.zeros_like(ds_ref)

        for j in reversed(range(per)):
            rows = _chunk_rows(j)
            dsn = [ds_ref[h] for h in range(HEADS)]
            dov = [do_ref[h, rows, :] for h in range(HEADS)]
            dvn = [_dot(attn_ref[j, h], dov[h], TN) + _dot(kg_ref[h, rows, :], dsn[h], NN) for h in range(HEADS)]
            eg = [jnp.exp(_colsum(g_ref[h, rows, :])) for h in range(HEADS)]
            for h in range(HEADS):
                dsn_ref[j, h] = dsn[h]
                ds_ref[h] = _dot(qg_ref[h, rows, :], dov[h], TN) + eg[h] * dsn[h] - _dot(w_ref[h, rows, :], dvn[h], TN)
                dvn_ref[h, rows, :] = dvn[h]

    act = pl.BlockSpec((HEADS, per * CHUNK, LANE), lambda n: (0, steps - 1 - n, 0))
    return _pcall(
        body, name=name,
        out_shape=(jax.ShapeDtypeStruct((HEADS, t, LANE), F32), jax.ShapeDtypeStruct((n_chunks, HEADS, LANE, LANE), F32)),
        grid=(steps,),
        in_specs=[act] * 3 + [pl.BlockSpec((per, HEADS, CHUNK, CHUNK), lambda n: (steps - 1 - n, 0, 0, 0)), act, act],
        out_specs=(act, pl.BlockSpec((per, HEADS, LANE, LANE), lambda n: (steps - 1 - n, 0, 0, 0))),
        scratch_shapes=[pltpu.VMEM((HEADS, LANE, LANE), F32)], semantics=("arbitrary",))(w, qg, kg, attn, g, do)


def _sum_all(x):
    return jnp.sum(jnp.sum(x, axis=1, keepdims=True), axis=0, keepdims=True)


def _rowsum(x):
    return jnp.sum(x, axis=1, keepdims=True)


def _deltanet_post(qkv, g, beta, tmats, states, dstates, do, dvn, vn, *, name):
    t = qkv.shape[1]
    n_chunks = t // CHUNK
    per = CHUNKS_PER_STEP
    probs = [(j, h) for j in range(per) for h in range(HEADS)]

    def body(qkv_ref, g_ref, b_ref, tm_ref, st_ref, dsn_ref, do_ref, dvn_ref, vn_ref, dqkv_ref, dg_ref, db_ref):
        incl, strict, _ = _chunk_masks()
        ones = jnp.ones((CHUNK, LANE), BF16)
        last_row = lax.broadcasted_iota(jnp.int32, (CHUNK, LANE), 0) == CHUNK - 1
        z = lambda f, *cols: [f(*a) for a in zip(*cols)]
        q = [qkv_ref[h, _chunk_rows(j), :] for j, h in probs]
        k = [qkv_ref[HEADS + h, _chunk_rows(j), :] for j, h in probs]
        v = [qkv_ref[2 * HEADS + h, _chunk_rows(j), :] for j, h in probs]
        bv = [b_ref[h, _chunk_rows(j), :] for j, h in probs]
        dov = [do_ref[h, _chunk_rows(j), :] for j, h in probs]
        dvn_ = [dvn_ref[h, _chunk_rows(j), :] for j, h in probs]
        vn_ = [vn_ref[h, _chunk_rows(j), :] for j, h in probs]
        tm = [tm_ref[j, h] for j, h in probs]
        s = [st_ref[j, h] for j, h in probs]
        dsn = [dsn_ref[j, h] for j, h in probs]
        dec = [_chunk_decay(g_ref[h, _chunk_rows(j), :], incl) for j, h in probs]
        decay = [d[2] for d in dec]
        egc = [jnp.exp(d[0]) for d in dec]
        ekg = [jnp.exp(d[1] - d[0]) for d in dec]
        kb = z(lambda a, b: a * b, k, bv)
        vb = z(lambda a, b: a * b, v, bv)
        kbg = z(lambda a, b: a * b, kb, egc)
        qg = z(lambda a, b: a * b, q, egc)
        kg = z(lambda a, b: a * b, k, ekg)
        kk = z(lambda a, b: _dot(a, b, NT), kb, k)
        qk = z(lambda a, b: _dot(a, b, NT), q, k)
        dattn = z(lambda a, b: jnp.where(incl, _dot(a, b, NT), 0.0), dov, vn_)
        dqg = z(lambda a, b: _dot(a, b, NT), dov, s)
        dkg = z(lambda a, b: _dot(a, b, NT), vn_, dsn)
        dglast = z(lambda a, b, c, d, e: _sum_all(a * b) * jnp.exp(e[1]) + _sum_all(c * d), s, dsn, dkg, kg, dec)
        dw = z(lambda a, b: -_dot(a, b, NT), dvn_, s)
        dtm = z(lambda a, b, c, d: _dot(a, b, NT) + _dot(c, d, NT), dvn_, vb, dw, kbg)
        dvb = z(lambda a, b: _dot(a, b, TN), tm, dvn_)
        dkbg = z(lambda a, b: _dot(a, b, TN), tm, dw)
        dlow = z(lambda a, b: jnp.where(strict, -_dot(_dot(a, b, TN), a, NT), 0.0), tm, dtm)
        dkk = z(lambda a, b: a * b, dlow, decay)
        dqk = z(lambda a, b: a * b, dattn, decay)
        dkb = z(lambda a, b, c, d: _dot(a, b, NN) + c * d, dkk, k, dkbg, egc)
        dk = z(lambda a, b, c, d, e, f, g_, h_: _dot(a, b, TN) + _dot(c, d, TN) + e * f + g_ * h_, dkk, kb, dqk, q, dkg, ekg, dkb, bv)
        dq = z(lambda a, b, c, d: _dot(a, b, NN) + c * d, dqk, k, dqg, egc)
        m = z(lambda a, b, c, d, e: (a * b + c * d) * e, dlow, kk, dattn, qk, decay)
        mcol = [_dot(mh, ones, TN) + _dot(ml, ones, TN) for mh, ml in (_split(a) for a in m)]
        for i, (j, h) in enumerate(probs):
            rows = _chunk_rows(j)
            dqkv_ref[h, rows, :] = dq[i]
            dqkv_ref[HEADS + h, rows, :] = dk[i]
            dqkv_ref[2 * HEADS + h, rows, :] = dvb[i] * bv[i]
            db_ref[h, rows, :] = jnp.broadcast_to(_rowsum(dkb[i] * k[i] + dvb[i] * v[i]), (CHUNK, LANE))
            dgc = (_rowsum(dqg[i] * qg[i] + dkbg[i] * kbg[i] - dkg[i] * kg[i]) + _rowsum(m[i]) - mcol[i]
                   + jnp.where(last_row, dglast[i], 0.0))
            dg_ref[h, rows, :] = _suffix_sum_rows(dgc)

    act = lambda heads: pl.BlockSpec((heads, per * CHUNK, LANE), lambda n: (0, n, 0))
    mat = lambda d: pl.BlockSpec((per, HEADS, d, d), lambda n: (n, 0, 0, 0))
    out = jax.ShapeDtypeStruct((HEADS, t, LANE), F32)
    return _pcall(
        body, name=name, out_shape=(jax.ShapeDtypeStruct((3 * HEADS, t, LANE), F32), out, out), grid=(n_chunks // per,),
        in_specs=[act(3 * HEADS), act(HEADS), act(HEADS), mat(CHUNK), mat(LANE), mat(LANE), act(HEADS), act(HEADS), act(HEADS)],
        out_specs=(act(3 * HEADS), act(HEADS), act(HEADS)), semantics=("parallel",),
        vmem_limit=VMEM_LIMIT)(qkv, g, beta, tmats, states, dstates, do, dvn, vn)


ANY = pl.BlockSpec(memory_space=pl.ANY)
PEERS = N_DEV - 1


def _all_gather(arrays, *, name):
    n = len(arrays)

    def body(*refs):
        ins, outs = refs[:n], refs[n:2 * n]
        send_sems, recv_sems, local_sems = refs[2 * n:]
        x, y, c = lax.axis_index("x"), lax.axis_index("y"), lax.axis_index("c")
        me, sibling = (x, y, c), (x, y, 1 - c)
        chips = [(1 - x, y), (x, 1 - y), (1 - x, 1 - y)]

        def copy(a, k, block, to, src=None):
            dst = outs[a].at[4 * block[0] + 2 * block[1] + block[2]]
            return pltpu.make_async_remote_copy(src_ref=dst if src is None else src, dst_ref=dst, send_sem=send_sems.at[a * PEERS + k],
                                                recv_sem=recv_sems.at[a * PEERS + k], device_id=to, device_id_type=MESH)

        local = [pltpu.make_async_copy(ins[a], outs[a].at[4 * x + 2 * y + c], local_sems.at[a]) for a in range(n)]
        for cp in local:
            cp.start()
        first = []
        for a in range(n):
            first.append(copy(a, 0, me, sibling, src=ins[a]))
            first += [copy(a, 1 + j, me, (*chip, c), src=ins[a]) for j, chip in enumerate(chips)]
        for cp in first:
            cp.start()
        passed = []
        for a in range(n):
            for j, chip in enumerate(chips):
                copy(a, 1 + j, (*chip, c), me).wait_recv()
                fwd = copy(a, 4 + j, (*chip, c), sibling)
                fwd.start()
                passed.append(fwd)
        for a in range(n):
            copy(a, 0, sibling, me).wait_recv()
            for j, chip in enumerate(chips):
                copy(a, 4 + j, (*chip, 1 - c), me).wait_recv()
        for cp in first + passed:
            cp.wait_send()
        for cp in local:
            cp.wait()

    return _pcall(body, name=name, out_shape=tuple(jax.ShapeDtypeStruct((N_DEV,) + a.shape, a.dtype) for a in arrays),
                  in_specs=[ANY] * n, out_specs=(ANY,) * n,
                  scratch_shapes=[pltpu.SemaphoreType.DMA((n * PEERS,)), pltpu.SemaphoreType.DMA((n * PEERS,)),
                                  pltpu.SemaphoreType.DMA((n,))])(*arrays)


CHIPS = 4


def _pair_exchange(arrays, *, name):
    n = len(arrays)

    def body(*refs):
        ins, outs = refs[:n], refs[n:2 * n]
        send_sems, recv_sems = refs[2 * n:]
        x, y, c = lax.axis_index("x"), lax.axis_index("y"), lax.axis_index("c")
        copies = []
        for a in range(n):
            for q in range(CHIPS):
                cp = pltpu.make_async_remote_copy(src_ref=ins[a].at[2 * q + 1 - c], dst_ref=outs[a].at[q],
                                                  send_sem=send_sems.at[a * CHIPS + q], recv_sem=recv_sems.at[a * CHIPS + q],
                                                  device_id=(x, y, 1 - c), device_id_type=MESH)
                cp.start()
                copies.append(cp)
        for cp in copies:
            cp.wait()

    return _pcall(body, name=name, out_shape=tuple(jax.ShapeDtypeStruct((CHIPS,) + a.shape[1:], a.dtype) for a in arrays),
                  in_specs=[ANY] * n, out_specs=(ANY,) * n,
                  scratch_shapes=[pltpu.SemaphoreType.DMA((n * CHIPS,)), pltpu.SemaphoreType.DMA((n * CHIPS,))])(*arrays)


def _pair_add(blocks, theirs, *, name):
    _, r, c_ = blocks.shape
    tr = _tile(r, 512, 16)

    def body(mine_ref, theirs_ref, o_ref):
        core = lax.axis_index("c")
        own = jnp.where(core == 0, mine_ref[0, 0].astype(F32), mine_ref[0, 1].astype(F32))
        o_ref[0] = (own + theirs_ref[0].astype(F32)).astype(o_ref.dtype)

    spec = pl.BlockSpec((1, tr, c_), lambda q, i: (q, i, 0))
    return _pcall(body, name=name, out_shape=jax.ShapeDtypeStruct(theirs.shape, theirs.dtype), grid=(CHIPS, r // tr),
                  in_specs=[pl.BlockSpec((1, 2, tr, c_), lambda q, i: (q, 0, i, 0)), spec], out_specs=spec,
                  semantics=("parallel", "parallel"), vmem_limit=VMEM_LIMIT)(blocks.reshape(CHIPS, 2, r, c_), theirs)


HBM = pl.BlockSpec(memory_space=pltpu.HBM)
SEM = pl.BlockSpec(memory_space=pltpu.SEMAPHORE)
EFFECT = pltpu.SideEffectType.DATAFLOW_SIDE_EFFECTING


CHIP_GATHER, CHIP_SCATTER = "chip_gather", "chip_scatter"
CHIP_PEERS = CHIPS - 1


def _direct_copies(srcs, lands, send_sems, recv_sems, local_sems, kind):
    x, y, c = lax.axis_index("x"), lax.axis_index("y"), lax.axis_index("c")
    mine = 2 * x + y
    copies = []
    for a, (src, land) in enumerate(zip(srcs, lands)):
        copies.append(pltpu.make_async_copy(src if kind == CHIP_GATHER else src.at[mine], land.at[mine], local_sems.at[a]))
        for k in range(1, CHIPS):
            px = 1 - x if k & 2 else x
            py = 1 - y if k & 1 else y
            copies.append(pltpu.make_async_remote_copy(
                src_ref=src if kind == CHIP_GATHER else src.at[2 * px + py], dst_ref=land.at[mine],
                send_sem=send_sems.at[a * CHIP_PEERS + k - 1], recv_sem=recv_sems.at[a * CHIP_PEERS + k - 1],
                device_id=(px, py, c), device_id_type=MESH))
    return copies


def _pair_swap(arrays, *, name):
    n = len(arrays)

    def body(*refs):
        ins, outs = refs[:n], refs[n:2 * n]
        send_sems, recv_sems, local_sems = refs[2 * n:]
        x, y, c = lax.axis_index("x"), lax.axis_index("y"), lax.axis_index("c")
        copies = []
        for a in range(n):
            for q in range(CHIPS):
                at = a * CHIPS + q
                copies.append(pltpu.make_async_copy(ins[a].at[q], outs[a].at[2 * q + c], local_sems.at[at]))
                copies.append(pltpu.make_async_remote_copy(src_ref=ins[a].at[q], dst_ref=outs[a].at[2 * q + c],
                                                           send_sem=send_sems.at[at], recv_sem=recv_sems.at[at],
                                                           device_id=(x, y, 1 - c), device_id_type=MESH))
        for cp in copies:
            cp.start()
        for cp in copies:
            cp.wait()

    return _pcall(body, name=name, out_shape=tuple(jax.ShapeDtypeStruct((N_DEV,) + a.shape[1:], a.dtype) for a in arrays),
                  in_specs=[ANY] * n, out_specs=(ANY,) * n,
                  scratch_shapes=[pltpu.SemaphoreType.DMA((n * CHIPS,))] * 3)(*arrays)


def _exchange_start(groups, kind, *, name, after=None):
    srcs = [s for group in groups for s in group]
    n = len(srcs)
    sizes = [len(group) for group in groups]
    starts = [sum(sizes[:g]) for g in range(len(groups))]
    land_shapes = [(CHIPS,) + s.shape if kind == CHIP_GATHER else s.shape for s in srcs]
    peers = CHIP_PEERS
    extra = [] if after is None else [after]

    def body(*refs):
        srcs_, lands = refs[:n], refs[n:2 * n]
        token = refs[-1]
        sem_refs = refs[2 * n + len(extra):]
        for g, (at, size) in enumerate(zip(starts, sizes)):
            send_sems, recv_sems, local_sems = sem_refs[3 * g:3 * g + 3]
            for cp in _direct_copies(srcs_[at:at + size], lands[at:at + size], send_sems, recv_sems, local_sems, kind):
                cp.start()
        token[...] = jnp.zeros_like(token)

    sems = tuple(t for size in sizes for t in (pltpu.SemaphoreType.DMA((size * peers,)), pltpu.SemaphoreType.DMA((size * peers,)),
                                               pltpu.SemaphoreType.DMA((size,))))
    thru = tuple(pltpu.HBM(s.shape, s.dtype) for s in srcs) + tuple(pltpu.HBM(shp, s.dtype) for shp, s in zip(land_shapes, srcs))
    ins = [pltpu.with_memory_space_constraint(s, pltpu.HBM) for s in srcs]
    ins += [pltpu.with_memory_space_constraint(lax.empty(shp, s.dtype), pltpu.HBM) for shp, s in zip(land_shapes, srcs)]
    out = pl.pallas_call(
        body, name=name, out_shape=sems + thru + (jax.ShapeDtypeStruct((SUBLANE, LANE), F32),),
        in_specs=[HBM] * (2 * n) + [ANY] * len(extra),
        out_specs=(SEM,) * len(sems) + (HBM,) * (2 * n) + (pl.BlockSpec(memory_space=pltpu.VMEM),),
        input_output_aliases={i: len(sems) + i for i in range(2 * n)},
        compiler_params=pltpu.CompilerParams(has_side_effects=EFFECT))(*ins, *extra)
    arrays = out[len(sems):-1]
    started = [tuple(out[3 * g:3 * g + 3]) + tuple(arrays[at:at + size]) + tuple(arrays[n + at:n + at + size])
               for g, (at, size) in enumerate(zip(starts, sizes))]
    return started, out[-1]


def _exchange_wait(started, after, kind, *, name):
    n = (len(started) - 3) // 2
    sems, arrays = started[:3], started[3:]

    def body(*refs):
        srcs_, lands = refs[:n], refs[n:2 * n]
        send_sems, recv_sems, local_sems = refs[2 * n:2 * n + 3]
        for cp in _direct_copies(srcs_, lands, send_sems, recv_sems, local_sems, kind):
            cp.wait()

    out = pl.pallas_call(
        body, name=name, out_shape=tuple(pltpu.HBM(a.shape, a.dtype) for a in arrays),
        in_specs=[HBM] * (2 * n) + [SEM] * 3 + [ANY], out_specs=(HBM,) * (2 * n),
        input_output_aliases={i: i for i in range(2 * n)},
        compiler_params=pltpu.CompilerParams(has_side_effects=EFFECT))(*arrays, *sems, after)
    return out[n:]


def _adamw_reduce(w, parts, m, v, *, name):
    layers, r, c = w.shape
    assert len(parts) == layers
    senders = parts[0].shape[0]
    tr = _tile(r, 512, 16)
    tiles = r // tr
    bc1 = 1.0 - ADAM_B1 ** ADAM_STEP
    bc2 = 1.0 - ADAM_B2 ** ADAM_STEP

    def body(w_ref, *rest):
        p_refs = rest[:layers]
        m_ref, v_ref, g_ref, d_ref, nm_ref, nv_ref = rest[layers:]

        def update(p_ref):
            g = p_ref[0, :, pl.ds(0, c)].astype(F32)
            for s in range(1, senders):
                g = g + p_ref[s, :, pl.ds(0, c)].astype(F32)
            nm = ADAM_B1 * m_ref[0] + (1.0 - ADAM_B1) * g
            nv = ADAM_B2 * v_ref[0] + (1.0 - ADAM_B2) * (g * g)
            g_ref[0] = g
            nm_ref[0] = nm
            nv_ref[0] = nv
            d_ref[0] = -ADAM_LR * ((nm / bc1) / (jnp.sqrt(nv / bc2) + ADAM_EPS) + ADAM_WD * w_ref[0])

        for layer in range(layers):
            pl.when(pl.program_id(0) == layer)(functools.partial(update, p_refs[layer]))

    def part_spec(layer, shape):
        rest = 0 if layer > 0 else tiles - 1
        return pl.BlockSpec((senders, tr, shape[2]), lambda l, i: (0, jnp.where(l == layer, i, rest), 0))

    spec = pl.BlockSpec((1, tr, c), lambda l, i: (l, i, 0))
    out = jax.ShapeDtypeStruct((layers, r, c), F32)
    return _pcall(body, name=name, out_shape=(out,) * 4, grid=(layers, tiles),
                  in_specs=[spec] + [part_spec(layer, p.shape) for layer, p in enumerate(parts)] + [spec, spec],
                  out_specs=(spec,) * 4, semantics=("arbitrary", "arbitrary"), vmem_limit=VMEM_LIMIT)(w, *parts, m, v)


def _pool_windows():
    return jnp.repeat(jnp.asarray(POOL_WINDOWS, F32), POOL_DIM // len(POOL_WINDOWS))[None, :]


def _block_diag_pairs(pool_w):
    z = jnp.zeros_like(pool_w[0])
    return jnp.stack([jnp.block([[pool_w[2 * b], z], [z, pool_w[2 * b + 1]]]) for b in range(2)])


def _pad_lanes(vec):
    return jnp.zeros((1, LANE), F32).at[0, :vec.shape[0]].set(vec)


FF_SHARD = D_FF // N_DEV
FF_BLOCK = 384
D_FF_PAD = N_DEV * FF_BLOCK


def _layer_fwd(x, p_i, wt, fetch):
    wt = {**wt, **fetch(0, x)}
    h1 = _rmsnorm_fwd(x, wt["norm1_g"], name="rmsnorm_fwd")
    proj = _matmul(h1, wt["w_in"], "nn", name="mm_in")
    qkv = _qkv_prep_fwd(proj, wt["conv_qkv"], name="qkv_prep_fwd")
    g, beta = _gates_fwd(proj, wt["a_log"], wt["dt_bias"], name="gates_fwd")
    u, w, qg, kg, attn, tmats = _deltanet_prep(qkv, g, beta, name="deltanet_prep")
    o, vn, states = _deltanet_scan(u, w, qg, kg, attn, g, name="deltanet_scan")
    o_a = _apost_fwd(o, proj, wt["onorm_g"], name="apost_fwd")
    o_b = _pool_fwd(proj, wt["pool_win"], wt["pool_wbd"], wt["pool_scale"], name="pool_fwd")
    o_c = _sconv_fwd(proj, wt["sconv_w"], name="sconv_fwd")
    mixed = jnp.concatenate([o_a, o_b, o_c], axis=1)
    wt.update(fetch(1, mixed))
    x1 = _matmul(mixed, wt["w_out"], "nn", res=x, name="mm_out")
    h2 = _rmsnorm_fwd(x1, wt["norm2_g"], name="rmsnorm_fwd")
    wt.update(fetch(2, h2))
    ff, gate, up = _swiglu_fwd(h2, wt["w_gate"], wt["w_up"], name="swiglu_fwd")
    wt.update(fetch(3, ff))
    x2 = _matmul(ff, wt["w_down"], "nn", res=x1, name="mm_down")
    wt.update(fetch(4, x2))
    pgl = _matmul(x2, wt["ple_gate"], "nn", name="mm_pleg")
    pp = _matmul(p_i, wt["ple_proj"], "nn", b_blocked=True, name="mm_plep")
    x3 = _ple_fwd(x2, pgl, pp, name="ple_fwd")
    saved = dict(x=x, h1=h1, proj=proj, qkv=qkv, g=g, beta=beta, o=o, states=states, tmats=tmats, mixed=mixed, x1=x1, h2=h2,
                 gate=gate, up=up, ff=ff, x2=x2, pgl=pgl, pp=pp, p=p_i, w=w, qg=qg, kg=kg, attn=attn, vn=vn, wt=wt)
    return x3, saved


def _col_blocks(g):
    a = g.shape[0]
    return jnp.transpose(g.reshape(a, N_DEV, -1), (1, 0, 2))


def _cols_joined(blocks):
    return jnp.transpose(blocks, (1, 0, 2)).reshape(blocks.shape[1], -1)


def _layer_bwd(dx3, sv, emit, after=None):
    gr, big = {}, {}
    wt = sv["wt"]
    rows = D_MODEL // N_DEV
    dpgl, dpp = _ple_bwd(dx3, sv["pgl"], sv["pp"], name="ple_bwd", after=after)
    big["ple_proj"] = _matmul(sv["p"], dpp, "tn", out_blocked=(N_DEV, rows), out_dtype=BF16, name="mm_dplep")
    big["ple_gate"] = _matmul(sv["x2"], dpgl, "tn", out_dtype=BF16, name="mm_dpleg").reshape(N_DEV, rows, D_MODEL)
    dx2 = _matmul(dpgl, wt["ple_gate"], "nt", res=dx3, name="mm_dx2")
    big["w_down"] = _matmul(sv["ff"], dx2, "tn", out_dtype=BF16, name="mm_ddown").reshape(N_DEV, FF_BLOCK, D_MODEL)
    dgate, dup = _swiglu_bwd(dx2, wt["w_down"], sv["gate"], sv["up"], name="swiglu_bwd", after=emit(0, big))
    big["w_gate"] = _matmul(sv["h2"], dgate, "tn", out_blocked=(N_DEV, FF_BLOCK), out_dtype=BF16, name="mm_dgate")
    big["w_up"] = _matmul(sv["h2"], dup, "tn", out_blocked=(N_DEV, FF_BLOCK), out_dtype=BF16, name="mm_dup")
    dh2 = _matmul(dgate, wt["w_gate"], "nt", b_blocked=True, name="mm_dh2_gate")
    dh2 = _matmul(dup, wt["w_up"], "nt", b_blocked=True, res=dh2, name="mm_dh2_up")
    dx1, gr["norm2_g"] = _rmsnorm_bwd(sv["x1"], wt["norm2_g"], dh2, dx2, name="rmsnorm_bwd")
    big["w_out"] = _matmul(sv["mixed"], dx1, "tn", out_dtype=BF16, name="mm_dout").reshape(N_DEV, rows, D_MODEL)
    dmixed = _matmul(dx1, wt["w_out"], "nt", name="mm_dmixed", after=emit(1, big))
    proj = sv["proj"]
    dcb, dcc, dch, dsconv = _sconv_bwd(proj, wt["sconv_w"], dmixed, name="sconv_bwd")
    big["sconv_w"] = _col_blocks(dsconv)
    dhp, dwbd, gr["pool_scale"] = _pool_bwd(proj, wt["pool_win"], wt["pool_wbd"], wt["pool_scale"], dmixed, name="pool_bwd")
    half = LANE // 2
    gr["pool_w"] = jnp.stack([dwbd[0, :half, :half], dwbd[0, half:, half:], dwbd[1, :half, :half], dwbd[1, half:, half:]])
    do, dz, gr["onorm_g"] = _apost_bwd(sv["o"], proj, wt["onorm_g"], dmixed, name="apost_bwd")
    dvn, dstates = _deltanet_bscan(sv["w"], sv["qg"], sv["kg"], sv["attn"], sv["g"], do, name="deltanet_bscan")
    dqkv_h, dg, dbeta = _deltanet_post(sv["qkv"], sv["g"], sv["beta"], sv["tmats"], sv["states"], dstates, do, dvn, sv["vn"],
                                       name="deltanet_post")
    dab, dalog, ddtb = _gates_bwd(proj, wt["a_log"], wt["dt_bias"], dg, dbeta, name="gates_bwd")
    gr["a_log"], gr["dt_bias"] = dalog[0, :HEADS], ddtb[0, :HEADS]
    dqkv, dconv = _qkv_prep_bwd(proj, wt["conv_qkv"], dqkv_h, name="qkv_prep_bwd")
    big["conv_qkv"] = _col_blocks(dconv)
    dproj = jnp.concatenate([dqkv, dz, dab, dhp, dcb, dcc, dch], axis=1)
    dwin = _matmul(sv["h1"], dproj, "tn", out_dtype=BF16, name="mm_din")
    big["w_in"] = _col_blocks(jnp.concatenate([dwin[:, :AB_COL + 2 * HEADS], dwin[:, AB_COL + LANE:]], axis=1))
    dh1 = _matmul(dproj, wt["w_in"], "nt", name="mm_dh1", after=emit(2, big))
    dx, gr["norm1_g"] = _rmsnorm_bwd(sv["x"], wt["norm1_g"], dh1, dx1, name="rmsnorm_bwd")
    return dx, gr


FETCH_GROUPS = (("w_in", "conv_qkv", "sconv_w"), ("w_out",), ("w_gate", "w_up"), ("w_down",), ("ple_gate", "ple_proj"))
EMIT_GROUPS = (("ple_proj", "ple_gate", "w_down"), ("w_gate", "w_up", "w_out"), ("w_in", "conv_qkv", "sconv_w"))


def _small_weights(w, i):
    return dict(
        norm1_g=w["norm1_g"][i][None], norm2_g=w["norm2_g"][i][None], onorm_g=w["onorm_g"][i][None],
        a_log=_pad_lanes(w["a_log"][i]), dt_bias=_pad_lanes(w["dt_bias"][i]),
        pool_scale=w["pool_scale"][i][None], pool_win=_pool_windows(), pool_wbd=_block_diag_pairs(w["pool_w"][i]))


def _as_read(name, gathered):
    if name == "w_in":
        w_in = _cols_joined(gathered)
        return jnp.concatenate([w_in[:, :AB_COL + 2 * HEADS], jnp.zeros((D_MODEL, LANE - 2 * HEADS), BF16),
                                w_in[:, AB_COL + 2 * HEADS:]], axis=1)
    if name in ("conv_qkv", "sconv_w"):
        return _cols_joined(gathered)
    if name in ("w_gate", "w_up", "ple_proj"):
        return gathered
    return gathered.reshape(-1, D_MODEL)


def _layer_weights(gathered, w, i):
    return {**_small_weights(w, i), **{k: _as_read(k, g) for k, g in gathered.items()}}


def _local_step(x, p, target, layers, final_g):
    saved = []
    h = x
    for i in range(DEPTH):
        replicated = {k: v for k, v in layers[i].items() if k not in SHARDED}
        h, sv = _layer_fwd(h, p[i], replicated, lambda group, after, i=i: {k: layers[i][k] for k in FETCH_GROUPS[group]})
        saved.append(sv)
    dx, dgf, loss = _loss_head(h, final_g, target, name="loss_head")
    big, small = [{} for _ in range(DEPTH)], [None] * DEPTH
    for i in reversed(range(DEPTH)):
        dx, small[i] = _layer_bwd(dx, saved[i], lambda group, blocks, i=i: big[i].update({k: blocks[k] for k in EMIT_GROUPS[group]}))
    return loss, dx, big, small, dgf


SHARDED = ("w_in", "w_gate", "w_up", "w_down", "w_out", "ple_gate", "ple_proj", "conv_qkv", "sconv_w")
SMALL = ("norm1_g", "a_log", "dt_bias", "onorm_g", "pool_w", "pool_scale", "norm2_g", "final_g")
SLAB_COLS = 1024


def _payload(name, shard):
    if name in ("conv_qkv", "sconv_w"):
        return shard
    out = shard.astype(BF16)
    if name in ("w_gate", "w_up"):
        out = jnp.pad(out, ((0, 0), (0, FF_BLOCK - FF_SHARD)))
    if name == "w_down":
        out = jnp.pad(out, ((0, FF_BLOCK - FF_SHARD), (0, 0)))
    return out


def _slab_rows(shape):
    size = 1
    for s in shape:
        size *= s
    return SUBLANE * -(-size // (SUBLANE * SLAB_COLS))


def _pack_slab(parts, extra_row):
    rows = []
    for name in SMALL:
        flat = parts[name].reshape(-1)
        nrow = _slab_rows(parts[name].shape)
        rows.append(jnp.pad(flat, (0, nrow * SLAB_COLS - flat.shape[0])).reshape(nrow, SLAB_COLS))
    rows.append(jnp.pad(extra_row, ((0, SUBLANE - 1), (0, 0))))
    return jnp.concatenate(rows, axis=0)


def _unpack_slab(slab, shapes):
    out, row = {}, 0
    for name in SMALL:
        size = 1
        for s in shapes[name]:
            size *= s
        out[name] = slab[row:row + _slab_rows(shapes[name])].reshape(-1)[:size].reshape(shapes[name])
        row += _slab_rows(shapes[name])
    return out, row


def kernel(x, p, norm1_g, w_in, conv_qkv, a_log, dt_bias, onorm_g, pool_w, pool_scale, sconv_w, w_out, norm2_g, w_gate, w_up, w_down, ple_proj, ple_gate, final_g, loss_target, m_norm1_g, m_w_in, m_conv_qkv, m_a_log, m_dt_bias, m_onorm_g, m_pool_w, m_pool_scale, m_sconv_w, m_w_out, m_norm2_g, m_w_gate, m_w_up, m_w_down, m_ple_proj, m_ple_gate, m_final_g, v_norm1_g, v_w_in, v_conv_qkv, v_a_log, v_dt_bias, v_onorm_g, v_pool_w, v_pool_scale, v_sconv_w, v_w_out, v_norm2_g, v_w_gate, v_w_up, v_w_down, v_ple_proj, v_ple_gate, v_final_g):
    names = ["norm1_g", "w_in", "conv_qkv", "a_log", "dt_bias", "onorm_g", "pool_w", "pool_scale", "sconv_w", "w_out", "norm2_g",
             "w_gate", "w_up", "w_down", "ple_proj", "ple_gate", "final_g"]
    w = dict(zip(names, [norm1_g, w_in, conv_qkv, a_log, dt_bias, onorm_g, pool_w, pool_scale, sconv_w, w_out, norm2_g, w_gate, w_up,
                         w_down, ple_proj, ple_gate, final_g]))
    m = dict(zip(names, [m_norm1_g, m_w_in, m_conv_qkv, m_a_log, m_dt_bias, m_onorm_g, m_pool_w, m_pool_scale, m_sconv_w, m_w_out,
                         m_norm2_g, m_w_gate, m_w_up, m_w_down, m_ple_proj, m_ple_gate, m_final_g]))
    v = dict(zip(names, [v_norm1_g, v_w_in, v_conv_qkv, v_a_log, v_dt_bias, v_onorm_g, v_pool_w, v_pool_scale, v_sconv_w, v_w_out,
                         v_norm2_g, v_w_gate, v_w_up, v_w_down, v_ple_proj, v_ple_gate, v_final_g]))

    in_order = tuple(k for members in FETCH_GROUPS for k in members)
    flying0, token = _exchange_start([[_payload(k, w[k][0]) for k in members] for members in FETCH_GROUPS], CHIP_GATHER,
                                     name="gather_start_0")
    replicated = [_small_weights(w, i) for i in range(DEPTH)]
    replicated[0]["norm1_g"] = replicated[0]["norm1_g"] + token[0, 0]
    gathered, flying1 = {}, []

    def fetch(i, group, after):
        if i == 0:
            landed = _exchange_wait(flying0[group], after, CHIP_GATHER, name=f"gather_wait_0_{group}")
            gathered.update(zip(FETCH_GROUPS[group], _pair_swap(landed, name="pair_swap")))
            if group == 2:
                started, _ = _exchange_start([[_payload(k, w[k][1]) for k in in_order]], CHIP_GATHER, name="gather_start_1",
                                             after=landed[0])
                flying1.extend(started)
        elif group == 0:
            landed = _exchange_wait(flying1[0], after, CHIP_GATHER, name="gather_wait_1")
            gathered.update(zip(in_order, _pair_swap(landed, name="pair_swap")))
        return {k: _as_read(k, gathered[k]) for k in FETCH_GROUPS[group]}

    def reduce_scatter_start(members, blocks, tag):
        mine = [blocks[k] for k in members]
        theirs = _pair_exchange(mine, name="pair_exchange")
        sums = [_pair_add(a, b, name="pair_add") for a, b in zip(mine, theirs)]
        (started,), token = _exchange_start([sums], CHIP_SCATTER, name="exchange_start_" + tag)
        return started, token

    h, saved0 = _layer_fwd(x[0], p[0, 0], replicated[0], functools.partial(fetch, 0))
    h, saved1 = _layer_fwd(h, p[1, 0], replicated[1], functools.partial(fetch, 1))
    dx, dgf, loss_part = _loss_head(h, final_g[None], loss_target[0], name="loss_head")
    small, big1, flying0 = [None] * DEPTH, {}, []
    dx, small[1] = _layer_bwd(dx, saved1, lambda group, blocks: big1.update({k: blocks[k] for k in EMIT_GROUPS[group]}))
    flying1, token = reduce_scatter_start(SHARDED, big1, "1")

    def emit(group, blocks):
        started, token = reduce_scatter_start(EMIT_GROUPS[group], blocks, f"0_{group}")
        flying0.append(started)
        return token

    dx, small[0] = _layer_bwd(dx, saved0, emit, after=token)
    received = [{}, dict(zip(SHARDED, _exchange_wait(flying1, dx, CHIP_SCATTER, name="exchange_wait_1")))]
    for group, members in enumerate(EMIT_GROUPS):
        received[0].update(zip(members, _exchange_wait(flying0[group], dx, CHIP_SCATTER, name=f"exchange_wait_0_{group}")))

    grads = {k: jnp.stack([small[i][k] for i in range(DEPTH)]) for k in small[0]}
    grads = {k: g[:, 0] if k in ("norm1_g", "norm2_g", "onorm_g", "pool_scale") else g for k, g in grads.items()}
    grads["final_g"] = dgf[0]
    loss_row = jnp.pad(loss_part, ((0, 0), (0, SLAB_COLS - LANE)))
    (small_parts,) = _all_gather([_pack_slab(grads, loss_row)], name="all_gather_small_grads")

    out_g, out_d, out_m, out_v = {}, {}, {}, {}
    for k in SHARDED:
        out_g[k], out_d[k], out_m[k], out_v[k] = _adamw_reduce(w[k], [received[i][k] for i in range(DEPTH)], m[k], v[k],
                                                                name="adamw_" + k)
    zero_row = jnp.zeros((1, SLAB_COLS), F32)
    slabs = _adamw_reduce(_pack_slab(w, zero_row)[None], [small_parts], _pack_slab(m, zero_row)[None],
                          _pack_slab(v, zero_row)[None], name="adamw_small")
    slabs = [s[0] for s in slabs]
    shapes = {k: w[k].shape for k in SMALL}
    for dst, slab in zip((out_g, out_d, out_m, out_v), slabs):
        vals, _ = _unpack_slab(slab, shapes)
        dst.update(vals)
    _, loss_at = _unpack_slab(slabs[0], shapes)
    loss = slabs[0][loss_at, 0]

    return (loss, dx[None], *[out_g[k] for k in names], *[out_d[k] for k in names], *[out_m[k] for k in names],
            *[out_v[k] for k in names])
```

```python
import functools

import jax
import jax.numpy as jnp
from jax import lax
from jax.experimental import pallas as pl
from jax.experimental.pallas import tpu as pltpu

F32 = jnp.float32
BF16 = jnp.bfloat16

D_MODEL = 1024
DEPTH = 2
PLE_DIM = 256
EPS = 1e-6
HEAD_DIM = 128
HEADS = 4
A_DIM = HEADS * HEAD_DIM
QKV_TAPS = 4
CHUNK = 64
POOL_WINDOWS = (2, 4, 8, 16)
POOL_DIM = 256
CONV_DIM = 256
CONV_TAPS = 3
D_FF = 2816
D_IN = 3080
D_IN_PAD = 3200
AB_COL = 2048
N_DEV = 8

ADAM_LR = 0.001
ADAM_B1 = 0.9
ADAM_B2 = 0.999
ADAM_EPS = 1e-08
ADAM_WD = 0.01
ADAM_STEP = 10

LANE = 128
SUBLANE = 8
VMEM_BYTES_V7X = 64 * 1024 * 1024
VMEM_LIMIT = 48 * 1024 * 1024

_HI = lax.Precision.HIGHEST
NN = ((1,), (0,))
NT = ((1,), (1,))
TN = ((0,), (0,))
MESH = pl.DeviceIdType.MESH


def _dot(a, b, dims, hi=False):
    if hi:
        return lax.dot_general(a, b, (dims, ((), ())), precision=_HI, preferred_element_type=F32)
    return lax.dot_general(a.astype(BF16), b.astype(BF16), (dims, ((), ())), preferred_element_type=F32)


def _pcall(body, *, name, out_shape, grid=(), in_specs=None, out_specs=None, scratch_shapes=(), semantics=None,
           vmem_limit=None, after=None, **kw):
    params = {}
    if semantics is not None:
        params["dimension_semantics"] = semantics
    if vmem_limit is not None:
        params["vmem_limit_bytes"] = vmem_limit
    if after is not None:
        n_in, inner = len(in_specs), body
        body = lambda *refs: inner(*refs[:n_in], *refs[n_in + 1:])
        in_specs = list(in_specs) + [pl.BlockSpec(after.shape, lambda *_: (0,) * after.ndim)]
    call = pl.pallas_call(
        body, name=name, out_shape=out_shape, grid=grid, in_specs=in_specs, out_specs=out_specs,
        scratch_shapes=list(scratch_shapes), compiler_params=pltpu.CompilerParams(**params), **kw)
    return call if after is None else (lambda *args: call(*args, after))


def _sigmoid(x):
    return 1.0 / (1.0 + jnp.exp(-x))


def _softplus(x):
    return jnp.maximum(x, 0.0) + jnp.log(1.0 + jnp.exp(-jnp.abs(x)))


def _tile(n, cap, mult):
    if n <= cap:
        return n
    best = None
    for t in range(mult, cap + 1, mult):
        if n % t == 0:
            best = t
    assert best is not None, (n, cap, mult)
    return best


ROWS_PER_STEP = 512
NARROW_RESULT = 1024
COLS_PER_DOT = 640


def _resident(weight):
    return pl.BlockSpec(weight.shape, lambda i: (0,) * weight.ndim, pipeline_mode=pl.Buffered(1))


def _matmul_rows(a, b, mode, *, name, res=None, out_dtype=F32, b_blocked=False, after=None):
    m, k = a.shape
    if b_blocked:
        nb, _, bw = b.shape
        n = nb * bw if mode == "nn" else b.shape[1]
    else:
        n = b.shape[1] if mode == "nn" else b.shape[0]
    tm = _tile(m, ROWS_PER_STEP if n > NARROW_RESULT else 2 * ROWS_PER_STEP, 16)
    cn = bw if (b_blocked and mode == "nn") else _tile(n, COLS_PER_DOT, LANE)
    has_res = res is not None

    def body(*refs):
        a_ref, b_ref = refs[0], refs[1]
        res_ref = refs[2] if has_res else None
        o_ref = refs[2 + has_res]
        if not (b_blocked and mode == "nt"):
            av = a_ref[...].astype(BF16)
        for j in range(n // cn):
            cols = pl.ds(j * cn, cn)
            if mode == "nn":
                part = _dot(av, b_ref[j] if b_blocked else b_ref[:, cols], NN)
            elif not b_blocked:
                part = _dot(av, b_ref[cols, :], NT)
            else:
                part = None
                for s in range(nb):
                    term = _dot(a_ref[:, pl.ds(s * bw, bw)], b_ref[s, cols, :], NT)
                    part = term if part is None else part + term
            if has_res:
                part = part + res_ref[:, cols]
            o_ref[:, cols] = part.astype(o_ref.dtype)

    row = lambda width: pl.BlockSpec((tm, width), lambda i: (i, 0))
    whole = _resident(b)
    ins = [a, b] + ([res] if has_res else [])
    specs = [row(k), whole] + ([row(n)] if has_res else [])
    return _pcall(body, name=name, out_shape=jax.ShapeDtypeStruct((m, n), out_dtype), grid=(m // tm,), in_specs=specs,
                  out_specs=row(n), semantics=("parallel",), vmem_limit=VMEM_LIMIT, after=after)(*ins)


def _matmul(a, b, mode, *, name, res=None, out_dtype=F32, b_blocked=False, out_blocked=None, after=None):
    if mode != "tn":
        return _matmul_rows(a, b, mode, name=name, res=res, out_dtype=out_dtype, b_blocked=b_blocked, after=after)
    assert res is None and not b_blocked and after is None
    (t, m), (t2, n) = a.shape, b.shape
    assert t == t2, (a.shape, b.shape)
    tm = _tile(m, 1024, LANE)
    tn = _tile(n, COLS_PER_DOT, LANE)
    if out_blocked is not None:
        assert out_blocked[0] * out_blocked[1] == n
        tn = out_blocked[1]

    def body(a_ref, b_ref, o_ref):
        part = _dot(a_ref[...], b_ref[...], TN).astype(o_ref.dtype)
        if out_blocked is None:
            o_ref[...] = part
        else:
            o_ref[0] = part

    o_spec = (pl.BlockSpec((tm, tn), lambda i, j: (i, j)) if out_blocked is None
              else pl.BlockSpec((1, tm, tn), lambda i, j: (j, i, 0)))
    o_shape = (m, n) if out_blocked is None else (out_blocked[0], m, out_blocked[1])
    return _pcall(body, name=name, out_shape=jax.ShapeDtypeStruct(o_shape, out_dtype), grid=(m // tm, n // tn),
                  in_specs=[pl.BlockSpec((t, tm), lambda i, j: (0, i)), pl.BlockSpec((t, tn), lambda i, j: (0, j))],
                  out_specs=o_spec, semantics=("parallel", "parallel"), vmem_limit=VMEM_LIMIT)(a, b)


ROW_TILE = 512


def _rows(t, width, idx=0):
    return pl.BlockSpec((ROW_TILE, width), lambda i: (i, idx))


def _vec(width):
    return pl.BlockSpec((1, width), lambda i: (0, 0))


def _rmsnorm_fwd(x, g, *, name):
    t, d = x.shape

    def body(x_ref, g_ref, h_ref):
        xv = x_ref[...]
        r = lax.rsqrt(jnp.mean(xv * xv, axis=-1, keepdims=True) + EPS)
        h_ref[...] = (xv * r * g_ref[...]).astype(BF16)

    return _pcall(body, name=name, out_shape=jax.ShapeDtypeStruct((t, d), BF16), grid=(t // ROW_TILE,),
                  in_specs=[_rows(t, d), _vec(d)], out_specs=_rows(t, d), semantics=("parallel",))(x, g)


def _rmsnorm_bwd(x, g, dh, dres, *, name):
    t, d = x.shape

    def body(x_ref, g_ref, dh_ref, dres_ref, dx_ref, dg_ref):
        xv = x_ref[...]
        r = lax.rsqrt(jnp.mean(xv * xv, axis=-1, keepdims=True) + EPS)
        xhat = xv * r
        dhv = dh_ref[...].astype(F32)
        dhg = dhv * g_ref[...]
        dx_ref[...] = dres_ref[...] + r * (dhg - xhat * jnp.mean(dhg * xhat, axis=-1, keepdims=True))
        part = jnp.sum(dhv * xhat, axis=0, keepdims=True)

        @pl.when(pl.program_id(0) == 0)
        def _():
            dg_ref[...] = part

        @pl.when(pl.program_id(0) > 0)
        def _():
            dg_ref[...] += part

    return _pcall(body, name=name, out_shape=(jax.ShapeDtypeStruct((t, d), F32), jax.ShapeDtypeStruct((1, d), F32)),
                  grid=(t // ROW_TILE,), in_specs=[_rows(t, d), _vec(d), _rows(t, d), _rows(t, d)],
                  out_specs=(_rows(t, d), _vec(d)), semantics=("arbitrary",))(x, g, dh, dres)


def _swiglu_fwd(h, w_gate, w_up, *, name):
    t, k = h.shape
    nb, _, bw = w_gate.shape
    tm = _tile(t, ROWS_PER_STEP, 16)

    def body(h_ref, wg_ref, wu_ref, ff_ref, gate_ref, up_ref):
        hv = h_ref[...]
        for j in range(nb):
            cols = pl.ds(j * bw, bw)
            gv = _dot(hv, wg_ref[j], NN)
            uv = _dot(hv, wu_ref[j], NN)
            gate_ref[:, cols] = gv.astype(BF16)
            up_ref[:, cols] = uv.astype(BF16)
            ff_ref[:, cols] = (gv * _sigmoid(gv) * uv).astype(BF16)

    row = lambda width: pl.BlockSpec((tm, width), lambda i: (i, 0))
    out = jax.ShapeDtypeStruct((t, nb * bw), BF16)
    return _pcall(body, name=name, out_shape=(out,) * 3, grid=(t // tm,), in_specs=[row(k), _resident(w_gate), _resident(w_up)],
                  out_specs=(row(nb * bw),) * 3, semantics=("parallel",), vmem_limit=VMEM_LIMIT)(h, w_gate, w_up)


def _swiglu_bwd(dx2, w_down, gate, up, *, name, after=None):
    t, d = dx2.shape
    f = w_down.shape[0]
    tm = _tile(t, ROWS_PER_STEP, 16)
    cn = _tile(f, COLS_PER_DOT, LANE)

    def body(dx_ref, w_ref, gate_ref, up_ref, dgate_ref, dup_ref):
        dxv = dx_ref[...].astype(BF16)
        for j in range(f // cn):
            cols = pl.ds(j * cn, cn)
            dffv = _dot(dxv, w_ref[cols, :], NT)
            gv = gate_ref[:, cols].astype(F32)
            sig = _sigmoid(gv)
            dgate_ref[:, cols] = (dffv * up_ref[:, cols].astype(F32) * sig * (1.0 + gv * (1.0 - sig))).astype(BF16)
            dup_ref[:, cols] = (dffv * gv * sig).astype(BF16)

    row = lambda width: pl.BlockSpec((tm, width), lambda i: (i, 0))
    out = jax.ShapeDtypeStruct((t, f), BF16)
    return _pcall(body, name=name, out_shape=(out, out), grid=(t // tm,), in_specs=[row(d), _resident(w_down), row(f), row(f)],
                  out_specs=(row(f), row(f)), semantics=("parallel",), vmem_limit=VMEM_LIMIT, after=after)(dx2, w_down, gate, up)


def _ple_fwd(x2, pgl, pp, *, name):
    t, d = x2.shape

    def body(x_ref, pgl_ref, pp_ref, o_ref):
        o_ref[...] = x_ref[...] + _sigmoid(pgl_ref[...]) * pp_ref[...]

    return _pcall(body, name=name, out_shape=jax.ShapeDtypeStruct((t, d), F32), grid=(t // ROW_TILE,),
                  in_specs=[_rows(t, d)] * 3, out_specs=_rows(t, d), semantics=("parallel",))(x2, pgl, pp)


def _ple_bwd(dx3, pgl, pp, *, name, after=None):
    t, d = dx3.shape

    def body(dx_ref, pgl_ref, pp_ref, dpgl_ref, dpp_ref):
        dxv = dx_ref[...]
        sig = _sigmoid(pgl_ref[...])
        dpp_ref[...] = (dxv * sig).astype(BF16)
        dpgl_ref[...] = (dxv * pp_ref[...] * sig * (1.0 - sig)).astype(BF16)

    return _pcall(body, name=name, out_shape=(jax.ShapeDtypeStruct((t, d), BF16),) * 2, grid=(t // ROW_TILE,),
                  in_specs=[_rows(t, d)] * 3, out_specs=(_rows(t, d),) * 2, semantics=("parallel",), after=after)(dx3, pgl, pp)


def _loss_head(x3, g, target, *, name):
    t, d = x3.shape

    def body(x_ref, g_ref, t_ref, dx_ref, dg_ref, loss_ref):
        xv = x_ref[...]
        r = lax.rsqrt(jnp.mean(xv * xv, axis=-1, keepdims=True) + EPS)
        xhat = xv * r
        gv = g_ref[...]
        err = xhat * gv - t_ref[...]
        row_loss = jnp.sum(err * err, axis=-1, keepdims=True) * (0.5 / d)
        lpart = jnp.broadcast_to(jnp.sum(row_loss, axis=0, keepdims=True), (1, LANE))
        dy = err * (1.0 / d)
        dyg = dy * gv
        dx_ref[...] = r * (dyg - xhat * jnp.mean(dyg * xhat, axis=-1, keepdims=True))
        gpart = jnp.sum(dy * xhat, axis=0, keepdims=True)

        @pl.when(pl.program_id(0) == 0)
        def _():
            dg_ref[...] = gpart
            loss_ref[...] = lpart

        @pl.when(pl.program_id(0) > 0)
        def _():
            dg_ref[...] += gpart
            loss_ref[...] += lpart

    return _pcall(body, name=name,
                  out_shape=(jax.ShapeDtypeStruct((t, d), F32), jax.ShapeDtypeStruct((1, d), F32), jax.ShapeDtypeStruct((1, LANE), F32)),
                  grid=(t // ROW_TILE,), in_specs=[_rows(t, d), _vec(d), _rows(t, d)],
                  out_specs=(_rows(t, d), _vec(d), _vec(LANE)), semantics=("arbitrary",))(x3, g, target)


def _shift_down(x, d):
    if d == 0:
        return x
    row = lax.broadcasted_iota(jnp.int32, x.shape, 0)
    return jnp.where(row >= d, pltpu.roll(x, d, 0), 0.0)


def _shift_up(x, d):
    if d == 0:
        return x
    t = x.shape[0]
    row = lax.broadcasted_iota(jnp.int32, x.shape, 0)
    return jnp.where(row < t - d, pltpu.roll(x, t - d, 0), 0.0)


def _colsum(x):
    return jnp.sum(x, axis=0, keepdims=True)


def _col(t, idx_fn):
    return pl.BlockSpec((t, LANE), idx_fn)


def _conv_fwd(x, w_ref, taps):
    acc = None
    for j in range(taps):
        term = w_ref[pl.ds(j, 1), :] * _shift_down(x, taps - 1 - j)
        acc = term if acc is None else acc + term
    return acc


def _conv_bwd(x, dy, w_ref, dw_ref, taps):
    dx = None
    for j in range(taps):
        term = w_ref[pl.ds(j, 1), :] * _shift_up(dy, taps - 1 - j)
        dx = term if dx is None else dx + term
        dw_ref[pl.ds(j, 1), :] = _colsum(dy * _shift_down(x, taps - 1 - j))
    return dx


def _qkv_prep_fwd(proj, conv_w, *, name):
    t = proj.shape[0]
    scale = HEAD_DIM ** -0.5

    def body(x_ref, w_ref, o_ref):
        j = pl.program_id(0)
        c = _conv_fwd(x_ref[...], w_ref, QKV_TAPS)
        s = c * _sigmoid(c)
        r = lax.rsqrt(jnp.sum(s * s, axis=-1, keepdims=True) + EPS)
        f = jnp.where(j < 2 * HEADS, r, 1.0) * jnp.where(j < HEADS, scale, 1.0)
        o_ref[0] = s * f

    return _pcall(body, name=name, out_shape=jax.ShapeDtypeStruct((3 * HEADS, t, LANE), F32), grid=(3 * HEADS,),
                  in_specs=[_col(t, lambda j: (0, j)), pl.BlockSpec((QKV_TAPS, LANE), lambda j: (0, j))],
                  out_specs=pl.BlockSpec((1, t, LANE), lambda j: (j, 0, 0)), semantics=("parallel",),
                  vmem_limit=VMEM_LIMIT)(proj, conv_w)


def _qkv_prep_bwd(proj, conv_w, dqkv, *, name):
    t = proj.shape[0]
    scale = HEAD_DIM ** -0.5

    def body(x_ref, w_ref, d_ref, dx_ref, dw_ref):
        j = pl.program_id(0)
        xv = x_ref[...]
        c = _conv_fwd(xv, w_ref, QKV_TAPS)
        sig = _sigmoid(c)
        s = c * sig
        r = lax.rsqrt(jnp.sum(s * s, axis=-1, keepdims=True) + EPS)
        n0 = s * r
        dv = d_ref[0]
        dn0 = dv * jnp.where(j < HEADS, scale, 1.0)
        ds_norm = r * (dn0 - n0 * jnp.sum(dn0 * n0, axis=-1, keepdims=True))
        ds = jnp.where(j < 2 * HEADS, ds_norm, dv)
        dc = ds * sig * (1.0 + c * (1.0 - sig))
        dx_ref[...] = _conv_bwd(xv, dc, w_ref, dw_ref, QKV_TAPS).astype(BF16)

    return _pcall(body, name=name,
                  out_shape=(jax.ShapeDtypeStruct((t, 3 * A_DIM), BF16), jax.ShapeDtypeStruct((QKV_TAPS, 3 * A_DIM), F32)),
                  grid=(3 * HEADS,),
                  in_specs=[_col(t, lambda j: (0, j)), pl.BlockSpec((QKV_TAPS, LANE), lambda j: (0, j)),
                            pl.BlockSpec((1, t, LANE), lambda j: (j, 0, 0))],
                  out_specs=(_col(t, lambda j: (0, j)), pl.BlockSpec((QKV_TAPS, LANE), lambda j: (0, j))),
                  semantics=("parallel",), vmem_limit=VMEM_LIMIT)(proj, conv_w, dqkv)


def _lane_pick(x, lane_idx, lane):
    return jnp.broadcast_to(jnp.sum(jnp.where(lane == lane_idx, x, 0.0), axis=-1, keepdims=True), x.shape)


def _gates_fwd(proj, alog, dtb, *, name):
    t = proj.shape[0]

    def body(x_ref, alog_ref, dtb_ref, g_ref, b_ref):
        xv = x_ref[...]
        lane = lax.broadcasted_iota(jnp.int32, xv.shape, 1)
        gall = -jnp.exp(alog_ref[...]) * _softplus(xv + dtb_ref[...])
        ball = _sigmoid(xv)
        for h in range(HEADS):
            g_ref[h] = _lane_pick(gall, h, lane)
            b_ref[h] = _lane_pick(ball, HEADS + h, lane)

    out = jax.ShapeDtypeStruct((HEADS, t, LANE), F32)
    whole = pl.BlockSpec((HEADS, t, LANE), lambda i: (0, 0, 0))
    return _pcall(body, name=name, out_shape=(out, out), grid=(1,),
                  in_specs=[_col(t, lambda i: (0, AB_COL // LANE)), _vec(LANE), _vec(LANE)], out_specs=(whole, whole),
                  semantics=("arbitrary",), vmem_limit=VMEM_LIMIT)(proj, alog, dtb)


def _gates_bwd(proj, alog, dtb, dg, dbeta, *, name):
    t = proj.shape[0]

    def body(x_ref, alog_ref, dtb_ref, dg_ref, db_ref, dab_ref, dalog_ref, ddtb_ref):
        xv = x_ref[...]
        lane = lax.broadcasted_iota(jnp.int32, xv.shape, 1)
        lane1 = lax.broadcasted_iota(jnp.int32, (1, LANE), 1)
        z = xv + dtb_ref[...]
        nea = -jnp.exp(alog_ref[...])
        da_f = nea * _sigmoid(z)
        g_f = nea * _softplus(z)
        ball = _sigmoid(xv)
        db_f = ball * (1.0 - ball)
        dab = jnp.zeros_like(xv)
        dalog = jnp.zeros((1, LANE), F32)
        for h in range(HEADS):
            dgh = dg_ref[h]
            dab = dab + jnp.where(lane == h, dgh * da_f, 0.0) + jnp.where(lane == HEADS + h, db_ref[h] * db_f, 0.0)
            dalog = dalog + jnp.where(lane1 == h, _colsum(dgh * g_f), 0.0)
        dab_ref[...] = dab.astype(BF16)
        dalog_ref[...] = dalog
        ddtb_ref[...] = jnp.where(lane1 < HEADS, _colsum(dab), 0.0)

    whole = pl.BlockSpec((HEADS, t, LANE), lambda i: (0, 0, 0))
    vec = jax.ShapeDtypeStruct((1, LANE), F32)
    return _pcall(body, name=name, out_shape=(jax.ShapeDtypeStruct((t, LANE), BF16), vec, vec), grid=(1,),
                  in_specs=[_col(t, lambda i: (0, AB_COL // LANE)), _vec(LANE), _vec(LANE), whole, whole],
                  out_specs=(_col(t, lambda i: (0, 0)), _vec(LANE), _vec(LANE)), semantics=("arbitrary",),
                  vmem_limit=VMEM_LIMIT)(proj, alog, dtb, dg, dbeta)


Z_COL = 3 * A_DIM // LANE


def _apost_fwd(o, proj, gn, *, name):
    t = proj.shape[0]

    def body(o_ref, z_ref, gn_ref, y_ref):
        ov = o_ref[0]
        z = z_ref[...]
        r = lax.rsqrt(jnp.mean(ov * ov, axis=-1, keepdims=True) + EPS)
        y_ref[...] = (ov * r * gn_ref[...] * (z * _sigmoid(z))).astype(BF16)

    return _pcall(body, name=name, out_shape=jax.ShapeDtypeStruct((t, A_DIM), BF16), grid=(HEADS,),
                  in_specs=[pl.BlockSpec((1, t, LANE), lambda h: (h, 0, 0)), _col(t, lambda h: (0, Z_COL + h)),
                            pl.BlockSpec((1, LANE), lambda h: (0, 0))],
                  out_specs=_col(t, lambda h: (0, h)), semantics=("parallel",), vmem_limit=VMEM_LIMIT)(o, proj, gn)


def _apost_bwd(o, proj, gn, dmixed, *, name):
    t = proj.shape[0]

    def body(o_ref, z_ref, gn_ref, d_ref, do_ref, dz_ref, dgn_ref):
        ov = o_ref[0]
        z = z_ref[...]
        gnv = gn_ref[...]
        dv = d_ref[...]
        r = lax.rsqrt(jnp.mean(ov * ov, axis=-1, keepdims=True) + EPS)
        ohat = ov * r
        sig = _sigmoid(z)
        dy = dv * (z * sig)
        dz_ref[...] = (dv * ohat * gnv * sig * (1.0 + z * (1.0 - sig))).astype(BF16)
        dyo = dy * gnv
        do_ref[0] = r * (dyo - ohat * jnp.mean(dyo * ohat, axis=-1, keepdims=True))
        part = _colsum(dy * ohat)

        @pl.when(pl.program_id(0) == 0)
        def _():
            dgn_ref[...] = part

        @pl.when(pl.program_id(0) > 0)
        def _():
            dgn_ref[...] += part

    return _pcall(body, name=name,
                  out_shape=(jax.ShapeDtypeStruct((HEADS, t, LANE), F32), jax.ShapeDtypeStruct((t, A_DIM), BF16),
                             jax.ShapeDtypeStruct((1, LANE), F32)),
                  grid=(HEADS,),
                  in_specs=[pl.BlockSpec((1, t, LANE), lambda h: (h, 0, 0)), _col(t, lambda h: (0, Z_COL + h)),
                            pl.BlockSpec((1, LANE), lambda h: (0, 0)), _col(t, lambda h: (0, h))],
                  out_specs=(pl.BlockSpec((1, t, LANE), lambda h: (h, 0, 0)), _col(t, lambda h: (0, h)),
                             pl.BlockSpec((1, LANE), lambda h: (0, 0))),
                  semantics=("arbitrary",), vmem_limit=VMEM_LIMIT)(o, proj, gn, dmixed)


POOL_COL = (AB_COL + LANE) // LANE
CB_COL = POOL_COL + POOL_DIM // LANE
CC_COL = CB_COL + CONV_DIM // LANE
CH_COL = CC_COL + CONV_DIM // LANE
MAX_WIN_LOG2 = 4


def _window_sums(x, shift):
    sums = []
    cur = x
    for k in range(MAX_WIN_LOG2):
        cur = cur + shift(cur, 1 << k)
        sums.append(cur)
    return sums


def _pick_window(sums, win):
    out = sums[-1]
    for k in range(MAX_WIN_LOG2 - 2, -1, -1):
        out = jnp.where(win == float(2 << k), sums[k], out)
    return out


def _pool_counts(shape, win):
    row = lax.broadcasted_iota(jnp.int32, shape, 0).astype(F32)
    return jnp.minimum(row + 1.0, win)


def _pool_fwd(proj, win, wbd, scale, *, name):
    t = proj.shape[0]

    def body(x_ref, win_ref, w_ref, s_ref, y_ref):
        xv = x_ref[...]
        winv = win_ref[...]
        pooled = _pick_window(_window_sums(xv, _shift_down), winv) / _pool_counts(xv.shape, winv) - xv
        y_ref[...] = (_dot(pooled, w_ref[0], NN) * s_ref[...]).astype(BF16)

    nb = POOL_DIM // LANE
    vec = pl.BlockSpec((1, LANE), lambda b: (0, b))
    return _pcall(body, name=name, out_shape=jax.ShapeDtypeStruct((t, POOL_DIM), BF16), grid=(nb,),
                  in_specs=[_col(t, lambda b: (0, POOL_COL + b)), vec, pl.BlockSpec((1, LANE, LANE), lambda b: (b, 0, 0)), vec],
                  out_specs=_col(t, lambda b: (0, b)), semantics=("parallel",), vmem_limit=VMEM_LIMIT)(proj, win, wbd, scale)


def _pool_bwd(proj, win, wbd, scale, dmixed, *, name):
    t = proj.shape[0]

    def body(x_ref, win_ref, w_ref, s_ref, d_ref, dx_ref, dw_ref, ds_ref):
        xv = x_ref[...]
        winv = win_ref[...]
        cnt = _pool_counts(xv.shape, winv)
        pooled = _pick_window(_window_sums(xv, _shift_down), winv) / cnt - xv
        dv = d_ref[...]
        ds_ref[...] = _colsum(dv * _dot(pooled, w_ref[0], NN))
        dy0 = dv * s_ref[...]
        dw_ref[0] = _dot(pooled, dy0, TN)
        dpooled = _dot(dy0, w_ref[0], NT)
        dmean = dpooled / cnt
        dx_ref[...] = (_pick_window(_window_sums(dmean, _shift_up), winv) - dpooled).astype(BF16)

    nb = POOL_DIM // LANE
    vec = pl.BlockSpec((1, LANE), lambda b: (0, b))
    mat = pl.BlockSpec((1, LANE, LANE), lambda b: (b, 0, 0))
    first = A_DIM // LANE
    return _pcall(body, name=name,
                  out_shape=(jax.ShapeDtypeStruct((t, POOL_DIM), BF16), jax.ShapeDtypeStruct((nb, LANE, LANE), F32),
                             jax.ShapeDtypeStruct((1, POOL_DIM), F32)),
                  grid=(nb,),
                  in_specs=[_col(t, lambda b: (0, POOL_COL + b)), vec, mat, vec, _col(t, lambda b: (0, first + b))],
                  out_specs=(_col(t, lambda b: (0, b)), mat, vec), semantics=("parallel",),
                  vmem_limit=VMEM_LIMIT)(proj, win, wbd, scale, dmixed)


def _sconv_fwd(proj, w, *, name):
    t = proj.shape[0]

    def body(cb_ref, cc_ref, ch_ref, w_ref, y_ref):
        y_ref[...] = (cb_ref[...] * _conv_fwd(cc_ref[...] * ch_ref[...], w_ref, CONV_TAPS)).astype(BF16)

    nb = CONV_DIM // LANE
    return _pcall(body, name=name, out_shape=jax.ShapeDtypeStruct((t, CONV_DIM), BF16), grid=(nb,),
                  in_specs=[_col(t, lambda b: (0, CB_COL + b)), _col(t, lambda b: (0, CC_COL + b)),
                            _col(t, lambda b: (0, CH_COL + b)), pl.BlockSpec((CONV_TAPS, LANE), lambda b: (0, b))],
                  out_specs=_col(t, lambda b: (0, b)), semantics=("parallel",), vmem_limit=VMEM_LIMIT)(proj, proj, proj, w)


def _sconv_bwd(proj, w, dmixed, *, name):
    t = proj.shape[0]

    def body(cb_ref, cc_ref, ch_ref, w_ref, d_ref, dcb_ref, dcc_ref, dch_ref, dw_ref):
        cc = cc_ref[...]
        ch = ch_ref[...]
        u = cc * ch
        dv = d_ref[...]
        dcb_ref[...] = (dv * _conv_fwd(u, w_ref, CONV_TAPS)).astype(BF16)
        du = _conv_bwd(u, dv * cb_ref[...], w_ref, dw_ref, CONV_TAPS)
        dcc_ref[...] = (du * ch).astype(BF16)
        dch_ref[...] = (du * cc).astype(BF16)

    nb = CONV_DIM // LANE
    first = (A_DIM + POOL_DIM) // LANE
    act = jax.ShapeDtypeStruct((t, CONV_DIM), BF16)
    wspec = pl.BlockSpec((CONV_TAPS, LANE), lambda b: (0, b))
    ospec = _col(t, lambda b: (0, b))
    return _pcall(body, name=name, out_shape=(act, act, act, jax.ShapeDtypeStruct((CONV_TAPS, CONV_DIM), F32)), grid=(nb,),
                  in_specs=[_col(t, lambda b: (0, CB_COL + b)), _col(t, lambda b: (0, CC_COL + b)),
                            _col(t, lambda b: (0, CH_COL + b)), wspec, _col(t, lambda b: (0, first + b))],
                  out_specs=(ospec, ospec, ospec, wspec), semantics=("parallel",),
                  vmem_limit=VMEM_LIMIT)(proj, proj, proj, w, dmixed)


def _chunk_masks():
    r = lax.broadcasted_iota(jnp.int32, (CHUNK, CHUNK), 0)
    c = lax.broadcasted_iota(jnp.int32, (CHUNK, CHUNK), 1)
    return r >= c, r > c, jnp.where(r == c, 1.0, 0.0).astype(F32)


def _split(a):
    hi = a.astype(BF16)
    return hi, (a - hi.astype(F32)).astype(BF16)


def _dot_split(a, b, dims):
    (ah, al), (bh, bl) = a, b
    return _dot(ah, bh, dims) + _dot(ah, bl, dims) + _dot(al, bh, dims)


def _tri_inv(lows, eye):
    xs = [eye - low for low in lows]
    ps = [_split(low) for low in lows]
    ps = [_split(_dot_split(p, p, NN)) for p in ps]
    for i in range(5):
        xs = [x + _dot_split(_split(x), p, NN) for x, p in zip(xs, ps)]
        if i < 4:
            ps = [_split(_dot_split(p, p, NN)) for p in ps]
    return xs


def _prefix_sum_rows(x):
    for k in range(6):
        x = x + _shift_down(x, 1 << k)
    return x


def _suffix_sum_rows(x):
    for k in range(6):
        x = x + _shift_up(x, 1 << k)
    return x


def _chunk_decay(g, incl):
    gcb = _prefix_sum_rows(g)
    gtot = _colsum(g)
    col = gcb[:, :CHUNK]
    row = gcb.T[:CHUNK, :]
    decay = jnp.exp(jnp.where(incl, col - row, -1e30))
    return gcb, gtot, decay


CHUNKS_PER_STEP = 4


def _heads_of(ref, base, rows):
    return [ref[base + h, rows, :] for h in range(HEADS)]


def _chunk_rows(j):
    return pl.ds(j * CHUNK, CHUNK)


def _deltanet_prep(qkv, g, beta, *, name):
    t = qkv.shape[1]
    n_chunks = t // CHUNK
    per = CHUNKS_PER_STEP
    probs = [(j, h) for j in range(per) for h in range(HEADS)]

    def body(qkv_ref, g_ref, b_ref, u_ref, w_ref, qg_ref, kg_ref, attn_ref, tm_ref):
        incl, strict, eye = _chunk_masks()
        q = [qkv_ref[h, _chunk_rows(j), :] for j, h in probs]
        k = [qkv_ref[HEADS + h, _chunk_rows(j), :] for j, h in probs]
        v = [qkv_ref[2 * HEADS + h, _chunk_rows(j), :] for j, h in probs]
        bv = [b_ref[h, _chunk_rows(j), :] for j, h in probs]
        dec = [_chunk_decay(g_ref[h, _chunk_rows(j), :], incl) for j, h in probs]
        kb = [a * b for a, b in zip(k, bv)]
        low = [jnp.where(strict, _dot(a, b, NT) * d[2], 0.0) for a, b, d in zip(kb, k, dec)]
        tm = _tri_inv(low, eye)
        egc = [jnp.exp(d[0]) for d in dec]
        u = [_dot(m, a * b, NN) for m, a, b in zip(tm, v, bv)]
        w = [_dot(m, a * e, NN) for m, a, e in zip(tm, kb, egc)]
        attn = [_dot(a, b, NT) * d[2] for a, b, d in zip(q, k, dec)]
        for i, (j, h) in enumerate(probs):
            rows = _chunk_rows(j)
            u_ref[h, rows, :] = u[i]
            w_ref[h, rows, :] = w[i].astype(BF16)
            qg_ref[h, rows, :] = (q[i] * egc[i]).astype(BF16)
            kg_ref[h, rows, :] = (k[i] * jnp.exp(dec[i][1] - dec[i][0])).astype(BF16)
            attn_ref[j, h] = attn[i].astype(BF16)
            tm_ref[j, h] = tm[i]

    act = lambda heads: pl.BlockSpec((heads, per * CHUNK, LANE), lambda n: (0, n, 0))
    mat = pl.BlockSpec((per, HEADS, CHUNK, CHUNK), lambda n: (n, 0, 0, 0))
    return _pcall(
        body, name=name,
        out_shape=(jax.ShapeDtypeStruct((HEADS, t, LANE), F32),) + (jax.ShapeDtypeStruct((HEADS, t, LANE), BF16),) * 3
        + (jax.ShapeDtypeStruct((n_chunks, HEADS, CHUNK, CHUNK), BF16), jax.ShapeDtypeStruct((n_chunks, HEADS, CHUNK, CHUNK), F32)),
        grid=(n_chunks // per,), in_specs=[act(3 * HEADS), act(HEADS), act(HEADS)],
        out_specs=(act(HEADS),) * 4 + (mat, mat), semantics=("parallel",), vmem_limit=VMEM_LIMIT)(qkv, g, beta)


SCAN_CHUNKS_PER_STEP = 8


def _deltanet_scan(u, w, qg, kg, attn, g, *, name):
    t = u.shape[1]
    n_chunks = t // CHUNK
    per = SCAN_CHUNKS_PER_STEP

    def body(u_ref, w_ref, qg_ref, kg_ref, attn_ref, g_ref, o_ref, vn_ref, st_ref, s_ref):
        @pl.when(pl.program_id(0) == 0)
        def _():
            s_ref[...] = jnp.zeros_like(s_ref)

        for j in range(per):
            rows = _chunk_rows(j)
            s = [s_ref[h] for h in range(HEADS)]
            vn = [u_ref[h, rows, :] - _dot(w_ref[h, rows, :], s[h], NN) for h in range(HEADS)]
            o = [_dot(qg_ref[h, rows, :], s[h], NN) + _dot(attn_ref[j, h], vn[h], NN) for h in range(HEADS)]
            eg = [jnp.exp(_colsum(g_ref[h, rows, :])) for h in range(HEADS)]
            for h in range(HEADS):
                st_ref[j, h] = s[h]
                s_ref[h] = s[h] * eg[h] + _dot(kg_ref[h, rows, :], vn[h], TN)
                o_ref[h, rows, :] = o[h]
                vn_ref[h, rows, :] = vn[h]

    act = pl.BlockSpec((HEADS, per * CHUNK, LANE), lambda n: (0, n, 0))
    out = jax.ShapeDtypeStruct((HEADS, t, LANE), F32)
    return _pcall(
        body, name=name, out_shape=(out, out, jax.ShapeDtypeStruct((n_chunks, HEADS, LANE, LANE), F32)), grid=(n_chunks // per,),
        in_specs=[act] * 4 + [pl.BlockSpec((per, HEADS, CHUNK, CHUNK), lambda n: (n, 0, 0, 0)), act],
        out_specs=(act, act, pl.BlockSpec((per, HEADS, LANE, LANE), lambda n: (n, 0, 0, 0))),
        scratch_shapes=[pltpu.VMEM((HEADS, LANE, LANE), F32)], semantics=("arbitrary",))(u, w, qg, kg, attn, g)


def _deltanet_bscan(w, qg, kg, attn, g, do, *, name):
    t = w.shape[1]
    n_chunks = t // CHUNK
    per = SCAN_CHUNKS_PER_STEP
    steps = n_chunks // per

    def body(w_ref, qg_ref, kg_ref, attn_ref, g_ref, do_ref, dvn_ref, dsn_ref, ds_ref):
        @pl.when(pl.program_id(0) == 0)
        def _():
            ds_ref[...] = jnp.zeros_like(ds_ref)

        for j in reversed(range(per)):
            rows = _chunk_rows(j)
            dsn = [ds_ref[h] for h in range(HEADS)]
            dov = [do_ref[h, rows, :] for h in range(HEADS)]
            dvn = [_dot(attn_ref[j, h], dov[h], TN) + _dot(kg_ref[h, rows, :], dsn[h], NN) for h in range(HEADS)]
            eg = [jnp.exp(_colsum(g_ref[h, rows, :])) for h in range(HEADS)]
            for h in range(HEADS):
                dsn_ref[j, h] = dsn[h]
                ds_ref[h] = _dot(qg_ref[h, rows, :], dov[h], TN) + eg[h] * dsn[h] - _dot(w_ref[h, rows, :], dvn[h], TN)
                dvn_ref[h, rows, :] = dvn[h]

    act = pl.BlockSpec((HEADS, per * CHUNK, LANE), lambda n: (0, steps - 1 - n, 0))
    return _pcall(
        body, name=name,
        out_shape=(jax.ShapeDtypeStruct((HEADS, t, LANE), F32), jax.ShapeDtypeStruct((n_chunks, HEADS, LANE, LANE), F32)),
        grid=(steps,),
        in_specs=[act] * 3 + [pl.BlockSpec((per, HEADS, CHUNK, CHUNK), lambda n: (steps - 1 - n, 0, 0, 0)), act, act],
        out_specs=(act, pl.BlockSpec((per, HEADS, LANE, LANE), lambda n: (steps - 1 - n, 0, 0, 0))),
        scratch_shapes=[pltpu.VMEM((HEADS, LANE, LANE), F32)], semantics=("arbitrary",))(w, qg, kg, attn, g, do)


def _sum_all(x):
    return jnp.sum(jnp.sum(x, axis=1, keepdims=True), axis=0, keepdims=True)


def _rowsum(x):
    return jnp.sum(x, axis=1, keepdims=True)


def _deltanet_post(qkv, g, beta, tmats, states, dstates, do, dvn, vn, *, name):
    t = qkv.shape[1]
    n_chunks = t // CHUNK
    per = CHUNKS_PER_STEP
    probs = [(j, h) for j in range(per) for h in range(HEADS)]

    def body(qkv_ref, g_ref, b_ref, tm_ref, st_ref, dsn_ref, do_ref, dvn_ref, vn_ref, dqkv_ref, dg_ref, db_ref):
        incl, strict, _ = _chunk_masks()
        ones = jnp.ones((CHUNK, LANE), BF16)
        last_row = lax.broadcasted_iota(jnp.int32, (CHUNK, LANE), 0) == CHUNK - 1
        z = lambda f, *cols: [f(*a) for a in zip(*cols)]
        q = [qkv_ref[h, _chunk_rows(j), :] for j, h in probs]
        k = [qkv_ref[HEADS + h, _chunk_rows(j), :] for j, h in probs]
        v = [qkv_ref[2 * HEADS + h, _chunk_rows(j), :] for j, h in probs]
        bv = [b_ref[h, _chunk_rows(j), :] for j, h in probs]
        dov = [do_ref[h, _chunk_rows(j), :] for j, h in probs]
        dvn_ = [dvn_ref[h, _chunk_rows(j), :] for j, h in probs]
        vn_ = [vn_ref[h, _chunk_rows(j), :] for j, h in probs]
        tm = [tm_ref[j, h] for j, h in probs]
        s = [st_ref[j, h] for j, h in probs]
        dsn = [dsn_ref[j, h] for j, h in probs]
        dec = [_chunk_decay(g_ref[h, _chunk_rows(j), :], incl) for j, h in probs]
        decay = [d[2] for d in dec]
        egc = [jnp.exp(d[0]) for d in dec]
        ekg = [jnp.exp(d[1] - d[0]) for d in dec]
        kb = z(lambda a, b: a * b, k, bv)
        vb = z(lambda a, b: a * b, v, bv)
        kbg = z(lambda a, b: a * b, kb, egc)
        qg = z(lambda a, b: a * b, q, egc)
        kg = z(lambda a, b: a * b, k, ekg)
        kk = z(lambda a, b: _dot(a, b, NT), kb, k)
        qk = z(lambda a, b: _dot(a, b, NT), q, k)
        dattn = z(lambda a, b: jnp.where(incl, _dot(a, b, NT), 0.0), dov, vn_)
        dqg = z(lambda a, b: _dot(a, b, NT), dov, s)
        dkg = z(lambda a, b: _dot(a, b, NT), vn_, dsn)
        dglast = z(lambda a, b, c, d, e: _sum_all(a * b) * jnp.exp(e[1]) + _sum_all(c * d), s, dsn, dkg, kg, dec)
        dw = z(lambda a, b: -_dot(a, b, NT), dvn_, s)
        dtm = z(lambda a, b, c, d: _dot(a, b, NT) + _dot(c, d, NT), dvn_, vb, dw, kbg)
        dvb = z(lambda a, b: _dot(a, b, TN), tm, dvn_)
        dkbg = z(lambda a, b: _dot(a, b, TN), tm, dw)
        dlow = z(lambda a, b: jnp.where(strict, -_dot(_dot(a, b, TN), a, NT), 0.0), tm, dtm)
        dkk = z(lambda a, b: a * b, dlow, decay)
        dqk = z(lambda a, b: a * b, dattn, decay)
        dkb = z(lambda a, b, c, d: _dot(a, b, NN) + c * d, dkk, k, dkbg, egc)
        dk = z(lambda a, b, c, d, e, f, g_, h_: _dot(a, b, TN) + _dot(c, d, TN) + e * f + g_ * h_, dkk, kb, dqk, q, dkg, ekg, dkb, bv)
        dq = z(lambda a, b, c, d: _dot(a, b, NN) + c * d, dqk, k, dqg, egc)
        m = z(lambda a, b, c, d, e: (a * b + c * d) * e, dlow, kk, dattn, qk, decay)
        mcol = [_dot(mh, ones, TN) + _dot(ml, ones, TN) for mh, ml in (_split(a) for a in m)]
        for i, (j, h) in enumerate(probs):
            rows = _chunk_rows(j)
            dqkv_ref[h, rows, :] = dq[i]
            dqkv_ref[HEADS + h, rows, :] = dk[i]
            dqkv_ref[2 * HEADS + h, rows, :] = dvb[i] * bv[i]
            db_ref[h, rows, :] = jnp.broadcast_to(_rowsum(dkb[i] * k[i] + dvb[i] * v[i]), (CHUNK, LANE))
            dgc = (_rowsum(dqg[i] * qg[i] + dkbg[i] * kbg[i] - dkg[i] * kg[i]) + _rowsum(m[i]) - mcol[i]
                   + jnp.where(last_row, dglast[i], 0.0))
            dg_ref[h, rows, :] = _suffix_sum_rows(dgc)

    act = lambda heads: pl.BlockSpec((heads, per * CHUNK, LANE), lambda n: (0, n, 0))
    mat = lambda d: pl.BlockSpec((per, HEADS, d, d), lambda n: (n, 0, 0, 0))
    out = jax.ShapeDtypeStruct((HEADS, t, LANE), F32)
    return _pcall(
        body, name=name, out_shape=(jax.ShapeDtypeStruct((3 * HEADS, t, LANE), F32), out, out), grid=(n_chunks // per,),
        in_specs=[act(3 * HEADS), act(HEADS), act(HEADS), mat(CHUNK), mat(LANE), mat(LANE), act(HEADS), act(HEADS), act(HEADS)],
        out_specs=(act(3 * HEADS), act(HEADS), act(HEADS)), semantics=("parallel",),
        vmem_limit=VMEM_LIMIT)(qkv, g, beta, tmats, states, dstates, do, dvn, vn)


ANY = pl.BlockSpec(memory_space=pl.ANY)
PEERS = N_DEV - 1


def _all_gather(arrays, *, name):
    n = len(arrays)

    def body(*refs):
        ins, outs = refs[:n], refs[n:2 * n]
        send_sems, recv_sems, local_sems = refs[2 * n:]
        x, y, c = lax.axis_index("x"), lax.axis_index("y"), lax.axis_index("c")
        me, sibling = (x, y, c), (x, y, 1 - c)
        chips = [(1 - x, y), (x, 1 - y), (1 - x, 1 - y)]

        def copy(a, k, block, to, src=None):
            dst = outs[a].at[4 * block[0] + 2 * block[1] + block[2]]
            return pltpu.make_async_remote_copy(src_ref=dst if src is None else src, dst_ref=dst, send_sem=send_sems.at[a * PEERS + k],
                                                recv_sem=recv_sems.at[a * PEERS + k], device_id=to, device_id_type=MESH)

        local = [pltpu.make_async_copy(ins[a], outs[a].at[4 * x + 2 * y + c], local_sems.at[a]) for a in range(n)]
        for cp in local:
            cp.start()
        first = []
        for a in range(n):
            first.append(copy(a, 0, me, sibling, src=ins[a]))
            first += [copy(a, 1 + j, me, (*chip, c), src=ins[a]) for j, chip in enumerate(chips)]
        for cp in first:
            cp.start()
        passed = []
        for a in range(n):
            for j, chip in enumerate(chips):
                copy(a, 1 + j, (*chip, c), me).wait_recv()
                fwd = copy(a, 4 + j, (*chip, c), sibling)
                fwd.start()
                passed.append(fwd)
        for a in range(n):
            copy(a, 0, sibling, me).wait_recv()
            for j, chip in enumerate(chips):
                copy(a, 4 + j, (*chip, 1 - c), me).wait_recv()
        for cp in first + passed:
            cp.wait_send()
        for cp in local:
            cp.wait()

    return _pcall(body, name=name, out_shape=tuple(jax.ShapeDtypeStruct((N_DEV,) + a.shape, a.dtype) for a in arrays),
                  in_specs=[ANY] * n, out_specs=(ANY,) * n,
                  scratch_shapes=[pltpu.SemaphoreType.DMA((n * PEERS,)), pltpu.SemaphoreType.DMA((n * PEERS,)),
                                  pltpu.SemaphoreType.DMA((n,))])(*arrays)


CHIPS = 4


def _pair_exchange(arrays, *, name):
    n = len(arrays)

    def body(*refs):
        ins, outs = refs[:n], refs[n:2 * n]
        send_sems, recv_sems = refs[2 * n:]
        x, y, c = lax.axis_index("x"), lax.axis_index("y"), lax.axis_index("c")
        copies = []
        for a in range(n):
            for q in range(CHIPS):
                cp = pltpu.make_async_remote_copy(src_ref=ins[a].at[2 * q + 1 - c], dst_ref=outs[a].at[q],
                                                  send_sem=send_sems.at[a * CHIPS + q], recv_sem=recv_sems.at[a * CHIPS + q],
                                                  device_id=(x, y, 1 - c), device_id_type=MESH)
                cp.start()
                copies.append(cp)
        for cp in copies:
            cp.wait()

    return _pcall(body, name=name, out_shape=tuple(jax.ShapeDtypeStruct((CHIPS,) + a.shape[1:], a.dtype) for a in arrays),
                  in_specs=[ANY] * n, out_specs=(ANY,) * n,
                  scratch_shapes=[pltpu.SemaphoreType.DMA((n * CHIPS,)), pltpu.SemaphoreType.DMA((n * CHIPS,))])(*arrays)


def _pair_add(blocks, theirs, *, name):
    _, r, c_ = blocks.shape
    tr = _tile(r, 512, 16)

    def body(mine_ref, theirs_ref, o_ref):
        core = lax.axis_index("c")
        own = jnp.where(core == 0, mine_ref[0, 0].astype(F32), mine_ref[0, 1].astype(F32))
        o_ref[0] = (own + theirs_ref[0].astype(F32)).astype(o_ref.dtype)

    spec = pl.BlockSpec((1, tr, c_), lambda q, i: (q, i, 0))
    return _pcall(body, name=name, out_shape=jax.ShapeDtypeStruct(theirs.shape, theirs.dtype), grid=(CHIPS, r // tr),
                  in_specs=[pl.BlockSpec((1, 2, tr, c_), lambda q, i: (q, 0, i, 0)), spec], out_specs=spec,
                  semantics=("parallel", "parallel"), vmem_limit=VMEM_LIMIT)(blocks.reshape(CHIPS, 2, r, c_), theirs)


HBM = pl.BlockSpec(memory_space=pltpu.HBM)
SEM = pl.BlockSpec(memory_space=pltpu.SEMAPHORE)
EFFECT = pltpu.SideEffectType.DATAFLOW_SIDE_EFFECTING


GATHER, CHIP_SCATTER = "gather", "chip_scatter"
PEERS_OF = {GATHER: N_DEV - 1, CHIP_SCATTER: CHIPS - 1}


def _direct_copies(srcs, lands, send_sems, recv_sems, local_sems, kind):
    x, y, c = lax.axis_index("x"), lax.axis_index("y"), lax.axis_index("c")
    peers = PEERS_OF[kind]
    copies = []
    for a, (src, land) in enumerate(zip(srcs, lands)):
        if kind == GATHER:
            mine = 4 * x + 2 * y + c
            copies.append(pltpu.make_async_copy(src, land.at[mine], local_sems.at[a]))
        else:
            mine = 2 * x + y
            copies.append(pltpu.make_async_copy(src.at[mine], land.at[mine], local_sems.at[a]))
        for k in range(1, peers + 1):
            bits = k if kind == GATHER else 2 * k
            px = 1 - x if bits & 4 else x
            py = 1 - y if bits & 2 else y
            pc = 1 - c if bits & 1 else c
            copies.append(pltpu.make_async_remote_copy(
                src_ref=src if kind == GATHER else src.at[2 * px + py], dst_ref=land.at[mine],
                send_sem=send_sems.at[a * peers + k - 1], recv_sem=recv_sems.at[a * peers + k - 1],
                device_id=(px, py, pc), device_id_type=MESH))
    return copies


def _exchange_start(groups, kind, *, name, after=None):
    srcs = [s for group in groups for s in group]
    n = len(srcs)
    sizes = [len(group) for group in groups]
    starts = [sum(sizes[:g]) for g in range(len(groups))]
    land_shapes = [(N_DEV,) + s.shape if kind == GATHER else s.shape for s in srcs]
    peers = PEERS_OF[kind]
    extra = [] if after is None else [after]

    def body(*refs):
        srcs_, lands = refs[:n], refs[n:2 * n]
        token = refs[-1]
        sem_refs = refs[2 * n + len(extra):]
        for g, (at, size) in enumerate(zip(starts, sizes)):
            send_sems, recv_sems, local_sems = sem_refs[3 * g:3 * g + 3]
            for cp in _direct_copies(srcs_[at:at + size], lands[at:at + size], send_sems, recv_sems, local_sems, kind):
                cp.start()
        token[...] = jnp.zeros_like(token)

    sems = tuple(t for size in sizes for t in (pltpu.SemaphoreType.DMA((size * peers,)), pltpu.SemaphoreType.DMA((size * peers,)),
                                               pltpu.SemaphoreType.DMA((size,))))
    thru = tuple(pltpu.HBM(s.shape, s.dtype) for s in srcs) + tuple(pltpu.HBM(shp, s.dtype) for shp, s in zip(land_shapes, srcs))
    ins = [pltpu.with_memory_space_constraint(s, pltpu.HBM) for s in srcs]
    ins += [pltpu.with_memory_space_constraint(lax.empty(shp, s.dtype), pltpu.HBM) for shp, s in zip(land_shapes, srcs)]
    out = pl.pallas_call(
        body, name=name, out_shape=sems + thru + (jax.ShapeDtypeStruct((SUBLANE, LANE), F32),),
        in_specs=[HBM] * (2 * n) + [ANY] * len(extra),
        out_specs=(SEM,) * len(sems) + (HBM,) * (2 * n) + (pl.BlockSpec(memory_space=pltpu.VMEM),),
        input_output_aliases={i: len(sems) + i for i in range(2 * n)},
        compiler_params=pltpu.CompilerParams(has_side_effects=EFFECT))(*ins, *extra)
    arrays = out[len(sems):-1]
    started = [tuple(out[3 * g:3 * g + 3]) + tuple(arrays[at:at + size]) + tuple(arrays[n + at:n + at + size])
               for g, (at, size) in enumerate(zip(starts, sizes))]
    return started, out[-1]


def _exchange_wait(started, after, kind, *, name):
    n = (len(started) - 3) // 2
    sems, arrays = started[:3], started[3:]

    def body(*refs):
        srcs_, lands = refs[:n], refs[n:2 * n]
        send_sems, recv_sems, local_sems = refs[2 * n:2 * n + 3]
        for cp in _direct_copies(srcs_, lands, send_sems, recv_sems, local_sems, kind):
            cp.wait()

    out = pl.pallas_call(
        body, name=name, out_shape=tuple(pltpu.HBM(a.shape, a.dtype) for a in arrays),
        in_specs=[HBM] * (2 * n) + [SEM] * 3 + [ANY], out_specs=(HBM,) * (2 * n),
        input_output_aliases={i: i for i in range(2 * n)},
        compiler_params=pltpu.CompilerParams(has_side_effects=EFFECT))(*arrays, *sems, after)
    return out[n:]


def _adamw_reduce(w, parts, m, v, *, name):
    layers, r, c = w.shape
    assert len(parts) == layers
    senders = parts[0].shape[0]
    tr = _tile(r, 512, 16)
    tiles = r // tr
    bc1 = 1.0 - ADAM_B1 ** ADAM_STEP
    bc2 = 1.0 - ADAM_B2 ** ADAM_STEP

    def body(w_ref, *rest):
        p_refs = rest[:layers]
        m_ref, v_ref, g_ref, d_ref, nm_ref, nv_ref = rest[layers:]

        def update(p_ref):
            g = p_ref[0, :, pl.ds(0, c)].astype(F32)
            for s in range(1, senders):
                g = g + p_ref[s, :, pl.ds(0, c)].astype(F32)
            nm = ADAM_B1 * m_ref[0] + (1.0 - ADAM_B1) * g
            nv = ADAM_B2 * v_ref[0] + (1.0 - ADAM_B2) * (g * g)
            g_ref[0] = g
            nm_ref[0] = nm
            nv_ref[0] = nv
            d_ref[0] = -ADAM_LR * ((nm / bc1) / (jnp.sqrt(nv / bc2) + ADAM_EPS) + ADAM_WD * w_ref[0])

        for layer in range(layers):
            pl.when(pl.program_id(0) == layer)(functools.partial(update, p_refs[layer]))

    def part_spec(layer, shape):
        rest = 0 if layer > 0 else tiles - 1
        return pl.BlockSpec((senders, tr, shape[2]), lambda l, i: (0, jnp.where(l == layer, i, rest), 0))

    spec = pl.BlockSpec((1, tr, c), lambda l, i: (l, i, 0))
    out = jax.ShapeDtypeStruct((layers, r, c), F32)
    return _pcall(body, name=name, out_shape=(out,) * 4, grid=(layers, tiles),
                  in_specs=[spec] + [part_spec(layer, p.shape) for layer, p in enumerate(parts)] + [spec, spec],
                  out_specs=(spec,) * 4, semantics=("arbitrary", "arbitrary"), vmem_limit=VMEM_LIMIT)(w, *parts, m, v)


def _pool_windows():
    return jnp.repeat(jnp.asarray(POOL_WINDOWS, F32), POOL_DIM // len(POOL_WINDOWS))[None, :]


def _block_diag_pairs(pool_w):
    z = jnp.zeros_like(pool_w[0])
    return jnp.stack([jnp.block([[pool_w[2 * b], z], [z, pool_w[2 * b + 1]]]) for b in range(2)])


def _pad_lanes(vec):
    return jnp.zeros((1, LANE), F32).at[0, :vec.shape[0]].set(vec)


FF_SHARD = D_FF // N_DEV
FF_BLOCK = 384
D_FF_PAD = N_DEV * FF_BLOCK


def _layer_fwd(x, p_i, wt, fetch):
    wt = {**wt, **fetch(0, x)}
    h1 = _rmsnorm_fwd(x, wt["norm1_g"], name="rmsnorm_fwd")
    proj = _matmul(h1, wt["w_in"], "nn", name="mm_in")
    qkv = _qkv_prep_fwd(proj, wt["conv_qkv"], name="qkv_prep_fwd")
    g, beta = _gates_fwd(proj, wt["a_log"], wt["dt_bias"], name="gates_fwd")
    u, w, qg, kg, attn, tmats = _deltanet_prep(qkv, g, beta, name="deltanet_prep")
    o, vn, states = _deltanet_scan(u, w, qg, kg, attn, g, name="deltanet_scan")
    o_a = _apost_fwd(o, proj, wt["onorm_g"], name="apost_fwd")
    o_b = _pool_fwd(proj, wt["pool_win"], wt["pool_wbd"], wt["pool_scale"], name="pool_fwd")
    o_c = _sconv_fwd(proj, wt["sconv_w"], name="sconv_fwd")
    mixed = jnp.concatenate([o_a, o_b, o_c], axis=1)
    wt.update(fetch(1, mixed))
    x1 = _matmul(mixed, wt["w_out"], "nn", res=x, name="mm_out")
    h2 = _rmsnorm_fwd(x1, wt["norm2_g"], name="rmsnorm_fwd")
    wt.update(fetch(2, h2))
    ff, gate, up = _swiglu_fwd(h2, wt["w_gate"], wt["w_up"], name="swiglu_fwd")
    wt.update(fetch(3, ff))
    x2 = _matmul(ff, wt["w_down"], "nn", res=x1, name="mm_down")
    wt.update(fetch(4, x2))
    pgl = _matmul(x2, wt["ple_gate"], "nn", name="mm_pleg")
    pp = _matmul(p_i, wt["ple_proj"], "nn", b_blocked=True, name="mm_plep")
    x3 = _ple_fwd(x2, pgl, pp, name="ple_fwd")
    saved = dict(x=x, h1=h1, proj=proj, qkv=qkv, g=g, beta=beta, o=o, states=states, tmats=tmats, mixed=mixed, x1=x1, h2=h2,
                 gate=gate, up=up, ff=ff, x2=x2, pgl=pgl, pp=pp, p=p_i, w=w, qg=qg, kg=kg, attn=attn, vn=vn, wt=wt)
    return x3, saved


def _col_blocks(g):
    a = g.shape[0]
    return jnp.transpose(g.reshape(a, N_DEV, -1), (1, 0, 2))


def _cols_joined(blocks):
    return jnp.transpose(blocks, (1, 0, 2)).reshape(blocks.shape[1], -1)


def _layer_bwd(dx3, sv, emit, after=None):
    gr, big = {}, {}
    wt = sv["wt"]
    rows = D_MODEL // N_DEV
    dpgl, dpp = _ple_bwd(dx3, sv["pgl"], sv["pp"], name="ple_bwd", after=after)
    big["ple_proj"] = _matmul(sv["p"], dpp, "tn", out_blocked=(N_DEV, rows), out_dtype=BF16, name="mm_dplep")
    big["ple_gate"] = _matmul(sv["x2"], dpgl, "tn", out_dtype=BF16, name="mm_dpleg").reshape(N_DEV, rows, D_MODEL)
    dx2 = _matmul(dpgl, wt["ple_gate"], "nt", res=dx3, name="mm_dx2")
    big["w_down"] = _matmul(sv["ff"], dx2, "tn", out_dtype=BF16, name="mm_ddown").reshape(N_DEV, FF_BLOCK, D_MODEL)
    dgate, dup = _swiglu_bwd(dx2, wt["w_down"], sv["gate"], sv["up"], name="swiglu_bwd", after=emit(0, big))
    big["w_gate"] = _matmul(sv["h2"], dgate, "tn", out_blocked=(N_DEV, FF_BLOCK), out_dtype=BF16, name="mm_dgate")
    big["w_up"] = _matmul(sv["h2"], dup, "tn", out_blocked=(N_DEV, FF_BLOCK), out_dtype=BF16, name="mm_dup")
    dh2 = _matmul(dgate, wt["w_gate"], "nt", b_blocked=True, name="mm_dh2_gate")
    dh2 = _matmul(dup, wt["w_up"], "nt", b_blocked=True, res=dh2, name="mm_dh2_up")
    dx1, gr["norm2_g"] = _rmsnorm_bwd(sv["x1"], wt["norm2_g"], dh2, dx2, name="rmsnorm_bwd")
    big["w_out"] = _matmul(sv["mixed"], dx1, "tn", out_dtype=BF16, name="mm_dout").reshape(N_DEV, rows, D_MODEL)
    dmixed = _matmul(dx1, wt["w_out"], "nt", name="mm_dmixed", after=emit(1, big))
    proj = sv["proj"]
    dcb, dcc, dch, dsconv = _sconv_bwd(proj, wt["sconv_w"], dmixed, name="sconv_bwd")
    big["sconv_w"] = _col_blocks(dsconv)
    dhp, dwbd, gr["pool_scale"] = _pool_bwd(proj, wt["pool_win"], wt["pool_wbd"], wt["pool_scale"], dmixed, name="pool_bwd")
    half = LANE // 2
    gr["pool_w"] = jnp.stack([dwbd[0, :half, :half], dwbd[0, half:, half:], dwbd[1, :half, :half], dwbd[1, half:, half:]])
    do, dz, gr["onorm_g"] = _apost_bwd(sv["o"], proj, wt["onorm_g"], dmixed, name="apost_bwd")
    dvn, dstates = _deltanet_bscan(sv["w"], sv["qg"], sv["kg"], sv["attn"], sv["g"], do, name="deltanet_bscan")
    dqkv_h, dg, dbeta = _deltanet_post(sv["qkv"], sv["g"], sv["beta"], sv["tmats"], sv["states"], dstates, do, dvn, sv["vn"],
                                       name="deltanet_post")
    dab, dalog, ddtb = _gates_bwd(proj, wt["a_log"], wt["dt_bias"], dg, dbeta, name="gates_bwd")
    gr["a_log"], gr["dt_bias"] = dalog[0, :HEADS], ddtb[0, :HEADS]
    dqkv, dconv = _qkv_prep_bwd(proj, wt["conv_qkv"], dqkv_h, name="qkv_prep_bwd")
    big["conv_qkv"] = _col_blocks(dconv)
    dproj = jnp.concatenate([dqkv, dz, dab, dhp, dcb, dcc, dch], axis=1)
    dwin = _matmul(sv["h1"], dproj, "tn", out_dtype=BF16, name="mm_din")
    big["w_in"] = _col_blocks(jnp.concatenate([dwin[:, :AB_COL + 2 * HEADS], dwin[:, AB_COL + LANE:]], axis=1))
    dh1 = _matmul(dproj, wt["w_in"], "nt", name="mm_dh1", after=emit(2, big))
    dx, gr["norm1_g"] = _rmsnorm_bwd(sv["x"], wt["norm1_g"], dh1, dx1, name="rmsnorm_bwd")
    return dx, gr


FETCH_GROUPS = (("w_in", "conv_qkv", "sconv_w"), ("w_out",), ("w_gate", "w_up"), ("w_down",), ("ple_gate", "ple_proj"))
EMIT_GROUPS = (("ple_proj", "ple_gate", "w_down"), ("w_gate", "w_up", "w_out"), ("w_in", "conv_qkv", "sconv_w"))


def _small_weights(w, i):
    return dict(
        norm1_g=w["norm1_g"][i][None], norm2_g=w["norm2_g"][i][None], onorm_g=w["onorm_g"][i][None],
        a_log=_pad_lanes(w["a_log"][i]), dt_bias=_pad_lanes(w["dt_bias"][i]),
        pool_scale=w["pool_scale"][i][None], pool_win=_pool_windows(), pool_wbd=_block_diag_pairs(w["pool_w"][i]))


def _as_read(name, gathered):
    if name == "w_in":
        w_in = _cols_joined(gathered)
        return jnp.concatenate([w_in[:, :AB_COL + 2 * HEADS], jnp.zeros((D_MODEL, LANE - 2 * HEADS), BF16),
                                w_in[:, AB_COL + 2 * HEADS:]], axis=1)
    if name in ("conv_qkv", "sconv_w"):
        return _cols_joined(gathered)
    if name in ("w_gate", "w_up", "ple_proj"):
        return gathered
    return gathered.reshape(-1, D_MODEL)


def _layer_weights(gathered, w, i):
    return {**_small_weights(w, i), **{k: _as_read(k, g) for k, g in gathered.items()}}


def _local_step(x, p, target, layers, final_g):
    saved = []
    h = x
    for i in range(DEPTH):
        replicated = {k: v for k, v in layers[i].items() if k not in SHARDED}
        h, sv = _layer_fwd(h, p[i], replicated, lambda group, after, i=i: {k: layers[i][k] for k in FETCH_GROUPS[group]})
        saved.append(sv)
    dx, dgf, loss = _loss_head(h, final_g, target, name="loss_head")
    big, small = [{} for _ in range(DEPTH)], [None] * DEPTH
    for i in reversed(range(DEPTH)):
        dx, small[i] = _layer_bwd(dx, saved[i], lambda group, blocks, i=i: big[i].update({k: blocks[k] for k in EMIT_GROUPS[group]}))
    return loss, dx, big, small, dgf


SHARDED = ("w_in", "w_gate", "w_up", "w_down", "w_out", "ple_gate", "ple_proj", "conv_qkv", "sconv_w")
SMALL = ("norm1_g", "a_log", "dt_bias", "onorm_g", "pool_w", "pool_scale", "norm2_g", "final_g")
SLAB_COLS = 1024


def _payload(name, shard):
    if name in ("conv_qkv", "sconv_w"):
        return shard
    out = shard.astype(BF16)
    if name in ("w_gate", "w_up"):
        out = jnp.pad(out, ((0, 0), (0, FF_BLOCK - FF_SHARD)))
    if name == "w_down":
        out = jnp.pad(out, ((0, FF_BLOCK - FF_SHARD), (0, 0)))
    return out


def _slab_rows(shape):
    size = 1
    for s in shape:
        size *= s
    return SUBLANE * -(-size // (SUBLANE * SLAB_COLS))


def _pack_slab(parts, extra_row):
    rows = []
    for name in SMALL:
        flat = parts[name].reshape(-1)
        nrow = _slab_rows(parts[name].shape)
        rows.append(jnp.pad(flat, (0, nrow * SLAB_COLS - flat.shape[0])).reshape(nrow, SLAB_COLS))
    rows.append(jnp.pad(extra_row, ((0, SUBLANE - 1), (0, 0))))
    return jnp.concatenate(rows, axis=0)


def _unpack_slab(slab, shapes):
    out, row = {}, 0
    for name in SMALL:
        size = 1
        for s in shapes[name]:
            size *= s
        out[name] = slab[row:row + _slab_rows(shapes[name])].reshape(-1)[:size].reshape(shapes[name])
        row += _slab_rows(shapes[name])
    return out, row


def kernel(x, p, norm1_g, w_in, conv_qkv, a_log, dt_bias, onorm_g, pool_w, pool_scale, sconv_w, w_out, norm2_g, w_gate, w_up, w_down, ple_proj, ple_gate, final_g, loss_target, m_norm1_g, m_w_in, m_conv_qkv, m_a_log, m_dt_bias, m_onorm_g, m_pool_w, m_pool_scale, m_sconv_w, m_w_out, m_norm2_g, m_w_gate, m_w_up, m_w_down, m_ple_proj, m_ple_gate, m_final_g, v_norm1_g, v_w_in, v_conv_qkv, v_a_log, v_dt_bias, v_onorm_g, v_pool_w, v_pool_scale, v_sconv_w, v_w_out, v_norm2_g, v_w_gate, v_w_up, v_w_down, v_ple_proj, v_ple_gate, v_final_g):
    names = ["norm1_g", "w_in", "conv_qkv", "a_log", "dt_bias", "onorm_g", "pool_w", "pool_scale", "sconv_w", "w_out", "norm2_g",
             "w_gate", "w_up", "w_down", "ple_proj", "ple_gate", "final_g"]
    w = dict(zip(names, [norm1_g, w_in, conv_qkv, a_log, dt_bias, onorm_g, pool_w, pool_scale, sconv_w, w_out, norm2_g, w_gate, w_up,
                         w_down, ple_proj, ple_gate, final_g]))
    m = dict(zip(names, [m_norm1_g, m_w_in, m_conv_qkv, m_a_log, m_dt_bias, m_onorm_g, m_pool_w, m_pool_scale, m_sconv_w, m_w_out,
                         m_norm2_g, m_w_gate, m_w_up, m_w_down, m_ple_proj, m_ple_gate, m_final_g]))
    v = dict(zip(names, [v_norm1_g, v_w_in, v_conv_qkv, v_a_log, v_dt_bias, v_onorm_g, v_pool_w, v_pool_scale, v_sconv_w, v_w_out,
                         v_norm2_g, v_w_gate, v_w_up, v_w_down, v_ple_proj, v_ple_gate, v_final_g]))

    gathered = dict(zip(SHARDED, _all_gather([_payload(k, w[k][0]) for k in SHARDED], name="all_gather_weights")))
    (flying,), token = _exchange_start([[_payload(k, w[k][1]) for k in SHARDED]], GATHER, name="gather_start")
    replicated = [_small_weights(w, i) for i in range(DEPTH)]
    replicated[0]["norm1_g"] = replicated[0]["norm1_g"] + token[0, 0]

    def fetch(i, group, after):
        if i == 1 and group == 0:
            gathered.update(zip(SHARDED, _exchange_wait(flying, after, GATHER, name="gather_wait")))
        return {k: _as_read(k, gathered[k]) for k in FETCH_GROUPS[group]}

    def reduce_scatter_start(members, blocks, tag):
        mine = [blocks[k] for k in members]
        theirs = _pair_exchange(mine, name="pair_exchange")
        sums = [_pair_add(a, b, name="pair_add") for a, b in zip(mine, theirs)]
        (started,), token = _exchange_start([sums], CHIP_SCATTER, name="exchange_start_" + tag)
        return started, token

    h, saved0 = _layer_fwd(x[0], p[0, 0], replicated[0], functools.partial(fetch, 0))
    h, saved1 = _layer_fwd(h, p[1, 0], replicated[1], functools.partial(fetch, 1))
    dx, dgf, loss_part = _loss_head(h, final_g[None], loss_target[0], name="loss_head")
    small, big1, flying0 = [None] * DEPTH, {}, []
    dx, small[1] = _layer_bwd(dx, saved1, lambda group, blocks: big1.update({k: blocks[k] for k in EMIT_GROUPS[group]}))
    flying1, token = reduce_scatter_start(SHARDED, big1, "1")

    def emit(group, blocks):
        started, token = reduce_scatter_start(EMIT_GROUPS[group], blocks, f"0_{group}")
        flying0.append(started)
        return token

    dx, small[0] = _layer_bwd(dx, saved0, emit, after=token)
    received = [{}, dict(zip(SHARDED, _exchange_wait(flying1, dx, CHIP_SCATTER, name="exchange_wait_1")))]
    for group, members in enumerate(EMIT_GROUPS):
        received[0].update(zip(members, _exchange_wait(flying0[group], dx, CHIP_SCATTER, name=f"exchange_wait_0_{group}")))

    grads = {k: jnp.stack([small[i][k] for i in range(DEPTH)]) for k in small[0]}
    grads = {k: g[:, 0] if k in ("norm1_g", "norm2_g", "onorm_g", "pool_scale") else g for k, g in grads.items()}
    grads["final_g"] = dgf[0]
    loss_row = jnp.pad(loss_part, ((0, 0), (0, SLAB_COLS - LANE)))
    (small_flying,), _ = _exchange_start([[_pack_slab(grads, loss_row)]], GATHER, name="small_gather_start")

    out_g, out_d, out_m, out_v = {}, {}, {}, {}
    for k in SHARDED:
        out_g[k], out_d[k], out_m[k], out_v[k] = _adamw_reduce(w[k], [received[i][k] for i in range(DEPTH)], m[k], v[k],
                                                                name="adamw_" + k)
    behind_all = jnp.stack([out_v[k][0, 0, 0] for k in SHARDED])
    (small_parts,) = _exchange_wait(small_flying, behind_all, GATHER, name="small_gather_wait")
    zero_row = jnp.zeros((1, SLAB_COLS), F32)
    slabs = _adamw_reduce(_pack_slab(w, zero_row)[None], [small_parts], _pack_slab(m, zero_row)[None],
                          _pack_slab(v, zero_row)[None], name="adamw_small")
    slabs = [s[0] for s in slabs]
    shapes = {k: w[k].shape for k in SMALL}
    for dst, slab in zip((out_g, out_d, out_m, out_v), slabs):
        vals, _ = _unpack_slab(slab, shapes)
        dst.update(vals)
    _, loss_at = _unpack_slab(slabs[0], shapes)
    loss = slabs[0][loss_at, 0]

    return (loss, dx[None], *[out_g[k] for k in names], *[out_d[k] for k in names], *[out_m[k] for k in names],
            *[out_v[k] for k in names])
```

```python
import functools

import jax
import jax.numpy as jnp
from jax import lax
from jax.experimental import pallas as pl
from jax.experimental.pallas import tpu as pltpu

F32 = jnp.float32
BF16 = jnp.bfloat16

D_MODEL = 1024
DEPTH = 2
PLE_DIM = 256
EPS = 1e-6
HEAD_DIM = 128
HEADS = 4
A_DIM = HEADS * HEAD_DIM
QKV_TAPS = 4
CHUNK = 64
POOL_WINDOWS = (2, 4, 8, 16)
POOL_DIM = 256
CONV_DIM = 256
CONV_TAPS = 3
D_FF = 2816
D_IN = 3080
D_IN_PAD = 3200
AB_COL = 2048
N_DEV = 8

ADAM_LR = 0.001
ADAM_B1 = 0.9
ADAM_B2 = 0.999
ADAM_EPS = 1e-08
ADAM_WD = 0.01
ADAM_STEP = 10

LANE = 128
SUBLANE = 8
VMEM_BYTES_V7X = 64 * 1024 * 1024
VMEM_LIMIT = 48 * 1024 * 1024

_HI = lax.Precision.HIGHEST
NN = ((1,), (0,))
NT = ((1,), (1,))
TN = ((0,), (0,))
MESH = pl.DeviceIdType.MESH


def _dot(a, b, dims, hi=False):
    if hi:
        return lax.dot_general(a, b, (dims, ((), ())), precision=_HI, preferred_element_type=F32)
    return lax.dot_general(a.astype(BF16), b.astype(BF16), (dims, ((), ())), preferred_element_type=F32)


def _pcall(body, *, name, out_shape, grid=(), in_specs=None, out_specs=None, scratch_shapes=(), semantics=None,
           vmem_limit=None, after=None, **kw):
    params = {}
    if semantics is not None:
        params["dimension_semantics"] = semantics
    if vmem_limit is not None:
        params["vmem_limit_bytes"] = vmem_limit
    if after is not None:
        n_in, inner = len(in_specs), body
        body = lambda *refs: inner(*refs[:n_in], *refs[n_in + 1:])
        in_specs = list(in_specs) + [pl.BlockSpec(after.shape, lambda *_: (0,) * after.ndim)]
    call = pl.pallas_call(
        body, name=name, out_shape=out_shape, grid=grid, in_specs=in_specs, out_specs=out_specs,
        scratch_shapes=list(scratch_shapes), compiler_params=pltpu.CompilerParams(**params), **kw)
    return call if after is None else (lambda *args: call(*args, after))


def _sigmoid(x):
    return 1.0 / (1.0 + jnp.exp(-x))


def _softplus(x):
    return jnp.maximum(x, 0.0) + jnp.log(1.0 + jnp.exp(-jnp.abs(x)))


def _tile(n, cap, mult):
    if n <= cap:
        return n
    best = None
    for t in range(mult, cap + 1, mult):
        if n % t == 0:
            best = t
    assert best is not None, (n, cap, mult)
    return best


ROWS_PER_STEP = 512
NARROW_RESULT = 1024
COLS_PER_DOT = 640


def _resident(weight):
    return pl.BlockSpec(weight.shape, lambda i: (0,) * weight.ndim, pipeline_mode=pl.Buffered(1))


def _matmul_rows(a, b, mode, *, name, res=None, out_dtype=F32, b_blocked=False, after=None):
    m, k = a.shape
    if b_blocked:
        nb, _, bw = b.shape
        n = nb * bw if mode == "nn" else b.shape[1]
    else:
        n = b.shape[1] if mode == "nn" else b.shape[0]
    tm = _tile(m, ROWS_PER_STEP if n > NARROW_RESULT else 2 * ROWS_PER_STEP, 16)
    cn = bw if (b_blocked and mode == "nn") else _tile(n, COLS_PER_DOT, LANE)
    has_res = res is not None

    def body(*refs):
        a_ref, b_ref = refs[0], refs[1]
        res_ref = refs[2] if has_res else None
        o_ref = refs[2 + has_res]
        if not (b_blocked and mode == "nt"):
            av = a_ref[...].astype(BF16)
        for j in range(n // cn):
            cols = pl.ds(j * cn, cn)
            if mode == "nn":
                part = _dot(av, b_ref[j] if b_blocked else b_ref[:, cols], NN)
            elif not b_blocked:
                part = _dot(av, b_ref[cols, :], NT)
            else:
                part = None
                for s in range(nb):
                    term = _dot(a_ref[:, pl.ds(s * bw, bw)], b_ref[s, cols, :], NT)
                    part = term if part is None else part + term
            if has_res:
                part = part + res_ref[:, cols]
            o_ref[:, cols] = part.astype(o_ref.dtype)

    row = lambda width: pl.BlockSpec((tm, width), lambda i: (i, 0))
    whole = _resident(b)
    ins = [a, b] + ([res] if has_res else [])
    specs = [row(k), whole] + ([row(n)] if has_res else [])
    return _pcall(body, name=name, out_shape=jax.ShapeDtypeStruct((m, n), out_dtype), grid=(m // tm,), in_specs=specs,
                  out_specs=row(n), semantics=("parallel",), vmem_limit=VMEM_LIMIT, after=after)(*ins)


def _matmul(a, b, mode, *, name, res=None, out_dtype=F32, b_blocked=False, out_blocked=None, after=None):
    if mode != "tn":
        return _matmul_rows(a, b, mode, name=name, res=res, out_dtype=out_dtype, b_blocked=b_blocked, after=after)
    assert res is None and not b_blocked and after is None
    (t, m), (t2, n) = a.shape, b.shape
    assert t == t2, (a.shape, b.shape)
    tm = _tile(m, 1024, LANE)
    tn = _tile(n, COLS_PER_DOT, LANE)
    if out_blocked is not None:
        assert out_blocked[0] * out_blocked[1] == n
        tn = out_blocked[1]

    def body(a_ref, b_ref, o_ref):
        part = _dot(a_ref[...], b_ref[...], TN).astype(o_ref.dtype)
        if out_blocked is None:
            o_ref[...] = part
        else:
            o_ref[0] = part

    o_spec = (pl.BlockSpec((tm, tn), lambda i, j: (i, j)) if out_blocked is None
              else pl.BlockSpec((1, tm, tn), lambda i, j: (j, i, 0)))
    o_shape = (m, n) if out_blocked is None else (out_blocked[0], m, out_blocked[1])
    return _pcall(body, name=name, out_shape=jax.ShapeDtypeStruct(o_shape, out_dtype), grid=(m // tm, n // tn),
                  in_specs=[pl.BlockSpec((t, tm), lambda i, j: (0, i)), pl.BlockSpec((t, tn), lambda i, j: (0, j))],
                  out_specs=o_spec, semantics=("parallel", "parallel"), vmem_limit=VMEM_LIMIT)(a, b)


ROW_TILE = 512


def _rows(t, width, idx=0):
    return pl.BlockSpec((ROW_TILE, width), lambda i: (i, idx))


def _vec(width):
    return pl.BlockSpec((1, width), lambda i: (0, 0))


def _rmsnorm_fwd(x, g, *, name):
    t, d = x.shape

    def body(x_ref, g_ref, h_ref):
        xv = x_ref[...]
        r = lax.rsqrt(jnp.mean(xv * xv, axis=-1, keepdims=True) + EPS)
        h_ref[...] = (xv * r * g_ref[...]).astype(BF16)

    return _pcall(body, name=name, out_shape=jax.ShapeDtypeStruct((t, d), BF16), grid=(t // ROW_TILE,),
                  in_specs=[_rows(t, d), _vec(d)], out_specs=_rows(t, d), semantics=("parallel",))(x, g)


def _rmsnorm_bwd(x, g, dh, dres, *, name):
    t, d = x.shape

    def body(x_ref, g_ref, dh_ref, dres_ref, dx_ref, dg_ref):
        xv = x_ref[...]
        r = lax.rsqrt(jnp.mean(xv * xv, axis=-1, keepdims=True) + EPS)
        xhat = xv * r
        dhv = dh_ref[...].astype(F32)
        dhg = dhv * g_ref[...]
        dx_ref[...] = dres_ref[...] + r * (dhg - xhat * jnp.mean(dhg * xhat, axis=-1, keepdims=True))
        part = jnp.sum(dhv * xhat, axis=0, keepdims=True)

        @pl.when(pl.program_id(0) == 0)
        def _():
            dg_ref[...] = part

        @pl.when(pl.program_id(0) > 0)
        def _():
            dg_ref[...] += part

    return _pcall(body, name=name, out_shape=(jax.ShapeDtypeStruct((t, d), F32), jax.ShapeDtypeStruct((1, d), F32)),
                  grid=(t // ROW_TILE,), in_specs=[_rows(t, d), _vec(d), _rows(t, d), _rows(t, d)],
                  out_specs=(_rows(t, d), _vec(d)), semantics=("arbitrary",))(x, g, dh, dres)


def _swiglu_fwd(h, w_gate, w_up, *, name):
    t, k = h.shape
    nb, _, bw = w_gate.shape
    tm = _tile(t, ROWS_PER_STEP, 16)

    def body(h_ref, wg_ref, wu_ref, ff_ref, gate_ref, up_ref):
        hv = h_ref[...]
        for j in range(nb):
            cols = pl.ds(j * bw, bw)
            gv = _dot(hv, wg_ref[j], NN)
            uv = _dot(hv, wu_ref[j], NN)
            gate_ref[:, cols] = gv.astype(BF16)
            up_ref[:, cols] = uv.astype(BF16)
            ff_ref[:, cols] = (gv * _sigmoid(gv) * uv).astype(BF16)

    row = lambda width: pl.BlockSpec((tm, width), lambda i: (i, 0))
    out = jax.ShapeDtypeStruct((t, nb * bw), BF16)
    return _pcall(body, name=name, out_shape=(out,) * 3, grid=(t // tm,), in_specs=[row(k), _resident(w_gate), _resident(w_up)],
                  out_specs=(row(nb * bw),) * 3, semantics=("parallel",), vmem_limit=VMEM_LIMIT)(h, w_gate, w_up)


def _swiglu_bwd(dx2, w_down, gate, up, *, name, after=None):
    t, d = dx2.shape
    f = w_down.shape[0]
    tm = _tile(t, ROWS_PER_STEP, 16)
    cn = _tile(f, COLS_PER_DOT, LANE)

    def body(dx_ref, w_ref, gate_ref, up_ref, dgate_ref, dup_ref):
        dxv = dx_ref[...].astype(BF16)
        for j in range(f // cn):
            cols = pl.ds(j * cn, cn)
            dffv = _dot(dxv, w_ref[cols, :], NT)
            gv = gate_ref[:, cols].astype(F32)
            sig = _sigmoid(gv)
            dgate_ref[:, cols] = (dffv * up_ref[:, cols].astype(F32) * sig * (1.0 + gv * (1.0 - sig))).astype(BF16)
            dup_ref[:, cols] = (dffv * gv * sig).astype(BF16)

    row = lambda width: pl.BlockSpec((tm, width), lambda i: (i, 0))
    out = jax.ShapeDtypeStruct((t, f), BF16)
    return _pcall(body, name=name, out_shape=(out, out), grid=(t // tm,), in_specs=[row(d), _resident(w_down), row(f), row(f)],
                  out_specs=(row(f), row(f)), semantics=("parallel",), vmem_limit=VMEM_LIMIT, after=after)(dx2, w_down, gate, up)


def _ple_fwd(x2, pgl, pp, *, name):
    t, d = x2.shape

    def body(x_ref, pgl_ref, pp_ref, o_ref):
        o_ref[...] = x_ref[...] + _sigmoid(pgl_ref[...]) * pp_ref[...]

    return _pcall(body, name=name, out_shape=jax.ShapeDtypeStruct((t, d), F32), grid=(t // ROW_TILE,),
                  in_specs=[_rows(t, d)] * 3, out_specs=_rows(t, d), semantics=("parallel",))(x2, pgl, pp)


def _ple_bwd(dx3, pgl, pp, *, name, after=None):
    t, d = dx3.shape

    def body(dx_ref, pgl_ref, pp_ref, dpgl_ref, dpp_ref):
        dxv = dx_ref[...]
        sig = _sigmoid(pgl_ref[...])
        dpp_ref[...] = (dxv * sig).astype(BF16)
        dpgl_ref[...] = (dxv * pp_ref[...] * sig * (1.0 - sig)).astype(BF16)

    return _pcall(body, name=name, out_shape=(jax.ShapeDtypeStruct((t, d), BF16),) * 2, grid=(t // ROW_TILE,),
                  in_specs=[_rows(t, d)] * 3, out_specs=(_rows(t, d),) * 2, semantics=("parallel",), after=after)(dx3, pgl, pp)


def _loss_head(x3, g, target, *, name):
    t, d = x3.shape

    def body(x_ref, g_ref, t_ref, dx_ref, dg_ref, loss_ref):
        xv = x_ref[...]
        r = lax.rsqrt(jnp.mean(xv * xv, axis=-1, keepdims=True) + EPS)
        xhat = xv * r
        gv = g_ref[...]
        err = xhat * gv - t_ref[...]
        row_loss = jnp.sum(err * err, axis=-1, keepdims=True) * (0.5 / d)
        lpart = jnp.broadcast_to(jnp.sum(row_loss, axis=0, keepdims=True), (1, LANE))
        dy = err * (1.0 / d)
        dyg = dy * gv
        dx_ref[...] = r * (dyg - xhat * jnp.mean(dyg * xhat, axis=-1, keepdims=True))
        gpart = jnp.sum(dy * xhat, axis=0, keepdims=True)

        @pl.when(pl.program_id(0) == 0)
        def _():
            dg_ref[...] = gpart
            loss_ref[...] = lpart

        @pl.when(pl.program_id(0) > 0)
        def _():
            dg_ref[...] += gpart
            loss_ref[...] += lpart

    return _pcall(body, name=name,
                  out_shape=(jax.ShapeDtypeStruct((t, d), F32), jax.ShapeDtypeStruct((1, d), F32), jax.ShapeDtypeStruct((1, LANE), F32)),
                  grid=(t // ROW_TILE,), in_specs=[_rows(t, d), _vec(d), _rows(t, d)],
                  out_specs=(_rows(t, d), _vec(d), _vec(LANE)), semantics=("arbitrary",))(x3, g, target)


def _shift_down(x, d):
    if d == 0:
        return x
    row = lax.broadcasted_iota(jnp.int32, x.shape, 0)
    return jnp.where(row >= d, pltpu.roll(x, d, 0), 0.0)


def _shift_up(x, d):
    if d == 0:
        return x
    t = x.shape[0]
    row = lax.broadcasted_iota(jnp.int32, x.shape, 0)
    return jnp.where(row < t - d, pltpu.roll(x, t - d, 0), 0.0)


def _colsum(x):
    return jnp.sum(x, axis=0, keepdims=True)


def _col(t, idx_fn):
    return pl.BlockSpec((t, LANE), idx_fn)


def _conv_fwd(x, w_ref, taps):
    acc = None
    for j in range(taps):
        term = w_ref[pl.ds(j, 1), :] * _shift_down(x, taps - 1 - j)
        acc = term if acc is None else acc + term
    return acc


def _conv_bwd(x, dy, w_ref, dw_ref, taps):
    dx = None
    for j in range(taps):
        term = w_ref[pl.ds(j, 1), :] * _shift_up(dy, taps - 1 - j)
        dx = term if dx is None else dx + term
        dw_ref[pl.ds(j, 1), :] = _colsum(dy * _shift_down(x, taps - 1 - j))
    return dx


def _qkv_prep_fwd(proj, conv_w, *, name):
    t = proj.shape[0]
    scale = HEAD_DIM ** -0.5

    def body(x_ref, w_ref, o_ref):
        j = pl.program_id(0)
        c = _conv_fwd(x_ref[...], w_ref, QKV_TAPS)
        s = c * _sigmoid(c)
        r = lax.rsqrt(jnp.sum(s * s, axis=-1, keepdims=True) + EPS)
        f = jnp.where(j < 2 * HEADS, r, 1.0) * jnp.where(j < HEADS, scale, 1.0)
        o_ref[0] = s * f

    return _pcall(body, name=name, out_shape=jax.ShapeDtypeStruct((3 * HEADS, t, LANE), F32), grid=(3 * HEADS,),
                  in_specs=[_col(t, lambda j: (0, j)), pl.BlockSpec((QKV_TAPS, LANE), lambda j: (0, j))],
                  out_specs=pl.BlockSpec((1, t, LANE), lambda j: (j, 0, 0)), semantics=("parallel",),
                  vmem_limit=VMEM_LIMIT)(proj, conv_w)


def _qkv_prep_bwd(proj, conv_w, dqkv, *, name):
    t = proj.shape[0]
    scale = HEAD_DIM ** -0.5

    def body(x_ref, w_ref, d_ref, dx_ref, dw_ref):
        j = pl.program_id(0)
        xv = x_ref[...]
        c = _conv_fwd(xv, w_ref, QKV_TAPS)
        sig = _sigmoid(c)
        s = c * sig
        r = lax.rsqrt(jnp.sum(s * s, axis=-1, keepdims=True) + EPS)
        n0 = s * r
        dv = d_ref[0]
        dn0 = dv * jnp.where(j < HEADS, scale, 1.0)
        ds_norm = r * (dn0 - n0 * jnp.sum(dn0 * n0, axis=-1, keepdims=True))
        ds = jnp.where(j < 2 * HEADS, ds_norm, dv)
        dc = ds * sig * (1.0 + c * (1.0 - sig))
        dx_ref[...] = _conv_bwd(xv, dc, w_ref, dw_ref, QKV_TAPS).astype(BF16)

    return _pcall(body, name=name,
                  out_shape=(jax.ShapeDtypeStruct((t, 3 * A_DIM), BF16), jax.ShapeDtypeStruct((QKV_TAPS, 3 * A_DIM), F32)),
                  grid=(3 * HEADS,),
                  in_specs=[_col(t, lambda j: (0, j)), pl.BlockSpec((QKV_TAPS, LANE), lambda j: (0, j)),
                            pl.BlockSpec((1, t, LANE), lambda j: (j, 0, 0))],
                  out_specs=(_col(t, lambda j: (0, j)), pl.BlockSpec((QKV_TAPS, LANE), lambda j: (0, j))),
                  semantics=("parallel",), vmem_limit=VMEM_LIMIT)(proj, conv_w, dqkv)


def _lane_pick(x, lane_idx, lane):
    return jnp.broadcast_to(jnp.sum(jnp.where(lane == lane_idx, x, 0.0), axis=-1, keepdims=True), x.shape)


def _gates_fwd(proj, alog, dtb, *, name):
    t = proj.shape[0]

    def body(x_ref, alog_ref, dtb_ref, g_ref, b_ref):
        xv = x_ref[...]
        lane = lax.broadcasted_iota(jnp.int32, xv.shape, 1)
        gall = -jnp.exp(alog_ref[...]) * _softplus(xv + dtb_ref[...])
        ball = _sigmoid(xv)
        for h in range(HEADS):
            g_ref[h] = _lane_pick(gall, h, lane)
            b_ref[h] = _lane_pick(ball, HEADS + h, lane)

    out = jax.ShapeDtypeStruct((HEADS, t, LANE), F32)
    whole = pl.BlockSpec((HEADS, t, LANE), lambda i: (0, 0, 0))
    return _pcall(body, name=name, out_shape=(out, out), grid=(1,),
                  in_specs=[_col(t, lambda i: (0, AB_COL // LANE)), _vec(LANE), _vec(LANE)], out_specs=(whole, whole),
                  semantics=("arbitrary",), vmem_limit=VMEM_LIMIT)(proj, alog, dtb)


def _gates_bwd(proj, alog, dtb, dg, dbeta, *, name):
    t = proj.shape[0]

    def body(x_ref, alog_ref, dtb_ref, dg_ref, db_ref, dab_ref, dalog_ref, ddtb_ref):
        xv = x_ref[...]
        lane = lax.broadcasted_iota(jnp.int32, xv.shape, 1)
        lane1 = lax.broadcasted_iota(jnp.int32, (1, LANE), 1)
        z = xv + dtb_ref[...]
        nea = -jnp.exp(alog_ref[...])
        da_f = nea * _sigmoid(z)
        g_f = nea * _softplus(z)
        ball = _sigmoid(xv)
        db_f = ball * (1.0 - ball)
        dab = jnp.zeros_like(xv)
        dalog = jnp.zeros((1, LANE), F32)
        for h in range(HEADS):
            dgh = dg_ref[h]
            dab = dab + jnp.where(lane == h, dgh * da_f, 0.0) + jnp.where(lane == HEADS + h, db_ref[h] * db_f, 0.0)
            dalog = dalog + jnp.where(lane1 == h, _colsum(dgh * g_f), 0.0)
        dab_ref[...] = dab.astype(BF16)
        dalog_ref[...] = dalog
        ddtb_ref[...] = jnp.where(lane1 < HEADS, _colsum(dab), 0.0)

    whole = pl.BlockSpec((HEADS, t, LANE), lambda i: (0, 0, 0))
    vec = jax.ShapeDtypeStruct((1, LANE), F32)
    return _pcall(body, name=name, out_shape=(jax.ShapeDtypeStruct((t, LANE), BF16), vec, vec), grid=(1,),
                  in_specs=[_col(t, lambda i: (0, AB_COL // LANE)), _vec(LANE), _vec(LANE), whole, whole],
                  out_specs=(_col(t, lambda i: (0, 0)), _vec(LANE), _vec(LANE)), semantics=("arbitrary",),
                  vmem_limit=VMEM_LIMIT)(proj, alog, dtb, dg, dbeta)


Z_COL = 3 * A_DIM // LANE


def _apost_fwd(o, proj, gn, *, name):
    t = proj.shape[0]

    def body(o_ref, z_ref, gn_ref, y_ref):
        ov = o_ref[0]
        z = z_ref[...]
        r = lax.rsqrt(jnp.mean(ov * ov, axis=-1, keepdims=True) + EPS)
        y_ref[...] = (ov * r * gn_ref[...] * (z * _sigmoid(z))).astype(BF16)

    return _pcall(body, name=name, out_shape=jax.ShapeDtypeStruct((t, A_DIM), BF16), grid=(HEADS,),
                  in_specs=[pl.BlockSpec((1, t, LANE), lambda h: (h, 0, 0)), _col(t, lambda h: (0, Z_COL + h)),
                            pl.BlockSpec((1, LANE), lambda h: (0, 0))],
                  out_specs=_col(t, lambda h: (0, h)), semantics=("parallel",), vmem_limit=VMEM_LIMIT)(o, proj, gn)


def _apost_bwd(o, proj, gn, dmixed, *, name):
    t = proj.shape[0]

    def body(o_ref, z_ref, gn_ref, d_ref, do_ref, dz_ref, dgn_ref):
        ov = o_ref[0]
        z = z_ref[...]
        gnv = gn_ref[...]
        dv = d_ref[...]
        r = lax.rsqrt(jnp.mean(ov * ov, axis=-1, keepdims=True) + EPS)
        ohat = ov * r
        sig = _sigmoid(z)
        dy = dv * (z * sig)
        dz_ref[...] = (dv * ohat * gnv * sig * (1.0 + z * (1.0 - sig))).astype(BF16)
        dyo = dy * gnv
        do_ref[0] = r * (dyo - ohat * jnp.mean(dyo * ohat, axis=-1, keepdims=True))
        part = _colsum(dy * ohat)

        @pl.when(pl.program_id(0) == 0)
        def _():
            dgn_ref[...] = part

        @pl.when(pl.program_id(0) > 0)
        def _():
            dgn_ref[...] += part

    return _pcall(body, name=name,
                  out_shape=(jax.ShapeDtypeStruct((HEADS, t, LANE), F32), jax.ShapeDtypeStruct((t, A_DIM), BF16),
                             jax.ShapeDtypeStruct((1, LANE), F32)),
                  grid=(HEADS,),
                  in_specs=[pl.BlockSpec((1, t, LANE), lambda h: (h, 0, 0)), _col(t, lambda h: (0, Z_COL + h)),
                            pl.BlockSpec((1, LANE), lambda h: (0, 0)), _col(t, lambda h: (0, h))],
                  out_specs=(pl.BlockSpec((1, t, LANE), lambda h: (h, 0, 0)), _col(t, lambda h: (0, h)),
                             pl.BlockSpec((1, LANE), lambda h: (0, 0))),
                  semantics=("arbitrary",), vmem_limit=VMEM_LIMIT)(o, proj, gn, dmixed)


POOL_COL = (AB_COL + LANE) // LANE
CB_COL = POOL_COL + POOL_DIM // LANE
CC_COL = CB_COL + CONV_DIM // LANE
CH_COL = CC_COL + CONV_DIM // LANE
MAX_WIN_LOG2 = 4


def _window_sums(x, shift):
    sums = []
    cur = x
    for k in range(MAX_WIN_LOG2):
        cur = cur + shift(cur, 1 << k)
        sums.append(cur)
    return sums


def _pick_window(sums, win):
    out = sums[-1]
    for k in range(MAX_WIN_LOG2 - 2, -1, -1):
        out = jnp.where(win == float(2 << k), sums[k], out)
    return out


def _pool_counts(shape, win):
    row = lax.broadcasted_iota(jnp.int32, shape, 0).astype(F32)
    return jnp.minimum(row + 1.0, win)


def _pool_fwd(proj, win, wbd, scale, *, name):
    t = proj.shape[0]

    def body(x_ref, win_ref, w_ref, s_ref, y_ref):
        xv = x_ref[...]
        winv = win_ref[...]
        pooled = _pick_window(_window_sums(xv, _shift_down), winv) / _pool_counts(xv.shape, winv) - xv
        y_ref[...] = (_dot(pooled, w_ref[0], NN) * s_ref[...]).astype(BF16)

    nb = POOL_DIM // LANE
    vec = pl.BlockSpec((1, LANE), lambda b: (0, b))
    return _pcall(body, name=name, out_shape=jax.ShapeDtypeStruct((t, POOL_DIM), BF16), grid=(nb,),
                  in_specs=[_col(t, lambda b: (0, POOL_COL + b)), vec, pl.BlockSpec((1, LANE, LANE), lambda b: (b, 0, 0)), vec],
                  out_specs=_col(t, lambda b: (0, b)), semantics=("parallel",), vmem_limit=VMEM_LIMIT)(proj, win, wbd, scale)


def _pool_bwd(proj, win, wbd, scale, dmixed, *, name):
    t = proj.shape[0]

    def body(x_ref, win_ref, w_ref, s_ref, d_ref, dx_ref, dw_ref, ds_ref):
        xv = x_ref[...]
        winv = win_ref[...]
        cnt = _pool_counts(xv.shape, winv)
        pooled = _pick_window(_window_sums(xv, _shift_down), winv) / cnt - xv
        dv = d_ref[...]
        ds_ref[...] = _colsum(dv * _dot(pooled, w_ref[0], NN))
        dy0 = dv * s_ref[...]
        dw_ref[0] = _dot(pooled, dy0, TN)
        dpooled = _dot(dy0, w_ref[0], NT)
        dmean = dpooled / cnt
        dx_ref[...] = (_pick_window(_window_sums(dmean, _shift_up), winv) - dpooled).astype(BF16)

    nb = POOL_DIM // LANE
    vec = pl.BlockSpec((1, LANE), lambda b: (0, b))
    mat = pl.BlockSpec((1, LANE, LANE), lambda b: (b, 0, 0))
    first = A_DIM // LANE
    return _pcall(body, name=name,
                  out_shape=(jax.ShapeDtypeStruct((t, POOL_DIM), BF16), jax.ShapeDtypeStruct((nb, LANE, LANE), F32),
                             jax.ShapeDtypeStruct((1, POOL_DIM), F32)),
                  grid=(nb,),
                  in_specs=[_col(t, lambda b: (0, POOL_COL + b)), vec, mat, vec, _col(t, lambda b: (0, first + b))],
                  out_specs=(_col(t, lambda b: (0, b)), mat, vec), semantics=("parallel",),
                  vmem_limit=VMEM_LIMIT)(proj, win, wbd, scale, dmixed)


def _sconv_fwd(proj, w, *, name):
    t = proj.shape[0]

    def body(cb_ref, cc_ref, ch_ref, w_ref, y_ref):
        y_ref[...] = (cb_ref[...] * _conv_fwd(cc_ref[...] * ch_ref[...], w_ref, CONV_TAPS)).astype(BF16)

    nb = CONV_DIM // LANE
    return _pcall(body, name=name, out_shape=jax.ShapeDtypeStruct((t, CONV_DIM), BF16), grid=(nb,),
                  in_specs=[_col(t, lambda b: (0, CB_COL + b)), _col(t, lambda b: (0, CC_COL + b)),
                            _col(t, lambda b: (0, CH_COL + b)), pl.BlockSpec((CONV_TAPS, LANE), lambda b: (0, b))],
                  out_specs=_col(t, lambda b: (0, b)), semantics=("parallel",), vmem_limit=VMEM_LIMIT)(proj, proj, proj, w)


def _sconv_bwd(proj, w, dmixed, *, name):
    t = proj.shape[0]

    def body(cb_ref, cc_ref, ch_ref, w_ref, d_ref, dcb_ref, dcc_ref, dch_ref, dw_ref):
        cc = cc_ref[...]
        ch = ch_ref[...]
        u = cc * ch
        dv = d_ref[...]
        dcb_ref[...] = (dv * _conv_fwd(u, w_ref, CONV_TAPS)).astype(BF16)
        du = _conv_bwd(u, dv * cb_ref[...], w_ref, dw_ref, CONV_TAPS)
        dcc_ref[...] = (du * ch).astype(BF16)
        dch_ref[...] = (du * cc).astype(BF16)

    nb = CONV_DIM // LANE
    first = (A_DIM + POOL_DIM) // LANE
    act = jax.ShapeDtypeStruct((t, CONV_DIM), BF16)
    wspec = pl.BlockSpec((CONV_TAPS, LANE), lambda b: (0, b))
    ospec = _col(t, lambda b: (0, b))
    return _pcall(body, name=name, out_shape=(act, act, act, jax.ShapeDtypeStruct((CONV_TAPS, CONV_DIM), F32)), grid=(nb,),
                  in_specs=[_col(t, lambda b: (0, CB_COL + b)), _col(t, lambda b: (0, CC_COL + b)),
                            _col(t, lambda b: (0, CH_COL + b)), wspec, _col(t, lambda b: (0, first + b))],
                  out_specs=(ospec, ospec, ospec, wspec), semantics=("parallel",),
                  vmem_limit=VMEM_LIMIT)(proj, proj, proj, w, dmixed)


def _chunk_masks():
    r = lax.broadcasted_iota(jnp.int32, (CHUNK, CHUNK), 0)
    c = lax.broadcasted_iota(jnp.int32, (CHUNK, CHUNK), 1)
    return r >= c, r > c, jnp.where(r == c, 1.0, 0.0).astype(F32)


def _split(a):
    hi = a.astype(BF16)
    return hi, (a - hi.astype(F32)).astype(BF16)


def _dot_split(a, b, dims):
    (ah, al), (bh, bl) = a, b
    return _dot(ah, bh, dims) + _dot(ah, bl, dims) + _dot(al, bh, dims)


def _tri_inv(lows, eye):
    xs = [eye - low for low in lows]
    ps = [_split(low) for low in lows]
    ps = [_split(_dot_split(p, p, NN)) for p in ps]
    for i in range(5):
        xs = [x + _dot_split(_split(x), p, NN) for x, p in zip(xs, ps)]
        if i < 4:
            ps = [_split(_dot_split(p, p, NN)) for p in ps]
    return xs


def _prefix_sum_rows(x):
    for k in range(6):
        x = x + _shift_down(x, 1 << k)
    return x


def _suffix_sum_rows(x):
    for k in range(6):
        x = x + _shift_up(x, 1 << k)
    return x


def _chunk_decay(g, incl):
    gcb = _prefix_sum_rows(g)
    gtot = _colsum(g)
    col = gcb[:, :CHUNK]
    row = gcb.T[:CHUNK, :]
    decay = jnp.exp(jnp.where(incl, col - row, -1e30))
    return gcb, gtot, decay


CHUNKS_PER_STEP = 4


def _heads_of(ref, base, rows):
    return [ref[base + h, rows, :] for h in range(HEADS)]


def _chunk_rows(j):
    return pl.ds(j * CHUNK, CHUNK)


def _deltanet_prep(qkv, g, beta, *, name):
    t = qkv.shape[1]
    n_chunks = t // CHUNK
    per = CHUNKS_PER_STEP
    probs = [(j, h) for j in range(per) for h in range(HEADS)]

    def body(qkv_ref, g_ref, b_ref, u_ref, w_ref, qg_ref, kg_ref, attn_ref, tm_ref):
        incl, strict, eye = _chunk_masks()
        q = [qkv_ref[h, _chunk_rows(j), :] for j, h in probs]
        k = [qkv_ref[HEADS + h, _chunk_rows(j), :] for j, h in probs]
        v = [qkv_ref[2 * HEADS + h, _chunk_rows(j), :] for j, h in probs]
        bv = [b_ref[h, _chunk_rows(j), :] for j, h in probs]
        dec = [_chunk_decay(g_ref[h, _chunk_rows(j), :], incl) for j, h in probs]
        kb = [a * b for a, b in zip(k, bv)]
        low = [jnp.where(strict, _dot(a, b, NT) * d[2], 0.0) for a, b, d in zip(kb, k, dec)]
        tm = _tri_inv(low, eye)
        egc = [jnp.exp(d[0]) for d in dec]
        u = [_dot(m, a * b, NN) for m, a, b in zip(tm, v, bv)]
        w = [_dot(m, a * e, NN) for m, a, e in zip(tm, kb, egc)]
        attn = [_dot(a, b, NT) * d[2] for a, b, d in zip(q, k, dec)]
        for i, (j, h) in enumerate(probs):
            rows = _chunk_rows(j)
            u_ref[h, rows, :] = u[i]
            w_ref[h, rows, :] = w[i].astype(BF16)
            qg_ref[h, rows, :] = (q[i] * egc[i]).astype(BF16)
            kg_ref[h, rows, :] = (k[i] * jnp.exp(dec[i][1] - dec[i][0])).astype(BF16)
            attn_ref[j, h] = attn[i].astype(BF16)
            tm_ref[j, h] = tm[i]

    act = lambda heads: pl.BlockSpec((heads, per * CHUNK, LANE), lambda n: (0, n, 0))
    mat = pl.BlockSpec((per, HEADS, CHUNK, CHUNK), lambda n: (n, 0, 0, 0))
    return _pcall(
        body, name=name,
        out_shape=(jax.ShapeDtypeStruct((HEADS, t, LANE), F32),) + (jax.ShapeDtypeStruct((HEADS, t, LANE), BF16),) * 3
        + (jax.ShapeDtypeStruct((n_chunks, HEADS, CHUNK, CHUNK), BF16), jax.ShapeDtypeStruct((n_chunks, HEADS, CHUNK, CHUNK), F32)),
        grid=(n_chunks // per,), in_specs=[act(3 * HEADS), act(HEADS), act(HEADS)],
        out_specs=(act(HEADS),) * 4 + (mat, mat), semantics=("parallel",), vmem_limit=VMEM_LIMIT)(qkv, g, beta)


SCAN_CHUNKS_PER_STEP = 8


def _deltanet_scan(u, w, qg, kg, attn, g, *, name):
    t = u.shape[1]
    n_chunks = t // CHUNK
    per = SCAN_CHUNKS_PER_STEP

    def body(u_ref, w_ref, qg_ref, kg_ref, attn_ref, g_ref, o_ref, vn_ref, st_ref, s_ref):
        @pl.when(pl.program_id(0) == 0)
        def _():
            s_ref[...] = jnp.zeros_like(s_ref)

        for j in range(per):
            rows = _chunk_rows(j)
            s = [s_ref[h] for h in range(HEADS)]
            vn = [u_ref[h, rows, :] - _dot(w_ref[h, rows, :], s[h], NN) for h in range(HEADS)]
            o = [_dot(qg_ref[h, rows, :], s[h], NN) + _dot(attn_ref[j, h], vn[h], NN) for h in range(HEADS)]
            eg = [jnp.exp(_colsum(g_ref[h, rows, :])) for h in range(HEADS)]
            for h in range(HEADS):
                st_ref[j, h] = s[h]
                s_ref[h] = s[h] * eg[h] + _dot(kg_ref[h, rows, :], vn[h], TN)
                o_ref[h, rows, :] = o[h]
                vn_ref[h, rows, :] = vn[h]

    act = pl.BlockSpec((HEADS, per * CHUNK, LANE), lambda n: (0, n, 0))
    out = jax.ShapeDtypeStruct((HEADS, t, LANE), F32)
    return _pcall(
        body, name=name, out_shape=(out, out, jax.ShapeDtypeStruct((n_chunks, HEADS, LANE, LANE), F32)), grid=(n_chunks // per,),
        in_specs=[act] * 4 + [pl.BlockSpec((per, HEADS, CHUNK, CHUNK), lambda n: (n, 0, 0, 0)), act],
        out_specs=(act, act, pl.BlockSpec((per, HEADS, LANE, LANE), lambda n: (n, 0, 0, 0))),
        scratch_shapes=[pltpu.VMEM((HEADS, LANE, LANE), F32)], semantics=("arbitrary",))(u, w, qg, kg, attn, g)


def _deltanet_bscan(w, qg, kg, attn, g, do, *, name):
    t = w.shape[1]
    n_chunks = t // CHUNK
    per = SCAN_CHUNKS_PER_STEP
    steps = n_chunks // per

    def body(w_ref, qg_ref, kg_ref, attn_ref, g_ref, do_ref, dvn_ref, dsn_ref, ds_ref):
        @pl.when(pl.program_id(0) == 0)
        def _():
            ds_ref[...] = jnp.zeros_like(ds_ref)

        for j in reversed(range(per)):
            rows = _chunk_rows(j)
            dsn = [ds_ref[h] for h in range(HEADS)]
            dov = [do_ref[h, rows, :] for h in range(HEADS)]
            dvn = [_dot(attn_ref[j, h], dov[h], TN) + _dot(kg_ref[h, rows, :], dsn[h], NN) for h in range(HEADS)]
            eg = [jnp.exp(_colsum(g_ref[h, rows, :])) for h in range(HEADS)]
            for h in range(HEADS):
                dsn_ref[j, h] = dsn[h]
                ds_ref[h] = _dot(qg_ref[h, rows, :], dov[h], TN) + eg[h] * dsn[h] - _dot(w_ref[h, rows, :], dvn[h], TN)
                dvn_ref[h, rows, :] = dvn[h]

    act = pl.BlockSpec((HEADS, per * CHUNK, LANE), lambda n: (0, steps - 1 - n, 0))
    return _pcall(
        body, name=name,
        out_shape=(jax.ShapeDtypeStruct((HEADS, t, LANE), F32), jax.ShapeDtypeStruct((n_chunks, HEADS, LANE, LANE), F32)),
        grid=(steps,),
        in_specs=[act] * 3 + [pl.BlockSpec((per, HEADS, CHUNK, CHUNK), lambda n: (steps - 1 - n, 0, 0, 0)), act, act],
        out_specs=(act, pl.BlockSpec((per, HEADS, LANE, LANE), lambda n: (steps - 1 - n, 0, 0, 0))),
        scratch_shapes=[pltpu.VMEM((HEADS, LANE, LANE), F32)], semantics=("arbitrary",))(w, qg, kg, attn, g, do)


def _sum_all(x):
    return jnp.sum(jnp.sum(x, axis=1, keepdims=True), axis=0, keepdims=True)


def _rowsum(x):
    return jnp.sum(x, axis=1, keepdims=True)


def _deltanet_post(qkv, g, beta, tmats, states, dstates, do, dvn, vn, *, name):
    t = qkv.shape[1]
    n_chunks = t // CHUNK
    per = CHUNKS_PER_STEP
    probs = [(j, h) for j in range(per) for h in range(HEADS)]

    def body(qkv_ref, g_ref, b_ref, tm_ref, st_ref, dsn_ref, do_ref, dvn_ref, vn_ref, dqkv_ref, dg_ref, db_ref):
        incl, strict, _ = _chunk_masks()
        ones = jnp.ones((CHUNK, LANE), BF16)
        last_row = lax.broadcasted_iota(jnp.int32, (CHUNK, LANE), 0) == CHUNK - 1
        z = lambda f, *cols: [f(*a) for a in zip(*cols)]
        q = [qkv_ref[h, _chunk_rows(j), :] for j, h in probs]
        k = [qkv_ref[HEADS + h, _chunk_rows(j), :] for j, h in probs]
        v = [qkv_ref[2 * HEADS + h, _chunk_rows(j), :] for j, h in probs]
        bv = [b_ref[h, _chunk_rows(j), :] for j, h in probs]
        dov = [do_ref[h, _chunk_rows(j), :] for j, h in probs]
        dvn_ = [dvn_ref[h, _chunk_rows(j), :] for j, h in probs]
        vn_ = [vn_ref[h, _chunk_rows(j), :] for j, h in probs]
        tm = [tm_ref[j, h] for j, h in probs]
        s = [st_ref[j, h] for j, h in probs]
        dsn = [dsn_ref[j, h] for j, h in probs]
        dec = [_chunk_decay(g_ref[h, _chunk_rows(j), :], incl) for j, h in probs]
        decay = [d[2] for d in dec]
        egc = [jnp.exp(d[0]) for d in dec]
        ekg = [jnp.exp(d[1] - d[0]) for d in dec]
        kb = z(lambda a, b: a * b, k, bv)
        vb = z(lambda a, b: a * b, v, bv)
        kbg = z(lambda a, b: a * b, kb, egc)
        qg = z(lambda a, b: a * b, q, egc)
        kg = z(lambda a, b: a * b, k, ekg)
        kk = z(lambda a, b: _dot(a, b, NT), kb, k)
        qk = z(lambda a, b: _dot(a, b, NT), q, k)
        dattn = z(lambda a, b: jnp.where(incl, _dot(a, b, NT), 0.0), dov, vn_)
        dqg = z(lambda a, b: _dot(a, b, NT), dov, s)
        dkg = z(lambda a, b: _dot(a, b, NT), vn_, dsn)
        dglast = z(lambda a, b, c, d, e: _sum_all(a * b) * jnp.exp(e[1]) + _sum_all(c * d), s, dsn, dkg, kg, dec)
        dw = z(lambda a, b: -_dot(a, b, NT), dvn_, s)
        dtm = z(lambda a, b, c, d: _dot(a, b, NT) + _dot(c, d, NT), dvn_, vb, dw, kbg)
        dvb = z(lambda a, b: _dot(a, b, TN), tm, dvn_)
        dkbg = z(lambda a, b: _dot(a, b, TN), tm, dw)
        dlow = z(lambda a, b: jnp.where(strict, -_dot(_dot(a, b, TN), a, NT), 0.0), tm, dtm)
        dkk = z(lambda a, b: a * b, dlow, decay)
        dqk = z(lambda a, b: a * b, dattn, decay)
        dkb = z(lambda a, b, c, d: _dot(a, b, NN) + c * d, dkk, k, dkbg, egc)
        dk = z(lambda a, b, c, d, e, f, g_, h_: _dot(a, b, TN) + _dot(c, d, TN) + e * f + g_ * h_, dkk, kb, dqk, q, dkg, ekg, dkb, bv)
        dq = z(lambda a, b, c, d: _dot(a, b, NN) + c * d, dqk, k, dqg, egc)
        m = z(lambda a, b, c, d, e: (a * b + c * d) * e, dlow, kk, dattn, qk, decay)
        mcol = [_dot(mh, ones, TN) + _dot(ml, ones, TN) for mh, ml in (_split(a) for a in m)]
        for i, (j, h) in enumerate(probs):
            rows = _chunk_rows(j)
            dqkv_ref[h, rows, :] = dq[i]
            dqkv_ref[HEADS + h, rows, :] = dk[i]
            dqkv_ref[2 * HEADS + h, rows, :] = dvb[i] * bv[i]
            db_ref[h, rows, :] = jnp.broadcast_to(_rowsum(dkb[i] * k[i] + dvb[i] * v[i]), (CHUNK, LANE))
            dgc = (_rowsum(dqg[i] * qg[i] + dkbg[i] * kbg[i] - dkg[i] * kg[i]) + _rowsum(m[i]) - mcol[i]
                   + jnp.where(last_row, dglast[i], 0.0))
            dg_ref[h, rows, :] = _suffix_sum_rows(dgc)

    act = lambda heads: pl.BlockSpec((heads, per * CHUNK, LANE), lambda n: (0, n, 0))
    mat = lambda d: pl.BlockSpec((per, HEADS, d, d), lambda n: (n, 0, 0, 0))
    out = jax.ShapeDtypeStruct((HEADS, t, LANE), F32)
    return _pcall(
        body, name=name, out_shape=(jax.ShapeDtypeStruct((3 * HEADS, t, LANE), F32), out, out), grid=(n_chunks // per,),
        in_specs=[act(3 * HEADS), act(HEADS), act(HEADS), mat(CHUNK), mat(LANE), mat(LANE), act(HEADS), act(HEADS), act(HEADS)],
        out_specs=(act(3 * HEADS), act(HEADS), act(HEADS)), semantics=("parallel",),
        vmem_limit=VMEM_LIMIT)(qkv, g, beta, tmats, states, dstates, do, dvn, vn)


ANY = pl.BlockSpec(memory_space=pl.ANY)
PEERS = N_DEV - 1


def _all_gather(arrays, *, name):
    n = len(arrays)

    def body(*refs):
        ins, outs = refs[:n], refs[n:2 * n]
        send_sems, recv_sems, local_sems = refs[2 * n:]
        x, y, c = lax.axis_index("x"), lax.axis_index("y"), lax.axis_index("c")
        me, sibling = (x, y, c), (x, y, 1 - c)
        chips = [(1 - x, y), (x, 1 - y), (1 - x, 1 - y)]

        def copy(a, k, block, to, src=None):
            dst = outs[a].at[4 * block[0] + 2 * block[1] + block[2]]
            return pltpu.make_async_remote_copy(src_ref=dst if src is None else src, dst_ref=dst, send_sem=send_sems.at[a * PEERS + k],
                                                recv_sem=recv_sems.at[a * PEERS + k], device_id=to, device_id_type=MESH)

        local = [pltpu.make_async_copy(ins[a], outs[a].at[4 * x + 2 * y + c], local_sems.at[a]) for a in range(n)]
        for cp in local:
            cp.start()
        first = []
        for a in range(n):
            first.append(copy(a, 0, me, sibling, src=ins[a]))
            first += [copy(a, 1 + j, me, (*chip, c), src=ins[a]) for j, chip in enumerate(chips)]
        for cp in first:
            cp.start()
        passed = []
        for a in range(n):
            for j, chip in enumerate(chips):
                copy(a, 1 + j, (*chip, c), me).wait_recv()
                fwd = copy(a, 4 + j, (*chip, c), sibling)
                fwd.start()
                passed.append(fwd)
        for a in range(n):
            copy(a, 0, sibling, me).wait_recv()
            for j, chip in enumerate(chips):
                copy(a, 4 + j, (*chip, 1 - c), me).wait_recv()
        for cp in first + passed:
            cp.wait_send()
        for cp in local:
            cp.wait()

    return _pcall(body, name=name, out_shape=tuple(jax.ShapeDtypeStruct((N_DEV,) + a.shape, a.dtype) for a in arrays),
                  in_specs=[ANY] * n, out_specs=(ANY,) * n,
                  scratch_shapes=[pltpu.SemaphoreType.DMA((n * PEERS,)), pltpu.SemaphoreType.DMA((n * PEERS,)),
                                  pltpu.SemaphoreType.DMA((n,))])(*arrays)


CHIPS = 4


def _pair_exchange(arrays, *, name):
    n = len(arrays)

    def body(*refs):
        ins, outs = refs[:n], refs[n:2 * n]
        send_sems, recv_sems = refs[2 * n:]
        x, y, c = lax.axis_index("x"), lax.axis_index("y"), lax.axis_index("c")
        copies = []
        for a in range(n):
            for q in range(CHIPS):
                cp = pltpu.make_async_remote_copy(src_ref=ins[a].at[2 * q + 1 - c], dst_ref=outs[a].at[q],
                                                  send_sem=send_sems.at[a * CHIPS + q], recv_sem=recv_sems.at[a * CHIPS + q],
                                                  device_id=(x, y, 1 - c), device_id_type=MESH)
                cp.start()
                copies.append(cp)
        for cp in copies:
            cp.wait()

    return _pcall(body, name=name, out_shape=tuple(jax.ShapeDtypeStruct((CHIPS,) + a.shape[1:], a.dtype) for a in arrays),
                  in_specs=[ANY] * n, out_specs=(ANY,) * n,
                  scratch_shapes=[pltpu.SemaphoreType.DMA((n * CHIPS,)), pltpu.SemaphoreType.DMA((n * CHIPS,))])(*arrays)


def _pair_add(blocks, theirs, *, name):
    _, r, c_ = blocks.shape
    tr = _tile(r, 512, 16)

    def body(mine_ref, theirs_ref, o_ref):
        core = lax.axis_index("c")
        own = jnp.where(core == 0, mine_ref[0, 0].astype(F32), mine_ref[0, 1].astype(F32))
        o_ref[0] = (own + theirs_ref[0].astype(F32)).astype(o_ref.dtype)

    spec = pl.BlockSpec((1, tr, c_), lambda q, i: (q, i, 0))
    return _pcall(body, name=name, out_shape=jax.ShapeDtypeStruct(theirs.shape, theirs.dtype), grid=(CHIPS, r // tr),
                  in_specs=[pl.BlockSpec((1, 2, tr, c_), lambda q, i: (q, 0, i, 0)), spec], out_specs=spec,
                  semantics=("parallel", "parallel"), vmem_limit=VMEM_LIMIT)(blocks.reshape(CHIPS, 2, r, c_), theirs)


HBM = pl.BlockSpec(memory_space=pltpu.HBM)
SEM = pl.BlockSpec(memory_space=pltpu.SEMAPHORE)
EFFECT = pltpu.SideEffectType.DATAFLOW_SIDE_EFFECTING


GATHER, CHIP_GATHER, CHIP_SCATTER = "gather", "chip_gather", "chip_scatter"
PEERS_OF = {GATHER: N_DEV - 1, CHIP_GATHER: CHIPS - 1, CHIP_SCATTER: CHIPS - 1}


def _direct_copies(srcs, lands, send_sems, recv_sems, local_sems, kind):
    x, y, c = lax.axis_index("x"), lax.axis_index("y"), lax.axis_index("c")
    peers = PEERS_OF[kind]
    mine = 4 * x + 2 * y + c if kind == GATHER else 2 * x + y
    copies = []
    for a, (src, land) in enumerate(zip(srcs, lands)):
        copies.append(pltpu.make_async_copy(src.at[mine] if kind == CHIP_SCATTER else src, land.at[mine], local_sems.at[a]))
        for k in range(1, peers + 1):
            bits = k if kind == GATHER else 2 * k
            px = 1 - x if bits & 4 else x
            py = 1 - y if bits & 2 else y
            pc = 1 - c if bits & 1 else c
            copies.append(pltpu.make_async_remote_copy(
                src_ref=src.at[2 * px + py] if kind == CHIP_SCATTER else src, dst_ref=land.at[mine],
                send_sem=send_sems.at[a * peers + k - 1], recv_sem=recv_sems.at[a * peers + k - 1],
                device_id=(px, py, pc), device_id_type=MESH))
    return copies


def _pair_swap(arrays, *, name):
    n = len(arrays)

    def body(*refs):
        ins, outs = refs[:n], refs[n:2 * n]
        send_sems, recv_sems, local_sems = refs[2 * n:]
        x, y, c = lax.axis_index("x"), lax.axis_index("y"), lax.axis_index("c")
        copies = []
        for a in range(n):
            for q in range(CHIPS):
                at = a * CHIPS + q
                copies.append(pltpu.make_async_copy(ins[a].at[q], outs[a].at[2 * q + c], local_sems.at[at]))
                copies.append(pltpu.make_async_remote_copy(src_ref=ins[a].at[q], dst_ref=outs[a].at[2 * q + c],
                                                           send_sem=send_sems.at[at], recv_sem=recv_sems.at[at],
                                                           device_id=(x, y, 1 - c), device_id_type=MESH))
        for cp in copies:
            cp.start()
        for cp in copies:
            cp.wait()

    return _pcall(body, name=name, out_shape=tuple(jax.ShapeDtypeStruct((N_DEV,) + a.shape[1:], a.dtype) for a in arrays),
                  in_specs=[ANY] * n, out_specs=(ANY,) * n,
                  scratch_shapes=[pltpu.SemaphoreType.DMA((n * CHIPS,))] * 3)(*arrays)


def _exchange_start(groups, kind, *, name, after=None):
    srcs = [s for group in groups for s in group]
    n = len(srcs)
    sizes = [len(group) for group in groups]
    starts = [sum(sizes[:g]) for g in range(len(groups))]
    land_shapes = [{GATHER: (N_DEV,) + s.shape, CHIP_GATHER: (CHIPS,) + s.shape, CHIP_SCATTER: s.shape}[kind] for s in srcs]
    peers = PEERS_OF[kind]
    extra = [] if after is None else [after]

    def body(*refs):
        srcs_, lands = refs[:n], refs[n:2 * n]
        token = refs[-1]
        sem_refs = refs[2 * n + len(extra):]
        for g, (at, size) in enumerate(zip(starts, sizes)):
            send_sems, recv_sems, local_sems = sem_refs[3 * g:3 * g + 3]
            for cp in _direct_copies(srcs_[at:at + size], lands[at:at + size], send_sems, recv_sems, local_sems, kind):
                cp.start()
        token[...] = jnp.zeros_like(token)

    sems = tuple(t for size in sizes for t in (pltpu.SemaphoreType.DMA((size * peers,)), pltpu.SemaphoreType.DMA((size * peers,)),
                                               pltpu.SemaphoreType.DMA((size,))))
    thru = tuple(pltpu.HBM(s.shape, s.dtype) for s in srcs) + tuple(pltpu.HBM(shp, s.dtype) for shp, s in zip(land_shapes, srcs))
    ins = [pltpu.with_memory_space_constraint(s, pltpu.HBM) for s in srcs]
    ins += [pltpu.with_memory_space_constraint(lax.empty(shp, s.dtype), pltpu.HBM) for shp, s in zip(land_shapes, srcs)]
    out = pl.pallas_call(
        body, name=name, out_shape=sems + thru + (jax.ShapeDtypeStruct((SUBLANE, LANE), F32),),
        in_specs=[HBM] * (2 * n) + [ANY] * len(extra),
        out_specs=(SEM,) * len(sems) + (HBM,) * (2 * n) + (pl.BlockSpec(memory_space=pltpu.VMEM),),
        input_output_aliases={i: len(sems) + i for i in range(2 * n)},
        compiler_params=pltpu.CompilerParams(has_side_effects=EFFECT))(*ins, *extra)
    arrays = out[len(sems):-1]
    started = [tuple(out[3 * g:3 * g + 3]) + tuple(arrays[at:at + size]) + tuple(arrays[n + at:n + at + size])
               for g, (at, size) in enumerate(zip(starts, sizes))]
    return started, out[-1]


def _exchange_wait(started, after, kind, *, name):
    n = (len(started) - 3) // 2
    sems, arrays = started[:3], started[3:]

    def body(*refs):
        srcs_, lands = refs[:n], refs[n:2 * n]
        send_sems, recv_sems, local_sems = refs[2 * n:2 * n + 3]
        for cp in _direct_copies(srcs_, lands, send_sems, recv_sems, local_sems, kind):
            cp.wait()

    out = pl.pallas_call(
        body, name=name, out_shape=tuple(pltpu.HBM(a.shape, a.dtype) for a in arrays),
        in_specs=[HBM] * (2 * n) + [SEM] * 3 + [ANY], out_specs=(HBM,) * (2 * n),
        input_output_aliases={i: i for i in range(2 * n)},
        compiler_params=pltpu.CompilerParams(has_side_effects=EFFECT))(*arrays, *sems, after)
    return out[n:]


def _adamw_reduce(w, parts, m, v, *, name, after=None):
    layers, r, c = w.shape
    assert len(parts) == layers
    senders = parts[0].shape[0]
    tr = _tile(r, 512, 16)
    tiles = r // tr
    bc1 = 1.0 - ADAM_B1 ** ADAM_STEP
    bc2 = 1.0 - ADAM_B2 ** ADAM_STEP

    def body(w_ref, *rest):
        p_refs = rest[:layers]
        m_ref, v_ref, g_ref, d_ref, nm_ref, nv_ref = rest[layers:]

        def update(p_ref):
            g = p_ref[0, :, pl.ds(0, c)].astype(F32)
            for s in range(1, senders):
                g = g + p_ref[s, :, pl.ds(0, c)].astype(F32)
            nm = ADAM_B1 * m_ref[0] + (1.0 - ADAM_B1) * g
            nv = ADAM_B2 * v_ref[0] + (1.0 - ADAM_B2) * (g * g)
            g_ref[0] = g
            nm_ref[0] = nm
            nv_ref[0] = nv
            d_ref[0] = -ADAM_LR * ((nm / bc1) / (jnp.sqrt(nv / bc2) + ADAM_EPS) + ADAM_WD * w_ref[0])

        for layer in range(layers):
            pl.when(pl.program_id(0) == layer)(functools.partial(update, p_refs[layer]))

    def part_spec(layer, shape):
        rest = 0 if layer > 0 else tiles - 1
        return pl.BlockSpec((senders, tr, shape[2]), lambda l, i: (0, jnp.where(l == layer, i, rest), 0))

    spec = pl.BlockSpec((1, tr, c), lambda l, i: (l, i, 0))
    out = jax.ShapeDtypeStruct((layers, r, c), F32)
    return _pcall(body, name=name, out_shape=(out,) * 4, grid=(layers, tiles),
                  in_specs=[spec] + [part_spec(layer, p.shape) for layer, p in enumerate(parts)] + [spec, spec],
                  out_specs=(spec,) * 4, semantics=("arbitrary", "arbitrary"), vmem_limit=VMEM_LIMIT, after=after)(w, *parts, m, v)


def _pool_windows():
    return jnp.repeat(jnp.asarray(POOL_WINDOWS, F32), POOL_DIM // len(POOL_WINDOWS))[None, :]


def _block_diag_pairs(pool_w):
    z = jnp.zeros_like(pool_w[0])
    return jnp.stack([jnp.block([[pool_w[2 * b], z], [z, pool_w[2 * b + 1]]]) for b in range(2)])


def _pad_lanes(vec):
    return jnp.zeros((1, LANE), F32).at[0, :vec.shape[0]].set(vec)


FF_SHARD = D_FF // N_DEV
FF_BLOCK = 384
D_FF_PAD = N_DEV * FF_BLOCK


def _layer_fwd(x, p_i, wt, fetch):
    wt = {**wt, **fetch(0, x)}
    h1 = _rmsnorm_fwd(x, wt["norm1_g"], name="rmsnorm_fwd")
    proj = _matmul(h1, wt["w_in"], "nn", name="mm_in")
    qkv = _qkv_prep_fwd(proj, wt["conv_qkv"], name="qkv_prep_fwd")
    g, beta = _gates_fwd(proj, wt["a_log"], wt["dt_bias"], name="gates_fwd")
    u, w, qg, kg, attn, tmats = _deltanet_prep(qkv, g, beta, name="deltanet_prep")
    o, vn, states = _deltanet_scan(u, w, qg, kg, attn, g, name="deltanet_scan")
    o_a = _apost_fwd(o, proj, wt["onorm_g"], name="apost_fwd")
    o_b = _pool_fwd(proj, wt["pool_win"], wt["pool_wbd"], wt["pool_scale"], name="pool_fwd")
    o_c = _sconv_fwd(proj, wt["sconv_w"], name="sconv_fwd")
    mixed = jnp.concatenate([o_a, o_b, o_c], axis=1)
    wt.update(fetch(1, mixed))
    x1 = _matmul(mixed, wt["w_out"], "nn", res=x, name="mm_out")
    h2 = _rmsnorm_fwd(x1, wt["norm2_g"], name="rmsnorm_fwd")
    wt.update(fetch(2, h2))
    ff, gate, up = _swiglu_fwd(h2, wt["w_gate"], wt["w_up"], name="swiglu_fwd")
    wt.update(fetch(3, ff))
    x2 = _matmul(ff, wt["w_down"], "nn", res=x1, name="mm_down")
    wt.update(fetch(4, x2))
    pgl = _matmul(x2, wt["ple_gate"], "nn", name="mm_pleg")
    pp = _matmul(p_i, wt["ple_proj"], "nn", b_blocked=True, name="mm_plep")
    x3 = _ple_fwd(x2, pgl, pp, name="ple_fwd")
    saved = dict(x=x, h1=h1, proj=proj, qkv=qkv, g=g, beta=beta, o=o, states=states, tmats=tmats, mixed=mixed, x1=x1, h2=h2,
                 gate=gate, up=up, ff=ff, x2=x2, pgl=pgl, pp=pp, p=p_i, w=w, qg=qg, kg=kg, attn=attn, vn=vn, wt=wt)
    return x3, saved


def _col_blocks(g):
    a = g.shape[0]
    return jnp.transpose(g.reshape(a, N_DEV, -1), (1, 0, 2))


def _cols_joined(blocks):
    return jnp.transpose(blocks, (1, 0, 2)).reshape(blocks.shape[1], -1)


def _layer_bwd(dx3, sv, emit, after=None):
    gr, big = {}, {}
    wt = sv["wt"]
    rows = D_MODEL // N_DEV
    dpgl, dpp = _ple_bwd(dx3, sv["pgl"], sv["pp"], name="ple_bwd", after=after)
    big["ple_proj"] = _matmul(sv["p"], dpp, "tn", out_blocked=(N_DEV, rows), out_dtype=BF16, name="mm_dplep")
    big["ple_gate"] = _matmul(sv["x2"], dpgl, "tn", out_dtype=BF16, name="mm_dpleg").reshape(N_DEV, rows, D_MODEL)
    dx2 = _matmul(dpgl, wt["ple_gate"], "nt", res=dx3, name="mm_dx2")
    big["w_down"] = _matmul(sv["ff"], dx2, "tn", out_dtype=BF16, name="mm_ddown").reshape(N_DEV, FF_BLOCK, D_MODEL)
    dgate, dup = _swiglu_bwd(dx2, wt["w_down"], sv["gate"], sv["up"], name="swiglu_bwd", after=emit(0, big))
    big["w_gate"] = _matmul(sv["h2"], dgate, "tn", out_blocked=(N_DEV, FF_BLOCK), out_dtype=BF16, name="mm_dgate")
    big["w_up"] = _matmul(sv["h2"], dup, "tn", out_blocked=(N_DEV, FF_BLOCK), out_dtype=BF16, name="mm_dup")
    dh2 = _matmul(dgate, wt["w_gate"], "nt", b_blocked=True, name="mm_dh2_gate")
    dh2 = _matmul(dup, wt["w_up"], "nt", b_blocked=True, res=dh2, name="mm_dh2_up")
    dx1, gr["norm2_g"] = _rmsnorm_bwd(sv["x1"], wt["norm2_g"], dh2, dx2, name="rmsnorm_bwd")
    big["w_out"] = _matmul(sv["mixed"], dx1, "tn", out_dtype=BF16, name="mm_dout").reshape(N_DEV, rows, D_MODEL)
    dmixed = _matmul(dx1, wt["w_out"], "nt", name="mm_dmixed", after=emit(1, big))
    proj = sv["proj"]
    dcb, dcc, dch, dsconv = _sconv_bwd(proj, wt["sconv_w"], dmixed, name="sconv_bwd")
    big["sconv_w"] = _col_blocks(dsconv)
    dhp, dwbd, gr["pool_scale"] = _pool_bwd(proj, wt["pool_win"], wt["pool_wbd"], wt["pool_scale"], dmixed, name="pool_bwd")
    half = LANE // 2
    gr["pool_w"] = jnp.stack([dwbd[0, :half, :half], dwbd[0, half:, half:], dwbd[1, :half, :half], dwbd[1, half:, half:]])
    do, dz, gr["onorm_g"] = _apost_bwd(sv["o"], proj, wt["onorm_g"], dmixed, name="apost_bwd")
    dvn, dstates = _deltanet_bscan(sv["w"], sv["qg"], sv["kg"], sv["attn"], sv["g"], do, name="deltanet_bscan")
    dqkv_h, dg, dbeta = _deltanet_post(sv["qkv"], sv["g"], sv["beta"], sv["tmats"], sv["states"], dstates, do, dvn, sv["vn"],
                                       name="deltanet_post")
    dab, dalog, ddtb = _gates_bwd(proj, wt["a_log"], wt["dt_bias"], dg, dbeta, name="gates_bwd")
    gr["a_log"], gr["dt_bias"] = dalog[0, :HEADS], ddtb[0, :HEADS]
    dqkv, dconv = _qkv_prep_bwd(proj, wt["conv_qkv"], dqkv_h, name="qkv_prep_bwd")
    big["conv_qkv"] = _col_blocks(dconv)
    dproj = jnp.concatenate([dqkv, dz, dab, dhp, dcb, dcc, dch], axis=1)
    dwin = _matmul(sv["h1"], dproj, "tn", out_dtype=BF16, name="mm_din")
    big["w_in"] = _col_blocks(jnp.concatenate([dwin[:, :AB_COL + 2 * HEADS], dwin[:, AB_COL + LANE:]], axis=1))
    dh1 = _matmul(dproj, wt["w_in"], "nt", name="mm_dh1", after=emit(2, big))
    dx, gr["norm1_g"] = _rmsnorm_bwd(sv["x"], wt["norm1_g"], dh1, dx1, name="rmsnorm_bwd")
    return dx, gr


FETCH_GROUPS = (("w_in", "conv_qkv", "sconv_w"), ("w_out",), ("w_gate", "w_up"), ("w_down",), ("ple_gate", "ple_proj"))
EMIT_GROUPS = (("ple_proj", "ple_gate", "w_down"), ("w_gate", "w_up", "w_out"), ("w_in", "conv_qkv", "sconv_w"))


def _small_weights(w, i):
    return dict(
        norm1_g=w["norm1_g"][i][None], norm2_g=w["norm2_g"][i][None], onorm_g=w["onorm_g"][i][None],
        a_log=_pad_lanes(w["a_log"][i]), dt_bias=_pad_lanes(w["dt_bias"][i]),
        pool_scale=w["pool_scale"][i][None], pool_win=_pool_windows(), pool_wbd=_block_diag_pairs(w["pool_w"][i]))


def _as_read(name, gathered):
    if name == "w_in":
        w_in = _cols_joined(gathered)
        return jnp.concatenate([w_in[:, :AB_COL + 2 * HEADS], jnp.zeros((D_MODEL, LANE - 2 * HEADS), BF16),
                                w_in[:, AB_COL + 2 * HEADS:]], axis=1)
    if name in ("conv_qkv", "sconv_w"):
        return _cols_joined(gathered)
    if name in ("w_gate", "w_up", "ple_proj"):
        return gathered
    return gathered.reshape(-1, D_MODEL)


def _layer_weights(gathered, w, i):
    return {**_small_weights(w, i), **{k: _as_read(k, g) for k, g in gathered.items()}}


def _local_step(x, p, target, layers, final_g):
    saved = []
    h = x
    for i in range(DEPTH):
        replicated = {k: v for k, v in layers[i].items() if k not in SHARDED}
        h, sv = _layer_fwd(h, p[i], replicated, lambda group, after, i=i: {k: layers[i][k] for k in FETCH_GROUPS[group]})
        saved.append(sv)
    dx, dgf, loss = _loss_head(h, final_g, target, name="loss_head")
    big, small = [{} for _ in range(DEPTH)], [None] * DEPTH
    for i in reversed(range(DEPTH)):
        dx, small[i] = _layer_bwd(dx, saved[i], lambda group, blocks, i=i: big[i].update({k: blocks[k] for k in EMIT_GROUPS[group]}))
    return loss, dx, big, small, dgf


SHARDED = ("w_in", "w_gate", "w_up", "w_down", "w_out", "ple_gate", "ple_proj", "conv_qkv", "sconv_w")
SMALL = ("norm1_g", "a_log", "dt_bias", "onorm_g", "pool_w", "pool_scale", "norm2_g", "final_g")
SLAB_COLS = 1024


def _payload(name, shard):
    if name in ("conv_qkv", "sconv_w"):
        return shard
    out = shard.astype(BF16)
    if name in ("w_gate", "w_up"):
        out = jnp.pad(out, ((0, 0), (0, FF_BLOCK - FF_SHARD)))
    if name == "w_down":
        out = jnp.pad(out, ((0, FF_BLOCK - FF_SHARD), (0, 0)))
    return out


def _slab_rows(shape):
    size = 1
    for s in shape:
        size *= s
    return SUBLANE * -(-size // (SUBLANE * SLAB_COLS))


def _pack_slab(parts, extra_row):
    rows = []
    for name in SMALL:
        flat = parts[name].reshape(-1)
        nrow = _slab_rows(parts[name].shape)
        rows.append(jnp.pad(flat, (0, nrow * SLAB_COLS - flat.shape[0])).reshape(nrow, SLAB_COLS))
    rows.append(jnp.pad(extra_row, ((0, SUBLANE - 1), (0, 0))))
    return jnp.concatenate(rows, axis=0)


def _unpack_slab(slab, shapes):
    out, row = {}, 0
    for name in SMALL:
        size = 1
        for s in shapes[name]:
            size *= s
        out[name] = slab[row:row + _slab_rows(shapes[name])].reshape(-1)[:size].reshape(shapes[name])
        row += _slab_rows(shapes[name])
    return out, row


def kernel(x, p, norm1_g, w_in, conv_qkv, a_log, dt_bias, onorm_g, pool_w, pool_scale, sconv_w, w_out, norm2_g, w_gate, w_up, w_down, ple_proj, ple_gate, final_g, loss_target, m_norm1_g, m_w_in, m_conv_qkv, m_a_log, m_dt_bias, m_onorm_g, m_pool_w, m_pool_scale, m_sconv_w, m_w_out, m_norm2_g, m_w_gate, m_w_up, m_w_down, m_ple_proj, m_ple_gate, m_final_g, v_norm1_g, v_w_in, v_conv_qkv, v_a_log, v_dt_bias, v_onorm_g, v_pool_w, v_pool_scale, v_sconv_w, v_w_out, v_norm2_g, v_w_gate, v_w_up, v_w_down, v_ple_proj, v_ple_gate, v_final_g):
    names = ["norm1_g", "w_in", "conv_qkv", "a_log", "dt_bias", "onorm_g", "pool_w", "pool_scale", "sconv_w", "w_out", "norm2_g",
             "w_gate", "w_up", "w_down", "ple_proj", "ple_gate", "final_g"]
    w = dict(zip(names, [norm1_g, w_in, conv_qkv, a_log, dt_bias, onorm_g, pool_w, pool_scale, sconv_w, w_out, norm2_g, w_gate, w_up,
                         w_down, ple_proj, ple_gate, final_g]))
    m = dict(zip(names, [m_norm1_g, m_w_in, m_conv_qkv, m_a_log, m_dt_bias, m_onorm_g, m_pool_w, m_pool_scale, m_sconv_w, m_w_out,
                         m_norm2_g, m_w_gate, m_w_up, m_w_down, m_ple_proj, m_ple_gate, m_final_g]))
    v = dict(zip(names, [v_norm1_g, v_w_in, v_conv_qkv, v_a_log, v_dt_bias, v_onorm_g, v_pool_w, v_pool_scale, v_sconv_w, v_w_out,
                         v_norm2_g, v_w_gate, v_w_up, v_w_down, v_ple_proj, v_ple_gate, v_final_g]))

    first, rest = FETCH_GROUPS[0], tuple(k for members in FETCH_GROUPS[1:] for k in members)
    gathered = dict(zip(first, _all_gather([_payload(k, w[k][0]) for k in first], name="all_gather_weights")))
    (flying0,), token = _exchange_start([[_payload(k, w[k][0]) for k in rest]], CHIP_GATHER, name="gather_start_0")
    replicated = [_small_weights(w, i) for i in range(DEPTH)]
    replicated[0]["norm1_g"] = replicated[0]["norm1_g"] + token[0, 0]
    flying1 = []

    def fetch(i, group, after):
        if i == 0 and group == 1:
            landed = _exchange_wait(flying0, after, CHIP_GATHER, name="gather_wait_0")
            gathered.update(zip(rest, _pair_swap(landed, name="pair_swap")))
            started, token = _exchange_start([[_payload(k, w[k][1]) for k in SHARDED]], CHIP_GATHER, name="gather_start_1",
                                             after=gathered[rest[0]])
            flying1.extend(started)
            return {**{k: _as_read(k, gathered[k]) for k in FETCH_GROUPS[group]}, "norm2_g": replicated[0]["norm2_g"] + token[0, 0]}
        if i == 1 and group == 0:
            landed = _exchange_wait(flying1[0], after, CHIP_GATHER, name="gather_wait_1")
            gathered.update(zip(SHARDED, _pair_swap(landed, name="pair_swap")))
        return {k: _as_read(k, gathered[k]) for k in FETCH_GROUPS[group]}

    def reduce_scatter_start(members, blocks, tag):
        mine = [blocks[k] for k in members]
        theirs = _pair_exchange(mine, name="pair_exchange")
        sums = [_pair_add(a, b, name="pair_add") for a, b in zip(mine, theirs)]
        (started,), token = _exchange_start([sums], CHIP_SCATTER, name="exchange_start_" + tag)
        return started, token

    h, saved0 = _layer_fwd(x[0], p[0, 0], replicated[0], functools.partial(fetch, 0))
    h, saved1 = _layer_fwd(h, p[1, 0], replicated[1], functools.partial(fetch, 1))
    dx, dgf, loss_part = _loss_head(h, final_g[None], loss_target[0], name="loss_head")
    small, big1, flying0 = [None] * DEPTH, {}, []
    dx, small[1] = _layer_bwd(dx, saved1, lambda group, blocks: big1.update({k: blocks[k] for k in EMIT_GROUPS[group]}))
    flying1, token = reduce_scatter_start(SHARDED, big1, "1")

    def emit(group, blocks):
        started, token = reduce_scatter_start(EMIT_GROUPS[group], blocks, f"0_{group}")
        flying0.append(started)
        return token

    dx, small[0] = _layer_bwd(dx, saved0, emit, after=token)
    received = [{}, dict(zip(SHARDED, _exchange_wait(flying1, dx, CHIP_SCATTER, name="exchange_wait_1")))]
    for group, members in enumerate(EMIT_GROUPS):
        received[0].update(zip(members, _exchange_wait(flying0[group], dx, CHIP_SCATTER, name=f"exchange_wait_0_{group}")))

    grads = {k: jnp.stack([small[i][k] for i in range(DEPTH)]) for k in small[0]}
    grads = {k: g[:, 0] if k in ("norm1_g", "norm2_g", "onorm_g", "pool_scale") else g for k, g in grads.items()}
    grads["final_g"] = dgf[0]
    loss_row = jnp.pad(loss_part, ((0, 0), (0, SLAB_COLS - LANE)))
    (small_flying,), token = _exchange_start([[_pack_slab(grads, loss_row)]], GATHER, name="small_gather_start")

    out_g, out_d, out_m, out_v = {}, {}, {}, {}
    for k in SHARDED:
        out_g[k], out_d[k], out_m[k], out_v[k] = _adamw_reduce(w[k], [received[i][k] for i in range(DEPTH)], m[k], v[k],
                                                                name="adamw_" + k, after=token)
    behind_all = jnp.stack([out_v[k][0, 0, 0] for k in SHARDED])
    (small_parts,) = _exchange_wait(small_flying, behind_all, GATHER, name="small_gather_wait")
    zero_row = jnp.zeros((1, SLAB_COLS), F32)
    slabs = _adamw_reduce(_pack_slab(w, zero_row)[None], [small_parts], _pack_slab(m, zero_row)[None],
                          _pack_slab(v, zero_row)[None], name="adamw_small")
    slabs = [s[0] for s in slabs]
    shapes = {k: w[k].shape for k in SMALL}
    for dst, slab in zip((out_g, out_d, out_m, out_v), slabs):
        vals, _ = _unpack_slab(slab, shapes)
        dst.update(vals)
    _, loss_at = _unpack_slab(slabs[0], shapes)
    loss = slabs[0][loss_at, 0]

    return (loss, dx[None], *[out_g[k] for k in names], *[out_d[k] for k in names], *[out_m[k] for k in names],
            *[out_v[k] for k in names])
```

```python
import functools

import jax
import jax.numpy as jnp
from jax import lax
from jax.experimental import pallas as pl
from jax.experimental.pallas import tpu as pltpu

F32 = jnp.float32
BF16 = jnp.bfloat16

D_MODEL = 1024
DEPTH = 2
PLE_DIM = 256
EPS = 1e-6
HEAD_DIM = 128
HEADS = 4
A_DIM = HEADS * HEAD_DIM
QKV_TAPS = 4
CHUNK = 64
POOL_WINDOWS = (2, 4, 8, 16)
POOL_DIM = 256
CONV_DIM = 256
CONV_TAPS = 3
D_FF = 2816
D_IN = 3080
D_IN_PAD = 3200
AB_COL = 2048
N_DEV = 8

ADAM_LR = 0.001
ADAM_B1 = 0.9
ADAM_B2 = 0.999
ADAM_EPS = 1e-08
ADAM_WD = 0.01
ADAM_STEP = 10

LANE = 128
SUBLANE = 8
VMEM_BYTES_V7X = 64 * 1024 * 1024
VMEM_LIMIT = 48 * 1024 * 1024

_HI = lax.Precision.HIGHEST
NN = ((1,), (0,))
NT = ((1,), (1,))
TN = ((0,), (0,))
MESH = pl.DeviceIdType.MESH


def _dot(a, b, dims, hi=False):
    if hi:
        return lax.dot_general(a, b, (dims, ((), ())), precision=_HI, preferred_element_type=F32)
    return lax.dot_general(a.astype(BF16), b.astype(BF16), (dims, ((), ())), preferred_element_type=F32)


def _pcall(body, *, name, out_shape, grid=(), in_specs=None, out_specs=None, scratch_shapes=(), semantics=None,
           vmem_limit=None, after=None, **kw):
    params = {}
    if semantics is not None:
        params["dimension_semantics"] = semantics
    if vmem_limit is not None:
        params["vmem_limit_bytes"] = vmem_limit
    if after is not None:
        n_in, inner = len(in_specs), body
        body = lambda *refs: inner(*refs[:n_in], *refs[n_in + 1:])
        in_specs = list(in_specs) + [pl.BlockSpec(after.shape, lambda *_: (0,) * after.ndim)]
    call = pl.pallas_call(
        body, name=name, out_shape=out_shape, grid=grid, in_specs=in_specs, out_specs=out_specs,
        scratch_shapes=list(scratch_shapes), compiler_params=pltpu.CompilerParams(**params), **kw)
    return call if after is None else (lambda *args: call(*args, after))


def _sigmoid(x):
    return 1.0 / (1.0 + jnp.exp(-x))


def _softplus(x):
    return jnp.maximum(x, 0.0) + jnp.log(1.0 + jnp.exp(-jnp.abs(x)))


def _tile(n, cap, mult):
    if n <= cap:
        return n
    best = None
    for t in range(mult, cap + 1, mult):
        if n % t == 0:
            best = t
    assert best is not None, (n, cap, mult)
    return best


ROWS_PER_STEP = 512
NARROW_RESULT = 1024
COLS_PER_DOT = 640


def _resident(weight):
    return pl.BlockSpec(weight.shape, lambda i: (0,) * weight.ndim, pipeline_mode=pl.Buffered(1))


def _matmul_rows(a, b, mode, *, name, res=None, out_dtype=F32, b_blocked=False, after=None):
    m, k = a.shape
    if b_blocked:
        nb, _, bw = b.shape
        n = nb * bw if mode == "nn" else b.shape[1]
    else:
        n = b.shape[1] if mode == "nn" else b.shape[0]
    tm = _tile(m, ROWS_PER_STEP if n > NARROW_RESULT else 2 * ROWS_PER_STEP, 16)
    cn = bw if (b_blocked and mode == "nn") else _tile(n, COLS_PER_DOT, LANE)
    has_res = res is not None

    def body(*refs):
        a_ref, b_ref = refs[0], refs[1]
        res_ref = refs[2] if has_res else None
        o_ref = refs[2 + has_res]
        if not (b_blocked and mode == "nt"):
            av = a_ref[...].astype(BF16)
        for j in range(n // cn):
            cols = pl.ds(j * cn, cn)
            if mode == "nn":
                part = _dot(av, b_ref[j] if b_blocked else b_ref[:, cols], NN)
            elif not b_blocked:
                part = _dot(av, b_ref[cols, :], NT)
            else:
                part = None
                for s in range(nb):
                    term = _dot(a_ref[:, pl.ds(s * bw, bw)], b_ref[s, cols, :], NT)
                    part = term if part is None else part + term
            if has_res:
                part = part + res_ref[:, cols]
            o_ref[:, cols] = part.astype(o_ref.dtype)

    row = lambda width: pl.BlockSpec((tm, width), lambda i: (i, 0))
    whole = _resident(b)
    ins = [a, b] + ([res] if has_res else [])
    specs = [row(k), whole] + ([row(n)] if has_res else [])
    return _pcall(body, name=name, out_shape=jax.ShapeDtypeStruct((m, n), out_dtype), grid=(m // tm,), in_specs=specs,
                  out_specs=row(n), semantics=("parallel",), vmem_limit=VMEM_LIMIT, after=after)(*ins)


def _matmul(a, b, mode, *, name, res=None, out_dtype=F32, b_blocked=False, out_blocked=None, after=None):
    if mode != "tn":
        return _matmul_rows(a, b, mode, name=name, res=res, out_dtype=out_dtype, b_blocked=b_blocked, after=after)
    assert res is None and not b_blocked and after is None
    (t, m), (t2, n) = a.shape, b.shape
    assert t == t2, (a.shape, b.shape)
    tm = _tile(m, 1024, LANE)
    tn = _tile(n, COLS_PER_DOT, LANE)
    if out_blocked is not None:
        assert out_blocked[0] * out_blocked[1] == n
        tn = out_blocked[1]

    def body(a_ref, b_ref, o_ref):
        part = _dot(a_ref[...], b_ref[...], TN).astype(o_ref.dtype)
        if out_blocked is None:
            o_ref[...] = part
        else:
            o_ref[0] = part

    o_spec = (pl.BlockSpec((tm, tn), lambda i, j: (i, j)) if out_blocked is None
              else pl.BlockSpec((1, tm, tn), lambda i, j: (j, i, 0)))
    o_shape = (m, n) if out_blocked is None else (out_blocked[0], m, out_blocked[1])
    return _pcall(body, name=name, out_shape=jax.ShapeDtypeStruct(o_shape, out_dtype), grid=(m // tm, n // tn),
                  in_specs=[pl.BlockSpec((t, tm), lambda i, j: (0, i)), pl.BlockSpec((t, tn), lambda i, j: (0, j))],
                  out_specs=o_spec, semantics=("parallel", "parallel"), vmem_limit=VMEM_LIMIT)(a, b)


ROW_TILE = 512


def _rows(t, width, idx=0):
    return pl.BlockSpec((ROW_TILE, width), lambda i: (i, idx))


def _vec(width):
    return pl.BlockSpec((1, width), lambda i: (0, 0))


def _rmsnorm_fwd(x, g, *, name):
    t, d = x.shape

    def body(x_ref, g_ref, h_ref):
        xv = x_ref[...]
        r = lax.rsqrt(jnp.mean(xv * xv, axis=-1, keepdims=True) + EPS)
        h_ref[...] = (xv * r * g_ref[...]).astype(BF16)

    return _pcall(body, name=name, out_shape=jax.ShapeDtypeStruct((t, d), BF16), grid=(t // ROW_TILE,),
                  in_specs=[_rows(t, d), _vec(d)], out_specs=_rows(t, d), semantics=("parallel",))(x, g)


def _rmsnorm_bwd(x, g, dh, dres, *, name):
    t, d = x.shape

    def body(x_ref, g_ref, dh_ref, dres_ref, dx_ref, dg_ref):
        xv = x_ref[...]
        r = lax.rsqrt(jnp.mean(xv * xv, axis=-1, keepdims=True) + EPS)
        xhat = xv * r
        dhv = dh_ref[...].astype(F32)
        dhg = dhv * g_ref[...]
        dx_ref[...] = dres_ref[...] + r * (dhg - xhat * jnp.mean(dhg * xhat, axis=-1, keepdims=True))
        part = jnp.sum(dhv * xhat, axis=0, keepdims=True)

        @pl.when(pl.program_id(0) == 0)
        def _():
            dg_ref[...] = part

        @pl.when(pl.program_id(0) > 0)
        def _():
            dg_ref[...] += part

    return _pcall(body, name=name, out_shape=(jax.ShapeDtypeStruct((t, d), F32), jax.ShapeDtypeStruct((1, d), F32)),
                  grid=(t // ROW_TILE,), in_specs=[_rows(t, d), _vec(d), _rows(t, d), _rows(t, d)],
                  out_specs=(_rows(t, d), _vec(d)), semantics=("arbitrary",))(x, g, dh, dres)


def _swiglu_fwd(h, w_gate, w_up, *, name):
    t, k = h.shape
    nb, _, bw = w_gate.shape
    tm = _tile(t, ROWS_PER_STEP, 16)

    def body(h_ref, wg_ref, wu_ref, ff_ref, gate_ref, up_ref):
        hv = h_ref[...]
        for j in range(nb):
            cols = pl.ds(j * bw, bw)
            gv = _dot(hv, wg_ref[j], NN)
            uv = _dot(hv, wu_ref[j], NN)
            gate_ref[:, cols] = gv.astype(BF16)
            up_ref[:, cols] = uv.astype(BF16)
            ff_ref[:, cols] = (gv * _sigmoid(gv) * uv).astype(BF16)

    row = lambda width: pl.BlockSpec((tm, width), lambda i: (i, 0))
    out = jax.ShapeDtypeStruct((t, nb * bw), BF16)
    return _pcall(body, name=name, out_shape=(out,) * 3, grid=(t // tm,), in_specs=[row(k), _resident(w_gate), _resident(w_up)],
                  out_specs=(row(nb * bw),) * 3, semantics=("parallel",), vmem_limit=VMEM_LIMIT)(h, w_gate, w_up)


def _swiglu_bwd(dx2, w_down, gate, up, *, name, after=None):
    t, d = dx2.shape
    f = w_down.shape[0]
    tm = _tile(t, ROWS_PER_STEP, 16)
    cn = _tile(f, COLS_PER_DOT, LANE)

    def body(dx_ref, w_ref, gate_ref, up_ref, dgate_ref, dup_ref):
        dxv = dx_ref[...].astype(BF16)
        for j in range(f // cn):
            cols = pl.ds(j * cn, cn)
            dffv = _dot(dxv, w_ref[cols, :], NT)
            gv = gate_ref[:, cols].astype(F32)
            sig = _sigmoid(gv)
            dgate_ref[:, cols] = (dffv * up_ref[:, cols].astype(F32) * sig * (1.0 + gv * (1.0 - sig))).astype(BF16)
            dup_ref[:, cols] = (dffv * gv * sig).astype(BF16)

    row = lambda width: pl.BlockSpec((tm, width), lambda i: (i, 0))
    out = jax.ShapeDtypeStruct((t, f), BF16)
    return _pcall(body, name=name, out_shape=(out, out), grid=(t // tm,), in_specs=[row(d), _resident(w_down), row(f), row(f)],
                  out_specs=(row(f), row(f)), semantics=("parallel",), vmem_limit=VMEM_LIMIT, after=after)(dx2, w_down, gate, up)


def _ple_fwd(x2, pgl, pp, *, name):
    t, d = x2.shape

    def body(x_ref, pgl_ref, pp_ref, o_ref):
        o_ref[...] = x_ref[...] + _sigmoid(pgl_ref[...]) * pp_ref[...]

    return _pcall(body, name=name, out_shape=jax.ShapeDtypeStruct((t, d), F32), grid=(t // ROW_TILE,),
                  in_specs=[_rows(t, d)] * 3, out_specs=_rows(t, d), semantics=("parallel",))(x2, pgl, pp)


def _ple_bwd(dx3, pgl, pp, *, name, after=None):
    t, d = dx3.shape

    def body(dx_ref, pgl_ref, pp_ref, dpgl_ref, dpp_ref):
        dxv = dx_ref[...]
        sig = _sigmoid(pgl_ref[...])
        dpp_ref[...] = (dxv * sig).astype(BF16)
        dpgl_ref[...] = (dxv * pp_ref[...] * sig * (1.0 - sig)).astype(BF16)

    return _pcall(body, name=name, out_shape=(jax.ShapeDtypeStruct((t, d), BF16),) * 2, grid=(t // ROW_TILE,),
                  in_specs=[_rows(t, d)] * 3, out_specs=(_rows(t, d),) * 2, semantics=("parallel",), after=after)(dx3, pgl, pp)


def _loss_head(x3, g, target, *, name):
    t, d = x3.shape

    def body(x_ref, g_ref, t_ref, dx_ref, dg_ref, loss_ref):
        xv = x_ref[...]
        r = lax.rsqrt(jnp.mean(xv * xv, axis=-1, keepdims=True) + EPS)
        xhat = xv * r
        gv = g_ref[...]
        err = xhat * gv - t_ref[...]
        row_loss = jnp.sum(err * err, axis=-1, keepdims=True) * (0.5 / d)
        lpart = jnp.broadcast_to(jnp.sum(row_loss, axis=0, keepdims=True), (1, LANE))
        dy = err * (1.0 / d)
        dyg = dy * gv
        dx_ref[...] = r * (dyg - xhat * jnp.mean(dyg * xhat, axis=-1, keepdims=True))
        gpart = jnp.sum(dy * xhat, axis=0, keepdims=True)

        @pl.when(pl.program_id(0) == 0)
        def _():
            dg_ref[...] = gpart
            loss_ref[...] = lpart

        @pl.when(pl.program_id(0) > 0)
        def _():
            dg_ref[...] += gpart
            loss_ref[...] += lpart

    return _pcall(body, name=name,
                  out_shape=(jax.ShapeDtypeStruct((t, d), F32), jax.ShapeDtypeStruct((1, d), F32), jax.ShapeDtypeStruct((1, LANE), F32)),
                  grid=(t // ROW_TILE,), in_specs=[_rows(t, d), _vec(d), _rows(t, d)],
                  out_specs=(_rows(t, d), _vec(d), _vec(LANE)), semantics=("arbitrary",))(x3, g, target)


def _shift_down(x, d):
    if d == 0:
        return x
    row = lax.broadcasted_iota(jnp.int32, x.shape, 0)
    return jnp.where(row >= d, pltpu.roll(x, d, 0), 0.0)


def _shift_up(x, d):
    if d == 0:
        return x
    t = x.shape[0]
    row = lax.broadcasted_iota(jnp.int32, x.shape, 0)
    return jnp.where(row < t - d, pltpu.roll(x, t - d, 0), 0.0)


def _colsum(x):
    return jnp.sum(x, axis=0, keepdims=True)


def _col(t, idx_fn):
    return pl.BlockSpec((t, LANE), idx_fn)


def _conv_fwd(x, w_ref, taps):
    acc = None
    for j in range(taps):
        term = w_ref[pl.ds(j, 1), :] * _shift_down(x, taps - 1 - j)
        acc = term if acc is None else acc + term
    return acc


def _conv_bwd(x, dy, w_ref, dw_ref, taps):
    dx = None
    for j in range(taps):
        term = w_ref[pl.ds(j, 1), :] * _shift_up(dy, taps - 1 - j)
        dx = term if dx is None else dx + term
        dw_ref[pl.ds(j, 1), :] = _colsum(dy * _shift_down(x, taps - 1 - j))
    return dx


def _qkv_prep_fwd(proj, conv_w, *, name):
    t = proj.shape[0]
    scale = HEAD_DIM ** -0.5

    def body(x_ref, w_ref, o_ref):
        j = pl.program_id(0)
        c = _conv_fwd(x_ref[...], w_ref, QKV_TAPS)
        s = c * _sigmoid(c)
        r = lax.rsqrt(jnp.sum(s * s, axis=-1, keepdims=True) + EPS)
        f = jnp.where(j < 2 * HEADS, r, 1.0) * jnp.where(j < HEADS, scale, 1.0)
        o_ref[0] = s * f

    return _pcall(body, name=name, out_shape=jax.ShapeDtypeStruct((3 * HEADS, t, LANE), F32), grid=(3 * HEADS,),
                  in_specs=[_col(t, lambda j: (0, j)), pl.BlockSpec((QKV_TAPS, LANE), lambda j: (0, j))],
                  out_specs=pl.BlockSpec((1, t, LANE), lambda j: (j, 0, 0)), semantics=("parallel",),
                  vmem_limit=VMEM_LIMIT)(proj, conv_w)


def _qkv_prep_bwd(proj, conv_w, dqkv, *, name):
    t = proj.shape[0]
    scale = HEAD_DIM ** -0.5

    def body(x_ref, w_ref, d_ref, dx_ref, dw_ref):
        j = pl.program_id(0)
        xv = x_ref[...]
        c = _conv_fwd(xv, w_ref, QKV_TAPS)
        sig = _sigmoid(c)
        s = c * sig
        r = lax.rsqrt(jnp.sum(s * s, axis=-1, keepdims=True) + EPS)
        n0 = s * r
        dv = d_ref[0]
        dn0 = dv * jnp.where(j < HEADS, scale, 1.0)
        ds_norm = r * (dn0 - n0 * jnp.sum(dn0 * n0, axis=-1, keepdims=True))
        ds = jnp.where(j < 2 * HEADS, ds_norm, dv)
        dc = ds * sig * (1.0 + c * (1.0 - sig))
        dx_ref[...] = _conv_bwd(xv, dc, w_ref, dw_ref, QKV_TAPS).astype(BF16)

    return _pcall(body, name=name,
                  out_shape=(jax.ShapeDtypeStruct((t, 3 * A_DIM), BF16), jax.ShapeDtypeStruct((QKV_TAPS, 3 * A_DIM), F32)),
                  grid=(3 * HEADS,),
                  in_specs=[_col(t, lambda j: (0, j)), pl.BlockSpec((QKV_TAPS, LANE), lambda j: (0, j)),
                            pl.BlockSpec((1, t, LANE), lambda j: (j, 0, 0))],
                  out_specs=(_col(t, lambda j: (0, j)), pl.BlockSpec((QKV_TAPS, LANE), lambda j: (0, j))),
                  semantics=("parallel",), vmem_limit=VMEM_LIMIT)(proj, conv_w, dqkv)


def _lane_pick(x, lane_idx, lane):
    return jnp.broadcast_to(jnp.sum(jnp.where(lane == lane_idx, x, 0.0), axis=-1, keepdims=True), x.shape)


def _gates_fwd(proj, alog, dtb, *, name):
    t = proj.shape[0]

    def body(x_ref, alog_ref, dtb_ref, g_ref, b_ref):
        xv = x_ref[...]
        lane = lax.broadcasted_iota(jnp.int32, xv.shape, 1)
        gall = -jnp.exp(alog_ref[...]) * _softplus(xv + dtb_ref[...])
        ball = _sigmoid(xv)
        for h in range(HEADS):
            g_ref[h] = _lane_pick(gall, h, lane)
            b_ref[h] = _lane_pick(ball, HEADS + h, lane)

    out = jax.ShapeDtypeStruct((HEADS, t, LANE), F32)
    whole = pl.BlockSpec((HEADS, t, LANE), lambda i: (0, 0, 0))
    return _pcall(body, name=name, out_shape=(out, out), grid=(1,),
                  in_specs=[_col(t, lambda i: (0, AB_COL // LANE)), _vec(LANE), _vec(LANE)], out_specs=(whole, whole),
                  semantics=("arbitrary",), vmem_limit=VMEM_LIMIT)(proj, alog, dtb)


def _gates_bwd(proj, alog, dtb, dg, dbeta, *, name):
    t = proj.shape[0]

    def body(x_ref, alog_ref, dtb_ref, dg_ref, db_ref, dab_ref, dalog_ref, ddtb_ref):
        xv = x_ref[...]
        lane = lax.broadcasted_iota(jnp.int32, xv.shape, 1)
        lane1 = lax.broadcasted_iota(jnp.int32, (1, LANE), 1)
        z = xv + dtb_ref[...]
        nea = -jnp.exp(alog_ref[...])
        da_f = nea * _sigmoid(z)
        g_f = nea * _softplus(z)
        ball = _sigmoid(xv)
        db_f = ball * (1.0 - ball)
        dab = jnp.zeros_like(xv)
        dalog = jnp.zeros((1, LANE), F32)
        for h in range(HEADS):
            dgh = dg_ref[h]
            dab = dab + jnp.where(lane == h, dgh * da_f, 0.0) + jnp.where(lane == HEADS + h, db_ref[h] * db_f, 0.0)
            dalog = dalog + jnp.where(lane1 == h, _colsum(dgh * g_f), 0.0)
        dab_ref[...] = dab.astype(BF16)
        dalog_ref[...] = dalog
        ddtb_ref[...] = jnp.where(lane1 < HEADS, _colsum(dab), 0.0)

    whole = pl.BlockSpec((HEADS, t, LANE), lambda i: (0, 0, 0))
    vec = jax.ShapeDtypeStruct((1, LANE), F32)
    return _pcall(body, name=name, out_shape=(jax.ShapeDtypeStruct((t, LANE), BF16), vec, vec), grid=(1,),
                  in_specs=[_col(t, lambda i: (0, AB_COL // LANE)), _vec(LANE), _vec(LANE), whole, whole],
                  out_specs=(_col(t, lambda i: (0, 0)), _vec(LANE), _vec(LANE)), semantics=("arbitrary",),
                  vmem_limit=VMEM_LIMIT)(proj, alog, dtb, dg, dbeta)


Z_COL = 3 * A_DIM // LANE


def _apost_fwd(o, proj, gn, *, name):
    t = proj.shape[0]

    def body(o_ref, z_ref, gn_ref, y_ref):
        ov = o_ref[0]
        z = z_ref[...]
        r = lax.rsqrt(jnp.mean(ov * ov, axis=-1, keepdims=True) + EPS)
        y_ref[...] = (ov * r * gn_ref[...] * (z * _sigmoid(z))).astype(BF16)

    return _pcall(body, name=name, out_shape=jax.ShapeDtypeStruct((t, A_DIM), BF16), grid=(HEADS,),
                  in_specs=[pl.BlockSpec((1, t, LANE), lambda h: (h, 0, 0)), _col(t, lambda h: (0, Z_COL + h)),
                            pl.BlockSpec((1, LANE), lambda h: (0, 0))],
                  out_specs=_col(t, lambda h: (0, h)), semantics=("parallel",), vmem_limit=VMEM_LIMIT)(o, proj, gn)


def _apost_bwd(o, proj, gn, dmixed, *, name):
    t = proj.shape[0]

    def body(o_ref, z_ref, gn_ref, d_ref, do_ref, dz_ref, dgn_ref):
        ov = o_ref[0]
        z = z_ref[...]
        gnv = gn_ref[...]
        dv = d_ref[...]
        r = lax.rsqrt(jnp.mean(ov * ov, axis=-1, keepdims=True) + EPS)
        ohat = ov * r
        sig = _sigmoid(z)
        dy = dv * (z * sig)
        dz_ref[...] = (dv * ohat * gnv * sig * (1.0 + z * (1.0 - sig))).astype(BF16)
        dyo = dy * gnv
        do_ref[0] = r * (dyo - ohat * jnp.mean(dyo * ohat, axis=-1, keepdims=True))
        part = _colsum(dy * ohat)

        @pl.when(pl.program_id(0) == 0)
        def _():
            dgn_ref[...] = part

        @pl.when(pl.program_id(0) > 0)
        def _():
            dgn_ref[...] += part

    return _pcall(body, name=name,
                  out_shape=(jax.ShapeDtypeStruct((HEADS, t, LANE), F32), jax.ShapeDtypeStruct((t, A_DIM), BF16),
                             jax.ShapeDtypeStruct((1, LANE), F32)),
                  grid=(HEADS,),
                  in_specs=[pl.BlockSpec((1, t, LANE), lambda h: (h, 0, 0)), _col(t, lambda h: (0, Z_COL + h)),
                            pl.BlockSpec((1, LANE), lambda h: (0, 0)), _col(t, lambda h: (0, h))],
                  out_specs=(pl.BlockSpec((1, t, LANE), lambda h: (h, 0, 0)), _col(t, lambda h: (0, h)),
                             pl.BlockSpec((1, LANE), lambda h: (0, 0))),
                  semantics=("arbitrary",), vmem_limit=VMEM_LIMIT)(o, proj, gn, dmixed)


POOL_COL = (AB_COL + LANE) // LANE
CB_COL = POOL_COL + POOL_DIM // LANE
CC_COL = CB_COL + CONV_DIM // LANE
CH_COL = CC_COL + CONV_DIM // LANE
MAX_WIN_LOG2 = 4


def _window_sums(x, shift):
    sums = []
    cur = x
    for k in range(MAX_WIN_LOG2):
        cur = cur + shift(cur, 1 << k)
        sums.append(cur)
    return sums


def _pick_window(sums, win):
    out = sums[-1]
    for k in range(MAX_WIN_LOG2 - 2, -1, -1):
        out = jnp.where(win == float(2 << k), sums[k], out)
    return out


def _pool_counts(shape, win):
    row = lax.broadcasted_iota(jnp.int32, shape, 0).astype(F32)
    return jnp.minimum(row + 1.0, win)


def _pool_fwd(proj, win, wbd, scale, *, name):
    t = proj.shape[0]

    def body(x_ref, win_ref, w_ref, s_ref, y_ref):
        xv = x_ref[...]
        winv = win_ref[...]
        pooled = _pick_window(_window_sums(xv, _shift_down), winv) / _pool_counts(xv.shape, winv) - xv
        y_ref[...] = (_dot(pooled, w_ref[0], NN) * s_ref[...]).astype(BF16)

    nb = POOL_DIM // LANE
    vec = pl.BlockSpec((1, LANE), lambda b: (0, b))
    return _pcall(body, name=name, out_shape=jax.ShapeDtypeStruct((t, POOL_DIM), BF16), grid=(nb,),
                  in_specs=[_col(t, lambda b: (0, POOL_COL + b)), vec, pl.BlockSpec((1, LANE, LANE), lambda b: (b, 0, 0)), vec],
                  out_specs=_col(t, lambda b: (0, b)), semantics=("parallel",), vmem_limit=VMEM_LIMIT)(proj, win, wbd, scale)


def _pool_bwd(proj, win, wbd, scale, dmixed, *, name):
    t = proj.shape[0]

    def body(x_ref, win_ref, w_ref, s_ref, d_ref, dx_ref, dw_ref, ds_ref):
        xv = x_ref[...]
        winv = win_ref[...]
        cnt = _pool_counts(xv.shape, winv)
        pooled = _pick_window(_window_sums(xv, _shift_down), winv) / cnt - xv
        dv = d_ref[...]
        ds_ref[...] = _colsum(dv * _dot(pooled, w_ref[0], NN))
        dy0 = dv * s_ref[...]
        dw_ref[0] = _dot(pooled, dy0, TN)
        dpooled = _dot(dy0, w_ref[0], NT)
        dmean = dpooled / cnt
        dx_ref[...] = (_pick_window(_window_sums(dmean, _shift_up), winv) - dpooled).astype(BF16)

    nb = POOL_DIM // LANE
    vec = pl.BlockSpec((1, LANE), lambda b: (0, b))
    mat = pl.BlockSpec((1, LANE, LANE), lambda b: (b, 0, 0))
    first = A_DIM // LANE
    return _pcall(body, name=name,
                  out_shape=(jax.ShapeDtypeStruct((t, POOL_DIM), BF16), jax.ShapeDtypeStruct((nb, LANE, LANE), F32),
                             jax.ShapeDtypeStruct((1, POOL_DIM), F32)),
                  grid=(nb,),
                  in_specs=[_col(t, lambda b: (0, POOL_COL + b)), vec, mat, vec, _col(t, lambda b: (0, first + b))],
                  out_specs=(_col(t, lambda b: (0, b)), mat, vec), semantics=("parallel",),
                  vmem_limit=VMEM_LIMIT)(proj, win, wbd, scale, dmixed)


def _sconv_fwd(proj, w, *, name):
    t = proj.shape[0]

    def body(cb_ref, cc_ref, ch_ref, w_ref, y_ref):
        y_ref[...] = (cb_ref[...] * _conv_fwd(cc_ref[...] * ch_ref[...], w_ref, CONV_TAPS)).astype(BF16)

    nb = CONV_DIM // LANE
    return _pcall(body, name=name, out_shape=jax.ShapeDtypeStruct((t, CONV_DIM), BF16), grid=(nb,),
                  in_specs=[_col(t, lambda b: (0, CB_COL + b)), _col(t, lambda b: (0, CC_COL + b)),
                            _col(t, lambda b: (0, CH_COL + b)), pl.BlockSpec((CONV_TAPS, LANE), lambda b: (0, b))],
                  out_specs=_col(t, lambda b: (0, b)), semantics=("parallel",), vmem_limit=VMEM_LIMIT)(proj, proj, proj, w)


def _sconv_bwd(proj, w, dmixed, *, name):
    t = proj.shape[0]

    def body(cb_ref, cc_ref, ch_ref, w_ref, d_ref, dcb_ref, dcc_ref, dch_ref, dw_ref):
        cc = cc_ref[...]
        ch = ch_ref[...]
        u = cc * ch
        dv = d_ref[...]
        dcb_ref[...] = (dv * _conv_fwd(u, w_ref, CONV_TAPS)).astype(BF16)
        du = _conv_bwd(u, dv * cb_ref[...], w_ref, dw_ref, CONV_TAPS)
        dcc_ref[...] = (du * ch).astype(BF16)
        dch_ref[...] = (du * cc).astype(BF16)

    nb = CONV_DIM // LANE
    first = (A_DIM + POOL_DIM) // LANE
    act = jax.ShapeDtypeStruct((t, CONV_DIM), BF16)
    wspec = pl.BlockSpec((CONV_TAPS, LANE), lambda b: (0, b))
    ospec = _col(t, lambda b: (0, b))
    return _pcall(body, name=name, out_shape=(act, act, act, jax.ShapeDtypeStruct((CONV_TAPS, CONV_DIM), F32)), grid=(nb,),
                  in_specs=[_col(t, lambda b: (0, CB_COL + b)), _col(t, lambda b: (0, CC_COL + b)),
                            _col(t, lambda b: (0, CH_COL + b)), wspec, _col(t, lambda b: (0, first + b))],
                  out_specs=(ospec, ospec, ospec, wspec), semantics=("parallel",),
                  vmem_limit=VMEM_LIMIT)(proj, proj, proj, w, dmixed)


def _chunk_masks():
    r = lax.broadcasted_iota(jnp.int32, (CHUNK, CHUNK), 0)
    c = lax.broadcasted_iota(jnp.int32, (CHUNK, CHUNK), 1)
    return r >= c, r > c, jnp.where(r == c, 1.0, 0.0).astype(F32)


def _split(a):
    hi = a.astype(BF16)
    return hi, (a - hi.astype(F32)).astype(BF16)


def _dot_split(a, b, dims):
    (ah, al), (bh, bl) = a, b
    return _dot(ah, bh, dims) + _dot(ah, bl, dims) + _dot(al, bh, dims)


def _tri_inv(lows, eye):
    xs = [eye - low for low in lows]
    ps = [_split(low) for low in lows]
    ps = [_split(_dot_split(p, p, NN)) for p in ps]
    for i in range(5):
        xs = [x + _dot_split(_split(x), p, NN) for x, p in zip(xs, ps)]
        if i < 4:
            ps = [_split(_dot_split(p, p, NN)) for p in ps]
    return xs


def _prefix_sum_rows(x):
    for k in range(6):
        x = x + _shift_down(x, 1 << k)
    return x


def _suffix_sum_rows(x):
    for k in range(6):
        x = x + _shift_up(x, 1 << k)
    return x


def _chunk_decay(g, incl):
    gcb = _prefix_sum_rows(g)
    gtot = _colsum(g)
    col = gcb[:, :CHUNK]
    row = gcb.T[:CHUNK, :]
    decay = jnp.exp(jnp.where(incl, col - row, -1e30))
    return gcb, gtot, decay


CHUNKS_PER_STEP = 4


def _heads_of(ref, base, rows):
    return [ref[base + h, rows, :] for h in range(HEADS)]


def _chunk_rows(j):
    return pl.ds(j * CHUNK, CHUNK)


def _deltanet_prep(qkv, g, beta, *, name):
    t = qkv.shape[1]
    n_chunks = t // CHUNK
    per = CHUNKS_PER_STEP
    probs = [(j, h) for j in range(per) for h in range(HEADS)]

    def body(qkv_ref, g_ref, b_ref, u_ref, w_ref, qg_ref, kg_ref, attn_ref, tm_ref):
        incl, strict, eye = _chunk_masks()
        q = [qkv_ref[h, _chunk_rows(j), :] for j, h in probs]
        k = [qkv_ref[HEADS + h, _chunk_rows(j), :] for j, h in probs]
        v = [qkv_ref[2 * HEADS + h, _chunk_rows(j), :] for j, h in probs]
        bv = [b_ref[h, _chunk_rows(j), :] for j, h in probs]
        dec = [_chunk_decay(g_ref[h, _chunk_rows(j), :], incl) for j, h in probs]
        kb = [a * b for a, b in zip(k, bv)]
        low = [jnp.where(strict, _dot(a, b, NT) * d[2], 0.0) for a, b, d in zip(kb, k, dec)]
        tm = _tri_inv(low, eye)
        egc = [jnp.exp(d[0]) for d in dec]
        u = [_dot(m, a * b, NN) for m, a, b in zip(tm, v, bv)]
        w = [_dot(m, a * e, NN) for m, a, e in zip(tm, kb, egc)]
        attn = [_dot(a, b, NT) * d[2] for a, b, d in zip(q, k, dec)]
        for i, (j, h) in enumerate(probs):
            rows = _chunk_rows(j)
            u_ref[h, rows, :] = u[i]
            w_ref[h, rows, :] = w[i].astype(BF16)
            qg_ref[h, rows, :] = (q[i] * egc[i]).astype(BF16)
            kg_ref[h, rows, :] = (k[i] * jnp.exp(dec[i][1] - dec[i][0])).astype(BF16)
            attn_ref[j, h] = attn[i].astype(BF16)
            tm_ref[j, h] = tm[i]

    act = lambda heads: pl.BlockSpec((heads, per * CHUNK, LANE), lambda n: (0, n, 0))
    mat = pl.BlockSpec((per, HEADS, CHUNK, CHUNK), lambda n: (n, 0, 0, 0))
    return _pcall(
        body, name=name,
        out_shape=(jax.ShapeDtypeStruct((HEADS, t, LANE), F32),) + (jax.ShapeDtypeStruct((HEADS, t, LANE), BF16),) * 3
        + (jax.ShapeDtypeStruct((n_chunks, HEADS, CHUNK, CHUNK), BF16), jax.ShapeDtypeStruct((n_chunks, HEADS, CHUNK, CHUNK), F32)),
        grid=(n_chunks // per,), in_specs=[act(3 * HEADS), act(HEADS), act(HEADS)],
        out_specs=(act(HEADS),) * 4 + (mat, mat), semantics=("parallel",), vmem_limit=VMEM_LIMIT)(qkv, g, beta)


SCAN_CHUNKS_PER_STEP = 8


def _deltanet_scan(u, w, qg, kg, attn, g, *, name):
    t = u.shape[1]
    n_chunks = t // CHUNK
    per = SCAN_CHUNKS_PER_STEP

    def body(u_ref, w_ref, qg_ref, kg_ref, attn_ref, g_ref, o_ref, vn_ref, st_ref, s_ref):
        @pl.when(pl.program_id(0) == 0)
        def _():
            s_ref[...] = jnp.zeros_like(s_ref)

        for j in range(per):
            rows = _chunk_rows(j)
            s = [s_ref[h] for h in range(HEADS)]
            vn = [u_ref[h, rows, :] - _dot(w_ref[h, rows, :], s[h], NN) for h in range(HEADS)]
            o = [_dot(qg_ref[h, rows, :], s[h], NN) + _dot(attn_ref[j, h], vn[h], NN) for h in range(HEADS)]
            eg = [jnp.exp(_colsum(g_ref[h, rows, :])) for h in range(HEADS)]
            for h in range(HEADS):
                st_ref[j, h] = s[h]
                s_ref[h] = s[h] * eg[h] + _dot(kg_ref[h, rows, :], vn[h], TN)
                o_ref[h, rows, :] = o[h]
                vn_ref[h, rows, :] = vn[h]

    act = pl.BlockSpec((HEADS, per * CHUNK, LANE), lambda n: (0, n, 0))
    out = jax.ShapeDtypeStruct((HEADS, t, LANE), F32)
    return _pcall(
        body, name=name, out_shape=(out, out, jax.ShapeDtypeStruct((n_chunks, HEADS, LANE, LANE), F32)), grid=(n_chunks // per,),
        in_specs=[act] * 4 + [pl.BlockSpec((per, HEADS, CHUNK, CHUNK), lambda n: (n, 0, 0, 0)), act],
        out_specs=(act, act, pl.BlockSpec((per, HEADS, LANE, LANE), lambda n: (n, 0, 0, 0))),
        scratch_shapes=[pltpu.VMEM((HEADS, LANE, LANE), F32)], semantics=("arbitrary",))(u, w, qg, kg, attn, g)


def _deltanet_bscan(w, qg, kg, attn, g, do, *, name):
    t = w.shape[1]
    n_chunks = t // CHUNK
    per = SCAN_CHUNKS_PER_STEP
    steps = n_chunks // per

    def body(w_ref, qg_ref, kg_ref, attn_ref, g_ref, do_ref, dvn_ref, dsn_ref, ds_ref):
        @pl.when(pl.program_id(0) == 0)
        def _():
            ds_ref[...] = jnp.zeros_like(ds_ref)

        for j in reversed(range(per)):
            rows = _chunk_rows(j)
            dsn = [ds_ref[h] for h in range(HEADS)]
            dov = [do_ref[h, rows, :] for h in range(HEADS)]
            dvn = [_dot(attn_ref[j, h], dov[h], TN) + _dot(kg_ref[h, rows, :], dsn[h], NN) for h in range(HEADS)]
            eg = [jnp.exp(_colsum(g_ref[h, rows, :])) for h in range(HEADS)]
            for h in range(HEADS):
                dsn_ref[j, h] = dsn[h]
                ds_ref[h] = _dot(qg_ref[h, rows, :], dov[h], TN) + eg[h] * dsn[h] - _dot(w_ref[h, rows, :], dvn[h], TN)
                dvn_ref[h, rows, :] = dvn[h]

    act = pl.BlockSpec((HEADS, per * CHUNK, LANE), lambda n: (0, steps - 1 - n, 0))
    return _pcall(
        body, name=name,
        out_shape=(jax.ShapeDtypeStruct((HEADS, t, LANE), F32), jax.ShapeDtypeStruct((n_chunks, HEADS, LANE, LANE), F32)),
        grid=(steps,),
        in_specs=[act] * 3 + [pl.BlockSpec((per, HEADS, CHUNK, CHUNK), lambda n: (steps - 1 - n, 0, 0, 0)), act, act],
        out_specs=(act, pl.BlockSpec((per, HEADS, LANE, LANE), lambda n: (steps - 1 - n, 0, 0, 0))),
        scratch_shapes=[pltpu.VMEM((HEADS, LANE, LANE), F32)], semantics=("arbitrary",))(w, qg, kg, attn, g, do)


def _sum_all(x):
    return jnp.sum(jnp.sum(x, axis=1, keepdims=True), axis=0, keepdims=True)


def _rowsum(x):
    return jnp.sum(x, axis=1, keepdims=True)


def _deltanet_post(qkv, g, beta, tmats, states, dstates, do, dvn, vn, *, name):
    t = qkv.shape[1]
    n_chunks = t // CHUNK
    per = CHUNKS_PER_STEP
    probs = [(j, h) for j in range(per) for h in range(HEADS)]

    def body(qkv_ref, g_ref, b_ref, tm_ref, st_ref, dsn_ref, do_ref, dvn_ref, vn_ref, dqkv_ref, dg_ref, db_ref):
        incl, strict, _ = _chunk_masks()
        ones = jnp.ones((CHUNK, LANE), BF16)
        last_row = lax.broadcasted_iota(jnp.int32, (CHUNK, LANE), 0) == CHUNK - 1
        z = lambda f, *cols: [f(*a) for a in zip(*cols)]
        q = [qkv_ref[h, _chunk_rows(j), :] for j, h in probs]
        k = [qkv_ref[HEADS + h, _chunk_rows(j), :] for j, h in probs]
        v = [qkv_ref[2 * HEADS + h, _chunk_rows(j), :] for j, h in probs]
        bv = [b_ref[h, _chunk_rows(j), :] for j, h in probs]
        dov = [do_ref[h, _chunk_rows(j), :] for j, h in probs]
        dvn_ = [dvn_ref[h, _chunk_rows(j), :] for j, h in probs]
        vn_ = [vn_ref[h, _chunk_rows(j), :] for j, h in probs]
        tm = [tm_ref[j, h] for j, h in probs]
        s = [st_ref[j, h] for j, h in probs]
        dsn = [dsn_ref[j, h] for j, h in probs]
        dec = [_chunk_decay(g_ref[h, _chunk_rows(j), :], incl) for j, h in probs]
        decay = [d[2] for d in dec]
        egc = [jnp.exp(d[0]) for d in dec]
        ekg = [jnp.exp(d[1] - d[0]) for d in dec]
        kb = z(lambda a, b: a * b, k, bv)
        vb = z(lambda a, b: a * b, v, bv)
        kbg = z(lambda a, b: a * b, kb, egc)
        qg = z(lambda a, b: a * b, q, egc)
        kg = z(lambda a, b: a * b, k, ekg)
        kk = z(lambda a, b: _dot(a, b, NT), kb, k)
        qk = z(lambda a, b: _dot(a, b, NT), q, k)
        dattn = z(lambda a, b: jnp.where(incl, _dot(a, b, NT), 0.0), dov, vn_)
        dqg = z(lambda a, b: _dot(a, b, NT), dov, s)
        dkg = z(lambda a, b: _dot(a, b, NT), vn_, dsn)
        dglast = z(lambda a, b, c, d, e: _sum_all(a * b) * jnp.exp(e[1]) + _sum_all(c * d), s, dsn, dkg, kg, dec)
        dw = z(lambda a, b: -_dot(a, b, NT), dvn_, s)
        dtm = z(lambda a, b, c, d: _dot(a, b, NT) + _dot(c, d, NT), dvn_, vb, dw, kbg)
        dvb = z(lambda a, b: _dot(a, b, TN), tm, dvn_)
        dkbg = z(lambda a, b: _dot(a, b, TN), tm, dw)
        dlow = z(lambda a, b: jnp.where(strict, -_dot(_dot(a, b, TN), a, NT), 0.0), tm, dtm)
        dkk = z(lambda a, b: a * b, dlow, decay)
        dqk = z(lambda a, b: a * b, dattn, decay)
        dkb = z(lambda a, b, c, d: _dot(a, b, NN) + c * d, dkk, k, dkbg, egc)
        dk = z(lambda a, b, c, d, e, f, g_, h_: _dot(a, b, TN) + _dot(c, d, TN) + e * f + g_ * h_, dkk, kb, dqk, q, dkg, ekg, dkb, bv)
        dq = z(lambda a, b, c, d: _dot(a, b, NN) + c * d, dqk, k, dqg, egc)
        m = z(lambda a, b, c, d, e: (a * b + c * d) * e, dlow, kk, dattn, qk, decay)
        mcol = [_dot(mh, ones, TN) + _dot(ml, ones, TN) for mh, ml in (_split(a) for a in m)]
        for i, (j, h) in enumerate(probs):
            rows = _chunk_rows(j)
            dqkv_ref[h, rows, :] = dq[i]
            dqkv_ref[HEADS + h, rows, :] = dk[i]
            dqkv_ref[2 * HEADS + h, rows, :] = dvb[i] * bv[i]
            db_ref[h, rows, :] = jnp.broadcast_to(_rowsum(dkb[i] * k[i] + dvb[i] * v[i]), (CHUNK, LANE))
            dgc = (_rowsum(dqg[i] * qg[i] + dkbg[i] * kbg[i] - dkg[i] * kg[i]) + _rowsum(m[i]) - mcol[i]
                   + jnp.where(last_row, dglast[i], 0.0))
            dg_ref[h, rows, :] = _suffix_sum_rows(dgc)

    act = lambda heads: pl.BlockSpec((heads, per * CHUNK, LANE), lambda n: (0, n, 0))
    mat = lambda d: pl.BlockSpec((per, HEADS, d, d), lambda n: (n, 0, 0, 0))
    out = jax.ShapeDtypeStruct((HEADS, t, LANE), F32)
    return _pcall(
        body, name=name, out_shape=(jax.ShapeDtypeStruct((3 * HEADS, t, LANE), F32), out, out), grid=(n_chunks // per,),
        in_specs=[act(3 * HEADS), act(HEADS), act(HEADS), mat(CHUNK), mat(LANE), mat(LANE), act(HEADS), act(HEADS), act(HEADS)],
        out_specs=(act(3 * HEADS), act(HEADS), act(HEADS)), semantics=("parallel",),
        vmem_limit=VMEM_LIMIT)(qkv, g, beta, tmats, states, dstates, do, dvn, vn)


ANY = pl.BlockSpec(memory_space=pl.ANY)
PEERS = N_DEV - 1


def _all_gather(arrays, *, name):
    n = len(arrays)

    def body(*refs):
        ins, outs = refs[:n], refs[n:2 * n]
        send_sems, recv_sems, local_sems = refs[2 * n:]
        x, y, c = lax.axis_index("x"), lax.axis_index("y"), lax.axis_index("c")
        me, sibling = (x, y, c), (x, y, 1 - c)
        chips = [(1 - x, y), (x, 1 - y), (1 - x, 1 - y)]

        def copy(a, k, block, to, src=None):
            dst = outs[a].at[4 * block[0] + 2 * block[1] + block[2]]
            return pltpu.make_async_remote_copy(src_ref=dst if src is None else src, dst_ref=dst, send_sem=send_sems.at[a * PEERS + k],
                                                recv_sem=recv_sems.at[a * PEERS + k], device_id=to, device_id_type=MESH)

        local = [pltpu.make_async_copy(ins[a], outs[a].at[4 * x + 2 * y + c], local_sems.at[a]) for a in range(n)]
        for cp in local:
            cp.start()
        first = []
        for a in range(n):
            first.append(copy(a, 0, me, sibling, src=ins[a]))
            first += [copy(a, 1 + j, me, (*chip, c), src=ins[a]) for j, chip in enumerate(chips)]
        for cp in first:
            cp.start()
        passed = []
        for a in range(n):
            for j, chip in enumerate(chips):
                copy(a, 1 + j, (*chip, c), me).wait_recv()
                fwd = copy(a, 4 + j, (*chip, c), sibling)
                fwd.start()
                passed.append(fwd)
        for a in range(n):
            copy(a, 0, sibling, me).wait_recv()
            for j, chip in enumerate(chips):
                copy(a, 4 + j, (*chip, 1 - c), me).wait_recv()
        for cp in first + passed:
            cp.wait_send()
        for cp in local:
            cp.wait()

    return _pcall(body, name=name, out_shape=tuple(jax.ShapeDtypeStruct((N_DEV,) + a.shape, a.dtype) for a in arrays),
                  in_specs=[ANY] * n, out_specs=(ANY,) * n,
                  scratch_shapes=[pltpu.SemaphoreType.DMA((n * PEERS,)), pltpu.SemaphoreType.DMA((n * PEERS,)),
                                  pltpu.SemaphoreType.DMA((n,))])(*arrays)


CHIPS = 4


def _pair_exchange(arrays, *, name):
    n = len(arrays)

    def body(*refs):
        ins, outs = refs[:n], refs[n:2 * n]
        send_sems, recv_sems = refs[2 * n:]
        x, y, c = lax.axis_index("x"), lax.axis_index("y"), lax.axis_index("c")
        copies = []
        for a in range(n):
            for q in range(CHIPS):
                cp = pltpu.make_async_remote_copy(src_ref=ins[a].at[2 * q + 1 - c], dst_ref=outs[a].at[q],
                                                  send_sem=send_sems.at[a * CHIPS + q], recv_sem=recv_sems.at[a * CHIPS + q],
                                                  device_id=(x, y, 1 - c), device_id_type=MESH)
                cp.start()
                copies.append(cp)
        for cp in copies:
            cp.wait()

    return _pcall(body, name=name, out_shape=tuple(jax.ShapeDtypeStruct((CHIPS,) + a.shape[1:], a.dtype) for a in arrays),
                  in_specs=[ANY] * n, out_specs=(ANY,) * n,
                  scratch_shapes=[pltpu.SemaphoreType.DMA((n * CHIPS,)), pltpu.SemaphoreType.DMA((n * CHIPS,))])(*arrays)


def _pair_add(blocks, theirs, *, name):
    _, r, c_ = blocks.shape
    tr = _tile(r, 512, 16)

    def body(mine_ref, theirs_ref, o_ref):
        core = lax.axis_index("c")
        own = jnp.where(core == 0, mine_ref[0, 0].astype(F32), mine_ref[0, 1].astype(F32))
        o_ref[0] = (own + theirs_ref[0].astype(F32)).astype(o_ref.dtype)

    spec = pl.BlockSpec((1, tr, c_), lambda q, i: (q, i, 0))
    return _pcall(body, name=name, out_shape=jax.ShapeDtypeStruct(theirs.shape, theirs.dtype), grid=(CHIPS, r // tr),
                  in_specs=[pl.BlockSpec((1, 2, tr, c_), lambda q, i: (q, 0, i, 0)), spec], out_specs=spec,
                  semantics=("parallel", "parallel"), vmem_limit=VMEM_LIMIT)(blocks.reshape(CHIPS, 2, r, c_), theirs)


HBM = pl.BlockSpec(memory_space=pltpu.HBM)
SEM = pl.BlockSpec(memory_space=pltpu.SEMAPHORE)
EFFECT = pltpu.SideEffectType.DATAFLOW_SIDE_EFFECTING


GATHER, CHIP_GATHER, CHIP_SCATTER = "gather", "chip_gather", "chip_scatter"
PEERS_OF = {GATHER: N_DEV - 1, CHIP_GATHER: CHIPS - 1, CHIP_SCATTER: CHIPS - 1}


def _direct_copies(srcs, lands, send_sems, recv_sems, local_sems, kind):
    x, y, c = lax.axis_index("x"), lax.axis_index("y"), lax.axis_index("c")
    peers = PEERS_OF[kind]
    mine = 2 * x + y if kind == CHIP_SCATTER else 4 * x + 2 * y + c
    copies = []
    for a, (src, land) in enumerate(zip(srcs, lands)):
        copies.append(pltpu.make_async_copy(src.at[mine] if kind == CHIP_SCATTER else src, land.at[mine], local_sems.at[a]))
        for k in range(1, peers + 1):
            bits = k if kind == GATHER else 2 * k
            px = 1 - x if bits & 4 else x
            py = 1 - y if bits & 2 else y
            pc = 1 - c if bits & 1 else c
            copies.append(pltpu.make_async_remote_copy(
                src_ref=src.at[2 * px + py] if kind == CHIP_SCATTER else src, dst_ref=land.at[mine],
                send_sem=send_sems.at[a * peers + k - 1], recv_sem=recv_sems.at[a * peers + k - 1],
                device_id=(px, py, pc), device_id_type=MESH))
    return copies


def _pair_swap(arrays, *, name):
    n = len(arrays)

    def body(*refs):
        mine, zones = refs[:n], refs[n:2 * n]
        send_sems, recv_sems = refs[2 * n:]
        x, y, c = lax.axis_index("x"), lax.axis_index("y"), lax.axis_index("c")
        copies = []
        for a in range(n):
            for q in range(CHIPS):
                copies.append(pltpu.make_async_remote_copy(
                    src_ref=mine[a].at[2 * q + c], dst_ref=zones[a].at[2 * q + c], send_sem=send_sems.at[a * CHIPS + q],
                    recv_sem=recv_sems.at[a * CHIPS + q], device_id=(x, y, 1 - c), device_id_type=MESH))
        for cp in copies:
            cp.start()
        for cp in copies:
            cp.wait()

    return _pcall(body, name=name, out_shape=tuple(jax.ShapeDtypeStruct(a.shape, a.dtype) for a in arrays),
                  in_specs=[ANY] * n, out_specs=(ANY,) * n, input_output_aliases={i: i for i in range(n)},
                  scratch_shapes=[pltpu.SemaphoreType.DMA((n * CHIPS,)), pltpu.SemaphoreType.DMA((n * CHIPS,))])(*arrays)


def _exchange_start(groups, kind, *, name, after=None):
    srcs = [s for group in groups for s in group]
    n = len(srcs)
    sizes = [len(group) for group in groups]
    starts = [sum(sizes[:g]) for g in range(len(groups))]
    land_shapes = [s.shape if kind == CHIP_SCATTER else (N_DEV,) + s.shape for s in srcs]
    peers = PEERS_OF[kind]
    extra = [] if after is None else [after]

    def body(*refs):
        srcs_, lands = refs[:n], refs[n:2 * n]
        token = refs[-1]
        sem_refs = refs[2 * n + len(extra):]
        for g, (at, size) in enumerate(zip(starts, sizes)):
            send_sems, recv_sems, local_sems = sem_refs[3 * g:3 * g + 3]
            for cp in _direct_copies(srcs_[at:at + size], lands[at:at + size], send_sems, recv_sems, local_sems, kind):
                cp.start()
        token[...] = jnp.zeros_like(token)

    sems = tuple(t for size in sizes for t in (pltpu.SemaphoreType.DMA((size * peers,)), pltpu.SemaphoreType.DMA((size * peers,)),
                                               pltpu.SemaphoreType.DMA((size,))))
    thru = tuple(pltpu.HBM(s.shape, s.dtype) for s in srcs) + tuple(pltpu.HBM(shp, s.dtype) for shp, s in zip(land_shapes, srcs))
    ins = [pltpu.with_memory_space_constraint(s, pltpu.HBM) for s in srcs]
    ins += [pltpu.with_memory_space_constraint(lax.empty(shp, s.dtype), pltpu.HBM) for shp, s in zip(land_shapes, srcs)]
    out = pl.pallas_call(
        body, name=name, out_shape=sems + thru + (jax.ShapeDtypeStruct((SUBLANE, LANE), F32),),
        in_specs=[HBM] * (2 * n) + [ANY] * len(extra),
        out_specs=(SEM,) * len(sems) + (HBM,) * (2 * n) + (pl.BlockSpec(memory_space=pltpu.VMEM),),
        input_output_aliases={i: len(sems) + i for i in range(2 * n)},
        compiler_params=pltpu.CompilerParams(has_side_effects=EFFECT))(*ins, *extra)
    arrays = out[len(sems):-1]
    started = [tuple(out[3 * g:3 * g + 3]) + tuple(arrays[at:at + size]) + tuple(arrays[n + at:n + at + size])
               for g, (at, size) in enumerate(zip(starts, sizes))]
    return started, out[-1]


def _exchange_wait(started, after, kind, *, name):
    n = (len(started) - 3) // 2
    sems, arrays = started[:3], started[3:]

    def body(*refs):
        srcs_, lands = refs[:n], refs[n:2 * n]
        send_sems, recv_sems, local_sems = refs[2 * n:2 * n + 3]
        for cp in _direct_copies(srcs_, lands, send_sems, recv_sems, local_sems, kind):
            cp.wait()

    out = pl.pallas_call(
        body, name=name, out_shape=tuple(pltpu.HBM(a.shape, a.dtype) for a in arrays),
        in_specs=[HBM] * (2 * n) + [SEM] * 3 + [ANY], out_specs=(HBM,) * (2 * n),
        input_output_aliases={i: i for i in range(2 * n)},
        compiler_params=pltpu.CompilerParams(has_side_effects=EFFECT))(*arrays, *sems, after)
    return out[n:]


def _adamw_reduce(w, parts, m, v, *, name, after=None):
    layers, r, c = w.shape
    assert len(parts) == layers
    senders = parts[0].shape[0]
    tr = _tile(r, 512, 16)
    tiles = r // tr
    bc1 = 1.0 - ADAM_B1 ** ADAM_STEP
    bc2 = 1.0 - ADAM_B2 ** ADAM_STEP

    def body(w_ref, *rest):
        p_refs = rest[:layers]
        m_ref, v_ref, g_ref, d_ref, nm_ref, nv_ref = rest[layers:]

        def update(p_ref):
            g = p_ref[0, :, pl.ds(0, c)].astype(F32)
            for s in range(1, senders):
                g = g + p_ref[s, :, pl.ds(0, c)].astype(F32)
            nm = ADAM_B1 * m_ref[0] + (1.0 - ADAM_B1) * g
            nv = ADAM_B2 * v_ref[0] + (1.0 - ADAM_B2) * (g * g)
            g_ref[0] = g
            nm_ref[0] = nm
            nv_ref[0] = nv
            d_ref[0] = -ADAM_LR * ((nm / bc1) / (jnp.sqrt(nv / bc2) + ADAM_EPS) + ADAM_WD * w_ref[0])

        for layer in range(layers):
            pl.when(pl.program_id(0) == layer)(functools.partial(update, p_refs[layer]))

    def part_spec(layer, shape):
        rest = 0 if layer > 0 else tiles - 1
        return pl.BlockSpec((senders, tr, shape[2]), lambda l, i: (0, jnp.where(l == layer, i, rest), 0))

    spec = pl.BlockSpec((1, tr, c), lambda l, i: (l, i, 0))
    out = jax.ShapeDtypeStruct((layers, r, c), F32)
    return _pcall(body, name=name, out_shape=(out,) * 4, grid=(layers, tiles),
                  in_specs=[spec] + [part_spec(layer, p.shape) for layer, p in enumerate(parts)] + [spec, spec],
                  out_specs=(spec,) * 4, semantics=("arbitrary", "arbitrary"), vmem_limit=VMEM_LIMIT, after=after)(w, *parts, m, v)


def _pool_windows():
    return jnp.repeat(jnp.asarray(POOL_WINDOWS, F32), POOL_DIM // len(POOL_WINDOWS))[None, :]


def _block_diag_pairs(pool_w):
    z = jnp.zeros_like(pool_w[0])
    return jnp.stack([jnp.block([[pool_w[2 * b], z], [z, pool_w[2 * b + 1]]]) for b in range(2)])


def _pad_lanes(vec):
    return jnp.zeros((1, LANE), F32).at[0, :vec.shape[0]].set(vec)


FF_SHARD = D_FF // N_DEV
FF_BLOCK = 384
D_FF_PAD = N_DEV * FF_BLOCK


def _layer_fwd(x, p_i, wt, fetch):
    wt = {**wt, **fetch(0, x)}
    h1 = _rmsnorm_fwd(x, wt["norm1_g"], name="rmsnorm_fwd")
    proj = _matmul(h1, wt["w_in"], "nn", name="mm_in")
    qkv = _qkv_prep_fwd(proj, wt["conv_qkv"], name="qkv_prep_fwd")
    g, beta = _gates_fwd(proj, wt["a_log"], wt["dt_bias"], name="gates_fwd")
    u, w, qg, kg, attn, tmats = _deltanet_prep(qkv, g, beta, name="deltanet_prep")
    o, vn, states = _deltanet_scan(u, w, qg, kg, attn, g, name="deltanet_scan")
    o_a = _apost_fwd(o, proj, wt["onorm_g"], name="apost_fwd")
    o_b = _pool_fwd(proj, wt["pool_win"], wt["pool_wbd"], wt["pool_scale"], name="pool_fwd")
    o_c = _sconv_fwd(proj, wt["sconv_w"], name="sconv_fwd")
    mixed = jnp.concatenate([o_a, o_b, o_c], axis=1)
    wt.update(fetch(1, mixed))
    x1 = _matmul(mixed, wt["w_out"], "nn", res=x, name="mm_out")
    h2 = _rmsnorm_fwd(x1, wt["norm2_g"], name="rmsnorm_fwd")
    wt.update(fetch(2, h2))
    ff, gate, up = _swiglu_fwd(h2, wt["w_gate"], wt["w_up"], name="swiglu_fwd")
    wt.update(fetch(3, ff))
    x2 = _matmul(ff, wt["w_down"], "nn", res=x1, name="mm_down")
    wt.update(fetch(4, x2))
    pgl = _matmul(x2, wt["ple_gate"], "nn", name="mm_pleg")
    pp = _matmul(p_i, wt["ple_proj"], "nn", b_blocked=True, name="mm_plep")
    x3 = _ple_fwd(x2, pgl, pp, name="ple_fwd")
    saved = dict(x=x, h1=h1, proj=proj, qkv=qkv, g=g, beta=beta, o=o, states=states, tmats=tmats, mixed=mixed, x1=x1, h2=h2,
                 gate=gate, up=up, ff=ff, x2=x2, pgl=pgl, pp=pp, p=p_i, w=w, qg=qg, kg=kg, attn=attn, vn=vn, wt=wt)
    return x3, saved


def _col_blocks(g):
    a = g.shape[0]
    return jnp.transpose(g.reshape(a, N_DEV, -1), (1, 0, 2))


def _cols_joined(blocks):
    return jnp.transpose(blocks, (1, 0, 2)).reshape(blocks.shape[1], -1)


def _layer_bwd(dx3, sv, emit, after=None):
    gr, big = {}, {}
    wt = sv["wt"]
    rows = D_MODEL // N_DEV
    dpgl, dpp = _ple_bwd(dx3, sv["pgl"], sv["pp"], name="ple_bwd", after=after)
    big["ple_proj"] = _matmul(sv["p"], dpp, "tn", out_blocked=(N_DEV, rows), out_dtype=BF16, name="mm_dplep")
    big["ple_gate"] = _matmul(sv["x2"], dpgl, "tn", out_dtype=BF16, name="mm_dpleg").reshape(N_DEV, rows, D_MODEL)
    dx2 = _matmul(dpgl, wt["ple_gate"], "nt", res=dx3, name="mm_dx2")
    big["w_down"] = _matmul(sv["ff"], dx2, "tn", out_dtype=BF16, name="mm_ddown").reshape(N_DEV, FF_BLOCK, D_MODEL)
    dgate, dup = _swiglu_bwd(dx2, wt["w_down"], sv["gate"], sv["up"], name="swiglu_bwd", after=emit(0, big))
    big["w_gate"] = _matmul(sv["h2"], dgate, "tn", out_blocked=(N_DEV, FF_BLOCK), out_dtype=BF16, name="mm_dgate")
    big["w_up"] = _matmul(sv["h2"], dup, "tn", out_blocked=(N_DEV, FF_BLOCK), out_dtype=BF16, name="mm_dup")
    dh2 = _matmul(dgate, wt["w_gate"], "nt", b_blocked=True, name="mm_dh2_gate")
    dh2 = _matmul(dup, wt["w_up"], "nt", b_blocked=True, res=dh2, name="mm_dh2_up")
    dx1, gr["norm2_g"] = _rmsnorm_bwd(sv["x1"], wt["norm2_g"], dh2, dx2, name="rmsnorm_bwd")
    big["w_out"] = _matmul(sv["mixed"], dx1, "tn", out_dtype=BF16, name="mm_dout").reshape(N_DEV, rows, D_MODEL)
    dmixed = _matmul(dx1, wt["w_out"], "nt", name="mm_dmixed", after=emit(1, big))
    proj = sv["proj"]
    dcb, dcc, dch, dsconv = _sconv_bwd(proj, wt["sconv_w"], dmixed, name="sconv_bwd")
    big["sconv_w"] = _col_blocks(dsconv)
    dhp, dwbd, gr["pool_scale"] = _pool_bwd(proj, wt["pool_win"], wt["pool_wbd"], wt["pool_scale"], dmixed, name="pool_bwd")
    half = LANE // 2
    gr["pool_w"] = jnp.stack([dwbd[0, :half, :half], dwbd[0, half:, half:], dwbd[1, :half, :half], dwbd[1, half:, half:]])
    do, dz, gr["onorm_g"] = _apost_bwd(sv["o"], proj, wt["onorm_g"], dmixed, name="apost_bwd")
    dvn, dstates = _deltanet_bscan(sv["w"], sv["qg"], sv["kg"], sv["attn"], sv["g"], do, name="deltanet_bscan")
    dqkv_h, dg, dbeta = _deltanet_post(sv["qkv"], sv["g"], sv["beta"], sv["tmats"], sv["states"], dstates, do, dvn, sv["vn"],
                                       name="deltanet_post")
    dab, dalog, ddtb = _gates_bwd(proj, wt["a_log"], wt["dt_bias"], dg, dbeta, name="gates_bwd")
    gr["a_log"], gr["dt_bias"] = dalog[0, :HEADS], ddtb[0, :HEADS]
    dqkv, dconv = _qkv_prep_bwd(proj, wt["conv_qkv"], dqkv_h, name="qkv_prep_bwd")
    big["conv_qkv"] = _col_blocks(dconv)
    dproj = jnp.concatenate([dqkv, dz, dab, dhp, dcb, dcc, dch], axis=1)
    dwin = _matmul(sv["h1"], dproj, "tn", out_dtype=BF16, name="mm_din")
    big["w_in"] = _col_blocks(jnp.concatenate([dwin[:, :AB_COL + 2 * HEADS], dwin[:, AB_COL + LANE:]], axis=1))
    dh1 = _matmul(dproj, wt["w_in"], "nt", name="mm_dh1", after=emit(2, big))
    dx, gr["norm1_g"] = _rmsnorm_bwd(sv["x"], wt["norm1_g"], dh1, dx1, name="rmsnorm_bwd")
    return dx, gr


FETCH_GROUPS = (("w_in", "conv_qkv", "sconv_w"), ("w_out",), ("w_gate", "w_up"), ("w_down",), ("ple_gate", "ple_proj"))
EMIT_GROUPS = (("ple_proj", "ple_gate", "w_down"), ("w_gate", "w_up", "w_out"), ("w_in", "conv_qkv", "sconv_w"))


def _small_weights(w, i):
    return dict(
        norm1_g=w["norm1_g"][i][None], norm2_g=w["norm2_g"][i][None], onorm_g=w["onorm_g"][i][None],
        a_log=_pad_lanes(w["a_log"][i]), dt_bias=_pad_lanes(w["dt_bias"][i]),
        pool_scale=w["pool_scale"][i][None], pool_win=_pool_windows(), pool_wbd=_block_diag_pairs(w["pool_w"][i]))


def _as_read(name, gathered):
    if name == "w_in":
        w_in = _cols_joined(gathered)
        return jnp.concatenate([w_in[:, :AB_COL + 2 * HEADS], jnp.zeros((D_MODEL, LANE - 2 * HEADS), BF16),
                                w_in[:, AB_COL + 2 * HEADS:]], axis=1)
    if name in ("conv_qkv", "sconv_w"):
        return _cols_joined(gathered)
    if name in ("w_gate", "w_up", "ple_proj"):
        return gathered
    return gathered.reshape(-1, D_MODEL)


def _layer_weights(gathered, w, i):
    return {**_small_weights(w, i), **{k: _as_read(k, g) for k, g in gathered.items()}}


def _local_step(x, p, target, layers, final_g):
    saved = []
    h = x
    for i in range(DEPTH):
        replicated = {k: v for k, v in layers[i].items() if k not in SHARDED}
        h, sv = _layer_fwd(h, p[i], replicated, lambda group, after, i=i: {k: layers[i][k] for k in FETCH_GROUPS[group]})
        saved.append(sv)
    dx, dgf, loss = _loss_head(h, final_g, target, name="loss_head")
    big, small = [{} for _ in range(DEPTH)], [None] * DEPTH
    for i in reversed(range(DEPTH)):
        dx, small[i] = _layer_bwd(dx, saved[i], lambda group, blocks, i=i: big[i].update({k: blocks[k] for k in EMIT_GROUPS[group]}))
    return loss, dx, big, small, dgf


SHARDED = ("w_in", "w_gate", "w_up", "w_down", "w_out", "ple_gate", "ple_proj", "conv_qkv", "sconv_w")
SMALL = ("norm1_g", "a_log", "dt_bias", "onorm_g", "pool_w", "pool_scale", "norm2_g", "final_g")
SLAB_COLS = 1024


def _payload(name, shard):
    if name in ("conv_qkv", "sconv_w"):
        return shard
    out = shard.astype(BF16)
    if name in ("w_gate", "w_up"):
        out = jnp.pad(out, ((0, 0), (0, FF_BLOCK - FF_SHARD)))
    if name == "w_down":
        out = jnp.pad(out, ((0, FF_BLOCK - FF_SHARD), (0, 0)))
    return out


def _slab_rows(shape):
    size = 1
    for s in shape:
        size *= s
    return SUBLANE * -(-size // (SUBLANE * SLAB_COLS))


def _pack_slab(parts, extra_row):
    rows = []
    for name in SMALL:
        flat = parts[name].reshape(-1)
        nrow = _slab_rows(parts[name].shape)
        rows.append(jnp.pad(flat, (0, nrow * SLAB_COLS - flat.shape[0])).reshape(nrow, SLAB_COLS))
    rows.append(jnp.pad(extra_row, ((0, SUBLANE - 1), (0, 0))))
    return jnp.concatenate(rows, axis=0)


def _unpack_slab(slab, shapes):
    out, row = {}, 0
    for name in SMALL:
        size = 1
        for s in shapes[name]:
            size *= s
        out[name] = slab[row:row + _slab_rows(shapes[name])].reshape(-1)[:size].reshape(shapes[name])
        row += _slab_rows(shapes[name])
    return out, row


def kernel(x, p, norm1_g, w_in, conv_qkv, a_log, dt_bias, onorm_g, pool_w, pool_scale, sconv_w, w_out, norm2_g, w_gate, w_up, w_down, ple_proj, ple_gate, final_g, loss_target, m_norm1_g, m_w_in, m_conv_qkv, m_a_log, m_dt_bias, m_onorm_g, m_pool_w, m_pool_scale, m_sconv_w, m_w_out, m_norm2_g, m_w_gate, m_w_up, m_w_down, m_ple_proj, m_ple_gate, m_final_g, v_norm1_g, v_w_in, v_conv_qkv, v_a_log, v_dt_bias, v_onorm_g, v_pool_w, v_pool_scale, v_sconv_w, v_w_out, v_norm2_g, v_w_gate, v_w_up, v_w_down, v_ple_proj, v_ple_gate, v_final_g):
    names = ["norm1_g", "w_in", "conv_qkv", "a_log", "dt_bias", "onorm_g", "pool_w", "pool_scale", "sconv_w", "w_out", "norm2_g",
             "w_gate", "w_up", "w_down", "ple_proj", "ple_gate", "final_g"]
    w = dict(zip(names, [norm1_g, w_in, conv_qkv, a_log, dt_bias, onorm_g, pool_w, pool_scale, sconv_w, w_out, norm2_g, w_gate, w_up,
                         w_down, ple_proj, ple_gate, final_g]))
    m = dict(zip(names, [m_norm1_g, m_w_in, m_conv_qkv, m_a_log, m_dt_bias, m_onorm_g, m_pool_w, m_pool_scale, m_sconv_w, m_w_out,
                         m_norm2_g, m_w_gate, m_w_up, m_w_down, m_ple_proj, m_ple_gate, m_final_g]))
    v = dict(zip(names, [v_norm1_g, v_w_in, v_conv_qkv, v_a_log, v_dt_bias, v_onorm_g, v_pool_w, v_pool_scale, v_sconv_w, v_w_out,
                         v_norm2_g, v_w_gate, v_w_up, v_w_down, v_ple_proj, v_ple_gate, v_final_g]))

    first, rest = FETCH_GROUPS[0], tuple(k for members in FETCH_GROUPS[1:] for k in members)
    gathered = dict(zip(first, _all_gather([_payload(k, w[k][0]) for k in first], name="all_gather_weights")))
    (flying0,), token = _exchange_start([[_payload(k, w[k][0]) for k in rest]], CHIP_GATHER, name="gather_start_0")
    replicated = [_small_weights(w, i) for i in range(DEPTH)]
    replicated[0]["norm1_g"] = replicated[0]["norm1_g"] + token[0, 0]
    flying1 = []

    def fetch(i, group, after):
        if i == 0 and group == 1:
            landed = _exchange_wait(flying0, after, CHIP_GATHER, name="gather_wait_0")
            gathered.update(zip(rest, _pair_swap(landed, name="pair_swap")))
            started, token = _exchange_start([[_payload(k, w[k][1]) for k in SHARDED]], CHIP_GATHER, name="gather_start_1",
                                             after=gathered[rest[0]])
            flying1.extend(started)
            return {**{k: _as_read(k, gathered[k]) for k in FETCH_GROUPS[group]}, "norm2_g": replicated[0]["norm2_g"] + token[0, 0]}
        if i == 1 and group == 0:
            landed = _exchange_wait(flying1[0], after, CHIP_GATHER, name="gather_wait_1")
            gathered.update(zip(SHARDED, _pair_swap(landed, name="pair_swap")))
        return {k: _as_read(k, gathered[k]) for k in FETCH_GROUPS[group]}

    def reduce_scatter_start(members, blocks, tag):
        mine = [blocks[k] for k in members]
        theirs = _pair_exchange(mine, name="pair_exchange")
        sums = [_pair_add(a, b, name="pair_add") for a, b in zip(mine, theirs)]
        (started,), token = _exchange_start([sums], CHIP_SCATTER, name="exchange_start_" + tag)
        return started, token

    h, saved0 = _layer_fwd(x[0], p[0, 0], replicated[0], functools.partial(fetch, 0))
    h, saved1 = _layer_fwd(h, p[1, 0], replicated[1], functools.partial(fetch, 1))
    dx, dgf, loss_part = _loss_head(h, final_g[None], loss_target[0], name="loss_head")
    small, big1, flying0 = [None] * DEPTH, {}, []
    dx, small[1] = _layer_bwd(dx, saved1, lambda group, blocks: big1.update({k: blocks[k] for k in EMIT_GROUPS[group]}))
    flying1, token = reduce_scatter_start(SHARDED, big1, "1")

    def emit(group, blocks):
        started, token = reduce_scatter_start(EMIT_GROUPS[group], blocks, f"0_{group}")
        flying0.append(started)
        return token

    dx, small[0] = _layer_bwd(dx, saved0, emit, after=token)
    received = [{}, dict(zip(SHARDED, _exchange_wait(flying1, dx, CHIP_SCATTER, name="exchange_wait_1")))]
    for group, members in enumerate(EMIT_GROUPS):
        received[0].update(zip(members, _exchange_wait(flying0[group], dx, CHIP_SCATTER, name=f"exchange_wait_0_{group}")))

    grads = {k: jnp.stack([small[i][k] for i in range(DEPTH)]) for k in small[0]}
    grads = {k: g[:, 0] if k in ("norm1_g", "norm2_g", "onorm_g", "pool_scale") else g for k, g in grads.items()}
    grads["final_g"] = dgf[0]
    loss_row = jnp.pad(loss_part, ((0, 0), (0, SLAB_COLS - LANE)))
    (small_flying,), token = _exchange_start([[_pack_slab(grads, loss_row)]], GATHER, name="small_gather_start")

    out_g, out_d, out_m, out_v = {}, {}, {}, {}
    for k in SHARDED:
        out_g[k], out_d[k], out_m[k], out_v[k] = _adamw_reduce(w[k], [received[i][k] for i in range(DEPTH)], m[k], v[k],
                                                                name="adamw_" + k, after=token)
    behind_all = jnp.stack([out_v[k][0, 0, 0] for k in SHARDED])
    (small_parts,) = _exchange_wait(small_flying, behind_all, GATHER, name="small_gather_wait")
    zero_row = jnp.zeros((1, SLAB_COLS), F32)
    slabs = _adamw_reduce(_pack_slab(w, zero_row)[None], [small_parts], _pack_slab(m, zero_row)[None],
                          _pack_slab(v, zero_row)[None], name="adamw_small")
    slabs = [s[0] for s in slabs]
    shapes = {k: w[k].shape for k in SMALL}
    for dst, slab in zip((out_g, out_d, out_m, out_v), slabs):
        vals, _ = _unpack_slab(slab, shapes)
        dst.update(vals)
    _, loss_at = _unpack_slab(slabs[0], shapes)
    loss = slabs[0][loss_at, 0]

    return (loss, dx[None], *[out_g[k] for k in names], *[out_d[k] for k in names], *[out_m[k] for k in names],
            *[out_v[k] for k in names])
```

```python
import functools

import jax
import jax.numpy as jnp
from jax import lax
from jax.experimental import pallas as pl
from jax.experimental.pallas import tpu as pltpu

F32 = jnp.float32
BF16 = jnp.bfloat16

D_MODEL = 1024
DEPTH = 2
PLE_DIM = 256
EPS = 1e-6
HEAD_DIM = 128
HEADS = 4
A_DIM = HEADS * HEAD_DIM
QKV_TAPS = 4
CHUNK = 64
POOL_WINDOWS = (2, 4, 8, 16)
POOL_DIM = 256
CONV_DIM = 256
CONV_TAPS = 3
D_FF = 2816
D_IN = 3080
D_IN_PAD = 3200
AB_COL = 2048
N_DEV = 8

ADAM_LR = 0.001
ADAM_B1 = 0.9
ADAM_B2 = 0.999
ADAM_EPS = 1e-08
ADAM_WD = 0.01
ADAM_STEP = 10

LANE = 128
SUBLANE = 8
VMEM_BYTES_V7X = 64 * 1024 * 1024
VMEM_LIMIT = 48 * 1024 * 1024

_HI = lax.Precision.HIGHEST
NN = ((1,), (0,))
NT = ((1,), (1,))
TN = ((0,), (0,))
MESH = pl.DeviceIdType.MESH


def _dot(a, b, dims, hi=False):
    if hi:
        return lax.dot_general(a, b, (dims, ((), ())), precision=_HI, preferred_element_type=F32)
    return lax.dot_general(a.astype(BF16), b.astype(BF16), (dims, ((), ())), preferred_element_type=F32)


def _pcall(body, *, name, out_shape, grid=(), in_specs=None, out_specs=None, scratch_shapes=(), semantics=None,
           vmem_limit=None, after=None, **kw):
    params = {}
    if semantics is not None:
        params["dimension_semantics"] = semantics
    if vmem_limit is not None:
        params["vmem_limit_bytes"] = vmem_limit
    if after is not None:
        n_in, inner = len(in_specs), body
        body = lambda *refs: inner(*refs[:n_in], *refs[n_in + 1:])
        in_specs = list(in_specs) + [pl.BlockSpec(after.shape, lambda *_: (0,) * after.ndim)]
    call = pl.pallas_call(
        body, name=name, out_shape=out_shape, grid=grid, in_specs=in_specs, out_specs=out_specs,
        scratch_shapes=list(scratch_shapes), compiler_params=pltpu.CompilerParams(**params), **kw)
    return call if after is None else (lambda *args: call(*args, after))


def _sigmoid(x):
    return 1.0 / (1.0 + jnp.exp(-x))


def _softplus(x):
    return jnp.maximum(x, 0.0) + jnp.log(1.0 + jnp.exp(-jnp.abs(x)))


def _tile(n, cap, mult):
    if n <= cap:
        return n
    best = None
    for t in range(mult, cap + 1, mult):
        if n % t == 0:
            best = t
    assert best is not None, (n, cap, mult)
    return best


ROWS_PER_STEP = 512
NARROW_RESULT = 1024
COLS_PER_DOT = 640


def _resident(weight):
    return pl.BlockSpec(weight.shape, lambda i: (0,) * weight.ndim, pipeline_mode=pl.Buffered(1))


def _matmul_rows(a, b, mode, *, name, res=None, out_dtype=F32, b_blocked=False, after=None):
    m, k = a.shape
    if b_blocked:
        nb, _, bw = b.shape
        n = nb * bw if mode == "nn" else b.shape[1]
    else:
        n = b.shape[1] if mode == "nn" else b.shape[0]
    tm = _tile(m, ROWS_PER_STEP if n > NARROW_RESULT else 2 * ROWS_PER_STEP, 16)
    cn = bw if (b_blocked and mode == "nn") else _tile(n, COLS_PER_DOT, LANE)
    has_res = res is not None

    def body(*refs):
        a_ref, b_ref = refs[0], refs[1]
        res_ref = refs[2] if has_res else None
        o_ref = refs[2 + has_res]
        if not (b_blocked and mode == "nt"):
            av = a_ref[...].astype(BF16)
        for j in range(n // cn):
            cols = pl.ds(j * cn, cn)
            if mode == "nn":
                part = _dot(av, b_ref[j] if b_blocked else b_ref[:, cols], NN)
            elif not b_blocked:
                part = _dot(av, b_ref[cols, :], NT)
            else:
                part = None
                for s in range(nb):
                    term = _dot(a_ref[:, pl.ds(s * bw, bw)], b_ref[s, cols, :], NT)
                    part = term if part is None else part + term
            if has_res:
                part = part + res_ref[:, cols]
            o_ref[:, cols] = part.astype(o_ref.dtype)

    row = lambda width: pl.BlockSpec((tm, width), lambda i: (i, 0))
    whole = _resident(b)
    ins = [a, b] + ([res] if has_res else [])
    specs = [row(k), whole] + ([row(n)] if has_res else [])
    return _pcall(body, name=name, out_shape=jax.ShapeDtypeStruct((m, n), out_dtype), grid=(m // tm,), in_specs=specs,
                  out_specs=row(n), semantics=("parallel",), vmem_limit=VMEM_LIMIT, after=after)(*ins)


def _matmul(a, b, mode, *, name, res=None, out_dtype=F32, b_blocked=False, out_blocked=None, after=None):
    if mode != "tn":
        return _matmul_rows(a, b, mode, name=name, res=res, out_dtype=out_dtype, b_blocked=b_blocked, after=after)
    assert res is None and not b_blocked and after is None
    (t, m), (t2, n) = a.shape, b.shape
    assert t == t2, (a.shape, b.shape)
    tm = _tile(m, 1024, LANE)
    tn = _tile(n, COLS_PER_DOT, LANE)
    if out_blocked is not None:
        assert out_blocked[0] * out_blocked[1] == n
        tn = out_blocked[1]

    def body(a_ref, b_ref, o_ref):
        part = _dot(a_ref[...], b_ref[...], TN).astype(o_ref.dtype)
        if out_blocked is None:
            o_ref[...] = part
        else:
            o_ref[0] = part

    o_spec = (pl.BlockSpec((tm, tn), lambda i, j: (i, j)) if out_blocked is None
              else pl.BlockSpec((1, tm, tn), lambda i, j: (j, i, 0)))
    o_shape = (m, n) if out_blocked is None else (out_blocked[0], m, out_blocked[1])
    return _pcall(body, name=name, out_shape=jax.ShapeDtypeStruct(o_shape, out_dtype), grid=(m // tm, n // tn),
                  in_specs=[pl.BlockSpec((t, tm), lambda i, j: (0, i)), pl.BlockSpec((t, tn), lambda i, j: (0, j))],
                  out_specs=o_spec, semantics=("parallel", "parallel"), vmem_limit=VMEM_LIMIT)(a, b)


ROW_TILE = 512


def _rows(t, width, idx=0):
    return pl.BlockSpec((ROW_TILE, width), lambda i: (i, idx))


def _vec(width):
    return pl.BlockSpec((1, width), lambda i: (0, 0))


def _rmsnorm_fwd(x, g, *, name):
    t, d = x.shape

    def body(x_ref, g_ref, h_ref):
        xv = x_ref[...]
        r = lax.rsqrt(jnp.mean(xv * xv, axis=-1, keepdims=True) + EPS)
        h_ref[...] = (xv * r * g_ref[...]).astype(BF16)

    return _pcall(body, name=name, out_shape=jax.ShapeDtypeStruct((t, d), BF16), grid=(t // ROW_TILE,),
                  in_specs=[_rows(t, d), _vec(d)], out_specs=_rows(t, d), semantics=("parallel",))(x, g)


def _rmsnorm_bwd(x, g, dh, dres, *, name):
    t, d = x.shape

    def body(x_ref, g_ref, dh_ref, dres_ref, dx_ref, dg_ref):
        xv = x_ref[...]
        r = lax.rsqrt(jnp.mean(xv * xv, axis=-1, keepdims=True) + EPS)
        xhat = xv * r
        dhv = dh_ref[...].astype(F32)
        dhg = dhv * g_ref[...]
        dx_ref[...] = dres_ref[...] + r * (dhg - xhat * jnp.mean(dhg * xhat, axis=-1, keepdims=True))
        part = jnp.sum(dhv * xhat, axis=0, keepdims=True)

        @pl.when(pl.program_id(0) == 0)
        def _():
            dg_ref[...] = part

        @pl.when(pl.program_id(0) > 0)
        def _():
            dg_ref[...] += part

    return _pcall(body, name=name, out_shape=(jax.ShapeDtypeStruct((t, d), F32), jax.ShapeDtypeStruct((1, d), F32)),
                  grid=(t // ROW_TILE,), in_specs=[_rows(t, d), _vec(d), _rows(t, d), _rows(t, d)],
                  out_specs=(_rows(t, d), _vec(d)), semantics=("arbitrary",))(x, g, dh, dres)


def _swiglu_fwd(h, w_gate, w_up, *, name):
    t, k = h.shape
    nb, _, bw = w_gate.shape
    tm = _tile(t, ROWS_PER_STEP, 16)

    def body(h_ref, wg_ref, wu_ref, ff_ref, gate_ref, up_ref):
        hv = h_ref[...]
        for j in range(nb):
            cols = pl.ds(j * bw, bw)
            gv = _dot(hv, wg_ref[j], NN)
            uv = _dot(hv, wu_ref[j], NN)
            gate_ref[:, cols] = gv.astype(BF16)
            up_ref[:, cols] = uv.astype(BF16)
            ff_ref[:, cols] = (gv * _sigmoid(gv) * uv).astype(BF16)

    row = lambda width: pl.BlockSpec((tm, width), lambda i: (i, 0))
    out = jax.ShapeDtypeStruct((t, nb * bw), BF16)
    return _pcall(body, name=name, out_shape=(out,) * 3, grid=(t // tm,), in_specs=[row(k), _resident(w_gate), _resident(w_up)],
                  out_specs=(row(nb * bw),) * 3, semantics=("parallel",), vmem_limit=VMEM_LIMIT)(h, w_gate, w_up)


def _swiglu_bwd(dx2, w_down, gate, up, *, name, after=None):
    t, d = dx2.shape
    f = w_down.shape[0]
    tm = _tile(t, ROWS_PER_STEP, 16)
    cn = _tile(f, COLS_PER_DOT, LANE)

    def body(dx_ref, w_ref, gate_ref, up_ref, dgate_ref, dup_ref):
        dxv = dx_ref[...].astype(BF16)
        for j in range(f // cn):
            cols = pl.ds(j * cn, cn)
            dffv = _dot(dxv, w_ref[cols, :], NT)
            gv = gate_ref[:, cols].astype(F32)
            sig = _sigmoid(gv)
            dgate_ref[:, cols] = (dffv * up_ref[:, cols].astype(F32) * sig * (1.0 + gv * (1.0 - sig))).astype(BF16)
            dup_ref[:, cols] = (dffv * gv * sig).astype(BF16)

    row = lambda width: pl.BlockSpec((tm, width), lambda i: (i, 0))
    out = jax.ShapeDtypeStruct((t, f), BF16)
    return _pcall(body, name=name, out_shape=(out, out), grid=(t // tm,), in_specs=[row(d), _resident(w_down), row(f), row(f)],
                  out_specs=(row(f), row(f)), semantics=("parallel",), vmem_limit=VMEM_LIMIT, after=after)(dx2, w_down, gate, up)


def _ple_fwd(x2, pgl, pp, *, name):
    t, d = x2.shape

    def body(x_ref, pgl_ref, pp_ref, o_ref):
        o_ref[...] = x_ref[...] + _sigmoid(pgl_ref[...]) * pp_ref[...]

    return _pcall(body, name=name, out_shape=jax.ShapeDtypeStruct((t, d), F32), grid=(t // ROW_TILE,),
                  in_specs=[_rows(t, d)] * 3, out_specs=_rows(t, d), semantics=("parallel",))(x2, pgl, pp)


def _ple_bwd(dx3, pgl, pp, *, name, after=None):
    t, d = dx3.shape

    def body(dx_ref, pgl_ref, pp_ref, dpgl_ref, dpp_ref):
        dxv = dx_ref[...]
        sig = _sigmoid(pgl_ref[...])
        dpp_ref[...] = (dxv * sig).astype(BF16)
        dpgl_ref[...] = (dxv * pp_ref[...] * sig * (1.0 - sig)).astype(BF16)

    return _pcall(body, name=name, out_shape=(jax.ShapeDtypeStruct((t, d), BF16),) * 2, grid=(t // ROW_TILE,),
                  in_specs=[_rows(t, d)] * 3, out_specs=(_rows(t, d),) * 2, semantics=("parallel",), after=after)(dx3, pgl, pp)


def _loss_head(x3, g, target, *, name):
    t, d = x3.shape

    def body(x_ref, g_ref, t_ref, dx_ref, dg_ref, loss_ref):
        xv = x_ref[...]
        r = lax.rsqrt(jnp.mean(xv * xv, axis=-1, keepdims=True) + EPS)
        xhat = xv * r
        gv = g_ref[...]
        err = xhat * gv - t_ref[...]
        row_loss = jnp.sum(err * err, axis=-1, keepdims=True) * (0.5 / d)
        lpart = jnp.broadcast_to(jnp.sum(row_loss, axis=0, keepdims=True), (1, LANE))
        dy = err * (1.0 / d)
        dyg = dy * gv
        dx_ref[...] = r * (dyg - xhat * jnp.mean(dyg * xhat, axis=-1, keepdims=True))
        gpart = jnp.sum(dy * xhat, axis=0, keepdims=True)

        @pl.when(pl.program_id(0) == 0)
        def _():
            dg_ref[...] = gpart
            loss_ref[...] = lpart

        @pl.when(pl.program_id(0) > 0)
        def _():
            dg_ref[...] += gpart
            loss_ref[...] += lpart

    return _pcall(body, name=name,
                  out_shape=(jax.ShapeDtypeStruct((t, d), F32), jax.ShapeDtypeStruct((1, d), F32), jax.ShapeDtypeStruct((1, LANE), F32)),
                  grid=(t // ROW_TILE,), in_specs=[_rows(t, d), _vec(d), _rows(t, d)],
                  out_specs=(_rows(t, d), _vec(d), _vec(LANE)), semantics=("arbitrary",))(x3, g, target)


def _shift_down(x, d):
    if d == 0:
        return x
    row = lax.broadcasted_iota(jnp.int32, x.shape, 0)
    return jnp.where(row >= d, pltpu.roll(x, d, 0), 0.0)


def _shift_up(x, d):
    if d == 0:
        return x
    t = x.shape[0]
    row = lax.broadcasted_iota(jnp.int32, x.shape, 0)
    return jnp.where(row < t - d, pltpu.roll(x, t - d, 0), 0.0)


def _colsum(x):
    return jnp.sum(x, axis=0, keepdims=True)


def _col(t, idx_fn):
    return pl.BlockSpec((t, LANE), idx_fn)


def _conv_fwd(x, w_ref, taps):
    acc = None
    for j in range(taps):
        term = w_ref[pl.ds(j, 1), :] * _shift_down(x, taps - 1 - j)
        acc = term if acc is None else acc + term
    return acc


def _conv_bwd(x, dy, w_ref, dw_ref, taps):
    dx = None
    for j in range(taps):
        term = w_ref[pl.ds(j, 1), :] * _shift_up(dy, taps - 1 - j)
        dx = term if dx is None else dx + term
        dw_ref[pl.ds(j, 1), :] = _colsum(dy * _shift_down(x, taps - 1 - j))
    return dx


def _qkv_prep_fwd(proj, conv_w, *, name):
    t = proj.shape[0]
    scale = HEAD_DIM ** -0.5

    def body(x_ref, w_ref, o_ref):
        j = pl.program_id(0)
        c = _conv_fwd(x_ref[...], w_ref, QKV_TAPS)
        s = c * _sigmoid(c)
        r = lax.rsqrt(jnp.sum(s * s, axis=-1, keepdims=True) + EPS)
        f = jnp.where(j < 2 * HEADS, r, 1.0) * jnp.where(j < HEADS, scale, 1.0)
        o_ref[0] = s * f

    return _pcall(body, name=name, out_shape=jax.ShapeDtypeStruct((3 * HEADS, t, LANE), F32), grid=(3 * HEADS,),
                  in_specs=[_col(t, lambda j: (0, j)), pl.BlockSpec((QKV_TAPS, LANE), lambda j: (0, j))],
                  out_specs=pl.BlockSpec((1, t, LANE), lambda j: (j, 0, 0)), semantics=("parallel",),
                  vmem_limit=VMEM_LIMIT)(proj, conv_w)


def _qkv_prep_bwd(proj, conv_w, dqkv, *, name):
    t = proj.shape[0]
    scale = HEAD_DIM ** -0.5

    def body(x_ref, w_ref, d_ref, dx_ref, dw_ref):
        j = pl.program_id(0)
        xv = x_ref[...]
        c = _conv_fwd(xv, w_ref, QKV_TAPS)
        sig = _sigmoid(c)
        s = c * sig
        r = lax.rsqrt(jnp.sum(s * s, axis=-1, keepdims=True) + EPS)
        n0 = s * r
        dv = d_ref[0]
        dn0 = dv * jnp.where(j < HEADS, scale, 1.0)
        ds_norm = r * (dn0 - n0 * jnp.sum(dn0 * n0, axis=-1, keepdims=True))
        ds = jnp.where(j < 2 * HEADS, ds_norm, dv)
        dc = ds * sig * (1.0 + c * (1.0 - sig))
        dx_ref[...] = _conv_bwd(xv, dc, w_ref, dw_ref, QKV_TAPS).astype(BF16)

    return _pcall(body, name=name,
                  out_shape=(jax.ShapeDtypeStruct((t, 3 * A_DIM), BF16), jax.ShapeDtypeStruct((QKV_TAPS, 3 * A_DIM), F32)),
                  grid=(3 * HEADS,),
                  in_specs=[_col(t, lambda j: (0, j)), pl.BlockSpec((QKV_TAPS, LANE), lambda j: (0, j)),
                            pl.BlockSpec((1, t, LANE), lambda j: (j, 0, 0))],
                  out_specs=(_col(t, lambda j: (0, j)), pl.BlockSpec((QKV_TAPS, LANE), lambda j: (0, j))),
                  semantics=("parallel",), vmem_limit=VMEM_LIMIT)(proj, conv_w, dqkv)


def _lane_pick(x, lane_idx, lane):
    return jnp.broadcast_to(jnp.sum(jnp.where(lane == lane_idx, x, 0.0), axis=-1, keepdims=True), x.shape)


def _gates_fwd(proj, alog, dtb, *, name):
    t = proj.shape[0]

    def body(x_ref, alog_ref, dtb_ref, g_ref, b_ref):
        xv = x_ref[...]
        lane = lax.broadcasted_iota(jnp.int32, xv.shape, 1)
        gall = -jnp.exp(alog_ref[...]) * _softplus(xv + dtb_ref[...])
        ball = _sigmoid(xv)
        for h in range(HEADS):
            g_ref[h] = _lane_pick(gall, h, lane)
            b_ref[h] = _lane_pick(ball, HEADS + h, lane)

    out = jax.ShapeDtypeStruct((HEADS, t, LANE), F32)
    whole = pl.BlockSpec((HEADS, t, LANE), lambda i: (0, 0, 0))
    return _pcall(body, name=name, out_shape=(out, out), grid=(1,),
                  in_specs=[_col(t, lambda i: (0, AB_COL // LANE)), _vec(LANE), _vec(LANE)], out_specs=(whole, whole),
                  semantics=("arbitrary",), vmem_limit=VMEM_LIMIT)(proj, alog, dtb)


def _gates_bwd(proj, alog, dtb, dg, dbeta, *, name):
    t = proj.shape[0]

    def body(x_ref, alog_ref, dtb_ref, dg_ref, db_ref, dab_ref, dalog_ref, ddtb_ref):
        xv = x_ref[...]
        lane = lax.broadcasted_iota(jnp.int32, xv.shape, 1)
        lane1 = lax.broadcasted_iota(jnp.int32, (1, LANE), 1)
        z = xv + dtb_ref[...]
        nea = -jnp.exp(alog_ref[...])
        da_f = nea * _sigmoid(z)
        g_f = nea * _softplus(z)
        ball = _sigmoid(xv)
        db_f = ball * (1.0 - ball)
        dab = jnp.zeros_like(xv)
        dalog = jnp.zeros((1, LANE), F32)
        for h in range(HEADS):
            dgh = dg_ref[h]
            dab = dab + jnp.where(lane == h, dgh * da_f, 0.0) + jnp.where(lane == HEADS + h, db_ref[h] * db_f, 0.0)
            dalog = dalog + jnp.where(lane1 == h, _colsum(dgh * g_f), 0.0)
        dab_ref[...] = dab.astype(BF16)
        dalog_ref[...] = dalog
        ddtb_ref[...] = jnp.where(lane1 < HEADS, _colsum(dab), 0.0)

    whole = pl.BlockSpec((HEADS, t, LANE), lambda i: (0, 0, 0))
    vec = jax.ShapeDtypeStruct((1, LANE), F32)
    return _pcall(body, name=name, out_shape=(jax.ShapeDtypeStruct((t, LANE), BF16), vec, vec), grid=(1,),
                  in_specs=[_col(t, lambda i: (0, AB_COL // LANE)), _vec(LANE), _vec(LANE), whole, whole],
                  out_specs=(_col(t, lambda i: (0, 0)), _vec(LANE), _vec(LANE)), semantics=("arbitrary",),
                  vmem_limit=VMEM_LIMIT)(proj, alog, dtb, dg, dbeta)


Z_COL = 3 * A_DIM // LANE


def _apost_fwd(o, proj, gn, *, name):
    t = proj.shape[0]

    def body(o_ref, z_ref, gn_ref, y_ref):
        ov = o_ref[0]
        z = z_ref[...]
        r = lax.rsqrt(jnp.mean(ov * ov, axis=-1, keepdims=True) + EPS)
        y_ref[...] = (ov * r * gn_ref[...] * (z * _sigmoid(z))).astype(BF16)

    return _pcall(body, name=name, out_shape=jax.ShapeDtypeStruct((t, A_DIM), BF16), grid=(HEADS,),
                  in_specs=[pl.BlockSpec((1, t, LANE), lambda h: (h, 0, 0)), _col(t, lambda h: (0, Z_COL + h)),
                            pl.BlockSpec((1, LANE), lambda h: (0, 0))],
                  out_specs=_col(t, lambda h: (0, h)), semantics=("parallel",), vmem_limit=VMEM_LIMIT)(o, proj, gn)


def _apost_bwd(o, proj, gn, dmixed, *, name):
    t = proj.shape[0]

    def body(o_ref, z_ref, gn_ref, d_ref, do_ref, dz_ref, dgn_ref):
        ov = o_ref[0]
        z = z_ref[...]
        gnv = gn_ref[...]
        dv = d_ref[...]
        r = lax.rsqrt(jnp.mean(ov * ov, axis=-1, keepdims=True) + EPS)
        ohat = ov * r
        sig = _sigmoid(z)
        dy = dv * (z * sig)
        dz_ref[...] = (dv * ohat * gnv * sig * (1.0 + z * (1.0 - sig))).astype(BF16)
        dyo = dy * gnv
        do_ref[0] = r * (dyo - ohat * jnp.mean(dyo * ohat, axis=-1, keepdims=True))
        part = _colsum(dy * ohat)

        @pl.when(pl.program_id(0) == 0)
        def _():
            dgn_ref[...] = part

        @pl.when(pl.program_id(0) > 0)
        def _():
            dgn_ref[...] += part

    return _pcall(body, name=name,
                  out_shape=(jax.ShapeDtypeStruct((HEADS, t, LANE), F32), jax.ShapeDtypeStruct((t, A_DIM), BF16),
                             jax.ShapeDtypeStruct((1, LANE), F32)),
                  grid=(HEADS,),
                  in_specs=[pl.BlockSpec((1, t, LANE), lambda h: (h, 0, 0)), _col(t, lambda h: (0, Z_COL + h)),
                            pl.BlockSpec((1, LANE), lambda h: (0, 0)), _col(t, lambda h: (0, h))],
                  out_specs=(pl.BlockSpec((1, t, LANE), lambda h: (h, 0, 0)), _col(t, lambda h: (0, h)),
                             pl.BlockSpec((1, LANE), lambda h: (0, 0))),
                  semantics=("arbitrary",), vmem_limit=VMEM_LIMIT)(o, proj, gn, dmixed)


POOL_COL = (AB_COL + LANE) // LANE
CB_COL = POOL_COL + POOL_DIM // LANE
CC_COL = CB_COL + CONV_DIM // LANE
CH_COL = CC_COL + CONV_DIM // LANE
MAX_WIN_LOG2 = 4


def _window_sums(x, shift):
    sums = []
    cur = x
    for k in range(MAX_WIN_LOG2):
        cur = cur + shift(cur, 1 << k)
        sums.append(cur)
    return sums


def _pick_window(sums, win):
    out = sums[-1]
    for k in range(MAX_WIN_LOG2 - 2, -1, -1):
        out = jnp.where(win == float(2 << k), sums[k], out)
    return out


def _pool_counts(shape, win):
    row = lax.broadcasted_iota(jnp.int32, shape, 0).astype(F32)
    return jnp.minimum(row + 1.0, win)


def _pool_fwd(proj, win, wbd, scale, *, name):
    t = proj.shape[0]

    def body(x_ref, win_ref, w_ref, s_ref, y_ref):
        xv = x_ref[...]
        winv = win_ref[...]
        pooled = _pick_window(_window_sums(xv, _shift_down), winv) / _pool_counts(xv.shape, winv) - xv
        y_ref[...] = (_dot(pooled, w_ref[0], NN) * s_ref[...]).astype(BF16)

    nb = POOL_DIM // LANE
    vec = pl.BlockSpec((1, LANE), lambda b: (0, b))
    return _pcall(body, name=name, out_shape=jax.ShapeDtypeStruct((t, POOL_DIM), BF16), grid=(nb,),
                  in_specs=[_col(t, lambda b: (0, POOL_COL + b)), vec, pl.BlockSpec((1, LANE, LANE), lambda b: (b, 0, 0)), vec],
                  out_specs=_col(t, lambda b: (0, b)), semantics=("parallel",), vmem_limit=VMEM_LIMIT)(proj, win, wbd, scale)


def _pool_bwd(proj, win, wbd, scale, dmixed, *, name):
    t = proj.shape[0]

    def body(x_ref, win_ref, w_ref, s_ref, d_ref, dx_ref, dw_ref, ds_ref):
        xv = x_ref[...]
        winv = win_ref[...]
        cnt = _pool_counts(xv.shape, winv)
        pooled = _pick_window(_window_sums(xv, _shift_down), winv) / cnt - xv
        dv = d_ref[...]
        ds_ref[...] = _colsum(dv * _dot(pooled, w_ref[0], NN))
        dy0 = dv * s_ref[...]
        dw_ref[0] = _dot(pooled, dy0, TN)
        dpooled = _dot(dy0, w_ref[0], NT)
        dmean = dpooled / cnt
        dx_ref[...] = (_pick_window(_window_sums(dmean, _shift_up), winv) - dpooled).astype(BF16)

    nb = POOL_DIM // LANE
    vec = pl.BlockSpec((1, LANE), lambda b: (0, b))
    mat = pl.BlockSpec((1, LANE, LANE), lambda b: (b, 0, 0))
    first = A_DIM // LANE
    return _pcall(body, name=name,
                  out_shape=(jax.ShapeDtypeStruct((t, POOL_DIM), BF16), jax.ShapeDtypeStruct((nb, LANE, LANE), F32),
                             jax.ShapeDtypeStruct((1, POOL_DIM), F32)),
                  grid=(nb,),
                  in_specs=[_col(t, lambda b: (0, POOL_COL + b)), vec, mat, vec, _col(t, lambda b: (0, first + b))],
                  out_specs=(_col(t, lambda b: (0, b)), mat, vec), semantics=("parallel",),
                  vmem_limit=VMEM_LIMIT)(proj, win, wbd, scale, dmixed)


def _sconv_fwd(proj, w, *, name):
    t = proj.shape[0]

    def body(cb_ref, cc_ref, ch_ref, w_ref, y_ref):
        y_ref[...] = (cb_ref[...] * _conv_fwd(cc_ref[...] * ch_ref[...], w_ref, CONV_TAPS)).astype(BF16)

    nb = CONV_DIM // LANE
    return _pcall(body, name=name, out_shape=jax.ShapeDtypeStruct((t, CONV_DIM), BF16), grid=(nb,),
                  in_specs=[_col(t, lambda b: (0, CB_COL + b)), _col(t, lambda b: (0, CC_COL + b)),
                            _col(t, lambda b: (0, CH_COL + b)), pl.BlockSpec((CONV_TAPS, LANE), lambda b: (0, b))],
                  out_specs=_col(t, lambda b: (0, b)), semantics=("parallel",), vmem_limit=VMEM_LIMIT)(proj, proj, proj, w)


def _sconv_bwd(proj, w, dmixed, *, name):
    t = proj.shape[0]

    def body(cb_ref, cc_ref, ch_ref, w_ref, d_ref, dcb_ref, dcc_ref, dch_ref, dw_ref):
        cc = cc_ref[...]
        ch = ch_ref[...]
        u = cc * ch
        dv = d_ref[...]
        dcb_ref[...] = (dv * _conv_fwd(u, w_ref, CONV_TAPS)).astype(BF16)
        du = _conv_bwd(u, dv * cb_ref[...], w_ref, dw_ref, CONV_TAPS)
        dcc_ref[...] = (du * ch).astype(BF16)
        dch_ref[...] = (du * cc).astype(BF16)

    nb = CONV_DIM // LANE
    first = (A_DIM + POOL_DIM) // LANE
    act = jax.ShapeDtypeStruct((t, CONV_DIM), BF16)
    wspec = pl.BlockSpec((CONV_TAPS, LANE), lambda b: (0, b))
    ospec = _col(t, lambda b: (0, b))
    return _pcall(body, name=name, out_shape=(act, act, act, jax.ShapeDtypeStruct((CONV_TAPS, CONV_DIM), F32)), grid=(nb,),
                  in_specs=[_col(t, lambda b: (0, CB_COL + b)), _col(t, lambda b: (0, CC_COL + b)),
                            _col(t, lambda b: (0, CH_COL + b)), wspec, _col(t, lambda b: (0, first + b))],
                  out_specs=(ospec, ospec, ospec, wspec), semantics=("parallel",),
                  vmem_limit=VMEM_LIMIT)(proj, proj, proj, w, dmixed)


def _chunk_masks():
    r = lax.broadcasted_iota(jnp.int32, (CHUNK, CHUNK), 0)
    c = lax.broadcasted_iota(jnp.int32, (CHUNK, CHUNK), 1)
    return r >= c, r > c, jnp.where(r == c, 1.0, 0.0).astype(F32)


def _split(a):
    hi = a.astype(BF16)
    return hi, (a - hi.astype(F32)).astype(BF16)


def _dot_split(a, b, dims):
    (ah, al), (bh, bl) = a, b
    return _dot(ah, bh, dims) + _dot(ah, bl, dims) + _dot(al, bh, dims)


def _tri_inv(lows, eye):
    xs = [eye - low for low in lows]
    ps = [_split(low) for low in lows]
    ps = [_split(_dot_split(p, p, NN)) for p in ps]
    for i in range(5):
        xs = [x + _dot_split(_split(x), p, NN) for x, p in zip(xs, ps)]
        if i < 4:
            ps = [_split(_dot_split(p, p, NN)) for p in ps]
    return xs


def _prefix_sum_rows(x):
    for k in range(6):
        x = x + _shift_down(x, 1 << k)
    return x


def _suffix_sum_rows(x):
    for k in range(6):
        x = x + _shift_up(x, 1 << k)
    return x


def _chunk_decay(g, incl):
    gcb = _prefix_sum_rows(g)
    gtot = _colsum(g)
    col = gcb[:, :CHUNK]
    row = gcb.T[:CHUNK, :]
    decay = jnp.exp(jnp.where(incl, col - row, -1e30))
    return gcb, gtot, decay


CHUNKS_PER_STEP = 4


def _heads_of(ref, base, rows):
    return [ref[base + h, rows, :] for h in range(HEADS)]


def _chunk_rows(j):
    return pl.ds(j * CHUNK, CHUNK)


def _deltanet_prep(qkv, g, beta, *, name):
    t = qkv.shape[1]
    n_chunks = t // CHUNK
    per = CHUNKS_PER_STEP
    probs = [(j, h) for j in range(per) for h in range(HEADS)]

    def body(qkv_ref, g_ref, b_ref, u_ref, w_ref, qg_ref, kg_ref, attn_ref, tm_ref):
        incl, strict, eye = _chunk_masks()
        q = [qkv_ref[h, _chunk_rows(j), :] for j, h in probs]
        k = [qkv_ref[HEADS + h, _chunk_rows(j), :] for j, h in probs]
        v = [qkv_ref[2 * HEADS + h, _chunk_rows(j), :] for j, h in probs]
        bv = [b_ref[h, _chunk_rows(j), :] for j, h in probs]
        dec = [_chunk_decay(g_ref[h, _chunk_rows(j), :], incl) for j, h in probs]
        kb = [a * b for a, b in zip(k, bv)]
        low = [jnp.where(strict, _dot(a, b, NT) * d[2], 0.0) for a, b, d in zip(kb, k, dec)]
        tm = _tri_inv(low, eye)
        egc = [jnp.exp(d[0]) for d in dec]
        u = [_dot(m, a * b, NN) for m, a, b in zip(tm, v, bv)]
        w = [_dot(m, a * e, NN) for m, a, e in zip(tm, kb, egc)]
        attn = [_dot(a, b, NT) * d[2] for a, b, d in zip(q, k, dec)]
        for i, (j, h) in enumerate(probs):
            rows = _chunk_rows(j)
            u_ref[h, rows, :] = u[i]
            w_ref[h, rows, :] = w[i].astype(BF16)
            qg_ref[h, rows, :] = (q[i] * egc[i]).astype(BF16)
            kg_ref[h, rows, :] = (k[i] * jnp.exp(dec[i][1] - dec[i][0])).astype(BF16)
            attn_ref[j, h] = attn[i].astype(BF16)
            tm_ref[j, h] = tm[i]

    act = lambda heads: pl.BlockSpec((heads, per * CHUNK, LANE), lambda n: (0, n, 0))
    mat = pl.BlockSpec((per, HEADS, CHUNK, CHUNK), lambda n: (n, 0, 0, 0))
    return _pcall(
        body, name=name,
        out_shape=(jax.ShapeDtypeStruct((HEADS, t, LANE), F32),) + (jax.ShapeDtypeStruct((HEADS, t, LANE), BF16),) * 3
        + (jax.ShapeDtypeStruct((n_chunks, HEADS, CHUNK, CHUNK), BF16), jax.ShapeDtypeStruct((n_chunks, HEADS, CHUNK, CHUNK), F32)),
        grid=(n_chunks // per,), in_specs=[act(3 * HEADS), act(HEADS), act(HEADS)],
        out_specs=(act(HEADS),) * 4 + (mat, mat), semantics=("parallel",), vmem_limit=VMEM_LIMIT)(qkv, g, beta)


SCAN_CHUNKS_PER_STEP = 8


def _deltanet_scan(u, w, qg, kg, attn, g, *, name):
    t = u.shape[1]
    n_chunks = t // CHUNK
    per = SCAN_CHUNKS_PER_STEP

    def body(u_ref, w_ref, qg_ref, kg_ref, attn_ref, g_ref, o_ref, vn_ref, st_ref, s_ref):
        @pl.when(pl.program_id(0) == 0)
        def _():
            s_ref[...] = jnp.zeros_like(s_ref)

        for j in range(per):
            rows = _chunk_rows(j)
            s = [s_ref[h] for h in range(HEADS)]
            vn = [u_ref[h, rows, :] - _dot(w_ref[h, rows, :], s[h], NN) for h in range(HEADS)]
            o = [_dot(qg_ref[h, rows, :], s[h], NN) + _dot(attn_ref[j, h], vn[h], NN) for h in range(HEADS)]
            eg = [jnp.exp(_colsum(g_ref[h, rows, :])) for h in range(HEADS)]
            for h in range(HEADS):
                st_ref[j, h] = s[h]
                s_ref[h] = s[h] * eg[h] + _dot(kg_ref[h, rows, :], vn[h], TN)
                o_ref[h, rows, :] = o[h]
                vn_ref[h, rows, :] = vn[h]

    act = pl.BlockSpec((HEADS, per * CHUNK, LANE), lambda n: (0, n, 0))
    out = jax.ShapeDtypeStruct((HEADS, t, LANE), F32)
    return _pcall(
        body, name=name, out_shape=(out, out, jax.ShapeDtypeStruct((n_chunks, HEADS, LANE, LANE), F32)), grid=(n_chunks // per,),
        in_specs=[act] * 4 + [pl.BlockSpec((per, HEADS, CHUNK, CHUNK), lambda n: (n, 0, 0, 0)), act],
        out_specs=(act, act, pl.BlockSpec((per, HEADS, LANE, LANE), lambda n: (n, 0, 0, 0))),
        scratch_shapes=[pltpu.VMEM((HEADS, LANE, LANE), F32)], semantics=("arbitrary",))(u, w, qg, kg, attn, g)


def _deltanet_bscan(w, qg, kg, attn, g, do, *, name):
    t = w.shape[1]
    n_chunks = t // CHUNK
    per = SCAN_CHUNKS_PER_STEP
    steps = n_chunks // per

    def body(w_ref, qg_ref, kg_ref, attn_ref, g_ref, do_ref, dvn_ref, dsn_ref, ds_ref):
        @pl.when(pl.program_id(0) == 0)
        def _():
            ds_ref[...] = jnp.zeros_like(ds_ref)

        for j in reversed(range(per)):
            rows = _chunk_rows(j)
            dsn = [ds_ref[h] for h in range(HEADS)]
            dov = [do_ref[h, rows, :] for h in range(HEADS)]
            dvn = [_dot(attn_ref[j, h], dov[h], TN) + _dot(kg_ref[h, rows, :], dsn[h], NN) for h in range(HEADS)]
            eg = [jnp.exp(_colsum(g_ref[h, rows, :])) for h in range(HEADS)]
            for h in range(HEADS):
                dsn_ref[j, h] = dsn[h]
                ds_ref[h] = _dot(qg_ref[h, rows, :], dov[h], TN) + eg[h] * dsn[h] - _dot(w_ref[h, rows, :], dvn[h], TN)
                dvn_ref[h, rows, :] = dvn[h]

    act = pl.BlockSpec((HEADS, per * CHUNK, LANE), lambda n: (0, steps - 1 - n, 0))
    return _pcall(
        body, name=name,
        out_shape=(jax.ShapeDtypeStruct((HEADS, t, LANE), F32), jax.ShapeDtypeStruct((n_chunks, HEADS, LANE, LANE), F32)),
        grid=(steps,),
        in_specs=[act] * 3 + [pl.BlockSpec((per, HEADS, CHUNK, CHUNK), lambda n: (steps - 1 - n, 0, 0, 0)), act, act],
        out_specs=(act, pl.BlockSpec((per, HEADS, LANE, LANE), lambda n: (steps - 1 - n, 0, 0, 0))),
        scratch_shapes=[pltpu.VMEM((HEADS, LANE, LANE), F32)], semantics=("arbitrary",))(w, qg, kg, attn, g, do)


def _sum_all(x):
    return jnp.sum(jnp.sum(x, axis=1, keepdims=True), axis=0, keepdims=True)


def _rowsum(x):
    return jnp.sum(x, axis=1, keepdims=True)


def _deltanet_post(qkv, g, beta, tmats, states, dstates, do, dvn, vn, *, name):
    t = qkv.shape[1]
    n_chunks = t // CHUNK
    per = CHUNKS_PER_STEP
    probs = [(j, h) for j in range(per) for h in range(HEADS)]

    def body(qkv_ref, g_ref, b_ref, tm_ref, st_ref, dsn_ref, do_ref, dvn_ref, vn_ref, dqkv_ref, dg_ref, db_ref):
        incl, strict, _ = _chunk_masks()
        ones = jnp.ones((CHUNK, LANE), BF16)
        last_row = lax.broadcasted_iota(jnp.int32, (CHUNK, LANE), 0) == CHUNK - 1
        z = lambda f, *cols: [f(*a) for a in zip(*cols)]
        q = [qkv_ref[h, _chunk_rows(j), :] for j, h in probs]
        k = [qkv_ref[HEADS + h, _chunk_rows(j), :] for j, h in probs]
        v = [qkv_ref[2 * HEADS + h, _chunk_rows(j), :] for j, h in probs]
        bv = [b_ref[h, _chunk_rows(j), :] for j, h in probs]
        dov = [do_ref[h, _chunk_rows(j), :] for j, h in probs]
        dvn_ = [dvn_ref[h, _chunk_rows(j), :] for j, h in probs]
        vn_ = [vn_ref[h, _chunk_rows(j), :] for j, h in probs]
        tm = [tm_ref[j, h] for j, h in probs]
        s = [st_ref[j, h] for j, h in probs]
        dsn = [dsn_ref[j, h] for j, h in probs]
        dec = [_chunk_decay(g_ref[h, _chunk_rows(j), :], incl) for j, h in probs]
        decay = [d[2] for d in dec]
        egc = [jnp.exp(d[0]) for d in dec]
        ekg = [jnp.exp(d[1] - d[0]) for d in dec]
        kb = z(lambda a, b: a * b, k, bv)
        vb = z(lambda a, b: a * b, v, bv)
        kbg = z(lambda a, b: a * b, kb, egc)
        qg = z(lambda a, b: a * b, q, egc)
        kg = z(lambda a, b: a * b, k, ekg)
        kk = z(lambda a, b: _dot(a, b, NT), kb, k)
        qk = z(lambda a, b: _dot(a, b, NT), q, k)
        dattn = z(lambda a, b: jnp.where(incl, _dot(a, b, NT), 0.0), dov, vn_)
        dqg = z(lambda a, b: _dot(a, b, NT), dov, s)
        dkg = z(lambda a, b: _dot(a, b, NT), vn_, dsn)
        dglast = z(lambda a, b, c, d, e: _sum_all(a * b) * jnp.exp(e[1]) + _sum_all(c * d), s, dsn, dkg, kg, dec)
        dw = z(lambda a, b: -_dot(a, b, NT), dvn_, s)
        dtm = z(lambda a, b, c, d: _dot(a, b, NT) + _dot(c, d, NT), dvn_, vb, dw, kbg)
        dvb = z(lambda a, b: _dot(a, b, TN), tm, dvn_)
        dkbg = z(lambda a, b: _dot(a, b, TN), tm, dw)
        dlow = z(lambda a, b: jnp.where(strict, -_dot(_dot(a, b, TN), a, NT), 0.0), tm, dtm)
        dkk = z(lambda a, b: a * b, dlow, decay)
        dqk = z(lambda a, b: a * b, dattn, decay)
        dkb = z(lambda a, b, c, d: _dot(a, b, NN) + c * d, dkk, k, dkbg, egc)
        dk = z(lambda a, b, c, d, e, f, g_, h_: _dot(a, b, TN) + _dot(c, d, TN) + e * f + g_ * h_, dkk, kb, dqk, q, dkg, ekg, dkb, bv)
        dq = z(lambda a, b, c, d: _dot(a, b, NN) + c * d, dqk, k, dqg, egc)
        m = z(lambda a, b, c, d, e: (a * b + c * d) * e, dlow, kk, dattn, qk, decay)
        mcol = [_dot(mh, ones, TN) + _dot(ml, ones, TN) for mh, ml in (_split(a) for a in m)]
        for i, (j, h) in enumerate(probs):
            rows = _chunk_rows(j)
            dqkv_ref[h, rows, :] = dq[i]
            dqkv_ref[HEADS + h, rows, :] = dk[i]
            dqkv_ref[2 * HEADS + h, rows, :] = dvb[i] * bv[i]
            db_ref[h, rows, :] = jnp.broadcast_to(_rowsum(dkb[i] * k[i] + dvb[i] * v[i]), (CHUNK, LANE))
            dgc = (_rowsum(dqg[i] * qg[i] + dkbg[i] * kbg[i] - dkg[i] * kg[i]) + _rowsum(m[i]) - mcol[i]
                   + jnp.where(last_row, dglast[i], 0.0))
            dg_ref[h, rows, :] = _suffix_sum_rows(dgc)

    act = lambda heads: pl.BlockSpec((heads, per * CHUNK, LANE), lambda n: (0, n, 0))
    mat = lambda d: pl.BlockSpec((per, HEADS, d, d), lambda n: (n, 0, 0, 0))
    out = jax.ShapeDtypeStruct((HEADS, t, LANE), F32)
    return _pcall(
        body, name=name, out_shape=(jax.ShapeDtypeStruct((3 * HEADS, t, LANE), F32), out, out), grid=(n_chunks // per,),
        in_specs=[act(3 * HEADS), act(HEADS), act(HEADS), mat(CHUNK), mat(LANE), mat(LANE), act(HEADS), act(HEADS), act(HEADS)],
        out_specs=(act(3 * HEADS), act(HEADS), act(HEADS)), semantics=("parallel",),
        vmem_limit=VMEM_LIMIT)(qkv, g, beta, tmats, states, dstates, do, dvn, vn)


ANY = pl.BlockSpec(memory_space=pl.ANY)
PEERS = N_DEV - 1


def _all_gather(arrays, *, name):
    n = len(arrays)

    def body(*refs):
        ins, outs = refs[:n], refs[n:2 * n]
        send_sems, recv_sems, local_sems = refs[2 * n:]
        x, y, c = lax.axis_index("x"), lax.axis_index("y"), lax.axis_index("c")
        me, sibling = (x, y, c), (x, y, 1 - c)
        chips = [(1 - x, y), (x, 1 - y), (1 - x, 1 - y)]

        def copy(a, k, block, to, src=None):
            dst = outs[a].at[4 * block[0] + 2 * block[1] + block[2]]
            return pltpu.make_async_remote_copy(src_ref=dst if src is None else src, dst_ref=dst, send_sem=send_sems.at[a * PEERS + k],
                                                recv_sem=recv_sems.at[a * PEERS + k], device_id=to, device_id_type=MESH)

        local = [pltpu.make_async_copy(ins[a], outs[a].at[4 * x + 2 * y + c], local_sems.at[a]) for a in range(n)]
        first = []
        for a in range(n):
            first += [copy(a, 1 + j, me, (*chip, c), src=ins[a]) for j, chip in enumerate(chips)]
            first.append(copy(a, 0, me, sibling, src=ins[a]))
        for cp in first:
            cp.start()
        passed = []
        for a in range(n):
            for j, chip in enumerate(chips):
                copy(a, 1 + j, (*chip, c), me).wait_recv()
                fwd = copy(a, 4 + j, (*chip, c), sibling)
                fwd.start()
                passed.append(fwd)
        for cp in local:
            cp.start()
        for a in range(n):
            copy(a, 0, sibling, me).wait_recv()
            for j, chip in enumerate(chips):
                copy(a, 4 + j, (*chip, 1 - c), me).wait_recv()
        for cp in first + passed:
            cp.wait_send()
        for cp in local:
            cp.wait()

    return _pcall(body, name=name, out_shape=tuple(jax.ShapeDtypeStruct((N_DEV,) + a.shape, a.dtype) for a in arrays),
                  in_specs=[ANY] * n, out_specs=(ANY,) * n,
                  scratch_shapes=[pltpu.SemaphoreType.DMA((n * PEERS,)), pltpu.SemaphoreType.DMA((n * PEERS,)),
                                  pltpu.SemaphoreType.DMA((n,))])(*arrays)


CHIPS = 4


def _pair_exchange(arrays, *, name):
    n = len(arrays)

    def body(*refs):
        ins, outs = refs[:n], refs[n:2 * n]
        send_sems, recv_sems = refs[2 * n:]
        x, y, c = lax.axis_index("x"), lax.axis_index("y"), lax.axis_index("c")
        copies = []
        for a in range(n):
            for q in range(CHIPS):
                cp = pltpu.make_async_remote_copy(src_ref=ins[a].at[2 * q + 1 - c], dst_ref=outs[a].at[q],
                                                  send_sem=send_sems.at[a * CHIPS + q], recv_sem=recv_sems.at[a * CHIPS + q],
                                                  device_id=(x, y, 1 - c), device_id_type=MESH)
                cp.start()
                copies.append(cp)
        for cp in copies:
            cp.wait()

    return _pcall(body, name=name, out_shape=tuple(jax.ShapeDtypeStruct((CHIPS,) + a.shape[1:], a.dtype) for a in arrays),
                  in_specs=[ANY] * n, out_specs=(ANY,) * n,
                  scratch_shapes=[pltpu.SemaphoreType.DMA((n * CHIPS,)), pltpu.SemaphoreType.DMA((n * CHIPS,))])(*arrays)


def _pair_add(blocks, theirs, *, name):
    _, r, c_ = blocks.shape
    tr = _tile(r, 512, 16)

    def body(mine_ref, theirs_ref, o_ref):
        core = lax.axis_index("c")
        own = jnp.where(core == 0, mine_ref[0, 0].astype(F32), mine_ref[0, 1].astype(F32))
        o_ref[0] = (own + theirs_ref[0].astype(F32)).astype(o_ref.dtype)

    spec = pl.BlockSpec((1, tr, c_), lambda q, i: (q, i, 0))
    return _pcall(body, name=name, out_shape=jax.ShapeDtypeStruct(theirs.shape, theirs.dtype), grid=(CHIPS, r // tr),
                  in_specs=[pl.BlockSpec((1, 2, tr, c_), lambda q, i: (q, 0, i, 0)), spec], out_specs=spec,
                  semantics=("parallel", "parallel"), vmem_limit=VMEM_LIMIT)(blocks.reshape(CHIPS, 2, r, c_), theirs)


HBM = pl.BlockSpec(memory_space=pltpu.HBM)
SEM = pl.BlockSpec(memory_space=pltpu.SEMAPHORE)
EFFECT = pltpu.SideEffectType.DATAFLOW_SIDE_EFFECTING


GATHER, CHIP_GATHER, CHIP_SCATTER = "gather", "chip_gather", "chip_scatter"
PEERS_OF = {GATHER: N_DEV - 1, CHIP_GATHER: CHIPS - 1, CHIP_SCATTER: CHIPS - 1}


def _direct_copies(srcs, lands, send_sems, recv_sems, local_sems, kind):
    x, y, c = lax.axis_index("x"), lax.axis_index("y"), lax.axis_index("c")
    peers = PEERS_OF[kind]
    mine = 2 * x + y if kind == CHIP_SCATTER else 4 * x + 2 * y + c
    copies = []
    for a, (src, land) in enumerate(zip(srcs, lands)):
        for k in range(1, peers + 1):
            bits = k if kind == GATHER else 2 * k
            px = 1 - x if bits & 4 else x
            py = 1 - y if bits & 2 else y
            pc = 1 - c if bits & 1 else c
            copies.append(pltpu.make_async_remote_copy(
                src_ref=src.at[2 * px + py] if kind == CHIP_SCATTER else src, dst_ref=land.at[mine],
                send_sem=send_sems.at[a * peers + k - 1], recv_sem=recv_sems.at[a * peers + k - 1],
                device_id=(px, py, pc), device_id_type=MESH))
    for a, (src, land) in enumerate(zip(srcs, lands)):
        copies.append(pltpu.make_async_copy(src.at[mine] if kind == CHIP_SCATTER else src, land.at[mine], local_sems.at[a]))
    return copies


def _pair_swap(arrays, *, name):
    n = len(arrays)

    def body(*refs):
        mine, zones = refs[:n], refs[n:2 * n]
        send_sems, recv_sems = refs[2 * n:]
        x, y, c = lax.axis_index("x"), lax.axis_index("y"), lax.axis_index("c")
        copies = []
        for a in range(n):
            for q in range(CHIPS):
                copies.append(pltpu.make_async_remote_copy(
                    src_ref=mine[a].at[2 * q + c], dst_ref=zones[a].at[2 * q + c], send_sem=send_sems.at[a * CHIPS + q],
                    recv_sem=recv_sems.at[a * CHIPS + q], device_id=(x, y, 1 - c), device_id_type=MESH))
        for cp in copies:
            cp.start()
        for cp in copies:
            cp.wait()

    return _pcall(body, name=name, out_shape=tuple(jax.ShapeDtypeStruct(a.shape, a.dtype) for a in arrays),
                  in_specs=[ANY] * n, out_specs=(ANY,) * n, input_output_aliases={i: i for i in range(n)},
                  scratch_shapes=[pltpu.SemaphoreType.DMA((n * CHIPS,)), pltpu.SemaphoreType.DMA((n * CHIPS,))])(*arrays)


def _exchange_start(groups, kind, *, name, after=None):
    srcs = [s for group in groups for s in group]
    n = len(srcs)
    sizes = [len(group) for group in groups]
    starts = [sum(sizes[:g]) for g in range(len(groups))]
    land_shapes = [s.shape if kind == CHIP_SCATTER else (N_DEV,) + s.shape for s in srcs]
    peers = PEERS_OF[kind]
    extra = [] if after is None else [after]

    def body(*refs):
        srcs_, lands = refs[:n], refs[n:2 * n]
        token = refs[-1]
        sem_refs = refs[2 * n + len(extra):]
        for g, (at, size) in enumerate(zip(starts, sizes)):
            send_sems, recv_sems, local_sems = sem_refs[3 * g:3 * g + 3]
            for cp in _direct_copies(srcs_[at:at + size], lands[at:at + size], send_sems, recv_sems, local_sems, kind):
                cp.start()
        token[...] = jnp.zeros_like(token)

    sems = tuple(t for size in sizes for t in (pltpu.SemaphoreType.DMA((size * peers,)), pltpu.SemaphoreType.DMA((size * peers,)),
                                               pltpu.SemaphoreType.DMA((size,))))
    thru = tuple(pltpu.HBM(s.shape, s.dtype) for s in srcs) + tuple(pltpu.HBM(shp, s.dtype) for shp, s in zip(land_shapes, srcs))
    ins = [pltpu.with_memory_space_constraint(s, pltpu.HBM) for s in srcs]
    ins += [pltpu.with_memory_space_constraint(lax.empty(shp, s.dtype), pltpu.HBM) for shp, s in zip(land_shapes, srcs)]
    out = pl.pallas_call(
        body, name=name, out_shape=sems + thru + (jax.ShapeDtypeStruct((SUBLANE, LANE), F32),),
        in_specs=[HBM] * (2 * n) + [ANY] * len(extra),
        out_specs=(SEM,) * len(sems) + (HBM,) * (2 * n) + (pl.BlockSpec(memory_space=pltpu.VMEM),),
        input_output_aliases={i: len(sems) + i for i in range(2 * n)},
        compiler_params=pltpu.CompilerParams(has_side_effects=EFFECT))(*ins, *extra)
    arrays = out[len(sems):-1]
    started = [tuple(out[3 * g:3 * g + 3]) + tuple(arrays[at:at + size]) + tuple(arrays[n + at:n + at + size])
               for g, (at, size) in enumerate(zip(starts, sizes))]
    return started, out[-1]


def _exchange_wait(started, after, kind, *, name):
    n = (len(started) - 3) // 2
    sems, arrays = started[:3], started[3:]

    def body(*refs):
        srcs_, lands = refs[:n], refs[n:2 * n]
        send_sems, recv_sems, local_sems = refs[2 * n:2 * n + 3]
        for cp in _direct_copies(srcs_, lands, send_sems, recv_sems, local_sems, kind):
            cp.wait()

    out = pl.pallas_call(
        body, name=name, out_shape=tuple(pltpu.HBM(a.shape, a.dtype) for a in arrays),
        in_specs=[HBM] * (2 * n) + [SEM] * 3 + [ANY], out_specs=(HBM,) * (2 * n),
        input_output_aliases={i: i for i in range(2 * n)},
        compiler_params=pltpu.CompilerParams(has_side_effects=EFFECT))(*arrays, *sems, after)
    return out[n:]


def _adamw_reduce(w, parts, m, v, *, name, after=None):
    layers, r, c = w.shape
    assert len(parts) == layers
    senders = parts[0].shape[0]
    tr = _tile(r, 512, 16)
    tiles = r // tr
    bc1 = 1.0 - ADAM_B1 ** ADAM_STEP
    bc2 = 1.0 - ADAM_B2 ** ADAM_STEP

    def body(w_ref, *rest):
        p_refs = rest[:layers]
        m_ref, v_ref, g_ref, d_ref, nm_ref, nv_ref = rest[layers:]

        def update(p_ref):
            g = p_ref[0, :, pl.ds(0, c)].astype(F32)
            for s in range(1, senders):
                g = g + p_ref[s, :, pl.ds(0, c)].astype(F32)
            nm = ADAM_B1 * m_ref[0] + (1.0 - ADAM_B1) * g
            nv = ADAM_B2 * v_ref[0] + (1.0 - ADAM_B2) * (g * g)
            g_ref[0] = g
            nm_ref[0] = nm
            nv_ref[0] = nv
            d_ref[0] = -ADAM_LR * ((nm / bc1) / (jnp.sqrt(nv / bc2) + ADAM_EPS) + ADAM_WD * w_ref[0])

        for layer in range(layers):
            pl.when(pl.program_id(0) == layer)(functools.partial(update, p_refs[layer]))

    def part_spec(layer, shape):
        rest = 0 if layer > 0 else tiles - 1
        return pl.BlockSpec((senders, tr, shape[2]), lambda l, i: (0, jnp.where(l == layer, i, rest), 0))

    spec = pl.BlockSpec((1, tr, c), lambda l, i: (l, i, 0))
    out = jax.ShapeDtypeStruct((layers, r, c), F32)
    return _pcall(body, name=name, out_shape=(out,) * 4, grid=(layers, tiles),
                  in_specs=[spec] + [part_spec(layer, p.shape) for layer, p in enumerate(parts)] + [spec, spec],
                  out_specs=(spec,) * 4, semantics=("arbitrary", "arbitrary"), vmem_limit=VMEM_LIMIT, after=after)(w, *parts, m, v)


def _pool_windows():
    return jnp.repeat(jnp.asarray(POOL_WINDOWS, F32), POOL_DIM // len(POOL_WINDOWS))[None, :]


def _block_diag_pairs(pool_w):
    z = jnp.zeros_like(pool_w[0])
    return jnp.stack([jnp.block([[pool_w[2 * b], z], [z, pool_w[2 * b + 1]]]) for b in range(2)])


def _pad_lanes(vec):
    return jnp.zeros((1, LANE), F32).at[0, :vec.shape[0]].set(vec)


FF_SHARD = D_FF // N_DEV
FF_BLOCK = 384
D_FF_PAD = N_DEV * FF_BLOCK


def _layer_fwd(x, p_i, wt, fetch):
    wt = {**wt, **fetch(0, x)}
    h1 = _rmsnorm_fwd(x, wt["norm1_g"], name="rmsnorm_fwd")
    proj = _matmul(h1, wt["w_in"], "nn", name="mm_in")
    wt.update(fetch(1, proj))
    qkv = _qkv_prep_fwd(proj, wt["conv_qkv"], name="qkv_prep_fwd")
    g, beta = _gates_fwd(proj, wt["a_log"], wt["dt_bias"], name="gates_fwd")
    u, w, qg, kg, attn, tmats = _deltanet_prep(qkv, g, beta, name="deltanet_prep")
    o, vn, states = _deltanet_scan(u, w, qg, kg, attn, g, name="deltanet_scan")
    o_a = _apost_fwd(o, proj, wt["onorm_g"], name="apost_fwd")
    o_b = _pool_fwd(proj, wt["pool_win"], wt["pool_wbd"], wt["pool_scale"], name="pool_fwd")
    o_c = _sconv_fwd(proj, wt["sconv_w"], name="sconv_fwd")
    mixed = jnp.concatenate([o_a, o_b, o_c], axis=1)
    x1 = _matmul(mixed, wt["w_out"], "nn", res=x, name="mm_out")
    h2 = _rmsnorm_fwd(x1, wt["norm2_g"], name="rmsnorm_fwd")
    wt.update(fetch(2, h2))
    ff, gate, up = _swiglu_fwd(h2, wt["w_gate"], wt["w_up"], name="swiglu_fwd")
    wt.update(fetch(3, ff))
    x2 = _matmul(ff, wt["w_down"], "nn", res=x1, name="mm_down")
    wt.update(fetch(4, x2))
    pgl = _matmul(x2, wt["ple_gate"], "nn", name="mm_pleg")
    pp = _matmul(p_i, wt["ple_proj"], "nn", b_blocked=True, name="mm_plep")
    x3 = _ple_fwd(x2, pgl, pp, name="ple_fwd")
    saved = dict(x=x, h1=h1, proj=proj, qkv=qkv, g=g, beta=beta, o=o, states=states, tmats=tmats, mixed=mixed, x1=x1, h2=h2,
                 gate=gate, up=up, ff=ff, x2=x2, pgl=pgl, pp=pp, p=p_i, w=w, qg=qg, kg=kg, attn=attn, vn=vn, wt=wt)
    return x3, saved


def _col_blocks(g):
    a = g.shape[0]
    return jnp.transpose(g.reshape(a, N_DEV, -1), (1, 0, 2))


def _cols_joined(blocks):
    return jnp.transpose(blocks, (1, 0, 2)).reshape(blocks.shape[1], -1)


def _layer_bwd(dx3, sv, emit, after=None):
    gr, big = {}, {}
    wt = sv["wt"]
    rows = D_MODEL // N_DEV
    dpgl, dpp = _ple_bwd(dx3, sv["pgl"], sv["pp"], name="ple_bwd", after=after)
    big["ple_proj"] = _matmul(sv["p"], dpp, "tn", out_blocked=(N_DEV, rows), out_dtype=BF16, name="mm_dplep")
    big["ple_gate"] = _matmul(sv["x2"], dpgl, "tn", out_dtype=BF16, name="mm_dpleg").reshape(N_DEV, rows, D_MODEL)
    dx2 = _matmul(dpgl, wt["ple_gate"], "nt", res=dx3, name="mm_dx2")
    big["w_down"] = _matmul(sv["ff"], dx2, "tn", out_dtype=BF16, name="mm_ddown").reshape(N_DEV, FF_BLOCK, D_MODEL)
    dgate, dup = _swiglu_bwd(dx2, wt["w_down"], sv["gate"], sv["up"], name="swiglu_bwd", after=emit(0, big))
    big["w_gate"] = _matmul(sv["h2"], dgate, "tn", out_blocked=(N_DEV, FF_BLOCK), out_dtype=BF16, name="mm_dgate")
    big["w_up"] = _matmul(sv["h2"], dup, "tn", out_blocked=(N_DEV, FF_BLOCK), out_dtype=BF16, name="mm_dup")
    dh2 = _matmul(dgate, wt["w_gate"], "nt", b_blocked=True, name="mm_dh2_gate")
    dh2 = _matmul(dup, wt["w_up"], "nt", b_blocked=True, res=dh2, name="mm_dh2_up")
    dx1, gr["norm2_g"] = _rmsnorm_bwd(sv["x1"], wt["norm2_g"], dh2, dx2, name="rmsnorm_bwd")
    big["w_out"] = _matmul(sv["mixed"], dx1, "tn", out_dtype=BF16, name="mm_dout").reshape(N_DEV, rows, D_MODEL)
    dmixed = _matmul(dx1, wt["w_out"], "nt", name="mm_dmixed", after=emit(1, big))
    proj = sv["proj"]
    dcb, dcc, dch, dsconv = _sconv_bwd(proj, wt["sconv_w"], dmixed, name="sconv_bwd")
    big["sconv_w"] = _col_blocks(dsconv)
    dhp, dwbd, gr["pool_scale"] = _pool_bwd(proj, wt["pool_win"], wt["pool_wbd"], wt["pool_scale"], dmixed, name="pool_bwd")
    half = LANE // 2
    gr["pool_w"] = jnp.stack([dwbd[0, :half, :half], dwbd[0, half:, half:], dwbd[1, :half, :half], dwbd[1, half:, half:]])
    do, dz, gr["onorm_g"] = _apost_bwd(sv["o"], proj, wt["onorm_g"], dmixed, name="apost_bwd")
    dvn, dstates = _deltanet_bscan(sv["w"], sv["qg"], sv["kg"], sv["attn"], sv["g"], do, name="deltanet_bscan")
    dqkv_h, dg, dbeta = _deltanet_post(sv["qkv"], sv["g"], sv["beta"], sv["tmats"], sv["states"], dstates, do, dvn, sv["vn"],
                                       name="deltanet_post")
    dab, dalog, ddtb = _gates_bwd(proj, wt["a_log"], wt["dt_bias"], dg, dbeta, name="gates_bwd")
    gr["a_log"], gr["dt_bias"] = dalog[0, :HEADS], ddtb[0, :HEADS]
    dqkv, dconv = _qkv_prep_bwd(proj, wt["conv_qkv"], dqkv_h, name="qkv_prep_bwd")
    big["conv_qkv"] = _col_blocks(dconv)
    dproj = jnp.concatenate([dqkv, dz, dab, dhp, dcb, dcc, dch], axis=1)
    dwin = _matmul(sv["h1"], dproj, "tn", out_dtype=BF16, name="mm_din")
    big["w_in"] = _col_blocks(jnp.concatenate([dwin[:, :AB_COL + 2 * HEADS], dwin[:, AB_COL + LANE:]], axis=1))
    dh1 = _matmul(dproj, wt["w_in"], "nt", name="mm_dh1", after=emit(2, big))
    dx, gr["norm1_g"] = _rmsnorm_bwd(sv["x"], wt["norm1_g"], dh1, dx1, name="rmsnorm_bwd")
    return dx, gr


FETCH_GROUPS = (("w_in", "conv_qkv", "sconv_w"), ("w_out",), ("w_gate", "w_up"), ("w_down",), ("ple_gate", "ple_proj"))
EMIT_GROUPS = (("ple_proj", "ple_gate", "w_down"), ("w_gate", "w_up", "w_out"), ("w_in", "conv_qkv", "sconv_w"))


def _small_weights(w, i):
    return dict(
        norm1_g=w["norm1_g"][i][None], norm2_g=w["norm2_g"][i][None], onorm_g=w["onorm_g"][i][None],
        a_log=_pad_lanes(w["a_log"][i]), dt_bias=_pad_lanes(w["dt_bias"][i]),
        pool_scale=w["pool_scale"][i][None], pool_win=_pool_windows(), pool_wbd=_block_diag_pairs(w["pool_w"][i]))


def _as_read(name, gathered):
    if name == "w_in":
        w_in = _cols_joined(gathered)
        return jnp.concatenate([w_in[:, :AB_COL + 2 * HEADS], jnp.zeros((D_MODEL, LANE - 2 * HEADS), BF16),
                                w_in[:, AB_COL + 2 * HEADS:]], axis=1)
    if name in ("conv_qkv", "sconv_w"):
        return _cols_joined(gathered)
    if name in ("w_gate", "w_up", "ple_proj"):
        return gathered
    return gathered.reshape(-1, D_MODEL)


def _layer_weights(gathered, w, i):
    return {**_small_weights(w, i), **{k: _as_read(k, g) for k, g in gathered.items()}}


def _local_step(x, p, target, layers, final_g):
    saved = []
    h = x
    for i in range(DEPTH):
        replicated = {k: v for k, v in layers[i].items() if k not in SHARDED}
        h, sv = _layer_fwd(h, p[i], replicated, lambda group, after, i=i: {k: layers[i][k] for k in FETCH_GROUPS[group]})
        saved.append(sv)
    dx, dgf, loss = _loss_head(h, final_g, target, name="loss_head")
    big, small = [{} for _ in range(DEPTH)], [None] * DEPTH
    for i in reversed(range(DEPTH)):
        dx, small[i] = _layer_bwd(dx, saved[i], lambda group, blocks, i=i: big[i].update({k: blocks[k] for k in EMIT_GROUPS[group]}))
    return loss, dx, big, small, dgf


SHARDED = ("w_in", "w_gate", "w_up", "w_down", "w_out", "ple_gate", "ple_proj", "conv_qkv", "sconv_w")
SMALL = ("norm1_g", "a_log", "dt_bias", "onorm_g", "pool_w", "pool_scale", "norm2_g", "final_g")
SLAB_COLS = 1024


def _payload(name, shard):
    if name in ("conv_qkv", "sconv_w"):
        return shard
    out = shard.astype(BF16)
    if name in ("w_gate", "w_up"):
        out = jnp.pad(out, ((0, 0), (0, FF_BLOCK - FF_SHARD)))
    if name == "w_down":
        out = jnp.pad(out, ((0, FF_BLOCK - FF_SHARD), (0, 0)))
    return out


def _slab_rows(shape):
    size = 1
    for s in shape:
        size *= s
    return SUBLANE * -(-size // (SUBLANE * SLAB_COLS))


def _pack_slab(parts, extra_row):
    rows = []
    for name in SMALL:
        flat = parts[name].reshape(-1)
        nrow = _slab_rows(parts[name].shape)
        rows.append(jnp.pad(flat, (0, nrow * SLAB_COLS - flat.shape[0])).reshape(nrow, SLAB_COLS))
    rows.append(jnp.pad(extra_row, ((0, SUBLANE - 1), (0, 0))))
    return jnp.concatenate(rows, axis=0)


def _unpack_slab(slab, shapes):
    out, row = {}, 0
    for name in SMALL:
        size = 1
        for s in shapes[name]:
            size *= s
        out[name] = slab[row:row + _slab_rows(shapes[name])].reshape(-1)[:size].reshape(shapes[name])
        row += _slab_rows(shapes[name])
    return out, row


def kernel(x, p, norm1_g, w_in, conv_qkv, a_log, dt_bias, onorm_g, pool_w, pool_scale, sconv_w, w_out, norm2_g, w_gate, w_up, w_down, ple_proj, ple_gate, final_g, loss_target, m_norm1_g, m_w_in, m_conv_qkv, m_a_log, m_dt_bias, m_onorm_g, m_pool_w, m_pool_scale, m_sconv_w, m_w_out, m_norm2_g, m_w_gate, m_w_up, m_w_down, m_ple_proj, m_ple_gate, m_final_g, v_norm1_g, v_w_in, v_conv_qkv, v_a_log, v_dt_bias, v_onorm_g, v_pool_w, v_pool_scale, v_sconv_w, v_w_out, v_norm2_g, v_w_gate, v_w_up, v_w_down, v_ple_proj, v_ple_gate, v_final_g):
    names = ["norm1_g", "w_in", "conv_qkv", "a_log", "dt_bias", "onorm_g", "pool_w", "pool_scale", "sconv_w", "w_out", "norm2_g",
             "w_gate", "w_up", "w_down", "ple_proj", "ple_gate", "final_g"]
    w = dict(zip(names, [norm1_g, w_in, conv_qkv, a_log, dt_bias, onorm_g, pool_w, pool_scale, sconv_w, w_out, norm2_g, w_gate, w_up,
                         w_down, ple_proj, ple_gate, final_g]))
    m = dict(zip(names, [m_norm1_g, m_w_in, m_conv_qkv, m_a_log, m_dt_bias, m_onorm_g, m_pool_w, m_pool_scale, m_sconv_w, m_w_out,
                         m_norm2_g, m_w_gate, m_w_up, m_w_down, m_ple_proj, m_ple_gate, m_final_g]))
    v = dict(zip(names, [v_norm1_g, v_w_in, v_conv_qkv, v_a_log, v_dt_bias, v_onorm_g, v_pool_w, v_pool_scale, v_sconv_w, v_w_out,
                         v_norm2_g, v_w_gate, v_w_up, v_w_down, v_ple_proj, v_ple_gate, v_final_g]))

    first, rest = FETCH_GROUPS[0], tuple(k for members in FETCH_GROUPS[1:] for k in members)
    gathered = dict(zip(first, _all_gather([_payload(k, w[k][0]) for k in first], name="all_gather_weights")))
    (flying0,), token = _exchange_start([[_payload(k, w[k][0]) for k in rest]], CHIP_GATHER, name="gather_start_0")
    replicated = [_small_weights(w, i) for i in range(DEPTH)]
    replicated[0]["norm1_g"] = replicated[0]["norm1_g"] + token[0, 0]
    flying1 = []

    def fetch(i, group, after):
        if i == 0 and group == 1:
            landed = _exchange_wait(flying0, after, CHIP_GATHER, name="gather_wait_0")
            gathered.update(zip(rest, _pair_swap(landed, name="pair_swap")))
            started, token = _exchange_start([[_payload(k, w[k][1]) for k in SHARDED]], CHIP_GATHER, name="gather_start_1",
                                             after=gathered[rest[0]])
            flying1.extend(started)
            return {**{k: _as_read(k, gathered[k]) for k in FETCH_GROUPS[group]},
                    "conv_qkv": _as_read("conv_qkv", gathered["conv_qkv"]) + token[0, 0]}
        if i == 1 and group == 0:
            landed = _exchange_wait(flying1[0], after, CHIP_GATHER, name="gather_wait_1")
            gathered.update(zip(SHARDED, _pair_swap(landed, name="pair_swap")))
        return {k: _as_read(k, gathered[k]) for k in FETCH_GROUPS[group]}

    def reduce_scatter_start(members, blocks, tag):
        mine = [blocks[k] for k in members]
        theirs = _pair_exchange(mine, name="pair_exchange")
        sums = [_pair_add(a, b, name="pair_add") for a, b in zip(mine, theirs)]
        (started,), token = _exchange_start([sums], CHIP_SCATTER, name="exchange_start_" + tag)
        return started, token

    h, saved0 = _layer_fwd(x[0], p[0, 0], replicated[0], functools.partial(fetch, 0))
    h, saved1 = _layer_fwd(h, p[1, 0], replicated[1], functools.partial(fetch, 1))
    dx, dgf, loss_part = _loss_head(h, final_g[None], loss_target[0], name="loss_head")
    small, big1, flying0 = [None] * DEPTH, {}, []
    dx, small[1] = _layer_bwd(dx, saved1, lambda group, blocks: big1.update({k: blocks[k] for k in EMIT_GROUPS[group]}))
    flying1, token = reduce_scatter_start(SHARDED, big1, "1")

    def emit(group, blocks):
        started, token = reduce_scatter_start(EMIT_GROUPS[group], blocks, f"0_{group}")
        flying0.append(started)
        return token

    dx, small[0] = _layer_bwd(dx, saved0, emit, after=token)
    received = [{}, dict(zip(SHARDED, _exchange_wait(flying1, dx, CHIP_SCATTER, name="exchange_wait_1")))]
    for group, members in enumerate(EMIT_GROUPS):
        received[0].update(zip(members, _exchange_wait(flying0[group], dx, CHIP_SCATTER, name=f"exchange_wait_0_{group}")))

    grads = {k: jnp.stack([small[i][k] for i in range(DEPTH)]) for k in small[0]}
    grads = {k: g[:, 0] if k in ("norm1_g", "norm2_g", "onorm_g", "pool_scale") else g for k, g in grads.items()}
    grads["final_g"] = dgf[0]
    loss_row = jnp.pad(loss_part, ((0, 0), (0, SLAB_COLS - LANE)))
    (small_flying,), token = _exchange_start([[_pack_slab(grads, loss_row)]], GATHER, name="small_gather_start")

    out_g, out_d, out_m, out_v = {}, {}, {}, {}
    for k in SHARDED:
        out_g[k], out_d[k], out_m[k], out_v[k] = _adamw_reduce(w[k], [received[i][k] for i in range(DEPTH)], m[k], v[k],
                                                                name="adamw_" + k, after=token)
    behind_all = jnp.stack([out_v[k][0, 0, 0] for k in SHARDED])
    (small_parts,) = _exchange_wait(small_flying, behind_all, GATHER, name="small_gather_wait")
    zero_row = jnp.zeros((1, SLAB_COLS), F32)
    slabs = _adamw_reduce(_pack_slab(w, zero_row)[None], [small_parts], _pack_slab(m, zero_row)[None],
                          _pack_slab(v, zero_row)[None], name="adamw_small")
    slabs = [s[0] for s in slabs]
    shapes = {k: w[k].shape for k in SMALL}
    for dst, slab in zip((out_g, out_d, out_m, out_v), slabs):
        vals, _ = _unpack_slab(slab, shapes)
        dst.update(vals)
    _, loss_at = _unpack_slab(slabs[0], shapes)
    loss = slabs[0][loss_at, 0]

    return (loss, dx[None], *[out_g[k] for k in names], *[out_d[k] for k in names], *[out_m[k] for k in names],
            *[out_v[k] for k in names])
```

```python
import functools

import jax
import jax.numpy as jnp
from jax import lax
from jax.experimental import pallas as pl
from jax.experimental.pallas import tpu as pltpu

F32 = jnp.float32
BF16 = jnp.bfloat16

D_MODEL = 1024
DEPTH = 2
PLE_DIM = 256
EPS = 1e-6
HEAD_DIM = 128
HEADS = 4
A_DIM = HEADS * HEAD_DIM
QKV_TAPS = 4
CHUNK = 64
POOL_WINDOWS = (2, 4, 8, 16)
POOL_DIM = 256
CONV_DIM = 256
CONV_TAPS = 3
D_FF = 2816
D_IN = 3080
D_IN_PAD = 3200
AB_COL = 2048
N_DEV = 8

ADAM_LR = 0.001
ADAM_B1 = 0.9
ADAM_B2 = 0.999
ADAM_EPS = 1e-08
ADAM_WD = 0.01
ADAM_STEP = 10

LANE = 128
SUBLANE = 8
VMEM_BYTES_V7X = 64 * 1024 * 1024
VMEM_LIMIT = 48 * 1024 * 1024

_HI = lax.Precision.HIGHEST
NN = ((1,), (0,))
NT = ((1,), (1,))
TN = ((0,), (0,))
MESH = pl.DeviceIdType.MESH


def _dot(a, b, dims, hi=False):
    if hi:
        return lax.dot_general(a, b, (dims, ((), ())), precision=_HI, preferred_element_type=F32)
    return lax.dot_general(a.astype(BF16), b.astype(BF16), (dims, ((), ())), preferred_element_type=F32)


def _pcall(body, *, name, out_shape, grid=(), in_specs=None, out_specs=None, scratch_shapes=(), semantics=None,
           vmem_limit=None, after=None, **kw):
    params = {}
    if semantics is not None:
        params["dimension_semantics"] = semantics
    if vmem_limit is not None:
        params["vmem_limit_bytes"] = vmem_limit
    if after is not None:
        n_in, inner = len(in_specs), body
        body = lambda *refs: inner(*refs[:n_in], *refs[n_in + 1:])
        in_specs = list(in_specs) + [pl.BlockSpec(after.shape, lambda *_: (0,) * after.ndim)]
    call = pl.pallas_call(
        body, name=name, out_shape=out_shape, grid=grid, in_specs=in_specs, out_specs=out_specs,
        scratch_shapes=list(scratch_shapes), compiler_params=pltpu.CompilerParams(**params), **kw)
    return call if after is None else (lambda *args: call(*args, after))


def _sigmoid(x):
    return 1.0 / (1.0 + jnp.exp(-x))


def _softplus(x):
    return jnp.maximum(x, 0.0) + jnp.log(1.0 + jnp.exp(-jnp.abs(x)))


def _tile(n, cap, mult):
    if n <= cap:
        return n
    best = None
    for t in range(mult, cap + 1, mult):
        if n % t == 0:
            best = t
    assert best is not None, (n, cap, mult)
    return best


ROWS_PER_STEP = 512
NARROW_RESULT = 1024
COLS_PER_DOT = 640


def _resident(weight):
    return pl.BlockSpec(weight.shape, lambda i: (0,) * weight.ndim, pipeline_mode=pl.Buffered(1))


def _matmul_rows(a, b, mode, *, name, res=None, out_dtype=F32, b_blocked=False, after=None):
    m, k = a.shape
    if b_blocked:
        nb, _, bw = b.shape
        n = nb * bw if mode == "nn" else b.shape[1]
    else:
        n = b.shape[1] if mode == "nn" else b.shape[0]
    tm = _tile(m, ROWS_PER_STEP if n > NARROW_RESULT else 2 * ROWS_PER_STEP, 16)
    cn = bw if (b_blocked and mode == "nn") else _tile(n, COLS_PER_DOT, LANE)
    has_res = res is not None

    def body(*refs):
        a_ref, b_ref = refs[0], refs[1]
        res_ref = refs[2] if has_res else None
        o_ref = refs[2 + has_res]
        if not (b_blocked and mode == "nt"):
            av = a_ref[...].astype(BF16)
        for j in range(n // cn):
            cols = pl.ds(j * cn, cn)
            if mode == "nn":
                part = _dot(av, b_ref[j] if b_blocked else b_ref[:, cols], NN)
            elif not b_blocked:
                part = _dot(av, b_ref[cols, :], NT)
            else:
                part = None
                for s in range(nb):
                    term = _dot(a_ref[:, pl.ds(s * bw, bw)], b_ref[s, cols, :], NT)
                    part = term if part is None else part + term
            if has_res:
                part = part + res_ref[:, cols]
            o_ref[:, cols] = part.astype(o_ref.dtype)

    row = lambda width: pl.BlockSpec((tm, width), lambda i: (i, 0))
    whole = _resident(b)
    ins = [a, b] + ([res] if has_res else [])
    specs = [row(k), whole] + ([row(n)] if has_res else [])
    return _pcall(body, name=name, out_shape=jax.ShapeDtypeStruct((m, n), out_dtype), grid=(m // tm,), in_specs=specs,
                  out_specs=row(n), semantics=("parallel",), vmem_limit=VMEM_LIMIT, after=after)(*ins)


def _matmul(a, b, mode, *, name, res=None, out_dtype=F32, b_blocked=False, out_blocked=None, after=None):
    if mode != "tn":
        return _matmul_rows(a, b, mode, name=name, res=res, out_dtype=out_dtype, b_blocked=b_blocked, after=after)
    assert res is None and not b_blocked and after is None
    (t, m), (t2, n) = a.shape, b.shape
    assert t == t2, (a.shape, b.shape)
    tm = _tile(m, 1024, LANE)
    tn = _tile(n, COLS_PER_DOT, LANE)
    if out_blocked is not None:
        assert out_blocked[0] * out_blocked[1] == n
        tn = out_blocked[1]

    def body(a_ref, b_ref, o_ref):
        part = _dot(a_ref[...], b_ref[...], TN).astype(o_ref.dtype)
        if out_blocked is None:
            o_ref[...] = part
        else:
            o_ref[0] = part

    o_spec = (pl.BlockSpec((tm, tn), lambda i, j: (i, j)) if out_blocked is None
              else pl.BlockSpec((1, tm, tn), lambda i, j: (j, i, 0)))
    o_shape = (m, n) if out_blocked is None else (out_blocked[0], m, out_blocked[1])
    return _pcall(body, name=name, out_shape=jax.ShapeDtypeStruct(o_shape, out_dtype), grid=(m // tm, n // tn),
                  in_specs=[pl.BlockSpec((t, tm), lambda i, j: (0, i)), pl.BlockSpec((t, tn), lambda i, j: (0, j))],
                  out_specs=o_spec, semantics=("parallel", "parallel"), vmem_limit=VMEM_LIMIT)(a, b)


ROW_TILE = 512


def _rows(t, width, idx=0):
    return pl.BlockSpec((ROW_TILE, width), lambda i: (i, idx))


def _vec(width):
    return pl.BlockSpec((1, width), lambda i: (0, 0))


def _rmsnorm_fwd(x, g, *, name):
    t, d = x.shape

    def body(x_ref, g_ref, h_ref):
        xv = x_ref[...]
        r = lax.rsqrt(jnp.mean(xv * xv, axis=-1, keepdims=True) + EPS)
        h_ref[...] = (xv * r * g_ref[...]).astype(BF16)

    return _pcall(body, name=name, out_shape=jax.ShapeDtypeStruct((t, d), BF16), grid=(t // ROW_TILE,),
                  in_specs=[_rows(t, d), _vec(d)], out_specs=_rows(t, d), semantics=("parallel",))(x, g)


def _rmsnorm_bwd(x, g, dh, dres, *, name):
    t, d = x.shape

    def body(x_ref, g_ref, dh_ref, dres_ref, dx_ref, dg_ref):
        xv = x_ref[...]
        r = lax.rsqrt(jnp.mean(xv * xv, axis=-1, keepdims=True) + EPS)
        xhat = xv * r
        dhv = dh_ref[...].astype(F32)
        dhg = dhv * g_ref[...]
        dx_ref[...] = dres_ref[...] + r * (dhg - xhat * jnp.mean(dhg * xhat, axis=-1, keepdims=True))
        part = jnp.sum(dhv * xhat, axis=0, keepdims=True)

        @pl.when(pl.program_id(0) == 0)
        def _():
            dg_ref[...] = part

        @pl.when(pl.program_id(0) > 0)
        def _():
            dg_ref[...] += part

    return _pcall(body, name=name, out_shape=(jax.ShapeDtypeStruct((t, d), F32), jax.ShapeDtypeStruct((1, d), F32)),
                  grid=(t // ROW_TILE,), in_specs=[_rows(t, d), _vec(d), _rows(t, d), _rows(t, d)],
                  out_specs=(_rows(t, d), _vec(d)), semantics=("arbitrary",))(x, g, dh, dres)


def _swiglu_fwd(h, w_gate, w_up, *, name):
    t, k = h.shape
    f = w_gate.shape[0]
    tm = _tile(t, ROWS_PER_STEP, 16)
    cn = _tile(f, COLS_PER_DOT, LANE)

    def body(h_ref, wg_ref, wu_ref, ff_ref, gate_ref, up_ref):
        hv = h_ref[...]
        for j in range(f // cn):
            cols = pl.ds(j * cn, cn)
            gv = _dot(hv, wg_ref[cols, :], NT)
            uv = _dot(hv, wu_ref[cols, :], NT)
            gate_ref[:, cols] = gv.astype(BF16)
            up_ref[:, cols] = uv.astype(BF16)
            ff_ref[:, cols] = (gv * _sigmoid(gv) * uv).astype(BF16)

    row = lambda width: pl.BlockSpec((tm, width), lambda i: (i, 0))
    out = jax.ShapeDtypeStruct((t, f), BF16)
    return _pcall(body, name=name, out_shape=(out,) * 3, grid=(t // tm,), in_specs=[row(k), _resident(w_gate), _resident(w_up)],
                  out_specs=(row(f),) * 3, semantics=("parallel",), vmem_limit=VMEM_LIMIT)(h, w_gate, w_up)


def _swiglu_bwd(dx2, w_down, gate, up, *, name, after=None):
    t, d = dx2.shape
    f = w_down.shape[0]
    tm = _tile(t, ROWS_PER_STEP, 16)
    cn = _tile(f, COLS_PER_DOT, LANE)

    def body(dx_ref, w_ref, gate_ref, up_ref, dgate_ref, dup_ref):
        dxv = dx_ref[...].astype(BF16)
        for j in range(f // cn):
            cols = pl.ds(j * cn, cn)
            dffv = _dot(dxv, w_ref[cols, :], NT)
            gv = gate_ref[:, cols].astype(F32)
            sig = _sigmoid(gv)
            dgate_ref[:, cols] = (dffv * up_ref[:, cols].astype(F32) * sig * (1.0 + gv * (1.0 - sig))).astype(BF16)
            dup_ref[:, cols] = (dffv * gv * sig).astype(BF16)

    row = lambda width: pl.BlockSpec((tm, width), lambda i: (i, 0))
    out = jax.ShapeDtypeStruct((t, f), BF16)
    return _pcall(body, name=name, out_shape=(out, out), grid=(t // tm,), in_specs=[row(d), _resident(w_down), row(f), row(f)],
                  out_specs=(row(f), row(f)), semantics=("parallel",), vmem_limit=VMEM_LIMIT, after=after)(dx2, w_down, gate, up)


def _ple_fwd(x2, pgl, pp, *, name):
    t, d = x2.shape

    def body(x_ref, pgl_ref, pp_ref, o_ref):
        o_ref[...] = x_ref[...] + _sigmoid(pgl_ref[...]) * pp_ref[...]

    return _pcall(body, name=name, out_shape=jax.ShapeDtypeStruct((t, d), F32), grid=(t // ROW_TILE,),
                  in_specs=[_rows(t, d)] * 3, out_specs=_rows(t, d), semantics=("parallel",))(x2, pgl, pp)


def _ple_bwd(dx3, pgl, pp, *, name, after=None):
    t, d = dx3.shape

    def body(dx_ref, pgl_ref, pp_ref, dpgl_ref, dpp_ref):
        dxv = dx_ref[...]
        sig = _sigmoid(pgl_ref[...])
        dpp_ref[...] = (dxv * sig).astype(BF16)
        dpgl_ref[...] = (dxv * pp_ref[...] * sig * (1.0 - sig)).astype(BF16)

    return _pcall(body, name=name, out_shape=(jax.ShapeDtypeStruct((t, d), BF16),) * 2, grid=(t // ROW_TILE,),
                  in_specs=[_rows(t, d)] * 3, out_specs=(_rows(t, d),) * 2, semantics=("parallel",), after=after)(dx3, pgl, pp)


def _loss_head(x3, g, target, *, name):
    t, d = x3.shape

    def body(x_ref, g_ref, t_ref, dx_ref, dg_ref, loss_ref):
        xv = x_ref[...]
        r = lax.rsqrt(jnp.mean(xv * xv, axis=-1, keepdims=True) + EPS)
        xhat = xv * r
        gv = g_ref[...]
        err = xhat * gv - t_ref[...]
        row_loss = jnp.sum(err * err, axis=-1, keepdims=True) * (0.5 / d)
        lpart = jnp.broadcast_to(jnp.sum(row_loss, axis=0, keepdims=True), (1, LANE))
        dy = err * (1.0 / d)
        dyg = dy * gv
        dx_ref[...] = r * (dyg - xhat * jnp.mean(dyg * xhat, axis=-1, keepdims=True))
        gpart = jnp.sum(dy * xhat, axis=0, keepdims=True)

        @pl.when(pl.program_id(0) == 0)
        def _():
            dg_ref[...] = gpart
            loss_ref[...] = lpart

        @pl.when(pl.program_id(0) > 0)
        def _():
            dg_ref[...] += gpart
            loss_ref[...] += lpart

    return _pcall(body, name=name,
                  out_shape=(jax.ShapeDtypeStruct((t, d), F32), jax.ShapeDtypeStruct((1, d), F32), jax.ShapeDtypeStruct((1, LANE), F32)),
                  grid=(t // ROW_TILE,), in_specs=[_rows(t, d), _vec(d), _rows(t, d)],
                  out_specs=(_rows(t, d), _vec(d), _vec(LANE)), semantics=("arbitrary",))(x3, g, target)


def _shift_down(x, d):
    if d == 0:
        return x
    row = lax.broadcasted_iota(jnp.int32, x.shape, 0)
    return jnp.where(row >= d, pltpu.roll(x, d, 0), 0.0)


def _shift_up(x, d):
    if d == 0:
        return x
    t = x.shape[0]
    row = lax.broadcasted_iota(jnp.int32, x.shape, 0)
    return jnp.where(row < t - d, pltpu.roll(x, t - d, 0), 0.0)


def _colsum(x):
    return jnp.sum(x, axis=0, keepdims=True)


def _col(t, idx_fn):
    return pl.BlockSpec((t, LANE), idx_fn)


def _conv_fwd(x, w_ref, taps):
    acc = None
    for j in range(taps):
        term = w_ref[pl.ds(j, 1), :] * _shift_down(x, taps - 1 - j)
        acc = term if acc is None else acc + term
    return acc


def _conv_bwd(x, dy, w_ref, dw_ref, taps):
    dx = None
    for j in range(taps):
        term = w_ref[pl.ds(j, 1), :] * _shift_up(dy, taps - 1 - j)
        dx = term if dx is None else dx + term
        dw_ref[pl.ds(j, 1), :] = _colsum(dy * _shift_down(x, taps - 1 - j))
    return dx


def _qkv_prep_fwd(proj, conv_w, *, name):
    t = proj.shape[0]
    scale = HEAD_DIM ** -0.5

    def body(x_ref, w_ref, o_ref):
        j = pl.program_id(0)
        c = _conv_fwd(x_ref[...], w_ref, QKV_TAPS)
        s = c * _sigmoid(c)
        r = lax.rsqrt(jnp.sum(s * s, axis=-1, keepdims=True) + EPS)
        f = jnp.where(j < 2 * HEADS, r, 1.0) * jnp.where(j < HEADS, scale, 1.0)
        o_ref[0] = s * f

    return _pcall(body, name=name, out_shape=jax.ShapeDtypeStruct((3 * HEADS, t, LANE), F32), grid=(3 * HEADS,),
                  in_specs=[_col(t, lambda j: (0, j)), pl.BlockSpec((QKV_TAPS, LANE), lambda j: (0, j))],
                  out_specs=pl.BlockSpec((1, t, LANE), lambda j: (j, 0, 0)), semantics=("parallel",),
                  vmem_limit=VMEM_LIMIT)(proj, conv_w)


def _qkv_prep_bwd(proj, conv_w, dqkv, *, name):
    t = proj.shape[0]
    scale = HEAD_DIM ** -0.5

    def body(x_ref, w_ref, d_ref, dx_ref, dw_ref):
        j = pl.program_id(0)
        xv = x_ref[...]
        c = _conv_fwd(xv, w_ref, QKV_TAPS)
        sig = _sigmoid(c)
        s = c * sig
        r = lax.rsqrt(jnp.sum(s * s, axis=-1, keepdims=True) + EPS)
        n0 = s * r
        dv = d_ref[0]
        dn0 = dv * jnp.where(j < HEADS, scale, 1.0)
        ds_norm = r * (dn0 - n0 * jnp.sum(dn0 * n0, axis=-1, keepdims=True))
        ds = jnp.where(j < 2 * HEADS, ds_norm, dv)
        dc = ds * sig * (1.0 + c * (1.0 - sig))
        dx_ref[...] = _conv_bwd(xv, dc, w_ref, dw_ref, QKV_TAPS).astype(BF16)

    return _pcall(body, name=name,
                  out_shape=(jax.ShapeDtypeStruct((t, 3 * A_DIM), BF16), jax.ShapeDtypeStruct((QKV_TAPS, 3 * A_DIM), F32)),
                  grid=(3 * HEADS,),
                  in_specs=[_col(t, lambda j: (0, j)), pl.BlockSpec((QKV_TAPS, LANE), lambda j: (0, j)),
                            pl.BlockSpec((1, t, LANE), lambda j: (j, 0, 0))],
                  out_specs=(_col(t, lambda j: (0, j)), pl.BlockSpec((QKV_TAPS, LANE), lambda j: (0, j))),
                  semantics=("parallel",), vmem_limit=VMEM_LIMIT)(proj, conv_w, dqkv)


def _lane_pick(x, lane_idx, lane):
    return jnp.broadcast_to(jnp.sum(jnp.where(lane == lane_idx, x, 0.0), axis=-1, keepdims=True), x.shape)


def _gates_fwd(proj, alog, dtb, *, name):
    t = proj.shape[0]

    def body(x_ref, alog_ref, dtb_ref, g_ref, b_ref):
        xv = x_ref[...]
        lane = lax.broadcasted_iota(jnp.int32, xv.shape, 1)
        gall = -jnp.exp(alog_ref[...]) * _softplus(xv + dtb_ref[...])
        ball = _sigmoid(xv)
        for h in range(HEADS):
            g_ref[h] = _lane_pick(gall, h, lane)
            b_ref[h] = _lane_pick(ball, HEADS + h, lane)

    out = jax.ShapeDtypeStruct((HEADS, t, LANE), F32)
    whole = pl.BlockSpec((HEADS, t, LANE), lambda i: (0, 0, 0))
    return _pcall(body, name=name, out_shape=(out, out), grid=(1,),
                  in_specs=[_col(t, lambda i: (0, AB_COL // LANE)), _vec(LANE), _vec(LANE)], out_specs=(whole, whole),
                  semantics=("arbitrary",), vmem_limit=VMEM_LIMIT)(proj, alog, dtb)


def _gates_bwd(proj, alog, dtb, dg, dbeta, *, name):
    t = proj.shape[0]

    def body(x_ref, alog_ref, dtb_ref, dg_ref, db_ref, dab_ref, dalog_ref, ddtb_ref):
        xv = x_ref[...]
        lane = lax.broadcasted_iota(jnp.int32, xv.shape, 1)
        lane1 = lax.broadcasted_iota(jnp.int32, (1, LANE), 1)
        z = xv + dtb_ref[...]
        nea = -jnp.exp(alog_ref[...])
        da_f = nea * _sigmoid(z)
        g_f = nea * _softplus(z)
        ball = _sigmoid(xv)
        db_f = ball * (1.0 - ball)
        dab = jnp.zeros_like(xv)
        dalog = jnp.zeros((1, LANE), F32)
        for h in range(HEADS):
            dgh = dg_ref[h]
            dab = dab + jnp.where(lane == h, dgh * da_f, 0.0) + jnp.where(lane == HEADS + h, db_ref[h] * db_f, 0.0)
            dalog = dalog + jnp.where(lane1 == h, _colsum(dgh * g_f), 0.0)
        dab_ref[...] = dab.astype(BF16)
        dalog_ref[...] = dalog
        ddtb_ref[...] = jnp.where(lane1 < HEADS, _colsum(dab), 0.0)

    whole = pl.BlockSpec((HEADS, t, LANE), lambda i: (0, 0, 0))
    vec = jax.ShapeDtypeStruct((1, LANE), F32)
    return _pcall(body, name=name, out_shape=(jax.ShapeDtypeStruct((t, LANE), BF16), vec, vec), grid=(1,),
                  in_specs=[_col(t, lambda i: (0, AB_COL // LANE)), _vec(LANE), _vec(LANE), whole, whole],
                  out_specs=(_col(t, lambda i: (0, 0)), _vec(LANE), _vec(LANE)), semantics=("arbitrary",),
                  vmem_limit=VMEM_LIMIT)(proj, alog, dtb, dg, dbeta)


Z_COL = 3 * A_DIM // LANE


def _apost_fwd(o, proj, gn, *, name):
    t = proj.shape[0]

    def body(o_ref, z_ref, gn_ref, y_ref):
        ov = o_ref[0]
        z = z_ref[...]
        r = lax.rsqrt(jnp.mean(ov * ov, axis=-1, keepdims=True) + EPS)
        y_ref[...] = (ov * r * gn_ref[...] * (z * _sigmoid(z))).astype(BF16)

    return _pcall(body, name=name, out_shape=jax.ShapeDtypeStruct((t, A_DIM), BF16), grid=(HEADS,),
                  in_specs=[pl.BlockSpec((1, t, LANE), lambda h: (h, 0, 0)), _col(t, lambda h: (0, Z_COL + h)),
                            pl.BlockSpec((1, LANE), lambda h: (0, 0))],
                  out_specs=_col(t, lambda h: (0, h)), semantics=("parallel",), vmem_limit=VMEM_LIMIT)(o, proj, gn)


def _apost_bwd(o, proj, gn, dmixed, *, name):
    t = proj.shape[0]

    def body(o_ref, z_ref, gn_ref, d_ref, do_ref, dz_ref, dgn_ref):
        ov = o_ref[0]
        z = z_ref[...]
        gnv = gn_ref[...]
        dv = d_ref[...]
        r = lax.rsqrt(jnp.mean(ov * ov, axis=-1, keepdims=True) + EPS)
        ohat = ov * r
        sig = _sigmoid(z)
        dy = dv * (z * sig)
        dz_ref[...] = (dv * ohat * gnv * sig * (1.0 + z * (1.0 - sig))).astype(BF16)
        dyo = dy * gnv
        do_ref[0] = r * (dyo - ohat * jnp.mean(dyo * ohat, axis=-1, keepdims=True))
        part = _colsum(dy * ohat)

        @pl.when(pl.program_id(0) == 0)
        def _():
            dgn_ref[...] = part

        @pl.when(pl.program_id(0) > 0)
        def _():
            dgn_ref[...] += part

    return _pcall(body, name=name,
                  out_shape=(jax.ShapeDtypeStruct((HEADS, t, LANE), F32), jax.ShapeDtypeStruct((t, A_DIM), BF16),
                             jax.ShapeDtypeStruct((1, LANE), F32)),
                  grid=(HEADS,),
                  in_specs=[pl.BlockSpec((1, t, LANE), lambda h: (h, 0, 0)), _col(t, lambda h: (0, Z_COL + h)),
                            pl.BlockSpec((1, LANE), lambda h: (0, 0)), _col(t, lambda h: (0, h))],
                  out_specs=(pl.BlockSpec((1, t, LANE), lambda h: (h, 0, 0)), _col(t, lambda h: (0, h)),
                             pl.BlockSpec((1, LANE), lambda h: (0, 0))),
                  semantics=("arbitrary",), vmem_limit=VMEM_LIMIT)(o, proj, gn, dmixed)


POOL_COL = (AB_COL + LANE) // LANE
CB_COL = POOL_COL + POOL_DIM // LANE
CC_COL = CB_COL + CONV_DIM // LANE
CH_COL = CC_COL + CONV_DIM // LANE
MAX_WIN_LOG2 = 4


def _window_sums(x, shift):
    sums = []
    cur = x
    for k in range(MAX_WIN_LOG2):
        cur = cur + shift(cur, 1 << k)
        sums.append(cur)
    return sums


def _pick_window(sums, win):
    out = sums[-1]
    for k in range(MAX_WIN_LOG2 - 2, -1, -1):
        out = jnp.where(win == float(2 << k), sums[k], out)
    return out


def _pool_counts(shape, win):
    row = lax.broadcasted_iota(jnp.int32, shape, 0).astype(F32)
    return jnp.minimum(row + 1.0, win)


def _pool_fwd(proj, win, wbd, scale, *, name):
    t = proj.shape[0]

    def body(x_ref, win_ref, w_ref, s_ref, y_ref):
        xv = x_ref[...]
        winv = win_ref[...]
        pooled = _pick_window(_window_sums(xv, _shift_down), winv) / _pool_counts(xv.shape, winv) - xv
        y_ref[...] = (_dot(pooled, w_ref[0], NN) * s_ref[...]).astype(BF16)

    nb = POOL_DIM // LANE
    vec = pl.BlockSpec((1, LANE), lambda b: (0, b))
    return _pcall(body, name=name, out_shape=jax.ShapeDtypeStruct((t, POOL_DIM), BF16), grid=(nb,),
                  in_specs=[_col(t, lambda b: (0, POOL_COL + b)), vec, pl.BlockSpec((1, LANE, LANE), lambda b: (b, 0, 0)), vec],
                  out_specs=_col(t, lambda b: (0, b)), semantics=("parallel",), vmem_limit=VMEM_LIMIT)(proj, win, wbd, scale)


def _pool_bwd(proj, win, wbd, scale, dmixed, *, name):
    t = proj.shape[0]

    def body(x_ref, win_ref, w_ref, s_ref, d_ref, dx_ref, dw_ref, ds_ref):
        xv = x_ref[...]
        winv = win_ref[...]
        cnt = _pool_counts(xv.shape, winv)
        pooled = _pick_window(_window_sums(xv, _shift_down), winv) / cnt - xv
        dv = d_ref[...]
        ds_ref[...] = _colsum(dv * _dot(pooled, w_ref[0], NN))
        dy0 = dv * s_ref[...]
        dw_ref[0] = _dot(pooled, dy0, TN)
        dpooled = _dot(dy0, w_ref[0], NT)
        dmean = dpooled / cnt
        dx_ref[...] = (_pick_window(_window_sums(dmean, _shift_up), winv) - dpooled).astype(BF16)

    nb = POOL_DIM // LANE
    vec = pl.BlockSpec((1, LANE), lambda b: (0, b))
    mat = pl.BlockSpec((1, LANE, LANE), lambda b: (b, 0, 0))
    first = A_DIM // LANE
    return _pcall(body, name=name,
                  out_shape=(jax.ShapeDtypeStruct((t, POOL_DIM), BF16), jax.ShapeDtypeStruct((nb, LANE, LANE), F32),
                             jax.ShapeDtypeStruct((1, POOL_DIM), F32)),
                  grid=(nb,),
                  in_specs=[_col(t, lambda b: (0, POOL_COL + b)), vec, mat, vec, _col(t, lambda b: (0, first + b))],
                  out_specs=(_col(t, lambda b: (0, b)), mat, vec), semantics=("parallel",),
                  vmem_limit=VMEM_LIMIT)(proj, win, wbd, scale, dmixed)


def _sconv_fwd(proj, w, *, name):
    t = proj.shape[0]

    def body(cb_ref, cc_ref, ch_ref, w_ref, y_ref):
        y_ref[...] = (cb_ref[...] * _conv_fwd(cc_ref[...] * ch_ref[...], w_ref, CONV_TAPS)).astype(BF16)

    nb = CONV_DIM // LANE
    return _pcall(body, name=name, out_shape=jax.ShapeDtypeStruct((t, CONV_DIM), BF16), grid=(nb,),
                  in_specs=[_col(t, lambda b: (0, CB_COL + b)), _col(t, lambda b: (0, CC_COL + b)),
                            _col(t, lambda b: (0, CH_COL + b)), pl.BlockSpec((CONV_TAPS, LANE), lambda b: (0, b))],
                  out_specs=_col(t, lambda b: (0, b)), semantics=("parallel",), vmem_limit=VMEM_LIMIT)(proj, proj, proj, w)


def _sconv_bwd(proj, w, dmixed, *, name):
    t = proj.shape[0]

    def body(cb_ref, cc_ref, ch_ref, w_ref, d_ref, dcb_ref, dcc_ref, dch_ref, dw_ref):
        cc = cc_ref[...]
        ch = ch_ref[...]
        u = cc * ch
        dv = d_ref[...]
        dcb_ref[...] = (dv * _conv_fwd(u, w_ref, CONV_TAPS)).astype(BF16)
        du = _conv_bwd(u, dv * cb_ref[...], w_ref, dw_ref, CONV_TAPS)
        dcc_ref[...] = (du * ch).astype(BF16)
        dch_ref[...] = (du * cc).astype(BF16)

    nb = CONV_DIM // LANE
    first = (A_DIM + POOL_DIM) // LANE
    act = jax.ShapeDtypeStruct((t, CONV_DIM), BF16)
    wspec = pl.BlockSpec((CONV_TAPS, LANE), lambda b: (0, b))
    ospec = _col(t, lambda b: (0, b))
    return _pcall(body, name=name, out_shape=(act, act, act, jax.ShapeDtypeStruct((CONV_TAPS, CONV_DIM), F32)), grid=(nb,),
                  in_specs=[_col(t, lambda b: (0, CB_COL + b)), _col(t, lambda b: (0, CC_COL + b)),
                            _col(t, lambda b: (0, CH_COL + b)), wspec, _col(t, lambda b: (0, first + b))],
                  out_specs=(ospec, ospec, ospec, wspec), semantics=("parallel",),
                  vmem_limit=VMEM_LIMIT)(proj, proj, proj, w, dmixed)


def _chunk_masks():
    r = lax.broadcasted_iota(jnp.int32, (CHUNK, CHUNK), 0)
    c = lax.broadcasted_iota(jnp.int32, (CHUNK, CHUNK), 1)
    return r >= c, r > c, jnp.where(r == c, 1.0, 0.0).astype(F32)


def _split(a):
    hi = a.astype(BF16)
    return hi, (a - hi.astype(F32)).astype(BF16)


def _dot_split(a, b, dims):
    (ah, al), (bh, bl) = a, b
    return _dot(ah, bh, dims) + _dot(ah, bl, dims) + _dot(al, bh, dims)


def _tri_inv(lows, eye):
    xs = [eye - low for low in lows]
    ps = [_split(low) for low in lows]
    ps = [_split(_dot_split(p, p, NN)) for p in ps]
    for i in range(5):
        xs = [x + _dot_split(_split(x), p, NN) for x, p in zip(xs, ps)]
        if i < 4:
            ps = [_split(_dot_split(p, p, NN)) for p in ps]
    return xs


def _prefix_sum_rows(x):
    for k in range(6):
        x = x + _shift_down(x, 1 << k)
    return x


def _suffix_sum_rows(x):
    for k in range(6):
        x = x + _shift_up(x, 1 << k)
    return x


def _chunk_decay(g, incl):
    gcb = _prefix_sum_rows(g)
    gtot = _colsum(g)
    col = gcb[:, :CHUNK]
    row = gcb.T[:CHUNK, :]
    decay = jnp.exp(jnp.where(incl, col - row, -1e30))
    return gcb, gtot, decay


CHUNKS_PER_STEP = 4


def _heads_of(ref, base, rows):
    return [ref[base + h, rows, :] for h in range(HEADS)]


def _chunk_rows(j):
    return pl.ds(j * CHUNK, CHUNK)


def _deltanet_prep(qkv, g, beta, *, name):
    t = qkv.shape[1]
    n_chunks = t // CHUNK
    per = CHUNKS_PER_STEP
    probs = [(j, h) for j in range(per) for h in range(HEADS)]

    def body(qkv_ref, g_ref, b_ref, u_ref, w_ref, qg_ref, kg_ref, attn_ref, tm_ref):
        incl, strict, eye = _chunk_masks()
        q = [qkv_ref[h, _chunk_rows(j), :] for j, h in probs]
        k = [qkv_ref[HEADS + h, _chunk_rows(j), :] for j, h in probs]
        v = [qkv_ref[2 * HEADS + h, _chunk_rows(j), :] for j, h in probs]
        bv = [b_ref[h, _chunk_rows(j), :] for j, h in probs]
        dec = [_chunk_decay(g_ref[h, _chunk_rows(j), :], incl) for j, h in probs]
        kb = [a * b for a, b in zip(k, bv)]
        low = [jnp.where(strict, _dot(a, b, NT) * d[2], 0.0) for a, b, d in zip(kb, k, dec)]
        tm = _tri_inv(low, eye)
        egc = [jnp.exp(d[0]) for d in dec]
        u = [_dot(m, a * b, NN) for m, a, b in zip(tm, v, bv)]
        w = [_dot(m, a * e, NN) for m, a, e in zip(tm, kb, egc)]
        attn = [_dot(a, b, NT) * d[2] for a, b, d in zip(q, k, dec)]
        for i, (j, h) in enumerate(probs):
            rows = _chunk_rows(j)
            u_ref[h, rows, :] = u[i]
            w_ref[h, rows, :] = w[i].astype(BF16)
            qg_ref[h, rows, :] = (q[i] * egc[i]).astype(BF16)
            kg_ref[h, rows, :] = (k[i] * jnp.exp(dec[i][1] - dec[i][0])).astype(BF16)
            attn_ref[j, h] = attn[i].astype(BF16)
            tm_ref[j, h] = tm[i]

    act = lambda heads: pl.BlockSpec((heads, per * CHUNK, LANE), lambda n: (0, n, 0))
    mat = pl.BlockSpec((per, HEADS, CHUNK, CHUNK), lambda n: (n, 0, 0, 0))
    return _pcall(
        body, name=name,
        out_shape=(jax.ShapeDtypeStruct((HEADS, t, LANE), F32),) + (jax.ShapeDtypeStruct((HEADS, t, LANE), BF16),) * 3
        + (jax.ShapeDtypeStruct((n_chunks, HEADS, CHUNK, CHUNK), BF16), jax.ShapeDtypeStruct((n_chunks, HEADS, CHUNK, CHUNK), F32)),
        grid=(n_chunks // per,), in_specs=[act(3 * HEADS), act(HEADS), act(HEADS)],
        out_specs=(act(HEADS),) * 4 + (mat, mat), semantics=("parallel",), vmem_limit=VMEM_LIMIT)(qkv, g, beta)


SCAN_CHUNKS_PER_STEP = 8


def _deltanet_scan(u, w, qg, kg, attn, g, *, name):
    t = u.shape[1]
    n_chunks = t // CHUNK
    per = SCAN_CHUNKS_PER_STEP

    def body(u_ref, w_ref, qg_ref, kg_ref, attn_ref, g_ref, o_ref, vn_ref, st_ref, s_ref):
        @pl.when(pl.program_id(0) == 0)
        def _():
            s_ref[...] = jnp.zeros_like(s_ref)

        for j in range(per):
            rows = _chunk_rows(j)
            s = [s_ref[h] for h in range(HEADS)]
            vn = [u_ref[h, rows, :] - _dot(w_ref[h, rows, :], s[h], NN) for h in range(HEADS)]
            o = [_dot(qg_ref[h, rows, :], s[h], NN) + _dot(attn_ref[j, h], vn[h], NN) for h in range(HEADS)]
            eg = [jnp.exp(_colsum(g_ref[h, rows, :])) for h in range(HEADS)]
            for h in range(HEADS):
                st_ref[j, h] = s[h]
                s_ref[h] = s[h] * eg[h] + _dot(kg_ref[h, rows, :], vn[h], TN)
                o_ref[h, rows, :] = o[h]
                vn_ref[h, rows, :] = vn[h]

    act = pl.BlockSpec((HEADS, per * CHUNK, LANE), lambda n: (0, n, 0))
    out = jax.ShapeDtypeStruct((HEADS, t, LANE), F32)
    return _pcall(
        body, name=name, out_shape=(out, out, jax.ShapeDtypeStruct((n_chunks, HEADS, LANE, LANE), F32)), grid=(n_chunks // per,),
        in_specs=[act] * 4 + [pl.BlockSpec((per, HEADS, CHUNK, CHUNK), lambda n: (n, 0, 0, 0)), act],
        out_specs=(act, act, pl.BlockSpec((per, HEADS, LANE, LANE), lambda n: (n, 0, 0, 0))),
        scratch_shapes=[pltpu.VMEM((HEADS, LANE, LANE), F32)], semantics=("arbitrary",))(u, w, qg, kg, attn, g)


def _deltanet_bscan(w, qg, kg, attn, g, do, *, name):
    t = w.shape[1]
    n_chunks = t // CHUNK
    per = SCAN_CHUNKS_PER_STEP
    steps = n_chunks // per

    def body(w_ref, qg_ref, kg_ref, attn_ref, g_ref, do_ref, dvn_ref, dsn_ref, ds_ref):
        @pl.when(pl.program_id(0) == 0)
        def _():
            ds_ref[...] = jnp.zeros_like(ds_ref)

        for j in reversed(range(per)):
            rows = _chunk_rows(j)
            dsn = [ds_ref[h] for h in range(HEADS)]
            dov = [do_ref[h, rows, :] for h in range(HEADS)]
            dvn = [_dot(attn_ref[j, h], dov[h], TN) + _dot(kg_ref[h, rows, :], dsn[h], NN) for h in range(HEADS)]
            eg = [jnp.exp(_colsum(g_ref[h, rows, :])) for h in range(HEADS)]
            for h in range(HEADS):
                dsn_ref[j, h] = dsn[h]
                ds_ref[h] = _dot(qg_ref[h, rows, :], dov[h], TN) + eg[h] * dsn[h] - _dot(w_ref[h, rows, :], dvn[h], TN)
                dvn_ref[h, rows, :] = dvn[h]

    act = pl.BlockSpec((HEADS, per * CHUNK, LANE), lambda n: (0, steps - 1 - n, 0))
    return _pcall(
        body, name=name,
        out_shape=(jax.ShapeDtypeStruct((HEADS, t, LANE), F32), jax.ShapeDtypeStruct((n_chunks, HEADS, LANE, LANE), F32)),
        grid=(steps,),
        in_specs=[act] * 3 + [pl.BlockSpec((per, HEADS, CHUNK, CHUNK), lambda n: (steps - 1 - n, 0, 0, 0)), act, act],
        out_specs=(act, pl.BlockSpec((per, HEADS, LANE, LANE), lambda n: (steps - 1 - n, 0, 0, 0))),
        scratch_shapes=[pltpu.VMEM((HEADS, LANE, LANE), F32)], semantics=("arbitrary",))(w, qg, kg, attn, g, do)


def _sum_all(x):
    return jnp.sum(jnp.sum(x, axis=1, keepdims=True), axis=0, keepdims=True)


def _rowsum(x):
    return jnp.sum(x, axis=1, keepdims=True)


def _deltanet_post(qkv, g, beta, tmats, states, dstates, do, dvn, vn, *, name):
    t = qkv.shape[1]
    n_chunks = t // CHUNK
    per = CHUNKS_PER_STEP
    probs = [(j, h) for j in range(per) for h in range(HEADS)]

    def body(qkv_ref, g_ref, b_ref, tm_ref, st_ref, dsn_ref, do_ref, dvn_ref, vn_ref, dqkv_ref, dg_ref, db_ref):
        incl, strict, _ = _chunk_masks()
        ones = jnp.ones((CHUNK, LANE), BF16)
        last_row = lax.broadcasted_iota(jnp.int32, (CHUNK, LANE), 0) == CHUNK - 1
        z = lambda f, *cols: [f(*a) for a in zip(*cols)]
        q = [qkv_ref[h, _chunk_rows(j), :] for j, h in probs]
        k = [qkv_ref[HEADS + h, _chunk_rows(j), :] for j, h in probs]
        v = [qkv_ref[2 * HEADS + h, _chunk_rows(j), :] for j, h in probs]
        bv = [b_ref[h, _chunk_rows(j), :] for j, h in probs]
        dov = [do_ref[h, _chunk_rows(j), :] for j, h in probs]
        dvn_ = [dvn_ref[h, _chunk_rows(j), :] for j, h in probs]
        vn_ = [vn_ref[h, _chunk_rows(j), :] for j, h in probs]
        tm = [tm_ref[j, h] for j, h in probs]
        s = [st_ref[j, h] for j, h in probs]
        dsn = [dsn_ref[j, h] for j, h in probs]
        dec = [_chunk_decay(g_ref[h, _chunk_rows(j), :], incl) for j, h in probs]
        decay = [d[2] for d in dec]
        egc = [jnp.exp(d[0]) for d in dec]
        ekg = [jnp.exp(d[1] - d[0]) for d in dec]
        kb = z(lambda a, b: a * b, k, bv)
        vb = z(lambda a, b: a * b, v, bv)
        kbg = z(lambda a, b: a * b, kb, egc)
        qg = z(lambda a, b: a * b, q, egc)
        kg = z(lambda a, b: a * b, k, ekg)
        kk = z(lambda a, b: _dot(a, b, NT), kb, k)
        qk = z(lambda a, b: _dot(a, b, NT), q, k)
        dattn = z(lambda a, b: jnp.where(incl, _dot(a, b, NT), 0.0), dov, vn_)
        dqg = z(lambda a, b: _dot(a, b, NT), dov, s)
        dkg = z(lambda a, b: _dot(a, b, NT), vn_, dsn)
        dglast = z(lambda a, b, c, d, e: _sum_all(a * b) * jnp.exp(e[1]) + _sum_all(c * d), s, dsn, dkg, kg, dec)
        dw = z(lambda a, b: -_dot(a, b, NT), dvn_, s)
        dtm = z(lambda a, b, c, d: _dot(a, b, NT) + _dot(c, d, NT), dvn_, vb, dw, kbg)
        dvb = z(lambda a, b: _dot(a, b, TN), tm, dvn_)
        dkbg = z(lambda a, b: _dot(a, b, TN), tm, dw)
        dlow = z(lambda a, b: jnp.where(strict, -_dot(_dot(a, b, TN), a, NT), 0.0), tm, dtm)
        dkk = z(lambda a, b: a * b, dlow, decay)
        dqk = z(lambda a, b: a * b, dattn, decay)
        dkb = z(lambda a, b, c, d: _dot(a, b, NN) + c * d, dkk, k, dkbg, egc)
        dk = z(lambda a, b, c, d, e, f, g_, h_: _dot(a, b, TN) + _dot(c, d, TN) + e * f + g_ * h_, dkk, kb, dqk, q, dkg, ekg, dkb, bv)
        dq = z(lambda a, b, c, d: _dot(a, b, NN) + c * d, dqk, k, dqg, egc)
        m = z(lambda a, b, c, d, e: (a * b + c * d) * e, dlow, kk, dattn, qk, decay)
        mcol = [_dot(mh, ones, TN) + _dot(ml, ones, TN) for mh, ml in (_split(a) for a in m)]
        for i, (j, h) in enumerate(probs):
            rows = _chunk_rows(j)
            dqkv_ref[h, rows, :] = dq[i]
            dqkv_ref[HEADS + h, rows, :] = dk[i]
            dqkv_ref[2 * HEADS + h, rows, :] = dvb[i] * bv[i]
            db_ref[h, rows, :] = jnp.broadcast_to(_rowsum(dkb[i] * k[i] + dvb[i] * v[i]), (CHUNK, LANE))
            dgc = (_rowsum(dqg[i] * qg[i] + dkbg[i] * kbg[i] - dkg[i] * kg[i]) + _rowsum(m[i]) - mcol[i]
                   + jnp.where(last_row, dglast[i], 0.0))
            dg_ref[h, rows, :] = _suffix_sum_rows(dgc)

    act = lambda heads: pl.BlockSpec((heads, per * CHUNK, LANE), lambda n: (0, n, 0))
    mat = lambda d: pl.BlockSpec((per, HEADS, d, d), lambda n: (n, 0, 0, 0))
    out = jax.ShapeDtypeStruct((HEADS, t, LANE), F32)
    return _pcall(
        body, name=name, out_shape=(jax.ShapeDtypeStruct((3 * HEADS, t, LANE), F32), out, out), grid=(n_chunks // per,),
        in_specs=[act(3 * HEADS), act(HEADS), act(HEADS), mat(CHUNK), mat(LANE), mat(LANE), act(HEADS), act(HEADS), act(HEADS)],
        out_specs=(act(3 * HEADS), act(HEADS), act(HEADS)), semantics=("parallel",),
        vmem_limit=VMEM_LIMIT)(qkv, g, beta, tmats, states, dstates, do, dvn, vn)


ANY = pl.BlockSpec(memory_space=pl.ANY)
PEERS = N_DEV - 1


def _all_gather(arrays, *, name):
    n = len(arrays)

    def body(*refs):
        ins, outs = refs[:n], refs[n:2 * n]
        send_sems, recv_sems, local_sems = refs[2 * n:]
        x, y, c = lax.axis_index("x"), lax.axis_index("y"), lax.axis_index("c")
        me, sibling = (x, y, c), (x, y, 1 - c)
        chips = [(1 - x, y), (x, 1 - y), (1 - x, 1 - y)]

        def copy(a, k, block, to, src=None):
            dst = outs[a].at[4 * block[0] + 2 * block[1] + block[2]]
            return pltpu.make_async_remote_copy(src_ref=dst if src is None else src, dst_ref=dst, send_sem=send_sems.at[a * PEERS + k],
                                                recv_sem=recv_sems.at[a * PEERS + k], device_id=to, device_id_type=MESH)

        local = [pltpu.make_async_copy(ins[a], outs[a].at[4 * x + 2 * y + c], local_sems.at[a]) for a in range(n)]
        for cp in local:
            cp.start()
        first = []
        for a in range(n):
            first += [copy(a, 1 + j, me, (*chip, c), src=ins[a]) for j, chip in enumerate(chips)]
            first.append(copy(a, 0, me, sibling, src=ins[a]))
        for cp in first:
            cp.start()
        passed = []
        for a in range(n):
            for j, chip in enumerate(chips):
                copy(a, 1 + j, (*chip, c), me).wait_recv()
                fwd = copy(a, 4 + j, (*chip, c), sibling)
                fwd.start()
                passed.append(fwd)
        for a in range(n):
            copy(a, 0, sibling, me).wait_recv()
            for j, chip in enumerate(chips):
                copy(a, 4 + j, (*chip, 1 - c), me).wait_recv()
        for cp in first + passed:
            cp.wait_send()
        for cp in local:
            cp.wait()

    return _pcall(body, name=name, out_shape=tuple(jax.ShapeDtypeStruct((N_DEV,) + a.shape, a.dtype) for a in arrays),
                  in_specs=[ANY] * n, out_specs=(ANY,) * n,
                  scratch_shapes=[pltpu.SemaphoreType.DMA((n * PEERS,)), pltpu.SemaphoreType.DMA((n * PEERS,)),
                                  pltpu.SemaphoreType.DMA((n,))])(*arrays)


CHIPS = 4


def _pair_exchange(arrays, *, name):
    n = len(arrays)

    def body(*refs):
        ins, outs = refs[:n], refs[n:2 * n]
        send_sems, recv_sems = refs[2 * n:]
        x, y, c = lax.axis_index("x"), lax.axis_index("y"), lax.axis_index("c")
        copies = []
        for a in range(n):
            for q in range(CHIPS):
                cp = pltpu.make_async_remote_copy(src_ref=ins[a].at[2 * q + 1 - c], dst_ref=outs[a].at[q],
                                                  send_sem=send_sems.at[a * CHIPS + q], recv_sem=recv_sems.at[a * CHIPS + q],
                                                  device_id=(x, y, 1 - c), device_id_type=MESH)
                cp.start()
                copies.append(cp)
        for cp in copies:
            cp.wait()

    return _pcall(body, name=name, out_shape=tuple(jax.ShapeDtypeStruct((CHIPS,) + a.shape[1:], a.dtype) for a in arrays),
                  in_specs=[ANY] * n, out_specs=(ANY,) * n,
                  scratch_shapes=[pltpu.SemaphoreType.DMA((n * CHIPS,)), pltpu.SemaphoreType.DMA((n * CHIPS,))])(*arrays)


def _pair_add(blocks, theirs, *, name):
    _, r, c_ = blocks.shape
    tr = _tile(r, 512, 16)

    def body(mine_ref, theirs_ref, o_ref):
        core = lax.axis_index("c")
        own = jnp.where(core == 0, mine_ref[0, 0].astype(F32), mine_ref[0, 1].astype(F32))
        o_ref[0] = (own + theirs_ref[0].astype(F32)).astype(o_ref.dtype)

    spec = pl.BlockSpec((1, tr, c_), lambda q, i: (q, i, 0))
    return _pcall(body, name=name, out_shape=jax.ShapeDtypeStruct(theirs.shape, theirs.dtype), grid=(CHIPS, r // tr),
                  in_specs=[pl.BlockSpec((1, 2, tr, c_), lambda q, i: (q, 0, i, 0)), spec], out_specs=spec,
                  semantics=("parallel", "parallel"), vmem_limit=VMEM_LIMIT)(blocks.reshape(CHIPS, 2, r, c_), theirs)


HBM = pl.BlockSpec(memory_space=pltpu.HBM)
SEM = pl.BlockSpec(memory_space=pltpu.SEMAPHORE)
EFFECT = pltpu.SideEffectType.DATAFLOW_SIDE_EFFECTING


GATHER, CHIP_GATHER, CHIP_SCATTER = "gather", "chip_gather", "chip_scatter"
PEERS_OF = {GATHER: N_DEV - 1, CHIP_GATHER: CHIPS - 1, CHIP_SCATTER: CHIPS - 1}


def _direct_copies(srcs, lands, send_sems, recv_sems, local_sems, kind):
    x, y, c = lax.axis_index("x"), lax.axis_index("y"), lax.axis_index("c")
    peers = PEERS_OF[kind]
    mine = 2 * x + y if kind == CHIP_SCATTER else 4 * x + 2 * y + c
    copies = []
    for a, (src, land) in enumerate(zip(srcs, lands)):
        for k in range(1, peers + 1):
            bits = k if kind == GATHER else 2 * k
            px = 1 - x if bits & 4 else x
            py = 1 - y if bits & 2 else y
            pc = 1 - c if bits & 1 else c
            copies.append(pltpu.make_async_remote_copy(
                src_ref=src.at[2 * px + py] if kind == CHIP_SCATTER else src, dst_ref=land.at[mine],
                send_sem=send_sems.at[a * peers + k - 1], recv_sem=recv_sems.at[a * peers + k - 1],
                device_id=(px, py, pc), device_id_type=MESH))
    for a, (src, land) in enumerate(zip(srcs, lands)):
        copies.append(pltpu.make_async_copy(src.at[mine] if kind == CHIP_SCATTER else src, land.at[mine], local_sems.at[a]))
    return copies


def _pair_swap(arrays, *, name):
    n = len(arrays)

    def body(*refs):
        mine, zones = refs[:n], refs[n:2 * n]
        send_sems, recv_sems = refs[2 * n:]
        x, y, c = lax.axis_index("x"), lax.axis_index("y"), lax.axis_index("c")
        copies = []
        for a in range(n):
            for q in range(CHIPS):
                copies.append(pltpu.make_async_remote_copy(
                    src_ref=mine[a].at[2 * q + c], dst_ref=zones[a].at[2 * q + c], send_sem=send_sems.at[a * CHIPS + q],
                    recv_sem=recv_sems.at[a * CHIPS + q], device_id=(x, y, 1 - c), device_id_type=MESH))
        for cp in copies:
            cp.start()
        for cp in copies:
            cp.wait()

    return _pcall(body, name=name, out_shape=tuple(jax.ShapeDtypeStruct(a.shape, a.dtype) for a in arrays),
                  in_specs=[ANY] * n, out_specs=(ANY,) * n, input_output_aliases={i: i for i in range(n)},
                  scratch_shapes=[pltpu.SemaphoreType.DMA((n * CHIPS,)), pltpu.SemaphoreType.DMA((n * CHIPS,))])(*arrays)


def _exchange_start(groups, kind, *, name, after=None):
    srcs = [s for group in groups for s in group]
    n = len(srcs)
    sizes = [len(group) for group in groups]
    starts = [sum(sizes[:g]) for g in range(len(groups))]
    land_shapes = [s.shape if kind == CHIP_SCATTER else (N_DEV,) + s.shape for s in srcs]
    peers = PEERS_OF[kind]
    extra = [] if after is None else [after]

    def body(*refs):
        srcs_, lands = refs[:n], refs[n:2 * n]
        token = refs[-1]
        sem_refs = refs[2 * n + len(extra):]
        for g, (at, size) in enumerate(zip(starts, sizes)):
            send_sems, recv_sems, local_sems = sem_refs[3 * g:3 * g + 3]
            for cp in _direct_copies(srcs_[at:at + size], lands[at:at + size], send_sems, recv_sems, local_sems, kind):
                cp.start()
        token[...] = jnp.zeros_like(token)

    sems = tuple(t for size in sizes for t in (pltpu.SemaphoreType.DMA((size * peers,)), pltpu.SemaphoreType.DMA((size * peers,)),
                                               pltpu.SemaphoreType.DMA((size,))))
    thru = tuple(pltpu.HBM(s.shape, s.dtype) for s in srcs) + tuple(pltpu.HBM(shp, s.dtype) for shp, s in zip(land_shapes, srcs))
    ins = [pltpu.with_memory_space_constraint(s, pltpu.HBM) for s in srcs]
    ins += [pltpu.with_memory_space_constraint(lax.empty(shp, s.dtype), pltpu.HBM) for shp, s in zip(land_shapes, srcs)]
    out = pl.pallas_call(
        body, name=name, out_shape=sems + thru + (jax.ShapeDtypeStruct((SUBLANE, LANE), F32),),
        in_specs=[HBM] * (2 * n) + [ANY] * len(extra),
        out_specs=(SEM,) * len(sems) + (HBM,) * (2 * n) + (pl.BlockSpec(memory_space=pltpu.VMEM),),
        input_output_aliases={i: len(sems) + i for i in range(2 * n)},
        compiler_params=pltpu.CompilerParams(has_side_effects=EFFECT))(*ins, *extra)
    arrays = out[len(sems):-1]
    started = [tuple(out[3 * g:3 * g + 3]) + tuple(arrays[at:at + size]) + tuple(arrays[n + at:n + at + size])
               for g, (at, size) in enumerate(zip(starts, sizes))]
    return started, out[-1]


def _exchange_wait(started, after, kind, *, name):
    n = (len(started) - 3) // 2
    sems, arrays = started[:3], started[3:]

    def body(*refs):
        srcs_, lands = refs[:n], refs[n:2 * n]
        send_sems, recv_sems, local_sems = refs[2 * n:2 * n + 3]
        for cp in _direct_copies(srcs_, lands, send_sems, recv_sems, local_sems, kind):
            cp.wait()

    out = pl.pallas_call(
        body, name=name, out_shape=tuple(pltpu.HBM(a.shape, a.dtype) for a in arrays),
        in_specs=[HBM] * (2 * n) + [SEM] * 3 + [ANY], out_specs=(HBM,) * (2 * n),
        input_output_aliases={i: i for i in range(2 * n)},
        compiler_params=pltpu.CompilerParams(has_side_effects=EFFECT))(*arrays, *sems, after)
    return out[n:]


def _adamw_reduce(w, parts, m, v, *, name, after=None):
    layers, r, c = w.shape
    assert len(parts) == layers
    senders = parts[0].shape[0]
    tr = _tile(r, 512, 16)
    tiles = r // tr
    bc1 = 1.0 - ADAM_B1 ** ADAM_STEP
    bc2 = 1.0 - ADAM_B2 ** ADAM_STEP

    def body(w_ref, *rest):
        p_refs = rest[:layers]
        m_ref, v_ref, g_ref, d_ref, nm_ref, nv_ref = rest[layers:]

        def update(p_ref):
            g = p_ref[0, :, pl.ds(0, c)].astype(F32)
            for s in range(1, senders):
                g = g + p_ref[s, :, pl.ds(0, c)].astype(F32)
            nm = ADAM_B1 * m_ref[0] + (1.0 - ADAM_B1) * g
            nv = ADAM_B2 * v_ref[0] + (1.0 - ADAM_B2) * (g * g)
            g_ref[0] = g
            nm_ref[0] = nm
            nv_ref[0] = nv
            d_ref[0] = -ADAM_LR * ((nm / bc1) / (jnp.sqrt(nv / bc2) + ADAM_EPS) + ADAM_WD * w_ref[0])

        for layer in range(layers):
            pl.when(pl.program_id(0) == layer)(functools.partial(update, p_refs[layer]))

    def part_spec(layer, shape):
        rest = 0 if layer > 0 else tiles - 1
        return pl.BlockSpec((senders, tr, shape[2]), lambda l, i: (0, jnp.where(l == layer, i, rest), 0))

    spec = pl.BlockSpec((1, tr, c), lambda l, i: (l, i, 0))
    out = jax.ShapeDtypeStruct((layers, r, c), F32)
    return _pcall(body, name=name, out_shape=(out,) * 4, grid=(layers, tiles),
                  in_specs=[spec] + [part_spec(layer, p.shape) for layer, p in enumerate(parts)] + [spec, spec],
                  out_specs=(spec,) * 4, semantics=("arbitrary", "arbitrary"), vmem_limit=VMEM_LIMIT, after=after)(w, *parts, m, v)


def _pool_windows():
    return jnp.repeat(jnp.asarray(POOL_WINDOWS, F32), POOL_DIM // len(POOL_WINDOWS))[None, :]


def _block_diag_pairs(pool_w):
    z = jnp.zeros_like(pool_w[0])
    return jnp.stack([jnp.block([[pool_w[2 * b], z], [z, pool_w[2 * b + 1]]]) for b in range(2)])


def _pad_lanes(vec):
    return jnp.zeros((1, LANE), F32).at[0, :vec.shape[0]].set(vec)


FF_SHARD = D_FF // N_DEV
FF_BLOCK = 384
D_FF_PAD = N_DEV * FF_BLOCK


def _layer_fwd(x, p_i, wt, fetch):
    wt = {**wt, **fetch(0, x)}
    h1 = _rmsnorm_fwd(x, wt["norm1_g"], name="rmsnorm_fwd")
    proj = _matmul(h1, wt["w_in"], "nn", name="mm_in")
    wt.update(fetch(1, proj))
    qkv = _qkv_prep_fwd(proj, wt["conv_qkv"], name="qkv_prep_fwd")
    g, beta = _gates_fwd(proj, wt["a_log"], wt["dt_bias"], name="gates_fwd")
    u, w, qg, kg, attn, tmats = _deltanet_prep(qkv, g, beta, name="deltanet_prep")
    o, vn, states = _deltanet_scan(u, w, qg, kg, attn, g, name="deltanet_scan")
    o_a = _apost_fwd(o, proj, wt["onorm_g"], name="apost_fwd")
    o_b = _pool_fwd(proj, wt["pool_win"], wt["pool_wbd"], wt["pool_scale"], name="pool_fwd")
    o_c = _sconv_fwd(proj, wt["sconv_w"], name="sconv_fwd")
    mixed = jnp.concatenate([o_a, o_b, o_c], axis=1)
    x1 = _matmul(mixed, wt["w_out"], "nn", res=x, name="mm_out")
    h2 = _rmsnorm_fwd(x1, wt["norm2_g"], name="rmsnorm_fwd")
    wt.update(fetch(2, h2))
    ff, gate, up = _swiglu_fwd(h2, wt["w_gate"], wt["w_up"], name="swiglu_fwd")
    wt.update(fetch(3, ff))
    x2 = _matmul(ff, wt["w_down"], "nn", res=x1, name="mm_down")
    wt.update(fetch(4, x2))
    pgl = _matmul(x2, wt["ple_gate"], "nn", name="mm_pleg")
    pp = _matmul(p_i, wt["ple_proj"], "nn", b_blocked=True, name="mm_plep")
    x3 = _ple_fwd(x2, pgl, pp, name="ple_fwd")
    saved = dict(x=x, h1=h1, proj=proj, qkv=qkv, g=g, beta=beta, o=o, states=states, tmats=tmats, mixed=mixed, x1=x1, h2=h2,
                 gate=gate, up=up, ff=ff, x2=x2, pgl=pgl, pp=pp, p=p_i, w=w, qg=qg, kg=kg, attn=attn, vn=vn, wt=wt)
    return x3, saved


def _col_blocks(g):
    a = g.shape[0]
    return jnp.transpose(g.reshape(a, N_DEV, -1), (1, 0, 2))


def _cols_joined(blocks):
    return jnp.transpose(blocks, (1, 0, 2)).reshape(blocks.shape[1], -1)


def _layer_bwd(dx3, sv, emit, after=None):
    gr, big = {}, {}
    wt = sv["wt"]
    rows = D_MODEL // N_DEV
    dpgl, dpp = _ple_bwd(dx3, sv["pgl"], sv["pp"], name="ple_bwd", after=after)
    big["ple_proj"] = _matmul(sv["p"], dpp, "tn", out_blocked=(N_DEV, rows), out_dtype=BF16, name="mm_dplep")
    big["ple_gate"] = _matmul(sv["x2"], dpgl, "tn", out_dtype=BF16, name="mm_dpleg").reshape(N_DEV, rows, D_MODEL)
    dx2 = _matmul(dpgl, wt["ple_gate"], "nt", res=dx3, name="mm_dx2")
    big["w_down"] = _matmul(sv["ff"], dx2, "tn", out_dtype=BF16, name="mm_ddown").reshape(N_DEV, FF_BLOCK, D_MODEL)
    dgate, dup = _swiglu_bwd(dx2, wt["w_down"], sv["gate"], sv["up"], name="swiglu_bwd", after=emit(0, big))
    big["w_gate"] = _matmul(dgate, sv["h2"], "tn", out_dtype=BF16, name="mm_dgate").reshape(N_DEV, FF_BLOCK, D_MODEL)
    big["w_up"] = _matmul(dup, sv["h2"], "tn", out_dtype=BF16, name="mm_dup").reshape(N_DEV, FF_BLOCK, D_MODEL)
    dh2 = _matmul(dgate, wt["w_gate"], "nn", name="mm_dh2_gate")
    dh2 = _matmul(dup, wt["w_up"], "nn", res=dh2, name="mm_dh2_up")
    dx1, gr["norm2_g"] = _rmsnorm_bwd(sv["x1"], wt["norm2_g"], dh2, dx2, name="rmsnorm_bwd")
    big["w_out"] = _matmul(sv["mixed"], dx1, "tn", out_dtype=BF16, name="mm_dout").reshape(N_DEV, rows, D_MODEL)
    dmixed = _matmul(dx1, wt["w_out"], "nt", name="mm_dmixed", after=emit(1, big))
    proj = sv["proj"]
    dcb, dcc, dch, dsconv = _sconv_bwd(proj, wt["sconv_w"], dmixed, name="sconv_bwd")
    big["sconv_w"] = _col_blocks(dsconv)
    dhp, dwbd, gr["pool_scale"] = _pool_bwd(proj, wt["pool_win"], wt["pool_wbd"], wt["pool_scale"], dmixed, name="pool_bwd")
    half = LANE // 2
    gr["pool_w"] = jnp.stack([dwbd[0, :half, :half], dwbd[0, half:, half:], dwbd[1, :half, :half], dwbd[1, half:, half:]])
    do, dz, gr["onorm_g"] = _apost_bwd(sv["o"], proj, wt["onorm_g"], dmixed, name="apost_bwd")
    dvn, dstates = _deltanet_bscan(sv["w"], sv["qg"], sv["kg"], sv["attn"], sv["g"], do, name="deltanet_bscan")
    dqkv_h, dg, dbeta = _deltanet_post(sv["qkv"], sv["g"], sv["beta"], sv["tmats"], sv["states"], dstates, do, dvn, sv["vn"],
                                       name="deltanet_post")
    dab, dalog, ddtb = _gates_bwd(proj, wt["a_log"], wt["dt_bias"], dg, dbeta, name="gates_bwd")
    gr["a_log"], gr["dt_bias"] = dalog[0, :HEADS], ddtb[0, :HEADS]
    dqkv, dconv = _qkv_prep_bwd(proj, wt["conv_qkv"], dqkv_h, name="qkv_prep_bwd")
    big["conv_qkv"] = _col_blocks(dconv)
    dproj = jnp.concatenate([dqkv, dz, dab, dhp, dcb, dcc, dch], axis=1)
    dwin = _matmul(sv["h1"], dproj, "tn", out_dtype=BF16, name="mm_din")
    big["w_in"] = _col_blocks(jnp.concatenate([dwin[:, :AB_COL + 2 * HEADS], dwin[:, AB_COL + LANE:]], axis=1))
    dh1 = _matmul(dproj, wt["w_in"], "nt", name="mm_dh1", after=emit(2, big))
    dx, gr["norm1_g"] = _rmsnorm_bwd(sv["x"], wt["norm1_g"], dh1, dx1, name="rmsnorm_bwd")
    return dx, gr


FETCH_GROUPS = (("w_in", "conv_qkv", "sconv_w"), ("w_out",), ("w_gate", "w_up"), ("w_down",), ("ple_gate", "ple_proj"))
EMIT_GROUPS = (("ple_proj", "ple_gate", "w_down"), ("w_gate", "w_up", "w_out"), ("w_in", "conv_qkv", "sconv_w"))


def _small_weights(w, i):
    return dict(
        norm1_g=w["norm1_g"][i][None], norm2_g=w["norm2_g"][i][None], onorm_g=w["onorm_g"][i][None],
        a_log=_pad_lanes(w["a_log"][i]), dt_bias=_pad_lanes(w["dt_bias"][i]),
        pool_scale=w["pool_scale"][i][None], pool_win=_pool_windows(), pool_wbd=_block_diag_pairs(w["pool_w"][i]))


def _as_read(name, gathered):
    if name == "w_in":
        w_in = _cols_joined(gathered)
        return jnp.concatenate([w_in[:, :AB_COL + 2 * HEADS], jnp.zeros((D_MODEL, LANE - 2 * HEADS), BF16),
                                w_in[:, AB_COL + 2 * HEADS:]], axis=1)
    if name in ("conv_qkv", "sconv_w"):
        return _cols_joined(gathered)
    if name == "ple_proj":
        return gathered
    return gathered.reshape(-1, D_MODEL)


def _layer_weights(gathered, w, i):
    return {**_small_weights(w, i), **{k: _as_read(k, g) for k, g in gathered.items()}}


def _local_step(x, p, target, layers, final_g):
    saved = []
    h = x
    for i in range(DEPTH):
        replicated = {k: v for k, v in layers[i].items() if k not in SHARDED}
        h, sv = _layer_fwd(h, p[i], replicated, lambda group, after, i=i: {k: layers[i][k] for k in FETCH_GROUPS[group]})
        saved.append(sv)
    dx, dgf, loss = _loss_head(h, final_g, target, name="loss_head")
    big, small = [{} for _ in range(DEPTH)], [None] * DEPTH
    for i in reversed(range(DEPTH)):
        dx, small[i] = _layer_bwd(dx, saved[i], lambda group, blocks, i=i: big[i].update({k: blocks[k] for k in EMIT_GROUPS[group]}))
    return loss, dx, big, small, dgf


SHARDED = ("w_in", "w_gate", "w_up", "w_down", "w_out", "ple_gate", "ple_proj", "conv_qkv", "sconv_w")
SMALL = ("norm1_g", "a_log", "dt_bias", "onorm_g", "pool_w", "pool_scale", "norm2_g", "final_g")
SLAB_COLS = 1024


def _payload(name, shard):
    if name in ("conv_qkv", "sconv_w"):
        return shard
    out = shard.astype(BF16)
    if name in TRANSPOSED + ("w_down",):
        out = jnp.pad(out, ((0, FF_BLOCK - FF_SHARD), (0, 0)))
    return out


TRANSPOSED = ("w_gate", "w_up")


def _ff_rows(t):
    return jnp.transpose(t, (0, 2, 1))


def _slab_rows(shape):
    size = 1
    for s in shape:
        size *= s
    return SUBLANE * -(-size // (SUBLANE * SLAB_COLS))


def _pack_slab(parts, extra_row):
    rows = []
    for name in SMALL:
        flat = parts[name].reshape(-1)
        nrow = _slab_rows(parts[name].shape)
        rows.append(jnp.pad(flat, (0, nrow * SLAB_COLS - flat.shape[0])).reshape(nrow, SLAB_COLS))
    rows.append(jnp.pad(extra_row, ((0, SUBLANE - 1), (0, 0))))
    return jnp.concatenate(rows, axis=0)


def _unpack_slab(slab, shapes):
    out, row = {}, 0
    for name in SMALL:
        size = 1
        for s in shapes[name]:
            size *= s
        out[name] = slab[row:row + _slab_rows(shapes[name])].reshape(-1)[:size].reshape(shapes[name])
        row += _slab_rows(shapes[name])
    return out, row


def kernel(x, p, norm1_g, w_in, conv_qkv, a_log, dt_bias, onorm_g, pool_w, pool_scale, sconv_w, w_out, norm2_g, w_gate, w_up, w_down, ple_proj, ple_gate, final_g, loss_target, m_norm1_g, m_w_in, m_conv_qkv, m_a_log, m_dt_bias, m_onorm_g, m_pool_w, m_pool_scale, m_sconv_w, m_w_out, m_norm2_g, m_w_gate, m_w_up, m_w_down, m_ple_proj, m_ple_gate, m_final_g, v_norm1_g, v_w_in, v_conv_qkv, v_a_log, v_dt_bias, v_onorm_g, v_pool_w, v_pool_scale, v_sconv_w, v_w_out, v_norm2_g, v_w_gate, v_w_up, v_w_down, v_ple_proj, v_ple_gate, v_final_g):
    names = ["norm1_g", "w_in", "conv_qkv", "a_log", "dt_bias", "onorm_g", "pool_w", "pool_scale", "sconv_w", "w_out", "norm2_g",
             "w_gate", "w_up", "w_down", "ple_proj", "ple_gate", "final_g"]
    w = dict(zip(names, [norm1_g, w_in, conv_qkv, a_log, dt_bias, onorm_g, pool_w, pool_scale, sconv_w, w_out, norm2_g, w_gate, w_up,
                         w_down, ple_proj, ple_gate, final_g]))
    m = dict(zip(names, [m_norm1_g, m_w_in, m_conv_qkv, m_a_log, m_dt_bias, m_onorm_g, m_pool_w, m_pool_scale, m_sconv_w, m_w_out,
                         m_norm2_g, m_w_gate, m_w_up, m_w_down, m_ple_proj, m_ple_gate, m_final_g]))
    v = dict(zip(names, [v_norm1_g, v_w_in, v_conv_qkv, v_a_log, v_dt_bias, v_onorm_g, v_pool_w, v_pool_scale, v_sconv_w, v_w_out,
                         v_norm2_g, v_w_gate, v_w_up, v_w_down, v_ple_proj, v_ple_gate, v_final_g]))
    for group in (w, m, v):
        group.update({k: _ff_rows(group[k]) for k in TRANSPOSED})

    first, rest = FETCH_GROUPS[0], tuple(k for members in FETCH_GROUPS[1:] for k in members)
    gathered = dict(zip(first, _all_gather([_payload(k, w[k][0]) for k in first], name="all_gather_weights")))
    (flying0,), token = _exchange_start([[_payload(k, w[k][0]) for k in rest]], CHIP_GATHER, name="gather_start_0")
    replicated = [_small_weights(w, i) for i in range(DEPTH)]
    replicated[0]["norm1_g"] = replicated[0]["norm1_g"] + token[0, 0]
    flying1 = []

    def fetch(i, group, after):
        if i == 0 and group == 1:
            landed = _exchange_wait(flying0, after, CHIP_GATHER, name="gather_wait_0")
            gathered.update(zip(rest, _pair_swap(landed, name="pair_swap")))
            started, token = _exchange_start([[_payload(k, w[k][1]) for k in SHARDED]], CHIP_GATHER, name="gather_start_1",
                                             after=gathered[rest[0]])
            flying1.extend(started)
            return {**{k: _as_read(k, gathered[k]) for k in FETCH_GROUPS[group]},
                    "conv_qkv": _as_read("conv_qkv", gathered["conv_qkv"]) + token[0, 0]}
        if i == 1 and group == 0:
            landed = _exchange_wait(flying1[0], after, CHIP_GATHER, name="gather_wait_1")
            gathered.update(zip(SHARDED, _pair_swap(landed, name="pair_swap")))
        return {k: _as_read(k, gathered[k]) for k in FETCH_GROUPS[group]}

    def reduce_scatter_start(members, blocks, tag):
        mine = [blocks[k] for k in members]
        theirs = _pair_exchange(mine, name="pair_exchange")
        sums = [_pair_add(a, b, name="pair_add") for a, b in zip(mine, theirs)]
        (started,), token = _exchange_start([sums], CHIP_SCATTER, name="exchange_start_" + tag)
        return started, token

    h, saved0 = _layer_fwd(x[0], p[0, 0], replicated[0], functools.partial(fetch, 0))
    h, saved1 = _layer_fwd(h, p[1, 0], replicated[1], functools.partial(fetch, 1))
    dx, dgf, loss_part = _loss_head(h, final_g[None], loss_target[0], name="loss_head")
    small, big1, flying0 = [None] * DEPTH, {}, []
    dx, small[1] = _layer_bwd(dx, saved1, lambda group, blocks: big1.update({k: blocks[k] for k in EMIT_GROUPS[group]}))
    flying1, token = reduce_scatter_start(SHARDED, big1, "1")

    def emit(group, blocks):
        started, token = reduce_scatter_start(EMIT_GROUPS[group], blocks, f"0_{group}")
        flying0.append(started)
        return token

    dx, small[0] = _layer_bwd(dx, saved0, emit, after=token)
    received = [{}, dict(zip(SHARDED, _exchange_wait(flying1, dx, CHIP_SCATTER, name="exchange_wait_1")))]
    for group, members in enumerate(EMIT_GROUPS):
        received[0].update(zip(members, _exchange_wait(flying0[group], dx, CHIP_SCATTER, name=f"exchange_wait_0_{group}")))

    grads = {k: jnp.stack([small[i][k] for i in range(DEPTH)]) for k in small[0]}
    grads = {k: g[:, 0] if k in ("norm1_g", "norm2_g", "onorm_g", "pool_scale") else g for k, g in grads.items()}
    grads["final_g"] = dgf[0]
    loss_row = jnp.pad(loss_part, ((0, 0), (0, SLAB_COLS - LANE)))
    (small_flying,), token = _exchange_start([[_pack_slab(grads, loss_row)]], GATHER, name="small_gather_start")

    out_g, out_d, out_m, out_v = {}, {}, {}, {}
    for k in SHARDED:
        out_g[k], out_d[k], out_m[k], out_v[k] = _adamw_reduce(w[k], [received[i][k] for i in range(DEPTH)], m[k], v[k],
                                                                name="adamw_" + k, after=token)
    behind_all = jnp.stack([out_v[k][0, 0, 0] for k in SHARDED])
    (small_parts,) = _exchange_wait(small_flying, behind_all, GATHER, name="small_gather_wait")
    zero_row = jnp.zeros((1, SLAB_COLS), F32)
    slabs = _adamw_reduce(_pack_slab(w, zero_row)[None], [small_parts], _pack_slab(m, zero_row)[None],
                          _pack_slab(v, zero_row)[None], name="adamw_small")
    slabs = [s[0] for s in slabs]
    shapes = {k: w[k].shape for k in SMALL}
    for dst, slab in zip((out_g, out_d, out_m, out_v), slabs):
        vals, _ = _unpack_slab(slab, shapes)
        dst.update(vals)
    _, loss_at = _unpack_slab(slabs[0], shapes)
    loss = slabs[0][loss_at, 0]
    for group in (out_g, out_d, out_m, out_v):
        group.update({k: _ff_rows(group[k]) for k in TRANSPOSED})

    return (loss, dx[None], *[out_g[k] for k in names], *[out_d[k] for k in names], *[out_m[k] for k in names],
            *[out_v[k] for k in names])
```

```python
import functools

import jax
import jax.numpy as jnp
from jax import lax
from jax.experimental import pallas as pl
from jax.experimental.pallas import tpu as pltpu

F32 = jnp.float32
BF16 = jnp.bfloat16

D_MODEL = 1024
DEPTH = 2
PLE_DIM = 256
EPS = 1e-6
HEAD_DIM = 128
HEADS = 4
A_DIM = HEADS * HEAD_DIM
QKV_TAPS = 4
CHUNK = 64
POOL_WINDOWS = (2, 4, 8, 16)
POOL_DIM = 256
CONV_DIM = 256
CONV_TAPS = 3
D_FF = 2816
D_IN = 3080
D_IN_PAD = 3200
AB_COL = 2048
N_DEV = 8

ADAM_LR = 0.001
ADAM_B1 = 0.9
ADAM_B2 = 0.999
ADAM_EPS = 1e-08
ADAM_WD = 0.01
ADAM_STEP = 10

LANE = 128
SUBLANE = 8
VMEM_BYTES_V7X = 64 * 1024 * 1024
VMEM_LIMIT = 48 * 1024 * 1024

_HI = lax.Precision.HIGHEST
NN = ((1,), (0,))
NT = ((1,), (1,))
TN = ((0,), (0,))
MESH = pl.DeviceIdType.MESH


def _dot(a, b, dims, hi=False):
    if hi:
        return lax.dot_general(a, b, (dims, ((), ())), precision=_HI, preferred_element_type=F32)
    return lax.dot_general(a.astype(BF16), b.astype(BF16), (dims, ((), ())), preferred_element_type=F32)


def _pcall(body, *, name, out_shape, grid=(), in_specs=None, out_specs=None, scratch_shapes=(), semantics=None,
           vmem_limit=None, after=None, **kw):
    params = {}
    if semantics is not None:
        params["dimension_semantics"] = semantics
    if vmem_limit is not None:
        params["vmem_limit_bytes"] = vmem_limit
    if after is not None:
        n_in, inner = len(in_specs), body
        body = lambda *refs: inner(*refs[:n_in], *refs[n_in + 1:])
        in_specs = list(in_specs) + [pl.BlockSpec(after.shape, lambda *_: (0,) * after.ndim)]
    call = pl.pallas_call(
        body, name=name, out_shape=out_shape, grid=grid, in_specs=in_specs, out_specs=out_specs,
        scratch_shapes=list(scratch_shapes), compiler_params=pltpu.CompilerParams(**params), **kw)
    return call if after is None else (lambda *args: call(*args, after))


def _sigmoid(x):
    return 1.0 / (1.0 + jnp.exp(-x))


def _softplus(x):
    return jnp.maximum(x, 0.0) + jnp.log(1.0 + jnp.exp(-jnp.abs(x)))


def _tile(n, cap, mult):
    if n <= cap:
        return n
    best = None
    for t in range(mult, cap + 1, mult):
        if n % t == 0:
            best = t
    assert best is not None, (n, cap, mult)
    return best


ROWS_PER_STEP = 512
NARROW_RESULT = 1024
COLS_PER_DOT = 640


def _resident(weight):
    return pl.BlockSpec(weight.shape, lambda i: (0,) * weight.ndim, pipeline_mode=pl.Buffered(1))


def _matmul_rows(a, b, mode, *, name, res=None, out_dtype=F32, b_blocked=False, after=None):
    m, k = a.shape
    if b_blocked:
        nb, _, bw = b.shape
        n = nb * bw if mode == "nn" else b.shape[1]
    else:
        n = b.shape[1] if mode == "nn" else b.shape[0]
    tm = _tile(m, ROWS_PER_STEP if n > NARROW_RESULT else 2 * ROWS_PER_STEP, 16)
    cn = bw if (b_blocked and mode == "nn") else _tile(n, COLS_PER_DOT, LANE)
    has_res = res is not None

    def body(*refs):
        a_ref, b_ref = refs[0], refs[1]
        res_ref = refs[2] if has_res else None
        o_ref = refs[2 + has_res]
        if not (b_blocked and mode == "nt"):
            av = a_ref[...].astype(BF16)
        for j in range(n // cn):
            cols = pl.ds(j * cn, cn)
            if mode == "nn":
                part = _dot(av, b_ref[j] if b_blocked else b_ref[:, cols], NN)
            elif not b_blocked:
                part = _dot(av, b_ref[cols, :], NT)
            else:
                part = None
                for s in range(nb):
                    term = _dot(a_ref[:, pl.ds(s * bw, bw)], b_ref[s, cols, :], NT)
                    part = term if part is None else part + term
            if has_res:
                part = part + res_ref[:, cols]
            o_ref[:, cols] = part.astype(o_ref.dtype)

    row = lambda width: pl.BlockSpec((tm, width), lambda i: (i, 0))
    whole = _resident(b)
    ins = [a, b] + ([res] if has_res else [])
    specs = [row(k), whole] + ([row(n)] if has_res else [])
    return _pcall(body, name=name, out_shape=jax.ShapeDtypeStruct((m, n), out_dtype), grid=(m // tm,), in_specs=specs,
                  out_specs=row(n), semantics=("parallel",), vmem_limit=VMEM_LIMIT, after=after)(*ins)


def _matmul(a, b, mode, *, name, res=None, out_dtype=F32, b_blocked=False, out_blocked=None, after=None):
    if mode != "tn":
        return _matmul_rows(a, b, mode, name=name, res=res, out_dtype=out_dtype, b_blocked=b_blocked, after=after)
    assert res is None and not b_blocked and after is None
    (t, m), (t2, n) = a.shape, b.shape
    assert t == t2, (a.shape, b.shape)
    tm = _tile(m, 1024, LANE)
    tn = _tile(n, COLS_PER_DOT, LANE)
    if out_blocked is not None:
        assert out_blocked[0] * out_blocked[1] == n
        tn = out_blocked[1]

    def body(a_ref, b_ref, o_ref):
        part = _dot(a_ref[...], b_ref[...], TN).astype(o_ref.dtype)
        if out_blocked is None:
            o_ref[...] = part
        else:
            o_ref[0] = part

    o_spec = (pl.BlockSpec((tm, tn), lambda i, j: (i, j)) if out_blocked is None
              else pl.BlockSpec((1, tm, tn), lambda i, j: (j, i, 0)))
    o_shape = (m, n) if out_blocked is None else (out_blocked[0], m, out_blocked[1])
    return _pcall(body, name=name, out_shape=jax.ShapeDtypeStruct(o_shape, out_dtype), grid=(m // tm, n // tn),
                  in_specs=[pl.BlockSpec((t, tm), lambda i, j: (0, i)), pl.BlockSpec((t, tn), lambda i, j: (0, j))],
                  out_specs=o_spec, semantics=("parallel", "parallel"), vmem_limit=VMEM_LIMIT)(a, b)


ROW_TILE = 512


def _rows(t, width, idx=0):
    return pl.BlockSpec((ROW_TILE, width), lambda i: (i, idx))


def _vec(width):
    return pl.BlockSpec((1, width), lambda i: (0, 0))


def _rmsnorm_fwd(x, g, *, name):
    t, d = x.shape

    def body(x_ref, g_ref, h_ref):
        xv = x_ref[...]
        r = lax.rsqrt(jnp.mean(xv * xv, axis=-1, keepdims=True) + EPS)
        h_ref[...] = (xv * r * g_ref[...]).astype(BF16)

    return _pcall(body, name=name, out_shape=jax.ShapeDtypeStruct((t, d), BF16), grid=(t // ROW_TILE,),
                  in_specs=[_rows(t, d), _vec(d)], out_specs=_rows(t, d), semantics=("parallel",))(x, g)


def _rmsnorm_bwd(x, g, dh, dres, *, name):
    t, d = x.shape

    def body(x_ref, g_ref, dh_ref, dres_ref, dx_ref, dg_ref):
        xv = x_ref[...]
        r = lax.rsqrt(jnp.mean(xv * xv, axis=-1, keepdims=True) + EPS)
        xhat = xv * r
        dhv = dh_ref[...].astype(F32)
        dhg = dhv * g_ref[...]
        dx_ref[...] = dres_ref[...] + r * (dhg - xhat * jnp.mean(dhg * xhat, axis=-1, keepdims=True))
        part = jnp.sum(dhv * xhat, axis=0, keepdims=True)

        @pl.when(pl.program_id(0) == 0)
        def _():
            dg_ref[...] = part

        @pl.when(pl.program_id(0) > 0)
        def _():
            dg_ref[...] += part

    return _pcall(body, name=name, out_shape=(jax.ShapeDtypeStruct((t, d), F32), jax.ShapeDtypeStruct((1, d), F32)),
                  grid=(t // ROW_TILE,), in_specs=[_rows(t, d), _vec(d), _rows(t, d), _rows(t, d)],
                  out_specs=(_rows(t, d), _vec(d)), semantics=("arbitrary",))(x, g, dh, dres)


def _swiglu_fwd(h, w_gate, w_up, *, name):
    t, k = h.shape
    f = w_gate.shape[0]
    tm = _tile(t, ROWS_PER_STEP, 16)
    cn = _tile(f, COLS_PER_DOT, LANE)

    def body(h_ref, wg_ref, wu_ref, ff_ref, gate_ref, up_ref):
        hv = h_ref[...]
        for j in range(f // cn):
            cols = pl.ds(j * cn, cn)
            gv = _dot(hv, wg_ref[cols, :], NT)
            uv = _dot(hv, wu_ref[cols, :], NT)
            gate_ref[:, cols] = gv.astype(BF16)
            up_ref[:, cols] = uv.astype(BF16)
            ff_ref[:, cols] = (gv * _sigmoid(gv) * uv).astype(BF16)

    row = lambda width: pl.BlockSpec((tm, width), lambda i: (i, 0))
    out = jax.ShapeDtypeStruct((t, f), BF16)
    return _pcall(body, name=name, out_shape=(out,) * 3, grid=(t // tm,), in_specs=[row(k), _resident(w_gate), _resident(w_up)],
                  out_specs=(row(f),) * 3, semantics=("parallel",), vmem_limit=VMEM_LIMIT)(h, w_gate, w_up)


def _swiglu_bwd(dx2, w_down, gate, up, *, name, after=None):
    t, d = dx2.shape
    f = w_down.shape[0]
    tm = _tile(t, ROWS_PER_STEP, 16)
    cn = _tile(f, COLS_PER_DOT, LANE)

    def body(dx_ref, w_ref, gate_ref, up_ref, dgate_ref, dup_ref):
        dxv = dx_ref[...].astype(BF16)
        for j in range(f // cn):
            cols = pl.ds(j * cn, cn)
            dffv = _dot(dxv, w_ref[cols, :], NT)
            gv = gate_ref[:, cols].astype(F32)
            sig = _sigmoid(gv)
            dgate_ref[:, cols] = (dffv * up_ref[:, cols].astype(F32) * sig * (1.0 + gv * (1.0 - sig))).astype(BF16)
            dup_ref[:, cols] = (dffv * gv * sig).astype(BF16)

    row = lambda width: pl.BlockSpec((tm, width), lambda i: (i, 0))
    out = jax.ShapeDtypeStruct((t, f), BF16)
    return _pcall(body, name=name, out_shape=(out, out), grid=(t // tm,), in_specs=[row(d), _resident(w_down), row(f), row(f)],
                  out_specs=(row(f), row(f)), semantics=("parallel",), vmem_limit=VMEM_LIMIT, after=after)(dx2, w_down, gate, up)


def _ple_fwd(x2, pgl, pp, *, name):
    t, d = x2.shape

    def body(x_ref, pgl_ref, pp_ref, o_ref):
        o_ref[...] = x_ref[...] + _sigmoid(pgl_ref[...]) * pp_ref[...]

    return _pcall(body, name=name, out_shape=jax.ShapeDtypeStruct((t, d), F32), grid=(t // ROW_TILE,),
                  in_specs=[_rows(t, d)] * 3, out_specs=_rows(t, d), semantics=("parallel",))(x2, pgl, pp)


def _ple_bwd(dx3, pgl, pp, *, name, after=None):
    t, d = dx3.shape

    def body(dx_ref, pgl_ref, pp_ref, dpgl_ref, dpp_ref):
        dxv = dx_ref[...]
        sig = _sigmoid(pgl_ref[...])
        dpp_ref[...] = (dxv * sig).astype(BF16)
        dpgl_ref[...] = (dxv * pp_ref[...] * sig * (1.0 - sig)).astype(BF16)

    return _pcall(body, name=name, out_shape=(jax.ShapeDtypeStruct((t, d), BF16),) * 2, grid=(t // ROW_TILE,),
                  in_specs=[_rows(t, d)] * 3, out_specs=(_rows(t, d),) * 2, semantics=("parallel",), after=after)(dx3, pgl, pp)


def _loss_head(x3, g, target, *, name):
    t, d = x3.shape

    def body(x_ref, g_ref, t_ref, dx_ref, dg_ref, loss_ref):
        xv = x_ref[...]
        r = lax.rsqrt(jnp.mean(xv * xv, axis=-1, keepdims=True) + EPS)
        xhat = xv * r
        gv = g_ref[...]
        err = xhat * gv - t_ref[...]
        row_loss = jnp.sum(err * err, axis=-1, keepdims=True) * (0.5 / d)
        lpart = jnp.broadcast_to(jnp.sum(row_loss, axis=0, keepdims=True), (1, LANE))
        dy = err * (1.0 / d)
        dyg = dy * gv
        dx_ref[...] = r * (dyg - xhat * jnp.mean(dyg * xhat, axis=-1, keepdims=True))
        gpart = jnp.sum(dy * xhat, axis=0, keepdims=True)

        @pl.when(pl.program_id(0) == 0)
        def _():
            dg_ref[...] = gpart
            loss_ref[...] = lpart

        @pl.when(pl.program_id(0) > 0)
        def _():
            dg_ref[...] += gpart
            loss_ref[...] += lpart

    return _pcall(body, name=name,
                  out_shape=(jax.ShapeDtypeStruct((t, d), F32), jax.ShapeDtypeStruct((1, d), F32), jax.ShapeDtypeStruct((1, LANE), F32)),
                  grid=(t // ROW_TILE,), in_specs=[_rows(t, d), _vec(d), _rows(t, d)],
                  out_specs=(_rows(t, d), _vec(d), _vec(LANE)), semantics=("arbitrary",))(x3, g, target)


def _shift_down(x, d):
    if d == 0:
        return x
    row = lax.broadcasted_iota(jnp.int32, x.shape, 0)
    return jnp.where(row >= d, pltpu.roll(x, d, 0), 0.0)


def _shift_up(x, d):
    if d == 0:
        return x
    t = x.shape[0]
    row = lax.broadcasted_iota(jnp.int32, x.shape, 0)
    return jnp.where(row < t - d, pltpu.roll(x, t - d, 0), 0.0)


def _colsum(x):
    return jnp.sum(x, axis=0, keepdims=True)


def _col(t, idx_fn):
    return pl.BlockSpec((t, LANE), idx_fn)


def _conv_fwd(x, w_ref, taps):
    acc = None
    for j in range(taps):
        term = w_ref[pl.ds(j, 1), :] * _shift_down(x, taps - 1 - j)
        acc = term if acc is None else acc + term
    return acc


def _conv_bwd(x, dy, w_ref, dw_ref, taps):
    dx = None
    for j in range(taps):
        term = w_ref[pl.ds(j, 1), :] * _shift_up(dy, taps - 1 - j)
        dx = term if dx is None else dx + term
        dw_ref[pl.ds(j, 1), :] = _colsum(dy * _shift_down(x, taps - 1 - j))
    return dx


def _qkv_prep_fwd(proj, conv_w, *, name):
    t = proj.shape[0]
    scale = HEAD_DIM ** -0.5

    def body(x_ref, w_ref, o_ref):
        j = pl.program_id(0)
        c = _conv_fwd(x_ref[...], w_ref, QKV_TAPS)
        s = c * _sigmoid(c)
        r = lax.rsqrt(jnp.sum(s * s, axis=-1, keepdims=True) + EPS)
        f = jnp.where(j < 2 * HEADS, r, 1.0) * jnp.where(j < HEADS, scale, 1.0)
        o_ref[0] = s * f

    return _pcall(body, name=name, out_shape=jax.ShapeDtypeStruct((3 * HEADS, t, LANE), F32), grid=(3 * HEADS,),
                  in_specs=[_col(t, lambda j: (0, j)), pl.BlockSpec((QKV_TAPS, LANE), lambda j: (0, j))],
                  out_specs=pl.BlockSpec((1, t, LANE), lambda j: (j, 0, 0)), semantics=("parallel",),
                  vmem_limit=VMEM_LIMIT)(proj, conv_w)


def _qkv_prep_bwd(proj, conv_w, dqkv, *, name):
    t = proj.shape[0]
    scale = HEAD_DIM ** -0.5

    def body(x_ref, w_ref, d_ref, dx_ref, dw_ref):
        j = pl.program_id(0)
        xv = x_ref[...]
        c = _conv_fwd(xv, w_ref, QKV_TAPS)
        sig = _sigmoid(c)
        s = c * sig
        r = lax.rsqrt(jnp.sum(s * s, axis=-1, keepdims=True) + EPS)
        n0 = s * r
        dv = d_ref[0]
        dn0 = dv * jnp.where(j < HEADS, scale, 1.0)
        ds_norm = r * (dn0 - n0 * jnp.sum(dn0 * n0, axis=-1, keepdims=True))
        ds = jnp.where(j < 2 * HEADS, ds_norm, dv)
        dc = ds * sig * (1.0 + c * (1.0 - sig))
        dx_ref[...] = _conv_bwd(xv, dc, w_ref, dw_ref, QKV_TAPS).astype(BF16)

    return _pcall(body, name=name,
                  out_shape=(jax.ShapeDtypeStruct((t, 3 * A_DIM), BF16), jax.ShapeDtypeStruct((QKV_TAPS, 3 * A_DIM), F32)),
                  grid=(3 * HEADS,),
                  in_specs=[_col(t, lambda j: (0, j)), pl.BlockSpec((QKV_TAPS, LANE), lambda j: (0, j)),
                            pl.BlockSpec((1, t, LANE), lambda j: (j, 0, 0))],
                  out_specs=(_col(t, lambda j: (0, j)), pl.BlockSpec((QKV_TAPS, LANE), lambda j: (0, j))),
                  semantics=("parallel",), vmem_limit=VMEM_LIMIT)(proj, conv_w, dqkv)


def _lane_pick(x, lane_idx, lane):
    return jnp.broadcast_to(jnp.sum(jnp.where(lane == lane_idx, x, 0.0), axis=-1, keepdims=True), x.shape)


def _gates_fwd(proj, alog, dtb, *, name):
    t = proj.shape[0]

    def body(x_ref, alog_ref, dtb_ref, g_ref, b_ref):
        xv = x_ref[...]
        lane = lax.broadcasted_iota(jnp.int32, xv.shape, 1)
        gall = -jnp.exp(alog_ref[...]) * _softplus(xv + dtb_ref[...])
        ball = _sigmoid(xv)
        for h in range(HEADS):
            g_ref[h] = _lane_pick(gall, h, lane)
            b_ref[h] = _lane_pick(ball, HEADS + h, lane)

    out = jax.ShapeDtypeStruct((HEADS, t, LANE), F32)
    whole = pl.BlockSpec((HEADS, t, LANE), lambda i: (0, 0, 0))
    return _pcall(body, name=name, out_shape=(out, out), grid=(1,),
                  in_specs=[_col(t, lambda i: (0, AB_COL // LANE)), _vec(LANE), _vec(LANE)], out_specs=(whole, whole),
                  semantics=("arbitrary",), vmem_limit=VMEM_LIMIT)(proj, alog, dtb)


def _gates_bwd(proj, alog, dtb, dg, dbeta, *, name):
    t = proj.shape[0]

    def body(x_ref, alog_ref, dtb_ref, dg_ref, db_ref, dab_ref, dalog_ref, ddtb_ref):
        xv = x_ref[...]
        lane = lax.broadcasted_iota(jnp.int32, xv.shape, 1)
        lane1 = lax.broadcasted_iota(jnp.int32, (1, LANE), 1)
        z = xv + dtb_ref[...]
        nea = -jnp.exp(alog_ref[...])
        da_f = nea * _sigmoid(z)
        g_f = nea * _softplus(z)
        ball = _sigmoid(xv)
        db_f = ball * (1.0 - ball)
        dab = jnp.zeros_like(xv)
        dalog = jnp.zeros((1, LANE), F32)
        for h in range(HEADS):
            dgh = dg_ref[h]
            dab = dab + jnp.where(lane == h, dgh * da_f, 0.0) + jnp.where(lane == HEADS + h, db_ref[h] * db_f, 0.0)
            dalog = dalog + jnp.where(lane1 == h, _colsum(dgh * g_f), 0.0)
        dab_ref[...] = dab.astype(BF16)
        dalog_ref[...] = dalog
        ddtb_ref[...] = jnp.where(lane1 < HEADS, _colsum(dab), 0.0)

    whole = pl.BlockSpec((HEADS, t, LANE), lambda i: (0, 0, 0))
    vec = jax.ShapeDtypeStruct((1, LANE), F32)
    return _pcall(body, name=name, out_shape=(jax.ShapeDtypeStruct((t, LANE), BF16), vec, vec), grid=(1,),
                  in_specs=[_col(t, lambda i: (0, AB_COL // LANE)), _vec(LANE), _vec(LANE), whole, whole],
                  out_specs=(_col(t, lambda i: (0, 0)), _vec(LANE), _vec(LANE)), semantics=("arbitrary",),
                  vmem_limit=VMEM_LIMIT)(proj, alog, dtb, dg, dbeta)


Z_COL = 3 * A_DIM // LANE


def _apost_fwd(o, proj, gn, *, name):
    t = proj.shape[0]

    def body(o_ref, z_ref, gn_ref, y_ref):
        ov = o_ref[0]
        z = z_ref[...]
        r = lax.rsqrt(jnp.mean(ov * ov, axis=-1, keepdims=True) + EPS)
        y_ref[...] = (ov * r * gn_ref[...] * (z * _sigmoid(z))).astype(BF16)

    return _pcall(body, name=name, out_shape=jax.ShapeDtypeStruct((t, A_DIM), BF16), grid=(HEADS,),
                  in_specs=[pl.BlockSpec((1, t, LANE), lambda h: (h, 0, 0)), _col(t, lambda h: (0, Z_COL + h)),
                            pl.BlockSpec((1, LANE), lambda h: (0, 0))],
                  out_specs=_col(t, lambda h: (0, h)), semantics=("parallel",), vmem_limit=VMEM_LIMIT)(o, proj, gn)


def _apost_bwd(o, proj, gn, dmixed, *, name):
    t = proj.shape[0]

    def body(o_ref, z_ref, gn_ref, d_ref, do_ref, dz_ref, dgn_ref):
        ov = o_ref[0]
        z = z_ref[...]
        gnv = gn_ref[...]
        dv = d_ref[...]
        r = lax.rsqrt(jnp.mean(ov * ov, axis=-1, keepdims=True) + EPS)
        ohat = ov * r
        sig = _sigmoid(z)
        dy = dv * (z * sig)
        dz_ref[...] = (dv * ohat * gnv * sig * (1.0 + z * (1.0 - sig))).astype(BF16)
        dyo = dy * gnv
        do_ref[0] = r * (dyo - ohat * jnp.mean(dyo * ohat, axis=-1, keepdims=True))
        part = _colsum(dy * ohat)

        @pl.when(pl.program_id(0) == 0)
        def _():
            dgn_ref[...] = part

        @pl.when(pl.program_id(0) > 0)
        def _():
            dgn_ref[...] += part

    return _pcall(body, name=name,
                  out_shape=(jax.ShapeDtypeStruct((HEADS, t, LANE), F32), jax.ShapeDtypeStruct((t, A_DIM), BF16),
                             jax.ShapeDtypeStruct((1, LANE), F32)),
                  grid=(HEADS,),
                  in_specs=[pl.BlockSpec((1, t, LANE), lambda h: (h, 0, 0)), _col(t, lambda h: (0, Z_COL + h)),
                            pl.BlockSpec((1, LANE), lambda h: (0, 0)), _col(t, lambda h: (0, h))],
                  out_specs=(pl.BlockSpec((1, t, LANE), lambda h: (h, 0, 0)), _col(t, lambda h: (0, h)),
                             pl.BlockSpec((1, LANE), lambda h: (0, 0))),
                  semantics=("arbitrary",), vmem_limit=VMEM_LIMIT)(o, proj, gn, dmixed)


POOL_COL = (AB_COL + LANE) // LANE
CB_COL = POOL_COL + POOL_DIM // LANE
CC_COL = CB_COL + CONV_DIM // LANE
CH_COL = CC_COL + CONV_DIM // LANE
MAX_WIN_LOG2 = 4


def _window_sums(x, shift):
    sums = []
    cur = x
    for k in range(MAX_WIN_LOG2):
        cur = cur + shift(cur, 1 << k)
        sums.append(cur)
    return sums


def _pick_window(sums, win):
    out = sums[-1]
    for k in range(MAX_WIN_LOG2 - 2, -1, -1):
        out = jnp.where(win == float(2 << k), sums[k], out)
    return out


def _pool_counts(shape, win):
    row = lax.broadcasted_iota(jnp.int32, shape, 0).astype(F32)
    return jnp.minimum(row + 1.0, win)


def _pool_fwd(proj, win, wbd, scale, *, name):
    t = proj.shape[0]

    def body(x_ref, win_ref, w_ref, s_ref, y_ref):
        xv = x_ref[...]
        winv = win_ref[...]
        pooled = _pick_window(_window_sums(xv, _shift_down), winv) / _pool_counts(xv.shape, winv) - xv
        y_ref[...] = (_dot(pooled, w_ref[0], NN) * s_ref[...]).astype(BF16)

    nb = POOL_DIM // LANE
    vec = pl.BlockSpec((1, LANE), lambda b: (0, b))
    return _pcall(body, name=name, out_shape=jax.ShapeDtypeStruct((t, POOL_DIM), BF16), grid=(nb,),
                  in_specs=[_col(t, lambda b: (0, POOL_COL + b)), vec, pl.BlockSpec((1, LANE, LANE), lambda b: (b, 0, 0)), vec],
                  out_specs=_col(t, lambda b: (0, b)), semantics=("parallel",), vmem_limit=VMEM_LIMIT)(proj, win, wbd, scale)


def _pool_bwd(proj, win, wbd, scale, dmixed, *, name):
    t = proj.shape[0]

    def body(x_ref, win_ref, w_ref, s_ref, d_ref, dx_ref, dw_ref, ds_ref):
        xv = x_ref[...]
        winv = win_ref[...]
        cnt = _pool_counts(xv.shape, winv)
        pooled = _pick_window(_window_sums(xv, _shift_down), winv) / cnt - xv
        dv = d_ref[...]
        ds_ref[...] = _colsum(dv * _dot(pooled, w_ref[0], NN))
        dy0 = dv * s_ref[...]
        dw_ref[0] = _dot(pooled, dy0, TN)
        dpooled = _dot(dy0, w_ref[0], NT)
        dmean = dpooled / cnt
        dx_ref[...] = (_pick_window(_window_sums(dmean, _shift_up), winv) - dpooled).astype(BF16)

    nb = POOL_DIM // LANE
    vec = pl.BlockSpec((1, LANE), lambda b: (0, b))
    mat = pl.BlockSpec((1, LANE, LANE), lambda b: (b, 0, 0))
    first = A_DIM // LANE
    return _pcall(body, name=name,
                  out_shape=(jax.ShapeDtypeStruct((t, POOL_DIM), BF16), jax.ShapeDtypeStruct((nb, LANE, LANE), F32),
                             jax.ShapeDtypeStruct((1, POOL_DIM), F32)),
                  grid=(nb,),
                  in_specs=[_col(t, lambda b: (0, POOL_COL + b)), vec, mat, vec, _col(t, lambda b: (0, first + b))],
                  out_specs=(_col(t, lambda b: (0, b)), mat, vec), semantics=("parallel",),
                  vmem_limit=VMEM_LIMIT)(proj, win, wbd, scale, dmixed)


def _sconv_fwd(proj, w, *, name):
    t = proj.shape[0]

    def body(cb_ref, cc_ref, ch_ref, w_ref, y_ref):
        y_ref[...] = (cb_ref[...] * _conv_fwd(cc_ref[...] * ch_ref[...], w_ref, CONV_TAPS)).astype(BF16)

    nb = CONV_DIM // LANE
    return _pcall(body, name=name, out_shape=jax.ShapeDtypeStruct((t, CONV_DIM), BF16), grid=(nb,),
                  in_specs=[_col(t, lambda b: (0, CB_COL + b)), _col(t, lambda b: (0, CC_COL + b)),
                            _col(t, lambda b: (0, CH_COL + b)), pl.BlockSpec((CONV_TAPS, LANE), lambda b: (0, b))],
                  out_specs=_col(t, lambda b: (0, b)), semantics=("parallel",), vmem_limit=VMEM_LIMIT)(proj, proj, proj, w)


def _sconv_bwd(proj, w, dmixed, *, name):
    t = proj.shape[0]

    def body(cb_ref, cc_ref, ch_ref, w_ref, d_ref, dcb_ref, dcc_ref, dch_ref, dw_ref):
        cc = cc_ref[...]
        ch = ch_ref[...]
        u = cc * ch
        dv = d_ref[...]
        dcb_ref[...] = (dv * _conv_fwd(u, w_ref, CONV_TAPS)).astype(BF16)
        du = _conv_bwd(u, dv * cb_ref[...], w_ref, dw_ref, CONV_TAPS)
        dcc_ref[...] = (du * ch).astype(BF16)
        dch_ref[...] = (du * cc).astype(BF16)

    nb = CONV_DIM // LANE
    first = (A_DIM + POOL_DIM) // LANE
    act = jax.ShapeDtypeStruct((t, CONV_DIM), BF16)
    wspec = pl.BlockSpec((CONV_TAPS, LANE), lambda b: (0, b))
    ospec = _col(t, lambda b: (0, b))
    return _pcall(body, name=name, out_shape=(act, act, act, jax.ShapeDtypeStruct((CONV_TAPS, CONV_DIM), F32)), grid=(nb,),
                  in_specs=[_col(t, lambda b: (0, CB_COL + b)), _col(t, lambda b: (0, CC_COL + b)),
                            _col(t, lambda b: (0, CH_COL + b)), wspec, _col(t, lambda b: (0, first + b))],
                  out_specs=(ospec, ospec, ospec, wspec), semantics=("parallel",),
                  vmem_limit=VMEM_LIMIT)(proj, proj, proj, w, dmixed)


def _chunk_masks():
    r = lax.broadcasted_iota(jnp.int32, (CHUNK, CHUNK), 0)
    c = lax.broadcasted_iota(jnp.int32, (CHUNK, CHUNK), 1)
    return r >= c, r > c, jnp.where(r == c, 1.0, 0.0).astype(F32)


def _split(a):
    hi = a.astype(BF16)
    return hi, (a - hi.astype(F32)).astype(BF16)


def _dot_split(a, b, dims):
    (ah, al), (bh, bl) = a, b
    return _dot(ah, bh, dims) + _dot(ah, bl, dims) + _dot(al, bh, dims)


def _tri_inv(lows, eye):
    xs = [eye - low for low in lows]
    ps = [_split(low) for low in lows]
    ps = [_split(_dot_split(p, p, NN)) for p in ps]
    for i in range(5):
        xs = [x + _dot_split(_split(x), p, NN) for x, p in zip(xs, ps)]
        if i < 4:
            ps = [_split(_dot_split(p, p, NN)) for p in ps]
    return xs


def _prefix_sum_rows(x):
    for k in range(6):
        x = x + _shift_down(x, 1 << k)
    return x


def _suffix_sum_rows(x):
    for k in range(6):
        x = x + _shift_up(x, 1 << k)
    return x


def _chunk_decay(g, incl):
    gcb = _prefix_sum_rows(g)
    gtot = _colsum(g)
    col = gcb[:, :CHUNK]
    row = gcb.T[:CHUNK, :]
    decay = jnp.exp(jnp.where(incl, col - row, -1e30))
    return gcb, gtot, decay


CHUNKS_PER_STEP = 4


def _heads_of(ref, base, rows):
    return [ref[base + h, rows, :] for h in range(HEADS)]


def _chunk_rows(j):
    return pl.ds(j * CHUNK, CHUNK)


def _deltanet_prep(qkv, g, beta, *, name):
    t = qkv.shape[1]
    n_chunks = t // CHUNK
    per = CHUNKS_PER_STEP
    probs = [(j, h) for j in range(per) for h in range(HEADS)]

    def body(qkv_ref, g_ref, b_ref, u_ref, w_ref, qg_ref, kg_ref, attn_ref, tm_ref):
        incl, strict, eye = _chunk_masks()
        q = [qkv_ref[h, _chunk_rows(j), :] for j, h in probs]
        k = [qkv_ref[HEADS + h, _chunk_rows(j), :] for j, h in probs]
        v = [qkv_ref[2 * HEADS + h, _chunk_rows(j), :] for j, h in probs]
        bv = [b_ref[h, _chunk_rows(j), :] for j, h in probs]
        dec = [_chunk_decay(g_ref[h, _chunk_rows(j), :], incl) for j, h in probs]
        kb = [a * b for a, b in zip(k, bv)]
        low = [jnp.where(strict, _dot(a, b, NT) * d[2], 0.0) for a, b, d in zip(kb, k, dec)]
        tm = _tri_inv(low, eye)
        egc = [jnp.exp(d[0]) for d in dec]
        u = [_dot(m, a * b, NN) for m, a, b in zip(tm, v, bv)]
        w = [_dot(m, a * e, NN) for m, a, e in zip(tm, kb, egc)]
        attn = [_dot(a, b, NT) * d[2] for a, b, d in zip(q, k, dec)]
        for i, (j, h) in enumerate(probs):
            rows = _chunk_rows(j)
            u_ref[h, rows, :] = u[i]
            w_ref[h, rows, :] = w[i].astype(BF16)
            qg_ref[h, rows, :] = (q[i] * egc[i]).astype(BF16)
            kg_ref[h, rows, :] = (k[i] * jnp.exp(dec[i][1] - dec[i][0])).astype(BF16)
            attn_ref[j, h] = attn[i].astype(BF16)
            tm_ref[j, h] = tm[i]

    act = lambda heads: pl.BlockSpec((heads, per * CHUNK, LANE), lambda n: (0, n, 0))
    mat = pl.BlockSpec((per, HEADS, CHUNK, CHUNK), lambda n: (n, 0, 0, 0))
    return _pcall(
        body, name=name,
        out_shape=(jax.ShapeDtypeStruct((HEADS, t, LANE), F32),) + (jax.ShapeDtypeStruct((HEADS, t, LANE), BF16),) * 3
        + (jax.ShapeDtypeStruct((n_chunks, HEADS, CHUNK, CHUNK), BF16), jax.ShapeDtypeStruct((n_chunks, HEADS, CHUNK, CHUNK), F32)),
        grid=(n_chunks // per,), in_specs=[act(3 * HEADS), act(HEADS), act(HEADS)],
        out_specs=(act(HEADS),) * 4 + (mat, mat), semantics=("parallel",), vmem_limit=VMEM_LIMIT)(qkv, g, beta)


SCAN_CHUNKS_PER_STEP = 8


def _deltanet_scan(u, w, qg, kg, attn, g, *, name):
    t = u.shape[1]
    n_chunks = t // CHUNK
    per = SCAN_CHUNKS_PER_STEP

    def body(u_ref, w_ref, qg_ref, kg_ref, attn_ref, g_ref, o_ref, vn_ref, st_ref, s_ref):
        @pl.when(pl.program_id(0) == 0)
        def _():
            s_ref[...] = jnp.zeros_like(s_ref)

        for j in range(per):
            rows = _chunk_rows(j)
            s = [s_ref[h] for h in range(HEADS)]
            vn = [u_ref[h, rows, :] - _dot(w_ref[h, rows, :], s[h], NN) for h in range(HEADS)]
            o = [_dot(qg_ref[h, rows, :], s[h], NN) + _dot(attn_ref[j, h], vn[h], NN) for h in range(HEADS)]
            eg = [jnp.exp(_colsum(g_ref[h, rows, :])) for h in range(HEADS)]
            for h in range(HEADS):
                st_ref[j, h] = s[h]
                s_ref[h] = s[h] * eg[h] + _dot(kg_ref[h, rows, :], vn[h], TN)
                o_ref[h, rows, :] = o[h]
                vn_ref[h, rows, :] = vn[h]

    act = pl.BlockSpec((HEADS, per * CHUNK, LANE), lambda n: (0, n, 0))
    out = jax.ShapeDtypeStruct((HEADS, t, LANE), F32)
    return _pcall(
        body, name=name, out_shape=(out, out, jax.ShapeDtypeStruct((n_chunks, HEADS, LANE, LANE), F32)), grid=(n_chunks // per,),
        in_specs=[act] * 4 + [pl.BlockSpec((per, HEADS, CHUNK, CHUNK), lambda n: (n, 0, 0, 0)), act],
        out_specs=(act, act, pl.BlockSpec((per, HEADS, LANE, LANE), lambda n: (n, 0, 0, 0))),
        scratch_shapes=[pltpu.VMEM((HEADS, LANE, LANE), F32)], semantics=("arbitrary",))(u, w, qg, kg, attn, g)


def _deltanet_bscan(w, qg, kg, attn, g, do, *, name):
    t = w.shape[1]
    n_chunks = t // CHUNK
    per = SCAN_CHUNKS_PER_STEP
    steps = n_chunks // per

    def body(w_ref, qg_ref, kg_ref, attn_ref, g_ref, do_ref, dvn_ref, dsn_ref, ds_ref):
        @pl.when(pl.program_id(0) == 0)
        def _():
            ds_ref[...] = jnp.zeros_like(ds_ref)

        for j in reversed(range(per)):
            rows = _chunk_rows(j)
            dsn = [ds_ref[h] for h in range(HEADS)]
            dov = [do_ref[h, rows, :] for h in range(HEADS)]
            dvn = [_dot(attn_ref[j, h], dov[h], TN) + _dot(kg_ref[h, rows, :], dsn[h], NN) for h in range(HEADS)]
            eg = [jnp.exp(_colsum(g_ref[h, rows, :])) for h in range(HEADS)]
            for h in range(HEADS):
                dsn_ref[j, h] = dsn[h]
                ds_ref[h] = _dot(qg_ref[h, rows, :], dov[h], TN) + eg[h] * dsn[h] - _dot(w_ref[h, rows, :], dvn[h], TN)
                dvn_ref[h, rows, :] = dvn[h]

    act = pl.BlockSpec((HEADS, per * CHUNK, LANE), lambda n: (0, steps - 1 - n, 0))
    return _pcall(
        body, name=name,
        out_shape=(jax.ShapeDtypeStruct((HEADS, t, LANE), F32), jax.ShapeDtypeStruct((n_chunks, HEADS, LANE, LANE), F32)),
        grid=(steps,),
        in_specs=[act] * 3 + [pl.BlockSpec((per, HEADS, CHUNK, CHUNK), lambda n: (steps - 1 - n, 0, 0, 0)), act, act],
        out_specs=(act, pl.BlockSpec((per, HEADS, LANE, LANE), lambda n: (steps - 1 - n, 0, 0, 0))),
        scratch_shapes=[pltpu.VMEM((HEADS, LANE, LANE), F32)], semantics=("arbitrary",))(w, qg, kg, attn, g, do)


def _sum_all(x):
    return jnp.sum(jnp.sum(x, axis=1, keepdims=True), axis=0, keepdims=True)


def _rowsum(x):
    return jnp.sum(x, axis=1, keepdims=True)


def _deltanet_post(qkv, g, beta, tmats, states, dstates, do, dvn, vn, *, name):
    t = qkv.shape[1]
    n_chunks = t // CHUNK
    per = CHUNKS_PER_STEP
    probs = [(j, h) for j in range(per) for h in range(HEADS)]

    def body(qkv_ref, g_ref, b_ref, tm_ref, st_ref, dsn_ref, do_ref, dvn_ref, vn_ref, dqkv_ref, dg_ref, db_ref):
        incl, strict, _ = _chunk_masks()
        ones = jnp.ones((CHUNK, LANE), BF16)
        last_row = lax.broadcasted_iota(jnp.int32, (CHUNK, LANE), 0) == CHUNK - 1
        z = lambda f, *cols: [f(*a) for a in zip(*cols)]
        q = [qkv_ref[h, _chunk_rows(j), :] for j, h in probs]
        k = [qkv_ref[HEADS + h, _chunk_rows(j), :] for j, h in probs]
        v = [qkv_ref[2 * HEADS + h, _chunk_rows(j), :] for j, h in probs]
        bv = [b_ref[h, _chunk_rows(j), :] for j, h in probs]
        dov = [do_ref[h, _chunk_rows(j), :] for j, h in probs]
        dvn_ = [dvn_ref[h, _chunk_rows(j), :] for j, h in probs]
        vn_ = [vn_ref[h, _chunk_rows(j), :] for j, h in probs]
        tm = [tm_ref[j, h] for j, h in probs]
        s = [st_ref[j, h] for j, h in probs]
        dsn = [dsn_ref[j, h] for j, h in probs]
        dec = [_chunk_decay(g_ref[h, _chunk_rows(j), :], incl) for j, h in probs]
        decay = [d[2] for d in dec]
        egc = [jnp.exp(d[0]) for d in dec]
        ekg = [jnp.exp(d[1] - d[0]) for d in dec]
        kb = z(lambda a, b: a * b, k, bv)
        vb = z(lambda a, b: a * b, v, bv)
        kbg = z(lambda a, b: a * b, kb, egc)
        qg = z(lambda a, b: a * b, q, egc)
        kg = z(lambda a, b: a * b, k, ekg)
        kk = z(lambda a, b: _dot(a, b, NT), kb, k)
        qk = z(lambda a, b: _dot(a, b, NT), q, k)
        dattn = z(lambda a, b: jnp.where(incl, _dot(a, b, NT), 0.0), dov, vn_)
        dqg = z(lambda a, b: _dot(a, b, NT), dov, s)
        dkg = z(lambda a, b: _dot(a, b, NT), vn_, dsn)
        dglast = z(lambda a, b, c, d, e: _sum_all(a * b) * jnp.exp(e[1]) + _sum_all(c * d), s, dsn, dkg, kg, dec)
        dw = z(lambda a, b: -_dot(a, b, NT), dvn_, s)
        dtm = z(lambda a, b, c, d: _dot(a, b, NT) + _dot(c, d, NT), dvn_, vb, dw, kbg)
        dvb = z(lambda a, b: _dot(a, b, TN), tm, dvn_)
        dkbg = z(lambda a, b: _dot(a, b, TN), tm, dw)
        dlow = z(lambda a, b: jnp.where(strict, -_dot(_dot(a, b, TN), a, NT), 0.0), tm, dtm)
        dkk = z(lambda a, b: a * b, dlow, decay)
        dqk = z(lambda a, b: a * b, dattn, decay)
        dkb = z(lambda a, b, c, d: _dot(a, b, NN) + c * d, dkk, k, dkbg, egc)
        dk = z(lambda a, b, c, d, e, f, g_, h_: _dot(a, b, TN) + _dot(c, d, TN) + e * f + g_ * h_, dkk, kb, dqk, q, dkg, ekg, dkb, bv)
        dq = z(lambda a, b, c, d: _dot(a, b, NN) + c * d, dqk, k, dqg, egc)
        m = z(lambda a, b, c, d, e: (a * b + c * d) * e, dlow, kk, dattn, qk, decay)
        mcol = [_dot(mh, ones, TN) + _dot(ml, ones, TN) for mh, ml in (_split(a) for a in m)]
        for i, (j, h) in enumerate(probs):
            rows = _chunk_rows(j)
            dqkv_ref[h, rows, :] = dq[i]
            dqkv_ref[HEADS + h, rows, :] = dk[i]
            dqkv_ref[2 * HEADS + h, rows, :] = dvb[i] * bv[i]
            db_ref[h, rows, :] = jnp.broadcast_to(_rowsum(dkb[i] * k[i] + dvb[i] * v[i]), (CHUNK, LANE))
            dgc = (_rowsum(dqg[i] * qg[i] + dkbg[i] * kbg[i] - dkg[i] * kg[i]) + _rowsum(m[i]) - mcol[i]
                   + jnp.where(last_row, dglast[i], 0.0))
            dg_ref[h, rows, :] = _suffix_sum_rows(dgc)

    act = lambda heads: pl.BlockSpec((heads, per * CHUNK, LANE), lambda n: (0, n, 0))
    mat = lambda d: pl.BlockSpec((per, HEADS, d, d), lambda n: (n, 0, 0, 0))
    out = jax.ShapeDtypeStruct((HEADS, t, LANE), F32)
    return _pcall(
        body, name=name, out_shape=(jax.ShapeDtypeStruct((3 * HEADS, t, LANE), F32), out, out), grid=(n_chunks // per,),
        in_specs=[act(3 * HEADS), act(HEADS), act(HEADS), mat(CHUNK), mat(LANE), mat(LANE), act(HEADS), act(HEADS), act(HEADS)],
        out_specs=(act(3 * HEADS), act(HEADS), act(HEADS)), semantics=("parallel",),
        vmem_limit=VMEM_LIMIT)(qkv, g, beta, tmats, states, dstates, do, dvn, vn)


ANY = pl.BlockSpec(memory_space=pl.ANY)
PEERS = N_DEV - 1


def _all_gather(arrays, *, name):
    n = len(arrays)

    def body(*refs):
        ins, outs = refs[:n], refs[n:2 * n]
        send_sems, recv_sems, local_sems = refs[2 * n:]
        x, y, c = lax.axis_index("x"), lax.axis_index("y"), lax.axis_index("c")
        me, sibling = (x, y, c), (x, y, 1 - c)
        chips = [(1 - x, y), (x, 1 - y), (1 - x, 1 - y)]

        def copy(a, k, block, to, src=None):
            dst = outs[a].at[4 * block[0] + 2 * block[1] + block[2]]
            return pltpu.make_async_remote_copy(src_ref=dst if src is None else src, dst_ref=dst, send_sem=send_sems.at[a * PEERS + k],
                                                recv_sem=recv_sems.at[a * PEERS + k], device_id=to, device_id_type=MESH)

        local = [pltpu.make_async_copy(ins[a], outs[a].at[4 * x + 2 * y + c], local_sems.at[a]) for a in range(n)]
        for cp in local:
            cp.start()
        first = []
        for a in range(n):
            first += [copy(a, 1 + j, me, (*chip, c), src=ins[a]) for j, chip in enumerate(chips)]
            first.append(copy(a, 0, me, sibling, src=ins[a]))
        for cp in first:
            cp.start()
        passed = []
        for a in range(n):
            for j, chip in enumerate(chips):
                copy(a, 1 + j, (*chip, c), me).wait_recv()
                fwd = copy(a, 4 + j, (*chip, c), sibling)
                fwd.start()
                passed.append(fwd)
        for a in range(n):
            copy(a, 0, sibling, me).wait_recv()
            for j, chip in enumerate(chips):
                copy(a, 4 + j, (*chip, 1 - c), me).wait_recv()
        for cp in first + passed:
            cp.wait_send()
        for cp in local:
            cp.wait()

    return _pcall(body, name=name, out_shape=tuple(jax.ShapeDtypeStruct((N_DEV,) + a.shape, a.dtype) for a in arrays),
                  in_specs=[ANY] * n, out_specs=(ANY,) * n,
                  scratch_shapes=[pltpu.SemaphoreType.DMA((n * PEERS,)), pltpu.SemaphoreType.DMA((n * PEERS,)),
                                  pltpu.SemaphoreType.DMA((n,))])(*arrays)


CHIPS = 4


def _pair_exchange(arrays, *, name):
    n = len(arrays)

    def body(*refs):
        ins, outs = refs[:n], refs[n:2 * n]
        send_sems, recv_sems = refs[2 * n:]
        x, y, c = lax.axis_index("x"), lax.axis_index("y"), lax.axis_index("c")
        copies = []
        for a in range(n):
            for q in range(CHIPS):
                cp = pltpu.make_async_remote_copy(src_ref=ins[a].at[2 * q + 1 - c], dst_ref=outs[a].at[q],
                                                  send_sem=send_sems.at[a * CHIPS + q], recv_sem=recv_sems.at[a * CHIPS + q],
                                                  device_id=(x, y, 1 - c), device_id_type=MESH)
                cp.start()
                copies.append(cp)
        for cp in copies:
            cp.wait()

    return _pcall(body, name=name, out_shape=tuple(jax.ShapeDtypeStruct((CHIPS,) + a.shape[1:], a.dtype) for a in arrays),
                  in_specs=[ANY] * n, out_specs=(ANY,) * n,
                  scratch_shapes=[pltpu.SemaphoreType.DMA((n * CHIPS,)), pltpu.SemaphoreType.DMA((n * CHIPS,))])(*arrays)


def _pair_add(blocks, theirs, *, name):
    _, r, c_ = blocks.shape
    tr = _tile(r, 512, 16)

    def body(mine_ref, theirs_ref, o_ref):
        core = lax.axis_index("c")
        own = jnp.where(core == 0, mine_ref[0, 0].astype(F32), mine_ref[0, 1].astype(F32))
        o_ref[0] = (own + theirs_ref[0].astype(F32)).astype(o_ref.dtype)

    spec = pl.BlockSpec((1, tr, c_), lambda q, i: (q, i, 0))
    return _pcall(body, name=name, out_shape=jax.ShapeDtypeStruct(theirs.shape, theirs.dtype), grid=(CHIPS, r // tr),
                  in_specs=[pl.BlockSpec((1, 2, tr, c_), lambda q, i: (q, 0, i, 0)), spec], out_specs=spec,
                  semantics=("parallel", "parallel"), vmem_limit=VMEM_LIMIT)(blocks.reshape(CHIPS, 2, r, c_), theirs)


HBM = pl.BlockSpec(memory_space=pltpu.HBM)
SEM = pl.BlockSpec(memory_space=pltpu.SEMAPHORE)
EFFECT = pltpu.SideEffectType.DATAFLOW_SIDE_EFFECTING


GATHER, CHIP_GATHER, CHIP_SCATTER = "gather", "chip_gather", "chip_scatter"
PEERS_OF = {GATHER: N_DEV - 1, CHIP_GATHER: CHIPS - 1, CHIP_SCATTER: CHIPS - 1}


def _direct_copies(srcs, lands, send_sems, recv_sems, local_sems, kind):
    x, y, c = lax.axis_index("x"), lax.axis_index("y"), lax.axis_index("c")
    peers = PEERS_OF[kind]
    mine = 2 * x + y if kind == CHIP_SCATTER else 4 * x + 2 * y + c
    copies = []
    for a, (src, land) in enumerate(zip(srcs, lands)):
        for k in range(1, peers + 1):
            bits = k if kind == GATHER else 2 * k
            px = 1 - x if bits & 4 else x
            py = 1 - y if bits & 2 else y
            pc = 1 - c if bits & 1 else c
            copies.append(pltpu.make_async_remote_copy(
                src_ref=src.at[2 * px + py] if kind == CHIP_SCATTER else src, dst_ref=land.at[mine],
                send_sem=send_sems.at[a * peers + k - 1], recv_sem=recv_sems.at[a * peers + k - 1],
                device_id=(px, py, pc), device_id_type=MESH))
    for a, (src, land) in enumerate(zip(srcs, lands)):
        copies.append(pltpu.make_async_copy(src.at[mine] if kind == CHIP_SCATTER else src, land.at[mine], local_sems.at[a]))
    return copies


def _pair_swap(arrays, *, name):
    n = len(arrays)

    def body(*refs):
        mine, zones = refs[:n], refs[n:2 * n]
        send_sems, recv_sems = refs[2 * n:]
        x, y, c = lax.axis_index("x"), lax.axis_index("y"), lax.axis_index("c")
        copies = []
        for a in range(n):
            for q in range(CHIPS):
                copies.append(pltpu.make_async_remote_copy(
                    src_ref=mine[a].at[2 * q + c], dst_ref=zones[a].at[2 * q + c], send_sem=send_sems.at[a * CHIPS + q],
                    recv_sem=recv_sems.at[a * CHIPS + q], device_id=(x, y, 1 - c), device_id_type=MESH))
        for cp in copies:
            cp.start()
        for cp in copies:
            cp.wait()

    return _pcall(body, name=name, out_shape=tuple(jax.ShapeDtypeStruct(a.shape, a.dtype) for a in arrays),
                  in_specs=[ANY] * n, out_specs=(ANY,) * n, input_output_aliases={i: i for i in range(n)},
                  scratch_shapes=[pltpu.SemaphoreType.DMA((n * CHIPS,)), pltpu.SemaphoreType.DMA((n * CHIPS,))])(*arrays)


def _exchange_start(groups, kind, *, name, after=None):
    srcs = [s for group in groups for s in group]
    n = len(srcs)
    sizes = [len(group) for group in groups]
    starts = [sum(sizes[:g]) for g in range(len(groups))]
    land_shapes = [s.shape if kind == CHIP_SCATTER else (N_DEV,) + s.shape for s in srcs]
    peers = PEERS_OF[kind]
    extra = [] if after is None else [after]

    def body(*refs):
        srcs_, lands = refs[:n], refs[n:2 * n]
        token = refs[-1]
        sem_refs = refs[2 * n + len(extra):]
        for g, (at, size) in enumerate(zip(starts, sizes)):
            send_sems, recv_sems, local_sems = sem_refs[3 * g:3 * g + 3]
            for cp in _direct_copies(srcs_[at:at + size], lands[at:at + size], send_sems, recv_sems, local_sems, kind):
                cp.start()
        token[...] = jnp.zeros_like(token)

    sems = tuple(t for size in sizes for t in (pltpu.SemaphoreType.DMA((size * peers,)), pltpu.SemaphoreType.DMA((size * peers,)),
                                               pltpu.SemaphoreType.DMA((size,))))
    thru = tuple(pltpu.HBM(s.shape, s.dtype) for s in srcs) + tuple(pltpu.HBM(shp, s.dtype) for shp, s in zip(land_shapes, srcs))
    ins = [pltpu.with_memory_space_constraint(s, pltpu.HBM) for s in srcs]
    ins += [pltpu.with_memory_space_constraint(lax.empty(shp, s.dtype), pltpu.HBM) for shp, s in zip(land_shapes, srcs)]
    out = pl.pallas_call(
        body, name=name, out_shape=sems + thru + (jax.ShapeDtypeStruct((SUBLANE, LANE), F32),),
        in_specs=[HBM] * (2 * n) + [ANY] * len(extra),
        out_specs=(SEM,) * len(sems) + (HBM,) * (2 * n) + (pl.BlockSpec(memory_space=pltpu.VMEM),),
        input_output_aliases={i: len(sems) + i for i in range(2 * n)},
        compiler_params=pltpu.CompilerParams(has_side_effects=EFFECT))(*ins, *extra)
    arrays = out[len(sems):-1]
    started = [tuple(out[3 * g:3 * g + 3]) + tuple(arrays[at:at + size]) + tuple(arrays[n + at:n + at + size])
               for g, (at, size) in enumerate(zip(starts, sizes))]
    return started, out[-1]


def _exchange_wait(started, after, kind, *, name):
    n = (len(started) - 3) // 2
    sems, arrays = started[:3], started[3:]

    def body(*refs):
        srcs_, lands = refs[:n], refs[n:2 * n]
        send_sems, recv_sems, local_sems = refs[2 * n:2 * n + 3]
        for cp in _direct_copies(srcs_, lands, send_sems, recv_sems, local_sems, kind):
            cp.wait()

    out = pl.pallas_call(
        body, name=name, out_shape=tuple(pltpu.HBM(a.shape, a.dtype) for a in arrays),
        in_specs=[HBM] * (2 * n) + [SEM] * 3 + [ANY], out_specs=(HBM,) * (2 * n),
        input_output_aliases={i: i for i in range(2 * n)},
        compiler_params=pltpu.CompilerParams(has_side_effects=EFFECT))(*arrays, *sems, after)
    return out[n:]


def _adamw_reduce(w, parts, m, v, *, name, after=None):
    layers, r, c = w.shape
    assert len(parts) == layers
    senders = parts[0].shape[0]
    tr = _tile(r, 512, 16)
    tiles = r // tr
    bc1 = 1.0 - ADAM_B1 ** ADAM_STEP
    bc2 = 1.0 - ADAM_B2 ** ADAM_STEP

    def body(w_ref, *rest):
        p_refs = rest[:layers]
        m_ref, v_ref, g_ref, d_ref, nm_ref, nv_ref = rest[layers:]

        def update(p_ref):
            g = p_ref[0, :, pl.ds(0, c)].astype(F32)
            for s in range(1, senders):
                g = g + p_ref[s, :, pl.ds(0, c)].astype(F32)
            nm = ADAM_B1 * m_ref[0] + (1.0 - ADAM_B1) * g
            nv = ADAM_B2 * v_ref[0] + (1.0 - ADAM_B2) * (g * g)
            g_ref[0] = g
            nm_ref[0] = nm
            nv_ref[0] = nv
            d_ref[0] = -ADAM_LR * ((nm / bc1) / (jnp.sqrt(nv / bc2) + ADAM_EPS) + ADAM_WD * w_ref[0])

        for layer in range(layers):
            pl.when(pl.program_id(0) == layer)(functools.partial(update, p_refs[layer]))

    def part_spec(layer, shape):
        rest = 0 if layer > 0 else tiles - 1
        return pl.BlockSpec((senders, tr, shape[2]), lambda l, i: (0, jnp.where(l == layer, i, rest), 0))

    spec = pl.BlockSpec((1, tr, c), lambda l, i: (l, i, 0))
    out = jax.ShapeDtypeStruct((layers, r, c), F32)
    return _pcall(body, name=name, out_shape=(out,) * 4, grid=(layers, tiles),
                  in_specs=[spec] + [part_spec(layer, p.shape) for layer, p in enumerate(parts)] + [spec, spec],
                  out_specs=(spec,) * 4, semantics=("arbitrary", "arbitrary"), vmem_limit=VMEM_LIMIT, after=after)(w, *parts, m, v)


def _pool_windows():
    return jnp.repeat(jnp.asarray(POOL_WINDOWS, F32), POOL_DIM // len(POOL_WINDOWS))[None, :]


def _block_diag_pairs(pool_w):
    z = jnp.zeros_like(pool_w[0])
    return jnp.stack([jnp.block([[pool_w[2 * b], z], [z, pool_w[2 * b + 1]]]) for b in range(2)])


def _pad_lanes(vec):
    return jnp.zeros((1, LANE), F32).at[0, :vec.shape[0]].set(vec)


FF_SHARD = D_FF // N_DEV
FF_BLOCK = 384
D_FF_PAD = N_DEV * FF_BLOCK


def _layer_fwd(x, p_i, wt, fetch):
    wt = {**wt, **fetch(0, x)}
    h1 = _rmsnorm_fwd(x, wt["norm1_g"], name="rmsnorm_fwd")
    proj = _matmul(h1, wt["w_in"], "nt", name="mm_in")
    wt.update(fetch(1, proj))
    qkv = _qkv_prep_fwd(proj, wt["conv_qkv"], name="qkv_prep_fwd")
    g, beta = _gates_fwd(proj, wt["a_log"], wt["dt_bias"], name="gates_fwd")
    u, w, qg, kg, attn, tmats = _deltanet_prep(qkv, g, beta, name="deltanet_prep")
    o, vn, states = _deltanet_scan(u, w, qg, kg, attn, g, name="deltanet_scan")
    o_a = _apost_fwd(o, proj, wt["onorm_g"], name="apost_fwd")
    o_b = _pool_fwd(proj, wt["pool_win"], wt["pool_wbd"], wt["pool_scale"], name="pool_fwd")
    o_c = _sconv_fwd(proj, wt["sconv_w"], name="sconv_fwd")
    mixed = jnp.concatenate([o_a, o_b, o_c], axis=1)
    x1 = _matmul(mixed, wt["w_out"], "nn", res=x, name="mm_out")
    h2 = _rmsnorm_fwd(x1, wt["norm2_g"], name="rmsnorm_fwd")
    wt.update(fetch(2, h2))
    ff, gate, up = _swiglu_fwd(h2, wt["w_gate"], wt["w_up"], name="swiglu_fwd")
    wt.update(fetch(3, ff))
    x2 = _matmul(ff, wt["w_down"], "nn", res=x1, name="mm_down")
    wt.update(fetch(4, x2))
    pgl = _matmul(x2, wt["ple_gate"], "nn", name="mm_pleg")
    pp = _matmul(p_i, wt["ple_proj"], "nn", b_blocked=True, name="mm_plep")
    x3 = _ple_fwd(x2, pgl, pp, name="ple_fwd")
    saved = dict(x=x, h1=h1, proj=proj, qkv=qkv, g=g, beta=beta, o=o, states=states, tmats=tmats, mixed=mixed, x1=x1, h2=h2,
                 gate=gate, up=up, ff=ff, x2=x2, pgl=pgl, pp=pp, p=p_i, w=w, qg=qg, kg=kg, attn=attn, vn=vn, wt=wt)
    return x3, saved


def _col_blocks(g):
    a = g.shape[0]
    return jnp.transpose(g.reshape(a, N_DEV, -1), (1, 0, 2))


def _cols_joined(blocks):
    return jnp.transpose(blocks, (1, 0, 2)).reshape(blocks.shape[1], -1)


def _layer_bwd(dx3, sv, emit, after=None):
    gr, big = {}, {}
    wt = sv["wt"]
    rows = D_MODEL // N_DEV
    dpgl, dpp = _ple_bwd(dx3, sv["pgl"], sv["pp"], name="ple_bwd", after=after)
    big["ple_proj"] = _matmul(sv["p"], dpp, "tn", out_blocked=(N_DEV, rows), out_dtype=BF16, name="mm_dplep")
    big["ple_gate"] = _matmul(sv["x2"], dpgl, "tn", out_dtype=BF16, name="mm_dpleg").reshape(N_DEV, rows, D_MODEL)
    dx2 = _matmul(dpgl, wt["ple_gate"], "nt", res=dx3, name="mm_dx2")
    big["w_down"] = _matmul(sv["ff"], dx2, "tn", out_dtype=BF16, name="mm_ddown").reshape(N_DEV, FF_BLOCK, D_MODEL)
    dgate, dup = _swiglu_bwd(dx2, wt["w_down"], sv["gate"], sv["up"], name="swiglu_bwd", after=emit(0, big))
    big["w_gate"] = _matmul(dgate, sv["h2"], "tn", out_dtype=BF16, name="mm_dgate").reshape(N_DEV, FF_BLOCK, D_MODEL)
    big["w_up"] = _matmul(dup, sv["h2"], "tn", out_dtype=BF16, name="mm_dup").reshape(N_DEV, FF_BLOCK, D_MODEL)
    dh2 = _matmul(dgate, wt["w_gate"], "nn", name="mm_dh2_gate")
    dh2 = _matmul(dup, wt["w_up"], "nn", res=dh2, name="mm_dh2_up")
    dx1, gr["norm2_g"] = _rmsnorm_bwd(sv["x1"], wt["norm2_g"], dh2, dx2, name="rmsnorm_bwd")
    big["w_out"] = _matmul(sv["mixed"], dx1, "tn", out_dtype=BF16, name="mm_dout").reshape(N_DEV, rows, D_MODEL)
    dmixed = _matmul(dx1, wt["w_out"], "nt", name="mm_dmixed", after=emit(1, big))
    proj = sv["proj"]
    dcb, dcc, dch, dsconv = _sconv_bwd(proj, wt["sconv_w"], dmixed, name="sconv_bwd")
    big["sconv_w"] = _col_blocks(dsconv)
    dhp, dwbd, gr["pool_scale"] = _pool_bwd(proj, wt["pool_win"], wt["pool_wbd"], wt["pool_scale"], dmixed, name="pool_bwd")
    half = LANE // 2
    gr["pool_w"] = jnp.stack([dwbd[0, :half, :half], dwbd[0, half:, half:], dwbd[1, :half, :half], dwbd[1, half:, half:]])
    do, dz, gr["onorm_g"] = _apost_bwd(sv["o"], proj, wt["onorm_g"], dmixed, name="apost_bwd")
    dvn, dstates = _deltanet_bscan(sv["w"], sv["qg"], sv["kg"], sv["attn"], sv["g"], do, name="deltanet_bscan")
    dqkv_h, dg, dbeta = _deltanet_post(sv["qkv"], sv["g"], sv["beta"], sv["tmats"], sv["states"], dstates, do, dvn, sv["vn"],
                                       name="deltanet_post")
    dab, dalog, ddtb = _gates_bwd(proj, wt["a_log"], wt["dt_bias"], dg, dbeta, name="gates_bwd")
    gr["a_log"], gr["dt_bias"] = dalog[0, :HEADS], ddtb[0, :HEADS]
    dqkv, dconv = _qkv_prep_bwd(proj, wt["conv_qkv"], dqkv_h, name="qkv_prep_bwd")
    big["conv_qkv"] = _col_blocks(dconv)
    dproj = jnp.concatenate([dqkv, dz, dab, dhp, dcb, dcc, dch], axis=1)
    dwin = _matmul(dproj, sv["h1"], "tn", out_dtype=BF16, name="mm_din")
    big["w_in"] = jnp.concatenate([dwin[:AB_COL + 2 * HEADS], dwin[AB_COL + LANE:]], axis=0).reshape(N_DEV, -1, D_MODEL)
    dh1 = _matmul(dproj, wt["w_in"], "nn", name="mm_dh1", after=emit(2, big))
    dx, gr["norm1_g"] = _rmsnorm_bwd(sv["x"], wt["norm1_g"], dh1, dx1, name="rmsnorm_bwd")
    return dx, gr


FETCH_GROUPS = (("w_in", "conv_qkv", "sconv_w"), ("w_out",), ("w_gate", "w_up"), ("w_down",), ("ple_gate", "ple_proj"))
EMIT_GROUPS = (("ple_proj", "ple_gate", "w_down"), ("w_gate", "w_up", "w_out"), ("w_in", "conv_qkv", "sconv_w"))


def _small_weights(w, i):
    return dict(
        norm1_g=w["norm1_g"][i][None], norm2_g=w["norm2_g"][i][None], onorm_g=w["onorm_g"][i][None],
        a_log=_pad_lanes(w["a_log"][i]), dt_bias=_pad_lanes(w["dt_bias"][i]),
        pool_scale=w["pool_scale"][i][None], pool_win=_pool_windows(), pool_wbd=_block_diag_pairs(w["pool_w"][i]))


def _as_read(name, gathered):
    if name == "w_in":
        rows = gathered.reshape(-1, D_MODEL)
        return jnp.concatenate([rows[:AB_COL + 2 * HEADS], jnp.zeros((LANE - 2 * HEADS, D_MODEL), BF16),
                                rows[AB_COL + 2 * HEADS:]], axis=0)
    if name in ("conv_qkv", "sconv_w"):
        return _cols_joined(gathered)
    if name == "ple_proj":
        return gathered
    return gathered.reshape(-1, D_MODEL)


def _layer_weights(gathered, w, i):
    return {**_small_weights(w, i), **{k: _as_read(k, g) for k, g in gathered.items()}}


def _local_step(x, p, target, layers, final_g):
    saved = []
    h = x
    for i in range(DEPTH):
        replicated = {k: v for k, v in layers[i].items() if k not in SHARDED}
        h, sv = _layer_fwd(h, p[i], replicated, lambda group, after, i=i: {k: layers[i][k] for k in FETCH_GROUPS[group]})
        saved.append(sv)
    dx, dgf, loss = _loss_head(h, final_g, target, name="loss_head")
    big, small = [{} for _ in range(DEPTH)], [None] * DEPTH
    for i in reversed(range(DEPTH)):
        dx, small[i] = _layer_bwd(dx, saved[i], lambda group, blocks, i=i: big[i].update({k: blocks[k] for k in EMIT_GROUPS[group]}))
    return loss, dx, big, small, dgf


SHARDED = ("w_in", "w_gate", "w_up", "w_down", "w_out", "ple_gate", "ple_proj", "conv_qkv", "sconv_w")
SMALL = ("norm1_g", "a_log", "dt_bias", "onorm_g", "pool_w", "pool_scale", "norm2_g", "final_g")
SLAB_COLS = 1024


def _payload(name, shard):
    if name in ("conv_qkv", "sconv_w"):
        return shard
    out = shard.astype(BF16)
    if name in ("w_gate", "w_up", "w_down"):
        out = jnp.pad(out, ((0, FF_BLOCK - FF_SHARD), (0, 0)))
    return out


TRANSPOSED = ("w_in", "w_gate", "w_up")


def _ff_rows(t):
    return jnp.transpose(t, (0, 2, 1))


def _slab_rows(shape):
    size = 1
    for s in shape:
        size *= s
    return SUBLANE * -(-size // (SUBLANE * SLAB_COLS))


def _pack_slab(parts, extra_row):
    rows = []
    for name in SMALL:
        flat = parts[name].reshape(-1)
        nrow = _slab_rows(parts[name].shape)
        rows.append(jnp.pad(flat, (0, nrow * SLAB_COLS - flat.shape[0])).reshape(nrow, SLAB_COLS))
    rows.append(jnp.pad(extra_row, ((0, SUBLANE - 1), (0, 0))))
    return jnp.concatenate(rows, axis=0)


def _unpack_slab(slab, shapes):
    out, row = {}, 0
    for name in SMALL:
        size = 1
        for s in shapes[name]:
            size *= s
        out[name] = slab[row:row + _slab_rows(shapes[name])].reshape(-1)[:size].reshape(shapes[name])
        row += _slab_rows(shapes[name])
    return out, row


def kernel(x, p, norm1_g, w_in, conv_qkv, a_log, dt_bias, onorm_g, pool_w, pool_scale, sconv_w, w_out, norm2_g, w_gate, w_up, w_down, ple_proj, ple_gate, final_g, loss_target, m_norm1_g, m_w_in, m_conv_qkv, m_a_log, m_dt_bias, m_onorm_g, m_pool_w, m_pool_scale, m_sconv_w, m_w_out, m_norm2_g, m_w_gate, m_w_up, m_w_down, m_ple_proj, m_ple_gate, m_final_g, v_norm1_g, v_w_in, v_conv_qkv, v_a_log, v_dt_bias, v_onorm_g, v_pool_w, v_pool_scale, v_sconv_w, v_w_out, v_norm2_g, v_w_gate, v_w_up, v_w_down, v_ple_proj, v_ple_gate, v_final_g):
    names = ["norm1_g", "w_in", "conv_qkv", "a_log", "dt_bias", "onorm_g", "pool_w", "pool_scale", "sconv_w", "w_out", "norm2_g",
             "w_gate", "w_up", "w_down", "ple_proj", "ple_gate", "final_g"]
    w = dict(zip(names, [norm1_g, w_in, conv_qkv, a_log, dt_bias, onorm_g, pool_w, pool_scale, sconv_w, w_out, norm2_g, w_gate, w_up,
                         w_down, ple_proj, ple_gate, final_g]))
    m = dict(zip(names, [m_norm1_g, m_w_in, m_conv_qkv, m_a_log, m_dt_bias, m_onorm_g, m_pool_w, m_pool_scale, m_sconv_w, m_w_out,
                         m_norm2_g, m_w_gate, m_w_up, m_w_down, m_ple_proj, m_ple_gate, m_final_g]))
    v = dict(zip(names, [v_norm1_g, v_w_in, v_conv_qkv, v_a_log, v_dt_bias, v_onorm_g, v_pool_w, v_pool_scale, v_sconv_w, v_w_out,
                         v_norm2_g, v_w_gate, v_w_up, v_w_down, v_ple_proj, v_ple_gate, v_final_g]))
    for group in (w, m, v):
        group.update({k: _ff_rows(group[k]) for k in TRANSPOSED})

    first, rest = FETCH_GROUPS[0], tuple(k for members in FETCH_GROUPS[1:] for k in members)
    gathered = dict(zip(first, _all_gather([_payload(k, w[k][0]) for k in first], name="all_gather_weights")))
    (flying0,), token = _exchange_start([[_payload(k, w[k][0]) for k in rest]], CHIP_GATHER, name="gather_start_0")
    replicated = [_small_weights(w, i) for i in range(DEPTH)]
    replicated[0]["norm1_g"] = replicated[0]["norm1_g"] + token[0, 0]
    flying1 = []

    def fetch(i, group, after):
        if i == 0 and group == 1:
            landed = _exchange_wait(flying0, after, CHIP_GATHER, name="gather_wait_0")
            gathered.update(zip(rest, _pair_swap(landed, name="pair_swap")))
            started, token = _exchange_start([[_payload(k, w[k][1]) for k in SHARDED]], CHIP_GATHER, name="gather_start_1",
                                             after=gathered[rest[0]])
            flying1.extend(started)
            return {**{k: _as_read(k, gathered[k]) for k in FETCH_GROUPS[group]},
                    "conv_qkv": _as_read("conv_qkv", gathered["conv_qkv"]) + token[0, 0]}
        if i == 1 and group == 0:
            landed = _exchange_wait(flying1[0], after, CHIP_GATHER, name="gather_wait_1")
            gathered.update(zip(SHARDED, _pair_swap(landed, name="pair_swap")))
        return {k: _as_read(k, gathered[k]) for k in FETCH_GROUPS[group]}

    def reduce_scatter_start(members, blocks, tag):
        mine = [blocks[k] for k in members]
        theirs = _pair_exchange(mine, name="pair_exchange")
        sums = [_pair_add(a, b, name="pair_add") for a, b in zip(mine, theirs)]
        (started,), token = _exchange_start([sums], CHIP_SCATTER, name="exchange_start_" + tag)
        return started, token

    h, saved0 = _layer_fwd(x[0], p[0, 0], replicated[0], functools.partial(fetch, 0))
    h, saved1 = _layer_fwd(h, p[1, 0], replicated[1], functools.partial(fetch, 1))
    dx, dgf, loss_part = _loss_head(h, final_g[None], loss_target[0], name="loss_head")
    small, big1, flying0 = [None] * DEPTH, {}, []
    dx, small[1] = _layer_bwd(dx, saved1, lambda group, blocks: big1.update({k: blocks[k] for k in EMIT_GROUPS[group]}))
    flying1, token = reduce_scatter_start(SHARDED, big1, "1")

    def emit(group, blocks):
        started, token = reduce_scatter_start(EMIT_GROUPS[group], blocks, f"0_{group}")
        flying0.append(started)
        return token

    dx, small[0] = _layer_bwd(dx, saved0, emit, after=token)
    received = [{}, dict(zip(SHARDED, _exchange_wait(flying1, dx, CHIP_SCATTER, name="exchange_wait_1")))]
    for group, members in enumerate(EMIT_GROUPS):
        received[0].update(zip(members, _exchange_wait(flying0[group], dx, CHIP_SCATTER, name=f"exchange_wait_0_{group}")))

    grads = {k: jnp.stack([small[i][k] for i in range(DEPTH)]) for k in small[0]}
    grads = {k: g[:, 0] if k in ("norm1_g", "norm2_g", "onorm_g", "pool_scale") else g for k, g in grads.items()}
    grads["final_g"] = dgf[0]
    loss_row = jnp.pad(loss_part, ((0, 0), (0, SLAB_COLS - LANE)))
    (small_flying,), token = _exchange_start([[_pack_slab(grads, loss_row)]], GATHER, name="small_gather_start")

    out_g, out_d, out_m, out_v = {}, {}, {}, {}
    for k in SHARDED:
        out_g[k], out_d[k], out_m[k], out_v[k] = _adamw_reduce(w[k], [received[i][k] for i in range(DEPTH)], m[k], v[k],
                                                                name="adamw_" + k, after=token)
    behind_all = jnp.stack([out_v[k][0, 0, 0] for k in SHARDED])
    (small_parts,) = _exchange_wait(small_flying, behind_all, GATHER, name="small_gather_wait")
    zero_row = jnp.zeros((1, SLAB_COLS), F32)
    slabs = _adamw_reduce(_pack_slab(w, zero_row)[None], [small_parts], _pack_slab(m, zero_row)[None],
                          _pack_slab(v, zero_row)[None], name="adamw_small")
    slabs = [s[0] for s in slabs]
    shapes = {k: w[k].shape for k in SMALL}
    for dst, slab in zip((out_g, out_d, out_m, out_v), slabs):
        vals, _ = _unpack_slab(slab, shapes)
        dst.update(vals)
    _, loss_at = _unpack_slab(slabs[0], shapes)
    loss = slabs[0][loss_at, 0]
    for group in (out_g, out_d, out_m, out_v):
        group.update({k: _ff_rows(group[k]) for k in TRANSPOSED})

    return (loss, dx[None], *[out_g[k] for k in names], *[out_d[k] for k in names], *[out_m[k] for k in names],
            *[out_v[k] for k in names])
```

```python
import functools

import jax
import jax.numpy as jnp
from jax import lax
from jax.experimental import pallas as pl
from jax.experimental.pallas import tpu as pltpu

F32 = jnp.float32
BF16 = jnp.bfloat16

D_MODEL = 1024
DEPTH = 2
PLE_DIM = 256
EPS = 1e-6
HEAD_DIM = 128
HEADS = 4
A_DIM = HEADS * HEAD_DIM
QKV_TAPS = 4
CHUNK = 64
POOL_WINDOWS = (2, 4, 8, 16)
POOL_DIM = 256
CONV_DIM = 256
CONV_TAPS = 3
D_FF = 2816
D_IN = 3080
D_IN_PAD = 3200
AB_COL = 2048
N_DEV = 8

ADAM_LR = 0.001
ADAM_B1 = 0.9
ADAM_B2 = 0.999
ADAM_EPS = 1e-08
ADAM_WD = 0.01
ADAM_STEP = 10

LANE = 128
SUBLANE = 8
VMEM_BYTES_V7X = 64 * 1024 * 1024
VMEM_LIMIT = 48 * 1024 * 1024

_HI = lax.Precision.HIGHEST
NN = ((1,), (0,))
NT = ((1,), (1,))
TN = ((0,), (0,))
MESH = pl.DeviceIdType.MESH


def _dot(a, b, dims, hi=False):
    if hi:
        return lax.dot_general(a, b, (dims, ((), ())), precision=_HI, preferred_element_type=F32)
    return lax.dot_general(a.astype(BF16), b.astype(BF16), (dims, ((), ())), preferred_element_type=F32)


def _pcall(body, *, name, out_shape, grid=(), in_specs=None, out_specs=None, scratch_shapes=(), semantics=None,
           vmem_limit=None, after=None, **kw):
    params = {}
    if semantics is not None:
        params["dimension_semantics"] = semantics
    if vmem_limit is not None:
        params["vmem_limit_bytes"] = vmem_limit
    if after is not None:
        n_in, inner = len(in_specs), body
        body = lambda *refs: inner(*refs[:n_in], *refs[n_in + 1:])
        in_specs = list(in_specs) + [pl.BlockSpec(after.shape, lambda *_: (0,) * after.ndim)]
    call = pl.pallas_call(
        body, name=name, out_shape=out_shape, grid=grid, in_specs=in_specs, out_specs=out_specs,
        scratch_shapes=list(scratch_shapes), compiler_params=pltpu.CompilerParams(**params), **kw)
    return call if after is None else (lambda *args: call(*args, after))


def _sigmoid(x):
    return 1.0 / (1.0 + jnp.exp(-x))


def _softplus(x):
    return jnp.maximum(x, 0.0) + jnp.log(1.0 + jnp.exp(-jnp.abs(x)))


def _tile(n, cap, mult):
    if n <= cap:
        return n
    best = None
    for t in range(mult, cap + 1, mult):
        if n % t == 0:
            best = t
    assert best is not None, (n, cap, mult)
    return best


ROWS_PER_STEP = 512
NARROW_RESULT = 1024
COLS_PER_DOT = 640


def _resident(weight):
    return pl.BlockSpec(weight.shape, lambda i: (0,) * weight.ndim, pipeline_mode=pl.Buffered(1))


def _matmul_rows(a, b, mode, *, name, res=None, out_dtype=F32, b_blocked=False, after=None):
    m, k = a.shape
    if b_blocked:
        nb, _, bw = b.shape
        n = nb * bw if mode == "nn" else b.shape[1]
    else:
        n = b.shape[1] if mode == "nn" else b.shape[0]
    tm = _tile(m, ROWS_PER_STEP if n > NARROW_RESULT else 2 * ROWS_PER_STEP, 16)
    cn = bw if (b_blocked and mode == "nn") else _tile(n, COLS_PER_DOT, LANE)
    has_res = res is not None

    def body(*refs):
        a_ref, b_ref = refs[0], refs[1]
        res_ref = refs[2] if has_res else None
        o_ref = refs[2 + has_res]
        if not (b_blocked and mode == "nt"):
            av = a_ref[...].astype(BF16)
        for j in range(n // cn):
            cols = pl.ds(j * cn, cn)
            if mode == "nn":
                part = _dot(av, b_ref[j] if b_blocked else b_ref[:, cols], NN)
            elif not b_blocked:
                part = _dot(av, b_ref[cols, :], NT)
            else:
                part = None
                for s in range(nb):
                    term = _dot(a_ref[:, pl.ds(s * bw, bw)], b_ref[s, cols, :], NT)
                    part = term if part is None else part + term
            if has_res:
                part = part + res_ref[:, cols]
            o_ref[:, cols] = part.astype(o_ref.dtype)

    row = lambda width: pl.BlockSpec((tm, width), lambda i: (i, 0))
    whole = _resident(b)
    ins = [a, b] + ([res] if has_res else [])
    specs = [row(k), whole] + ([row(n)] if has_res else [])
    return _pcall(body, name=name, out_shape=jax.ShapeDtypeStruct((m, n), out_dtype), grid=(m // tm,), in_specs=specs,
                  out_specs=row(n), semantics=("parallel",), vmem_limit=VMEM_LIMIT, after=after)(*ins)


def _matmul(a, b, mode, *, name, res=None, out_dtype=F32, b_blocked=False, out_blocked=None, after=None):
    if mode != "tn":
        return _matmul_rows(a, b, mode, name=name, res=res, out_dtype=out_dtype, b_blocked=b_blocked, after=after)
    assert res is None and not b_blocked and after is None
    (t, m), (t2, n) = a.shape, b.shape
    assert t == t2, (a.shape, b.shape)
    tm = _tile(m, 1024, LANE)
    tn = _tile(n, COLS_PER_DOT, LANE)
    if out_blocked is not None:
        assert out_blocked[0] * out_blocked[1] == n
        tn = out_blocked[1]

    def body(a_ref, b_ref, o_ref):
        part = _dot(a_ref[...], b_ref[...], TN).astype(o_ref.dtype)
        if out_blocked is None:
            o_ref[...] = part
        else:
            o_ref[0] = part

    o_spec = (pl.BlockSpec((tm, tn), lambda i, j: (i, j)) if out_blocked is None
              else pl.BlockSpec((1, tm, tn), lambda i, j: (j, i, 0)))
    o_shape = (m, n) if out_blocked is None else (out_blocked[0], m, out_blocked[1])
    return _pcall(body, name=name, out_shape=jax.ShapeDtypeStruct(o_shape, out_dtype), grid=(m // tm, n // tn),
                  in_specs=[pl.BlockSpec((t, tm), lambda i, j: (0, i)), pl.BlockSpec((t, tn), lambda i, j: (0, j))],
                  out_specs=o_spec, semantics=("parallel", "parallel"), vmem_limit=VMEM_LIMIT)(a, b)


ROW_TILE = 512


def _rows(t, width, idx=0):
    return pl.BlockSpec((ROW_TILE, width), lambda i: (i, idx))


def _vec(width):
    return pl.BlockSpec((1, width), lambda i: (0, 0))


def _rmsnorm_fwd(x, g, *, name):
    t, d = x.shape

    def body(x_ref, g_ref, h_ref):
        xv = x_ref[...]
        r = lax.rsqrt(jnp.mean(xv * xv, axis=-1, keepdims=True) + EPS)
        h_ref[...] = (xv * r * g_ref[...]).astype(BF16)

    return _pcall(body, name=name, out_shape=jax.ShapeDtypeStruct((t, d), BF16), grid=(t // ROW_TILE,),
                  in_specs=[_rows(t, d), _vec(d)], out_specs=_rows(t, d), semantics=("parallel",))(x, g)


def _rmsnorm_bwd(x, g, dh, dres, *, name):
    t, d = x.shape

    def body(x_ref, g_ref, dh_ref, dres_ref, dx_ref, dg_ref):
        xv = x_ref[...]
        r = lax.rsqrt(jnp.mean(xv * xv, axis=-1, keepdims=True) + EPS)
        xhat = xv * r
        dhv = dh_ref[...].astype(F32)
        dhg = dhv * g_ref[...]
        dx_ref[...] = dres_ref[...] + r * (dhg - xhat * jnp.mean(dhg * xhat, axis=-1, keepdims=True))
        part = jnp.sum(dhv * xhat, axis=0, keepdims=True)

        @pl.when(pl.program_id(0) == 0)
        def _():
            dg_ref[...] = part

        @pl.when(pl.program_id(0) > 0)
        def _():
            dg_ref[...] += part

    return _pcall(body, name=name, out_shape=(jax.ShapeDtypeStruct((t, d), F32), jax.ShapeDtypeStruct((1, d), F32)),
                  grid=(t // ROW_TILE,), in_specs=[_rows(t, d), _vec(d), _rows(t, d), _rows(t, d)],
                  out_specs=(_rows(t, d), _vec(d)), semantics=("arbitrary",))(x, g, dh, dres)


def _swiglu_fwd(h, w_gate, w_up, *, name):
    t, k = h.shape
    f = w_gate.shape[0]
    tm = _tile(t, ROWS_PER_STEP, 16)
    cn = _tile(f, COLS_PER_DOT, LANE)

    def body(h_ref, wg_ref, wu_ref, ff_ref, gate_ref, up_ref):
        hv = h_ref[...]
        for j in range(f // cn):
            cols = pl.ds(j * cn, cn)
            gv = _dot(hv, wg_ref[cols, :], NT)
            uv = _dot(hv, wu_ref[cols, :], NT)
            gate_ref[:, cols] = gv.astype(BF16)
            up_ref[:, cols] = uv.astype(BF16)
            ff_ref[:, cols] = (gv * _sigmoid(gv) * uv).astype(BF16)

    row = lambda width: pl.BlockSpec((tm, width), lambda i: (i, 0))
    out = jax.ShapeDtypeStruct((t, f), BF16)
    return _pcall(body, name=name, out_shape=(out,) * 3, grid=(t // tm,), in_specs=[row(k), _resident(w_gate), _resident(w_up)],
                  out_specs=(row(f),) * 3, semantics=("parallel",), vmem_limit=VMEM_LIMIT)(h, w_gate, w_up)


def _swiglu_bwd(dx2, w_down, gate, up, *, name, after=None):
    t, d = dx2.shape
    f = w_down.shape[0]
    tm = _tile(t, ROWS_PER_STEP, 16)
    cn = _tile(f, COLS_PER_DOT, LANE)

    def body(dx_ref, w_ref, gate_ref, up_ref, dgate_ref, dup_ref):
        dxv = dx_ref[...].astype(BF16)
        for j in range(f // cn):
            cols = pl.ds(j * cn, cn)
            dffv = _dot(dxv, w_ref[cols, :], NT)
            gv = gate_ref[:, cols].astype(F32)
            sig = _sigmoid(gv)
            dgate_ref[:, cols] = (dffv * up_ref[:, cols].astype(F32) * sig * (1.0 + gv * (1.0 - sig))).astype(BF16)
            dup_ref[:, cols] = (dffv * gv * sig).astype(BF16)

    row = lambda width: pl.BlockSpec((tm, width), lambda i: (i, 0))
    out = jax.ShapeDtypeStruct((t, f), BF16)
    return _pcall(body, name=name, out_shape=(out, out), grid=(t // tm,), in_specs=[row(d), _resident(w_down), row(f), row(f)],
                  out_specs=(row(f), row(f)), semantics=("parallel",), vmem_limit=VMEM_LIMIT, after=after)(dx2, w_down, gate, up)


def _ple_fwd(x2, pgl, pp, *, name):
    t, d = x2.shape

    def body(x_ref, pgl_ref, pp_ref, o_ref):
        o_ref[...] = x_ref[...] + _sigmoid(pgl_ref[...]) * pp_ref[...]

    return _pcall(body, name=name, out_shape=jax.ShapeDtypeStruct((t, d), F32), grid=(t // ROW_TILE,),
                  in_specs=[_rows(t, d)] * 3, out_specs=_rows(t, d), semantics=("parallel",))(x2, pgl, pp)


def _ple_bwd(dx3, pgl, pp, *, name, after=None):
    t, d = dx3.shape

    def body(dx_ref, pgl_ref, pp_ref, dpgl_ref, dpp_ref):
        dxv = dx_ref[...]
        sig = _sigmoid(pgl_ref[...])
        dpp_ref[...] = (dxv * sig).astype(BF16)
        dpgl_ref[...] = (dxv * pp_ref[...] * sig * (1.0 - sig)).astype(BF16)

    return _pcall(body, name=name, out_shape=(jax.ShapeDtypeStruct((t, d), BF16),) * 2, grid=(t // ROW_TILE,),
                  in_specs=[_rows(t, d)] * 3, out_specs=(_rows(t, d),) * 2, semantics=("parallel",), after=after)(dx3, pgl, pp)


def _loss_head(x3, g, target, *, name):
    t, d = x3.shape

    def body(x_ref, g_ref, t_ref, dx_ref, dg_ref, loss_ref):
        xv = x_ref[...]
        r = lax.rsqrt(jnp.mean(xv * xv, axis=-1, keepdims=True) + EPS)
        xhat = xv * r
        gv = g_ref[...]
        err = xhat * gv - t_ref[...]
        row_loss = jnp.sum(err * err, axis=-1, keepdims=True) * (0.5 / d)
        lpart = jnp.broadcast_to(jnp.sum(row_loss, axis=0, keepdims=True), (1, LANE))
        dy = err * (1.0 / d)
        dyg = dy * gv
        dx_ref[...] = r * (dyg - xhat * jnp.mean(dyg * xhat, axis=-1, keepdims=True))
        gpart = jnp.sum(dy * xhat, axis=0, keepdims=True)

        @pl.when(pl.program_id(0) == 0)
        def _():
            dg_ref[...] = gpart
            loss_ref[...] = lpart

        @pl.when(pl.program_id(0) > 0)
        def _():
            dg_ref[...] += gpart
            loss_ref[...] += lpart

    return _pcall(body, name=name,
                  out_shape=(jax.ShapeDtypeStruct((t, d), F32), jax.ShapeDtypeStruct((1, d), F32), jax.ShapeDtypeStruct((1, LANE), F32)),
                  grid=(t // ROW_TILE,), in_specs=[_rows(t, d), _vec(d), _rows(t, d)],
                  out_specs=(_rows(t, d), _vec(d), _vec(LANE)), semantics=("arbitrary",))(x3, g, target)


def _shift_down(x, d):
    if d == 0:
        return x
    row = lax.broadcasted_iota(jnp.int32, x.shape, 0)
    return jnp.where(row >= d, pltpu.roll(x, d, 0), 0.0)


def _shift_up(x, d):
    if d == 0:
        return x
    t = x.shape[0]
    row = lax.broadcasted_iota(jnp.int32, x.shape, 0)
    return jnp.where(row < t - d, pltpu.roll(x, t - d, 0), 0.0)


def _colsum(x):
    return jnp.sum(x, axis=0, keepdims=True)


def _col(t, idx_fn):
    return pl.BlockSpec((t, LANE), idx_fn)


def _conv_fwd(x, w_ref, taps):
    acc = None
    for j in range(taps):
        term = w_ref[pl.ds(j, 1), :] * _shift_down(x, taps - 1 - j)
        acc = term if acc is None else acc + term
    return acc


def _conv_bwd(x, dy, w_ref, dw_ref, taps):
    dx = None
    for j in range(taps):
        term = w_ref[pl.ds(j, 1), :] * _shift_up(dy, taps - 1 - j)
        dx = term if dx is None else dx + term
        dw_ref[pl.ds(j, 1), :] = _colsum(dy * _shift_down(x, taps - 1 - j))
    return dx


def _qkv_prep_fwd(proj, conv_w, *, name):
    t = proj.shape[0]
    scale = HEAD_DIM ** -0.5

    def body(x_ref, w_ref, o_ref):
        j = pl.program_id(0)
        c = _conv_fwd(x_ref[...], w_ref, QKV_TAPS)
        s = c * _sigmoid(c)
        r = lax.rsqrt(jnp.sum(s * s, axis=-1, keepdims=True) + EPS)
        f = jnp.where(j < 2 * HEADS, r, 1.0) * jnp.where(j < HEADS, scale, 1.0)
        o_ref[0] = s * f

    return _pcall(body, name=name, out_shape=jax.ShapeDtypeStruct((3 * HEADS, t, LANE), F32), grid=(3 * HEADS,),
                  in_specs=[_col(t, lambda j: (0, j)), pl.BlockSpec((QKV_TAPS, LANE), lambda j: (0, j))],
                  out_specs=pl.BlockSpec((1, t, LANE), lambda j: (j, 0, 0)), semantics=("parallel",),
                  vmem_limit=VMEM_LIMIT)(proj, conv_w)


def _qkv_prep_bwd(proj, conv_w, dqkv, *, name):
    t = proj.shape[0]
    scale = HEAD_DIM ** -0.5

    def body(x_ref, w_ref, d_ref, dx_ref, dw_ref):
        j = pl.program_id(0)
        xv = x_ref[...]
        c = _conv_fwd(xv, w_ref, QKV_TAPS)
        sig = _sigmoid(c)
        s = c * sig
        r = lax.rsqrt(jnp.sum(s * s, axis=-1, keepdims=True) + EPS)
        n0 = s * r
        dv = d_ref[0]
        dn0 = dv * jnp.where(j < HEADS, scale, 1.0)
        ds_norm = r * (dn0 - n0 * jnp.sum(dn0 * n0, axis=-1, keepdims=True))
        ds = jnp.where(j < 2 * HEADS, ds_norm, dv)
        dc = ds * sig * (1.0 + c * (1.0 - sig))
        dx_ref[...] = _conv_bwd(xv, dc, w_ref, dw_ref, QKV_TAPS).astype(BF16)

    return _pcall(body, name=name,
                  out_shape=(jax.ShapeDtypeStruct((t, 3 * A_DIM), BF16), jax.ShapeDtypeStruct((QKV_TAPS, 3 * A_DIM), F32)),
                  grid=(3 * HEADS,),
                  in_specs=[_col(t, lambda j: (0, j)), pl.BlockSpec((QKV_TAPS, LANE), lambda j: (0, j)),
                            pl.BlockSpec((1, t, LANE), lambda j: (j, 0, 0))],
                  out_specs=(_col(t, lambda j: (0, j)), pl.BlockSpec((QKV_TAPS, LANE), lambda j: (0, j))),
                  semantics=("parallel",), vmem_limit=VMEM_LIMIT)(proj, conv_w, dqkv)


def _lane_pick(x, lane_idx, lane):
    return jnp.broadcast_to(jnp.sum(jnp.where(lane == lane_idx, x, 0.0), axis=-1, keepdims=True), x.shape)


def _gates_fwd(proj, alog, dtb, *, name):
    t = proj.shape[0]

    def body(x_ref, alog_ref, dtb_ref, g_ref, b_ref):
        xv = x_ref[...]
        lane = lax.broadcasted_iota(jnp.int32, xv.shape, 1)
        gall = -jnp.exp(alog_ref[...]) * _softplus(xv + dtb_ref[...])
        ball = _sigmoid(xv)
        for h in range(HEADS):
            g_ref[h] = _lane_pick(gall, h, lane)
            b_ref[h] = _lane_pick(ball, HEADS + h, lane)

    out = jax.ShapeDtypeStruct((HEADS, t, LANE), F32)
    whole = pl.BlockSpec((HEADS, t, LANE), lambda i: (0, 0, 0))
    return _pcall(body, name=name, out_shape=(out, out), grid=(1,),
                  in_specs=[_col(t, lambda i: (0, AB_COL // LANE)), _vec(LANE), _vec(LANE)], out_specs=(whole, whole),
                  semantics=("arbitrary",), vmem_limit=VMEM_LIMIT)(proj, alog, dtb)


def _gates_bwd(proj, alog, dtb, dg, dbeta, *, name):
    t = proj.shape[0]

    def body(x_ref, alog_ref, dtb_ref, dg_ref, db_ref, dab_ref, dalog_ref, ddtb_ref):
        xv = x_ref[...]
        lane = lax.broadcasted_iota(jnp.int32, xv.shape, 1)
        lane1 = lax.broadcasted_iota(jnp.int32, (1, LANE), 1)
        z = xv + dtb_ref[...]
        nea = -jnp.exp(alog_ref[...])
        da_f = nea * _sigmoid(z)
        g_f = nea * _softplus(z)
        ball = _sigmoid(xv)
        db_f = ball * (1.0 - ball)
        dab = jnp.zeros_like(xv)
        dalog = jnp.zeros((1, LANE), F32)
        for h in range(HEADS):
            dgh = dg_ref[h]
            dab = dab + jnp.where(lane == h, dgh * da_f, 0.0) + jnp.where(lane == HEADS + h, db_ref[h] * db_f, 0.0)
            dalog = dalog + jnp.where(lane1 == h, _colsum(dgh * g_f), 0.0)
        dab_ref[...] = dab.astype(BF16)
        dalog_ref[...] = dalog
        ddtb_ref[...] = jnp.where(lane1 < HEADS, _colsum(dab), 0.0)

    whole = pl.BlockSpec((HEADS, t, LANE), lambda i: (0, 0, 0))
    vec = jax.ShapeDtypeStruct((1, LANE), F32)
    return _pcall(body, name=name, out_shape=(jax.ShapeDtypeStruct((t, LANE), BF16), vec, vec), grid=(1,),
                  in_specs=[_col(t, lambda i: (0, AB_COL // LANE)), _vec(LANE), _vec(LANE), whole, whole],
                  out_specs=(_col(t, lambda i: (0, 0)), _vec(LANE), _vec(LANE)), semantics=("arbitrary",),
                  vmem_limit=VMEM_LIMIT)(proj, alog, dtb, dg, dbeta)


Z_COL = 3 * A_DIM // LANE


def _apost_fwd(o, proj, gn, *, name):
    t = proj.shape[0]

    def body(o_ref, z_ref, gn_ref, y_ref):
        ov = o_ref[0]
        z = z_ref[...]
        r = lax.rsqrt(jnp.mean(ov * ov, axis=-1, keepdims=True) + EPS)
        y_ref[...] = (ov * r * gn_ref[...] * (z * _sigmoid(z))).astype(BF16)

    return _pcall(body, name=name, out_shape=jax.ShapeDtypeStruct((t, A_DIM), BF16), grid=(HEADS,),
                  in_specs=[pl.BlockSpec((1, t, LANE), lambda h: (h, 0, 0)), _col(t, lambda h: (0, Z_COL + h)),
                            pl.BlockSpec((1, LANE), lambda h: (0, 0))],
                  out_specs=_col(t, lambda h: (0, h)), semantics=("parallel",), vmem_limit=VMEM_LIMIT)(o, proj, gn)


def _apost_bwd(o, proj, gn, dmixed, *, name):
    t = proj.shape[0]

    def body(o_ref, z_ref, gn_ref, d_ref, do_ref, dz_ref, dgn_ref):
        ov = o_ref[0]
        z = z_ref[...]
        gnv = gn_ref[...]
        dv = d_ref[...]
        r = lax.rsqrt(jnp.mean(ov * ov, axis=-1, keepdims=True) + EPS)
        ohat = ov * r
        sig = _sigmoid(z)
        dy = dv * (z * sig)
        dz_ref[...] = (dv * ohat * gnv * sig * (1.0 + z * (1.0 - sig))).astype(BF16)
        dyo = dy * gnv
        do_ref[0] = r * (dyo - ohat * jnp.mean(dyo * ohat, axis=-1, keepdims=True))
        part = _colsum(dy * ohat)

        @pl.when(pl.program_id(0) == 0)
        def _():
            dgn_ref[...] = part

        @pl.when(pl.program_id(0) > 0)
        def _():
            dgn_ref[...] += part

    return _pcall(body, name=name,
                  out_shape=(jax.ShapeDtypeStruct((HEADS, t, LANE), F32), jax.ShapeDtypeStruct((t, A_DIM), BF16),
                             jax.ShapeDtypeStruct((1, LANE), F32)),
                  grid=(HEADS,),
                  in_specs=[pl.BlockSpec((1, t, LANE), lambda h: (h, 0, 0)), _col(t, lambda h: (0, Z_COL + h)),
                            pl.BlockSpec((1, LANE), lambda h: (0, 0)), _col(t, lambda h: (0, h))],
                  out_specs=(pl.BlockSpec((1, t, LANE), lambda h: (h, 0, 0)), _col(t, lambda h: (0, h)),
                             pl.BlockSpec((1, LANE), lambda h: (0, 0))),
                  semantics=("arbitrary",), vmem_limit=VMEM_LIMIT)(o, proj, gn, dmixed)


POOL_COL = (AB_COL + LANE) // LANE
CB_COL = POOL_COL + POOL_DIM // LANE
CC_COL = CB_COL + CONV_DIM // LANE
CH_COL = CC_COL + CONV_DIM // LANE
MAX_WIN_LOG2 = 4


def _window_sums(x, shift):
    sums = []
    cur = x
    for k in range(MAX_WIN_LOG2):
        cur = cur + shift(cur, 1 << k)
        sums.append(cur)
    return sums


def _pick_window(sums, win):
    out = sums[-1]
    for k in range(MAX_WIN_LOG2 - 2, -1, -1):
        out = jnp.where(win == float(2 << k), sums[k], out)
    return out


def _pool_counts(shape, win):
    row = lax.broadcasted_iota(jnp.int32, shape, 0).astype(F32)
    return jnp.minimum(row + 1.0, win)


def _pool_fwd(proj, win, wbd, scale, *, name):
    t = proj.shape[0]

    def body(x_ref, win_ref, w_ref, s_ref, y_ref):
        xv = x_ref[...]
        winv = win_ref[...]
        pooled = _pick_window(_window_sums(xv, _shift_down), winv) / _pool_counts(xv.shape, winv) - xv
        y_ref[...] = (_dot(pooled, w_ref[0], NN) * s_ref[...]).astype(BF16)

    nb = POOL_DIM // LANE
    vec = pl.BlockSpec((1, LANE), lambda b: (0, b))
    return _pcall(body, name=name, out_shape=jax.ShapeDtypeStruct((t, POOL_DIM), BF16), grid=(nb,),
                  in_specs=[_col(t, lambda b: (0, POOL_COL + b)), vec, pl.BlockSpec((1, LANE, LANE), lambda b: (b, 0, 0)), vec],
                  out_specs=_col(t, lambda b: (0, b)), semantics=("parallel",), vmem_limit=VMEM_LIMIT)(proj, win, wbd, scale)


def _pool_bwd(proj, win, wbd, scale, dmixed, *, name):
    t = proj.shape[0]

    def body(x_ref, win_ref, w_ref, s_ref, d_ref, dx_ref, dw_ref, ds_ref):
        xv = x_ref[...]
        winv = win_ref[...]
        cnt = _pool_counts(xv.shape, winv)
        pooled = _pick_window(_window_sums(xv, _shift_down), winv) / cnt - xv
        dv = d_ref[...]
        ds_ref[...] = _colsum(dv * _dot(pooled, w_ref[0], NN))
        dy0 = dv * s_ref[...]
        dw_ref[0] = _dot(pooled, dy0, TN)
        dpooled = _dot(dy0, w_ref[0], NT)
        dmean = dpooled / cnt
        dx_ref[...] = (_pick_window(_window_sums(dmean, _shift_up), winv) - dpooled).astype(BF16)

    nb = POOL_DIM // LANE
    vec = pl.BlockSpec((1, LANE), lambda b: (0, b))
    mat = pl.BlockSpec((1, LANE, LANE), lambda b: (b, 0, 0))
    first = A_DIM // LANE
    return _pcall(body, name=name,
                  out_shape=(jax.ShapeDtypeStruct((t, POOL_DIM), BF16), jax.ShapeDtypeStruct((nb, LANE, LANE), F32),
                             jax.ShapeDtypeStruct((1, POOL_DIM), F32)),
                  grid=(nb,),
                  in_specs=[_col(t, lambda b: (0, POOL_COL + b)), vec, mat, vec, _col(t, lambda b: (0, first + b))],
                  out_specs=(_col(t, lambda b: (0, b)), mat, vec), semantics=("parallel",),
                  vmem_limit=VMEM_LIMIT)(proj, win, wbd, scale, dmixed)


def _sconv_fwd(proj, w, *, name):
    t = proj.shape[0]

    def body(cb_ref, cc_ref, ch_ref, w_ref, y_ref):
        y_ref[...] = (cb_ref[...] * _conv_fwd(cc_ref[...] * ch_ref[...], w_ref, CONV_TAPS)).astype(BF16)

    nb = CONV_DIM // LANE
    return _pcall(body, name=name, out_shape=jax.ShapeDtypeStruct((t, CONV_DIM), BF16), grid=(nb,),
                  in_specs=[_col(t, lambda b: (0, CB_COL + b)), _col(t, lambda b: (0, CC_COL + b)),
                            _col(t, lambda b: (0, CH_COL + b)), pl.BlockSpec((CONV_TAPS, LANE), lambda b: (0, b))],
                  out_specs=_col(t, lambda b: (0, b)), semantics=("parallel",), vmem_limit=VMEM_LIMIT)(proj, proj, proj, w)


def _sconv_bwd(proj, w, dmixed, *, name):
    t = proj.shape[0]

    def body(cb_ref, cc_ref, ch_ref, w_ref, d_ref, dcb_ref, dcc_ref, dch_ref, dw_ref):
        cc = cc_ref[...]
        ch = ch_ref[...]
        u = cc * ch
        dv = d_ref[...]
        dcb_ref[...] = (dv * _conv_fwd(u, w_ref, CONV_TAPS)).astype(BF16)
        du = _conv_bwd(u, dv * cb_ref[...], w_ref, dw_ref, CONV_TAPS)
        dcc_ref[...] = (du * ch).astype(BF16)
        dch_ref[...] = (du * cc).astype(BF16)

    nb = CONV_DIM // LANE
    first = (A_DIM + POOL_DIM) // LANE
    act = jax.ShapeDtypeStruct((t, CONV_DIM), BF16)
    wspec = pl.BlockSpec((CONV_TAPS, LANE), lambda b: (0, b))
    ospec = _col(t, lambda b: (0, b))
    return _pcall(body, name=name, out_shape=(act, act, act, jax.ShapeDtypeStruct((CONV_TAPS, CONV_DIM), F32)), grid=(nb,),
                  in_specs=[_col(t, lambda b: (0, CB_COL + b)), _col(t, lambda b: (0, CC_COL + b)),
                            _col(t, lambda b: (0, CH_COL + b)), wspec, _col(t, lambda b: (0, first + b))],
                  out_specs=(ospec, ospec, ospec, wspec), semantics=("parallel",),
                  vmem_limit=VMEM_LIMIT)(proj, proj, proj, w, dmixed)


def _chunk_masks():
    r = lax.broadcasted_iota(jnp.int32, (CHUNK, CHUNK), 0)
    c = lax.broadcasted_iota(jnp.int32, (CHUNK, CHUNK), 1)
    return r >= c, r > c, jnp.where(r == c, 1.0, 0.0).astype(F32)


def _split(a):
    hi = a.astype(BF16)
    return hi, (a - hi.astype(F32)).astype(BF16)


def _dot_split(a, b, dims):
    (ah, al), (bh, bl) = a, b
    return _dot(ah, bh, dims) + _dot(ah, bl, dims) + _dot(al, bh, dims)


def _tri_inv(lows, eye):
    xs = [eye - low for low in lows]
    ps = [_split(low) for low in lows]
    ps = [_split(_dot_split(p, p, NN)) for p in ps]
    for i in range(5):
        xs = [x + _dot_split(_split(x), p, NN) for x, p in zip(xs, ps)]
        if i < 4:
            ps = [_split(_dot_split(p, p, NN)) for p in ps]
    return xs


def _prefix_sum_rows(x):
    for k in range(6):
        x = x + _shift_down(x, 1 << k)
    return x


def _suffix_sum_rows(x):
    for k in range(6):
        x = x + _shift_up(x, 1 << k)
    return x


def _chunk_decay(g, incl):
    gcb = _prefix_sum_rows(g)
    gtot = _colsum(g)
    col = gcb[:, :CHUNK]
    row = gcb.T[:CHUNK, :]
    decay = jnp.exp(jnp.where(incl, col - row, -1e30))
    return gcb, gtot, decay


CHUNKS_PER_STEP = 4


def _heads_of(ref, base, rows):
    return [ref[base + h, rows, :] for h in range(HEADS)]


def _chunk_rows(j):
    return pl.ds(j * CHUNK, CHUNK)


def _deltanet_prep(qkv, g, beta, *, name):
    t = qkv.shape[1]
    n_chunks = t // CHUNK
    per = CHUNKS_PER_STEP
    probs = [(j, h) for j in range(per) for h in range(HEADS)]

    def body(qkv_ref, g_ref, b_ref, u_ref, w_ref, qg_ref, kg_ref, attn_ref, tm_ref):
        incl, strict, eye = _chunk_masks()
        q = [qkv_ref[h, _chunk_rows(j), :] for j, h in probs]
        k = [qkv_ref[HEADS + h, _chunk_rows(j), :] for j, h in probs]
        v = [qkv_ref[2 * HEADS + h, _chunk_rows(j), :] for j, h in probs]
        bv = [b_ref[h, _chunk_rows(j), :] for j, h in probs]
        dec = [_chunk_decay(g_ref[h, _chunk_rows(j), :], incl) for j, h in probs]
        kb = [a * b for a, b in zip(k, bv)]
        low = [jnp.where(strict, _dot(a, b, NT) * d[2], 0.0) for a, b, d in zip(kb, k, dec)]
        tm = _tri_inv(low, eye)
        egc = [jnp.exp(d[0]) for d in dec]
        u = [_dot(m, a * b, NN) for m, a, b in zip(tm, v, bv)]
        w = [_dot(m, a * e, NN) for m, a, e in zip(tm, kb, egc)]
        attn = [_dot(a, b, NT) * d[2] for a, b, d in zip(q, k, dec)]
        for i, (j, h) in enumerate(probs):
            rows = _chunk_rows(j)
            u_ref[h, rows, :] = u[i]
            w_ref[h, rows, :] = w[i].astype(BF16)
            qg_ref[h, rows, :] = (q[i] * egc[i]).astype(BF16)
            kg_ref[h, rows, :] = (k[i] * jnp.exp(dec[i][1] - dec[i][0])).astype(BF16)
            attn_ref[j, h] = attn[i].astype(BF16)
            tm_ref[j, h] = tm[i]

    act = lambda heads: pl.BlockSpec((heads, per * CHUNK, LANE), lambda n: (0, n, 0))
    mat = pl.BlockSpec((per, HEADS, CHUNK, CHUNK), lambda n: (n, 0, 0, 0))
    return _pcall(
        body, name=name,
        out_shape=(jax.ShapeDtypeStruct((HEADS, t, LANE), F32),) + (jax.ShapeDtypeStruct((HEADS, t, LANE), BF16),) * 3
        + (jax.ShapeDtypeStruct((n_chunks, HEADS, CHUNK, CHUNK), BF16), jax.ShapeDtypeStruct((n_chunks, HEADS, CHUNK, CHUNK), F32)),
        grid=(n_chunks // per,), in_specs=[act(3 * HEADS), act(HEADS), act(HEADS)],
        out_specs=(act(HEADS),) * 4 + (mat, mat), semantics=("parallel",), vmem_limit=VMEM_LIMIT)(qkv, g, beta)


SCAN_CHUNKS_PER_STEP = 8


def _deltanet_scan(u, w, qg, kg, attn, g, *, name):
    t = u.shape[1]
    n_chunks = t // CHUNK
    per = SCAN_CHUNKS_PER_STEP

    def body(u_ref, w_ref, qg_ref, kg_ref, attn_ref, g_ref, o_ref, vn_ref, st_ref, s_ref):
        @pl.when(pl.program_id(0) == 0)
        def _():
            s_ref[...] = jnp.zeros_like(s_ref)

        for j in range(per):
            rows = _chunk_rows(j)
            s = [s_ref[h] for h in range(HEADS)]
            vn = [u_ref[h, rows, :] - _dot(w_ref[h, rows, :], s[h], NN) for h in range(HEADS)]
            o = [_dot(qg_ref[h, rows, :], s[h], NN) + _dot(attn_ref[j, h], vn[h], NN) for h in range(HEADS)]
            eg = [jnp.exp(_colsum(g_ref[h, rows, :])) for h in range(HEADS)]
            for h in range(HEADS):
                st_ref[j, h] = s[h]
                s_ref[h] = s[h] * eg[h] + _dot(kg_ref[h, rows, :], vn[h], TN)
                o_ref[h, rows, :] = o[h]
                vn_ref[h, rows, :] = vn[h]

    act = pl.BlockSpec((HEADS, per * CHUNK, LANE), lambda n: (0, n, 0))
    out = jax.ShapeDtypeStruct((HEADS, t, LANE), F32)
    return _pcall(
        body, name=name, out_shape=(out, out, jax.ShapeDtypeStruct((n_chunks, HEADS, LANE, LANE), F32)), grid=(n_chunks // per,),
        in_specs=[act] * 4 + [pl.BlockSpec((per, HEADS, CHUNK, CHUNK), lambda n: (n, 0, 0, 0)), act],
        out_specs=(act, act, pl.BlockSpec((per, HEADS, LANE, LANE), lambda n: (n, 0, 0, 0))),
        scratch_shapes=[pltpu.VMEM((HEADS, LANE, LANE), F32)], semantics=("arbitrary",))(u, w, qg, kg, attn, g)


def _deltanet_bscan(w, qg, kg, attn, g, do, *, name):
    t = w.shape[1]
    n_chunks = t // CHUNK
    per = SCAN_CHUNKS_PER_STEP
    steps = n_chunks // per

    def body(w_ref, qg_ref, kg_ref, attn_ref, g_ref, do_ref, dvn_ref, dsn_ref, ds_ref):
        @pl.when(pl.program_id(0) == 0)
        def _():
            ds_ref[...] = jnp.zeros_like(ds_ref)

        for j in reversed(range(per)):
            rows = _chunk_rows(j)
            dsn = [ds_ref[h] for h in range(HEADS)]
            dov = [do_ref[h, rows, :] for h in range(HEADS)]
            dvn = [_dot(attn_ref[j, h], dov[h], TN) + _dot(kg_ref[h, rows, :], dsn[h], NN) for h in range(HEADS)]
            eg = [jnp.exp(_colsum(g_ref[h, rows, :])) for h in range(HEADS)]
            for h in range(HEADS):
                dsn_ref[j, h] = dsn[h]
                ds_ref[h] = _dot(qg_ref[h, rows, :], dov[h], TN) + eg[h] * dsn[h] - _dot(w_ref[h, rows, :], dvn[h], TN)
                dvn_ref[h, rows, :] = dvn[h]

    act = pl.BlockSpec((HEADS, per * CHUNK, LANE), lambda n: (0, steps - 1 - n, 0))
    return _pcall(
        body, name=name,
        out_shape=(jax.ShapeDtypeStruct((HEADS, t, LANE), F32), jax.ShapeDtypeStruct((n_chunks, HEADS, LANE, LANE), F32)),
        grid=(steps,),
        in_specs=[act] * 3 + [pl.BlockSpec((per, HEADS, CHUNK, CHUNK), lambda n: (steps - 1 - n, 0, 0, 0)), act, act],
        out_specs=(act, pl.BlockSpec((per, HEADS, LANE, LANE), lambda n: (steps - 1 - n, 0, 0, 0))),
        scratch_shapes=[pltpu.VMEM((HEADS, LANE, LANE), F32)], semantics=("arbitrary",))(w, qg, kg, attn, g, do)


def _sum_all(x):
    return jnp.sum(jnp.sum(x, axis=1, keepdims=True), axis=0, keepdims=True)


def _rowsum(x):
    return jnp.sum(x, axis=1, keepdims=True)


def _deltanet_post(qkv, g, beta, tmats, states, dstates, do, dvn, vn, *, name):
    t = qkv.shape[1]
    n_chunks = t // CHUNK
    per = CHUNKS_PER_STEP
    probs = [(j, h) for j in range(per) for h in range(HEADS)]

    def body(qkv_ref, g_ref, b_ref, tm_ref, st_ref, dsn_ref, do_ref, dvn_ref, vn_ref, dqkv_ref, dg_ref, db_ref):
        incl, strict, _ = _chunk_masks()
        ones = jnp.ones((CHUNK, LANE), BF16)
        last_row = lax.broadcasted_iota(jnp.int32, (CHUNK, LANE), 0) == CHUNK - 1
        z = lambda f, *cols: [f(*a) for a in zip(*cols)]
        q = [qkv_ref[h, _chunk_rows(j), :] for j, h in probs]
        k = [qkv_ref[HEADS + h, _chunk_rows(j), :] for j, h in probs]
        v = [qkv_ref[2 * HEADS + h, _chunk_rows(j), :] for j, h in probs]
        bv = [b_ref[h, _chunk_rows(j), :] for j, h in probs]
        dov = [do_ref[h, _chunk_rows(j), :] for j, h in probs]
        dvn_ = [dvn_ref[h, _chunk_rows(j), :] for j, h in probs]
        vn_ = [vn_ref[h, _chunk_rows(j), :] for j, h in probs]
        tm = [tm_ref[j, h] for j, h in probs]
        s = [st_ref[j, h] for j, h in probs]
        dsn = [dsn_ref[j, h] for j, h in probs]
        dec = [_chunk_decay(g_ref[h, _chunk_rows(j), :], incl) for j, h in probs]
        decay = [d[2] for d in dec]
        egc = [jnp.exp(d[0]) for d in dec]
        ekg = [jnp.exp(d[1] - d[0]) for d in dec]
        kb = z(lambda a, b: a * b, k, bv)
        vb = z(lambda a, b: a * b, v, bv)
        kbg = z(lambda a, b: a * b, kb, egc)
        qg = z(lambda a, b: a * b, q, egc)
        kg = z(lambda a, b: a * b, k, ekg)
        kk = z(lambda a, b: _dot(a, b, NT), kb, k)
        qk = z(lambda a, b: _dot(a, b, NT), q, k)
        dattn = z(lambda a, b: jnp.where(incl, _dot(a, b, NT), 0.0), dov, vn_)
        dqg = z(lambda a, b: _dot(a, b, NT), dov, s)
        dkg = z(lambda a, b: _dot(a, b, NT), vn_, dsn)
        dglast = z(lambda a, b, c, d, e: _sum_all(a * b) * jnp.exp(e[1]) + _sum_all(c * d), s, dsn, dkg, kg, dec)
        dw = z(lambda a, b: -_dot(a, b, NT), dvn_, s)
        dtm = z(lambda a, b, c, d: _dot(a, b, NT) + _dot(c, d, NT), dvn_, vb, dw, kbg)
        dvb = z(lambda a, b: _dot(a, b, TN), tm, dvn_)
        dkbg = z(lambda a, b: _dot(a, b, TN), tm, dw)
        dlow = z(lambda a, b: jnp.where(strict, -_dot(_dot(a, b, TN), a, NT), 0.0), tm, dtm)
        dkk = z(lambda a, b: a * b, dlow, decay)
        dqk = z(lambda a, b: a * b, dattn, decay)
        dkb = z(lambda a, b, c, d: _dot(a, b, NN) + c * d, dkk, k, dkbg, egc)
        dk = z(lambda a, b, c, d, e, f, g_, h_: _dot(a, b, TN) + _dot(c, d, TN) + e * f + g_ * h_, dkk, kb, dqk, q, dkg, ekg, dkb, bv)
        dq = z(lambda a, b, c, d: _dot(a, b, NN) + c * d, dqk, k, dqg, egc)
        m = z(lambda a, b, c, d, e: (a * b + c * d) * e, dlow, kk, dattn, qk, decay)
        mcol = [_dot(mh, ones, TN) + _dot(ml, ones, TN) for mh, ml in (_split(a) for a in m)]
        for i, (j, h) in enumerate(probs):
            rows = _chunk_rows(j)
            dqkv_ref[h, rows, :] = dq[i]
            dqkv_ref[HEADS + h, rows, :] = dk[i]
            dqkv_ref[2 * HEADS + h, rows, :] = dvb[i] * bv[i]
            db_ref[h, rows, :] = jnp.broadcast_to(_rowsum(dkb[i] * k[i] + dvb[i] * v[i]), (CHUNK, LANE))
            dgc = (_rowsum(dqg[i] * qg[i] + dkbg[i] * kbg[i] - dkg[i] * kg[i]) + _rowsum(m[i]) - mcol[i]
                   + jnp.where(last_row, dglast[i], 0.0))
            dg_ref[h, rows, :] = _suffix_sum_rows(dgc)

    act = lambda heads: pl.BlockSpec((heads, per * CHUNK, LANE), lambda n: (0, n, 0))
    mat = lambda d: pl.BlockSpec((per, HEADS, d, d), lambda n: (n, 0, 0, 0))
    out = jax.ShapeDtypeStruct((HEADS, t, LANE), F32)
    return _pcall(
        body, name=name, out_shape=(jax.ShapeDtypeStruct((3 * HEADS, t, LANE), F32), out, out), grid=(n_chunks // per,),
        in_specs=[act(3 * HEADS), act(HEADS), act(HEADS), mat(CHUNK), mat(LANE), mat(LANE), act(HEADS), act(HEADS), act(HEADS)],
        out_specs=(act(3 * HEADS), act(HEADS), act(HEADS)), semantics=("parallel",),
        vmem_limit=VMEM_LIMIT)(qkv, g, beta, tmats, states, dstates, do, dvn, vn)


ANY = pl.BlockSpec(memory_space=pl.ANY)
PEERS = N_DEV - 1


def _all_gather(arrays, *, name):
    n = len(arrays)

    def body(*refs):
        ins, outs = refs[:n], refs[n:2 * n]
        send_sems, recv_sems, local_sems = refs[2 * n:]
        x, y, c = lax.axis_index("x"), lax.axis_index("y"), lax.axis_index("c")
        me, sibling = (x, y, c), (x, y, 1 - c)
        chips = [(1 - x, y), (x, 1 - y), (1 - x, 1 - y)]

        def copy(a, k, block, to, src=None):
            dst = outs[a].at[4 * block[0] + 2 * block[1] + block[2]]
            return pltpu.make_async_remote_copy(src_ref=dst if src is None else src, dst_ref=dst, send_sem=send_sems.at[a * PEERS + k],
                                                recv_sem=recv_sems.at[a * PEERS + k], device_id=to, device_id_type=MESH)

        local = [pltpu.make_async_copy(ins[a], outs[a].at[4 * x + 2 * y + c], local_sems.at[a]) for a in range(n)]
        for cp in local:
            cp.start()
        first = []
        for a in range(n):
            first += [copy(a, 1 + j, me, (*chip, c), src=ins[a]) for j, chip in enumerate(chips)]
            first.append(copy(a, 0, me, sibling, src=ins[a]))
        for cp in first:
            cp.start()
        passed = []
        for a in range(n):
            for j, chip in enumerate(chips):
                copy(a, 1 + j, (*chip, c), me).wait_recv()
                fwd = copy(a, 4 + j, (*chip, c), sibling)
                fwd.start()
                passed.append(fwd)
        for a in range(n):
            copy(a, 0, sibling, me).wait_recv()
            for j, chip in enumerate(chips):
                copy(a, 4 + j, (*chip, 1 - c), me).wait_recv()
        for cp in first + passed:
            cp.wait_send()
        for cp in local:
            cp.wait()

    return _pcall(body, name=name, out_shape=tuple(jax.ShapeDtypeStruct((N_DEV,) + a.shape, a.dtype) for a in arrays),
                  in_specs=[ANY] * n, out_specs=(ANY,) * n,
                  scratch_shapes=[pltpu.SemaphoreType.DMA((n * PEERS,)), pltpu.SemaphoreType.DMA((n * PEERS,)),
                                  pltpu.SemaphoreType.DMA((n,))])(*arrays)


CHIPS = 4


def _pair_exchange(arrays, *, name):
    n = len(arrays)

    def body(*refs):
        ins, outs = refs[:n], refs[n:2 * n]
        send_sems, recv_sems = refs[2 * n:]
        x, y, c = lax.axis_index("x"), lax.axis_index("y"), lax.axis_index("c")
        copies = []
        for a in range(n):
            for q in range(CHIPS):
                cp = pltpu.make_async_remote_copy(src_ref=ins[a].at[2 * q + 1 - c], dst_ref=outs[a].at[q],
                                                  send_sem=send_sems.at[a * CHIPS + q], recv_sem=recv_sems.at[a * CHIPS + q],
                                                  device_id=(x, y, 1 - c), device_id_type=MESH)
                cp.start()
                copies.append(cp)
        for cp in copies:
            cp.wait()

    return _pcall(body, name=name, out_shape=tuple(jax.ShapeDtypeStruct((CHIPS,) + a.shape[1:], a.dtype) for a in arrays),
                  in_specs=[ANY] * n, out_specs=(ANY,) * n,
                  scratch_shapes=[pltpu.SemaphoreType.DMA((n * CHIPS,)), pltpu.SemaphoreType.DMA((n * CHIPS,))])(*arrays)


def _pair_add(blocks, theirs, *, name):
    _, r, c_ = blocks.shape
    tr = _tile(r, 512, 16)

    def body(mine_ref, theirs_ref, o_ref):
        core = lax.axis_index("c")
        own = jnp.where(core == 0, mine_ref[0, 0].astype(F32), mine_ref[0, 1].astype(F32))
        o_ref[0] = (own + theirs_ref[0].astype(F32)).astype(o_ref.dtype)

    spec = pl.BlockSpec((1, tr, c_), lambda q, i: (q, i, 0))
    return _pcall(body, name=name, out_shape=jax.ShapeDtypeStruct(theirs.shape, theirs.dtype), grid=(CHIPS, r // tr),
                  in_specs=[pl.BlockSpec((1, 2, tr, c_), lambda q, i: (q, 0, i, 0)), spec], out_specs=spec,
                  semantics=("parallel", "parallel"), vmem_limit=VMEM_LIMIT)(blocks.reshape(CHIPS, 2, r, c_), theirs)


HBM = pl.BlockSpec(memory_space=pltpu.HBM)
SEM = pl.BlockSpec(memory_space=pltpu.SEMAPHORE)
EFFECT = pltpu.SideEffectType.DATAFLOW_SIDE_EFFECTING


GATHER, CHIP_GATHER, CHIP_SCATTER = "gather", "chip_gather", "chip_scatter"
PEERS_OF = {GATHER: N_DEV - 1, CHIP_GATHER: CHIPS - 1, CHIP_SCATTER: CHIPS - 1}


def _direct_copies(srcs, lands, send_sems, recv_sems, local_sems, kind):
    x, y, c = lax.axis_index("x"), lax.axis_index("y"), lax.axis_index("c")
    peers = PEERS_OF[kind]
    mine = 2 * x + y if kind == CHIP_SCATTER else 4 * x + 2 * y + c
    copies = []
    for a, (src, land) in enumerate(zip(srcs, lands)):
        for k in range(1, peers + 1):
            bits = k if kind == GATHER else 2 * k
            px = 1 - x if bits & 4 else x
            py = 1 - y if bits & 2 else y
            pc = 1 - c if bits & 1 else c
            copies.append(pltpu.make_async_remote_copy(
                src_ref=src.at[2 * px + py] if kind == CHIP_SCATTER else src, dst_ref=land.at[mine],
                send_sem=send_sems.at[a * peers + k - 1], recv_sem=recv_sems.at[a * peers + k - 1],
                device_id=(px, py, pc), device_id_type=MESH))
    for a, (src, land) in enumerate(zip(srcs, lands)):
        copies.append(pltpu.make_async_copy(src.at[mine] if kind == CHIP_SCATTER else src, land.at[mine], local_sems.at[a]))
    return copies


def _pair_swap(arrays, *, name):
    n = len(arrays)

    def body(*refs):
        mine, zones = refs[:n], refs[n:2 * n]
        send_sems, recv_sems = refs[2 * n:]
        x, y, c = lax.axis_index("x"), lax.axis_index("y"), lax.axis_index("c")
        copies = []
        for a in range(n):
            for q in range(CHIPS):
                copies.append(pltpu.make_async_remote_copy(
                    src_ref=mine[a].at[2 * q + c], dst_ref=zones[a].at[2 * q + c], send_sem=send_sems.at[a * CHIPS + q],
                    recv_sem=recv_sems.at[a * CHIPS + q], device_id=(x, y, 1 - c), device_id_type=MESH))
        for cp in copies:
            cp.start()
        for cp in copies:
            cp.wait()

    return _pcall(body, name=name, out_shape=tuple(jax.ShapeDtypeStruct(a.shape, a.dtype) for a in arrays),
                  in_specs=[ANY] * n, out_specs=(ANY,) * n, input_output_aliases={i: i for i in range(n)},
                  scratch_shapes=[pltpu.SemaphoreType.DMA((n * CHIPS,)), pltpu.SemaphoreType.DMA((n * CHIPS,))])(*arrays)


def _exchange_start(groups, kind, *, name, after=None):
    srcs = [s for group in groups for s in group]
    n = len(srcs)
    sizes = [len(group) for group in groups]
    starts = [sum(sizes[:g]) for g in range(len(groups))]
    land_shapes = [s.shape if kind == CHIP_SCATTER else (N_DEV,) + s.shape for s in srcs]
    peers = PEERS_OF[kind]
    extra = [] if after is None else [after]

    def body(*refs):
        srcs_, lands = refs[:n], refs[n:2 * n]
        token = refs[-1]
        sem_refs = refs[2 * n + len(extra):]
        for g, (at, size) in enumerate(zip(starts, sizes)):
            send_sems, recv_sems, local_sems = sem_refs[3 * g:3 * g + 3]
            for cp in _direct_copies(srcs_[at:at + size], lands[at:at + size], send_sems, recv_sems, local_sems, kind):
                cp.start()
        token[...] = jnp.zeros_like(token)

    sems = tuple(t for size in sizes for t in (pltpu.SemaphoreType.DMA((size * peers,)), pltpu.SemaphoreType.DMA((size * peers,)),
                                               pltpu.SemaphoreType.DMA((size,))))
    thru = tuple(pltpu.HBM(s.shape, s.dtype) for s in srcs) + tuple(pltpu.HBM(shp, s.dtype) for shp, s in zip(land_shapes, srcs))
    ins = [pltpu.with_memory_space_constraint(s, pltpu.HBM) for s in srcs]
    ins += [pltpu.with_memory_space_constraint(lax.empty(shp, s.dtype), pltpu.HBM) for shp, s in zip(land_shapes, srcs)]
    out = pl.pallas_call(
        body, name=name, out_shape=sems + thru + (jax.ShapeDtypeStruct((SUBLANE, LANE), F32),),
        in_specs=[HBM] * (2 * n) + [ANY] * len(extra),
        out_specs=(SEM,) * len(sems) + (HBM,) * (2 * n) + (pl.BlockSpec(memory_space=pltpu.VMEM),),
        input_output_aliases={i: len(sems) + i for i in range(2 * n)},
        compiler_params=pltpu.CompilerParams(has_side_effects=EFFECT))(*ins, *extra)
    arrays = out[len(sems):-1]
    started = [tuple(out[3 * g:3 * g + 3]) + tuple(arrays[at:at + size]) + tuple(arrays[n + at:n + at + size])
               for g, (at, size) in enumerate(zip(starts, sizes))]
    return started, out[-1]


def _exchange_wait(started, after, kind, *, name):
    n = (len(started) - 3) // 2
    sems, arrays = started[:3], started[3:]

    def body(*refs):
        srcs_, lands = refs[:n], refs[n:2 * n]
        send_sems, recv_sems, local_sems = refs[2 * n:2 * n + 3]
        for cp in _direct_copies(srcs_, lands, send_sems, recv_sems, local_sems, kind):
            cp.wait()

    out = pl.pallas_call(
        body, name=name, out_shape=tuple(pltpu.HBM(a.shape, a.dtype) for a in arrays),
        in_specs=[HBM] * (2 * n) + [SEM] * 3 + [ANY], out_specs=(HBM,) * (2 * n),
        input_output_aliases={i: i for i in range(2 * n)},
        compiler_params=pltpu.CompilerParams(has_side_effects=EFFECT))(*arrays, *sems, after)
    return out[n:]


def _adamw_reduce(w, parts, m, v, *, name, after=None):
    layers, r, c = w.shape
    assert len(parts) == layers
    senders = parts[0].shape[0]
    tr = _tile(r, 512, 16)
    tiles = r // tr
    bc1 = 1.0 - ADAM_B1 ** ADAM_STEP
    bc2 = 1.0 - ADAM_B2 ** ADAM_STEP

    def body(w_ref, *rest):
        p_refs = rest[:layers]
        m_ref, v_ref, g_ref, d_ref, nm_ref, nv_ref = rest[layers:]

        def update(p_ref):
            g = p_ref[0, :, pl.ds(0, c)].astype(F32)
            for s in range(1, senders):
                g = g + p_ref[s, :, pl.ds(0, c)].astype(F32)
            nm = ADAM_B1 * m_ref[0] + (1.0 - ADAM_B1) * g
            nv = ADAM_B2 * v_ref[0] + (1.0 - ADAM_B2) * (g * g)
            g_ref[0] = g
            nm_ref[0] = nm
            nv_ref[0] = nv
            d_ref[0] = -ADAM_LR * ((nm / bc1) / (jnp.sqrt(nv / bc2) + ADAM_EPS) + ADAM_WD * w_ref[0])

        for layer in range(layers):
            pl.when(pl.program_id(0) == layer)(functools.partial(update, p_refs[layer]))

    def part_spec(layer, shape):
        rest = 0 if layer > 0 else tiles - 1
        return pl.BlockSpec((senders, tr, shape[2]), lambda l, i: (0, jnp.where(l == layer, i, rest), 0))

    spec = pl.BlockSpec((1, tr, c), lambda l, i: (l, i, 0))
    out = jax.ShapeDtypeStruct((layers, r, c), F32)
    return _pcall(body, name=name, out_shape=(out,) * 4, grid=(layers, tiles),
                  in_specs=[spec] + [part_spec(layer, p.shape) for layer, p in enumerate(parts)] + [spec, spec],
                  out_specs=(spec,) * 4, semantics=("arbitrary", "arbitrary"), vmem_limit=VMEM_LIMIT, after=after)(w, *parts, m, v)


def _pool_windows():
    return jnp.repeat(jnp.asarray(POOL_WINDOWS, F32), POOL_DIM // len(POOL_WINDOWS))[None, :]


def _block_diag_pairs(pool_w):
    z = jnp.zeros_like(pool_w[0])
    return jnp.stack([jnp.block([[pool_w[2 * b], z], [z, pool_w[2 * b + 1]]]) for b in range(2)])


def _pad_lanes(vec):
    return jnp.zeros((1, LANE), F32).at[0, :vec.shape[0]].set(vec)


FF_SHARD = D_FF // N_DEV
FF_BLOCK = 384
D_FF_PAD = N_DEV * FF_BLOCK


def _layer_fwd(x, p_i, wt, fetch):
    wt = {**wt, **fetch(0, x)}
    h1 = _rmsnorm_fwd(x, wt["norm1_g"], name="rmsnorm_fwd")
    proj = _matmul(h1, wt["w_in"], "nt", name="mm_in")
    wt.update(fetch(1, proj))
    qkv = _qkv_prep_fwd(proj, wt["conv_qkv"], name="qkv_prep_fwd")
    g, beta = _gates_fwd(proj, wt["a_log"], wt["dt_bias"], name="gates_fwd")
    u, w, qg, kg, attn, tmats = _deltanet_prep(qkv, g, beta, name="deltanet_prep")
    o, vn, states = _deltanet_scan(u, w, qg, kg, attn, g, name="deltanet_scan")
    o_a = _apost_fwd(o, proj, wt["onorm_g"], name="apost_fwd")
    o_b = _pool_fwd(proj, wt["pool_win"], wt["pool_wbd"], wt["pool_scale"], name="pool_fwd")
    o_c = _sconv_fwd(proj, wt["sconv_w"], name="sconv_fwd")
    mixed = jnp.concatenate([o_a, o_b, o_c], axis=1)
    x1 = _matmul(mixed, wt["w_out"], "nn", res=x, name="mm_out")
    h2 = _rmsnorm_fwd(x1, wt["norm2_g"], name="rmsnorm_fwd")
    wt.update(fetch(2, h2))
    ff, gate, up = _swiglu_fwd(h2, wt["w_gate"], wt["w_up"], name="swiglu_fwd")
    wt.update(fetch(3, ff))
    x2 = _matmul(ff, wt["w_down"], "nn", res=x1, name="mm_down")
    wt.update(fetch(4, x2))
    pgl = _matmul(x2, wt["ple_gate"], "nn", name="mm_pleg")
    pp = _matmul(p_i, wt["ple_proj"], "nn", b_blocked=True, name="mm_plep")
    x3 = _ple_fwd(x2, pgl, pp, name="ple_fwd")
    saved = dict(x=x, h1=h1, proj=proj, qkv=qkv, g=g, beta=beta, o=o, states=states, tmats=tmats, mixed=mixed, x1=x1, h2=h2,
                 gate=gate, up=up, ff=ff, x2=x2, pgl=pgl, pp=pp, p=p_i, w=w, qg=qg, kg=kg, attn=attn, vn=vn, wt=wt)
    return x3, saved


def _col_blocks(g):
    a = g.shape[0]
    return jnp.transpose(g.reshape(a, N_DEV, -1), (1, 0, 2))


def _cols_joined(blocks):
    return jnp.transpose(blocks, (1, 0, 2)).reshape(blocks.shape[1], -1)


def _layer_bwd(dx3, sv, emit, after=None):
    gr, big = {}, {}
    wt = sv["wt"]
    rows = D_MODEL // N_DEV
    dpgl, dpp = _ple_bwd(dx3, sv["pgl"], sv["pp"], name="ple_bwd", after=after)
    big["ple_proj"] = _matmul(sv["p"], dpp, "tn", out_blocked=(N_DEV, rows), out_dtype=BF16, name="mm_dplep")
    big["ple_gate"] = _matmul(sv["x2"], dpgl, "tn", out_dtype=BF16, name="mm_dpleg").reshape(N_DEV, rows, D_MODEL)
    dx2 = _matmul(dpgl, wt["ple_gate"], "nt", res=dx3, name="mm_dx2")
    big["w_down"] = _matmul(sv["ff"], dx2, "tn", out_dtype=BF16, name="mm_ddown").reshape(N_DEV, FF_BLOCK, D_MODEL)
    dgate, dup = _swiglu_bwd(dx2, wt["w_down"], sv["gate"], sv["up"], name="swiglu_bwd", after=emit(0, big))
    big["w_gate"] = _matmul(dgate, sv["h2"], "tn", out_dtype=BF16, name="mm_dgate").reshape(N_DEV, FF_BLOCK, D_MODEL)
    big["w_up"] = _matmul(dup, sv["h2"], "tn", out_dtype=BF16, name="mm_dup").reshape(N_DEV, FF_BLOCK, D_MODEL)
    dh2 = _matmul(dgate, wt["w_gate"], "nn", name="mm_dh2_gate")
    dh2 = _matmul(dup, wt["w_up"], "nn", res=dh2, name="mm_dh2_up")
    dx1, gr["norm2_g"] = _rmsnorm_bwd(sv["x1"], wt["norm2_g"], dh2, dx2, name="rmsnorm_bwd")
    big["w_out"] = _matmul(sv["mixed"], dx1, "tn", out_dtype=BF16, name="mm_dout").reshape(N_DEV, rows, D_MODEL)
    dmixed = _matmul(dx1, wt["w_out"], "nt", name="mm_dmixed", after=emit(1, big))
    proj = sv["proj"]
    dcb, dcc, dch, dsconv = _sconv_bwd(proj, wt["sconv_w"], dmixed, name="sconv_bwd")
    big["sconv_w"] = _col_blocks(dsconv)
    dhp, dwbd, gr["pool_scale"] = _pool_bwd(proj, wt["pool_win"], wt["pool_wbd"], wt["pool_scale"], dmixed, name="pool_bwd")
    half = LANE // 2
    gr["pool_w"] = jnp.stack([dwbd[0, :half, :half], dwbd[0, half:, half:], dwbd[1, :half, :half], dwbd[1, half:, half:]])
    do, dz, gr["onorm_g"] = _apost_bwd(sv["o"], proj, wt["onorm_g"], dmixed, name="apost_bwd")
    dvn, dstates = _deltanet_bscan(sv["w"], sv["qg"], sv["kg"], sv["attn"], sv["g"], do, name="deltanet_bscan")
    dqkv_h, dg, dbeta = _deltanet_post(sv["qkv"], sv["g"], sv["beta"], sv["tmats"], sv["states"], dstates, do, dvn, sv["vn"],
                                       name="deltanet_post")
    dab, dalog, ddtb = _gates_bwd(proj, wt["a_log"], wt["dt_bias"], dg, dbeta, name="gates_bwd")
    gr["a_log"], gr["dt_bias"] = dalog[0, :HEADS], ddtb[0, :HEADS]
    dqkv, dconv = _qkv_prep_bwd(proj, wt["conv_qkv"], dqkv_h, name="qkv_prep_bwd")
    big["conv_qkv"] = _col_blocks(dconv)
    dproj = jnp.concatenate([dqkv, dz, dab, dhp, dcb, dcc, dch], axis=1)
    dwin = _matmul(dproj, sv["h1"], "tn", out_dtype=BF16, name="mm_din")
    big["w_in"] = jnp.concatenate([dwin[:AB_COL + 2 * HEADS], dwin[AB_COL + LANE:]], axis=0).reshape(N_DEV, -1, D_MODEL)
    dh1 = _matmul(dproj, wt["w_in"], "nn", name="mm_dh1", after=emit(2, big))
    dx, gr["norm1_g"] = _rmsnorm_bwd(sv["x"], wt["norm1_g"], dh1, dx1, name="rmsnorm_bwd")
    return dx, gr


FETCH_GROUPS = (("w_in", "conv_qkv", "sconv_w"), ("w_out",), ("w_gate", "w_up"), ("w_down",), ("ple_gate", "ple_proj"))
EMIT_GROUPS = (("ple_proj", "ple_gate", "w_down"), ("w_gate", "w_up", "w_out"), ("w_in", "conv_qkv", "sconv_w"))


def _small_weights(w, i):
    return dict(
        norm1_g=w["norm1_g"][i][None], norm2_g=w["norm2_g"][i][None], onorm_g=w["onorm_g"][i][None],
        a_log=_pad_lanes(w["a_log"][i]), dt_bias=_pad_lanes(w["dt_bias"][i]),
        pool_scale=w["pool_scale"][i][None], pool_win=_pool_windows(), pool_wbd=_block_diag_pairs(w["pool_w"][i]))


def _as_read(name, gathered):
    if name == "w_in":
        rows = gathered[:, :D_IN // N_DEV].reshape(-1, D_MODEL)
        return jnp.concatenate([rows[:AB_COL + 2 * HEADS], jnp.zeros((LANE - 2 * HEADS, D_MODEL), BF16),
                                rows[AB_COL + 2 * HEADS:]], axis=0)
    if name in ("conv_qkv", "sconv_w"):
        return _cols_joined(gathered)
    if name == "ple_proj":
        return gathered
    return gathered.reshape(-1, D_MODEL)


def _layer_weights(gathered, w, i):
    return {**_small_weights(w, i), **{k: _as_read(k, g) for k, g in gathered.items()}}


def _local_step(x, p, target, layers, final_g):
    saved = []
    h = x
    for i in range(DEPTH):
        replicated = {k: v for k, v in layers[i].items() if k not in SHARDED}
        h, sv = _layer_fwd(h, p[i], replicated, lambda group, after, i=i: {k: layers[i][k] for k in FETCH_GROUPS[group]})
        saved.append(sv)
    dx, dgf, loss = _loss_head(h, final_g, target, name="loss_head")
    big, small = [{} for _ in range(DEPTH)], [None] * DEPTH
    for i in reversed(range(DEPTH)):
        dx, small[i] = _layer_bwd(dx, saved[i], lambda group, blocks, i=i: big[i].update({k: blocks[k] for k in EMIT_GROUPS[group]}))
    return loss, dx, big, small, dgf


SHARDED = ("w_in", "w_gate", "w_up", "w_down", "w_out", "ple_gate", "ple_proj", "conv_qkv", "sconv_w")
SMALL = ("norm1_g", "a_log", "dt_bias", "onorm_g", "pool_w", "pool_scale", "norm2_g", "final_g")
SLAB_COLS = 1024


def _payload(name, shard):
    if name in ("conv_qkv", "sconv_w"):
        return shard
    out = shard.astype(BF16)
    if name in ("w_gate", "w_up", "w_down"):
        out = jnp.pad(out, ((0, FF_BLOCK - FF_SHARD), (0, 0)))
    if name == "w_in":
        out = jnp.pad(out, ((0, -out.shape[0] % (2 * SUBLANE)), (0, 0)))
    return out


TRANSPOSED = ("w_in", "w_gate", "w_up")


def _ff_rows(t):
    return jnp.transpose(t, (0, 2, 1))


def _slab_rows(shape):
    size = 1
    for s in shape:
        size *= s
    return SUBLANE * -(-size // (SUBLANE * SLAB_COLS))


def _pack_slab(parts, extra_row):
    rows = []
    for name in SMALL:
        flat = parts[name].reshape(-1)
        nrow = _slab_rows(parts[name].shape)
        rows.append(jnp.pad(flat, (0, nrow * SLAB_COLS - flat.shape[0])).reshape(nrow, SLAB_COLS))
    rows.append(jnp.pad(extra_row, ((0, SUBLANE - 1), (0, 0))))
    return jnp.concatenate(rows, axis=0)


def _unpack_slab(slab, shapes):
    out, row = {}, 0
    for name in SMALL:
        size = 1
        for s in shapes[name]:
            size *= s
        out[name] = slab[row:row + _slab_rows(shapes[name])].reshape(-1)[:size].reshape(shapes[name])
        row += _slab_rows(shapes[name])
    return out, row


def kernel(x, p, norm1_g, w_in, conv_qkv, a_log, dt_bias, onorm_g, pool_w, pool_scale, sconv_w, w_out, norm2_g, w_gate, w_up, w_down, ple_proj, ple_gate, final_g, loss_target, m_norm1_g, m_w_in, m_conv_qkv, m_a_log, m_dt_bias, m_onorm_g, m_pool_w, m_pool_scale, m_sconv_w, m_w_out, m_norm2_g, m_w_gate, m_w_up, m_w_down, m_ple_proj, m_ple_gate, m_final_g, v_norm1_g, v_w_in, v_conv_qkv, v_a_log, v_dt_bias, v_onorm_g, v_pool_w, v_pool_scale, v_sconv_w, v_w_out, v_norm2_g, v_w_gate, v_w_up, v_w_down, v_ple_proj, v_ple_gate, v_final_g):
    names = ["norm1_g", "w_in", "conv_qkv", "a_log", "dt_bias", "onorm_g", "pool_w", "pool_scale", "sconv_w", "w_out", "norm2_g",
             "w_gate", "w_up", "w_down", "ple_proj", "ple_gate", "final_g"]
    w = dict(zip(names, [norm1_g, w_in, conv_qkv, a_log, dt_bias, onorm_g, pool_w, pool_scale, sconv_w, w_out, norm2_g, w_gate, w_up,
                         w_down, ple_proj, ple_gate, final_g]))
    m = dict(zip(names, [m_norm1_g, m_w_in, m_conv_qkv, m_a_log, m_dt_bias, m_onorm_g, m_pool_w, m_pool_scale, m_sconv_w, m_w_out,
                         m_norm2_g, m_w_gate, m_w_up, m_w_down, m_ple_proj, m_ple_gate, m_final_g]))
    v = dict(zip(names, [v_norm1_g, v_w_in, v_conv_qkv, v_a_log, v_dt_bias, v_onorm_g, v_pool_w, v_pool_scale, v_sconv_w, v_w_out,
                         v_norm2_g, v_w_gate, v_w_up, v_w_down, v_ple_proj, v_ple_gate, v_final_g]))
    for group in (w, m, v):
        group.update({k: _ff_rows(group[k]) for k in TRANSPOSED})

    first, rest = FETCH_GROUPS[0], tuple(k for members in FETCH_GROUPS[1:] for k in members)
    gathered = dict(zip(first, _all_gather([_payload(k, w[k][0]) for k in first], name="all_gather_weights")))
    (flying0,), token = _exchange_start([[_payload(k, w[k][0]) for k in rest]], CHIP_GATHER, name="gather_start_0")
    replicated = [_small_weights(w, i) for i in range(DEPTH)]
    replicated[0]["norm1_g"] = replicated[0]["norm1_g"] + token[0, 0]
    flying1 = []

    def fetch(i, group, after):
        if i == 0 and group == 1:
            landed = _exchange_wait(flying0, after, CHIP_GATHER, name="gather_wait_0")
            gathered.update(zip(rest, _pair_swap(landed, name="pair_swap")))
            started, token = _exchange_start([[_payload(k, w[k][1]) for k in SHARDED]], CHIP_GATHER, name="gather_start_1",
                                             after=gathered[rest[0]])
            flying1.extend(started)
            return {**{k: _as_read(k, gathered[k]) for k in FETCH_GROUPS[group]},
                    "conv_qkv": _as_read("conv_qkv", gathered["conv_qkv"]) + token[0, 0]}
        if i == 1 and group == 0:
            landed = _exchange_wait(flying1[0], after, CHIP_GATHER, name="gather_wait_1")
            gathered.update(zip(SHARDED, _pair_swap(landed, name="pair_swap")))
        return {k: _as_read(k, gathered[k]) for k in FETCH_GROUPS[group]}

    def reduce_scatter_start(members, blocks, tag):
        mine = [blocks[k] for k in members]
        theirs = _pair_exchange(mine, name="pair_exchange")
        sums = [_pair_add(a, b, name="pair_add") for a, b in zip(mine, theirs)]
        (started,), token = _exchange_start([sums], CHIP_SCATTER, name="exchange_start_" + tag)
        return started, token

    h, saved0 = _layer_fwd(x[0], p[0, 0], replicated[0], functools.partial(fetch, 0))
    h, saved1 = _layer_fwd(h, p[1, 0], replicated[1], functools.partial(fetch, 1))
    dx, dgf, loss_part = _loss_head(h, final_g[None], loss_target[0], name="loss_head")
    small, big1, flying0 = [None] * DEPTH, {}, []
    dx, small[1] = _layer_bwd(dx, saved1, lambda group, blocks: big1.update({k: blocks[k] for k in EMIT_GROUPS[group]}))
    flying1, token = reduce_scatter_start(SHARDED, big1, "1")

    def emit(group, blocks):
        started, token = reduce_scatter_start(EMIT_GROUPS[group], blocks, f"0_{group}")
        flying0.append(started)
        return token

    dx, small[0] = _layer_bwd(dx, saved0, emit, after=token)
    received = [{}, dict(zip(SHARDED, _exchange_wait(flying1, dx, CHIP_SCATTER, name="exchange_wait_1")))]
    for group, members in enumerate(EMIT_GROUPS):
        received[0].update(zip(members, _exchange_wait(flying0[group], dx, CHIP_SCATTER, name=f"exchange_wait_0_{group}")))

    grads = {k: jnp.stack([small[i][k] for i in range(DEPTH)]) for k in small[0]}
    grads = {k: g[:, 0] if k in ("norm1_g", "norm2_g", "onorm_g", "pool_scale") else g for k, g in grads.items()}
    grads["final_g"] = dgf[0]
    loss_row = jnp.pad(loss_part, ((0, 0), (0, SLAB_COLS - LANE)))
    (small_flying,), token = _exchange_start([[_pack_slab(grads, loss_row)]], GATHER, name="small_gather_start")

    out_g, out_d, out_m, out_v = {}, {}, {}, {}
    for k in SHARDED:
        out_g[k], out_d[k], out_m[k], out_v[k] = _adamw_reduce(w[k], [received[i][k] for i in range(DEPTH)], m[k], v[k],
                                                                name="adamw_" + k, after=token)
    behind_all = jnp.stack([out_v[k][0, 0, 0] for k in SHARDED])
    (small_parts,) = _exchange_wait(small_flying, behind_all, GATHER, name="small_gather_wait")
    zero_row = jnp.zeros((1, SLAB_COLS), F32)
    slabs = _adamw_reduce(_pack_slab(w, zero_row)[None], [small_parts], _pack_slab(m, zero_row)[None],
                          _pack_slab(v, zero_row)[None], name="adamw_small")
    slabs = [s[0] for s in slabs]
    shapes = {k: w[k].shape for k in SMALL}
    for dst, slab in zip((out_g, out_d, out_m, out_v), slabs):
        vals, _ = _unpack_slab(slab, shapes)
        dst.update(vals)
    _, loss_at = _unpack_slab(slabs[0], shapes)
    loss = slabs[0][loss_at, 0]
    for group in (out_g, out_d, out_m, out_v):
        group.update({k: _ff_rows(group[k]) for k in TRANSPOSED})

    return (loss, dx[None], *[out_g[k] for k in names], *[out_d[k] for k in names], *[out_m[k] for k in names],
            *[out_v[k] for k in names])
```

```python
import functools

import jax
import jax.numpy as jnp
from jax import lax
from jax.experimental import pallas as pl
from jax.experimental.pallas import tpu as pltpu

F32 = jnp.float32
BF16 = jnp.bfloat16

D_MODEL = 1024
DEPTH = 2
PLE_DIM = 256
EPS = 1e-6
HEAD_DIM = 128
HEADS = 4
A_DIM = HEADS * HEAD_DIM
QKV_TAPS = 4
CHUNK = 64
POOL_WINDOWS = (2, 4, 8, 16)
POOL_DIM = 256
CONV_DIM = 256
CONV_TAPS = 3
D_FF = 2816
D_IN = 3080
D_IN_PAD = 3200
AB_COL = 2048
N_DEV = 8

ADAM_LR = 0.001
ADAM_B1 = 0.9
ADAM_B2 = 0.999
ADAM_EPS = 1e-08
ADAM_WD = 0.01
ADAM_STEP = 10

LANE = 128
SUBLANE = 8
VMEM_BYTES_V7X = 64 * 1024 * 1024
VMEM_LIMIT = 48 * 1024 * 1024

_HI = lax.Precision.HIGHEST
NN = ((1,), (0,))
NT = ((1,), (1,))
TN = ((0,), (0,))
MESH = pl.DeviceIdType.MESH


def _dot(a, b, dims, hi=False):
    if hi:
        return lax.dot_general(a, b, (dims, ((), ())), precision=_HI, preferred_element_type=F32)
    return lax.dot_general(a.astype(BF16), b.astype(BF16), (dims, ((), ())), preferred_element_type=F32)


def _pcall(body, *, name, out_shape, grid=(), in_specs=None, out_specs=None, scratch_shapes=(), semantics=None,
           vmem_limit=None, after=None, **kw):
    params = {}
    if semantics is not None:
        params["dimension_semantics"] = semantics
    if vmem_limit is not None:
        params["vmem_limit_bytes"] = vmem_limit
    if after is not None:
        n_in, inner = len(in_specs), body
        body = lambda *refs: inner(*refs[:n_in], *refs[n_in + 1:])
        in_specs = list(in_specs) + [pl.BlockSpec(after.shape, lambda *_: (0,) * after.ndim)]
    call = pl.pallas_call(
        body, name=name, out_shape=out_shape, grid=grid, in_specs=in_specs, out_specs=out_specs,
        scratch_shapes=list(scratch_shapes), compiler_params=pltpu.CompilerParams(**params), **kw)
    return call if after is None else (lambda *args: call(*args, after))


def _sigmoid(x):
    return 1.0 / (1.0 + jnp.exp(-x))


def _softplus(x):
    return jnp.maximum(x, 0.0) + jnp.log(1.0 + jnp.exp(-jnp.abs(x)))


def _tile(n, cap, mult):
    if n <= cap:
        return n
    best = None
    for t in range(mult, cap + 1, mult):
        if n % t == 0:
            best = t
    assert best is not None, (n, cap, mult)
    return best


ROWS_PER_STEP = 512
NARROW_RESULT = 1024
COLS_PER_DOT = 640


def _resident(weight):
    return pl.BlockSpec(weight.shape, lambda i: (0,) * weight.ndim, pipeline_mode=pl.Buffered(1))


def _matmul_rows(a, b, mode, *, name, res=None, out_dtype=F32, b_blocked=False, after=None):
    m, k = a.shape
    if b_blocked:
        nb, _, bw = b.shape
        n = nb * bw if mode == "nn" else b.shape[1]
    else:
        n = b.shape[1] if mode == "nn" else b.shape[0]
    tm = _tile(m, ROWS_PER_STEP if n > NARROW_RESULT else 2 * ROWS_PER_STEP, 16)
    cn = bw if (b_blocked and mode == "nn") else _tile(n, COLS_PER_DOT, LANE)
    has_res = res is not None

    def body(*refs):
        a_ref, b_ref = refs[0], refs[1]
        res_ref = refs[2] if has_res else None
        o_ref = refs[2 + has_res]
        if not (b_blocked and mode == "nt"):
            av = a_ref[...].astype(BF16)
        for j in range(n // cn):
            cols = pl.ds(j * cn, cn)
            if mode == "nn":
                part = _dot(av, b_ref[j] if b_blocked else b_ref[:, cols], NN)
            elif not b_blocked:
                part = _dot(av, b_ref[cols, :], NT)
            else:
                part = None
                for s in range(nb):
                    term = _dot(a_ref[:, pl.ds(s * bw, bw)], b_ref[s, cols, :], NT)
                    part = term if part is None else part + term
            if has_res:
                part = part + res_ref[:, cols]
            o_ref[:, cols] = part.astype(o_ref.dtype)

    row = lambda width: pl.BlockSpec((tm, width), lambda i: (i, 0))
    whole = _resident(b)
    ins = [a, b] + ([res] if has_res else [])
    specs = [row(k), whole] + ([row(n)] if has_res else [])
    return _pcall(body, name=name, out_shape=jax.ShapeDtypeStruct((m, n), out_dtype), grid=(m // tm,), in_specs=specs,
                  out_specs=row(n), semantics=("parallel",), vmem_limit=VMEM_LIMIT, after=after)(*ins)


def _matmul(a, b, mode, *, name, res=None, out_dtype=F32, b_blocked=False, out_blocked=None, after=None):
    if mode != "tn":
        return _matmul_rows(a, b, mode, name=name, res=res, out_dtype=out_dtype, b_blocked=b_blocked, after=after)
    assert res is None and not b_blocked and after is None
    (t, m), (t2, n) = a.shape, b.shape
    assert t == t2, (a.shape, b.shape)
    tm = _tile(m, 1024, LANE)
    tn = _tile(n, COLS_PER_DOT, LANE)
    if out_blocked is not None:
        assert out_blocked[0] * out_blocked[1] == n
        tn = out_blocked[1]

    def body(a_ref, b_ref, o_ref):
        part = _dot(a_ref[...], b_ref[...], TN).astype(o_ref.dtype)
        if out_blocked is None:
            o_ref[...] = part
        else:
            o_ref[0] = part

    o_spec = (pl.BlockSpec((tm, tn), lambda i, j: (i, j)) if out_blocked is None
              else pl.BlockSpec((1, tm, tn), lambda i, j: (j, i, 0)))
    o_shape = (m, n) if out_blocked is None else (out_blocked[0], m, out_blocked[1])
    return _pcall(body, name=name, out_shape=jax.ShapeDtypeStruct(o_shape, out_dtype), grid=(m // tm, n // tn),
                  in_specs=[pl.BlockSpec((t, tm), lambda i, j: (0, i)), pl.BlockSpec((t, tn), lambda i, j: (0, j))],
                  out_specs=o_spec, semantics=("parallel", "parallel"), vmem_limit=VMEM_LIMIT)(a, b)


ROW_TILE = 512


def _rows(t, width, idx=0):
    return pl.BlockSpec((ROW_TILE, width), lambda i: (i, idx))


def _vec(width):
    return pl.BlockSpec((1, width), lambda i: (0, 0))


def _rmsnorm_fwd(x, g, *, name):
    t, d = x.shape

    def body(x_ref, g_ref, h_ref):
        xv = x_ref[...]
        r = lax.rsqrt(jnp.mean(xv * xv, axis=-1, keepdims=True) + EPS)
        h_ref[...] = (xv * r * g_ref[...]).astype(BF16)

    return _pcall(body, name=name, out_shape=jax.ShapeDtypeStruct((t, d), BF16), grid=(t // ROW_TILE,),
                  in_specs=[_rows(t, d), _vec(d)], out_specs=_rows(t, d), semantics=("parallel",))(x, g)


def _rmsnorm_bwd(x, g, dh, dres, *, name):
    t, d = x.shape

    def body(x_ref, g_ref, dh_ref, dres_ref, dx_ref, dg_ref):
        xv = x_ref[...]
        r = lax.rsqrt(jnp.mean(xv * xv, axis=-1, keepdims=True) + EPS)
        xhat = xv * r
        dhv = dh_ref[...].astype(F32)
        dhg = dhv * g_ref[...]
        dx_ref[...] = dres_ref[...] + r * (dhg - xhat * jnp.mean(dhg * xhat, axis=-1, keepdims=True))
        part = jnp.sum(dhv * xhat, axis=0, keepdims=True)

        @pl.when(pl.program_id(0) == 0)
        def _():
            dg_ref[...] = part

        @pl.when(pl.program_id(0) > 0)
        def _():
            dg_ref[...] += part

    return _pcall(body, name=name, out_shape=(jax.ShapeDtypeStruct((t, d), F32), jax.ShapeDtypeStruct((1, d), F32)),
                  grid=(t // ROW_TILE,), in_specs=[_rows(t, d), _vec(d), _rows(t, d), _rows(t, d)],
                  out_specs=(_rows(t, d), _vec(d)), semantics=("arbitrary",))(x, g, dh, dres)


def _swiglu_fwd(h, w_gate, w_up, *, name):
    t, k = h.shape
    f = w_gate.shape[0]
    tm = _tile(t, ROWS_PER_STEP, 16)
    cn = _tile(f, COLS_PER_DOT, LANE)

    def body(h_ref, wg_ref, wu_ref, ff_ref, gate_ref, up_ref):
        hv = h_ref[...]
        for j in range(f // cn):
            cols = pl.ds(j * cn, cn)
            gv = _dot(hv, wg_ref[cols, :], NT)
            uv = _dot(hv, wu_ref[cols, :], NT)
            gate_ref[:, cols] = gv.astype(BF16)
            up_ref[:, cols] = uv.astype(BF16)
            ff_ref[:, cols] = (gv * _sigmoid(gv) * uv).astype(BF16)

    row = lambda width: pl.BlockSpec((tm, width), lambda i: (i, 0))
    out = jax.ShapeDtypeStruct((t, f), BF16)
    return _pcall(body, name=name, out_shape=(out,) * 3, grid=(t // tm,), in_specs=[row(k), _resident(w_gate), _resident(w_up)],
                  out_specs=(row(f),) * 3, semantics=("parallel",), vmem_limit=VMEM_LIMIT)(h, w_gate, w_up)


def _swiglu_bwd(dx2, w_down, gate, up, *, name, after=None):
    t, d = dx2.shape
    f = w_down.shape[0]
    tm = _tile(t, ROWS_PER_STEP, 16)
    cn = _tile(f, COLS_PER_DOT, LANE)

    def body(dx_ref, w_ref, gate_ref, up_ref, dgate_ref, dup_ref):
        dxv = dx_ref[...].astype(BF16)
        for j in range(f // cn):
            cols = pl.ds(j * cn, cn)
            dffv = _dot(dxv, w_ref[cols, :], NT)
            gv = gate_ref[:, cols].astype(F32)
            sig = _sigmoid(gv)
            dgate_ref[:, cols] = (dffv * up_ref[:, cols].astype(F32) * sig * (1.0 + gv * (1.0 - sig))).astype(BF16)
            dup_ref[:, cols] = (dffv * gv * sig).astype(BF16)

    row = lambda width: pl.BlockSpec((tm, width), lambda i: (i, 0))
    out = jax.ShapeDtypeStruct((t, f), BF16)
    return _pcall(body, name=name, out_shape=(out, out), grid=(t // tm,), in_specs=[row(d), _resident(w_down), row(f), row(f)],
                  out_specs=(row(f), row(f)), semantics=("parallel",), vmem_limit=VMEM_LIMIT, after=after)(dx2, w_down, gate, up)


def _ple_fwd(x2, pgl, pp, *, name):
    t, d = x2.shape

    def body(x_ref, pgl_ref, pp_ref, o_ref):
        o_ref[...] = x_ref[...] + _sigmoid(pgl_ref[...]) * pp_ref[...]

    return _pcall(body, name=name, out_shape=jax.ShapeDtypeStruct((t, d), F32), grid=(t // ROW_TILE,),
                  in_specs=[_rows(t, d)] * 3, out_specs=_rows(t, d), semantics=("parallel",))(x2, pgl, pp)


def _ple_bwd(dx3, pgl, pp, *, name, after=None):
    t, d = dx3.shape

    def body(dx_ref, pgl_ref, pp_ref, dpgl_ref, dpp_ref):
        dxv = dx_ref[...]
        sig = _sigmoid(pgl_ref[...])
        dpp_ref[...] = (dxv * sig).astype(BF16)
        dpgl_ref[...] = (dxv * pp_ref[...] * sig * (1.0 - sig)).astype(BF16)

    return _pcall(body, name=name, out_shape=(jax.ShapeDtypeStruct((t, d), BF16),) * 2, grid=(t // ROW_TILE,),
                  in_specs=[_rows(t, d)] * 3, out_specs=(_rows(t, d),) * 2, semantics=("parallel",), after=after)(dx3, pgl, pp)


def _loss_head(x3, g, target, *, name):
    t, d = x3.shape

    def body(x_ref, g_ref, t_ref, dx_ref, dg_ref, loss_ref):
        xv = x_ref[...]
        r = lax.rsqrt(jnp.mean(xv * xv, axis=-1, keepdims=True) + EPS)
        xhat = xv * r
        gv = g_ref[...]
        err = xhat * gv - t_ref[...]
        row_loss = jnp.sum(err * err, axis=-1, keepdims=True) * (0.5 / d)
        lpart = jnp.broadcast_to(jnp.sum(row_loss, axis=0, keepdims=True), (1, LANE))
        dy = err * (1.0 / d)
        dyg = dy * gv
        dx_ref[...] = r * (dyg - xhat * jnp.mean(dyg * xhat, axis=-1, keepdims=True))
        gpart = jnp.sum(dy * xhat, axis=0, keepdims=True)

        @pl.when(pl.program_id(0) == 0)
        def _():
            dg_ref[...] = gpart
            loss_ref[...] = lpart

        @pl.when(pl.program_id(0) > 0)
        def _():
            dg_ref[...] += gpart
            loss_ref[...] += lpart

    return _pcall(body, name=name,
                  out_shape=(jax.ShapeDtypeStruct((t, d), F32), jax.ShapeDtypeStruct((1, d), F32), jax.ShapeDtypeStruct((1, LANE), F32)),
                  grid=(t // ROW_TILE,), in_specs=[_rows(t, d), _vec(d), _rows(t, d)],
                  out_specs=(_rows(t, d), _vec(d), _vec(LANE)), semantics=("arbitrary",))(x3, g, target)


def _shift_down(x, d):
    if d == 0:
        return x
    row = lax.broadcasted_iota(jnp.int32, x.shape, 0)
    return jnp.where(row >= d, pltpu.roll(x, d, 0), 0.0)


def _shift_up(x, d):
    if d == 0:
        return x
    t = x.shape[0]
    row = lax.broadcasted_iota(jnp.int32, x.shape, 0)
    return jnp.where(row < t - d, pltpu.roll(x, t - d, 0), 0.0)


def _colsum(x):
    return jnp.sum(x, axis=0, keepdims=True)


def _col(t, idx_fn):
    return pl.BlockSpec((t, LANE), idx_fn)


def _conv_fwd(x, w_ref, taps):
    acc = None
    for j in range(taps):
        term = w_ref[pl.ds(j, 1), :] * _shift_down(x, taps - 1 - j)
        acc = term if acc is None else acc + term
    return acc


def _conv_bwd(x, dy, w_ref, dw_ref, taps):
    dx = None
    for j in range(taps):
        term = w_ref[pl.ds(j, 1), :] * _shift_up(dy, taps - 1 - j)
        dx = term if dx is None else dx + term
        dw_ref[pl.ds(j, 1), :] = _colsum(dy * _shift_down(x, taps - 1 - j))
    return dx


def _qkv_prep_fwd(proj, conv_w, *, name):
    t = proj.shape[0]
    scale = HEAD_DIM ** -0.5

    def body(x_ref, w_ref, o_ref):
        j = pl.program_id(0)
        c = _conv_fwd(x_ref[...], w_ref, QKV_TAPS)
        s = c * _sigmoid(c)
        r = lax.rsqrt(jnp.sum(s * s, axis=-1, keepdims=True) + EPS)
        f = jnp.where(j < 2 * HEADS, r, 1.0) * jnp.where(j < HEADS, scale, 1.0)
        o_ref[0] = s * f

    return _pcall(body, name=name, out_shape=jax.ShapeDtypeStruct((3 * HEADS, t, LANE), F32), grid=(3 * HEADS,),
                  in_specs=[_col(t, lambda j: (0, j)), pl.BlockSpec((QKV_TAPS, LANE), lambda j: (0, j))],
                  out_specs=pl.BlockSpec((1, t, LANE), lambda j: (j, 0, 0)), semantics=("parallel",),
                  vmem_limit=VMEM_LIMIT)(proj, conv_w)


def _qkv_prep_bwd(proj, conv_w, dqkv, *, name):
    t = proj.shape[0]
    scale = HEAD_DIM ** -0.5

    def body(x_ref, w_ref, d_ref, dx_ref, dw_ref):
        j = pl.program_id(0)
        xv = x_ref[...]
        c = _conv_fwd(xv, w_ref, QKV_TAPS)
        sig = _sigmoid(c)
        s = c * sig
        r = lax.rsqrt(jnp.sum(s * s, axis=-1, keepdims=True) + EPS)
        n0 = s * r
        dv = d_ref[0]
        dn0 = dv * jnp.where(j < HEADS, scale, 1.0)
        ds_norm = r * (dn0 - n0 * jnp.sum(dn0 * n0, axis=-1, keepdims=True))
        ds = jnp.where(j < 2 * HEADS, ds_norm, dv)
        dc = ds * sig * (1.0 + c * (1.0 - sig))
        dx_ref[...] = _conv_bwd(xv, dc, w_ref, dw_ref, QKV_TAPS).astype(BF16)

    return _pcall(body, name=name,
                  out_shape=(jax.ShapeDtypeStruct((t, 3 * A_DIM), BF16), jax.ShapeDtypeStruct((QKV_TAPS, 3 * A_DIM), F32)),
                  grid=(3 * HEADS,),
                  in_specs=[_col(t, lambda j: (0, j)), pl.BlockSpec((QKV_TAPS, LANE), lambda j: (0, j)),
                            pl.BlockSpec((1, t, LANE), lambda j: (j, 0, 0))],
                  out_specs=(_col(t, lambda j: (0, j)), pl.BlockSpec((QKV_TAPS, LANE), lambda j: (0, j))),
                  semantics=("parallel",), vmem_limit=VMEM_LIMIT)(proj, conv_w, dqkv)


def _lane_pick(x, lane_idx, lane):
    return jnp.broadcast_to(jnp.sum(jnp.where(lane == lane_idx, x, 0.0), axis=-1, keepdims=True), x.shape)


def _gates_fwd(proj, alog, dtb, *, name):
    t = proj.shape[0]

    def body(x_ref, alog_ref, dtb_ref, g_ref, b_ref):
        xv = x_ref[...]
        lane = lax.broadcasted_iota(jnp.int32, xv.shape, 1)
        gall = -jnp.exp(alog_ref[...]) * _softplus(xv + dtb_ref[...])
        ball = _sigmoid(xv)
        for h in range(HEADS):
            g_ref[h] = _lane_pick(gall, h, lane)
            b_ref[h] = _lane_pick(ball, HEADS + h, lane)

    out = jax.ShapeDtypeStruct((HEADS, t, LANE), F32)
    whole = pl.BlockSpec((HEADS, t, LANE), lambda i: (0, 0, 0))
    return _pcall(body, name=name, out_shape=(out, out), grid=(1,),
                  in_specs=[_col(t, lambda i: (0, AB_COL // LANE)), _vec(LANE), _vec(LANE)], out_specs=(whole, whole),
                  semantics=("arbitrary",), vmem_limit=VMEM_LIMIT)(proj, alog, dtb)


def _gates_bwd(proj, alog, dtb, dg, dbeta, *, name):
    t = proj.shape[0]

    def body(x_ref, alog_ref, dtb_ref, dg_ref, db_ref, dab_ref, dalog_ref, ddtb_ref):
        xv = x_ref[...]
        lane = lax.broadcasted_iota(jnp.int32, xv.shape, 1)
        lane1 = lax.broadcasted_iota(jnp.int32, (1, LANE), 1)
        z = xv + dtb_ref[...]
        nea = -jnp.exp(alog_ref[...])
        da_f = nea * _sigmoid(z)
        g_f = nea * _softplus(z)
        ball = _sigmoid(xv)
        db_f = ball * (1.0 - ball)
        dab = jnp.zeros_like(xv)
        dalog = jnp.zeros((1, LANE), F32)
        for h in range(HEADS):
            dgh = dg_ref[h]
            dab = dab + jnp.where(lane == h, dgh * da_f, 0.0) + jnp.where(lane == HEADS + h, db_ref[h] * db_f, 0.0)
            dalog = dalog + jnp.where(lane1 == h, _colsum(dgh * g_f), 0.0)
        dab_ref[...] = dab.astype(BF16)
        dalog_ref[...] = dalog
        ddtb_ref[...] = jnp.where(lane1 < HEADS, _colsum(dab), 0.0)

    whole = pl.BlockSpec((HEADS, t, LANE), lambda i: (0, 0, 0))
    vec = jax.ShapeDtypeStruct((1, LANE), F32)
    return _pcall(body, name=name, out_shape=(jax.ShapeDtypeStruct((t, LANE), BF16), vec, vec), grid=(1,),
                  in_specs=[_col(t, lambda i: (0, AB_COL // LANE)), _vec(LANE), _vec(LANE), whole, whole],
                  out_specs=(_col(t, lambda i: (0, 0)), _vec(LANE), _vec(LANE)), semantics=("arbitrary",),
                  vmem_limit=VMEM_LIMIT)(proj, alog, dtb, dg, dbeta)


Z_COL = 3 * A_DIM // LANE


def _apost_fwd(o, proj, gn, *, name):
    t = proj.shape[0]

    def body(o_ref, z_ref, gn_ref, y_ref):
        ov = o_ref[0]
        z = z_ref[...]
        r = lax.rsqrt(jnp.mean(ov * ov, axis=-1, keepdims=True) + EPS)
        y_ref[...] = (ov * r * gn_ref[...] * (z * _sigmoid(z))).astype(BF16)

    return _pcall(body, name=name, out_shape=jax.ShapeDtypeStruct((t, A_DIM), BF16), grid=(HEADS,),
                  in_specs=[pl.BlockSpec((1, t, LANE), lambda h: (h, 0, 0)), _col(t, lambda h: (0, Z_COL + h)),
                            pl.BlockSpec((1, LANE), lambda h: (0, 0))],
                  out_specs=_col(t, lambda h: (0, h)), semantics=("parallel",), vmem_limit=VMEM_LIMIT)(o, proj, gn)


def _apost_bwd(o, proj, gn, dmixed, *, name):
    t = proj.shape[0]

    def body(o_ref, z_ref, gn_ref, d_ref, do_ref, dz_ref, dgn_ref):
        ov = o_ref[0]
        z = z_ref[...]
        gnv = gn_ref[...]
        dv = d_ref[...]
        r = lax.rsqrt(jnp.mean(ov * ov, axis=-1, keepdims=True) + EPS)
        ohat = ov * r
        sig = _sigmoid(z)
        dy = dv * (z * sig)
        dz_ref[...] = (dv * ohat * gnv * sig * (1.0 + z * (1.0 - sig))).astype(BF16)
        dyo = dy * gnv
        do_ref[0] = r * (dyo - ohat * jnp.mean(dyo * ohat, axis=-1, keepdims=True))
        part = _colsum(dy * ohat)

        @pl.when(pl.program_id(0) == 0)
        def _():
            dgn_ref[...] = part

        @pl.when(pl.program_id(0) > 0)
        def _():
            dgn_ref[...] += part

    return _pcall(body, name=name,
                  out_shape=(jax.ShapeDtypeStruct((HEADS, t, LANE), F32), jax.ShapeDtypeStruct((t, A_DIM), BF16),
                             jax.ShapeDtypeStruct((1, LANE), F32)),
                  grid=(HEADS,),
                  in_specs=[pl.BlockSpec((1, t, LANE), lambda h: (h, 0, 0)), _col(t, lambda h: (0, Z_COL + h)),
                            pl.BlockSpec((1, LANE), lambda h: (0, 0)), _col(t, lambda h: (0, h))],
                  out_specs=(pl.BlockSpec((1, t, LANE), lambda h: (h, 0, 0)), _col(t, lambda h: (0, h)),
                             pl.BlockSpec((1, LANE), lambda h: (0, 0))),
                  semantics=("arbitrary",), vmem_limit=VMEM_LIMIT)(o, proj, gn, dmixed)


POOL_COL = (AB_COL + LANE) // LANE
CB_COL = POOL_COL + POOL_DIM // LANE
CC_COL = CB_COL + CONV_DIM // LANE
CH_COL = CC_COL + CONV_DIM // LANE
MAX_WIN_LOG2 = 4


def _window_sums(x, shift):
    sums = []
    cur = x
    for k in range(MAX_WIN_LOG2):
        cur = cur + shift(cur, 1 << k)
        sums.append(cur)
    return sums


def _pick_window(sums, win):
    out = sums[-1]
    for k in range(MAX_WIN_LOG2 - 2, -1, -1):
        out = jnp.where(win == float(2 << k), sums[k], out)
    return out


def _pool_counts(shape, win):
    row = lax.broadcasted_iota(jnp.int32, shape, 0).astype(F32)
    return jnp.minimum(row + 1.0, win)


def _pool_fwd(proj, win, wbd, scale, *, name):
    t = proj.shape[0]

    def body(x_ref, win_ref, w_ref, s_ref, y_ref):
        xv = x_ref[...]
        winv = win_ref[...]
        pooled = _pick_window(_window_sums(xv, _shift_down), winv) / _pool_counts(xv.shape, winv) - xv
        y_ref[...] = (_dot(pooled, w_ref[0], NN) * s_ref[...]).astype(BF16)

    nb = POOL_DIM // LANE
    vec = pl.BlockSpec((1, LANE), lambda b: (0, b))
    return _pcall(body, name=name, out_shape=jax.ShapeDtypeStruct((t, POOL_DIM), BF16), grid=(nb,),
                  in_specs=[_col(t, lambda b: (0, POOL_COL + b)), vec, pl.BlockSpec((1, LANE, LANE), lambda b: (b, 0, 0)), vec],
                  out_specs=_col(t, lambda b: (0, b)), semantics=("parallel",), vmem_limit=VMEM_LIMIT)(proj, win, wbd, scale)


def _pool_bwd(proj, win, wbd, scale, dmixed, *, name):
    t = proj.shape[0]

    def body(x_ref, win_ref, w_ref, s_ref, d_ref, dx_ref, dw_ref, ds_ref):
        xv = x_ref[...]
        winv = win_ref[...]
        cnt = _pool_counts(xv.shape, winv)
        pooled = _pick_window(_window_sums(xv, _shift_down), winv) / cnt - xv
        dv = d_ref[...]
        ds_ref[...] = _colsum(dv * _dot(pooled, w_ref[0], NN))
        dy0 = dv * s_ref[...]
        dw_ref[0] = _dot(pooled, dy0, TN)
        dpooled = _dot(dy0, w_ref[0], NT)
        dmean = dpooled / cnt
        dx_ref[...] = (_pick_window(_window_sums(dmean, _shift_up), winv) - dpooled).astype(BF16)

    nb = POOL_DIM // LANE
    vec = pl.BlockSpec((1, LANE), lambda b: (0, b))
    mat = pl.BlockSpec((1, LANE, LANE), lambda b: (b, 0, 0))
    first = A_DIM // LANE
    return _pcall(body, name=name,
                  out_shape=(jax.ShapeDtypeStruct((t, POOL_DIM), BF16), jax.ShapeDtypeStruct((nb, LANE, LANE), F32),
                             jax.ShapeDtypeStruct((1, POOL_DIM), F32)),
                  grid=(nb,),
                  in_specs=[_col(t, lambda b: (0, POOL_COL + b)), vec, mat, vec, _col(t, lambda b: (0, first + b))],
                  out_specs=(_col(t, lambda b: (0, b)), mat, vec), semantics=("parallel",),
                  vmem_limit=VMEM_LIMIT)(proj, win, wbd, scale, dmixed)


def _sconv_fwd(proj, w, *, name):
    t = proj.shape[0]

    def body(cb_ref, cc_ref, ch_ref, w_ref, y_ref):
        y_ref[...] = (cb_ref[...] * _conv_fwd(cc_ref[...] * ch_ref[...], w_ref, CONV_TAPS)).astype(BF16)

    nb = CONV_DIM // LANE
    return _pcall(body, name=name, out_shape=jax.ShapeDtypeStruct((t, CONV_DIM), BF16), grid=(nb,),
                  in_specs=[_col(t, lambda b: (0, CB_COL + b)), _col(t, lambda b: (0, CC_COL + b)),
                            _col(t, lambda b: (0, CH_COL + b)), pl.BlockSpec((CONV_TAPS, LANE), lambda b: (0, b))],
                  out_specs=_col(t, lambda b: (0, b)), semantics=("parallel",), vmem_limit=VMEM_LIMIT)(proj, proj, proj, w)


def _sconv_bwd(proj, w, dmixed, *, name):
    t = proj.shape[0]

    def body(cb_ref, cc_ref, ch_ref, w_ref, d_ref, dcb_ref, dcc_ref, dch_ref, dw_ref):
        cc = cc_ref[...]
        ch = ch_ref[...]
        u = cc * ch
        dv = d_ref[...]
        dcb_ref[...] = (dv * _conv_fwd(u, w_ref, CONV_TAPS)).astype(BF16)
        du = _conv_bwd(u, dv * cb_ref[...], w_ref, dw_ref, CONV_TAPS)
        dcc_ref[...] = (du * ch).astype(BF16)
        dch_ref[...] = (du * cc).astype(BF16)

    nb = CONV_DIM // LANE
    first = (A_DIM + POOL_DIM) // LANE
    act = jax.ShapeDtypeStruct((t, CONV_DIM), BF16)
    wspec = pl.BlockSpec((CONV_TAPS, LANE), lambda b: (0, b))
    ospec = _col(t, lambda b: (0, b))
    return _pcall(body, name=name, out_shape=(act, act, act, jax.ShapeDtypeStruct((CONV_TAPS, CONV_DIM), F32)), grid=(nb,),
                  in_specs=[_col(t, lambda b: (0, CB_COL + b)), _col(t, lambda b: (0, CC_COL + b)),
                            _col(t, lambda b: (0, CH_COL + b)), wspec, _col(t, lambda b: (0, first + b))],
                  out_specs=(ospec, ospec, ospec, wspec), semantics=("parallel",),
                  vmem_limit=VMEM_LIMIT)(proj, proj, proj, w, dmixed)


def _chunk_masks():
    r = lax.broadcasted_iota(jnp.int32, (CHUNK, CHUNK), 0)
    c = lax.broadcasted_iota(jnp.int32, (CHUNK, CHUNK), 1)
    return r >= c, r > c, jnp.where(r == c, 1.0, 0.0).astype(F32)


def _split(a):
    hi = a.astype(BF16)
    return hi, (a - hi.astype(F32)).astype(BF16)


def _dot_split(a, b, dims):
    (ah, al), (bh, bl) = a, b
    return _dot(ah, bh, dims) + _dot(ah, bl, dims) + _dot(al, bh, dims)


def _tri_inv(lows, eye):
    xs = [eye - low for low in lows]
    ps = [_split(low) for low in lows]
    ps = [_split(_dot_split(p, p, NN)) for p in ps]
    for i in range(5):
        xs = [x + _dot_split(_split(x), p, NN) for x, p in zip(xs, ps)]
        if i < 4:
            ps = [_split(_dot_split(p, p, NN)) for p in ps]
    return xs


def _prefix_sum_rows(x):
    for k in range(6):
        x = x + _shift_down(x, 1 << k)
    return x


def _suffix_sum_rows(x):
    for k in range(6):
        x = x + _shift_up(x, 1 << k)
    return x


def _chunk_decay(g, incl):
    gcb = _prefix_sum_rows(g)
    gtot = _colsum(g)
    col = gcb[:, :CHUNK]
    row = gcb.T[:CHUNK, :]
    decay = jnp.exp(jnp.where(incl, col - row, -1e30))
    return gcb, gtot, decay


CHUNKS_PER_STEP = 4


def _heads_of(ref, base, rows):
    return [ref[base + h, rows, :] for h in range(HEADS)]


def _chunk_rows(j):
    return pl.ds(j * CHUNK, CHUNK)


def _deltanet_prep(qkv, g, beta, *, name):
    t = qkv.shape[1]
    n_chunks = t // CHUNK
    per = CHUNKS_PER_STEP
    probs = [(j, h) for j in range(per) for h in range(HEADS)]

    def body(qkv_ref, g_ref, b_ref, u_ref, w_ref, qg_ref, kg_ref, attn_ref, tm_ref):
        incl, strict, eye = _chunk_masks()
        q = [qkv_ref[h, _chunk_rows(j), :] for j, h in probs]
        k = [qkv_ref[HEADS + h, _chunk_rows(j), :] for j, h in probs]
        v = [qkv_ref[2 * HEADS + h, _chunk_rows(j), :] for j, h in probs]
        bv = [b_ref[h, _chunk_rows(j), :] for j, h in probs]
        dec = [_chunk_decay(g_ref[h, _chunk_rows(j), :], incl) for j, h in probs]
        kb = [a * b for a, b in zip(k, bv)]
        low = [jnp.where(strict, _dot(a, b, NT) * d[2], 0.0) for a, b, d in zip(kb, k, dec)]
        tm = _tri_inv(low, eye)
        egc = [jnp.exp(d[0]) for d in dec]
        u = [_dot(m, a * b, NN) for m, a, b in zip(tm, v, bv)]
        w = [_dot(m, a * e, NN) for m, a, e in zip(tm, kb, egc)]
        attn = [_dot(a, b, NT) * d[2] for a, b, d in zip(q, k, dec)]
        for i, (j, h) in enumerate(probs):
            rows = _chunk_rows(j)
            u_ref[h, rows, :] = u[i]
            w_ref[h, rows, :] = w[i].astype(BF16)
            qg_ref[h, rows, :] = (q[i] * egc[i]).astype(BF16)
            kg_ref[h, rows, :] = (k[i] * jnp.exp(dec[i][1] - dec[i][0])).astype(BF16)
            attn_ref[j, h] = attn[i].astype(BF16)
            tm_ref[j, h] = tm[i]

    act = lambda heads: pl.BlockSpec((heads, per * CHUNK, LANE), lambda n: (0, n, 0))
    mat = pl.BlockSpec((per, HEADS, CHUNK, CHUNK), lambda n: (n, 0, 0, 0))
    return _pcall(
        body, name=name,
        out_shape=(jax.ShapeDtypeStruct((HEADS, t, LANE), F32),) + (jax.ShapeDtypeStruct((HEADS, t, LANE), BF16),) * 3
        + (jax.ShapeDtypeStruct((n_chunks, HEADS, CHUNK, CHUNK), BF16), jax.ShapeDtypeStruct((n_chunks, HEADS, CHUNK, CHUNK), F32)),
        grid=(n_chunks // per,), in_specs=[act(3 * HEADS), act(HEADS), act(HEADS)],
        out_specs=(act(HEADS),) * 4 + (mat, mat), semantics=("parallel",), vmem_limit=VMEM_LIMIT)(qkv, g, beta)


SCAN_CHUNKS_PER_STEP = 8


def _deltanet_scan(u, w, qg, kg, attn, g, *, name):
    t = u.shape[1]
    n_chunks = t // CHUNK
    per = SCAN_CHUNKS_PER_STEP

    def body(u_ref, w_ref, qg_ref, kg_ref, attn_ref, g_ref, o_ref, vn_ref, st_ref, s_ref):
        @pl.when(pl.program_id(0) == 0)
        def _():
            s_ref[...] = jnp.zeros_like(s_ref)

        for j in range(per):
            rows = _chunk_rows(j)
            s = [s_ref[h] for h in range(HEADS)]
            vn = [u_ref[h, rows, :] - _dot(w_ref[h, rows, :], s[h], NN) for h in range(HEADS)]
            o = [_dot(qg_ref[h, rows, :], s[h], NN) + _dot(attn_ref[j, h], vn[h], NN) for h in range(HEADS)]
            eg = [jnp.exp(_colsum(g_ref[h, rows, :])) for h in range(HEADS)]
            for h in range(HEADS):
                st_ref[j, h] = s[h]
                s_ref[h] = s[h] * eg[h] + _dot(kg_ref[h, rows, :], vn[h], TN)
                o_ref[h, rows, :] = o[h]
                vn_ref[h, rows, :] = vn[h]

    act = pl.BlockSpec((HEADS, per * CHUNK, LANE), lambda n: (0, n, 0))
    out = jax.ShapeDtypeStruct((HEADS, t, LANE), F32)
    return _pcall(
        body, name=name, out_shape=(out, out, jax.ShapeDtypeStruct((n_chunks, HEADS, LANE, LANE), F32)), grid=(n_chunks // per,),
        in_specs=[act] * 4 + [pl.BlockSpec((per, HEADS, CHUNK, CHUNK), lambda n: (n, 0, 0, 0)), act],
        out_specs=(act, act, pl.BlockSpec((per, HEADS, LANE, LANE), lambda n: (n, 0, 0, 0))),
        scratch_shapes=[pltpu.VMEM((HEADS, LANE, LANE), F32)], semantics=("arbitrary",))(u, w, qg, kg, attn, g)


def _deltanet_bscan(w, qg, kg, attn, g, do, *, name):
    t = w.shape[1]
    n_chunks = t // CHUNK
    per = SCAN_CHUNKS_PER_STEP
    steps = n_chunks // per

    def body(w_ref, qg_ref, kg_ref, attn_ref, g_ref, do_ref, dvn_ref, dsn_ref, ds_ref):
        @pl.when(pl.program_id(0) == 0)
        def _():
            ds_ref[...] = jnp.zeros_like(ds_ref)

        for j in reversed(range(per)):
            rows = _chunk_rows(j)
            dsn = [ds_ref[h] for h in range(HEADS)]
            dov = [do_ref[h, rows, :] for h in range(HEADS)]
            dvn = [_dot(attn_ref[j, h], dov[h], TN) + _dot(kg_ref[h, rows, :], dsn[h], NN) for h in range(HEADS)]
            eg = [jnp.exp(_colsum(g_ref[h, rows, :])) for h in range(HEADS)]
            for h in range(HEADS):
                dsn_ref[j, h] = dsn[h]
                ds_ref[h] = _dot(qg_ref[h, rows, :], dov[h], TN) + eg[h] * dsn[h] - _dot(w_ref[h, rows, :], dvn[h], TN)
                dvn_ref[h, rows, :] = dvn[h]

    act = pl.BlockSpec((HEADS, per * CHUNK, LANE), lambda n: (0, steps - 1 - n, 0))
    return _pcall(
        body, name=name,
        out_shape=(jax.ShapeDtypeStruct((HEADS, t, LANE), F32), jax.ShapeDtypeStruct((n_chunks, HEADS, LANE, LANE), F32)),
        grid=(steps,),
        in_specs=[act] * 3 + [pl.BlockSpec((per, HEADS, CHUNK, CHUNK), lambda n: (steps - 1 - n, 0, 0, 0)), act, act],
        out_specs=(act, pl.BlockSpec((per, HEADS, LANE, LANE), lambda n: (steps - 1 - n, 0, 0, 0))),
        scratch_shapes=[pltpu.VMEM((HEADS, LANE, LANE), F32)], semantics=("arbitrary",))(w, qg, kg, attn, g, do)


def _sum_all(x):
    return jnp.sum(jnp.sum(x, axis=1, keepdims=True), axis=0, keepdims=True)


def _rowsum(x):
    return jnp.sum(x, axis=1, keepdims=True)


def _deltanet_post(qkv, g, beta, tmats, states, dstates, do, dvn, vn, *, name):
    t = qkv.shape[1]
    n_chunks = t // CHUNK
    per = CHUNKS_PER_STEP
    probs = [(j, h) for j in range(per) for h in range(HEADS)]

    def body(qkv_ref, g_ref, b_ref, tm_ref, st_ref, dsn_ref, do_ref, dvn_ref, vn_ref, dqkv_ref, dg_ref, db_ref):
        incl, strict, _ = _chunk_masks()
        ones = jnp.ones((CHUNK, LANE), BF16)
        last_row = lax.broadcasted_iota(jnp.int32, (CHUNK, LANE), 0) == CHUNK - 1
        z = lambda f, *cols: [f(*a) for a in zip(*cols)]
        q = [qkv_ref[h, _chunk_rows(j), :] for j, h in probs]
        k = [qkv_ref[HEADS + h, _chunk_rows(j), :] for j, h in probs]
        v = [qkv_ref[2 * HEADS + h, _chunk_rows(j), :] for j, h in probs]
        bv = [b_ref[h, _chunk_rows(j), :] for j, h in probs]
        dov = [do_ref[h, _chunk_rows(j), :] for j, h in probs]
        dvn_ = [dvn_ref[h, _chunk_rows(j), :] for j, h in probs]
        vn_ = [vn_ref[h, _chunk_rows(j), :] for j, h in probs]
        tm = [tm_ref[j, h] for j, h in probs]
        s = [st_ref[j, h] for j, h in probs]
        dsn = [dsn_ref[j, h] for j, h in probs]
        dec = [_chunk_decay(g_ref[h, _chunk_rows(j), :], incl) for j, h in probs]
        decay = [d[2] for d in dec]
        egc = [jnp.exp(d[0]) for d in dec]
        ekg = [jnp.exp(d[1] - d[0]) for d in dec]
        kb = z(lambda a, b: a * b, k, bv)
        vb = z(lambda a, b: a * b, v, bv)
        kbg = z(lambda a, b: a * b, kb, egc)
        qg = z(lambda a, b: a * b, q, egc)
        kg = z(lambda a, b: a * b, k, ekg)
        kk = z(lambda a, b: _dot(a, b, NT), kb, k)
        qk = z(lambda a, b: _dot(a, b, NT), q, k)
        dattn = z(lambda a, b: jnp.where(incl, _dot(a, b, NT), 0.0), dov, vn_)
        dqg = z(lambda a, b: _dot(a, b, NT), dov, s)
        dkg = z(lambda a, b: _dot(a, b, NT), vn_, dsn)
        dglast = z(lambda a, b, c, d, e: _sum_all(a * b) * jnp.exp(e[1]) + _sum_all(c * d), s, dsn, dkg, kg, dec)
        dw = z(lambda a, b: -_dot(a, b, NT), dvn_, s)
        dtm = z(lambda a, b, c, d: _dot(a, b, NT) + _dot(c, d, NT), dvn_, vb, dw, kbg)
        dvb = z(lambda a, b: _dot(a, b, TN), tm, dvn_)
        dkbg = z(lambda a, b: _dot(a, b, TN), tm, dw)
        dlow = z(lambda a, b: jnp.where(strict, -_dot(_dot(a, b, TN), a, NT), 0.0), tm, dtm)
        dkk = z(lambda a, b: a * b, dlow, decay)
        dqk = z(lambda a, b: a * b, dattn, decay)
        dkb = z(lambda a, b, c, d: _dot(a, b, NN) + c * d, dkk, k, dkbg, egc)
        dk = z(lambda a, b, c, d, e, f, g_, h_: _dot(a, b, TN) + _dot(c, d, TN) + e * f + g_ * h_, dkk, kb, dqk, q, dkg, ekg, dkb, bv)
        dq = z(lambda a, b, c, d: _dot(a, b, NN) + c * d, dqk, k, dqg, egc)
        m = z(lambda a, b, c, d, e: (a * b + c * d) * e, dlow, kk, dattn, qk, decay)
        mcol = [_dot(mh, ones, TN) + _dot(ml, ones, TN) for mh, ml in (_split(a) for a in m)]
        for i, (j, h) in enumerate(probs):
            rows = _chunk_rows(j)
            dqkv_ref[h, rows, :] = dq[i]
            dqkv_ref[HEADS + h, rows, :] = dk[i]
            dqkv_ref[2 * HEADS + h, rows, :] = dvb[i] * bv[i]
            db_ref[h, rows, :] = jnp.broadcast_to(_rowsum(dkb[i] * k[i] + dvb[i] * v[i]), (CHUNK, LANE))
            dgc = (_rowsum(dqg[i] * qg[i] + dkbg[i] * kbg[i] - dkg[i] * kg[i]) + _rowsum(m[i]) - mcol[i]
                   + jnp.where(last_row, dglast[i], 0.0))
            dg_ref[h, rows, :] = _suffix_sum_rows(dgc)

    act = lambda heads: pl.BlockSpec((heads, per * CHUNK, LANE), lambda n: (0, n, 0))
    mat = lambda d: pl.BlockSpec((per, HEADS, d, d), lambda n: (n, 0, 0, 0))
    out = jax.ShapeDtypeStruct((HEADS, t, LANE), F32)
    return _pcall(
        body, name=name, out_shape=(jax.ShapeDtypeStruct((3 * HEADS, t, LANE), F32), out, out), grid=(n_chunks // per,),
        in_specs=[act(3 * HEADS), act(HEADS), act(HEADS), mat(CHUNK), mat(LANE), mat(LANE), act(HEADS), act(HEADS), act(HEADS)],
        out_specs=(act(3 * HEADS), act(HEADS), act(HEADS)), semantics=("parallel",),
        vmem_limit=VMEM_LIMIT)(qkv, g, beta, tmats, states, dstates, do, dvn, vn)


ANY = pl.BlockSpec(memory_space=pl.ANY)
PEERS = N_DEV - 1


def _all_gather(arrays, *, name):
    n = len(arrays)

    def body(*refs):
        ins, outs = refs[:n], refs[n:2 * n]
        send_sems, recv_sems, local_sems = refs[2 * n:]
        x, y, c = lax.axis_index("x"), lax.axis_index("y"), lax.axis_index("c")
        me, sibling = (x, y, c), (x, y, 1 - c)
        chips = [(1 - x, y), (x, 1 - y), (1 - x, 1 - y)]

        def copy(a, k, block, to, src=None):
            dst = outs[a].at[4 * block[0] + 2 * block[1] + block[2]]
            return pltpu.make_async_remote_copy(src_ref=dst if src is None else src, dst_ref=dst, send_sem=send_sems.at[a * PEERS + k],
                                                recv_sem=recv_sems.at[a * PEERS + k], device_id=to, device_id_type=MESH)

        local = [pltpu.make_async_copy(ins[a], outs[a].at[4 * x + 2 * y + c], local_sems.at[a]) for a in range(n)]
        for cp in local:
            cp.start()
        first = []
        for a in range(n):
            first += [copy(a, 1 + j, me, (*chip, c), src=ins[a]) for j, chip in enumerate(chips)]
            first.append(copy(a, 0, me, sibling, src=ins[a]))
        for cp in first:
            cp.start()
        passed = []
        for a in range(n):
            for j, chip in enumerate(chips):
                copy(a, 1 + j, (*chip, c), me).wait_recv()
                fwd = copy(a, 4 + j, (*chip, c), sibling)
                fwd.start()
                passed.append(fwd)
        for a in range(n):
            copy(a, 0, sibling, me).wait_recv()
            for j, chip in enumerate(chips):
                copy(a, 4 + j, (*chip, 1 - c), me).wait_recv()
        for cp in first + passed:
            cp.wait_send()
        for cp in local:
            cp.wait()

    return _pcall(body, name=name, out_shape=tuple(jax.ShapeDtypeStruct((N_DEV,) + a.shape, a.dtype) for a in arrays),
                  in_specs=[ANY] * n, out_specs=(ANY,) * n,
                  scratch_shapes=[pltpu.SemaphoreType.DMA((n * PEERS,)), pltpu.SemaphoreType.DMA((n * PEERS,)),
                                  pltpu.SemaphoreType.DMA((n,))])(*arrays)


CHIPS = 4


def _pair_exchange(arrays, *, name):
    n = len(arrays)

    def body(*refs):
        ins, outs = refs[:n], refs[n:2 * n]
        send_sems, recv_sems = refs[2 * n:]
        x, y, c = lax.axis_index("x"), lax.axis_index("y"), lax.axis_index("c")
        copies = []
        for a in range(n):
            for q in range(CHIPS):
                cp = pltpu.make_async_remote_copy(src_ref=ins[a].at[2 * q + 1 - c], dst_ref=outs[a].at[q],
                                                  send_sem=send_sems.at[a * CHIPS + q], recv_sem=recv_sems.at[a * CHIPS + q],
                                                  device_id=(x, y, 1 - c), device_id_type=MESH)
                cp.start()
                copies.append(cp)
        for cp in copies:
            cp.wait()

    return _pcall(body, name=name, out_shape=tuple(jax.ShapeDtypeStruct((CHIPS,) + a.shape[1:], a.dtype) for a in arrays),
                  in_specs=[ANY] * n, out_specs=(ANY,) * n,
                  scratch_shapes=[pltpu.SemaphoreType.DMA((n * CHIPS,)), pltpu.SemaphoreType.DMA((n * CHIPS,))])(*arrays)


def _pair_add(blocks, theirs, *, name):
    _, r, c_ = blocks.shape
    tr = _tile(r, 512, 16)

    def body(mine_ref, theirs_ref, o_ref):
        core = lax.axis_index("c")
        own = jnp.where(core == 0, mine_ref[0, 0].astype(F32), mine_ref[0, 1].astype(F32))
        o_ref[0] = (own + theirs_ref[0].astype(F32)).astype(o_ref.dtype)

    spec = pl.BlockSpec((1, tr, c_), lambda q, i: (q, i, 0))
    return _pcall(body, name=name, out_shape=jax.ShapeDtypeStruct(theirs.shape, theirs.dtype), grid=(CHIPS, r // tr),
                  in_specs=[pl.BlockSpec((1, 2, tr, c_), lambda q, i: (q, 0, i, 0)), spec], out_specs=spec,
                  semantics=("parallel", "parallel"), vmem_limit=VMEM_LIMIT)(blocks.reshape(CHIPS, 2, r, c_), theirs)


HBM = pl.BlockSpec(memory_space=pltpu.HBM)
SEM = pl.BlockSpec(memory_space=pltpu.SEMAPHORE)
EFFECT = pltpu.SideEffectType.DATAFLOW_SIDE_EFFECTING


GATHER, CHIP_GATHER, CHIP_SCATTER = "gather", "chip_gather", "chip_scatter"
PEERS_OF = {GATHER: N_DEV - 1, CHIP_GATHER: CHIPS - 1, CHIP_SCATTER: CHIPS - 1}


def _direct_copies(srcs, lands, send_sems, recv_sems, local_sems, kind):
    x, y, c = lax.axis_index("x"), lax.axis_index("y"), lax.axis_index("c")
    peers = PEERS_OF[kind]
    mine = 2 * x + y if kind == CHIP_SCATTER else 4 * x + 2 * y + c
    copies = []
    for a, (src, land) in enumerate(zip(srcs, lands)):
        for k in range(1, peers + 1):
            bits = k if kind == GATHER else 2 * k
            px = 1 - x if bits & 4 else x
            py = 1 - y if bits & 2 else y
            pc = 1 - c if bits & 1 else c
            copies.append(pltpu.make_async_remote_copy(
                src_ref=src.at[2 * px + py] if kind == CHIP_SCATTER else src, dst_ref=land.at[mine],
                send_sem=send_sems.at[a * peers + k - 1], recv_sem=recv_sems.at[a * peers + k - 1],
                device_id=(px, py, pc), device_id_type=MESH))
    for a, (src, land) in enumerate(zip(srcs, lands)):
        copies.append(pltpu.make_async_copy(src.at[mine] if kind == CHIP_SCATTER else src, land.at[mine], local_sems.at[a]))
    return copies


def _pair_swap(arrays, *, name):
    n = len(arrays)

    def body(*refs):
        mine, zones = refs[:n], refs[n:2 * n]
        send_sems, recv_sems = refs[2 * n:]
        x, y, c = lax.axis_index("x"), lax.axis_index("y"), lax.axis_index("c")
        copies = []
        for a in range(n):
            for q in range(CHIPS):
                copies.append(pltpu.make_async_remote_copy(
                    src_ref=mine[a].at[2 * q + c], dst_ref=zones[a].at[2 * q + c], send_sem=send_sems.at[a * CHIPS + q],
                    recv_sem=recv_sems.at[a * CHIPS + q], device_id=(x, y, 1 - c), device_id_type=MESH))
        for cp in copies:
            cp.start()
        for cp in copies:
            cp.wait()

    return _pcall(body, name=name, out_shape=tuple(jax.ShapeDtypeStruct(a.shape, a.dtype) for a in arrays),
                  in_specs=[ANY] * n, out_specs=(ANY,) * n, input_output_aliases={i: i for i in range(n)},
                  scratch_shapes=[pltpu.SemaphoreType.DMA((n * CHIPS,)), pltpu.SemaphoreType.DMA((n * CHIPS,))])(*arrays)


def _exchange_start(groups, kind, *, name, after=None):
    srcs = [s for group in groups for s in group]
    n = len(srcs)
    sizes = [len(group) for group in groups]
    starts = [sum(sizes[:g]) for g in range(len(groups))]
    land_shapes = [s.shape if kind == CHIP_SCATTER else (N_DEV,) + s.shape for s in srcs]
    peers = PEERS_OF[kind]
    extra = [] if after is None else [after]

    def body(*refs):
        srcs_, lands = refs[:n], refs[n:2 * n]
        token = refs[-1]
        sem_refs = refs[2 * n + len(extra):]
        for g, (at, size) in enumerate(zip(starts, sizes)):
            send_sems, recv_sems, local_sems = sem_refs[3 * g:3 * g + 3]
            for cp in _direct_copies(srcs_[at:at + size], lands[at:at + size], send_sems, recv_sems, local_sems, kind):
                cp.start()
        token[...] = jnp.zeros_like(token)

    sems = tuple(t for size in sizes for t in (pltpu.SemaphoreType.DMA((size * peers,)), pltpu.SemaphoreType.DMA((size * peers,)),
                                               pltpu.SemaphoreType.DMA((size,))))
    thru = tuple(pltpu.HBM(s.shape, s.dtype) for s in srcs) + tuple(pltpu.HBM(shp, s.dtype) for shp, s in zip(land_shapes, srcs))
    ins = [pltpu.with_memory_space_constraint(s, pltpu.HBM) for s in srcs]
    ins += [pltpu.with_memory_space_constraint(lax.empty(shp, s.dtype), pltpu.HBM) for shp, s in zip(land_shapes, srcs)]
    out = pl.pallas_call(
        body, name=name, out_shape=sems + thru + (jax.ShapeDtypeStruct((SUBLANE, LANE), F32),),
        in_specs=[HBM] * (2 * n) + [ANY] * len(extra),
        out_specs=(SEM,) * len(sems) + (HBM,) * (2 * n) + (pl.BlockSpec(memory_space=pltpu.VMEM),),
        input_output_aliases={i: len(sems) + i for i in range(2 * n)},
        compiler_params=pltpu.CompilerParams(has_side_effects=EFFECT))(*ins, *extra)
    arrays = out[len(sems):-1]
    started = [tuple(out[3 * g:3 * g + 3]) + tuple(arrays[at:at + size]) + tuple(arrays[n + at:n + at + size])
               for g, (at, size) in enumerate(zip(starts, sizes))]
    return started, out[-1]


def _exchange_wait(started, after, kind, *, name):
    n = (len(started) - 3) // 2
    sems, arrays = started[:3], started[3:]

    def body(*refs):
        srcs_, lands = refs[:n], refs[n:2 * n]
        send_sems, recv_sems, local_sems = refs[2 * n:2 * n + 3]
        for cp in _direct_copies(srcs_, lands, send_sems, recv_sems, local_sems, kind):
            cp.wait()

    out = pl.pallas_call(
        body, name=name, out_shape=tuple(pltpu.HBM(a.shape, a.dtype) for a in arrays),
        in_specs=[HBM] * (2 * n) + [SEM] * 3 + [ANY], out_specs=(HBM,) * (2 * n),
        input_output_aliases={i: i for i in range(2 * n)},
        compiler_params=pltpu.CompilerParams(has_side_effects=EFFECT))(*arrays, *sems, after)
    return out[n:]


def _adamw_reduce(w, parts, m, v, *, name, after=None):
    layers, r, c = w.shape
    assert len(parts) == layers
    senders = parts[0].shape[0]
    tr = _tile(r, 512, 16)
    tiles = r // tr
    bc1 = 1.0 - ADAM_B1 ** ADAM_STEP
    bc2 = 1.0 - ADAM_B2 ** ADAM_STEP

    def body(w_ref, *rest):
        p_refs = rest[:layers]
        m_ref, v_ref, g_ref, d_ref, nm_ref, nv_ref = rest[layers:]

        def update(p_ref):
            g = p_ref[0, :, pl.ds(0, c)].astype(F32)
            for s in range(1, senders):
                g = g + p_ref[s, :, pl.ds(0, c)].astype(F32)
            nm = ADAM_B1 * m_ref[0] + (1.0 - ADAM_B1) * g
            nv = ADAM_B2 * v_ref[0] + (1.0 - ADAM_B2) * (g * g)
            g_ref[0] = g
            nm_ref[0] = nm
            nv_ref[0] = nv
            d_ref[0] = -ADAM_LR * ((nm / bc1) / (jnp.sqrt(nv / bc2) + ADAM_EPS) + ADAM_WD * w_ref[0])

        for layer in range(layers):
            pl.when(pl.program_id(0) == layer)(functools.partial(update, p_refs[layer]))

    def part_spec(layer, shape):
        rest = 0 if layer > 0 else tiles - 1
        return pl.BlockSpec((senders, tr, shape[2]), lambda l, i: (0, jnp.where(l == layer, i, rest), 0))

    spec = pl.BlockSpec((1, tr, c), lambda l, i: (l, i, 0))
    out = jax.ShapeDtypeStruct((layers, r, c), F32)
    return _pcall(body, name=name, out_shape=(out,) * 4, grid=(layers, tiles),
                  in_specs=[spec] + [part_spec(layer, p.shape) for layer, p in enumerate(parts)] + [spec, spec],
                  out_specs=(spec,) * 4, semantics=("arbitrary", "arbitrary"), vmem_limit=VMEM_LIMIT, after=after)(w, *parts, m, v)


def _pool_windows():
    return jnp.repeat(jnp.asarray(POOL_WINDOWS, F32), POOL_DIM // len(POOL_WINDOWS))[None, :]


def _block_diag_pairs(pool_w):
    z = jnp.zeros_like(pool_w[0])
    return jnp.stack([jnp.block([[pool_w[2 * b], z], [z, pool_w[2 * b + 1]]]) for b in range(2)])


def _pad_lanes(vec):
    return jnp.zeros((1, LANE), F32).at[0, :vec.shape[0]].set(vec)


FF_SHARD = D_FF // N_DEV
FF_BLOCK = 384
D_FF_PAD = N_DEV * FF_BLOCK


def _layer_fwd(x, p_i, wt, fetch):
    wt = {**wt, **fetch(0, x)}
    h1 = _rmsnorm_fwd(x, wt["norm1_g"], name="rmsnorm_fwd")
    proj = _matmul(h1, wt["w_in"], "nt", name="mm_in")
    wt.update(fetch(1, proj))
    qkv = _qkv_prep_fwd(proj, wt["conv_qkv"], name="qkv_prep_fwd")
    g, beta = _gates_fwd(proj, wt["a_log"], wt["dt_bias"], name="gates_fwd")
    u, w, qg, kg, attn, tmats = _deltanet_prep(qkv, g, beta, name="deltanet_prep")
    o, vn, states = _deltanet_scan(u, w, qg, kg, attn, g, name="deltanet_scan")
    o_a = _apost_fwd(o, proj, wt["onorm_g"], name="apost_fwd")
    o_b = _pool_fwd(proj, wt["pool_win"], wt["pool_wbd"], wt["pool_scale"], name="pool_fwd")
    o_c = _sconv_fwd(proj, wt["sconv_w"], name="sconv_fwd")
    mixed = jnp.concatenate([o_a, o_b, o_c], axis=1)
    x1 = _matmul(mixed, wt["w_out"], "nn", res=x, name="mm_out")
    h2 = _rmsnorm_fwd(x1, wt["norm2_g"], name="rmsnorm_fwd")
    wt.update(fetch(2, h2))
    ff, gate, up = _swiglu_fwd(h2, wt["w_gate"], wt["w_up"], name="swiglu_fwd")
    wt.update(fetch(3, ff))
    x2 = _matmul(ff, wt["w_down"], "nn", res=x1, name="mm_down")
    wt.update(fetch(4, x2))
    pgl = _matmul(x2, wt["ple_gate"], "nn", name="mm_pleg")
    pp = _matmul(p_i, wt["ple_proj"], "nn", b_blocked=True, name="mm_plep")
    x3 = _ple_fwd(x2, pgl, pp, name="ple_fwd")
    saved = dict(x=x, h1=h1, proj=proj, qkv=qkv, g=g, beta=beta, o=o, states=states, tmats=tmats, mixed=mixed, x1=x1, h2=h2,
                 gate=gate, up=up, ff=ff, x2=x2, pgl=pgl, pp=pp, p=p_i, w=w, qg=qg, kg=kg, attn=attn, vn=vn, wt=wt)
    return x3, saved


def _col_blocks(g):
    a = g.shape[0]
    return jnp.transpose(g.reshape(a, N_DEV, -1), (1, 0, 2))


def _cols_joined(blocks):
    return jnp.transpose(blocks, (1, 0, 2)).reshape(blocks.shape[1], -1)


def _layer_bwd(dx3, sv, emit, after=None):
    gr, big = {}, {}
    wt = sv["wt"]
    rows = D_MODEL // N_DEV
    dpgl, dpp = _ple_bwd(dx3, sv["pgl"], sv["pp"], name="ple_bwd", after=after)
    big["ple_proj"] = _matmul(sv["p"], dpp, "tn", out_blocked=(N_DEV, rows), out_dtype=BF16, name="mm_dplep")
    big["ple_gate"] = _matmul(sv["x2"], dpgl, "tn", out_dtype=BF16, name="mm_dpleg").reshape(N_DEV, rows, D_MODEL)
    dx2 = _matmul(dpgl, wt["ple_gate"], "nt", res=dx3, name="mm_dx2")
    big["w_down"] = _matmul(sv["ff"], dx2, "tn", out_dtype=BF16, name="mm_ddown").reshape(N_DEV, FF_BLOCK, D_MODEL)
    dgate, dup = _swiglu_bwd(dx2, wt["w_down"], sv["gate"], sv["up"], name="swiglu_bwd", after=emit(0, big))
    big["w_gate"] = _matmul(dgate, sv["h2"], "tn", out_dtype=BF16, name="mm_dgate").reshape(N_DEV, FF_BLOCK, D_MODEL)
    big["w_up"] = _matmul(dup, sv["h2"], "tn", out_dtype=BF16, name="mm_dup").reshape(N_DEV, FF_BLOCK, D_MODEL)
    dh2 = _matmul(dgate, wt["w_gate"], "nn", name="mm_dh2_gate")
    dh2 = _matmul(dup, wt["w_up"], "nn", res=dh2, name="mm_dh2_up")
    dx1, gr["norm2_g"] = _rmsnorm_bwd(sv["x1"], wt["norm2_g"], dh2, dx2, name="rmsnorm_bwd")
    big["w_out"] = _matmul(sv["mixed"], dx1, "tn", out_dtype=BF16, name="mm_dout").reshape(N_DEV, rows, D_MODEL)
    dmixed = _matmul(dx1, wt["w_out"], "nt", name="mm_dmixed", after=emit(1, big))
    proj = sv["proj"]
    dcb, dcc, dch, dsconv = _sconv_bwd(proj, wt["sconv_w"], dmixed, name="sconv_bwd")
    big["sconv_w"] = _col_blocks(dsconv)
    dhp, dwbd, gr["pool_scale"] = _pool_bwd(proj, wt["pool_win"], wt["pool_wbd"], wt["pool_scale"], dmixed, name="pool_bwd")
    half = LANE // 2
    gr["pool_w"] = jnp.stack([dwbd[0, :half, :half], dwbd[0, half:, half:], dwbd[1, :half, :half], dwbd[1, half:, half:]])
    do, dz, gr["onorm_g"] = _apost_bwd(sv["o"], proj, wt["onorm_g"], dmixed, name="apost_bwd")
    dvn, dstates = _deltanet_bscan(sv["w"], sv["qg"], sv["kg"], sv["attn"], sv["g"], do, name="deltanet_bscan")
    dqkv_h, dg, dbeta = _deltanet_post(sv["qkv"], sv["g"], sv["beta"], sv["tmats"], sv["states"], dstates, do, dvn, sv["vn"],
                                       name="deltanet_post")
    dab, dalog, ddtb = _gates_bwd(proj, wt["a_log"], wt["dt_bias"], dg, dbeta, name="gates_bwd")
    gr["a_log"], gr["dt_bias"] = dalog[0, :HEADS], ddtb[0, :HEADS]
    dqkv, dconv = _qkv_prep_bwd(proj, wt["conv_qkv"], dqkv_h, name="qkv_prep_bwd")
    big["conv_qkv"] = _col_blocks(dconv)
    dproj = jnp.concatenate([dqkv, dz, dab, dhp, dcb, dcc, dch], axis=1)
    dwin = _matmul(dproj, sv["h1"], "tn", out_dtype=BF16, name="mm_din")
    big["w_in"] = jnp.concatenate([dwin[:AB_COL + 2 * HEADS], dwin[AB_COL + LANE:]], axis=0).reshape(N_DEV, -1, D_MODEL)
    dh1 = _matmul(dproj, wt["w_in"], "nn", name="mm_dh1", after=emit(2, big))
    dx, gr["norm1_g"] = _rmsnorm_bwd(sv["x"], wt["norm1_g"], dh1, dx1, name="rmsnorm_bwd")
    return dx, gr


FETCH_GROUPS = (("w_in", "conv_qkv", "sconv_w"), ("w_out",), ("w_gate", "w_up"), ("w_down",), ("ple_gate", "ple_proj"))
EMIT_GROUPS = (("ple_proj", "ple_gate", "w_down"), ("w_gate", "w_up", "w_out"), ("w_in", "conv_qkv", "sconv_w"))


def _small_weights(w, i):
    return dict(
        norm1_g=w["norm1_g"][i][None], norm2_g=w["norm2_g"][i][None], onorm_g=w["onorm_g"][i][None],
        a_log=_pad_lanes(w["a_log"][i]), dt_bias=_pad_lanes(w["dt_bias"][i]),
        pool_scale=w["pool_scale"][i][None], pool_win=_pool_windows(), pool_wbd=_block_diag_pairs(w["pool_w"][i]))


def _as_read(name, gathered):
    if name == "w_in":
        rows = gathered[:, :D_IN // N_DEV].reshape(-1, D_MODEL)
        return jnp.concatenate([rows[:AB_COL + 2 * HEADS], jnp.zeros((LANE - 2 * HEADS, D_MODEL), BF16),
                                rows[AB_COL + 2 * HEADS:]], axis=0)
    if name in ("conv_qkv", "sconv_w"):
        return _cols_joined(gathered)
    if name == "ple_proj":
        return gathered
    return gathered.reshape(-1, D_MODEL)


def _layer_weights(gathered, w, i):
    return {**_small_weights(w, i), **{k: _as_read(k, g) for k, g in gathered.items()}}


def _local_step(x, p, target, layers, final_g):
    saved = []
    h = x
    for i in range(DEPTH):
        replicated = {k: v for k, v in layers[i].items() if k not in SHARDED}
        h, sv = _layer_fwd(h, p[i], replicated, lambda group, after, i=i: {k: layers[i][k] for k in FETCH_GROUPS[group]})
        saved.append(sv)
    dx, dgf, loss = _loss_head(h, final_g, target, name="loss_head")
    big, small = [{} for _ in range(DEPTH)], [None] * DEPTH
    for i in reversed(range(DEPTH)):
        dx, small[i] = _layer_bwd(dx, saved[i], lambda group, blocks, i=i: big[i].update({k: blocks[k] for k in EMIT_GROUPS[group]}))
    return loss, dx, big, small, dgf


SHARDED = ("w_in", "w_gate", "w_up", "w_down", "w_out", "ple_gate", "ple_proj", "conv_qkv", "sconv_w")
SMALL = ("norm1_g", "a_log", "dt_bias", "onorm_g", "pool_w", "pool_scale", "norm2_g", "final_g")
SLAB_COLS = 1024


def _payload(name, shard):
    if name in ("conv_qkv", "sconv_w"):
        return shard
    out = shard.astype(BF16)
    if name in ("w_gate", "w_up", "w_down"):
        out = jnp.pad(out, ((0, FF_BLOCK - FF_SHARD), (0, 0)))
    if name == "w_in":
        out = jnp.pad(out, ((0, -out.shape[0] % (2 * SUBLANE)), (0, 0)))
    return out


TRANSPOSED = ("w_in", "w_gate", "w_up")


def _ff_rows(t):
    return jnp.transpose(t, (0, 2, 1))


def _slab_rows(shape):
    size = 1
    for s in shape:
        size *= s
    return SUBLANE * -(-size // (SUBLANE * SLAB_COLS))


def _pack_slab(parts, extra_row):
    rows = []
    for name in SMALL:
        flat = parts[name].reshape(-1)
        nrow = _slab_rows(parts[name].shape)
        rows.append(jnp.pad(flat, (0, nrow * SLAB_COLS - flat.shape[0])).reshape(nrow, SLAB_COLS))
    rows.append(jnp.pad(extra_row, ((0, SUBLANE - 1), (0, 0))))
    return jnp.concatenate(rows, axis=0)


def _unpack_slab(slab, shapes):
    out, row = {}, 0
    for name in SMALL:
        size = 1
        for s in shapes[name]:
            size *= s
        out[name] = slab[row:row + _slab_rows(shapes[name])].reshape(-1)[:size].reshape(shapes[name])
        row += _slab_rows(shapes[name])
    return out, row


def kernel(x, p, norm1_g, w_in, conv_qkv, a_log, dt_bias, onorm_g, pool_w, pool_scale, sconv_w, w_out, norm2_g, w_gate, w_up, w_down, ple_proj, ple_gate, final_g, loss_target, m_norm1_g, m_w_in, m_conv_qkv, m_a_log, m_dt_bias, m_onorm_g, m_pool_w, m_pool_scale, m_sconv_w, m_w_out, m_norm2_g, m_w_gate, m_w_up, m_w_down, m_ple_proj, m_ple_gate, m_final_g, v_norm1_g, v_w_in, v_conv_qkv, v_a_log, v_dt_bias, v_onorm_g, v_pool_w, v_pool_scale, v_sconv_w, v_w_out, v_norm2_g, v_w_gate, v_w_up, v_w_down, v_ple_proj, v_ple_gate, v_final_g):
    names = ["norm1_g", "w_in", "conv_qkv", "a_log", "dt_bias", "onorm_g", "pool_w", "pool_scale", "sconv_w", "w_out", "norm2_g",
             "w_gate", "w_up", "w_down", "ple_proj", "ple_gate", "final_g"]
    w = dict(zip(names, [norm1_g, w_in, conv_qkv, a_log, dt_bias, onorm_g, pool_w, pool_scale, sconv_w, w_out, norm2_g, w_gate, w_up,
                         w_down, ple_proj, ple_gate, final_g]))
    m = dict(zip(names, [m_norm1_g, m_w_in, m_conv_qkv, m_a_log, m_dt_bias, m_onorm_g, m_pool_w, m_pool_scale, m_sconv_w, m_w_out,
                         m_norm2_g, m_w_gate, m_w_up, m_w_down, m_ple_proj, m_ple_gate, m_final_g]))
    v = dict(zip(names, [v_norm1_g, v_w_in, v_conv_qkv, v_a_log, v_dt_bias, v_onorm_g, v_pool_w, v_pool_scale, v_sconv_w, v_w_out,
                         v_norm2_g, v_w_gate, v_w_up, v_w_down, v_ple_proj, v_ple_gate, v_final_g]))
    w.update({k: _ff_rows(w[k]) for k in TRANSPOSED})

    first, rest = FETCH_GROUPS[0], tuple(k for members in FETCH_GROUPS[1:] for k in members)
    gathered = dict(zip(first, _all_gather([_payload(k, w[k][0]) for k in first], name="all_gather_weights")))
    (flying0,), token = _exchange_start([[_payload(k, w[k][0]) for k in rest]], CHIP_GATHER, name="gather_start_0")
    replicated = [_small_weights(w, i) for i in range(DEPTH)]
    replicated[0]["norm1_g"] = replicated[0]["norm1_g"] + token[0, 0]
    for group in (m, v):
        group.update({k: _ff_rows(group[k] + token[0, 0]) for k in TRANSPOSED})
    flying1 = []

    def fetch(i, group, after):
        if i == 0 and group == 1:
            landed = _exchange_wait(flying0, after, CHIP_GATHER, name="gather_wait_0")
            gathered.update(zip(rest, _pair_swap(landed, name="pair_swap")))
            started, token = _exchange_start([[_payload(k, w[k][1]) for k in SHARDED]], CHIP_GATHER, name="gather_start_1",
                                             after=gathered[rest[0]])
            flying1.extend(started)
            return {**{k: _as_read(k, gathered[k]) for k in FETCH_GROUPS[group]},
                    "conv_qkv": _as_read("conv_qkv", gathered["conv_qkv"]) + token[0, 0]}
        if i == 1 and group == 0:
            landed = _exchange_wait(flying1[0], after, CHIP_GATHER, name="gather_wait_1")
            gathered.update(zip(SHARDED, _pair_swap(landed, name="pair_swap")))
        return {k: _as_read(k, gathered[k]) for k in FETCH_GROUPS[group]}

    def reduce_scatter_start(members, blocks, tag):
        mine = [blocks[k] for k in members]
        theirs = _pair_exchange(mine, name="pair_exchange")
        sums = [_pair_add(a, b, name="pair_add") for a, b in zip(mine, theirs)]
        (started,), token = _exchange_start([sums], CHIP_SCATTER, name="exchange_start_" + tag)
        return started, token

    h, saved0 = _layer_fwd(x[0], p[0, 0], replicated[0], functools.partial(fetch, 0))
    h, saved1 = _layer_fwd(h, p[1, 0], replicated[1], functools.partial(fetch, 1))
    dx, dgf, loss_part = _loss_head(h, final_g[None], loss_target[0], name="loss_head")
    small, big1, flying0 = [None] * DEPTH, {}, []
    dx, small[1] = _layer_bwd(dx, saved1, lambda group, blocks: big1.update({k: blocks[k] for k in EMIT_GROUPS[group]}))
    flying1, token = reduce_scatter_start(SHARDED, big1, "1")

    def emit(group, blocks):
        started, token = reduce_scatter_start(EMIT_GROUPS[group], blocks, f"0_{group}")
        flying0.append(started)
        return token

    dx, small[0] = _layer_bwd(dx, saved0, emit, after=token)
    received = [{}, dict(zip(SHARDED, _exchange_wait(flying1, dx, CHIP_SCATTER, name="exchange_wait_1")))]
    for group, members in enumerate(EMIT_GROUPS):
        received[0].update(zip(members, _exchange_wait(flying0[group], dx, CHIP_SCATTER, name=f"exchange_wait_0_{group}")))

    grads = {k: jnp.stack([small[i][k] for i in range(DEPTH)]) for k in small[0]}
    grads = {k: g[:, 0] if k in ("norm1_g", "norm2_g", "onorm_g", "pool_scale") else g for k, g in grads.items()}
    grads["final_g"] = dgf[0]
    loss_row = jnp.pad(loss_part, ((0, 0), (0, SLAB_COLS - LANE)))
    (small_flying,), token = _exchange_start([[_pack_slab(grads, loss_row)]], GATHER, name="small_gather_start")

    out_g, out_d, out_m, out_v = {}, {}, {}, {}
    for k in SHARDED:
        out_g[k], out_d[k], out_m[k], out_v[k] = _adamw_reduce(w[k], [received[i][k] for i in range(DEPTH)], m[k], v[k],
                                                                name="adamw_" + k, after=token)
    behind_all = jnp.stack([out_v[k][0, 0, 0] for k in SHARDED])
    (small_parts,) = _exchange_wait(small_flying, behind_all, GATHER, name="small_gather_wait")
    zero_row = jnp.zeros((1, SLAB_COLS), F32)
    slabs = _adamw_reduce(_pack_slab(w, zero_row)[None], [small_parts], _pack_slab(m, zero_row)[None],
                          _pack_slab(v, zero_row)[None], name="adamw_small")
    slabs = [s[0] for s in slabs]
    shapes = {k: w[k].shape for k in SMALL}
    for dst, slab in zip((out_g, out_d, out_m, out_v), slabs):
        vals, _ = _unpack_slab(slab, shapes)
        dst.update(vals)
    _, loss_at = _unpack_slab(slabs[0], shapes)
    loss = slabs[0][loss_at, 0]
    for group in (out_g, out_d, out_m, out_v):
        group.update({k: _ff_rows(group[k]) for k in TRANSPOSED})

    return (loss, dx[None], *[out_g[k] for k in names], *[out_d[k] for k in names], *[out_m[k] for k in names],
            *[out_v[k] for k in names])
```

```python
import functools

import jax
import jax.numpy as jnp
from jax import lax
from jax.experimental import pallas as pl
from jax.experimental.pallas import tpu as pltpu

F32 = jnp.float32
BF16 = jnp.bfloat16

D_MODEL = 1024
DEPTH = 2
PLE_DIM = 256
EPS = 1e-6
HEAD_DIM = 128
HEADS = 4
A_DIM = HEADS * HEAD_DIM
QKV_TAPS = 4
CHUNK = 64
POOL_WINDOWS = (2, 4, 8, 16)
POOL_DIM = 256
CONV_DIM = 256
CONV_TAPS = 3
D_FF = 2816
D_IN = 3080
D_IN_PAD = 3200
AB_COL = 2048
N_DEV = 8

ADAM_LR = 0.001
ADAM_B1 = 0.9
ADAM_B2 = 0.999
ADAM_EPS = 1e-08
ADAM_WD = 0.01
ADAM_STEP = 10

LANE = 128
SUBLANE = 8
VMEM_BYTES_V7X = 64 * 1024 * 1024
VMEM_LIMIT = 48 * 1024 * 1024

_HI = lax.Precision.HIGHEST
NN = ((1,), (0,))
NT = ((1,), (1,))
TN = ((0,), (0,))
MESH = pl.DeviceIdType.MESH


def _dot(a, b, dims, hi=False):
    if hi:
        return lax.dot_general(a, b, (dims, ((), ())), precision=_HI, preferred_element_type=F32)
    return lax.dot_general(a.astype(BF16), b.astype(BF16), (dims, ((), ())), preferred_element_type=F32)


def _pcall(body, *, name, out_shape, grid=(), in_specs=None, out_specs=None, scratch_shapes=(), semantics=None,
           vmem_limit=None, after=None, **kw):
    params = {}
    if semantics is not None:
        params["dimension_semantics"] = semantics
    if vmem_limit is not None:
        params["vmem_limit_bytes"] = vmem_limit
    if after is not None:
        n_in, inner = len(in_specs), body
        body = lambda *refs: inner(*refs[:n_in], *refs[n_in + 1:])
        in_specs = list(in_specs) + [pl.BlockSpec(after.shape, lambda *_: (0,) * after.ndim)]
    call = pl.pallas_call(
        body, name=name, out_shape=out_shape, grid=grid, in_specs=in_specs, out_specs=out_specs,
        scratch_shapes=list(scratch_shapes), compiler_params=pltpu.CompilerParams(**params), **kw)
    return call if after is None else (lambda *args: call(*args, after))


def _sigmoid(x):
    return 1.0 / (1.0 + jnp.exp(-x))


def _softplus(x):
    return jnp.maximum(x, 0.0) + jnp.log(1.0 + jnp.exp(-jnp.abs(x)))


def _tile(n, cap, mult):
    if n <= cap:
        return n
    best = None
    for t in range(mult, cap + 1, mult):
        if n % t == 0:
            best = t
    assert best is not None, (n, cap, mult)
    return best


ROWS_PER_STEP = 512
NARROW_RESULT = 1024
COLS_PER_DOT = 640


def _resident(weight):
    return pl.BlockSpec(weight.shape, lambda i: (0,) * weight.ndim, pipeline_mode=pl.Buffered(1))


def _matmul_rows(a, b, mode, *, name, res=None, out_dtype=F32, b_blocked=False, after=None):
    m, k = a.shape
    if b_blocked:
        nb, _, bw = b.shape
        n = nb * bw if mode == "nn" else b.shape[1]
    else:
        n = b.shape[1] if mode == "nn" else b.shape[0]
    tm = _tile(m, ROWS_PER_STEP if n > NARROW_RESULT else 2 * ROWS_PER_STEP, 16)
    cn = bw if (b_blocked and mode == "nn") else _tile(n, COLS_PER_DOT, LANE)
    has_res = res is not None

    def body(*refs):
        a_ref, b_ref = refs[0], refs[1]
        res_ref = refs[2] if has_res else None
        o_ref = refs[2 + has_res]
        if not (b_blocked and mode == "nt"):
            av = a_ref[...].astype(BF16)
        for j in range(n // cn):
            cols = pl.ds(j * cn, cn)
            if mode == "nn":
                part = _dot(av, b_ref[j] if b_blocked else b_ref[:, cols], NN)
            elif not b_blocked:
                part = _dot(av, b_ref[cols, :], NT)
            else:
                part = None
                for s in range(nb):
                    term = _dot(a_ref[:, pl.ds(s * bw, bw)], b_ref[s, cols, :], NT)
                    part = term if part is None else part + term
            if has_res:
                part = part + res_ref[:, cols]
            o_ref[:, cols] = part.astype(o_ref.dtype)

    row = lambda width: pl.BlockSpec((tm, width), lambda i: (i, 0))
    whole = _resident(b)
    ins = [a, b] + ([res] if has_res else [])
    specs = [row(k), whole] + ([row(n)] if has_res else [])
    return _pcall(body, name=name, out_shape=jax.ShapeDtypeStruct((m, n), out_dtype), grid=(m // tm,), in_specs=specs,
                  out_specs=row(n), semantics=("parallel",), vmem_limit=VMEM_LIMIT, after=after)(*ins)


def _matmul(a, b, mode, *, name, res=None, out_dtype=F32, b_blocked=False, out_blocked=None, after=None):
    if mode != "tn":
        return _matmul_rows(a, b, mode, name=name, res=res, out_dtype=out_dtype, b_blocked=b_blocked, after=after)
    assert res is None and not b_blocked and after is None
    (t, m), (t2, n) = a.shape, b.shape
    assert t == t2, (a.shape, b.shape)
    tm = _tile(m, 1024, LANE)
    tn = _tile(n, COLS_PER_DOT, LANE)
    if out_blocked is not None:
        assert out_blocked[0] * out_blocked[1] == n
        tn = out_blocked[1]

    def body(a_ref, b_ref, o_ref):
        part = _dot(a_ref[...], b_ref[...], TN).astype(o_ref.dtype)
        if out_blocked is None:
            o_ref[...] = part
        else:
            o_ref[0] = part

    o_spec = (pl.BlockSpec((tm, tn), lambda i, j: (i, j)) if out_blocked is None
              else pl.BlockSpec((1, tm, tn), lambda i, j: (j, i, 0)))
    o_shape = (m, n) if out_blocked is None else (out_blocked[0], m, out_blocked[1])
    return _pcall(body, name=name, out_shape=jax.ShapeDtypeStruct(o_shape, out_dtype), grid=(m // tm, n // tn),
                  in_specs=[pl.BlockSpec((t, tm), lambda i, j: (0, i)), pl.BlockSpec((t, tn), lambda i, j: (0, j))],
                  out_specs=o_spec, semantics=("parallel", "parallel"), vmem_limit=VMEM_LIMIT)(a, b)


ROW_TILE = 512


def _rows(t, width, idx=0):
    return pl.BlockSpec((ROW_TILE, width), lambda i: (i, idx))


def _vec(width):
    return pl.BlockSpec((1, width), lambda i: (0, 0))


def _rmsnorm_fwd(x, g, *, name):
    t, d = x.shape

    def body(x_ref, g_ref, h_ref):
        xv = x_ref[...]
        r = lax.rsqrt(jnp.mean(xv * xv, axis=-1, keepdims=True) + EPS)
        h_ref[...] = (xv * r * g_ref[...]).astype(BF16)

    return _pcall(body, name=name, out_shape=jax.ShapeDtypeStruct((t, d), BF16), grid=(t // ROW_TILE,),
                  in_specs=[_rows(t, d), _vec(d)], out_specs=_rows(t, d), semantics=("parallel",))(x, g)


def _rmsnorm_bwd(x, g, dh, dres, *, name):
    t, d = x.shape

    def body(x_ref, g_ref, dh_ref, dres_ref, dx_ref, dg_ref):
        xv = x_ref[...]
        r = lax.rsqrt(jnp.mean(xv * xv, axis=-1, keepdims=True) + EPS)
        xhat = xv * r
        dhv = dh_ref[...].astype(F32)
        dhg = dhv * g_ref[...]
        dx_ref[...] = dres_ref[...] + r * (dhg - xhat * jnp.mean(dhg * xhat, axis=-1, keepdims=True))
        part = jnp.sum(dhv * xhat, axis=0, keepdims=True)

        @pl.when(pl.program_id(0) == 0)
        def _():
            dg_ref[...] = part

        @pl.when(pl.program_id(0) > 0)
        def _():
            dg_ref[...] += part

    return _pcall(body, name=name, out_shape=(jax.ShapeDtypeStruct((t, d), F32), jax.ShapeDtypeStruct((1, d), F32)),
                  grid=(t // ROW_TILE,), in_specs=[_rows(t, d), _vec(d), _rows(t, d), _rows(t, d)],
                  out_specs=(_rows(t, d), _vec(d)), semantics=("arbitrary",))(x, g, dh, dres)


def _swiglu_fwd(h, w_gate, w_up, *, name):
    t, k = h.shape
    f = w_gate.shape[0]
    tm = _tile(t, ROWS_PER_STEP, 16)
    cn = _tile(f, COLS_PER_DOT, LANE)

    def body(h_ref, wg_ref, wu_ref, ff_ref, gate_ref, up_ref):
        hv = h_ref[...]
        for j in range(f // cn):
            cols = pl.ds(j * cn, cn)
            gv = _dot(hv, wg_ref[cols, :], NT)
            uv = _dot(hv, wu_ref[cols, :], NT)
            gate_ref[:, cols] = gv.astype(BF16)
            up_ref[:, cols] = uv.astype(BF16)
            ff_ref[:, cols] = (gv * _sigmoid(gv) * uv).astype(BF16)

    row = lambda width: pl.BlockSpec((tm, width), lambda i: (i, 0))
    out = jax.ShapeDtypeStruct((t, f), BF16)
    return _pcall(body, name=name, out_shape=(out,) * 3, grid=(t // tm,), in_specs=[row(k), _resident(w_gate), _resident(w_up)],
                  out_specs=(row(f),) * 3, semantics=("parallel",), vmem_limit=VMEM_LIMIT)(h, w_gate, w_up)


def _swiglu_bwd(dx2, w_down, gate, up, *, name, after=None):
    t, d = dx2.shape
    f = w_down.shape[0]
    tm = _tile(t, ROWS_PER_STEP, 16)
    cn = _tile(f, COLS_PER_DOT, LANE)

    def body(dx_ref, w_ref, gate_ref, up_ref, dgate_ref, dup_ref):
        dxv = dx_ref[...].astype(BF16)
        for j in range(f // cn):
            cols = pl.ds(j * cn, cn)
            dffv = _dot(dxv, w_ref[cols, :], NT)
            gv = gate_ref[:, cols].astype(F32)
            sig = _sigmoid(gv)
            dgate_ref[:, cols] = (dffv * up_ref[:, cols].astype(F32) * sig * (1.0 + gv * (1.0 - sig))).astype(BF16)
            dup_ref[:, cols] = (dffv * gv * sig).astype(BF16)

    row = lambda width: pl.BlockSpec((tm, width), lambda i: (i, 0))
    out = jax.ShapeDtypeStruct((t, f), BF16)
    return _pcall(body, name=name, out_shape=(out, out), grid=(t // tm,), in_specs=[row(d), _resident(w_down), row(f), row(f)],
                  out_specs=(row(f), row(f)), semantics=("parallel",), vmem_limit=VMEM_LIMIT, after=after)(dx2, w_down, gate, up)


def _ple_fwd(x2, pgl, pp, *, name):
    t, d = x2.shape

    def body(x_ref, pgl_ref, pp_ref, o_ref):
        o_ref[...] = x_ref[...] + _sigmoid(pgl_ref[...]) * pp_ref[...]

    return _pcall(body, name=name, out_shape=jax.ShapeDtypeStruct((t, d), F32), grid=(t // ROW_TILE,),
                  in_specs=[_rows(t, d)] * 3, out_specs=_rows(t, d), semantics=("parallel",))(x2, pgl, pp)


def _ple_bwd(dx3, pgl, pp, *, name, after=None):
    t, d = dx3.shape

    def body(dx_ref, pgl_ref, pp_ref, dpgl_ref, dpp_ref):
        dxv = dx_ref[...]
        sig = _sigmoid(pgl_ref[...])
        dpp_ref[...] = (dxv * sig).astype(BF16)
        dpgl_ref[...] = (dxv * pp_ref[...] * sig * (1.0 - sig)).astype(BF16)

    return _pcall(body, name=name, out_shape=(jax.ShapeDtypeStruct((t, d), BF16),) * 2, grid=(t // ROW_TILE,),
                  in_specs=[_rows(t, d)] * 3, out_specs=(_rows(t, d),) * 2, semantics=("parallel",), after=after)(dx3, pgl, pp)


def _loss_head(x3, g, target, *, name):
    t, d = x3.shape

    def body(x_ref, g_ref, t_ref, dx_ref, dg_ref, loss_ref):
        xv = x_ref[...]
        r = lax.rsqrt(jnp.mean(xv * xv, axis=-1, keepdims=True) + EPS)
        xhat = xv * r
        gv = g_ref[...]
        err = xhat * gv - t_ref[...]
        row_loss = jnp.sum(err * err, axis=-1, keepdims=True) * (0.5 / d)
        lpart = jnp.broadcast_to(jnp.sum(row_loss, axis=0, keepdims=True), (1, LANE))
        dy = err * (1.0 / d)
        dyg = dy * gv
        dx_ref[...] = r * (dyg - xhat * jnp.mean(dyg * xhat, axis=-1, keepdims=True))
        gpart = jnp.sum(dy * xhat, axis=0, keepdims=True)

        @pl.when(pl.program_id(0) == 0)
        def _():
            dg_ref[...] = gpart
            loss_ref[...] = lpart

        @pl.when(pl.program_id(0) > 0)
        def _():
            dg_ref[...] += gpart
            loss_ref[...] += lpart

    return _pcall(body, name=name,
                  out_shape=(jax.ShapeDtypeStruct((t, d), F32), jax.ShapeDtypeStruct((1, d), F32), jax.ShapeDtypeStruct((1, LANE), F32)),
                  grid=(t // ROW_TILE,), in_specs=[_rows(t, d), _vec(d), _rows(t, d)],
                  out_specs=(_rows(t, d), _vec(d), _vec(LANE)), semantics=("arbitrary",))(x3, g, target)


def _shift_down(x, d):
    if d == 0:
        return x
    row = lax.broadcasted_iota(jnp.int32, x.shape, 0)
    return jnp.where(row >= d, pltpu.roll(x, d, 0), 0.0)


def _shift_up(x, d):
    if d == 0:
        return x
    t = x.shape[0]
    row = lax.broadcasted_iota(jnp.int32, x.shape, 0)
    return jnp.where(row < t - d, pltpu.roll(x, t - d, 0), 0.0)


def _colsum(x):
    return jnp.sum(x, axis=0, keepdims=True)


def _col(t, idx_fn):
    return pl.BlockSpec((t, LANE), idx_fn)


def _conv_fwd(x, w_ref, taps):
    acc = None
    for j in range(taps):
        term = w_ref[pl.ds(j, 1), :] * _shift_down(x, taps - 1 - j)
        acc = term if acc is None else acc + term
    return acc


def _conv_bwd(x, dy, w_ref, dw_ref, taps):
    dx = None
    for j in range(taps):
        term = w_ref[pl.ds(j, 1), :] * _shift_up(dy, taps - 1 - j)
        dx = term if dx is None else dx + term
        dw_ref[pl.ds(j, 1), :] = _colsum(dy * _shift_down(x, taps - 1 - j))
    return dx


def _qkv_prep_fwd(proj, conv_w, *, name):
    t = proj.shape[0]
    scale = HEAD_DIM ** -0.5

    def body(x_ref, w_ref, o_ref):
        j = pl.program_id(0)
        c = _conv_fwd(x_ref[...], w_ref, QKV_TAPS)
        s = c * _sigmoid(c)
        r = lax.rsqrt(jnp.sum(s * s, axis=-1, keepdims=True) + EPS)
        f = jnp.where(j < 2 * HEADS, r, 1.0) * jnp.where(j < HEADS, scale, 1.0)
        o_ref[0] = s * f

    return _pcall(body, name=name, out_shape=jax.ShapeDtypeStruct((3 * HEADS, t, LANE), F32), grid=(3 * HEADS,),
                  in_specs=[_col(t, lambda j: (0, j)), pl.BlockSpec((QKV_TAPS, LANE), lambda j: (0, j))],
                  out_specs=pl.BlockSpec((1, t, LANE), lambda j: (j, 0, 0)), semantics=("parallel",),
                  vmem_limit=VMEM_LIMIT)(proj, conv_w)


def _qkv_prep_bwd(proj, conv_w, dqkv, *, name):
    t = proj.shape[0]
    scale = HEAD_DIM ** -0.5

    def body(x_ref, w_ref, d_ref, dx_ref, dw_ref):
        j = pl.program_id(0)
        xv = x_ref[...]
        c = _conv_fwd(xv, w_ref, QKV_TAPS)
        sig = _sigmoid(c)
        s = c * sig
        r = lax.rsqrt(jnp.sum(s * s, axis=-1, keepdims=True) + EPS)
        n0 = s * r
        dv = d_ref[0]
        dn0 = dv * jnp.where(j < HEADS, scale, 1.0)
        ds_norm = r * (dn0 - n0 * jnp.sum(dn0 * n0, axis=-1, keepdims=True))
        ds = jnp.where(j < 2 * HEADS, ds_norm, dv)
        dc = ds * sig * (1.0 + c * (1.0 - sig))
        dx_ref[...] = _conv_bwd(xv, dc, w_ref, dw_ref, QKV_TAPS).astype(BF16)

    return _pcall(body, name=name,
                  out_shape=(jax.ShapeDtypeStruct((t, 3 * A_DIM), BF16), jax.ShapeDtypeStruct((QKV_TAPS, 3 * A_DIM), F32)),
                  grid=(3 * HEADS,),
                  in_specs=[_col(t, lambda j: (0, j)), pl.BlockSpec((QKV_TAPS, LANE), lambda j: (0, j)),
                            pl.BlockSpec((1, t, LANE), lambda j: (j, 0, 0))],
                  out_specs=(_col(t, lambda j: (0, j)), pl.BlockSpec((QKV_TAPS, LANE), lambda j: (0, j))),
                  semantics=("parallel",), vmem_limit=VMEM_LIMIT)(proj, conv_w, dqkv)


def _lane_pick(x, lane_idx, lane):
    return jnp.broadcast_to(jnp.sum(jnp.where(lane == lane_idx, x, 0.0), axis=-1, keepdims=True), x.shape)


def _gates_fwd(proj, alog, dtb, *, name):
    t = proj.shape[0]

    def body(x_ref, alog_ref, dtb_ref, g_ref, b_ref):
        xv = x_ref[...]
        lane = lax.broadcasted_iota(jnp.int32, xv.shape, 1)
        gall = -jnp.exp(alog_ref[...]) * _softplus(xv + dtb_ref[...])
        ball = _sigmoid(xv)
        for h in range(HEADS):
            g_ref[h] = _lane_pick(gall, h, lane)
            b_ref[h] = _lane_pick(ball, HEADS + h, lane)

    out = jax.ShapeDtypeStruct((HEADS, t, LANE), F32)
    whole = pl.BlockSpec((HEADS, t, LANE), lambda i: (0, 0, 0))
    return _pcall(body, name=name, out_shape=(out, out), grid=(1,),
                  in_specs=[_col(t, lambda i: (0, AB_COL // LANE)), _vec(LANE), _vec(LANE)], out_specs=(whole, whole),
                  semantics=("arbitrary",), vmem_limit=VMEM_LIMIT)(proj, alog, dtb)


def _gates_bwd(proj, alog, dtb, dg, dbeta, *, name):
    t = proj.shape[0]

    def body(x_ref, alog_ref, dtb_ref, dg_ref, db_ref, dab_ref, dalog_ref, ddtb_ref):
        xv = x_ref[...]
        lane = lax.broadcasted_iota(jnp.int32, xv.shape, 1)
        lane1 = lax.broadcasted_iota(jnp.int32, (1, LANE), 1)
        z = xv + dtb_ref[...]
        nea = -jnp.exp(alog_ref[...])
        da_f = nea * _sigmoid(z)
        g_f = nea * _softplus(z)
        ball = _sigmoid(xv)
        db_f = ball * (1.0 - ball)
        dab = jnp.zeros_like(xv)
        dalog = jnp.zeros((1, LANE), F32)
        for h in range(HEADS):
            dgh = dg_ref[h]
            dab = dab + jnp.where(lane == h, dgh * da_f, 0.0) + jnp.where(lane == HEADS + h, db_ref[h] * db_f, 0.0)
            dalog = dalog + jnp.where(lane1 == h, _colsum(dgh * g_f), 0.0)
        dab_ref[...] = dab.astype(BF16)
        dalog_ref[...] = dalog
        ddtb_ref[...] = jnp.where(lane1 < HEADS, _colsum(dab), 0.0)

    whole = pl.BlockSpec((HEADS, t, LANE), lambda i: (0, 0, 0))
    vec = jax.ShapeDtypeStruct((1, LANE), F32)
    return _pcall(body, name=name, out_shape=(jax.ShapeDtypeStruct((t, LANE), BF16), vec, vec), grid=(1,),
                  in_specs=[_col(t, lambda i: (0, AB_COL // LANE)), _vec(LANE), _vec(LANE), whole, whole],
                  out_specs=(_col(t, lambda i: (0, 0)), _vec(LANE), _vec(LANE)), semantics=("arbitrary",),
                  vmem_limit=VMEM_LIMIT)(proj, alog, dtb, dg, dbeta)


Z_COL = 3 * A_DIM // LANE


def _apost_fwd(o, proj, gn, *, name):
    t = proj.shape[0]

    def body(o_ref, z_ref, gn_ref, y_ref):
        ov = o_ref[0]
        z = z_ref[...]
        r = lax.rsqrt(jnp.mean(ov * ov, axis=-1, keepdims=True) + EPS)
        y_ref[...] = (ov * r * gn_ref[...] * (z * _sigmoid(z))).astype(BF16)

    return _pcall(body, name=name, out_shape=jax.ShapeDtypeStruct((t, A_DIM), BF16), grid=(HEADS,),
                  in_specs=[pl.BlockSpec((1, t, LANE), lambda h: (h, 0, 0)), _col(t, lambda h: (0, Z_COL + h)),
                            pl.BlockSpec((1, LANE), lambda h: (0, 0))],
                  out_specs=_col(t, lambda h: (0, h)), semantics=("parallel",), vmem_limit=VMEM_LIMIT)(o, proj, gn)


def _apost_bwd(o, proj, gn, dmixed, *, name):
    t = proj.shape[0]

    def body(o_ref, z_ref, gn_ref, d_ref, do_ref, dz_ref, dgn_ref):
        ov = o_ref[0]
        z = z_ref[...]
        gnv = gn_ref[...]
        dv = d_ref[...]
        r = lax.rsqrt(jnp.mean(ov * ov, axis=-1, keepdims=True) + EPS)
        ohat = ov * r
        sig = _sigmoid(z)
        dy = dv * (z * sig)
        dz_ref[...] = (dv * ohat * gnv * sig * (1.0 + z * (1.0 - sig))).astype(BF16)
        dyo = dy * gnv
        do_ref[0] = r * (dyo - ohat * jnp.mean(dyo * ohat, axis=-1, keepdims=True))
        part = _colsum(dy * ohat)

        @pl.when(pl.program_id(0) == 0)
        def _():
            dgn_ref[...] = part

        @pl.when(pl.program_id(0) > 0)
        def _():
            dgn_ref[...] += part

    return _pcall(body, name=name,
                  out_shape=(jax.ShapeDtypeStruct((HEADS, t, LANE), F32), jax.ShapeDtypeStruct((t, A_DIM), BF16),
                             jax.ShapeDtypeStruct((1, LANE), F32)),
                  grid=(HEADS,),
                  in_specs=[pl.BlockSpec((1, t, LANE), lambda h: (h, 0, 0)), _col(t, lambda h: (0, Z_COL + h)),
                            pl.BlockSpec((1, LANE), lambda h: (0, 0)), _col(t, lambda h: (0, h))],
                  out_specs=(pl.BlockSpec((1, t, LANE), lambda h: (h, 0, 0)), _col(t, lambda h: (0, h)),
                             pl.BlockSpec((1, LANE), lambda h: (0, 0))),
                  semantics=("arbitrary",), vmem_limit=VMEM_LIMIT)(o, proj, gn, dmixed)


POOL_COL = (AB_COL + LANE) // LANE
CB_COL = POOL_COL + POOL_DIM // LANE
CC_COL = CB_COL + CONV_DIM // LANE
CH_COL = CC_COL + CONV_DIM // LANE
MAX_WIN_LOG2 = 4


def _window_sums(x, shift):
    sums = []
    cur = x
    for k in range(MAX_WIN_LOG2):
        cur = cur + shift(cur, 1 << k)
        sums.append(cur)
    return sums


def _pick_window(sums, win):
    out = sums[-1]
    for k in range(MAX_WIN_LOG2 - 2, -1, -1):
        out = jnp.where(win == float(2 << k), sums[k], out)
    return out


def _pool_counts(shape, win):
    row = lax.broadcasted_iota(jnp.int32, shape, 0).astype(F32)
    return jnp.minimum(row + 1.0, win)


def _pool_fwd(proj, win, wbd, scale, *, name):
    t = proj.shape[0]

    def body(x_ref, win_ref, w_ref, s_ref, y_ref):
        xv = x_ref[...]
        winv = win_ref[...]
        pooled = _pick_window(_window_sums(xv, _shift_down), winv) / _pool_counts(xv.shape, winv) - xv
        y_ref[...] = (_dot(pooled, w_ref[0], NN) * s_ref[...]).astype(BF16)

    nb = POOL_DIM // LANE
    vec = pl.BlockSpec((1, LANE), lambda b: (0, b))
    return _pcall(body, name=name, out_shape=jax.ShapeDtypeStruct((t, POOL_DIM), BF16), grid=(nb,),
                  in_specs=[_col(t, lambda b: (0, POOL_COL + b)), vec, pl.BlockSpec((1, LANE, LANE), lambda b: (b, 0, 0)), vec],
                  out_specs=_col(t, lambda b: (0, b)), semantics=("parallel",), vmem_limit=VMEM_LIMIT)(proj, win, wbd, scale)


def _pool_bwd(proj, win, wbd, scale, dmixed, *, name):
    t = proj.shape[0]

    def body(x_ref, win_ref, w_ref, s_ref, d_ref, dx_ref, dw_ref, ds_ref):
        xv = x_ref[...]
        winv = win_ref[...]
        cnt = _pool_counts(xv.shape, winv)
        pooled = _pick_window(_window_sums(xv, _shift_down), winv) / cnt - xv
        dv = d_ref[...]
        ds_ref[...] = _colsum(dv * _dot(pooled, w_ref[0], NN))
        dy0 = dv * s_ref[...]
        dw_ref[0] = _dot(pooled, dy0, TN)
        dpooled = _dot(dy0, w_ref[0], NT)
        dmean = dpooled / cnt
        dx_ref[...] = (_pick_window(_window_sums(dmean, _shift_up), winv) - dpooled).astype(BF16)

    nb = POOL_DIM // LANE
    vec = pl.BlockSpec((1, LANE), lambda b: (0, b))
    mat = pl.BlockSpec((1, LANE, LANE), lambda b: (b, 0, 0))
    first = A_DIM // LANE
    return _pcall(body, name=name,
                  out_shape=(jax.ShapeDtypeStruct((t, POOL_DIM), BF16), jax.ShapeDtypeStruct((nb, LANE, LANE), F32),
                             jax.ShapeDtypeStruct((1, POOL_DIM), F32)),
                  grid=(nb,),
                  in_specs=[_col(t, lambda b: (0, POOL_COL + b)), vec, mat, vec, _col(t, lambda b: (0, first + b))],
                  out_specs=(_col(t, lambda b: (0, b)), mat, vec), semantics=("parallel",),
                  vmem_limit=VMEM_LIMIT)(proj, win, wbd, scale, dmixed)


def _sconv_fwd(proj, w, *, name):
    t = proj.shape[0]

    def body(cb_ref, cc_ref, ch_ref, w_ref, y_ref):
        y_ref[...] = (cb_ref[...] * _conv_fwd(cc_ref[...] * ch_ref[...], w_ref, CONV_TAPS)).astype(BF16)

    nb = CONV_DIM // LANE
    return _pcall(body, name=name, out_shape=jax.ShapeDtypeStruct((t, CONV_DIM), BF16), grid=(nb,),
                  in_specs=[_col(t, lambda b: (0, CB_COL + b)), _col(t, lambda b: (0, CC_COL + b)),
                            _col(t, lambda b: (0, CH_COL + b)), pl.BlockSpec((CONV_TAPS, LANE), lambda b: (0, b))],
                  out_specs=_col(t, lambda b: (0, b)), semantics=("parallel",), vmem_limit=VMEM_LIMIT)(proj, proj, proj, w)


def _sconv_bwd(proj, w, dmixed, *, name):
    t = proj.shape[0]

    def body(cb_ref, cc_ref, ch_ref, w_ref, d_ref, dcb_ref, dcc_ref, dch_ref, dw_ref):
        cc = cc_ref[...]
        ch = ch_ref[...]
        u = cc * ch
        dv = d_ref[...]
        dcb_ref[...] = (dv * _conv_fwd(u, w_ref, CONV_TAPS)).astype(BF16)
        du = _conv_bwd(u, dv * cb_ref[...], w_ref, dw_ref, CONV_TAPS)
        dcc_ref[...] = (du * ch).astype(BF16)
        dch_ref[...] = (du * cc).astype(BF16)

    nb = CONV_DIM // LANE
    first = (A_DIM + POOL_DIM) // LANE
    act = jax.ShapeDtypeStruct((t, CONV_DIM), BF16)
    wspec = pl.BlockSpec((CONV_TAPS, LANE), lambda b: (0, b))
    ospec = _col(t, lambda b: (0, b))
    return _pcall(body, name=name, out_shape=(act, act, act, jax.ShapeDtypeStruct((CONV_TAPS, CONV_DIM), F32)), grid=(nb,),
                  in_specs=[_col(t, lambda b: (0, CB_COL + b)), _col(t, lambda b: (0, CC_COL + b)),
                            _col(t, lambda b: (0, CH_COL + b)), wspec, _col(t, lambda b: (0, first + b))],
                  out_specs=(ospec, ospec, ospec, wspec), semantics=("parallel",),
                  vmem_limit=VMEM_LIMIT)(proj, proj, proj, w, dmixed)


def _chunk_masks():
    r = lax.broadcasted_iota(jnp.int32, (CHUNK, CHUNK), 0)
    c = lax.broadcasted_iota(jnp.int32, (CHUNK, CHUNK), 1)
    return r >= c, r > c, jnp.where(r == c, 1.0, 0.0).astype(F32)


def _split(a):
    hi = a.astype(BF16)
    return hi, (a - hi.astype(F32)).astype(BF16)


def _dot_split(a, b, dims):
    (ah, al), (bh, bl) = a, b
    return _dot(ah, bh, dims) + _dot(ah, bl, dims) + _dot(al, bh, dims)


def _tri_inv(lows, eye):
    xs = [eye - low for low in lows]
    ps = [_split(low) for low in lows]
    ps = [_split(_dot_split(p, p, NN)) for p in ps]
    for i in range(5):
        xs = [x + _dot_split(_split(x), p, NN) for x, p in zip(xs, ps)]
        if i < 4:
            ps = [_split(_dot_split(p, p, NN)) for p in ps]
    return xs


def _prefix_sum_rows(x):
    for k in range(6):
        x = x + _shift_down(x, 1 << k)
    return x


def _suffix_sum_rows(x):
    for k in range(6):
        x = x + _shift_up(x, 1 << k)
    return x


def _chunk_decay(g, incl):
    gcb = _prefix_sum_rows(g)
    gtot = _colsum(g)
    col = gcb[:, :CHUNK]
    row = gcb.T[:CHUNK, :]
    decay = jnp.exp(jnp.where(incl, col - row, -1e30))
    return gcb, gtot, decay


CHUNKS_PER_STEP = 4


def _heads_of(ref, base, rows):
    return [ref[base + h, rows, :] for h in range(HEADS)]


def _chunk_rows(j):
    return pl.ds(j * CHUNK, CHUNK)


def _deltanet_prep(qkv, g, beta, *, name):
    t = qkv.shape[1]
    n_chunks = t // CHUNK
    per = CHUNKS_PER_STEP
    probs = [(j, h) for j in range(per) for h in range(HEADS)]

    def body(qkv_ref, g_ref, b_ref, u_ref, w_ref, qg_ref, kg_ref, attn_ref, tm_ref):
        incl, strict, eye = _chunk_masks()
        q = [qkv_ref[h, _chunk_rows(j), :] for j, h in probs]
        k = [qkv_ref[HEADS + h, _chunk_rows(j), :] for j, h in probs]
        v = [qkv_ref[2 * HEADS + h, _chunk_rows(j), :] for j, h in probs]
        bv = [b_ref[h, _chunk_rows(j), :] for j, h in probs]
        dec = [_chunk_decay(g_ref[h, _chunk_rows(j), :], incl) for j, h in probs]
        kb = [a * b for a, b in zip(k, bv)]
        low = [jnp.where(strict, _dot(a, b, NT) * d[2], 0.0) for a, b, d in zip(kb, k, dec)]
        tm = _tri_inv(low, eye)
        egc = [jnp.exp(d[0]) for d in dec]
        u = [_dot(m, a * b, NN) for m, a, b in zip(tm, v, bv)]
        w = [_dot(m, a * e, NN) for m, a, e in zip(tm, kb, egc)]
        attn = [_dot(a, b, NT) * d[2] for a, b, d in zip(q, k, dec)]
        for i, (j, h) in enumerate(probs):
            rows = _chunk_rows(j)
            u_ref[h, rows, :] = u[i]
            w_ref[h, rows, :] = w[i].astype(BF16)
            qg_ref[h, rows, :] = (q[i] * egc[i]).astype(BF16)
            kg_ref[h, rows, :] = (k[i] * jnp.exp(dec[i][1] - dec[i][0])).astype(BF16)
            attn_ref[j, h] = attn[i].astype(BF16)
            tm_ref[j, h] = tm[i]

    act = lambda heads: pl.BlockSpec((heads, per * CHUNK, LANE), lambda n: (0, n, 0))
    mat = pl.BlockSpec((per, HEADS, CHUNK, CHUNK), lambda n: (n, 0, 0, 0))
    return _pcall(
        body, name=name,
        out_shape=(jax.ShapeDtypeStruct((HEADS, t, LANE), F32),) + (jax.ShapeDtypeStruct((HEADS, t, LANE), BF16),) * 3
        + (jax.ShapeDtypeStruct((n_chunks, HEADS, CHUNK, CHUNK), BF16), jax.ShapeDtypeStruct((n_chunks, HEADS, CHUNK, CHUNK), F32)),
        grid=(n_chunks // per,), in_specs=[act(3 * HEADS), act(HEADS), act(HEADS)],
        out_specs=(act(HEADS),) * 4 + (mat, mat), semantics=("parallel",), vmem_limit=VMEM_LIMIT)(qkv, g, beta)


SCAN_CHUNKS_PER_STEP = 8


def _deltanet_scan(u, w, qg, kg, attn, g, *, name):
    t = u.shape[1]
    n_chunks = t // CHUNK
    per = SCAN_CHUNKS_PER_STEP

    def body(u_ref, w_ref, qg_ref, kg_ref, attn_ref, g_ref, o_ref, vn_ref, st_ref, s_ref):
        @pl.when(pl.program_id(0) == 0)
        def _():
            s_ref[...] = jnp.zeros_like(s_ref)

        for j in range(per):
            rows = _chunk_rows(j)
            s = [s_ref[h] for h in range(HEADS)]
            vn = [u_ref[h, rows, :] - _dot(w_ref[h, rows, :], s[h], NN) for h in range(HEADS)]
            o = [_dot(qg_ref[h, rows, :], s[h], NN) + _dot(attn_ref[j, h], vn[h], NN) for h in range(HEADS)]
            eg = [jnp.exp(_colsum(g_ref[h, rows, :])) for h in range(HEADS)]
            for h in range(HEADS):
                st_ref[j, h] = s[h]
                s_ref[h] = s[h] * eg[h] + _dot(kg_ref[h, rows, :], vn[h], TN)
                o_ref[h, rows, :] = o[h]
                vn_ref[h, rows, :] = vn[h]

    act = pl.BlockSpec((HEADS, per * CHUNK, LANE), lambda n: (0, n, 0))
    out = jax.ShapeDtypeStruct((HEADS, t, LANE), F32)
    return _pcall(
        body, name=name, out_shape=(out, out, jax.ShapeDtypeStruct((n_chunks, HEADS, LANE, LANE), F32)), grid=(n_chunks // per,),
        in_specs=[act] * 4 + [pl.BlockSpec((per, HEADS, CHUNK, CHUNK), lambda n: (n, 0, 0, 0)), act],
        out_specs=(act, act, pl.BlockSpec((per, HEADS, LANE, LANE), lambda n: (n, 0, 0, 0))),
        scratch_shapes=[pltpu.VMEM((HEADS, LANE, LANE), F32)], semantics=("arbitrary",))(u, w, qg, kg, attn, g)


def _deltanet_bscan(w, qg, kg, attn, g, do, *, name):
    t = w.shape[1]
    n_chunks = t // CHUNK
    per = SCAN_CHUNKS_PER_STEP
    steps = n_chunks // per

    def body(w_ref, qg_ref, kg_ref, attn_ref, g_ref, do_ref, dvn_ref, dsn_ref, ds_ref):
        @pl.when(pl.program_id(0) == 0)
        def _():
            ds_ref[...] = jnp.zeros_like(ds_ref)

        for j in reversed(range(per)):
            rows = _chunk_rows(j)
            dsn = [ds_ref[h] for h in range(HEADS)]
            dov = [do_ref[h, rows, :] for h in range(HEADS)]
            dvn = [_dot(attn_ref[j, h], dov[h], TN) + _dot(kg_ref[h, rows, :], dsn[h], NN) for h in range(HEADS)]
            eg = [jnp.exp(_colsum(g_ref[h, rows, :])) for h in range(HEADS)]
            for h in range(HEADS):
                dsn_ref[j, h] = dsn[h]
                ds_ref[h] = _dot(qg_ref[h, rows, :], dov[h], TN) + eg[h] * dsn[h] - _dot(w_ref[h, rows, :], dvn[h], TN)
                dvn_ref[h, rows, :] = dvn[h]

    act = pl.BlockSpec((HEADS, per * CHUNK, LANE), lambda n: (0, steps - 1 - n, 0))
    return _pcall(
        body, name=name,
        out_shape=(jax.ShapeDtypeStruct((HEADS, t, LANE), F32), jax.ShapeDtypeStruct((n_chunks, HEADS, LANE, LANE), F32)),
        grid=(steps,),
        in_specs=[act] * 3 + [pl.BlockSpec((per, HEADS, CHUNK, CHUNK), lambda n: (steps - 1 - n, 0, 0, 0)), act, act],
        out_specs=(act, pl.BlockSpec((per, HEADS, LANE, LANE), lambda n: (steps - 1 - n, 0, 0, 0))),
        scratch_shapes=[pltpu.VMEM((HEADS, LANE, LANE), F32)], semantics=("arbitrary",))(w, qg, kg, attn, g, do)


def _sum_all(x):
    return jnp.sum(jnp.sum(x, axis=1, keepdims=True), axis=0, keepdims=True)


def _rowsum(x):
    return jnp.sum(x, axis=1, keepdims=True)


def _deltanet_post(qkv, g, beta, tmats, states, dstates, do, dvn, vn, *, name):
    t = qkv.shape[1]
    n_chunks = t // CHUNK
    per = CHUNKS_PER_STEP
    probs = [(j, h) for j in range(per) for h in range(HEADS)]

    def body(qkv_ref, g_ref, b_ref, tm_ref, st_ref, dsn_ref, do_ref, dvn_ref, vn_ref, dqkv_ref, dg_ref, db_ref):
        incl, strict, _ = _chunk_masks()
        ones = jnp.ones((CHUNK, LANE), BF16)
        last_row = lax.broadcasted_iota(jnp.int32, (CHUNK, LANE), 0) == CHUNK - 1
        z = lambda f, *cols: [f(*a) for a in zip(*cols)]
        q = [qkv_ref[h, _chunk_rows(j), :] for j, h in probs]
        k = [qkv_ref[HEADS + h, _chunk_rows(j), :] for j, h in probs]
        v = [qkv_ref[2 * HEADS + h, _chunk_rows(j), :] for j, h in probs]
        bv = [b_ref[h, _chunk_rows(j), :] for j, h in probs]
        dov = [do_ref[h, _chunk_rows(j), :] for j, h in probs]
        dvn_ = [dvn_ref[h, _chunk_rows(j), :] for j, h in probs]
        vn_ = [vn_ref[h, _chunk_rows(j), :] for j, h in probs]
        tm = [tm_ref[j, h] for j, h in probs]
        s = [st_ref[j, h] for j, h in probs]
        dsn = [dsn_ref[j, h] for j, h in probs]
        dec = [_chunk_decay(g_ref[h, _chunk_rows(j), :], incl) for j, h in probs]
        decay = [d[2] for d in dec]
        egc = [jnp.exp(d[0]) for d in dec]
        ekg = [jnp.exp(d[1] - d[0]) for d in dec]
        kb = z(lambda a, b: a * b, k, bv)
        vb = z(lambda a, b: a * b, v, bv)
        kbg = z(lambda a, b: a * b, kb, egc)
        qg = z(lambda a, b: a * b, q, egc)
        kg = z(lambda a, b: a * b, k, ekg)
        kk = z(lambda a, b: _dot(a, b, NT), kb, k)
        qk = z(lambda a, b: _dot(a, b, NT), q, k)
        dattn = z(lambda a, b: jnp.where(incl, _dot(a, b, NT), 0.0), dov, vn_)
        dqg = z(lambda a, b: _dot(a, b, NT), dov, s)
        dkg = z(lambda a, b: _dot(a, b, NT), vn_, dsn)
        dglast = z(lambda a, b, c, d, e: _sum_all(a * b) * jnp.exp(e[1]) + _sum_all(c * d), s, dsn, dkg, kg, dec)
        dw = z(lambda a, b: -_dot(a, b, NT), dvn_, s)
        dtm = z(lambda a, b, c, d: _dot(a, b, NT) + _dot(c, d, NT), dvn_, vb, dw, kbg)
        dvb = z(lambda a, b: _dot(a, b, TN), tm, dvn_)
        dkbg = z(lambda a, b: _dot(a, b, TN), tm, dw)
        dlow = z(lambda a, b: jnp.where(strict, -_dot(_dot(a, b, TN), a, NT), 0.0), tm, dtm)
        dkk = z(lambda a, b: a * b, dlow, decay)
        dqk = z(lambda a, b: a * b, dattn, decay)
        dkb = z(lambda a, b, c, d: _dot(a, b, NN) + c * d, dkk, k, dkbg, egc)
        dk = z(lambda a, b, c, d, e, f, g_, h_: _dot(a, b, TN) + _dot(c, d, TN) + e * f + g_ * h_, dkk, kb, dqk, q, dkg, ekg, dkb, bv)
        dq = z(lambda a, b, c, d: _dot(a, b, NN) + c * d, dqk, k, dqg, egc)
        m = z(lambda a, b, c, d, e: (a * b + c * d) * e, dlow, kk, dattn, qk, decay)
        mcol = [_dot(mh, ones, TN) + _dot(ml, ones, TN) for mh, ml in (_split(a) for a in m)]
        for i, (j, h) in enumerate(probs):
            rows = _chunk_rows(j)
            dqkv_ref[h, rows, :] = dq[i]
            dqkv_ref[HEADS + h, rows, :] = dk[i]
            dqkv_ref[2 * HEADS + h, rows, :] = dvb[i] * bv[i]
            db_ref[h, rows, :] = jnp.broadcast_to(_rowsum(dkb[i] * k[i] + dvb[i] * v[i]), (CHUNK, LANE))
            dgc = (_rowsum(dqg[i] * qg[i] + dkbg[i] * kbg[i] - dkg[i] * kg[i]) + _rowsum(m[i]) - mcol[i]
                   + jnp.where(last_row, dglast[i], 0.0))
            dg_ref[h, rows, :] = _suffix_sum_rows(dgc)

    act = lambda heads: pl.BlockSpec((heads, per * CHUNK, LANE), lambda n: (0, n, 0))
    mat = lambda d: pl.BlockSpec((per, HEADS, d, d), lambda n: (n, 0, 0, 0))
    out = jax.ShapeDtypeStruct((HEADS, t, LANE), F32)
    return _pcall(
        body, name=name, out_shape=(jax.ShapeDtypeStruct((3 * HEADS, t, LANE), F32), out, out), grid=(n_chunks // per,),
        in_specs=[act(3 * HEADS), act(HEADS), act(HEADS), mat(CHUNK), mat(LANE), mat(LANE), act(HEADS), act(HEADS), act(HEADS)],
        out_specs=(act(3 * HEADS), act(HEADS), act(HEADS)), semantics=("parallel",),
        vmem_limit=VMEM_LIMIT)(qkv, g, beta, tmats, states, dstates, do, dvn, vn)


ANY = pl.BlockSpec(memory_space=pl.ANY)
PEERS = N_DEV - 1


def _all_gather(arrays, *, name):
    n = len(arrays)

    def body(*refs):
        ins, outs = refs[:n], refs[n:2 * n]
        send_sems, recv_sems, local_sems = refs[2 * n:]
        x, y, c = lax.axis_index("x"), lax.axis_index("y"), lax.axis_index("c")
        me, sibling = (x, y, c), (x, y, 1 - c)
        chips = [(1 - x, y), (x, 1 - y), (1 - x, 1 - y)]

        def copy(a, k, block, to, src=None):
            dst = outs[a].at[4 * block[0] + 2 * block[1] + block[2]]
            return pltpu.make_async_remote_copy(src_ref=dst if src is None else src, dst_ref=dst, send_sem=send_sems.at[a * PEERS + k],
                                                recv_sem=recv_sems.at[a * PEERS + k], device_id=to, device_id_type=MESH)

        local = [pltpu.make_async_copy(ins[a], outs[a].at[4 * x + 2 * y + c], local_sems.at[a]) for a in range(n)]
        for cp in local:
            cp.start()
        first = []
        for a in range(n):
            first += [copy(a, 1 + j, me, (*chip, c), src=ins[a]) for j, chip in enumerate(chips)]
            first.append(copy(a, 0, me, sibling, src=ins[a]))
        for cp in first:
            cp.start()
        passed = []
        for a in range(n):
            for j, chip in enumerate(chips):
                copy(a, 1 + j, (*chip, c), me).wait_recv()
                fwd = copy(a, 4 + j, (*chip, c), sibling)
                fwd.start()
                passed.append(fwd)
        for a in range(n):
            copy(a, 0, sibling, me).wait_recv()
            for j, chip in enumerate(chips):
                copy(a, 4 + j, (*chip, 1 - c), me).wait_recv()
        for cp in first + passed:
            cp.wait_send()
        for cp in local:
            cp.wait()

    return _pcall(body, name=name, out_shape=tuple(jax.ShapeDtypeStruct((N_DEV,) + a.shape, a.dtype) for a in arrays),
                  in_specs=[ANY] * n, out_specs=(ANY,) * n,
                  scratch_shapes=[pltpu.SemaphoreType.DMA((n * PEERS,)), pltpu.SemaphoreType.DMA((n * PEERS,)),
                                  pltpu.SemaphoreType.DMA((n,))])(*arrays)


CHIPS = 4


def _pair_exchange(arrays, *, name):
    n = len(arrays)

    def body(*refs):
        ins, outs = refs[:n], refs[n:2 * n]
        send_sems, recv_sems = refs[2 * n:]
        x, y, c = lax.axis_index("x"), lax.axis_index("y"), lax.axis_index("c")
        copies = []
        for a in range(n):
            for q in range(CHIPS):
                cp = pltpu.make_async_remote_copy(src_ref=ins[a].at[2 * q + 1 - c], dst_ref=outs[a].at[q],
                                                  send_sem=send_sems.at[a * CHIPS + q], recv_sem=recv_sems.at[a * CHIPS + q],
                                                  device_id=(x, y, 1 - c), device_id_type=MESH)
                cp.start()
                copies.append(cp)
        for cp in copies:
            cp.wait()

    return _pcall(body, name=name, out_shape=tuple(jax.ShapeDtypeStruct((CHIPS,) + a.shape[1:], a.dtype) for a in arrays),
                  in_specs=[ANY] * n, out_specs=(ANY,) * n,
                  scratch_shapes=[pltpu.SemaphoreType.DMA((n * CHIPS,)), pltpu.SemaphoreType.DMA((n * CHIPS,))])(*arrays)


def _pair_add(blocks, theirs, *, name):
    _, r, c_ = blocks.shape
    tr = _tile(r, 512, 16)

    def body(mine_ref, theirs_ref, o_ref):
        core = lax.axis_index("c")
        own = jnp.where(core == 0, mine_ref[0, 0].astype(F32), mine_ref[0, 1].astype(F32))
        o_ref[0] = (own + theirs_ref[0].astype(F32)).astype(o_ref.dtype)

    spec = pl.BlockSpec((1, tr, c_), lambda q, i: (q, i, 0))
    return _pcall(body, name=name, out_shape=jax.ShapeDtypeStruct(theirs.shape, theirs.dtype), grid=(CHIPS, r // tr),
                  in_specs=[pl.BlockSpec((1, 2, tr, c_), lambda q, i: (q, 0, i, 0)), spec], out_specs=spec,
                  semantics=("parallel", "parallel"), vmem_limit=VMEM_LIMIT)(blocks.reshape(CHIPS, 2, r, c_), theirs)


HBM = pl.BlockSpec(memory_space=pltpu.HBM)
SEM = pl.BlockSpec(memory_space=pltpu.SEMAPHORE)
EFFECT = pltpu.SideEffectType.DATAFLOW_SIDE_EFFECTING


GATHER, CHIP_GATHER, CHIP_SCATTER = "gather", "chip_gather", "chip_scatter"
PEERS_OF = {GATHER: N_DEV - 1, CHIP_GATHER: CHIPS - 1, CHIP_SCATTER: CHIPS - 1}


def _direct_copies(srcs, lands, send_sems, recv_sems, local_sems, kind):
    x, y, c = lax.axis_index("x"), lax.axis_index("y"), lax.axis_index("c")
    peers = PEERS_OF[kind]
    mine = 2 * x + y if kind == CHIP_SCATTER else 4 * x + 2 * y + c
    copies = []
    for a, (src, land) in enumerate(zip(srcs, lands)):
        for k in range(1, peers + 1):
            bits = k if kind == GATHER else 2 * k
            px = 1 - x if bits & 4 else x
            py = 1 - y if bits & 2 else y
            pc = 1 - c if bits & 1 else c
            copies.append(pltpu.make_async_remote_copy(
                src_ref=src.at[2 * px + py] if kind == CHIP_SCATTER else src, dst_ref=land.at[mine],
                send_sem=send_sems.at[a * peers + k - 1], recv_sem=recv_sems.at[a * peers + k - 1],
                device_id=(px, py, pc), device_id_type=MESH))
    for a, (src, land) in enumerate(zip(srcs, lands)):
        copies.append(pltpu.make_async_copy(src.at[mine] if kind == CHIP_SCATTER else src, land.at[mine], local_sems.at[a]))
    return copies


def _pair_swap(arrays, *, name):
    n = len(arrays)

    def body(*refs):
        mine, zones = refs[:n], refs[n:2 * n]
        send_sems, recv_sems = refs[2 * n:]
        x, y, c = lax.axis_index("x"), lax.axis_index("y"), lax.axis_index("c")
        copies = []
        for a in range(n):
            for q in range(CHIPS):
                copies.append(pltpu.make_async_remote_copy(
                    src_ref=mine[a].at[2 * q + c], dst_ref=zones[a].at[2 * q + c], send_sem=send_sems.at[a * CHIPS + q],
                    recv_sem=recv_sems.at[a * CHIPS + q], device_id=(x, y, 1 - c), device_id_type=MESH))
        for cp in copies:
            cp.start()
        for cp in copies:
            cp.wait()

    return _pcall(body, name=name, out_shape=tuple(jax.ShapeDtypeStruct(a.shape, a.dtype) for a in arrays),
                  in_specs=[ANY] * n, out_specs=(ANY,) * n, input_output_aliases={i: i for i in range(n)},
                  scratch_shapes=[pltpu.SemaphoreType.DMA((n * CHIPS,)), pltpu.SemaphoreType.DMA((n * CHIPS,))])(*arrays)


def _exchange_start(groups, kind, *, name, after=None):
    srcs = [s for group in groups for s in group]
    n = len(srcs)
    sizes = [len(group) for group in groups]
    starts = [sum(sizes[:g]) for g in range(len(groups))]
    land_shapes = [s.shape if kind == CHIP_SCATTER else (N_DEV,) + s.shape for s in srcs]
    peers = PEERS_OF[kind]
    extra = [] if after is None else [after]

    def body(*refs):
        srcs_, lands = refs[:n], refs[n:2 * n]
        token = refs[-1]
        sem_refs = refs[2 * n + len(extra):]
        for g, (at, size) in enumerate(zip(starts, sizes)):
            send_sems, recv_sems, local_sems = sem_refs[3 * g:3 * g + 3]
            for cp in _direct_copies(srcs_[at:at + size], lands[at:at + size], send_sems, recv_sems, local_sems, kind):
                cp.start()
        token[...] = jnp.zeros_like(token)

    sems = tuple(t for size in sizes for t in (pltpu.SemaphoreType.DMA((size * peers,)), pltpu.SemaphoreType.DMA((size * peers,)),
                                               pltpu.SemaphoreType.DMA((size,))))
    thru = tuple(pltpu.HBM(s.shape, s.dtype) for s in srcs) + tuple(pltpu.HBM(shp, s.dtype) for shp, s in zip(land_shapes, srcs))
    ins = [pltpu.with_memory_space_constraint(s, pltpu.HBM) for s in srcs]
    ins += [pltpu.with_memory_space_constraint(lax.empty(shp, s.dtype), pltpu.HBM) for shp, s in zip(land_shapes, srcs)]
    out = pl.pallas_call(
        body, name=name, out_shape=sems + thru + (jax.ShapeDtypeStruct((SUBLANE, LANE), F32),),
        in_specs=[HBM] * (2 * n) + [ANY] * len(extra),
        out_specs=(SEM,) * len(sems) + (HBM,) * (2 * n) + (pl.BlockSpec(memory_space=pltpu.VMEM),),
        input_output_aliases={i: len(sems) + i for i in range(2 * n)},
        compiler_params=pltpu.CompilerParams(has_side_effects=EFFECT))(*ins, *extra)
    arrays = out[len(sems):-1]
    started = [tuple(out[3 * g:3 * g + 3]) + tuple(arrays[at:at + size]) + tuple(arrays[n + at:n + at + size])
               for g, (at, size) in enumerate(zip(starts, sizes))]
    return started, out[-1]


def _exchange_wait(started, after, kind, *, name):
    n = (len(started) - 3) // 2
    sems, arrays = started[:3], started[3:]

    def body(*refs):
        srcs_, lands = refs[:n], refs[n:2 * n]
        send_sems, recv_sems, local_sems = refs[2 * n:2 * n + 3]
        for cp in _direct_copies(srcs_, lands, send_sems, recv_sems, local_sems, kind):
            cp.wait()

    out = pl.pallas_call(
        body, name=name, out_shape=tuple(pltpu.HBM(a.shape, a.dtype) for a in arrays),
        in_specs=[HBM] * (2 * n) + [SEM] * 3 + [ANY], out_specs=(HBM,) * (2 * n),
        input_output_aliases={i: i for i in range(2 * n)},
        compiler_params=pltpu.CompilerParams(has_side_effects=EFFECT))(*arrays, *sems, after)
    return out[n:]


def _adamw_reduce(w, parts, m, v, *, name, after=None):
    layers, r, c = w.shape
    assert len(parts) == layers
    senders = parts[0].shape[0]
    tr = _tile(r, 512, 16)
    tiles = r // tr
    bc1 = 1.0 - ADAM_B1 ** ADAM_STEP
    bc2 = 1.0 - ADAM_B2 ** ADAM_STEP

    def body(w_ref, *rest):
        p_refs = rest[:layers]
        m_ref, v_ref, g_ref, d_ref, nm_ref, nv_ref = rest[layers:]

        def update(p_ref):
            g = p_ref[0, :, pl.ds(0, c)].astype(F32)
            for s in range(1, senders):
                g = g + p_ref[s, :, pl.ds(0, c)].astype(F32)
            nm = ADAM_B1 * m_ref[0] + (1.0 - ADAM_B1) * g
            nv = ADAM_B2 * v_ref[0] + (1.0 - ADAM_B2) * (g * g)
            g_ref[0] = g
            nm_ref[0] = nm
            nv_ref[0] = nv
            d_ref[0] = -ADAM_LR * ((nm / bc1) / (jnp.sqrt(nv / bc2) + ADAM_EPS) + ADAM_WD * w_ref[0])

        for layer in range(layers):
            pl.when(pl.program_id(0) == layer)(functools.partial(update, p_refs[layer]))

    def part_spec(layer, shape):
        rest = 0 if layer > 0 else tiles - 1
        return pl.BlockSpec((senders, tr, shape[2]), lambda l, i: (0, jnp.where(l == layer, i, rest), 0))

    spec = pl.BlockSpec((1, tr, c), lambda l, i: (l, i, 0))
    out = jax.ShapeDtypeStruct((layers, r, c), F32)
    return _pcall(body, name=name, out_shape=(out,) * 4, grid=(layers, tiles),
                  in_specs=[spec] + [part_spec(layer, p.shape) for layer, p in enumerate(parts)] + [spec, spec],
                  out_specs=(spec,) * 4, semantics=("arbitrary", "arbitrary"), vmem_limit=VMEM_LIMIT, after=after)(w, *parts, m, v)


def _pool_windows():
    return jnp.repeat(jnp.asarray(POOL_WINDOWS, F32), POOL_DIM // len(POOL_WINDOWS))[None, :]


def _block_diag_pairs(pool_w):
    z = jnp.zeros_like(pool_w[0])
    return jnp.stack([jnp.block([[pool_w[2 * b], z], [z, pool_w[2 * b + 1]]]) for b in range(2)])


def _pad_lanes(vec):
    return jnp.zeros((1, LANE), F32).at[0, :vec.shape[0]].set(vec)


FF_SHARD = D_FF // N_DEV
FF_BLOCK = 384
D_FF_PAD = N_DEV * FF_BLOCK


def _layer_fwd(x, p_i, wt, fetch):
    wt = {**wt, **fetch(0, x)}
    h1 = _rmsnorm_fwd(x, wt["norm1_g"], name="rmsnorm_fwd")
    proj = _matmul(h1, wt["w_in"], "nt", name="mm_in")
    wt.update(fetch(1, proj))
    qkv = _qkv_prep_fwd(proj, wt["conv_qkv"], name="qkv_prep_fwd")
    g, beta = _gates_fwd(proj, wt["a_log"], wt["dt_bias"], name="gates_fwd")
    u, w, qg, kg, attn, tmats = _deltanet_prep(qkv, g, beta, name="deltanet_prep")
    o, vn, states = _deltanet_scan(u, w, qg, kg, attn, g, name="deltanet_scan")
    o_a = _apost_fwd(o, proj, wt["onorm_g"], name="apost_fwd")
    o_b = _pool_fwd(proj, wt["pool_win"], wt["pool_wbd"], wt["pool_scale"], name="pool_fwd")
    o_c = _sconv_fwd(proj, wt["sconv_w"], name="sconv_fwd")
    mixed = jnp.concatenate([o_a, o_b, o_c], axis=1)
    x1 = _matmul(mixed, wt["w_out"], "nn", res=x, name="mm_out")
    h2 = _rmsnorm_fwd(x1, wt["norm2_g"], name="rmsnorm_fwd")
    wt.update(fetch(2, h2))
    ff, gate, up = _swiglu_fwd(h2, wt["w_gate"], wt["w_up"], name="swiglu_fwd")
    wt.update(fetch(3, ff))
    x2 = _matmul(ff, wt["w_down"], "nn", res=x1, name="mm_down")
    wt.update(fetch(4, x2))
    pgl = _matmul(x2, wt["ple_gate"], "nn", name="mm_pleg")
    pp = _matmul(p_i, wt["ple_proj"], "nn", b_blocked=True, name="mm_plep")
    x3 = _ple_fwd(x2, pgl, pp, name="ple_fwd")
    saved = dict(x=x, h1=h1, proj=proj, qkv=qkv, g=g, beta=beta, o=o, states=states, tmats=tmats, mixed=mixed, x1=x1, h2=h2,
                 gate=gate, up=up, ff=ff, x2=x2, pgl=pgl, pp=pp, p=p_i, w=w, qg=qg, kg=kg, attn=attn, vn=vn, wt=wt)
    return x3, saved


def _col_blocks(g):
    a = g.shape[0]
    return jnp.transpose(g.reshape(a, N_DEV, -1), (1, 0, 2))


def _cols_joined(blocks):
    return jnp.transpose(blocks, (1, 0, 2)).reshape(blocks.shape[1], -1)


def _layer_bwd(dx3, sv, emit, after=None):
    gr, big = {}, {}
    wt = sv["wt"]
    rows = D_MODEL // N_DEV
    dpgl, dpp = _ple_bwd(dx3, sv["pgl"], sv["pp"], name="ple_bwd", after=after)
    big["ple_proj"] = _matmul(sv["p"], dpp, "tn", out_blocked=(N_DEV, rows), out_dtype=BF16, name="mm_dplep")
    big["ple_gate"] = _matmul(sv["x2"], dpgl, "tn", out_dtype=BF16, name="mm_dpleg").reshape(N_DEV, rows, D_MODEL)
    dx2 = _matmul(dpgl, wt["ple_gate"], "nt", res=dx3, name="mm_dx2")
    big["w_down"] = _matmul(sv["ff"], dx2, "tn", out_dtype=BF16, name="mm_ddown").reshape(N_DEV, FF_BLOCK, D_MODEL)
    dgate, dup = _swiglu_bwd(dx2, wt["w_down"], sv["gate"], sv["up"], name="swiglu_bwd", after=emit(0, big))
    big["w_gate"] = _matmul(dgate, sv["h2"], "tn", out_dtype=BF16, name="mm_dgate").reshape(N_DEV, FF_BLOCK, D_MODEL)
    big["w_up"] = _matmul(dup, sv["h2"], "tn", out_dtype=BF16, name="mm_dup").reshape(N_DEV, FF_BLOCK, D_MODEL)
    dh2 = _matmul(dgate, wt["w_gate"], "nn", name="mm_dh2_gate")
    dh2 = _matmul(dup, wt["w_up"], "nn", res=dh2, name="mm_dh2_up")
    dx1, gr["norm2_g"] = _rmsnorm_bwd(sv["x1"], wt["norm2_g"], dh2, dx2, name="rmsnorm_bwd")
    big["w_out"] = _matmul(sv["mixed"], dx1, "tn", out_dtype=BF16, name="mm_dout").reshape(N_DEV, rows, D_MODEL)
    dmixed = _matmul(dx1, wt["w_out"], "nt", name="mm_dmixed", after=emit(1, big))
    proj = sv["proj"]
    dcb, dcc, dch, dsconv = _sconv_bwd(proj, wt["sconv_w"], dmixed, name="sconv_bwd")
    big["sconv_w"] = _col_blocks(dsconv)
    dhp, dwbd, gr["pool_scale"] = _pool_bwd(proj, wt["pool_win"], wt["pool_wbd"], wt["pool_scale"], dmixed, name="pool_bwd")
    half = LANE // 2
    gr["pool_w"] = jnp.stack([dwbd[0, :half, :half], dwbd[0, half:, half:], dwbd[1, :half, :half], dwbd[1, half:, half:]])
    do, dz, gr["onorm_g"] = _apost_bwd(sv["o"], proj, wt["onorm_g"], dmixed, name="apost_bwd")
    dvn, dstates = _deltanet_bscan(sv["w"], sv["qg"], sv["kg"], sv["attn"], sv["g"], do, name="deltanet_bscan")
    dqkv_h, dg, dbeta = _deltanet_post(sv["qkv"], sv["g"], sv["beta"], sv["tmats"], sv["states"], dstates, do, dvn, sv["vn"],
                                       name="deltanet_post")
    dab, dalog, ddtb = _gates_bwd(proj, wt["a_log"], wt["dt_bias"], dg, dbeta, name="gates_bwd")
    gr["a_log"], gr["dt_bias"] = dalog[0, :HEADS], ddtb[0, :HEADS]
    dqkv, dconv = _qkv_prep_bwd(proj, wt["conv_qkv"], dqkv_h, name="qkv_prep_bwd")
    big["conv_qkv"] = _col_blocks(dconv)
    dproj = jnp.concatenate([dqkv, dz, dab, dhp, dcb, dcc, dch], axis=1)
    dwin = _matmul(dproj, sv["h1"], "tn", out_dtype=BF16, name="mm_din")
    big["w_in"] = jnp.concatenate([dwin[:AB_COL + 2 * HEADS], dwin[AB_COL + LANE:]], axis=0).reshape(N_DEV, -1, D_MODEL)
    dh1 = _matmul(dproj, wt["w_in"], "nn", name="mm_dh1", after=emit(2, big))
    dx, gr["norm1_g"] = _rmsnorm_bwd(sv["x"], wt["norm1_g"], dh1, dx1, name="rmsnorm_bwd")
    return dx, gr


FETCH_GROUPS = (("w_in", "conv_qkv", "sconv_w"), ("w_out",), ("w_gate", "w_up"), ("w_down",), ("ple_gate", "ple_proj"))
EMIT_GROUPS = (("ple_proj", "ple_gate", "w_down"), ("w_gate", "w_up", "w_out"), ("w_in", "conv_qkv", "sconv_w"))


def _small_weights(w, i):
    return dict(
        norm1_g=w["norm1_g"][i][None], norm2_g=w["norm2_g"][i][None], onorm_g=w["onorm_g"][i][None],
        a_log=_pad_lanes(w["a_log"][i]), dt_bias=_pad_lanes(w["dt_bias"][i]),
        pool_scale=w["pool_scale"][i][None], pool_win=_pool_windows(), pool_wbd=_block_diag_pairs(w["pool_w"][i]))


def _as_read(name, gathered):
    if name == "w_in":
        rows = gathered[:, :D_IN // N_DEV].reshape(-1, D_MODEL)
        return jnp.concatenate([rows[:AB_COL + 2 * HEADS], jnp.zeros((LANE - 2 * HEADS, D_MODEL), BF16),
                                rows[AB_COL + 2 * HEADS:]], axis=0)
    if name in ("conv_qkv", "sconv_w"):
        return _cols_joined(gathered)
    if name == "ple_proj":
        return gathered
    return gathered.reshape(-1, D_MODEL)


def _layer_weights(gathered, w, i):
    return {**_small_weights(w, i), **{k: _as_read(k, g) for k, g in gathered.items()}}


def _local_step(x, p, target, layers, final_g):
    saved = []
    h = x
    for i in range(DEPTH):
        replicated = {k: v for k, v in layers[i].items() if k not in SHARDED}
        h, sv = _layer_fwd(h, p[i], replicated, lambda group, after, i=i: {k: layers[i][k] for k in FETCH_GROUPS[group]})
        saved.append(sv)
    dx, dgf, loss = _loss_head(h, final_g, target, name="loss_head")
    big, small = [{} for _ in range(DEPTH)], [None] * DEPTH
    for i in reversed(range(DEPTH)):
        dx, small[i] = _layer_bwd(dx, saved[i], lambda group, blocks, i=i: big[i].update({k: blocks[k] for k in EMIT_GROUPS[group]}))
    return loss, dx, big, small, dgf


SHARDED = ("w_in", "w_gate", "w_up", "w_down", "w_out", "ple_gate", "ple_proj", "conv_qkv", "sconv_w")
SMALL = ("norm1_g", "a_log", "dt_bias", "onorm_g", "pool_w", "pool_scale", "norm2_g", "final_g")
SLAB_COLS = 1024


def _payload(name, shard):
    if name in ("conv_qkv", "sconv_w"):
        return shard
    out = shard.astype(BF16)
    if name in ("w_gate", "w_up", "w_down"):
        out = jnp.pad(out, ((0, FF_BLOCK - FF_SHARD), (0, 0)))
    if name == "w_in":
        out = jnp.pad(out, ((0, -out.shape[0] % (2 * SUBLANE)), (0, 0)))
    return out


TRANSPOSED = ("w_in", "w_gate", "w_up")


def _ff_rows(t):
    return jnp.transpose(t, (0, 2, 1))


def _slab_rows(shape):
    size = 1
    for s in shape:
        size *= s
    return SUBLANE * -(-size // (SUBLANE * SLAB_COLS))


def _pack_slab(parts, extra_row):
    rows = []
    for name in SMALL:
        flat = parts[name].reshape(-1)
        nrow = _slab_rows(parts[name].shape)
        rows.append(jnp.pad(flat, (0, nrow * SLAB_COLS - flat.shape[0])).reshape(nrow, SLAB_COLS))
    rows.append(jnp.pad(extra_row, ((0, SUBLANE - 1), (0, 0))))
    return jnp.concatenate(rows, axis=0)


def _unpack_slab(slab, shapes):
    out, row = {}, 0
    for name in SMALL:
        size = 1
        for s in shapes[name]:
            size *= s
        out[name] = slab[row:row + _slab_rows(shapes[name])].reshape(-1)[:size].reshape(shapes[name])
        row += _slab_rows(shapes[name])
    return out, row


def kernel(x, p, norm1_g, w_in, conv_qkv, a_log, dt_bias, onorm_g, pool_w, pool_scale, sconv_w, w_out, norm2_g, w_gate, w_up, w_down, ple_proj, ple_gate, final_g, loss_target, m_norm1_g, m_w_in, m_conv_qkv, m_a_log, m_dt_bias, m_onorm_g, m_pool_w, m_pool_scale, m_sconv_w, m_w_out, m_norm2_g, m_w_gate, m_w_up, m_w_down, m_ple_proj, m_ple_gate, m_final_g, v_norm1_g, v_w_in, v_conv_qkv, v_a_log, v_dt_bias, v_onorm_g, v_pool_w, v_pool_scale, v_sconv_w, v_w_out, v_norm2_g, v_w_gate, v_w_up, v_w_down, v_ple_proj, v_ple_gate, v_final_g):
    names = ["norm1_g", "w_in", "conv_qkv", "a_log", "dt_bias", "onorm_g", "pool_w", "pool_scale", "sconv_w", "w_out", "norm2_g",
             "w_gate", "w_up", "w_down", "ple_proj", "ple_gate", "final_g"]
    w = dict(zip(names, [norm1_g, w_in, conv_qkv, a_log, dt_bias, onorm_g, pool_w, pool_scale, sconv_w, w_out, norm2_g, w_gate, w_up,
                         w_down, ple_proj, ple_gate, final_g]))
    m = dict(zip(names, [m_norm1_g, m_w_in, m_conv_qkv, m_a_log, m_dt_bias, m_onorm_g, m_pool_w, m_pool_scale, m_sconv_w, m_w_out,
                         m_norm2_g, m_w_gate, m_w_up, m_w_down, m_ple_proj, m_ple_gate, m_final_g]))
    v = dict(zip(names, [v_norm1_g, v_w_in, v_conv_qkv, v_a_log, v_dt_bias, v_onorm_g, v_pool_w, v_pool_scale, v_sconv_w, v_w_out,
                         v_norm2_g, v_w_gate, v_w_up, v_w_down, v_ple_proj, v_ple_gate, v_final_g]))
    w.update({k: _ff_rows(w[k]) for k in TRANSPOSED})

    first, rest = FETCH_GROUPS[0], tuple(k for members in FETCH_GROUPS[1:] for k in members)
    gathered = dict(zip(first, _all_gather([_payload(k, w[k][0]) for k in first], name="all_gather_weights")))
    (flying0,), token = _exchange_start([[_payload(k, w[k][0]) for k in rest]], CHIP_GATHER, name="gather_start_0",
                                        after=gathered[first[0]])
    replicated = [_small_weights(w, i) for i in range(DEPTH)]
    replicated[0]["norm1_g"] = replicated[0]["norm1_g"] + token[0, 0]
    for group in (m, v):
        group.update({k: _ff_rows(group[k] + token[0, 0]) for k in TRANSPOSED})
    flying1 = []

    def fetch(i, group, after):
        if i == 0 and group == 1:
            landed = _exchange_wait(flying0, after, CHIP_GATHER, name="gather_wait_0")
            gathered.update(zip(rest, _pair_swap(landed, name="pair_swap")))
            started, token = _exchange_start([[_payload(k, w[k][1]) for k in SHARDED]], CHIP_GATHER, name="gather_start_1",
                                             after=gathered[rest[0]])
            flying1.extend(started)
            return {**{k: _as_read(k, gathered[k]) for k in FETCH_GROUPS[group]},
                    "conv_qkv": _as_read("conv_qkv", gathered["conv_qkv"]) + token[0, 0]}
        if i == 1 and group == 0:
            landed = _exchange_wait(flying1[0], after, CHIP_GATHER, name="gather_wait_1")
            gathered.update(zip(SHARDED, _pair_swap(landed, name="pair_swap")))
        return {k: _as_read(k, gathered[k]) for k in FETCH_GROUPS[group]}

    def reduce_scatter_start(members, blocks, tag):
        mine = [blocks[k] for k in members]
        theirs = _pair_exchange(mine, name="pair_exchange")
        sums = [_pair_add(a, b, name="pair_add") for a, b in zip(mine, theirs)]
        (started,), token = _exchange_start([sums], CHIP_SCATTER, name="exchange_start_" + tag)
        return started, token

    h, saved0 = _layer_fwd(x[0], p[0, 0], replicated[0], functools.partial(fetch, 0))
    h, saved1 = _layer_fwd(h, p[1, 0], replicated[1], functools.partial(fetch, 1))
    dx, dgf, loss_part = _loss_head(h, final_g[None], loss_target[0], name="loss_head")
    small, big1, flying0 = [None] * DEPTH, {}, []
    dx, small[1] = _layer_bwd(dx, saved1, lambda group, blocks: big1.update({k: blocks[k] for k in EMIT_GROUPS[group]}))
    flying1, token = reduce_scatter_start(SHARDED, big1, "1")

    def emit(group, blocks):
        started, token = reduce_scatter_start(EMIT_GROUPS[group], blocks, f"0_{group}")
        flying0.append(started)
        return token

    dx, small[0] = _layer_bwd(dx, saved0, emit, after=token)
    received = [{}, dict(zip(SHARDED, _exchange_wait(flying1, dx, CHIP_SCATTER, name="exchange_wait_1")))]
    for group, members in enumerate(EMIT_GROUPS):
        received[0].update(zip(members, _exchange_wait(flying0[group], dx, CHIP_SCATTER, name=f"exchange_wait_0_{group}")))

    grads = {k: jnp.stack([small[i][k] for i in range(DEPTH)]) for k in small[0]}
    grads = {k: g[:, 0] if k in ("norm1_g", "norm2_g", "onorm_g", "pool_scale") else g for k, g in grads.items()}
    grads["final_g"] = dgf[0]
    loss_row = jnp.pad(loss_part, ((0, 0), (0, SLAB_COLS - LANE)))
    (small_flying,), token = _exchange_start([[_pack_slab(grads, loss_row)]], GATHER, name="small_gather_start")

    out_g, out_d, out_m, out_v = {}, {}, {}, {}
    for k in SHARDED:
        out_g[k], out_d[k], out_m[k], out_v[k] = _adamw_reduce(w[k], [received[i][k] for i in range(DEPTH)], m[k], v[k],
                                                                name="adamw_" + k, after=token)
    behind_all = jnp.stack([out_v[k][0, 0, 0] for k in SHARDED])
    (small_parts,) = _exchange_wait(small_flying, behind_all, GATHER, name="small_gather_wait")
    zero_row = jnp.zeros((1, SLAB_COLS), F32)
    slabs = _adamw_reduce(_pack_slab(w, zero_row)[None], [small_parts], _pack_slab(m, zero_row)[None],
                          _pack_slab(v, zero_row)[None], name="adamw_small")
    slabs = [s[0] for s in slabs]
    shapes = {k: w[k].shape for k in SMALL}
    for dst, slab in zip((out_g, out_d, out_m, out_v), slabs):
        vals, _ = _unpack_slab(slab, shapes)
        dst.update(vals)
    _, loss_at = _unpack_slab(slabs[0], shapes)
    loss = slabs[0][loss_at, 0]
    for group in (out_g, out_d, out_m, out_v):
        group.update({k: _ff_rows(group[k]) for k in TRANSPOSED})

    return (loss, dx[None], *[out_g[k] for k in names], *[out_d[k] for k in names], *[out_m[k] for k in names],
            *[out_v[k] for k in names])
```

```python
import functools

import jax
import jax.numpy as jnp
from jax import lax
from jax.experimental import pallas as pl
from jax.experimental.pallas import tpu as pltpu

F32 = jnp.float32
BF16 = jnp.bfloat16

D_MODEL = 1024
DEPTH = 2
PLE_DIM = 256
EPS = 1e-6
HEAD_DIM = 128
HEADS = 4
A_DIM = HEADS * HEAD_DIM
QKV_TAPS = 4
CHUNK = 64
POOL_WINDOWS = (2, 4, 8, 16)
POOL_DIM = 256
CONV_DIM = 256
CONV_TAPS = 3
D_FF = 2816
D_IN = 3080
D_IN_PAD = 3200
AB_COL = 2048
N_DEV = 8

ADAM_LR = 0.001
ADAM_B1 = 0.9
ADAM_B2 = 0.999
ADAM_EPS = 1e-08
ADAM_WD = 0.01
ADAM_STEP = 10

LANE = 128
SUBLANE = 8
VMEM_BYTES_V7X = 64 * 1024 * 1024
VMEM_LIMIT = 48 * 1024 * 1024

_HI = lax.Precision.HIGHEST
NN = ((1,), (0,))
NT = ((1,), (1,))
TN = ((0,), (0,))
MESH = pl.DeviceIdType.MESH


def _dot(a, b, dims, hi=False):
    if hi:
        return lax.dot_general(a, b, (dims, ((), ())), precision=_HI, preferred_element_type=F32)
    return lax.dot_general(a.astype(BF16), b.astype(BF16), (dims, ((), ())), preferred_element_type=F32)


def _pcall(body, *, name, out_shape, grid=(), in_specs=None, out_specs=None, scratch_shapes=(), semantics=None,
           vmem_limit=None, after=None, **kw):
    params = {}
    if semantics is not None:
        params["dimension_semantics"] = semantics
    if vmem_limit is not None:
        params["vmem_limit_bytes"] = vmem_limit
    if after is not None:
        n_in, inner = len(in_specs), body
        body = lambda *refs: inner(*refs[:n_in], *refs[n_in + 1:])
        in_specs = list(in_specs) + [pl.BlockSpec(after.shape, lambda *_: (0,) * after.ndim)]
    call = pl.pallas_call(
        body, name=name, out_shape=out_shape, grid=grid, in_specs=in_specs, out_specs=out_specs,
        scratch_shapes=list(scratch_shapes), compiler_params=pltpu.CompilerParams(**params), **kw)
    return call if after is None else (lambda *args: call(*args, after))


def _sigmoid(x):
    return 1.0 / (1.0 + jnp.exp(-x))


def _softplus(x):
    return jnp.maximum(x, 0.0) + jnp.log(1.0 + jnp.exp(-jnp.abs(x)))


def _tile(n, cap, mult):
    if n <= cap:
        return n
    best = None
    for t in range(mult, cap + 1, mult):
        if n % t == 0:
            best = t
    assert best is not None, (n, cap, mult)
    return best


ROWS_PER_STEP = 512
NARROW_RESULT = 1024
COLS_PER_DOT = 640


def _resident(weight):
    return pl.BlockSpec(weight.shape, lambda i: (0,) * weight.ndim, pipeline_mode=pl.Buffered(1))


def _matmul_rows(a, b, mode, *, name, res=None, out_dtype=F32, b_blocked=False, after=None):
    m, k = a.shape
    if b_blocked:
        nb, _, bw = b.shape
        n = nb * bw if mode == "nn" else b.shape[1]
    else:
        n = b.shape[1] if mode == "nn" else b.shape[0]
    tm = _tile(m, ROWS_PER_STEP if n > NARROW_RESULT else 2 * ROWS_PER_STEP, 16)
    cn = bw if (b_blocked and mode == "nn") else _tile(n, COLS_PER_DOT, LANE)
    has_res = res is not None

    def body(*refs):
        a_ref, b_ref = refs[0], refs[1]
        res_ref = refs[2] if has_res else None
        o_ref = refs[2 + has_res]
        if not (b_blocked and mode == "nt"):
            av = a_ref[...].astype(BF16)
        for j in range(n // cn):
            cols = pl.ds(j * cn, cn)
            if mode == "nn":
                part = _dot(av, b_ref[j] if b_blocked else b_ref[:, cols], NN)
            elif not b_blocked:
                part = _dot(av, b_ref[cols, :], NT)
            else:
                part = None
                for s in range(nb):
                    term = _dot(a_ref[:, pl.ds(s * bw, bw)], b_ref[s, cols, :], NT)
                    part = term if part is None else part + term
            if has_res:
                part = part + res_ref[:, cols]
            o_ref[:, cols] = part.astype(o_ref.dtype)

    row = lambda width: pl.BlockSpec((tm, width), lambda i: (i, 0))
    whole = _resident(b)
    ins = [a, b] + ([res] if has_res else [])
    specs = [row(k), whole] + ([row(n)] if has_res else [])
    return _pcall(body, name=name, out_shape=jax.ShapeDtypeStruct((m, n), out_dtype), grid=(m // tm,), in_specs=specs,
                  out_specs=row(n), semantics=("parallel",), vmem_limit=VMEM_LIMIT, after=after)(*ins)


def _matmul(a, b, mode, *, name, res=None, out_dtype=F32, b_blocked=False, out_blocked=None, after=None):
    if mode != "tn":
        return _matmul_rows(a, b, mode, name=name, res=res, out_dtype=out_dtype, b_blocked=b_blocked, after=after)
    assert res is None and not b_blocked and after is None
    (t, m), (t2, n) = a.shape, b.shape
    assert t == t2, (a.shape, b.shape)
    tm = _tile(m, 1024, LANE)
    tn = _tile(n, COLS_PER_DOT, LANE)
    if out_blocked is not None:
        assert out_blocked[0] * out_blocked[1] == n
        tn = out_blocked[1]

    def body(a_ref, b_ref, o_ref):
        part = _dot(a_ref[...], b_ref[...], TN).astype(o_ref.dtype)
        if out_blocked is None:
            o_ref[...] = part
        else:
            o_ref[0] = part

    o_spec = (pl.BlockSpec((tm, tn), lambda i, j: (i, j)) if out_blocked is None
              else pl.BlockSpec((1, tm, tn), lambda i, j: (j, i, 0)))
    o_shape = (m, n) if out_blocked is None else (out_blocked[0], m, out_blocked[1])
    return _pcall(body, name=name, out_shape=jax.ShapeDtypeStruct(o_shape, out_dtype), grid=(m // tm, n // tn),
                  in_specs=[pl.BlockSpec((t, tm), lambda i, j: (0, i)), pl.BlockSpec((t, tn), lambda i, j: (0, j))],
                  out_specs=o_spec, semantics=("parallel", "parallel"), vmem_limit=VMEM_LIMIT)(a, b)


ROW_TILE = 512


def _rows(t, width, idx=0):
    return pl.BlockSpec((ROW_TILE, width), lambda i: (i, idx))


def _vec(width):
    return pl.BlockSpec((1, width), lambda i: (0, 0))


def _rmsnorm_fwd(x, g, *, name):
    t, d = x.shape

    def body(x_ref, g_ref, h_ref):
        xv = x_ref[...]
        r = lax.rsqrt(jnp.mean(xv * xv, axis=-1, keepdims=True) + EPS)
        h_ref[...] = (xv * r * g_ref[...]).astype(BF16)

    return _pcall(body, name=name, out_shape=jax.ShapeDtypeStruct((t, d), BF16), grid=(t // ROW_TILE,),
                  in_specs=[_rows(t, d), _vec(d)], out_specs=_rows(t, d), semantics=("parallel",))(x, g)


def _rmsnorm_bwd(x, g, dh, dres, *, name):
    t, d = x.shape

    def body(x_ref, g_ref, dh_ref, dres_ref, dx_ref, dg_ref):
        xv = x_ref[...]
        r = lax.rsqrt(jnp.mean(xv * xv, axis=-1, keepdims=True) + EPS)
        xhat = xv * r
        dhv = dh_ref[...].astype(F32)
        dhg = dhv * g_ref[...]
        dx_ref[...] = dres_ref[...] + r * (dhg - xhat * jnp.mean(dhg * xhat, axis=-1, keepdims=True))
        part = jnp.sum(dhv * xhat, axis=0, keepdims=True)

        @pl.when(pl.program_id(0) == 0)
        def _():
            dg_ref[...] = part

        @pl.when(pl.program_id(0) > 0)
        def _():
            dg_ref[...] += part

    return _pcall(body, name=name, out_shape=(jax.ShapeDtypeStruct((t, d), F32), jax.ShapeDtypeStruct((1, d), F32)),
                  grid=(t // ROW_TILE,), in_specs=[_rows(t, d), _vec(d), _rows(t, d), _rows(t, d)],
                  out_specs=(_rows(t, d), _vec(d)), semantics=("arbitrary",))(x, g, dh, dres)


def _swiglu_fwd(h, w_gate, w_up, *, name):
    t, k = h.shape
    f = w_gate.shape[0]
    tm = _tile(t, ROWS_PER_STEP, 16)
    cn = _tile(f, COLS_PER_DOT, LANE)

    def body(h_ref, wg_ref, wu_ref, ff_ref, gate_ref, up_ref):
        hv = h_ref[...]
        for j in range(f // cn):
            cols = pl.ds(j * cn, cn)
            gv = _dot(hv, wg_ref[cols, :], NT)
            uv = _dot(hv, wu_ref[cols, :], NT)
            gate_ref[:, cols] = gv.astype(BF16)
            up_ref[:, cols] = uv.astype(BF16)
            ff_ref[:, cols] = (gv * _sigmoid(gv) * uv).astype(BF16)

    row = lambda width: pl.BlockSpec((tm, width), lambda i: (i, 0))
    out = jax.ShapeDtypeStruct((t, f), BF16)
    return _pcall(body, name=name, out_shape=(out,) * 3, grid=(t // tm,), in_specs=[row(k), _resident(w_gate), _resident(w_up)],
                  out_specs=(row(f),) * 3, semantics=("parallel",), vmem_limit=VMEM_LIMIT)(h, w_gate, w_up)


def _swiglu_bwd(dx2, w_down, gate, up, *, name, after=None):
    t, d = dx2.shape
    f = w_down.shape[0]
    tm = _tile(t, ROWS_PER_STEP, 16)
    cn = _tile(f, COLS_PER_DOT, LANE)

    def body(dx_ref, w_ref, gate_ref, up_ref, dgate_ref, dup_ref):
        dxv = dx_ref[...].astype(BF16)
        for j in range(f // cn):
            cols = pl.ds(j * cn, cn)
            dffv = _dot(dxv, w_ref[cols, :], NT)
            gv = gate_ref[:, cols].astype(F32)
            sig = _sigmoid(gv)
            dgate_ref[:, cols] = (dffv * up_ref[:, cols].astype(F32) * sig * (1.0 + gv * (1.0 - sig))).astype(BF16)
            dup_ref[:, cols] = (dffv * gv * sig).astype(BF16)

    row = lambda width: pl.BlockSpec((tm, width), lambda i: (i, 0))
    out = jax.ShapeDtypeStruct((t, f), BF16)
    return _pcall(body, name=name, out_shape=(out, out), grid=(t // tm,), in_specs=[row(d), _resident(w_down), row(f), row(f)],
                  out_specs=(row(f), row(f)), semantics=("parallel",), vmem_limit=VMEM_LIMIT, after=after)(dx2, w_down, gate, up)


def _ple_fwd(x2, pgl, pp, *, name):
    t, d = x2.shape

    def body(x_ref, pgl_ref, pp_ref, o_ref):
        o_ref[...] = x_ref[...] + _sigmoid(pgl_ref[...]) * pp_ref[...]

    return _pcall(body, name=name, out_shape=jax.ShapeDtypeStruct((t, d), F32), grid=(t // ROW_TILE,),
                  in_specs=[_rows(t, d)] * 3, out_specs=_rows(t, d), semantics=("parallel",))(x2, pgl, pp)


def _ple_bwd(dx3, pgl, pp, *, name, after=None):
    t, d = dx3.shape

    def body(dx_ref, pgl_ref, pp_ref, dpgl_ref, dpp_ref):
        dxv = dx_ref[...]
        sig = _sigmoid(pgl_ref[...])
        dpp_ref[...] = (dxv * sig).astype(BF16)
        dpgl_ref[...] = (dxv * pp_ref[...] * sig * (1.0 - sig)).astype(BF16)

    return _pcall(body, name=name, out_shape=(jax.ShapeDtypeStruct((t, d), BF16),) * 2, grid=(t // ROW_TILE,),
                  in_specs=[_rows(t, d)] * 3, out_specs=(_rows(t, d),) * 2, semantics=("parallel",), after=after)(dx3, pgl, pp)


def _loss_head(x3, g, target, *, name):
    t, d = x3.shape

    def body(x_ref, g_ref, t_ref, dx_ref, dg_ref, loss_ref):
        xv = x_ref[...]
        r = lax.rsqrt(jnp.mean(xv * xv, axis=-1, keepdims=True) + EPS)
        xhat = xv * r
        gv = g_ref[...]
        err = xhat * gv - t_ref[...]
        row_loss = jnp.sum(err * err, axis=-1, keepdims=True) * (0.5 / d)
        lpart = jnp.broadcast_to(jnp.sum(row_loss, axis=0, keepdims=True), (1, LANE))
        dy = err * (1.0 / d)
        dyg = dy * gv
        dx_ref[...] = r * (dyg - xhat * jnp.mean(dyg * xhat, axis=-1, keepdims=True))
        gpart = jnp.sum(dy * xhat, axis=0, keepdims=True)

        @pl.when(pl.program_id(0) == 0)
        def _():
            dg_ref[...] = gpart
            loss_ref[...] = lpart

        @pl.when(pl.program_id(0) > 0)
        def _():
            dg_ref[...] += gpart
            loss_ref[...] += lpart

    return _pcall(body, name=name,
                  out_shape=(jax.ShapeDtypeStruct((t, d), F32), jax.ShapeDtypeStruct((1, d), F32), jax.ShapeDtypeStruct((1, LANE), F32)),
                  grid=(t // ROW_TILE,), in_specs=[_rows(t, d), _vec(d), _rows(t, d)],
                  out_specs=(_rows(t, d), _vec(d), _vec(LANE)), semantics=("arbitrary",))(x3, g, target)


def _shift_down(x, d):
    if d == 0:
        return x
    row = lax.broadcasted_iota(jnp.int32, x.shape, 0)
    return jnp.where(row >= d, pltpu.roll(x, d, 0), 0.0)


def _shift_up(x, d):
    if d == 0:
        return x
    t = x.shape[0]
    row = lax.broadcasted_iota(jnp.int32, x.shape, 0)
    return jnp.where(row < t - d, pltpu.roll(x, t - d, 0), 0.0)


def _colsum(x):
    return jnp.sum(x, axis=0, keepdims=True)


def _col(t, idx_fn):
    return pl.BlockSpec((t, LANE), idx_fn)


def _conv_fwd(x, w_ref, taps):
    acc = None
    for j in range(taps):
        term = w_ref[pl.ds(j, 1), :] * _shift_down(x, taps - 1 - j)
        acc = term if acc is None else acc + term
    return acc


def _conv_bwd(x, dy, w_ref, dw_ref, taps):
    dx = None
    for j in range(taps):
        term = w_ref[pl.ds(j, 1), :] * _shift_up(dy, taps - 1 - j)
        dx = term if dx is None else dx + term
        dw_ref[pl.ds(j, 1), :] = _colsum(dy * _shift_down(x, taps - 1 - j))
    return dx


def _qkv_prep_fwd(proj, conv_w, *, name):
    t = proj.shape[0]
    scale = HEAD_DIM ** -0.5

    def body(x_ref, w_ref, o_ref):
        j = pl.program_id(0)
        c = _conv_fwd(x_ref[...], w_ref, QKV_TAPS)
        s = c * _sigmoid(c)
        r = lax.rsqrt(jnp.sum(s * s, axis=-1, keepdims=True) + EPS)
        f = jnp.where(j < 2 * HEADS, r, 1.0) * jnp.where(j < HEADS, scale, 1.0)
        o_ref[0] = s * f

    return _pcall(body, name=name, out_shape=jax.ShapeDtypeStruct((3 * HEADS, t, LANE), F32), grid=(3 * HEADS,),
                  in_specs=[_col(t, lambda j: (0, j)), pl.BlockSpec((QKV_TAPS, LANE), lambda j: (0, j))],
                  out_specs=pl.BlockSpec((1, t, LANE), lambda j: (j, 0, 0)), semantics=("parallel",),
                  vmem_limit=VMEM_LIMIT)(proj, conv_w)


def _qkv_prep_bwd(proj, conv_w, dqkv, *, name):
    t = proj.shape[0]
    scale = HEAD_DIM ** -0.5

    def body(x_ref, w_ref, d_ref, dx_ref, dw_ref):
        j = pl.program_id(0)
        xv = x_ref[...]
        c = _conv_fwd(xv, w_ref, QKV_TAPS)
        sig = _sigmoid(c)
        s = c * sig
        r = lax.rsqrt(jnp.sum(s * s, axis=-1, keepdims=True) + EPS)
        n0 = s * r
        dv = d_ref[0]
        dn0 = dv * jnp.where(j < HEADS, scale, 1.0)
        ds_norm = r * (dn0 - n0 * jnp.sum(dn0 * n0, axis=-1, keepdims=True))
        ds = jnp.where(j < 2 * HEADS, ds_norm, dv)
        dc = ds * sig * (1.0 + c * (1.0 - sig))
        dx_ref[...] = _conv_bwd(xv, dc, w_ref, dw_ref, QKV_TAPS).astype(BF16)

    return _pcall(body, name=name,
                  out_shape=(jax.ShapeDtypeStruct((t, 3 * A_DIM), BF16), jax.ShapeDtypeStruct((QKV_TAPS, 3 * A_DIM), F32)),
                  grid=(3 * HEADS,),
                  in_specs=[_col(t, lambda j: (0, j)), pl.BlockSpec((QKV_TAPS, LANE), lambda j: (0, j)),
                            pl.BlockSpec((1, t, LANE), lambda j: (j, 0, 0))],
                  out_specs=(_col(t, lambda j: (0, j)), pl.BlockSpec((QKV_TAPS, LANE), lambda j: (0, j))),
                  semantics=("parallel",), vmem_limit=VMEM_LIMIT)(proj, conv_w, dqkv)


def _lane_pick(x, lane_idx, lane):
    return jnp.broadcast_to(jnp.sum(jnp.where(lane == lane_idx, x, 0.0), axis=-1, keepdims=True), x.shape)


def _gates_fwd(proj, alog, dtb, *, name):
    t = proj.shape[0]

    def body(x_ref, alog_ref, dtb_ref, g_ref, b_ref):
        xv = x_ref[...]
        lane = lax.broadcasted_iota(jnp.int32, xv.shape, 1)
        gall = -jnp.exp(alog_ref[...]) * _softplus(xv + dtb_ref[...])
        ball = _sigmoid(xv)
        for h in range(HEADS):
            g_ref[h] = _lane_pick(gall, h, lane)
            b_ref[h] = _lane_pick(ball, HEADS + h, lane)

    out = jax.ShapeDtypeStruct((HEADS, t, LANE), F32)
    whole = pl.BlockSpec((HEADS, t, LANE), lambda i: (0, 0, 0))
    return _pcall(body, name=name, out_shape=(out, out), grid=(1,),
                  in_specs=[_col(t, lambda i: (0, AB_COL // LANE)), _vec(LANE), _vec(LANE)], out_specs=(whole, whole),
                  semantics=("arbitrary",), vmem_limit=VMEM_LIMIT)(proj, alog, dtb)


def _gates_bwd(proj, alog, dtb, dg, dbeta, *, name):
    t = proj.shape[0]

    def body(x_ref, alog_ref, dtb_ref, dg_ref, db_ref, dab_ref, dalog_ref, ddtb_ref):
        xv = x_ref[...]
        lane = lax.broadcasted_iota(jnp.int32, xv.shape, 1)
        lane1 = lax.broadcasted_iota(jnp.int32, (1, LANE), 1)
        z = xv + dtb_ref[...]
        nea = -jnp.exp(alog_ref[...])
        da_f = nea * _sigmoid(z)
        g_f = nea * _softplus(z)
        ball = _sigmoid(xv)
        db_f = ball * (1.0 - ball)
        dab = jnp.zeros_like(xv)
        dalog = jnp.zeros((1, LANE), F32)
        for h in range(HEADS):
            dgh = dg_ref[h]
            dab = dab + jnp.where(lane == h, dgh * da_f, 0.0) + jnp.where(lane == HEADS + h, db_ref[h] * db_f, 0.0)
            dalog = dalog + jnp.where(lane1 == h, _colsum(dgh * g_f), 0.0)
        dab_ref[...] = dab.astype(BF16)
        dalog_ref[...] = dalog
        ddtb_ref[...] = jnp.where(lane1 < HEADS, _colsum(dab), 0.0)

    whole = pl.BlockSpec((HEADS, t, LANE), lambda i: (0, 0, 0))
    vec = jax.ShapeDtypeStruct((1, LANE), F32)
    return _pcall(body, name=name, out_shape=(jax.ShapeDtypeStruct((t, LANE), BF16), vec, vec), grid=(1,),
                  in_specs=[_col(t, lambda i: (0, AB_COL // LANE)), _vec(LANE), _vec(LANE), whole, whole],
                  out_specs=(_col(t, lambda i: (0, 0)), _vec(LANE), _vec(LANE)), semantics=("arbitrary",),
                  vmem_limit=VMEM_LIMIT)(proj, alog, dtb, dg, dbeta)


Z_COL = 3 * A_DIM // LANE


def _apost_fwd(o, proj, gn, *, name):
    t = proj.shape[0]

    def body(o_ref, z_ref, gn_ref, y_ref):
        ov = o_ref[0]
        z = z_ref[...]
        r = lax.rsqrt(jnp.mean(ov * ov, axis=-1, keepdims=True) + EPS)
        y_ref[...] = (ov * r * gn_ref[...] * (z * _sigmoid(z))).astype(BF16)

    return _pcall(body, name=name, out_shape=jax.ShapeDtypeStruct((t, A_DIM), BF16), grid=(HEADS,),
                  in_specs=[pl.BlockSpec((1, t, LANE), lambda h: (h, 0, 0)), _col(t, lambda h: (0, Z_COL + h)),
                            pl.BlockSpec((1, LANE), lambda h: (0, 0))],
                  out_specs=_col(t, lambda h: (0, h)), semantics=("parallel",), vmem_limit=VMEM_LIMIT)(o, proj, gn)


def _apost_bwd(o, proj, gn, dmixed, *, name):
    t = proj.shape[0]

    def body(o_ref, z_ref, gn_ref, d_ref, do_ref, dz_ref, dgn_ref):
        ov = o_ref[0]
        z = z_ref[...]
        gnv = gn_ref[...]
        dv = d_ref[...]
        r = lax.rsqrt(jnp.mean(ov * ov, axis=-1, keepdims=True) + EPS)
        ohat = ov * r
        sig = _sigmoid(z)
        dy = dv * (z * sig)
        dz_ref[...] = (dv * ohat * gnv * sig * (1.0 + z * (1.0 - sig))).astype(BF16)
        dyo = dy * gnv
        do_ref[0] = r * (dyo - ohat * jnp.mean(dyo * ohat, axis=-1, keepdims=True))
        part = _colsum(dy * ohat)

        @pl.when(pl.program_id(0) == 0)
        def _():
            dgn_ref[...] = part

        @pl.when(pl.program_id(0) > 0)
        def _():
            dgn_ref[...] += part

    return _pcall(body, name=name,
                  out_shape=(jax.ShapeDtypeStruct((HEADS, t, LANE), F32), jax.ShapeDtypeStruct((t, A_DIM), BF16),
                             jax.ShapeDtypeStruct((1, LANE), F32)),
                  grid=(HEADS,),
                  in_specs=[pl.BlockSpec((1, t, LANE), lambda h: (h, 0, 0)), _col(t, lambda h: (0, Z_COL + h)),
                            pl.BlockSpec((1, LANE), lambda h: (0, 0)), _col(t, lambda h: (0, h))],
                  out_specs=(pl.BlockSpec((1, t, LANE), lambda h: (h, 0, 0)), _col(t, lambda h: (0, h)),
                             pl.BlockSpec((1, LANE), lambda h: (0, 0))),
                  semantics=("arbitrary",), vmem_limit=VMEM_LIMIT)(o, proj, gn, dmixed)


POOL_COL = (AB_COL + LANE) // LANE
CB_COL = POOL_COL + POOL_DIM // LANE
CC_COL = CB_COL + CONV_DIM // LANE
CH_COL = CC_COL + CONV_DIM // LANE
MAX_WIN_LOG2 = 4


def _window_sums(x, shift):
    sums = []
    cur = x
    for k in range(MAX_WIN_LOG2):
        cur = cur + shift(cur, 1 << k)
        sums.append(cur)
    return sums


def _pick_window(sums, win):
    out = sums[-1]
    for k in range(MAX_WIN_LOG2 - 2, -1, -1):
        out = jnp.where(win == float(2 << k), sums[k], out)
    return out


def _pool_counts(shape, win):
    row = lax.broadcasted_iota(jnp.int32, shape, 0).astype(F32)
    return jnp.minimum(row + 1.0, win)


def _pool_fwd(proj, win, wbd, scale, *, name):
    t = proj.shape[0]

    def body(x_ref, win_ref, w_ref, s_ref, y_ref):
        xv = x_ref[...]
        winv = win_ref[...]
        pooled = _pick_window(_window_sums(xv, _shift_down), winv) / _pool_counts(xv.shape, winv) - xv
        y_ref[...] = (_dot(pooled, w_ref[0], NN) * s_ref[...]).astype(BF16)

    nb = POOL_DIM // LANE
    vec = pl.BlockSpec((1, LANE), lambda b: (0, b))
    return _pcall(body, name=name, out_shape=jax.ShapeDtypeStruct((t, POOL_DIM), BF16), grid=(nb,),
                  in_specs=[_col(t, lambda b: (0, POOL_COL + b)), vec, pl.BlockSpec((1, LANE, LANE), lambda b: (b, 0, 0)), vec],
                  out_specs=_col(t, lambda b: (0, b)), semantics=("parallel",), vmem_limit=VMEM_LIMIT)(proj, win, wbd, scale)


def _pool_bwd(proj, win, wbd, scale, dmixed, *, name):
    t = proj.shape[0]

    def body(x_ref, win_ref, w_ref, s_ref, d_ref, dx_ref, dw_ref, ds_ref):
        xv = x_ref[...]
        winv = win_ref[...]
        cnt = _pool_counts(xv.shape, winv)
        pooled = _pick_window(_window_sums(xv, _shift_down), winv) / cnt - xv
        dv = d_ref[...]
        ds_ref[...] = _colsum(dv * _dot(pooled, w_ref[0], NN))
        dy0 = dv * s_ref[...]
        dw_ref[0] = _dot(pooled, dy0, TN)
        dpooled = _dot(dy0, w_ref[0], NT)
        dmean = dpooled / cnt
        dx_ref[...] = (_pick_window(_window_sums(dmean, _shift_up), winv) - dpooled).astype(BF16)

    nb = POOL_DIM // LANE
    vec = pl.BlockSpec((1, LANE), lambda b: (0, b))
    mat = pl.BlockSpec((1, LANE, LANE), lambda b: (b, 0, 0))
    first = A_DIM // LANE
    return _pcall(body, name=name,
                  out_shape=(jax.ShapeDtypeStruct((t, POOL_DIM), BF16), jax.ShapeDtypeStruct((nb, LANE, LANE), F32),
                             jax.ShapeDtypeStruct((1, POOL_DIM), F32)),
                  grid=(nb,),
                  in_specs=[_col(t, lambda b: (0, POOL_COL + b)), vec, mat, vec, _col(t, lambda b: (0, first + b))],
                  out_specs=(_col(t, lambda b: (0, b)), mat, vec), semantics=("parallel",),
                  vmem_limit=VMEM_LIMIT)(proj, win, wbd, scale, dmixed)


def _sconv_fwd(proj, w, *, name):
    t = proj.shape[0]

    def body(cb_ref, cc_ref, ch_ref, w_ref, y_ref):
        y_ref[...] = (cb_ref[...] * _conv_fwd(cc_ref[...] * ch_ref[...], w_ref, CONV_TAPS)).astype(BF16)

    nb = CONV_DIM // LANE
    return _pcall(body, name=name, out_shape=jax.ShapeDtypeStruct((t, CONV_DIM), BF16), grid=(nb,),
                  in_specs=[_col(t, lambda b: (0, CB_COL + b)), _col(t, lambda b: (0, CC_COL + b)),
                            _col(t, lambda b: (0, CH_COL + b)), pl.BlockSpec((CONV_TAPS, LANE), lambda b: (0, b))],
                  out_specs=_col(t, lambda b: (0, b)), semantics=("parallel",), vmem_limit=VMEM_LIMIT)(proj, proj, proj, w)


def _sconv_bwd(proj, w, dmixed, *, name):
    t = proj.shape[0]

    def body(cb_ref, cc_ref, ch_ref, w_ref, d_ref, dcb_ref, dcc_ref, dch_ref, dw_ref):
        cc = cc_ref[...]
        ch = ch_ref[...]
        u = cc * ch
        dv = d_ref[...]
        dcb_ref[...] = (dv * _conv_fwd(u, w_ref, CONV_TAPS)).astype(BF16)
        du = _conv_bwd(u, dv * cb_ref[...], w_ref, dw_ref, CONV_TAPS)
        dcc_ref[...] = (du * ch).astype(BF16)
        dch_ref[...] = (du * cc).astype(BF16)

    nb = CONV_DIM // LANE
    first = (A_DIM + POOL_DIM) // LANE
    act = jax.ShapeDtypeStruct((t, CONV_DIM), BF16)
    wspec = pl.BlockSpec((CONV_TAPS, LANE), lambda b: (0, b))
    ospec = _col(t, lambda b: (0, b))
    return _pcall(body, name=name, out_shape=(act, act, act, jax.ShapeDtypeStruct((CONV_TAPS, CONV_DIM), F32)), grid=(nb,),
                  in_specs=[_col(t, lambda b: (0, CB_COL + b)), _col(t, lambda b: (0, CC_COL + b)),
                            _col(t, lambda b: (0, CH_COL + b)), wspec, _col(t, lambda b: (0, first + b))],
                  out_specs=(ospec, ospec, ospec, wspec), semantics=("parallel",),
                  vmem_limit=VMEM_LIMIT)(proj, proj, proj, w, dmixed)


def _chunk_masks():
    r = lax.broadcasted_iota(jnp.int32, (CHUNK, CHUNK), 0)
    c = lax.broadcasted_iota(jnp.int32, (CHUNK, CHUNK), 1)
    return r >= c, r > c, jnp.where(r == c, 1.0, 0.0).astype(F32)


def _split(a):
    hi = a.astype(BF16)
    return hi, (a - hi.astype(F32)).astype(BF16)


def _dot_split(a, b, dims):
    (ah, al), (bh, bl) = a, b
    return _dot(ah, bh, dims) + _dot(ah, bl, dims) + _dot(al, bh, dims)


def _tri_inv(lows, eye):
    xs = [eye - low for low in lows]
    ps = [_split(low) for low in lows]
    ps = [_split(_dot_split(p, p, NN)) for p in ps]
    for i in range(5):
        xs = [x + _dot_split(_split(x), p, NN) for x, p in zip(xs, ps)]
        if i < 4:
            ps = [_split(_dot_split(p, p, NN)) for p in ps]
    return xs


def _prefix_sum_rows(x):
    for k in range(6):
        x = x + _shift_down(x, 1 << k)
    return x


def _suffix_sum_rows(x):
    for k in range(6):
        x = x + _shift_up(x, 1 << k)
    return x


def _chunk_decay(g, incl):
    gcb = _prefix_sum_rows(g)
    gtot = _colsum(g)
    col = gcb[:, :CHUNK]
    row = gcb.T[:CHUNK, :]
    decay = jnp.exp(jnp.where(incl, col - row, -1e30))
    return gcb, gtot, decay


CHUNKS_PER_STEP = 4


def _heads_of(ref, base, rows):
    return [ref[base + h, rows, :] for h in range(HEADS)]


def _chunk_rows(j):
    return pl.ds(j * CHUNK, CHUNK)


def _deltanet_prep(qkv, g, beta, *, name):
    t = qkv.shape[1]
    n_chunks = t // CHUNK
    per = CHUNKS_PER_STEP
    probs = [(j, h) for j in range(per) for h in range(HEADS)]

    def body(qkv_ref, g_ref, b_ref, u_ref, w_ref, qg_ref, kg_ref, attn_ref, tm_ref):
        incl, strict, eye = _chunk_masks()
        q = [qkv_ref[h, _chunk_rows(j), :] for j, h in probs]
        k = [qkv_ref[HEADS + h, _chunk_rows(j), :] for j, h in probs]
        v = [qkv_ref[2 * HEADS + h, _chunk_rows(j), :] for j, h in probs]
        bv = [b_ref[h, _chunk_rows(j), :] for j, h in probs]
        dec = [_chunk_decay(g_ref[h, _chunk_rows(j), :], incl) for j, h in probs]
        kb = [a * b for a, b in zip(k, bv)]
        low = [jnp.where(strict, _dot(a, b, NT) * d[2], 0.0) for a, b, d in zip(kb, k, dec)]
        tm = _tri_inv(low, eye)
        egc = [jnp.exp(d[0]) for d in dec]
        u = [_dot(m, a * b, NN) for m, a, b in zip(tm, v, bv)]
        w = [_dot(m, a * e, NN) for m, a, e in zip(tm, kb, egc)]
        attn = [_dot(a, b, NT) * d[2] for a, b, d in zip(q, k, dec)]
        for i, (j, h) in enumerate(probs):
            rows = _chunk_rows(j)
            u_ref[h, rows, :] = u[i]
            w_ref[h, rows, :] = w[i].astype(BF16)
            qg_ref[h, rows, :] = (q[i] * egc[i]).astype(BF16)
            kg_ref[h, rows, :] = (k[i] * jnp.exp(dec[i][1] - dec[i][0])).astype(BF16)
            attn_ref[j, h] = attn[i].astype(BF16)
            tm_ref[j, h] = tm[i]

    act = lambda heads: pl.BlockSpec((heads, per * CHUNK, LANE), lambda n: (0, n, 0))
    mat = pl.BlockSpec((per, HEADS, CHUNK, CHUNK), lambda n: (n, 0, 0, 0))
    return _pcall(
        body, name=name,
        out_shape=(jax.ShapeDtypeStruct((HEADS, t, LANE), F32),) + (jax.ShapeDtypeStruct((HEADS, t, LANE), BF16),) * 3
        + (jax.ShapeDtypeStruct((n_chunks, HEADS, CHUNK, CHUNK), BF16), jax.ShapeDtypeStruct((n_chunks, HEADS, CHUNK, CHUNK), F32)),
        grid=(n_chunks // per,), in_specs=[act(3 * HEADS), act(HEADS), act(HEADS)],
        out_specs=(act(HEADS),) * 4 + (mat, mat), semantics=("parallel",), vmem_limit=VMEM_LIMIT)(qkv, g, beta)


SCAN_CHUNKS_PER_STEP = 8


def _deltanet_scan(u, w, qg, kg, attn, g, *, name):
    t = u.shape[1]
    n_chunks = t // CHUNK
    per = SCAN_CHUNKS_PER_STEP

    def body(u_ref, w_ref, qg_ref, kg_ref, attn_ref, g_ref, o_ref, vn_ref, st_ref, s_ref):
        @pl.when(pl.program_id(0) == 0)
        def _():
            s_ref[...] = jnp.zeros_like(s_ref)

        for j in range(per):
            rows = _chunk_rows(j)
            s = [s_ref[h] for h in range(HEADS)]
            vn = [u_ref[h, rows, :] - _dot(w_ref[h, rows, :], s[h], NN) for h in range(HEADS)]
            o = [_dot(qg_ref[h, rows, :], s[h], NN) + _dot(attn_ref[j, h], vn[h], NN) for h in range(HEADS)]
            eg = [jnp.exp(_colsum(g_ref[h, rows, :])) for h in range(HEADS)]
            for h in range(HEADS):
                st_ref[j, h] = s[h]
                s_ref[h] = s[h] * eg[h] + _dot(kg_ref[h, rows, :], vn[h], TN)
                o_ref[h, rows, :] = o[h]
                vn_ref[h, rows, :] = vn[h]

    act = pl.BlockSpec((HEADS, per * CHUNK, LANE), lambda n: (0, n, 0))
    out = jax.ShapeDtypeStruct((HEADS, t, LANE), F32)
    return _pcall(
        body, name=name, out_shape=(out, out, jax.ShapeDtypeStruct((n_chunks, HEADS, LANE, LANE), F32)), grid=(n_chunks // per,),
        in_specs=[act] * 4 + [pl.BlockSpec((per, HEADS, CHUNK, CHUNK), lambda n: (n, 0, 0, 0)), act],
        out_specs=(act, act, pl.BlockSpec((per, HEADS, LANE, LANE), lambda n: (n, 0, 0, 0))),
        scratch_shapes=[pltpu.VMEM((HEADS, LANE, LANE), F32)], semantics=("arbitrary",))(u, w, qg, kg, attn, g)


def _deltanet_bscan(w, qg, kg, attn, g, do, *, name):
    t = w.shape[1]
    n_chunks = t // CHUNK
    per = SCAN_CHUNKS_PER_STEP
    steps = n_chunks // per

    def body(w_ref, qg_ref, kg_ref, attn_ref, g_ref, do_ref, dvn_ref, dsn_ref, ds_ref):
        @pl.when(pl.program_id(0) == 0)
        def _():
            ds_ref[...] = jnp.zeros_like(ds_ref)

        for j in reversed(range(per)):
            rows = _chunk_rows(j)
            dsn = [ds_ref[h] for h in range(HEADS)]
            dov = [do_ref[h, rows, :] for h in range(HEADS)]
            dvn = [_dot(attn_ref[j, h], dov[h], TN) + _dot(kg_ref[h, rows, :], dsn[h], NN) for h in range(HEADS)]
            eg = [jnp.exp(_colsum(g_ref[h, rows, :])) for h in range(HEADS)]
            for h in range(HEADS):
                dsn_ref[j, h] = dsn[h]
                ds_ref[h] = _dot(qg_ref[h, rows, :], dov[h], TN) + eg[h] * dsn[h] - _dot(w_ref[h, rows, :], dvn[h], TN)
                dvn_ref[h, rows, :] = dvn[h]

    act = pl.BlockSpec((HEADS, per * CHUNK, LANE), lambda n: (0, steps - 1 - n, 0))
    return _pcall(
        body, name=name,
        out_shape=(jax.ShapeDtypeStruct((HEADS, t, LANE), F32), jax.ShapeDtypeStruct((n_chunks, HEADS, LANE, LANE), F32)),
        grid=(steps,),
        in_specs=[act] * 3 + [pl.BlockSpec((per, HEADS, CHUNK, CHUNK), lambda n: (steps - 1 - n, 0, 0, 0)), act, act],
        out_specs=(act, pl.BlockSpec((per, HEADS, LANE, LANE), lambda n: (steps - 1 - n, 0, 0, 0))),
        scratch_shapes=[pltpu.VMEM((HEADS, LANE, LANE), F32)], semantics=("arbitrary",))(w, qg, kg, attn, g, do)


def _sum_all(x):
    return jnp.sum(jnp.sum(x, axis=1, keepdims=True), axis=0, keepdims=True)


def _rowsum(x):
    return jnp.sum(x, axis=1, keepdims=True)


def _deltanet_post(qkv, g, beta, tmats, states, dstates, do, dvn, vn, *, name):
    t = qkv.shape[1]
    n_chunks = t // CHUNK
    per = CHUNKS_PER_STEP
    probs = [(j, h) for j in range(per) for h in range(HEADS)]

    def body(qkv_ref, g_ref, b_ref, tm_ref, st_ref, dsn_ref, do_ref, dvn_ref, vn_ref, dqkv_ref, dg_ref, db_ref):
        incl, strict, _ = _chunk_masks()
        ones = jnp.ones((CHUNK, LANE), BF16)
        last_row = lax.broadcasted_iota(jnp.int32, (CHUNK, LANE), 0) == CHUNK - 1
        z = lambda f, *cols: [f(*a) for a in zip(*cols)]
        q = [qkv_ref[h, _chunk_rows(j), :] for j, h in probs]
        k = [qkv_ref[HEADS + h, _chunk_rows(j), :] for j, h in probs]
        v = [qkv_ref[2 * HEADS + h, _chunk_rows(j), :] for j, h in probs]
        bv = [b_ref[h, _chunk_rows(j), :] for j, h in probs]
        dov = [do_ref[h, _chunk_rows(j), :] for j, h in probs]
        dvn_ = [dvn_ref[h, _chunk_rows(j), :] for j, h in probs]
        vn_ = [vn_ref[h, _chunk_rows(j), :] for j, h in probs]
        tm = [tm_ref[j, h] for j, h in probs]
        s = [st_ref[j, h] for j, h in probs]
        dsn = [dsn_ref[j, h] for j, h in probs]
        dec = [_chunk_decay(g_ref[h, _chunk_rows(j), :], incl) for j, h in probs]
        decay = [d[2] for d in dec]
        egc = [jnp.exp(d[0]) for d in dec]
        ekg = [jnp.exp(d[1] - d[0]) for d in dec]
        kb = z(lambda a, b: a * b, k, bv)
        vb = z(lambda a, b: a * b, v, bv)
        kbg = z(lambda a, b: a * b, kb, egc)
        qg = z(lambda a, b: a * b, q, egc)
        kg = z(lambda a, b: a * b, k, ekg)
        kk = z(lambda a, b: _dot(a, b, NT), kb, k)
        qk = z(lambda a, b: _dot(a, b, NT), q, k)
        dattn = z(lambda a, b: jnp.where(incl, _dot(a, b, NT), 0.0), dov, vn_)
        dqg = z(lambda a, b: _dot(a, b, NT), dov, s)
        dkg = z(lambda a, b: _dot(a, b, NT), vn_, dsn)
        dglast = z(lambda a, b, c, d, e: _sum_all(a * b) * jnp.exp(e[1]) + _sum_all(c * d), s, dsn, dkg, kg, dec)
        dw = z(lambda a, b: -_dot(a, b, NT), dvn_, s)
        dtm = z(lambda a, b, c, d: _dot(a, b, NT) + _dot(c, d, NT), dvn_, vb, dw, kbg)
        dvb = z(lambda a, b: _dot(a, b, TN), tm, dvn_)
        dkbg = z(lambda a, b: _dot(a, b, TN), tm, dw)
        dlow = z(lambda a, b: jnp.where(strict, -_dot(_dot(a, b, TN), a, NT), 0.0), tm, dtm)
        dkk = z(lambda a, b: a * b, dlow, decay)
        dqk = z(lambda a, b: a * b, dattn, decay)
        dkb = z(lambda a, b, c, d: _dot(a, b, NN) + c * d, dkk, k, dkbg, egc)
        dk = z(lambda a, b, c, d, e, f, g_, h_: _dot(a, b, TN) + _dot(c, d, TN) + e * f + g_ * h_, dkk, kb, dqk, q, dkg, ekg, dkb, bv)
        dq = z(lambda a, b, c, d: _dot(a, b, NN) + c * d, dqk, k, dqg, egc)
        m = z(lambda a, b, c, d, e: (a * b + c * d) * e, dlow, kk, dattn, qk, decay)
        mcol = [_dot(mh, ones, TN) + _dot(ml, ones, TN) for mh, ml in (_split(a) for a in m)]
        for i, (j, h) in enumerate(probs):
            rows = _chunk_rows(j)
            dqkv_ref[h, rows, :] = dq[i]
            dqkv_ref[HEADS + h, rows, :] = dk[i]
            dqkv_ref[2 * HEADS + h, rows, :] = dvb[i] * bv[i]
            db_ref[h, rows, :] = jnp.broadcast_to(_rowsum(dkb[i] * k[i] + dvb[i] * v[i]), (CHUNK, LANE))
            dgc = (_rowsum(dqg[i] * qg[i] + dkbg[i] * kbg[i] - dkg[i] * kg[i]) + _rowsum(m[i]) - mcol[i]
                   + jnp.where(last_row, dglast[i], 0.0))
            dg_ref[h, rows, :] = _suffix_sum_rows(dgc)

    act = lambda heads: pl.BlockSpec((heads, per * CHUNK, LANE), lambda n: (0, n, 0))
    mat = lambda d: pl.BlockSpec((per, HEADS, d, d), lambda n: (n, 0, 0, 0))
    out = jax.ShapeDtypeStruct((HEADS, t, LANE), F32)
    return _pcall(
        body, name=name, out_shape=(jax.ShapeDtypeStruct((3 * HEADS, t, LANE), F32), out, out), grid=(n_chunks // per,),
        in_specs=[act(3 * HEADS), act(HEADS), act(HEADS), mat(CHUNK), mat(LANE), mat(LANE), act(HEADS), act(HEADS), act(HEADS)],
        out_specs=(act(3 * HEADS), act(HEADS), act(HEADS)), semantics=("parallel",),
        vmem_limit=VMEM_LIMIT)(qkv, g, beta, tmats, states, dstates, do, dvn, vn)


ANY = pl.BlockSpec(memory_space=pl.ANY)
PEERS = N_DEV - 1


def _all_gather(arrays, *, name):
    n = len(arrays)

    def body(*refs):
        ins, outs = refs[:n], refs[n:2 * n]
        send_sems, recv_sems, local_sems = refs[2 * n:]
        x, y, c = lax.axis_index("x"), lax.axis_index("y"), lax.axis_index("c")
        me, sibling = (x, y, c), (x, y, 1 - c)
        chips = [(1 - x, y), (x, 1 - y), (1 - x, 1 - y)]

        def copy(a, k, block, to, src=None):
            dst = outs[a].at[4 * block[0] + 2 * block[1] + block[2]]
            return pltpu.make_async_remote_copy(src_ref=dst if src is None else src, dst_ref=dst, send_sem=send_sems.at[a * PEERS + k],
                                                recv_sem=recv_sems.at[a * PEERS + k], device_id=to, device_id_type=MESH)

        local = [pltpu.make_async_copy(ins[a], outs[a].at[4 * x + 2 * y + c], local_sems.at[a]) for a in range(n)]
        for cp in local:
            cp.start()
        first = []
        for a in range(n):
            first += [copy(a, 1 + j, me, (*chip, c), src=ins[a]) for j, chip in enumerate(chips)]
            first.append(copy(a, 0, me, sibling, src=ins[a]))
        for cp in first:
            cp.start()
        passed = []
        for a in range(n):
            for j, chip in enumerate(chips):
                copy(a, 1 + j, (*chip, c), me).wait_recv()
                fwd = copy(a, 4 + j, (*chip, c), sibling)
                fwd.start()
                passed.append(fwd)
        for a in range(n):
            copy(a, 0, sibling, me).wait_recv()
            for j, chip in enumerate(chips):
                copy(a, 4 + j, (*chip, 1 - c), me).wait_recv()
        for cp in first + passed:
            cp.wait_send()
        for cp in local:
            cp.wait()

    return _pcall(body, name=name, out_shape=tuple(jax.ShapeDtypeStruct((N_DEV,) + a.shape, a.dtype) for a in arrays),
                  in_specs=[ANY] * n, out_specs=(ANY,) * n,
                  scratch_shapes=[pltpu.SemaphoreType.DMA((n * PEERS,)), pltpu.SemaphoreType.DMA((n * PEERS,)),
                                  pltpu.SemaphoreType.DMA((n,))])(*arrays)


CHIPS = 4


def _pair_exchange(arrays, *, name):
    n = len(arrays)

    def body(*refs):
        ins, outs = refs[:n], refs[n:2 * n]
        send_sems, recv_sems = refs[2 * n:]
        x, y, c = lax.axis_index("x"), lax.axis_index("y"), lax.axis_index("c")
        copies = []
        for a in range(n):
            for q in range(CHIPS):
                cp = pltpu.make_async_remote_copy(src_ref=ins[a].at[2 * q + 1 - c], dst_ref=outs[a].at[q],
                                                  send_sem=send_sems.at[a * CHIPS + q], recv_sem=recv_sems.at[a * CHIPS + q],
                                                  device_id=(x, y, 1 - c), device_id_type=MESH)
                cp.start()
                copies.append(cp)
        for cp in copies:
            cp.wait()

    return _pcall(body, name=name, out_shape=tuple(jax.ShapeDtypeStruct((CHIPS,) + a.shape[1:], a.dtype) for a in arrays),
                  in_specs=[ANY] * n, out_specs=(ANY,) * n,
                  scratch_shapes=[pltpu.SemaphoreType.DMA((n * CHIPS,)), pltpu.SemaphoreType.DMA((n * CHIPS,))])(*arrays)


def _pair_add(blocks, theirs, *, name):
    _, r, c_ = blocks.shape
    tr = _tile(r, 512, 16)

    def body(mine_ref, theirs_ref, o_ref):
        core = lax.axis_index("c")
        own = jnp.where(core == 0, mine_ref[0, 0].astype(F32), mine_ref[0, 1].astype(F32))
        o_ref[0] = (own + theirs_ref[0].astype(F32)).astype(o_ref.dtype)

    spec = pl.BlockSpec((1, tr, c_), lambda q, i: (q, i, 0))
    return _pcall(body, name=name, out_shape=jax.ShapeDtypeStruct(theirs.shape, theirs.dtype), grid=(CHIPS, r // tr),
                  in_specs=[pl.BlockSpec((1, 2, tr, c_), lambda q, i: (q, 0, i, 0)), spec], out_specs=spec,
                  semantics=("parallel", "parallel"), vmem_limit=VMEM_LIMIT)(blocks.reshape(CHIPS, 2, r, c_), theirs)


HBM = pl.BlockSpec(memory_space=pltpu.HBM)
SEM = pl.BlockSpec(memory_space=pltpu.SEMAPHORE)
EFFECT = pltpu.SideEffectType.DATAFLOW_SIDE_EFFECTING


GATHER, CHIP_GATHER, CHIP_SCATTER = "gather", "chip_gather", "chip_scatter"
PEERS_OF = {GATHER: N_DEV - 1, CHIP_GATHER: CHIPS - 1, CHIP_SCATTER: CHIPS - 1}


def _direct_copies(srcs, lands, send_sems, recv_sems, local_sems, kind):
    x, y, c = lax.axis_index("x"), lax.axis_index("y"), lax.axis_index("c")
    peers = PEERS_OF[kind]
    mine = 2 * x + y if kind == CHIP_SCATTER else 4 * x + 2 * y + c
    copies = []
    for a, (src, land) in enumerate(zip(srcs, lands)):
        for k in range(1, peers + 1):
            bits = k if kind == GATHER else 2 * k
            px = 1 - x if bits & 4 else x
            py = 1 - y if bits & 2 else y
            pc = 1 - c if bits & 1 else c
            copies.append(pltpu.make_async_remote_copy(
                src_ref=src.at[2 * px + py] if kind == CHIP_SCATTER else src, dst_ref=land.at[mine],
                send_sem=send_sems.at[a * peers + k - 1], recv_sem=recv_sems.at[a * peers + k - 1],
                device_id=(px, py, pc), device_id_type=MESH))
    for a, (src, land) in enumerate(zip(srcs, lands)):
        copies.append(pltpu.make_async_copy(src.at[mine] if kind == CHIP_SCATTER else src, land.at[mine], local_sems.at[a]))
    return copies


def _pair_swap(arrays, *, name):
    n = len(arrays)

    def body(*refs):
        mine, zones = refs[:n], refs[n:2 * n]
        send_sems, recv_sems = refs[2 * n:]
        x, y, c = lax.axis_index("x"), lax.axis_index("y"), lax.axis_index("c")
        copies = []
        for a in range(n):
            for q in range(CHIPS):
                copies.append(pltpu.make_async_remote_copy(
                    src_ref=mine[a].at[2 * q + c], dst_ref=zones[a].at[2 * q + c], send_sem=send_sems.at[a * CHIPS + q],
                    recv_sem=recv_sems.at[a * CHIPS + q], device_id=(x, y, 1 - c), device_id_type=MESH))
        for cp in copies:
            cp.start()
        for cp in copies:
            cp.wait()

    return _pcall(body, name=name, out_shape=tuple(jax.ShapeDtypeStruct(a.shape, a.dtype) for a in arrays),
                  in_specs=[ANY] * n, out_specs=(ANY,) * n, input_output_aliases={i: i for i in range(n)},
                  scratch_shapes=[pltpu.SemaphoreType.DMA((n * CHIPS,)), pltpu.SemaphoreType.DMA((n * CHIPS,))])(*arrays)


def _exchange_start(groups, kind, *, name, after=None):
    srcs = [s for group in groups for s in group]
    n = len(srcs)
    sizes = [len(group) for group in groups]
    starts = [sum(sizes[:g]) for g in range(len(groups))]
    land_shapes = [s.shape if kind == CHIP_SCATTER else (N_DEV,) + s.shape for s in srcs]
    peers = PEERS_OF[kind]
    extra = [] if after is None else [after]

    def body(*refs):
        srcs_, lands = refs[:n], refs[n:2 * n]
        token = refs[-1]
        sem_refs = refs[2 * n + len(extra):]
        for g, (at, size) in enumerate(zip(starts, sizes)):
            send_sems, recv_sems, local_sems = sem_refs[3 * g:3 * g + 3]
            for cp in _direct_copies(srcs_[at:at + size], lands[at:at + size], send_sems, recv_sems, local_sems, kind):
                cp.start()
        token[...] = jnp.zeros_like(token)

    sems = tuple(t for size in sizes for t in (pltpu.SemaphoreType.DMA((size * peers,)), pltpu.SemaphoreType.DMA((size * peers,)),
                                               pltpu.SemaphoreType.DMA((size,))))
    thru = tuple(pltpu.HBM(s.shape, s.dtype) for s in srcs) + tuple(pltpu.HBM(shp, s.dtype) for shp, s in zip(land_shapes, srcs))
    ins = [pltpu.with_memory_space_constraint(s, pltpu.HBM) for s in srcs]
    ins += [pltpu.with_memory_space_constraint(lax.empty(shp, s.dtype), pltpu.HBM) for shp, s in zip(land_shapes, srcs)]
    out = pl.pallas_call(
        body, name=name, out_shape=sems + thru + (jax.ShapeDtypeStruct((SUBLANE, LANE), F32),),
        in_specs=[HBM] * (2 * n) + [ANY] * len(extra),
        out_specs=(SEM,) * len(sems) + (HBM,) * (2 * n) + (pl.BlockSpec(memory_space=pltpu.VMEM),),
        input_output_aliases={i: len(sems) + i for i in range(2 * n)},
        compiler_params=pltpu.CompilerParams(has_side_effects=EFFECT))(*ins, *extra)
    arrays = out[len(sems):-1]
    started = [tuple(out[3 * g:3 * g + 3]) + tuple(arrays[at:at + size]) + tuple(arrays[n + at:n + at + size])
               for g, (at, size) in enumerate(zip(starts, sizes))]
    return started, out[-1]


def _exchange_wait(started, after, kind, *, name):
    n = (len(started) - 3) // 2
    sems, arrays = started[:3], started[3:]

    def body(*refs):
        srcs_, lands = refs[:n], refs[n:2 * n]
        send_sems, recv_sems, local_sems = refs[2 * n:2 * n + 3]
        for cp in _direct_copies(srcs_, lands, send_sems, recv_sems, local_sems, kind):
            cp.wait()

    out = pl.pallas_call(
        body, name=name, out_shape=tuple(pltpu.HBM(a.shape, a.dtype) for a in arrays),
        in_specs=[HBM] * (2 * n) + [SEM] * 3 + [ANY], out_specs=(HBM,) * (2 * n),
        input_output_aliases={i: i for i in range(2 * n)},
        compiler_params=pltpu.CompilerParams(has_side_effects=EFFECT))(*arrays, *sems, after)
    return out[n:]


def _adamw_reduce(w, parts, m, v, *, name, after=None):
    layers, r, c = w.shape
    assert len(parts) == layers
    senders = parts[0].shape[0]
    tr = _tile(r, 512, 16)
    tiles = r // tr
    bc1 = 1.0 - ADAM_B1 ** ADAM_STEP
    bc2 = 1.0 - ADAM_B2 ** ADAM_STEP

    def body(w_ref, *rest):
        p_refs = rest[:layers]
        m_ref, v_ref, g_ref, d_ref, nm_ref, nv_ref = rest[layers:]

        def update(p_ref):
            g = p_ref[0, :, pl.ds(0, c)].astype(F32)
            for s in range(1, senders):
                g = g + p_ref[s, :, pl.ds(0, c)].astype(F32)
            nm = ADAM_B1 * m_ref[0] + (1.0 - ADAM_B1) * g
            nv = ADAM_B2 * v_ref[0] + (1.0 - ADAM_B2) * (g * g)
            g_ref[0] = g
            nm_ref[0] = nm
            nv_ref[0] = nv
            d_ref[0] = -ADAM_LR * ((nm / bc1) / (jnp.sqrt(nv / bc2) + ADAM_EPS) + ADAM_WD * w_ref[0])

        for layer in range(layers):
            pl.when(pl.program_id(0) == layer)(functools.partial(update, p_refs[layer]))

    def part_spec(layer, shape):
        rest = 0 if layer > 0 else tiles - 1
        return pl.BlockSpec((senders, tr, shape[2]), lambda l, i: (0, jnp.where(l == layer, i, rest), 0))

    spec = pl.BlockSpec((1, tr, c), lambda l, i: (l, i, 0))
    out = jax.ShapeDtypeStruct((layers, r, c), F32)
    return _pcall(body, name=name, out_shape=(out,) * 4, grid=(layers, tiles),
                  in_specs=[spec] + [part_spec(layer, p.shape) for layer, p in enumerate(parts)] + [spec, spec],
                  out_specs=(spec,) * 4, semantics=("arbitrary", "arbitrary"), vmem_limit=VMEM_LIMIT, after=after)(w, *parts, m, v)


def _pool_windows():
    return jnp.repeat(jnp.asarray(POOL_WINDOWS, F32), POOL_DIM // len(POOL_WINDOWS))[None, :]


def _block_diag_pairs(pool_w):
    z = jnp.zeros_like(pool_w[0])
    return jnp.stack([jnp.block([[pool_w[2 * b], z], [z, pool_w[2 * b + 1]]]) for b in range(2)])


def _pad_lanes(vec):
    return jnp.zeros((1, LANE), F32).at[0, :vec.shape[0]].set(vec)


FF_SHARD = D_FF // N_DEV
FF_BLOCK = 384
D_FF_PAD = N_DEV * FF_BLOCK


def _layer_fwd(x, p_i, wt, fetch):
    wt = {**wt, **fetch(0, x)}
    h1 = _rmsnorm_fwd(x, wt["norm1_g"], name="rmsnorm_fwd")
    proj = _matmul(h1, wt["w_in"], "nt", name="mm_in")
    qkv = _qkv_prep_fwd(proj, wt["conv_qkv"], name="qkv_prep_fwd")
    g, beta = _gates_fwd(proj, wt["a_log"], wt["dt_bias"], name="gates_fwd")
    u, w, qg, kg, attn, tmats = _deltanet_prep(qkv, g, beta, name="deltanet_prep")
    o, vn, states = _deltanet_scan(u, w, qg, kg, attn, g, name="deltanet_scan")
    wt.update(fetch(1, o))
    o_a = _apost_fwd(o, proj, wt["onorm_g"], name="apost_fwd")
    o_b = _pool_fwd(proj, wt["pool_win"], wt["pool_wbd"], wt["pool_scale"], name="pool_fwd")
    o_c = _sconv_fwd(proj, wt["sconv_w"], name="sconv_fwd")
    mixed = jnp.concatenate([o_a, o_b, o_c], axis=1)
    x1 = _matmul(mixed, wt["w_out"], "nn", res=x, name="mm_out")
    h2 = _rmsnorm_fwd(x1, wt["norm2_g"], name="rmsnorm_fwd")
    wt.update(fetch(2, h2))
    ff, gate, up = _swiglu_fwd(h2, wt["w_gate"], wt["w_up"], name="swiglu_fwd")
    wt.update(fetch(3, ff))
    x2 = _matmul(ff, wt["w_down"], "nn", res=x1, name="mm_down")
    wt.update(fetch(4, x2))
    pgl = _matmul(x2, wt["ple_gate"], "nn", name="mm_pleg")
    pp = _matmul(p_i, wt["ple_proj"], "nn", b_blocked=True, name="mm_plep")
    x3 = _ple_fwd(x2, pgl, pp, name="ple_fwd")
    saved = dict(x=x, h1=h1, proj=proj, qkv=qkv, g=g, beta=beta, o=o, states=states, tmats=tmats, mixed=mixed, x1=x1, h2=h2,
                 gate=gate, up=up, ff=ff, x2=x2, pgl=pgl, pp=pp, p=p_i, w=w, qg=qg, kg=kg, attn=attn, vn=vn, wt=wt)
    return x3, saved


def _col_blocks(g):
    a = g.shape[0]
    return jnp.transpose(g.reshape(a, N_DEV, -1), (1, 0, 2))


def _cols_joined(blocks):
    return jnp.transpose(blocks, (1, 0, 2)).reshape(blocks.shape[1], -1)


def _layer_bwd(dx3, sv, emit, after=None):
    gr, big = {}, {}
    wt = sv["wt"]
    rows = D_MODEL // N_DEV
    dpgl, dpp = _ple_bwd(dx3, sv["pgl"], sv["pp"], name="ple_bwd", after=after)
    big["ple_proj"] = _matmul(sv["p"], dpp, "tn", out_blocked=(N_DEV, rows), out_dtype=BF16, name="mm_dplep")
    big["ple_gate"] = _matmul(sv["x2"], dpgl, "tn", out_dtype=BF16, name="mm_dpleg").reshape(N_DEV, rows, D_MODEL)
    dx2 = _matmul(dpgl, wt["ple_gate"], "nt", res=dx3, name="mm_dx2")
    big["w_down"] = _matmul(sv["ff"], dx2, "tn", out_dtype=BF16, name="mm_ddown").reshape(N_DEV, FF_BLOCK, D_MODEL)
    dgate, dup = _swiglu_bwd(dx2, wt["w_down"], sv["gate"], sv["up"], name="swiglu_bwd", after=emit(0, big))
    big["w_gate"] = _matmul(dgate, sv["h2"], "tn", out_dtype=BF16, name="mm_dgate").reshape(N_DEV, FF_BLOCK, D_MODEL)
    big["w_up"] = _matmul(dup, sv["h2"], "tn", out_dtype=BF16, name="mm_dup").reshape(N_DEV, FF_BLOCK, D_MODEL)
    dh2 = _matmul(dgate, wt["w_gate"], "nn", name="mm_dh2_gate")
    dh2 = _matmul(dup, wt["w_up"], "nn", res=dh2, name="mm_dh2_up")
    dx1, gr["norm2_g"] = _rmsnorm_bwd(sv["x1"], wt["norm2_g"], dh2, dx2, name="rmsnorm_bwd")
    big["w_out"] = _matmul(sv["mixed"], dx1, "tn", out_dtype=BF16, name="mm_dout").reshape(N_DEV, rows, D_MODEL)
    dmixed = _matmul(dx1, wt["w_out"], "nt", name="mm_dmixed", after=emit(1, big))
    proj = sv["proj"]
    dcb, dcc, dch, dsconv = _sconv_bwd(proj, wt["sconv_w"], dmixed, name="sconv_bwd")
    big["sconv_w"] = _col_blocks(dsconv)
    dhp, dwbd, gr["pool_scale"] = _pool_bwd(proj, wt["pool_win"], wt["pool_wbd"], wt["pool_scale"], dmixed, name="pool_bwd")
    half = LANE // 2
    gr["pool_w"] = jnp.stack([dwbd[0, :half, :half], dwbd[0, half:, half:], dwbd[1, :half, :half], dwbd[1, half:, half:]])
    do, dz, gr["onorm_g"] = _apost_bwd(sv["o"], proj, wt["onorm_g"], dmixed, name="apost_bwd")
    dvn, dstates = _deltanet_bscan(sv["w"], sv["qg"], sv["kg"], sv["attn"], sv["g"], do, name="deltanet_bscan")
    dqkv_h, dg, dbeta = _deltanet_post(sv["qkv"], sv["g"], sv["beta"], sv["tmats"], sv["states"], dstates, do, dvn, sv["vn"],
                                       name="deltanet_post")
    dab, dalog, ddtb = _gates_bwd(proj, wt["a_log"], wt["dt_bias"], dg, dbeta, name="gates_bwd")
    gr["a_log"], gr["dt_bias"] = dalog[0, :HEADS], ddtb[0, :HEADS]
    dqkv, dconv = _qkv_prep_bwd(proj, wt["conv_qkv"], dqkv_h, name="qkv_prep_bwd")
    big["conv_qkv"] = _col_blocks(dconv)
    dproj = jnp.concatenate([dqkv, dz, dab, dhp, dcb, dcc, dch], axis=1)
    dwin = _matmul(dproj, sv["h1"], "tn", out_dtype=BF16, name="mm_din")
    big["w_in"] = jnp.concatenate([dwin[:AB_COL + 2 * HEADS], dwin[AB_COL + LANE:]], axis=0).reshape(N_DEV, -1, D_MODEL)
    dh1 = _matmul(dproj, wt["w_in"], "nn", name="mm_dh1", after=emit(2, big))
    dx, gr["norm1_g"] = _rmsnorm_bwd(sv["x"], wt["norm1_g"], dh1, dx1, name="rmsnorm_bwd")
    return dx, gr


FETCH_GROUPS = (("w_in", "conv_qkv", "sconv_w"), ("w_out",), ("w_gate", "w_up"), ("w_down",), ("ple_gate", "ple_proj"))
EMIT_GROUPS = (("ple_proj", "ple_gate", "w_down"), ("w_gate", "w_up", "w_out"), ("w_in", "conv_qkv", "sconv_w"))


def _small_weights(w, i):
    return dict(
        norm1_g=w["norm1_g"][i][None], norm2_g=w["norm2_g"][i][None], onorm_g=w["onorm_g"][i][None],
        a_log=_pad_lanes(w["a_log"][i]), dt_bias=_pad_lanes(w["dt_bias"][i]),
        pool_scale=w["pool_scale"][i][None], pool_win=_pool_windows(), pool_wbd=_block_diag_pairs(w["pool_w"][i]))


def _as_read(name, gathered):
    if name == "w_in":
        rows = gathered[:, :D_IN // N_DEV].reshape(-1, D_MODEL)
        return jnp.concatenate([rows[:AB_COL + 2 * HEADS], jnp.zeros((LANE - 2 * HEADS, D_MODEL), BF16),
                                rows[AB_COL + 2 * HEADS:]], axis=0)
    if name in ("conv_qkv", "sconv_w"):
        return _cols_joined(gathered)
    if name == "ple_proj":
        return gathered
    return gathered.reshape(-1, D_MODEL)


def _layer_weights(gathered, w, i):
    return {**_small_weights(w, i), **{k: _as_read(k, g) for k, g in gathered.items()}}


def _local_step(x, p, target, layers, final_g):
    saved = []
    h = x
    for i in range(DEPTH):
        replicated = {k: v for k, v in layers[i].items() if k not in SHARDED}
        h, sv = _layer_fwd(h, p[i], replicated, lambda group, after, i=i: {k: layers[i][k] for k in FETCH_GROUPS[group]})
        saved.append(sv)
    dx, dgf, loss = _loss_head(h, final_g, target, name="loss_head")
    big, small = [{} for _ in range(DEPTH)], [None] * DEPTH
    for i in reversed(range(DEPTH)):
        dx, small[i] = _layer_bwd(dx, saved[i], lambda group, blocks, i=i: big[i].update({k: blocks[k] for k in EMIT_GROUPS[group]}))
    return loss, dx, big, small, dgf


SHARDED = ("w_in", "w_gate", "w_up", "w_down", "w_out", "ple_gate", "ple_proj", "conv_qkv", "sconv_w")
SMALL = ("norm1_g", "a_log", "dt_bias", "onorm_g", "pool_w", "pool_scale", "norm2_g", "final_g")
SLAB_COLS = 1024


def _payload(name, shard):
    if name in ("conv_qkv", "sconv_w"):
        return shard
    out = shard.astype(BF16)
    if name in ("w_gate", "w_up", "w_down"):
        out = jnp.pad(out, ((0, FF_BLOCK - FF_SHARD), (0, 0)))
    if name == "w_in":
        out = jnp.pad(out, ((0, -out.shape[0] % (2 * SUBLANE)), (0, 0)))
    return out


TRANSPOSED = ("w_in", "w_gate", "w_up")


def _ff_rows(t):
    return jnp.transpose(t, (0, 2, 1))


def _slab_rows(shape):
    size = 1
    for s in shape:
        size *= s
    return SUBLANE * -(-size // (SUBLANE * SLAB_COLS))


def _pack_slab(parts, extra_row):
    rows = []
    for name in SMALL:
        flat = parts[name].reshape(-1)
        nrow = _slab_rows(parts[name].shape)
        rows.append(jnp.pad(flat, (0, nrow * SLAB_COLS - flat.shape[0])).reshape(nrow, SLAB_COLS))
    rows.append(jnp.pad(extra_row, ((0, SUBLANE - 1), (0, 0))))
    return jnp.concatenate(rows, axis=0)


def _unpack_slab(slab, shapes):
    out, row = {}, 0
    for name in SMALL:
        size = 1
        for s in shapes[name]:
            size *= s
        out[name] = slab[row:row + _slab_rows(shapes[name])].reshape(-1)[:size].reshape(shapes[name])
        row += _slab_rows(shapes[name])
    return out, row


def kernel(x, p, norm1_g, w_in, conv_qkv, a_log, dt_bias, onorm_g, pool_w, pool_scale, sconv_w, w_out, norm2_g, w_gate, w_up, w_down, ple_proj, ple_gate, final_g, loss_target, m_norm1_g, m_w_in, m_conv_qkv, m_a_log, m_dt_bias, m_onorm_g, m_pool_w, m_pool_scale, m_sconv_w, m_w_out, m_norm2_g, m_w_gate, m_w_up, m_w_down, m_ple_proj, m_ple_gate, m_final_g, v_norm1_g, v_w_in, v_conv_qkv, v_a_log, v_dt_bias, v_onorm_g, v_pool_w, v_pool_scale, v_sconv_w, v_w_out, v_norm2_g, v_w_gate, v_w_up, v_w_down, v_ple_proj, v_ple_gate, v_final_g):
    names = ["norm1_g", "w_in", "conv_qkv", "a_log", "dt_bias", "onorm_g", "pool_w", "pool_scale", "sconv_w", "w_out", "norm2_g",
             "w_gate", "w_up", "w_down", "ple_proj", "ple_gate", "final_g"]
    w = dict(zip(names, [norm1_g, w_in, conv_qkv, a_log, dt_bias, onorm_g, pool_w, pool_scale, sconv_w, w_out, norm2_g, w_gate, w_up,
                         w_down, ple_proj, ple_gate, final_g]))
    m = dict(zip(names, [m_norm1_g, m_w_in, m_conv_qkv, m_a_log, m_dt_bias, m_onorm_g, m_pool_w, m_pool_scale, m_sconv_w, m_w_out,
                         m_norm2_g, m_w_gate, m_w_up, m_w_down, m_ple_proj, m_ple_gate, m_final_g]))
    v = dict(zip(names, [v_norm1_g, v_w_in, v_conv_qkv, v_a_log, v_dt_bias, v_onorm_g, v_pool_w, v_pool_scale, v_sconv_w, v_w_out,
                         v_norm2_g, v_w_gate, v_w_up, v_w_down, v_ple_proj, v_ple_gate, v_final_g]))
    w.update({k: _ff_rows(w[k]) for k in TRANSPOSED})

    first, rest = FETCH_GROUPS[0], tuple(k for members in FETCH_GROUPS[1:] for k in members)
    gathered = dict(zip(first, _all_gather([_payload(k, w[k][0]) for k in first], name="all_gather_weights")))
    (flying0,), token = _exchange_start([[_payload(k, w[k][0]) for k in rest]], CHIP_GATHER, name="gather_start_0",
                                        after=gathered[first[0]])
    replicated = [_small_weights(w, i) for i in range(DEPTH)]
    replicated[0]["norm1_g"] = replicated[0]["norm1_g"] + token[0, 0]
    for group in (m, v):
        group.update({k: _ff_rows(group[k] + token[0, 0]) for k in TRANSPOSED})
    flying1 = []

    def fetch(i, group, after):
        if i == 0 and group == 1:
            landed = _exchange_wait(flying0, after, CHIP_GATHER, name="gather_wait_0")
            gathered.update(zip(rest, _pair_swap(landed, name="pair_swap")))
            started, token = _exchange_start([[_payload(k, w[k][1]) for k in SHARDED]], CHIP_GATHER, name="gather_start_1",
                                             after=gathered[rest[0]])
            flying1.extend(started)
            return {**{k: _as_read(k, gathered[k]) for k in FETCH_GROUPS[group]},
                    "onorm_g": replicated[0]["onorm_g"] + token[0, 0]}
        if i == 1 and group == 0:
            landed = _exchange_wait(flying1[0], after, CHIP_GATHER, name="gather_wait_1")
            gathered.update(zip(SHARDED, _pair_swap(landed, name="pair_swap")))
        return {k: _as_read(k, gathered[k]) for k in FETCH_GROUPS[group]}

    def reduce_scatter_start(members, blocks, tag):
        mine = [blocks[k] for k in members]
        theirs = _pair_exchange(mine, name="pair_exchange")
        sums = [_pair_add(a, b, name="pair_add") for a, b in zip(mine, theirs)]
        (started,), token = _exchange_start([sums], CHIP_SCATTER, name="exchange_start_" + tag)
        return started, token

    h, saved0 = _layer_fwd(x[0], p[0, 0], replicated[0], functools.partial(fetch, 0))
    h, saved1 = _layer_fwd(h, p[1, 0], replicated[1], functools.partial(fetch, 1))
    dx, dgf, loss_part = _loss_head(h, final_g[None], loss_target[0], name="loss_head")
    small, big1, flying0 = [None] * DEPTH, {}, []
    dx, small[1] = _layer_bwd(dx, saved1, lambda group, blocks: big1.update({k: blocks[k] for k in EMIT_GROUPS[group]}))
    flying1, token = reduce_scatter_start(SHARDED, big1, "1")

    def emit(group, blocks):
        started, token = reduce_scatter_start(EMIT_GROUPS[group], blocks, f"0_{group}")
        flying0.append(started)
        return token

    dx, small[0] = _layer_bwd(dx, saved0, emit, after=token)
    received = [{}, dict(zip(SHARDED, _exchange_wait(flying1, dx, CHIP_SCATTER, name="exchange_wait_1")))]
    for group, members in enumerate(EMIT_GROUPS):
        received[0].update(zip(members, _exchange_wait(flying0[group], dx, CHIP_SCATTER, name=f"exchange_wait_0_{group}")))

    grads = {k: jnp.stack([small[i][k] for i in range(DEPTH)]) for k in small[0]}
    grads = {k: g[:, 0] if k in ("norm1_g", "norm2_g", "onorm_g", "pool_scale") else g for k, g in grads.items()}
    grads["final_g"] = dgf[0]
    loss_row = jnp.pad(loss_part, ((0, 0), (0, SLAB_COLS - LANE)))
    (small_flying,), token = _exchange_start([[_pack_slab(grads, loss_row)]], GATHER, name="small_gather_start")

    out_g, out_d, out_m, out_v = {}, {}, {}, {}
    for k in SHARDED:
        out_g[k], out_d[k], out_m[k], out_v[k] = _adamw_reduce(w[k], [received[i][k] for i in range(DEPTH)], m[k], v[k],
                                                                name="adamw_" + k, after=token)
    behind_all = jnp.stack([out_v[k][0, 0, 0] for k in SHARDED])
    (small_parts,) = _exchange_wait(small_flying, behind_all, GATHER, name="small_gather_wait")
    zero_row = jnp.zeros((1, SLAB_COLS), F32)
    slabs = _adamw_reduce(_pack_slab(w, zero_row)[None], [small_parts], _pack_slab(m, zero_row)[None],
                          _pack_slab(v, zero_row)[None], name="adamw_small")
    slabs = [s[0] for s in slabs]
    shapes = {k: w[k].shape for k in SMALL}
    for dst, slab in zip((out_g, out_d, out_m, out_v), slabs):
        vals, _ = _unpack_slab(slab, shapes)
        dst.update(vals)
    _, loss_at = _unpack_slab(slabs[0], shapes)
    loss = slabs[0][loss_at, 0]
    for group in (out_g, out_d, out_m, out_v):
        group.update({k: _ff_rows(group[k]) for k in TRANSPOSED})

    return (loss, dx[None], *[out_g[k] for k in names], *[out_d[k] for k in names], *[out_m[k] for k in names],
            *[out_v[k] for k in names])
```

```python
import functools

import jax
import jax.numpy as jnp
from jax import lax
from jax.experimental import pallas as pl
from jax.experimental.pallas import tpu as pltpu

F32 = jnp.float32
BF16 = jnp.bfloat16

D_MODEL = 1024
DEPTH = 2
PLE_DIM = 256
EPS = 1e-6
HEAD_DIM = 128
HEADS = 4
A_DIM = HEADS * HEAD_DIM
QKV_TAPS = 4
CHUNK = 64
POOL_WINDOWS = (2, 4, 8, 16)
POOL_DIM = 256
CONV_DIM = 256
CONV_TAPS = 3
D_FF = 2816
D_IN = 3080
D_IN_PAD = 3200
AB_COL = 2048
N_DEV = 8

ADAM_LR = 0.001
ADAM_B1 = 0.9
ADAM_B2 = 0.999
ADAM_EPS = 1e-08
ADAM_WD = 0.01
ADAM_STEP = 10

LANE = 128
SUBLANE = 8
VMEM_BYTES_V7X = 64 * 1024 * 1024
VMEM_LIMIT = 48 * 1024 * 1024

_HI = lax.Precision.HIGHEST
NN = ((1,), (0,))
NT = ((1,), (1,))
TN = ((0,), (0,))
MESH = pl.DeviceIdType.MESH


def _dot(a, b, dims, hi=False):
    if hi:
        return lax.dot_general(a, b, (dims, ((), ())), precision=_HI, preferred_element_type=F32)
    return lax.dot_general(a.astype(BF16), b.astype(BF16), (dims, ((), ())), preferred_element_type=F32)


def _pcall(body, *, name, out_shape, grid=(), in_specs=None, out_specs=None, scratch_shapes=(), semantics=None,
           vmem_limit=None, after=None, **kw):
    params = {}
    if semantics is not None:
        params["dimension_semantics"] = semantics
    if vmem_limit is not None:
        params["vmem_limit_bytes"] = vmem_limit
    if after is not None:
        n_in, inner = len(in_specs), body
        body = lambda *refs: inner(*refs[:n_in], *refs[n_in + 1:])
        in_specs = list(in_specs) + [pl.BlockSpec(after.shape, lambda *_: (0,) * after.ndim)]
    call = pl.pallas_call(
        body, name=name, out_shape=out_shape, grid=grid, in_specs=in_specs, out_specs=out_specs,
        scratch_shapes=list(scratch_shapes), compiler_params=pltpu.CompilerParams(**params), **kw)
    return call if after is None else (lambda *args: call(*args, after))


def _sigmoid(x):
    return 1.0 / (1.0 + jnp.exp(-x))


def _softplus(x):
    return jnp.maximum(x, 0.0) + jnp.log(1.0 + jnp.exp(-jnp.abs(x)))


def _tile(n, cap, mult):
    if n <= cap:
        return n
    best = None
    for t in range(mult, cap + 1, mult):
        if n % t == 0:
            best = t
    assert best is not None, (n, cap, mult)
    return best


ROWS_PER_STEP = 512
NARROW_RESULT = 1024
COLS_PER_DOT = 640


def _resident(weight):
    return pl.BlockSpec(weight.shape, lambda i: (0,) * weight.ndim, pipeline_mode=pl.Buffered(1))


def _matmul_rows(a, b, mode, *, name, res=None, out_dtype=F32, b_blocked=False, after=None, norm_g=None):
    m, k = a.shape
    if b_blocked:
        nb, _, bw = b.shape
        n = nb * bw if mode == "nn" else b.shape[1]
    else:
        n = b.shape[1] if mode == "nn" else b.shape[0]
    tm = _tile(m, ROWS_PER_STEP if n > NARROW_RESULT else 2 * ROWS_PER_STEP, 16)
    cn = bw if (b_blocked and mode == "nn") else _tile(n, COLS_PER_DOT, LANE)
    has_res = res is not None
    normed = norm_g is not None

    def body(*refs):
        a_ref, b_ref = refs[0], refs[1]
        g_ref = refs[2] if normed else None
        res_ref = refs[2 + normed] if has_res else None
        o_ref = refs[2 + normed + has_res]
        if normed:
            av = _rms_normed(a_ref[...], g_ref[...])
            refs[3 + normed + has_res][...] = av
        elif not (b_blocked and mode == "nt"):
            av = a_ref[...].astype(BF16)
        for j in range(n // cn):
            cols = pl.ds(j * cn, cn)
            if mode == "nn":
                part = _dot(av, b_ref[j] if b_blocked else b_ref[:, cols], NN)
            elif not b_blocked:
                part = _dot(av, b_ref[cols, :], NT)
            else:
                part = None
                for s in range(nb):
                    term = _dot(a_ref[:, pl.ds(s * bw, bw)], b_ref[s, cols, :], NT)
                    part = term if part is None else part + term
            if has_res:
                part = part + res_ref[:, cols]
            o_ref[:, cols] = part.astype(o_ref.dtype)

    row = lambda width: pl.BlockSpec((tm, width), lambda i: (i, 0))
    whole = _resident(b)
    ins = [a, b] + ([norm_g] if normed else []) + ([res] if has_res else [])
    specs = [row(k), whole] + ([pl.BlockSpec((1, k), lambda i: (0, 0))] if normed else []) + ([row(n)] if has_res else [])
    out = jax.ShapeDtypeStruct((m, n), out_dtype)
    return _pcall(body, name=name, out_shape=(out, jax.ShapeDtypeStruct((m, k), BF16)) if normed else out, grid=(m // tm,),
                  in_specs=specs, out_specs=(row(n), row(k)) if normed else row(n), semantics=("parallel",),
                  vmem_limit=VMEM_LIMIT, after=after)(*ins)


def _rms_normed(xv, gv):
    return (xv * lax.rsqrt(jnp.mean(xv * xv, axis=-1, keepdims=True) + EPS) * gv).astype(BF16)


def _matmul(a, b, mode, *, name, res=None, out_dtype=F32, b_blocked=False, out_blocked=None, after=None, norm_g=None):
    if mode != "tn":
        return _matmul_rows(a, b, mode, name=name, res=res, out_dtype=out_dtype, b_blocked=b_blocked, after=after, norm_g=norm_g)
    assert res is None and not b_blocked and after is None and norm_g is None
    (t, m), (t2, n) = a.shape, b.shape
    assert t == t2, (a.shape, b.shape)
    tm = _tile(m, 1024, LANE)
    tn = _tile(n, COLS_PER_DOT, LANE)
    if out_blocked is not None:
        assert out_blocked[0] * out_blocked[1] == n
        tn = out_blocked[1]

    def body(a_ref, b_ref, o_ref):
        part = _dot(a_ref[...], b_ref[...], TN).astype(o_ref.dtype)
        if out_blocked is None:
            o_ref[...] = part
        else:
            o_ref[0] = part

    o_spec = (pl.BlockSpec((tm, tn), lambda i, j: (i, j)) if out_blocked is None
              else pl.BlockSpec((1, tm, tn), lambda i, j: (j, i, 0)))
    o_shape = (m, n) if out_blocked is None else (out_blocked[0], m, out_blocked[1])
    return _pcall(body, name=name, out_shape=jax.ShapeDtypeStruct(o_shape, out_dtype), grid=(m // tm, n // tn),
                  in_specs=[pl.BlockSpec((t, tm), lambda i, j: (0, i)), pl.BlockSpec((t, tn), lambda i, j: (0, j))],
                  out_specs=o_spec, semantics=("parallel", "parallel"), vmem_limit=VMEM_LIMIT)(a, b)


ROW_TILE = 512


def _rows(t, width, idx=0):
    return pl.BlockSpec((ROW_TILE, width), lambda i: (i, idx))


def _vec(width):
    return pl.BlockSpec((1, width), lambda i: (0, 0))


def _rmsnorm_bwd(x, g, dh, dres, *, name):
    t, d = x.shape

    def body(x_ref, g_ref, dh_ref, dres_ref, dx_ref, dg_ref):
        xv = x_ref[...]
        r = lax.rsqrt(jnp.mean(xv * xv, axis=-1, keepdims=True) + EPS)
        xhat = xv * r
        dhv = dh_ref[...].astype(F32)
        dhg = dhv * g_ref[...]
        dx_ref[...] = dres_ref[...] + r * (dhg - xhat * jnp.mean(dhg * xhat, axis=-1, keepdims=True))
        part = jnp.sum(dhv * xhat, axis=0, keepdims=True)

        @pl.when(pl.program_id(0) == 0)
        def _():
            dg_ref[...] = part

        @pl.when(pl.program_id(0) > 0)
        def _():
            dg_ref[...] += part

    return _pcall(body, name=name, out_shape=(jax.ShapeDtypeStruct((t, d), F32), jax.ShapeDtypeStruct((1, d), F32)),
                  grid=(t // ROW_TILE,), in_specs=[_rows(t, d), _vec(d), _rows(t, d), _rows(t, d)],
                  out_specs=(_rows(t, d), _vec(d)), semantics=("arbitrary",))(x, g, dh, dres)


def _swiglu_fwd(x, norm_g, w_gate, w_up, *, name):
    t, k = x.shape
    f = w_gate.shape[0]
    tm = _tile(t, ROWS_PER_STEP, 16)
    cn = _tile(f, COLS_PER_DOT, LANE)

    def body(x_ref, g_ref, wg_ref, wu_ref, ff_ref, gate_ref, up_ref, h_ref):
        hv = _rms_normed(x_ref[...], g_ref[...])
        h_ref[...] = hv
        for j in range(f // cn):
            cols = pl.ds(j * cn, cn)
            gv = _dot(hv, wg_ref[cols, :], NT)
            uv = _dot(hv, wu_ref[cols, :], NT)
            gate_ref[:, cols] = gv.astype(BF16)
            up_ref[:, cols] = uv.astype(BF16)
            ff_ref[:, cols] = (gv * _sigmoid(gv) * uv).astype(BF16)

    row = lambda width: pl.BlockSpec((tm, width), lambda i: (i, 0))
    out = jax.ShapeDtypeStruct((t, f), BF16)
    return _pcall(body, name=name, out_shape=(out,) * 3 + (jax.ShapeDtypeStruct((t, k), BF16),), grid=(t // tm,),
                  in_specs=[row(k), pl.BlockSpec((1, k), lambda i: (0, 0)), _resident(w_gate), _resident(w_up)],
                  out_specs=(row(f),) * 3 + (row(k),), semantics=("parallel",), vmem_limit=VMEM_LIMIT)(x, norm_g, w_gate, w_up)


def _swiglu_bwd(dx2, w_down, gate, up, *, name, after=None):
    t, d = dx2.shape
    f = w_down.shape[0]
    tm = _tile(t, ROWS_PER_STEP, 16)
    cn = _tile(f, COLS_PER_DOT, LANE)

    def body(dx_ref, w_ref, gate_ref, up_ref, dgate_ref, dup_ref):
        dxv = dx_ref[...].astype(BF16)
        for j in range(f // cn):
            cols = pl.ds(j * cn, cn)
            dffv = _dot(dxv, w_ref[cols, :], NT)
            gv = gate_ref[:, cols].astype(F32)
            sig = _sigmoid(gv)
            dgate_ref[:, cols] = (dffv * up_ref[:, cols].astype(F32) * sig * (1.0 + gv * (1.0 - sig))).astype(BF16)
            dup_ref[:, cols] = (dffv * gv * sig).astype(BF16)

    row = lambda width: pl.BlockSpec((tm, width), lambda i: (i, 0))
    out = jax.ShapeDtypeStruct((t, f), BF16)
    return _pcall(body, name=name, out_shape=(out, out), grid=(t // tm,), in_specs=[row(d), _resident(w_down), row(f), row(f)],
                  out_specs=(row(f), row(f)), semantics=("parallel",), vmem_limit=VMEM_LIMIT, after=after)(dx2, w_down, gate, up)


def _ple_fwd(x2, pgl, pp, *, name):
    t, d = x2.shape

    def body(x_ref, pgl_ref, pp_ref, o_ref):
        o_ref[...] = x_ref[...] + _sigmoid(pgl_ref[...]) * pp_ref[...]

    return _pcall(body, name=name, out_shape=jax.ShapeDtypeStruct((t, d), F32), grid=(t // ROW_TILE,),
                  in_specs=[_rows(t, d)] * 3, out_specs=_rows(t, d), semantics=("parallel",))(x2, pgl, pp)


def _ple_bwd(dx3, pgl, pp, *, name, after=None):
    t, d = dx3.shape

    def body(dx_ref, pgl_ref, pp_ref, dpgl_ref, dpp_ref):
        dxv = dx_ref[...]
        sig = _sigmoid(pgl_ref[...])
        dpp_ref[...] = (dxv * sig).astype(BF16)
        dpgl_ref[...] = (dxv * pp_ref[...] * sig * (1.0 - sig)).astype(BF16)

    return _pcall(body, name=name, out_shape=(jax.ShapeDtypeStruct((t, d), BF16),) * 2, grid=(t // ROW_TILE,),
                  in_specs=[_rows(t, d)] * 3, out_specs=(_rows(t, d),) * 2, semantics=("parallel",), after=after)(dx3, pgl, pp)


def _loss_head(x3, g, target, *, name):
    t, d = x3.shape

    def body(x_ref, g_ref, t_ref, dx_ref, dg_ref, loss_ref):
        xv = x_ref[...]
        r = lax.rsqrt(jnp.mean(xv * xv, axis=-1, keepdims=True) + EPS)
        xhat = xv * r
        gv = g_ref[...]
        err = xhat * gv - t_ref[...]
        row_loss = jnp.sum(err * err, axis=-1, keepdims=True) * (0.5 / d)
        lpart = jnp.broadcast_to(jnp.sum(row_loss, axis=0, keepdims=True), (1, LANE))
        dy = err * (1.0 / d)
        dyg = dy * gv
        dx_ref[...] = r * (dyg - xhat * jnp.mean(dyg * xhat, axis=-1, keepdims=True))
        gpart = jnp.sum(dy * xhat, axis=0, keepdims=True)

        @pl.when(pl.program_id(0) == 0)
        def _():
            dg_ref[...] = gpart
            loss_ref[...] = lpart

        @pl.when(pl.program_id(0) > 0)
        def _():
            dg_ref[...] += gpart
            loss_ref[...] += lpart

    return _pcall(body, name=name,
                  out_shape=(jax.ShapeDtypeStruct((t, d), F32), jax.ShapeDtypeStruct((1, d), F32), jax.ShapeDtypeStruct((1, LANE), F32)),
                  grid=(t // ROW_TILE,), in_specs=[_rows(t, d), _vec(d), _rows(t, d)],
                  out_specs=(_rows(t, d), _vec(d), _vec(LANE)), semantics=("arbitrary",))(x3, g, target)


def _shift_down(x, d):
    if d == 0:
        return x
    row = lax.broadcasted_iota(jnp.int32, x.shape, 0)
    return jnp.where(row >= d, pltpu.roll(x, d, 0), 0.0)


def _shift_up(x, d):
    if d == 0:
        return x
    t = x.shape[0]
    row = lax.broadcasted_iota(jnp.int32, x.shape, 0)
    return jnp.where(row < t - d, pltpu.roll(x, t - d, 0), 0.0)


def _colsum(x):
    return jnp.sum(x, axis=0, keepdims=True)


def _col(t, idx_fn):
    return pl.BlockSpec((t, LANE), idx_fn)


def _conv_fwd(x, w_ref, taps):
    acc = None
    for j in range(taps):
        term = w_ref[pl.ds(j, 1), :] * _shift_down(x, taps - 1 - j)
        acc = term if acc is None else acc + term
    return acc


def _conv_bwd(x, dy, w_ref, dw_ref, taps):
    dx = None
    for j in range(taps):
        term = w_ref[pl.ds(j, 1), :] * _shift_up(dy, taps - 1 - j)
        dx = term if dx is None else dx + term
        dw_ref[pl.ds(j, 1), :] = _colsum(dy * _shift_down(x, taps - 1 - j))
    return dx


def _qkv_prep_fwd(proj, conv_w, *, name):
    t = proj.shape[0]
    scale = HEAD_DIM ** -0.5

    def body(x_ref, w_ref, o_ref):
        j = pl.program_id(0)
        c = _conv_fwd(x_ref[...], w_ref, QKV_TAPS)
        s = c * _sigmoid(c)
        r = lax.rsqrt(jnp.sum(s * s, axis=-1, keepdims=True) + EPS)
        f = jnp.where(j < 2 * HEADS, r, 1.0) * jnp.where(j < HEADS, scale, 1.0)
        o_ref[0] = s * f

    return _pcall(body, name=name, out_shape=jax.ShapeDtypeStruct((3 * HEADS, t, LANE), F32), grid=(3 * HEADS,),
                  in_specs=[_col(t, lambda j: (0, j)), pl.BlockSpec((QKV_TAPS, LANE), lambda j: (0, j))],
                  out_specs=pl.BlockSpec((1, t, LANE), lambda j: (j, 0, 0)), semantics=("parallel",),
                  vmem_limit=VMEM_LIMIT)(proj, conv_w)


def _qkv_prep_bwd(proj, conv_w, dqkv, *, name):
    t = proj.shape[0]
    scale = HEAD_DIM ** -0.5

    def body(x_ref, w_ref, d_ref, dx_ref, dw_ref):
        j = pl.program_id(0)
        xv = x_ref[...]
        c = _conv_fwd(xv, w_ref, QKV_TAPS)
        sig = _sigmoid(c)
        s = c * sig
        r = lax.rsqrt(jnp.sum(s * s, axis=-1, keepdims=True) + EPS)
        n0 = s * r
        dv = d_ref[0]
        dn0 = dv * jnp.where(j < HEADS, scale, 1.0)
        ds_norm = r * (dn0 - n0 * jnp.sum(dn0 * n0, axis=-1, keepdims=True))
        ds = jnp.where(j < 2 * HEADS, ds_norm, dv)
        dc = ds * sig * (1.0 + c * (1.0 - sig))
        dx_ref[...] = _conv_bwd(xv, dc, w_ref, dw_ref, QKV_TAPS).astype(BF16)

    return _pcall(body, name=name,
                  out_shape=(jax.ShapeDtypeStruct((t, 3 * A_DIM), BF16), jax.ShapeDtypeStruct((QKV_TAPS, 3 * A_DIM), F32)),
                  grid=(3 * HEADS,),
                  in_specs=[_col(t, lambda j: (0, j)), pl.BlockSpec((QKV_TAPS, LANE), lambda j: (0, j)),
                            pl.BlockSpec((1, t, LANE), lambda j: (j, 0, 0))],
                  out_specs=(_col(t, lambda j: (0, j)), pl.BlockSpec((QKV_TAPS, LANE), lambda j: (0, j))),
                  semantics=("parallel",), vmem_limit=VMEM_LIMIT)(proj, conv_w, dqkv)


def _lane_pick(x, lane_idx, lane):
    return jnp.broadcast_to(jnp.sum(jnp.where(lane == lane_idx, x, 0.0), axis=-1, keepdims=True), x.shape)


def _gates_fwd(proj, alog, dtb, *, name):
    t = proj.shape[0]

    def body(x_ref, alog_ref, dtb_ref, g_ref, b_ref):
        xv = x_ref[...]
        lane = lax.broadcasted_iota(jnp.int32, xv.shape, 1)
        gall = -jnp.exp(alog_ref[...]) * _softplus(xv + dtb_ref[...])
        ball = _sigmoid(xv)
        for h in range(HEADS):
            g_ref[h] = _lane_pick(gall, h, lane)
            b_ref[h] = _lane_pick(ball, HEADS + h, lane)

    out = jax.ShapeDtypeStruct((HEADS, t, LANE), F32)
    whole = pl.BlockSpec((HEADS, t, LANE), lambda i: (0, 0, 0))
    return _pcall(body, name=name, out_shape=(out, out), grid=(1,),
                  in_specs=[_col(t, lambda i: (0, AB_COL // LANE)), _vec(LANE), _vec(LANE)], out_specs=(whole, whole),
                  semantics=("arbitrary",), vmem_limit=VMEM_LIMIT)(proj, alog, dtb)


def _gates_bwd(proj, alog, dtb, dg, dbeta, *, name):
    t = proj.shape[0]

    def body(x_ref, alog_ref, dtb_ref, dg_ref, db_ref, dab_ref, dalog_ref, ddtb_ref):
        xv = x_ref[...]
        lane = lax.broadcasted_iota(jnp.int32, xv.shape, 1)
        lane1 = lax.broadcasted_iota(jnp.int32, (1, LANE), 1)
        z = xv + dtb_ref[...]
        nea = -jnp.exp(alog_ref[...])
        da_f = nea * _sigmoid(z)
        g_f = nea * _softplus(z)
        ball = _sigmoid(xv)
        db_f = ball * (1.0 - ball)
        dab = jnp.zeros_like(xv)
        dalog = jnp.zeros((1, LANE), F32)
        for h in range(HEADS):
            dgh = dg_ref[h]
            dab = dab + jnp.where(lane == h, dgh * da_f, 0.0) + jnp.where(lane == HEADS + h, db_ref[h] * db_f, 0.0)
            dalog = dalog + jnp.where(lane1 == h, _colsum(dgh * g_f), 0.0)
        dab_ref[...] = dab.astype(BF16)
        dalog_ref[...] = dalog
        ddtb_ref[...] = jnp.where(lane1 < HEADS, _colsum(dab), 0.0)

    whole = pl.BlockSpec((HEADS, t, LANE), lambda i: (0, 0, 0))
    vec = jax.ShapeDtypeStruct((1, LANE), F32)
    return _pcall(body, name=name, out_shape=(jax.ShapeDtypeStruct((t, LANE), BF16), vec, vec), grid=(1,),
                  in_specs=[_col(t, lambda i: (0, AB_COL // LANE)), _vec(LANE), _vec(LANE), whole, whole],
                  out_specs=(_col(t, lambda i: (0, 0)), _vec(LANE), _vec(LANE)), semantics=("arbitrary",),
                  vmem_limit=VMEM_LIMIT)(proj, alog, dtb, dg, dbeta)


Z_COL = 3 * A_DIM // LANE


def _apost_fwd(o, proj, gn, *, name):
    t = proj.shape[0]

    def body(o_ref, z_ref, gn_ref, y_ref):
        ov = o_ref[0]
        z = z_ref[...]
        r = lax.rsqrt(jnp.mean(ov * ov, axis=-1, keepdims=True) + EPS)
        y_ref[...] = (ov * r * gn_ref[...] * (z * _sigmoid(z))).astype(BF16)

    return _pcall(body, name=name, out_shape=jax.ShapeDtypeStruct((t, A_DIM), BF16), grid=(HEADS,),
                  in_specs=[pl.BlockSpec((1, t, LANE), lambda h: (h, 0, 0)), _col(t, lambda h: (0, Z_COL + h)),
                            pl.BlockSpec((1, LANE), lambda h: (0, 0))],
                  out_specs=_col(t, lambda h: (0, h)), semantics=("parallel",), vmem_limit=VMEM_LIMIT)(o, proj, gn)


def _apost_bwd(o, proj, gn, dmixed, *, name):
    t = proj.shape[0]

    def body(o_ref, z_ref, gn_ref, d_ref, do_ref, dz_ref, dgn_ref):
        ov = o_ref[0]
        z = z_ref[...]
        gnv = gn_ref[...]
        dv = d_ref[...]
        r = lax.rsqrt(jnp.mean(ov * ov, axis=-1, keepdims=True) + EPS)
        ohat = ov * r
        sig = _sigmoid(z)
        dy = dv * (z * sig)
        dz_ref[...] = (dv * ohat * gnv * sig * (1.0 + z * (1.0 - sig))).astype(BF16)
        dyo = dy * gnv
        do_ref[0] = r * (dyo - ohat * jnp.mean(dyo * ohat, axis=-1, keepdims=True))
        part = _colsum(dy * ohat)

        @pl.when(pl.program_id(0) == 0)
        def _():
            dgn_ref[...] = part

        @pl.when(pl.program_id(0) > 0)
        def _():
            dgn_ref[...] += part

    return _pcall(body, name=name,
                  out_shape=(jax.ShapeDtypeStruct((HEADS, t, LANE), F32), jax.ShapeDtypeStruct((t, A_DIM), BF16),
                             jax.ShapeDtypeStruct((1, LANE), F32)),
                  grid=(HEADS,),
                  in_specs=[pl.BlockSpec((1, t, LANE), lambda h: (h, 0, 0)), _col(t, lambda h: (0, Z_COL + h)),
                            pl.BlockSpec((1, LANE), lambda h: (0, 0)), _col(t, lambda h: (0, h))],
                  out_specs=(pl.BlockSpec((1, t, LANE), lambda h: (h, 0, 0)), _col(t, lambda h: (0, h)),
                             pl.BlockSpec((1, LANE), lambda h: (0, 0))),
                  semantics=("arbitrary",), vmem_limit=VMEM_LIMIT)(o, proj, gn, dmixed)


POOL_COL = (AB_COL + LANE) // LANE
CB_COL = POOL_COL + POOL_DIM // LANE
CC_COL = CB_COL + CONV_DIM // LANE
CH_COL = CC_COL + CONV_DIM // LANE
MAX_WIN_LOG2 = 4


def _window_sums(x, shift):
    sums = []
    cur = x
    for k in range(MAX_WIN_LOG2):
        cur = cur + shift(cur, 1 << k)
        sums.append(cur)
    return sums


def _pick_window(sums, win):
    out = sums[-1]
    for k in range(MAX_WIN_LOG2 - 2, -1, -1):
        out = jnp.where(win == float(2 << k), sums[k], out)
    return out


def _pool_counts(shape, win):
    row = lax.broadcasted_iota(jnp.int32, shape, 0).astype(F32)
    return jnp.minimum(row + 1.0, win)


def _pool_fwd(proj, win, wbd, scale, *, name):
    t = proj.shape[0]

    def body(x_ref, win_ref, w_ref, s_ref, y_ref):
        xv = x_ref[...]
        winv = win_ref[...]
        pooled = _pick_window(_window_sums(xv, _shift_down), winv) / _pool_counts(xv.shape, winv) - xv
        y_ref[...] = (_dot(pooled, w_ref[0], NN) * s_ref[...]).astype(BF16)

    nb = POOL_DIM // LANE
    vec = pl.BlockSpec((1, LANE), lambda b: (0, b))
    return _pcall(body, name=name, out_shape=jax.ShapeDtypeStruct((t, POOL_DIM), BF16), grid=(nb,),
                  in_specs=[_col(t, lambda b: (0, POOL_COL + b)), vec, pl.BlockSpec((1, LANE, LANE), lambda b: (b, 0, 0)), vec],
                  out_specs=_col(t, lambda b: (0, b)), semantics=("parallel",), vmem_limit=VMEM_LIMIT)(proj, win, wbd, scale)


def _pool_bwd(proj, win, wbd, scale, dmixed, *, name):
    t = proj.shape[0]

    def body(x_ref, win_ref, w_ref, s_ref, d_ref, dx_ref, dw_ref, ds_ref):
        xv = x_ref[...]
        winv = win_ref[...]
        cnt = _pool_counts(xv.shape, winv)
        pooled = _pick_window(_window_sums(xv, _shift_down), winv) / cnt - xv
        dv = d_ref[...]
        ds_ref[...] = _colsum(dv * _dot(pooled, w_ref[0], NN))
        dy0 = dv * s_ref[...]
        dw_ref[0] = _dot(pooled, dy0, TN)
        dpooled = _dot(dy0, w_ref[0], NT)
        dmean = dpooled / cnt
        dx_ref[...] = (_pick_window(_window_sums(dmean, _shift_up), winv) - dpooled).astype(BF16)

    nb = POOL_DIM // LANE
    vec = pl.BlockSpec((1, LANE), lambda b: (0, b))
    mat = pl.BlockSpec((1, LANE, LANE), lambda b: (b, 0, 0))
    first = A_DIM // LANE
    return _pcall(body, name=name,
                  out_shape=(jax.ShapeDtypeStruct((t, POOL_DIM), BF16), jax.ShapeDtypeStruct((nb, LANE, LANE), F32),
                             jax.ShapeDtypeStruct((1, POOL_DIM), F32)),
                  grid=(nb,),
                  in_specs=[_col(t, lambda b: (0, POOL_COL + b)), vec, mat, vec, _col(t, lambda b: (0, first + b))],
                  out_specs=(_col(t, lambda b: (0, b)), mat, vec), semantics=("parallel",),
                  vmem_limit=VMEM_LIMIT)(proj, win, wbd, scale, dmixed)


def _sconv_fwd(proj, w, *, name):
    t = proj.shape[0]

    def body(cb_ref, cc_ref, ch_ref, w_ref, y_ref):
        y_ref[...] = (cb_ref[...] * _conv_fwd(cc_ref[...] * ch_ref[...], w_ref, CONV_TAPS)).astype(BF16)

    nb = CONV_DIM // LANE
    return _pcall(body, name=name, out_shape=jax.ShapeDtypeStruct((t, CONV_DIM), BF16), grid=(nb,),
                  in_specs=[_col(t, lambda b: (0, CB_COL + b)), _col(t, lambda b: (0, CC_COL + b)),
                            _col(t, lambda b: (0, CH_COL + b)), pl.BlockSpec((CONV_TAPS, LANE), lambda b: (0, b))],
                  out_specs=_col(t, lambda b: (0, b)), semantics=("parallel",), vmem_limit=VMEM_LIMIT)(proj, proj, proj, w)


def _sconv_bwd(proj, w, dmixed, *, name):
    t = proj.shape[0]

    def body(cb_ref, cc_ref, ch_ref, w_ref, d_ref, dcb_ref, dcc_ref, dch_ref, dw_ref):
        cc = cc_ref[...]
        ch = ch_ref[...]
        u = cc * ch
        dv = d_ref[...]
        dcb_ref[...] = (dv * _conv_fwd(u, w_ref, CONV_TAPS)).astype(BF16)
        du = _conv_bwd(u, dv * cb_ref[...], w_ref, dw_ref, CONV_TAPS)
        dcc_ref[...] = (du * ch).astype(BF16)
        dch_ref[...] = (du * cc).astype(BF16)

    nb = CONV_DIM // LANE
    first = (A_DIM + POOL_DIM) // LANE
    act = jax.ShapeDtypeStruct((t, CONV_DIM), BF16)
    wspec = pl.BlockSpec((CONV_TAPS, LANE), lambda b: (0, b))
    ospec = _col(t, lambda b: (0, b))
    return _pcall(body, name=name, out_shape=(act, act, act, jax.ShapeDtypeStruct((CONV_TAPS, CONV_DIM), F32)), grid=(nb,),
                  in_specs=[_col(t, lambda b: (0, CB_COL + b)), _col(t, lambda b: (0, CC_COL + b)),
                            _col(t, lambda b: (0, CH_COL + b)), wspec, _col(t, lambda b: (0, first + b))],
                  out_specs=(ospec, ospec, ospec, wspec), semantics=("parallel",),
                  vmem_limit=VMEM_LIMIT)(proj, proj, proj, w, dmixed)


def _chunk_masks():
    r = lax.broadcasted_iota(jnp.int32, (CHUNK, CHUNK), 0)
    c = lax.broadcasted_iota(jnp.int32, (CHUNK, CHUNK), 1)
    return r >= c, r > c, jnp.where(r == c, 1.0, 0.0).astype(F32)


def _split(a):
    hi = a.astype(BF16)
    return hi, (a - hi.astype(F32)).astype(BF16)


def _dot_split(a, b, dims):
    (ah, al), (bh, bl) = a, b
    return _dot(ah, bh, dims) + _dot(ah, bl, dims) + _dot(al, bh, dims)


def _tri_inv(lows, eye):
    xs = [eye - low for low in lows]
    ps = [_split(low) for low in lows]
    ps = [_split(_dot_split(p, p, NN)) for p in ps]
    for i in range(5):
        xs = [x + _dot_split(_split(x), p, NN) for x, p in zip(xs, ps)]
        if i < 4:
            ps = [_split(_dot_split(p, p, NN)) for p in ps]
    return xs


def _prefix_sum_rows(x):
    for k in range(6):
        x = x + _shift_down(x, 1 << k)
    return x


def _suffix_sum_rows(x):
    for k in range(6):
        x = x + _shift_up(x, 1 << k)
    return x


def _chunk_decay(g, incl):
    gcb = _prefix_sum_rows(g)
    gtot = _colsum(g)
    col = gcb[:, :CHUNK]
    row = gcb.T[:CHUNK, :]
    decay = jnp.exp(jnp.where(incl, col - row, -1e30))
    return gcb, gtot, decay


CHUNKS_PER_STEP = 4


def _heads_of(ref, base, rows):
    return [ref[base + h, rows, :] for h in range(HEADS)]


def _chunk_rows(j):
    return pl.ds(j * CHUNK, CHUNK)


def _deltanet_prep(qkv, g, beta, *, name):
    t = qkv.shape[1]
    n_chunks = t // CHUNK
    per = CHUNKS_PER_STEP
    probs = [(j, h) for j in range(per) for h in range(HEADS)]

    def body(qkv_ref, g_ref, b_ref, u_ref, w_ref, qg_ref, kg_ref, attn_ref, tm_ref):
        incl, strict, eye = _chunk_masks()
        q = [qkv_ref[h, _chunk_rows(j), :] for j, h in probs]
        k = [qkv_ref[HEADS + h, _chunk_rows(j), :] for j, h in probs]
        v = [qkv_ref[2 * HEADS + h, _chunk_rows(j), :] for j, h in probs]
        bv = [b_ref[h, _chunk_rows(j), :] for j, h in probs]
        dec = [_chunk_decay(g_ref[h, _chunk_rows(j), :], incl) for j, h in probs]
        kb = [a * b for a, b in zip(k, bv)]
        low = [jnp.where(strict, _dot(a, b, NT) * d[2], 0.0) for a, b, d in zip(kb, k, dec)]
        tm = _tri_inv(low, eye)
        egc = [jnp.exp(d[0]) for d in dec]
        u = [_dot(m, a * b, NN) for m, a, b in zip(tm, v, bv)]
        w = [_dot(m, a * e, NN) for m, a, e in zip(tm, kb, egc)]
        attn = [_dot(a, b, NT) * d[2] for a, b, d in zip(q, k, dec)]
        for i, (j, h) in enumerate(probs):
            rows = _chunk_rows(j)
            u_ref[h, rows, :] = u[i]
            w_ref[h, rows, :] = w[i].astype(BF16)
            qg_ref[h, rows, :] = (q[i] * egc[i]).astype(BF16)
            kg_ref[h, rows, :] = (k[i] * jnp.exp(dec[i][1] - dec[i][0])).astype(BF16)
            attn_ref[j, h] = attn[i].astype(BF16)
            tm_ref[j, h] = tm[i]

    act = lambda heads: pl.BlockSpec((heads, per * CHUNK, LANE), lambda n: (0, n, 0))
    mat = pl.BlockSpec((per, HEADS, CHUNK, CHUNK), lambda n: (n, 0, 0, 0))
    return _pcall(
        body, name=name,
        out_shape=(jax.ShapeDtypeStruct((HEADS, t, LANE), F32),) + (jax.ShapeDtypeStruct((HEADS, t, LANE), BF16),) * 3
        + (jax.ShapeDtypeStruct((n_chunks, HEADS, CHUNK, CHUNK), BF16), jax.ShapeDtypeStruct((n_chunks, HEADS, CHUNK, CHUNK), F32)),
        grid=(n_chunks // per,), in_specs=[act(3 * HEADS), act(HEADS), act(HEADS)],
        out_specs=(act(HEADS),) * 4 + (mat, mat), semantics=("parallel",), vmem_limit=VMEM_LIMIT)(qkv, g, beta)


SCAN_CHUNKS_PER_STEP = 8


def _deltanet_scan(u, w, qg, kg, attn, g, *, name):
    t = u.shape[1]
    n_chunks = t // CHUNK
    per = SCAN_CHUNKS_PER_STEP

    def body(u_ref, w_ref, qg_ref, kg_ref, attn_ref, g_ref, o_ref, vn_ref, st_ref, s_ref):
        @pl.when(pl.program_id(0) == 0)
        def _():
            s_ref[...] = jnp.zeros_like(s_ref)

        for j in range(per):
            rows = _chunk_rows(j)
            s = [s_ref[h] for h in range(HEADS)]
            vn = [u_ref[h, rows, :] - _dot(w_ref[h, rows, :], s[h], NN) for h in range(HEADS)]
            o = [_dot(qg_ref[h, rows, :], s[h], NN) + _dot(attn_ref[j, h], vn[h], NN) for h in range(HEADS)]
            eg = [jnp.exp(_colsum(g_ref[h, rows, :])) for h in range(HEADS)]
            for h in range(HEADS):
                st_ref[j, h] = s[h]
                s_ref[h] = s[h] * eg[h] + _dot(kg_ref[h, rows, :], vn[h], TN)
                o_ref[h, rows, :] = o[h]
                vn_ref[h, rows, :] = vn[h]

    act = pl.BlockSpec((HEADS, per * CHUNK, LANE), lambda n: (0, n, 0))
    out = jax.ShapeDtypeStruct((HEADS, t, LANE), F32)
    return _pcall(
        body, name=name, out_shape=(out, out, jax.ShapeDtypeStruct((n_chunks, HEADS, LANE, LANE), F32)), grid=(n_chunks // per,),
        in_specs=[act] * 4 + [pl.BlockSpec((per, HEADS, CHUNK, CHUNK), lambda n: (n, 0, 0, 0)), act],
        out_specs=(act, act, pl.BlockSpec((per, HEADS, LANE, LANE), lambda n: (n, 0, 0, 0))),
        scratch_shapes=[pltpu.VMEM((HEADS, LANE, LANE), F32)], semantics=("arbitrary",))(u, w, qg, kg, attn, g)


def _deltanet_bscan(w, qg, kg, attn, g, do, *, name):
    t = w.shape[1]
    n_chunks = t // CHUNK
    per = SCAN_CHUNKS_PER_STEP
    steps = n_chunks // per

    def body(w_ref, qg_ref, kg_ref, attn_ref, g_ref, do_ref, dvn_ref, dsn_ref, ds_ref):
        @pl.when(pl.program_id(0) == 0)
        def _():
            ds_ref[...] = jnp.zeros_like(ds_ref)

        for j in reversed(range(per)):
            rows = _chunk_rows(j)
            dsn = [ds_ref[h] for h in range(HEADS)]
            dov = [do_ref[h, rows, :] for h in range(HEADS)]
            dvn = [_dot(attn_ref[j, h], dov[h], TN) + _dot(kg_ref[h, rows, :], dsn[h], NN) for h in range(HEADS)]
            eg = [jnp.exp(_colsum(g_ref[h, rows, :])) for h in range(HEADS)]
            for h in range(HEADS):
                dsn_ref[j, h] = dsn[h]
                ds_ref[h] = _dot(qg_ref[h, rows, :], dov[h], TN) + eg[h] * dsn[h] - _dot(w_ref[h, rows, :], dvn[h], TN)
                dvn_ref[h, rows, :] = dvn[h]

    act = pl.BlockSpec((HEADS, per * CHUNK, LANE), lambda n: (0, steps - 1 - n, 0))
    return _pcall(
        body, name=name,
        out_shape=(jax.ShapeDtypeStruct((HEADS, t, LANE), F32), jax.ShapeDtypeStruct((n_chunks, HEADS, LANE, LANE), F32)),
        grid=(steps,),
        in_specs=[act] * 3 + [pl.BlockSpec((per, HEADS, CHUNK, CHUNK), lambda n: (steps - 1 - n, 0, 0, 0)), act, act],
        out_specs=(act, pl.BlockSpec((per, HEADS, LANE, LANE), lambda n: (steps - 1 - n, 0, 0, 0))),
        scratch_shapes=[pltpu.VMEM((HEADS, LANE, LANE), F32)], semantics=("arbitrary",))(w, qg, kg, attn, g, do)


def _sum_all(x):
    return jnp.sum(jnp.sum(x, axis=1, keepdims=True), axis=0, keepdims=True)


def _rowsum(x):
    return jnp.sum(x, axis=1, keepdims=True)


def _deltanet_post(qkv, g, beta, tmats, states, dstates, do, dvn, vn, *, name):
    t = qkv.shape[1]
    n_chunks = t // CHUNK
    per = CHUNKS_PER_STEP
    probs = [(j, h) for j in range(per) for h in range(HEADS)]

    def body(qkv_ref, g_ref, b_ref, tm_ref, st_ref, dsn_ref, do_ref, dvn_ref, vn_ref, dqkv_ref, dg_ref, db_ref):
        incl, strict, _ = _chunk_masks()
        ones = jnp.ones((CHUNK, LANE), BF16)
        last_row = lax.broadcasted_iota(jnp.int32, (CHUNK, LANE), 0) == CHUNK - 1
        z = lambda f, *cols: [f(*a) for a in zip(*cols)]
        q = [qkv_ref[h, _chunk_rows(j), :] for j, h in probs]
        k = [qkv_ref[HEADS + h, _chunk_rows(j), :] for j, h in probs]
        v = [qkv_ref[2 * HEADS + h, _chunk_rows(j), :] for j, h in probs]
        bv = [b_ref[h, _chunk_rows(j), :] for j, h in probs]
        dov = [do_ref[h, _chunk_rows(j), :] for j, h in probs]
        dvn_ = [dvn_ref[h, _chunk_rows(j), :] for j, h in probs]
        vn_ = [vn_ref[h, _chunk_rows(j), :] for j, h in probs]
        tm = [tm_ref[j, h] for j, h in probs]
        s = [st_ref[j, h] for j, h in probs]
        dsn = [dsn_ref[j, h] for j, h in probs]
        dec = [_chunk_decay(g_ref[h, _chunk_rows(j), :], incl) for j, h in probs]
        decay = [d[2] for d in dec]
        egc = [jnp.exp(d[0]) for d in dec]
        ekg = [jnp.exp(d[1] - d[0]) for d in dec]
        kb = z(lambda a, b: a * b, k, bv)
        vb = z(lambda a, b: a * b, v, bv)
        kbg = z(lambda a, b: a * b, kb, egc)
        qg = z(lambda a, b: a * b, q, egc)
        kg = z(lambda a, b: a * b, k, ekg)
        kk = z(lambda a, b: _dot(a, b, NT), kb, k)
        qk = z(lambda a, b: _dot(a, b, NT), q, k)
        dattn = z(lambda a, b: jnp.where(incl, _dot(a, b, NT), 0.0), dov, vn_)
        dqg = z(lambda a, b: _dot(a, b, NT), dov, s)
        dkg = z(lambda a, b: _dot(a, b, NT), vn_, dsn)
        dglast = z(lambda a, b, c, d, e: _sum_all(a * b) * jnp.exp(e[1]) + _sum_all(c * d), s, dsn, dkg, kg, dec)
        dw = z(lambda a, b: -_dot(a, b, NT), dvn_, s)
        dtm = z(lambda a, b, c, d: _dot(a, b, NT) + _dot(c, d, NT), dvn_, vb, dw, kbg)
        dvb = z(lambda a, b: _dot(a, b, TN), tm, dvn_)
        dkbg = z(lambda a, b: _dot(a, b, TN), tm, dw)
        dlow = z(lambda a, b: jnp.where(strict, -_dot(_dot(a, b, TN), a, NT), 0.0), tm, dtm)
        dkk = z(lambda a, b: a * b, dlow, decay)
        dqk = z(lambda a, b: a * b, dattn, decay)
        dkb = z(lambda a, b, c, d: _dot(a, b, NN) + c * d, dkk, k, dkbg, egc)
        dk = z(lambda a, b, c, d, e, f, g_, h_: _dot(a, b, TN) + _dot(c, d, TN) + e * f + g_ * h_, dkk, kb, dqk, q, dkg, ekg, dkb, bv)
        dq = z(lambda a, b, c, d: _dot(a, b, NN) + c * d, dqk, k, dqg, egc)
        m = z(lambda a, b, c, d, e: (a * b + c * d) * e, dlow, kk, dattn, qk, decay)
        mcol = [_dot(mh, ones, TN) + _dot(ml, ones, TN) for mh, ml in (_split(a) for a in m)]
        for i, (j, h) in enumerate(probs):
            rows = _chunk_rows(j)
            dqkv_ref[h, rows, :] = dq[i]
            dqkv_ref[HEADS + h, rows, :] = dk[i]
            dqkv_ref[2 * HEADS + h, rows, :] = dvb[i] * bv[i]
            db_ref[h, rows, :] = jnp.broadcast_to(_rowsum(dkb[i] * k[i] + dvb[i] * v[i]), (CHUNK, LANE))
            dgc = (_rowsum(dqg[i] * qg[i] + dkbg[i] * kbg[i] - dkg[i] * kg[i]) + _rowsum(m[i]) - mcol[i]
                   + jnp.where(last_row, dglast[i], 0.0))
            dg_ref[h, rows, :] = _suffix_sum_rows(dgc)

    act = lambda heads: pl.BlockSpec((heads, per * CHUNK, LANE), lambda n: (0, n, 0))
    mat = lambda d: pl.BlockSpec((per, HEADS, d, d), lambda n: (n, 0, 0, 0))
    out = jax.ShapeDtypeStruct((HEADS, t, LANE), F32)
    return _pcall(
        body, name=name, out_shape=(jax.ShapeDtypeStruct((3 * HEADS, t, LANE), F32), out, out), grid=(n_chunks // per,),
        in_specs=[act(3 * HEADS), act(HEADS), act(HEADS), mat(CHUNK), mat(LANE), mat(LANE), act(HEADS), act(HEADS), act(HEADS)],
        out_specs=(act(3 * HEADS), act(HEADS), act(HEADS)), semantics=("parallel",),
        vmem_limit=VMEM_LIMIT)(qkv, g, beta, tmats, states, dstates, do, dvn, vn)


ANY = pl.BlockSpec(memory_space=pl.ANY)
PEERS = N_DEV - 1


def _all_gather(arrays, *, name):
    n = len(arrays)

    def body(*refs):
        ins, outs = refs[:n], refs[n:2 * n]
        send_sems, recv_sems, local_sems = refs[2 * n:]
        x, y, c = lax.axis_index("x"), lax.axis_index("y"), lax.axis_index("c")
        me, sibling = (x, y, c), (x, y, 1 - c)
        chips = [(1 - x, y), (x, 1 - y), (1 - x, 1 - y)]

        def copy(a, k, block, to, src=None):
            dst = outs[a].at[4 * block[0] + 2 * block[1] + block[2]]
            return pltpu.make_async_remote_copy(src_ref=dst if src is None else src, dst_ref=dst, send_sem=send_sems.at[a * PEERS + k],
                                                recv_sem=recv_sems.at[a * PEERS + k], device_id=to, device_id_type=MESH)

        local = [pltpu.make_async_copy(ins[a], outs[a].at[4 * x + 2 * y + c], local_sems.at[a]) for a in range(n)]
        for cp in local:
            cp.start()
        first = []
        for a in range(n):
            first += [copy(a, 1 + j, me, (*chip, c), src=ins[a]) for j, chip in enumerate(chips)]
            first.append(copy(a, 0, me, sibling, src=ins[a]))
        for cp in first:
            cp.start()
        passed = []
        for a in range(n):
            for j, chip in enumerate(chips):
                copy(a, 1 + j, (*chip, c), me).wait_recv()
                fwd = copy(a, 4 + j, (*chip, c), sibling)
                fwd.start()
                passed.append(fwd)
        for a in range(n):
            copy(a, 0, sibling, me).wait_recv()
            for j, chip in enumerate(chips):
                copy(a, 4 + j, (*chip, 1 - c), me).wait_recv()
        for cp in first + passed:
            cp.wait_send()
        for cp in local:
            cp.wait()

    return _pcall(body, name=name, out_shape=tuple(jax.ShapeDtypeStruct((N_DEV,) + a.shape, a.dtype) for a in arrays),
                  in_specs=[ANY] * n, out_specs=(ANY,) * n,
                  scratch_shapes=[pltpu.SemaphoreType.DMA((n * PEERS,)), pltpu.SemaphoreType.DMA((n * PEERS,)),
                                  pltpu.SemaphoreType.DMA((n,))])(*arrays)


CHIPS = 4


def _pair_exchange(arrays, *, name):
    n = len(arrays)

    def body(*refs):
        ins, outs = refs[:n], refs[n:2 * n]
        send_sems, recv_sems = refs[2 * n:]
        x, y, c = lax.axis_index("x"), lax.axis_index("y"), lax.axis_index("c")
        copies = []
        for a in range(n):
            for q in range(CHIPS):
                cp = pltpu.make_async_remote_copy(src_ref=ins[a].at[2 * q + 1 - c], dst_ref=outs[a].at[q],
                                                  send_sem=send_sems.at[a * CHIPS + q], recv_sem=recv_sems.at[a * CHIPS + q],
                                                  device_id=(x, y, 1 - c), device_id_type=MESH)
                cp.start()
                copies.append(cp)
        for cp in copies:
            cp.wait()

    return _pcall(body, name=name, out_shape=tuple(jax.ShapeDtypeStruct((CHIPS,) + a.shape[1:], a.dtype) for a in arrays),
                  in_specs=[ANY] * n, out_specs=(ANY,) * n,
                  scratch_shapes=[pltpu.SemaphoreType.DMA((n * CHIPS,)), pltpu.SemaphoreType.DMA((n * CHIPS,))])(*arrays)


def _pair_add(blocks, theirs, *, name):
    _, r, c_ = blocks.shape
    tr = _tile(r, 512, 16)

    def body(mine_ref, theirs_ref, o_ref):
        core = lax.axis_index("c")
        own = jnp.where(core == 0, mine_ref[0, 0].astype(F32), mine_ref[0, 1].astype(F32))
        o_ref[0] = (own + theirs_ref[0].astype(F32)).astype(o_ref.dtype)

    spec = pl.BlockSpec((1, tr, c_), lambda q, i: (q, i, 0))
    return _pcall(body, name=name, out_shape=jax.ShapeDtypeStruct(theirs.shape, theirs.dtype), grid=(CHIPS, r // tr),
                  in_specs=[pl.BlockSpec((1, 2, tr, c_), lambda q, i: (q, 0, i, 0)), spec], out_specs=spec,
                  semantics=("parallel", "parallel"), vmem_limit=VMEM_LIMIT)(blocks.reshape(CHIPS, 2, r, c_), theirs)


HBM = pl.BlockSpec(memory_space=pltpu.HBM)
SEM = pl.BlockSpec(memory_space=pltpu.SEMAPHORE)
EFFECT = pltpu.SideEffectType.DATAFLOW_SIDE_EFFECTING


GATHER, CHIP_GATHER, CHIP_SCATTER = "gather", "chip_gather", "chip_scatter"
PEERS_OF = {GATHER: N_DEV - 1, CHIP_GATHER: CHIPS - 1, CHIP_SCATTER: CHIPS - 1}


def _direct_copies(srcs, lands, send_sems, recv_sems, local_sems, kind):
    x, y, c = lax.axis_index("x"), lax.axis_index("y"), lax.axis_index("c")
    peers = PEERS_OF[kind]
    mine = 2 * x + y if kind == CHIP_SCATTER else 4 * x + 2 * y + c
    copies = []
    for a, (src, land) in enumerate(zip(srcs, lands)):
        for k in range(1, peers + 1):
            bits = k if kind == GATHER else 2 * k
            px = 1 - x if bits & 4 else x
            py = 1 - y if bits & 2 else y
            pc = 1 - c if bits & 1 else c
            copies.append(pltpu.make_async_remote_copy(
                src_ref=src.at[2 * px + py] if kind == CHIP_SCATTER else src, dst_ref=land.at[mine],
                send_sem=send_sems.at[a * peers + k - 1], recv_sem=recv_sems.at[a * peers + k - 1],
                device_id=(px, py, pc), device_id_type=MESH))
    for a, (src, land) in enumerate(zip(srcs, lands)):
        copies.append(pltpu.make_async_copy(src.at[mine] if kind == CHIP_SCATTER else src, land.at[mine], local_sems.at[a]))
    return copies


def _pair_swap(arrays, *, name):
    n = len(arrays)

    def body(*refs):
        mine, zones = refs[:n], refs[n:2 * n]
        send_sems, recv_sems = refs[2 * n:]
        x, y, c = lax.axis_index("x"), lax.axis_index("y"), lax.axis_index("c")
        copies = []
        for a in range(n):
            for q in range(CHIPS):
                copies.append(pltpu.make_async_remote_copy(
                    src_ref=mine[a].at[2 * q + c], dst_ref=zones[a].at[2 * q + c], send_sem=send_sems.at[a * CHIPS + q],
                    recv_sem=recv_sems.at[a * CHIPS + q], device_id=(x, y, 1 - c), device_id_type=MESH))
        for cp in copies:
            cp.start()
        for cp in copies:
            cp.wait()

    return _pcall(body, name=name, out_shape=tuple(jax.ShapeDtypeStruct(a.shape, a.dtype) for a in arrays),
                  in_specs=[ANY] * n, out_specs=(ANY,) * n, input_output_aliases={i: i for i in range(n)},
                  scratch_shapes=[pltpu.SemaphoreType.DMA((n * CHIPS,)), pltpu.SemaphoreType.DMA((n * CHIPS,))])(*arrays)


def _exchange_start(groups, kind, *, name, after=None):
    srcs = [s for group in groups for s in group]
    n = len(srcs)
    sizes = [len(group) for group in groups]
    starts = [sum(sizes[:g]) for g in range(len(groups))]
    land_shapes = [s.shape if kind == CHIP_SCATTER else (N_DEV,) + s.shape for s in srcs]
    peers = PEERS_OF[kind]
    extra = [] if after is None else [after]

    def body(*refs):
        srcs_, lands = refs[:n], refs[n:2 * n]
        token = refs[-1]
        sem_refs = refs[2 * n + len(extra):]
        for g, (at, size) in enumerate(zip(starts, sizes)):
            send_sems, recv_sems, local_sems = sem_refs[3 * g:3 * g + 3]
            for cp in _direct_copies(srcs_[at:at + size], lands[at:at + size], send_sems, recv_sems, local_sems, kind):
                cp.start()
        token[...] = jnp.zeros_like(token)

    sems = tuple(t for size in sizes for t in (pltpu.SemaphoreType.DMA((size * peers,)), pltpu.SemaphoreType.DMA((size * peers,)),
                                               pltpu.SemaphoreType.DMA((size,))))
    thru = tuple(pltpu.HBM(s.shape, s.dtype) for s in srcs) + tuple(pltpu.HBM(shp, s.dtype) for shp, s in zip(land_shapes, srcs))
    ins = [pltpu.with_memory_space_constraint(s, pltpu.HBM) for s in srcs]
    ins += [pltpu.with_memory_space_constraint(lax.empty(shp, s.dtype), pltpu.HBM) for shp, s in zip(land_shapes, srcs)]
    out = pl.pallas_call(
        body, name=name, out_shape=sems + thru + (jax.ShapeDtypeStruct((SUBLANE, LANE), F32),),
        in_specs=[HBM] * (2 * n) + [ANY] * len(extra),
        out_specs=(SEM,) * len(sems) + (HBM,) * (2 * n) + (pl.BlockSpec(memory_space=pltpu.VMEM),),
        input_output_aliases={i: len(sems) + i for i in range(2 * n)},
        compiler_params=pltpu.CompilerParams(has_side_effects=EFFECT))(*ins, *extra)
    arrays = out[len(sems):-1]
    started = [tuple(out[3 * g:3 * g + 3]) + tuple(arrays[at:at + size]) + tuple(arrays[n + at:n + at + size])
               for g, (at, size) in enumerate(zip(starts, sizes))]
    return started, out[-1]


def _exchange_wait(started, after, kind, *, name):
    n = (len(started) - 3) // 2
    sems, arrays = started[:3], started[3:]

    def body(*refs):
        srcs_, lands = refs[:n], refs[n:2 * n]
        send_sems, recv_sems, local_sems = refs[2 * n:2 * n + 3]
        for cp in _direct_copies(srcs_, lands, send_sems, recv_sems, local_sems, kind):
            cp.wait()

    out = pl.pallas_call(
        body, name=name, out_shape=tuple(pltpu.HBM(a.shape, a.dtype) for a in arrays),
        in_specs=[HBM] * (2 * n) + [SEM] * 3 + [ANY], out_specs=(HBM,) * (2 * n),
        input_output_aliases={i: i for i in range(2 * n)},
        compiler_params=pltpu.CompilerParams(has_side_effects=EFFECT))(*arrays, *sems, after)
    return out[n:]


def _adamw_reduce(w, parts, m, v, *, name, after=None):
    layers, r, c = w.shape
    assert len(parts) == layers
    senders = parts[0].shape[0]
    tr = _tile(r, 512, 16)
    tiles = r // tr
    bc1 = 1.0 - ADAM_B1 ** ADAM_STEP
    bc2 = 1.0 - ADAM_B2 ** ADAM_STEP

    def body(w_ref, *rest):
        p_refs = rest[:layers]
        m_ref, v_ref, g_ref, d_ref, nm_ref, nv_ref = rest[layers:]

        def update(p_ref):
            g = p_ref[0, :, pl.ds(0, c)].astype(F32)
            for s in range(1, senders):
                g = g + p_ref[s, :, pl.ds(0, c)].astype(F32)
            nm = ADAM_B1 * m_ref[0] + (1.0 - ADAM_B1) * g
            nv = ADAM_B2 * v_ref[0] + (1.0 - ADAM_B2) * (g * g)
            g_ref[0] = g
            nm_ref[0] = nm
            nv_ref[0] = nv
            d_ref[0] = -ADAM_LR * ((nm / bc1) / (jnp.sqrt(nv / bc2) + ADAM_EPS) + ADAM_WD * w_ref[0])

        for layer in range(layers):
            pl.when(pl.program_id(0) == layer)(functools.partial(update, p_refs[layer]))

    def part_spec(layer, shape):
        rest = 0 if layer > 0 else tiles - 1
        return pl.BlockSpec((senders, tr, shape[2]), lambda l, i: (0, jnp.where(l == layer, i, rest), 0))

    spec = pl.BlockSpec((1, tr, c), lambda l, i: (l, i, 0))
    out = jax.ShapeDtypeStruct((layers, r, c), F32)
    return _pcall(body, name=name, out_shape=(out,) * 4, grid=(layers, tiles),
                  in_specs=[spec] + [part_spec(layer, p.shape) for layer, p in enumerate(parts)] + [spec, spec],
                  out_specs=(spec,) * 4, semantics=("arbitrary", "arbitrary"), vmem_limit=VMEM_LIMIT, after=after)(w, *parts, m, v)


def _pool_windows():
    return jnp.repeat(jnp.asarray(POOL_WINDOWS, F32), POOL_DIM // len(POOL_WINDOWS))[None, :]


def _block_diag_pairs(pool_w):
    z = jnp.zeros_like(pool_w[0])
    return jnp.stack([jnp.block([[pool_w[2 * b], z], [z, pool_w[2 * b + 1]]]) for b in range(2)])


def _pad_lanes(vec):
    return jnp.zeros((1, LANE), F32).at[0, :vec.shape[0]].set(vec)


FF_SHARD = D_FF // N_DEV
FF_BLOCK = 384
D_FF_PAD = N_DEV * FF_BLOCK


def _layer_fwd(x, p_i, wt, fetch):
    wt = {**wt, **fetch(0, x)}
    proj, h1 = _matmul(x, wt["w_in"], "nt", norm_g=wt["norm1_g"], name="mm_in")
    qkv = _qkv_prep_fwd(proj, wt["conv_qkv"], name="qkv_prep_fwd")
    g, beta = _gates_fwd(proj, wt["a_log"], wt["dt_bias"], name="gates_fwd")
    u, w, qg, kg, attn, tmats = _deltanet_prep(qkv, g, beta, name="deltanet_prep")
    o, vn, states = _deltanet_scan(u, w, qg, kg, attn, g, name="deltanet_scan")
    wt.update(fetch(1, o))
    o_a = _apost_fwd(o, proj, wt["onorm_g"], name="apost_fwd")
    o_b = _pool_fwd(proj, wt["pool_win"], wt["pool_wbd"], wt["pool_scale"], name="pool_fwd")
    o_c = _sconv_fwd(proj, wt["sconv_w"], name="sconv_fwd")
    mixed = jnp.concatenate([o_a, o_b, o_c], axis=1)
    x1 = _matmul(mixed, wt["w_out"], "nn", res=x, name="mm_out")
    wt.update(fetch(2, x1))
    ff, gate, up, h2 = _swiglu_fwd(x1, wt["norm2_g"], wt["w_gate"], wt["w_up"], name="swiglu_fwd")
    wt.update(fetch(3, ff))
    x2 = _matmul(ff, wt["w_down"], "nn", res=x1, name="mm_down")
    wt.update(fetch(4, x2))
    pgl = _matmul(x2, wt["ple_gate"], "nn", name="mm_pleg")
    pp = _matmul(p_i, wt["ple_proj"], "nn", b_blocked=True, name="mm_plep")
    x3 = _ple_fwd(x2, pgl, pp, name="ple_fwd")
    saved = dict(x=x, h1=h1, proj=proj, qkv=qkv, g=g, beta=beta, o=o, states=states, tmats=tmats, mixed=mixed, x1=x1, h2=h2,
                 gate=gate, up=up, ff=ff, x2=x2, pgl=pgl, pp=pp, p=p_i, w=w, qg=qg, kg=kg, attn=attn, vn=vn, wt=wt)
    return x3, saved


def _col_blocks(g):
    a = g.shape[0]
    return jnp.transpose(g.reshape(a, N_DEV, -1), (1, 0, 2))


def _cols_joined(blocks):
    return jnp.transpose(blocks, (1, 0, 2)).reshape(blocks.shape[1], -1)


def _layer_bwd(dx3, sv, emit, after=None):
    gr, big = {}, {}
    wt = sv["wt"]
    rows = D_MODEL // N_DEV
    dpgl, dpp = _ple_bwd(dx3, sv["pgl"], sv["pp"], name="ple_bwd", after=after)
    big["ple_proj"] = _matmul(sv["p"], dpp, "tn", out_blocked=(N_DEV, rows), out_dtype=BF16, name="mm_dplep")
    big["ple_gate"] = _matmul(sv["x2"], dpgl, "tn", out_dtype=BF16, name="mm_dpleg").reshape(N_DEV, rows, D_MODEL)
    dx2 = _matmul(dpgl, wt["ple_gate"], "nt", res=dx3, name="mm_dx2")
    big["w_down"] = _matmul(sv["ff"], dx2, "tn", out_dtype=BF16, name="mm_ddown").reshape(N_DEV, FF_BLOCK, D_MODEL)
    dgate, dup = _swiglu_bwd(dx2, wt["w_down"], sv["gate"], sv["up"], name="swiglu_bwd", after=emit(0, big))
    big["w_gate"] = _matmul(dgate, sv["h2"], "tn", out_dtype=BF16, name="mm_dgate").reshape(N_DEV, FF_BLOCK, D_MODEL)
    big["w_up"] = _matmul(dup, sv["h2"], "tn", out_dtype=BF16, name="mm_dup").reshape(N_DEV, FF_BLOCK, D_MODEL)
    dh2 = _matmul(dgate, wt["w_gate"], "nn", name="mm_dh2_gate")
    dh2 = _matmul(dup, wt["w_up"], "nn", res=dh2, name="mm_dh2_up")
    dx1, gr["norm2_g"] = _rmsnorm_bwd(sv["x1"], wt["norm2_g"], dh2, dx2, name="rmsnorm_bwd")
    big["w_out"] = _matmul(sv["mixed"], dx1, "tn", out_dtype=BF16, name="mm_dout").reshape(N_DEV, rows, D_MODEL)
    dmixed = _matmul(dx1, wt["w_out"], "nt", name="mm_dmixed", after=emit(1, big))
    proj = sv["proj"]
    dcb, dcc, dch, dsconv = _sconv_bwd(proj, wt["sconv_w"], dmixed, name="sconv_bwd")
    big["sconv_w"] = _col_blocks(dsconv)
    dhp, dwbd, gr["pool_scale"] = _pool_bwd(proj, wt["pool_win"], wt["pool_wbd"], wt["pool_scale"], dmixed, name="pool_bwd")
    half = LANE // 2
    gr["pool_w"] = jnp.stack([dwbd[0, :half, :half], dwbd[0, half:, half:], dwbd[1, :half, :half], dwbd[1, half:, half:]])
    do, dz, gr["onorm_g"] = _apost_bwd(sv["o"], proj, wt["onorm_g"], dmixed, name="apost_bwd")
    dvn, dstates = _deltanet_bscan(sv["w"], sv["qg"], sv["kg"], sv["attn"], sv["g"], do, name="deltanet_bscan")
    dqkv_h, dg, dbeta = _deltanet_post(sv["qkv"], sv["g"], sv["beta"], sv["tmats"], sv["states"], dstates, do, dvn, sv["vn"],
                                       name="deltanet_post")
    dab, dalog, ddtb = _gates_bwd(proj, wt["a_log"], wt["dt_bias"], dg, dbeta, name="gates_bwd")
    gr["a_log"], gr["dt_bias"] = dalog[0, :HEADS], ddtb[0, :HEADS]
    dqkv, dconv = _qkv_prep_bwd(proj, wt["conv_qkv"], dqkv_h, name="qkv_prep_bwd")
    big["conv_qkv"] = _col_blocks(dconv)
    dproj = jnp.concatenate([dqkv, dz, dab, dhp, dcb, dcc, dch], axis=1)
    dwin = _matmul(dproj, sv["h1"], "tn", out_dtype=BF16, name="mm_din")
    big["w_in"] = jnp.concatenate([dwin[:AB_COL + 2 * HEADS], dwin[AB_COL + LANE:]], axis=0).reshape(N_DEV, -1, D_MODEL)
    dh1 = _matmul(dproj, wt["w_in"], "nn", name="mm_dh1", after=emit(2, big))
    dx, gr["norm1_g"] = _rmsnorm_bwd(sv["x"], wt["norm1_g"], dh1, dx1, name="rmsnorm_bwd")
    return dx, gr


FETCH_GROUPS = (("w_in", "conv_qkv", "sconv_w"), ("w_out",), ("w_gate", "w_up"), ("w_down",), ("ple_gate", "ple_proj"))
EMIT_GROUPS = (("ple_proj", "ple_gate", "w_down"), ("w_gate", "w_up", "w_out"), ("w_in", "conv_qkv", "sconv_w"))


def _small_weights(w, i):
    return dict(
        norm1_g=w["norm1_g"][i][None], norm2_g=w["norm2_g"][i][None], onorm_g=w["onorm_g"][i][None],
        a_log=_pad_lanes(w["a_log"][i]), dt_bias=_pad_lanes(w["dt_bias"][i]),
        pool_scale=w["pool_scale"][i][None], pool_win=_pool_windows(), pool_wbd=_block_diag_pairs(w["pool_w"][i]))


def _as_read(name, gathered):
    if name == "w_in":
        rows = gathered[:, :D_IN // N_DEV].reshape(-1, D_MODEL)
        return jnp.concatenate([rows[:AB_COL + 2 * HEADS], jnp.zeros((LANE - 2 * HEADS, D_MODEL), BF16),
                                rows[AB_COL + 2 * HEADS:]], axis=0)
    if name in ("conv_qkv", "sconv_w"):
        return _cols_joined(gathered)
    if name == "ple_proj":
        return gathered
    return gathered.reshape(-1, D_MODEL)


def _layer_weights(gathered, w, i):
    return {**_small_weights(w, i), **{k: _as_read(k, g) for k, g in gathered.items()}}


def _local_step(x, p, target, layers, final_g):
    saved = []
    h = x
    for i in range(DEPTH):
        replicated = {k: v for k, v in layers[i].items() if k not in SHARDED}
        h, sv = _layer_fwd(h, p[i], replicated, lambda group, after, i=i: {k: layers[i][k] for k in FETCH_GROUPS[group]})
        saved.append(sv)
    dx, dgf, loss = _loss_head(h, final_g, target, name="loss_head")
    big, small = [{} for _ in range(DEPTH)], [None] * DEPTH
    for i in reversed(range(DEPTH)):
        dx, small[i] = _layer_bwd(dx, saved[i], lambda group, blocks, i=i: big[i].update({k: blocks[k] for k in EMIT_GROUPS[group]}))
    return loss, dx, big, small, dgf


SHARDED = ("w_in", "w_gate", "w_up", "w_down", "w_out", "ple_gate", "ple_proj", "conv_qkv", "sconv_w")
SMALL = ("norm1_g", "a_log", "dt_bias", "onorm_g", "pool_w", "pool_scale", "norm2_g", "final_g")
SLAB_COLS = 1024


def _payload(name, shard):
    if name in ("conv_qkv", "sconv_w"):
        return shard
    out = shard.astype(BF16)
    if name in ("w_gate", "w_up", "w_down"):
        out = jnp.pad(out, ((0, FF_BLOCK - FF_SHARD), (0, 0)))
    if name == "w_in":
        out = jnp.pad(out, ((0, -out.shape[0] % (2 * SUBLANE)), (0, 0)))
    return out


TRANSPOSED = ("w_in", "w_gate", "w_up")


def _ff_rows(t):
    return jnp.transpose(t, (0, 2, 1))


def _slab_rows(shape):
    size = 1
    for s in shape:
        size *= s
    return SUBLANE * -(-size // (SUBLANE * SLAB_COLS))


def _pack_slab(parts, extra_row):
    rows = []
    for name in SMALL:
        flat = parts[name].reshape(-1)
        nrow = _slab_rows(parts[name].shape)
        rows.append(jnp.pad(flat, (0, nrow * SLAB_COLS - flat.shape[0])).reshape(nrow, SLAB_COLS))
    rows.append(jnp.pad(extra_row, ((0, SUBLANE - 1), (0, 0))))
    return jnp.concatenate(rows, axis=0)


def _unpack_slab(slab, shapes):
    out, row = {}, 0
    for name in SMALL:
        size = 1
        for s in shapes[name]:
            size *= s
        out[name] = slab[row:row + _slab_rows(shapes[name])].reshape(-1)[:size].reshape(shapes[name])
        row += _slab_rows(shapes[name])
    return out, row


def kernel(x, p, norm1_g, w_in, conv_qkv, a_log, dt_bias, onorm_g, pool_w, pool_scale, sconv_w, w_out, norm2_g, w_gate, w_up, w_down, ple_proj, ple_gate, final_g, loss_target, m_norm1_g, m_w_in, m_conv_qkv, m_a_log, m_dt_bias, m_onorm_g, m_pool_w, m_pool_scale, m_sconv_w, m_w_out, m_norm2_g, m_w_gate, m_w_up, m_w_down, m_ple_proj, m_ple_gate, m_final_g, v_norm1_g, v_w_in, v_conv_qkv, v_a_log, v_dt_bias, v_onorm_g, v_pool_w, v_pool_scale, v_sconv_w, v_w_out, v_norm2_g, v_w_gate, v_w_up, v_w_down, v_ple_proj, v_ple_gate, v_final_g):
    names = ["norm1_g", "w_in", "conv_qkv", "a_log", "dt_bias", "onorm_g", "pool_w", "pool_scale", "sconv_w", "w_out", "norm2_g",
             "w_gate", "w_up", "w_down", "ple_proj", "ple_gate", "final_g"]
    w = dict(zip(names, [norm1_g, w_in, conv_qkv, a_log, dt_bias, onorm_g, pool_w, pool_scale, sconv_w, w_out, norm2_g, w_gate, w_up,
                         w_down, ple_proj, ple_gate, final_g]))
    m = dict(zip(names, [m_norm1_g, m_w_in, m_conv_qkv, m_a_log, m_dt_bias, m_onorm_g, m_pool_w, m_pool_scale, m_sconv_w, m_w_out,
                         m_norm2_g, m_w_gate, m_w_up, m_w_down, m_ple_proj, m_ple_gate, m_final_g]))
    v = dict(zip(names, [v_norm1_g, v_w_in, v_conv_qkv, v_a_log, v_dt_bias, v_onorm_g, v_pool_w, v_pool_scale, v_sconv_w, v_w_out,
                         v_norm2_g, v_w_gate, v_w_up, v_w_down, v_ple_proj, v_ple_gate, v_final_g]))
    w.update({k: _ff_rows(w[k]) for k in TRANSPOSED})

    first, rest = FETCH_GROUPS[0], tuple(k for members in FETCH_GROUPS[1:] for k in members)
    gathered = dict(zip(first, _all_gather([_payload(k, w[k][0]) for k in first], name="all_gather_weights")))
    (flying0,), token = _exchange_start([[_payload(k, w[k][0]) for k in rest]], CHIP_GATHER, name="gather_start_0",
                                        after=gathered[first[0]])
    replicated = [_small_weights(w, i) for i in range(DEPTH)]
    replicated[0]["norm1_g"] = replicated[0]["norm1_g"] + token[0, 0]
    for group in (m, v):
        group.update({k: _ff_rows(group[k] + token[0, 0]) for k in TRANSPOSED})
    flying1 = []

    def fetch(i, group, after):
        if i == 0 and group == 1:
            landed = _exchange_wait(flying0, after, CHIP_GATHER, name="gather_wait_0")
            gathered.update(zip(rest, _pair_swap(landed, name="pair_swap")))
            started, token = _exchange_start([[_payload(k, w[k][1]) for k in SHARDED]], CHIP_GATHER, name="gather_start_1",
                                             after=gathered[rest[0]])
            flying1.extend(started)
            return {**{k: _as_read(k, gathered[k]) for k in FETCH_GROUPS[group]},
                    "onorm_g": replicated[0]["onorm_g"] + token[0, 0]}
        if i == 1 and group == 0:
            landed = _exchange_wait(flying1[0], after, CHIP_GATHER, name="gather_wait_1")
            gathered.update(zip(SHARDED, _pair_swap(landed, name="pair_swap")))
        return {k: _as_read(k, gathered[k]) for k in FETCH_GROUPS[group]}

    def reduce_scatter_start(members, blocks, tag):
        mine = [blocks[k] for k in members]
        theirs = _pair_exchange(mine, name="pair_exchange")
        sums = [_pair_add(a, b, name="pair_add") for a, b in zip(mine, theirs)]
        (started,), token = _exchange_start([sums], CHIP_SCATTER, name="exchange_start_" + tag)
        return started, token

    h, saved0 = _layer_fwd(x[0], p[0, 0], replicated[0], functools.partial(fetch, 0))
    h, saved1 = _layer_fwd(h, p[1, 0], replicated[1], functools.partial(fetch, 1))
    dx, dgf, loss_part = _loss_head(h, final_g[None], loss_target[0], name="loss_head")
    small, big1, flying0 = [None] * DEPTH, {}, []
    dx, small[1] = _layer_bwd(dx, saved1, lambda group, blocks: big1.update({k: blocks[k] for k in EMIT_GROUPS[group]}))
    flying1, token = reduce_scatter_start(SHARDED, big1, "1")

    def emit(group, blocks):
        started, token = reduce_scatter_start(EMIT_GROUPS[group], blocks, f"0_{group}")
        flying0.append(started)
        return token

    dx, small[0] = _layer_bwd(dx, saved0, emit, after=token)
    received = [{}, dict(zip(SHARDED, _exchange_wait(flying1, dx, CHIP_SCATTER, name="exchange_wait_1")))]
    for group, members in enumerate(EMIT_GROUPS):
        received[0].update(zip(members, _exchange_wait(flying0[group], dx, CHIP_SCATTER, name=f"exchange_wait_0_{group}")))

    grads = {k: jnp.stack([small[i][k] for i in range(DEPTH)]) for k in small[0]}
    grads = {k: g[:, 0] if k in ("norm1_g", "norm2_g", "onorm_g", "pool_scale") else g for k, g in grads.items()}
    grads["final_g"] = dgf[0]
    loss_row = jnp.pad(loss_part, ((0, 0), (0, SLAB_COLS - LANE)))
    (small_flying,), token = _exchange_start([[_pack_slab(grads, loss_row)]], GATHER, name="small_gather_start")

    out_g, out_d, out_m, out_v = {}, {}, {}, {}
    for k in SHARDED:
        out_g[k], out_d[k], out_m[k], out_v[k] = _adamw_reduce(w[k], [received[i][k] for i in range(DEPTH)], m[k], v[k],
                                                                name="adamw_" + k, after=token)
    behind_all = jnp.stack([out_v[k][0, 0, 0] for k in SHARDED])
    (small_parts,) = _exchange_wait(small_flying, behind_all, GATHER, name="small_gather_wait")
    zero_row = jnp.zeros((1, SLAB_COLS), F32)
    slabs = _adamw_reduce(_pack_slab(w, zero_row)[None], [small_parts], _pack_slab(m, zero_row)[None],
                          _pack_slab(v, zero_row)[None], name="adamw_small")
    slabs = [s[0] for s in slabs]
    shapes = {k: w[k].shape for k in SMALL}
    for dst, slab in zip((out_g, out_d, out_m, out_v), slabs):
        vals, _ = _unpack_slab(slab, shapes)
        dst.update(vals)
    _, loss_at = _unpack_slab(slabs[0], shapes)
    loss = slabs[0][loss_at, 0]
    for group in (out_g, out_d, out_m, out_v):
        group.update({k: _ff_rows(group[k]) for k in TRANSPOSED})

    return (loss, dx[None], *[out_g[k] for k in names], *[out_d[k] for k in names], *[out_m[k] for k in names],
            *[out_v[k] for k in names])
```

```python
import functools

import jax
import jax.numpy as jnp
from jax import lax
from jax.experimental import pallas as pl
from jax.experimental.pallas import tpu as pltpu

F32 = jnp.float32
BF16 = jnp.bfloat16

D_MODEL = 1024
DEPTH = 2
PLE_DIM = 256
EPS = 1e-6
HEAD_DIM = 128
HEADS = 4
A_DIM = HEADS * HEAD_DIM
QKV_TAPS = 4
CHUNK = 64
POOL_WINDOWS = (2, 4, 8, 16)
POOL_DIM = 256
CONV_DIM = 256
CONV_TAPS = 3
D_FF = 2816
D_IN = 3080
D_IN_PAD = 3200
AB_COL = 2048
N_DEV = 8

ADAM_LR = 0.001
ADAM_B1 = 0.9
ADAM_B2 = 0.999
ADAM_EPS = 1e-08
ADAM_WD = 0.01
ADAM_STEP = 10

LANE = 128
SUBLANE = 8
VMEM_BYTES_V7X = 64 * 1024 * 1024
VMEM_LIMIT = 48 * 1024 * 1024

_HI = lax.Precision.HIGHEST
NN = ((1,), (0,))
NT = ((1,), (1,))
TN = ((0,), (0,))
MESH = pl.DeviceIdType.MESH


def _dot(a, b, dims, hi=False):
    if hi:
        return lax.dot_general(a, b, (dims, ((), ())), precision=_HI, preferred_element_type=F32)
    return lax.dot_general(a.astype(BF16), b.astype(BF16), (dims, ((), ())), preferred_element_type=F32)


def _pcall(body, *, name, out_shape, grid=(), in_specs=None, out_specs=None, scratch_shapes=(), semantics=None,
           vmem_limit=None, after=None, **kw):
    params = {}
    if semantics is not None:
        params["dimension_semantics"] = semantics
    if vmem_limit is not None:
        params["vmem_limit_bytes"] = vmem_limit
    if after is not None:
        n_in, inner = len(in_specs), body
        body = lambda *refs: inner(*refs[:n_in], *refs[n_in + 1:])
        in_specs = list(in_specs) + [pl.BlockSpec(after.shape, lambda *_: (0,) * after.ndim)]
    call = pl.pallas_call(
        body, name=name, out_shape=out_shape, grid=grid, in_specs=in_specs, out_specs=out_specs,
        scratch_shapes=list(scratch_shapes), compiler_params=pltpu.CompilerParams(**params), **kw)
    return call if after is None else (lambda *args: call(*args, after))


def _sigmoid(x):
    return 1.0 / (1.0 + jnp.exp(-x))


def _softplus(x):
    return jnp.maximum(x, 0.0) + jnp.log(1.0 + jnp.exp(-jnp.abs(x)))


def _tile(n, cap, mult):
    if n <= cap:
        return n
    best = None
    for t in range(mult, cap + 1, mult):
        if n % t == 0:
            best = t
    assert best is not None, (n, cap, mult)
    return best


ROWS_PER_STEP = 512
NARROW_RESULT = 1024
COLS_PER_DOT = 640


def _resident(weight):
    return pl.BlockSpec(weight.shape, lambda i: (0,) * weight.ndim, pipeline_mode=pl.Buffered(1))


def _matmul_rows(a, b, mode, *, name, res=None, out_dtype=F32, b_blocked=False, after=None, norm_g=None):
    m, k = a.shape
    if b_blocked:
        nb, _, bw = b.shape
        n = nb * bw if mode == "nn" else b.shape[1]
    else:
        n = b.shape[1] if mode == "nn" else b.shape[0]
    tm = _tile(m, ROWS_PER_STEP if n > NARROW_RESULT else 2 * ROWS_PER_STEP, 16)
    cn = bw if (b_blocked and mode == "nn") else _tile(n, COLS_PER_DOT, LANE)
    has_res = res is not None
    normed = norm_g is not None

    def body(*refs):
        a_ref, b_ref = refs[0], refs[1]
        g_ref = refs[2] if normed else None
        res_ref = refs[2 + normed] if has_res else None
        o_ref = refs[2 + normed + has_res]
        if normed:
            av = _rms_normed(a_ref[...], g_ref[...])
            refs[3 + normed + has_res][...] = av
        elif not (b_blocked and mode == "nt"):
            av = a_ref[...].astype(BF16)
        for j in range(n // cn):
            cols = pl.ds(j * cn, cn)
            if mode == "nn":
                part = _dot(av, b_ref[j] if b_blocked else b_ref[:, cols], NN)
            elif not b_blocked:
                part = _dot(av, b_ref[cols, :], NT)
            else:
                part = None
                for s in range(nb):
                    term = _dot(a_ref[:, pl.ds(s * bw, bw)], b_ref[s, cols, :], NT)
                    part = term if part is None else part + term
            if has_res:
                part = part + res_ref[:, cols]
            o_ref[:, cols] = part.astype(o_ref.dtype)

    row = lambda width: pl.BlockSpec((tm, width), lambda i: (i, 0))
    whole = _resident(b)
    ins = [a, b] + ([norm_g] if normed else []) + ([res] if has_res else [])
    specs = [row(k), whole] + ([pl.BlockSpec((1, k), lambda i: (0, 0))] if normed else []) + ([row(n)] if has_res else [])
    out = jax.ShapeDtypeStruct((m, n), out_dtype)
    return _pcall(body, name=name, out_shape=(out, jax.ShapeDtypeStruct((m, k), BF16)) if normed else out, grid=(m // tm,),
                  in_specs=specs, out_specs=(row(n), row(k)) if normed else row(n), semantics=("parallel",),
                  vmem_limit=VMEM_LIMIT, after=after)(*ins)


def _rms_normed(xv, gv):
    return (xv * lax.rsqrt(jnp.mean(xv * xv, axis=-1, keepdims=True) + EPS) * gv).astype(BF16)


def _matmul(a, b, mode, *, name, res=None, out_dtype=F32, b_blocked=False, out_blocked=None, after=None, norm_g=None):
    if mode != "tn":
        return _matmul_rows(a, b, mode, name=name, res=res, out_dtype=out_dtype, b_blocked=b_blocked, after=after, norm_g=norm_g)
    assert res is None and not b_blocked and after is None and norm_g is None
    (t, m), (t2, n) = a.shape, b.shape
    assert t == t2, (a.shape, b.shape)
    tm = _tile(m, 1024, LANE)
    tn = _tile(n, COLS_PER_DOT, LANE)
    if out_blocked is not None:
        assert out_blocked[0] * out_blocked[1] == n
        tn = out_blocked[1]

    def body(a_ref, b_ref, o_ref):
        part = _dot(a_ref[...], b_ref[...], TN).astype(o_ref.dtype)
        if out_blocked is None:
            o_ref[...] = part
        else:
            o_ref[0] = part

    o_spec = (pl.BlockSpec((tm, tn), lambda i, j: (i, j)) if out_blocked is None
              else pl.BlockSpec((1, tm, tn), lambda i, j: (j, i, 0)))
    o_shape = (m, n) if out_blocked is None else (out_blocked[0], m, out_blocked[1])
    return _pcall(body, name=name, out_shape=jax.ShapeDtypeStruct(o_shape, out_dtype), grid=(m // tm, n // tn),
                  in_specs=[pl.BlockSpec((t, tm), lambda i, j: (0, i)), pl.BlockSpec((t, tn), lambda i, j: (0, j))],
                  out_specs=o_spec, semantics=("parallel", "parallel"), vmem_limit=VMEM_LIMIT)(a, b)


ROW_TILE = 512


def _rows(t, width, idx=0):
    return pl.BlockSpec((ROW_TILE, width), lambda i: (i, idx))


def _vec(width):
    return pl.BlockSpec((1, width), lambda i: (0, 0))


def _rmsnorm_bwd(x, g, dh, dres, *, name):
    t, d = x.shape

    def body(x_ref, g_ref, dh_ref, dres_ref, dx_ref, dg_ref):
        xv = x_ref[...]
        r = lax.rsqrt(jnp.mean(xv * xv, axis=-1, keepdims=True) + EPS)
        xhat = xv * r
        dhv = dh_ref[...].astype(F32)
        dhg = dhv * g_ref[...]
        dx_ref[...] = dres_ref[...] + r * (dhg - xhat * jnp.mean(dhg * xhat, axis=-1, keepdims=True))
        part = jnp.sum(dhv * xhat, axis=0, keepdims=True)

        @pl.when(pl.program_id(0) == 0)
        def _():
            dg_ref[...] = part

        @pl.when(pl.program_id(0) > 0)
        def _():
            dg_ref[...] += part

    return _pcall(body, name=name, out_shape=(jax.ShapeDtypeStruct((t, d), F32), jax.ShapeDtypeStruct((1, d), F32)),
                  grid=(t // ROW_TILE,), in_specs=[_rows(t, d), _vec(d), _rows(t, d), _rows(t, d)],
                  out_specs=(_rows(t, d), _vec(d)), semantics=("arbitrary",))(x, g, dh, dres)


def _swiglu_fwd(x, norm_g, w_gate, w_up, *, name):
    t, k = x.shape
    f = w_gate.shape[0]
    tm = _tile(t, ROWS_PER_STEP, 16)
    cn = _tile(f, COLS_PER_DOT, LANE)

    def body(x_ref, g_ref, wg_ref, wu_ref, ff_ref, gate_ref, up_ref, h_ref):
        hv = _rms_normed(x_ref[...], g_ref[...])
        h_ref[...] = hv
        for j in range(f // cn):
            cols = pl.ds(j * cn, cn)
            gv = _dot(hv, wg_ref[cols, :], NT)
            uv = _dot(hv, wu_ref[cols, :], NT)
            gate_ref[:, cols] = gv.astype(BF16)
            up_ref[:, cols] = uv.astype(BF16)
            ff_ref[:, cols] = (gv * _sigmoid(gv) * uv).astype(BF16)

    row = lambda width: pl.BlockSpec((tm, width), lambda i: (i, 0))
    out = jax.ShapeDtypeStruct((t, f), BF16)
    return _pcall(body, name=name, out_shape=(out,) * 3 + (jax.ShapeDtypeStruct((t, k), BF16),), grid=(t // tm,),
                  in_specs=[row(k), pl.BlockSpec((1, k), lambda i: (0, 0)), _resident(w_gate), _resident(w_up)],
                  out_specs=(row(f),) * 3 + (row(k),), semantics=("parallel",), vmem_limit=VMEM_LIMIT)(x, norm_g, w_gate, w_up)


def _swiglu_bwd(dx2, w_down, gate, up, *, name, after=None):
    t, d = dx2.shape
    f = w_down.shape[0]
    tm = _tile(t, ROWS_PER_STEP, 16)
    cn = _tile(f, COLS_PER_DOT, LANE)

    def body(dx_ref, w_ref, gate_ref, up_ref, dgate_ref, dup_ref):
        dxv = dx_ref[...].astype(BF16)
        for j in range(f // cn):
            cols = pl.ds(j * cn, cn)
            dffv = _dot(dxv, w_ref[cols, :], NT)
            gv = gate_ref[:, cols].astype(F32)
            sig = _sigmoid(gv)
            dgate_ref[:, cols] = (dffv * up_ref[:, cols].astype(F32) * sig * (1.0 + gv * (1.0 - sig))).astype(BF16)
            dup_ref[:, cols] = (dffv * gv * sig).astype(BF16)

    row = lambda width: pl.BlockSpec((tm, width), lambda i: (i, 0))
    out = jax.ShapeDtypeStruct((t, f), BF16)
    return _pcall(body, name=name, out_shape=(out, out), grid=(t // tm,), in_specs=[row(d), _resident(w_down), row(f), row(f)],
                  out_specs=(row(f), row(f)), semantics=("parallel",), vmem_limit=VMEM_LIMIT, after=after)(dx2, w_down, gate, up)


def _ple_fwd(x2, pgl, pp, *, name):
    t, d = x2.shape

    def body(x_ref, pgl_ref, pp_ref, o_ref):
        o_ref[...] = x_ref[...] + _sigmoid(pgl_ref[...]) * pp_ref[...]

    return _pcall(body, name=name, out_shape=jax.ShapeDtypeStruct((t, d), F32), grid=(t // ROW_TILE,),
                  in_specs=[_rows(t, d)] * 3, out_specs=_rows(t, d), semantics=("parallel",))(x2, pgl, pp)


def _ple_bwd(dx3, pgl, pp, *, name, after=None):
    t, d = dx3.shape

    def body(dx_ref, pgl_ref, pp_ref, dpgl_ref, dpp_ref):
        dxv = dx_ref[...]
        sig = _sigmoid(pgl_ref[...])
        dpp_ref[...] = (dxv * sig).astype(BF16)
        dpgl_ref[...] = (dxv * pp_ref[...] * sig * (1.0 - sig)).astype(BF16)

    return _pcall(body, name=name, out_shape=(jax.ShapeDtypeStruct((t, d), BF16),) * 2, grid=(t // ROW_TILE,),
                  in_specs=[_rows(t, d)] * 3, out_specs=(_rows(t, d),) * 2, semantics=("parallel",), after=after)(dx3, pgl, pp)


def _loss_head(x3, g, target, *, name):
    t, d = x3.shape

    def body(x_ref, g_ref, t_ref, dx_ref, dg_ref, loss_ref):
        xv = x_ref[...]
        r = lax.rsqrt(jnp.mean(xv * xv, axis=-1, keepdims=True) + EPS)
        xhat = xv * r
        gv = g_ref[...]
        err = xhat * gv - t_ref[...]
        row_loss = jnp.sum(err * err, axis=-1, keepdims=True) * (0.5 / d)
        lpart = jnp.broadcast_to(jnp.sum(row_loss, axis=0, keepdims=True), (1, LANE))
        dy = err * (1.0 / d)
        dyg = dy * gv
        dx_ref[...] = r * (dyg - xhat * jnp.mean(dyg * xhat, axis=-1, keepdims=True))
        gpart = jnp.sum(dy * xhat, axis=0, keepdims=True)

        @pl.when(pl.program_id(0) == 0)
        def _():
            dg_ref[...] = gpart
            loss_ref[...] = lpart

        @pl.when(pl.program_id(0) > 0)
        def _():
            dg_ref[...] += gpart
            loss_ref[...] += lpart

    return _pcall(body, name=name,
                  out_shape=(jax.ShapeDtypeStruct((t, d), F32), jax.ShapeDtypeStruct((1, d), F32), jax.ShapeDtypeStruct((1, LANE), F32)),
                  grid=(t // ROW_TILE,), in_specs=[_rows(t, d), _vec(d), _rows(t, d)],
                  out_specs=(_rows(t, d), _vec(d), _vec(LANE)), semantics=("arbitrary",))(x3, g, target)


def _shift_down(x, d):
    if d == 0:
        return x
    row = lax.broadcasted_iota(jnp.int32, x.shape, 0)
    return jnp.where(row >= d, pltpu.roll(x, d, 0), 0.0)


def _shift_up(x, d):
    if d == 0:
        return x
    t = x.shape[0]
    row = lax.broadcasted_iota(jnp.int32, x.shape, 0)
    return jnp.where(row < t - d, pltpu.roll(x, t - d, 0), 0.0)


def _colsum(x):
    return jnp.sum(x, axis=0, keepdims=True)


def _col(t, idx_fn):
    return pl.BlockSpec((t, LANE), idx_fn)


def _conv_fwd(x, w_ref, taps):
    acc = None
    for j in range(taps):
        term = w_ref[pl.ds(j, 1), :] * _shift_down(x, taps - 1 - j)
        acc = term if acc is None else acc + term
    return acc


def _conv_bwd(x, dy, w_ref, dw_ref, taps):
    dx = None
    for j in range(taps):
        term = w_ref[pl.ds(j, 1), :] * _shift_up(dy, taps - 1 - j)
        dx = term if dx is None else dx + term
        dw_ref[pl.ds(j, 1), :] = _colsum(dy * _shift_down(x, taps - 1 - j))
    return dx


def _qkv_prep_fwd(proj, conv_w, *, name):
    t = proj.shape[0]
    scale = HEAD_DIM ** -0.5

    def body(x_ref, w_ref, o_ref):
        j = pl.program_id(0)
        c = _conv_fwd(x_ref[...], w_ref, QKV_TAPS)
        s = c * _sigmoid(c)
        r = lax.rsqrt(jnp.sum(s * s, axis=-1, keepdims=True) + EPS)
        f = jnp.where(j < 2 * HEADS, r, 1.0) * jnp.where(j < HEADS, scale, 1.0)
        o_ref[0] = s * f

    return _pcall(body, name=name, out_shape=jax.ShapeDtypeStruct((3 * HEADS, t, LANE), F32), grid=(3 * HEADS,),
                  in_specs=[_col(t, lambda j: (0, j)), pl.BlockSpec((QKV_TAPS, LANE), lambda j: (0, j))],
                  out_specs=pl.BlockSpec((1, t, LANE), lambda j: (j, 0, 0)), semantics=("parallel",),
                  vmem_limit=VMEM_LIMIT)(proj, conv_w)


def _qkv_prep_bwd(proj, conv_w, dqkv, *, name):
    t = proj.shape[0]
    scale = HEAD_DIM ** -0.5

    def body(x_ref, w_ref, d_ref, dx_ref, dw_ref):
        j = pl.program_id(0)
        xv = x_ref[...]
        c = _conv_fwd(xv, w_ref, QKV_TAPS)
        sig = _sigmoid(c)
        s = c * sig
        r = lax.rsqrt(jnp.sum(s * s, axis=-1, keepdims=True) + EPS)
        n0 = s * r
        dv = d_ref[0]
        dn0 = dv * jnp.where(j < HEADS, scale, 1.0)
        ds_norm = r * (dn0 - n0 * jnp.sum(dn0 * n0, axis=-1, keepdims=True))
        ds = jnp.where(j < 2 * HEADS, ds_norm, dv)
        dc = ds * sig * (1.0 + c * (1.0 - sig))
        dx_ref[...] = _conv_bwd(xv, dc, w_ref, dw_ref, QKV_TAPS).astype(BF16)

    return _pcall(body, name=name,
                  out_shape=(jax.ShapeDtypeStruct((t, 3 * A_DIM), BF16), jax.ShapeDtypeStruct((QKV_TAPS, 3 * A_DIM), F32)),
                  grid=(3 * HEADS,),
                  in_specs=[_col(t, lambda j: (0, j)), pl.BlockSpec((QKV_TAPS, LANE), lambda j: (0, j)),
                            pl.BlockSpec((1, t, LANE), lambda j: (j, 0, 0))],
                  out_specs=(_col(t, lambda j: (0, j)), pl.BlockSpec((QKV_TAPS, LANE), lambda j: (0, j))),
                  semantics=("parallel",), vmem_limit=VMEM_LIMIT)(proj, conv_w, dqkv)


def _lane_pick(x, lane_idx, lane):
    return jnp.broadcast_to(jnp.sum(jnp.where(lane == lane_idx, x, 0.0), axis=-1, keepdims=True), x.shape)


def _gates_fwd(proj, alog, dtb, *, name):
    t = proj.shape[0]

    def body(x_ref, alog_ref, dtb_ref, g_ref, b_ref):
        xv = x_ref[...]
        lane = lax.broadcasted_iota(jnp.int32, xv.shape, 1)
        gall = -jnp.exp(alog_ref[...]) * _softplus(xv + dtb_ref[...])
        ball = _sigmoid(xv)
        for h in range(HEADS):
            g_ref[h] = _lane_pick(gall, h, lane)
            b_ref[h] = _lane_pick(ball, HEADS + h, lane)

    out = jax.ShapeDtypeStruct((HEADS, t, LANE), F32)
    whole = pl.BlockSpec((HEADS, t, LANE), lambda i: (0, 0, 0))
    return _pcall(body, name=name, out_shape=(out, out), grid=(1,),
                  in_specs=[_col(t, lambda i: (0, AB_COL // LANE)), _vec(LANE), _vec(LANE)], out_specs=(whole, whole),
                  semantics=("arbitrary",), vmem_limit=VMEM_LIMIT)(proj, alog, dtb)


def _gates_bwd(proj, alog, dtb, dg, dbeta, *, name):
    t = proj.shape[0]

    def body(x_ref, alog_ref, dtb_ref, dg_ref, db_ref, dab_ref, dalog_ref, ddtb_ref):
        xv = x_ref[...]
        lane = lax.broadcasted_iota(jnp.int32, xv.shape, 1)
        lane1 = lax.broadcasted_iota(jnp.int32, (1, LANE), 1)
        z = xv + dtb_ref[...]
        nea = -jnp.exp(alog_ref[...])
        da_f = nea * _sigmoid(z)
        g_f = nea * _softplus(z)
        ball = _sigmoid(xv)
        db_f = ball * (1.0 - ball)
        dab = jnp.zeros_like(xv)
        dalog = jnp.zeros((1, LANE), F32)
        for h in range(HEADS):
            dgh = dg_ref[h]
            dab = dab + jnp.where(lane == h, dgh * da_f, 0.0) + jnp.where(lane == HEADS + h, db_ref[h] * db_f, 0.0)
            dalog = dalog + jnp.where(lane1 == h, _colsum(dgh * g_f), 0.0)
        dab_ref[...] = dab.astype(BF16)
        dalog_ref[...] = dalog
        ddtb_ref[...] = jnp.where(lane1 < HEADS, _colsum(dab), 0.0)

    whole = pl.BlockSpec((HEADS, t, LANE), lambda i: (0, 0, 0))
    vec = jax.ShapeDtypeStruct((1, LANE), F32)
    return _pcall(body, name=name, out_shape=(jax.ShapeDtypeStruct((t, LANE), BF16), vec, vec), grid=(1,),
                  in_specs=[_col(t, lambda i: (0, AB_COL // LANE)), _vec(LANE), _vec(LANE), whole, whole],
                  out_specs=(_col(t, lambda i: (0, 0)), _vec(LANE), _vec(LANE)), semantics=("arbitrary",),
                  vmem_limit=VMEM_LIMIT)(proj, alog, dtb, dg, dbeta)


Z_COL = 3 * A_DIM // LANE


def _apost_fwd(o, proj, gn, *, name):
    t = proj.shape[0]

    def body(o_ref, z_ref, gn_ref, y_ref):
        ov = o_ref[0]
        z = z_ref[...]
        r = lax.rsqrt(jnp.mean(ov * ov, axis=-1, keepdims=True) + EPS)
        y_ref[...] = (ov * r * gn_ref[...] * (z * _sigmoid(z))).astype(BF16)

    return _pcall(body, name=name, out_shape=jax.ShapeDtypeStruct((t, A_DIM), BF16), grid=(HEADS,),
                  in_specs=[pl.BlockSpec((1, t, LANE), lambda h: (h, 0, 0)), _col(t, lambda h: (0, Z_COL + h)),
                            pl.BlockSpec((1, LANE), lambda h: (0, 0))],
                  out_specs=_col(t, lambda h: (0, h)), semantics=("parallel",), vmem_limit=VMEM_LIMIT)(o, proj, gn)


def _apost_bwd(o, proj, gn, dmixed, *, name):
    t = proj.shape[0]

    def body(o_ref, z_ref, gn_ref, d_ref, do_ref, dz_ref, dgn_ref):
        ov = o_ref[0]
        z = z_ref[...]
        gnv = gn_ref[...]
        dv = d_ref[...]
        r = lax.rsqrt(jnp.mean(ov * ov, axis=-1, keepdims=True) + EPS)
        ohat = ov * r
        sig = _sigmoid(z)
        dy = dv * (z * sig)
        dz_ref[...] = (dv * ohat * gnv * sig * (1.0 + z * (1.0 - sig))).astype(BF16)
        dyo = dy * gnv
        do_ref[0] = r * (dyo - ohat * jnp.mean(dyo * ohat, axis=-1, keepdims=True))
        part = _colsum(dy * ohat)

        @pl.when(pl.program_id(0) == 0)
        def _():
            dgn_ref[...] = part

        @pl.when(pl.program_id(0) > 0)
        def _():
            dgn_ref[...] += part

    return _pcall(body, name=name,
                  out_shape=(jax.ShapeDtypeStruct((HEADS, t, LANE), F32), jax.ShapeDtypeStruct((t, A_DIM), BF16),
                             jax.ShapeDtypeStruct((1, LANE), F32)),
                  grid=(HEADS,),
                  in_specs=[pl.BlockSpec((1, t, LANE), lambda h: (h, 0, 0)), _col(t, lambda h: (0, Z_COL + h)),
                            pl.BlockSpec((1, LANE), lambda h: (0, 0)), _col(t, lambda h: (0, h))],
                  out_specs=(pl.BlockSpec((1, t, LANE), lambda h: (h, 0, 0)), _col(t, lambda h: (0, h)),
                             pl.BlockSpec((1, LANE), lambda h: (0, 0))),
                  semantics=("arbitrary",), vmem_limit=VMEM_LIMIT)(o, proj, gn, dmixed)


POOL_COL = (AB_COL + LANE) // LANE
CB_COL = POOL_COL + POOL_DIM // LANE
CC_COL = CB_COL + CONV_DIM // LANE
CH_COL = CC_COL + CONV_DIM // LANE
MAX_WIN_LOG2 = 4


def _window_sums(x, shift):
    sums = []
    cur = x
    for k in range(MAX_WIN_LOG2):
        cur = cur + shift(cur, 1 << k)
        sums.append(cur)
    return sums


def _pick_window(sums, win):
    out = sums[-1]
    for k in range(MAX_WIN_LOG2 - 2, -1, -1):
        out = jnp.where(win == float(2 << k), sums[k], out)
    return out


def _pool_counts(shape, win):
    row = lax.broadcasted_iota(jnp.int32, shape, 0).astype(F32)
    return jnp.minimum(row + 1.0, win)


def _pool_fwd(proj, win, wbd, scale, *, name):
    t = proj.shape[0]

    def body(x_ref, win_ref, w_ref, s_ref, y_ref):
        xv = x_ref[...]
        winv = win_ref[...]
        pooled = _pick_window(_window_sums(xv, _shift_down), winv) / _pool_counts(xv.shape, winv) - xv
        y_ref[...] = (_dot(pooled, w_ref[0], NN) * s_ref[...]).astype(BF16)

    nb = POOL_DIM // LANE
    vec = pl.BlockSpec((1, LANE), lambda b: (0, b))
    return _pcall(body, name=name, out_shape=jax.ShapeDtypeStruct((t, POOL_DIM), BF16), grid=(nb,),
                  in_specs=[_col(t, lambda b: (0, POOL_COL + b)), vec, pl.BlockSpec((1, LANE, LANE), lambda b: (b, 0, 0)), vec],
                  out_specs=_col(t, lambda b: (0, b)), semantics=("parallel",), vmem_limit=VMEM_LIMIT)(proj, win, wbd, scale)


def _pool_bwd(proj, win, wbd, scale, dmixed, *, name):
    t = proj.shape[0]

    def body(x_ref, win_ref, w_ref, s_ref, d_ref, dx_ref, dw_ref, ds_ref):
        xv = x_ref[...]
        winv = win_ref[...]
        cnt = _pool_counts(xv.shape, winv)
        pooled = _pick_window(_window_sums(xv, _shift_down), winv) / cnt - xv
        dv = d_ref[...]
        ds_ref[...] = _colsum(dv * _dot(pooled, w_ref[0], NN))
        dy0 = dv * s_ref[...]
        dw_ref[0] = _dot(pooled, dy0, TN)
        dpooled = _dot(dy0, w_ref[0], NT)
        dmean = dpooled / cnt
        dx_ref[...] = (_pick_window(_window_sums(dmean, _shift_up), winv) - dpooled).astype(BF16)

    nb = POOL_DIM // LANE
    vec = pl.BlockSpec((1, LANE), lambda b: (0, b))
    mat = pl.BlockSpec((1, LANE, LANE), lambda b: (b, 0, 0))
    first = A_DIM // LANE
    return _pcall(body, name=name,
                  out_shape=(jax.ShapeDtypeStruct((t, POOL_DIM), BF16), jax.ShapeDtypeStruct((nb, LANE, LANE), F32),
                             jax.ShapeDtypeStruct((1, POOL_DIM), F32)),
                  grid=(nb,),
                  in_specs=[_col(t, lambda b: (0, POOL_COL + b)), vec, mat, vec, _col(t, lambda b: (0, first + b))],
                  out_specs=(_col(t, lambda b: (0, b)), mat, vec), semantics=("parallel",),
                  vmem_limit=VMEM_LIMIT)(proj, win, wbd, scale, dmixed)


def _sconv_fwd(proj, w, *, name):
    t = proj.shape[0]

    def body(cb_ref, cc_ref, ch_ref, w_ref, y_ref):
        y_ref[...] = (cb_ref[...] * _conv_fwd(cc_ref[...] * ch_ref[...], w_ref, CONV_TAPS)).astype(BF16)

    nb = CONV_DIM // LANE
    return _pcall(body, name=name, out_shape=jax.ShapeDtypeStruct((t, CONV_DIM), BF16), grid=(nb,),
                  in_specs=[_col(t, lambda b: (0, CB_COL + b)), _col(t, lambda b: (0, CC_COL + b)),
                            _col(t, lambda b: (0, CH_COL + b)), pl.BlockSpec((CONV_TAPS, LANE), lambda b: (0, b))],
                  out_specs=_col(t, lambda b: (0, b)), semantics=("parallel",), vmem_limit=VMEM_LIMIT)(proj, proj, proj, w)


def _sconv_bwd(proj, w, dmixed, *, name):
    t = proj.shape[0]

    def body(cb_ref, cc_ref, ch_ref, w_ref, d_ref, dcb_ref, dcc_ref, dch_ref, dw_ref):
        cc = cc_ref[...]
        ch = ch_ref[...]
        u = cc * ch
        dv = d_ref[...]
        dcb_ref[...] = (dv * _conv_fwd(u, w_ref, CONV_TAPS)).astype(BF16)
        du = _conv_bwd(u, dv * cb_ref[...], w_ref, dw_ref, CONV_TAPS)
        dcc_ref[...] = (du * ch).astype(BF16)
        dch_ref[...] = (du * cc).astype(BF16)

    nb = CONV_DIM // LANE
    first = (A_DIM + POOL_DIM) // LANE
    act = jax.ShapeDtypeStruct((t, CONV_DIM), BF16)
    wspec = pl.BlockSpec((CONV_TAPS, LANE), lambda b: (0, b))
    ospec = _col(t, lambda b: (0, b))
    return _pcall(body, name=name, out_shape=(act, act, act, jax.ShapeDtypeStruct((CONV_TAPS, CONV_DIM), F32)), grid=(nb,),
                  in_specs=[_col(t, lambda b: (0, CB_COL + b)), _col(t, lambda b: (0, CC_COL + b)),
                            _col(t, lambda b: (0, CH_COL + b)), wspec, _col(t, lambda b: (0, first + b))],
                  out_specs=(ospec, ospec, ospec, wspec), semantics=("parallel",),
                  vmem_limit=VMEM_LIMIT)(proj, proj, proj, w, dmixed)


def _chunk_masks():
    r = lax.broadcasted_iota(jnp.int32, (CHUNK, CHUNK), 0)
    c = lax.broadcasted_iota(jnp.int32, (CHUNK, CHUNK), 1)
    return r >= c, r > c, jnp.where(r == c, 1.0, 0.0).astype(F32)


def _split(a):
    hi = a.astype(BF16)
    return hi, (a - hi.astype(F32)).astype(BF16)


def _dot_split(a, b, dims):
    (ah, al), (bh, bl) = a, b
    return _dot(ah, bh, dims) + _dot(ah, bl, dims) + _dot(al, bh, dims)


def _tri_inv(lows, eye):
    xs = [eye - low for low in lows]
    ps = [_split(low) for low in lows]
    ps = [_split(_dot_split(p, p, NN)) for p in ps]
    for i in range(5):
        xs = [x + _dot_split(_split(x), p, NN) for x, p in zip(xs, ps)]
        if i < 4:
            ps = [_split(_dot_split(p, p, NN)) for p in ps]
    return xs


def _prefix_sum_rows(x):
    for k in range(6):
        x = x + _shift_down(x, 1 << k)
    return x


def _suffix_sum_rows(x):
    for k in range(6):
        x = x + _shift_up(x, 1 << k)
    return x


def _chunk_decay(g, incl):
    gcb = _prefix_sum_rows(g)
    gtot = _colsum(g)
    col = gcb[:, :CHUNK]
    row = gcb.T[:CHUNK, :]
    decay = jnp.exp(jnp.where(incl, col - row, -1e30))
    return gcb, gtot, decay


CHUNKS_PER_STEP = 4


def _heads_of(ref, base, rows):
    return [ref[base + h, rows, :] for h in range(HEADS)]


def _chunk_rows(j):
    return pl.ds(j * CHUNK, CHUNK)


def _deltanet_prep(qkv, g, beta, *, name):
    t = qkv.shape[1]
    n_chunks = t // CHUNK
    per = CHUNKS_PER_STEP
    probs = [(j, h) for j in range(per) for h in range(HEADS)]

    def body(qkv_ref, g_ref, b_ref, u_ref, w_ref, qg_ref, kg_ref, attn_ref, tm_ref):
        incl, strict, eye = _chunk_masks()
        q = [qkv_ref[h, _chunk_rows(j), :] for j, h in probs]
        k = [qkv_ref[HEADS + h, _chunk_rows(j), :] for j, h in probs]
        v = [qkv_ref[2 * HEADS + h, _chunk_rows(j), :] for j, h in probs]
        bv = [b_ref[h, _chunk_rows(j), :] for j, h in probs]
        dec = [_chunk_decay(g_ref[h, _chunk_rows(j), :], incl) for j, h in probs]
        kb = [a * b for a, b in zip(k, bv)]
        low = [jnp.where(strict, _dot(a, b, NT) * d[2], 0.0) for a, b, d in zip(kb, k, dec)]
        tm = _tri_inv(low, eye)
        egc = [jnp.exp(d[0]) for d in dec]
        u = [_dot(m, a * b, NN) for m, a, b in zip(tm, v, bv)]
        w = [_dot(m, a * e, NN) for m, a, e in zip(tm, kb, egc)]
        attn = [_dot(a, b, NT) * d[2] for a, b, d in zip(q, k, dec)]
        for i, (j, h) in enumerate(probs):
            rows = _chunk_rows(j)
            u_ref[h, rows, :] = u[i]
            w_ref[h, rows, :] = w[i].astype(BF16)
            qg_ref[h, rows, :] = (q[i] * egc[i]).astype(BF16)
            kg_ref[h, rows, :] = (k[i] * jnp.exp(dec[i][1] - dec[i][0])).astype(BF16)
            attn_ref[j, h] = attn[i].astype(BF16)
            tm_ref[j, h] = tm[i]

    act = lambda heads: pl.BlockSpec((heads, per * CHUNK, LANE), lambda n: (0, n, 0))
    mat = pl.BlockSpec((per, HEADS, CHUNK, CHUNK), lambda n: (n, 0, 0, 0))
    return _pcall(
        body, name=name,
        out_shape=(jax.ShapeDtypeStruct((HEADS, t, LANE), F32),) + (jax.ShapeDtypeStruct((HEADS, t, LANE), BF16),) * 3
        + (jax.ShapeDtypeStruct((n_chunks, HEADS, CHUNK, CHUNK), BF16), jax.ShapeDtypeStruct((n_chunks, HEADS, CHUNK, CHUNK), F32)),
        grid=(n_chunks // per,), in_specs=[act(3 * HEADS), act(HEADS), act(HEADS)],
        out_specs=(act(HEADS),) * 4 + (mat, mat), semantics=("parallel",), vmem_limit=VMEM_LIMIT)(qkv, g, beta)


SCAN_CHUNKS_PER_STEP = 8


def _deltanet_scan(u, w, qg, kg, attn, g, *, name):
    t = u.shape[1]
    n_chunks = t // CHUNK
    per = SCAN_CHUNKS_PER_STEP

    def body(u_ref, w_ref, qg_ref, kg_ref, attn_ref, g_ref, o_ref, vn_ref, st_ref, s_ref):
        @pl.when(pl.program_id(0) == 0)
        def _():
            s_ref[...] = jnp.zeros_like(s_ref)

        for j in range(per):
            rows = _chunk_rows(j)
            s = [s_ref[h] for h in range(HEADS)]
            vn = [u_ref[h, rows, :] - _dot(w_ref[h, rows, :], s[h], NN) for h in range(HEADS)]
            o = [_dot(qg_ref[h, rows, :], s[h], NN) + _dot(attn_ref[j, h], vn[h], NN) for h in range(HEADS)]
            eg = [jnp.exp(_colsum(g_ref[h, rows, :])) for h in range(HEADS)]
            for h in range(HEADS):
                st_ref[j, h] = s[h]
                s_ref[h] = s[h] * eg[h] + _dot(kg_ref[h, rows, :], vn[h], TN)
                o_ref[h, rows, :] = o[h]
                vn_ref[h, rows, :] = vn[h]

    act = pl.BlockSpec((HEADS, per * CHUNK, LANE), lambda n: (0, n, 0))
    out = jax.ShapeDtypeStruct((HEADS, t, LANE), F32)
    return _pcall(
        body, name=name, out_shape=(out, out, jax.ShapeDtypeStruct((n_chunks, HEADS, LANE, LANE), F32)), grid=(n_chunks // per,),
        in_specs=[act] * 4 + [pl.BlockSpec((per, HEADS, CHUNK, CHUNK), lambda n: (n, 0, 0, 0)), act],
        out_specs=(act, act, pl.BlockSpec((per, HEADS, LANE, LANE), lambda n: (n, 0, 0, 0))),
        scratch_shapes=[pltpu.VMEM((HEADS, LANE, LANE), F32)], semantics=("arbitrary",))(u, w, qg, kg, attn, g)


def _deltanet_bscan(w, qg, kg, attn, g, do, *, name):
    t = w.shape[1]
    n_chunks = t // CHUNK
    per = SCAN_CHUNKS_PER_STEP
    steps = n_chunks // per

    def body(w_ref, qg_ref, kg_ref, attn_ref, g_ref, do_ref, dvn_ref, dsn_ref, ds_ref):
        @pl.when(pl.program_id(0) == 0)
        def _():
            ds_ref[...] = jnp.zeros_like(ds_ref)

        for j in reversed(range(per)):
            rows = _chunk_rows(j)
            dsn = [ds_ref[h] for h in range(HEADS)]
            dov = [do_ref[h, rows, :] for h in range(HEADS)]
            dvn = [_dot(attn_ref[j, h], dov[h], TN) + _dot(kg_ref[h, rows, :], dsn[h], NN) for h in range(HEADS)]
            eg = [jnp.exp(_colsum(g_ref[h, rows, :])) for h in range(HEADS)]
            for h in range(HEADS):
                dsn_ref[j, h] = dsn[h]
                ds_ref[h] = _dot(qg_ref[h, rows, :], dov[h], TN) + eg[h] * dsn[h] - _dot(w_ref[h, rows, :], dvn[h], TN)
                dvn_ref[h, rows, :] = dvn[h]

    act = pl.BlockSpec((HEADS, per * CHUNK, LANE), lambda n: (0, steps - 1 - n, 0))
    return _pcall(
        body, name=name,
        out_shape=(jax.ShapeDtypeStruct((HEADS, t, LANE), F32), jax.ShapeDtypeStruct((n_chunks, HEADS, LANE, LANE), F32)),
        grid=(steps,),
        in_specs=[act] * 3 + [pl.BlockSpec((per, HEADS, CHUNK, CHUNK), lambda n: (steps - 1 - n, 0, 0, 0)), act, act],
        out_specs=(act, pl.BlockSpec((per, HEADS, LANE, LANE), lambda n: (steps - 1 - n, 0, 0, 0))),
        scratch_shapes=[pltpu.VMEM((HEADS, LANE, LANE), F32)], semantics=("arbitrary",))(w, qg, kg, attn, g, do)


def _sum_all(x):
    return jnp.sum(jnp.sum(x, axis=1, keepdims=True), axis=0, keepdims=True)


def _rowsum(x):
    return jnp.sum(x, axis=1, keepdims=True)


def _deltanet_post(qkv, g, beta, tmats, states, dstates, do, dvn, vn, *, name):
    t = qkv.shape[1]
    n_chunks = t // CHUNK
    per = CHUNKS_PER_STEP
    probs = [(j, h) for j in range(per) for h in range(HEADS)]

    def body(qkv_ref, g_ref, b_ref, tm_ref, st_ref, dsn_ref, do_ref, dvn_ref, vn_ref, dqkv_ref, dg_ref, db_ref):
        incl, strict, _ = _chunk_masks()
        ones = jnp.ones((CHUNK, LANE), BF16)
        last_row = lax.broadcasted_iota(jnp.int32, (CHUNK, LANE), 0) == CHUNK - 1
        z = lambda f, *cols: [f(*a) for a in zip(*cols)]
        q = [qkv_ref[h, _chunk_rows(j), :] for j, h in probs]
        k = [qkv_ref[HEADS + h, _chunk_rows(j), :] for j, h in probs]
        v = [qkv_ref[2 * HEADS + h, _chunk_rows(j), :] for j, h in probs]
        bv = [b_ref[h, _chunk_rows(j), :] for j, h in probs]
        dov = [do_ref[h, _chunk_rows(j), :] for j, h in probs]
        dvn_ = [dvn_ref[h, _chunk_rows(j), :] for j, h in probs]
        vn_ = [vn_ref[h, _chunk_rows(j), :] for j, h in probs]
        tm = [tm_ref[j, h] for j, h in probs]
        s = [st_ref[j, h] for j, h in probs]
        dsn = [dsn_ref[j, h] for j, h in probs]
        dec = [_chunk_decay(g_ref[h, _chunk_rows(j), :], incl) for j, h in probs]
        decay = [d[2] for d in dec]
        egc = [jnp.exp(d[0]) for d in dec]
        ekg = [jnp.exp(d[1] - d[0]) for d in dec]
        kb = z(lambda a, b: a * b, k, bv)
        vb = z(lambda a, b: a * b, v, bv)
        kbg = z(lambda a, b: a * b, kb, egc)
        qg = z(lambda a, b: a * b, q, egc)
        kg = z(lambda a, b: a * b, k, ekg)
        kk = z(lambda a, b: _dot(a, b, NT), kb, k)
        qk = z(lambda a, b: _dot(a, b, NT), q, k)
        dattn = z(lambda a, b: jnp.where(incl, _dot(a, b, NT), 0.0), dov, vn_)
        dqg = z(lambda a, b: _dot(a, b, NT), dov, s)
        dkg = z(lambda a, b: _dot(a, b, NT), vn_, dsn)
        dglast = z(lambda a, b, c, d, e: _sum_all(a * b) * jnp.exp(e[1]) + _sum_all(c * d), s, dsn, dkg, kg, dec)
        dw = z(lambda a, b: -_dot(a, b, NT), dvn_, s)
        dtm = z(lambda a, b, c, d: _dot(a, b, NT) + _dot(c, d, NT), dvn_, vb, dw, kbg)
        dvb = z(lambda a, b: _dot(a, b, TN), tm, dvn_)
        dkbg = z(lambda a, b: _dot(a, b, TN), tm, dw)
        dlow = z(lambda a, b: jnp.where(strict, -_dot(_dot(a, b, TN), a, NT), 0.0), tm, dtm)
        dkk = z(lambda a, b: a * b, dlow, decay)
        dqk = z(lambda a, b: a * b, dattn, decay)
        dkb = z(lambda a, b, c, d: _dot(a, b, NN) + c * d, dkk, k, dkbg, egc)
        dk = z(lambda a, b, c, d, e, f, g_, h_: _dot(a, b, TN) + _dot(c, d, TN) + e * f + g_ * h_, dkk, kb, dqk, q, dkg, ekg, dkb, bv)
        dq = z(lambda a, b, c, d: _dot(a, b, NN) + c * d, dqk, k, dqg, egc)
        m = z(lambda a, b, c, d, e: (a * b + c * d) * e, dlow, kk, dattn, qk, decay)
        mcol = [_dot(mh, ones, TN) + _dot(ml, ones, TN) for mh, ml in (_split(a) for a in m)]
        for i, (j, h) in enumerate(probs):
            rows = _chunk_rows(j)
            dqkv_ref[h, rows, :] = dq[i]
            dqkv_ref[HEADS + h, rows, :] = dk[i]
            dqkv_ref[2 * HEADS + h, rows, :] = dvb[i] * bv[i]
            db_ref[h, rows, :] = jnp.broadcast_to(_rowsum(dkb[i] * k[i] + dvb[i] * v[i]), (CHUNK, LANE))
            dgc = (_rowsum(dqg[i] * qg[i] + dkbg[i] * kbg[i] - dkg[i] * kg[i]) + _rowsum(m[i]) - mcol[i]
                   + jnp.where(last_row, dglast[i], 0.0))
            dg_ref[h, rows, :] = _suffix_sum_rows(dgc)

    act = lambda heads: pl.BlockSpec((heads, per * CHUNK, LANE), lambda n: (0, n, 0))
    mat = lambda d: pl.BlockSpec((per, HEADS, d, d), lambda n: (n, 0, 0, 0))
    out = jax.ShapeDtypeStruct((HEADS, t, LANE), F32)
    return _pcall(
        body, name=name, out_shape=(jax.ShapeDtypeStruct((3 * HEADS, t, LANE), F32), out, out), grid=(n_chunks // per,),
        in_specs=[act(3 * HEADS), act(HEADS), act(HEADS), mat(CHUNK), mat(LANE), mat(LANE), act(HEADS), act(HEADS), act(HEADS)],
        out_specs=(act(3 * HEADS), act(HEADS), act(HEADS)), semantics=("parallel",),
        vmem_limit=VMEM_LIMIT)(qkv, g, beta, tmats, states, dstates, do, dvn, vn)


ANY = pl.BlockSpec(memory_space=pl.ANY)
PEERS = N_DEV - 1


def _all_gather(arrays, *, name):
    n = len(arrays)

    def body(*refs):
        ins, outs = refs[:n], refs[n:2 * n]
        send_sems, recv_sems, local_sems = refs[2 * n:]
        x, y, c = lax.axis_index("x"), lax.axis_index("y"), lax.axis_index("c")
        me, sibling = (x, y, c), (x, y, 1 - c)
        chips = [(1 - x, y), (x, 1 - y), (1 - x, 1 - y)]

        def copy(a, k, block, to, src=None):
            dst = outs[a].at[4 * block[0] + 2 * block[1] + block[2]]
            return pltpu.make_async_remote_copy(src_ref=dst if src is None else src, dst_ref=dst, send_sem=send_sems.at[a * PEERS + k],
                                                recv_sem=recv_sems.at[a * PEERS + k], device_id=to, device_id_type=MESH)

        local = [pltpu.make_async_copy(ins[a], outs[a].at[4 * x + 2 * y + c], local_sems.at[a]) for a in range(n)]
        for cp in local:
            cp.start()
        first = []
        for a in range(n):
            first += [copy(a, 1 + j, me, (*chip, c), src=ins[a]) for j, chip in enumerate(chips)]
            first.append(copy(a, 0, me, sibling, src=ins[a]))
        for cp in first:
            cp.start()
        passed = []
        for a in range(n):
            for j, chip in enumerate(chips):
                copy(a, 1 + j, (*chip, c), me).wait_recv()
                fwd = copy(a, 4 + j, (*chip, c), sibling)
                fwd.start()
                passed.append(fwd)
        for a in range(n):
            copy(a, 0, sibling, me).wait_recv()
            for j, chip in enumerate(chips):
                copy(a, 4 + j, (*chip, 1 - c), me).wait_recv()
        for cp in first + passed:
            cp.wait_send()
        for cp in local:
            cp.wait()

    return _pcall(body, name=name, out_shape=tuple(jax.ShapeDtypeStruct((N_DEV,) + a.shape, a.dtype) for a in arrays),
                  in_specs=[ANY] * n, out_specs=(ANY,) * n,
                  scratch_shapes=[pltpu.SemaphoreType.DMA((n * PEERS,)), pltpu.SemaphoreType.DMA((n * PEERS,)),
                                  pltpu.SemaphoreType.DMA((n,))])(*arrays)


CHIPS = 4


def _pair_exchange(arrays, *, name):
    n = len(arrays)

    def body(*refs):
        ins, outs = refs[:n], refs[n:2 * n]
        send_sems, recv_sems = refs[2 * n:]
        x, y, c = lax.axis_index("x"), lax.axis_index("y"), lax.axis_index("c")
        copies = []
        for a in range(n):
            for q in range(CHIPS):
                cp = pltpu.make_async_remote_copy(src_ref=ins[a].at[2 * q + 1 - c], dst_ref=outs[a].at[q],
                                                  send_sem=send_sems.at[a * CHIPS + q], recv_sem=recv_sems.at[a * CHIPS + q],
                                                  device_id=(x, y, 1 - c), device_id_type=MESH)
                cp.start()
                copies.append(cp)
        for cp in copies:
            cp.wait()

    return _pcall(body, name=name, out_shape=tuple(jax.ShapeDtypeStruct((CHIPS,) + a.shape[1:], a.dtype) for a in arrays),
                  in_specs=[ANY] * n, out_specs=(ANY,) * n,
                  scratch_shapes=[pltpu.SemaphoreType.DMA((n * CHIPS,)), pltpu.SemaphoreType.DMA((n * CHIPS,))])(*arrays)


def _pair_add(blocks, theirs, *, name):
    _, r, c_ = blocks.shape
    tr = _tile(r, 512, 16)

    def body(mine_ref, theirs_ref, o_ref):
        core = lax.axis_index("c")
        own = jnp.where(core == 0, mine_ref[0, 0].astype(F32), mine_ref[0, 1].astype(F32))
        o_ref[0] = (own + theirs_ref[0].astype(F32)).astype(o_ref.dtype)

    spec = pl.BlockSpec((1, tr, c_), lambda q, i: (q, i, 0))
    return _pcall(body, name=name, out_shape=jax.ShapeDtypeStruct(theirs.shape, theirs.dtype), grid=(CHIPS, r // tr),
                  in_specs=[pl.BlockSpec((1, 2, tr, c_), lambda q, i: (q, 0, i, 0)), spec], out_specs=spec,
                  semantics=("parallel", "parallel"), vmem_limit=VMEM_LIMIT)(blocks.reshape(CHIPS, 2, r, c_), theirs)


HBM = pl.BlockSpec(memory_space=pltpu.HBM)
SEM = pl.BlockSpec(memory_space=pltpu.SEMAPHORE)
EFFECT = pltpu.SideEffectType.DATAFLOW_SIDE_EFFECTING


GATHER, CHIP_GATHER, CHIP_SCATTER = "gather", "chip_gather", "chip_scatter"
PEERS_OF = {GATHER: N_DEV - 1, CHIP_GATHER: CHIPS - 1, CHIP_SCATTER: CHIPS - 1}


def _direct_copies(srcs, lands, send_sems, recv_sems, local_sems, kind):
    x, y, c = lax.axis_index("x"), lax.axis_index("y"), lax.axis_index("c")
    peers = PEERS_OF[kind]
    mine = 2 * x + y if kind == CHIP_SCATTER else 4 * x + 2 * y + c
    copies = []
    for a, (src, land) in enumerate(zip(srcs, lands)):
        copies.append(pltpu.make_async_copy(src.at[mine] if kind == CHIP_SCATTER else src, land.at[mine], local_sems.at[a]))
        for k in range(1, peers + 1):
            bits = k if kind == GATHER else 2 * k
            px = 1 - x if bits & 4 else x
            py = 1 - y if bits & 2 else y
            pc = 1 - c if bits & 1 else c
            copies.append(pltpu.make_async_remote_copy(
                src_ref=src.at[2 * px + py] if kind == CHIP_SCATTER else src, dst_ref=land.at[mine],
                send_sem=send_sems.at[a * peers + k - 1], recv_sem=recv_sems.at[a * peers + k - 1],
                device_id=(px, py, pc), device_id_type=MESH))
    return copies


def _pair_swap(arrays, *, name):
    n = len(arrays)

    def body(*refs):
        mine, zones = refs[:n], refs[n:2 * n]
        send_sems, recv_sems = refs[2 * n:]
        x, y, c = lax.axis_index("x"), lax.axis_index("y"), lax.axis_index("c")
        copies = []
        for a in range(n):
            for q in range(CHIPS):
                copies.append(pltpu.make_async_remote_copy(
                    src_ref=mine[a].at[2 * q + c], dst_ref=zones[a].at[2 * q + c], send_sem=send_sems.at[a * CHIPS + q],
                    recv_sem=recv_sems.at[a * CHIPS + q], device_id=(x, y, 1 - c), device_id_type=MESH))
        for cp in copies:
            cp.start()
        for cp in copies:
            cp.wait()

    return _pcall(body, name=name, out_shape=tuple(jax.ShapeDtypeStruct(a.shape, a.dtype) for a in arrays),
                  in_specs=[ANY] * n, out_specs=(ANY,) * n, input_output_aliases={i: i for i in range(n)},
                  scratch_shapes=[pltpu.SemaphoreType.DMA((n * CHIPS,)), pltpu.SemaphoreType.DMA((n * CHIPS,))])(*arrays)


def _exchange_start(groups, kind, *, name, after=None):
    srcs = [s for group in groups for s in group]
    n = len(srcs)
    sizes = [len(group) for group in groups]
    starts = [sum(sizes[:g]) for g in range(len(groups))]
    land_shapes = [s.shape if kind == CHIP_SCATTER else (N_DEV,) + s.shape for s in srcs]
    peers = PEERS_OF[kind]
    extra = [] if after is None else [after]

    def body(*refs):
        srcs_, lands = refs[:n], refs[n:2 * n]
        token = refs[-1]
        sem_refs = refs[2 * n + len(extra):]
        for g, (at, size) in enumerate(zip(starts, sizes)):
            send_sems, recv_sems, local_sems = sem_refs[3 * g:3 * g + 3]
            for cp in _direct_copies(srcs_[at:at + size], lands[at:at + size], send_sems, recv_sems, local_sems, kind):
                cp.start()
        token[...] = jnp.zeros_like(token)

    sems = tuple(t for size in sizes for t in (pltpu.SemaphoreType.DMA((size * peers,)), pltpu.SemaphoreType.DMA((size * peers,)),
                                               pltpu.SemaphoreType.DMA((size,))))
    thru = tuple(pltpu.HBM(s.shape, s.dtype) for s in srcs) + tuple(pltpu.HBM(shp, s.dtype) for shp, s in zip(land_shapes, srcs))
    ins = [pltpu.with_memory_space_constraint(s, pltpu.HBM) for s in srcs]
    ins += [pltpu.with_memory_space_constraint(lax.empty(shp, s.dtype), pltpu.HBM) for shp, s in zip(land_shapes, srcs)]
    out = pl.pallas_call(
        body, name=name, out_shape=sems + thru + (jax.ShapeDtypeStruct((SUBLANE, LANE), F32),),
        in_specs=[HBM] * (2 * n) + [ANY] * len(extra),
        out_specs=(SEM,) * len(sems) + (HBM,) * (2 * n) + (pl.BlockSpec(memory_space=pltpu.VMEM),),
        input_output_aliases={i: len(sems) + i for i in range(2 * n)},
        compiler_params=pltpu.CompilerParams(has_side_effects=EFFECT))(*ins, *extra)
    arrays = out[len(sems):-1]
    started = [tuple(out[3 * g:3 * g + 3]) + tuple(arrays[at:at + size]) + tuple(arrays[n + at:n + at + size])
               for g, (at, size) in enumerate(zip(starts, sizes))]
    return started, out[-1]


def _exchange_wait(started, after, kind, *, name):
    n = (len(started) - 3) // 2
    sems, arrays = started[:3], started[3:]

    def body(*refs):
        srcs_, lands = refs[:n], refs[n:2 * n]
        send_sems, recv_sems, local_sems = refs[2 * n:2 * n + 3]
        for cp in _direct_copies(srcs_, lands, send_sems, recv_sems, local_sems, kind):
            cp.wait()

    out = pl.pallas_call(
        body, name=name, out_shape=tuple(pltpu.HBM(a.shape, a.dtype) for a in arrays),
        in_specs=[HBM] * (2 * n) + [SEM] * 3 + [ANY], out_specs=(HBM,) * (2 * n),
        input_output_aliases={i: i for i in range(2 * n)},
        compiler_params=pltpu.CompilerParams(has_side_effects=EFFECT))(*arrays, *sems, after)
    return out[n:]


def _adamw_reduce(w, parts, m, v, *, name, after=None):
    layers, r, c = w.shape
    assert len(parts) == layers
    senders = parts[0].shape[0]
    tr = _tile(r, 512, 16)
    tiles = r // tr
    bc1 = 1.0 - ADAM_B1 ** ADAM_STEP
    bc2 = 1.0 - ADAM_B2 ** ADAM_STEP

    def body(w_ref, *rest):
        p_refs = rest[:layers]
        m_ref, v_ref, g_ref, d_ref, nm_ref, nv_ref = rest[layers:]

        def update(p_ref):
            g = p_ref[0, :, pl.ds(0, c)].astype(F32)
            for s in range(1, senders):
                g = g + p_ref[s, :, pl.ds(0, c)].astype(F32)
            nm = ADAM_B1 * m_ref[0] + (1.0 - ADAM_B1) * g
            nv = ADAM_B2 * v_ref[0] + (1.0 - ADAM_B2) * (g * g)
            g_ref[0] = g
            nm_ref[0] = nm
            nv_ref[0] = nv
            d_ref[0] = -ADAM_LR * ((nm / bc1) / (jnp.sqrt(nv / bc2) + ADAM_EPS) + ADAM_WD * w_ref[0])

        for layer in range(layers):
            pl.when(pl.program_id(0) == layer)(functools.partial(update, p_refs[layer]))

    def part_spec(layer, shape):
        rest = 0 if layer > 0 else tiles - 1
        return pl.BlockSpec((senders, tr, shape[2]), lambda l, i: (0, jnp.where(l == layer, i, rest), 0))

    spec = pl.BlockSpec((1, tr, c), lambda l, i: (l, i, 0))
    out = jax.ShapeDtypeStruct((layers, r, c), F32)
    return _pcall(body, name=name, out_shape=(out,) * 4, grid=(layers, tiles),
                  in_specs=[spec] + [part_spec(layer, p.shape) for layer, p in enumerate(parts)] + [spec, spec],
                  out_specs=(spec,) * 4, semantics=("arbitrary", "arbitrary"), vmem_limit=VMEM_LIMIT, after=after)(w, *parts, m, v)


def _pool_windows():
    return jnp.repeat(jnp.asarray(POOL_WINDOWS, F32), POOL_DIM // len(POOL_WINDOWS))[None, :]


def _block_diag_pairs(pool_w):
    z = jnp.zeros_like(pool_w[0])
    return jnp.stack([jnp.block([[pool_w[2 * b], z], [z, pool_w[2 * b + 1]]]) for b in range(2)])


def _pad_lanes(vec):
    return jnp.zeros((1, LANE), F32).at[0, :vec.shape[0]].set(vec)


FF_SHARD = D_FF // N_DEV
FF_BLOCK = 384
D_FF_PAD = N_DEV * FF_BLOCK


def _layer_fwd(x, p_i, wt, fetch):
    wt = {**wt, **fetch(0, x)}
    proj, h1 = _matmul(x, wt["w_in"], "nt", norm_g=wt["norm1_g"], name="mm_in")
    qkv = _qkv_prep_fwd(proj, wt["conv_qkv"], name="qkv_prep_fwd")
    g, beta = _gates_fwd(proj, wt["a_log"], wt["dt_bias"], name="gates_fwd")
    u, w, qg, kg, attn, tmats = _deltanet_prep(qkv, g, beta, name="deltanet_prep")
    o, vn, states = _deltanet_scan(u, w, qg, kg, attn, g, name="deltanet_scan")
    wt.update(fetch(1, o))
    o_a = _apost_fwd(o, proj, wt["onorm_g"], name="apost_fwd")
    o_b = _pool_fwd(proj, wt["pool_win"], wt["pool_wbd"], wt["pool_scale"], name="pool_fwd")
    o_c = _sconv_fwd(proj, wt["sconv_w"], name="sconv_fwd")
    mixed = jnp.concatenate([o_a, o_b, o_c], axis=1)
    x1 = _matmul(mixed, wt["w_out"], "nn", res=x, name="mm_out")
    wt.update(fetch(2, x1))
    ff, gate, up, h2 = _swiglu_fwd(x1, wt["norm2_g"], wt["w_gate"], wt["w_up"], name="swiglu_fwd")
    wt.update(fetch(3, ff))
    x2 = _matmul(ff, wt["w_down"], "nn", res=x1, name="mm_down")
    wt.update(fetch(4, x2))
    pgl = _matmul(x2, wt["ple_gate"], "nn", name="mm_pleg")
    pp = _matmul(p_i, wt["ple_proj"], "nn", b_blocked=True, name="mm_plep")
    x3 = _ple_fwd(x2, pgl, pp, name="ple_fwd")
    saved = dict(x=x, h1=h1, proj=proj, qkv=qkv, g=g, beta=beta, o=o, states=states, tmats=tmats, mixed=mixed, x1=x1, h2=h2,
                 gate=gate, up=up, ff=ff, x2=x2, pgl=pgl, pp=pp, p=p_i, w=w, qg=qg, kg=kg, attn=attn, vn=vn, wt=wt)
    return x3, saved


def _col_blocks(g):
    a = g.shape[0]
    return jnp.transpose(g.reshape(a, N_DEV, -1), (1, 0, 2))


def _cols_joined(blocks):
    return jnp.transpose(blocks, (1, 0, 2)).reshape(blocks.shape[1], -1)


def _layer_bwd(dx3, sv, emit, after=None):
    gr, big = {}, {}
    wt = sv["wt"]
    rows = D_MODEL // N_DEV
    dpgl, dpp = _ple_bwd(dx3, sv["pgl"], sv["pp"], name="ple_bwd", after=after)
    big["ple_proj"] = _matmul(sv["p"], dpp, "tn", out_blocked=(N_DEV, rows), out_dtype=BF16, name="mm_dplep")
    big["ple_gate"] = _matmul(sv["x2"], dpgl, "tn", out_dtype=BF16, name="mm_dpleg").reshape(N_DEV, rows, D_MODEL)
    dx2 = _matmul(dpgl, wt["ple_gate"], "nt", res=dx3, name="mm_dx2")
    big["w_down"] = _matmul(sv["ff"], dx2, "tn", out_dtype=BF16, name="mm_ddown").reshape(N_DEV, FF_BLOCK, D_MODEL)
    dgate, dup = _swiglu_bwd(dx2, wt["w_down"], sv["gate"], sv["up"], name="swiglu_bwd", after=emit(0, big))
    big["w_gate"] = _matmul(dgate, sv["h2"], "tn", out_dtype=BF16, name="mm_dgate").reshape(N_DEV, FF_BLOCK, D_MODEL)
    big["w_up"] = _matmul(dup, sv["h2"], "tn", out_dtype=BF16, name="mm_dup").reshape(N_DEV, FF_BLOCK, D_MODEL)
    dh2 = _matmul(dgate, wt["w_gate"], "nn", name="mm_dh2_gate")
    dh2 = _matmul(dup, wt["w_up"], "nn", res=dh2, name="mm_dh2_up")
    dx1, gr["norm2_g"] = _rmsnorm_bwd(sv["x1"], wt["norm2_g"], dh2, dx2, name="rmsnorm_bwd")
    big["w_out"] = _matmul(sv["mixed"], dx1, "tn", out_dtype=BF16, name="mm_dout").reshape(N_DEV, rows, D_MODEL)
    dmixed = _matmul(dx1, wt["w_out"], "nt", name="mm_dmixed", after=emit(1, big))
    proj = sv["proj"]
    dcb, dcc, dch, dsconv = _sconv_bwd(proj, wt["sconv_w"], dmixed, name="sconv_bwd")
    big["sconv_w"] = _col_blocks(dsconv)
    dhp, dwbd, gr["pool_scale"] = _pool_bwd(proj, wt["pool_win"], wt["pool_wbd"], wt["pool_scale"], dmixed, name="pool_bwd")
    half = LANE // 2
    gr["pool_w"] = jnp.stack([dwbd[0, :half, :half], dwbd[0, half:, half:], dwbd[1, :half, :half], dwbd[1, half:, half:]])
    do, dz, gr["onorm_g"] = _apost_bwd(sv["o"], proj, wt["onorm_g"], dmixed, name="apost_bwd")
    dvn, dstates = _deltanet_bscan(sv["w"], sv["qg"], sv["kg"], sv["attn"], sv["g"], do, name="deltanet_bscan")
    dqkv_h, dg, dbeta = _deltanet_post(sv["qkv"], sv["g"], sv["beta"], sv["tmats"], sv["states"], dstates, do, dvn, sv["vn"],
                                       name="deltanet_post")
    dab, dalog, ddtb = _gates_bwd(proj, wt["a_log"], wt["dt_bias"], dg, dbeta, name="gates_bwd")
    gr["a_log"], gr["dt_bias"] = dalog[0, :HEADS], ddtb[0, :HEADS]
    dqkv, dconv = _qkv_prep_bwd(proj, wt["conv_qkv"], dqkv_h, name="qkv_prep_bwd")
    big["conv_qkv"] = _col_blocks(dconv)
    dproj = jnp.concatenate([dqkv, dz, dab, dhp, dcb, dcc, dch], axis=1)
    dwin = _matmul(dproj, sv["h1"], "tn", out_dtype=BF16, name="mm_din")
    big["w_in"] = jnp.concatenate([dwin[:AB_COL + 2 * HEADS], dwin[AB_COL + LANE:]], axis=0).reshape(N_DEV, -1, D_MODEL)
    dh1 = _matmul(dproj, wt["w_in"], "nn", name="mm_dh1", after=emit(2, big))
    dx, gr["norm1_g"] = _rmsnorm_bwd(sv["x"], wt["norm1_g"], dh1, dx1, name="rmsnorm_bwd")
    return dx, gr


FETCH_GROUPS = (("w_in", "conv_qkv", "sconv_w"), ("w_out",), ("w_gate", "w_up"), ("w_down",), ("ple_gate", "ple_proj"))
EMIT_GROUPS = (("ple_proj", "ple_gate", "w_down"), ("w_gate", "w_up", "w_out"), ("w_in", "conv_qkv", "sconv_w"))


def _small_weights(w, i):
    return dict(
        norm1_g=w["norm1_g"][i][None], norm2_g=w["norm2_g"][i][None], onorm_g=w["onorm_g"][i][None],
        a_log=_pad_lanes(w["a_log"][i]), dt_bias=_pad_lanes(w["dt_bias"][i]),
        pool_scale=w["pool_scale"][i][None], pool_win=_pool_windows(), pool_wbd=_block_diag_pairs(w["pool_w"][i]))


def _as_read(name, gathered):
    if name == "w_in":
        rows = gathered[:, :D_IN // N_DEV].reshape(-1, D_MODEL)
        return jnp.concatenate([rows[:AB_COL + 2 * HEADS], jnp.zeros((LANE - 2 * HEADS, D_MODEL), BF16),
                                rows[AB_COL + 2 * HEADS:]], axis=0)
    if name in ("conv_qkv", "sconv_w"):
        return _cols_joined(gathered)
    if name == "ple_proj":
        return gathered
    return gathered.reshape(-1, D_MODEL)


def _layer_weights(gathered, w, i):
    return {**_small_weights(w, i), **{k: _as_read(k, g) for k, g in gathered.items()}}


def _local_step(x, p, target, layers, final_g):
    saved = []
    h = x
    for i in range(DEPTH):
        replicated = {k: v for k, v in layers[i].items() if k not in SHARDED}
        h, sv = _layer_fwd(h, p[i], replicated, lambda group, after, i=i: {k: layers[i][k] for k in FETCH_GROUPS[group]})
        saved.append(sv)
    dx, dgf, loss = _loss_head(h, final_g, target, name="loss_head")
    big, small = [{} for _ in range(DEPTH)], [None] * DEPTH
    for i in reversed(range(DEPTH)):
        dx, small[i] = _layer_bwd(dx, saved[i], lambda group, blocks, i=i: big[i].update({k: blocks[k] for k in EMIT_GROUPS[group]}))
    return loss, dx, big, small, dgf


SHARDED = ("w_in", "w_gate", "w_up", "w_down", "w_out", "ple_gate", "ple_proj", "conv_qkv", "sconv_w")
SMALL = ("norm1_g", "a_log", "dt_bias", "onorm_g", "pool_w", "pool_scale", "norm2_g", "final_g")
SLAB_COLS = 1024


def _payload(name, shard):
    if name in ("conv_qkv", "sconv_w"):
        return shard
    out = shard.astype(BF16)
    if name in ("w_gate", "w_up", "w_down"):
        out = jnp.pad(out, ((0, FF_BLOCK - FF_SHARD), (0, 0)))
    if name == "w_in":
        out = jnp.pad(out, ((0, -out.shape[0] % (2 * SUBLANE)), (0, 0)))
    return out


TRANSPOSED = ("w_in", "w_gate", "w_up")


def _ff_rows(t):
    return jnp.transpose(t, (0, 2, 1))


def _slab_rows(shape):
    size = 1
    for s in shape:
        size *= s
    return SUBLANE * -(-size // (SUBLANE * SLAB_COLS))


def _pack_slab(parts, extra_row):
    rows = []
    for name in SMALL:
        flat = parts[name].reshape(-1)
        nrow = _slab_rows(parts[name].shape)
        rows.append(jnp.pad(flat, (0, nrow * SLAB_COLS - flat.shape[0])).reshape(nrow, SLAB_COLS))
    rows.append(jnp.pad(extra_row, ((0, SUBLANE - 1), (0, 0))))
    return jnp.concatenate(rows, axis=0)


def _unpack_slab(slab, shapes):
    out, row = {}, 0
    for name in SMALL:
        size = 1
        for s in shapes[name]:
            size *= s
        out[name] = slab[row:row + _slab_rows(shapes[name])].reshape(-1)[:size].reshape(shapes[name])
        row += _slab_rows(shapes[name])
    return out, row


def kernel(x, p, norm1_g, w_in, conv_qkv, a_log, dt_bias, onorm_g, pool_w, pool_scale, sconv_w, w_out, norm2_g, w_gate, w_up, w_down, ple_proj, ple_gate, final_g, loss_target, m_norm1_g, m_w_in, m_conv_qkv, m_a_log, m_dt_bias, m_onorm_g, m_pool_w, m_pool_scale, m_sconv_w, m_w_out, m_norm2_g, m_w_gate, m_w_up, m_w_down, m_ple_proj, m_ple_gate, m_final_g, v_norm1_g, v_w_in, v_conv_qkv, v_a_log, v_dt_bias, v_onorm_g, v_pool_w, v_pool_scale, v_sconv_w, v_w_out, v_norm2_g, v_w_gate, v_w_up, v_w_down, v_ple_proj, v_ple_gate, v_final_g):
    names = ["norm1_g", "w_in", "conv_qkv", "a_log", "dt_bias", "onorm_g", "pool_w", "pool_scale", "sconv_w", "w_out", "norm2_g",
             "w_gate", "w_up", "w_down", "ple_proj", "ple_gate", "final_g"]
    w = dict(zip(names, [norm1_g, w_in, conv_qkv, a_log, dt_bias, onorm_g, pool_w, pool_scale, sconv_w, w_out, norm2_g, w_gate, w_up,
                         w_down, ple_proj, ple_gate, final_g]))
    m = dict(zip(names, [m_norm1_g, m_w_in, m_conv_qkv, m_a_log, m_dt_bias, m_onorm_g, m_pool_w, m_pool_scale, m_sconv_w, m_w_out,
                         m_norm2_g, m_w_gate, m_w_up, m_w_down, m_ple_proj, m_ple_gate, m_final_g]))
    v = dict(zip(names, [v_norm1_g, v_w_in, v_conv_qkv, v_a_log, v_dt_bias, v_onorm_g, v_pool_w, v_pool_scale, v_sconv_w, v_w_out,
                         v_norm2_g, v_w_gate, v_w_up, v_w_down, v_ple_proj, v_ple_gate, v_final_g]))
    w.update({k: _ff_rows(w[k]) for k in TRANSPOSED})

    first, rest = FETCH_GROUPS[0], tuple(k for members in FETCH_GROUPS[1:] for k in members)
    gathered = dict(zip(first, _all_gather([_payload(k, w[k][0]) for k in first], name="all_gather_weights")))
    (flying0,), token = _exchange_start([[_payload(k, w[k][0]) for k in rest]], CHIP_GATHER, name="gather_start_0",
                                        after=gathered[first[0]])
    replicated = [_small_weights(w, i) for i in range(DEPTH)]
    replicated[0]["norm1_g"] = replicated[0]["norm1_g"] + token[0, 0]
    for group in (m, v):
        group.update({k: _ff_rows(group[k] + token[0, 0]) for k in TRANSPOSED})
    flying1 = []

    def fetch(i, group, after):
        if i == 0 and group == 1:
            landed = _exchange_wait(flying0, after, CHIP_GATHER, name="gather_wait_0")
            gathered.update(zip(rest, _pair_swap(landed, name="pair_swap")))
            started, token = _exchange_start([[_payload(k, w[k][1]) for k in SHARDED]], CHIP_GATHER, name="gather_start_1",
                                             after=gathered[rest[0]])
            flying1.extend(started)
            return {**{k: _as_read(k, gathered[k]) for k in FETCH_GROUPS[group]},
                    "onorm_g": replicated[0]["onorm_g"] + token[0, 0]}
        if i == 1 and group == 0:
            landed = _exchange_wait(flying1[0], after, CHIP_GATHER, name="gather_wait_1")
            gathered.update(zip(SHARDED, _pair_swap(landed, name="pair_swap")))
        return {k: _as_read(k, gathered[k]) for k in FETCH_GROUPS[group]}

    def reduce_scatter_start(members, blocks, tag):
        mine = [blocks[k] for k in members]
        theirs = _pair_exchange(mine, name="pair_exchange")
        sums = [_pair_add(a, b, name="pair_add") for a, b in zip(mine, theirs)]
        (started,), token = _exchange_start([sums], CHIP_SCATTER, name="exchange_start_" + tag)
        return started, token

    h, saved0 = _layer_fwd(x[0], p[0, 0], replicated[0], functools.partial(fetch, 0))
    h, saved1 = _layer_fwd(h, p[1, 0], replicated[1], functools.partial(fetch, 1))
    dx, dgf, loss_part = _loss_head(h, final_g[None], loss_target[0], name="loss_head")
    small, big1, flying0 = [None] * DEPTH, {}, []
    dx, small[1] = _layer_bwd(dx, saved1, lambda group, blocks: big1.update({k: blocks[k] for k in EMIT_GROUPS[group]}))
    flying1, token = reduce_scatter_start(SHARDED, big1, "1")

    def emit(group, blocks):
        started, token = reduce_scatter_start(EMIT_GROUPS[group], blocks, f"0_{group}")
        flying0.append(started)
        return token

    dx, small[0] = _layer_bwd(dx, saved0, emit, after=token)
    received = [{}, dict(zip(SHARDED, _exchange_wait(flying1, dx, CHIP_SCATTER, name="exchange_wait_1")))]
    for group, members in enumerate(EMIT_GROUPS):
        received[0].update(zip(members, _exchange_wait(flying0[group], dx, CHIP_SCATTER, name=f"exchange_wait_0_{group}")))

    grads = {k: jnp.stack([small[i][k] for i in range(DEPTH)]) for k in small[0]}
    grads = {k: g[:, 0] if k in ("norm1_g", "norm2_g", "onorm_g", "pool_scale") else g for k, g in grads.items()}
    grads["final_g"] = dgf[0]
    loss_row = jnp.pad(loss_part, ((0, 0), (0, SLAB_COLS - LANE)))
    (small_flying,), token = _exchange_start([[_pack_slab(grads, loss_row)]], GATHER, name="small_gather_start")

    out_g, out_d, out_m, out_v = {}, {}, {}, {}
    for k in SHARDED:
        out_g[k], out_d[k], out_m[k], out_v[k] = _adamw_reduce(w[k], [received[i][k] for i in range(DEPTH)], m[k], v[k],
                                                                name="adamw_" + k, after=token)
    behind_all = jnp.stack([out_v[k][0, 0, 0] for k in SHARDED])
    (small_parts,) = _exchange_wait(small_flying, behind_all, GATHER, name="small_gather_wait")
    zero_row = jnp.zeros((1, SLAB_COLS), F32)
    slabs = _adamw_reduce(_pack_slab(w, zero_row)[None], [small_parts], _pack_slab(m, zero_row)[None],
                          _pack_slab(v, zero_row)[None], name="adamw_small")
    slabs = [s[0] for s in slabs]
    shapes = {k: w[k].shape for k in SMALL}
    for dst, slab in zip((out_g, out_d, out_m, out_v), slabs):
        vals, _ = _unpack_slab(slab, shapes)
        dst.update(vals)
    _, loss_at = _unpack_slab(slabs[0], shapes)
    loss = slabs[0][loss_at, 0]
    for group in (out_g, out_d, out_m, out_v):
        group.update({k: _ff_rows(group[k]) for k in TRANSPOSED})

    return (loss, dx[None], *[out_g[k] for k in names], *[out_d[k] for k in names], *[out_m[k] for k in names],
            *[out_v[k] for k in names])
```

```python
import functools

import jax
import jax.numpy as jnp
from jax import lax
from jax.experimental import pallas as pl
from jax.experimental.pallas import tpu as pltpu

F32 = jnp.float32
BF16 = jnp.bfloat16

D_MODEL = 1024
DEPTH = 2
PLE_DIM = 256
EPS = 1e-6
HEAD_DIM = 128
HEADS = 4
A_DIM = HEADS * HEAD_DIM
QKV_TAPS = 4
CHUNK = 64
POOL_WINDOWS = (2, 4, 8, 16)
POOL_DIM = 256
CONV_DIM = 256
CONV_TAPS = 3
D_FF = 2816
D_IN = 3080
D_IN_PAD = 3200
AB_COL = 2048
N_DEV = 8

ADAM_LR = 0.001
ADAM_B1 = 0.9
ADAM_B2 = 0.999
ADAM_EPS = 1e-08
ADAM_WD = 0.01
ADAM_STEP = 10

LANE = 128
SUBLANE = 8
VMEM_BYTES_V7X = 64 * 1024 * 1024
VMEM_LIMIT = 48 * 1024 * 1024

_HI = lax.Precision.HIGHEST
NN = ((1,), (0,))
NT = ((1,), (1,))
TN = ((0,), (0,))
MESH = pl.DeviceIdType.MESH


def _dot(a, b, dims, hi=False):
    if hi:
        return lax.dot_general(a, b, (dims, ((), ())), precision=_HI, preferred_element_type=F32)
    return lax.dot_general(a.astype(BF16), b.astype(BF16), (dims, ((), ())), preferred_element_type=F32)


def _pcall(body, *, name, out_shape, grid=(), in_specs=None, out_specs=None, scratch_shapes=(), semantics=None,
           vmem_limit=None, after=None, **kw):
    params = {}
    if semantics is not None:
        params["dimension_semantics"] = semantics
    if vmem_limit is not None:
        params["vmem_limit_bytes"] = vmem_limit
    if after is not None:
        n_in, inner = len(in_specs), body
        body = lambda *refs: inner(*refs[:n_in], *refs[n_in + 1:])
        in_specs = list(in_specs) + [pl.BlockSpec(after.shape, lambda *_: (0,) * after.ndim)]
    call = pl.pallas_call(
        body, name=name, out_shape=out_shape, grid=grid, in_specs=in_specs, out_specs=out_specs,
        scratch_shapes=list(scratch_shapes), compiler_params=pltpu.CompilerParams(**params), **kw)
    return call if after is None else (lambda *args: call(*args, after))


def _sigmoid(x):
    return 1.0 / (1.0 + jnp.exp(-x))


def _softplus(x):
    return jnp.maximum(x, 0.0) + jnp.log(1.0 + jnp.exp(-jnp.abs(x)))


def _tile(n, cap, mult):
    if n <= cap:
        return n
    best = None
    for t in range(mult, cap + 1, mult):
        if n % t == 0:
            best = t
    assert best is not None, (n, cap, mult)
    return best


ROWS_PER_STEP = 512
NARROW_RESULT = 1024
COLS_PER_DOT = 640


def _resident(weight):
    return pl.BlockSpec(weight.shape, lambda i: (0,) * weight.ndim, pipeline_mode=pl.Buffered(1))


def _matmul_rows(a, b, mode, *, name, res=None, out_dtype=F32, b_blocked=False, after=None, norm_g=None):
    m, k = a.shape
    if b_blocked:
        nb, _, bw = b.shape
        n = nb * bw if mode == "nn" else b.shape[1]
    else:
        n = b.shape[1] if mode == "nn" else b.shape[0]
    tm = _tile(m, ROWS_PER_STEP if n > NARROW_RESULT else 2 * ROWS_PER_STEP, 16)
    cn = bw if (b_blocked and mode == "nn") else _tile(n, COLS_PER_DOT, LANE)
    has_res = res is not None
    normed = norm_g is not None

    def body(*refs):
        a_ref, b_ref = refs[0], refs[1]
        g_ref = refs[2] if normed else None
        res_ref = refs[2 + normed] if has_res else None
        o_ref = refs[2 + normed + has_res]
        if normed:
            av = _rms_normed(a_ref[...], g_ref[...])
            refs[3 + normed + has_res][...] = av
        elif not (b_blocked and mode == "nt"):
            av = a_ref[...].astype(BF16)
        for j in range(n // cn):
            cols = pl.ds(j * cn, cn)
            if mode == "nn":
                part = _dot(av, b_ref[j] if b_blocked else b_ref[:, cols], NN)
            elif not b_blocked:
                part = _dot(av, b_ref[cols, :], NT)
            else:
                part = None
                for s in range(nb):
                    term = _dot(a_ref[:, pl.ds(s * bw, bw)], b_ref[s, cols, :], NT)
                    part = term if part is None else part + term
            if has_res:
                part = part + res_ref[:, cols]
            o_ref[:, cols] = part.astype(o_ref.dtype)

    row = lambda width: pl.BlockSpec((tm, width), lambda i: (i, 0))
    whole = _resident(b)
    ins = [a, b] + ([norm_g] if normed else []) + ([res] if has_res else [])
    specs = [row(k), whole] + ([pl.BlockSpec((1, k), lambda i: (0, 0))] if normed else []) + ([row(n)] if has_res else [])
    out = jax.ShapeDtypeStruct((m, n), out_dtype)
    return _pcall(body, name=name, out_shape=(out, jax.ShapeDtypeStruct((m, k), BF16)) if normed else out, grid=(m // tm,),
                  in_specs=specs, out_specs=(row(n), row(k)) if normed else row(n), semantics=("parallel",),
                  vmem_limit=VMEM_LIMIT, after=after)(*ins)


def _rms_normed(xv, gv):
    return (xv * lax.rsqrt(jnp.mean(xv * xv, axis=-1, keepdims=True) + EPS) * gv).astype(BF16)


def _matmul(a, b, mode, *, name, res=None, out_dtype=F32, b_blocked=False, out_blocked=None, after=None, norm_g=None):
    if mode != "tn":
        return _matmul_rows(a, b, mode, name=name, res=res, out_dtype=out_dtype, b_blocked=b_blocked, after=after, norm_g=norm_g)
    assert res is None and not b_blocked and after is None and norm_g is None
    (t, m), (t2, n) = a.shape, b.shape
    assert t == t2, (a.shape, b.shape)
    tm = _tile(m, 1024, LANE)
    tn = _tile(n, COLS_PER_DOT, LANE)
    if out_blocked is not None:
        assert out_blocked[0] * out_blocked[1] == n
        tn = out_blocked[1]

    def body(a_ref, b_ref, o_ref):
        part = _dot(a_ref[...], b_ref[...], TN).astype(o_ref.dtype)
        if out_blocked is None:
            o_ref[...] = part
        else:
            o_ref[0] = part

    o_spec = (pl.BlockSpec((tm, tn), lambda i, j: (i, j)) if out_blocked is None
              else pl.BlockSpec((1, tm, tn), lambda i, j: (j, i, 0)))
    o_shape = (m, n) if out_blocked is None else (out_blocked[0], m, out_blocked[1])
    return _pcall(body, name=name, out_shape=jax.ShapeDtypeStruct(o_shape, out_dtype), grid=(m // tm, n // tn),
                  in_specs=[pl.BlockSpec((t, tm), lambda i, j: (0, i)), pl.BlockSpec((t, tn), lambda i, j: (0, j))],
                  out_specs=o_spec, semantics=("parallel", "parallel"), vmem_limit=VMEM_LIMIT)(a, b)


ROW_TILE = 512


def _rows(t, width, idx=0):
    return pl.BlockSpec((ROW_TILE, width), lambda i: (i, idx))


def _vec(width):
    return pl.BlockSpec((1, width), lambda i: (0, 0))


def _rmsnorm_bwd(x, g, dh, dres, *, name):
    t, d = x.shape

    def body(x_ref, g_ref, dh_ref, dres_ref, dx_ref, dg_ref):
        xv = x_ref[...]
        r = lax.rsqrt(jnp.mean(xv * xv, axis=-1, keepdims=True) + EPS)
        xhat = xv * r
        dhv = dh_ref[...].astype(F32)
        dhg = dhv * g_ref[...]
        dx_ref[...] = dres_ref[...] + r * (dhg - xhat * jnp.mean(dhg * xhat, axis=-1, keepdims=True))
        part = jnp.sum(dhv * xhat, axis=0, keepdims=True)

        @pl.when(pl.program_id(0) == 0)
        def _():
            dg_ref[...] = part

        @pl.when(pl.program_id(0) > 0)
        def _():
            dg_ref[...] += part

    return _pcall(body, name=name, out_shape=(jax.ShapeDtypeStruct((t, d), F32), jax.ShapeDtypeStruct((1, d), F32)),
                  grid=(t // ROW_TILE,), in_specs=[_rows(t, d), _vec(d), _rows(t, d), _rows(t, d)],
                  out_specs=(_rows(t, d), _vec(d)), semantics=("arbitrary",))(x, g, dh, dres)


def _swiglu_fwd(x, norm_g, w_gate, w_up, *, name):
    t, k = x.shape
    f = w_gate.shape[0]
    tm = _tile(t, ROWS_PER_STEP, 16)
    cn = _tile(f, COLS_PER_DOT, LANE)

    def body(x_ref, g_ref, wg_ref, wu_ref, ff_ref, gate_ref, up_ref, h_ref):
        hv = _rms_normed(x_ref[...], g_ref[...])
        h_ref[...] = hv
        for j in range(f // cn):
            cols = pl.ds(j * cn, cn)
            gv = _dot(hv, wg_ref[cols, :], NT)
            uv = _dot(hv, wu_ref[cols, :], NT)
            gate_ref[:, cols] = gv.astype(BF16)
            up_ref[:, cols] = uv.astype(BF16)
            ff_ref[:, cols] = (gv * _sigmoid(gv) * uv).astype(BF16)

    row = lambda width: pl.BlockSpec((tm, width), lambda i: (i, 0))
    out = jax.ShapeDtypeStruct((t, f), BF16)
    return _pcall(body, name=name, out_shape=(out,) * 3 + (jax.ShapeDtypeStruct((t, k), BF16),), grid=(t // tm,),
                  in_specs=[row(k), pl.BlockSpec((1, k), lambda i: (0, 0)), _resident(w_gate), _resident(w_up)],
                  out_specs=(row(f),) * 3 + (row(k),), semantics=("parallel",), vmem_limit=VMEM_LIMIT)(x, norm_g, w_gate, w_up)


def _swiglu_bwd(dx2, w_down, gate, up, *, name, after=None):
    t, d = dx2.shape
    f = w_down.shape[0]
    tm = _tile(t, ROWS_PER_STEP, 16)
    cn = _tile(f, COLS_PER_DOT, LANE)

    def body(dx_ref, w_ref, gate_ref, up_ref, dgate_ref, dup_ref):
        dxv = dx_ref[...].astype(BF16)
        for j in range(f // cn):
            cols = pl.ds(j * cn, cn)
            dffv = _dot(dxv, w_ref[cols, :], NT)
            gv = gate_ref[:, cols].astype(F32)
            sig = _sigmoid(gv)
            dgate_ref[:, cols] = (dffv * up_ref[:, cols].astype(F32) * sig * (1.0 + gv * (1.0 - sig))).astype(BF16)
            dup_ref[:, cols] = (dffv * gv * sig).astype(BF16)

    row = lambda width: pl.BlockSpec((tm, width), lambda i: (i, 0))
    out = jax.ShapeDtypeStruct((t, f), BF16)
    return _pcall(body, name=name, out_shape=(out, out), grid=(t // tm,), in_specs=[row(d), _resident(w_down), row(f), row(f)],
                  out_specs=(row(f), row(f)), semantics=("parallel",), vmem_limit=VMEM_LIMIT, after=after)(dx2, w_down, gate, up)


def _ple_fwd(x2, p, w_gate, w_proj, *, name):
    t, d = x2.shape
    nb, pdim, bw = w_proj.shape
    tm = _tile(t, ROWS_PER_STEP, 16)
    cn = _tile(d, COLS_PER_DOT, LANE)

    def body(x_ref, p_ref, wg_ref, wp_ref, x3_ref, pgl_ref, pp_ref):
        xb = x_ref[...].astype(BF16)
        pb = p_ref[...].astype(BF16)
        per = cn // bw
        for c in range(d // cn):
            cols = pl.ds(c * cn, cn)
            pgl = _dot(xb, wg_ref[:, cols], NN)
            pp = jnp.concatenate([_dot(pb, wp_ref[c * per + j], NN) for j in range(per)], axis=1)
            pgl_ref[:, cols] = pgl
            pp_ref[:, cols] = pp
            x3_ref[:, cols] = x_ref[:, cols] + _sigmoid(pgl) * pp

    row = lambda width: pl.BlockSpec((tm, width), lambda i: (i, 0))
    out = jax.ShapeDtypeStruct((t, d), F32)
    return _pcall(body, name=name, out_shape=(out,) * 3, grid=(t // tm,),
                  in_specs=[row(d), row(pdim), _resident(w_gate), _resident(w_proj)], out_specs=(row(d),) * 3,
                  semantics=("parallel",), vmem_limit=VMEM_LIMIT)(x2, p, w_gate, w_proj)


def _ple_bwd(dx3, pgl, pp, *, name, after=None):
    t, d = dx3.shape

    def body(dx_ref, pgl_ref, pp_ref, dpgl_ref, dpp_ref):
        dxv = dx_ref[...]
        sig = _sigmoid(pgl_ref[...])
        dpp_ref[...] = (dxv * sig).astype(BF16)
        dpgl_ref[...] = (dxv * pp_ref[...] * sig * (1.0 - sig)).astype(BF16)

    return _pcall(body, name=name, out_shape=(jax.ShapeDtypeStruct((t, d), BF16),) * 2, grid=(t // ROW_TILE,),
                  in_specs=[_rows(t, d)] * 3, out_specs=(_rows(t, d),) * 2, semantics=("parallel",), after=after)(dx3, pgl, pp)


def _loss_head(x3, g, target, *, name):
    t, d = x3.shape

    def body(x_ref, g_ref, t_ref, dx_ref, dg_ref, loss_ref):
        xv = x_ref[...]
        r = lax.rsqrt(jnp.mean(xv * xv, axis=-1, keepdims=True) + EPS)
        xhat = xv * r
        gv = g_ref[...]
        err = xhat * gv - t_ref[...]
        row_loss = jnp.sum(err * err, axis=-1, keepdims=True) * (0.5 / d)
        lpart = jnp.broadcast_to(jnp.sum(row_loss, axis=0, keepdims=True), (1, LANE))
        dy = err * (1.0 / d)
        dyg = dy * gv
        dx_ref[...] = r * (dyg - xhat * jnp.mean(dyg * xhat, axis=-1, keepdims=True))
        gpart = jnp.sum(dy * xhat, axis=0, keepdims=True)

        @pl.when(pl.program_id(0) == 0)
        def _():
            dg_ref[...] = gpart
            loss_ref[...] = lpart

        @pl.when(pl.program_id(0) > 0)
        def _():
            dg_ref[...] += gpart
            loss_ref[...] += lpart

    return _pcall(body, name=name,
                  out_shape=(jax.ShapeDtypeStruct((t, d), F32), jax.ShapeDtypeStruct((1, d), F32), jax.ShapeDtypeStruct((1, LANE), F32)),
                  grid=(t // ROW_TILE,), in_specs=[_rows(t, d), _vec(d), _rows(t, d)],
                  out_specs=(_rows(t, d), _vec(d), _vec(LANE)), semantics=("arbitrary",))(x3, g, target)


def _shift_down(x, d):
    if d == 0:
        return x
    row = lax.broadcasted_iota(jnp.int32, x.shape, 0)
    return jnp.where(row >= d, pltpu.roll(x, d, 0), 0.0)


def _shift_up(x, d):
    if d == 0:
        return x
    t = x.shape[0]
    row = lax.broadcasted_iota(jnp.int32, x.shape, 0)
    return jnp.where(row < t - d, pltpu.roll(x, t - d, 0), 0.0)


def _colsum(x):
    return jnp.sum(x, axis=0, keepdims=True)


def _col(t, idx_fn):
    return pl.BlockSpec((t, LANE), idx_fn)


def _conv_fwd(x, w_ref, taps):
    acc = None
    for j in range(taps):
        term = w_ref[pl.ds(j, 1), :] * _shift_down(x, taps - 1 - j)
        acc = term if acc is None else acc + term
    return acc


def _conv_bwd(x, dy, w_ref, dw_ref, taps):
    dx = None
    for j in range(taps):
        term = w_ref[pl.ds(j, 1), :] * _shift_up(dy, taps - 1 - j)
        dx = term if dx is None else dx + term
        dw_ref[pl.ds(j, 1), :] = _colsum(dy * _shift_down(x, taps - 1 - j))
    return dx


def _qkv_prep_fwd(proj, conv_w, *, name):
    t = proj.shape[0]
    scale = HEAD_DIM ** -0.5

    def body(x_ref, w_ref, o_ref):
        j = pl.program_id(0)
        c = _conv_fwd(x_ref[...], w_ref, QKV_TAPS)
        s = c * _sigmoid(c)
        r = lax.rsqrt(jnp.sum(s * s, axis=-1, keepdims=True) + EPS)
        f = jnp.where(j < 2 * HEADS, r, 1.0) * jnp.where(j < HEADS, scale, 1.0)
        o_ref[0] = s * f

    return _pcall(body, name=name, out_shape=jax.ShapeDtypeStruct((3 * HEADS, t, LANE), F32), grid=(3 * HEADS,),
                  in_specs=[_col(t, lambda j: (0, j)), pl.BlockSpec((QKV_TAPS, LANE), lambda j: (0, j))],
                  out_specs=pl.BlockSpec((1, t, LANE), lambda j: (j, 0, 0)), semantics=("parallel",),
                  vmem_limit=VMEM_LIMIT)(proj, conv_w)


def _qkv_prep_bwd(proj, conv_w, dqkv, *, name):
    t = proj.shape[0]
    scale = HEAD_DIM ** -0.5

    def body(x_ref, w_ref, d_ref, dx_ref, dw_ref):
        j = pl.program_id(0)
        xv = x_ref[...]
        c = _conv_fwd(xv, w_ref, QKV_TAPS)
        sig = _sigmoid(c)
        s = c * sig
        r = lax.rsqrt(jnp.sum(s * s, axis=-1, keepdims=True) + EPS)
        n0 = s * r
        dv = d_ref[0]
        dn0 = dv * jnp.where(j < HEADS, scale, 1.0)
        ds_norm = r * (dn0 - n0 * jnp.sum(dn0 * n0, axis=-1, keepdims=True))
        ds = jnp.where(j < 2 * HEADS, ds_norm, dv)
        dc = ds * sig * (1.0 + c * (1.0 - sig))
        dx_ref[...] = _conv_bwd(xv, dc, w_ref, dw_ref, QKV_TAPS).astype(BF16)

    return _pcall(body, name=name,
                  out_shape=(jax.ShapeDtypeStruct((t, 3 * A_DIM), BF16), jax.ShapeDtypeStruct((QKV_TAPS, 3 * A_DIM), F32)),
                  grid=(3 * HEADS,),
                  in_specs=[_col(t, lambda j: (0, j)), pl.BlockSpec((QKV_TAPS, LANE), lambda j: (0, j)),
                            pl.BlockSpec((1, t, LANE), lambda j: (j, 0, 0))],
                  out_specs=(_col(t, lambda j: (0, j)), pl.BlockSpec((QKV_TAPS, LANE), lambda j: (0, j))),
                  semantics=("parallel",), vmem_limit=VMEM_LIMIT)(proj, conv_w, dqkv)


def _lane_pick(x, lane_idx, lane):
    return jnp.broadcast_to(jnp.sum(jnp.where(lane == lane_idx, x, 0.0), axis=-1, keepdims=True), x.shape)


def _gates_fwd(proj, alog, dtb, *, name):
    t = proj.shape[0]

    def body(x_ref, alog_ref, dtb_ref, g_ref, b_ref):
        xv = x_ref[...]
        lane = lax.broadcasted_iota(jnp.int32, xv.shape, 1)
        gall = -jnp.exp(alog_ref[...]) * _softplus(xv + dtb_ref[...])
        ball = _sigmoid(xv)
        for h in range(HEADS):
            g_ref[h] = _lane_pick(gall, h, lane)
            b_ref[h] = _lane_pick(ball, HEADS + h, lane)

    out = jax.ShapeDtypeStruct((HEADS, t, LANE), F32)
    whole = pl.BlockSpec((HEADS, t, LANE), lambda i: (0, 0, 0))
    return _pcall(body, name=name, out_shape=(out, out), grid=(1,),
                  in_specs=[_col(t, lambda i: (0, AB_COL // LANE)), _vec(LANE), _vec(LANE)], out_specs=(whole, whole),
                  semantics=("arbitrary",), vmem_limit=VMEM_LIMIT)(proj, alog, dtb)


def _gates_bwd(proj, alog, dtb, dg, dbeta, *, name):
    t = proj.shape[0]

    def body(x_ref, alog_ref, dtb_ref, dg_ref, db_ref, dab_ref, dalog_ref, ddtb_ref):
        xv = x_ref[...]
        lane = lax.broadcasted_iota(jnp.int32, xv.shape, 1)
        lane1 = lax.broadcasted_iota(jnp.int32, (1, LANE), 1)
        z = xv + dtb_ref[...]
        nea = -jnp.exp(alog_ref[...])
        da_f = nea * _sigmoid(z)
        g_f = nea * _softplus(z)
        ball = _sigmoid(xv)
        db_f = ball * (1.0 - ball)
        dab = jnp.zeros_like(xv)
        dalog = jnp.zeros((1, LANE), F32)
        for h in range(HEADS):
            dgh = dg_ref[h]
            dab = dab + jnp.where(lane == h, dgh * da_f, 0.0) + jnp.where(lane == HEADS + h, db_ref[h] * db_f, 0.0)
            dalog = dalog + jnp.where(lane1 == h, _colsum(dgh * g_f), 0.0)
        dab_ref[...] = dab.astype(BF16)
        dalog_ref[...] = dalog
        ddtb_ref[...] = jnp.where(lane1 < HEADS, _colsum(dab), 0.0)

    whole = pl.BlockSpec((HEADS, t, LANE), lambda i: (0, 0, 0))
    vec = jax.ShapeDtypeStruct((1, LANE), F32)
    return _pcall(body, name=name, out_shape=(jax.ShapeDtypeStruct((t, LANE), BF16), vec, vec), grid=(1,),
                  in_specs=[_col(t, lambda i: (0, AB_COL // LANE)), _vec(LANE), _vec(LANE), whole, whole],
                  out_specs=(_col(t, lambda i: (0, 0)), _vec(LANE), _vec(LANE)), semantics=("arbitrary",),
                  vmem_limit=VMEM_LIMIT)(proj, alog, dtb, dg, dbeta)


Z_COL = 3 * A_DIM // LANE


def _apost_fwd(o, proj, gn, *, name):
    t = proj.shape[0]

    def body(o_ref, z_ref, gn_ref, y_ref):
        ov = o_ref[0]
        z = z_ref[...]
        r = lax.rsqrt(jnp.mean(ov * ov, axis=-1, keepdims=True) + EPS)
        y_ref[...] = (ov * r * gn_ref[...] * (z * _sigmoid(z))).astype(BF16)

    return _pcall(body, name=name, out_shape=jax.ShapeDtypeStruct((t, A_DIM), BF16), grid=(HEADS,),
                  in_specs=[pl.BlockSpec((1, t, LANE), lambda h: (h, 0, 0)), _col(t, lambda h: (0, Z_COL + h)),
                            pl.BlockSpec((1, LANE), lambda h: (0, 0))],
                  out_specs=_col(t, lambda h: (0, h)), semantics=("parallel",), vmem_limit=VMEM_LIMIT)(o, proj, gn)


def _apost_bwd(o, proj, gn, dmixed, *, name):
    t = proj.shape[0]

    def body(o_ref, z_ref, gn_ref, d_ref, do_ref, dz_ref, dgn_ref):
        ov = o_ref[0]
        z = z_ref[...]
        gnv = gn_ref[...]
        dv = d_ref[...]
        r = lax.rsqrt(jnp.mean(ov * ov, axis=-1, keepdims=True) + EPS)
        ohat = ov * r
        sig = _sigmoid(z)
        dy = dv * (z * sig)
        dz_ref[...] = (dv * ohat * gnv * sig * (1.0 + z * (1.0 - sig))).astype(BF16)
        dyo = dy * gnv
        do_ref[0] = r * (dyo - ohat * jnp.mean(dyo * ohat, axis=-1, keepdims=True))
        part = _colsum(dy * ohat)

        @pl.when(pl.program_id(0) == 0)
        def _():
            dgn_ref[...] = part

        @pl.when(pl.program_id(0) > 0)
        def _():
            dgn_ref[...] += part

    return _pcall(body, name=name,
                  out_shape=(jax.ShapeDtypeStruct((HEADS, t, LANE), F32), jax.ShapeDtypeStruct((t, A_DIM), BF16),
                             jax.ShapeDtypeStruct((1, LANE), F32)),
                  grid=(HEADS,),
                  in_specs=[pl.BlockSpec((1, t, LANE), lambda h: (h, 0, 0)), _col(t, lambda h: (0, Z_COL + h)),
                            pl.BlockSpec((1, LANE), lambda h: (0, 0)), _col(t, lambda h: (0, h))],
                  out_specs=(pl.BlockSpec((1, t, LANE), lambda h: (h, 0, 0)), _col(t, lambda h: (0, h)),
                             pl.BlockSpec((1, LANE), lambda h: (0, 0))),
                  semantics=("arbitrary",), vmem_limit=VMEM_LIMIT)(o, proj, gn, dmixed)


POOL_COL = (AB_COL + LANE) // LANE
CB_COL = POOL_COL + POOL_DIM // LANE
CC_COL = CB_COL + CONV_DIM // LANE
CH_COL = CC_COL + CONV_DIM // LANE
MAX_WIN_LOG2 = 4


def _window_sums(x, shift):
    sums = []
    cur = x
    for k in range(MAX_WIN_LOG2):
        cur = cur + shift(cur, 1 << k)
        sums.append(cur)
    return sums


def _pick_window(sums, win):
    out = sums[-1]
    for k in range(MAX_WIN_LOG2 - 2, -1, -1):
        out = jnp.where(win == float(2 << k), sums[k], out)
    return out


def _pool_counts(shape, win):
    row = lax.broadcasted_iota(jnp.int32, shape, 0).astype(F32)
    return jnp.minimum(row + 1.0, win)


def _pool_fwd(proj, win, wbd, scale, *, name):
    t = proj.shape[0]

    def body(x_ref, win_ref, w_ref, s_ref, y_ref):
        xv = x_ref[...]
        winv = win_ref[...]
        pooled = _pick_window(_window_sums(xv, _shift_down), winv) / _pool_counts(xv.shape, winv) - xv
        y_ref[...] = (_dot(pooled, w_ref[0], NN) * s_ref[...]).astype(BF16)

    nb = POOL_DIM // LANE
    vec = pl.BlockSpec((1, LANE), lambda b: (0, b))
    return _pcall(body, name=name, out_shape=jax.ShapeDtypeStruct((t, POOL_DIM), BF16), grid=(nb,),
                  in_specs=[_col(t, lambda b: (0, POOL_COL + b)), vec, pl.BlockSpec((1, LANE, LANE), lambda b: (b, 0, 0)), vec],
                  out_specs=_col(t, lambda b: (0, b)), semantics=("parallel",), vmem_limit=VMEM_LIMIT)(proj, win, wbd, scale)


def _pool_bwd(proj, win, wbd, scale, dmixed, *, name):
    t = proj.shape[0]

    def body(x_ref, win_ref, w_ref, s_ref, d_ref, dx_ref, dw_ref, ds_ref):
        xv = x_ref[...]
        winv = win_ref[...]
        cnt = _pool_counts(xv.shape, winv)
        pooled = _pick_window(_window_sums(xv, _shift_down), winv) / cnt - xv
        dv = d_ref[...]
        ds_ref[...] = _colsum(dv * _dot(pooled, w_ref[0], NN))
        dy0 = dv * s_ref[...]
        dw_ref[0] = _dot(pooled, dy0, TN)
        dpooled = _dot(dy0, w_ref[0], NT)
        dmean = dpooled / cnt
        dx_ref[...] = (_pick_window(_window_sums(dmean, _shift_up), winv) - dpooled).astype(BF16)

    nb = POOL_DIM // LANE
    vec = pl.BlockSpec((1, LANE), lambda b: (0, b))
    mat = pl.BlockSpec((1, LANE, LANE), lambda b: (b, 0, 0))
    first = A_DIM // LANE
    return _pcall(body, name=name,
                  out_shape=(jax.ShapeDtypeStruct((t, POOL_DIM), BF16), jax.ShapeDtypeStruct((nb, LANE, LANE), F32),
                             jax.ShapeDtypeStruct((1, POOL_DIM), F32)),
                  grid=(nb,),
                  in_specs=[_col(t, lambda b: (0, POOL_COL + b)), vec, mat, vec, _col(t, lambda b: (0, first + b))],
                  out_specs=(_col(t, lambda b: (0, b)), mat, vec), semantics=("parallel",),
                  vmem_limit=VMEM_LIMIT)(proj, win, wbd, scale, dmixed)


def _sconv_fwd(proj, w, *, name):
    t = proj.shape[0]

    def body(cb_ref, cc_ref, ch_ref, w_ref, y_ref):
        y_ref[...] = (cb_ref[...] * _conv_fwd(cc_ref[...] * ch_ref[...], w_ref, CONV_TAPS)).astype(BF16)

    nb = CONV_DIM // LANE
    return _pcall(body, name=name, out_shape=jax.ShapeDtypeStruct((t, CONV_DIM), BF16), grid=(nb,),
                  in_specs=[_col(t, lambda b: (0, CB_COL + b)), _col(t, lambda b: (0, CC_COL + b)),
                            _col(t, lambda b: (0, CH_COL + b)), pl.BlockSpec((CONV_TAPS, LANE), lambda b: (0, b))],
                  out_specs=_col(t, lambda b: (0, b)), semantics=("parallel",), vmem_limit=VMEM_LIMIT)(proj, proj, proj, w)


def _sconv_bwd(proj, w, dmixed, *, name):
    t = proj.shape[0]

    def body(cb_ref, cc_ref, ch_ref, w_ref, d_ref, dcb_ref, dcc_ref, dch_ref, dw_ref):
        cc = cc_ref[...]
        ch = ch_ref[...]
        u = cc * ch
        dv = d_ref[...]
        dcb_ref[...] = (dv * _conv_fwd(u, w_ref, CONV_TAPS)).astype(BF16)
        du = _conv_bwd(u, dv * cb_ref[...], w_ref, dw_ref, CONV_TAPS)
        dcc_ref[...] = (du * ch).astype(BF16)
        dch_ref[...] = (du * cc).astype(BF16)

    nb = CONV_DIM // LANE
    first = (A_DIM + POOL_DIM) // LANE
    act = jax.ShapeDtypeStruct((t, CONV_DIM), BF16)
    wspec = pl.BlockSpec((CONV_TAPS, LANE), lambda b: (0, b))
    ospec = _col(t, lambda b: (0, b))
    return _pcall(body, name=name, out_shape=(act, act, act, jax.ShapeDtypeStruct((CONV_TAPS, CONV_DIM), F32)), grid=(nb,),
                  in_specs=[_col(t, lambda b: (0, CB_COL + b)), _col(t, lambda b: (0, CC_COL + b)),
                            _col(t, lambda b: (0, CH_COL + b)), wspec, _col(t, lambda b: (0, first + b))],
                  out_specs=(ospec, ospec, ospec, wspec), semantics=("parallel",),
                  vmem_limit=VMEM_LIMIT)(proj, proj, proj, w, dmixed)


def _chunk_masks():
    r = lax.broadcasted_iota(jnp.int32, (CHUNK, CHUNK), 0)
    c = lax.broadcasted_iota(jnp.int32, (CHUNK, CHUNK), 1)
    return r >= c, r > c, jnp.where(r == c, 1.0, 0.0).astype(F32)


def _split(a):
    hi = a.astype(BF16)
    return hi, (a - hi.astype(F32)).astype(BF16)


def _dot_split(a, b, dims):
    (ah, al), (bh, bl) = a, b
    return _dot(ah, bh, dims) + _dot(ah, bl, dims) + _dot(al, bh, dims)


def _tri_inv(lows, eye):
    xs = [eye - low for low in lows]
    ps = [_split(low) for low in lows]
    ps = [_split(_dot_split(p, p, NN)) for p in ps]
    for i in range(5):
        xs = [x + _dot_split(_split(x), p, NN) for x, p in zip(xs, ps)]
        if i < 4:
            ps = [_split(_dot_split(p, p, NN)) for p in ps]
    return xs


def _prefix_sum_rows(x):
    for k in range(6):
        x = x + _shift_down(x, 1 << k)
    return x


def _suffix_sum_rows(x):
    for k in range(6):
        x = x + _shift_up(x, 1 << k)
    return x


def _chunk_decay(g, incl):
    gcb = _prefix_sum_rows(g)
    gtot = _colsum(g)
    col = gcb[:, :CHUNK]
    row = gcb.T[:CHUNK, :]
    decay = jnp.exp(jnp.where(incl, col - row, -1e30))
    return gcb, gtot, decay


CHUNKS_PER_STEP = 4


def _heads_of(ref, base, rows):
    return [ref[base + h, rows, :] for h in range(HEADS)]


def _chunk_rows(j):
    return pl.ds(j * CHUNK, CHUNK)


def _deltanet_prep(qkv, g, beta, *, name):
    t = qkv.shape[1]
    n_chunks = t // CHUNK
    per = CHUNKS_PER_STEP
    probs = [(j, h) for j in range(per) for h in range(HEADS)]

    def body(qkv_ref, g_ref, b_ref, u_ref, w_ref, qg_ref, kg_ref, attn_ref, tm_ref):
        incl, strict, eye = _chunk_masks()
        q = [qkv_ref[h, _chunk_rows(j), :] for j, h in probs]
        k = [qkv_ref[HEADS + h, _chunk_rows(j), :] for j, h in probs]
        v = [qkv_ref[2 * HEADS + h, _chunk_rows(j), :] for j, h in probs]
        bv = [b_ref[h, _chunk_rows(j), :] for j, h in probs]
        dec = [_chunk_decay(g_ref[h, _chunk_rows(j), :], incl) for j, h in probs]
        kb = [a * b for a, b in zip(k, bv)]
        low = [jnp.where(strict, _dot(a, b, NT) * d[2], 0.0) for a, b, d in zip(kb, k, dec)]
        tm = _tri_inv(low, eye)
        egc = [jnp.exp(d[0]) for d in dec]
        u = [_dot(m, a * b, NN) for m, a, b in zip(tm, v, bv)]
        w = [_dot(m, a * e, NN) for m, a, e in zip(tm, kb, egc)]
        attn = [_dot(a, b, NT) * d[2] for a, b, d in zip(q, k, dec)]
        for i, (j, h) in enumerate(probs):
            rows = _chunk_rows(j)
            u_ref[h, rows, :] = u[i]
            w_ref[h, rows, :] = w[i].astype(BF16)
            qg_ref[h, rows, :] = (q[i] * egc[i]).astype(BF16)
            kg_ref[h, rows, :] = (k[i] * jnp.exp(dec[i][1] - dec[i][0])).astype(BF16)
            attn_ref[j, h] = attn[i].astype(BF16)
            tm_ref[j, h] = tm[i]

    act = lambda heads: pl.BlockSpec((heads, per * CHUNK, LANE), lambda n: (0, n, 0))
    mat = pl.BlockSpec((per, HEADS, CHUNK, CHUNK), lambda n: (n, 0, 0, 0))
    return _pcall(
        body, name=name,
        out_shape=(jax.ShapeDtypeStruct((HEADS, t, LANE), F32),) + (jax.ShapeDtypeStruct((HEADS, t, LANE), BF16),) * 3
        + (jax.ShapeDtypeStruct((n_chunks, HEADS, CHUNK, CHUNK), BF16), jax.ShapeDtypeStruct((n_chunks, HEADS, CHUNK, CHUNK), F32)),
        grid=(n_chunks // per,), in_specs=[act(3 * HEADS), act(HEADS), act(HEADS)],
        out_specs=(act(HEADS),) * 4 + (mat, mat), semantics=("parallel",), vmem_limit=VMEM_LIMIT)(qkv, g, beta)


SCAN_CHUNKS_PER_STEP = 8


def _deltanet_scan(u, w, qg, kg, attn, g, *, name, after=None):
    t = u.shape[1]
    n_chunks = t // CHUNK
    per = SCAN_CHUNKS_PER_STEP

    def body(u_ref, w_ref, qg_ref, kg_ref, attn_ref, g_ref, o_ref, vn_ref, st_ref, s_ref):
        @pl.when(pl.program_id(0) == 0)
        def _():
            s_ref[...] = jnp.zeros_like(s_ref)

        for j in range(per):
            rows = _chunk_rows(j)
            s = [s_ref[h] for h in range(HEADS)]
            vn = [u_ref[h, rows, :] - _dot(w_ref[h, rows, :], s[h], NN) for h in range(HEADS)]
            o = [_dot(qg_ref[h, rows, :], s[h], NN) + _dot(attn_ref[j, h], vn[h], NN) for h in range(HEADS)]
            eg = [jnp.exp(_colsum(g_ref[h, rows, :])) for h in range(HEADS)]
            for h in range(HEADS):
                st_ref[j, h] = s[h]
                s_ref[h] = s[h] * eg[h] + _dot(kg_ref[h, rows, :], vn[h], TN)
                o_ref[h, rows, :] = o[h]
                vn_ref[h, rows, :] = vn[h]

    act = pl.BlockSpec((HEADS, per * CHUNK, LANE), lambda n: (0, n, 0))
    out = jax.ShapeDtypeStruct((HEADS, t, LANE), F32)
    return _pcall(
        body, name=name, out_shape=(out, out, jax.ShapeDtypeStruct((n_chunks, HEADS, LANE, LANE), F32)), grid=(n_chunks // per,),
        in_specs=[act] * 4 + [pl.BlockSpec((per, HEADS, CHUNK, CHUNK), lambda n: (n, 0, 0, 0)), act],
        out_specs=(act, act, pl.BlockSpec((per, HEADS, LANE, LANE), lambda n: (n, 0, 0, 0))),
        scratch_shapes=[pltpu.VMEM((HEADS, LANE, LANE), F32)], semantics=("arbitrary",), after=after)(u, w, qg, kg, attn, g)


def _deltanet_bscan(w, qg, kg, attn, g, do, *, name):
    t = w.shape[1]
    n_chunks = t // CHUNK
    per = SCAN_CHUNKS_PER_STEP
    steps = n_chunks // per

    def body(w_ref, qg_ref, kg_ref, attn_ref, g_ref, do_ref, dvn_ref, dsn_ref, ds_ref):
        @pl.when(pl.program_id(0) == 0)
        def _():
            ds_ref[...] = jnp.zeros_like(ds_ref)

        for j in reversed(range(per)):
            rows = _chunk_rows(j)
            dsn = [ds_ref[h] for h in range(HEADS)]
            dov = [do_ref[h, rows, :] for h in range(HEADS)]
            dvn = [_dot(attn_ref[j, h], dov[h], TN) + _dot(kg_ref[h, rows, :], dsn[h], NN) for h in range(HEADS)]
            eg = [jnp.exp(_colsum(g_ref[h, rows, :])) for h in range(HEADS)]
            for h in range(HEADS):
                dsn_ref[j, h] = dsn[h]
                ds_ref[h] = _dot(qg_ref[h, rows, :], dov[h], TN) + eg[h] * dsn[h] - _dot(w_ref[h, rows, :], dvn[h], TN)
                dvn_ref[h, rows, :] = dvn[h]

    act = pl.BlockSpec((HEADS, per * CHUNK, LANE), lambda n: (0, steps - 1 - n, 0))
    return _pcall(
        body, name=name,
        out_shape=(jax.ShapeDtypeStruct((HEADS, t, LANE), F32), jax.ShapeDtypeStruct((n_chunks, HEADS, LANE, LANE), F32)),
        grid=(steps,),
        in_specs=[act] * 3 + [pl.BlockSpec((per, HEADS, CHUNK, CHUNK), lambda n: (steps - 1 - n, 0, 0, 0)), act, act],
        out_specs=(act, pl.BlockSpec((per, HEADS, LANE, LANE), lambda n: (steps - 1 - n, 0, 0, 0))),
        scratch_shapes=[pltpu.VMEM((HEADS, LANE, LANE), F32)], semantics=("arbitrary",))(w, qg, kg, attn, g, do)


def _sum_all(x):
    return jnp.sum(jnp.sum(x, axis=1, keepdims=True), axis=0, keepdims=True)


def _rowsum(x):
    return jnp.sum(x, axis=1, keepdims=True)


def _deltanet_post(qkv, g, beta, tmats, states, dstates, do, dvn, vn, *, name):
    t = qkv.shape[1]
    n_chunks = t // CHUNK
    per = CHUNKS_PER_STEP
    probs = [(j, h) for j in range(per) for h in range(HEADS)]

    def body(qkv_ref, g_ref, b_ref, tm_ref, st_ref, dsn_ref, do_ref, dvn_ref, vn_ref, dqkv_ref, dg_ref, db_ref):
        incl, strict, _ = _chunk_masks()
        ones = jnp.ones((CHUNK, LANE), BF16)
        last_row = lax.broadcasted_iota(jnp.int32, (CHUNK, LANE), 0) == CHUNK - 1
        z = lambda f, *cols: [f(*a) for a in zip(*cols)]
        q = [qkv_ref[h, _chunk_rows(j), :] for j, h in probs]
        k = [qkv_ref[HEADS + h, _chunk_rows(j), :] for j, h in probs]
        v = [qkv_ref[2 * HEADS + h, _chunk_rows(j), :] for j, h in probs]
        bv = [b_ref[h, _chunk_rows(j), :] for j, h in probs]
        dov = [do_ref[h, _chunk_rows(j), :] for j, h in probs]
        dvn_ = [dvn_ref[h, _chunk_rows(j), :] for j, h in probs]
        vn_ = [vn_ref[h, _chunk_rows(j), :] for j, h in probs]
        tm = [tm_ref[j, h] for j, h in probs]
        s = [st_ref[j, h] for j, h in probs]
        dsn = [dsn_ref[j, h] for j, h in probs]
        dec = [_chunk_decay(g_ref[h, _chunk_rows(j), :], incl) for j, h in probs]
        decay = [d[2] for d in dec]
        egc = [jnp.exp(d[0]) for d in dec]
        ekg = [jnp.exp(d[1] - d[0]) for d in dec]
        kb = z(lambda a, b: a * b, k, bv)
        vb = z(lambda a, b: a * b, v, bv)
        kbg = z(lambda a, b: a * b, kb, egc)
        qg = z(lambda a, b: a * b, q, egc)
        kg = z(lambda a, b: a * b, k, ekg)
        kk = z(lambda a, b: _dot(a, b, NT), kb, k)
        qk = z(lambda a, b: _dot(a, b, NT), q, k)
        dattn = z(lambda a, b: jnp.where(incl, _dot(a, b, NT), 0.0), dov, vn_)
        dqg = z(lambda a, b: _dot(a, b, NT), dov, s)
        dkg = z(lambda a, b: _dot(a, b, NT), vn_, dsn)
        dglast = z(lambda a, b, c, d, e: _sum_all(a * b) * jnp.exp(e[1]) + _sum_all(c * d), s, dsn, dkg, kg, dec)
        dw = z(lambda a, b: -_dot(a, b, NT), dvn_, s)
        dtm = z(lambda a, b, c, d: _dot(a, b, NT) + _dot(c, d, NT), dvn_, vb, dw, kbg)
        dvb = z(lambda a, b: _dot(a, b, TN), tm, dvn_)
        dkbg = z(lambda a, b: _dot(a, b, TN), tm, dw)
        dlow = z(lambda a, b: jnp.where(strict, -_dot(_dot(a, b, TN), a, NT), 0.0), tm, dtm)
        dkk = z(lambda a, b: a * b, dlow, decay)
        dqk = z(lambda a, b: a * b, dattn, decay)
        dkb = z(lambda a, b, c, d: _dot(a, b, NN) + c * d, dkk, k, dkbg, egc)
        dk = z(lambda a, b, c, d, e, f, g_, h_: _dot(a, b, TN) + _dot(c, d, TN) + e * f + g_ * h_, dkk, kb, dqk, q, dkg, ekg, dkb, bv)
        dq = z(lambda a, b, c, d: _dot(a, b, NN) + c * d, dqk, k, dqg, egc)
        m = z(lambda a, b, c, d, e: (a * b + c * d) * e, dlow, kk, dattn, qk, decay)
        mcol = [_dot(mh, ones, TN) + _dot(ml, ones, TN) for mh, ml in (_split(a) for a in m)]
        for i, (j, h) in enumerate(probs):
            rows = _chunk_rows(j)
            dqkv_ref[h, rows, :] = dq[i]
            dqkv_ref[HEADS + h, rows, :] = dk[i]
            dqkv_ref[2 * HEADS + h, rows, :] = dvb[i] * bv[i]
            db_ref[h, rows, :] = jnp.broadcast_to(_rowsum(dkb[i] * k[i] + dvb[i] * v[i]), (CHUNK, LANE))
            dgc = (_rowsum(dqg[i] * qg[i] + dkbg[i] * kbg[i] - dkg[i] * kg[i]) + _rowsum(m[i]) - mcol[i]
                   + jnp.where(last_row, dglast[i], 0.0))
            dg_ref[h, rows, :] = _suffix_sum_rows(dgc)

    act = lambda heads: pl.BlockSpec((heads, per * CHUNK, LANE), lambda n: (0, n, 0))
    mat = lambda d: pl.BlockSpec((per, HEADS, d, d), lambda n: (n, 0, 0, 0))
    out = jax.ShapeDtypeStruct((HEADS, t, LANE), F32)
    return _pcall(
        body, name=name, out_shape=(jax.ShapeDtypeStruct((3 * HEADS, t, LANE), F32), out, out), grid=(n_chunks // per,),
        in_specs=[act(3 * HEADS), act(HEADS), act(HEADS), mat(CHUNK), mat(LANE), mat(LANE), act(HEADS), act(HEADS), act(HEADS)],
        out_specs=(act(3 * HEADS), act(HEADS), act(HEADS)), semantics=("parallel",),
        vmem_limit=VMEM_LIMIT)(qkv, g, beta, tmats, states, dstates, do, dvn, vn)


ANY = pl.BlockSpec(memory_space=pl.ANY)
PEERS = N_DEV - 1


def _all_gather(arrays, *, name):
    n = len(arrays)

    def body(*refs):
        ins, outs = refs[:n], refs[n:2 * n]
        send_sems, recv_sems, local_sems = refs[2 * n:]
        x, y, c = lax.axis_index("x"), lax.axis_index("y"), lax.axis_index("c")
        me, sibling = (x, y, c), (x, y, 1 - c)
        chips = [(1 - x, y), (x, 1 - y), (1 - x, 1 - y)]

        def copy(a, k, block, to, src=None):
            dst = outs[a].at[4 * block[0] + 2 * block[1] + block[2]]
            return pltpu.make_async_remote_copy(src_ref=dst if src is None else src, dst_ref=dst, send_sem=send_sems.at[a * PEERS + k],
                                                recv_sem=recv_sems.at[a * PEERS + k], device_id=to, device_id_type=MESH)

        local = [pltpu.make_async_copy(ins[a], outs[a].at[4 * x + 2 * y + c], local_sems.at[a]) for a in range(n)]
        for cp in local:
            cp.start()
        first = []
        for a in range(n):
            first += [copy(a, 1 + j, me, (*chip, c), src=ins[a]) for j, chip in enumerate(chips)]
            first.append(copy(a, 0, me, sibling, src=ins[a]))
        for cp in first:
            cp.start()
        passed = []
        for a in range(n):
            for j, chip in enumerate(chips):
                copy(a, 1 + j, (*chip, c), me).wait_recv()
                fwd = copy(a, 4 + j, (*chip, c), sibling)
                fwd.start()
                passed.append(fwd)
        for a in range(n):
            copy(a, 0, sibling, me).wait_recv()
            for j, chip in enumerate(chips):
                copy(a, 4 + j, (*chip, 1 - c), me).wait_recv()
        for cp in first + passed:
            cp.wait_send()
        for cp in local:
            cp.wait()

    return _pcall(body, name=name, out_shape=tuple(jax.ShapeDtypeStruct((N_DEV,) + a.shape, a.dtype) for a in arrays),
                  in_specs=[ANY] * n, out_specs=(ANY,) * n,
                  scratch_shapes=[pltpu.SemaphoreType.DMA((n * PEERS,)), pltpu.SemaphoreType.DMA((n * PEERS,)),
                                  pltpu.SemaphoreType.DMA((n,))])(*arrays)


CHIPS = 4


def _pair_exchange(arrays, *, name):
    n = len(arrays)

    def body(*refs):
        ins, outs = refs[:n], refs[n:2 * n]
        send_sems, recv_sems = refs[2 * n:]
        x, y, c = lax.axis_index("x"), lax.axis_index("y"), lax.axis_index("c")
        copies = []
        for a in range(n):
            for q in range(CHIPS):
                cp = pltpu.make_async_remote_copy(src_ref=ins[a].at[2 * q + 1 - c], dst_ref=outs[a].at[q],
                                                  send_sem=send_sems.at[a * CHIPS + q], recv_sem=recv_sems.at[a * CHIPS + q],
                                                  device_id=(x, y, 1 - c), device_id_type=MESH)
                cp.start()
                copies.append(cp)
        for cp in copies:
            cp.wait()

    return _pcall(body, name=name, out_shape=tuple(jax.ShapeDtypeStruct((CHIPS,) + a.shape[1:], a.dtype) for a in arrays),
                  in_specs=[ANY] * n, out_specs=(ANY,) * n,
                  scratch_shapes=[pltpu.SemaphoreType.DMA((n * CHIPS,)), pltpu.SemaphoreType.DMA((n * CHIPS,))])(*arrays)


def _pair_add(blocks, theirs, *, name):
    _, r, c_ = blocks.shape
    tr = _tile(r, 512, 16)

    def body(mine_ref, theirs_ref, o_ref):
        core = lax.axis_index("c")
        own = jnp.where(core == 0, mine_ref[0, 0].astype(F32), mine_ref[0, 1].astype(F32))
        o_ref[0] = (own + theirs_ref[0].astype(F32)).astype(o_ref.dtype)

    spec = pl.BlockSpec((1, tr, c_), lambda q, i: (q, i, 0))
    return _pcall(body, name=name, out_shape=jax.ShapeDtypeStruct(theirs.shape, theirs.dtype), grid=(CHIPS, r // tr),
                  in_specs=[pl.BlockSpec((1, 2, tr, c_), lambda q, i: (q, 0, i, 0)), spec], out_specs=spec,
                  semantics=("parallel", "parallel"), vmem_limit=VMEM_LIMIT)(blocks.reshape(CHIPS, 2, r, c_), theirs)


HBM = pl.BlockSpec(memory_space=pltpu.HBM)
SEM = pl.BlockSpec(memory_space=pltpu.SEMAPHORE)
EFFECT = pltpu.SideEffectType.DATAFLOW_SIDE_EFFECTING


GATHER, CHIP_GATHER, CHIP_SCATTER = "gather", "chip_gather", "chip_scatter"
PEERS_OF = {GATHER: N_DEV - 1, CHIP_GATHER: CHIPS - 1, CHIP_SCATTER: CHIPS - 1}


def _direct_copies(srcs, lands, send_sems, recv_sems, local_sems, kind):
    x, y, c = lax.axis_index("x"), lax.axis_index("y"), lax.axis_index("c")
    peers = PEERS_OF[kind]
    mine = 2 * x + y if kind == CHIP_SCATTER else 4 * x + 2 * y + c
    copies = []
    for a, (src, land) in enumerate(zip(srcs, lands)):
        copies.append(pltpu.make_async_copy(src.at[mine] if kind == CHIP_SCATTER else src, land.at[mine], local_sems.at[a]))
        for k in range(1, peers + 1):
            bits = k if kind == GATHER else 2 * k
            px = 1 - x if bits & 4 else x
            py = 1 - y if bits & 2 else y
            pc = 1 - c if bits & 1 else c
            copies.append(pltpu.make_async_remote_copy(
                src_ref=src.at[2 * px + py] if kind == CHIP_SCATTER else src, dst_ref=land.at[mine],
                send_sem=send_sems.at[a * peers + k - 1], recv_sem=recv_sems.at[a * peers + k - 1],
                device_id=(px, py, pc), device_id_type=MESH))
    return copies


def _pair_swap(arrays, *, name):
    n = len(arrays)

    def body(*refs):
        mine, zones = refs[:n], refs[n:2 * n]
        send_sems, recv_sems = refs[2 * n:]
        x, y, c = lax.axis_index("x"), lax.axis_index("y"), lax.axis_index("c")
        copies = []
        for a in range(n):
            for q in range(CHIPS):
                copies.append(pltpu.make_async_remote_copy(
                    src_ref=mine[a].at[2 * q + c], dst_ref=zones[a].at[2 * q + c], send_sem=send_sems.at[a * CHIPS + q],
                    recv_sem=recv_sems.at[a * CHIPS + q], device_id=(x, y, 1 - c), device_id_type=MESH))
        for cp in copies:
            cp.start()
        for cp in copies:
            cp.wait()

    return _pcall(body, name=name, out_shape=tuple(jax.ShapeDtypeStruct(a.shape, a.dtype) for a in arrays),
                  in_specs=[ANY] * n, out_specs=(ANY,) * n, input_output_aliases={i: i for i in range(n)},
                  scratch_shapes=[pltpu.SemaphoreType.DMA((n * CHIPS,)), pltpu.SemaphoreType.DMA((n * CHIPS,))])(*arrays)


def _exchange_start(groups, kind, *, name, after=None):
    srcs = [s for group in groups for s in group]
    n = len(srcs)
    sizes = [len(group) for group in groups]
    starts = [sum(sizes[:g]) for g in range(len(groups))]
    land_shapes = [s.shape if kind == CHIP_SCATTER else (N_DEV,) + s.shape for s in srcs]
    peers = PEERS_OF[kind]
    extra = [] if after is None else [after]

    def body(*refs):
        srcs_, lands = refs[:n], refs[n:2 * n]
        token = refs[-1]
        sem_refs = refs[2 * n + len(extra):]
        for g, (at, size) in enumerate(zip(starts, sizes)):
            send_sems, recv_sems, local_sems = sem_refs[3 * g:3 * g + 3]
            for cp in _direct_copies(srcs_[at:at + size], lands[at:at + size], send_sems, recv_sems, local_sems, kind):
                cp.start()
        token[...] = jnp.zeros_like(token)

    sems = tuple(t for size in sizes for t in (pltpu.SemaphoreType.DMA((size * peers,)), pltpu.SemaphoreType.DMA((size * peers,)),
                                               pltpu.SemaphoreType.DMA((size,))))
    thru = tuple(pltpu.HBM(s.shape, s.dtype) for s in srcs) + tuple(pltpu.HBM(shp, s.dtype) for shp, s in zip(land_shapes, srcs))
    ins = [pltpu.with_memory_space_constraint(s, pltpu.HBM) for s in srcs]
    ins += [pltpu.with_memory_space_constraint(lax.empty(shp, s.dtype), pltpu.HBM) for shp, s in zip(land_shapes, srcs)]
    out = pl.pallas_call(
        body, name=name, out_shape=sems + thru + (jax.ShapeDtypeStruct((SUBLANE, LANE), F32),),
        in_specs=[HBM] * (2 * n) + [ANY] * len(extra),
        out_specs=(SEM,) * len(sems) + (HBM,) * (2 * n) + (pl.BlockSpec(memory_space=pltpu.VMEM),),
        input_output_aliases={i: len(sems) + i for i in range(2 * n)},
        compiler_params=pltpu.CompilerParams(has_side_effects=EFFECT))(*ins, *extra)
    arrays = out[len(sems):-1]
    started = [tuple(out[3 * g:3 * g + 3]) + tuple(arrays[at:at + size]) + tuple(arrays[n + at:n + at + size])
               for g, (at, size) in enumerate(zip(starts, sizes))]
    return started, out[-1]


def _exchange_wait(started, after, kind, *, name):
    n = (len(started) - 3) // 2
    sems, arrays = started[:3], started[3:]

    def body(*refs):
        srcs_, lands = refs[:n], refs[n:2 * n]
        send_sems, recv_sems, local_sems = refs[2 * n:2 * n + 3]
        for cp in _direct_copies(srcs_, lands, send_sems, recv_sems, local_sems, kind):
            cp.wait()

    out = pl.pallas_call(
        body, name=name, out_shape=tuple(pltpu.HBM(a.shape, a.dtype) for a in arrays),
        in_specs=[HBM] * (2 * n) + [SEM] * 3 + [ANY], out_specs=(HBM,) * (2 * n),
        input_output_aliases={i: i for i in range(2 * n)},
        compiler_params=pltpu.CompilerParams(has_side_effects=EFFECT))(*arrays, *sems, after)
    return out[n:]


def _adamw_reduce(w, parts, m, v, *, name, after=None):
    layers, r, c = w.shape
    assert len(parts) == layers
    senders = parts[0].shape[0]
    tr = _tile(r, 512, 16)
    tiles = r // tr
    bc1 = 1.0 - ADAM_B1 ** ADAM_STEP
    bc2 = 1.0 - ADAM_B2 ** ADAM_STEP

    def body(w_ref, *rest):
        p_refs = rest[:layers]
        m_ref, v_ref, g_ref, d_ref, nm_ref, nv_ref = rest[layers:]

        def update(p_ref):
            g = p_ref[0, :, pl.ds(0, c)].astype(F32)
            for s in range(1, senders):
                g = g + p_ref[s, :, pl.ds(0, c)].astype(F32)
            nm = ADAM_B1 * m_ref[0] + (1.0 - ADAM_B1) * g
            nv = ADAM_B2 * v_ref[0] + (1.0 - ADAM_B2) * (g * g)
            g_ref[0] = g
            nm_ref[0] = nm
            nv_ref[0] = nv
            d_ref[0] = -ADAM_LR * ((nm / bc1) / (jnp.sqrt(nv / bc2) + ADAM_EPS) + ADAM_WD * w_ref[0])

        for layer in range(layers):
            pl.when(pl.program_id(0) == layer)(functools.partial(update, p_refs[layer]))

    def part_spec(layer, shape):
        rest = 0 if layer > 0 else tiles - 1
        return pl.BlockSpec((senders, tr, shape[2]), lambda l, i: (0, jnp.where(l == layer, i, rest), 0))

    spec = pl.BlockSpec((1, tr, c), lambda l, i: (l, i, 0))
    out = jax.ShapeDtypeStruct((layers, r, c), F32)
    return _pcall(body, name=name, out_shape=(out,) * 4, grid=(layers, tiles),
                  in_specs=[spec] + [part_spec(layer, p.shape) for layer, p in enumerate(parts)] + [spec, spec],
                  out_specs=(spec,) * 4, semantics=("arbitrary", "arbitrary"), vmem_limit=VMEM_LIMIT, after=after)(w, *parts, m, v)


def _pool_windows():
    return jnp.repeat(jnp.asarray(POOL_WINDOWS, F32), POOL_DIM // len(POOL_WINDOWS))[None, :]


def _block_diag_pairs(pool_w):
    z = jnp.zeros_like(pool_w[0])
    return jnp.stack([jnp.block([[pool_w[2 * b], z], [z, pool_w[2 * b + 1]]]) for b in range(2)])


def _pad_lanes(vec):
    return jnp.zeros((1, LANE), F32).at[0, :vec.shape[0]].set(vec)


FF_SHARD = D_FF // N_DEV
FF_BLOCK = 384
D_FF_PAD = N_DEV * FF_BLOCK


def _layer_fwd(x, p_i, wt, fetch):
    wt = {**wt, **fetch(0, x)}
    proj, h1 = _matmul(x, wt["w_in"], "nt", norm_g=wt["norm1_g"], name="mm_in")
    qkv = _qkv_prep_fwd(proj, wt["conv_qkv"], name="qkv_prep_fwd")
    g, beta = _gates_fwd(proj, wt["a_log"], wt["dt_bias"], name="gates_fwd")
    u, w, qg, kg, attn, tmats = _deltanet_prep(qkv, g, beta, name="deltanet_prep")
    wt.update(fetch(1, u))
    o, vn, states = _deltanet_scan(u, w, qg, kg, attn, g, name="deltanet_scan", after=wt.get("behind"))
    o_a = _apost_fwd(o, proj, wt["onorm_g"], name="apost_fwd")
    o_b = _pool_fwd(proj, wt["pool_win"], wt["pool_wbd"], wt["pool_scale"], name="pool_fwd")
    o_c = _sconv_fwd(proj, wt["sconv_w"], name="sconv_fwd")
    mixed = jnp.concatenate([o_a, o_b, o_c], axis=1)
    x1 = _matmul(mixed, wt["w_out"], "nn", res=x, name="mm_out")
    wt.update(fetch(2, x1))
    ff, gate, up, h2 = _swiglu_fwd(x1, wt["norm2_g"], wt["w_gate"], wt["w_up"], name="swiglu_fwd")
    wt.update(fetch(3, ff))
    x2 = _matmul(ff, wt["w_down"], "nn", res=x1, name="mm_down")
    wt.update(fetch(4, x2))
    x3, pgl, pp = _ple_fwd(x2, p_i, wt["ple_gate"], wt["ple_proj"], name="ple_fwd")
    saved = dict(x=x, h1=h1, proj=proj, qkv=qkv, g=g, beta=beta, o=o, states=states, tmats=tmats, mixed=mixed, x1=x1, h2=h2,
                 gate=gate, up=up, ff=ff, x2=x2, pgl=pgl, pp=pp, p=p_i, w=w, qg=qg, kg=kg, attn=attn, vn=vn, wt=wt)
    return x3, saved


def _col_blocks(g):
    a = g.shape[0]
    return jnp.transpose(g.reshape(a, N_DEV, -1), (1, 0, 2))


def _cols_joined(blocks):
    return jnp.transpose(blocks, (1, 0, 2)).reshape(blocks.shape[1], -1)


def _layer_bwd(dx3, sv, emit, after=None):
    gr, big = {}, {}
    wt = sv["wt"]
    rows = D_MODEL // N_DEV
    dpgl, dpp = _ple_bwd(dx3, sv["pgl"], sv["pp"], name="ple_bwd", after=after)
    big["ple_proj"] = _matmul(sv["p"], dpp, "tn", out_blocked=(N_DEV, rows), out_dtype=BF16, name="mm_dplep")
    big["ple_gate"] = _matmul(sv["x2"], dpgl, "tn", out_dtype=BF16, name="mm_dpleg").reshape(N_DEV, rows, D_MODEL)
    dx2 = _matmul(dpgl, wt["ple_gate"], "nt", res=dx3, name="mm_dx2")
    big["w_down"] = _matmul(sv["ff"], dx2, "tn", out_dtype=BF16, name="mm_ddown").reshape(N_DEV, FF_BLOCK, D_MODEL)
    dgate, dup = _swiglu_bwd(dx2, wt["w_down"], sv["gate"], sv["up"], name="swiglu_bwd", after=emit(0, big))
    big["w_gate"] = _matmul(dgate, sv["h2"], "tn", out_dtype=BF16, name="mm_dgate").reshape(N_DEV, FF_BLOCK, D_MODEL)
    big["w_up"] = _matmul(dup, sv["h2"], "tn", out_dtype=BF16, name="mm_dup").reshape(N_DEV, FF_BLOCK, D_MODEL)
    dh2 = _matmul(dgate, wt["w_gate"], "nn", name="mm_dh2_gate")
    dh2 = _matmul(dup, wt["w_up"], "nn", res=dh2, name="mm_dh2_up")
    dx1, gr["norm2_g"] = _rmsnorm_bwd(sv["x1"], wt["norm2_g"], dh2, dx2, name="rmsnorm_bwd")
    big["w_out"] = _matmul(sv["mixed"], dx1, "tn", out_dtype=BF16, name="mm_dout").reshape(N_DEV, rows, D_MODEL)
    dmixed = _matmul(dx1, wt["w_out"], "nt", name="mm_dmixed", after=emit(1, big))
    proj = sv["proj"]
    dcb, dcc, dch, dsconv = _sconv_bwd(proj, wt["sconv_w"], dmixed, name="sconv_bwd")
    big["sconv_w"] = _col_blocks(dsconv)
    dhp, dwbd, gr["pool_scale"] = _pool_bwd(proj, wt["pool_win"], wt["pool_wbd"], wt["pool_scale"], dmixed, name="pool_bwd")
    half = LANE // 2
    gr["pool_w"] = jnp.stack([dwbd[0, :half, :half], dwbd[0, half:, half:], dwbd[1, :half, :half], dwbd[1, half:, half:]])
    do, dz, gr["onorm_g"] = _apost_bwd(sv["o"], proj, wt["onorm_g"], dmixed, name="apost_bwd")
    dvn, dstates = _deltanet_bscan(sv["w"], sv["qg"], sv["kg"], sv["attn"], sv["g"], do, name="deltanet_bscan")
    dqkv_h, dg, dbeta = _deltanet_post(sv["qkv"], sv["g"], sv["beta"], sv["tmats"], sv["states"], dstates, do, dvn, sv["vn"],
                                       name="deltanet_post")
    dab, dalog, ddtb = _gates_bwd(proj, wt["a_log"], wt["dt_bias"], dg, dbeta, name="gates_bwd")
    gr["a_log"], gr["dt_bias"] = dalog[0, :HEADS], ddtb[0, :HEADS]
    dqkv, dconv = _qkv_prep_bwd(proj, wt["conv_qkv"], dqkv_h, name="qkv_prep_bwd")
    big["conv_qkv"] = _col_blocks(dconv)
    dproj = jnp.concatenate([dqkv, dz, dab, dhp, dcb, dcc, dch], axis=1)
    dwin = _matmul(dproj, sv["h1"], "tn", out_dtype=BF16, name="mm_din")
    big["w_in"] = jnp.concatenate([dwin[:AB_COL + 2 * HEADS], dwin[AB_COL + LANE:]], axis=0).reshape(N_DEV, -1, D_MODEL)
    dh1 = _matmul(dproj, wt["w_in"], "nn", name="mm_dh1", after=emit(2, big))
    dx, gr["norm1_g"] = _rmsnorm_bwd(sv["x"], wt["norm1_g"], dh1, dx1, name="rmsnorm_bwd")
    return dx, gr


FETCH_GROUPS = (("w_in", "conv_qkv", "sconv_w"), ("w_out",), ("w_gate", "w_up"), ("w_down",), ("ple_gate", "ple_proj"))
EMIT_GROUPS = (("ple_proj", "ple_gate", "w_down"), ("w_gate", "w_up", "w_out"), ("w_in", "conv_qkv", "sconv_w"))


def _small_weights(w, i):
    return dict(
        norm1_g=w["norm1_g"][i][None], norm2_g=w["norm2_g"][i][None], onorm_g=w["onorm_g"][i][None],
        a_log=_pad_lanes(w["a_log"][i]), dt_bias=_pad_lanes(w["dt_bias"][i]),
        pool_scale=w["pool_scale"][i][None], pool_win=_pool_windows(), pool_wbd=_block_diag_pairs(w["pool_w"][i]))


def _as_read(name, gathered):
    if name == "w_in":
        rows = gathered[:, :D_IN // N_DEV].reshape(-1, D_MODEL)
        return jnp.concatenate([rows[:AB_COL + 2 * HEADS], jnp.zeros((LANE - 2 * HEADS, D_MODEL), BF16),
                                rows[AB_COL + 2 * HEADS:]], axis=0)
    if name in ("conv_qkv", "sconv_w"):
        return _cols_joined(gathered)
    if name == "ple_proj":
        return gathered
    return gathered.reshape(-1, D_MODEL)


def _layer_weights(gathered, w, i):
    return {**_small_weights(w, i), **{k: _as_read(k, g) for k, g in gathered.items()}}


def _local_step(x, p, target, layers, final_g):
    saved = []
    h = x
    for i in range(DEPTH):
        replicated = {k: v for k, v in layers[i].items() if k not in SHARDED}
        h, sv = _layer_fwd(h, p[i], replicated, lambda group, after, i=i: {k: layers[i][k] for k in FETCH_GROUPS[group]})
        saved.append(sv)
    dx, dgf, loss = _loss_head(h, final_g, target, name="loss_head")
    big, small = [{} for _ in range(DEPTH)], [None] * DEPTH
    for i in reversed(range(DEPTH)):
        dx, small[i] = _layer_bwd(dx, saved[i], lambda group, blocks, i=i: big[i].update({k: blocks[k] for k in EMIT_GROUPS[group]}))
    return loss, dx, big, small, dgf


SHARDED = ("w_in", "w_gate", "w_up", "w_down", "w_out", "ple_gate", "ple_proj", "conv_qkv", "sconv_w")
SMALL = ("norm1_g", "a_log", "dt_bias", "onorm_g", "pool_w", "pool_scale", "norm2_g", "final_g")
SLAB_COLS = 1024


def _payload(name, shard):
    if name in ("conv_qkv", "sconv_w"):
        return shard
    out = shard.astype(BF16)
    if name in ("w_gate", "w_up", "w_down"):
        out = jnp.pad(out, ((0, FF_BLOCK - FF_SHARD), (0, 0)))
    if name == "w_in":
        out = jnp.pad(out, ((0, -out.shape[0] % (2 * SUBLANE)), (0, 0)))
    return out


TRANSPOSED = ("w_in", "w_gate", "w_up")


def _ff_rows(t):
    return jnp.transpose(t, (0, 2, 1))


def _slab_rows(shape):
    size = 1
    for s in shape:
        size *= s
    return SUBLANE * -(-size // (SUBLANE * SLAB_COLS))


def _pack_slab(parts, extra_row):
    rows = []
    for name in SMALL:
        flat = parts[name].reshape(-1)
        nrow = _slab_rows(parts[name].shape)
        rows.append(jnp.pad(flat, (0, nrow * SLAB_COLS - flat.shape[0])).reshape(nrow, SLAB_COLS))
    rows.append(jnp.pad(extra_row, ((0, SUBLANE - 1), (0, 0))))
    return jnp.concatenate(rows, axis=0)


def _unpack_slab(slab, shapes):
    out, row = {}, 0
    for name in SMALL:
        size = 1
        for s in shapes[name]:
            size *= s
        out[name] = slab[row:row + _slab_rows(shapes[name])].reshape(-1)[:size].reshape(shapes[name])
        row += _slab_rows(shapes[name])
    return out, row


def kernel(x, p, norm1_g, w_in, conv_qkv, a_log, dt_bias, onorm_g, pool_w, pool_scale, sconv_w, w_out, norm2_g, w_gate, w_up, w_down, ple_proj, ple_gate, final_g, loss_target, m_norm1_g, m_w_in, m_conv_qkv, m_a_log, m_dt_bias, m_onorm_g, m_pool_w, m_pool_scale, m_sconv_w, m_w_out, m_norm2_g, m_w_gate, m_w_up, m_w_down, m_ple_proj, m_ple_gate, m_final_g, v_norm1_g, v_w_in, v_conv_qkv, v_a_log, v_dt_bias, v_onorm_g, v_pool_w, v_pool_scale, v_sconv_w, v_w_out, v_norm2_g, v_w_gate, v_w_up, v_w_down, v_ple_proj, v_ple_gate, v_final_g):
    names = ["norm1_g", "w_in", "conv_qkv", "a_log", "dt_bias", "onorm_g", "pool_w", "pool_scale", "sconv_w", "w_out", "norm2_g",
             "w_gate", "w_up", "w_down", "ple_proj", "ple_gate", "final_g"]
    w = dict(zip(names, [norm1_g, w_in, conv_qkv, a_log, dt_bias, onorm_g, pool_w, pool_scale, sconv_w, w_out, norm2_g, w_gate, w_up,
                         w_down, ple_proj, ple_gate, final_g]))
    m = dict(zip(names, [m_norm1_g, m_w_in, m_conv_qkv, m_a_log, m_dt_bias, m_onorm_g, m_pool_w, m_pool_scale, m_sconv_w, m_w_out,
                         m_norm2_g, m_w_gate, m_w_up, m_w_down, m_ple_proj, m_ple_gate, m_final_g]))
    v = dict(zip(names, [v_norm1_g, v_w_in, v_conv_qkv, v_a_log, v_dt_bias, v_onorm_g, v_pool_w, v_pool_scale, v_sconv_w, v_w_out,
                         v_norm2_g, v_w_gate, v_w_up, v_w_down, v_ple_proj, v_ple_gate, v_final_g]))
    w.update({k: _ff_rows(w[k]) for k in TRANSPOSED})

    first, rest = FETCH_GROUPS[0], tuple(k for members in FETCH_GROUPS[1:] for k in members)
    gathered = dict(zip(first, _all_gather([_payload(k, w[k][0]) for k in first], name="all_gather_weights")))
    (flying0,), token = _exchange_start([[_payload(k, w[k][0]) for k in rest]], CHIP_GATHER, name="gather_start_0",
                                        after=gathered[first[0]])
    replicated = [_small_weights(w, i) for i in range(DEPTH)]
    replicated[0]["norm1_g"] = replicated[0]["norm1_g"] + token[0, 0]
    for group in (m, v):
        group.update({k: _ff_rows(group[k] + token[0, 0]) for k in TRANSPOSED})
    flying1 = []

    def fetch(i, group, after):
        if i == 0 and group == 1:
            landed = _exchange_wait(flying0, after, CHIP_GATHER, name="gather_wait_0")
            gathered.update(zip(rest, _pair_swap(landed, name="pair_swap")))
            started, token = _exchange_start([[_payload(k, w[k][1]) for k in SHARDED]], CHIP_GATHER, name="gather_start_1",
                                             after=gathered[rest[0]])
            flying1.extend(started)
            return {**{k: _as_read(k, gathered[k]) for k in FETCH_GROUPS[group]}, "behind": token}
        if i == 1 and group == 0:
            landed = _exchange_wait(flying1[0], after, CHIP_GATHER, name="gather_wait_1")
            gathered.update(zip(SHARDED, _pair_swap(landed, name="pair_swap")))
        return {k: _as_read(k, gathered[k]) for k in FETCH_GROUPS[group]}

    def reduce_scatter_start(members, blocks, tag):
        mine = [blocks[k] for k in members]
        theirs = _pair_exchange(mine, name="pair_exchange")
        sums = [_pair_add(a, b, name="pair_add") for a, b in zip(mine, theirs)]
        (started,), token = _exchange_start([sums], CHIP_SCATTER, name="exchange_start_" + tag)
        return started, token

    h, saved0 = _layer_fwd(x[0], p[0, 0], replicated[0], functools.partial(fetch, 0))
    h, saved1 = _layer_fwd(h, p[1, 0], replicated[1], functools.partial(fetch, 1))
    dx, dgf, loss_part = _loss_head(h, final_g[None], loss_target[0], name="loss_head")
    small, big1, flying0 = [None] * DEPTH, {}, []
    dx, small[1] = _layer_bwd(dx, saved1, lambda group, blocks: big1.update({k: blocks[k] for k in EMIT_GROUPS[group]}))
    flying1, token = reduce_scatter_start(SHARDED, big1, "1")

    def emit(group, blocks):
        started, token = reduce_scatter_start(EMIT_GROUPS[group], blocks, f"0_{group}")
        flying0.append(started)
        return token

    dx, small[0] = _layer_bwd(dx, saved0, emit, after=token)
    received = [{}, dict(zip(SHARDED, _exchange_wait(flying1, dx, CHIP_SCATTER, name="exchange_wait_1")))]
    for group, members in enumerate(EMIT_GROUPS):
        received[0].update(zip(members, _exchange_wait(flying0[group], dx, CHIP_SCATTER, name=f"exchange_wait_0_{group}")))

    grads = {k: jnp.stack([small[i][k] for i in range(DEPTH)]) for k in small[0]}
    grads = {k: g[:, 0] if k in ("norm1_g", "norm2_g", "onorm_g", "pool_scale") else g for k, g in grads.items()}
    grads["final_g"] = dgf[0]
    loss_row = jnp.pad(loss_part, ((0, 0), (0, SLAB_COLS - LANE)))
    (small_flying,), token = _exchange_start([[_pack_slab(grads, loss_row)]], GATHER, name="small_gather_start")

    out_g, out_d, out_m, out_v = {}, {}, {}, {}
    for k in SHARDED:
        out_g[k], out_d[k], out_m[k], out_v[k] = _adamw_reduce(w[k], [received[i][k] for i in range(DEPTH)], m[k], v[k],
                                                                name="adamw_" + k, after=token)
    behind_all = jnp.stack([out_v[k][0, 0, 0] for k in SHARDED])
    (small_parts,) = _exchange_wait(small_flying, behind_all, GATHER, name="small_gather_wait")
    zero_row = jnp.zeros((1, SLAB_COLS), F32)
    slabs = _adamw_reduce(_pack_slab(w, zero_row)[None], [small_parts], _pack_slab(m, zero_row)[None],
                          _pack_slab(v, zero_row)[None], name="adamw_small")
    slabs = [s[0] for s in slabs]
    shapes = {k: w[k].shape for k in SMALL}
    for dst, slab in zip((out_g, out_d, out_m, out_v), slabs):
        vals, _ = _unpack_slab(slab, shapes)
        dst.update(vals)
    _, loss_at = _unpack_slab(slabs[0], shapes)
    loss = slabs[0][loss_at, 0]
    for group in (out_g, out_d, out_m, out_v):
        group.update({k: _ff_rows(group[k]) for k in TRANSPOSED})

    return (loss, dx[None], *[out_g[k] for k in names], *[out_d[k] for k in names], *[out_m[k] for k in names],
            *[out_v[k] for k in names])
```

```python
import functools

import jax
import jax.numpy as jnp
from jax import lax
from jax.experimental import pallas as pl
from jax.experimental.pallas import tpu as pltpu

F32 = jnp.float32
BF16 = jnp.bfloat16

D_MODEL = 1024
DEPTH = 2
PLE_DIM = 256
EPS = 1e-6
HEAD_DIM = 128
HEADS = 4
A_DIM = HEADS * HEAD_DIM
QKV_TAPS = 4
CHUNK = 64
POOL_WINDOWS = (2, 4, 8, 16)
POOL_DIM = 256
CONV_DIM = 256
CONV_TAPS = 3
D_FF = 2816
D_IN = 3080
D_IN_PAD = 3200
AB_COL = 2048
N_DEV = 8

ADAM_LR = 0.001
ADAM_B1 = 0.9
ADAM_B2 = 0.999
ADAM_EPS = 1e-08
ADAM_WD = 0.01
ADAM_STEP = 10

LANE = 128
SUBLANE = 8
VMEM_BYTES_V7X = 64 * 1024 * 1024
VMEM_LIMIT = 48 * 1024 * 1024

_HI = lax.Precision.HIGHEST
NN = ((1,), (0,))
NT = ((1,), (1,))
TN = ((0,), (0,))
MESH = pl.DeviceIdType.MESH


def _dot(a, b, dims, hi=False):
    if hi:
        return lax.dot_general(a, b, (dims, ((), ())), precision=_HI, preferred_element_type=F32)
    return lax.dot_general(a.astype(BF16), b.astype(BF16), (dims, ((), ())), preferred_element_type=F32)


def _pcall(body, *, name, out_shape, grid=(), in_specs=None, out_specs=None, scratch_shapes=(), semantics=None,
           vmem_limit=None, after=None, **kw):
    params = {}
    if semantics is not None:
        params["dimension_semantics"] = semantics
    if vmem_limit is not None:
        params["vmem_limit_bytes"] = vmem_limit
    if after is not None:
        n_in, inner = len(in_specs), body
        body = lambda *refs: inner(*refs[:n_in], *refs[n_in + 1:])
        in_specs = list(in_specs) + [pl.BlockSpec(after.shape, lambda *_: (0,) * after.ndim)]
    call = pl.pallas_call(
        body, name=name, out_shape=out_shape, grid=grid, in_specs=in_specs, out_specs=out_specs,
        scratch_shapes=list(scratch_shapes), compiler_params=pltpu.CompilerParams(**params), **kw)
    return call if after is None else (lambda *args: call(*args, after))


def _sigmoid(x):
    return 1.0 / (1.0 + jnp.exp(-x))


def _softplus(x):
    return jnp.maximum(x, 0.0) + jnp.log(1.0 + jnp.exp(-jnp.abs(x)))


def _tile(n, cap, mult):
    if n <= cap:
        return n
    best = None
    for t in range(mult, cap + 1, mult):
        if n % t == 0:
            best = t
    assert best is not None, (n, cap, mult)
    return best


ROWS_PER_STEP = 512
NARROW_RESULT = 1024
COLS_PER_DOT = 640


def _resident(weight):
    return pl.BlockSpec(weight.shape, lambda i: (0,) * weight.ndim, pipeline_mode=pl.Buffered(1))


def _matmul_rows(a, b, mode, *, name, res=None, out_dtype=F32, b_blocked=False, after=None, norm_g=None):
    m, k = a.shape
    if b_blocked:
        nb, _, bw = b.shape
        n = nb * bw if mode == "nn" else b.shape[1]
    else:
        n = b.shape[1] if mode == "nn" else b.shape[0]
    tm = _tile(m, ROWS_PER_STEP if n > NARROW_RESULT else 2 * ROWS_PER_STEP, 16)
    cn = bw if (b_blocked and mode == "nn") else _tile(n, COLS_PER_DOT, LANE)
    has_res = res is not None
    normed = norm_g is not None

    def body(*refs):
        a_ref, b_ref = refs[0], refs[1]
        g_ref = refs[2] if normed else None
        res_ref = refs[2 + normed] if has_res else None
        o_ref = refs[2 + normed + has_res]
        if normed:
            av = _rms_normed(a_ref[...], g_ref[...])
            refs[3 + normed + has_res][...] = av
        elif not (b_blocked and mode == "nt"):
            av = a_ref[...].astype(BF16)
        for j in range(n // cn):
            cols = pl.ds(j * cn, cn)
            if mode == "nn":
                part = _dot(av, b_ref[j] if b_blocked else b_ref[:, cols], NN)
            elif not b_blocked:
                part = _dot(av, b_ref[cols, :], NT)
            else:
                part = None
                for s in range(nb):
                    term = _dot(a_ref[:, pl.ds(s * bw, bw)], b_ref[s, cols, :], NT)
                    part = term if part is None else part + term
            if has_res:
                part = part + res_ref[:, cols]
            o_ref[:, cols] = part.astype(o_ref.dtype)

    row = lambda width: pl.BlockSpec((tm, width), lambda i: (i, 0))
    whole = _resident(b)
    ins = [a, b] + ([norm_g] if normed else []) + ([res] if has_res else [])
    specs = [row(k), whole] + ([pl.BlockSpec((1, k), lambda i: (0, 0))] if normed else []) + ([row(n)] if has_res else [])
    out = jax.ShapeDtypeStruct((m, n), out_dtype)
    return _pcall(body, name=name, out_shape=(out, jax.ShapeDtypeStruct((m, k), BF16)) if normed else out, grid=(m // tm,),
                  in_specs=specs, out_specs=(row(n), row(k)) if normed else row(n), semantics=("parallel",),
                  vmem_limit=VMEM_LIMIT, after=after)(*ins)


def _matmul_norm_bwd(a, b, x, g, dres, *, name, res=None, after=None):
    m, k = a.shape
    d = b.shape[1]
    tm = _tile(m, ROWS_PER_STEP, 16)
    cn = _tile(d, COLS_PER_DOT, LANE)
    has_res = res is not None

    def body(*refs):
        a_ref, b_ref, x_ref, g_ref, dres_ref = refs[:5]
        res_ref = refs[5] if has_res else None
        dx_ref, dg_ref = refs[5 + has_res], refs[6 + has_res]
        av = a_ref[...].astype(BF16)
        for j in range(d // cn):
            cols = pl.ds(j * cn, cn)
            part = _dot(av, b_ref[:, cols], NN)
            dx_ref[:, cols] = part + res_ref[:, cols] if has_res else part
        dhv = dx_ref[...]
        xv = x_ref[...]
        r = lax.rsqrt(jnp.mean(xv * xv, axis=-1, keepdims=True) + EPS)
        xhat = xv * r
        dhg = dhv * g_ref[...]
        dx_ref[...] = dres_ref[...] + r * (dhg - xhat * jnp.mean(dhg * xhat, axis=-1, keepdims=True))
        part_g = jnp.sum(dhv * xhat, axis=0, keepdims=True)

        @pl.when(pl.program_id(0) == 0)
        def _():
            dg_ref[...] = part_g

        @pl.when(pl.program_id(0) > 0)
        def _():
            dg_ref[...] += part_g

    row = lambda width: pl.BlockSpec((tm, width), lambda i: (i, 0))
    vec = pl.BlockSpec((1, d), lambda i: (0, 0))
    ins = [a, b, x, g, dres] + ([res] if has_res else [])
    specs = [row(k), _resident(b), row(d), vec, row(d)] + ([row(d)] if has_res else [])
    return _pcall(body, name=name, out_shape=(jax.ShapeDtypeStruct((m, d), F32), jax.ShapeDtypeStruct((1, d), F32)),
                  grid=(m // tm,), in_specs=specs, out_specs=(row(d), vec), semantics=("arbitrary",), vmem_limit=VMEM_LIMIT,
                  after=after)(*ins)


def _rms_normed(xv, gv):
    return (xv * lax.rsqrt(jnp.mean(xv * xv, axis=-1, keepdims=True) + EPS) * gv).astype(BF16)


def _matmul(a, b, mode, *, name, res=None, out_dtype=F32, b_blocked=False, out_blocked=None, after=None, norm_g=None):
    if mode != "tn":
        return _matmul_rows(a, b, mode, name=name, res=res, out_dtype=out_dtype, b_blocked=b_blocked, after=after, norm_g=norm_g)
    assert res is None and not b_blocked and after is None and norm_g is None
    (t, m), (t2, n) = a.shape, b.shape
    assert t == t2, (a.shape, b.shape)
    tm = _tile(m, 1024, LANE)
    tn = _tile(n, COLS_PER_DOT, LANE)
    if out_blocked is not None:
        assert out_blocked[0] * out_blocked[1] == n
        tn = out_blocked[1]

    def body(a_ref, b_ref, o_ref):
        part = _dot(a_ref[...], b_ref[...], TN).astype(o_ref.dtype)
        if out_blocked is None:
            o_ref[...] = part
        else:
            o_ref[0] = part

    o_spec = (pl.BlockSpec((tm, tn), lambda i, j: (i, j)) if out_blocked is None
              else pl.BlockSpec((1, tm, tn), lambda i, j: (j, i, 0)))
    o_shape = (m, n) if out_blocked is None else (out_blocked[0], m, out_blocked[1])
    return _pcall(body, name=name, out_shape=jax.ShapeDtypeStruct(o_shape, out_dtype), grid=(m // tm, n // tn),
                  in_specs=[pl.BlockSpec((t, tm), lambda i, j: (0, i)), pl.BlockSpec((t, tn), lambda i, j: (0, j))],
                  out_specs=o_spec, semantics=("parallel", "parallel"), vmem_limit=VMEM_LIMIT)(a, b)


ROW_TILE = 512


def _rows(t, width, idx=0):
    return pl.BlockSpec((ROW_TILE, width), lambda i: (i, idx))


def _vec(width):
    return pl.BlockSpec((1, width), lambda i: (0, 0))


def _swiglu_fwd(x, norm_g, w_gate, w_up, *, name):
    t, k = x.shape
    f = w_gate.shape[0]
    tm = _tile(t, ROWS_PER_STEP, 16)
    cn = _tile(f, COLS_PER_DOT, LANE)

    def body(x_ref, g_ref, wg_ref, wu_ref, ff_ref, gate_ref, up_ref, h_ref):
        hv = _rms_normed(x_ref[...], g_ref[...])
        h_ref[...] = hv
        for j in range(f // cn):
            cols = pl.ds(j * cn, cn)
            gv = _dot(hv, wg_ref[cols, :], NT)
            uv = _dot(hv, wu_ref[cols, :], NT)
            gate_ref[:, cols] = gv.astype(BF16)
            up_ref[:, cols] = uv.astype(BF16)
            ff_ref[:, cols] = (gv * _sigmoid(gv) * uv).astype(BF16)

    row = lambda width: pl.BlockSpec((tm, width), lambda i: (i, 0))
    out = jax.ShapeDtypeStruct((t, f), BF16)
    return _pcall(body, name=name, out_shape=(out,) * 3 + (jax.ShapeDtypeStruct((t, k), BF16),), grid=(t // tm,),
                  in_specs=[row(k), pl.BlockSpec((1, k), lambda i: (0, 0)), _resident(w_gate), _resident(w_up)],
                  out_specs=(row(f),) * 3 + (row(k),), semantics=("parallel",), vmem_limit=VMEM_LIMIT)(x, norm_g, w_gate, w_up)


def _swiglu_bwd(dx2, w_down, gate, up, *, name, after=None):
    t, d = dx2.shape
    f = w_down.shape[0]
    tm = _tile(t, ROWS_PER_STEP, 16)
    cn = _tile(f, COLS_PER_DOT, LANE)

    def body(dx_ref, w_ref, gate_ref, up_ref, dgate_ref, dup_ref):
        dxv = dx_ref[...].astype(BF16)
        for j in range(f // cn):
            cols = pl.ds(j * cn, cn)
            dffv = _dot(dxv, w_ref[cols, :], NT)
            gv = gate_ref[:, cols].astype(F32)
            sig = _sigmoid(gv)
            dgate_ref[:, cols] = (dffv * up_ref[:, cols].astype(F32) * sig * (1.0 + gv * (1.0 - sig))).astype(BF16)
            dup_ref[:, cols] = (dffv * gv * sig).astype(BF16)

    row = lambda width: pl.BlockSpec((tm, width), lambda i: (i, 0))
    out = jax.ShapeDtypeStruct((t, f), BF16)
    return _pcall(body, name=name, out_shape=(out, out), grid=(t // tm,), in_specs=[row(d), _resident(w_down), row(f), row(f)],
                  out_specs=(row(f), row(f)), semantics=("parallel",), vmem_limit=VMEM_LIMIT, after=after)(dx2, w_down, gate, up)


def _ple_fwd(x2, p, w_gate, w_proj, *, name):
    t, d = x2.shape
    nb, pdim, bw = w_proj.shape
    tm = _tile(t, ROWS_PER_STEP, 16)
    cn = _tile(d, COLS_PER_DOT, LANE)

    def body(x_ref, p_ref, wg_ref, wp_ref, x3_ref, pgl_ref, pp_ref):
        xb = x_ref[...].astype(BF16)
        pb = p_ref[...].astype(BF16)
        per = cn // bw
        for c in range(d // cn):
            cols = pl.ds(c * cn, cn)
            pgl = _dot(xb, wg_ref[:, cols], NN)
            pp = jnp.concatenate([_dot(pb, wp_ref[c * per + j], NN) for j in range(per)], axis=1)
            pgl_ref[:, cols] = pgl
            pp_ref[:, cols] = pp
            x3_ref[:, cols] = x_ref[:, cols] + _sigmoid(pgl) * pp

    row = lambda width: pl.BlockSpec((tm, width), lambda i: (i, 0))
    out = jax.ShapeDtypeStruct((t, d), F32)
    return _pcall(body, name=name, out_shape=(out,) * 3, grid=(t // tm,),
                  in_specs=[row(d), row(pdim), _resident(w_gate), _resident(w_proj)], out_specs=(row(d),) * 3,
                  semantics=("parallel",), vmem_limit=VMEM_LIMIT)(x2, p, w_gate, w_proj)


def _ple_bwd(dx3, pgl, pp, *, name, after=None):
    t, d = dx3.shape

    def body(dx_ref, pgl_ref, pp_ref, dpgl_ref, dpp_ref):
        dxv = dx_ref[...]
        sig = _sigmoid(pgl_ref[...])
        dpp_ref[...] = (dxv * sig).astype(BF16)
        dpgl_ref[...] = (dxv * pp_ref[...] * sig * (1.0 - sig)).astype(BF16)

    return _pcall(body, name=name, out_shape=(jax.ShapeDtypeStruct((t, d), BF16),) * 2, grid=(t // ROW_TILE,),
                  in_specs=[_rows(t, d)] * 3, out_specs=(_rows(t, d),) * 2, semantics=("parallel",), after=after)(dx3, pgl, pp)


def _loss_head(x3, g, target, *, name):
    t, d = x3.shape

    def body(x_ref, g_ref, t_ref, dx_ref, dg_ref, loss_ref):
        xv = x_ref[...]
        r = lax.rsqrt(jnp.mean(xv * xv, axis=-1, keepdims=True) + EPS)
        xhat = xv * r
        gv = g_ref[...]
        err = xhat * gv - t_ref[...]
        row_loss = jnp.sum(err * err, axis=-1, keepdims=True) * (0.5 / d)
        lpart = jnp.broadcast_to(jnp.sum(row_loss, axis=0, keepdims=True), (1, LANE))
        dy = err * (1.0 / d)
        dyg = dy * gv
        dx_ref[...] = r * (dyg - xhat * jnp.mean(dyg * xhat, axis=-1, keepdims=True))
        gpart = jnp.sum(dy * xhat, axis=0, keepdims=True)

        @pl.when(pl.program_id(0) == 0)
        def _():
            dg_ref[...] = gpart
            loss_ref[...] = lpart

        @pl.when(pl.program_id(0) > 0)
        def _():
            dg_ref[...] += gpart
            loss_ref[...] += lpart

    return _pcall(body, name=name,
                  out_shape=(jax.ShapeDtypeStruct((t, d), F32), jax.ShapeDtypeStruct((1, d), F32), jax.ShapeDtypeStruct((1, LANE), F32)),
                  grid=(t // ROW_TILE,), in_specs=[_rows(t, d), _vec(d), _rows(t, d)],
                  out_specs=(_rows(t, d), _vec(d), _vec(LANE)), semantics=("arbitrary",))(x3, g, target)


def _shift_down(x, d):
    if d == 0:
        return x
    row = lax.broadcasted_iota(jnp.int32, x.shape, 0)
    return jnp.where(row >= d, pltpu.roll(x, d, 0), 0.0)


def _shift_up(x, d):
    if d == 0:
        return x
    t = x.shape[0]
    row = lax.broadcasted_iota(jnp.int32, x.shape, 0)
    return jnp.where(row < t - d, pltpu.roll(x, t - d, 0), 0.0)


def _colsum(x):
    return jnp.sum(x, axis=0, keepdims=True)


def _col(t, idx_fn):
    return pl.BlockSpec((t, LANE), idx_fn)


def _conv_fwd(x, w_ref, taps):
    acc = None
    for j in range(taps):
        term = w_ref[pl.ds(j, 1), :] * _shift_down(x, taps - 1 - j)
        acc = term if acc is None else acc + term
    return acc


def _conv_bwd(x, dy, w_ref, dw_ref, taps):
    dx = None
    for j in range(taps):
        term = w_ref[pl.ds(j, 1), :] * _shift_up(dy, taps - 1 - j)
        dx = term if dx is None else dx + term
        dw_ref[pl.ds(j, 1), :] = _colsum(dy * _shift_down(x, taps - 1 - j))
    return dx


def _qkv_prep_fwd(proj, conv_w, *, name):
    t = proj.shape[0]
    scale = HEAD_DIM ** -0.5

    def body(x_ref, w_ref, o_ref):
        j = pl.program_id(0)
        c = _conv_fwd(x_ref[...], w_ref, QKV_TAPS)
        s = c * _sigmoid(c)
        r = lax.rsqrt(jnp.sum(s * s, axis=-1, keepdims=True) + EPS)
        f = jnp.where(j < 2 * HEADS, r, 1.0) * jnp.where(j < HEADS, scale, 1.0)
        o_ref[0] = s * f

    return _pcall(body, name=name, out_shape=jax.ShapeDtypeStruct((3 * HEADS, t, LANE), F32), grid=(3 * HEADS,),
                  in_specs=[_col(t, lambda j: (0, j)), pl.BlockSpec((QKV_TAPS, LANE), lambda j: (0, j))],
                  out_specs=pl.BlockSpec((1, t, LANE), lambda j: (j, 0, 0)), semantics=("parallel",),
                  vmem_limit=VMEM_LIMIT)(proj, conv_w)


def _qkv_prep_bwd(proj, conv_w, dqkv, *, name):
    t = proj.shape[0]
    scale = HEAD_DIM ** -0.5

    def body(x_ref, w_ref, d_ref, dx_ref, dw_ref):
        j = pl.program_id(0)
        xv = x_ref[...]
        c = _conv_fwd(xv, w_ref, QKV_TAPS)
        sig = _sigmoid(c)
        s = c * sig
        r = lax.rsqrt(jnp.sum(s * s, axis=-1, keepdims=True) + EPS)
        n0 = s * r
        dv = d_ref[0]
        dn0 = dv * jnp.where(j < HEADS, scale, 1.0)
        ds_norm = r * (dn0 - n0 * jnp.sum(dn0 * n0, axis=-1, keepdims=True))
        ds = jnp.where(j < 2 * HEADS, ds_norm, dv)
        dc = ds * sig * (1.0 + c * (1.0 - sig))
        dx_ref[...] = _conv_bwd(xv, dc, w_ref, dw_ref, QKV_TAPS).astype(BF16)

    return _pcall(body, name=name,
                  out_shape=(jax.ShapeDtypeStruct((t, 3 * A_DIM), BF16), jax.ShapeDtypeStruct((QKV_TAPS, 3 * A_DIM), F32)),
                  grid=(3 * HEADS,),
                  in_specs=[_col(t, lambda j: (0, j)), pl.BlockSpec((QKV_TAPS, LANE), lambda j: (0, j)),
                            pl.BlockSpec((1, t, LANE), lambda j: (j, 0, 0))],
                  out_specs=(_col(t, lambda j: (0, j)), pl.BlockSpec((QKV_TAPS, LANE), lambda j: (0, j))),
                  semantics=("parallel",), vmem_limit=VMEM_LIMIT)(proj, conv_w, dqkv)


def _lane_pick(x, lane_idx, lane):
    return jnp.broadcast_to(jnp.sum(jnp.where(lane == lane_idx, x, 0.0), axis=-1, keepdims=True), x.shape)


def _gates_fwd(proj, alog, dtb, *, name):
    t = proj.shape[0]

    def body(x_ref, alog_ref, dtb_ref, g_ref, b_ref):
        xv = x_ref[...]
        lane = lax.broadcasted_iota(jnp.int32, xv.shape, 1)
        gall = -jnp.exp(alog_ref[...]) * _softplus(xv + dtb_ref[...])
        ball = _sigmoid(xv)
        for h in range(HEADS):
            g_ref[h] = _lane_pick(gall, h, lane)
            b_ref[h] = _lane_pick(ball, HEADS + h, lane)

    out = jax.ShapeDtypeStruct((HEADS, t, LANE), F32)
    whole = pl.BlockSpec((HEADS, t, LANE), lambda i: (0, 0, 0))
    return _pcall(body, name=name, out_shape=(out, out), grid=(1,),
                  in_specs=[_col(t, lambda i: (0, AB_COL // LANE)), _vec(LANE), _vec(LANE)], out_specs=(whole, whole),
                  semantics=("arbitrary",), vmem_limit=VMEM_LIMIT)(proj, alog, dtb)


def _gates_bwd(proj, alog, dtb, dg, dbeta, *, name):
    t = proj.shape[0]

    def body(x_ref, alog_ref, dtb_ref, dg_ref, db_ref, dab_ref, dalog_ref, ddtb_ref):
        xv = x_ref[...]
        lane = lax.broadcasted_iota(jnp.int32, xv.shape, 1)
        lane1 = lax.broadcasted_iota(jnp.int32, (1, LANE), 1)
        z = xv + dtb_ref[...]
        nea = -jnp.exp(alog_ref[...])
        da_f = nea * _sigmoid(z)
        g_f = nea * _softplus(z)
        ball = _sigmoid(xv)
        db_f = ball * (1.0 - ball)
        dab = jnp.zeros_like(xv)
        dalog = jnp.zeros((1, LANE), F32)
        for h in range(HEADS):
            dgh = dg_ref[h]
            dab = dab + jnp.where(lane == h, dgh * da_f, 0.0) + jnp.where(lane == HEADS + h, db_ref[h] * db_f, 0.0)
            dalog = dalog + jnp.where(lane1 == h, _colsum(dgh * g_f), 0.0)
        dab_ref[...] = dab.astype(BF16)
        dalog_ref[...] = dalog
        ddtb_ref[...] = jnp.where(lane1 < HEADS, _colsum(dab), 0.0)

    whole = pl.BlockSpec((HEADS, t, LANE), lambda i: (0, 0, 0))
    vec = jax.ShapeDtypeStruct((1, LANE), F32)
    return _pcall(body, name=name, out_shape=(jax.ShapeDtypeStruct((t, LANE), BF16), vec, vec), grid=(1,),
                  in_specs=[_col(t, lambda i: (0, AB_COL // LANE)), _vec(LANE), _vec(LANE), whole, whole],
                  out_specs=(_col(t, lambda i: (0, 0)), _vec(LANE), _vec(LANE)), semantics=("arbitrary",),
                  vmem_limit=VMEM_LIMIT)(proj, alog, dtb, dg, dbeta)


Z_COL = 3 * A_DIM // LANE


def _apost_fwd(o, proj, gn, *, name):
    t = proj.shape[0]

    def body(o_ref, z_ref, gn_ref, y_ref):
        ov = o_ref[0]
        z = z_ref[...]
        r = lax.rsqrt(jnp.mean(ov * ov, axis=-1, keepdims=True) + EPS)
        y_ref[...] = (ov * r * gn_ref[...] * (z * _sigmoid(z))).astype(BF16)

    return _pcall(body, name=name, out_shape=jax.ShapeDtypeStruct((t, A_DIM), BF16), grid=(HEADS,),
                  in_specs=[pl.BlockSpec((1, t, LANE), lambda h: (h, 0, 0)), _col(t, lambda h: (0, Z_COL + h)),
                            pl.BlockSpec((1, LANE), lambda h: (0, 0))],
                  out_specs=_col(t, lambda h: (0, h)), semantics=("parallel",), vmem_limit=VMEM_LIMIT)(o, proj, gn)


def _apost_bwd(o, proj, gn, dmixed, *, name):
    t = proj.shape[0]

    def body(o_ref, z_ref, gn_ref, d_ref, do_ref, dz_ref, dgn_ref):
        ov = o_ref[0]
        z = z_ref[...]
        gnv = gn_ref[...]
        dv = d_ref[...]
        r = lax.rsqrt(jnp.mean(ov * ov, axis=-1, keepdims=True) + EPS)
        ohat = ov * r
        sig = _sigmoid(z)
        dy = dv * (z * sig)
        dz_ref[...] = (dv * ohat * gnv * sig * (1.0 + z * (1.0 - sig))).astype(BF16)
        dyo = dy * gnv
        do_ref[0] = r * (dyo - ohat * jnp.mean(dyo * ohat, axis=-1, keepdims=True))
        part = _colsum(dy * ohat)

        @pl.when(pl.program_id(0) == 0)
        def _():
            dgn_ref[...] = part

        @pl.when(pl.program_id(0) > 0)
        def _():
            dgn_ref[...] += part

    return _pcall(body, name=name,
                  out_shape=(jax.ShapeDtypeStruct((HEADS, t, LANE), F32), jax.ShapeDtypeStruct((t, A_DIM), BF16),
                             jax.ShapeDtypeStruct((1, LANE), F32)),
                  grid=(HEADS,),
                  in_specs=[pl.BlockSpec((1, t, LANE), lambda h: (h, 0, 0)), _col(t, lambda h: (0, Z_COL + h)),
                            pl.BlockSpec((1, LANE), lambda h: (0, 0)), _col(t, lambda h: (0, h))],
                  out_specs=(pl.BlockSpec((1, t, LANE), lambda h: (h, 0, 0)), _col(t, lambda h: (0, h)),
                             pl.BlockSpec((1, LANE), lambda h: (0, 0))),
                  semantics=("arbitrary",), vmem_limit=VMEM_LIMIT)(o, proj, gn, dmixed)


POOL_COL = (AB_COL + LANE) // LANE
CB_COL = POOL_COL + POOL_DIM // LANE
CC_COL = CB_COL + CONV_DIM // LANE
CH_COL = CC_COL + CONV_DIM // LANE
MAX_WIN_LOG2 = 4


def _window_sums(x, shift):
    sums = []
    cur = x
    for k in range(MAX_WIN_LOG2):
        cur = cur + shift(cur, 1 << k)
        sums.append(cur)
    return sums


def _pick_window(sums, win):
    out = sums[-1]
    for k in range(MAX_WIN_LOG2 - 2, -1, -1):
        out = jnp.where(win == float(2 << k), sums[k], out)
    return out


def _pool_counts(shape, win):
    row = lax.broadcasted_iota(jnp.int32, shape, 0).astype(F32)
    return jnp.minimum(row + 1.0, win)


def _pool_fwd(proj, win, wbd, scale, *, name):
    t = proj.shape[0]

    def body(x_ref, win_ref, w_ref, s_ref, y_ref):
        xv = x_ref[...]
        winv = win_ref[...]
        pooled = _pick_window(_window_sums(xv, _shift_down), winv) / _pool_counts(xv.shape, winv) - xv
        y_ref[...] = (_dot(pooled, w_ref[0], NN) * s_ref[...]).astype(BF16)

    nb = POOL_DIM // LANE
    vec = pl.BlockSpec((1, LANE), lambda b: (0, b))
    return _pcall(body, name=name, out_shape=jax.ShapeDtypeStruct((t, POOL_DIM), BF16), grid=(nb,),
                  in_specs=[_col(t, lambda b: (0, POOL_COL + b)), vec, pl.BlockSpec((1, LANE, LANE), lambda b: (b, 0, 0)), vec],
                  out_specs=_col(t, lambda b: (0, b)), semantics=("parallel",), vmem_limit=VMEM_LIMIT)(proj, win, wbd, scale)


def _pool_bwd(proj, win, wbd, scale, dmixed, *, name):
    t = proj.shape[0]

    def body(x_ref, win_ref, w_ref, s_ref, d_ref, dx_ref, dw_ref, ds_ref):
        xv = x_ref[...]
        winv = win_ref[...]
        cnt = _pool_counts(xv.shape, winv)
        pooled = _pick_window(_window_sums(xv, _shift_down), winv) / cnt - xv
        dv = d_ref[...]
        ds_ref[...] = _colsum(dv * _dot(pooled, w_ref[0], NN))
        dy0 = dv * s_ref[...]
        dw_ref[0] = _dot(pooled, dy0, TN)
        dpooled = _dot(dy0, w_ref[0], NT)
        dmean = dpooled / cnt
        dx_ref[...] = (_pick_window(_window_sums(dmean, _shift_up), winv) - dpooled).astype(BF16)

    nb = POOL_DIM // LANE
    vec = pl.BlockSpec((1, LANE), lambda b: (0, b))
    mat = pl.BlockSpec((1, LANE, LANE), lambda b: (b, 0, 0))
    first = A_DIM // LANE
    return _pcall(body, name=name,
                  out_shape=(jax.ShapeDtypeStruct((t, POOL_DIM), BF16), jax.ShapeDtypeStruct((nb, LANE, LANE), F32),
                             jax.ShapeDtypeStruct((1, POOL_DIM), F32)),
                  grid=(nb,),
                  in_specs=[_col(t, lambda b: (0, POOL_COL + b)), vec, mat, vec, _col(t, lambda b: (0, first + b))],
                  out_specs=(_col(t, lambda b: (0, b)), mat, vec), semantics=("parallel",),
                  vmem_limit=VMEM_LIMIT)(proj, win, wbd, scale, dmixed)


def _sconv_fwd(proj, w, *, name):
    t = proj.shape[0]

    def body(cb_ref, cc_ref, ch_ref, w_ref, y_ref):
        y_ref[...] = (cb_ref[...] * _conv_fwd(cc_ref[...] * ch_ref[...], w_ref, CONV_TAPS)).astype(BF16)

    nb = CONV_DIM // LANE
    return _pcall(body, name=name, out_shape=jax.ShapeDtypeStruct((t, CONV_DIM), BF16), grid=(nb,),
                  in_specs=[_col(t, lambda b: (0, CB_COL + b)), _col(t, lambda b: (0, CC_COL + b)),
                            _col(t, lambda b: (0, CH_COL + b)), pl.BlockSpec((CONV_TAPS, LANE), lambda b: (0, b))],
                  out_specs=_col(t, lambda b: (0, b)), semantics=("parallel",), vmem_limit=VMEM_LIMIT)(proj, proj, proj, w)


def _sconv_bwd(proj, w, dmixed, *, name):
    t = proj.shape[0]

    def body(cb_ref, cc_ref, ch_ref, w_ref, d_ref, dcb_ref, dcc_ref, dch_ref, dw_ref):
        cc = cc_ref[...]
        ch = ch_ref[...]
        u = cc * ch
        dv = d_ref[...]
        dcb_ref[...] = (dv * _conv_fwd(u, w_ref, CONV_TAPS)).astype(BF16)
        du = _conv_bwd(u, dv * cb_ref[...], w_ref, dw_ref, CONV_TAPS)
        dcc_ref[...] = (du * ch).astype(BF16)
        dch_ref[...] = (du * cc).astype(BF16)

    nb = CONV_DIM // LANE
    first = (A_DIM + POOL_DIM) // LANE
    act = jax.ShapeDtypeStruct((t, CONV_DIM), BF16)
    wspec = pl.BlockSpec((CONV_TAPS, LANE), lambda b: (0, b))
    ospec = _col(t, lambda b: (0, b))
    return _pcall(body, name=name, out_shape=(act, act, act, jax.ShapeDtypeStruct((CONV_TAPS, CONV_DIM), F32)), grid=(nb,),
                  in_specs=[_col(t, lambda b: (0, CB_COL + b)), _col(t, lambda b: (0, CC_COL + b)),
                            _col(t, lambda b: (0, CH_COL + b)), wspec, _col(t, lambda b: (0, first + b))],
                  out_specs=(ospec, ospec, ospec, wspec), semantics=("parallel",),
                  vmem_limit=VMEM_LIMIT)(proj, proj, proj, w, dmixed)


def _chunk_masks():
    r = lax.broadcasted_iota(jnp.int32, (CHUNK, CHUNK), 0)
    c = lax.broadcasted_iota(jnp.int32, (CHUNK, CHUNK), 1)
    return r >= c, r > c, jnp.where(r == c, 1.0, 0.0).astype(F32)


def _split(a):
    hi = a.astype(BF16)
    return hi, (a - hi.astype(F32)).astype(BF16)


def _dot_split(a, b, dims):
    (ah, al), (bh, bl) = a, b
    return _dot(ah, bh, dims) + _dot(ah, bl, dims) + _dot(al, bh, dims)


def _tri_inv(lows, eye):
    xs = [eye - low for low in lows]
    ps = [_split(low) for low in lows]
    ps = [_split(_dot_split(p, p, NN)) for p in ps]
    for i in range(5):
        xs = [x + _dot_split(_split(x), p, NN) for x, p in zip(xs, ps)]
        if i < 4:
            ps = [_split(_dot_split(p, p, NN)) for p in ps]
    return xs


def _prefix_sum_rows(x):
    for k in range(6):
        x = x + _shift_down(x, 1 << k)
    return x


def _suffix_sum_rows(x):
    for k in range(6):
        x = x + _shift_up(x, 1 << k)
    return x


def _chunk_decay(g, incl):
    gcb = _prefix_sum_rows(g)
    gtot = _colsum(g)
    col = gcb[:, :CHUNK]
    row = gcb.T[:CHUNK, :]
    decay = jnp.exp(jnp.where(incl, col - row, -1e30))
    return gcb, gtot, decay


CHUNKS_PER_STEP = 4


def _heads_of(ref, base, rows):
    return [ref[base + h, rows, :] for h in range(HEADS)]


def _chunk_rows(j):
    return pl.ds(j * CHUNK, CHUNK)


def _deltanet_prep(qkv, g, beta, *, name):
    t = qkv.shape[1]
    n_chunks = t // CHUNK
    per = CHUNKS_PER_STEP
    probs = [(j, h) for j in range(per) for h in range(HEADS)]

    def body(qkv_ref, g_ref, b_ref, u_ref, w_ref, qg_ref, kg_ref, attn_ref, tm_ref):
        incl, strict, eye = _chunk_masks()
        q = [qkv_ref[h, _chunk_rows(j), :] for j, h in probs]
        k = [qkv_ref[HEADS + h, _chunk_rows(j), :] for j, h in probs]
        v = [qkv_ref[2 * HEADS + h, _chunk_rows(j), :] for j, h in probs]
        bv = [b_ref[h, _chunk_rows(j), :] for j, h in probs]
        dec = [_chunk_decay(g_ref[h, _chunk_rows(j), :], incl) for j, h in probs]
        kb = [a * b for a, b in zip(k, bv)]
        low = [jnp.where(strict, _dot(a, b, NT) * d[2], 0.0) for a, b, d in zip(kb, k, dec)]
        tm = _tri_inv(low, eye)
        egc = [jnp.exp(d[0]) for d in dec]
        u = [_dot(m, a * b, NN) for m, a, b in zip(tm, v, bv)]
        w = [_dot(m, a * e, NN) for m, a, e in zip(tm, kb, egc)]
        attn = [_dot(a, b, NT) * d[2] for a, b, d in zip(q, k, dec)]
        for i, (j, h) in enumerate(probs):
            rows = _chunk_rows(j)
            u_ref[h, rows, :] = u[i]
            w_ref[h, rows, :] = w[i].astype(BF16)
            qg_ref[h, rows, :] = (q[i] * egc[i]).astype(BF16)
            kg_ref[h, rows, :] = (k[i] * jnp.exp(dec[i][1] - dec[i][0])).astype(BF16)
            attn_ref[j, h] = attn[i].astype(BF16)
            tm_ref[j, h] = tm[i]

    act = lambda heads: pl.BlockSpec((heads, per * CHUNK, LANE), lambda n: (0, n, 0))
    mat = pl.BlockSpec((per, HEADS, CHUNK, CHUNK), lambda n: (n, 0, 0, 0))
    return _pcall(
        body, name=name,
        out_shape=(jax.ShapeDtypeStruct((HEADS, t, LANE), F32),) + (jax.ShapeDtypeStruct((HEADS, t, LANE), BF16),) * 3
        + (jax.ShapeDtypeStruct((n_chunks, HEADS, CHUNK, CHUNK), BF16), jax.ShapeDtypeStruct((n_chunks, HEADS, CHUNK, CHUNK), F32)),
        grid=(n_chunks // per,), in_specs=[act(3 * HEADS), act(HEADS), act(HEADS)],
        out_specs=(act(HEADS),) * 4 + (mat, mat), semantics=("parallel",), vmem_limit=VMEM_LIMIT)(qkv, g, beta)


SCAN_CHUNKS_PER_STEP = 8


def _deltanet_scan(u, w, qg, kg, attn, g, *, name, after=None):
    t = u.shape[1]
    n_chunks = t // CHUNK
    per = SCAN_CHUNKS_PER_STEP

    def body(u_ref, w_ref, qg_ref, kg_ref, attn_ref, g_ref, o_ref, vn_ref, st_ref, s_ref):
        @pl.when(pl.program_id(0) == 0)
        def _():
            s_ref[...] = jnp.zeros_like(s_ref)

        for j in range(per):
            rows = _chunk_rows(j)
            s = [s_ref[h] for h in range(HEADS)]
            vn = [u_ref[h, rows, :] - _dot(w_ref[h, rows, :], s[h], NN) for h in range(HEADS)]
            o = [_dot(qg_ref[h, rows, :], s[h], NN) + _dot(attn_ref[j, h], vn[h], NN) for h in range(HEADS)]
            eg = [jnp.exp(_colsum(g_ref[h, rows, :])) for h in range(HEADS)]
            for h in range(HEADS):
                st_ref[j, h] = s[h]
                s_ref[h] = s[h] * eg[h] + _dot(kg_ref[h, rows, :], vn[h], TN)
                o_ref[h, rows, :] = o[h]
                vn_ref[h, rows, :] = vn[h]

    act = pl.BlockSpec((HEADS, per * CHUNK, LANE), lambda n: (0, n, 0))
    out = jax.ShapeDtypeStruct((HEADS, t, LANE), F32)
    return _pcall(
        body, name=name, out_shape=(out, out, jax.ShapeDtypeStruct((n_chunks, HEADS, LANE, LANE), F32)), grid=(n_chunks // per,),
        in_specs=[act] * 4 + [pl.BlockSpec((per, HEADS, CHUNK, CHUNK), lambda n: (n, 0, 0, 0)), act],
        out_specs=(act, act, pl.BlockSpec((per, HEADS, LANE, LANE), lambda n: (n, 0, 0, 0))),
        scratch_shapes=[pltpu.VMEM((HEADS, LANE, LANE), F32)], semantics=("arbitrary",), after=after)(u, w, qg, kg, attn, g)


def _deltanet_bscan(w, qg, kg, attn, g, do, *, name):
    t = w.shape[1]
    n_chunks = t // CHUNK
    per = SCAN_CHUNKS_PER_STEP
    steps = n_chunks // per

    def body(w_ref, qg_ref, kg_ref, attn_ref, g_ref, do_ref, dvn_ref, dsn_ref, ds_ref):
        @pl.when(pl.program_id(0) == 0)
        def _():
            ds_ref[...] = jnp.zeros_like(ds_ref)

        for j in reversed(range(per)):
            rows = _chunk_rows(j)
            dsn = [ds_ref[h] for h in range(HEADS)]
            dov = [do_ref[h, rows, :] for h in range(HEADS)]
            dvn = [_dot(attn_ref[j, h], dov[h], TN) + _dot(kg_ref[h, rows, :], dsn[h], NN) for h in range(HEADS)]
            eg = [jnp.exp(_colsum(g_ref[h, rows, :])) for h in range(HEADS)]
            for h in range(HEADS):
                dsn_ref[j, h] = dsn[h]
                ds_ref[h] = _dot(qg_ref[h, rows, :], dov[h], TN) + eg[h] * dsn[h] - _dot(w_ref[h, rows, :], dvn[h], TN)
                dvn_ref[h, rows, :] = dvn[h]

    act = pl.BlockSpec((HEADS, per * CHUNK, LANE), lambda n: (0, steps - 1 - n, 0))
    return _pcall(
        body, name=name,
        out_shape=(jax.ShapeDtypeStruct((HEADS, t, LANE), F32), jax.ShapeDtypeStruct((n_chunks, HEADS, LANE, LANE), F32)),
        grid=(steps,),
        in_specs=[act] * 3 + [pl.BlockSpec((per, HEADS, CHUNK, CHUNK), lambda n: (steps - 1 - n, 0, 0, 0)), act, act],
        out_specs=(act, pl.BlockSpec((per, HEADS, LANE, LANE), lambda n: (steps - 1 - n, 0, 0, 0))),
        scratch_shapes=[pltpu.VMEM((HEADS, LANE, LANE), F32)], semantics=("arbitrary",))(w, qg, kg, attn, g, do)


def _sum_all(x):
    return jnp.sum(jnp.sum(x, axis=1, keepdims=True), axis=0, keepdims=True)


def _rowsum(x):
    return jnp.sum(x, axis=1, keepdims=True)


def _deltanet_post(qkv, g, beta, tmats, states, dstates, do, dvn, vn, *, name):
    t = qkv.shape[1]
    n_chunks = t // CHUNK
    per = CHUNKS_PER_STEP
    probs = [(j, h) for j in range(per) for h in range(HEADS)]

    def body(qkv_ref, g_ref, b_ref, tm_ref, st_ref, dsn_ref, do_ref, dvn_ref, vn_ref, dqkv_ref, dg_ref, db_ref):
        incl, strict, _ = _chunk_masks()
        ones = jnp.ones((CHUNK, LANE), BF16)
        last_row = lax.broadcasted_iota(jnp.int32, (CHUNK, LANE), 0) == CHUNK - 1
        z = lambda f, *cols: [f(*a) for a in zip(*cols)]
        q = [qkv_ref[h, _chunk_rows(j), :] for j, h in probs]
        k = [qkv_ref[HEADS + h, _chunk_rows(j), :] for j, h in probs]
        v = [qkv_ref[2 * HEADS + h, _chunk_rows(j), :] for j, h in probs]
        bv = [b_ref[h, _chunk_rows(j), :] for j, h in probs]
        dov = [do_ref[h, _chunk_rows(j), :] for j, h in probs]
        dvn_ = [dvn_ref[h, _chunk_rows(j), :] for j, h in probs]
        vn_ = [vn_ref[h, _chunk_rows(j), :] for j, h in probs]
        tm = [tm_ref[j, h] for j, h in probs]
        s = [st_ref[j, h] for j, h in probs]
        dsn = [dsn_ref[j, h] for j, h in probs]
        dec = [_chunk_decay(g_ref[h, _chunk_rows(j), :], incl) for j, h in probs]
        decay = [d[2] for d in dec]
        egc = [jnp.exp(d[0]) for d in dec]
        ekg = [jnp.exp(d[1] - d[0]) for d in dec]
        kb = z(lambda a, b: a * b, k, bv)
        vb = z(lambda a, b: a * b, v, bv)
        kbg = z(lambda a, b: a * b, kb, egc)
        qg = z(lambda a, b: a * b, q, egc)
        kg = z(lambda a, b: a * b, k, ekg)
        kk = z(lambda a, b: _dot(a, b, NT), kb, k)
        qk = z(lambda a, b: _dot(a, b, NT), q, k)
        dattn = z(lambda a, b: jnp.where(incl, _dot(a, b, NT), 0.0), dov, vn_)
        dqg = z(lambda a, b: _dot(a, b, NT), dov, s)
        dkg = z(lambda a, b: _dot(a, b, NT), vn_, dsn)
        dglast = z(lambda a, b, c, d, e: _sum_all(a * b) * jnp.exp(e[1]) + _sum_all(c * d), s, dsn, dkg, kg, dec)
        dw = z(lambda a, b: -_dot(a, b, NT), dvn_, s)
        dtm = z(lambda a, b, c, d: _dot(a, b, NT) + _dot(c, d, NT), dvn_, vb, dw, kbg)
        dvb = z(lambda a, b: _dot(a, b, TN), tm, dvn_)
        dkbg = z(lambda a, b: _dot(a, b, TN), tm, dw)
        dlow = z(lambda a, b: jnp.where(strict, -_dot(_dot(a, b, TN), a, NT), 0.0), tm, dtm)
        dkk = z(lambda a, b: a * b, dlow, decay)
        dqk = z(lambda a, b: a * b, dattn, decay)
        dkb = z(lambda a, b, c, d: _dot(a, b, NN) + c * d, dkk, k, dkbg, egc)
        dk = z(lambda a, b, c, d, e, f, g_, h_: _dot(a, b, TN) + _dot(c, d, TN) + e * f + g_ * h_, dkk, kb, dqk, q, dkg, ekg, dkb, bv)
        dq = z(lambda a, b, c, d: _dot(a, b, NN) + c * d, dqk, k, dqg, egc)
        m = z(lambda a, b, c, d, e: (a * b + c * d) * e, dlow, kk, dattn, qk, decay)
        mcol = [_dot(mh, ones, TN) + _dot(ml, ones, TN) for mh, ml in (_split(a) for a in m)]
        for i, (j, h) in enumerate(probs):
            rows = _chunk_rows(j)
            dqkv_ref[h, rows, :] = dq[i]
            dqkv_ref[HEADS + h, rows, :] = dk[i]
            dqkv_ref[2 * HEADS + h, rows, :] = dvb[i] * bv[i]
            db_ref[h, rows, :] = jnp.broadcast_to(_rowsum(dkb[i] * k[i] + dvb[i] * v[i]), (CHUNK, LANE))
            dgc = (_rowsum(dqg[i] * qg[i] + dkbg[i] * kbg[i] - dkg[i] * kg[i]) + _rowsum(m[i]) - mcol[i]
                   + jnp.where(last_row, dglast[i], 0.0))
            dg_ref[h, rows, :] = _suffix_sum_rows(dgc)

    act = lambda heads: pl.BlockSpec((heads, per * CHUNK, LANE), lambda n: (0, n, 0))
    mat = lambda d: pl.BlockSpec((per, HEADS, d, d), lambda n: (n, 0, 0, 0))
    out = jax.ShapeDtypeStruct((HEADS, t, LANE), F32)
    return _pcall(
        body, name=name, out_shape=(jax.ShapeDtypeStruct((3 * HEADS, t, LANE), F32), out, out), grid=(n_chunks // per,),
        in_specs=[act(3 * HEADS), act(HEADS), act(HEADS), mat(CHUNK), mat(LANE), mat(LANE), act(HEADS), act(HEADS), act(HEADS)],
        out_specs=(act(3 * HEADS), act(HEADS), act(HEADS)), semantics=("parallel",),
        vmem_limit=VMEM_LIMIT)(qkv, g, beta, tmats, states, dstates, do, dvn, vn)


ANY = pl.BlockSpec(memory_space=pl.ANY)
PEERS = N_DEV - 1


def _all_gather(arrays, *, name):
    n = len(arrays)

    def body(*refs):
        ins, outs = refs[:n], refs[n:2 * n]
        send_sems, recv_sems, local_sems = refs[2 * n:]
        x, y, c = lax.axis_index("x"), lax.axis_index("y"), lax.axis_index("c")
        me, sibling = (x, y, c), (x, y, 1 - c)
        chips = [(1 - x, y), (x, 1 - y), (1 - x, 1 - y)]

        def copy(a, k, block, to, src=None):
            dst = outs[a].at[4 * block[0] + 2 * block[1] + block[2]]
            return pltpu.make_async_remote_copy(src_ref=dst if src is None else src, dst_ref=dst, send_sem=send_sems.at[a * PEERS + k],
                                                recv_sem=recv_sems.at[a * PEERS + k], device_id=to, device_id_type=MESH)

        local = [pltpu.make_async_copy(ins[a], outs[a].at[4 * x + 2 * y + c], local_sems.at[a]) for a in range(n)]
        for cp in local:
            cp.start()
        first = []
        for a in range(n):
            first += [copy(a, 1 + j, me, (*chip, c), src=ins[a]) for j, chip in enumerate(chips)]
            first.append(copy(a, 0, me, sibling, src=ins[a]))
        for cp in first:
            cp.start()
        passed = []
        for a in range(n):
            for j, chip in enumerate(chips):
                copy(a, 1 + j, (*chip, c), me).wait_recv()
                fwd = copy(a, 4 + j, (*chip, c), sibling)
                fwd.start()
                passed.append(fwd)
        for a in range(n):
            copy(a, 0, sibling, me).wait_recv()
            for j, chip in enumerate(chips):
                copy(a, 4 + j, (*chip, 1 - c), me).wait_recv()
        for cp in first + passed:
            cp.wait_send()
        for cp in local:
            cp.wait()

    return _pcall(body, name=name, out_shape=tuple(jax.ShapeDtypeStruct((N_DEV,) + a.shape, a.dtype) for a in arrays),
                  in_specs=[ANY] * n, out_specs=(ANY,) * n,
                  scratch_shapes=[pltpu.SemaphoreType.DMA((n * PEERS,)), pltpu.SemaphoreType.DMA((n * PEERS,)),
                                  pltpu.SemaphoreType.DMA((n,))])(*arrays)


CHIPS = 4


def _pair_exchange(arrays, *, name):
    n = len(arrays)

    def body(*refs):
        ins, outs = refs[:n], refs[n:2 * n]
        send_sems, recv_sems = refs[2 * n:]
        x, y, c = lax.axis_index("x"), lax.axis_index("y"), lax.axis_index("c")
        copies = []
        for a in range(n):
            for q in range(CHIPS):
                cp = pltpu.make_async_remote_copy(src_ref=ins[a].at[2 * q + 1 - c], dst_ref=outs[a].at[q],
                                                  send_sem=send_sems.at[a * CHIPS + q], recv_sem=recv_sems.at[a * CHIPS + q],
                                                  device_id=(x, y, 1 - c), device_id_type=MESH)
                cp.start()
                copies.append(cp)
        for cp in copies:
            cp.wait()

    return _pcall(body, name=name, out_shape=tuple(jax.ShapeDtypeStruct((CHIPS,) + a.shape[1:], a.dtype) for a in arrays),
                  in_specs=[ANY] * n, out_specs=(ANY,) * n,
                  scratch_shapes=[pltpu.SemaphoreType.DMA((n * CHIPS,)), pltpu.SemaphoreType.DMA((n * CHIPS,))])(*arrays)


def _pair_add(blocks, theirs, *, name):
    _, r, c_ = blocks.shape
    tr = _tile(r, 512, 16)

    def body(mine_ref, theirs_ref, o_ref):
        core = lax.axis_index("c")
        own = jnp.where(core == 0, mine_ref[0, 0].astype(F32), mine_ref[0, 1].astype(F32))
        o_ref[0] = (own + theirs_ref[0].astype(F32)).astype(o_ref.dtype)

    spec = pl.BlockSpec((1, tr, c_), lambda q, i: (q, i, 0))
    return _pcall(body, name=name, out_shape=jax.ShapeDtypeStruct(theirs.shape, theirs.dtype), grid=(CHIPS, r // tr),
                  in_specs=[pl.BlockSpec((1, 2, tr, c_), lambda q, i: (q, 0, i, 0)), spec], out_specs=spec,
                  semantics=("parallel", "parallel"), vmem_limit=VMEM_LIMIT)(blocks.reshape(CHIPS, 2, r, c_), theirs)


HBM = pl.BlockSpec(memory_space=pltpu.HBM)
SEM = pl.BlockSpec(memory_space=pltpu.SEMAPHORE)
EFFECT = pltpu.SideEffectType.DATAFLOW_SIDE_EFFECTING


GATHER, CHIP_GATHER, CHIP_SCATTER = "gather", "chip_gather", "chip_scatter"
PEERS_OF = {GATHER: N_DEV - 1, CHIP_GATHER: CHIPS - 1, CHIP_SCATTER: CHIPS - 1}


def _direct_copies(srcs, lands, send_sems, recv_sems, local_sems, kind):
    x, y, c = lax.axis_index("x"), lax.axis_index("y"), lax.axis_index("c")
    peers = PEERS_OF[kind]
    mine = 2 * x + y if kind == CHIP_SCATTER else 4 * x + 2 * y + c
    copies = []
    for a, (src, land) in enumerate(zip(srcs, lands)):
        copies.append(pltpu.make_async_copy(src.at[mine] if kind == CHIP_SCATTER else src, land.at[mine], local_sems.at[a]))
        for k in range(1, peers + 1):
            bits = k if kind == GATHER else 2 * k
            px = 1 - x if bits & 4 else x
            py = 1 - y if bits & 2 else y
            pc = 1 - c if bits & 1 else c
            copies.append(pltpu.make_async_remote_copy(
                src_ref=src.at[2 * px + py] if kind == CHIP_SCATTER else src, dst_ref=land.at[mine],
                send_sem=send_sems.at[a * peers + k - 1], recv_sem=recv_sems.at[a * peers + k - 1],
                device_id=(px, py, pc), device_id_type=MESH))
    return copies


def _pair_swap(arrays, *, name):
    n = len(arrays)

    def body(*refs):
        mine, zones = refs[:n], refs[n:2 * n]
        send_sems, recv_sems = refs[2 * n:]
        x, y, c = lax.axis_index("x"), lax.axis_index("y"), lax.axis_index("c")
        copies = []
        for a in range(n):
            for q in range(CHIPS):
                copies.append(pltpu.make_async_remote_copy(
                    src_ref=mine[a].at[2 * q + c], dst_ref=zones[a].at[2 * q + c], send_sem=send_sems.at[a * CHIPS + q],
                    recv_sem=recv_sems.at[a * CHIPS + q], device_id=(x, y, 1 - c), device_id_type=MESH))
        for cp in copies:
            cp.start()
        for cp in copies:
            cp.wait()

    return _pcall(body, name=name, out_shape=tuple(jax.ShapeDtypeStruct(a.shape, a.dtype) for a in arrays),
                  in_specs=[ANY] * n, out_specs=(ANY,) * n, input_output_aliases={i: i for i in range(n)},
                  scratch_shapes=[pltpu.SemaphoreType.DMA((n * CHIPS,)), pltpu.SemaphoreType.DMA((n * CHIPS,))])(*arrays)


def _exchange_start(groups, kind, *, name, after=None):
    srcs = [s for group in groups for s in group]
    n = len(srcs)
    sizes = [len(group) for group in groups]
    starts = [sum(sizes[:g]) for g in range(len(groups))]
    land_shapes = [s.shape if kind == CHIP_SCATTER else (N_DEV,) + s.shape for s in srcs]
    peers = PEERS_OF[kind]
    extra = [] if after is None else [after]

    def body(*refs):
        srcs_, lands = refs[:n], refs[n:2 * n]
        token = refs[-1]
        sem_refs = refs[2 * n + len(extra):]
        for g, (at, size) in enumerate(zip(starts, sizes)):
            send_sems, recv_sems, local_sems = sem_refs[3 * g:3 * g + 3]
            for cp in _direct_copies(srcs_[at:at + size], lands[at:at + size], send_sems, recv_sems, local_sems, kind):
                cp.start()
        token[...] = jnp.zeros_like(token)

    sems = tuple(t for size in sizes for t in (pltpu.SemaphoreType.DMA((size * peers,)), pltpu.SemaphoreType.DMA((size * peers,)),
                                               pltpu.SemaphoreType.DMA((size,))))
    thru = tuple(pltpu.HBM(s.shape, s.dtype) for s in srcs) + tuple(pltpu.HBM(shp, s.dtype) for shp, s in zip(land_shapes, srcs))
    ins = [pltpu.with_memory_space_constraint(s, pltpu.HBM) for s in srcs]
    ins += [pltpu.with_memory_space_constraint(lax.empty(shp, s.dtype), pltpu.HBM) for shp, s in zip(land_shapes, srcs)]
    out = pl.pallas_call(
        body, name=name, out_shape=sems + thru + (jax.ShapeDtypeStruct((SUBLANE, LANE), F32),),
        in_specs=[HBM] * (2 * n) + [ANY] * len(extra),
        out_specs=(SEM,) * len(sems) + (HBM,) * (2 * n) + (pl.BlockSpec(memory_space=pltpu.VMEM),),
        input_output_aliases={i: len(sems) + i for i in range(2 * n)},
        compiler_params=pltpu.CompilerParams(has_side_effects=EFFECT))(*ins, *extra)
    arrays = out[len(sems):-1]
    started = [tuple(out[3 * g:3 * g + 3]) + tuple(arrays[at:at + size]) + tuple(arrays[n + at:n + at + size])
               for g, (at, size) in enumerate(zip(starts, sizes))]
    return started, out[-1]


def _exchange_wait(started, after, kind, *, name):
    n = (len(started) - 3) // 2
    sems, arrays = started[:3], started[3:]

    def body(*refs):
        srcs_, lands = refs[:n], refs[n:2 * n]
        send_sems, recv_sems, local_sems = refs[2 * n:2 * n + 3]
        for cp in _direct_copies(srcs_, lands, send_sems, recv_sems, local_sems, kind):
            cp.wait()

    out = pl.pallas_call(
        body, name=name, out_shape=tuple(pltpu.HBM(a.shape, a.dtype) for a in arrays),
        in_specs=[HBM] * (2 * n) + [SEM] * 3 + [ANY], out_specs=(HBM,) * (2 * n),
        input_output_aliases={i: i for i in range(2 * n)},
        compiler_params=pltpu.CompilerParams(has_side_effects=EFFECT))(*arrays, *sems, after)
    return out[n:]


def _adamw_reduce(w, parts, m, v, *, name, after=None):
    layers, r, c = w.shape
    assert len(parts) == layers
    senders = parts[0].shape[0]
    tr = _tile(r, 512, 16)
    tiles = r // tr
    bc1 = 1.0 - ADAM_B1 ** ADAM_STEP
    bc2 = 1.0 - ADAM_B2 ** ADAM_STEP

    def body(w_ref, *rest):
        p_refs = rest[:layers]
        m_ref, v_ref, g_ref, d_ref, nm_ref, nv_ref = rest[layers:]

        def update(p_ref):
            g = p_ref[0, :, pl.ds(0, c)].astype(F32)
            for s in range(1, senders):
                g = g + p_ref[s, :, pl.ds(0, c)].astype(F32)
            nm = ADAM_B1 * m_ref[0] + (1.0 - ADAM_B1) * g
            nv = ADAM_B2 * v_ref[0] + (1.0 - ADAM_B2) * (g * g)
            g_ref[0] = g
            nm_ref[0] = nm
            nv_ref[0] = nv
            d_ref[0] = -ADAM_LR * ((nm / bc1) / (jnp.sqrt(nv / bc2) + ADAM_EPS) + ADAM_WD * w_ref[0])

        for layer in range(layers):
            pl.when(pl.program_id(0) == layer)(functools.partial(update, p_refs[layer]))

    def part_spec(layer, shape):
        rest = 0 if layer > 0 else tiles - 1
        return pl.BlockSpec((senders, tr, shape[2]), lambda l, i: (0, jnp.where(l == layer, i, rest), 0))

    spec = pl.BlockSpec((1, tr, c), lambda l, i: (l, i, 0))
    out = jax.ShapeDtypeStruct((layers, r, c), F32)
    return _pcall(body, name=name, out_shape=(out,) * 4, grid=(layers, tiles),
                  in_specs=[spec] + [part_spec(layer, p.shape) for layer, p in enumerate(parts)] + [spec, spec],
                  out_specs=(spec,) * 4, semantics=("arbitrary", "arbitrary"), vmem_limit=VMEM_LIMIT, after=after)(w, *parts, m, v)


def _pool_windows():
    return jnp.repeat(jnp.asarray(POOL_WINDOWS, F32), POOL_DIM // len(POOL_WINDOWS))[None, :]


def _block_diag_pairs(pool_w):
    z = jnp.zeros_like(pool_w[0])
    return jnp.stack([jnp.block([[pool_w[2 * b], z], [z, pool_w[2 * b + 1]]]) for b in range(2)])


def _pad_lanes(vec):
    return jnp.zeros((1, LANE), F32).at[0, :vec.shape[0]].set(vec)


FF_SHARD = D_FF // N_DEV
FF_BLOCK = 384
D_FF_PAD = N_DEV * FF_BLOCK


def _layer_fwd(x, p_i, wt, fetch):
    wt = {**wt, **fetch(0, x)}
    proj, h1 = _matmul(x, wt["w_in"], "nt", norm_g=wt["norm1_g"], name="mm_in")
    qkv = _qkv_prep_fwd(proj, wt["conv_qkv"], name="qkv_prep_fwd")
    g, beta = _gates_fwd(proj, wt["a_log"], wt["dt_bias"], name="gates_fwd")
    u, w, qg, kg, attn, tmats = _deltanet_prep(qkv, g, beta, name="deltanet_prep")
    wt.update(fetch(1, u))
    o, vn, states = _deltanet_scan(u, w, qg, kg, attn, g, name="deltanet_scan", after=wt.get("behind"))
    o_a = _apost_fwd(o, proj, wt["onorm_g"], name="apost_fwd")
    o_b = _pool_fwd(proj, wt["pool_win"], wt["pool_wbd"], wt["pool_scale"], name="pool_fwd")
    o_c = _sconv_fwd(proj, wt["sconv_w"], name="sconv_fwd")
    mixed = jnp.concatenate([o_a, o_b, o_c], axis=1)
    x1 = _matmul(mixed, wt["w_out"], "nn", res=x, name="mm_out")
    wt.update(fetch(2, x1))
    ff, gate, up, h2 = _swiglu_fwd(x1, wt["norm2_g"], wt["w_gate"], wt["w_up"], name="swiglu_fwd")
    wt.update(fetch(3, ff))
    x2 = _matmul(ff, wt["w_down"], "nn", res=x1, name="mm_down")
    wt.update(fetch(4, x2))
    x3, pgl, pp = _ple_fwd(x2, p_i, wt["ple_gate"], wt["ple_proj"], name="ple_fwd")
    saved = dict(x=x, h1=h1, proj=proj, qkv=qkv, g=g, beta=beta, o=o, states=states, tmats=tmats, mixed=mixed, x1=x1, h2=h2,
                 gate=gate, up=up, ff=ff, x2=x2, pgl=pgl, pp=pp, p=p_i, w=w, qg=qg, kg=kg, attn=attn, vn=vn, wt=wt)
    return x3, saved


def _col_blocks(g):
    a = g.shape[0]
    return jnp.transpose(g.reshape(a, N_DEV, -1), (1, 0, 2))


def _cols_joined(blocks):
    return jnp.transpose(blocks, (1, 0, 2)).reshape(blocks.shape[1], -1)


def _layer_bwd(dx3, sv, emit, after=None):
    gr, big = {}, {}
    wt = sv["wt"]
    rows = D_MODEL // N_DEV
    dpgl, dpp = _ple_bwd(dx3, sv["pgl"], sv["pp"], name="ple_bwd", after=after)
    big["ple_proj"] = _matmul(sv["p"], dpp, "tn", out_blocked=(N_DEV, rows), out_dtype=BF16, name="mm_dplep")
    big["ple_gate"] = _matmul(sv["x2"], dpgl, "tn", out_dtype=BF16, name="mm_dpleg").reshape(N_DEV, rows, D_MODEL)
    dx2 = _matmul(dpgl, wt["ple_gate"], "nt", res=dx3, name="mm_dx2")
    big["w_down"] = _matmul(sv["ff"], dx2, "tn", out_dtype=BF16, name="mm_ddown").reshape(N_DEV, FF_BLOCK, D_MODEL)
    dgate, dup = _swiglu_bwd(dx2, wt["w_down"], sv["gate"], sv["up"], name="swiglu_bwd", after=emit(0, big))
    big["w_gate"] = _matmul(dgate, sv["h2"], "tn", out_dtype=BF16, name="mm_dgate").reshape(N_DEV, FF_BLOCK, D_MODEL)
    big["w_up"] = _matmul(dup, sv["h2"], "tn", out_dtype=BF16, name="mm_dup").reshape(N_DEV, FF_BLOCK, D_MODEL)
    dh2 = _matmul(dgate, wt["w_gate"], "nn", name="mm_dh2_gate")
    dx1, gr["norm2_g"] = _matmul_norm_bwd(dup, wt["w_up"], sv["x1"], wt["norm2_g"], dx2, res=dh2, name="mm_dh2_up")
    big["w_out"] = _matmul(sv["mixed"], dx1, "tn", out_dtype=BF16, name="mm_dout").reshape(N_DEV, rows, D_MODEL)
    dmixed = _matmul(dx1, wt["w_out"], "nt", name="mm_dmixed", after=emit(1, big))
    proj = sv["proj"]
    dcb, dcc, dch, dsconv = _sconv_bwd(proj, wt["sconv_w"], dmixed, name="sconv_bwd")
    big["sconv_w"] = _col_blocks(dsconv)
    dhp, dwbd, gr["pool_scale"] = _pool_bwd(proj, wt["pool_win"], wt["pool_wbd"], wt["pool_scale"], dmixed, name="pool_bwd")
    half = LANE // 2
    gr["pool_w"] = jnp.stack([dwbd[0, :half, :half], dwbd[0, half:, half:], dwbd[1, :half, :half], dwbd[1, half:, half:]])
    do, dz, gr["onorm_g"] = _apost_bwd(sv["o"], proj, wt["onorm_g"], dmixed, name="apost_bwd")
    dvn, dstates = _deltanet_bscan(sv["w"], sv["qg"], sv["kg"], sv["attn"], sv["g"], do, name="deltanet_bscan")
    dqkv_h, dg, dbeta = _deltanet_post(sv["qkv"], sv["g"], sv["beta"], sv["tmats"], sv["states"], dstates, do, dvn, sv["vn"],
                                       name="deltanet_post")
    dab, dalog, ddtb = _gates_bwd(proj, wt["a_log"], wt["dt_bias"], dg, dbeta, name="gates_bwd")
    gr["a_log"], gr["dt_bias"] = dalog[0, :HEADS], ddtb[0, :HEADS]
    dqkv, dconv = _qkv_prep_bwd(proj, wt["conv_qkv"], dqkv_h, name="qkv_prep_bwd")
    big["conv_qkv"] = _col_blocks(dconv)
    dproj = jnp.concatenate([dqkv, dz, dab, dhp, dcb, dcc, dch], axis=1)
    dwin = _matmul(dproj, sv["h1"], "tn", out_dtype=BF16, name="mm_din")
    big["w_in"] = jnp.concatenate([dwin[:AB_COL + 2 * HEADS], dwin[AB_COL + LANE:]], axis=0).reshape(N_DEV, -1, D_MODEL)
    dx, gr["norm1_g"] = _matmul_norm_bwd(dproj, wt["w_in"], sv["x"], wt["norm1_g"], dx1, name="mm_dh1", after=emit(2, big))
    return dx, gr


FETCH_GROUPS = (("w_in", "conv_qkv", "sconv_w"), ("w_out",), ("w_gate", "w_up"), ("w_down",), ("ple_gate", "ple_proj"))
EMIT_GROUPS = (("ple_proj", "ple_gate", "w_down"), ("w_gate", "w_up", "w_out"), ("w_in", "conv_qkv", "sconv_w"))


def _small_weights(w, i):
    return dict(
        norm1_g=w["norm1_g"][i][None], norm2_g=w["norm2_g"][i][None], onorm_g=w["onorm_g"][i][None],
        a_log=_pad_lanes(w["a_log"][i]), dt_bias=_pad_lanes(w["dt_bias"][i]),
        pool_scale=w["pool_scale"][i][None], pool_win=_pool_windows(), pool_wbd=_block_diag_pairs(w["pool_w"][i]))


def _as_read(name, gathered):
    if name == "w_in":
        rows = gathered[:, :D_IN // N_DEV].reshape(-1, D_MODEL)
        return jnp.concatenate([rows[:AB_COL + 2 * HEADS], jnp.zeros((LANE - 2 * HEADS, D_MODEL), BF16),
                                rows[AB_COL + 2 * HEADS:]], axis=0)
    if name in ("conv_qkv", "sconv_w"):
        return _cols_joined(gathered)
    if name == "ple_proj":
        return gathered
    return gathered.reshape(-1, D_MODEL)


def _layer_weights(gathered, w, i):
    return {**_small_weights(w, i), **{k: _as_read(k, g) for k, g in gathered.items()}}


def _local_step(x, p, target, layers, final_g):
    saved = []
    h = x
    for i in range(DEPTH):
        replicated = {k: v for k, v in layers[i].items() if k not in SHARDED}
        h, sv = _layer_fwd(h, p[i], replicated, lambda group, after, i=i: {k: layers[i][k] for k in FETCH_GROUPS[group]})
        saved.append(sv)
    dx, dgf, loss = _loss_head(h, final_g, target, name="loss_head")
    big, small = [{} for _ in range(DEPTH)], [None] * DEPTH
    for i in reversed(range(DEPTH)):
        dx, small[i] = _layer_bwd(dx, saved[i], lambda group, blocks, i=i: big[i].update({k: blocks[k] for k in EMIT_GROUPS[group]}))
    return loss, dx, big, small, dgf


SHARDED = ("w_in", "w_gate", "w_up", "w_down", "w_out", "ple_gate", "ple_proj", "conv_qkv", "sconv_w")
SMALL = ("norm1_g", "a_log", "dt_bias", "onorm_g", "pool_w", "pool_scale", "norm2_g", "final_g")
SLAB_COLS = 1024


def _payload(name, shard):
    if name in ("conv_qkv", "sconv_w"):
        return shard
    out = shard.astype(BF16)
    if name in ("w_gate", "w_up", "w_down"):
        out = jnp.pad(out, ((0, FF_BLOCK - FF_SHARD), (0, 0)))
    if name == "w_in":
        out = jnp.pad(out, ((0, -out.shape[0] % (2 * SUBLANE)), (0, 0)))
    return out


TRANSPOSED = ("w_in", "w_gate", "w_up")


def _ff_rows(t):
    return jnp.transpose(t, (0, 2, 1))


def _slab_rows(shape):
    size = 1
    for s in shape:
        size *= s
    return SUBLANE * -(-size // (SUBLANE * SLAB_COLS))


def _pack_slab(parts, extra_row):
    rows = []
    for name in SMALL:
        flat = parts[name].reshape(-1)
        nrow = _slab_rows(parts[name].shape)
        rows.append(jnp.pad(flat, (0, nrow * SLAB_COLS - flat.shape[0])).reshape(nrow, SLAB_COLS))
    rows.append(jnp.pad(extra_row, ((0, SUBLANE - 1), (0, 0))))
    return jnp.concatenate(rows, axis=0)


def _unpack_slab(slab, shapes):
    out, row = {}, 0
    for name in SMALL:
        size = 1
        for s in shapes[name]:
            size *= s
        out[name] = slab[row:row + _slab_rows(shapes[name])].reshape(-1)[:size].reshape(shapes[name])
        row += _slab_rows(shapes[name])
    return out, row


def kernel(x, p, norm1_g, w_in, conv_qkv, a_log, dt_bias, onorm_g, pool_w, pool_scale, sconv_w, w_out, norm2_g, w_gate, w_up, w_down, ple_proj, ple_gate, final_g, loss_target, m_norm1_g, m_w_in, m_conv_qkv, m_a_log, m_dt_bias, m_onorm_g, m_pool_w, m_pool_scale, m_sconv_w, m_w_out, m_norm2_g, m_w_gate, m_w_up, m_w_down, m_ple_proj, m_ple_gate, m_final_g, v_norm1_g, v_w_in, v_conv_qkv, v_a_log, v_dt_bias, v_onorm_g, v_pool_w, v_pool_scale, v_sconv_w, v_w_out, v_norm2_g, v_w_gate, v_w_up, v_w_down, v_ple_proj, v_ple_gate, v_final_g):
    names = ["norm1_g", "w_in", "conv_qkv", "a_log", "dt_bias", "onorm_g", "pool_w", "pool_scale", "sconv_w", "w_out", "norm2_g",
             "w_gate", "w_up", "w_down", "ple_proj", "ple_gate", "final_g"]
    w = dict(zip(names, [norm1_g, w_in, conv_qkv, a_log, dt_bias, onorm_g, pool_w, pool_scale, sconv_w, w_out, norm2_g, w_gate, w_up,
                         w_down, ple_proj, ple_gate, final_g]))
    m = dict(zip(names, [m_norm1_g, m_w_in, m_conv_qkv, m_a_log, m_dt_bias, m_onorm_g, m_pool_w, m_pool_scale, m_sconv_w, m_w_out,
                         m_norm2_g, m_w_gate, m_w_up, m_w_down, m_ple_proj, m_ple_gate, m_final_g]))
    v = dict(zip(names, [v_norm1_g, v_w_in, v_conv_qkv, v_a_log, v_dt_bias, v_onorm_g, v_pool_w, v_pool_scale, v_sconv_w, v_w_out,
                         v_norm2_g, v_w_gate, v_w_up, v_w_down, v_ple_proj, v_ple_gate, v_final_g]))
    w.update({k: _ff_rows(w[k]) for k in TRANSPOSED})

    first, rest = FETCH_GROUPS[0], tuple(k for members in FETCH_GROUPS[1:] for k in members)
    gathered = dict(zip(first, _all_gather([_payload(k, w[k][0]) for k in first], name="all_gather_weights")))
    (flying0,), token = _exchange_start([[_payload(k, w[k][0]) for k in rest]], CHIP_GATHER, name="gather_start_0",
                                        after=gathered[first[0]])
    replicated = [_small_weights(w, i) for i in range(DEPTH)]
    replicated[0]["norm1_g"] = replicated[0]["norm1_g"] + token[0, 0]
    for group in (m, v):
        group.update({k: _ff_rows(group[k] + token[0, 0]) for k in TRANSPOSED})
    flying1 = []

    def fetch(i, group, after):
        if i == 0 and group == 1:
            landed = _exchange_wait(flying0, after, CHIP_GATHER, name="gather_wait_0")
            gathered.update(zip(rest, _pair_swap(landed, name="pair_swap")))
            started, token = _exchange_start([[_payload(k, w[k][1]) for k in SHARDED]], CHIP_GATHER, name="gather_start_1",
                                             after=gathered[rest[0]])
            flying1.extend(started)
            return {**{k: _as_read(k, gathered[k]) for k in FETCH_GROUPS[group]}, "behind": token}
        if i == 1 and group == 0:
            landed = _exchange_wait(flying1[0], after, CHIP_GATHER, name="gather_wait_1")
            gathered.update(zip(SHARDED, _pair_swap(landed, name="pair_swap")))
        return {k: _as_read(k, gathered[k]) for k in FETCH_GROUPS[group]}

    def reduce_scatter_start(members, blocks, tag):
        mine = [blocks[k] for k in members]
        theirs = _pair_exchange(mine, name="pair_exchange")
        sums = [_pair_add(a, b, name="pair_add") for a, b in zip(mine, theirs)]
        (started,), token = _exchange_start([sums], CHIP_SCATTER, name="exchange_start_" + tag)
        return started, token

    h, saved0 = _layer_fwd(x[0], p[0, 0], replicated[0], functools.partial(fetch, 0))
    h, saved1 = _layer_fwd(h, p[1, 0], replicated[1], functools.partial(fetch, 1))
    dx, dgf, loss_part = _loss_head(h, final_g[None], loss_target[0], name="loss_head")
    small, big1, flying0 = [None] * DEPTH, {}, []
    dx, small[1] = _layer_bwd(dx, saved1, lambda group, blocks: big1.update({k: blocks[k] for k in EMIT_GROUPS[group]}))
    flying1, token = reduce_scatter_start(SHARDED, big1, "1")

    def emit(group, blocks):
        started, token = reduce_scatter_start(EMIT_GROUPS[group], blocks, f"0_{group}")
        flying0.append(started)
        return token

    dx, small[0] = _layer_bwd(dx, saved0, emit, after=token)
    received = [{}, dict(zip(SHARDED, _exchange_wait(flying1, dx, CHIP_SCATTER, name="exchange_wait_1")))]
    for group, members in enumerate(EMIT_GROUPS):
        received[0].update(zip(members, _exchange_wait(flying0[group], dx, CHIP_SCATTER, name=f"exchange_wait_0_{group}")))

    grads = {k: jnp.stack([small[i][k] for i in range(DEPTH)]) for k in small[0]}
    grads = {k: g[:, 0] if k in ("norm1_g", "norm2_g", "onorm_g", "pool_scale") else g for k, g in grads.items()}
    grads["final_g"] = dgf[0]
    loss_row = jnp.pad(loss_part, ((0, 0), (0, SLAB_COLS - LANE)))
    (small_flying,), token = _exchange_start([[_pack_slab(grads, loss_row)]], GATHER, name="small_gather_start")

    out_g, out_d, out_m, out_v = {}, {}, {}, {}
    for k in SHARDED:
        out_g[k], out_d[k], out_m[k], out_v[k] = _adamw_reduce(w[k], [received[i][k] for i in range(DEPTH)], m[k], v[k],
                                                                name="adamw_" + k, after=token)
    behind_all = jnp.stack([out_v[k][0, 0, 0] for k in SHARDED])
    (small_parts,) = _exchange_wait(small_flying, behind_all, GATHER, name="small_gather_wait")
    zero_row = jnp.zeros((1, SLAB_COLS), F32)
    slabs = _adamw_reduce(_pack_slab(w, zero_row)[None], [small_parts], _pack_slab(m, zero_row)[None],
                          _pack_slab(v, zero_row)[None], name="adamw_small")
    slabs = [s[0] for s in slabs]
    shapes = {k: w[k].shape for k in SMALL}
    for dst, slab in zip((out_g, out_d, out_m, out_v), slabs):
        vals, _ = _unpack_slab(slab, shapes)
        dst.update(vals)
    _, loss_at = _unpack_slab(slabs[0], shapes)
    loss = slabs[0][loss_at, 0]
    for group in (out_g, out_d, out_m, out_v):
        group.update({k: _ff_rows(group[k]) for k in TRANSPOSED})

    return (loss, dx[None], *[out_g[k] for k in names], *[out_d[k] for k in names], *[out_m[k] for k in names],
            *[out_v[k] for k in names])
```

```python
import functools

import jax
import jax.numpy as jnp
from jax import lax
from jax.experimental import pallas as pl
from jax.experimental.pallas import tpu as pltpu

F32 = jnp.float32
BF16 = jnp.bfloat16

D_MODEL = 1024
DEPTH = 2
PLE_DIM = 256
EPS = 1e-6
HEAD_DIM = 128
HEADS = 4
A_DIM = HEADS * HEAD_DIM
QKV_TAPS = 4
CHUNK = 64
POOL_WINDOWS = (2, 4, 8, 16)
POOL_DIM = 256
CONV_DIM = 256
CONV_TAPS = 3
D_FF = 2816
D_IN = 3080
D_IN_PAD = 3200
AB_COL = 2048
N_DEV = 8

ADAM_LR = 0.001
ADAM_B1 = 0.9
ADAM_B2 = 0.999
ADAM_EPS = 1e-08
ADAM_WD = 0.01
ADAM_STEP = 10

LANE = 128
SUBLANE = 8
VMEM_BYTES_V7X = 64 * 1024 * 1024
VMEM_LIMIT = 48 * 1024 * 1024

_HI = lax.Precision.HIGHEST
NN = ((1,), (0,))
NT = ((1,), (1,))
TN = ((0,), (0,))
MESH = pl.DeviceIdType.MESH


def _dot(a, b, dims, hi=False):
    if hi:
        return lax.dot_general(a, b, (dims, ((), ())), precision=_HI, preferred_element_type=F32)
    return lax.dot_general(a.astype(BF16), b.astype(BF16), (dims, ((), ())), preferred_element_type=F32)


def _pcall(body, *, name, out_shape, grid=(), in_specs=None, out_specs=None, scratch_shapes=(), semantics=None,
           vmem_limit=None, after=None, **kw):
    params = {}
    if semantics is not None:
        params["dimension_semantics"] = semantics
    if vmem_limit is not None:
        params["vmem_limit_bytes"] = vmem_limit
    if after is not None:
        n_in, inner = len(in_specs), body
        body = lambda *refs: inner(*refs[:n_in], *refs[n_in + 1:])
        in_specs = list(in_specs) + [pl.BlockSpec(after.shape, lambda *_: (0,) * after.ndim)]
    call = pl.pallas_call(
        body, name=name, out_shape=out_shape, grid=grid, in_specs=in_specs, out_specs=out_specs,
        scratch_shapes=list(scratch_shapes), compiler_params=pltpu.CompilerParams(**params), **kw)
    return call if after is None else (lambda *args: call(*args, after))


def _sigmoid(x):
    return 1.0 / (1.0 + jnp.exp(-x))


def _softplus(x):
    return jnp.maximum(x, 0.0) + jnp.log(1.0 + jnp.exp(-jnp.abs(x)))


def _tile(n, cap, mult):
    if n <= cap:
        return n
    best = None
    for t in range(mult, cap + 1, mult):
        if n % t == 0:
            best = t
    assert best is not None, (n, cap, mult)
    return best


ROWS_PER_STEP = 512
NARROW_RESULT = 1024
COLS_PER_DOT = 640


def _resident(weight):
    return pl.BlockSpec(weight.shape, lambda i: (0,) * weight.ndim, pipeline_mode=pl.Buffered(1))


def _matmul_rows(a, b, mode, *, name, res=None, out_dtype=F32, b_blocked=False, after=None, norm_g=None):
    m, k = a.shape
    if b_blocked:
        nb, _, bw = b.shape
        n = nb * bw if mode == "nn" else b.shape[1]
    else:
        n = b.shape[1] if mode == "nn" else b.shape[0]
    tm = _tile(m, ROWS_PER_STEP if n > NARROW_RESULT else 2 * ROWS_PER_STEP, 16)
    cn = bw if (b_blocked and mode == "nn") else _tile(n, COLS_PER_DOT, LANE)
    has_res = res is not None
    normed = norm_g is not None

    def body(*refs):
        a_ref, b_ref = refs[0], refs[1]
        g_ref = refs[2] if normed else None
        res_ref = refs[2 + normed] if has_res else None
        o_ref = refs[2 + normed + has_res]
        if normed:
            av = _rms_normed(a_ref[...], g_ref[...])
            refs[3 + normed + has_res][...] = av
        elif not (b_blocked and mode == "nt"):
            av = a_ref[...].astype(BF16)
        for j in range(n // cn):
            cols = pl.ds(j * cn, cn)
            if mode == "nn":
                part = _dot(av, b_ref[j] if b_blocked else b_ref[:, cols], NN)
            elif not b_blocked:
                part = _dot(av, b_ref[cols, :], NT)
            else:
                part = None
                for s in range(nb):
                    term = _dot(a_ref[:, pl.ds(s * bw, bw)], b_ref[s, cols, :], NT)
                    part = term if part is None else part + term
            if has_res:
                part = part + res_ref[:, cols]
            o_ref[:, cols] = part.astype(o_ref.dtype)

    row = lambda width: pl.BlockSpec((tm, width), lambda i: (i, 0))
    whole = _resident(b)
    ins = [a, b] + ([norm_g] if normed else []) + ([res] if has_res else [])
    specs = [row(k), whole] + ([pl.BlockSpec((1, k), lambda i: (0, 0))] if normed else []) + ([row(n)] if has_res else [])
    out = jax.ShapeDtypeStruct((m, n), out_dtype)
    return _pcall(body, name=name, out_shape=(out, jax.ShapeDtypeStruct((m, k), BF16)) if normed else out, grid=(m // tm,),
                  in_specs=specs, out_specs=(row(n), row(k)) if normed else row(n), semantics=("parallel",),
                  vmem_limit=VMEM_LIMIT, after=after)(*ins)


def _matmul_norm_bwd(a, b, x, g, dres, *, name, res=None, after=None):
    m, k = a.shape
    d = b.shape[1]
    tm = _tile(m, ROWS_PER_STEP, 16)
    cn = _tile(d, COLS_PER_DOT, LANE)
    has_res = res is not None

    def body(*refs):
        a_ref, b_ref, x_ref, g_ref, dres_ref = refs[:5]
        res_ref = refs[5] if has_res else None
        dx_ref, dg_ref = refs[5 + has_res], refs[6 + has_res]
        av = a_ref[...].astype(BF16)
        for j in range(d // cn):
            cols = pl.ds(j * cn, cn)
            part = _dot(av, b_ref[:, cols], NN)
            dx_ref[:, cols] = part + res_ref[:, cols] if has_res else part
        dhv = dx_ref[...]
        xv = x_ref[...]
        r = lax.rsqrt(jnp.mean(xv * xv, axis=-1, keepdims=True) + EPS)
        xhat = xv * r
        dhg = dhv * g_ref[...]
        dx_ref[...] = dres_ref[...] + r * (dhg - xhat * jnp.mean(dhg * xhat, axis=-1, keepdims=True))
        part_g = jnp.sum(dhv * xhat, axis=0, keepdims=True)

        @pl.when(pl.program_id(0) == 0)
        def _():
            dg_ref[...] = part_g

        @pl.when(pl.program_id(0) > 0)
        def _():
            dg_ref[...] += part_g

    row = lambda width: pl.BlockSpec((tm, width), lambda i: (i, 0))
    vec = pl.BlockSpec((1, d), lambda i: (0, 0))
    ins = [a, b, x, g, dres] + ([res] if has_res else [])
    specs = [row(k), _resident(b), row(d), vec, row(d)] + ([row(d)] if has_res else [])
    return _pcall(body, name=name, out_shape=(jax.ShapeDtypeStruct((m, d), F32), jax.ShapeDtypeStruct((1, d), F32)),
                  grid=(m // tm,), in_specs=specs, out_specs=(row(d), vec), semantics=("arbitrary",), vmem_limit=VMEM_LIMIT,
                  after=after)(*ins)


def _rms_normed(xv, gv):
    return (xv * lax.rsqrt(jnp.mean(xv * xv, axis=-1, keepdims=True) + EPS) * gv).astype(BF16)


def _matmul(a, b, mode, *, name, res=None, out_dtype=F32, b_blocked=False, out_blocked=None, after=None, norm_g=None):
    if mode != "tn":
        return _matmul_rows(a, b, mode, name=name, res=res, out_dtype=out_dtype, b_blocked=b_blocked, after=after, norm_g=norm_g)
    assert res is None and not b_blocked and after is None and norm_g is None
    (t, m), (t2, n) = a.shape, b.shape
    assert t == t2, (a.shape, b.shape)
    tm = _tile(m, 1024, LANE)
    tn = _tile(n, NARROW_RESULT if n <= NARROW_RESULT else COLS_PER_DOT, LANE)
    if out_blocked is not None:
        assert out_blocked[0] * out_blocked[1] == n
        tn = out_blocked[1]

    def body(a_ref, b_ref, o_ref):
        part = _dot(a_ref[...], b_ref[...], TN).astype(o_ref.dtype)
        if out_blocked is None:
            o_ref[...] = part
        else:
            o_ref[0] = part

    o_spec = (pl.BlockSpec((tm, tn), lambda i, j: (i, j)) if out_blocked is None
              else pl.BlockSpec((1, tm, tn), lambda i, j: (j, i, 0)))
    o_shape = (m, n) if out_blocked is None else (out_blocked[0], m, out_blocked[1])
    return _pcall(body, name=name, out_shape=jax.ShapeDtypeStruct(o_shape, out_dtype), grid=(m // tm, n // tn),
                  in_specs=[pl.BlockSpec((t, tm), lambda i, j: (0, i)), pl.BlockSpec((t, tn), lambda i, j: (0, j))],
                  out_specs=o_spec, semantics=("parallel", "parallel"), vmem_limit=VMEM_LIMIT)(a, b)


ROW_TILE = 512


def _rows(t, width, idx=0):
    return pl.BlockSpec((ROW_TILE, width), lambda i: (i, idx))


def _vec(width):
    return pl.BlockSpec((1, width), lambda i: (0, 0))


def _swiglu_fwd(x, norm_g, w_gate, w_up, *, name):
    t, k = x.shape
    f = w_gate.shape[0]
    tm = _tile(t, ROWS_PER_STEP, 16)
    cn = _tile(f, COLS_PER_DOT, LANE)

    def body(x_ref, g_ref, wg_ref, wu_ref, ff_ref, gate_ref, up_ref, h_ref):
        hv = _rms_normed(x_ref[...], g_ref[...])
        h_ref[...] = hv
        for j in range(f // cn):
            cols = pl.ds(j * cn, cn)
            gv = _dot(hv, wg_ref[cols, :], NT)
            uv = _dot(hv, wu_ref[cols, :], NT)
            gate_ref[:, cols] = gv.astype(BF16)
            up_ref[:, cols] = uv.astype(BF16)
            ff_ref[:, cols] = (gv * _sigmoid(gv) * uv).astype(BF16)

    row = lambda width: pl.BlockSpec((tm, width), lambda i: (i, 0))
    out = jax.ShapeDtypeStruct((t, f), BF16)
    return _pcall(body, name=name, out_shape=(out,) * 3 + (jax.ShapeDtypeStruct((t, k), BF16),), grid=(t // tm,),
                  in_specs=[row(k), pl.BlockSpec((1, k), lambda i: (0, 0)), _resident(w_gate), _resident(w_up)],
                  out_specs=(row(f),) * 3 + (row(k),), semantics=("parallel",), vmem_limit=VMEM_LIMIT)(x, norm_g, w_gate, w_up)


def _swiglu_bwd(dx2, w_down, gate, up, *, name, after=None):
    t, d = dx2.shape
    f = w_down.shape[0]
    tm = _tile(t, ROWS_PER_STEP, 16)
    cn = _tile(f, COLS_PER_DOT, LANE)

    def body(dx_ref, w_ref, gate_ref, up_ref, dgate_ref, dup_ref):
        dxv = dx_ref[...].astype(BF16)
        for j in range(f // cn):
            cols = pl.ds(j * cn, cn)
            dffv = _dot(dxv, w_ref[cols, :], NT)
            gv = gate_ref[:, cols].astype(F32)
            sig = _sigmoid(gv)
            dgate_ref[:, cols] = (dffv * up_ref[:, cols].astype(F32) * sig * (1.0 + gv * (1.0 - sig))).astype(BF16)
            dup_ref[:, cols] = (dffv * gv * sig).astype(BF16)

    row = lambda width: pl.BlockSpec((tm, width), lambda i: (i, 0))
    out = jax.ShapeDtypeStruct((t, f), BF16)
    return _pcall(body, name=name, out_shape=(out, out), grid=(t // tm,), in_specs=[row(d), _resident(w_down), row(f), row(f)],
                  out_specs=(row(f), row(f)), semantics=("parallel",), vmem_limit=VMEM_LIMIT, after=after)(dx2, w_down, gate, up)


def _ple_fwd(x2, p, w_gate, w_proj, *, name):
    t, d = x2.shape
    nb, pdim, bw = w_proj.shape
    tm = _tile(t, ROWS_PER_STEP, 16)
    cn = _tile(d, COLS_PER_DOT, LANE)

    def body(x_ref, p_ref, wg_ref, wp_ref, x3_ref, pgl_ref, pp_ref):
        xb = x_ref[...].astype(BF16)
        pb = p_ref[...].astype(BF16)
        per = cn // bw
        for c in range(d // cn):
            cols = pl.ds(c * cn, cn)
            pgl = _dot(xb, wg_ref[:, cols], NN)
            pp = jnp.concatenate([_dot(pb, wp_ref[c * per + j], NN) for j in range(per)], axis=1)
            pgl_ref[:, cols] = pgl
            pp_ref[:, cols] = pp
            x3_ref[:, cols] = x_ref[:, cols] + _sigmoid(pgl) * pp

    row = lambda width: pl.BlockSpec((tm, width), lambda i: (i, 0))
    out = jax.ShapeDtypeStruct((t, d), F32)
    return _pcall(body, name=name, out_shape=(out,) * 3, grid=(t // tm,),
                  in_specs=[row(d), row(pdim), _resident(w_gate), _resident(w_proj)], out_specs=(row(d),) * 3,
                  semantics=("parallel",), vmem_limit=VMEM_LIMIT)(x2, p, w_gate, w_proj)


def _ple_bwd(dx3, pgl, pp, *, name, after=None):
    t, d = dx3.shape

    def body(dx_ref, pgl_ref, pp_ref, dpgl_ref, dpp_ref):
        dxv = dx_ref[...]
        sig = _sigmoid(pgl_ref[...])
        dpp_ref[...] = (dxv * sig).astype(BF16)
        dpgl_ref[...] = (dxv * pp_ref[...] * sig * (1.0 - sig)).astype(BF16)

    return _pcall(body, name=name, out_shape=(jax.ShapeDtypeStruct((t, d), BF16),) * 2, grid=(t // ROW_TILE,),
                  in_specs=[_rows(t, d)] * 3, out_specs=(_rows(t, d),) * 2, semantics=("parallel",), after=after)(dx3, pgl, pp)


def _loss_head(x3, g, target, *, name):
    t, d = x3.shape

    def body(x_ref, g_ref, t_ref, dx_ref, dg_ref, loss_ref):
        xv = x_ref[...]
        r = lax.rsqrt(jnp.mean(xv * xv, axis=-1, keepdims=True) + EPS)
        xhat = xv * r
        gv = g_ref[...]
        err = xhat * gv - t_ref[...]
        row_loss = jnp.sum(err * err, axis=-1, keepdims=True) * (0.5 / d)
        lpart = jnp.broadcast_to(jnp.sum(row_loss, axis=0, keepdims=True), (1, LANE))
        dy = err * (1.0 / d)
        dyg = dy * gv
        dx_ref[...] = r * (dyg - xhat * jnp.mean(dyg * xhat, axis=-1, keepdims=True))
        gpart = jnp.sum(dy * xhat, axis=0, keepdims=True)

        @pl.when(pl.program_id(0) == 0)
        def _():
            dg_ref[...] = gpart
            loss_ref[...] = lpart

        @pl.when(pl.program_id(0) > 0)
        def _():
            dg_ref[...] += gpart
            loss_ref[...] += lpart

    return _pcall(body, name=name,
                  out_shape=(jax.ShapeDtypeStruct((t, d), F32), jax.ShapeDtypeStruct((1, d), F32), jax.ShapeDtypeStruct((1, LANE), F32)),
                  grid=(t // ROW_TILE,), in_specs=[_rows(t, d), _vec(d), _rows(t, d)],
                  out_specs=(_rows(t, d), _vec(d), _vec(LANE)), semantics=("arbitrary",))(x3, g, target)


def _shift_down(x, d):
    if d == 0:
        return x
    row = lax.broadcasted_iota(jnp.int32, x.shape, 0)
    return jnp.where(row >= d, pltpu.roll(x, d, 0), 0.0)


def _shift_up(x, d):
    if d == 0:
        return x
    t = x.shape[0]
    row = lax.broadcasted_iota(jnp.int32, x.shape, 0)
    return jnp.where(row < t - d, pltpu.roll(x, t - d, 0), 0.0)


def _colsum(x):
    return jnp.sum(x, axis=0, keepdims=True)


def _col(t, idx_fn):
    return pl.BlockSpec((t, LANE), idx_fn)


def _conv_fwd(x, w_ref, taps):
    acc = None
    for j in range(taps):
        term = w_ref[pl.ds(j, 1), :] * _shift_down(x, taps - 1 - j)
        acc = term if acc is None else acc + term
    return acc


def _conv_bwd(x, dy, w_ref, dw_ref, taps):
    dx = None
    for j in range(taps):
        term = w_ref[pl.ds(j, 1), :] * _shift_up(dy, taps - 1 - j)
        dx = term if dx is None else dx + term
        dw_ref[pl.ds(j, 1), :] = _colsum(dy * _shift_down(x, taps - 1 - j))
    return dx


def _qkv_prep_fwd(proj, conv_w, *, name):
    t = proj.shape[0]
    scale = HEAD_DIM ** -0.5

    def body(x_ref, w_ref, o_ref):
        j = pl.program_id(0)
        c = _conv_fwd(x_ref[...], w_ref, QKV_TAPS)
        s = c * _sigmoid(c)
        r = lax.rsqrt(jnp.sum(s * s, axis=-1, keepdims=True) + EPS)
        f = jnp.where(j < 2 * HEADS, r, 1.0) * jnp.where(j < HEADS, scale, 1.0)
        o_ref[0] = s * f

    return _pcall(body, name=name, out_shape=jax.ShapeDtypeStruct((3 * HEADS, t, LANE), F32), grid=(3 * HEADS,),
                  in_specs=[_col(t, lambda j: (0, j)), pl.BlockSpec((QKV_TAPS, LANE), lambda j: (0, j))],
                  out_specs=pl.BlockSpec((1, t, LANE), lambda j: (j, 0, 0)), semantics=("parallel",),
                  vmem_limit=VMEM_LIMIT)(proj, conv_w)


def _qkv_prep_bwd(proj, conv_w, dqkv, *, name):
    t = proj.shape[0]
    scale = HEAD_DIM ** -0.5

    def body(x_ref, w_ref, d_ref, dx_ref, dw_ref):
        j = pl.program_id(0)
        xv = x_ref[...]
        c = _conv_fwd(xv, w_ref, QKV_TAPS)
        sig = _sigmoid(c)
        s = c * sig
        r = lax.rsqrt(jnp.sum(s * s, axis=-1, keepdims=True) + EPS)
        n0 = s * r
        dv = d_ref[0]
        dn0 = dv * jnp.where(j < HEADS, scale, 1.0)
        ds_norm = r * (dn0 - n0 * jnp.sum(dn0 * n0, axis=-1, keepdims=True))
        ds = jnp.where(j < 2 * HEADS, ds_norm, dv)
        dc = ds * sig * (1.0 + c * (1.0 - sig))
        dx_ref[...] = _conv_bwd(xv, dc, w_ref, dw_ref, QKV_TAPS).astype(BF16)

    return _pcall(body, name=name,
                  out_shape=(jax.ShapeDtypeStruct((t, 3 * A_DIM), BF16), jax.ShapeDtypeStruct((QKV_TAPS, 3 * A_DIM), F32)),
                  grid=(3 * HEADS,),
                  in_specs=[_col(t, lambda j: (0, j)), pl.BlockSpec((QKV_TAPS, LANE), lambda j: (0, j)),
                            pl.BlockSpec((1, t, LANE), lambda j: (j, 0, 0))],
                  out_specs=(_col(t, lambda j: (0, j)), pl.BlockSpec((QKV_TAPS, LANE), lambda j: (0, j))),
                  semantics=("parallel",), vmem_limit=VMEM_LIMIT)(proj, conv_w, dqkv)


def _lane_pick(x, lane_idx, lane):
    return jnp.broadcast_to(jnp.sum(jnp.where(lane == lane_idx, x, 0.0), axis=-1, keepdims=True), x.shape)


def _gates_fwd(proj, alog, dtb, *, name):
    t = proj.shape[0]

    def body(x_ref, alog_ref, dtb_ref, g_ref, b_ref):
        xv = x_ref[...]
        lane = lax.broadcasted_iota(jnp.int32, xv.shape, 1)
        gall = -jnp.exp(alog_ref[...]) * _softplus(xv + dtb_ref[...])
        ball = _sigmoid(xv)
        for h in range(HEADS):
            g_ref[h] = _lane_pick(gall, h, lane)
            b_ref[h] = _lane_pick(ball, HEADS + h, lane)

    out = jax.ShapeDtypeStruct((HEADS, t, LANE), F32)
    whole = pl.BlockSpec((HEADS, t, LANE), lambda i: (0, 0, 0))
    return _pcall(body, name=name, out_shape=(out, out), grid=(1,),
                  in_specs=[_col(t, lambda i: (0, AB_COL // LANE)), _vec(LANE), _vec(LANE)], out_specs=(whole, whole),
                  semantics=("arbitrary",), vmem_limit=VMEM_LIMIT)(proj, alog, dtb)


def _gates_bwd(proj, alog, dtb, dg, dbeta, *, name):
    t = proj.shape[0]

    def body(x_ref, alog_ref, dtb_ref, dg_ref, db_ref, dab_ref, dalog_ref, ddtb_ref):
        xv = x_ref[...]
        lane = lax.broadcasted_iota(jnp.int32, xv.shape, 1)
        lane1 = lax.broadcasted_iota(jnp.int32, (1, LANE), 1)
        z = xv + dtb_ref[...]
        nea = -jnp.exp(alog_ref[...])
        da_f = nea * _sigmoid(z)
        g_f = nea * _softplus(z)
        ball = _sigmoid(xv)
        db_f = ball * (1.0 - ball)
        dab = jnp.zeros_like(xv)
        dalog = jnp.zeros((1, LANE), F32)
        for h in range(HEADS):
            dgh = dg_ref[h]
            dab = dab + jnp.where(lane == h, dgh * da_f, 0.0) + jnp.where(lane == HEADS + h, db_ref[h] * db_f, 0.0)
            dalog = dalog + jnp.where(lane1 == h, _colsum(dgh * g_f), 0.0)
        dab_ref[...] = dab.astype(BF16)
        dalog_ref[...] = dalog
        ddtb_ref[...] = jnp.where(lane1 < HEADS, _colsum(dab), 0.0)

    whole = pl.BlockSpec((HEADS, t, LANE), lambda i: (0, 0, 0))
    vec = jax.ShapeDtypeStruct((1, LANE), F32)
    return _pcall(body, name=name, out_shape=(jax.ShapeDtypeStruct((t, LANE), BF16), vec, vec), grid=(1,),
                  in_specs=[_col(t, lambda i: (0, AB_COL // LANE)), _vec(LANE), _vec(LANE), whole, whole],
                  out_specs=(_col(t, lambda i: (0, 0)), _vec(LANE), _vec(LANE)), semantics=("arbitrary",),
                  vmem_limit=VMEM_LIMIT)(proj, alog, dtb, dg, dbeta)


Z_COL = 3 * A_DIM // LANE


def _apost_fwd(o, proj, gn, *, name):
    t = proj.shape[0]

    def body(o_ref, z_ref, gn_ref, y_ref):
        ov = o_ref[0]
        z = z_ref[...]
        r = lax.rsqrt(jnp.mean(ov * ov, axis=-1, keepdims=True) + EPS)
        y_ref[...] = (ov * r * gn_ref[...] * (z * _sigmoid(z))).astype(BF16)

    return _pcall(body, name=name, out_shape=jax.ShapeDtypeStruct((t, A_DIM), BF16), grid=(HEADS,),
                  in_specs=[pl.BlockSpec((1, t, LANE), lambda h: (h, 0, 0)), _col(t, lambda h: (0, Z_COL + h)),
                            pl.BlockSpec((1, LANE), lambda h: (0, 0))],
                  out_specs=_col(t, lambda h: (0, h)), semantics=("parallel",), vmem_limit=VMEM_LIMIT)(o, proj, gn)


def _apost_bwd(o, proj, gn, dmixed, *, name):
    t = proj.shape[0]

    def body(o_ref, z_ref, gn_ref, d_ref, do_ref, dz_ref, dgn_ref):
        ov = o_ref[0]
        z = z_ref[...]
        gnv = gn_ref[...]
        dv = d_ref[...]
        r = lax.rsqrt(jnp.mean(ov * ov, axis=-1, keepdims=True) + EPS)
        ohat = ov * r
        sig = _sigmoid(z)
        dy = dv * (z * sig)
        dz_ref[...] = (dv * ohat * gnv * sig * (1.0 + z * (1.0 - sig))).astype(BF16)
        dyo = dy * gnv
        do_ref[0] = r * (dyo - ohat * jnp.mean(dyo * ohat, axis=-1, keepdims=True))
        part = _colsum(dy * ohat)

        @pl.when(pl.program_id(0) == 0)
        def _():
            dgn_ref[...] = part

        @pl.when(pl.program_id(0) > 0)
        def _():
            dgn_ref[...] += part

    return _pcall(body, name=name,
                  out_shape=(jax.ShapeDtypeStruct((HEADS, t, LANE), F32), jax.ShapeDtypeStruct((t, A_DIM), BF16),
                             jax.ShapeDtypeStruct((1, LANE), F32)),
                  grid=(HEADS,),
                  in_specs=[pl.BlockSpec((1, t, LANE), lambda h: (h, 0, 0)), _col(t, lambda h: (0, Z_COL + h)),
                            pl.BlockSpec((1, LANE), lambda h: (0, 0)), _col(t, lambda h: (0, h))],
                  out_specs=(pl.BlockSpec((1, t, LANE), lambda h: (h, 0, 0)), _col(t, lambda h: (0, h)),
                             pl.BlockSpec((1, LANE), lambda h: (0, 0))),
                  semantics=("arbitrary",), vmem_limit=VMEM_LIMIT)(o, proj, gn, dmixed)


POOL_COL = (AB_COL + LANE) // LANE
CB_COL = POOL_COL + POOL_DIM // LANE
CC_COL = CB_COL + CONV_DIM // LANE
CH_COL = CC_COL + CONV_DIM // LANE
MAX_WIN_LOG2 = 4


def _window_sums(x, shift):
    sums = []
    cur = x
    for k in range(MAX_WIN_LOG2):
        cur = cur + shift(cur, 1 << k)
        sums.append(cur)
    return sums


def _pick_window(sums, win):
    out = sums[-1]
    for k in range(MAX_WIN_LOG2 - 2, -1, -1):
        out = jnp.where(win == float(2 << k), sums[k], out)
    return out


def _pool_counts(shape, win):
    row = lax.broadcasted_iota(jnp.int32, shape, 0).astype(F32)
    return jnp.minimum(row + 1.0, win)


def _pool_fwd(proj, win, wbd, scale, *, name):
    t = proj.shape[0]

    def body(x_ref, win_ref, w_ref, s_ref, y_ref):
        xv = x_ref[...]
        winv = win_ref[...]
        pooled = _pick_window(_window_sums(xv, _shift_down), winv) / _pool_counts(xv.shape, winv) - xv
        y_ref[...] = (_dot(pooled, w_ref[0], NN) * s_ref[...]).astype(BF16)

    nb = POOL_DIM // LANE
    vec = pl.BlockSpec((1, LANE), lambda b: (0, b))
    return _pcall(body, name=name, out_shape=jax.ShapeDtypeStruct((t, POOL_DIM), BF16), grid=(nb,),
                  in_specs=[_col(t, lambda b: (0, POOL_COL + b)), vec, pl.BlockSpec((1, LANE, LANE), lambda b: (b, 0, 0)), vec],
                  out_specs=_col(t, lambda b: (0, b)), semantics=("parallel",), vmem_limit=VMEM_LIMIT)(proj, win, wbd, scale)


def _pool_bwd(proj, win, wbd, scale, dmixed, *, name):
    t = proj.shape[0]

    def body(x_ref, win_ref, w_ref, s_ref, d_ref, dx_ref, dw_ref, ds_ref):
        xv = x_ref[...]
        winv = win_ref[...]
        cnt = _pool_counts(xv.shape, winv)
        pooled = _pick_window(_window_sums(xv, _shift_down), winv) / cnt - xv
        dv = d_ref[...]
        ds_ref[...] = _colsum(dv * _dot(pooled, w_ref[0], NN))
        dy0 = dv * s_ref[...]
        dw_ref[0] = _dot(pooled, dy0, TN)
        dpooled = _dot(dy0, w_ref[0], NT)
        dmean = dpooled / cnt
        dx_ref[...] = (_pick_window(_window_sums(dmean, _shift_up), winv) - dpooled).astype(BF16)

    nb = POOL_DIM // LANE
    vec = pl.BlockSpec((1, LANE), lambda b: (0, b))
    mat = pl.BlockSpec((1, LANE, LANE), lambda b: (b, 0, 0))
    first = A_DIM // LANE
    return _pcall(body, name=name,
                  out_shape=(jax.ShapeDtypeStruct((t, POOL_DIM), BF16), jax.ShapeDtypeStruct((nb, LANE, LANE), F32),
                             jax.ShapeDtypeStruct((1, POOL_DIM), F32)),
                  grid=(nb,),
                  in_specs=[_col(t, lambda b: (0, POOL_COL + b)), vec, mat, vec, _col(t, lambda b: (0, first + b))],
                  out_specs=(_col(t, lambda b: (0, b)), mat, vec), semantics=("parallel",),
                  vmem_limit=VMEM_LIMIT)(proj, win, wbd, scale, dmixed)


def _sconv_fwd(proj, w, *, name):
    t = proj.shape[0]

    def body(cb_ref, cc_ref, ch_ref, w_ref, y_ref):
        y_ref[...] = (cb_ref[...] * _conv_fwd(cc_ref[...] * ch_ref[...], w_ref, CONV_TAPS)).astype(BF16)

    nb = CONV_DIM // LANE
    return _pcall(body, name=name, out_shape=jax.ShapeDtypeStruct((t, CONV_DIM), BF16), grid=(nb,),
                  in_specs=[_col(t, lambda b: (0, CB_COL + b)), _col(t, lambda b: (0, CC_COL + b)),
                            _col(t, lambda b: (0, CH_COL + b)), pl.BlockSpec((CONV_TAPS, LANE), lambda b: (0, b))],
                  out_specs=_col(t, lambda b: (0, b)), semantics=("parallel",), vmem_limit=VMEM_LIMIT)(proj, proj, proj, w)


def _sconv_bwd(proj, w, dmixed, *, name):
    t = proj.shape[0]

    def body(cb_ref, cc_ref, ch_ref, w_ref, d_ref, dcb_ref, dcc_ref, dch_ref, dw_ref):
        cc = cc_ref[...]
        ch = ch_ref[...]
        u = cc * ch
        dv = d_ref[...]
        dcb_ref[...] = (dv * _conv_fwd(u, w_ref, CONV_TAPS)).astype(BF16)
        du = _conv_bwd(u, dv * cb_ref[...], w_ref, dw_ref, CONV_TAPS)
        dcc_ref[...] = (du * ch).astype(BF16)
        dch_ref[...] = (du * cc).astype(BF16)

    nb = CONV_DIM // LANE
    first = (A_DIM + POOL_DIM) // LANE
    act = jax.ShapeDtypeStruct((t, CONV_DIM), BF16)
    wspec = pl.BlockSpec((CONV_TAPS, LANE), lambda b: (0, b))
    ospec = _col(t, lambda b: (0, b))
    return _pcall(body, name=name, out_shape=(act, act, act, jax.ShapeDtypeStruct((CONV_TAPS, CONV_DIM), F32)), grid=(nb,),
                  in_specs=[_col(t, lambda b: (0, CB_COL + b)), _col(t, lambda b: (0, CC_COL + b)),
                            _col(t, lambda b: (0, CH_COL + b)), wspec, _col(t, lambda b: (0, first + b))],
                  out_specs=(ospec, ospec, ospec, wspec), semantics=("parallel",),
                  vmem_limit=VMEM_LIMIT)(proj, proj, proj, w, dmixed)


def _chunk_masks():
    r = lax.broadcasted_iota(jnp.int32, (CHUNK, CHUNK), 0)
    c = lax.broadcasted_iota(jnp.int32, (CHUNK, CHUNK), 1)
    return r >= c, r > c, jnp.where(r == c, 1.0, 0.0).astype(F32)


def _split(a):
    hi = a.astype(BF16)
    return hi, (a - hi.astype(F32)).astype(BF16)


def _dot_split(a, b, dims):
    (ah, al), (bh, bl) = a, b
    return _dot(ah, bh, dims) + _dot(ah, bl, dims) + _dot(al, bh, dims)


def _tri_inv(lows, eye):
    xs = [eye - low for low in lows]
    ps = [_split(low) for low in lows]
    ps = [_split(_dot_split(p, p, NN)) for p in ps]
    for i in range(5):
        xs = [x + _dot_split(_split(x), p, NN) for x, p in zip(xs, ps)]
        if i < 4:
            ps = [_split(_dot_split(p, p, NN)) for p in ps]
    return xs


def _prefix_sum_rows(x):
    for k in range(6):
        x = x + _shift_down(x, 1 << k)
    return x


def _suffix_sum_rows(x):
    for k in range(6):
        x = x + _shift_up(x, 1 << k)
    return x


def _chunk_decay(g, incl):
    gcb = _prefix_sum_rows(g)
    gtot = _colsum(g)
    col = gcb[:, :CHUNK]
    row = gcb.T[:CHUNK, :]
    decay = jnp.exp(jnp.where(incl, col - row, -1e30))
    return gcb, gtot, decay


CHUNKS_PER_STEP = 4


def _heads_of(ref, base, rows):
    return [ref[base + h, rows, :] for h in range(HEADS)]


def _chunk_rows(j):
    return pl.ds(j * CHUNK, CHUNK)


def _deltanet_prep(qkv, g, beta, *, name):
    t = qkv.shape[1]
    n_chunks = t // CHUNK
    per = CHUNKS_PER_STEP
    probs = [(j, h) for j in range(per) for h in range(HEADS)]

    def body(qkv_ref, g_ref, b_ref, u_ref, w_ref, qg_ref, kg_ref, attn_ref, tm_ref):
        incl, strict, eye = _chunk_masks()
        q = [qkv_ref[h, _chunk_rows(j), :] for j, h in probs]
        k = [qkv_ref[HEADS + h, _chunk_rows(j), :] for j, h in probs]
        v = [qkv_ref[2 * HEADS + h, _chunk_rows(j), :] for j, h in probs]
        bv = [b_ref[h, _chunk_rows(j), :] for j, h in probs]
        dec = [_chunk_decay(g_ref[h, _chunk_rows(j), :], incl) for j, h in probs]
        kb = [a * b for a, b in zip(k, bv)]
        low = [jnp.where(strict, _dot(a, b, NT) * d[2], 0.0) for a, b, d in zip(kb, k, dec)]
        tm = _tri_inv(low, eye)
        egc = [jnp.exp(d[0]) for d in dec]
        u = [_dot(m, a * b, NN) for m, a, b in zip(tm, v, bv)]
        w = [_dot(m, a * e, NN) for m, a, e in zip(tm, kb, egc)]
        attn = [_dot(a, b, NT) * d[2] for a, b, d in zip(q, k, dec)]
        for i, (j, h) in enumerate(probs):
            rows = _chunk_rows(j)
            u_ref[h, rows, :] = u[i]
            w_ref[h, rows, :] = w[i].astype(BF16)
            qg_ref[h, rows, :] = (q[i] * egc[i]).astype(BF16)
            kg_ref[h, rows, :] = (k[i] * jnp.exp(dec[i][1] - dec[i][0])).astype(BF16)
            attn_ref[j, h] = attn[i].astype(BF16)
            tm_ref[j, h] = tm[i]

    act = lambda heads: pl.BlockSpec((heads, per * CHUNK, LANE), lambda n: (0, n, 0))
    mat = pl.BlockSpec((per, HEADS, CHUNK, CHUNK), lambda n: (n, 0, 0, 0))
    return _pcall(
        body, name=name,
        out_shape=(jax.ShapeDtypeStruct((HEADS, t, LANE), F32),) + (jax.ShapeDtypeStruct((HEADS, t, LANE), BF16),) * 3
        + (jax.ShapeDtypeStruct((n_chunks, HEADS, CHUNK, CHUNK), BF16), jax.ShapeDtypeStruct((n_chunks, HEADS, CHUNK, CHUNK), F32)),
        grid=(n_chunks // per,), in_specs=[act(3 * HEADS), act(HEADS), act(HEADS)],
        out_specs=(act(HEADS),) * 4 + (mat, mat), semantics=("parallel",), vmem_limit=VMEM_LIMIT)(qkv, g, beta)


SCAN_CHUNKS_PER_STEP = 8


def _deltanet_scan(u, w, qg, kg, attn, g, *, name, after=None):
    t = u.shape[1]
    n_chunks = t // CHUNK
    per = SCAN_CHUNKS_PER_STEP

    def body(u_ref, w_ref, qg_ref, kg_ref, attn_ref, g_ref, o_ref, vn_ref, st_ref, s_ref):
        @pl.when(pl.program_id(0) == 0)
        def _():
            s_ref[...] = jnp.zeros_like(s_ref)

        for j in range(per):
            rows = _chunk_rows(j)
            s = [s_ref[h] for h in range(HEADS)]
            vn = [u_ref[h, rows, :] - _dot(w_ref[h, rows, :], s[h], NN) for h in range(HEADS)]
            o = [_dot(qg_ref[h, rows, :], s[h], NN) + _dot(attn_ref[j, h], vn[h], NN) for h in range(HEADS)]
            eg = [jnp.exp(_colsum(g_ref[h, rows, :])) for h in range(HEADS)]
            for h in range(HEADS):
                st_ref[j, h] = s[h]
                s_ref[h] = s[h] * eg[h] + _dot(kg_ref[h, rows, :], vn[h], TN)
                o_ref[h, rows, :] = o[h]
                vn_ref[h, rows, :] = vn[h]

    act = pl.BlockSpec((HEADS, per * CHUNK, LANE), lambda n: (0, n, 0))
    out = jax.ShapeDtypeStruct((HEADS, t, LANE), F32)
    return _pcall(
        body, name=name, out_shape=(out, out, jax.ShapeDtypeStruct((n_chunks, HEADS, LANE, LANE), F32)), grid=(n_chunks // per,),
        in_specs=[act] * 4 + [pl.BlockSpec((per, HEADS, CHUNK, CHUNK), lambda n: (n, 0, 0, 0)), act],
        out_specs=(act, act, pl.BlockSpec((per, HEADS, LANE, LANE), lambda n: (n, 0, 0, 0))),
        scratch_shapes=[pltpu.VMEM((HEADS, LANE, LANE), F32)], semantics=("arbitrary",), after=after)(u, w, qg, kg, attn, g)


def _deltanet_bscan(w, qg, kg, attn, g, do, *, name):
    t = w.shape[1]
    n_chunks = t // CHUNK
    per = SCAN_CHUNKS_PER_STEP
    steps = n_chunks // per

    def body(w_ref, qg_ref, kg_ref, attn_ref, g_ref, do_ref, dvn_ref, dsn_ref, ds_ref):
        @pl.when(pl.program_id(0) == 0)
        def _():
            ds_ref[...] = jnp.zeros_like(ds_ref)

        for j in reversed(range(per)):
            rows = _chunk_rows(j)
            dsn = [ds_ref[h] for h in range(HEADS)]
            dov = [do_ref[h, rows, :] for h in range(HEADS)]
            dvn = [_dot(attn_ref[j, h], dov[h], TN) + _dot(kg_ref[h, rows, :], dsn[h], NN) for h in range(HEADS)]
            eg = [jnp.exp(_colsum(g_ref[h, rows, :])) for h in range(HEADS)]
            for h in range(HEADS):
                dsn_ref[j, h] = dsn[h]
                ds_ref[h] = _dot(qg_ref[h, rows, :], dov[h], TN) + eg[h] * dsn[h] - _dot(w_ref[h, rows, :], dvn[h], TN)
                dvn_ref[h, rows, :] = dvn[h]

    act = pl.BlockSpec((HEADS, per * CHUNK, LANE), lambda n: (0, steps - 1 - n, 0))
    return _pcall(
        body, name=name,
        out_shape=(jax.ShapeDtypeStruct((HEADS, t, LANE), F32), jax.ShapeDtypeStruct((n_chunks, HEADS, LANE, LANE), F32)),
        grid=(steps,),
        in_specs=[act] * 3 + [pl.BlockSpec((per, HEADS, CHUNK, CHUNK), lambda n: (steps - 1 - n, 0, 0, 0)), act, act],
        out_specs=(act, pl.BlockSpec((per, HEADS, LANE, LANE), lambda n: (steps - 1 - n, 0, 0, 0))),
        scratch_shapes=[pltpu.VMEM((HEADS, LANE, LANE), F32)], semantics=("arbitrary",))(w, qg, kg, attn, g, do)


def _sum_all(x):
    return jnp.sum(jnp.sum(x, axis=1, keepdims=True), axis=0, keepdims=True)


def _rowsum(x):
    return jnp.sum(x, axis=1, keepdims=True)


def _deltanet_post(qkv, g, beta, tmats, states, dstates, do, dvn, vn, *, name):
    t = qkv.shape[1]
    n_chunks = t // CHUNK
    per = CHUNKS_PER_STEP
    probs = [(j, h) for j in range(per) for h in range(HEADS)]

    def body(qkv_ref, g_ref, b_ref, tm_ref, st_ref, dsn_ref, do_ref, dvn_ref, vn_ref, dqkv_ref, dg_ref, db_ref):
        incl, strict, _ = _chunk_masks()
        ones = jnp.ones((CHUNK, LANE), BF16)
        last_row = lax.broadcasted_iota(jnp.int32, (CHUNK, LANE), 0) == CHUNK - 1
        z = lambda f, *cols: [f(*a) for a in zip(*cols)]
        q = [qkv_ref[h, _chunk_rows(j), :] for j, h in probs]
        k = [qkv_ref[HEADS + h, _chunk_rows(j), :] for j, h in probs]
        v = [qkv_ref[2 * HEADS + h, _chunk_rows(j), :] for j, h in probs]
        bv = [b_ref[h, _chunk_rows(j), :] for j, h in probs]
        dov = [do_ref[h, _chunk_rows(j), :] for j, h in probs]
        dvn_ = [dvn_ref[h, _chunk_rows(j), :] for j, h in probs]
        vn_ = [vn_ref[h, _chunk_rows(j), :] for j, h in probs]
        tm = [tm_ref[j, h] for j, h in probs]
        s = [st_ref[j, h] for j, h in probs]
        dsn = [dsn_ref[j, h] for j, h in probs]
        dec = [_chunk_decay(g_ref[h, _chunk_rows(j), :], incl) for j, h in probs]
        decay = [d[2] for d in dec]
        egc = [jnp.exp(d[0]) for d in dec]
        ekg = [jnp.exp(d[1] - d[0]) for d in dec]
        kb = z(lambda a, b: a * b, k, bv)
        vb = z(lambda a, b: a * b, v, bv)
        kbg = z(lambda a, b: a * b, kb, egc)
        qg = z(lambda a, b: a * b, q, egc)
        kg = z(lambda a, b: a * b, k, ekg)
        kk = z(lambda a, b: _dot(a, b, NT), kb, k)
        qk = z(lambda a, b: _dot(a, b, NT), q, k)
        dattn = z(lambda a, b: jnp.where(incl, _dot(a, b, NT), 0.0), dov, vn_)
        dqg = z(lambda a, b: _dot(a, b, NT), dov, s)
        dkg = z(lambda a, b: _dot(a, b, NT), vn_, dsn)
        dglast = z(lambda a, b, c, d, e: _sum_all(a * b) * jnp.exp(e[1]) + _sum_all(c * d), s, dsn, dkg, kg, dec)
        dw = z(lambda a, b: -_dot(a, b, NT), dvn_, s)
        dtm = z(lambda a, b, c, d: _dot(a, b, NT) + _dot(c, d, NT), dvn_, vb, dw, kbg)
        dvb = z(lambda a, b: _dot(a, b, TN), tm, dvn_)
        dkbg = z(lambda a, b: _dot(a, b, TN), tm, dw)
        dlow = z(lambda a, b: jnp.where(strict, -_dot(_dot(a, b, TN), a, NT), 0.0), tm, dtm)
        dkk = z(lambda a, b: a * b, dlow, decay)
        dqk = z(lambda a, b: a * b, dattn, decay)
        dkb = z(lambda a, b, c, d: _dot(a, b, NN) + c * d, dkk, k, dkbg, egc)
        dk = z(lambda a, b, c, d, e, f, g_, h_: _dot(a, b, TN) + _dot(c, d, TN) + e * f + g_ * h_, dkk, kb, dqk, q, dkg, ekg, dkb, bv)
        dq = z(lambda a, b, c, d: _dot(a, b, NN) + c * d, dqk, k, dqg, egc)
        m = z(lambda a, b, c, d, e: (a * b + c * d) * e, dlow, kk, dattn, qk, decay)
        mcol = [_dot(mh, ones, TN) + _dot(ml, ones, TN) for mh, ml in (_split(a) for a in m)]
        for i, (j, h) in enumerate(probs):
            rows = _chunk_rows(j)
            dqkv_ref[h, rows, :] = dq[i]
            dqkv_ref[HEADS + h, rows, :] = dk[i]
            dqkv_ref[2 * HEADS + h, rows, :] = dvb[i] * bv[i]
            db_ref[h, rows, :] = jnp.broadcast_to(_rowsum(dkb[i] * k[i] + dvb[i] * v[i]), (CHUNK, LANE))
            dgc = (_rowsum(dqg[i] * qg[i] + dkbg[i] * kbg[i] - dkg[i] * kg[i]) + _rowsum(m[i]) - mcol[i]
                   + jnp.where(last_row, dglast[i], 0.0))
            dg_ref[h, rows, :] = _suffix_sum_rows(dgc)

    act = lambda heads: pl.BlockSpec((heads, per * CHUNK, LANE), lambda n: (0, n, 0))
    mat = lambda d: pl.BlockSpec((per, HEADS, d, d), lambda n: (n, 0, 0, 0))
    out = jax.ShapeDtypeStruct((HEADS, t, LANE), F32)
    return _pcall(
        body, name=name, out_shape=(jax.ShapeDtypeStruct((3 * HEADS, t, LANE), F32), out, out), grid=(n_chunks // per,),
        in_specs=[act(3 * HEADS), act(HEADS), act(HEADS), mat(CHUNK), mat(LANE), mat(LANE), act(HEADS), act(HEADS), act(HEADS)],
        out_specs=(act(3 * HEADS), act(HEADS), act(HEADS)), semantics=("parallel",),
        vmem_limit=VMEM_LIMIT)(qkv, g, beta, tmats, states, dstates, do, dvn, vn)


ANY = pl.BlockSpec(memory_space=pl.ANY)
PEERS = N_DEV - 1


def _all_gather(arrays, *, name):
    n = len(arrays)

    def body(*refs):
        ins, outs = refs[:n], refs[n:2 * n]
        send_sems, recv_sems, local_sems = refs[2 * n:]
        x, y, c = lax.axis_index("x"), lax.axis_index("y"), lax.axis_index("c")
        me, sibling = (x, y, c), (x, y, 1 - c)
        chips = [(1 - x, y), (x, 1 - y), (1 - x, 1 - y)]

        def copy(a, k, block, to, src=None):
            dst = outs[a].at[4 * block[0] + 2 * block[1] + block[2]]
            return pltpu.make_async_remote_copy(src_ref=dst if src is None else src, dst_ref=dst, send_sem=send_sems.at[a * PEERS + k],
                                                recv_sem=recv_sems.at[a * PEERS + k], device_id=to, device_id_type=MESH)

        local = [pltpu.make_async_copy(ins[a], outs[a].at[4 * x + 2 * y + c], local_sems.at[a]) for a in range(n)]
        for cp in local:
            cp.start()
        first = []
        for a in range(n):
            first += [copy(a, 1 + j, me, (*chip, c), src=ins[a]) for j, chip in enumerate(chips)]
            first.append(copy(a, 0, me, sibling, src=ins[a]))
        for cp in first:
            cp.start()
        passed = []
        for a in range(n):
            for j, chip in enumerate(chips):
                copy(a, 1 + j, (*chip, c), me).wait_recv()
                fwd = copy(a, 4 + j, (*chip, c), sibling)
                fwd.start()
                passed.append(fwd)
        for a in range(n):
            copy(a, 0, sibling, me).wait_recv()
            for j, chip in enumerate(chips):
                copy(a, 4 + j, (*chip, 1 - c), me).wait_recv()
        for cp in first + passed:
            cp.wait_send()
        for cp in local:
            cp.wait()

    return _pcall(body, name=name, out_shape=tuple(jax.ShapeDtypeStruct((N_DEV,) + a.shape, a.dtype) for a in arrays),
                  in_specs=[ANY] * n, out_specs=(ANY,) * n,
                  scratch_shapes=[pltpu.SemaphoreType.DMA((n * PEERS,)), pltpu.SemaphoreType.DMA((n * PEERS,)),
                                  pltpu.SemaphoreType.DMA((n,))])(*arrays)


CHIPS = 4


def _pair_exchange(arrays, *, name):
    n = len(arrays)

    def body(*refs):
        ins, outs = refs[:n], refs[n:2 * n]
        send_sems, recv_sems = refs[2 * n:]
        x, y, c = lax.axis_index("x"), lax.axis_index("y"), lax.axis_index("c")
        copies = []
        for a in range(n):
            for q in range(CHIPS):
                cp = pltpu.make_async_remote_copy(src_ref=ins[a].at[2 * q + 1 - c], dst_ref=outs[a].at[q],
                                                  send_sem=send_sems.at[a * CHIPS + q], recv_sem=recv_sems.at[a * CHIPS + q],
                                                  device_id=(x, y, 1 - c), device_id_type=MESH)
                cp.start()
                copies.append(cp)
        for cp in copies:
            cp.wait()

    return _pcall(body, name=name, out_shape=tuple(jax.ShapeDtypeStruct((CHIPS,) + a.shape[1:], a.dtype) for a in arrays),
                  in_specs=[ANY] * n, out_specs=(ANY,) * n,
                  scratch_shapes=[pltpu.SemaphoreType.DMA((n * CHIPS,)), pltpu.SemaphoreType.DMA((n * CHIPS,))])(*arrays)


def _pair_add(blocks, theirs, *, name):
    _, r, c_ = blocks.shape
    tr = _tile(r, 512, 16)

    def body(mine_ref, theirs_ref, o_ref):
        core = lax.axis_index("c")
        own = jnp.where(core == 0, mine_ref[0, 0].astype(F32), mine_ref[0, 1].astype(F32))
        o_ref[0] = (own + theirs_ref[0].astype(F32)).astype(o_ref.dtype)

    spec = pl.BlockSpec((1, tr, c_), lambda q, i: (q, i, 0))
    return _pcall(body, name=name, out_shape=jax.ShapeDtypeStruct(theirs.shape, theirs.dtype), grid=(CHIPS, r // tr),
                  in_specs=[pl.BlockSpec((1, 2, tr, c_), lambda q, i: (q, 0, i, 0)), spec], out_specs=spec,
                  semantics=("parallel", "parallel"), vmem_limit=VMEM_LIMIT)(blocks.reshape(CHIPS, 2, r, c_), theirs)


HBM = pl.BlockSpec(memory_space=pltpu.HBM)
SEM = pl.BlockSpec(memory_space=pltpu.SEMAPHORE)
EFFECT = pltpu.SideEffectType.DATAFLOW_SIDE_EFFECTING


GATHER, CHIP_GATHER, CHIP_SCATTER = "gather", "chip_gather", "chip_scatter"
PEERS_OF = {GATHER: N_DEV - 1, CHIP_GATHER: CHIPS - 1, CHIP_SCATTER: CHIPS - 1}


def _direct_copies(srcs, lands, send_sems, recv_sems, local_sems, kind):
    x, y, c = lax.axis_index("x"), lax.axis_index("y"), lax.axis_index("c")
    peers = PEERS_OF[kind]
    mine = 2 * x + y if kind == CHIP_SCATTER else 4 * x + 2 * y + c
    copies = []
    for a, (src, land) in enumerate(zip(srcs, lands)):
        copies.append(pltpu.make_async_copy(src.at[mine] if kind == CHIP_SCATTER else src, land.at[mine], local_sems.at[a]))
        for k in range(1, peers + 1):
            bits = k if kind == GATHER else 2 * k
            px = 1 - x if bits & 4 else x
            py = 1 - y if bits & 2 else y
            pc = 1 - c if bits & 1 else c
            copies.append(pltpu.make_async_remote_copy(
                src_ref=src.at[2 * px + py] if kind == CHIP_SCATTER else src, dst_ref=land.at[mine],
                send_sem=send_sems.at[a * peers + k - 1], recv_sem=recv_sems.at[a * peers + k - 1],
                device_id=(px, py, pc), device_id_type=MESH))
    return copies


def _pair_swap(arrays, *, name):
    n = len(arrays)

    def body(*refs):
        mine, zones = refs[:n], refs[n:2 * n]
        send_sems, recv_sems = refs[2 * n:]
        x, y, c = lax.axis_index("x"), lax.axis_index("y"), lax.axis_index("c")
        copies = []
        for a in range(n):
            for q in range(CHIPS):
                copies.append(pltpu.make_async_remote_copy(
                    src_ref=mine[a].at[2 * q + c], dst_ref=zones[a].at[2 * q + c], send_sem=send_sems.at[a * CHIPS + q],
                    recv_sem=recv_sems.at[a * CHIPS + q], device_id=(x, y, 1 - c), device_id_type=MESH))
        for cp in copies:
            cp.start()
        for cp in copies:
            cp.wait()

    return _pcall(body, name=name, out_shape=tuple(jax.ShapeDtypeStruct(a.shape, a.dtype) for a in arrays),
                  in_specs=[ANY] * n, out_specs=(ANY,) * n, input_output_aliases={i: i for i in range(n)},
                  scratch_shapes=[pltpu.SemaphoreType.DMA((n * CHIPS,)), pltpu.SemaphoreType.DMA((n * CHIPS,))])(*arrays)


def _exchange_start(groups, kind, *, name, after=None):
    srcs = [s for group in groups for s in group]
    n = len(srcs)
    sizes = [len(group) for group in groups]
    starts = [sum(sizes[:g]) for g in range(len(groups))]
    land_shapes = [s.shape if kind == CHIP_SCATTER else (N_DEV,) + s.shape for s in srcs]
    peers = PEERS_OF[kind]
    extra = [] if after is None else [after]

    def body(*refs):
        srcs_, lands = refs[:n], refs[n:2 * n]
        token = refs[-1]
        sem_refs = refs[2 * n + len(extra):]
        for g, (at, size) in enumerate(zip(starts, sizes)):
            send_sems, recv_sems, local_sems = sem_refs[3 * g:3 * g + 3]
            for cp in _direct_copies(srcs_[at:at + size], lands[at:at + size], send_sems, recv_sems, local_sems, kind):
                cp.start()
        token[...] = jnp.zeros_like(token)

    sems = tuple(t for size in sizes for t in (pltpu.SemaphoreType.DMA((size * peers,)), pltpu.SemaphoreType.DMA((size * peers,)),
                                               pltpu.SemaphoreType.DMA((size,))))
    thru = tuple(pltpu.HBM(s.shape, s.dtype) for s in srcs) + tuple(pltpu.HBM(shp, s.dtype) for shp, s in zip(land_shapes, srcs))
    ins = [pltpu.with_memory_space_constraint(s, pltpu.HBM) for s in srcs]
    ins += [pltpu.with_memory_space_constraint(lax.empty(shp, s.dtype), pltpu.HBM) for shp, s in zip(land_shapes, srcs)]
    out = pl.pallas_call(
        body, name=name, out_shape=sems + thru + (jax.ShapeDtypeStruct((SUBLANE, LANE), F32),),
        in_specs=[HBM] * (2 * n) + [ANY] * len(extra),
        out_specs=(SEM,) * len(sems) + (HBM,) * (2 * n) + (pl.BlockSpec(memory_space=pltpu.VMEM),),
        input_output_aliases={i: len(sems) + i for i in range(2 * n)},
        compiler_params=pltpu.CompilerParams(has_side_effects=EFFECT))(*ins, *extra)
    arrays = out[len(sems):-1]
    started = [tuple(out[3 * g:3 * g + 3]) + tuple(arrays[at:at + size]) + tuple(arrays[n + at:n + at + size])
               for g, (at, size) in enumerate(zip(starts, sizes))]
    return started, out[-1]


def _exchange_wait(started, after, kind, *, name):
    n = (len(started) - 3) // 2
    sems, arrays = started[:3], started[3:]

    def body(*refs):
        srcs_, lands = refs[:n], refs[n:2 * n]
        send_sems, recv_sems, local_sems = refs[2 * n:2 * n + 3]
        for cp in _direct_copies(srcs_, lands, send_sems, recv_sems, local_sems, kind):
            cp.wait()

    out = pl.pallas_call(
        body, name=name, out_shape=tuple(pltpu.HBM(a.shape, a.dtype) for a in arrays),
        in_specs=[HBM] * (2 * n) + [SEM] * 3 + [ANY], out_specs=(HBM,) * (2 * n),
        input_output_aliases={i: i for i in range(2 * n)},
        compiler_params=pltpu.CompilerParams(has_side_effects=EFFECT))(*arrays, *sems, after)
    return out[n:]


def _adamw_reduce(w, parts, m, v, *, name, after=None):
    layers, r, c = w.shape
    assert len(parts) == layers
    senders = parts[0].shape[0]
    tr = _tile(r, 512, 16)
    tiles = r // tr
    bc1 = 1.0 - ADAM_B1 ** ADAM_STEP
    bc2 = 1.0 - ADAM_B2 ** ADAM_STEP

    def body(w_ref, *rest):
        p_refs = rest[:layers]
        m_ref, v_ref, g_ref, d_ref, nm_ref, nv_ref = rest[layers:]

        def update(p_ref):
            g = p_ref[0, :, pl.ds(0, c)].astype(F32)
            for s in range(1, senders):
                g = g + p_ref[s, :, pl.ds(0, c)].astype(F32)
            nm = ADAM_B1 * m_ref[0] + (1.0 - ADAM_B1) * g
            nv = ADAM_B2 * v_ref[0] + (1.0 - ADAM_B2) * (g * g)
            g_ref[0] = g
            nm_ref[0] = nm
            nv_ref[0] = nv
            d_ref[0] = -ADAM_LR * ((nm / bc1) / (jnp.sqrt(nv / bc2) + ADAM_EPS) + ADAM_WD * w_ref[0])

        for layer in range(layers):
            pl.when(pl.program_id(0) == layer)(functools.partial(update, p_refs[layer]))

    def part_spec(layer, shape):
        rest = 0 if layer > 0 else tiles - 1
        return pl.BlockSpec((senders, tr, shape[2]), lambda l, i: (0, jnp.where(l == layer, i, rest), 0))

    spec = pl.BlockSpec((1, tr, c), lambda l, i: (l, i, 0))
    out = jax.ShapeDtypeStruct((layers, r, c), F32)
    return _pcall(body, name=name, out_shape=(out,) * 4, grid=(layers, tiles),
                  in_specs=[spec] + [part_spec(layer, p.shape) for layer, p in enumerate(parts)] + [spec, spec],
                  out_specs=(spec,) * 4, semantics=("arbitrary", "arbitrary"), vmem_limit=VMEM_LIMIT, after=after)(w, *parts, m, v)


def _pool_windows():
    return jnp.repeat(jnp.asarray(POOL_WINDOWS, F32), POOL_DIM // len(POOL_WINDOWS))[None, :]


def _block_diag_pairs(pool_w):
    z = jnp.zeros_like(pool_w[0])
    return jnp.stack([jnp.block([[pool_w[2 * b], z], [z, pool_w[2 * b + 1]]]) for b in range(2)])


def _pad_lanes(vec):
    return jnp.zeros((1, LANE), F32).at[0, :vec.shape[0]].set(vec)


FF_SHARD = D_FF // N_DEV
FF_BLOCK = 384
D_FF_PAD = N_DEV * FF_BLOCK


def _layer_fwd(x, p_i, wt, fetch):
    wt = {**wt, **fetch(0, x)}
    proj, h1 = _matmul(x, wt["w_in"], "nt", norm_g=wt["norm1_g"], name="mm_in")
    qkv = _qkv_prep_fwd(proj, wt["conv_qkv"], name="qkv_prep_fwd")
    g, beta = _gates_fwd(proj, wt["a_log"], wt["dt_bias"], name="gates_fwd")
    u, w, qg, kg, attn, tmats = _deltanet_prep(qkv, g, beta, name="deltanet_prep")
    wt.update(fetch(1, u))
    o, vn, states = _deltanet_scan(u, w, qg, kg, attn, g, name="deltanet_scan", after=wt.get("behind"))
    o_a = _apost_fwd(o, proj, wt["onorm_g"], name="apost_fwd")
    o_b = _pool_fwd(proj, wt["pool_win"], wt["pool_wbd"], wt["pool_scale"], name="pool_fwd")
    o_c = _sconv_fwd(proj, wt["sconv_w"], name="sconv_fwd")
    mixed = jnp.concatenate([o_a, o_b, o_c], axis=1)
    x1 = _matmul(mixed, wt["w_out"], "nn", res=x, name="mm_out")
    wt.update(fetch(2, x1))
    ff, gate, up, h2 = _swiglu_fwd(x1, wt["norm2_g"], wt["w_gate"], wt["w_up"], name="swiglu_fwd")
    wt.update(fetch(3, ff))
    x2 = _matmul(ff, wt["w_down"], "nn", res=x1, name="mm_down")
    wt.update(fetch(4, x2))
    x3, pgl, pp = _ple_fwd(x2, p_i, wt["ple_gate"], wt["ple_proj"], name="ple_fwd")
    saved = dict(x=x, h1=h1, proj=proj, qkv=qkv, g=g, beta=beta, o=o, states=states, tmats=tmats, mixed=mixed, x1=x1, h2=h2,
                 gate=gate, up=up, ff=ff, x2=x2, pgl=pgl, pp=pp, p=p_i, w=w, qg=qg, kg=kg, attn=attn, vn=vn, wt=wt)
    return x3, saved


def _col_blocks(g):
    a = g.shape[0]
    return jnp.transpose(g.reshape(a, N_DEV, -1), (1, 0, 2))


def _cols_joined(blocks):
    return jnp.transpose(blocks, (1, 0, 2)).reshape(blocks.shape[1], -1)


def _layer_bwd(dx3, sv, emit, after=None):
    gr, big = {}, {}
    wt = sv["wt"]
    rows = D_MODEL // N_DEV
    dpgl, dpp = _ple_bwd(dx3, sv["pgl"], sv["pp"], name="ple_bwd", after=after)
    big["ple_proj"] = _matmul(sv["p"], dpp, "tn", out_blocked=(N_DEV, rows), out_dtype=BF16, name="mm_dplep")
    big["ple_gate"] = _matmul(sv["x2"], dpgl, "tn", out_dtype=BF16, name="mm_dpleg").reshape(N_DEV, rows, D_MODEL)
    dx2 = _matmul(dpgl, wt["ple_gate"], "nt", res=dx3, name="mm_dx2")
    big["w_down"] = _matmul(sv["ff"], dx2, "tn", out_dtype=BF16, name="mm_ddown").reshape(N_DEV, FF_BLOCK, D_MODEL)
    dgate, dup = _swiglu_bwd(dx2, wt["w_down"], sv["gate"], sv["up"], name="swiglu_bwd", after=emit(0, big))
    big["w_gate"] = _matmul(dgate, sv["h2"], "tn", out_dtype=BF16, name="mm_dgate").reshape(N_DEV, FF_BLOCK, D_MODEL)
    big["w_up"] = _matmul(dup, sv["h2"], "tn", out_dtype=BF16, name="mm_dup").reshape(N_DEV, FF_BLOCK, D_MODEL)
    dh2 = _matmul(dgate, wt["w_gate"], "nn", name="mm_dh2_gate")
    dx1, gr["norm2_g"] = _matmul_norm_bwd(dup, wt["w_up"], sv["x1"], wt["norm2_g"], dx2, res=dh2, name="mm_dh2_up")
    big["w_out"] = _matmul(sv["mixed"], dx1, "tn", out_dtype=BF16, name="mm_dout").reshape(N_DEV, rows, D_MODEL)
    dmixed = _matmul(dx1, wt["w_out"], "nt", name="mm_dmixed", after=emit(1, big))
    proj = sv["proj"]
    dcb, dcc, dch, dsconv = _sconv_bwd(proj, wt["sconv_w"], dmixed, name="sconv_bwd")
    big["sconv_w"] = _col_blocks(dsconv)
    dhp, dwbd, gr["pool_scale"] = _pool_bwd(proj, wt["pool_win"], wt["pool_wbd"], wt["pool_scale"], dmixed, name="pool_bwd")
    half = LANE // 2
    gr["pool_w"] = jnp.stack([dwbd[0, :half, :half], dwbd[0, half:, half:], dwbd[1, :half, :half], dwbd[1, half:, half:]])
    do, dz, gr["onorm_g"] = _apost_bwd(sv["o"], proj, wt["onorm_g"], dmixed, name="apost_bwd")
    dvn, dstates = _deltanet_bscan(sv["w"], sv["qg"], sv["kg"], sv["attn"], sv["g"], do, name="deltanet_bscan")
    dqkv_h, dg, dbeta = _deltanet_post(sv["qkv"], sv["g"], sv["beta"], sv["tmats"], sv["states"], dstates, do, dvn, sv["vn"],
                                       name="deltanet_post")
    dab, dalog, ddtb = _gates_bwd(proj, wt["a_log"], wt["dt_bias"], dg, dbeta, name="gates_bwd")
    gr["a_log"], gr["dt_bias"] = dalog[0, :HEADS], ddtb[0, :HEADS]
    dqkv, dconv = _qkv_prep_bwd(proj, wt["conv_qkv"], dqkv_h, name="qkv_prep_bwd")
    big["conv_qkv"] = _col_blocks(dconv)
    dproj = jnp.concatenate([dqkv, dz, dab, dhp, dcb, dcc, dch], axis=1)
    dwin = _matmul(dproj, sv["h1"], "tn", out_dtype=BF16, name="mm_din")
    big["w_in"] = jnp.concatenate([dwin[:AB_COL + 2 * HEADS], dwin[AB_COL + LANE:]], axis=0).reshape(N_DEV, -1, D_MODEL)
    dx, gr["norm1_g"] = _matmul_norm_bwd(dproj, wt["w_in"], sv["x"], wt["norm1_g"], dx1, name="mm_dh1", after=emit(2, big))
    return dx, gr


FETCH_GROUPS = (("w_in", "conv_qkv", "sconv_w"), ("w_out",), ("w_gate", "w_up"), ("w_down",), ("ple_gate", "ple_proj"))
EMIT_GROUPS = (("ple_proj", "ple_gate", "w_down"), ("w_gate", "w_up", "w_out"), ("w_in", "conv_qkv", "sconv_w"))


def _small_weights(w, i):
    return dict(
        norm1_g=w["norm1_g"][i][None], norm2_g=w["norm2_g"][i][None], onorm_g=w["onorm_g"][i][None],
        a_log=_pad_lanes(w["a_log"][i]), dt_bias=_pad_lanes(w["dt_bias"][i]),
        pool_scale=w["pool_scale"][i][None], pool_win=_pool_windows(), pool_wbd=_block_diag_pairs(w["pool_w"][i]))


def _as_read(name, gathered):
    if name == "w_in":
        rows = gathered[:, :D_IN // N_DEV].reshape(-1, D_MODEL)
        return jnp.concatenate([rows[:AB_COL + 2 * HEADS], jnp.zeros((LANE - 2 * HEADS, D_MODEL), BF16),
                                rows[AB_COL + 2 * HEADS:]], axis=0)
    if name in ("conv_qkv", "sconv_w"):
        return _cols_joined(gathered)
    if name == "ple_proj":
        return gathered
    return gathered.reshape(-1, D_MODEL)


def _layer_weights(gathered, w, i):
    return {**_small_weights(w, i), **{k: _as_read(k, g) for k, g in gathered.items()}}


def _local_step(x, p, target, layers, final_g):
    saved = []
    h = x
    for i in range(DEPTH):
        replicated = {k: v for k, v in layers[i].items() if k not in SHARDED}
        h, sv = _layer_fwd(h, p[i], replicated, lambda group, after, i=i: {k: layers[i][k] for k in FETCH_GROUPS[group]})
        saved.append(sv)
    dx, dgf, loss = _loss_head(h, final_g, target, name="loss_head")
    big, small = [{} for _ in range(DEPTH)], [None] * DEPTH
    for i in reversed(range(DEPTH)):
        dx, small[i] = _layer_bwd(dx, saved[i], lambda group, blocks, i=i: big[i].update({k: blocks[k] for k in EMIT_GROUPS[group]}))
    return loss, dx, big, small, dgf


SHARDED = ("w_in", "w_gate", "w_up", "w_down", "w_out", "ple_gate", "ple_proj", "conv_qkv", "sconv_w")
SMALL = ("norm1_g", "a_log", "dt_bias", "onorm_g", "pool_w", "pool_scale", "norm2_g", "final_g")
SLAB_COLS = 1024


def _payload(name, shard):
    if name in ("conv_qkv", "sconv_w"):
        return shard
    out = shard.astype(BF16)
    if name in ("w_gate", "w_up", "w_down"):
        out = jnp.pad(out, ((0, FF_BLOCK - FF_SHARD), (0, 0)))
    if name == "w_in":
        out = jnp.pad(out, ((0, -out.shape[0] % (2 * SUBLANE)), (0, 0)))
    return out


TRANSPOSED = ("w_in", "w_gate", "w_up")


def _ff_rows(t):
    return jnp.transpose(t, (0, 2, 1))


def _slab_rows(shape):
    size = 1
    for s in shape:
        size *= s
    return SUBLANE * -(-size // (SUBLANE * SLAB_COLS))


def _pack_slab(parts, extra_row):
    rows = []
    for name in SMALL:
        flat = parts[name].reshape(-1)
        nrow = _slab_rows(parts[name].shape)
        rows.append(jnp.pad(flat, (0, nrow * SLAB_COLS - flat.shape[0])).reshape(nrow, SLAB_COLS))
    rows.append(jnp.pad(extra_row, ((0, SUBLANE - 1), (0, 0))))
    return jnp.concatenate(rows, axis=0)


def _unpack_slab(slab, shapes):
    out, row = {}, 0
    for name in SMALL:
        size = 1
        for s in shapes[name]:
            size *= s
        out[name] = slab[row:row + _slab_rows(shapes[name])].reshape(-1)[:size].reshape(shapes[name])
        row += _slab_rows(shapes[name])
    return out, row


def kernel(x, p, norm1_g, w_in, conv_qkv, a_log, dt_bias, onorm_g, pool_w, pool_scale, sconv_w, w_out, norm2_g, w_gate, w_up, w_down, ple_proj, ple_gate, final_g, loss_target, m_norm1_g, m_w_in, m_conv_qkv, m_a_log, m_dt_bias, m_onorm_g, m_pool_w, m_pool_scale, m_sconv_w, m_w_out, m_norm2_g, m_w_gate, m_w_up, m_w_down, m_ple_proj, m_ple_gate, m_final_g, v_norm1_g, v_w_in, v_conv_qkv, v_a_log, v_dt_bias, v_onorm_g, v_pool_w, v_pool_scale, v_sconv_w, v_w_out, v_norm2_g, v_w_gate, v_w_up, v_w_down, v_ple_proj, v_ple_gate, v_final_g):
    names = ["norm1_g", "w_in", "conv_qkv", "a_log", "dt_bias", "onorm_g", "pool_w", "pool_scale", "sconv_w", "w_out", "norm2_g",
             "w_gate", "w_up", "w_down", "ple_proj", "ple_gate", "final_g"]
    w = dict(zip(names, [norm1_g, w_in, conv_qkv, a_log, dt_bias, onorm_g, pool_w, pool_scale, sconv_w, w_out, norm2_g, w_gate, w_up,
                         w_down, ple_proj, ple_gate, final_g]))
    m = dict(zip(names, [m_norm1_g, m_w_in, m_conv_qkv, m_a_log, m_dt_bias, m_onorm_g, m_pool_w, m_pool_scale, m_sconv_w, m_w_out,
                         m_norm2_g, m_w_gate, m_w_up, m_w_down, m_ple_proj, m_ple_gate, m_final_g]))
    v = dict(zip(names, [v_norm1_g, v_w_in, v_conv_qkv, v_a_log, v_dt_bias, v_onorm_g, v_pool_w, v_pool_scale, v_sconv_w, v_w_out,
                         v_norm2_g, v_w_gate, v_w_up, v_w_down, v_ple_proj, v_ple_gate, v_final_g]))
    w.update({k: _ff_rows(w[k]) for k in TRANSPOSED})

    first, rest = FETCH_GROUPS[0], tuple(k for members in FETCH_GROUPS[1:] for k in members)
    gathered = dict(zip(first, _all_gather([_payload(k, w[k][0]) for k in first], name="all_gather_weights")))
    (flying0,), token = _exchange_start([[_payload(k, w[k][0]) for k in rest]], CHIP_GATHER, name="gather_start_0",
                                        after=gathered[first[0]])
    replicated = [_small_weights(w, i) for i in range(DEPTH)]
    replicated[0]["norm1_g"] = replicated[0]["norm1_g"] + token[0, 0]
    for group in (m, v):
        group.update({k: _ff_rows(group[k] + token[0, 0]) for k in TRANSPOSED})
    flying1 = []

    def fetch(i, group, after):
        if i == 0 and group == 1:
            landed = _exchange_wait(flying0, after, CHIP_GATHER, name="gather_wait_0")
            gathered.update(zip(rest, _pair_swap(landed, name="pair_swap")))
            started, token = _exchange_start([[_payload(k, w[k][1]) for k in SHARDED]], CHIP_GATHER, name="gather_start_1",
                                             after=gathered[rest[0]])
            flying1.extend(started)
            return {**{k: _as_read(k, gathered[k]) for k in FETCH_GROUPS[group]}, "behind": token}
        if i == 1 and group == 0:
            landed = _exchange_wait(flying1[0], after, CHIP_GATHER, name="gather_wait_1")
            gathered.update(zip(SHARDED, _pair_swap(landed, name="pair_swap")))
        return {k: _as_read(k, gathered[k]) for k in FETCH_GROUPS[group]}

    def reduce_scatter_start(members, blocks, tag):
        mine = [blocks[k] for k in members]
        theirs = _pair_exchange(mine, name="pair_exchange")
        sums = [_pair_add(a, b, name="pair_add") for a, b in zip(mine, theirs)]
        (started,), token = _exchange_start([sums], CHIP_SCATTER, name="exchange_start_" + tag)
        return started, token

    h, saved0 = _layer_fwd(x[0], p[0, 0], replicated[0], functools.partial(fetch, 0))
    h, saved1 = _layer_fwd(h, p[1, 0], replicated[1], functools.partial(fetch, 1))
    dx, dgf, loss_part = _loss_head(h, final_g[None], loss_target[0], name="loss_head")
    small, big1, flying0 = [None] * DEPTH, {}, []
    dx, small[1] = _layer_bwd(dx, saved1, lambda group, blocks: big1.update({k: blocks[k] for k in EMIT_GROUPS[group]}))
    flying1, token = reduce_scatter_start(SHARDED, big1, "1")

    def emit(group, blocks):
        started, token = reduce_scatter_start(EMIT_GROUPS[group], blocks, f"0_{group}")
        flying0.append(started)
        return token

    dx, small[0] = _layer_bwd(dx, saved0, emit, after=token)
    received = [{}, dict(zip(SHARDED, _exchange_wait(flying1, dx, CHIP_SCATTER, name="exchange_wait_1")))]
    for group, members in enumerate(EMIT_GROUPS):
        received[0].update(zip(members, _exchange_wait(flying0[group], dx, CHIP_SCATTER, name=f"exchange_wait_0_{group}")))

    grads = {k: jnp.stack([small[i][k] for i in range(DEPTH)]) for k in small[0]}
    grads = {k: g[:, 0] if k in ("norm1_g", "norm2_g", "onorm_g", "pool_scale") else g for k, g in grads.items()}
    grads["final_g"] = dgf[0]
    loss_row = jnp.pad(loss_part, ((0, 0), (0, SLAB_COLS - LANE)))
    (small_flying,), token = _exchange_start([[_pack_slab(grads, loss_row)]], GATHER, name="small_gather_start")

    out_g, out_d, out_m, out_v = {}, {}, {}, {}
    for k in SHARDED:
        out_g[k], out_d[k], out_m[k], out_v[k] = _adamw_reduce(w[k], [received[i][k] for i in range(DEPTH)], m[k], v[k],
                                                                name="adamw_" + k, after=token)
    behind_all = jnp.stack([out_v[k][0, 0, 0] for k in SHARDED])
    (small_parts,) = _exchange_wait(small_flying, behind_all, GATHER, name="small_gather_wait")
    zero_row = jnp.zeros((1, SLAB_COLS), F32)
    slabs = _adamw_reduce(_pack_slab(w, zero_row)[None], [small_parts], _pack_slab(m, zero_row)[None],
                          _pack_slab(v, zero_row)[None], name="adamw_small")
    slabs = [s[0] for s in slabs]
    shapes = {k: w[k].shape for k in SMALL}
    for dst, slab in zip((out_g, out_d, out_m, out_v), slabs):
        vals, _ = _unpack_slab(slab, shapes)
        dst.update(vals)
    _, loss_at = _unpack_slab(slabs[0], shapes)
    loss = slabs[0][loss_at, 0]
    for group in (out_g, out_d, out_m, out_v):
        group.update({k: _ff_rows(group[k]) for k in TRANSPOSED})

    return (loss, dx[None], *[out_g[k] for k in names], *[out_d[k] for k in names], *[out_m[k] for k in names],
            *[out_v[k] for k in names])
```

```python
import functools

import jax
import jax.numpy as jnp
from jax import lax
from jax.experimental import pallas as pl
from jax.experimental.pallas import tpu as pltpu

F32 = jnp.float32
BF16 = jnp.bfloat16

D_MODEL = 1024
DEPTH = 2
PLE_DIM = 256
EPS = 1e-6
HEAD_DIM = 128
HEADS = 4
A_DIM = HEADS * HEAD_DIM
QKV_TAPS = 4
CHUNK = 64
POOL_WINDOWS = (2, 4, 8, 16)
POOL_DIM = 256
CONV_DIM = 256
CONV_TAPS = 3
D_FF = 2816
D_IN = 3080
AB_COL = 2048
N_DEV = 8

ADAM_LR = 0.001
ADAM_B1 = 0.9
ADAM_B2 = 0.999
ADAM_EPS = 1e-08
ADAM_WD = 0.01
ADAM_STEP = 10

LANE = 128
SUBLANE = 8
VMEM_BYTES_V7X = 64 * 1024 * 1024
VMEM_LIMIT = VMEM_BYTES_V7X * 3 // 4

NN = ((1,), (0,))
NT = ((1,), (1,))
TN = ((0,), (0,))
MESH = pl.DeviceIdType.MESH


def _dot(a, b, dims):
    return lax.dot_general(a.astype(BF16), b.astype(BF16), (dims, ((), ())), preferred_element_type=F32)


def _pcall(body, *, name, out_shape, grid=(), in_specs=None, out_specs=None, scratch_shapes=(), semantics=None,
           vmem_limit=None, after=None, **kw):
    params = {}
    if semantics is not None:
        params["dimension_semantics"] = semantics
    if vmem_limit is not None:
        params["vmem_limit_bytes"] = vmem_limit
    if after is not None:
        n_in, inner = len(in_specs), body
        body = lambda *refs: inner(*refs[:n_in], *refs[n_in + 1:])
        in_specs = list(in_specs) + [pl.BlockSpec(after.shape, lambda *_: (0,) * after.ndim)]
    call = pl.pallas_call(
        body, name=name, out_shape=out_shape, grid=grid, in_specs=in_specs, out_specs=out_specs,
        scratch_shapes=list(scratch_shapes), compiler_params=pltpu.CompilerParams(**params), **kw)
    return call if after is None else (lambda *args: call(*args, after))


def _sigmoid(x):
    return 1.0 / (1.0 + jnp.exp(-x))


def _softplus(x):
    return jnp.maximum(x, 0.0) + jnp.log(1.0 + jnp.exp(-jnp.abs(x)))


def _tile(n, cap, mult):
    if n <= cap:
        return n
    best = None
    for t in range(mult, cap + 1, mult):
        if n % t == 0:
            best = t
    assert best is not None, (n, cap, mult)
    return best


ROWS_PER_STEP = 512
NARROW_RESULT = 1024
COLS_PER_DOT = 640


def _resident(weight):
    return pl.BlockSpec(weight.shape, lambda i: (0,) * weight.ndim, pipeline_mode=pl.Buffered(1))


def _matmul_rows(a, b, mode, *, name, res=None, out_dtype=F32, b_blocked=False, after=None, norm_g=None):
    m, k = a.shape
    if b_blocked:
        nb, _, bw = b.shape
        n = nb * bw if mode == "nn" else b.shape[1]
    else:
        n = b.shape[1] if mode == "nn" else b.shape[0]
    tm = _tile(m, ROWS_PER_STEP if n > NARROW_RESULT else 2 * ROWS_PER_STEP, 16)
    cn = bw if (b_blocked and mode == "nn") else _tile(n, COLS_PER_DOT, LANE)
    has_res = res is not None
    normed = norm_g is not None

    def body(*refs):
        a_ref, b_ref = refs[0], refs[1]
        g_ref = refs[2] if normed else None
        res_ref = refs[2 + normed] if has_res else None
        o_ref = refs[2 + normed + has_res]
        if normed:
            av = _rms_normed(a_ref[...], g_ref[...])
            refs[3 + normed + has_res][...] = av
        elif not (b_blocked and mode == "nt"):
            av = a_ref[...].astype(BF16)
        for j in range(n // cn):
            cols = pl.ds(j * cn, cn)
            if mode == "nn":
                part = _dot(av, b_ref[j] if b_blocked else b_ref[:, cols], NN)
            elif not b_blocked:
                part = _dot(av, b_ref[cols, :], NT)
            else:
                part = None
                for s in range(nb):
                    term = _dot(a_ref[:, pl.ds(s * bw, bw)], b_ref[s, cols, :], NT)
                    part = term if part is None else part + term
            if has_res:
                part = part + res_ref[:, cols]
            o_ref[:, cols] = part.astype(o_ref.dtype)

    row = lambda width: pl.BlockSpec((tm, width), lambda i: (i, 0))
    whole = _resident(b)
    ins = [a, b] + ([norm_g] if normed else []) + ([res] if has_res else [])
    specs = [row(k), whole] + ([pl.BlockSpec((1, k), lambda i: (0, 0))] if normed else []) + ([row(n)] if has_res else [])
    out = jax.ShapeDtypeStruct((m, n), out_dtype)
    return _pcall(body, name=name, out_shape=(out, jax.ShapeDtypeStruct((m, k), BF16)) if normed else out, grid=(m // tm,),
                  in_specs=specs, out_specs=(row(n), row(k)) if normed else row(n), semantics=("parallel",),
                  vmem_limit=VMEM_LIMIT, after=after)(*ins)


def _matmul_norm_bwd(a, b, x, g, dres, *, name, more=None, after=None):
    m, k = a.shape
    d = b.shape[1]
    tm = _tile(m, ROWS_PER_STEP, 16)
    cn = _tile(d, COLS_PER_DOT, LANE)
    pairs = 1 if more is None else 2

    def body(*refs):
        a_ref, b_ref, x_ref, g_ref, dres_ref = refs[:5]
        dx_ref, dg_ref = refs[3 + 2 * pairs], refs[4 + 2 * pairs]
        av = a_ref[...].astype(BF16)
        for j in range(d // cn):
            cols = pl.ds(j * cn, cn)
            part = _dot(av, b_ref[:, cols], NN)
            if more is not None:
                part = part + _dot(refs[5][...], refs[6][:, cols], NN)
            dx_ref[:, cols] = part
        dhv = dx_ref[...]
        xv = x_ref[...]
        r = lax.rsqrt(jnp.mean(xv * xv, axis=-1, keepdims=True) + EPS)
        xhat = xv * r
        dhg = dhv * g_ref[...]
        dx_ref[...] = dres_ref[...] + r * (dhg - xhat * jnp.mean(dhg * xhat, axis=-1, keepdims=True))
        part_g = jnp.sum(dhv * xhat, axis=0, keepdims=True)

        @pl.when(pl.program_id(0) == 0)
        def _():
            dg_ref[...] = part_g

        @pl.when(pl.program_id(0) > 0)
        def _():
            dg_ref[...] += part_g

    row = lambda width: pl.BlockSpec((tm, width), lambda i: (i, 0))
    vec = pl.BlockSpec((1, d), lambda i: (0, 0))
    ins = [a, b, x, g, dres] + (list(more) if more is not None else [])
    specs = [row(k), _resident(b), row(d), vec, row(d)] + ([row(more[0].shape[1]), _resident(more[1])] if more is not None else [])
    return _pcall(body, name=name, out_shape=(jax.ShapeDtypeStruct((m, d), F32), jax.ShapeDtypeStruct((1, d), F32)),
                  grid=(m // tm,), in_specs=specs, out_specs=(row(d), vec), semantics=("arbitrary",), vmem_limit=VMEM_LIMIT,
                  after=after)(*ins)


def _rms_normed(xv, gv):
    return (xv * lax.rsqrt(jnp.mean(xv * xv, axis=-1, keepdims=True) + EPS) * gv).astype(BF16)


def _matmul(a, b, mode, *, name, res=None, out_dtype=F32, b_blocked=False, out_blocked=None, after=None, norm_g=None):
    if mode != "tn":
        return _matmul_rows(a, b, mode, name=name, res=res, out_dtype=out_dtype, b_blocked=b_blocked, after=after, norm_g=norm_g)
    assert res is None and not b_blocked and after is None and norm_g is None
    (t, m), (t2, n) = a.shape, b.shape
    assert t == t2, (a.shape, b.shape)
    tm = _tile(m, 1024, LANE)
    tn = _tile(n, NARROW_RESULT if n <= NARROW_RESULT else COLS_PER_DOT, LANE)
    if out_blocked is not None:
        assert out_blocked[0] * out_blocked[1] == n
        tn = out_blocked[1]

    def body(a_ref, b_ref, o_ref):
        part = _dot(a_ref[...], b_ref[...], TN).astype(o_ref.dtype)
        if out_blocked is None:
            o_ref[...] = part
        else:
            o_ref[0] = part

    o_spec = (pl.BlockSpec((tm, tn), lambda i, j: (i, j)) if out_blocked is None
              else pl.BlockSpec((1, tm, tn), lambda i, j: (j, i, 0)))
    o_shape = (m, n) if out_blocked is None else (out_blocked[0], m, out_blocked[1])
    return _pcall(body, name=name, out_shape=jax.ShapeDtypeStruct(o_shape, out_dtype), grid=(m // tm, n // tn),
                  in_specs=[pl.BlockSpec((t, tm), lambda i, j: (0, i)), pl.BlockSpec((t, tn), lambda i, j: (0, j))],
                  out_specs=o_spec, semantics=("parallel", "parallel"), vmem_limit=VMEM_LIMIT)(a, b)


ROW_TILE = 512


def _rows(t, width, idx=0):
    return pl.BlockSpec((ROW_TILE, width), lambda i: (i, idx))


def _vec(width):
    return pl.BlockSpec((1, width), lambda i: (0, 0))


def _swiglu_fwd(x, norm_g, w_gate, w_up, *, name):
    t, k = x.shape
    f = w_gate.shape[0]
    tm = _tile(t, ROWS_PER_STEP, 16)
    cn = _tile(f, COLS_PER_DOT, LANE)

    def body(x_ref, g_ref, wg_ref, wu_ref, ff_ref, gate_ref, up_ref, h_ref):
        hv = _rms_normed(x_ref[...], g_ref[...])
        h_ref[...] = hv
        for j in range(f // cn):
            cols = pl.ds(j * cn, cn)
            gv = _dot(hv, wg_ref[cols, :], NT)
            uv = _dot(hv, wu_ref[cols, :], NT)
            gate_ref[:, cols] = gv.astype(BF16)
            up_ref[:, cols] = uv.astype(BF16)
            ff_ref[:, cols] = (gv * _sigmoid(gv) * uv).astype(BF16)

    row = lambda width: pl.BlockSpec((tm, width), lambda i: (i, 0))
    out = jax.ShapeDtypeStruct((t, f), BF16)
    return _pcall(body, name=name, out_shape=(out,) * 3 + (jax.ShapeDtypeStruct((t, k), BF16),), grid=(t // tm,),
                  in_specs=[row(k), pl.BlockSpec((1, k), lambda i: (0, 0)), _resident(w_gate), _resident(w_up)],
                  out_specs=(row(f),) * 3 + (row(k),), semantics=("parallel",), vmem_limit=VMEM_LIMIT)(x, norm_g, w_gate, w_up)


def _swiglu_bwd(dx2, w_down, gate, up, *, name, after=None):
    t, d = dx2.shape
    f = w_down.shape[0]
    tm = _tile(t, ROWS_PER_STEP, 16)
    cn = _tile(f, COLS_PER_DOT, LANE)

    def body(dx_ref, w_ref, gate_ref, up_ref, dgate_ref, dup_ref):
        dxv = dx_ref[...].astype(BF16)
        for j in range(f // cn):
            cols = pl.ds(j * cn, cn)
            dffv = _dot(dxv, w_ref[cols, :], NT)
            gv = gate_ref[:, cols].astype(F32)
            sig = _sigmoid(gv)
            dgate_ref[:, cols] = (dffv * up_ref[:, cols].astype(F32) * sig * (1.0 + gv * (1.0 - sig))).astype(BF16)
            dup_ref[:, cols] = (dffv * gv * sig).astype(BF16)

    row = lambda width: pl.BlockSpec((tm, width), lambda i: (i, 0))
    out = jax.ShapeDtypeStruct((t, f), BF16)
    return _pcall(body, name=name, out_shape=(out, out), grid=(t // tm,), in_specs=[row(d), _resident(w_down), row(f), row(f)],
                  out_specs=(row(f), row(f)), semantics=("parallel",), vmem_limit=VMEM_LIMIT, after=after)(dx2, w_down, gate, up)


def _ple_fwd(x2, p, w_gate, w_proj, *, name):
    t, d = x2.shape
    nb, pdim, bw = w_proj.shape
    tm = _tile(t, ROWS_PER_STEP, 16)
    cn = _tile(d, COLS_PER_DOT, LANE)

    def body(x_ref, p_ref, wg_ref, wp_ref, x3_ref, pgl_ref, pp_ref):
        xb = x_ref[...].astype(BF16)
        pb = p_ref[...].astype(BF16)
        per = cn // bw
        for c in range(d // cn):
            cols = pl.ds(c * cn, cn)
            pgl = _dot(xb, wg_ref[:, cols], NN)
            pp = jnp.concatenate([_dot(pb, wp_ref[c * per + j], NN) for j in range(per)], axis=1)
            pgl_ref[:, cols] = pgl
            pp_ref[:, cols] = pp
            x3_ref[:, cols] = x_ref[:, cols] + _sigmoid(pgl) * pp

    row = lambda width: pl.BlockSpec((tm, width), lambda i: (i, 0))
    out = jax.ShapeDtypeStruct((t, d), F32)
    return _pcall(body, name=name, out_shape=(out,) * 3, grid=(t // tm,),
                  in_specs=[row(d), row(pdim), _resident(w_gate), _resident(w_proj)], out_specs=(row(d),) * 3,
                  semantics=("parallel",), vmem_limit=VMEM_LIMIT)(x2, p, w_gate, w_proj)


def _ple_bwd(dx3, pgl, pp, *, name, after=None):
    t, d = dx3.shape

    def body(dx_ref, pgl_ref, pp_ref, dpgl_ref, dpp_ref):
        dxv = dx_ref[...]
        sig = _sigmoid(pgl_ref[...])
        dpp_ref[...] = (dxv * sig).astype(BF16)
        dpgl_ref[...] = (dxv * pp_ref[...] * sig * (1.0 - sig)).astype(BF16)

    return _pcall(body, name=name, out_shape=(jax.ShapeDtypeStruct((t, d), BF16),) * 2, grid=(t // ROW_TILE,),
                  in_specs=[_rows(t, d)] * 3, out_specs=(_rows(t, d),) * 2, semantics=("parallel",), after=after)(dx3, pgl, pp)


def _loss_head(x3, g, target, *, name):
    t, d = x3.shape

    def body(x_ref, g_ref, t_ref, dx_ref, dg_ref, loss_ref):
        xv = x_ref[...]
        r = lax.rsqrt(jnp.mean(xv * xv, axis=-1, keepdims=True) + EPS)
        xhat = xv * r
        gv = g_ref[...]
        err = xhat * gv - t_ref[...]
        row_loss = jnp.sum(err * err, axis=-1, keepdims=True) * (0.5 / d)
        lpart = jnp.broadcast_to(jnp.sum(row_loss, axis=0, keepdims=True), (1, LANE))
        dy = err * (1.0 / d)
        dyg = dy * gv
        dx_ref[...] = r * (dyg - xhat * jnp.mean(dyg * xhat, axis=-1, keepdims=True))
        gpart = jnp.sum(dy * xhat, axis=0, keepdims=True)

        @pl.when(pl.program_id(0) == 0)
        def _():
            dg_ref[...] = gpart
            loss_ref[...] = lpart

        @pl.when(pl.program_id(0) > 0)
        def _():
            dg_ref[...] += gpart
            loss_ref[...] += lpart

    return _pcall(body, name=name,
                  out_shape=(jax.ShapeDtypeStruct((t, d), F32), jax.ShapeDtypeStruct((1, d), F32), jax.ShapeDtypeStruct((1, LANE), F32)),
                  grid=(t // ROW_TILE,), in_specs=[_rows(t, d), _vec(d), _rows(t, d)],
                  out_specs=(_rows(t, d), _vec(d), _vec(LANE)), semantics=("arbitrary",))(x3, g, target)


def _shift_down(x, d):
    if d == 0:
        return x
    row = lax.broadcasted_iota(jnp.int32, x.shape, 0)
    return jnp.where(row >= d, pltpu.roll(x, d, 0), 0.0)


def _shift_up(x, d):
    if d == 0:
        return x
    t = x.shape[0]
    row = lax.broadcasted_iota(jnp.int32, x.shape, 0)
    return jnp.where(row < t - d, pltpu.roll(x, t - d, 0), 0.0)


def _colsum(x):
    return jnp.sum(x, axis=0, keepdims=True)


def _col(t, idx_fn):
    return pl.BlockSpec((t, LANE), idx_fn)


def _conv_fwd(x, w_ref, taps):
    acc = None
    for j in range(taps):
        term = w_ref[pl.ds(j, 1), :] * _shift_down(x, taps - 1 - j)
        acc = term if acc is None else acc + term
    return acc


def _conv_bwd(x, dy, w_ref, dw_ref, taps):
    dx = None
    for j in range(taps):
        term = w_ref[pl.ds(j, 1), :] * _shift_up(dy, taps - 1 - j)
        dx = term if dx is None else dx + term
        dw_ref[pl.ds(j, 1), :] = _colsum(dy * _shift_down(x, taps - 1 - j))
    return dx


def _qkv_prep_fwd(proj, conv_w, *, name):
    t = proj.shape[0]
    scale = HEAD_DIM ** -0.5

    def body(x_ref, w_ref, o_ref):
        j = pl.program_id(0)
        c = _conv_fwd(x_ref[...], w_ref, QKV_TAPS)
        s = c * _sigmoid(c)
        r = lax.rsqrt(jnp.sum(s * s, axis=-1, keepdims=True) + EPS)
        f = jnp.where(j < 2 * HEADS, r, 1.0) * jnp.where(j < HEADS, scale, 1.0)
        o_ref[0] = s * f

    return _pcall(body, name=name, out_shape=jax.ShapeDtypeStruct((3 * HEADS, t, LANE), F32), grid=(3 * HEADS,),
                  in_specs=[_col(t, lambda j: (0, j)), pl.BlockSpec((QKV_TAPS, LANE), lambda j: (0, j))],
                  out_specs=pl.BlockSpec((1, t, LANE), lambda j: (j, 0, 0)), semantics=("parallel",),
                  vmem_limit=VMEM_LIMIT)(proj, conv_w)


def _qkv_prep_bwd(proj, conv_w, dqkv, *, name):
    t = proj.shape[0]
    scale = HEAD_DIM ** -0.5

    def body(x_ref, w_ref, d_ref, dx_ref, dw_ref):
        j = pl.program_id(0)
        xv = x_ref[...]
        c = _conv_fwd(xv, w_ref, QKV_TAPS)
        sig = _sigmoid(c)
        s = c * sig
        r = lax.rsqrt(jnp.sum(s * s, axis=-1, keepdims=True) + EPS)
        n0 = s * r
        dv = d_ref[0]
        dn0 = dv * jnp.where(j < HEADS, scale, 1.0)
        ds_norm = r * (dn0 - n0 * jnp.sum(dn0 * n0, axis=-1, keepdims=True))
        ds = jnp.where(j < 2 * HEADS, ds_norm, dv)
        dc = ds * sig * (1.0 + c * (1.0 - sig))
        dx_ref[...] = _conv_bwd(xv, dc, w_ref, dw_ref, QKV_TAPS).astype(BF16)

    return _pcall(body, name=name,
                  out_shape=(jax.ShapeDtypeStruct((t, 3 * A_DIM), BF16), jax.ShapeDtypeStruct((QKV_TAPS, 3 * A_DIM), F32)),
                  grid=(3 * HEADS,),
                  in_specs=[_col(t, lambda j: (0, j)), pl.BlockSpec((QKV_TAPS, LANE), lambda j: (0, j)),
                            pl.BlockSpec((1, t, LANE), lambda j: (j, 0, 0))],
                  out_specs=(_col(t, lambda j: (0, j)), pl.BlockSpec((QKV_TAPS, LANE), lambda j: (0, j))),
                  semantics=("parallel",), vmem_limit=VMEM_LIMIT)(proj, conv_w, dqkv)


def _lane_pick(x, lane_idx, lane):
    return jnp.broadcast_to(jnp.sum(jnp.where(lane == lane_idx, x, 0.0), axis=-1, keepdims=True), x.shape)


def _gates_fwd(proj, alog, dtb, *, name):
    t = proj.shape[0]

    def body(x_ref, alog_ref, dtb_ref, g_ref, b_ref):
        xv = x_ref[...]
        lane = lax.broadcasted_iota(jnp.int32, xv.shape, 1)
        gall = -jnp.exp(alog_ref[...]) * _softplus(xv + dtb_ref[...])
        ball = _sigmoid(xv)
        for h in range(HEADS):
            g_ref[h] = _lane_pick(gall, h, lane)
            b_ref[h] = _lane_pick(ball, HEADS + h, lane)

    out = jax.ShapeDtypeStruct((HEADS, t, LANE), F32)
    whole = pl.BlockSpec((HEADS, t, LANE), lambda i: (0, 0, 0))
    return _pcall(body, name=name, out_shape=(out, out), grid=(1,),
                  in_specs=[_col(t, lambda i: (0, AB_COL // LANE)), _vec(LANE), _vec(LANE)], out_specs=(whole, whole),
                  semantics=("arbitrary",), vmem_limit=VMEM_LIMIT)(proj, alog, dtb)


def _gates_bwd(proj, alog, dtb, dg, dbeta, *, name):
    t = proj.shape[0]

    def body(x_ref, alog_ref, dtb_ref, dg_ref, db_ref, dab_ref, dalog_ref, ddtb_ref):
        xv = x_ref[...]
        lane = lax.broadcasted_iota(jnp.int32, xv.shape, 1)
        lane1 = lax.broadcasted_iota(jnp.int32, (1, LANE), 1)
        z = xv + dtb_ref[...]
        nea = -jnp.exp(alog_ref[...])
        da_f = nea * _sigmoid(z)
        g_f = nea * _softplus(z)
        ball = _sigmoid(xv)
        db_f = ball * (1.0 - ball)
        dab = jnp.zeros_like(xv)
        dalog = jnp.zeros((1, LANE), F32)
        for h in range(HEADS):
            dgh = dg_ref[h]
            dab = dab + jnp.where(lane == h, dgh * da_f, 0.0) + jnp.where(lane == HEADS + h, db_ref[h] * db_f, 0.0)
            dalog = dalog + jnp.where(lane1 == h, _colsum(dgh * g_f), 0.0)
        dab_ref[...] = dab.astype(BF16)
        dalog_ref[...] = dalog
        ddtb_ref[...] = jnp.where(lane1 < HEADS, _colsum(dab), 0.0)

    whole = pl.BlockSpec((HEADS, t, LANE), lambda i: (0, 0, 0))
    vec = jax.ShapeDtypeStruct((1, LANE), F32)
    return _pcall(body, name=name, out_shape=(jax.ShapeDtypeStruct((t, LANE), BF16), vec, vec), grid=(1,),
                  in_specs=[_col(t, lambda i: (0, AB_COL // LANE)), _vec(LANE), _vec(LANE), whole, whole],
                  out_specs=(_col(t, lambda i: (0, 0)), _vec(LANE), _vec(LANE)), semantics=("arbitrary",),
                  vmem_limit=VMEM_LIMIT)(proj, alog, dtb, dg, dbeta)


Z_COL = 3 * A_DIM // LANE


def _apost_fwd(o, proj, gn, *, name):
    t = proj.shape[0]

    def body(o_ref, z_ref, gn_ref, y_ref):
        ov = o_ref[0]
        z = z_ref[...]
        r = lax.rsqrt(jnp.mean(ov * ov, axis=-1, keepdims=True) + EPS)
        y_ref[...] = (ov * r * gn_ref[...] * (z * _sigmoid(z))).astype(BF16)

    return _pcall(body, name=name, out_shape=jax.ShapeDtypeStruct((t, A_DIM), BF16), grid=(HEADS,),
                  in_specs=[pl.BlockSpec((1, t, LANE), lambda h: (h, 0, 0)), _col(t, lambda h: (0, Z_COL + h)),
                            pl.BlockSpec((1, LANE), lambda h: (0, 0))],
                  out_specs=_col(t, lambda h: (0, h)), semantics=("parallel",), vmem_limit=VMEM_LIMIT)(o, proj, gn)


def _apost_bwd(o, proj, gn, dmixed, *, name):
    t = proj.shape[0]

    def body(o_ref, z_ref, gn_ref, d_ref, do_ref, dz_ref, dgn_ref):
        ov = o_ref[0]
        z = z_ref[...]
        gnv = gn_ref[...]
        dv = d_ref[...]
        r = lax.rsqrt(jnp.mean(ov * ov, axis=-1, keepdims=True) + EPS)
        ohat = ov * r
        sig = _sigmoid(z)
        dy = dv * (z * sig)
        dz_ref[...] = (dv * ohat * gnv * sig * (1.0 + z * (1.0 - sig))).astype(BF16)
        dyo = dy * gnv
        do_ref[0] = r * (dyo - ohat * jnp.mean(dyo * ohat, axis=-1, keepdims=True))
        part = _colsum(dy * ohat)

        @pl.when(pl.program_id(0) == 0)
        def _():
            dgn_ref[...] = part

        @pl.when(pl.program_id(0) > 0)
        def _():
            dgn_ref[...] += part

    return _pcall(body, name=name,
                  out_shape=(jax.ShapeDtypeStruct((HEADS, t, LANE), F32), jax.ShapeDtypeStruct((t, A_DIM), BF16),
                             jax.ShapeDtypeStruct((1, LANE), F32)),
                  grid=(HEADS,),
                  in_specs=[pl.BlockSpec((1, t, LANE), lambda h: (h, 0, 0)), _col(t, lambda h: (0, Z_COL + h)),
                            pl.BlockSpec((1, LANE), lambda h: (0, 0)), _col(t, lambda h: (0, h))],
                  out_specs=(pl.BlockSpec((1, t, LANE), lambda h: (h, 0, 0)), _col(t, lambda h: (0, h)),
                             pl.BlockSpec((1, LANE), lambda h: (0, 0))),
                  semantics=("arbitrary",), vmem_limit=VMEM_LIMIT)(o, proj, gn, dmixed)


POOL_COL = (AB_COL + LANE) // LANE
CB_COL = POOL_COL + POOL_DIM // LANE
CC_COL = CB_COL + CONV_DIM // LANE
CH_COL = CC_COL + CONV_DIM // LANE
MAX_WIN_LOG2 = 4


def _window_sums(x, shift):
    sums = []
    cur = x
    for k in range(MAX_WIN_LOG2):
        cur = cur + shift(cur, 1 << k)
        sums.append(cur)
    return sums


def _pick_window(sums, win):
    out = sums[-1]
    for k in range(MAX_WIN_LOG2 - 2, -1, -1):
        out = jnp.where(win == float(2 << k), sums[k], out)
    return out


def _pool_counts(shape, win):
    row = lax.broadcasted_iota(jnp.int32, shape, 0).astype(F32)
    return jnp.minimum(row + 1.0, win)


def _pool_fwd(proj, win, wbd, scale, *, name):
    t = proj.shape[0]

    def body(x_ref, win_ref, w_ref, s_ref, y_ref):
        xv = x_ref[...]
        winv = win_ref[...]
        pooled = _pick_window(_window_sums(xv, _shift_down), winv) / _pool_counts(xv.shape, winv) - xv
        y_ref[...] = (_dot(pooled, w_ref[0], NN) * s_ref[...]).astype(BF16)

    nb = POOL_DIM // LANE
    vec = pl.BlockSpec((1, LANE), lambda b: (0, b))
    return _pcall(body, name=name, out_shape=jax.ShapeDtypeStruct((t, POOL_DIM), BF16), grid=(nb,),
                  in_specs=[_col(t, lambda b: (0, POOL_COL + b)), vec, pl.BlockSpec((1, LANE, LANE), lambda b: (b, 0, 0)), vec],
                  out_specs=_col(t, lambda b: (0, b)), semantics=("parallel",), vmem_limit=VMEM_LIMIT)(proj, win, wbd, scale)


def _pool_bwd(proj, win, wbd, scale, dmixed, *, name):
    t = proj.shape[0]

    def body(x_ref, win_ref, w_ref, s_ref, d_ref, dx_ref, dw_ref, ds_ref):
        xv = x_ref[...]
        winv = win_ref[...]
        cnt = _pool_counts(xv.shape, winv)
        pooled = _pick_window(_window_sums(xv, _shift_down), winv) / cnt - xv
        dv = d_ref[...]
        ds_ref[...] = _colsum(dv * _dot(pooled, w_ref[0], NN))
        dy0 = dv * s_ref[...]
        dw_ref[0] = _dot(pooled, dy0, TN)
        dpooled = _dot(dy0, w_ref[0], NT)
        dmean = dpooled / cnt
        dx_ref[...] = (_pick_window(_window_sums(dmean, _shift_up), winv) - dpooled).astype(BF16)

    nb = POOL_DIM // LANE
    vec = pl.BlockSpec((1, LANE), lambda b: (0, b))
    mat = pl.BlockSpec((1, LANE, LANE), lambda b: (b, 0, 0))
    first = A_DIM // LANE
    return _pcall(body, name=name,
                  out_shape=(jax.ShapeDtypeStruct((t, POOL_DIM), BF16), jax.ShapeDtypeStruct((nb, LANE, LANE), F32),
                             jax.ShapeDtypeStruct((1, POOL_DIM), F32)),
                  grid=(nb,),
                  in_specs=[_col(t, lambda b: (0, POOL_COL + b)), vec, mat, vec, _col(t, lambda b: (0, first + b))],
                  out_specs=(_col(t, lambda b: (0, b)), mat, vec), semantics=("parallel",),
                  vmem_limit=VMEM_LIMIT)(proj, win, wbd, scale, dmixed)


def _sconv_fwd(proj, w, *, name):
    t = proj.shape[0]

    def body(cb_ref, cc_ref, ch_ref, w_ref, y_ref):
        y_ref[...] = (cb_ref[...] * _conv_fwd(cc_ref[...] * ch_ref[...], w_ref, CONV_TAPS)).astype(BF16)

    nb = CONV_DIM // LANE
    return _pcall(body, name=name, out_shape=jax.ShapeDtypeStruct((t, CONV_DIM), BF16), grid=(nb,),
                  in_specs=[_col(t, lambda b: (0, CB_COL + b)), _col(t, lambda b: (0, CC_COL + b)),
                            _col(t, lambda b: (0, CH_COL + b)), pl.BlockSpec((CONV_TAPS, LANE), lambda b: (0, b))],
                  out_specs=_col(t, lambda b: (0, b)), semantics=("parallel",), vmem_limit=VMEM_LIMIT)(proj, proj, proj, w)


def _sconv_bwd(proj, w, dmixed, *, name):
    t = proj.shape[0]

    def body(cb_ref, cc_ref, ch_ref, w_ref, d_ref, dcb_ref, dcc_ref, dch_ref, dw_ref):
        cc = cc_ref[...]
        ch = ch_ref[...]
        u = cc * ch
        dv = d_ref[...]
        dcb_ref[...] = (dv * _conv_fwd(u, w_ref, CONV_TAPS)).astype(BF16)
        du = _conv_bwd(u, dv * cb_ref[...], w_ref, dw_ref, CONV_TAPS)
        dcc_ref[...] = (du * ch).astype(BF16)
        dch_ref[...] = (du * cc).astype(BF16)

    nb = CONV_DIM // LANE
    first = (A_DIM + POOL_DIM) // LANE
    act = jax.ShapeDtypeStruct((t, CONV_DIM), BF16)
    wspec = pl.BlockSpec((CONV_TAPS, LANE), lambda b: (0, b))
    ospec = _col(t, lambda b: (0, b))
    return _pcall(body, name=name, out_shape=(act, act, act, jax.ShapeDtypeStruct((CONV_TAPS, CONV_DIM), F32)), grid=(nb,),
                  in_specs=[_col(t, lambda b: (0, CB_COL + b)), _col(t, lambda b: (0, CC_COL + b)),
                            _col(t, lambda b: (0, CH_COL + b)), wspec, _col(t, lambda b: (0, first + b))],
                  out_specs=(ospec, ospec, ospec, wspec), semantics=("parallel",),
                  vmem_limit=VMEM_LIMIT)(proj, proj, proj, w, dmixed)


def _chunk_masks():
    r = lax.broadcasted_iota(jnp.int32, (CHUNK, CHUNK), 0)
    c = lax.broadcasted_iota(jnp.int32, (CHUNK, CHUNK), 1)
    return r >= c, r > c, jnp.where(r == c, 1.0, 0.0).astype(F32)


def _split(a):
    hi = a.astype(BF16)
    return hi, (a - hi.astype(F32)).astype(BF16)


def _dot_split(a, b, dims):
    (ah, al), (bh, bl) = a, b
    return _dot(ah, bh, dims) + _dot(ah, bl, dims) + _dot(al, bh, dims)


def _tri_inv(lows, eye):
    xs = [eye - low for low in lows]
    ps = [_split(low) for low in lows]
    ps = [_split(_dot_split(p, p, NN)) for p in ps]
    for i in range(5):
        xs = [x + _dot_split(_split(x), p, NN) for x, p in zip(xs, ps)]
        if i < 4:
            ps = [_split(_dot_split(p, p, NN)) for p in ps]
    return xs


def _prefix_sum_rows(x):
    for k in range(6):
        x = x + _shift_down(x, 1 << k)
    return x


def _suffix_sum_rows(x):
    for k in range(6):
        x = x + _shift_up(x, 1 << k)
    return x


def _chunk_decay(g, incl):
    gcb = _prefix_sum_rows(g)
    gtot = _colsum(g)
    col = gcb[:, :CHUNK]
    row = gcb.T[:CHUNK, :]
    decay = jnp.exp(jnp.where(incl, col - row, -1e30))
    return gcb, gtot, decay


CHUNKS_PER_STEP = 4


def _chunk_rows(j):
    return pl.ds(j * CHUNK, CHUNK)


def _deltanet_prep(qkv, g, beta, *, name):
    t = qkv.shape[1]
    n_chunks = t // CHUNK
    per = CHUNKS_PER_STEP
    probs = [(j, h) for j in range(per) for h in range(HEADS)]

    def body(qkv_ref, g_ref, b_ref, u_ref, w_ref, qg_ref, kg_ref, attn_ref, tm_ref):
        incl, strict, eye = _chunk_masks()
        q = [qkv_ref[h, _chunk_rows(j), :] for j, h in probs]
        k = [qkv_ref[HEADS + h, _chunk_rows(j), :] for j, h in probs]
        v = [qkv_ref[2 * HEADS + h, _chunk_rows(j), :] for j, h in probs]
        bv = [b_ref[h, _chunk_rows(j), :] for j, h in probs]
        dec = [_chunk_decay(g_ref[h, _chunk_rows(j), :], incl) for j, h in probs]
        kb = [a * b for a, b in zip(k, bv)]
        low = [jnp.where(strict, _dot(a, b, NT) * d[2], 0.0) for a, b, d in zip(kb, k, dec)]
        tm = _tri_inv(low, eye)
        egc = [jnp.exp(d[0]) for d in dec]
        u = [_dot(m, a * b, NN) for m, a, b in zip(tm, v, bv)]
        w = [_dot(m, a * e, NN) for m, a, e in zip(tm, kb, egc)]
        attn = [_dot(a, b, NT) * d[2] for a, b, d in zip(q, k, dec)]
        for i, (j, h) in enumerate(probs):
            rows = _chunk_rows(j)
            u_ref[h, rows, :] = u[i]
            w_ref[h, rows, :] = w[i].astype(BF16)
            qg_ref[h, rows, :] = (q[i] * egc[i]).astype(BF16)
            kg_ref[h, rows, :] = (k[i] * jnp.exp(dec[i][1] - dec[i][0])).astype(BF16)
            attn_ref[j, h] = attn[i].astype(BF16)
            tm_ref[j, h] = tm[i]

    act = lambda heads: pl.BlockSpec((heads, per * CHUNK, LANE), lambda n: (0, n, 0))
    mat = pl.BlockSpec((per, HEADS, CHUNK, CHUNK), lambda n: (n, 0, 0, 0))
    return _pcall(
        body, name=name,
        out_shape=(jax.ShapeDtypeStruct((HEADS, t, LANE), F32),) + (jax.ShapeDtypeStruct((HEADS, t, LANE), BF16),) * 3
        + (jax.ShapeDtypeStruct((n_chunks, HEADS, CHUNK, CHUNK), BF16), jax.ShapeDtypeStruct((n_chunks, HEADS, CHUNK, CHUNK), F32)),
        grid=(n_chunks // per,), in_specs=[act(3 * HEADS), act(HEADS), act(HEADS)],
        out_specs=(act(HEADS),) * 4 + (mat, mat), semantics=("parallel",), vmem_limit=VMEM_LIMIT)(qkv, g, beta)


SCAN_CHUNKS_PER_STEP = 8


def _deltanet_scan(u, w, qg, kg, attn, g, *, name, after=None):
    t = u.shape[1]
    n_chunks = t // CHUNK
    per = SCAN_CHUNKS_PER_STEP

    def body(u_ref, w_ref, qg_ref, kg_ref, attn_ref, g_ref, o_ref, vn_ref, st_ref, s_ref):
        @pl.when(pl.program_id(0) == 0)
        def _():
            s_ref[...] = jnp.zeros_like(s_ref)

        for j in range(per):
            rows = _chunk_rows(j)
            s = [s_ref[h] for h in range(HEADS)]
            vn = [u_ref[h, rows, :] - _dot(w_ref[h, rows, :], s[h], NN) for h in range(HEADS)]
            o = [_dot(qg_ref[h, rows, :], s[h], NN) + _dot(attn_ref[j, h], vn[h], NN) for h in range(HEADS)]
            eg = [jnp.exp(_colsum(g_ref[h, rows, :])) for h in range(HEADS)]
            for h in range(HEADS):
                st_ref[j, h] = s[h]
                s_ref[h] = s[h] * eg[h] + _dot(kg_ref[h, rows, :], vn[h], TN)
                o_ref[h, rows, :] = o[h]
                vn_ref[h, rows, :] = vn[h]

    act = pl.BlockSpec((HEADS, per * CHUNK, LANE), lambda n: (0, n, 0))
    out = jax.ShapeDtypeStruct((HEADS, t, LANE), F32)
    return _pcall(
        body, name=name, out_shape=(out, out, jax.ShapeDtypeStruct((n_chunks, HEADS, LANE, LANE), F32)), grid=(n_chunks // per,),
        in_specs=[act] * 4 + [pl.BlockSpec((per, HEADS, CHUNK, CHUNK), lambda n: (n, 0, 0, 0)), act],
        out_specs=(act, act, pl.BlockSpec((per, HEADS, LANE, LANE), lambda n: (n, 0, 0, 0))),
        scratch_shapes=[pltpu.VMEM((HEADS, LANE, LANE), F32)], semantics=("arbitrary",), after=after)(u, w, qg, kg, attn, g)


def _deltanet_bscan(w, qg, kg, attn, g, do, *, name):
    t = w.shape[1]
    n_chunks = t // CHUNK
    per = SCAN_CHUNKS_PER_STEP
    steps = n_chunks // per

    def body(w_ref, qg_ref, kg_ref, attn_ref, g_ref, do_ref, dvn_ref, dsn_ref, ds_ref):
        @pl.when(pl.program_id(0) == 0)
        def _():
            ds_ref[...] = jnp.zeros_like(ds_ref)

        for j in reversed(range(per)):
            rows = _chunk_rows(j)
            dsn = [ds_ref[h] for h in range(HEADS)]
            dov = [do_ref[h, rows, :] for h in range(HEADS)]
            dvn = [_dot(attn_ref[j, h], dov[h], TN) + _dot(kg_ref[h, rows, :], dsn[h], NN) for h in range(HEADS)]
            eg = [jnp.exp(_colsum(g_ref[h, rows, :])) for h in range(HEADS)]
            for h in range(HEADS):
                dsn_ref[j, h] = dsn[h]
                ds_ref[h] = _dot(qg_ref[h, rows, :], dov[h], TN) + eg[h] * dsn[h] - _dot(w_ref[h, rows, :], dvn[h], TN)
                dvn_ref[h, rows, :] = dvn[h]

    act = pl.BlockSpec((HEADS, per * CHUNK, LANE), lambda n: (0, steps - 1 - n, 0))
    return _pcall(
        body, name=name,
        out_shape=(jax.ShapeDtypeStruct((HEADS, t, LANE), F32), jax.ShapeDtypeStruct((n_chunks, HEADS, LANE, LANE), F32)),
        grid=(steps,),
        in_specs=[act] * 3 + [pl.BlockSpec((per, HEADS, CHUNK, CHUNK), lambda n: (steps - 1 - n, 0, 0, 0)), act, act],
        out_specs=(act, pl.BlockSpec((per, HEADS, LANE, LANE), lambda n: (steps - 1 - n, 0, 0, 0))),
        scratch_shapes=[pltpu.VMEM((HEADS, LANE, LANE), F32)], semantics=("arbitrary",))(w, qg, kg, attn, g, do)


def _sum_all(x):
    return jnp.sum(jnp.sum(x, axis=1, keepdims=True), axis=0, keepdims=True)


def _rowsum(x):
    return jnp.sum(x, axis=1, keepdims=True)


def _deltanet_post(qkv, g, beta, tmats, states, dstates, do, dvn, vn, *, name):
    t = qkv.shape[1]
    n_chunks = t // CHUNK
    per = CHUNKS_PER_STEP
    probs = [(j, h) for j in range(per) for h in range(HEADS)]

    def body(qkv_ref, g_ref, b_ref, tm_ref, st_ref, dsn_ref, do_ref, dvn_ref, vn_ref, dqkv_ref, dg_ref, db_ref):
        incl, strict, _ = _chunk_masks()
        ones = jnp.ones((CHUNK, LANE), BF16)
        last_row = lax.broadcasted_iota(jnp.int32, (CHUNK, LANE), 0) == CHUNK - 1
        z = lambda f, *cols: [f(*a) for a in zip(*cols)]
        q = [qkv_ref[h, _chunk_rows(j), :] for j, h in probs]
        k = [qkv_ref[HEADS + h, _chunk_rows(j), :] for j, h in probs]
        v = [qkv_ref[2 * HEADS + h, _chunk_rows(j), :] for j, h in probs]
        bv = [b_ref[h, _chunk_rows(j), :] for j, h in probs]
        dov = [do_ref[h, _chunk_rows(j), :] for j, h in probs]
        dvn_ = [dvn_ref[h, _chunk_rows(j), :] for j, h in probs]
        vn_ = [vn_ref[h, _chunk_rows(j), :] for j, h in probs]
        tm = [tm_ref[j, h] for j, h in probs]
        s = [st_ref[j, h] for j, h in probs]
        dsn = [dsn_ref[j, h] for j, h in probs]
        dec = [_chunk_decay(g_ref[h, _chunk_rows(j), :], incl) for j, h in probs]
        decay = [d[2] for d in dec]
        egc = [jnp.exp(d[0]) for d in dec]
        ekg = [jnp.exp(d[1] - d[0]) for d in dec]
        kb = z(lambda a, b: a * b, k, bv)
        vb = z(lambda a, b: a * b, v, bv)
        kbg = z(lambda a, b: a * b, kb, egc)
        qg = z(lambda a, b: a * b, q, egc)
        kg = z(lambda a, b: a * b, k, ekg)
        kk = z(lambda a, b: _dot(a, b, NT), kb, k)
        qk = z(lambda a, b: _dot(a, b, NT), q, k)
        dattn = z(lambda a, b: jnp.where(incl, _dot(a, b, NT), 0.0), dov, vn_)
        dqg = z(lambda a, b: _dot(a, b, NT), dov, s)
        dkg = z(lambda a, b: _dot(a, b, NT), vn_, dsn)
        dglast = z(lambda a, b, c, d, e: _sum_all(a * b) * jnp.exp(e[1]) + _sum_all(c * d), s, dsn, dkg, kg, dec)
        dw = z(lambda a, b: -_dot(a, b, NT), dvn_, s)
        dtm = z(lambda a, b, c, d: _dot(a, b, NT) + _dot(c, d, NT), dvn_, vb, dw, kbg)
        dvb = z(lambda a, b: _dot(a, b, TN), tm, dvn_)
        dkbg = z(lambda a, b: _dot(a, b, TN), tm, dw)
        dlow = z(lambda a, b: jnp.where(strict, -_dot(_dot(a, b, TN), a, NT), 0.0), tm, dtm)
        dkk = z(lambda a, b: a * b, dlow, decay)
        dqk = z(lambda a, b: a * b, dattn, decay)
        dkb = z(lambda a, b, c, d: _dot(a, b, NN) + c * d, dkk, k, dkbg, egc)
        dk = z(lambda a, b, c, d, e, f, g_, h_: _dot(a, b, TN) + _dot(c, d, TN) + e * f + g_ * h_, dkk, kb, dqk, q, dkg, ekg, dkb, bv)
        dq = z(lambda a, b, c, d: _dot(a, b, NN) + c * d, dqk, k, dqg, egc)
        m = z(lambda a, b, c, d, e: (a * b + c * d) * e, dlow, kk, dattn, qk, decay)
        mcol = [_dot(mh, ones, TN) + _dot(ml, ones, TN) for mh, ml in (_split(a) for a in m)]
        for i, (j, h) in enumerate(probs):
            rows = _chunk_rows(j)
            dqkv_ref[h, rows, :] = dq[i]
            dqkv_ref[HEADS + h, rows, :] = dk[i]
            dqkv_ref[2 * HEADS + h, rows, :] = dvb[i] * bv[i]
            db_ref[h, rows, :] = jnp.broadcast_to(_rowsum(dkb[i] * k[i] + dvb[i] * v[i]), (CHUNK, LANE))
            dgc = (_rowsum(dqg[i] * qg[i] + dkbg[i] * kbg[i] - dkg[i] * kg[i]) + _rowsum(m[i]) - mcol[i]
                   + jnp.where(last_row, dglast[i], 0.0))
            dg_ref[h, rows, :] = _suffix_sum_rows(dgc)

    act = lambda heads: pl.BlockSpec((heads, per * CHUNK, LANE), lambda n: (0, n, 0))
    mat = lambda d: pl.BlockSpec((per, HEADS, d, d), lambda n: (n, 0, 0, 0))
    out = jax.ShapeDtypeStruct((HEADS, t, LANE), F32)
    return _pcall(
        body, name=name, out_shape=(jax.ShapeDtypeStruct((3 * HEADS, t, LANE), F32), out, out), grid=(n_chunks // per,),
        in_specs=[act(3 * HEADS), act(HEADS), act(HEADS), mat(CHUNK), mat(LANE), mat(LANE), act(HEADS), act(HEADS), act(HEADS)],
        out_specs=(act(3 * HEADS), act(HEADS), act(HEADS)), semantics=("parallel",),
        vmem_limit=VMEM_LIMIT)(qkv, g, beta, tmats, states, dstates, do, dvn, vn)


ANY = pl.BlockSpec(memory_space=pl.ANY)
PEERS = N_DEV - 1


def _all_gather(arrays, *, name):
    n = len(arrays)

    def body(*refs):
        ins, outs = refs[:n], refs[n:2 * n]
        send_sems, recv_sems, local_sems = refs[2 * n:]
        x, y, c = lax.axis_index("x"), lax.axis_index("y"), lax.axis_index("c")
        me, sibling = (x, y, c), (x, y, 1 - c)
        chips = [(1 - x, y), (x, 1 - y), (1 - x, 1 - y)]

        def copy(a, k, block, to, src=None):
            dst = outs[a].at[4 * block[0] + 2 * block[1] + block[2]]
            return pltpu.make_async_remote_copy(src_ref=dst if src is None else src, dst_ref=dst, send_sem=send_sems.at[a * PEERS + k],
                                                recv_sem=recv_sems.at[a * PEERS + k], device_id=to, device_id_type=MESH)

        local = [pltpu.make_async_copy(ins[a], outs[a].at[4 * x + 2 * y + c], local_sems.at[a]) for a in range(n)]
        for cp in local:
            cp.start()
        first = []
        for a in range(n):
            first += [copy(a, 1 + j, me, (*chip, c), src=ins[a]) for j, chip in enumerate(chips)]
            first.append(copy(a, 0, me, sibling, src=ins[a]))
        for cp in first:
            cp.start()
        passed = []
        for a in range(n):
            for j, chip in enumerate(chips):
                copy(a, 1 + j, (*chip, c), me).wait_recv()
                fwd = copy(a, 4 + j, (*chip, c), sibling)
                fwd.start()
                passed.append(fwd)
        for a in range(n):
            copy(a, 0, sibling, me).wait_recv()
            for j, chip in enumerate(chips):
                copy(a, 4 + j, (*chip, 1 - c), me).wait_recv()
        for cp in first + passed:
            cp.wait_send()
        for cp in local:
            cp.wait()

    return _pcall(body, name=name, out_shape=tuple(jax.ShapeDtypeStruct((N_DEV,) + a.shape, a.dtype) for a in arrays),
                  in_specs=[ANY] * n, out_specs=(ANY,) * n,
                  scratch_shapes=[pltpu.SemaphoreType.DMA((n * PEERS,)), pltpu.SemaphoreType.DMA((n * PEERS,)),
                                  pltpu.SemaphoreType.DMA((n,))])(*arrays)


CHIPS = 4


def _pair_exchange(arrays, *, name):
    n = len(arrays)

    def body(*refs):
        ins, outs = refs[:n], refs[n:2 * n]
        send_sems, recv_sems = refs[2 * n:]
        x, y, c = lax.axis_index("x"), lax.axis_index("y"), lax.axis_index("c")
        copies = []
        for a in range(n):
            for q in range(CHIPS):
                cp = pltpu.make_async_remote_copy(src_ref=ins[a].at[2 * q + 1 - c], dst_ref=outs[a].at[q],
                                                  send_sem=send_sems.at[a * CHIPS + q], recv_sem=recv_sems.at[a * CHIPS + q],
                                                  device_id=(x, y, 1 - c), device_id_type=MESH)
                cp.start()
                copies.append(cp)
        for cp in copies:
            cp.wait()

    return _pcall(body, name=name, out_shape=tuple(jax.ShapeDtypeStruct((CHIPS,) + a.shape[1:], a.dtype) for a in arrays),
                  in_specs=[ANY] * n, out_specs=(ANY,) * n,
                  scratch_shapes=[pltpu.SemaphoreType.DMA((n * CHIPS,)), pltpu.SemaphoreType.DMA((n * CHIPS,))])(*arrays)


def _pair_add(blocks, theirs, *, name):
    _, r, c_ = blocks.shape
    tr = _tile(r, 512, 16)

    def body(mine_ref, theirs_ref, o_ref):
        core = lax.axis_index("c")
        own = jnp.where(core == 0, mine_ref[0, 0].astype(F32), mine_ref[0, 1].astype(F32))
        o_ref[0] = (own + theirs_ref[0].astype(F32)).astype(o_ref.dtype)

    spec = pl.BlockSpec((1, tr, c_), lambda q, i: (q, i, 0))
    return _pcall(body, name=name, out_shape=jax.ShapeDtypeStruct(theirs.shape, theirs.dtype), grid=(CHIPS, r // tr),
                  in_specs=[pl.BlockSpec((1, 2, tr, c_), lambda q, i: (q, 0, i, 0)), spec], out_specs=spec,
                  semantics=("parallel", "parallel"), vmem_limit=VMEM_LIMIT)(blocks.reshape(CHIPS, 2, r, c_), theirs)


HBM = pl.BlockSpec(memory_space=pltpu.HBM)
SEM = pl.BlockSpec(memory_space=pltpu.SEMAPHORE)
EFFECT = pltpu.SideEffectType.DATAFLOW_SIDE_EFFECTING


GATHER, CHIP_GATHER, CHIP_SCATTER = "gather", "chip_gather", "chip_scatter"
PEERS_OF = {GATHER: N_DEV - 1, CHIP_GATHER: CHIPS - 1, CHIP_SCATTER: CHIPS - 1}


def _direct_copies(srcs, lands, send_sems, recv_sems, local_sems, kind):
    x, y, c = lax.axis_index("x"), lax.axis_index("y"), lax.axis_index("c")
    peers = PEERS_OF[kind]
    mine = 2 * x + y if kind == CHIP_SCATTER else 4 * x + 2 * y + c
    copies = []
    for a, (src, land) in enumerate(zip(srcs, lands)):
        copies.append(pltpu.make_async_copy(src.at[mine] if kind == CHIP_SCATTER else src, land.at[mine], local_sems.at[a]))
        for k in range(1, peers + 1):
            bits = k if kind == GATHER else 2 * k
            px = 1 - x if bits & 4 else x
            py = 1 - y if bits & 2 else y
            pc = 1 - c if bits & 1 else c
            copies.append(pltpu.make_async_remote_copy(
                src_ref=src.at[2 * px + py] if kind == CHIP_SCATTER else src, dst_ref=land.at[mine],
                send_sem=send_sems.at[a * peers + k - 1], recv_sem=recv_sems.at[a * peers + k - 1],
                device_id=(px, py, pc), device_id_type=MESH))
    return copies


def _pair_swap(arrays, *, name):
    n = len(arrays)

    def body(*refs):
        mine, zones = refs[:n], refs[n:2 * n]
        send_sems, recv_sems = refs[2 * n:]
        x, y, c = lax.axis_index("x"), lax.axis_index("y"), lax.axis_index("c")
        copies = []
        for a in range(n):
            for q in range(CHIPS):
                copies.append(pltpu.make_async_remote_copy(
                    src_ref=mine[a].at[2 * q + c], dst_ref=zones[a].at[2 * q + c], send_sem=send_sems.at[a * CHIPS + q],
                    recv_sem=recv_sems.at[a * CHIPS + q], device_id=(x, y, 1 - c), device_id_type=MESH))
        for cp in copies:
            cp.start()
        for cp in copies:
            cp.wait()

    return _pcall(body, name=name, out_shape=tuple(jax.ShapeDtypeStruct(a.shape, a.dtype) for a in arrays),
                  in_specs=[ANY] * n, out_specs=(ANY,) * n, input_output_aliases={i: i for i in range(n)},
                  scratch_shapes=[pltpu.SemaphoreType.DMA((n * CHIPS,)), pltpu.SemaphoreType.DMA((n * CHIPS,))])(*arrays)


def _exchange_start(groups, kind, *, name, after=None):
    srcs = [s for group in groups for s in group]
    n = len(srcs)
    sizes = [len(group) for group in groups]
    starts = [sum(sizes[:g]) for g in range(len(groups))]
    land_shapes = [s.shape if kind == CHIP_SCATTER else (N_DEV,) + s.shape for s in srcs]
    peers = PEERS_OF[kind]
    extra = [] if after is None else [after]

    def body(*refs):
        srcs_, lands = refs[:n], refs[n:2 * n]
        token = refs[-1]
        sem_refs = refs[2 * n + len(extra):]
        for g, (at, size) in enumerate(zip(starts, sizes)):
            send_sems, recv_sems, local_sems = sem_refs[3 * g:3 * g + 3]
            for cp in _direct_copies(srcs_[at:at + size], lands[at:at + size], send_sems, recv_sems, local_sems, kind):
                cp.start()
        token[...] = jnp.zeros_like(token)

    sems = tuple(t for size in sizes for t in (pltpu.SemaphoreType.DMA((size * peers,)), pltpu.SemaphoreType.DMA((size * peers,)),
                                               pltpu.SemaphoreType.DMA((size,))))
    thru = tuple(pltpu.HBM(s.shape, s.dtype) for s in srcs) + tuple(pltpu.HBM(shp, s.dtype) for shp, s in zip(land_shapes, srcs))
    ins = [pltpu.with_memory_space_constraint(s, pltpu.HBM) for s in srcs]
    ins += [pltpu.with_memory_space_constraint(lax.empty(shp, s.dtype), pltpu.HBM) for shp, s in zip(land_shapes, srcs)]
    out = pl.pallas_call(
        body, name=name, out_shape=sems + thru + (jax.ShapeDtypeStruct((SUBLANE, LANE), F32),),
        in_specs=[HBM] * (2 * n) + [ANY] * len(extra),
        out_specs=(SEM,) * len(sems) + (HBM,) * (2 * n) + (pl.BlockSpec(memory_space=pltpu.VMEM),),
        input_output_aliases={i: len(sems) + i for i in range(2 * n)},
        compiler_params=pltpu.CompilerParams(has_side_effects=EFFECT))(*ins, *extra)
    arrays = out[len(sems):-1]
    started = [tuple(out[3 * g:3 * g + 3]) + tuple(arrays[at:at + size]) + tuple(arrays[n + at:n + at + size])
               for g, (at, size) in enumerate(zip(starts, sizes))]
    return started, out[-1]


def _exchange_wait(started, after, kind, *, name):
    n = (len(started) - 3) // 2
    sems, arrays = started[:3], started[3:]

    def body(*refs):
        srcs_, lands = refs[:n], refs[n:2 * n]
        send_sems, recv_sems, local_sems = refs[2 * n:2 * n + 3]
        for cp in _direct_copies(srcs_, lands, send_sems, recv_sems, local_sems, kind):
            cp.wait()

    out = pl.pallas_call(
        body, name=name, out_shape=tuple(pltpu.HBM(a.shape, a.dtype) for a in arrays),
        in_specs=[HBM] * (2 * n) + [SEM] * 3 + [ANY], out_specs=(HBM,) * (2 * n),
        input_output_aliases={i: i for i in range(2 * n)},
        compiler_params=pltpu.CompilerParams(has_side_effects=EFFECT))(*arrays, *sems, after)
    return out[n:]


def _adamw_reduce(w, parts, m, v, *, name, after=None):
    layers, r, c = w.shape
    assert len(parts) == layers
    senders = parts[0].shape[0]
    tr = _tile(r, 512, 16)
    tiles = r // tr
    bc1 = 1.0 - ADAM_B1 ** ADAM_STEP
    bc2 = 1.0 - ADAM_B2 ** ADAM_STEP

    def body(w_ref, *rest):
        p_refs = rest[:layers]
        m_ref, v_ref, g_ref, d_ref, nm_ref, nv_ref = rest[layers:]

        def update(p_ref):
            g = p_ref[0, :, pl.ds(0, c)].astype(F32)
            for s in range(1, senders):
                g = g + p_ref[s, :, pl.ds(0, c)].astype(F32)
            nm = ADAM_B1 * m_ref[0] + (1.0 - ADAM_B1) * g
            nv = ADAM_B2 * v_ref[0] + (1.0 - ADAM_B2) * (g * g)
            g_ref[0] = g
            nm_ref[0] = nm
            nv_ref[0] = nv
            d_ref[0] = -ADAM_LR * ((nm / bc1) / (jnp.sqrt(nv / bc2) + ADAM_EPS) + ADAM_WD * w_ref[0])

        for layer in range(layers):
            pl.when(pl.program_id(0) == layer)(functools.partial(update, p_refs[layer]))

    def part_spec(layer, shape):
        rest = 0 if layer > 0 else tiles - 1
        return pl.BlockSpec((senders, tr, shape[2]), lambda l, i: (0, jnp.where(l == layer, i, rest), 0))

    spec = pl.BlockSpec((1, tr, c), lambda l, i: (l, i, 0))
    out = jax.ShapeDtypeStruct((layers, r, c), F32)
    return _pcall(body, name=name, out_shape=(out,) * 4, grid=(layers, tiles),
                  in_specs=[spec] + [part_spec(layer, p.shape) for layer, p in enumerate(parts)] + [spec, spec],
                  out_specs=(spec,) * 4, semantics=("arbitrary", "arbitrary"), vmem_limit=VMEM_LIMIT, after=after)(w, *parts, m, v)


def _pool_windows():
    return jnp.repeat(jnp.asarray(POOL_WINDOWS, F32), POOL_DIM // len(POOL_WINDOWS))[None, :]


def _block_diag_pairs(pool_w):
    z = jnp.zeros_like(pool_w[0])
    return jnp.stack([jnp.block([[pool_w[2 * b], z], [z, pool_w[2 * b + 1]]]) for b in range(2)])


def _pad_lanes(vec):
    return jnp.zeros((1, LANE), F32).at[0, :vec.shape[0]].set(vec)


FF_SHARD = D_FF // N_DEV
FF_BLOCK = 384


def _layer_fwd(x, p_i, wt, fetch):
    wt = {**wt, **fetch(0, x)}
    proj, h1 = _matmul(x, wt["w_in"], "nt", norm_g=wt["norm1_g"], name="mm_in")
    qkv = _qkv_prep_fwd(proj, wt["conv_qkv"], name="qkv_prep_fwd")
    g, beta = _gates_fwd(proj, wt["a_log"], wt["dt_bias"], name="gates_fwd")
    u, w, qg, kg, attn, tmats = _deltanet_prep(qkv, g, beta, name="deltanet_prep")
    wt.update(fetch(1, u))
    o, vn, states = _deltanet_scan(u, w, qg, kg, attn, g, name="deltanet_scan", after=wt.get("behind"))
    o_a = _apost_fwd(o, proj, wt["onorm_g"], name="apost_fwd")
    o_b = _pool_fwd(proj, wt["pool_win"], wt["pool_wbd"], wt["pool_scale"], name="pool_fwd")
    o_c = _sconv_fwd(proj, wt["sconv_w"], name="sconv_fwd")
    mixed = jnp.concatenate([o_a, o_b, o_c], axis=1)
    x1 = _matmul(mixed, wt["w_out"], "nn", res=x, name="mm_out")
    wt.update(fetch(2, x1))
    ff, gate, up, h2 = _swiglu_fwd(x1, wt["norm2_g"], wt["w_gate"], wt["w_up"], name="swiglu_fwd")
    wt.update(fetch(3, ff))
    x2 = _matmul(ff, wt["w_down"], "nn", res=x1, name="mm_down")
    wt.update(fetch(4, x2))
    x3, pgl, pp = _ple_fwd(x2, p_i, wt["ple_gate"], wt["ple_proj"], name="ple_fwd")
    saved = dict(x=x, h1=h1, proj=proj, qkv=qkv, g=g, beta=beta, o=o, states=states, tmats=tmats, mixed=mixed, x1=x1, h2=h2,
                 gate=gate, up=up, ff=ff, x2=x2, pgl=pgl, pp=pp, p=p_i, w=w, qg=qg, kg=kg, attn=attn, vn=vn, wt=wt)
    return x3, saved


def _col_blocks(g):
    a = g.shape[0]
    return jnp.transpose(g.reshape(a, N_DEV, -1), (1, 0, 2))


def _cols_joined(blocks):
    return jnp.transpose(blocks, (1, 0, 2)).reshape(blocks.shape[1], -1)


def _layer_bwd(dx3, sv, emit, after=None):
    gr, big = {}, {}
    wt = sv["wt"]
    rows = D_MODEL // N_DEV
    dpgl, dpp = _ple_bwd(dx3, sv["pgl"], sv["pp"], name="ple_bwd", after=after)
    big["ple_proj"] = _matmul(sv["p"], dpp, "tn", out_blocked=(N_DEV, rows), out_dtype=BF16, name="mm_dplep")
    big["ple_gate"] = _matmul(sv["x2"], dpgl, "tn", out_dtype=BF16, name="mm_dpleg").reshape(N_DEV, rows, D_MODEL)
    dx2 = _matmul(dpgl, wt["ple_gate"], "nt", res=dx3, name="mm_dx2")
    big["w_down"] = _matmul(sv["ff"], dx2, "tn", out_dtype=BF16, name="mm_ddown").reshape(N_DEV, FF_BLOCK, D_MODEL)
    dgate, dup = _swiglu_bwd(dx2, wt["w_down"], sv["gate"], sv["up"], name="swiglu_bwd", after=emit(0, big))
    big["w_gate"] = _matmul(dgate, sv["h2"], "tn", out_dtype=BF16, name="mm_dgate").reshape(N_DEV, FF_BLOCK, D_MODEL)
    big["w_up"] = _matmul(dup, sv["h2"], "tn", out_dtype=BF16, name="mm_dup").reshape(N_DEV, FF_BLOCK, D_MODEL)
    dx1, gr["norm2_g"] = _matmul_norm_bwd(dgate, wt["w_gate"], sv["x1"], wt["norm2_g"], dx2, more=(dup, wt["w_up"]), name="mm_dh2")
    big["w_out"] = _matmul(sv["mixed"], dx1, "tn", out_dtype=BF16, name="mm_dout").reshape(N_DEV, rows, D_MODEL)
    dmixed = _matmul(dx1, wt["w_out"], "nt", name="mm_dmixed", after=emit(1, big))
    proj = sv["proj"]
    dcb, dcc, dch, dsconv = _sconv_bwd(proj, wt["sconv_w"], dmixed, name="sconv_bwd")
    big["sconv_w"] = _col_blocks(dsconv)
    dhp, dwbd, gr["pool_scale"] = _pool_bwd(proj, wt["pool_win"], wt["pool_wbd"], wt["pool_scale"], dmixed, name="pool_bwd")
    half = LANE // 2
    gr["pool_w"] = jnp.stack([dwbd[0, :half, :half], dwbd[0, half:, half:], dwbd[1, :half, :half], dwbd[1, half:, half:]])
    do, dz, gr["onorm_g"] = _apost_bwd(sv["o"], proj, wt["onorm_g"], dmixed, name="apost_bwd")
    dvn, dstates = _deltanet_bscan(sv["w"], sv["qg"], sv["kg"], sv["attn"], sv["g"], do, name="deltanet_bscan")
    dqkv_h, dg, dbeta = _deltanet_post(sv["qkv"], sv["g"], sv["beta"], sv["tmats"], sv["states"], dstates, do, dvn, sv["vn"],
                                       name="deltanet_post")
    dab, dalog, ddtb = _gates_bwd(proj, wt["a_log"], wt["dt_bias"], dg, dbeta, name="gates_bwd")
    gr["a_log"], gr["dt_bias"] = dalog[0, :HEADS], ddtb[0, :HEADS]
    dqkv, dconv = _qkv_prep_bwd(proj, wt["conv_qkv"], dqkv_h, name="qkv_prep_bwd")
    big["conv_qkv"] = _col_blocks(dconv)
    dproj = jnp.concatenate([dqkv, dz, dab, dhp, dcb, dcc, dch], axis=1)
    dwin = _matmul(dproj, sv["h1"], "tn", out_dtype=BF16, name="mm_din")
    big["w_in"] = jnp.concatenate([dwin[:AB_COL + 2 * HEADS], dwin[AB_COL + LANE:]], axis=0).reshape(N_DEV, -1, D_MODEL)
    dx, gr["norm1_g"] = _matmul_norm_bwd(dproj, wt["w_in"], sv["x"], wt["norm1_g"], dx1, name="mm_dh1", after=emit(2, big))
    return dx, gr


FETCH_GROUPS = (("w_in", "conv_qkv", "sconv_w"), ("w_out",), ("w_gate", "w_up"), ("w_down",), ("ple_gate", "ple_proj"))
EMIT_GROUPS = (("ple_proj", "ple_gate", "w_down"), ("w_gate", "w_up", "w_out"), ("w_in", "conv_qkv", "sconv_w"))


def _small_weights(w, i):
    return dict(
        norm1_g=w["norm1_g"][i][None], norm2_g=w["norm2_g"][i][None], onorm_g=w["onorm_g"][i][None],
        a_log=_pad_lanes(w["a_log"][i]), dt_bias=_pad_lanes(w["dt_bias"][i]),
        pool_scale=w["pool_scale"][i][None], pool_win=_pool_windows(), pool_wbd=_block_diag_pairs(w["pool_w"][i]))


def _as_read(name, gathered):
    if name == "w_in":
        rows = gathered[:, :D_IN // N_DEV].reshape(-1, D_MODEL)
        return jnp.concatenate([rows[:AB_COL + 2 * HEADS], jnp.zeros((LANE - 2 * HEADS, D_MODEL), BF16),
                                rows[AB_COL + 2 * HEADS:]], axis=0)
    if name in ("conv_qkv", "sconv_w"):
        return _cols_joined(gathered)
    if name == "ple_proj":
        return gathered
    return gathered.reshape(-1, D_MODEL)


SHARDED = ("w_in", "w_gate", "w_up", "w_down", "w_out", "ple_gate", "ple_proj", "conv_qkv", "sconv_w")
SMALL = ("norm1_g", "a_log", "dt_bias", "onorm_g", "pool_w", "pool_scale", "norm2_g", "final_g")
SLAB_COLS = 1024


def _payload(name, shard):
    if name in ("conv_qkv", "sconv_w"):
        return shard
    out = shard.astype(BF16)
    if name in ("w_gate", "w_up", "w_down"):
        out = jnp.pad(out, ((0, FF_BLOCK - FF_SHARD), (0, 0)))
    if name == "w_in":
        out = jnp.pad(out, ((0, -out.shape[0] % (2 * SUBLANE)), (0, 0)))
    return out


TRANSPOSED = ("w_in", "w_gate", "w_up")


def _ff_rows(t):
    return jnp.transpose(t, (0, 2, 1))


def _slab_rows(shape):
    size = 1
    for s in shape:
        size *= s
    return SUBLANE * -(-size // (SUBLANE * SLAB_COLS))


def _pack_slab(parts, extra_row):
    rows = []
    for name in SMALL:
        flat = parts[name].reshape(-1)
        nrow = _slab_rows(parts[name].shape)
        rows.append(jnp.pad(flat, (0, nrow * SLAB_COLS - flat.shape[0])).reshape(nrow, SLAB_COLS))
    rows.append(jnp.pad(extra_row, ((0, SUBLANE - 1), (0, 0))))
    return jnp.concatenate(rows, axis=0)


def _unpack_slab(slab, shapes):
    out, row = {}, 0
    for name in SMALL:
        size = 1
        for s in shapes[name]:
            size *= s
        out[name] = slab[row:row + _slab_rows(shapes[name])].reshape(-1)[:size].reshape(shapes[name])
        row += _slab_rows(shapes[name])
    return out, row


def kernel(x, p, norm1_g, w_in, conv_qkv, a_log, dt_bias, onorm_g, pool_w, pool_scale, sconv_w, w_out, norm2_g, w_gate, w_up, w_down, ple_proj, ple_gate, final_g, loss_target, m_norm1_g, m_w_in, m_conv_qkv, m_a_log, m_dt_bias, m_onorm_g, m_pool_w, m_pool_scale, m_sconv_w, m_w_out, m_norm2_g, m_w_gate, m_w_up, m_w_down, m_ple_proj, m_ple_gate, m_final_g, v_norm1_g, v_w_in, v_conv_qkv, v_a_log, v_dt_bias, v_onorm_g, v_pool_w, v_pool_scale, v_sconv_w, v_w_out, v_norm2_g, v_w_gate, v_w_up, v_w_down, v_ple_proj, v_ple_gate, v_final_g):
    names = ["norm1_g", "w_in", "conv_qkv", "a_log", "dt_bias", "onorm_g", "pool_w", "pool_scale", "sconv_w", "w_out", "norm2_g",
             "w_gate", "w_up", "w_down", "ple_proj", "ple_gate", "final_g"]
    w = dict(zip(names, [norm1_g, w_in, conv_qkv, a_log, dt_bias, onorm_g, pool_w, pool_scale, sconv_w, w_out, norm2_g, w_gate, w_up,
                         w_down, ple_proj, ple_gate, final_g]))
    m = dict(zip(names, [m_norm1_g, m_w_in, m_conv_qkv, m_a_log, m_dt_bias, m_onorm_g, m_pool_w, m_pool_scale, m_sconv_w, m_w_out,
                         m_norm2_g, m_w_gate, m_w_up, m_w_down, m_ple_proj, m_ple_gate, m_final_g]))
    v = dict(zip(names, [v_norm1_g, v_w_in, v_conv_qkv, v_a_log, v_dt_bias, v_onorm_g, v_pool_w, v_pool_scale, v_sconv_w, v_w_out,
                         v_norm2_g, v_w_gate, v_w_up, v_w_down, v_ple_proj, v_ple_gate, v_final_g]))
    w.update({k: _ff_rows(w[k]) for k in TRANSPOSED})

    first, rest = FETCH_GROUPS[0], tuple(k for members in FETCH_GROUPS[1:] for k in members)
    gathered = dict(zip(first, _all_gather([_payload(k, w[k][0]) for k in first], name="all_gather_weights")))
    (flying0,), token = _exchange_start([[_payload(k, w[k][0]) for k in rest]], CHIP_GATHER, name="gather_start_0",
                                        after=gathered[first[0]])
    replicated = [_small_weights(w, i) for i in range(DEPTH)]
    replicated[0]["norm1_g"] = replicated[0]["norm1_g"] + token[0, 0]
    for group in (m, v):
        group.update({k: _ff_rows(group[k] + token[0, 0]) for k in TRANSPOSED})
    flying1 = []

    def fetch(i, group, after):
        if i == 0 and group == 1:
            landed = _exchange_wait(flying0, after, CHIP_GATHER, name="gather_wait_0")
            gathered.update(zip(rest, _pair_swap(landed, name="pair_swap")))
            started, token = _exchange_start([[_payload(k, w[k][1]) for k in SHARDED]], CHIP_GATHER, name="gather_start_1",
                                             after=gathered[rest[0]])
            flying1.extend(started)
            return {**{k: _as_read(k, gathered[k]) for k in FETCH_GROUPS[group]}, "behind": token}
        if i == 1 and group == 0:
            landed = _exchange_wait(flying1[0], after, CHIP_GATHER, name="gather_wait_1")
            gathered.update(zip(SHARDED, _pair_swap(landed, name="pair_swap")))
        return {k: _as_read(k, gathered[k]) for k in FETCH_GROUPS[group]}

    def reduce_scatter_start(members, blocks, tag):
        mine = [blocks[k] for k in members]
        theirs = _pair_exchange(mine, name="pair_exchange")
        sums = [_pair_add(a, b, name="pair_add") for a, b in zip(mine, theirs)]
        (started,), token = _exchange_start([sums], CHIP_SCATTER, name="exchange_start_" + tag)
        return started, token

    h, saved0 = _layer_fwd(x[0], p[0, 0], replicated[0], functools.partial(fetch, 0))
    h, saved1 = _layer_fwd(h, p[1, 0], replicated[1], functools.partial(fetch, 1))
    dx, dgf, loss_part = _loss_head(h, final_g[None], loss_target[0], name="loss_head")
    small, big1, flying0 = [None] * DEPTH, {}, []
    dx, small[1] = _layer_bwd(dx, saved1, lambda group, blocks: big1.update({k: blocks[k] for k in EMIT_GROUPS[group]}))
    flying1, token = reduce_scatter_start(SHARDED, big1, "1")

    def emit(group, blocks):
        started, token = reduce_scatter_start(EMIT_GROUPS[group], blocks, f"0_{group}")
        flying0.append(started)
        return token

    dx, small[0] = _layer_bwd(dx, saved0, emit, after=token)
    received = [{}, dict(zip(SHARDED, _exchange_wait(flying1, dx, CHIP_SCATTER, name="exchange_wait_1")))]
    for group, members in enumerate(EMIT_GROUPS):
        received[0].update(zip(members, _exchange_wait(flying0[group], dx, CHIP_SCATTER, name=f"exchange_wait_0_{group}")))

    grads = {k: jnp.stack([small[i][k] for i in range(DEPTH)]) for k in small[0]}
    grads = {k: g[:, 0] if k in ("norm1_g", "norm2_g", "onorm_g", "pool_scale") else g for k, g in grads.items()}
    grads["final_g"] = dgf[0]
    loss_row = jnp.pad(loss_part, ((0, 0), (0, SLAB_COLS - LANE)))
    (small_flying,), token = _exchange_start([[_pack_slab(grads, loss_row)]], GATHER, name="small_gather_start")

    out_g, out_d, out_m, out_v = {}, {}, {}, {}
    for k in SHARDED:
        out_g[k], out_d[k], out_m[k], out_v[k] = _adamw_reduce(w[k], [received[i][k] for i in range(DEPTH)], m[k], v[k],
                                                                name="adamw_" + k, after=token)
    behind_all = jnp.stack([out_v[k][0, 0, 0] for k in SHARDED])
    (small_parts,) = _exchange_wait(small_flying, behind_all, GATHER, name="small_gather_wait")
    zero_row = jnp.zeros((1, SLAB_COLS), F32)
    slabs = _adamw_reduce(_pack_slab(w, zero_row)[None], [small_parts], _pack_slab(m, zero_row)[None],
                          _pack_slab(v, zero_row)[None], name="adamw_small")
    slabs = [s[0] for s in slabs]
    shapes = {k: w[k].shape for k in SMALL}
    for dst, slab in zip((out_g, out_d, out_m, out_v), slabs):
        vals, _ = _unpack_slab(slab, shapes)
        dst.update(vals)
    _, loss_at = _unpack_slab(slabs[0], shapes)
    loss = slabs[0][loss_at, 0]
    for group in (out_g, out_d, out_m, out_v):
        group.update({k: _ff_rows(group[k]) for k in TRANSPOSED})

    return (loss, dx[None], *[out_g[k] for k in names], *[out_d[k] for k in names], *[out_m[k] for k in names],
            *[out_v[k] for k in names])
```

```python
import functools

import jax
import jax.numpy as jnp
from jax import lax
from jax.experimental import pallas as pl
from jax.experimental.pallas import tpu as pltpu

F32 = jnp.float32
BF16 = jnp.bfloat16

D_MODEL = 1024
DEPTH = 2
PLE_DIM = 256
EPS = 1e-6
HEAD_DIM = 128
HEADS = 4
A_DIM = HEADS * HEAD_DIM
QKV_TAPS = 4
CHUNK = 64
POOL_WINDOWS = (2, 4, 8, 16)
POOL_DIM = 256
CONV_DIM = 256
CONV_TAPS = 3
D_FF = 2816
D_IN = 3080
AB_COL = 2048
N_DEV = 8

ADAM_LR = 0.001
ADAM_B1 = 0.9
ADAM_B2 = 0.999
ADAM_EPS = 1e-08
ADAM_WD = 0.01
ADAM_STEP = 10

LANE = 128
SUBLANE = 8
VMEM_BYTES_V7X = 64 * 1024 * 1024
VMEM_LIMIT = VMEM_BYTES_V7X * 3 // 4

NN = ((1,), (0,))
NT = ((1,), (1,))
TN = ((0,), (0,))
MESH = pl.DeviceIdType.MESH


def _dot(a, b, dims):
    return lax.dot_general(a.astype(BF16), b.astype(BF16), (dims, ((), ())), preferred_element_type=F32)


def _pcall(body, *, name, out_shape, grid=(), in_specs=None, out_specs=None, scratch_shapes=(), semantics=None,
           vmem_limit=None, after=None, **kw):
    params = {}
    if semantics is not None:
        params["dimension_semantics"] = semantics
    if vmem_limit is not None:
        params["vmem_limit_bytes"] = vmem_limit
    if after is not None:
        n_in, inner = len(in_specs), body
        body = lambda *refs: inner(*refs[:n_in], *refs[n_in + 1:])
        in_specs = list(in_specs) + [pl.BlockSpec(after.shape, lambda *_: (0,) * after.ndim)]
    call = pl.pallas_call(
        body, name=name, out_shape=out_shape, grid=grid, in_specs=in_specs, out_specs=out_specs,
        scratch_shapes=list(scratch_shapes), compiler_params=pltpu.CompilerParams(**params), **kw)
    return call if after is None else (lambda *args: call(*args, after))


def _sigmoid(x):
    return 1.0 / (1.0 + jnp.exp(-x))


def _softplus(x):
    return jnp.maximum(x, 0.0) + jnp.log(1.0 + jnp.exp(-jnp.abs(x)))


def _tile(n, cap, mult):
    if n <= cap:
        return n
    best = None
    for t in range(mult, cap + 1, mult):
        if n % t == 0:
            best = t
    assert best is not None, (n, cap, mult)
    return best


ROWS_PER_STEP = 512
NARROW_RESULT = 1024
COLS_PER_DOT = 640


def _resident(weight):
    return pl.BlockSpec(weight.shape, lambda i: (0,) * weight.ndim, pipeline_mode=pl.Buffered(1))


def _matmul_rows(a, b, mode, *, name, res=None, out_dtype=F32, b_blocked=False, after=None, norm_g=None):
    m, k = a.shape
    if b_blocked:
        nb, _, bw = b.shape
        n = nb * bw if mode == "nn" else b.shape[1]
    else:
        n = b.shape[1] if mode == "nn" else b.shape[0]
    tm = _tile(m, ROWS_PER_STEP if n > NARROW_RESULT else 2 * ROWS_PER_STEP, 16)
    cn = bw if (b_blocked and mode == "nn") else _tile(n, COLS_PER_DOT, LANE)
    has_res = res is not None
    normed = norm_g is not None

    def body(*refs):
        a_ref, b_ref = refs[0], refs[1]
        g_ref = refs[2] if normed else None
        res_ref = refs[2 + normed] if has_res else None
        o_ref = refs[2 + normed + has_res]
        if normed:
            av = _rms_normed(a_ref[...], g_ref[...])
            refs[3 + normed + has_res][...] = av
        elif not (b_blocked and mode == "nt"):
            av = a_ref[...].astype(BF16)
        for j in range(n // cn):
            cols = pl.ds(j * cn, cn)
            if mode == "nn":
                part = _dot(av, b_ref[j] if b_blocked else b_ref[:, cols], NN)
            elif not b_blocked:
                part = _dot(av, b_ref[cols, :], NT)
            else:
                part = None
                for s in range(nb):
                    term = _dot(a_ref[:, pl.ds(s * bw, bw)], b_ref[s, cols, :], NT)
                    part = term if part is None else part + term
            if has_res:
                part = part + res_ref[:, cols]
            o_ref[:, cols] = part.astype(o_ref.dtype)

    row = lambda width: pl.BlockSpec((tm, width), lambda i: (i, 0))
    whole = _resident(b)
    ins = [a, b] + ([norm_g] if normed else []) + ([res] if has_res else [])
    specs = [row(k), whole] + ([pl.BlockSpec((1, k), lambda i: (0, 0))] if normed else []) + ([row(n)] if has_res else [])
    out = jax.ShapeDtypeStruct((m, n), out_dtype)
    return _pcall(body, name=name, out_shape=(out, jax.ShapeDtypeStruct((m, k), BF16)) if normed else out, grid=(m // tm,),
                  in_specs=specs, out_specs=(row(n), row(k)) if normed else row(n), semantics=("parallel",),
                  vmem_limit=VMEM_LIMIT, after=after)(*ins)


def _matmul_norm_bwd(a, b, x, g, dres, *, name, more=None, after=None):
    m, k = a.shape
    d = b.shape[1]
    tm = _tile(m, ROWS_PER_STEP, 16)
    cn = _tile(d, COLS_PER_DOT, LANE)
    pairs = 1 if more is None else 2

    def body(*refs):
        a_ref, b_ref, x_ref, g_ref, dres_ref = refs[:5]
        dx_ref, dg_ref = refs[3 + 2 * pairs], refs[4 + 2 * pairs]
        av = a_ref[...].astype(BF16)
        for j in range(d // cn):
            cols = pl.ds(j * cn, cn)
            part = _dot(av, b_ref[:, cols], NN)
            if more is not None:
                part = part + _dot(refs[5][...], refs[6][:, cols], NN)
            dx_ref[:, cols] = part
        dhv = dx_ref[...]
        xv = x_ref[...]
        r = lax.rsqrt(jnp.mean(xv * xv, axis=-1, keepdims=True) + EPS)
        xhat = xv * r
        dhg = dhv * g_ref[...]
        dx_ref[...] = dres_ref[...] + r * (dhg - xhat * jnp.mean(dhg * xhat, axis=-1, keepdims=True))
        part_g = jnp.sum(dhv * xhat, axis=0, keepdims=True)

        @pl.when(pl.program_id(0) == 0)
        def _():
            dg_ref[...] = part_g

        @pl.when(pl.program_id(0) > 0)
        def _():
            dg_ref[...] += part_g

    row = lambda width: pl.BlockSpec((tm, width), lambda i: (i, 0))
    vec = pl.BlockSpec((1, d), lambda i: (0, 0))
    ins = [a, b, x, g, dres] + (list(more) if more is not None else [])
    specs = [row(k), _resident(b), row(d), vec, row(d)] + ([row(more[0].shape[1]), _resident(more[1])] if more is not None else [])
    return _pcall(body, name=name, out_shape=(jax.ShapeDtypeStruct((m, d), F32), jax.ShapeDtypeStruct((1, d), F32)),
                  grid=(m // tm,), in_specs=specs, out_specs=(row(d), vec), semantics=("arbitrary",), vmem_limit=VMEM_LIMIT,
                  after=after)(*ins)


def _rms_normed(xv, gv):
    return (xv * lax.rsqrt(jnp.mean(xv * xv, axis=-1, keepdims=True) + EPS) * gv).astype(BF16)


def _matmul(a, b, mode, *, name, res=None, out_dtype=F32, b_blocked=False, out_blocked=None, after=None, norm_g=None):
    if mode != "tn":
        return _matmul_rows(a, b, mode, name=name, res=res, out_dtype=out_dtype, b_blocked=b_blocked, after=after, norm_g=norm_g)
    assert res is None and not b_blocked and after is None and norm_g is None
    (t, m), (t2, n) = a.shape, b.shape
    assert t == t2, (a.shape, b.shape)
    tm = _tile(m, 1024, LANE)
    tn = _tile(n, NARROW_RESULT if n <= NARROW_RESULT else COLS_PER_DOT, LANE)
    if out_blocked is not None:
        assert out_blocked[0] * out_blocked[1] == n
        tn = out_blocked[1]

    def body(a_ref, b_ref, o_ref):
        part = _dot(a_ref[...], b_ref[...], TN).astype(o_ref.dtype)
        if out_blocked is None:
            o_ref[...] = part
        else:
            o_ref[0] = part

    o_spec = (pl.BlockSpec((tm, tn), lambda i, j: (i, j)) if out_blocked is None
              else pl.BlockSpec((1, tm, tn), lambda i, j: (j, i, 0)))
    o_shape = (m, n) if out_blocked is None else (out_blocked[0], m, out_blocked[1])
    return _pcall(body, name=name, out_shape=jax.ShapeDtypeStruct(o_shape, out_dtype), grid=(m // tm, n // tn),
                  in_specs=[pl.BlockSpec((t, tm), lambda i, j: (0, i)), pl.BlockSpec((t, tn), lambda i, j: (0, j))],
                  out_specs=o_spec, semantics=("parallel", "parallel"), vmem_limit=VMEM_LIMIT)(a, b)


ROW_TILE = 512


def _rows(t, width, idx=0):
    return pl.BlockSpec((ROW_TILE, width), lambda i: (i, idx))


def _vec(width):
    return pl.BlockSpec((1, width), lambda i: (0, 0))


def _swiglu_fwd(x, norm_g, w_gate, w_up, *, name):
    t, k = x.shape
    f = w_gate.shape[0]
    tm = _tile(t, ROWS_PER_STEP, 16)
    cn = _tile(f, COLS_PER_DOT, LANE)

    def body(x_ref, g_ref, wg_ref, wu_ref, ff_ref, gate_ref, up_ref, h_ref):
        hv = _rms_normed(x_ref[...], g_ref[...])
        h_ref[...] = hv
        for j in range(f // cn):
            cols = pl.ds(j * cn, cn)
            gv = _dot(hv, wg_ref[cols, :], NT)
            uv = _dot(hv, wu_ref[cols, :], NT)
            gate_ref[:, cols] = gv.astype(BF16)
            up_ref[:, cols] = uv.astype(BF16)
            ff_ref[:, cols] = (gv * _sigmoid(gv) * uv).astype(BF16)

    row = lambda width: pl.BlockSpec((tm, width), lambda i: (i, 0))
    out = jax.ShapeDtypeStruct((t, f), BF16)
    return _pcall(body, name=name, out_shape=(out,) * 3 + (jax.ShapeDtypeStruct((t, k), BF16),), grid=(t // tm,),
                  in_specs=[row(k), pl.BlockSpec((1, k), lambda i: (0, 0)), _resident(w_gate), _resident(w_up)],
                  out_specs=(row(f),) * 3 + (row(k),), semantics=("parallel",), vmem_limit=VMEM_LIMIT)(x, norm_g, w_gate, w_up)


def _swiglu_bwd(dx2, w_down, gate, up, *, name, after=None):
    t, d = dx2.shape
    f = w_down.shape[0]
    tm = _tile(t, ROWS_PER_STEP, 16)
    cn = _tile(f, COLS_PER_DOT, LANE)

    def body(dx_ref, w_ref, gate_ref, up_ref, dgate_ref, dup_ref):
        dxv = dx_ref[...].astype(BF16)
        for j in range(f // cn):
            cols = pl.ds(j * cn, cn)
            dffv = _dot(dxv, w_ref[cols, :], NT)
            gv = gate_ref[:, cols].astype(F32)
            sig = _sigmoid(gv)
            dgate_ref[:, cols] = (dffv * up_ref[:, cols].astype(F32) * sig * (1.0 + gv * (1.0 - sig))).astype(BF16)
            dup_ref[:, cols] = (dffv * gv * sig).astype(BF16)

    row = lambda width: pl.BlockSpec((tm, width), lambda i: (i, 0))
    out = jax.ShapeDtypeStruct((t, f), BF16)
    return _pcall(body, name=name, out_shape=(out, out), grid=(t // tm,), in_specs=[row(d), _resident(w_down), row(f), row(f)],
                  out_specs=(row(f), row(f)), semantics=("parallel",), vmem_limit=VMEM_LIMIT, after=after)(dx2, w_down, gate, up)


def _ple_fwd(x2, p, w_gate, w_proj, *, name):
    t, d = x2.shape
    nb, pdim, bw = w_proj.shape
    tm = _tile(t, ROWS_PER_STEP, 16)
    cn = _tile(d, COLS_PER_DOT, LANE)

    def body(x_ref, p_ref, wg_ref, wp_ref, x3_ref, pgl_ref, pp_ref):
        xb = x_ref[...].astype(BF16)
        pb = p_ref[...].astype(BF16)
        per = cn // bw
        for c in range(d // cn):
            cols = pl.ds(c * cn, cn)
            pgl = _dot(xb, wg_ref[:, cols], NN)
            pp = jnp.concatenate([_dot(pb, wp_ref[c * per + j], NN) for j in range(per)], axis=1)
            pgl_ref[:, cols] = pgl
            pp_ref[:, cols] = pp
            x3_ref[:, cols] = x_ref[:, cols] + _sigmoid(pgl) * pp

    row = lambda width: pl.BlockSpec((tm, width), lambda i: (i, 0))
    out = jax.ShapeDtypeStruct((t, d), F32)
    return _pcall(body, name=name, out_shape=(out,) * 3, grid=(t // tm,),
                  in_specs=[row(d), row(pdim), _resident(w_gate), _resident(w_proj)], out_specs=(row(d),) * 3,
                  semantics=("parallel",), vmem_limit=VMEM_LIMIT)(x2, p, w_gate, w_proj)


def _ple_bwd(dx3, pgl, pp, *, name, after=None):
    t, d = dx3.shape

    def body(dx_ref, pgl_ref, pp_ref, dpgl_ref, dpp_ref):
        dxv = dx_ref[...]
        sig = _sigmoid(pgl_ref[...])
        dpp_ref[...] = (dxv * sig).astype(BF16)
        dpgl_ref[...] = (dxv * pp_ref[...] * sig * (1.0 - sig)).astype(BF16)

    return _pcall(body, name=name, out_shape=(jax.ShapeDtypeStruct((t, d), BF16),) * 2, grid=(t // ROW_TILE,),
                  in_specs=[_rows(t, d)] * 3, out_specs=(_rows(t, d),) * 2, semantics=("parallel",), after=after)(dx3, pgl, pp)


def _loss_head(x3, g, target, *, name):
    t, d = x3.shape

    def body(x_ref, g_ref, t_ref, dx_ref, dg_ref, loss_ref):
        xv = x_ref[...]
        r = lax.rsqrt(jnp.mean(xv * xv, axis=-1, keepdims=True) + EPS)
        xhat = xv * r
        gv = g_ref[...]
        err = xhat * gv - t_ref[...]
        row_loss = jnp.sum(err * err, axis=-1, keepdims=True) * (0.5 / d)
        lpart = jnp.broadcast_to(jnp.sum(row_loss, axis=0, keepdims=True), (1, LANE))
        dy = err * (1.0 / d)
        dyg = dy * gv
        dx_ref[...] = r * (dyg - xhat * jnp.mean(dyg * xhat, axis=-1, keepdims=True))
        gpart = jnp.sum(dy * xhat, axis=0, keepdims=True)

        @pl.when(pl.program_id(0) == 0)
        def _():
            dg_ref[...] = gpart
            loss_ref[...] = lpart

        @pl.when(pl.program_id(0) > 0)
        def _():
            dg_ref[...] += gpart
            loss_ref[...] += lpart

    return _pcall(body, name=name,
                  out_shape=(jax.ShapeDtypeStruct((t, d), F32), jax.ShapeDtypeStruct((1, d), F32), jax.ShapeDtypeStruct((1, LANE), F32)),
                  grid=(t // ROW_TILE,), in_specs=[_rows(t, d), _vec(d), _rows(t, d)],
                  out_specs=(_rows(t, d), _vec(d), _vec(LANE)), semantics=("arbitrary",))(x3, g, target)


def _shift_down(x, d):
    if d == 0:
        return x
    row = lax.broadcasted_iota(jnp.int32, x.shape, 0)
    return jnp.where(row >= d, pltpu.roll(x, d, 0), 0.0)


def _shift_up(x, d):
    if d == 0:
        return x
    t = x.shape[0]
    row = lax.broadcasted_iota(jnp.int32, x.shape, 0)
    return jnp.where(row < t - d, pltpu.roll(x, t - d, 0), 0.0)


def _colsum(x):
    return jnp.sum(x, axis=0, keepdims=True)


def _col(t, idx_fn):
    return pl.BlockSpec((t, LANE), idx_fn)


def _conv_fwd(x, w_ref, taps):
    acc = None
    for j in range(taps):
        term = w_ref[pl.ds(j, 1), :] * _shift_down(x, taps - 1 - j)
        acc = term if acc is None else acc + term
    return acc


def _conv_bwd(x, dy, w_ref, dw_ref, taps):
    dx = None
    for j in range(taps):
        term = w_ref[pl.ds(j, 1), :] * _shift_up(dy, taps - 1 - j)
        dx = term if dx is None else dx + term
        dw_ref[pl.ds(j, 1), :] = _colsum(dy * _shift_down(x, taps - 1 - j))
    return dx


def _qkv_prep_fwd(proj, conv_w, *, name):
    t = proj.shape[0]
    scale = HEAD_DIM ** -0.5

    def body(x_ref, w_ref, o_ref):
        j = pl.program_id(0)
        c = _conv_fwd(x_ref[...], w_ref, QKV_TAPS)
        s = c * _sigmoid(c)
        r = lax.rsqrt(jnp.sum(s * s, axis=-1, keepdims=True) + EPS)
        f = jnp.where(j < 2 * HEADS, r, 1.0) * jnp.where(j < HEADS, scale, 1.0)
        o_ref[0] = s * f

    return _pcall(body, name=name, out_shape=jax.ShapeDtypeStruct((3 * HEADS, t, LANE), F32), grid=(3 * HEADS,),
                  in_specs=[_col(t, lambda j: (0, j)), pl.BlockSpec((QKV_TAPS, LANE), lambda j: (0, j))],
                  out_specs=pl.BlockSpec((1, t, LANE), lambda j: (j, 0, 0)), semantics=("parallel",),
                  vmem_limit=VMEM_LIMIT)(proj, conv_w)


def _qkv_prep_bwd(proj, conv_w, dqkv, *, name):
    t = proj.shape[0]
    scale = HEAD_DIM ** -0.5

    def body(x_ref, w_ref, d_ref, dx_ref, dw_ref):
        j = pl.program_id(0)
        xv = x_ref[...]
        c = _conv_fwd(xv, w_ref, QKV_TAPS)
        sig = _sigmoid(c)
        s = c * sig
        r = lax.rsqrt(jnp.sum(s * s, axis=-1, keepdims=True) + EPS)
        n0 = s * r
        dv = d_ref[0]
        dn0 = dv * jnp.where(j < HEADS, scale, 1.0)
        ds_norm = r * (dn0 - n0 * jnp.sum(dn0 * n0, axis=-1, keepdims=True))
        ds = jnp.where(j < 2 * HEADS, ds_norm, dv)
        dc = ds * sig * (1.0 + c * (1.0 - sig))
        dx_ref[...] = _conv_bwd(xv, dc, w_ref, dw_ref, QKV_TAPS).astype(BF16)

    return _pcall(body, name=name,
                  out_shape=(jax.ShapeDtypeStruct((t, 3 * A_DIM), BF16), jax.ShapeDtypeStruct((QKV_TAPS, 3 * A_DIM), F32)),
                  grid=(3 * HEADS,),
                  in_specs=[_col(t, lambda j: (0, j)), pl.BlockSpec((QKV_TAPS, LANE), lambda j: (0, j)),
                            pl.BlockSpec((1, t, LANE), lambda j: (j, 0, 0))],
                  out_specs=(_col(t, lambda j: (0, j)), pl.BlockSpec((QKV_TAPS, LANE), lambda j: (0, j))),
                  semantics=("parallel",), vmem_limit=VMEM_LIMIT)(proj, conv_w, dqkv)


def _lane_pick(x, lane_idx, lane):
    return jnp.broadcast_to(jnp.sum(jnp.where(lane == lane_idx, x, 0.0), axis=-1, keepdims=True), x.shape)


def _gates_fwd(proj, alog, dtb, *, name):
    t = proj.shape[0]

    def body(x_ref, alog_ref, dtb_ref, g_ref, b_ref):
        xv = x_ref[...]
        lane = lax.broadcasted_iota(jnp.int32, xv.shape, 1)
        gall = -jnp.exp(alog_ref[...]) * _softplus(xv + dtb_ref[...])
        ball = _sigmoid(xv)
        for h in range(HEADS):
            g_ref[h] = _lane_pick(gall, h, lane)
            b_ref[h] = _lane_pick(ball, HEADS + h, lane)

    out = jax.ShapeDtypeStruct((HEADS, t, LANE), F32)
    whole = pl.BlockSpec((HEADS, t, LANE), lambda i: (0, 0, 0))
    return _pcall(body, name=name, out_shape=(out, out), grid=(1,),
                  in_specs=[_col(t, lambda i: (0, AB_COL // LANE)), _vec(LANE), _vec(LANE)], out_specs=(whole, whole),
                  semantics=("arbitrary",), vmem_limit=VMEM_LIMIT)(proj, alog, dtb)


def _gates_bwd(proj, alog, dtb, dg, dbeta, *, name):
    t = proj.shape[0]

    def body(x_ref, alog_ref, dtb_ref, dg_ref, db_ref, dab_ref, dalog_ref, ddtb_ref):
        xv = x_ref[...]
        lane = lax.broadcasted_iota(jnp.int32, xv.shape, 1)
        lane1 = lax.broadcasted_iota(jnp.int32, (1, LANE), 1)
        z = xv + dtb_ref[...]
        nea = -jnp.exp(alog_ref[...])
        da_f = nea * _sigmoid(z)
        g_f = nea * _softplus(z)
        ball = _sigmoid(xv)
        db_f = ball * (1.0 - ball)
        dab = jnp.zeros_like(xv)
        dalog = jnp.zeros((1, LANE), F32)
        for h in range(HEADS):
            dgh = dg_ref[h]
            dab = dab + jnp.where(lane == h, dgh * da_f, 0.0) + jnp.where(lane == HEADS + h, db_ref[h] * db_f, 0.0)
            dalog = dalog + jnp.where(lane1 == h, _colsum(dgh * g_f), 0.0)
        dab_ref[...] = dab.astype(BF16)
        dalog_ref[...] = dalog
        ddtb_ref[...] = jnp.where(lane1 < HEADS, _colsum(dab), 0.0)

    whole = pl.BlockSpec((HEADS, t, LANE), lambda i: (0, 0, 0))
    vec = jax.ShapeDtypeStruct((1, LANE), F32)
    return _pcall(body, name=name, out_shape=(jax.ShapeDtypeStruct((t, LANE), BF16), vec, vec), grid=(1,),
                  in_specs=[_col(t, lambda i: (0, AB_COL // LANE)), _vec(LANE), _vec(LANE), whole, whole],
                  out_specs=(_col(t, lambda i: (0, 0)), _vec(LANE), _vec(LANE)), semantics=("arbitrary",),
                  vmem_limit=VMEM_LIMIT)(proj, alog, dtb, dg, dbeta)


Z_COL = 3 * A_DIM // LANE


def _apost_fwd(o, proj, gn, *, name):
    t = proj.shape[0]

    def body(o_ref, z_ref, gn_ref, y_ref):
        ov = o_ref[0]
        z = z_ref[...]
        r = lax.rsqrt(jnp.mean(ov * ov, axis=-1, keepdims=True) + EPS)
        y_ref[...] = (ov * r * gn_ref[...] * (z * _sigmoid(z))).astype(BF16)

    return _pcall(body, name=name, out_shape=jax.ShapeDtypeStruct((t, A_DIM), BF16), grid=(HEADS,),
                  in_specs=[pl.BlockSpec((1, t, LANE), lambda h: (h, 0, 0)), _col(t, lambda h: (0, Z_COL + h)),
                            pl.BlockSpec((1, LANE), lambda h: (0, 0))],
                  out_specs=_col(t, lambda h: (0, h)), semantics=("parallel",), vmem_limit=VMEM_LIMIT)(o, proj, gn)


def _apost_bwd(o, proj, gn, dmixed, *, name):
    t = proj.shape[0]

    def body(o_ref, z_ref, gn_ref, d_ref, do_ref, dz_ref, dgn_ref):
        ov = o_ref[0]
        z = z_ref[...]
        gnv = gn_ref[...]
        dv = d_ref[...]
        r = lax.rsqrt(jnp.mean(ov * ov, axis=-1, keepdims=True) + EPS)
        ohat = ov * r
        sig = _sigmoid(z)
        dy = dv * (z * sig)
        dz_ref[...] = (dv * ohat * gnv * sig * (1.0 + z * (1.0 - sig))).astype(BF16)
        dyo = dy * gnv
        do_ref[0] = r * (dyo - ohat * jnp.mean(dyo * ohat, axis=-1, keepdims=True))
        part = _colsum(dy * ohat)

        @pl.when(pl.program_id(0) == 0)
        def _():
            dgn_ref[...] = part

        @pl.when(pl.program_id(0) > 0)
        def _():
            dgn_ref[...] += part

    return _pcall(body, name=name,
                  out_shape=(jax.ShapeDtypeStruct((HEADS, t, LANE), F32), jax.ShapeDtypeStruct((t, A_DIM), BF16),
                             jax.ShapeDtypeStruct((1, LANE), F32)),
                  grid=(HEADS,),
                  in_specs=[pl.BlockSpec((1, t, LANE), lambda h: (h, 0, 0)), _col(t, lambda h: (0, Z_COL + h)),
                            pl.BlockSpec((1, LANE), lambda h: (0, 0)), _col(t, lambda h: (0, h))],
                  out_specs=(pl.BlockSpec((1, t, LANE), lambda h: (h, 0, 0)), _col(t, lambda h: (0, h)),
                             pl.BlockSpec((1, LANE), lambda h: (0, 0))),
                  semantics=("arbitrary",), vmem_limit=VMEM_LIMIT)(o, proj, gn, dmixed)


POOL_COL = (AB_COL + LANE) // LANE
CB_COL = POOL_COL + POOL_DIM // LANE
CC_COL = CB_COL + CONV_DIM // LANE
CH_COL = CC_COL + CONV_DIM // LANE
MAX_WIN_LOG2 = 4


def _window_sums(x, shift):
    sums = []
    cur = x
    for k in range(MAX_WIN_LOG2):
        cur = cur + shift(cur, 1 << k)
        sums.append(cur)
    return sums


def _pick_window(sums, win):
    out = sums[-1]
    for k in range(MAX_WIN_LOG2 - 2, -1, -1):
        out = jnp.where(win == float(2 << k), sums[k], out)
    return out


def _pool_counts(shape, win):
    row = lax.broadcasted_iota(jnp.int32, shape, 0).astype(F32)
    return jnp.minimum(row + 1.0, win)


def _pool_fwd(proj, win, wbd, scale, *, name):
    t = proj.shape[0]

    def body(x_ref, win_ref, w_ref, s_ref, y_ref):
        xv = x_ref[...]
        winv = win_ref[...]
        pooled = _pick_window(_window_sums(xv, _shift_down), winv) / _pool_counts(xv.shape, winv) - xv
        y_ref[...] = (_dot(pooled, w_ref[0], NN) * s_ref[...]).astype(BF16)

    nb = POOL_DIM // LANE
    vec = pl.BlockSpec((1, LANE), lambda b: (0, b))
    return _pcall(body, name=name, out_shape=jax.ShapeDtypeStruct((t, POOL_DIM), BF16), grid=(nb,),
                  in_specs=[_col(t, lambda b: (0, POOL_COL + b)), vec, pl.BlockSpec((1, LANE, LANE), lambda b: (b, 0, 0)), vec],
                  out_specs=_col(t, lambda b: (0, b)), semantics=("parallel",), vmem_limit=VMEM_LIMIT)(proj, win, wbd, scale)


def _pool_bwd(proj, win, wbd, scale, dmixed, *, name):
    t = proj.shape[0]

    def body(x_ref, win_ref, w_ref, s_ref, d_ref, dx_ref, dw_ref, ds_ref):
        xv = x_ref[...]
        winv = win_ref[...]
        cnt = _pool_counts(xv.shape, winv)
        pooled = _pick_window(_window_sums(xv, _shift_down), winv) / cnt - xv
        dv = d_ref[...]
        ds_ref[...] = _colsum(dv * _dot(pooled, w_ref[0], NN))
        dy0 = dv * s_ref[...]
        dw_ref[0] = _dot(pooled, dy0, TN)
        dpooled = _dot(dy0, w_ref[0], NT)
        dmean = dpooled / cnt
        dx_ref[...] = (_pick_window(_window_sums(dmean, _shift_up), winv) - dpooled).astype(BF16)

    nb = POOL_DIM // LANE
    vec = pl.BlockSpec((1, LANE), lambda b: (0, b))
    mat = pl.BlockSpec((1, LANE, LANE), lambda b: (b, 0, 0))
    first = A_DIM // LANE
    return _pcall(body, name=name,
                  out_shape=(jax.ShapeDtypeStruct((t, POOL_DIM), BF16), jax.ShapeDtypeStruct((nb, LANE, LANE), F32),
                             jax.ShapeDtypeStruct((1, POOL_DIM), F32)),
                  grid=(nb,),
                  in_specs=[_col(t, lambda b: (0, POOL_COL + b)), vec, mat, vec, _col(t, lambda b: (0, first + b))],
                  out_specs=(_col(t, lambda b: (0, b)), mat, vec), semantics=("parallel",),
                  vmem_limit=VMEM_LIMIT)(proj, win, wbd, scale, dmixed)


def _sconv_fwd(proj, w, *, name):
    t = proj.shape[0]

    def body(cb_ref, cc_ref, ch_ref, w_ref, y_ref):
        y_ref[...] = (cb_ref[...] * _conv_fwd(cc_ref[...] * ch_ref[...], w_ref, CONV_TAPS)).astype(BF16)

    nb = CONV_DIM // LANE
    return _pcall(body, name=name, out_shape=jax.ShapeDtypeStruct((t, CONV_DIM), BF16), grid=(nb,),
                  in_specs=[_col(t, lambda b: (0, CB_COL + b)), _col(t, lambda b: (0, CC_COL + b)),
                            _col(t, lambda b: (0, CH_COL + b)), pl.BlockSpec((CONV_TAPS, LANE), lambda b: (0, b))],
                  out_specs=_col(t, lambda b: (0, b)), semantics=("parallel",), vmem_limit=VMEM_LIMIT)(proj, proj, proj, w)


def _sconv_bwd(proj, w, dmixed, *, name):
    t = proj.shape[0]

    def body(cb_ref, cc_ref, ch_ref, w_ref, d_ref, dcb_ref, dcc_ref, dch_ref, dw_ref):
        cc = cc_ref[...]
        ch = ch_ref[...]
        u = cc * ch
        dv = d_ref[...]
        dcb_ref[...] = (dv * _conv_fwd(u, w_ref, CONV_TAPS)).astype(BF16)
        du = _conv_bwd(u, dv * cb_ref[...], w_ref, dw_ref, CONV_TAPS)
        dcc_ref[...] = (du * ch).astype(BF16)
        dch_ref[...] = (du * cc).astype(BF16)

    nb = CONV_DIM // LANE
    first = (A_DIM + POOL_DIM) // LANE
    act = jax.ShapeDtypeStruct((t, CONV_DIM), BF16)
    wspec = pl.BlockSpec((CONV_TAPS, LANE), lambda b: (0, b))
    ospec = _col(t, lambda b: (0, b))
    return _pcall(body, name=name, out_shape=(act, act, act, jax.ShapeDtypeStruct((CONV_TAPS, CONV_DIM), F32)), grid=(nb,),
                  in_specs=[_col(t, lambda b: (0, CB_COL + b)), _col(t, lambda b: (0, CC_COL + b)),
                            _col(t, lambda b: (0, CH_COL + b)), wspec, _col(t, lambda b: (0, first + b))],
                  out_specs=(ospec, ospec, ospec, wspec), semantics=("parallel",),
                  vmem_limit=VMEM_LIMIT)(proj, proj, proj, w, dmixed)


def _chunk_masks():
    r = lax.broadcasted_iota(jnp.int32, (CHUNK, CHUNK), 0)
    c = lax.broadcasted_iota(jnp.int32, (CHUNK, CHUNK), 1)
    return r >= c, r > c, jnp.where(r == c, 1.0, 0.0).astype(F32)


def _split(a):
    hi = a.astype(BF16)
    return hi, (a - hi.astype(F32)).astype(BF16)


def _dot_split(a, b, dims):
    (ah, al), (bh, bl) = a, b
    return _dot(ah, bh, dims) + _dot(ah, bl, dims) + _dot(al, bh, dims)


def _tri_inv(lows, eye):
    xs = [eye - low for low in lows]
    ps = [_split(low) for low in lows]
    ps = [_split(_dot_split(p, p, NN)) for p in ps]
    for i in range(5):
        xs = [x + _dot_split(_split(x), p, NN) for x, p in zip(xs, ps)]
        if i < 4:
            ps = [_split(_dot_split(p, p, NN)) for p in ps]
    return xs


def _prefix_sum_rows(x):
    for k in range(6):
        x = x + _shift_down(x, 1 << k)
    return x


def _suffix_sum_rows(x):
    for k in range(6):
        x = x + _shift_up(x, 1 << k)
    return x


def _chunk_decay(g, incl):
    gcb = _prefix_sum_rows(g)
    gtot = _colsum(g)
    col = gcb[:, :CHUNK]
    row = gcb.T[:CHUNK, :]
    decay = jnp.exp(jnp.where(incl, col - row, -1e30))
    return gcb, gtot, decay


CHUNKS_PER_STEP = 4


def _chunk_rows(j):
    return pl.ds(j * CHUNK, CHUNK)


def _deltanet_prep(qkv, g, beta, *, name):
    t = qkv.shape[1]
    n_chunks = t // CHUNK
    per = CHUNKS_PER_STEP
    probs = [(j, h) for j in range(per) for h in range(HEADS)]

    def body(qkv_ref, g_ref, b_ref, u_ref, w_ref, qg_ref, kg_ref, attn_ref, tm_ref):
        incl, strict, eye = _chunk_masks()
        q = [qkv_ref[h, _chunk_rows(j), :] for j, h in probs]
        k = [qkv_ref[HEADS + h, _chunk_rows(j), :] for j, h in probs]
        v = [qkv_ref[2 * HEADS + h, _chunk_rows(j), :] for j, h in probs]
        bv = [b_ref[h, _chunk_rows(j), :] for j, h in probs]
        dec = [_chunk_decay(g_ref[h, _chunk_rows(j), :], incl) for j, h in probs]
        kb = [a * b for a, b in zip(k, bv)]
        low = [jnp.where(strict, _dot(a, b, NT) * d[2], 0.0) for a, b, d in zip(kb, k, dec)]
        tm = _tri_inv(low, eye)
        egc = [jnp.exp(d[0]) for d in dec]
        u = [_dot(m, a * b, NN) for m, a, b in zip(tm, v, bv)]
        w = [_dot(m, a * e, NN) for m, a, e in zip(tm, kb, egc)]
        attn = [_dot(a, b, NT) * d[2] for a, b, d in zip(q, k, dec)]
        for i, (j, h) in enumerate(probs):
            rows = _chunk_rows(j)
            u_ref[h, rows, :] = u[i]
            w_ref[h, rows, :] = w[i].astype(BF16)
            qg_ref[h, rows, :] = (q[i] * egc[i]).astype(BF16)
            kg_ref[h, rows, :] = (k[i] * jnp.exp(dec[i][1] - dec[i][0])).astype(BF16)
            attn_ref[j, h] = attn[i].astype(BF16)
            tm_ref[j, h] = tm[i]

    act = lambda heads: pl.BlockSpec((heads, per * CHUNK, LANE), lambda n: (0, n, 0))
    mat = pl.BlockSpec((per, HEADS, CHUNK, CHUNK), lambda n: (n, 0, 0, 0))
    return _pcall(
        body, name=name,
        out_shape=(jax.ShapeDtypeStruct((HEADS, t, LANE), F32),) + (jax.ShapeDtypeStruct((HEADS, t, LANE), BF16),) * 3
        + (jax.ShapeDtypeStruct((n_chunks, HEADS, CHUNK, CHUNK), BF16), jax.ShapeDtypeStruct((n_chunks, HEADS, CHUNK, CHUNK), F32)),
        grid=(n_chunks // per,), in_specs=[act(3 * HEADS), act(HEADS), act(HEADS)],
        out_specs=(act(HEADS),) * 4 + (mat, mat), semantics=("parallel",), vmem_limit=VMEM_LIMIT)(qkv, g, beta)


SCAN_CHUNKS_PER_STEP = 8


def _deltanet_scan(u, w, qg, kg, attn, g, *, name, after=None):
    t = u.shape[1]
    n_chunks = t // CHUNK
    per = SCAN_CHUNKS_PER_STEP

    def body(u_ref, w_ref, qg_ref, kg_ref, attn_ref, g_ref, o_ref, vn_ref, st_ref, s_ref):
        @pl.when(pl.program_id(0) == 0)
        def _():
            s_ref[...] = jnp.zeros_like(s_ref)

        for j in range(per):
            rows = _chunk_rows(j)
            s = [s_ref[h] for h in range(HEADS)]
            vn = [u_ref[h, rows, :] - _dot(w_ref[h, rows, :], s[h], NN) for h in range(HEADS)]
            o = [_dot(qg_ref[h, rows, :], s[h], NN) + _dot(attn_ref[j, h], vn[h], NN) for h in range(HEADS)]
            eg = [jnp.exp(_colsum(g_ref[h, rows, :])) for h in range(HEADS)]
            for h in range(HEADS):
                st_ref[j, h] = s[h]
                s_ref[h] = s[h] * eg[h] + _dot(kg_ref[h, rows, :], vn[h], TN)
                o_ref[h, rows, :] = o[h]
                vn_ref[h, rows, :] = vn[h]

    act = pl.BlockSpec((HEADS, per * CHUNK, LANE), lambda n: (0, n, 0))
    out = jax.ShapeDtypeStruct((HEADS, t, LANE), F32)
    return _pcall(
        body, name=name, out_shape=(out, out, jax.ShapeDtypeStruct((n_chunks, HEADS, LANE, LANE), F32)), grid=(n_chunks // per,),
        in_specs=[act] * 4 + [pl.BlockSpec((per, HEADS, CHUNK, CHUNK), lambda n: (n, 0, 0, 0)), act],
        out_specs=(act, act, pl.BlockSpec((per, HEADS, LANE, LANE), lambda n: (n, 0, 0, 0))),
        scratch_shapes=[pltpu.VMEM((HEADS, LANE, LANE), F32)], semantics=("arbitrary",), after=after)(u, w, qg, kg, attn, g)


def _deltanet_bscan(w, qg, kg, attn, g, do, *, name):
    t = w.shape[1]
    n_chunks = t // CHUNK
    per = SCAN_CHUNKS_PER_STEP
    steps = n_chunks // per

    def body(w_ref, qg_ref, kg_ref, attn_ref, g_ref, do_ref, dvn_ref, dsn_ref, ds_ref):
        @pl.when(pl.program_id(0) == 0)
        def _():
            ds_ref[...] = jnp.zeros_like(ds_ref)

        for j in reversed(range(per)):
            rows = _chunk_rows(j)
            dsn = [ds_ref[h] for h in range(HEADS)]
            dov = [do_ref[h, rows, :] for h in range(HEADS)]
            dvn = [_dot(attn_ref[j, h], dov[h], TN) + _dot(kg_ref[h, rows, :], dsn[h], NN) for h in range(HEADS)]
            eg = [jnp.exp(_colsum(g_ref[h, rows, :])) for h in range(HEADS)]
            for h in range(HEADS):
                dsn_ref[j, h] = dsn[h]
                ds_ref[h] = _dot(qg_ref[h, rows, :], dov[h], TN) + eg[h] * dsn[h] - _dot(w_ref[h, rows, :], dvn[h], TN)
                dvn_ref[h, rows, :] = dvn[h]

    act = pl.BlockSpec((HEADS, per * CHUNK, LANE), lambda n: (0, steps - 1 - n, 0))
    return _pcall(
        body, name=name,
        out_shape=(jax.ShapeDtypeStruct((HEADS, t, LANE), F32), jax.ShapeDtypeStruct((n_chunks, HEADS, LANE, LANE), F32)),
        grid=(steps,),
        in_specs=[act] * 3 + [pl.BlockSpec((per, HEADS, CHUNK, CHUNK), lambda n: (steps - 1 - n, 0, 0, 0)), act, act],
        out_specs=(act, pl.BlockSpec((per, HEADS, LANE, LANE), lambda n: (steps - 1 - n, 0, 0, 0))),
        scratch_shapes=[pltpu.VMEM((HEADS, LANE, LANE), F32)], semantics=("arbitrary",))(w, qg, kg, attn, g, do)


def _sum_all(x):
    return jnp.sum(jnp.sum(x, axis=1, keepdims=True), axis=0, keepdims=True)


def _rowsum(x):
    return jnp.sum(x, axis=1, keepdims=True)


def _deltanet_post(qkv, g, beta, tmats, states, dstates, do, dvn, vn, *, name):
    t = qkv.shape[1]
    n_chunks = t // CHUNK
    per = CHUNKS_PER_STEP
    probs = [(j, h) for j in range(per) for h in range(HEADS)]

    def body(qkv_ref, g_ref, b_ref, tm_ref, st_ref, dsn_ref, do_ref, dvn_ref, vn_ref, dqkv_ref, dg_ref, db_ref):
        incl, strict, _ = _chunk_masks()
        ones = jnp.ones((CHUNK, LANE), BF16)
        last_row = lax.broadcasted_iota(jnp.int32, (CHUNK, LANE), 0) == CHUNK - 1
        z = lambda f, *cols: [f(*a) for a in zip(*cols)]
        q = [qkv_ref[h, _chunk_rows(j), :] for j, h in probs]
        k = [qkv_ref[HEADS + h, _chunk_rows(j), :] for j, h in probs]
        v = [qkv_ref[2 * HEADS + h, _chunk_rows(j), :] for j, h in probs]
        bv = [b_ref[h, _chunk_rows(j), :] for j, h in probs]
        dov = [do_ref[h, _chunk_rows(j), :] for j, h in probs]
        dvn_ = [dvn_ref[h, _chunk_rows(j), :] for j, h in probs]
        vn_ = [vn_ref[h, _chunk_rows(j), :] for j, h in probs]
        tm = [tm_ref[j, h] for j, h in probs]
        s = [st_ref[j, h] for j, h in probs]
        dsn = [dsn_ref[j, h] for j, h in probs]
        dec = [_chunk_decay(g_ref[h, _chunk_rows(j), :], incl) for j, h in probs]
        decay = [d[2] for d in dec]
        egc = [jnp.exp(d[0]) for d in dec]
        ekg = [jnp.exp(d[1] - d[0]) for d in dec]
        kb = z(lambda a, b: a * b, k, bv)
        vb = z(lambda a, b: a * b, v, bv)
        kbg = z(lambda a, b: a * b, kb, egc)
        qg = z(lambda a, b: a * b, q, egc)
        kg = z(lambda a, b: a * b, k, ekg)
        kk = z(lambda a, b: _dot(a, b, NT), kb, k)
        qk = z(lambda a, b: _dot(a, b, NT), q, k)
        dattn = z(lambda a, b: jnp.where(incl, _dot(a, b, NT), 0.0), dov, vn_)
        dqg = z(lambda a, b: _dot(a, b, NT), dov, s)
        dkg = z(lambda a, b: _dot(a, b, NT), vn_, dsn)
        dglast = z(lambda a, b, c, d, e: _sum_all(a * b) * jnp.exp(e[1]) + _sum_all(c * d), s, dsn, dkg, kg, dec)
        dw = z(lambda a, b: -_dot(a, b, NT), dvn_, s)
        dtm = z(lambda a, b, c, d: _dot(a, b, NT) + _dot(c, d, NT), dvn_, vb, dw, kbg)
        dvb = z(lambda a, b: _dot(a, b, TN), tm, dvn_)
        dkbg = z(lambda a, b: _dot(a, b, TN), tm, dw)
        dlow = z(lambda a, b: jnp.where(strict, -_dot(_dot(a, b, TN), a, NT), 0.0), tm, dtm)
        dkk = z(lambda a, b: a * b, dlow, decay)
        dqk = z(lambda a, b: a * b, dattn, decay)
        dkb = z(lambda a, b, c, d: _dot(a, b, NN) + c * d, dkk, k, dkbg, egc)
        dk = z(lambda a, b, c, d, e, f, g_, h_: _dot(a, b, TN) + _dot(c, d, TN) + e * f + g_ * h_, dkk, kb, dqk, q, dkg, ekg, dkb, bv)
        dq = z(lambda a, b, c, d: _dot(a, b, NN) + c * d, dqk, k, dqg, egc)
        m = z(lambda a, b, c, d, e: (a * b + c * d) * e, dlow, kk, dattn, qk, decay)
        mcol = [_dot(mh, ones, TN) + _dot(ml, ones, TN) for mh, ml in (_split(a) for a in m)]
        for i, (j, h) in enumerate(probs):
            rows = _chunk_rows(j)
            dqkv_ref[h, rows, :] = dq[i]
            dqkv_ref[HEADS + h, rows, :] = dk[i]
            dqkv_ref[2 * HEADS + h, rows, :] = dvb[i] * bv[i]
            db_ref[h, rows, :] = jnp.broadcast_to(_rowsum(dkb[i] * k[i] + dvb[i] * v[i]), (CHUNK, LANE))
            dgc = (_rowsum(dqg[i] * qg[i] + dkbg[i] * kbg[i] - dkg[i] * kg[i]) + _rowsum(m[i]) - mcol[i]
                   + jnp.where(last_row, dglast[i], 0.0))
            dg_ref[h, rows, :] = _suffix_sum_rows(dgc)

    act = lambda heads: pl.BlockSpec((heads, per * CHUNK, LANE), lambda n: (0, n, 0))
    mat = lambda d: pl.BlockSpec((per, HEADS, d, d), lambda n: (n, 0, 0, 0))
    out = jax.ShapeDtypeStruct((HEADS, t, LANE), F32)
    return _pcall(
        body, name=name, out_shape=(jax.ShapeDtypeStruct((3 * HEADS, t, LANE), F32), out, out), grid=(n_chunks // per,),
        in_specs=[act(3 * HEADS), act(HEADS), act(HEADS), mat(CHUNK), mat(LANE), mat(LANE), act(HEADS), act(HEADS), act(HEADS)],
        out_specs=(act(3 * HEADS), act(HEADS), act(HEADS)), semantics=("parallel",),
        vmem_limit=VMEM_LIMIT)(qkv, g, beta, tmats, states, dstates, do, dvn, vn)


ANY = pl.BlockSpec(memory_space=pl.ANY)
PEERS = N_DEV - 1


def _all_gather(arrays, *, name):
    n = len(arrays)

    def body(*refs):
        ins, outs = refs[:n], refs[n:2 * n]
        send_sems, recv_sems, local_sems = refs[2 * n:]
        x, y, c = lax.axis_index("x"), lax.axis_index("y"), lax.axis_index("c")
        me, sibling = (x, y, c), (x, y, 1 - c)
        chips = [(1 - x, y), (x, 1 - y), (1 - x, 1 - y)]

        def copy(a, k, block, to, src=None):
            dst = outs[a].at[4 * block[0] + 2 * block[1] + block[2]]
            return pltpu.make_async_remote_copy(src_ref=dst if src is None else src, dst_ref=dst, send_sem=send_sems.at[a * PEERS + k],
                                                recv_sem=recv_sems.at[a * PEERS + k], device_id=to, device_id_type=MESH)

        local = [pltpu.make_async_copy(ins[a], outs[a].at[4 * x + 2 * y + c], local_sems.at[a]) for a in range(n)]
        for cp in local:
            cp.start()
        first = []
        for a in range(n):
            first += [copy(a, 1 + j, me, (*chip, c), src=ins[a]) for j, chip in enumerate(chips)]
            first.append(copy(a, 0, me, sibling, src=ins[a]))
        for cp in first:
            cp.start()
        passed = []
        for a in range(n):
            for j, chip in enumerate(chips):
                copy(a, 1 + j, (*chip, c), me).wait_recv()
                fwd = copy(a, 4 + j, (*chip, c), sibling)
                fwd.start()
                passed.append(fwd)
        for a in range(n):
            copy(a, 0, sibling, me).wait_recv()
            for j, chip in enumerate(chips):
                copy(a, 4 + j, (*chip, 1 - c), me).wait_recv()
        for cp in first + passed:
            cp.wait_send()
        for cp in local:
            cp.wait()

    return _pcall(body, name=name, out_shape=tuple(jax.ShapeDtypeStruct((N_DEV,) + a.shape, a.dtype) for a in arrays),
                  in_specs=[ANY] * n, out_specs=(ANY,) * n,
                  scratch_shapes=[pltpu.SemaphoreType.DMA((n * PEERS,)), pltpu.SemaphoreType.DMA((n * PEERS,)),
                                  pltpu.SemaphoreType.DMA((n,))])(*arrays)


CHIPS = 4


def _pair_exchange(arrays, *, name):
    n = len(arrays)

    def body(*refs):
        ins, outs = refs[:n], refs[n:2 * n]
        send_sems, recv_sems = refs[2 * n:]
        x, y, c = lax.axis_index("x"), lax.axis_index("y"), lax.axis_index("c")
        copies = []
        for a in range(n):
            for q in range(CHIPS):
                cp = pltpu.make_async_remote_copy(src_ref=ins[a].at[2 * q + 1 - c], dst_ref=outs[a].at[q],
                                                  send_sem=send_sems.at[a * CHIPS + q], recv_sem=recv_sems.at[a * CHIPS + q],
                                                  device_id=(x, y, 1 - c), device_id_type=MESH)
                cp.start()
                copies.append(cp)
        for cp in copies:
            cp.wait()

    return _pcall(body, name=name, out_shape=tuple(jax.ShapeDtypeStruct((CHIPS,) + a.shape[1:], a.dtype) for a in arrays),
                  in_specs=[ANY] * n, out_specs=(ANY,) * n,
                  scratch_shapes=[pltpu.SemaphoreType.DMA((n * CHIPS,)), pltpu.SemaphoreType.DMA((n * CHIPS,))])(*arrays)


def _pair_add(blocks, theirs, *, name):
    _, r, c_ = blocks.shape
    tr = _tile(r, 512, 16)

    def body(mine_ref, theirs_ref, o_ref):
        core = lax.axis_index("c")
        own = jnp.where(core == 0, mine_ref[0, 0].astype(F32), mine_ref[0, 1].astype(F32))
        o_ref[0] = (own + theirs_ref[0].astype(F32)).astype(o_ref.dtype)

    spec = pl.BlockSpec((1, tr, c_), lambda q, i: (q, i, 0))
    return _pcall(body, name=name, out_shape=jax.ShapeDtypeStruct(theirs.shape, theirs.dtype), grid=(CHIPS, r // tr),
                  in_specs=[pl.BlockSpec((1, 2, tr, c_), lambda q, i: (q, 0, i, 0)), spec], out_specs=spec,
                  semantics=("parallel", "parallel"), vmem_limit=VMEM_LIMIT)(blocks.reshape(CHIPS, 2, r, c_), theirs)


HBM = pl.BlockSpec(memory_space=pltpu.HBM)
SEM = pl.BlockSpec(memory_space=pltpu.SEMAPHORE)
EFFECT = pltpu.SideEffectType.DATAFLOW_SIDE_EFFECTING


GATHER, CHIP_GATHER, CHIP_SCATTER = "gather", "chip_gather", "chip_scatter"
PEERS_OF = {GATHER: N_DEV - 1, CHIP_GATHER: CHIPS - 1, CHIP_SCATTER: CHIPS - 1}


def _direct_copies(srcs, lands, send_sems, recv_sems, local_sems, kind):
    x, y, c = lax.axis_index("x"), lax.axis_index("y"), lax.axis_index("c")
    peers = PEERS_OF[kind]
    mine = 2 * x + y if kind == CHIP_SCATTER else 4 * x + 2 * y + c
    copies = []
    for a, (src, land) in enumerate(zip(srcs, lands)):
        copies.append(pltpu.make_async_copy(src.at[mine] if kind == CHIP_SCATTER else src, land.at[mine], local_sems.at[a]))
        for k in range(1, peers + 1):
            bits = k if kind == GATHER else 2 * k
            px = 1 - x if bits & 4 else x
            py = 1 - y if bits & 2 else y
            pc = 1 - c if bits & 1 else c
            copies.append(pltpu.make_async_remote_copy(
                src_ref=src.at[2 * px + py] if kind == CHIP_SCATTER else src, dst_ref=land.at[mine],
                send_sem=send_sems.at[a * peers + k - 1], recv_sem=recv_sems.at[a * peers + k - 1],
                device_id=(px, py, pc), device_id_type=MESH))
    return copies


def _pair_swap(arrays, *, name):
    n = len(arrays)

    def body(*refs):
        mine, zones = refs[:n], refs[n:2 * n]
        send_sems, recv_sems = refs[2 * n:]
        x, y, c = lax.axis_index("x"), lax.axis_index("y"), lax.axis_index("c")
        copies = []
        for a in range(n):
            for q in range(CHIPS):
                copies.append(pltpu.make_async_remote_copy(
                    src_ref=mine[a].at[2 * q + c], dst_ref=zones[a].at[2 * q + c], send_sem=send_sems.at[a * CHIPS + q],
                    recv_sem=recv_sems.at[a * CHIPS + q], device_id=(x, y, 1 - c), device_id_type=MESH))
        for cp in copies:
            cp.start()
        for cp in copies:
            cp.wait()

    return _pcall(body, name=name, out_shape=tuple(jax.ShapeDtypeStruct(a.shape, a.dtype) for a in arrays),
                  in_specs=[ANY] * n, out_specs=(ANY,) * n, input_output_aliases={i: i for i in range(n)},
                  scratch_shapes=[pltpu.SemaphoreType.DMA((n * CHIPS,)), pltpu.SemaphoreType.DMA((n * CHIPS,))])(*arrays)


def _exchange_start(groups, kind, *, name, after=None):
    srcs = [s for group in groups for s in group]
    n = len(srcs)
    sizes = [len(group) for group in groups]
    starts = [sum(sizes[:g]) for g in range(len(groups))]
    land_shapes = [s.shape if kind == CHIP_SCATTER else (N_DEV,) + s.shape for s in srcs]
    peers = PEERS_OF[kind]
    extra = [] if after is None else [after]

    def body(*refs):
        srcs_, lands = refs[:n], refs[n:2 * n]
        token = refs[-1]
        sem_refs = refs[2 * n + len(extra):]
        for g, (at, size) in enumerate(zip(starts, sizes)):
            send_sems, recv_sems, local_sems = sem_refs[3 * g:3 * g + 3]
            for cp in _direct_copies(srcs_[at:at + size], lands[at:at + size], send_sems, recv_sems, local_sems, kind):
                cp.start()
        token[...] = jnp.zeros_like(token)

    sems = tuple(t for size in sizes for t in (pltpu.SemaphoreType.DMA((size * peers,)), pltpu.SemaphoreType.DMA((size * peers,)),
                                               pltpu.SemaphoreType.DMA((size,))))
    thru = tuple(pltpu.HBM(s.shape, s.dtype) for s in srcs) + tuple(pltpu.HBM(shp, s.dtype) for shp, s in zip(land_shapes, srcs))
    ins = [pltpu.with_memory_space_constraint(s, pltpu.HBM) for s in srcs]
    ins += [pltpu.with_memory_space_constraint(lax.empty(shp, s.dtype), pltpu.HBM) for shp, s in zip(land_shapes, srcs)]
    out = pl.pallas_call(
        body, name=name, out_shape=sems + thru + (jax.ShapeDtypeStruct((SUBLANE, LANE), F32),),
        in_specs=[HBM] * (2 * n) + [ANY] * len(extra),
        out_specs=(SEM,) * len(sems) + (HBM,) * (2 * n) + (pl.BlockSpec(memory_space=pltpu.VMEM),),
        input_output_aliases={i: len(sems) + i for i in range(2 * n)},
        compiler_params=pltpu.CompilerParams(has_side_effects=EFFECT))(*ins, *extra)
    arrays = out[len(sems):-1]
    started = [tuple(out[3 * g:3 * g + 3]) + tuple(arrays[at:at + size]) + tuple(arrays[n + at:n + at + size])
               for g, (at, size) in enumerate(zip(starts, sizes))]
    return started, out[-1]


def _exchange_wait(started, after, kind, *, name):
    n = (len(started) - 3) // 2
    sems, arrays = started[:3], started[3:]

    def body(*refs):
        srcs_, lands = refs[:n], refs[n:2 * n]
        send_sems, recv_sems, local_sems = refs[2 * n:2 * n + 3]
        for cp in _direct_copies(srcs_, lands, send_sems, recv_sems, local_sems, kind):
            cp.wait()

    out = pl.pallas_call(
        body, name=name, out_shape=tuple(pltpu.HBM(a.shape, a.dtype) for a in arrays),
        in_specs=[HBM] * (2 * n) + [SEM] * 3 + [ANY], out_specs=(HBM,) * (2 * n),
        input_output_aliases={i: i for i in range(2 * n)},
        compiler_params=pltpu.CompilerParams(has_side_effects=EFFECT))(*arrays, *sems, after)
    return out[n:]


def _adamw_reduce(w, parts, m, v, *, name, after=None):
    layers, r, c = w.shape
    assert len(parts) == layers
    senders = parts[0].shape[0]
    tr = _tile(r, 512, 16)
    tiles = r // tr
    bc1 = 1.0 - ADAM_B1 ** ADAM_STEP
    bc2 = 1.0 - ADAM_B2 ** ADAM_STEP

    def body(w_ref, *rest):
        p_refs = rest[:layers]
        m_ref, v_ref, g_ref, d_ref, nm_ref, nv_ref = rest[layers:]

        def update(p_ref):
            g = p_ref[0, :, pl.ds(0, c)].astype(F32)
            for s in range(1, senders):
                g = g + p_ref[s, :, pl.ds(0, c)].astype(F32)
            nm = ADAM_B1 * m_ref[0] + (1.0 - ADAM_B1) * g
            nv = ADAM_B2 * v_ref[0] + (1.0 - ADAM_B2) * (g * g)
            g_ref[0] = g
            nm_ref[0] = nm
            nv_ref[0] = nv
            d_ref[0] = -ADAM_LR * ((nm / bc1) / (jnp.sqrt(nv / bc2) + ADAM_EPS) + ADAM_WD * w_ref[0])

        for layer in range(layers):
            pl.when(pl.program_id(0) == layer)(functools.partial(update, p_refs[layer]))

    def part_spec(layer, shape):
        rest = 0 if layer > 0 else tiles - 1
        return pl.BlockSpec((senders, tr, shape[2]), lambda l, i: (0, jnp.where(l == layer, i, rest), 0))

    spec = pl.BlockSpec((1, tr, c), lambda l, i: (l, i, 0))
    out = jax.ShapeDtypeStruct((layers, r, c), F32)
    return _pcall(body, name=name, out_shape=(out,) * 4, grid=(layers, tiles),
                  in_specs=[spec] + [part_spec(layer, p.shape) for layer, p in enumerate(parts)] + [spec, spec],
                  out_specs=(spec,) * 4, semantics=("arbitrary", "arbitrary"), vmem_limit=VMEM_LIMIT, after=after)(w, *parts, m, v)


def _pool_windows():
    return jnp.repeat(jnp.asarray(POOL_WINDOWS, F32), POOL_DIM // len(POOL_WINDOWS))[None, :]


def _block_diag_pairs(pool_w):
    z = jnp.zeros_like(pool_w[0])
    return jnp.stack([jnp.block([[pool_w[2 * b], z], [z, pool_w[2 * b + 1]]]) for b in range(2)])


def _pad_lanes(vec):
    return jnp.zeros((1, LANE), F32).at[0, :vec.shape[0]].set(vec)


FF_SHARD = D_FF // N_DEV
FF_BLOCK = 384


def _layer_fwd(x, p_i, wt, fetch):
    wt = {**wt, **fetch(0, x)}
    proj, h1 = _matmul(x, wt["w_in"], "nt", norm_g=wt["norm1_g"], name="mm_in")
    qkv = _qkv_prep_fwd(proj, wt["conv_qkv"], name="qkv_prep_fwd")
    g, beta = _gates_fwd(proj, wt["a_log"], wt["dt_bias"], name="gates_fwd")
    u, w, qg, kg, attn, tmats = _deltanet_prep(qkv, g, beta, name="deltanet_prep")
    wt.update(fetch(1, u))
    o, vn, states = _deltanet_scan(u, w, qg, kg, attn, g, name="deltanet_scan", after=wt.get("behind"))
    o_a = _apost_fwd(o, proj, wt["onorm_g"], name="apost_fwd")
    o_b = _pool_fwd(proj, wt["pool_win"], wt["pool_wbd"], wt["pool_scale"], name="pool_fwd")
    o_c = _sconv_fwd(proj, wt["sconv_w"], name="sconv_fwd")
    mixed = jnp.concatenate([o_a, o_b, o_c], axis=1)
    x1 = _matmul(mixed, wt["w_out"], "nn", res=x, name="mm_out")
    wt.update(fetch(2, x1))
    ff, gate, up, h2 = _swiglu_fwd(x1, wt["norm2_g"], wt["w_gate"], wt["w_up"], name="swiglu_fwd")
    wt.update(fetch(3, ff))
    x2 = _matmul(ff, wt["w_down"], "nn", res=x1, name="mm_down")
    wt.update(fetch(4, x2))
    x3, pgl, pp = _ple_fwd(x2, p_i, wt["ple_gate"], wt["ple_proj"], name="ple_fwd")
    saved = dict(x=x, h1=h1, proj=proj, qkv=qkv, g=g, beta=beta, o=o, states=states, tmats=tmats, mixed=mixed, x1=x1, h2=h2,
                 gate=gate, up=up, ff=ff, x2=x2, pgl=pgl, pp=pp, p=p_i, w=w, qg=qg, kg=kg, attn=attn, vn=vn, wt=wt)
    return x3, saved


def _col_blocks(g):
    a = g.shape[0]
    return jnp.transpose(g.reshape(a, N_DEV, -1), (1, 0, 2))


def _cols_joined(blocks):
    return jnp.transpose(blocks, (1, 0, 2)).reshape(blocks.shape[1], -1)


def _layer_bwd(dx3, sv, emit, after=None):
    gr, big = {}, {}
    wt = sv["wt"]
    rows = D_MODEL // N_DEV
    dpgl, dpp = _ple_bwd(dx3, sv["pgl"], sv["pp"], name="ple_bwd", after=after)
    big["ple_proj"] = _matmul(sv["p"], dpp, "tn", out_blocked=(N_DEV, rows), out_dtype=BF16, name="mm_dplep")
    big["ple_gate"] = _matmul(sv["x2"], dpgl, "tn", out_dtype=BF16, name="mm_dpleg").reshape(N_DEV, rows, D_MODEL)
    dx2 = _matmul(dpgl, wt["ple_gate"], "nt", res=dx3, name="mm_dx2")
    big["w_down"] = _matmul(sv["ff"], dx2, "tn", out_dtype=BF16, name="mm_ddown").reshape(N_DEV, FF_BLOCK, D_MODEL)
    dgate, dup = _swiglu_bwd(dx2, wt["w_down"], sv["gate"], sv["up"], name="swiglu_bwd", after=emit(0, big))
    big["w_gate"] = _matmul(dgate, sv["h2"], "tn", out_dtype=BF16, name="mm_dgate").reshape(N_DEV, FF_BLOCK, D_MODEL)
    big["w_up"] = _matmul(dup, sv["h2"], "tn", out_dtype=BF16, name="mm_dup").reshape(N_DEV, FF_BLOCK, D_MODEL)
    dx1, gr["norm2_g"] = _matmul_norm_bwd(dgate, wt["w_gate"], sv["x1"], wt["norm2_g"], dx2, more=(dup, wt["w_up"]), name="mm_dh2")
    big["w_out"] = _matmul(sv["mixed"], dx1, "tn", out_dtype=BF16, name="mm_dout").reshape(N_DEV, rows, D_MODEL)
    dmixed = _matmul(dx1, wt["w_out"], "nt", name="mm_dmixed", after=emit(1, big))
    proj = sv["proj"]
    dcb, dcc, dch, dsconv = _sconv_bwd(proj, wt["sconv_w"], dmixed, name="sconv_bwd")
    big["sconv_w"] = _col_blocks(dsconv)
    dhp, dwbd, gr["pool_scale"] = _pool_bwd(proj, wt["pool_win"], wt["pool_wbd"], wt["pool_scale"], dmixed, name="pool_bwd")
    half = LANE // 2
    gr["pool_w"] = jnp.stack([dwbd[0, :half, :half], dwbd[0, half:, half:], dwbd[1, :half, :half], dwbd[1, half:, half:]])
    do, dz, gr["onorm_g"] = _apost_bwd(sv["o"], proj, wt["onorm_g"], dmixed, name="apost_bwd")
    dvn, dstates = _deltanet_bscan(sv["w"], sv["qg"], sv["kg"], sv["attn"], sv["g"], do, name="deltanet_bscan")
    dqkv_h, dg, dbeta = _deltanet_post(sv["qkv"], sv["g"], sv["beta"], sv["tmats"], sv["states"], dstates, do, dvn, sv["vn"],
                                       name="deltanet_post")
    dab, dalog, ddtb = _gates_bwd(proj, wt["a_log"], wt["dt_bias"], dg, dbeta, name="gates_bwd")
    gr["a_log"], gr["dt_bias"] = dalog[0, :HEADS], ddtb[0, :HEADS]
    dqkv, dconv = _qkv_prep_bwd(proj, wt["conv_qkv"], dqkv_h, name="qkv_prep_bwd")
    big["conv_qkv"] = _col_blocks(dconv)
    dproj = jnp.concatenate([dqkv, dz, dab, dhp, dcb, dcc, dch], axis=1)
    dwin = _matmul(dproj, sv["h1"], "tn", out_dtype=BF16, name="mm_din")
    big["w_in"] = jnp.concatenate([dwin[:AB_COL + 2 * HEADS], dwin[AB_COL + LANE:]], axis=0).reshape(N_DEV, -1, D_MODEL)
    dx, gr["norm1_g"] = _matmul_norm_bwd(dproj, wt["w_in"], sv["x"], wt["norm1_g"], dx1, name="mm_dh1", after=emit(2, big))
    return dx, gr


FETCH_GROUPS = (("w_in", "conv_qkv", "sconv_w"), ("w_out",), ("w_gate", "w_up"), ("w_down",), ("ple_gate", "ple_proj"))
EMIT_GROUPS = (("ple_proj", "ple_gate", "w_down"), ("w_gate", "w_up", "w_out"), ("w_in", "conv_qkv", "sconv_w"))


def _small_weights(w, i):
    return dict(
        norm1_g=w["norm1_g"][i][None], norm2_g=w["norm2_g"][i][None], onorm_g=w["onorm_g"][i][None],
        a_log=_pad_lanes(w["a_log"][i]), dt_bias=_pad_lanes(w["dt_bias"][i]),
        pool_scale=w["pool_scale"][i][None], pool_win=_pool_windows(), pool_wbd=_block_diag_pairs(w["pool_w"][i]))


def _as_read(name, gathered):
    if name == "w_in":
        rows = gathered[:, :D_IN // N_DEV].reshape(-1, D_MODEL)
        return jnp.concatenate([rows[:AB_COL + 2 * HEADS], jnp.zeros((LANE - 2 * HEADS, D_MODEL), BF16),
                                rows[AB_COL + 2 * HEADS:]], axis=0)
    if name in ("conv_qkv", "sconv_w"):
        return _cols_joined(gathered)
    if name == "ple_proj":
        return gathered
    return gathered.reshape(-1, D_MODEL)


SHARDED = ("w_in", "w_gate", "w_up", "w_down", "w_out", "ple_gate", "ple_proj", "conv_qkv", "sconv_w")
SMALL = ("norm1_g", "a_log", "dt_bias", "onorm_g", "pool_w", "pool_scale", "norm2_g", "final_g")
SLAB_COLS = 1024


def _payload(name, shard):
    if name in ("conv_qkv", "sconv_w"):
        return shard
    out = shard.astype(BF16)
    if name in ("w_gate", "w_up", "w_down"):
        out = jnp.pad(out, ((0, FF_BLOCK - FF_SHARD), (0, 0)))
    if name == "w_in":
        out = jnp.pad(out, ((0, -out.shape[0] % (2 * SUBLANE)), (0, 0)))
    return out


TRANSPOSED = ("w_in", "w_gate", "w_up")


def _ff_rows(t):
    return jnp.transpose(t, (0, 2, 1))


def _slab_rows(shape):
    size = 1
    for s in shape:
        size *= s
    return SUBLANE * -(-size // (SUBLANE * SLAB_COLS))


def _pack_slab(parts, extra_row):
    rows = []
    for name in SMALL:
        flat = parts[name].reshape(-1)
        nrow = _slab_rows(parts[name].shape)
        rows.append(jnp.pad(flat, (0, nrow * SLAB_COLS - flat.shape[0])).reshape(nrow, SLAB_COLS))
    rows.append(jnp.pad(extra_row, ((0, SUBLANE - 1), (0, 0))))
    return jnp.concatenate(rows, axis=0)


def _unpack_slab(slab, shapes):
    out, row = {}, 0
    for name in SMALL:
        size = 1
        for s in shapes[name]:
            size *= s
        out[name] = slab[row:row + _slab_rows(shapes[name])].reshape(-1)[:size].reshape(shapes[name])
        row += _slab_rows(shapes[name])
    return out, row


def kernel(x, p, norm1_g, w_in, conv_qkv, a_log, dt_bias, onorm_g, pool_w, pool_scale, sconv_w, w_out, norm2_g, w_gate, w_up, w_down, ple_proj, ple_gate, final_g, loss_target, m_norm1_g, m_w_in, m_conv_qkv, m_a_log, m_dt_bias, m_onorm_g, m_pool_w, m_pool_scale, m_sconv_w, m_w_out, m_norm2_g, m_w_gate, m_w_up, m_w_down, m_ple_proj, m_ple_gate, m_final_g, v_norm1_g, v_w_in, v_conv_qkv, v_a_log, v_dt_bias, v_onorm_g, v_pool_w, v_pool_scale, v_sconv_w, v_w_out, v_norm2_g, v_w_gate, v_w_up, v_w_down, v_ple_proj, v_ple_gate, v_final_g):
    names = ["norm1_g", "w_in", "conv_qkv", "a_log", "dt_bias", "onorm_g", "pool_w", "pool_scale", "sconv_w", "w_out", "norm2_g",
             "w_gate", "w_up", "w_down", "ple_proj", "ple_gate", "final_g"]
    w = dict(zip(names, [norm1_g, w_in, conv_qkv, a_log, dt_bias, onorm_g, pool_w, pool_scale, sconv_w, w_out, norm2_g, w_gate, w_up,
                         w_down, ple_proj, ple_gate, final_g]))
    m = dict(zip(names, [m_norm1_g, m_w_in, m_conv_qkv, m_a_log, m_dt_bias, m_onorm_g, m_pool_w, m_pool_scale, m_sconv_w, m_w_out,
                         m_norm2_g, m_w_gate, m_w_up, m_w_down, m_ple_proj, m_ple_gate, m_final_g]))
    v = dict(zip(names, [v_norm1_g, v_w_in, v_conv_qkv, v_a_log, v_dt_bias, v_onorm_g, v_pool_w, v_pool_scale, v_sconv_w, v_w_out,
                         v_norm2_g, v_w_gate, v_w_up, v_w_down, v_ple_proj, v_ple_gate, v_final_g]))
    w.update({k: _ff_rows(w[k]) for k in TRANSPOSED})

    first, rest = FETCH_GROUPS[0], tuple(k for members in FETCH_GROUPS[1:] for k in members)
    gathered = dict(zip(first, _all_gather([_payload(k, w[k][0]) for k in first], name="all_gather_weights")))
    (flying0,), token = _exchange_start([[_payload(k, w[k][0]) for k in rest]], CHIP_GATHER, name="gather_start_0",
                                        after=gathered[first[0]])
    replicated = [_small_weights(w, i) for i in range(DEPTH)]
    replicated[0]["norm1_g"] = replicated[0]["norm1_g"] + token[0, 0]
    for group in (m, v):
        group.update({k: _ff_rows(group[k] + token[0, 0]) for k in TRANSPOSED})
    flying = {}

    def fetch(i, group, after):
        extra = {}
        if i == 0 and group == 1:
            landed = _exchange_wait(flying0, after, CHIP_GATHER, name="gather_wait_0")
            gathered.update(zip(rest, _pair_swap(landed, name="pair_swap")))
            (flying["first"],), token = _exchange_start([[_payload(k, w[k][1]) for k in first]], CHIP_GATHER,
                                                        name="gather_start_1_first", after=gathered[rest[0]])
            extra = {"behind": token}
        if i == 1 and group == 0:
            landed = _exchange_wait(flying["first"], after, CHIP_GATHER, name="gather_wait_1_first")
            gathered.update(zip(first, _pair_swap(landed, name="pair_swap")))
            (flying["rest"],), token = _exchange_start([[_payload(k, w[k][1]) for k in rest]], CHIP_GATHER,
                                                       name="gather_start_1_rest", after=gathered[first[0]])
            extra = {"norm1_g": replicated[1]["norm1_g"] + token[0, 0]}
        if i == 1 and group == 1:
            landed = _exchange_wait(flying["rest"], after, CHIP_GATHER, name="gather_wait_1_rest")
            gathered.update(zip(rest, _pair_swap(landed, name="pair_swap")))
        return {**{k: _as_read(k, gathered[k]) for k in FETCH_GROUPS[group]}, **extra}

    def reduce_scatter_start(members, blocks, tag):
        mine = [blocks[k] for k in members]
        theirs = _pair_exchange(mine, name="pair_exchange")
        sums = [_pair_add(a, b, name="pair_add") for a, b in zip(mine, theirs)]
        (started,), token = _exchange_start([sums], CHIP_SCATTER, name="exchange_start_" + tag)
        return started, token

    h, saved0 = _layer_fwd(x[0], p[0, 0], replicated[0], functools.partial(fetch, 0))
    h, saved1 = _layer_fwd(h, p[1, 0], replicated[1], functools.partial(fetch, 1))
    dx, dgf, loss_part = _loss_head(h, final_g[None], loss_target[0], name="loss_head")
    small, big1, flying0 = [None] * DEPTH, {}, []
    dx, small[1] = _layer_bwd(dx, saved1, lambda group, blocks: big1.update({k: blocks[k] for k in EMIT_GROUPS[group]}))
    flying1, token = reduce_scatter_start(SHARDED, big1, "1")

    def emit(group, blocks):
        started, token = reduce_scatter_start(EMIT_GROUPS[group], blocks, f"0_{group}")
        flying0.append(started)
        return token

    dx, small[0] = _layer_bwd(dx, saved0, emit, after=token)
    received = [{}, dict(zip(SHARDED, _exchange_wait(flying1, dx, CHIP_SCATTER, name="exchange_wait_1")))]
    for group, members in enumerate(EMIT_GROUPS):
        received[0].update(zip(members, _exchange_wait(flying0[group], dx, CHIP_SCATTER, name=f"exchange_wait_0_{group}")))

    grads = {k: jnp.stack([small[i][k] for i in range(DEPTH)]) for k in small[0]}
    grads = {k: g[:, 0] if k in ("norm1_g", "norm2_g", "onorm_g", "pool_scale") else g for k, g in grads.items()}
    grads["final_g"] = dgf[0]
    loss_row = jnp.pad(loss_part, ((0, 0), (0, SLAB_COLS - LANE)))
    (small_flying,), token = _exchange_start([[_pack_slab(grads, loss_row)]], GATHER, name="small_gather_start")

    out_g, out_d, out_m, out_v = {}, {}, {}, {}
    for k in SHARDED:
        out_g[k], out_d[k], out_m[k], out_v[k] = _adamw_reduce(w[k], [received[i][k] for i in range(DEPTH)], m[k], v[k],
                                                                name="adamw_" + k, after=token)
    behind_all = jnp.stack([out_v[k][0, 0, 0] for k in SHARDED])
    (small_parts,) = _exchange_wait(small_flying, behind_all, GATHER, name="small_gather_wait")
    zero_row = jnp.zeros((1, SLAB_COLS), F32)
    slabs = _adamw_reduce(_pack_slab(w, zero_row)[None], [small_parts], _pack_slab(m, zero_row)[None],
                          _pack_slab(v, zero_row)[None], name="adamw_small")
    slabs = [s[0] for s in slabs]
    shapes = {k: w[k].shape for k in SMALL}
    for dst, slab in zip((out_g, out_d, out_m, out_v), slabs):
        vals, _ = _unpack_slab(slab, shapes)
        dst.update(vals)
    _, loss_at = _unpack_slab(slabs[0], shapes)
    loss = slabs[0][loss_at, 0]
    for group in (out_g, out_d, out_m, out_v):
        group.update({k: _ff_rows(group[k]) for k in TRANSPOSED})

    return (loss, dx[None], *[out_g[k] for k in names], *[out_d[k] for k in names], *[out_m[k] for k in names],
            *[out_v[k] for k in names])
```

```python
import functools

import jax
import jax.numpy as jnp
from jax import lax
from jax.experimental import pallas as pl
from jax.experimental.pallas import tpu as pltpu

F32 = jnp.float32
BF16 = jnp.bfloat16

D_MODEL = 1024
DEPTH = 2
PLE_DIM = 256
EPS = 1e-6
HEAD_DIM = 128
HEADS = 4
A_DIM = HEADS * HEAD_DIM
QKV_TAPS = 4
CHUNK = 64
POOL_WINDOWS = (2, 4, 8, 16)
POOL_DIM = 256
CONV_DIM = 256
CONV_TAPS = 3
D_FF = 2816
D_IN = 3080
AB_COL = 2048
N_DEV = 8

ADAM_LR = 0.001
ADAM_B1 = 0.9
ADAM_B2 = 0.999
ADAM_EPS = 1e-08
ADAM_WD = 0.01
ADAM_STEP = 10

LANE = 128
SUBLANE = 8
VMEM_BYTES_V7X = 64 * 1024 * 1024
VMEM_LIMIT = VMEM_BYTES_V7X * 3 // 4

NN = ((1,), (0,))
NT = ((1,), (1,))
TN = ((0,), (0,))
MESH = pl.DeviceIdType.MESH


def _dot(a, b, dims):
    return lax.dot_general(a.astype(BF16), b.astype(BF16), (dims, ((), ())), preferred_element_type=F32)


def _pcall(body, *, name, out_shape, grid=(), in_specs=None, out_specs=None, scratch_shapes=(), semantics=None,
           vmem_limit=None, after=None, **kw):
    params = {}
    if semantics is not None:
        params["dimension_semantics"] = semantics
    if vmem_limit is not None:
        params["vmem_limit_bytes"] = vmem_limit
    if after is not None:
        n_in, inner = len(in_specs), body
        body = lambda *refs: inner(*refs[:n_in], *refs[n_in + 1:])
        in_specs = list(in_specs) + [pl.BlockSpec(after.shape, lambda *_: (0,) * after.ndim)]
    call = pl.pallas_call(
        body, name=name, out_shape=out_shape, grid=grid, in_specs=in_specs, out_specs=out_specs,
        scratch_shapes=list(scratch_shapes), compiler_params=pltpu.CompilerParams(**params), **kw)
    return call if after is None else (lambda *args: call(*args, after))


def _sigmoid(x):
    return 1.0 / (1.0 + jnp.exp(-x))


def _softplus(x):
    return jnp.maximum(x, 0.0) + jnp.log(1.0 + jnp.exp(-jnp.abs(x)))


def _tile(n, cap, mult):
    if n <= cap:
        return n
    best = None
    for t in range(mult, cap + 1, mult):
        if n % t == 0:
            best = t
    assert best is not None, (n, cap, mult)
    return best


ROWS_PER_STEP = 512
NARROW_RESULT = 1024
COLS_PER_DOT = 640


def _resident(weight):
    return pl.BlockSpec(weight.shape, lambda i: (0,) * weight.ndim, pipeline_mode=pl.Buffered(1))


def _matmul_rows(a, b, mode, *, name, res=None, out_dtype=F32, b_blocked=False, after=None, norm_g=None):
    m, k = a.shape
    if b_blocked:
        nb, _, bw = b.shape
        n = nb * bw if mode == "nn" else b.shape[1]
    else:
        n = b.shape[1] if mode == "nn" else b.shape[0]
    tm = _tile(m, ROWS_PER_STEP if n > NARROW_RESULT else 2 * ROWS_PER_STEP, 16)
    cn = bw if (b_blocked and mode == "nn") else _tile(n, COLS_PER_DOT, LANE)
    has_res = res is not None
    normed = norm_g is not None

    def body(*refs):
        a_ref, b_ref = refs[0], refs[1]
        g_ref = refs[2] if normed else None
        res_ref = refs[2 + normed] if has_res else None
        o_ref = refs[2 + normed + has_res]
        if normed:
            av = _rms_normed(a_ref[...], g_ref[...])
            refs[3 + normed + has_res][...] = av
        elif not (b_blocked and mode == "nt"):
            av = a_ref[...].astype(BF16)
        for j in range(n // cn):
            cols = pl.ds(j * cn, cn)
            if mode == "nn":
                part = _dot(av, b_ref[j] if b_blocked else b_ref[:, cols], NN)
            elif not b_blocked:
                part = _dot(av, b_ref[cols, :], NT)
            else:
                part = None
                for s in range(nb):
                    term = _dot(a_ref[:, pl.ds(s * bw, bw)], b_ref[s, cols, :], NT)
                    part = term if part is None else part + term
            if has_res:
                part = part + res_ref[:, cols]
            o_ref[:, cols] = part.astype(o_ref.dtype)

    row = lambda width: pl.BlockSpec((tm, width), lambda i: (i, 0))
    whole = _resident(b)
    ins = [a, b] + ([norm_g] if normed else []) + ([res] if has_res else [])
    specs = [row(k), whole] + ([pl.BlockSpec((1, k), lambda i: (0, 0))] if normed else []) + ([row(n)] if has_res else [])
    out = jax.ShapeDtypeStruct((m, n), out_dtype)
    return _pcall(body, name=name, out_shape=(out, jax.ShapeDtypeStruct((m, k), BF16)) if normed else out, grid=(m // tm,),
                  in_specs=specs, out_specs=(row(n), row(k)) if normed else row(n), semantics=("parallel",),
                  vmem_limit=VMEM_LIMIT, after=after)(*ins)


def _matmul_norm_bwd(a, b, x, g, dres, *, name, more=None, after=None):
    m, k = a.shape
    d = b.shape[1]
    tm = _tile(m, ROWS_PER_STEP, 16)
    cn = _tile(d, COLS_PER_DOT, LANE)
    pairs = 1 if more is None else 2

    def body(*refs):
        a_ref, b_ref, x_ref, g_ref, dres_ref = refs[:5]
        dx_ref, dg_ref = refs[3 + 2 * pairs], refs[4 + 2 * pairs]
        av = a_ref[...].astype(BF16)
        for j in range(d // cn):
            cols = pl.ds(j * cn, cn)
            part = _dot(av, b_ref[:, cols], NN)
            if more is not None:
                part = part + _dot(refs[5][...], refs[6][:, cols], NN)
            dx_ref[:, cols] = part
        dhv = dx_ref[...]
        xv = x_ref[...]
        r = lax.rsqrt(jnp.mean(xv * xv, axis=-1, keepdims=True) + EPS)
        xhat = xv * r
        dhg = dhv * g_ref[...]
        dx_ref[...] = dres_ref[...] + r * (dhg - xhat * jnp.mean(dhg * xhat, axis=-1, keepdims=True))
        part_g = jnp.sum(dhv * xhat, axis=0, keepdims=True)

        @pl.when(pl.program_id(0) == 0)
        def _():
            dg_ref[...] = part_g

        @pl.when(pl.program_id(0) > 0)
        def _():
            dg_ref[...] += part_g

    row = lambda width: pl.BlockSpec((tm, width), lambda i: (i, 0))
    vec = pl.BlockSpec((1, d), lambda i: (0, 0))
    ins = [a, b, x, g, dres] + (list(more) if more is not None else [])
    specs = [row(k), _resident(b), row(d), vec, row(d)] + ([row(more[0].shape[1]), _resident(more[1])] if more is not None else [])
    return _pcall(body, name=name, out_shape=(jax.ShapeDtypeStruct((m, d), F32), jax.ShapeDtypeStruct((1, d), F32)),
                  grid=(m // tm,), in_specs=specs, out_specs=(row(d), vec), semantics=("arbitrary",), vmem_limit=VMEM_LIMIT,
                  after=after)(*ins)


def _rms_normed(xv, gv):
    return (xv * lax.rsqrt(jnp.mean(xv * xv, axis=-1, keepdims=True) + EPS) * gv).astype(BF16)


def _matmul(a, b, mode, *, name, res=None, out_dtype=F32, b_blocked=False, out_blocked=None, after=None, norm_g=None):
    if mode != "tn":
        return _matmul_rows(a, b, mode, name=name, res=res, out_dtype=out_dtype, b_blocked=b_blocked, after=after, norm_g=norm_g)
    assert res is None and not b_blocked and after is None and norm_g is None
    (t, m), (t2, n) = a.shape, b.shape
    assert t == t2, (a.shape, b.shape)
    tm = _tile(m, 1024, LANE)
    tn = _tile(n, NARROW_RESULT if n <= NARROW_RESULT else COLS_PER_DOT, LANE)
    if out_blocked is not None:
        assert out_blocked[0] * out_blocked[1] == n
        tn = out_blocked[1]

    def body(a_ref, b_ref, o_ref):
        part = _dot(a_ref[...], b_ref[...], TN).astype(o_ref.dtype)
        if out_blocked is None:
            o_ref[...] = part
        else:
            o_ref[0] = part

    o_spec = (pl.BlockSpec((tm, tn), lambda i, j: (i, j)) if out_blocked is None
              else pl.BlockSpec((1, tm, tn), lambda i, j: (j, i, 0)))
    o_shape = (m, n) if out_blocked is None else (out_blocked[0], m, out_blocked[1])
    return _pcall(body, name=name, out_shape=jax.ShapeDtypeStruct(o_shape, out_dtype), grid=(m // tm, n // tn),
                  in_specs=[pl.BlockSpec((t, tm), lambda i, j: (0, i)), pl.BlockSpec((t, tn), lambda i, j: (0, j))],
                  out_specs=o_spec, semantics=("parallel", "parallel"), vmem_limit=VMEM_LIMIT)(a, b)


ROW_TILE = 512


def _rows(t, width, idx=0):
    return pl.BlockSpec((ROW_TILE, width), lambda i: (i, idx))


def _vec(width):
    return pl.BlockSpec((1, width), lambda i: (0, 0))


def _swiglu_fwd(x, norm_g, w_gate, w_up, *, name):
    t, k = x.shape
    f = w_gate.shape[0]
    tm = _tile(t, ROWS_PER_STEP, 16)
    cn = _tile(f, COLS_PER_DOT, LANE)

    def body(x_ref, g_ref, wg_ref, wu_ref, ff_ref, gate_ref, up_ref, h_ref):
        hv = _rms_normed(x_ref[...], g_ref[...])
        h_ref[...] = hv
        for j in range(f // cn):
            cols = pl.ds(j * cn, cn)
            gv = _dot(hv, wg_ref[cols, :], NT)
            uv = _dot(hv, wu_ref[cols, :], NT)
            gate_ref[:, cols] = gv.astype(BF16)
            up_ref[:, cols] = uv.astype(BF16)
            ff_ref[:, cols] = (gv * _sigmoid(gv) * uv).astype(BF16)

    row = lambda width: pl.BlockSpec((tm, width), lambda i: (i, 0))
    out = jax.ShapeDtypeStruct((t, f), BF16)
    return _pcall(body, name=name, out_shape=(out,) * 3 + (jax.ShapeDtypeStruct((t, k), BF16),), grid=(t // tm,),
                  in_specs=[row(k), pl.BlockSpec((1, k), lambda i: (0, 0)), _resident(w_gate), _resident(w_up)],
                  out_specs=(row(f),) * 3 + (row(k),), semantics=("parallel",), vmem_limit=VMEM_LIMIT)(x, norm_g, w_gate, w_up)


def _swiglu_bwd(dx2, w_down, gate, up, *, name, after=None):
    t, d = dx2.shape
    f = w_down.shape[0]
    tm = _tile(t, ROWS_PER_STEP, 16)
    cn = _tile(f, COLS_PER_DOT, LANE)

    def body(dx_ref, w_ref, gate_ref, up_ref, dgate_ref, dup_ref):
        dxv = dx_ref[...].astype(BF16)
        for j in range(f // cn):
            cols = pl.ds(j * cn, cn)
            dffv = _dot(dxv, w_ref[cols, :], NT)
            gv = gate_ref[:, cols].astype(F32)
            sig = _sigmoid(gv)
            dgate_ref[:, cols] = (dffv * up_ref[:, cols].astype(F32) * sig * (1.0 + gv * (1.0 - sig))).astype(BF16)
            dup_ref[:, cols] = (dffv * gv * sig).astype(BF16)

    row = lambda width: pl.BlockSpec((tm, width), lambda i: (i, 0))
    out = jax.ShapeDtypeStruct((t, f), BF16)
    return _pcall(body, name=name, out_shape=(out, out), grid=(t // tm,), in_specs=[row(d), _resident(w_down), row(f), row(f)],
                  out_specs=(row(f), row(f)), semantics=("parallel",), vmem_limit=VMEM_LIMIT, after=after)(dx2, w_down, gate, up)


def _ple_fwd(x2, p, w_gate, w_proj, *, name):
    t, d = x2.shape
    nb, pdim, bw = w_proj.shape
    tm = _tile(t, ROWS_PER_STEP, 16)
    cn = _tile(d, COLS_PER_DOT, LANE)

    def body(x_ref, p_ref, wg_ref, wp_ref, x3_ref, pgl_ref, pp_ref):
        xb = x_ref[...].astype(BF16)
        pb = p_ref[...].astype(BF16)
        per = cn // bw
        for c in range(d // cn):
            cols = pl.ds(c * cn, cn)
            pgl = _dot(xb, wg_ref[:, cols], NN)
            pp = jnp.concatenate([_dot(pb, wp_ref[c * per + j], NN) for j in range(per)], axis=1)
            pgl_ref[:, cols] = pgl
            pp_ref[:, cols] = pp
            x3_ref[:, cols] = x_ref[:, cols] + _sigmoid(pgl) * pp

    row = lambda width: pl.BlockSpec((tm, width), lambda i: (i, 0))
    out = jax.ShapeDtypeStruct((t, d), F32)
    return _pcall(body, name=name, out_shape=(out,) * 3, grid=(t // tm,),
                  in_specs=[row(d), row(pdim), _resident(w_gate), _resident(w_proj)], out_specs=(row(d),) * 3,
                  semantics=("parallel",), vmem_limit=VMEM_LIMIT)(x2, p, w_gate, w_proj)


def _ple_bwd(dx3, pgl, pp, *, name, after=None):
    t, d = dx3.shape

    def body(dx_ref, pgl_ref, pp_ref, dpgl_ref, dpp_ref):
        dxv = dx_ref[...]
        sig = _sigmoid(pgl_ref[...])
        dpp_ref[...] = (dxv * sig).astype(BF16)
        dpgl_ref[...] = (dxv * pp_ref[...] * sig * (1.0 - sig)).astype(BF16)

    return _pcall(body, name=name, out_shape=(jax.ShapeDtypeStruct((t, d), BF16),) * 2, grid=(t // ROW_TILE,),
                  in_specs=[_rows(t, d)] * 3, out_specs=(_rows(t, d),) * 2, semantics=("parallel",), after=after)(dx3, pgl, pp)


def _loss_head(x3, g, target, *, name):
    t, d = x3.shape

    def body(x_ref, g_ref, t_ref, dx_ref, dg_ref, loss_ref):
        xv = x_ref[...]
        r = lax.rsqrt(jnp.mean(xv * xv, axis=-1, keepdims=True) + EPS)
        xhat = xv * r
        gv = g_ref[...]
        err = xhat * gv - t_ref[...]
        row_loss = jnp.sum(err * err, axis=-1, keepdims=True) * (0.5 / d)
        lpart = jnp.broadcast_to(jnp.sum(row_loss, axis=0, keepdims=True), (1, LANE))
        dy = err * (1.0 / d)
        dyg = dy * gv
        dx_ref[...] = r * (dyg - xhat * jnp.mean(dyg * xhat, axis=-1, keepdims=True))
        gpart = jnp.sum(dy * xhat, axis=0, keepdims=True)

        @pl.when(pl.program_id(0) == 0)
        def _():
            dg_ref[...] = gpart
            loss_ref[...] = lpart

        @pl.when(pl.program_id(0) > 0)
        def _():
            dg_ref[...] += gpart
            loss_ref[...] += lpart

    return _pcall(body, name=name,
                  out_shape=(jax.ShapeDtypeStruct((t, d), F32), jax.ShapeDtypeStruct((1, d), F32), jax.ShapeDtypeStruct((1, LANE), F32)),
                  grid=(t // ROW_TILE,), in_specs=[_rows(t, d), _vec(d), _rows(t, d)],
                  out_specs=(_rows(t, d), _vec(d), _vec(LANE)), semantics=("arbitrary",))(x3, g, target)


def _shift_down(x, d):
    if d == 0:
        return x
    row = lax.broadcasted_iota(jnp.int32, x.shape, 0)
    return jnp.where(row >= d, pltpu.roll(x, d, 0), 0.0)


def _shift_up(x, d):
    if d == 0:
        return x
    t = x.shape[0]
    row = lax.broadcasted_iota(jnp.int32, x.shape, 0)
    return jnp.where(row < t - d, pltpu.roll(x, t - d, 0), 0.0)


def _colsum(x):
    return jnp.sum(x, axis=0, keepdims=True)


def _col(t, idx_fn):
    return pl.BlockSpec((t, LANE), idx_fn)


def _conv_fwd(x, w_ref, taps):
    acc = None
    for j in range(taps):
        term = w_ref[pl.ds(j, 1), :] * _shift_down(x, taps - 1 - j)
        acc = term if acc is None else acc + term
    return acc


def _conv_bwd(x, dy, w_ref, dw_ref, taps):
    dx = None
    for j in range(taps):
        term = w_ref[pl.ds(j, 1), :] * _shift_up(dy, taps - 1 - j)
        dx = term if dx is None else dx + term
        dw_ref[pl.ds(j, 1), :] = _colsum(dy * _shift_down(x, taps - 1 - j))
    return dx


def _qkv_prep_fwd(proj, conv_w, *, name):
    t = proj.shape[0]
    scale = HEAD_DIM ** -0.5

    def body(x_ref, w_ref, o_ref):
        j = pl.program_id(0)
        c = _conv_fwd(x_ref[...], w_ref, QKV_TAPS)
        s = c * _sigmoid(c)
        r = lax.rsqrt(jnp.sum(s * s, axis=-1, keepdims=True) + EPS)
        f = jnp.where(j < 2 * HEADS, r, 1.0) * jnp.where(j < HEADS, scale, 1.0)
        o_ref[0] = s * f

    return _pcall(body, name=name, out_shape=jax.ShapeDtypeStruct((3 * HEADS, t, LANE), F32), grid=(3 * HEADS,),
                  in_specs=[_col(t, lambda j: (0, j)), pl.BlockSpec((QKV_TAPS, LANE), lambda j: (0, j))],
                  out_specs=pl.BlockSpec((1, t, LANE), lambda j: (j, 0, 0)), semantics=("parallel",),
                  vmem_limit=VMEM_LIMIT)(proj, conv_w)


def _qkv_prep_bwd(proj, conv_w, dqkv, *, name):
    t = proj.shape[0]
    scale = HEAD_DIM ** -0.5

    def body(x_ref, w_ref, d_ref, dx_ref, dw_ref):
        j = pl.program_id(0)
        xv = x_ref[...]
        c = _conv_fwd(xv, w_ref, QKV_TAPS)
        sig = _sigmoid(c)
        s = c * sig
        r = lax.rsqrt(jnp.sum(s * s, axis=-1, keepdims=True) + EPS)
        n0 = s * r
        dv = d_ref[0]
        dn0 = dv * jnp.where(j < HEADS, scale, 1.0)
        ds_norm = r * (dn0 - n0 * jnp.sum(dn0 * n0, axis=-1, keepdims=True))
        ds = jnp.where(j < 2 * HEADS, ds_norm, dv)
        dc = ds * sig * (1.0 + c * (1.0 - sig))
        dx_ref[...] = _conv_bwd(xv, dc, w_ref, dw_ref, QKV_TAPS).astype(BF16)

    return _pcall(body, name=name,
                  out_shape=(jax.ShapeDtypeStruct((t, 3 * A_DIM), BF16), jax.ShapeDtypeStruct((QKV_TAPS, 3 * A_DIM), F32)),
                  grid=(3 * HEADS,),
                  in_specs=[_col(t, lambda j: (0, j)), pl.BlockSpec((QKV_TAPS, LANE), lambda j: (0, j)),
                            pl.BlockSpec((1, t, LANE), lambda j: (j, 0, 0))],
                  out_specs=(_col(t, lambda j: (0, j)), pl.BlockSpec((QKV_TAPS, LANE), lambda j: (0, j))),
                  semantics=("parallel",), vmem_limit=VMEM_LIMIT)(proj, conv_w, dqkv)


def _lane_pick(x, lane_idx, lane):
    return jnp.broadcast_to(jnp.sum(jnp.where(lane == lane_idx, x, 0.0), axis=-1, keepdims=True), x.shape)


def _gates_fwd(proj, alog, dtb, *, name):
    t = proj.shape[0]

    def body(x_ref, alog_ref, dtb_ref, g_ref, b_ref):
        xv = x_ref[...]
        lane = lax.broadcasted_iota(jnp.int32, xv.shape, 1)
        gall = -jnp.exp(alog_ref[...]) * _softplus(xv + dtb_ref[...])
        ball = _sigmoid(xv)
        for h in range(HEADS):
            g_ref[h] = _lane_pick(gall, h, lane)
            b_ref[h] = _lane_pick(ball, HEADS + h, lane)

    out = jax.ShapeDtypeStruct((HEADS, t, LANE), F32)
    whole = pl.BlockSpec((HEADS, t, LANE), lambda i: (0, 0, 0))
    return _pcall(body, name=name, out_shape=(out, out), grid=(1,),
                  in_specs=[_col(t, lambda i: (0, AB_COL // LANE)), _vec(LANE), _vec(LANE)], out_specs=(whole, whole),
                  semantics=("arbitrary",), vmem_limit=VMEM_LIMIT)(proj, alog, dtb)


def _gates_bwd(proj, alog, dtb, dg, dbeta, *, name):
    t = proj.shape[0]

    def body(x_ref, alog_ref, dtb_ref, dg_ref, db_ref, dab_ref, dalog_ref, ddtb_ref):
        xv = x_ref[...]
        lane = lax.broadcasted_iota(jnp.int32, xv.shape, 1)
        lane1 = lax.broadcasted_iota(jnp.int32, (1, LANE), 1)
        z = xv + dtb_ref[...]
        nea = -jnp.exp(alog_ref[...])
        da_f = nea * _sigmoid(z)
        g_f = nea * _softplus(z)
        ball = _sigmoid(xv)
        db_f = ball * (1.0 - ball)
        dab = jnp.zeros_like(xv)
        dalog = jnp.zeros((1, LANE), F32)
        for h in range(HEADS):
            dgh = dg_ref[h]
            dab = dab + jnp.where(lane == h, dgh * da_f, 0.0) + jnp.where(lane == HEADS + h, db_ref[h] * db_f, 0.0)
            dalog = dalog + jnp.where(lane1 == h, _colsum(dgh * g_f), 0.0)
        dab_ref[...] = dab.astype(BF16)
        dalog_ref[...] = dalog
        ddtb_ref[...] = jnp.where(lane1 < HEADS, _colsum(dab), 0.0)

    whole = pl.BlockSpec((HEADS, t, LANE), lambda i: (0, 0, 0))
    vec = jax.ShapeDtypeStruct((1, LANE), F32)
    return _pcall(body, name=name, out_shape=(jax.ShapeDtypeStruct((t, LANE), BF16), vec, vec), grid=(1,),
                  in_specs=[_col(t, lambda i: (0, AB_COL // LANE)), _vec(LANE), _vec(LANE), whole, whole],
                  out_specs=(_col(t, lambda i: (0, 0)), _vec(LANE), _vec(LANE)), semantics=("arbitrary",),
                  vmem_limit=VMEM_LIMIT)(proj, alog, dtb, dg, dbeta)


Z_COL = 3 * A_DIM // LANE


def _apost_fwd(o, proj, gn, *, name):
    t = proj.shape[0]

    def body(o_ref, z_ref, gn_ref, y_ref):
        ov = o_ref[0]
        z = z_ref[...]
        r = lax.rsqrt(jnp.mean(ov * ov, axis=-1, keepdims=True) + EPS)
        y_ref[...] = (ov * r * gn_ref[...] * (z * _sigmoid(z))).astype(BF16)

    return _pcall(body, name=name, out_shape=jax.ShapeDtypeStruct((t, A_DIM), BF16), grid=(HEADS,),
                  in_specs=[pl.BlockSpec((1, t, LANE), lambda h: (h, 0, 0)), _col(t, lambda h: (0, Z_COL + h)),
                            pl.BlockSpec((1, LANE), lambda h: (0, 0))],
                  out_specs=_col(t, lambda h: (0, h)), semantics=("parallel",), vmem_limit=VMEM_LIMIT)(o, proj, gn)


def _apost_bwd(o, proj, gn, dmixed, *, name):
    t = proj.shape[0]

    def body(o_ref, z_ref, gn_ref, d_ref, do_ref, dz_ref, dgn_ref):
        ov = o_ref[0]
        z = z_ref[...]
        gnv = gn_ref[...]
        dv = d_ref[...]
        r = lax.rsqrt(jnp.mean(ov * ov, axis=-1, keepdims=True) + EPS)
        ohat = ov * r
        sig = _sigmoid(z)
        dy = dv * (z * sig)
        dz_ref[...] = (dv * ohat * gnv * sig * (1.0 + z * (1.0 - sig))).astype(BF16)
        dyo = dy * gnv
        do_ref[0] = r * (dyo - ohat * jnp.mean(dyo * ohat, axis=-1, keepdims=True))
        part = _colsum(dy * ohat)

        @pl.when(pl.program_id(0) == 0)
        def _():
            dgn_ref[...] = part

        @pl.when(pl.program_id(0) > 0)
        def _():
            dgn_ref[...] += part

    return _pcall(body, name=name,
                  out_shape=(jax.ShapeDtypeStruct((HEADS, t, LANE), F32), jax.ShapeDtypeStruct((t, A_DIM), BF16),
                             jax.ShapeDtypeStruct((1, LANE), F32)),
                  grid=(HEADS,),
                  in_specs=[pl.BlockSpec((1, t, LANE), lambda h: (h, 0, 0)), _col(t, lambda h: (0, Z_COL + h)),
                            pl.BlockSpec((1, LANE), lambda h: (0, 0)), _col(t, lambda h: (0, h))],
                  out_specs=(pl.BlockSpec((1, t, LANE), lambda h: (h, 0, 0)), _col(t, lambda h: (0, h)),
                             pl.BlockSpec((1, LANE), lambda h: (0, 0))),
                  semantics=("arbitrary",), vmem_limit=VMEM_LIMIT)(o, proj, gn, dmixed)


POOL_COL = (AB_COL + LANE) // LANE
CB_COL = POOL_COL + POOL_DIM // LANE
CC_COL = CB_COL + CONV_DIM // LANE
CH_COL = CC_COL + CONV_DIM // LANE
MAX_WIN_LOG2 = 4


def _window_sums(x, shift):
    sums = []
    cur = x
    for k in range(MAX_WIN_LOG2):
        cur = cur + shift(cur, 1 << k)
        sums.append(cur)
    return sums


def _pick_window(sums, win):
    out = sums[-1]
    for k in range(MAX_WIN_LOG2 - 2, -1, -1):
        out = jnp.where(win == float(2 << k), sums[k], out)
    return out


def _pool_counts(shape, win):
    row = lax.broadcasted_iota(jnp.int32, shape, 0).astype(F32)
    return jnp.minimum(row + 1.0, win)


def _pool_fwd(proj, win, wbd, scale, *, name):
    t = proj.shape[0]

    def body(x_ref, win_ref, w_ref, s_ref, y_ref):
        xv = x_ref[...]
        winv = win_ref[...]
        pooled = _pick_window(_window_sums(xv, _shift_down), winv) / _pool_counts(xv.shape, winv) - xv
        y_ref[...] = (_dot(pooled, w_ref[0], NN) * s_ref[...]).astype(BF16)

    nb = POOL_DIM // LANE
    vec = pl.BlockSpec((1, LANE), lambda b: (0, b))
    return _pcall(body, name=name, out_shape=jax.ShapeDtypeStruct((t, POOL_DIM), BF16), grid=(nb,),
                  in_specs=[_col(t, lambda b: (0, POOL_COL + b)), vec, pl.BlockSpec((1, LANE, LANE), lambda b: (b, 0, 0)), vec],
                  out_specs=_col(t, lambda b: (0, b)), semantics=("parallel",), vmem_limit=VMEM_LIMIT)(proj, win, wbd, scale)


def _pool_bwd(proj, win, wbd, scale, dmixed, *, name):
    t = proj.shape[0]

    def body(x_ref, win_ref, w_ref, s_ref, d_ref, dx_ref, dw_ref, ds_ref):
        xv = x_ref[...]
        winv = win_ref[...]
        cnt = _pool_counts(xv.shape, winv)
        pooled = _pick_window(_window_sums(xv, _shift_down), winv) / cnt - xv
        dv = d_ref[...]
        ds_ref[...] = _colsum(dv * _dot(pooled, w_ref[0], NN))
        dy0 = dv * s_ref[...]
        dw_ref[0] = _dot(pooled, dy0, TN)
        dpooled = _dot(dy0, w_ref[0], NT)
        dmean = dpooled / cnt
        dx_ref[...] = (_pick_window(_window_sums(dmean, _shift_up), winv) - dpooled).astype(BF16)

    nb = POOL_DIM // LANE
    vec = pl.BlockSpec((1, LANE), lambda b: (0, b))
    mat = pl.BlockSpec((1, LANE, LANE), lambda b: (b, 0, 0))
    first = A_DIM // LANE
    return _pcall(body, name=name,
                  out_shape=(jax.ShapeDtypeStruct((t, POOL_DIM), BF16), jax.ShapeDtypeStruct((nb, LANE, LANE), F32),
                             jax.ShapeDtypeStruct((1, POOL_DIM), F32)),
                  grid=(nb,),
                  in_specs=[_col(t, lambda b: (0, POOL_COL + b)), vec, mat, vec, _col(t, lambda b: (0, first + b))],
                  out_specs=(_col(t, lambda b: (0, b)), mat, vec), semantics=("parallel",),
                  vmem_limit=VMEM_LIMIT)(proj, win, wbd, scale, dmixed)


def _sconv_fwd(proj, w, *, name):
    t = proj.shape[0]

    def body(cb_ref, cc_ref, ch_ref, w_ref, y_ref):
        y_ref[...] = (cb_ref[...] * _conv_fwd(cc_ref[...] * ch_ref[...], w_ref, CONV_TAPS)).astype(BF16)

    nb = CONV_DIM // LANE
    return _pcall(body, name=name, out_shape=jax.ShapeDtypeStruct((t, CONV_DIM), BF16), grid=(nb,),
                  in_specs=[_col(t, lambda b: (0, CB_COL + b)), _col(t, lambda b: (0, CC_COL + b)),
                            _col(t, lambda b: (0, CH_COL + b)), pl.BlockSpec((CONV_TAPS, LANE), lambda b: (0, b))],
                  out_specs=_col(t, lambda b: (0, b)), semantics=("parallel",), vmem_limit=VMEM_LIMIT)(proj, proj, proj, w)


def _sconv_bwd(proj, w, dmixed, *, name):
    t = proj.shape[0]

    def body(cb_ref, cc_ref, ch_ref, w_ref, d_ref, dcb_ref, dcc_ref, dch_ref, dw_ref):
        cc = cc_ref[...]
        ch = ch_ref[...]
        u = cc * ch
        dv = d_ref[...]
        dcb_ref[...] = (dv * _conv_fwd(u, w_ref, CONV_TAPS)).astype(BF16)
        du = _conv_bwd(u, dv * cb_ref[...], w_ref, dw_ref, CONV_TAPS)
        dcc_ref[...] = (du * ch).astype(BF16)
        dch_ref[...] = (du * cc).astype(BF16)

    nb = CONV_DIM // LANE
    first = (A_DIM + POOL_DIM) // LANE
    act = jax.ShapeDtypeStruct((t, CONV_DIM), BF16)
    wspec = pl.BlockSpec((CONV_TAPS, LANE), lambda b: (0, b))
    ospec = _col(t, lambda b: (0, b))
    return _pcall(body, name=name, out_shape=(act, act, act, jax.ShapeDtypeStruct((CONV_TAPS, CONV_DIM), F32)), grid=(nb,),
                  in_specs=[_col(t, lambda b: (0, CB_COL + b)), _col(t, lambda b: (0, CC_COL + b)),
                            _col(t, lambda b: (0, CH_COL + b)), wspec, _col(t, lambda b: (0, first + b))],
                  out_specs=(ospec, ospec, ospec, wspec), semantics=("parallel",),
                  vmem_limit=VMEM_LIMIT)(proj, proj, proj, w, dmixed)


def _chunk_masks():
    r = lax.broadcasted_iota(jnp.int32, (CHUNK, CHUNK), 0)
    c = lax.broadcasted_iota(jnp.int32, (CHUNK, CHUNK), 1)
    return r >= c, r > c, jnp.where(r == c, 1.0, 0.0).astype(F32)


def _split(a):
    hi = a.astype(BF16)
    return hi, (a - hi.astype(F32)).astype(BF16)


def _dot_split(a, b, dims):
    (ah, al), (bh, bl) = a, b
    return _dot(ah, bh, dims) + _dot(ah, bl, dims) + _dot(al, bh, dims)


def _tri_inv(lows, eye):
    xs = [eye - low for low in lows]
    ps = [_split(low) for low in lows]
    ps = [_split(_dot_split(p, p, NN)) for p in ps]
    for i in range(5):
        xs = [x + _dot_split(_split(x), p, NN) for x, p in zip(xs, ps)]
        if i < 4:
            ps = [_split(_dot_split(p, p, NN)) for p in ps]
    return xs


def _prefix_sum_rows(x):
    for k in range(6):
        x = x + _shift_down(x, 1 << k)
    return x


def _suffix_sum_rows(x):
    for k in range(6):
        x = x + _shift_up(x, 1 << k)
    return x


def _chunk_decay(g, incl):
    gcb = _prefix_sum_rows(g)
    gtot = _colsum(g)
    col = gcb[:, :CHUNK]
    row = gcb.T[:CHUNK, :]
    decay = jnp.exp(jnp.where(incl, col - row, -1e30))
    return gcb, gtot, decay


CHUNKS_PER_STEP = 4


def _chunk_rows(j):
    return pl.ds(j * CHUNK, CHUNK)


def _deltanet_prep(qkv, g, beta, *, name):
    t = qkv.shape[1]
    n_chunks = t // CHUNK
    per = CHUNKS_PER_STEP
    probs = [(j, h) for j in range(per) for h in range(HEADS)]

    def body(qkv_ref, g_ref, b_ref, u_ref, w_ref, qg_ref, kg_ref, attn_ref, tm_ref):
        incl, strict, eye = _chunk_masks()
        q = [qkv_ref[h, _chunk_rows(j), :] for j, h in probs]
        k = [qkv_ref[HEADS + h, _chunk_rows(j), :] for j, h in probs]
        v = [qkv_ref[2 * HEADS + h, _chunk_rows(j), :] for j, h in probs]
        bv = [b_ref[h, _chunk_rows(j), :] for j, h in probs]
        dec = [_chunk_decay(g_ref[h, _chunk_rows(j), :], incl) for j, h in probs]
        kb = [a * b for a, b in zip(k, bv)]
        low = [jnp.where(strict, _dot(a, b, NT) * d[2], 0.0) for a, b, d in zip(kb, k, dec)]
        tm = _tri_inv(low, eye)
        egc = [jnp.exp(d[0]) for d in dec]
        u = [_dot(m, a * b, NN) for m, a, b in zip(tm, v, bv)]
        w = [_dot(m, a * e, NN) for m, a, e in zip(tm, kb, egc)]
        attn = [_dot(a, b, NT) * d[2] for a, b, d in zip(q, k, dec)]
        for i, (j, h) in enumerate(probs):
            rows = _chunk_rows(j)
            u_ref[h, rows, :] = u[i]
            w_ref[h, rows, :] = w[i].astype(BF16)
            qg_ref[h, rows, :] = (q[i] * egc[i]).astype(BF16)
            kg_ref[h, rows, :] = (k[i] * jnp.exp(dec[i][1] - dec[i][0])).astype(BF16)
            attn_ref[j, h] = attn[i].astype(BF16)
            tm_ref[j, h] = tm[i]

    act = lambda heads: pl.BlockSpec((heads, per * CHUNK, LANE), lambda n: (0, n, 0))
    mat = pl.BlockSpec((per, HEADS, CHUNK, CHUNK), lambda n: (n, 0, 0, 0))
    return _pcall(
        body, name=name,
        out_shape=(jax.ShapeDtypeStruct((HEADS, t, LANE), F32),) + (jax.ShapeDtypeStruct((HEADS, t, LANE), BF16),) * 3
        + (jax.ShapeDtypeStruct((n_chunks, HEADS, CHUNK, CHUNK), BF16), jax.ShapeDtypeStruct((n_chunks, HEADS, CHUNK, CHUNK), F32)),
        grid=(n_chunks // per,), in_specs=[act(3 * HEADS), act(HEADS), act(HEADS)],
        out_specs=(act(HEADS),) * 4 + (mat, mat), semantics=("parallel",), vmem_limit=VMEM_LIMIT)(qkv, g, beta)


SCAN_CHUNKS_PER_STEP = 8


def _deltanet_scan(u, w, qg, kg, attn, g, *, name, after=None):
    t = u.shape[1]
    n_chunks = t // CHUNK
    per = SCAN_CHUNKS_PER_STEP

    def body(u_ref, w_ref, qg_ref, kg_ref, attn_ref, g_ref, o_ref, vn_ref, st_ref, s_ref):
        @pl.when(pl.program_id(0) == 0)
        def _():
            s_ref[...] = jnp.zeros_like(s_ref)

        for j in range(per):
            rows = _chunk_rows(j)
            s = [s_ref[h] for h in range(HEADS)]
            vn = [u_ref[h, rows, :] - _dot(w_ref[h, rows, :], s[h], NN) for h in range(HEADS)]
            o = [_dot(qg_ref[h, rows, :], s[h], NN) + _dot(attn_ref[j, h], vn[h], NN) for h in range(HEADS)]
            eg = [jnp.exp(_colsum(g_ref[h, rows, :])) for h in range(HEADS)]
            for h in range(HEADS):
                st_ref[j, h] = s[h]
                s_ref[h] = s[h] * eg[h] + _dot(kg_ref[h, rows, :], vn[h], TN)
                o_ref[h, rows, :] = o[h]
                vn_ref[h, rows, :] = vn[h]

    act = pl.BlockSpec((HEADS, per * CHUNK, LANE), lambda n: (0, n, 0))
    out = jax.ShapeDtypeStruct((HEADS, t, LANE), F32)
    return _pcall(
        body, name=name, out_shape=(out, out, jax.ShapeDtypeStruct((n_chunks, HEADS, LANE, LANE), F32)), grid=(n_chunks // per,),
        in_specs=[act] * 4 + [pl.BlockSpec((per, HEADS, CHUNK, CHUNK), lambda n: (n, 0, 0, 0)), act],
        out_specs=(act, act, pl.BlockSpec((per, HEADS, LANE, LANE), lambda n: (n, 0, 0, 0))),
        scratch_shapes=[pltpu.VMEM((HEADS, LANE, LANE), F32)], semantics=("arbitrary",), after=after)(u, w, qg, kg, attn, g)


def _deltanet_bscan(w, qg, kg, attn, g, do, *, name):
    t = w.shape[1]
    n_chunks = t // CHUNK
    per = SCAN_CHUNKS_PER_STEP
    steps = n_chunks // per

    def body(w_ref, qg_ref, kg_ref, attn_ref, g_ref, do_ref, dvn_ref, dsn_ref, ds_ref):
        @pl.when(pl.program_id(0) == 0)
        def _():
            ds_ref[...] = jnp.zeros_like(ds_ref)

        for j in reversed(range(per)):
            rows = _chunk_rows(j)
            dsn = [ds_ref[h] for h in range(HEADS)]
            dov = [do_ref[h, rows, :] for h in range(HEADS)]
            dvn = [_dot(attn_ref[j, h], dov[h], TN) + _dot(kg_ref[h, rows, :], dsn[h], NN) for h in range(HEADS)]
            eg = [jnp.exp(_colsum(g_ref[h, rows, :])) for h in range(HEADS)]
            for h in range(HEADS):
                dsn_ref[j, h] = dsn[h]
                ds_ref[h] = _dot(qg_ref[h, rows, :], dov[h], TN) + eg[h] * dsn[h] - _dot(w_ref[h, rows, :], dvn[h], TN)
                dvn_ref[h, rows, :] = dvn[h]

    act = pl.BlockSpec((HEADS, per * CHUNK, LANE), lambda n: (0, steps - 1 - n, 0))
    return _pcall(
        body, name=name,
        out_shape=(jax.ShapeDtypeStruct((HEADS, t, LANE), F32), jax.ShapeDtypeStruct((n_chunks, HEADS, LANE, LANE), F32)),
        grid=(steps,),
        in_specs=[act] * 3 + [pl.BlockSpec((per, HEADS, CHUNK, CHUNK), lambda n: (steps - 1 - n, 0, 0, 0)), act, act],
        out_specs=(act, pl.BlockSpec((per, HEADS, LANE, LANE), lambda n: (steps - 1 - n, 0, 0, 0))),
        scratch_shapes=[pltpu.VMEM((HEADS, LANE, LANE), F32)], semantics=("arbitrary",))(w, qg, kg, attn, g, do)


def _sum_all(x):
    return jnp.sum(jnp.sum(x, axis=1, keepdims=True), axis=0, keepdims=True)


def _rowsum(x):
    return jnp.sum(x, axis=1, keepdims=True)


def _deltanet_post(qkv, g, beta, tmats, states, dstates, do, dvn, vn, *, name):
    t = qkv.shape[1]
    n_chunks = t // CHUNK
    per = CHUNKS_PER_STEP
    probs = [(j, h) for j in range(per) for h in range(HEADS)]

    def body(qkv_ref, g_ref, b_ref, tm_ref, st_ref, dsn_ref, do_ref, dvn_ref, vn_ref, dqkv_ref, dg_ref, db_ref):
        incl, strict, _ = _chunk_masks()
        ones = jnp.ones((CHUNK, LANE), BF16)
        last_row = lax.broadcasted_iota(jnp.int32, (CHUNK, LANE), 0) == CHUNK - 1
        z = lambda f, *cols: [f(*a) for a in zip(*cols)]
        q = [qkv_ref[h, _chunk_rows(j), :] for j, h in probs]
        k = [qkv_ref[HEADS + h, _chunk_rows(j), :] for j, h in probs]
        v = [qkv_ref[2 * HEADS + h, _chunk_rows(j), :] for j, h in probs]
        bv = [b_ref[h, _chunk_rows(j), :] for j, h in probs]
        dov = [do_ref[h, _chunk_rows(j), :] for j, h in probs]
        dvn_ = [dvn_ref[h, _chunk_rows(j), :] for j, h in probs]
        vn_ = [vn_ref[h, _chunk_rows(j), :] for j, h in probs]
        tm = [tm_ref[j, h] for j, h in probs]
        s = [st_ref[j, h] for j, h in probs]
        dsn = [dsn_ref[j, h] for j, h in probs]
        dec = [_chunk_decay(g_ref[h, _chunk_rows(j), :], incl) for j, h in probs]
        decay = [d[2] for d in dec]
        egc = [jnp.exp(d[0]) for d in dec]
        ekg = [jnp.exp(d[1] - d[0]) for d in dec]
        kb = z(lambda a, b: a * b, k, bv)
        vb = z(lambda a, b: a * b, v, bv)
        kbg = z(lambda a, b: a * b, kb, egc)
        qg = z(lambda a, b: a * b, q, egc)
        kg = z(lambda a, b: a * b, k, ekg)
        kk = z(lambda a, b: _dot(a, b, NT), kb, k)
        qk = z(lambda a, b: _dot(a, b, NT), q, k)
        dattn = z(lambda a, b: jnp.where(incl, _dot(a, b, NT), 0.0), dov, vn_)
        dqg = z(lambda a, b: _dot(a, b, NT), dov, s)
        dkg = z(lambda a, b: _dot(a, b, NT), vn_, dsn)
        dglast = z(lambda a, b, c, d, e: _sum_all(a * b) * jnp.exp(e[1]) + _sum_all(c * d), s, dsn, dkg, kg, dec)
        dw = z(lambda a, b: -_dot(a, b, NT), dvn_, s)
        dtm = z(lambda a, b, c, d: _dot(a, b, NT) + _dot(c, d, NT), dvn_, vb, dw, kbg)
        dvb = z(lambda a, b: _dot(a, b, TN), tm, dvn_)
        dkbg = z(lambda a, b: _dot(a, b, TN), tm, dw)
        dlow = z(lambda a, b: jnp.where(strict, -_dot(_dot(a, b, TN), a, NT), 0.0), tm, dtm)
        dkk = z(lambda a, b: a * b, dlow, decay)
        dqk = z(lambda a, b: a * b, dattn, decay)
        dkb = z(lambda a, b, c, d: _dot(a, b, NN) + c * d, dkk, k, dkbg, egc)
        dk = z(lambda a, b, c, d, e, f, g_, h_: _dot(a, b, TN) + _dot(c, d, TN) + e * f + g_ * h_, dkk, kb, dqk, q, dkg, ekg, dkb, bv)
        dq = z(lambda a, b, c, d: _dot(a, b, NN) + c * d, dqk, k, dqg, egc)
        m = z(lambda a, b, c, d, e: (a * b + c * d) * e, dlow, kk, dattn, qk, decay)
        mcol = [_dot(mh, ones, TN) + _dot(ml, ones, TN) for mh, ml in (_split(a) for a in m)]
        for i, (j, h) in enumerate(probs):
            rows = _chunk_rows(j)
            dqkv_ref[h, rows, :] = dq[i]
            dqkv_ref[HEADS + h, rows, :] = dk[i]
            dqkv_ref[2 * HEADS + h, rows, :] = dvb[i] * bv[i]
            db_ref[h, rows, :] = jnp.broadcast_to(_rowsum(dkb[i] * k[i] + dvb[i] * v[i]), (CHUNK, LANE))
            dgc = (_rowsum(dqg[i] * qg[i] + dkbg[i] * kbg[i] - dkg[i] * kg[i]) + _rowsum(m[i]) - mcol[i]
                   + jnp.where(last_row, dglast[i], 0.0))
            dg_ref[h, rows, :] = _suffix_sum_rows(dgc)

    act = lambda heads: pl.BlockSpec((heads, per * CHUNK, LANE), lambda n: (0, n, 0))
    mat = lambda d: pl.BlockSpec((per, HEADS, d, d), lambda n: (n, 0, 0, 0))
    out = jax.ShapeDtypeStruct((HEADS, t, LANE), F32)
    return _pcall(
        body, name=name, out_shape=(jax.ShapeDtypeStruct((3 * HEADS, t, LANE), F32), out, out), grid=(n_chunks // per,),
        in_specs=[act(3 * HEADS), act(HEADS), act(HEADS), mat(CHUNK), mat(LANE), mat(LANE), act(HEADS), act(HEADS), act(HEADS)],
        out_specs=(act(3 * HEADS), act(HEADS), act(HEADS)), semantics=("parallel",),
        vmem_limit=VMEM_LIMIT)(qkv, g, beta, tmats, states, dstates, do, dvn, vn)


ANY = pl.BlockSpec(memory_space=pl.ANY)
PEERS = N_DEV - 1


def _all_gather(arrays, *, name):
    n = len(arrays)

    def body(*refs):
        ins, outs = refs[:n], refs[n:2 * n]
        send_sems, recv_sems, local_sems = refs[2 * n:]
        x, y, c = lax.axis_index("x"), lax.axis_index("y"), lax.axis_index("c")
        me, sibling = (x, y, c), (x, y, 1 - c)
        chips = [(1 - x, y), (x, 1 - y), (1 - x, 1 - y)]

        def copy(a, k, block, to, src=None):
            dst = outs[a].at[4 * block[0] + 2 * block[1] + block[2]]
            return pltpu.make_async_remote_copy(src_ref=dst if src is None else src, dst_ref=dst, send_sem=send_sems.at[a * PEERS + k],
                                                recv_sem=recv_sems.at[a * PEERS + k], device_id=to, device_id_type=MESH)

        local = [pltpu.make_async_copy(ins[a], outs[a].at[4 * x + 2 * y + c], local_sems.at[a]) for a in range(n)]
        for cp in local:
            cp.start()
        first = []
        for a in range(n):
            first += [copy(a, 1 + j, me, (*chip, c), src=ins[a]) for j, chip in enumerate(chips)]
            first.append(copy(a, 0, me, sibling, src=ins[a]))
        for cp in first:
            cp.start()
        passed = []
        for a in range(n):
            for j, chip in enumerate(chips):
                copy(a, 1 + j, (*chip, c), me).wait_recv()
                fwd = copy(a, 4 + j, (*chip, c), sibling)
                fwd.start()
                passed.append(fwd)
        for a in range(n):
            copy(a, 0, sibling, me).wait_recv()
            for j, chip in enumerate(chips):
                copy(a, 4 + j, (*chip, 1 - c), me).wait_recv()
        for cp in first + passed:
            cp.wait_send()
        for cp in local:
            cp.wait()

    return _pcall(body, name=name, out_shape=tuple(jax.ShapeDtypeStruct((N_DEV,) + a.shape, a.dtype) for a in arrays),
                  in_specs=[ANY] * n, out_specs=(ANY,) * n,
                  scratch_shapes=[pltpu.SemaphoreType.DMA((n * PEERS,)), pltpu.SemaphoreType.DMA((n * PEERS,)),
                                  pltpu.SemaphoreType.DMA((n,))])(*arrays)


CHIPS = 4


def _pair_exchange(arrays, *, name):
    n = len(arrays)

    def body(*refs):
        ins, outs = refs[:n], refs[n:2 * n]
        send_sems, recv_sems = refs[2 * n:]
        x, y, c = lax.axis_index("x"), lax.axis_index("y"), lax.axis_index("c")
        copies = []
        for a in range(n):
            for q in range(CHIPS):
                cp = pltpu.make_async_remote_copy(src_ref=ins[a].at[2 * q + 1 - c], dst_ref=outs[a].at[q],
                                                  send_sem=send_sems.at[a * CHIPS + q], recv_sem=recv_sems.at[a * CHIPS + q],
                                                  device_id=(x, y, 1 - c), device_id_type=MESH)
                cp.start()
                copies.append(cp)
        for cp in copies:
            cp.wait()

    return _pcall(body, name=name, out_shape=tuple(jax.ShapeDtypeStruct((CHIPS,) + a.shape[1:], a.dtype) for a in arrays),
                  in_specs=[ANY] * n, out_specs=(ANY,) * n,
                  scratch_shapes=[pltpu.SemaphoreType.DMA((n * CHIPS,)), pltpu.SemaphoreType.DMA((n * CHIPS,))])(*arrays)


def _pair_add(blocks, theirs, *, name):
    _, r, c_ = blocks.shape
    tr = _tile(r, 512, 16)

    def body(mine_ref, theirs_ref, o_ref):
        core = lax.axis_index("c")
        own = jnp.where(core == 0, mine_ref[0, 0].astype(F32), mine_ref[0, 1].astype(F32))
        o_ref[0] = (own + theirs_ref[0].astype(F32)).astype(o_ref.dtype)

    spec = pl.BlockSpec((1, tr, c_), lambda q, i: (q, i, 0))
    return _pcall(body, name=name, out_shape=jax.ShapeDtypeStruct(theirs.shape, theirs.dtype), grid=(CHIPS, r // tr),
                  in_specs=[pl.BlockSpec((1, 2, tr, c_), lambda q, i: (q, 0, i, 0)), spec], out_specs=spec,
                  semantics=("parallel", "parallel"), vmem_limit=VMEM_LIMIT)(blocks.reshape(CHIPS, 2, r, c_), theirs)


HBM = pl.BlockSpec(memory_space=pltpu.HBM)
SEM = pl.BlockSpec(memory_space=pltpu.SEMAPHORE)
EFFECT = pltpu.SideEffectType.DATAFLOW_SIDE_EFFECTING


GATHER, CHIP_GATHER, CHIP_SCATTER = "gather", "chip_gather", "chip_scatter"
PEERS_OF = {GATHER: N_DEV - 1, CHIP_GATHER: CHIPS - 1, CHIP_SCATTER: CHIPS - 1}


def _direct_copies(srcs, lands, send_sems, recv_sems, local_sems, kind):
    x, y, c = lax.axis_index("x"), lax.axis_index("y"), lax.axis_index("c")
    peers = PEERS_OF[kind]
    mine = 2 * x + y if kind == CHIP_SCATTER else 4 * x + 2 * y + c
    copies = []
    for a, (src, land) in enumerate(zip(srcs, lands)):
        copies.append(pltpu.make_async_copy(src.at[mine] if kind == CHIP_SCATTER else src, land.at[mine], local_sems.at[a]))
        for k in range(1, peers + 1):
            bits = k if kind == GATHER else 2 * k
            px = 1 - x if bits & 4 else x
            py = 1 - y if bits & 2 else y
            pc = 1 - c if bits & 1 else c
            copies.append(pltpu.make_async_remote_copy(
                src_ref=src.at[2 * px + py] if kind == CHIP_SCATTER else src, dst_ref=land.at[mine],
                send_sem=send_sems.at[a * peers + k - 1], recv_sem=recv_sems.at[a * peers + k - 1],
                device_id=(px, py, pc), device_id_type=MESH))
    return copies


def _pair_swap(arrays, *, name):
    n = len(arrays)

    def body(*refs):
        mine, zones = refs[:n], refs[n:2 * n]
        send_sems, recv_sems = refs[2 * n:]
        x, y, c = lax.axis_index("x"), lax.axis_index("y"), lax.axis_index("c")
        copies = []
        for a in range(n):
            for q in range(CHIPS):
                copies.append(pltpu.make_async_remote_copy(
                    src_ref=mine[a].at[2 * q + c], dst_ref=zones[a].at[2 * q + c], send_sem=send_sems.at[a * CHIPS + q],
                    recv_sem=recv_sems.at[a * CHIPS + q], device_id=(x, y, 1 - c), device_id_type=MESH))
        for cp in copies:
            cp.start()
        for cp in copies:
            cp.wait()

    return _pcall(body, name=name, out_shape=tuple(jax.ShapeDtypeStruct(a.shape, a.dtype) for a in arrays),
                  in_specs=[ANY] * n, out_specs=(ANY,) * n, input_output_aliases={i: i for i in range(n)},
                  scratch_shapes=[pltpu.SemaphoreType.DMA((n * CHIPS,)), pltpu.SemaphoreType.DMA((n * CHIPS,))])(*arrays)


def _exchange_start(groups, kind, *, name, after=None):
    srcs = [s for group in groups for s in group]
    n = len(srcs)
    sizes = [len(group) for group in groups]
    starts = [sum(sizes[:g]) for g in range(len(groups))]
    land_shapes = [s.shape if kind == CHIP_SCATTER else (N_DEV,) + s.shape for s in srcs]
    peers = PEERS_OF[kind]
    extra = [] if after is None else [after]

    def body(*refs):
        srcs_, lands = refs[:n], refs[n:2 * n]
        token = refs[-1]
        sem_refs = refs[2 * n + len(extra):]
        for g, (at, size) in enumerate(zip(starts, sizes)):
            send_sems, recv_sems, local_sems = sem_refs[3 * g:3 * g + 3]
            for cp in _direct_copies(srcs_[at:at + size], lands[at:at + size], send_sems, recv_sems, local_sems, kind):
                cp.start()
        token[...] = jnp.zeros_like(token)

    sems = tuple(t for size in sizes for t in (pltpu.SemaphoreType.DMA((size * peers,)), pltpu.SemaphoreType.DMA((size * peers,)),
                                               pltpu.SemaphoreType.DMA((size,))))
    thru = tuple(pltpu.HBM(s.shape, s.dtype) for s in srcs) + tuple(pltpu.HBM(shp, s.dtype) for shp, s in zip(land_shapes, srcs))
    ins = [pltpu.with_memory_space_constraint(s, pltpu.HBM) for s in srcs]
    ins += [pltpu.with_memory_space_constraint(lax.empty(shp, s.dtype), pltpu.HBM) for shp, s in zip(land_shapes, srcs)]
    out = pl.pallas_call(
        body, name=name, out_shape=sems + thru + (jax.ShapeDtypeStruct((SUBLANE, LANE), F32),),
        in_specs=[HBM] * (2 * n) + [ANY] * len(extra),
        out_specs=(SEM,) * len(sems) + (HBM,) * (2 * n) + (pl.BlockSpec(memory_space=pltpu.VMEM),),
        input_output_aliases={i: len(sems) + i for i in range(2 * n)},
        compiler_params=pltpu.CompilerParams(has_side_effects=EFFECT))(*ins, *extra)
    arrays = out[len(sems):-1]
    started = [tuple(out[3 * g:3 * g + 3]) + tuple(arrays[at:at + size]) + tuple(arrays[n + at:n + at + size])
               for g, (at, size) in enumerate(zip(starts, sizes))]
    return started, out[-1]


def _exchange_wait(started, after, kind, *, name):
    n = (len(started) - 3) // 2
    sems, arrays = started[:3], started[3:]

    def body(*refs):
        srcs_, lands = refs[:n], refs[n:2 * n]
        send_sems, recv_sems, local_sems = refs[2 * n:2 * n + 3]
        for cp in _direct_copies(srcs_, lands, send_sems, recv_sems, local_sems, kind):
            cp.wait()

    out = pl.pallas_call(
        body, name=name, out_shape=tuple(pltpu.HBM(a.shape, a.dtype) for a in arrays),
        in_specs=[HBM] * (2 * n) + [SEM] * 3 + [ANY], out_specs=(HBM,) * (2 * n),
        input_output_aliases={i: i for i in range(2 * n)},
        compiler_params=pltpu.CompilerParams(has_side_effects=EFFECT))(*arrays, *sems, after)
    return out[n:]


def _adamw_reduce(w, parts, m, v, *, name, after=None):
    layers, r, c = w.shape
    assert len(parts) == layers
    senders = parts[0].shape[0]
    tr = _tile(r, 512, 16)
    tiles = r // tr
    bc1 = 1.0 - ADAM_B1 ** ADAM_STEP
    bc2 = 1.0 - ADAM_B2 ** ADAM_STEP

    def body(w_ref, *rest):
        p_refs = rest[:layers]
        m_ref, v_ref, g_ref, d_ref, nm_ref, nv_ref = rest[layers:]

        def update(p_ref):
            g = p_ref[0, :, pl.ds(0, c)].astype(F32)
            for s in range(1, senders):
                g = g + p_ref[s, :, pl.ds(0, c)].astype(F32)
            nm = ADAM_B1 * m_ref[0] + (1.0 - ADAM_B1) * g
            nv = ADAM_B2 * v_ref[0] + (1.0 - ADAM_B2) * (g * g)
            g_ref[0] = g
            nm_ref[0] = nm
            nv_ref[0] = nv
            d_ref[0] = -ADAM_LR * ((nm / bc1) / (jnp.sqrt(nv / bc2) + ADAM_EPS) + ADAM_WD * w_ref[0])

        for layer in range(layers):
            pl.when(pl.program_id(0) == layer)(functools.partial(update, p_refs[layer]))

    def part_spec(layer, shape):
        rest = 0 if layer > 0 else tiles - 1
        return pl.BlockSpec((senders, tr, shape[2]), lambda l, i: (0, jnp.where(l == layer, i, rest), 0))

    spec = pl.BlockSpec((1, tr, c), lambda l, i: (l, i, 0))
    out = jax.ShapeDtypeStruct((layers, r, c), F32)
    return _pcall(body, name=name, out_shape=(out,) * 4, grid=(layers, tiles),
                  in_specs=[spec] + [part_spec(layer, p.shape) for layer, p in enumerate(parts)] + [spec, spec],
                  out_specs=(spec,) * 4, semantics=("arbitrary", "arbitrary"), vmem_limit=VMEM_LIMIT, after=after)(w, *parts, m, v)


def _pool_windows():
    return jnp.repeat(jnp.asarray(POOL_WINDOWS, F32), POOL_DIM // len(POOL_WINDOWS))[None, :]


def _block_diag_pairs(pool_w):
    z = jnp.zeros_like(pool_w[0])
    return jnp.stack([jnp.block([[pool_w[2 * b], z], [z, pool_w[2 * b + 1]]]) for b in range(2)])


def _pad_lanes(vec):
    return jnp.zeros((1, LANE), F32).at[0, :vec.shape[0]].set(vec)


FF_SHARD = D_FF // N_DEV
FF_BLOCK = 384


def _layer_fwd(x, p_i, wt, fetch):
    wt = {**wt, **fetch(0, x)}
    proj, h1 = _matmul(x, wt["w_in"], "nt", norm_g=wt["norm1_g"], name="mm_in")
    qkv = _qkv_prep_fwd(proj, wt["conv_qkv"], name="qkv_prep_fwd")
    g, beta = _gates_fwd(proj, wt["a_log"], wt["dt_bias"], name="gates_fwd")
    u, w, qg, kg, attn, tmats = _deltanet_prep(qkv, g, beta, name="deltanet_prep")
    wt.update(fetch(1, u))
    o, vn, states = _deltanet_scan(u, w, qg, kg, attn, g, name="deltanet_scan", after=wt.get("behind"))
    o_a = _apost_fwd(o, proj, wt["onorm_g"], name="apost_fwd")
    o_b = _pool_fwd(proj, wt["pool_win"], wt["pool_wbd"], wt["pool_scale"], name="pool_fwd")
    o_c = _sconv_fwd(proj, wt["sconv_w"], name="sconv_fwd")
    mixed = jnp.concatenate([o_a, o_b, o_c], axis=1)
    wt.update(fetch(1, mixed))
    x1 = _matmul(mixed, wt["w_out"], "nn", res=x, name="mm_out")
    wt.update(fetch(2, x1))
    ff, gate, up, h2 = _swiglu_fwd(x1, wt["norm2_g"], wt["w_gate"], wt["w_up"], name="swiglu_fwd")
    wt.update(fetch(3, ff))
    x2 = _matmul(ff, wt["w_down"], "nn", res=x1, name="mm_down")
    wt.update(fetch(4, x2))
    x3, pgl, pp = _ple_fwd(x2, p_i, wt["ple_gate"], wt["ple_proj"], name="ple_fwd")
    saved = dict(x=x, h1=h1, proj=proj, qkv=qkv, g=g, beta=beta, o=o, states=states, tmats=tmats, mixed=mixed, x1=x1, h2=h2,
                 gate=gate, up=up, ff=ff, x2=x2, pgl=pgl, pp=pp, p=p_i, w=w, qg=qg, kg=kg, attn=attn, vn=vn, wt=wt)
    return x3, saved


def _col_blocks(g):
    a = g.shape[0]
    return jnp.transpose(g.reshape(a, N_DEV, -1), (1, 0, 2))


def _cols_joined(blocks):
    return jnp.transpose(blocks, (1, 0, 2)).reshape(blocks.shape[1], -1)


def _layer_bwd(dx3, sv, emit, after=None):
    gr, big = {}, {}
    wt = sv["wt"]
    rows = D_MODEL // N_DEV
    dpgl, dpp = _ple_bwd(dx3, sv["pgl"], sv["pp"], name="ple_bwd", after=after)
    big["ple_proj"] = _matmul(sv["p"], dpp, "tn", out_blocked=(N_DEV, rows), out_dtype=BF16, name="mm_dplep")
    big["ple_gate"] = _matmul(sv["x2"], dpgl, "tn", out_dtype=BF16, name="mm_dpleg").reshape(N_DEV, rows, D_MODEL)
    dx2 = _matmul(dpgl, wt["ple_gate"], "nt", res=dx3, name="mm_dx2")
    big["w_down"] = _matmul(sv["ff"], dx2, "tn", out_dtype=BF16, name="mm_ddown").reshape(N_DEV, FF_BLOCK, D_MODEL)
    dgate, dup = _swiglu_bwd(dx2, wt["w_down"], sv["gate"], sv["up"], name="swiglu_bwd", after=emit(0, big))
    big["w_gate"] = _matmul(dgate, sv["h2"], "tn", out_dtype=BF16, name="mm_dgate").reshape(N_DEV, FF_BLOCK, D_MODEL)
    big["w_up"] = _matmul(dup, sv["h2"], "tn", out_dtype=BF16, name="mm_dup").reshape(N_DEV, FF_BLOCK, D_MODEL)
    dx1, gr["norm2_g"] = _matmul_norm_bwd(dgate, wt["w_gate"], sv["x1"], wt["norm2_g"], dx2, more=(dup, wt["w_up"]), name="mm_dh2")
    big["w_out"] = _matmul(sv["mixed"], dx1, "tn", out_dtype=BF16, name="mm_dout").reshape(N_DEV, rows, D_MODEL)
    dmixed = _matmul(dx1, wt["w_out"], "nt", name="mm_dmixed", after=emit(1, big))
    proj = sv["proj"]
    dcb, dcc, dch, dsconv = _sconv_bwd(proj, wt["sconv_w"], dmixed, name="sconv_bwd")
    big["sconv_w"] = _col_blocks(dsconv)
    dhp, dwbd, gr["pool_scale"] = _pool_bwd(proj, wt["pool_win"], wt["pool_wbd"], wt["pool_scale"], dmixed, name="pool_bwd")
    half = LANE // 2
    gr["pool_w"] = jnp.stack([dwbd[0, :half, :half], dwbd[0, half:, half:], dwbd[1, :half, :half], dwbd[1, half:, half:]])
    do, dz, gr["onorm_g"] = _apost_bwd(sv["o"], proj, wt["onorm_g"], dmixed, name="apost_bwd")
    dvn, dstates = _deltanet_bscan(sv["w"], sv["qg"], sv["kg"], sv["attn"], sv["g"], do, name="deltanet_bscan")
    dqkv_h, dg, dbeta = _deltanet_post(sv["qkv"], sv["g"], sv["beta"], sv["tmats"], sv["states"], dstates, do, dvn, sv["vn"],
                                       name="deltanet_post")
    dab, dalog, ddtb = _gates_bwd(proj, wt["a_log"], wt["dt_bias"], dg, dbeta, name="gates_bwd")
    gr["a_log"], gr["dt_bias"] = dalog[0, :HEADS], ddtb[0, :HEADS]
    dqkv, dconv = _qkv_prep_bwd(proj, wt["conv_qkv"], dqkv_h, name="qkv_prep_bwd")
    big["conv_qkv"] = _col_blocks(dconv)
    dproj = jnp.concatenate([dqkv, dz, dab, dhp, dcb, dcc, dch], axis=1)
    dwin = _matmul(dproj, sv["h1"], "tn", out_dtype=BF16, name="mm_din")
    big["w_in"] = jnp.concatenate([dwin[:AB_COL + 2 * HEADS], dwin[AB_COL + LANE:]], axis=0).reshape(N_DEV, -1, D_MODEL)
    dx, gr["norm1_g"] = _matmul_norm_bwd(dproj, wt["w_in"], sv["x"], wt["norm1_g"], dx1, name="mm_dh1", after=emit(2, big))
    return dx, gr


FETCH_GROUPS = (("w_in", "conv_qkv", "sconv_w"), ("w_out",), ("w_gate", "w_up"), ("w_down",), ("ple_gate", "ple_proj"))
EMIT_GROUPS = (("ple_proj", "ple_gate", "w_down"), ("w_gate", "w_up", "w_out"), ("w_in", "conv_qkv", "sconv_w"))


def _small_weights(w, i):
    return dict(
        norm1_g=w["norm1_g"][i][None], norm2_g=w["norm2_g"][i][None], onorm_g=w["onorm_g"][i][None],
        a_log=_pad_lanes(w["a_log"][i]), dt_bias=_pad_lanes(w["dt_bias"][i]),
        pool_scale=w["pool_scale"][i][None], pool_win=_pool_windows(), pool_wbd=_block_diag_pairs(w["pool_w"][i]))


def _as_read(name, gathered):
    if name == "w_in":
        rows = gathered[:, :D_IN // N_DEV].reshape(-1, D_MODEL)
        return jnp.concatenate([rows[:AB_COL + 2 * HEADS], jnp.zeros((LANE - 2 * HEADS, D_MODEL), BF16),
                                rows[AB_COL + 2 * HEADS:]], axis=0)
    if name in ("conv_qkv", "sconv_w"):
        return _cols_joined(gathered)
    if name == "ple_proj":
        return gathered
    return gathered.reshape(-1, D_MODEL)


SHARDED = ("w_in", "w_gate", "w_up", "w_down", "w_out", "ple_gate", "ple_proj", "conv_qkv", "sconv_w")
SMALL = ("norm1_g", "a_log", "dt_bias", "onorm_g", "pool_w", "pool_scale", "norm2_g", "final_g")
SLAB_COLS = 1024


def _payload(name, shard):
    if name in ("conv_qkv", "sconv_w"):
        return shard
    out = shard.astype(BF16)
    if name in ("w_gate", "w_up", "w_down"):
        out = jnp.pad(out, ((0, FF_BLOCK - FF_SHARD), (0, 0)))
    if name == "w_in":
        out = jnp.pad(out, ((0, -out.shape[0] % (2 * SUBLANE)), (0, 0)))
    return out


TRANSPOSED = ("w_in", "w_gate", "w_up")


def _ff_rows(t):
    return jnp.transpose(t, (0, 2, 1))


def _slab_rows(shape):
    size = 1
    for s in shape:
        size *= s
    return SUBLANE * -(-size // (SUBLANE * SLAB_COLS))


def _pack_slab(parts, extra_row):
    rows = []
    for name in SMALL:
        flat = parts[name].reshape(-1)
        nrow = _slab_rows(parts[name].shape)
        rows.append(jnp.pad(flat, (0, nrow * SLAB_COLS - flat.shape[0])).reshape(nrow, SLAB_COLS))
    rows.append(jnp.pad(extra_row, ((0, SUBLANE - 1), (0, 0))))
    return jnp.concatenate(rows, axis=0)


def _unpack_slab(slab, shapes):
    out, row = {}, 0
    for name in SMALL:
        size = 1
        for s in shapes[name]:
            size *= s
        out[name] = slab[row:row + _slab_rows(shapes[name])].reshape(-1)[:size].reshape(shapes[name])
        row += _slab_rows(shapes[name])
    return out, row


def kernel(x, p, norm1_g, w_in, conv_qkv, a_log, dt_bias, onorm_g, pool_w, pool_scale, sconv_w, w_out, norm2_g, w_gate, w_up, w_down, ple_proj, ple_gate, final_g, loss_target, m_norm1_g, m_w_in, m_conv_qkv, m_a_log, m_dt_bias, m_onorm_g, m_pool_w, m_pool_scale, m_sconv_w, m_w_out, m_norm2_g, m_w_gate, m_w_up, m_w_down, m_ple_proj, m_ple_gate, m_final_g, v_norm1_g, v_w_in, v_conv_qkv, v_a_log, v_dt_bias, v_onorm_g, v_pool_w, v_pool_scale, v_sconv_w, v_w_out, v_norm2_g, v_w_gate, v_w_up, v_w_down, v_ple_proj, v_ple_gate, v_final_g):
    names = ["norm1_g", "w_in", "conv_qkv", "a_log", "dt_bias", "onorm_g", "pool_w", "pool_scale", "sconv_w", "w_out", "norm2_g",
             "w_gate", "w_up", "w_down", "ple_proj", "ple_gate", "final_g"]
    w = dict(zip(names, [norm1_g, w_in, conv_qkv, a_log, dt_bias, onorm_g, pool_w, pool_scale, sconv_w, w_out, norm2_g, w_gate, w_up,
                         w_down, ple_proj, ple_gate, final_g]))
    m = dict(zip(names, [m_norm1_g, m_w_in, m_conv_qkv, m_a_log, m_dt_bias, m_onorm_g, m_pool_w, m_pool_scale, m_sconv_w, m_w_out,
                         m_norm2_g, m_w_gate, m_w_up, m_w_down, m_ple_proj, m_ple_gate, m_final_g]))
    v = dict(zip(names, [v_norm1_g, v_w_in, v_conv_qkv, v_a_log, v_dt_bias, v_onorm_g, v_pool_w, v_pool_scale, v_sconv_w, v_w_out,
                         v_norm2_g, v_w_gate, v_w_up, v_w_down, v_ple_proj, v_ple_gate, v_final_g]))
    w.update({k: _ff_rows(w[k]) for k in TRANSPOSED})

    first, rest = FETCH_GROUPS[0], tuple(k for members in FETCH_GROUPS[1:] for k in members)
    gathered = dict(zip(first, _all_gather([_payload(k, w[k][0]) for k in first], name="all_gather_weights")))
    (flying0,), token = _exchange_start([[_payload(k, w[k][0]) for k in rest]], CHIP_GATHER, name="gather_start_0",
                                        after=gathered[first[0]])
    replicated = [_small_weights(w, i) for i in range(DEPTH)]
    replicated[0]["norm1_g"] = replicated[0]["norm1_g"] + token[0, 0]
    for group in (m, v):
        group.update({k: _ff_rows(group[k] + token[0, 0]) for k in TRANSPOSED})
    flying = {}

    asked = set()

    def fetch(i, group, after):
        early = (i, group) not in asked
        asked.add((i, group))
        if group == 1 and early != (i == 0):
            return {}
        extra = {}
        if i == 0 and group == 1:
            landed = _exchange_wait(flying0, after, CHIP_GATHER, name="gather_wait_0")
            gathered.update(zip(rest, _pair_swap(landed, name="pair_swap")))
            (flying["first"],), token = _exchange_start([[_payload(k, w[k][1]) for k in first]], CHIP_GATHER,
                                                        name="gather_start_1_first", after=gathered[rest[0]])
            extra = {"behind": token}
        if i == 1 and group == 0:
            landed = _exchange_wait(flying["first"], after, CHIP_GATHER, name="gather_wait_1_first")
            gathered.update(zip(first, _pair_swap(landed, name="pair_swap")))
            (flying["rest"],), token = _exchange_start([[_payload(k, w[k][1]) for k in rest]], CHIP_GATHER,
                                                       name="gather_start_1_rest", after=gathered[first[0]])
            extra = {"norm1_g": replicated[1]["norm1_g"] + token[0, 0]}
        if i == 1 and group == 1:
            landed = _exchange_wait(flying["rest"], after, CHIP_GATHER, name="gather_wait_1_rest")
            gathered.update(zip(rest, _pair_swap(landed, name="pair_swap")))
        return {**{k: _as_read(k, gathered[k]) for k in FETCH_GROUPS[group]}, **extra}

    def reduce_scatter_start(members, blocks, tag):
        mine = [blocks[k] for k in members]
        theirs = _pair_exchange(mine, name="pair_exchange")
        sums = [_pair_add(a, b, name="pair_add") for a, b in zip(mine, theirs)]
        (started,), token = _exchange_start([sums], CHIP_SCATTER, name="exchange_start_" + tag)
        return started, token

    h, saved0 = _layer_fwd(x[0], p[0, 0], replicated[0], functools.partial(fetch, 0))
    h, saved1 = _layer_fwd(h, p[1, 0], replicated[1], functools.partial(fetch, 1))
    dx, dgf, loss_part = _loss_head(h, final_g[None], loss_target[0], name="loss_head")
    small, big1, flying0 = [None] * DEPTH, {}, []
    dx, small[1] = _layer_bwd(dx, saved1, lambda group, blocks: big1.update({k: blocks[k] for k in EMIT_GROUPS[group]}))
    flying1, token = reduce_scatter_start(SHARDED, big1, "1")

    def emit(group, blocks):
        started, token = reduce_scatter_start(EMIT_GROUPS[group], blocks, f"0_{group}")
        flying0.append(started)
        return token

    dx, small[0] = _layer_bwd(dx, saved0, emit, after=token)
    received = [{}, dict(zip(SHARDED, _exchange_wait(flying1, dx, CHIP_SCATTER, name="exchange_wait_1")))]
    for group, members in enumerate(EMIT_GROUPS):
        received[0].update(zip(members, _exchange_wait(flying0[group], dx, CHIP_SCATTER, name=f"exchange_wait_0_{group}")))

    grads = {k: jnp.stack([small[i][k] for i in range(DEPTH)]) for k in small[0]}
    grads = {k: g[:, 0] if k in ("norm1_g", "norm2_g", "onorm_g", "pool_scale") else g for k, g in grads.items()}
    grads["final_g"] = dgf[0]
    loss_row = jnp.pad(loss_part, ((0, 0), (0, SLAB_COLS - LANE)))
    (small_flying,), token = _exchange_start([[_pack_slab(grads, loss_row)]], GATHER, name="small_gather_start")

    out_g, out_d, out_m, out_v = {}, {}, {}, {}
    for k in SHARDED:
        out_g[k], out_d[k], out_m[k], out_v[k] = _adamw_reduce(w[k], [received[i][k] for i in range(DEPTH)], m[k], v[k],
                                                                name="adamw_" + k, after=token)
    behind_all = jnp.stack([out_v[k][0, 0, 0] for k in SHARDED])
    (small_parts,) = _exchange_wait(small_flying, behind_all, GATHER, name="small_gather_wait")
    zero_row = jnp.zeros((1, SLAB_COLS), F32)
    slabs = _adamw_reduce(_pack_slab(w, zero_row)[None], [small_parts], _pack_slab(m, zero_row)[None],
                          _pack_slab(v, zero_row)[None], name="adamw_small")
    slabs = [s[0] for s in slabs]
    shapes = {k: w[k].shape for k in SMALL}
    for dst, slab in zip((out_g, out_d, out_m, out_v), slabs):
        vals, _ = _unpack_slab(slab, shapes)
        dst.update(vals)
    _, loss_at = _unpack_slab(slabs[0], shapes)
    loss = slabs[0][loss_at, 0]
    for group in (out_g, out_d, out_m, out_v):
        group.update({k: _ff_rows(group[k]) for k in TRANSPOSED})

    return (loss, dx[None], *[out_g[k] for k in names], *[out_d[k] for k in names], *[out_m[k] for k in names],
            *[out_v[k] for k in names])
```

```python
import functools

import jax
import jax.numpy as jnp
from jax import lax
from jax.experimental import pallas as pl
from jax.experimental.pallas import tpu as pltpu

F32 = jnp.float32
BF16 = jnp.bfloat16

D_MODEL = 1024
DEPTH = 2
PLE_DIM = 256
EPS = 1e-6
HEAD_DIM = 128
HEADS = 4
A_DIM = HEADS * HEAD_DIM
QKV_TAPS = 4
CHUNK = 64
POOL_WINDOWS = (2, 4, 8, 16)
POOL_DIM = 256
CONV_DIM = 256
CONV_TAPS = 3
D_FF = 2816
D_IN = 3080
AB_COL = 2048
N_DEV = 8

ADAM_LR = 0.001
ADAM_B1 = 0.9
ADAM_B2 = 0.999
ADAM_EPS = 1e-08
ADAM_WD = 0.01
ADAM_STEP = 10

LANE = 128
SUBLANE = 8
VMEM_BYTES_V7X = 64 * 1024 * 1024
VMEM_LIMIT = VMEM_BYTES_V7X * 3 // 4

NN = ((1,), (0,))
NT = ((1,), (1,))
TN = ((0,), (0,))
MESH = pl.DeviceIdType.MESH


def _dot(a, b, dims):
    return lax.dot_general(a.astype(BF16), b.astype(BF16), (dims, ((), ())), preferred_element_type=F32)


def _pcall(body, *, name, out_shape, grid=(), in_specs=None, out_specs=None, scratch_shapes=(), semantics=None,
           vmem_limit=None, after=None, **kw):
    params = {}
    if semantics is not None:
        params["dimension_semantics"] = semantics
    if vmem_limit is not None:
        params["vmem_limit_bytes"] = vmem_limit
    if after is not None:
        n_in, inner = len(in_specs), body
        body = lambda *refs: inner(*refs[:n_in], *refs[n_in + 1:])
        in_specs = list(in_specs) + [pl.BlockSpec(after.shape, lambda *_: (0,) * after.ndim)]
    call = pl.pallas_call(
        body, name=name, out_shape=out_shape, grid=grid, in_specs=in_specs, out_specs=out_specs,
        scratch_shapes=list(scratch_shapes), compiler_params=pltpu.CompilerParams(**params), **kw)
    return call if after is None else (lambda *args: call(*args, after))


def _sigmoid(x):
    return 1.0 / (1.0 + jnp.exp(-x))


def _softplus(x):
    return jnp.maximum(x, 0.0) + jnp.log(1.0 + jnp.exp(-jnp.abs(x)))


def _tile(n, cap, mult):
    if n <= cap:
        return n
    best = None
    for t in range(mult, cap + 1, mult):
        if n % t == 0:
            best = t
    assert best is not None, (n, cap, mult)
    return best


ROWS_PER_STEP = 512
NARROW_RESULT = 1024
COLS_PER_DOT = 640


def _resident(weight):
    return pl.BlockSpec(weight.shape, lambda i: (0,) * weight.ndim, pipeline_mode=pl.Buffered(1))


def _matmul_rows(a, b, mode, *, name, res=None, out_dtype=F32, b_blocked=False, after=None, norm_g=None):
    m, k = a.shape
    if b_blocked:
        nb, _, bw = b.shape
        n = nb * bw if mode == "nn" else b.shape[1]
    else:
        n = b.shape[1] if mode == "nn" else b.shape[0]
    tm = _tile(m, ROWS_PER_STEP if n > NARROW_RESULT else 2 * ROWS_PER_STEP, 16)
    cn = bw if (b_blocked and mode == "nn") else _tile(n, COLS_PER_DOT, LANE)
    has_res = res is not None
    normed = norm_g is not None

    def body(*refs):
        a_ref, b_ref = refs[0], refs[1]
        g_ref = refs[2] if normed else None
        res_ref = refs[2 + normed] if has_res else None
        o_ref = refs[2 + normed + has_res]
        if normed:
            av = _rms_normed(a_ref[...], g_ref[...])
            refs[3 + normed + has_res][...] = av
        elif not (b_blocked and mode == "nt"):
            av = a_ref[...].astype(BF16)
        for j in range(n // cn):
            cols = pl.ds(j * cn, cn)
            if mode == "nn":
                part = _dot(av, b_ref[j] if b_blocked else b_ref[:, cols], NN)
            elif not b_blocked:
                part = _dot(av, b_ref[cols, :], NT)
            else:
                part = None
                for s in range(nb):
                    term = _dot(a_ref[:, pl.ds(s * bw, bw)], b_ref[s, cols, :], NT)
                    part = term if part is None else part + term
            if has_res:
                part = part + res_ref[:, cols]
            o_ref[:, cols] = part.astype(o_ref.dtype)

    row = lambda width: pl.BlockSpec((tm, width), lambda i: (i, 0))
    whole = _resident(b)
    ins = [a, b] + ([norm_g] if normed else []) + ([res] if has_res else [])
    specs = [row(k), whole] + ([pl.BlockSpec((1, k), lambda i: (0, 0))] if normed else []) + ([row(n)] if has_res else [])
    out = jax.ShapeDtypeStruct((m, n), out_dtype)
    return _pcall(body, name=name, out_shape=(out, jax.ShapeDtypeStruct((m, k), BF16)) if normed else out, grid=(m // tm,),
                  in_specs=specs, out_specs=(row(n), row(k)) if normed else row(n), semantics=("parallel",),
                  vmem_limit=VMEM_LIMIT, after=after)(*ins)


def _matmul_norm_bwd(a, b, x, g, dres, *, name, more=None, after=None):
    m, k = a.shape
    d = b.shape[1]
    tm = _tile(m, ROWS_PER_STEP, 16)
    cn = _tile(d, COLS_PER_DOT, LANE)
    pairs = 1 if more is None else 2

    def body(*refs):
        a_ref, b_ref, x_ref, g_ref, dres_ref = refs[:5]
        dx_ref, dg_ref = refs[3 + 2 * pairs], refs[4 + 2 * pairs]
        av = a_ref[...].astype(BF16)
        for j in range(d // cn):
            cols = pl.ds(j * cn, cn)
            part = _dot(av, b_ref[:, cols], NN)
            if more is not None:
                part = part + _dot(refs[5][...], refs[6][:, cols], NN)
            dx_ref[:, cols] = part
        dhv = dx_ref[...]
        xv = x_ref[...]
        r = lax.rsqrt(jnp.mean(xv * xv, axis=-1, keepdims=True) + EPS)
        xhat = xv * r
        dhg = dhv * g_ref[...]
        dx_ref[...] = dres_ref[...] + r * (dhg - xhat * jnp.mean(dhg * xhat, axis=-1, keepdims=True))
        part_g = jnp.sum(dhv * xhat, axis=0, keepdims=True)

        @pl.when(pl.program_id(0) == 0)
        def _():
            dg_ref[...] = part_g

        @pl.when(pl.program_id(0) > 0)
        def _():
            dg_ref[...] += part_g

    row = lambda width: pl.BlockSpec((tm, width), lambda i: (i, 0))
    vec = pl.BlockSpec((1, d), lambda i: (0, 0))
    ins = [a, b, x, g, dres] + (list(more) if more is not None else [])
    specs = [row(k), _resident(b), row(d), vec, row(d)] + ([row(more[0].shape[1]), _resident(more[1])] if more is not None else [])
    return _pcall(body, name=name, out_shape=(jax.ShapeDtypeStruct((m, d), F32), jax.ShapeDtypeStruct((1, d), F32)),
                  grid=(m // tm,), in_specs=specs, out_specs=(row(d), vec), semantics=("arbitrary",), vmem_limit=VMEM_LIMIT,
                  after=after)(*ins)


def _rms_normed(xv, gv):
    return (xv * lax.rsqrt(jnp.mean(xv * xv, axis=-1, keepdims=True) + EPS) * gv).astype(BF16)


def _matmul(a, b, mode, *, name, res=None, out_dtype=F32, b_blocked=False, out_blocked=None, after=None, norm_g=None):
    if mode != "tn":
        return _matmul_rows(a, b, mode, name=name, res=res, out_dtype=out_dtype, b_blocked=b_blocked, after=after, norm_g=norm_g)
    assert res is None and not b_blocked and after is None and norm_g is None
    (t, m), (t2, n) = a.shape, b.shape
    assert t == t2, (a.shape, b.shape)
    tm = _tile(m, 1024, LANE)
    tn = _tile(n, NARROW_RESULT if n <= NARROW_RESULT else COLS_PER_DOT, LANE)
    if out_blocked is not None:
        assert out_blocked[0] * out_blocked[1] == n
        tn = out_blocked[1]

    def body(a_ref, b_ref, o_ref):
        part = _dot(a_ref[...], b_ref[...], TN).astype(o_ref.dtype)
        if out_blocked is None:
            o_ref[...] = part
        else:
            o_ref[0] = part

    o_spec = (pl.BlockSpec((tm, tn), lambda i, j: (i, j)) if out_blocked is None
              else pl.BlockSpec((1, tm, tn), lambda i, j: (j, i, 0)))
    o_shape = (m, n) if out_blocked is None else (out_blocked[0], m, out_blocked[1])
    return _pcall(body, name=name, out_shape=jax.ShapeDtypeStruct(o_shape, out_dtype), grid=(m // tm, n // tn),
                  in_specs=[pl.BlockSpec((t, tm), lambda i, j: (0, i)), pl.BlockSpec((t, tn), lambda i, j: (0, j))],
                  out_specs=o_spec, semantics=("parallel", "parallel"), vmem_limit=VMEM_LIMIT)(a, b)


ROW_TILE = 512


def _rows(t, width, idx=0):
    return pl.BlockSpec((ROW_TILE, width), lambda i: (i, idx))


def _vec(width):
    return pl.BlockSpec((1, width), lambda i: (0, 0))


def _swiglu_fwd(x, norm_g, w_gate, w_up, *, name):
    t, k = x.shape
    f = w_gate.shape[0]
    tm = _tile(t, ROWS_PER_STEP, 16)
    cn = _tile(f, COLS_PER_DOT, LANE)

    def body(x_ref, g_ref, wg_ref, wu_ref, ff_ref, gate_ref, up_ref, h_ref):
        hv = _rms_normed(x_ref[...], g_ref[...])
        h_ref[...] = hv
        for j in range(f // cn):
            cols = pl.ds(j * cn, cn)
            gv = _dot(hv, wg_ref[cols, :], NT)
            uv = _dot(hv, wu_ref[cols, :], NT)
            gate_ref[:, cols] = gv.astype(BF16)
            up_ref[:, cols] = uv.astype(BF16)
            ff_ref[:, cols] = (gv * _sigmoid(gv) * uv).astype(BF16)

    row = lambda width: pl.BlockSpec((tm, width), lambda i: (i, 0))
    out = jax.ShapeDtypeStruct((t, f), BF16)
    return _pcall(body, name=name, out_shape=(out,) * 3 + (jax.ShapeDtypeStruct((t, k), BF16),), grid=(t // tm,),
                  in_specs=[row(k), pl.BlockSpec((1, k), lambda i: (0, 0)), _resident(w_gate), _resident(w_up)],
                  out_specs=(row(f),) * 3 + (row(k),), semantics=("parallel",), vmem_limit=VMEM_LIMIT)(x, norm_g, w_gate, w_up)


def _swiglu_bwd(dx2, w_down, gate, up, *, name, after=None):
    t, d = dx2.shape
    f = w_down.shape[0]
    tm = _tile(t, ROWS_PER_STEP, 16)
    cn = _tile(f, COLS_PER_DOT, LANE)

    def body(dx_ref, w_ref, gate_ref, up_ref, dgate_ref, dup_ref):
        dxv = dx_ref[...].astype(BF16)
        for j in range(f // cn):
            cols = pl.ds(j * cn, cn)
            dffv = _dot(dxv, w_ref[cols, :], NT)
            gv = gate_ref[:, cols].astype(F32)
            sig = _sigmoid(gv)
            dgate_ref[:, cols] = (dffv * up_ref[:, cols].astype(F32) * sig * (1.0 + gv * (1.0 - sig))).astype(BF16)
            dup_ref[:, cols] = (dffv * gv * sig).astype(BF16)

    row = lambda width: pl.BlockSpec((tm, width), lambda i: (i, 0))
    out = jax.ShapeDtypeStruct((t, f), BF16)
    return _pcall(body, name=name, out_shape=(out, out), grid=(t // tm,), in_specs=[row(d), _resident(w_down), row(f), row(f)],
                  out_specs=(row(f), row(f)), semantics=("parallel",), vmem_limit=VMEM_LIMIT, after=after)(dx2, w_down, gate, up)


def _ple_fwd(x2, p, w_gate, w_proj, *, name):
    t, d = x2.shape
    nb, pdim, bw = w_proj.shape
    tm = _tile(t, ROWS_PER_STEP, 16)
    cn = _tile(d, COLS_PER_DOT, LANE)

    def body(x_ref, p_ref, wg_ref, wp_ref, x3_ref, pgl_ref, pp_ref):
        xb = x_ref[...].astype(BF16)
        pb = p_ref[...].astype(BF16)
        per = cn // bw
        for c in range(d // cn):
            cols = pl.ds(c * cn, cn)
            pgl = _dot(xb, wg_ref[:, cols], NN)
            pp = jnp.concatenate([_dot(pb, wp_ref[c * per + j], NN) for j in range(per)], axis=1)
            pgl_ref[:, cols] = pgl
            pp_ref[:, cols] = pp
            x3_ref[:, cols] = x_ref[:, cols] + _sigmoid(pgl) * pp

    row = lambda width: pl.BlockSpec((tm, width), lambda i: (i, 0))
    out = jax.ShapeDtypeStruct((t, d), F32)
    return _pcall(body, name=name, out_shape=(out,) * 3, grid=(t // tm,),
                  in_specs=[row(d), row(pdim), _resident(w_gate), _resident(w_proj)], out_specs=(row(d),) * 3,
                  semantics=("parallel",), vmem_limit=VMEM_LIMIT)(x2, p, w_gate, w_proj)


def _ple_bwd(dx3, pgl, pp, *, name, after=None):
    t, d = dx3.shape

    def body(dx_ref, pgl_ref, pp_ref, dpgl_ref, dpp_ref):
        dxv = dx_ref[...]
        sig = _sigmoid(pgl_ref[...])
        dpp_ref[...] = (dxv * sig).astype(BF16)
        dpgl_ref[...] = (dxv * pp_ref[...] * sig * (1.0 - sig)).astype(BF16)

    return _pcall(body, name=name, out_shape=(jax.ShapeDtypeStruct((t, d), BF16),) * 2, grid=(t // ROW_TILE,),
                  in_specs=[_rows(t, d)] * 3, out_specs=(_rows(t, d),) * 2, semantics=("parallel",), after=after)(dx3, pgl, pp)


def _loss_head(x3, g, target, *, name):
    t, d = x3.shape

    def body(x_ref, g_ref, t_ref, dx_ref, dg_ref, loss_ref):
        xv = x_ref[...]
        r = lax.rsqrt(jnp.mean(xv * xv, axis=-1, keepdims=True) + EPS)
        xhat = xv * r
        gv = g_ref[...]
        err = xhat * gv - t_ref[...]
        row_loss = jnp.sum(err * err, axis=-1, keepdims=True) * (0.5 / d)
        lpart = jnp.broadcast_to(jnp.sum(row_loss, axis=0, keepdims=True), (1, LANE))
        dy = err * (1.0 / d)
        dyg = dy * gv
        dx_ref[...] = r * (dyg - xhat * jnp.mean(dyg * xhat, axis=-1, keepdims=True))
        gpart = jnp.sum(dy * xhat, axis=0, keepdims=True)

        @pl.when(pl.program_id(0) == 0)
        def _():
            dg_ref[...] = gpart
            loss_ref[...] = lpart

        @pl.when(pl.program_id(0) > 0)
        def _():
            dg_ref[...] += gpart
            loss_ref[...] += lpart

    return _pcall(body, name=name,
                  out_shape=(jax.ShapeDtypeStruct((t, d), F32), jax.ShapeDtypeStruct((1, d), F32), jax.ShapeDtypeStruct((1, LANE), F32)),
                  grid=(t // ROW_TILE,), in_specs=[_rows(t, d), _vec(d), _rows(t, d)],
                  out_specs=(_rows(t, d), _vec(d), _vec(LANE)), semantics=("arbitrary",))(x3, g, target)


def _shift_down(x, d):
    if d == 0:
        return x
    row = lax.broadcasted_iota(jnp.int32, x.shape, 0)
    return jnp.where(row >= d, pltpu.roll(x, d, 0), 0.0)


def _shift_up(x, d):
    if d == 0:
        return x
    t = x.shape[0]
    row = lax.broadcasted_iota(jnp.int32, x.shape, 0)
    return jnp.where(row < t - d, pltpu.roll(x, t - d, 0), 0.0)


def _colsum(x):
    return jnp.sum(x, axis=0, keepdims=True)


def _col(t, idx_fn):
    return pl.BlockSpec((t, LANE), idx_fn)


def _conv_fwd(x, w_ref, taps):
    acc = None
    for j in range(taps):
        term = w_ref[pl.ds(j, 1), :] * _shift_down(x, taps - 1 - j)
        acc = term if acc is None else acc + term
    return acc


def _conv_bwd(x, dy, w_ref, dw_ref, taps):
    dx = None
    for j in range(taps):
        term = w_ref[pl.ds(j, 1), :] * _shift_up(dy, taps - 1 - j)
        dx = term if dx is None else dx + term
        dw_ref[pl.ds(j, 1), :] = _colsum(dy * _shift_down(x, taps - 1 - j))
    return dx


def _qkv_prep_fwd(proj, conv_w, *, name):
    t = proj.shape[0]
    scale = HEAD_DIM ** -0.5

    def body(x_ref, w_ref, o_ref):
        j = pl.program_id(0)
        c = _conv_fwd(x_ref[...], w_ref, QKV_TAPS)
        s = c * _sigmoid(c)
        r = lax.rsqrt(jnp.sum(s * s, axis=-1, keepdims=True) + EPS)
        f = jnp.where(j < 2 * HEADS, r, 1.0) * jnp.where(j < HEADS, scale, 1.0)
        o_ref[0] = s * f

    return _pcall(body, name=name, out_shape=jax.ShapeDtypeStruct((3 * HEADS, t, LANE), F32), grid=(3 * HEADS,),
                  in_specs=[_col(t, lambda j: (0, j)), pl.BlockSpec((QKV_TAPS, LANE), lambda j: (0, j))],
                  out_specs=pl.BlockSpec((1, t, LANE), lambda j: (j, 0, 0)), semantics=("parallel",),
                  vmem_limit=VMEM_LIMIT)(proj, conv_w)


def _qkv_prep_bwd(proj, conv_w, dqkv, *, name):
    t = proj.shape[0]
    scale = HEAD_DIM ** -0.5

    def body(x_ref, w_ref, d_ref, dx_ref, dw_ref):
        j = pl.program_id(0)
        xv = x_ref[...]
        c = _conv_fwd(xv, w_ref, QKV_TAPS)
        sig = _sigmoid(c)
        s = c * sig
        r = lax.rsqrt(jnp.sum(s * s, axis=-1, keepdims=True) + EPS)
        n0 = s * r
        dv = d_ref[0]
        dn0 = dv * jnp.where(j < HEADS, scale, 1.0)
        ds_norm = r * (dn0 - n0 * jnp.sum(dn0 * n0, axis=-1, keepdims=True))
        ds = jnp.where(j < 2 * HEADS, ds_norm, dv)
        dc = ds * sig * (1.0 + c * (1.0 - sig))
        dx_ref[...] = _conv_bwd(xv, dc, w_ref, dw_ref, QKV_TAPS).astype(BF16)

    return _pcall(body, name=name,
                  out_shape=(jax.ShapeDtypeStruct((t, 3 * A_DIM), BF16), jax.ShapeDtypeStruct((QKV_TAPS, 3 * A_DIM), F32)),
                  grid=(3 * HEADS,),
                  in_specs=[_col(t, lambda j: (0, j)), pl.BlockSpec((QKV_TAPS, LANE), lambda j: (0, j)),
                            pl.BlockSpec((1, t, LANE), lambda j: (j, 0, 0))],
                  out_specs=(_col(t, lambda j: (0, j)), pl.BlockSpec((QKV_TAPS, LANE), lambda j: (0, j))),
                  semantics=("parallel",), vmem_limit=VMEM_LIMIT)(proj, conv_w, dqkv)


def _lane_pick(x, lane_idx, lane):
    return jnp.broadcast_to(jnp.sum(jnp.where(lane == lane_idx, x, 0.0), axis=-1, keepdims=True), x.shape)


def _gates_fwd(proj, alog, dtb, *, name):
    t = proj.shape[0]

    def body(x_ref, alog_ref, dtb_ref, g_ref, b_ref):
        xv = x_ref[...]
        lane = lax.broadcasted_iota(jnp.int32, xv.shape, 1)
        gall = -jnp.exp(alog_ref[...]) * _softplus(xv + dtb_ref[...])
        ball = _sigmoid(xv)
        for h in range(HEADS):
            g_ref[h] = _lane_pick(gall, h, lane)
            b_ref[h] = _lane_pick(ball, HEADS + h, lane)

    out = jax.ShapeDtypeStruct((HEADS, t, LANE), F32)
    whole = pl.BlockSpec((HEADS, t, LANE), lambda i: (0, 0, 0))
    return _pcall(body, name=name, out_shape=(out, out), grid=(1,),
                  in_specs=[_col(t, lambda i: (0, AB_COL // LANE)), _vec(LANE), _vec(LANE)], out_specs=(whole, whole),
                  semantics=("arbitrary",), vmem_limit=VMEM_LIMIT)(proj, alog, dtb)


def _gates_bwd(proj, alog, dtb, dg, dbeta, *, name):
    t = proj.shape[0]

    def body(x_ref, alog_ref, dtb_ref, dg_ref, db_ref, dab_ref, dalog_ref, ddtb_ref):
        xv = x_ref[...]
        lane = lax.broadcasted_iota(jnp.int32, xv.shape, 1)
        lane1 = lax.broadcasted_iota(jnp.int32, (1, LANE), 1)
        z = xv + dtb_ref[...]
        nea = -jnp.exp(alog_ref[...])
        da_f = nea * _sigmoid(z)
        g_f = nea * _softplus(z)
        ball = _sigmoid(xv)
        db_f = ball * (1.0 - ball)
        dab = jnp.zeros_like(xv)
        dalog = jnp.zeros((1, LANE), F32)
        for h in range(HEADS):
            dgh = dg_ref[h]
            dab = dab + jnp.where(lane == h, dgh * da_f, 0.0) + jnp.where(lane == HEADS + h, db_ref[h] * db_f, 0.0)
            dalog = dalog + jnp.where(lane1 == h, _colsum(dgh * g_f), 0.0)
        dab_ref[...] = dab.astype(BF16)
        dalog_ref[...] = dalog
        ddtb_ref[...] = jnp.where(lane1 < HEADS, _colsum(dab), 0.0)

    whole = pl.BlockSpec((HEADS, t, LANE), lambda i: (0, 0, 0))
    vec = jax.ShapeDtypeStruct((1, LANE), F32)
    return _pcall(body, name=name, out_shape=(jax.ShapeDtypeStruct((t, LANE), BF16), vec, vec), grid=(1,),
                  in_specs=[_col(t, lambda i: (0, AB_COL // LANE)), _vec(LANE), _vec(LANE), whole, whole],
                  out_specs=(_col(t, lambda i: (0, 0)), _vec(LANE), _vec(LANE)), semantics=("arbitrary",),
                  vmem_limit=VMEM_LIMIT)(proj, alog, dtb, dg, dbeta)


Z_COL = 3 * A_DIM // LANE


def _apost_fwd(o, proj, gn, *, name):
    t = proj.shape[0]

    def body(o_ref, z_ref, gn_ref, y_ref):
        ov = o_ref[0]
        z = z_ref[...]
        r = lax.rsqrt(jnp.mean(ov * ov, axis=-1, keepdims=True) + EPS)
        y_ref[...] = (ov * r * gn_ref[...] * (z * _sigmoid(z))).astype(BF16)

    return _pcall(body, name=name, out_shape=jax.ShapeDtypeStruct((t, A_DIM), BF16), grid=(HEADS,),
                  in_specs=[pl.BlockSpec((1, t, LANE), lambda h: (h, 0, 0)), _col(t, lambda h: (0, Z_COL + h)),
                            pl.BlockSpec((1, LANE), lambda h: (0, 0))],
                  out_specs=_col(t, lambda h: (0, h)), semantics=("parallel",), vmem_limit=VMEM_LIMIT)(o, proj, gn)


def _apost_bwd(o, proj, gn, dmixed, *, name):
    t = proj.shape[0]

    def body(o_ref, z_ref, gn_ref, d_ref, do_ref, dz_ref, dgn_ref):
        ov = o_ref[0]
        z = z_ref[...]
        gnv = gn_ref[...]
        dv = d_ref[...]
        r = lax.rsqrt(jnp.mean(ov * ov, axis=-1, keepdims=True) + EPS)
        ohat = ov * r
        sig = _sigmoid(z)
        dy = dv * (z * sig)
        dz_ref[...] = (dv * ohat * gnv * sig * (1.0 + z * (1.0 - sig))).astype(BF16)
        dyo = dy * gnv
        do_ref[0] = r * (dyo - ohat * jnp.mean(dyo * ohat, axis=-1, keepdims=True))
        part = _colsum(dy * ohat)

        @pl.when(pl.program_id(0) == 0)
        def _():
            dgn_ref[...] = part

        @pl.when(pl.program_id(0) > 0)
        def _():
            dgn_ref[...] += part

    return _pcall(body, name=name,
                  out_shape=(jax.ShapeDtypeStruct((HEADS, t, LANE), F32), jax.ShapeDtypeStruct((t, A_DIM), BF16),
                             jax.ShapeDtypeStruct((1, LANE), F32)),
                  grid=(HEADS,),
                  in_specs=[pl.BlockSpec((1, t, LANE), lambda h: (h, 0, 0)), _col(t, lambda h: (0, Z_COL + h)),
                            pl.BlockSpec((1, LANE), lambda h: (0, 0)), _col(t, lambda h: (0, h))],
                  out_specs=(pl.BlockSpec((1, t, LANE), lambda h: (h, 0, 0)), _col(t, lambda h: (0, h)),
                             pl.BlockSpec((1, LANE), lambda h: (0, 0))),
                  semantics=("arbitrary",), vmem_limit=VMEM_LIMIT)(o, proj, gn, dmixed)


POOL_COL = (AB_COL + LANE) // LANE
CB_COL = POOL_COL + POOL_DIM // LANE
CC_COL = CB_COL + CONV_DIM // LANE
CH_COL = CC_COL + CONV_DIM // LANE
MAX_WIN_LOG2 = 4


def _window_sums(x, shift):
    sums = []
    cur = x
    for k in range(MAX_WIN_LOG2):
        cur = cur + shift(cur, 1 << k)
        sums.append(cur)
    return sums


def _pick_window(sums, win):
    out = sums[-1]
    for k in range(MAX_WIN_LOG2 - 2, -1, -1):
        out = jnp.where(win == float(2 << k), sums[k], out)
    return out


def _pool_counts(shape, win):
    row = lax.broadcasted_iota(jnp.int32, shape, 0).astype(F32)
    return jnp.minimum(row + 1.0, win)


def _pool_fwd(proj, win, wbd, scale, *, name):
    t = proj.shape[0]

    def body(x_ref, win_ref, w_ref, s_ref, y_ref):
        xv = x_ref[...]
        winv = win_ref[...]
        pooled = _pick_window(_window_sums(xv, _shift_down), winv) / _pool_counts(xv.shape, winv) - xv
        y_ref[...] = (_dot(pooled, w_ref[0], NN) * s_ref[...]).astype(BF16)

    nb = POOL_DIM // LANE
    vec = pl.BlockSpec((1, LANE), lambda b: (0, b))
    return _pcall(body, name=name, out_shape=jax.ShapeDtypeStruct((t, POOL_DIM), BF16), grid=(nb,),
                  in_specs=[_col(t, lambda b: (0, POOL_COL + b)), vec, pl.BlockSpec((1, LANE, LANE), lambda b: (b, 0, 0)), vec],
                  out_specs=_col(t, lambda b: (0, b)), semantics=("parallel",), vmem_limit=VMEM_LIMIT)(proj, win, wbd, scale)


def _pool_bwd(proj, win, wbd, scale, dmixed, *, name):
    t = proj.shape[0]

    def body(x_ref, win_ref, w_ref, s_ref, d_ref, dx_ref, dw_ref, ds_ref):
        xv = x_ref[...]
        winv = win_ref[...]
        cnt = _pool_counts(xv.shape, winv)
        pooled = _pick_window(_window_sums(xv, _shift_down), winv) / cnt - xv
        dv = d_ref[...]
        ds_ref[...] = _colsum(dv * _dot(pooled, w_ref[0], NN))
        dy0 = dv * s_ref[...]
        dw_ref[0] = _dot(pooled, dy0, TN)
        dpooled = _dot(dy0, w_ref[0], NT)
        dmean = dpooled / cnt
        dx_ref[...] = (_pick_window(_window_sums(dmean, _shift_up), winv) - dpooled).astype(BF16)

    nb = POOL_DIM // LANE
    vec = pl.BlockSpec((1, LANE), lambda b: (0, b))
    mat = pl.BlockSpec((1, LANE, LANE), lambda b: (b, 0, 0))
    first = A_DIM // LANE
    return _pcall(body, name=name,
                  out_shape=(jax.ShapeDtypeStruct((t, POOL_DIM), BF16), jax.ShapeDtypeStruct((nb, LANE, LANE), F32),
                             jax.ShapeDtypeStruct((1, POOL_DIM), F32)),
                  grid=(nb,),
                  in_specs=[_col(t, lambda b: (0, POOL_COL + b)), vec, mat, vec, _col(t, lambda b: (0, first + b))],
                  out_specs=(_col(t, lambda b: (0, b)), mat, vec), semantics=("parallel",),
                  vmem_limit=VMEM_LIMIT)(proj, win, wbd, scale, dmixed)


def _sconv_fwd(proj, w, *, name):
    t = proj.shape[0]

    def body(cb_ref, cc_ref, ch_ref, w_ref, y_ref):
        y_ref[...] = (cb_ref[...] * _conv_fwd(cc_ref[...] * ch_ref[...], w_ref, CONV_TAPS)).astype(BF16)

    nb = CONV_DIM // LANE
    return _pcall(body, name=name, out_shape=jax.ShapeDtypeStruct((t, CONV_DIM), BF16), grid=(nb,),
                  in_specs=[_col(t, lambda b: (0, CB_COL + b)), _col(t, lambda b: (0, CC_COL + b)),
                            _col(t, lambda b: (0, CH_COL + b)), pl.BlockSpec((CONV_TAPS, LANE), lambda b: (0, b))],
                  out_specs=_col(t, lambda b: (0, b)), semantics=("parallel",), vmem_limit=VMEM_LIMIT)(proj, proj, proj, w)


def _sconv_bwd(proj, w, dmixed, *, name):
    t = proj.shape[0]

    def body(cb_ref, cc_ref, ch_ref, w_ref, d_ref, dcb_ref, dcc_ref, dch_ref, dw_ref):
        cc = cc_ref[...]
        ch = ch_ref[...]
        u = cc * ch
        dv = d_ref[...]
        dcb_ref[...] = (dv * _conv_fwd(u, w_ref, CONV_TAPS)).astype(BF16)
        du = _conv_bwd(u, dv * cb_ref[...], w_ref, dw_ref, CONV_TAPS)
        dcc_ref[...] = (du * ch).astype(BF16)
        dch_ref[...] = (du * cc).astype(BF16)

    nb = CONV_DIM // LANE
    first = (A_DIM + POOL_DIM) // LANE
    act = jax.ShapeDtypeStruct((t, CONV_DIM), BF16)
    wspec = pl.BlockSpec((CONV_TAPS, LANE), lambda b: (0, b))
    ospec = _col(t, lambda b: (0, b))
    return _pcall(body, name=name, out_shape=(act, act, act, jax.ShapeDtypeStruct((CONV_TAPS, CONV_DIM), F32)), grid=(nb,),
                  in_specs=[_col(t, lambda b: (0, CB_COL + b)), _col(t, lambda b: (0, CC_COL + b)),
                            _col(t, lambda b: (0, CH_COL + b)), wspec, _col(t, lambda b: (0, first + b))],
                  out_specs=(ospec, ospec, ospec, wspec), semantics=("parallel",),
                  vmem_limit=VMEM_LIMIT)(proj, proj, proj, w, dmixed)


def _chunk_masks():
    r = lax.broadcasted_iota(jnp.int32, (CHUNK, CHUNK), 0)
    c = lax.broadcasted_iota(jnp.int32, (CHUNK, CHUNK), 1)
    return r >= c, r > c, jnp.where(r == c, 1.0, 0.0).astype(F32)


def _split(a):
    hi = a.astype(BF16)
    return hi, (a - hi.astype(F32)).astype(BF16)


def _dot_split(a, b, dims):
    (ah, al), (bh, bl) = a, b
    return _dot(ah, bh, dims) + _dot(ah, bl, dims) + _dot(al, bh, dims)


def _tri_inv(lows, eye):
    xs = [eye - low for low in lows]
    ps = [_split(low) for low in lows]
    ps = [_split(_dot_split(p, p, NN)) for p in ps]
    for i in range(5):
        xs = [x + _dot_split(_split(x), p, NN) for x, p in zip(xs, ps)]
        if i < 4:
            ps = [_split(_dot_split(p, p, NN)) for p in ps]
    return xs


def _prefix_sum_rows(x):
    for k in range(6):
        x = x + _shift_down(x, 1 << k)
    return x


def _suffix_sum_rows(x):
    for k in range(6):
        x = x + _shift_up(x, 1 << k)
    return x


def _chunk_decay(g, incl):
    gcb = _prefix_sum_rows(g)
    gtot = _colsum(g)
    col = gcb[:, :CHUNK]
    row = gcb.T[:CHUNK, :]
    decay = jnp.exp(jnp.where(incl, col - row, -1e30))
    return gcb, gtot, decay


CHUNKS_PER_STEP = 4


def _chunk_rows(j):
    return pl.ds(j * CHUNK, CHUNK)


def _deltanet_prep(qkv, g, beta, *, name):
    t = qkv.shape[1]
    n_chunks = t // CHUNK
    per = CHUNKS_PER_STEP
    probs = [(j, h) for j in range(per) for h in range(HEADS)]

    def body(qkv_ref, g_ref, b_ref, u_ref, w_ref, qg_ref, kg_ref, attn_ref, tm_ref):
        incl, strict, eye = _chunk_masks()
        q = [qkv_ref[h, _chunk_rows(j), :] for j, h in probs]
        k = [qkv_ref[HEADS + h, _chunk_rows(j), :] for j, h in probs]
        v = [qkv_ref[2 * HEADS + h, _chunk_rows(j), :] for j, h in probs]
        bv = [b_ref[h, _chunk_rows(j), :] for j, h in probs]
        dec = [_chunk_decay(g_ref[h, _chunk_rows(j), :], incl) for j, h in probs]
        kb = [a * b for a, b in zip(k, bv)]
        low = [jnp.where(strict, _dot(a, b, NT) * d[2], 0.0) for a, b, d in zip(kb, k, dec)]
        tm = _tri_inv(low, eye)
        egc = [jnp.exp(d[0]) for d in dec]
        u = [_dot(m, a * b, NN) for m, a, b in zip(tm, v, bv)]
        w = [_dot(m, a * e, NN) for m, a, e in zip(tm, kb, egc)]
        attn = [_dot(a, b, NT) * d[2] for a, b, d in zip(q, k, dec)]
        for i, (j, h) in enumerate(probs):
            rows = _chunk_rows(j)
            u_ref[h, rows, :] = u[i]
            w_ref[h, rows, :] = w[i].astype(BF16)
            qg_ref[h, rows, :] = (q[i] * egc[i]).astype(BF16)
            kg_ref[h, rows, :] = (k[i] * jnp.exp(dec[i][1] - dec[i][0])).astype(BF16)
            attn_ref[j, h] = attn[i].astype(BF16)
            tm_ref[j, h] = tm[i]

    act = lambda heads: pl.BlockSpec((heads, per * CHUNK, LANE), lambda n: (0, n, 0))
    mat = pl.BlockSpec((per, HEADS, CHUNK, CHUNK), lambda n: (n, 0, 0, 0))
    return _pcall(
        body, name=name,
        out_shape=(jax.ShapeDtypeStruct((HEADS, t, LANE), F32),) + (jax.ShapeDtypeStruct((HEADS, t, LANE), BF16),) * 3
        + (jax.ShapeDtypeStruct((n_chunks, HEADS, CHUNK, CHUNK), BF16), jax.ShapeDtypeStruct((n_chunks, HEADS, CHUNK, CHUNK), F32)),
        grid=(n_chunks // per,), in_specs=[act(3 * HEADS), act(HEADS), act(HEADS)],
        out_specs=(act(HEADS),) * 4 + (mat, mat), semantics=("parallel",), vmem_limit=VMEM_LIMIT)(qkv, g, beta)


SCAN_CHUNKS_PER_STEP = 8


def _deltanet_scan(u, w, qg, kg, attn, g, *, name, after=None):
    t = u.shape[1]
    n_chunks = t // CHUNK
    per = SCAN_CHUNKS_PER_STEP

    def body(u_ref, w_ref, qg_ref, kg_ref, attn_ref, g_ref, o_ref, vn_ref, st_ref, s_ref):
        @pl.when(pl.program_id(0) == 0)
        def _():
            s_ref[...] = jnp.zeros_like(s_ref)

        for j in range(per):
            rows = _chunk_rows(j)
            s = [s_ref[h] for h in range(HEADS)]
            vn = [u_ref[h, rows, :] - _dot(w_ref[h, rows, :], s[h], NN) for h in range(HEADS)]
            o = [_dot(qg_ref[h, rows, :], s[h], NN) + _dot(attn_ref[j, h], vn[h], NN) for h in range(HEADS)]
            eg = [jnp.exp(_colsum(g_ref[h, rows, :])) for h in range(HEADS)]
            for h in range(HEADS):
                st_ref[j, h] = s[h]
                s_ref[h] = s[h] * eg[h] + _dot(kg_ref[h, rows, :], vn[h], TN)
                o_ref[h, rows, :] = o[h]
                vn_ref[h, rows, :] = vn[h]

    act = pl.BlockSpec((HEADS, per * CHUNK, LANE), lambda n: (0, n, 0))
    out = jax.ShapeDtypeStruct((HEADS, t, LANE), F32)
    return _pcall(
        body, name=name, out_shape=(out, out, jax.ShapeDtypeStruct((n_chunks, HEADS, LANE, LANE), F32)), grid=(n_chunks // per,),
        in_specs=[act] * 4 + [pl.BlockSpec((per, HEADS, CHUNK, CHUNK), lambda n: (n, 0, 0, 0)), act],
        out_specs=(act, act, pl.BlockSpec((per, HEADS, LANE, LANE), lambda n: (n, 0, 0, 0))),
        scratch_shapes=[pltpu.VMEM((HEADS, LANE, LANE), F32)], semantics=("arbitrary",), after=after)(u, w, qg, kg, attn, g)


def _deltanet_bscan(w, qg, kg, attn, g, do, *, name):
    t = w.shape[1]
    n_chunks = t // CHUNK
    per = SCAN_CHUNKS_PER_STEP
    steps = n_chunks // per

    def body(w_ref, qg_ref, kg_ref, attn_ref, g_ref, do_ref, dvn_ref, dsn_ref, ds_ref):
        @pl.when(pl.program_id(0) == 0)
        def _():
            ds_ref[...] = jnp.zeros_like(ds_ref)

        for j in reversed(range(per)):
            rows = _chunk_rows(j)
            dsn = [ds_ref[h] for h in range(HEADS)]
            dov = [do_ref[h, rows, :] for h in range(HEADS)]
            dvn = [_dot(attn_ref[j, h], dov[h], TN) + _dot(kg_ref[h, rows, :], dsn[h], NN) for h in range(HEADS)]
            eg = [jnp.exp(_colsum(g_ref[h, rows, :])) for h in range(HEADS)]
            for h in range(HEADS):
                dsn_ref[j, h] = dsn[h]
                ds_ref[h] = _dot(qg_ref[h, rows, :], dov[h], TN) + eg[h] * dsn[h] - _dot(w_ref[h, rows, :], dvn[h], TN)
                dvn_ref[h, rows, :] = dvn[h]

    act = pl.BlockSpec((HEADS, per * CHUNK, LANE), lambda n: (0, steps - 1 - n, 0))
    return _pcall(
        body, name=name,
        out_shape=(jax.ShapeDtypeStruct((HEADS, t, LANE), F32), jax.ShapeDtypeStruct((n_chunks, HEADS, LANE, LANE), F32)),
        grid=(steps,),
        in_specs=[act] * 3 + [pl.BlockSpec((per, HEADS, CHUNK, CHUNK), lambda n: (steps - 1 - n, 0, 0, 0)), act, act],
        out_specs=(act, pl.BlockSpec((per, HEADS, LANE, LANE), lambda n: (steps - 1 - n, 0, 0, 0))),
        scratch_shapes=[pltpu.VMEM((HEADS, LANE, LANE), F32)], semantics=("arbitrary",))(w, qg, kg, attn, g, do)


def _sum_all(x):
    return jnp.sum(jnp.sum(x, axis=1, keepdims=True), axis=0, keepdims=True)


def _rowsum(x):
    return jnp.sum(x, axis=1, keepdims=True)


def _deltanet_post(qkv, g, beta, tmats, states, dstates, do, dvn, vn, *, name):
    t = qkv.shape[1]
    n_chunks = t // CHUNK
    per = CHUNKS_PER_STEP
    probs = [(j, h) for j in range(per) for h in range(HEADS)]

    def body(qkv_ref, g_ref, b_ref, tm_ref, st_ref, dsn_ref, do_ref, dvn_ref, vn_ref, dqkv_ref, dg_ref, db_ref):
        incl, strict, _ = _chunk_masks()
        ones = jnp.ones((CHUNK, LANE), BF16)
        last_row = lax.broadcasted_iota(jnp.int32, (CHUNK, LANE), 0) == CHUNK - 1
        z = lambda f, *cols: [f(*a) for a in zip(*cols)]
        q = [qkv_ref[h, _chunk_rows(j), :] for j, h in probs]
        k = [qkv_ref[HEADS + h, _chunk_rows(j), :] for j, h in probs]
        v = [qkv_ref[2 * HEADS + h, _chunk_rows(j), :] for j, h in probs]
        bv = [b_ref[h, _chunk_rows(j), :] for j, h in probs]
        dov = [do_ref[h, _chunk_rows(j), :] for j, h in probs]
        dvn_ = [dvn_ref[h, _chunk_rows(j), :] for j, h in probs]
        vn_ = [vn_ref[h, _chunk_rows(j), :] for j, h in probs]
        tm = [tm_ref[j, h] for j, h in probs]
        s = [st_ref[j, h] for j, h in probs]
        dsn = [dsn_ref[j, h] for j, h in probs]
        dec = [_chunk_decay(g_ref[h, _chunk_rows(j), :], incl) for j, h in probs]
        decay = [d[2] for d in dec]
        egc = [jnp.exp(d[0]) for d in dec]
        ekg = [jnp.exp(d[1] - d[0]) for d in dec]
        kb = z(lambda a, b: a * b, k, bv)
        vb = z(lambda a, b: a * b, v, bv)
        kbg = z(lambda a, b: a * b, kb, egc)
        qg = z(lambda a, b: a * b, q, egc)
        kg = z(lambda a, b: a * b, k, ekg)
        kk = z(lambda a, b: _dot(a, b, NT), kb, k)
        qk = z(lambda a, b: _dot(a, b, NT), q, k)
        dattn = z(lambda a, b: jnp.where(incl, _dot(a, b, NT), 0.0), dov, vn_)
        dqg = z(lambda a, b: _dot(a, b, NT), dov, s)
        dkg = z(lambda a, b: _dot(a, b, NT), vn_, dsn)
        dglast = z(lambda a, b, c, d, e: _sum_all(a * b) * jnp.exp(e[1]) + _sum_all(c * d), s, dsn, dkg, kg, dec)
        dw = z(lambda a, b: -_dot(a, b, NT), dvn_, s)
        dtm = z(lambda a, b, c, d: _dot(a, b, NT) + _dot(c, d, NT), dvn_, vb, dw, kbg)
        dvb = z(lambda a, b: _dot(a, b, TN), tm, dvn_)
        dkbg = z(lambda a, b: _dot(a, b, TN), tm, dw)
        dlow = z(lambda a, b: jnp.where(strict, -_dot(_dot(a, b, TN), a, NT), 0.0), tm, dtm)
        dkk = z(lambda a, b: a * b, dlow, decay)
        dqk = z(lambda a, b: a * b, dattn, decay)
        dkb = z(lambda a, b, c, d: _dot(a, b, NN) + c * d, dkk, k, dkbg, egc)
        dk = z(lambda a, b, c, d, e, f, g_, h_: _dot(a, b, TN) + _dot(c, d, TN) + e * f + g_ * h_, dkk, kb, dqk, q, dkg, ekg, dkb, bv)
        dq = z(lambda a, b, c, d: _dot(a, b, NN) + c * d, dqk, k, dqg, egc)
        m = z(lambda a, b, c, d, e: (a * b + c * d) * e, dlow, kk, dattn, qk, decay)
        mcol = [_dot(mh, ones, TN) + _dot(ml, ones, TN) for mh, ml in (_split(a) for a in m)]
        for i, (j, h) in enumerate(probs):
            rows = _chunk_rows(j)
            dqkv_ref[h, rows, :] = dq[i]
            dqkv_ref[HEADS + h, rows, :] = dk[i]
            dqkv_ref[2 * HEADS + h, rows, :] = dvb[i] * bv[i]
            db_ref[h, rows, :] = jnp.broadcast_to(_rowsum(dkb[i] * k[i] + dvb[i] * v[i]), (CHUNK, LANE))
            dgc = (_rowsum(dqg[i] * qg[i] + dkbg[i] * kbg[i] - dkg[i] * kg[i]) + _rowsum(m[i]) - mcol[i]
                   + jnp.where(last_row, dglast[i], 0.0))
            dg_ref[h, rows, :] = _suffix_sum_rows(dgc)

    act = lambda heads: pl.BlockSpec((heads, per * CHUNK, LANE), lambda n: (0, n, 0))
    mat = lambda d: pl.BlockSpec((per, HEADS, d, d), lambda n: (n, 0, 0, 0))
    out = jax.ShapeDtypeStruct((HEADS, t, LANE), F32)
    return _pcall(
        body, name=name, out_shape=(jax.ShapeDtypeStruct((3 * HEADS, t, LANE), F32), out, out), grid=(n_chunks // per,),
        in_specs=[act(3 * HEADS), act(HEADS), act(HEADS), mat(CHUNK), mat(LANE), mat(LANE), act(HEADS), act(HEADS), act(HEADS)],
        out_specs=(act(3 * HEADS), act(HEADS), act(HEADS)), semantics=("parallel",),
        vmem_limit=VMEM_LIMIT)(qkv, g, beta, tmats, states, dstates, do, dvn, vn)


ANY = pl.BlockSpec(memory_space=pl.ANY)
PEERS = N_DEV - 1


def _all_gather(arrays, *, name):
    n = len(arrays)

    def body(*refs):
        ins, outs = refs[:n], refs[n:2 * n]
        send_sems, recv_sems, local_sems = refs[2 * n:]
        x, y, c = lax.axis_index("x"), lax.axis_index("y"), lax.axis_index("c")
        me, sibling = (x, y, c), (x, y, 1 - c)
        chips = [(1 - x, y), (x, 1 - y), (1 - x, 1 - y)]

        def copy(a, k, block, to, src=None):
            dst = outs[a].at[4 * block[0] + 2 * block[1] + block[2]]
            return pltpu.make_async_remote_copy(src_ref=dst if src is None else src, dst_ref=dst, send_sem=send_sems.at[a * PEERS + k],
                                                recv_sem=recv_sems.at[a * PEERS + k], device_id=to, device_id_type=MESH)

        local = [pltpu.make_async_copy(ins[a], outs[a].at[4 * x + 2 * y + c], local_sems.at[a]) for a in range(n)]
        for cp in local:
            cp.start()
        first = []
        for a in range(n):
            first += [copy(a, 1 + j, me, (*chip, c), src=ins[a]) for j, chip in enumerate(chips)]
            first.append(copy(a, 0, me, sibling, src=ins[a]))
        for cp in first:
            cp.start()
        passed = []
        for a in range(n):
            for j, chip in enumerate(chips):
                copy(a, 1 + j, (*chip, c), me).wait_recv()
                fwd = copy(a, 4 + j, (*chip, c), sibling)
                fwd.start()
                passed.append(fwd)
        for a in range(n):
            copy(a, 0, sibling, me).wait_recv()
            for j, chip in enumerate(chips):
                copy(a, 4 + j, (*chip, 1 - c), me).wait_recv()
        for cp in first + passed:
            cp.wait_send()
        for cp in local:
            cp.wait()

    return _pcall(body, name=name, out_shape=tuple(jax.ShapeDtypeStruct((N_DEV,) + a.shape, a.dtype) for a in arrays),
                  in_specs=[ANY] * n, out_specs=(ANY,) * n,
                  scratch_shapes=[pltpu.SemaphoreType.DMA((n * PEERS,)), pltpu.SemaphoreType.DMA((n * PEERS,)),
                                  pltpu.SemaphoreType.DMA((n,))])(*arrays)


CHIPS = 4


def _pair_exchange(arrays, *, name):
    n = len(arrays)

    def body(*refs):
        ins, outs = refs[:n], refs[n:2 * n]
        send_sems, recv_sems = refs[2 * n:]
        x, y, c = lax.axis_index("x"), lax.axis_index("y"), lax.axis_index("c")
        copies = []
        for a in range(n):
            for q in range(CHIPS):
                cp = pltpu.make_async_remote_copy(src_ref=ins[a].at[2 * q + 1 - c], dst_ref=outs[a].at[q],
                                                  send_sem=send_sems.at[a * CHIPS + q], recv_sem=recv_sems.at[a * CHIPS + q],
                                                  device_id=(x, y, 1 - c), device_id_type=MESH)
                cp.start()
                copies.append(cp)
        for cp in copies:
            cp.wait()

    return _pcall(body, name=name, out_shape=tuple(jax.ShapeDtypeStruct((CHIPS,) + a.shape[1:], a.dtype) for a in arrays),
                  in_specs=[ANY] * n, out_specs=(ANY,) * n,
                  scratch_shapes=[pltpu.SemaphoreType.DMA((n * CHIPS,)), pltpu.SemaphoreType.DMA((n * CHIPS,))])(*arrays)


def _pair_add(blocks, theirs, *, name):
    _, r, c_ = blocks.shape
    tr = _tile(r, 512, 16)

    def body(mine_ref, theirs_ref, o_ref):
        core = lax.axis_index("c")
        own = jnp.where(core == 0, mine_ref[0, 0].astype(F32), mine_ref[0, 1].astype(F32))
        o_ref[0] = (own + theirs_ref[0].astype(F32)).astype(o_ref.dtype)

    spec = pl.BlockSpec((1, tr, c_), lambda q, i: (q, i, 0))
    return _pcall(body, name=name, out_shape=jax.ShapeDtypeStruct(theirs.shape, theirs.dtype), grid=(CHIPS, r // tr),
                  in_specs=[pl.BlockSpec((1, 2, tr, c_), lambda q, i: (q, 0, i, 0)), spec], out_specs=spec,
                  semantics=("parallel", "parallel"), vmem_limit=VMEM_LIMIT)(blocks.reshape(CHIPS, 2, r, c_), theirs)


HBM = pl.BlockSpec(memory_space=pltpu.HBM)
SEM = pl.BlockSpec(memory_space=pltpu.SEMAPHORE)
EFFECT = pltpu.SideEffectType.DATAFLOW_SIDE_EFFECTING


GATHER, CHIP_GATHER, CHIP_SCATTER = "gather", "chip_gather", "chip_scatter"
PEERS_OF = {GATHER: N_DEV - 1, CHIP_GATHER: CHIPS - 1, CHIP_SCATTER: CHIPS - 1}


def _direct_copies(srcs, lands, send_sems, recv_sems, local_sems, kind):
    x, y, c = lax.axis_index("x"), lax.axis_index("y"), lax.axis_index("c")
    peers = PEERS_OF[kind]
    mine = 2 * x + y if kind == CHIP_SCATTER else 4 * x + 2 * y + c
    copies = []
    for a, (src, land) in enumerate(zip(srcs, lands)):
        copies.append(pltpu.make_async_copy(src.at[mine] if kind == CHIP_SCATTER else src, land.at[mine], local_sems.at[a]))
        for k in range(1, peers + 1):
            bits = k if kind == GATHER else 2 * k
            px = 1 - x if bits & 4 else x
            py = 1 - y if bits & 2 else y
            pc = 1 - c if bits & 1 else c
            copies.append(pltpu.make_async_remote_copy(
                src_ref=src.at[2 * px + py] if kind == CHIP_SCATTER else src, dst_ref=land.at[mine],
                send_sem=send_sems.at[a * peers + k - 1], recv_sem=recv_sems.at[a * peers + k - 1],
                device_id=(px, py, pc), device_id_type=MESH))
    return copies


def _pair_swap(arrays, *, name):
    n = len(arrays)

    def body(*refs):
        mine, zones = refs[:n], refs[n:2 * n]
        send_sems, recv_sems = refs[2 * n:]
        x, y, c = lax.axis_index("x"), lax.axis_index("y"), lax.axis_index("c")
        copies = []
        for a in range(n):
            for q in range(CHIPS):
                copies.append(pltpu.make_async_remote_copy(
                    src_ref=mine[a].at[2 * q + c], dst_ref=zones[a].at[2 * q + c], send_sem=send_sems.at[a * CHIPS + q],
                    recv_sem=recv_sems.at[a * CHIPS + q], device_id=(x, y, 1 - c), device_id_type=MESH))
        for cp in copies:
            cp.start()
        for cp in copies:
            cp.wait()

    return _pcall(body, name=name, out_shape=tuple(jax.ShapeDtypeStruct(a.shape, a.dtype) for a in arrays),
                  in_specs=[ANY] * n, out_specs=(ANY,) * n, input_output_aliases={i: i for i in range(n)},
                  scratch_shapes=[pltpu.SemaphoreType.DMA((n * CHIPS,)), pltpu.SemaphoreType.DMA((n * CHIPS,))])(*arrays)


def _exchange_start(groups, kind, *, name, after=None):
    srcs = [s for group in groups for s in group]
    n = len(srcs)
    sizes = [len(group) for group in groups]
    starts = [sum(sizes[:g]) for g in range(len(groups))]
    land_shapes = [s.shape if kind == CHIP_SCATTER else (N_DEV,) + s.shape for s in srcs]
    peers = PEERS_OF[kind]
    extra = [] if after is None else [after]

    def body(*refs):
        srcs_, lands = refs[:n], refs[n:2 * n]
        token = refs[-1]
        sem_refs = refs[2 * n + len(extra):]
        for g, (at, size) in enumerate(zip(starts, sizes)):
            send_sems, recv_sems, local_sems = sem_refs[3 * g:3 * g + 3]
            for cp in _direct_copies(srcs_[at:at + size], lands[at:at + size], send_sems, recv_sems, local_sems, kind):
                cp.start()
        token[...] = jnp.zeros_like(token)

    sems = tuple(t for size in sizes for t in (pltpu.SemaphoreType.DMA((size * peers,)), pltpu.SemaphoreType.DMA((size * peers,)),
                                               pltpu.SemaphoreType.DMA((size,))))
    thru = tuple(pltpu.HBM(s.shape, s.dtype) for s in srcs) + tuple(pltpu.HBM(shp, s.dtype) for shp, s in zip(land_shapes, srcs))
    ins = [pltpu.with_memory_space_constraint(s, pltpu.HBM) for s in srcs]
    ins += [pltpu.with_memory_space_constraint(lax.empty(shp, s.dtype), pltpu.HBM) for shp, s in zip(land_shapes, srcs)]
    out = pl.pallas_call(
        body, name=name, out_shape=sems + thru + (jax.ShapeDtypeStruct((SUBLANE, LANE), F32),),
        in_specs=[HBM] * (2 * n) + [ANY] * len(extra),
        out_specs=(SEM,) * len(sems) + (HBM,) * (2 * n) + (pl.BlockSpec(memory_space=pltpu.VMEM),),
        input_output_aliases={i: len(sems) + i for i in range(2 * n)},
        compiler_params=pltpu.CompilerParams(has_side_effects=EFFECT))(*ins, *extra)
    arrays = out[len(sems):-1]
    started = [tuple(out[3 * g:3 * g + 3]) + tuple(arrays[at:at + size]) + tuple(arrays[n + at:n + at + size])
               for g, (at, size) in enumerate(zip(starts, sizes))]
    return started, out[-1]


def _exchange_wait(started, after, kind, *, name):
    n = (len(started) - 3) // 2
    sems, arrays = started[:3], started[3:]

    def body(*refs):
        srcs_, lands = refs[:n], refs[n:2 * n]
        send_sems, recv_sems, local_sems = refs[2 * n:2 * n + 3]
        for cp in _direct_copies(srcs_, lands, send_sems, recv_sems, local_sems, kind):
            cp.wait()

    out = pl.pallas_call(
        body, name=name, out_shape=tuple(pltpu.HBM(a.shape, a.dtype) for a in arrays),
        in_specs=[HBM] * (2 * n) + [SEM] * 3 + [ANY], out_specs=(HBM,) * (2 * n),
        input_output_aliases={i: i for i in range(2 * n)},
        compiler_params=pltpu.CompilerParams(has_side_effects=EFFECT))(*arrays, *sems, after)
    return out[n:]


def _adamw_reduce(w, parts, m, v, *, name, after=None):
    layers, r, c = w.shape
    assert len(parts) == layers
    senders = parts[0].shape[0]
    tr = _tile(r, 512, 16)
    tiles = r // tr
    bc1 = 1.0 - ADAM_B1 ** ADAM_STEP
    bc2 = 1.0 - ADAM_B2 ** ADAM_STEP

    def body(w_ref, *rest):
        p_refs = rest[:layers]
        m_ref, v_ref, g_ref, d_ref, nm_ref, nv_ref = rest[layers:]

        def update(p_ref):
            g = p_ref[0, :, pl.ds(0, c)].astype(F32)
            for s in range(1, senders):
                g = g + p_ref[s, :, pl.ds(0, c)].astype(F32)
            nm = ADAM_B1 * m_ref[0] + (1.0 - ADAM_B1) * g
            nv = ADAM_B2 * v_ref[0] + (1.0 - ADAM_B2) * (g * g)
            g_ref[0] = g
            nm_ref[0] = nm
            nv_ref[0] = nv
            d_ref[0] = -ADAM_LR * ((nm / bc1) / (jnp.sqrt(nv / bc2) + ADAM_EPS) + ADAM_WD * w_ref[0])

        for layer in range(layers):
            pl.when(pl.program_id(0) == layer)(functools.partial(update, p_refs[layer]))

    def part_spec(layer, shape):
        rest = 0 if layer > 0 else tiles - 1
        return pl.BlockSpec((senders, tr, shape[2]), lambda l, i: (0, jnp.where(l == layer, i, rest), 0))

    spec = pl.BlockSpec((1, tr, c), lambda l, i: (l, i, 0))
    out = jax.ShapeDtypeStruct((layers, r, c), F32)
    return _pcall(body, name=name, out_shape=(out,) * 4, grid=(layers, tiles),
                  in_specs=[spec] + [part_spec(layer, p.shape) for layer, p in enumerate(parts)] + [spec, spec],
                  out_specs=(spec,) * 4, semantics=("arbitrary", "arbitrary"), vmem_limit=VMEM_LIMIT, after=after)(w, *parts, m, v)


def _pool_windows():
    return jnp.repeat(jnp.asarray(POOL_WINDOWS, F32), POOL_DIM // len(POOL_WINDOWS))[None, :]


def _block_diag_pairs(pool_w):
    z = jnp.zeros_like(pool_w[0])
    return jnp.stack([jnp.block([[pool_w[2 * b], z], [z, pool_w[2 * b + 1]]]) for b in range(2)])


def _pad_lanes(vec):
    return jnp.zeros((1, LANE), F32).at[0, :vec.shape[0]].set(vec)


FF_SHARD = D_FF // N_DEV
FF_BLOCK = 384


def _layer_fwd(x, p_i, wt, fetch):
    wt = {**wt, **fetch(0, x)}
    proj, h1 = _matmul(x, wt["w_in"], "nt", norm_g=wt["norm1_g"], name="mm_in")
    qkv = _qkv_prep_fwd(proj, wt["conv_qkv"], name="qkv_prep_fwd")
    g, beta = _gates_fwd(proj, wt["a_log"], wt["dt_bias"], name="gates_fwd")
    u, w, qg, kg, attn, tmats = _deltanet_prep(qkv, g, beta, name="deltanet_prep")
    o, vn, states = _deltanet_scan(u, w, qg, kg, attn, g, name="deltanet_scan")
    o_a = _apost_fwd(o, proj, wt["onorm_g"], name="apost_fwd")
    o_b = _pool_fwd(proj, wt["pool_win"], wt["pool_wbd"], wt["pool_scale"], name="pool_fwd")
    o_c = _sconv_fwd(proj, wt["sconv_w"], name="sconv_fwd")
    mixed = jnp.concatenate([o_a, o_b, o_c], axis=1)
    wt.update(fetch(1, mixed))
    x1 = _matmul(mixed, wt["w_out"], "nn", res=x, name="mm_out", after=wt.get("behind"))
    wt.update(fetch(2, x1))
    ff, gate, up, h2 = _swiglu_fwd(x1, wt["norm2_g"], wt["w_gate"], wt["w_up"], name="swiglu_fwd")
    wt.update(fetch(3, ff))
    x2 = _matmul(ff, wt["w_down"], "nn", res=x1, name="mm_down")
    wt.update(fetch(4, x2))
    x3, pgl, pp = _ple_fwd(x2, p_i, wt["ple_gate"], wt["ple_proj"], name="ple_fwd")
    saved = dict(x=x, h1=h1, proj=proj, qkv=qkv, g=g, beta=beta, o=o, states=states, tmats=tmats, mixed=mixed, x1=x1, h2=h2,
                 gate=gate, up=up, ff=ff, x2=x2, pgl=pgl, pp=pp, p=p_i, w=w, qg=qg, kg=kg, attn=attn, vn=vn, wt=wt)
    return x3, saved


def _col_blocks(g):
    a = g.shape[0]
    return jnp.transpose(g.reshape(a, N_DEV, -1), (1, 0, 2))


def _cols_joined(blocks):
    return jnp.transpose(blocks, (1, 0, 2)).reshape(blocks.shape[1], -1)


def _layer_bwd(dx3, sv, emit, after=None):
    gr, big = {}, {}
    wt = sv["wt"]
    rows = D_MODEL // N_DEV
    dpgl, dpp = _ple_bwd(dx3, sv["pgl"], sv["pp"], name="ple_bwd", after=after)
    big["ple_proj"] = _matmul(sv["p"], dpp, "tn", out_blocked=(N_DEV, rows), out_dtype=BF16, name="mm_dplep")
    big["ple_gate"] = _matmul(sv["x2"], dpgl, "tn", out_dtype=BF16, name="mm_dpleg").reshape(N_DEV, rows, D_MODEL)
    dx2 = _matmul(dpgl, wt["ple_gate"], "nt", res=dx3, name="mm_dx2")
    big["w_down"] = _matmul(sv["ff"], dx2, "tn", out_dtype=BF16, name="mm_ddown").reshape(N_DEV, FF_BLOCK, D_MODEL)
    dgate, dup = _swiglu_bwd(dx2, wt["w_down"], sv["gate"], sv["up"], name="swiglu_bwd", after=emit(0, big))
    big["w_gate"] = _matmul(dgate, sv["h2"], "tn", out_dtype=BF16, name="mm_dgate").reshape(N_DEV, FF_BLOCK, D_MODEL)
    big["w_up"] = _matmul(dup, sv["h2"], "tn", out_dtype=BF16, name="mm_dup").reshape(N_DEV, FF_BLOCK, D_MODEL)
    dx1, gr["norm2_g"] = _matmul_norm_bwd(dgate, wt["w_gate"], sv["x1"], wt["norm2_g"], dx2, more=(dup, wt["w_up"]), name="mm_dh2")
    big["w_out"] = _matmul(sv["mixed"], dx1, "tn", out_dtype=BF16, name="mm_dout").reshape(N_DEV, rows, D_MODEL)
    dmixed = _matmul(dx1, wt["w_out"], "nt", name="mm_dmixed", after=emit(1, big))
    proj = sv["proj"]
    dcb, dcc, dch, dsconv = _sconv_bwd(proj, wt["sconv_w"], dmixed, name="sconv_bwd")
    big["sconv_w"] = _col_blocks(dsconv)
    dhp, dwbd, gr["pool_scale"] = _pool_bwd(proj, wt["pool_win"], wt["pool_wbd"], wt["pool_scale"], dmixed, name="pool_bwd")
    half = LANE // 2
    gr["pool_w"] = jnp.stack([dwbd[0, :half, :half], dwbd[0, half:, half:], dwbd[1, :half, :half], dwbd[1, half:, half:]])
    do, dz, gr["onorm_g"] = _apost_bwd(sv["o"], proj, wt["onorm_g"], dmixed, name="apost_bwd")
    dvn, dstates = _deltanet_bscan(sv["w"], sv["qg"], sv["kg"], sv["attn"], sv["g"], do, name="deltanet_bscan")
    dqkv_h, dg, dbeta = _deltanet_post(sv["qkv"], sv["g"], sv["beta"], sv["tmats"], sv["states"], dstates, do, dvn, sv["vn"],
                                       name="deltanet_post")
    dab, dalog, ddtb = _gates_bwd(proj, wt["a_log"], wt["dt_bias"], dg, dbeta, name="gates_bwd")
    gr["a_log"], gr["dt_bias"] = dalog[0, :HEADS], ddtb[0, :HEADS]
    dqkv, dconv = _qkv_prep_bwd(proj, wt["conv_qkv"], dqkv_h, name="qkv_prep_bwd")
    big["conv_qkv"] = _col_blocks(dconv)
    dproj = jnp.concatenate([dqkv, dz, dab, dhp, dcb, dcc, dch], axis=1)
    dwin = _matmul(dproj, sv["h1"], "tn", out_dtype=BF16, name="mm_din")
    big["w_in"] = jnp.concatenate([dwin[:AB_COL + 2 * HEADS], dwin[AB_COL + LANE:]], axis=0).reshape(N_DEV, -1, D_MODEL)
    dx, gr["norm1_g"] = _matmul_norm_bwd(dproj, wt["w_in"], sv["x"], wt["norm1_g"], dx1, name="mm_dh1", after=emit(2, big))
    return dx, gr


FETCH_GROUPS = (("w_in", "conv_qkv", "sconv_w"), ("w_out",), ("w_gate", "w_up"), ("w_down",), ("ple_gate", "ple_proj"))
EMIT_GROUPS = (("ple_proj", "ple_gate", "w_down"), ("w_gate", "w_up", "w_out"), ("w_in", "conv_qkv", "sconv_w"))


def _small_weights(w, i):
    return dict(
        norm1_g=w["norm1_g"][i][None], norm2_g=w["norm2_g"][i][None], onorm_g=w["onorm_g"][i][None],
        a_log=_pad_lanes(w["a_log"][i]), dt_bias=_pad_lanes(w["dt_bias"][i]),
        pool_scale=w["pool_scale"][i][None], pool_win=_pool_windows(), pool_wbd=_block_diag_pairs(w["pool_w"][i]))


def _as_read(name, gathered):
    if name == "w_in":
        rows = gathered[:, :D_IN // N_DEV].reshape(-1, D_MODEL)
        return jnp.concatenate([rows[:AB_COL + 2 * HEADS], jnp.zeros((LANE - 2 * HEADS, D_MODEL), BF16),
                                rows[AB_COL + 2 * HEADS:]], axis=0)
    if name in ("conv_qkv", "sconv_w"):
        return _cols_joined(gathered)
    if name == "ple_proj":
        return gathered
    return gathered.reshape(-1, D_MODEL)


SHARDED = ("w_in", "w_gate", "w_up", "w_down", "w_out", "ple_gate", "ple_proj", "conv_qkv", "sconv_w")
SMALL = ("norm1_g", "a_log", "dt_bias", "onorm_g", "pool_w", "pool_scale", "norm2_g", "final_g")
SLAB_COLS = 1024


def _payload(name, shard):
    if name in ("conv_qkv", "sconv_w"):
        return shard
    out = shard.astype(BF16)
    if name in ("w_gate", "w_up", "w_down"):
        out = jnp.pad(out, ((0, FF_BLOCK - FF_SHARD), (0, 0)))
    if name == "w_in":
        out = jnp.pad(out, ((0, -out.shape[0] % (2 * SUBLANE)), (0, 0)))
    return out


TRANSPOSED = ("w_in", "w_gate", "w_up")


def _ff_rows(t):
    return jnp.transpose(t, (0, 2, 1))


def _slab_rows(shape):
    size = 1
    for s in shape:
        size *= s
    return SUBLANE * -(-size // (SUBLANE * SLAB_COLS))


def _pack_slab(parts, extra_row):
    rows = []
    for name in SMALL:
        flat = parts[name].reshape(-1)
        nrow = _slab_rows(parts[name].shape)
        rows.append(jnp.pad(flat, (0, nrow * SLAB_COLS - flat.shape[0])).reshape(nrow, SLAB_COLS))
    rows.append(jnp.pad(extra_row, ((0, SUBLANE - 1), (0, 0))))
    return jnp.concatenate(rows, axis=0)


def _unpack_slab(slab, shapes):
    out, row = {}, 0
    for name in SMALL:
        size = 1
        for s in shapes[name]:
            size *= s
        out[name] = slab[row:row + _slab_rows(shapes[name])].reshape(-1)[:size].reshape(shapes[name])
        row += _slab_rows(shapes[name])
    return out, row


def kernel(x, p, norm1_g, w_in, conv_qkv, a_log, dt_bias, onorm_g, pool_w, pool_scale, sconv_w, w_out, norm2_g, w_gate, w_up, w_down, ple_proj, ple_gate, final_g, loss_target, m_norm1_g, m_w_in, m_conv_qkv, m_a_log, m_dt_bias, m_onorm_g, m_pool_w, m_pool_scale, m_sconv_w, m_w_out, m_norm2_g, m_w_gate, m_w_up, m_w_down, m_ple_proj, m_ple_gate, m_final_g, v_norm1_g, v_w_in, v_conv_qkv, v_a_log, v_dt_bias, v_onorm_g, v_pool_w, v_pool_scale, v_sconv_w, v_w_out, v_norm2_g, v_w_gate, v_w_up, v_w_down, v_ple_proj, v_ple_gate, v_final_g):
    names = ["norm1_g", "w_in", "conv_qkv", "a_log", "dt_bias", "onorm_g", "pool_w", "pool_scale", "sconv_w", "w_out", "norm2_g",
             "w_gate", "w_up", "w_down", "ple_proj", "ple_gate", "final_g"]
    w = dict(zip(names, [norm1_g, w_in, conv_qkv, a_log, dt_bias, onorm_g, pool_w, pool_scale, sconv_w, w_out, norm2_g, w_gate, w_up,
                         w_down, ple_proj, ple_gate, final_g]))
    m = dict(zip(names, [m_norm1_g, m_w_in, m_conv_qkv, m_a_log, m_dt_bias, m_onorm_g, m_pool_w, m_pool_scale, m_sconv_w, m_w_out,
                         m_norm2_g, m_w_gate, m_w_up, m_w_down, m_ple_proj, m_ple_gate, m_final_g]))
    v = dict(zip(names, [v_norm1_g, v_w_in, v_conv_qkv, v_a_log, v_dt_bias, v_onorm_g, v_pool_w, v_pool_scale, v_sconv_w, v_w_out,
                         v_norm2_g, v_w_gate, v_w_up, v_w_down, v_ple_proj, v_ple_gate, v_final_g]))
    w.update({k: _ff_rows(w[k]) for k in TRANSPOSED})

    first, rest = FETCH_GROUPS[0], tuple(k for members in FETCH_GROUPS[1:] for k in members)
    gathered = dict(zip(first, _all_gather([_payload(k, w[k][0]) for k in first], name="all_gather_weights")))
    (flying0,), token = _exchange_start([[_payload(k, w[k][0]) for k in rest]], CHIP_GATHER, name="gather_start_0",
                                        after=gathered[first[0]])
    replicated = [_small_weights(w, i) for i in range(DEPTH)]
    replicated[0]["norm1_g"] = replicated[0]["norm1_g"] + token[0, 0]
    for group in (m, v):
        group.update({k: _ff_rows(group[k] + token[0, 0]) for k in TRANSPOSED})
    flying = {}

    def fetch(i, group, after):
        extra = {}
        if i == 0 and group == 1:
            landed = _exchange_wait(flying0, after, CHIP_GATHER, name="gather_wait_0")
            gathered.update(zip(rest, _pair_swap(landed, name="pair_swap")))
            (flying["first"],), token = _exchange_start([[_payload(k, w[k][1]) for k in first]], CHIP_GATHER,
                                                        name="gather_start_1_first", after=gathered[rest[0]])
            extra = {"behind": token}
        if i == 1 and group == 0:
            landed = _exchange_wait(flying["first"], after, CHIP_GATHER, name="gather_wait_1_first")
            gathered.update(zip(first, _pair_swap(landed, name="pair_swap")))
            (flying["rest"],), token = _exchange_start([[_payload(k, w[k][1]) for k in rest]], CHIP_GATHER,
                                                       name="gather_start_1_rest", after=gathered[first[0]])
            extra = {"norm1_g": replicated[1]["norm1_g"] + token[0, 0]}
        if i == 1 and group == 1:
            landed = _exchange_wait(flying["rest"], after, CHIP_GATHER, name="gather_wait_1_rest")
            gathered.update(zip(rest, _pair_swap(landed, name="pair_swap")))
        return {**{k: _as_read(k, gathered[k]) for k in FETCH_GROUPS[group]}, **extra}

    def reduce_scatter_start(members, blocks, tag):
        mine = [blocks[k] for k in members]
        theirs = _pair_exchange(mine, name="pair_exchange")
        sums = [_pair_add(a, b, name="pair_add") for a, b in zip(mine, theirs)]
        (started,), token = _exchange_start([sums], CHIP_SCATTER, name="exchange_start_" + tag)
        return started, token

    h, saved0 = _layer_fwd(x[0], p[0, 0], replicated[0], functools.partial(fetch, 0))
    h, saved1 = _layer_fwd(h, p[1, 0], replicated[1], functools.partial(fetch, 1))
    dx, dgf, loss_part = _loss_head(h, final_g[None], loss_target[0], name="loss_head")
    small, big1, flying0 = [None] * DEPTH, {}, []
    dx, small[1] = _layer_bwd(dx, saved1, lambda group, blocks: big1.update({k: blocks[k] for k in EMIT_GROUPS[group]}))
    flying1, token = reduce_scatter_start(SHARDED, big1, "1")

    def emit(group, blocks):
        started, token = reduce_scatter_start(EMIT_GROUPS[group], blocks, f"0_{group}")
        flying0.append(started)
        return token

    dx, small[0] = _layer_bwd(dx, saved0, emit, after=token)
    received = [{}, dict(zip(SHARDED, _exchange_wait(flying1, dx, CHIP_SCATTER, name="exchange_wait_1")))]
    for group, members in enumerate(EMIT_GROUPS):
        received[0].update(zip(members, _exchange_wait(flying0[group], dx, CHIP_SCATTER, name=f"exchange_wait_0_{group}")))

    grads = {k: jnp.stack([small[i][k] for i in range(DEPTH)]) for k in small[0]}
    grads = {k: g[:, 0] if k in ("norm1_g", "norm2_g", "onorm_g", "pool_scale") else g for k, g in grads.items()}
    grads["final_g"] = dgf[0]
    loss_row = jnp.pad(loss_part, ((0, 0), (0, SLAB_COLS - LANE)))
    (small_flying,), token = _exchange_start([[_pack_slab(grads, loss_row)]], GATHER, name="small_gather_start")

    out_g, out_d, out_m, out_v = {}, {}, {}, {}
    for k in SHARDED:
        out_g[k], out_d[k], out_m[k], out_v[k] = _adamw_reduce(w[k], [received[i][k] for i in range(DEPTH)], m[k], v[k],
                                                                name="adamw_" + k, after=token)
    behind_all = jnp.stack([out_v[k][0, 0, 0] for k in SHARDED])
    (small_parts,) = _exchange_wait(small_flying, behind_all, GATHER, name="small_gather_wait")
    zero_row = jnp.zeros((1, SLAB_COLS), F32)
    slabs = _adamw_reduce(_pack_slab(w, zero_row)[None], [small_parts], _pack_slab(m, zero_row)[None],
                          _pack_slab(v, zero_row)[None], name="adamw_small")
    slabs = [s[0] for s in slabs]
    shapes = {k: w[k].shape for k in SMALL}
    for dst, slab in zip((out_g, out_d, out_m, out_v), slabs):
        vals, _ = _unpack_slab(slab, shapes)
        dst.update(vals)
    _, loss_at = _unpack_slab(slabs[0], shapes)
    loss = slabs[0][loss_at, 0]
    for group in (out_g, out_d, out_m, out_v):
        group.update({k: _ff_rows(group[k]) for k in TRANSPOSED})

    return (loss, dx[None], *[out_g[k] for k in names], *[out_d[k] for k in names], *[out_m[k] for k in names],
            *[out_v[k] for k in names])
```

```python
import functools

import jax
import jax.numpy as jnp
from jax import lax
from jax.experimental import pallas as pl
from jax.experimental.pallas import tpu as pltpu

F32 = jnp.float32
BF16 = jnp.bfloat16

D_MODEL = 1024
DEPTH = 2
PLE_DIM = 256
EPS = 1e-6
HEAD_DIM = 128
HEADS = 4
A_DIM = HEADS * HEAD_DIM
QKV_TAPS = 4
CHUNK = 64
POOL_WINDOWS = (2, 4, 8, 16)
POOL_DIM = 256
CONV_DIM = 256
CONV_TAPS = 3
D_FF = 2816
D_IN = 3080
AB_COL = 2048
N_DEV = 8

ADAM_LR = 0.001
ADAM_B1 = 0.9
ADAM_B2 = 0.999
ADAM_EPS = 1e-08
ADAM_WD = 0.01
ADAM_STEP = 10

LANE = 128
SUBLANE = 8
VMEM_BYTES_V7X = 64 * 1024 * 1024
VMEM_LIMIT = VMEM_BYTES_V7X * 3 // 4

NN = ((1,), (0,))
NT = ((1,), (1,))
TN = ((0,), (0,))
MESH = pl.DeviceIdType.MESH


def _dot(a, b, dims):
    return lax.dot_general(a.astype(BF16), b.astype(BF16), (dims, ((), ())), preferred_element_type=F32)


def _pcall(body, *, name, out_shape, grid=(), in_specs=None, out_specs=None, scratch_shapes=(), semantics=None,
           vmem_limit=None, after=None, **kw):
    params = {}
    if semantics is not None:
        params["dimension_semantics"] = semantics
    if vmem_limit is not None:
        params["vmem_limit_bytes"] = vmem_limit
    if after is not None:
        n_in, inner = len(in_specs), body
        body = lambda *refs: inner(*refs[:n_in], *refs[n_in + 1:])
        in_specs = list(in_specs) + [pl.BlockSpec(after.shape, lambda *_: (0,) * after.ndim)]
    call = pl.pallas_call(
        body, name=name, out_shape=out_shape, grid=grid, in_specs=in_specs, out_specs=out_specs,
        scratch_shapes=list(scratch_shapes), compiler_params=pltpu.CompilerParams(**params), **kw)
    return call if after is None else (lambda *args: call(*args, after))


def _sigmoid(x):
    return 1.0 / (1.0 + jnp.exp(-x))


def _softplus(x):
    return jnp.maximum(x, 0.0) + jnp.log(1.0 + jnp.exp(-jnp.abs(x)))


def _tile(n, cap, mult):
    if n <= cap:
        return n
    best = None
    for t in range(mult, cap + 1, mult):
        if n % t == 0:
            best = t
    assert best is not None, (n, cap, mult)
    return best


ROWS_PER_STEP = 512
NARROW_RESULT = 1024
COLS_PER_DOT = 640


def _resident(weight):
    return pl.BlockSpec(weight.shape, lambda i: (0,) * weight.ndim, pipeline_mode=pl.Buffered(1))


def _matmul_rows(a, b, mode, *, name, res=None, out_dtype=F32, b_blocked=False, after=None, norm_g=None):
    m, k = a.shape
    if b_blocked:
        nb, _, bw = b.shape
        n = nb * bw if mode == "nn" else b.shape[1]
    else:
        n = b.shape[1] if mode == "nn" else b.shape[0]
    tm = _tile(m, ROWS_PER_STEP if n > NARROW_RESULT else 2 * ROWS_PER_STEP, 16)
    cn = bw if (b_blocked and mode == "nn") else _tile(n, COLS_PER_DOT, LANE)
    has_res = res is not None
    normed = norm_g is not None

    def body(*refs):
        a_ref, b_ref = refs[0], refs[1]
        g_ref = refs[2] if normed else None
        res_ref = refs[2 + normed] if has_res else None
        o_ref = refs[2 + normed + has_res]
        if normed:
            av = _rms_normed(a_ref[...], g_ref[...])
            refs[3 + normed + has_res][...] = av
        elif not (b_blocked and mode == "nt"):
            av = a_ref[...].astype(BF16)
        for j in range(n // cn):
            cols = pl.ds(j * cn, cn)
            if mode == "nn":
                part = _dot(av, b_ref[j] if b_blocked else b_ref[:, cols], NN)
            elif not b_blocked:
                part = _dot(av, b_ref[cols, :], NT)
            else:
                part = None
                for s in range(nb):
                    term = _dot(a_ref[:, pl.ds(s * bw, bw)], b_ref[s, cols, :], NT)
                    part = term if part is None else part + term
            if has_res:
                part = part + res_ref[:, cols]
            o_ref[:, cols] = part.astype(o_ref.dtype)

    row = lambda width: pl.BlockSpec((tm, width), lambda i: (i, 0))
    whole = _resident(b)
    ins = [a, b] + ([norm_g] if normed else []) + ([res] if has_res else [])
    specs = [row(k), whole] + ([pl.BlockSpec((1, k), lambda i: (0, 0))] if normed else []) + ([row(n)] if has_res else [])
    out = jax.ShapeDtypeStruct((m, n), out_dtype)
    return _pcall(body, name=name, out_shape=(out, jax.ShapeDtypeStruct((m, k), BF16)) if normed else out, grid=(m // tm,),
                  in_specs=specs, out_specs=(row(n), row(k)) if normed else row(n), semantics=("parallel",),
                  vmem_limit=VMEM_LIMIT, after=after)(*ins)


def _matmul_norm_bwd(a, b, x, g, dres, *, name, more=None, after=None):
    m, k = a.shape
    d = b.shape[1]
    tm = _tile(m, ROWS_PER_STEP, 16)
    cn = _tile(d, COLS_PER_DOT, LANE)
    pairs = 1 if more is None else 2

    def body(*refs):
        a_ref, b_ref, x_ref, g_ref, dres_ref = refs[:5]
        dx_ref, dg_ref = refs[3 + 2 * pairs], refs[4 + 2 * pairs]
        av = a_ref[...].astype(BF16)
        for j in range(d // cn):
            cols = pl.ds(j * cn, cn)
            part = _dot(av, b_ref[:, cols], NN)
            if more is not None:
                part = part + _dot(refs[5][...], refs[6][:, cols], NN)
            dx_ref[:, cols] = part
        dhv = dx_ref[...]
        xv = x_ref[...]
        r = lax.rsqrt(jnp.mean(xv * xv, axis=-1, keepdims=True) + EPS)
        xhat = xv * r
        dhg = dhv * g_ref[...]
        dx_ref[...] = dres_ref[...] + r * (dhg - xhat * jnp.mean(dhg * xhat, axis=-1, keepdims=True))
        part_g = jnp.sum(dhv * xhat, axis=0, keepdims=True)

        @pl.when(pl.program_id(0) == 0)
        def _():
            dg_ref[...] = part_g

        @pl.when(pl.program_id(0) > 0)
        def _():
            dg_ref[...] += part_g

    row = lambda width: pl.BlockSpec((tm, width), lambda i: (i, 0))
    vec = pl.BlockSpec((1, d), lambda i: (0, 0))
    ins = [a, b, x, g, dres] + (list(more) if more is not None else [])
    specs = [row(k), _resident(b), row(d), vec, row(d)] + ([row(more[0].shape[1]), _resident(more[1])] if more is not None else [])
    return _pcall(body, name=name, out_shape=(jax.ShapeDtypeStruct((m, d), F32), jax.ShapeDtypeStruct((1, d), F32)),
                  grid=(m // tm,), in_specs=specs, out_specs=(row(d), vec), semantics=("arbitrary",), vmem_limit=VMEM_LIMIT,
                  after=after)(*ins)


def _rms_normed(xv, gv):
    return (xv * lax.rsqrt(jnp.mean(xv * xv, axis=-1, keepdims=True) + EPS) * gv).astype(BF16)


def _matmul(a, b, mode, *, name, res=None, out_dtype=F32, b_blocked=False, out_blocked=None, after=None, norm_g=None):
    if mode != "tn":
        return _matmul_rows(a, b, mode, name=name, res=res, out_dtype=out_dtype, b_blocked=b_blocked, after=after, norm_g=norm_g)
    assert res is None and not b_blocked and after is None and norm_g is None
    (t, m), (t2, n) = a.shape, b.shape
    assert t == t2, (a.shape, b.shape)
    tm = _tile(m, 1024, LANE)
    tn = _tile(n, NARROW_RESULT if n <= NARROW_RESULT else COLS_PER_DOT, LANE)
    if out_blocked is not None:
        assert out_blocked[0] * out_blocked[1] == n
        tn = out_blocked[1]

    def body(a_ref, b_ref, o_ref):
        part = _dot(a_ref[...], b_ref[...], TN).astype(o_ref.dtype)
        if out_blocked is None:
            o_ref[...] = part
        else:
            o_ref[0] = part

    o_spec = (pl.BlockSpec((tm, tn), lambda i, j: (i, j)) if out_blocked is None
              else pl.BlockSpec((1, tm, tn), lambda i, j: (j, i, 0)))
    o_shape = (m, n) if out_blocked is None else (out_blocked[0], m, out_blocked[1])
    return _pcall(body, name=name, out_shape=jax.ShapeDtypeStruct(o_shape, out_dtype), grid=(m // tm, n // tn),
                  in_specs=[pl.BlockSpec((t, tm), lambda i, j: (0, i)), pl.BlockSpec((t, tn), lambda i, j: (0, j))],
                  out_specs=o_spec, semantics=("parallel", "parallel"), vmem_limit=VMEM_LIMIT)(a, b)


ROW_TILE = 512


def _rows(t, width, idx=0):
    return pl.BlockSpec((ROW_TILE, width), lambda i: (i, idx))


def _vec(width):
    return pl.BlockSpec((1, width), lambda i: (0, 0))


def _swiglu_fwd(x, norm_g, w_gate, w_up, *, name):
    t, k = x.shape
    f = w_gate.shape[0]
    tm = _tile(t, ROWS_PER_STEP, 16)
    cn = _tile(f, COLS_PER_DOT, LANE)

    def body(x_ref, g_ref, wg_ref, wu_ref, ff_ref, gate_ref, up_ref, h_ref):
        hv = _rms_normed(x_ref[...], g_ref[...])
        h_ref[...] = hv
        for j in range(f // cn):
            cols = pl.ds(j * cn, cn)
            gv = _dot(hv, wg_ref[cols, :], NT)
            uv = _dot(hv, wu_ref[cols, :], NT)
            gate_ref[:, cols] = gv.astype(BF16)
            up_ref[:, cols] = uv.astype(BF16)
            ff_ref[:, cols] = (gv * _sigmoid(gv) * uv).astype(BF16)

    row = lambda width: pl.BlockSpec((tm, width), lambda i: (i, 0))
    out = jax.ShapeDtypeStruct((t, f), BF16)
    return _pcall(body, name=name, out_shape=(out,) * 3 + (jax.ShapeDtypeStruct((t, k), BF16),), grid=(t // tm,),
                  in_specs=[row(k), pl.BlockSpec((1, k), lambda i: (0, 0)), _resident(w_gate), _resident(w_up)],
                  out_specs=(row(f),) * 3 + (row(k),), semantics=("parallel",), vmem_limit=VMEM_LIMIT)(x, norm_g, w_gate, w_up)


def _swiglu_bwd(dx2, w_down, gate, up, *, name, after=None):
    t, d = dx2.shape
    f = w_down.shape[0]
    tm = _tile(t, ROWS_PER_STEP, 16)
    cn = _tile(f, COLS_PER_DOT, LANE)

    def body(dx_ref, w_ref, gate_ref, up_ref, dgate_ref, dup_ref):
        dxv = dx_ref[...].astype(BF16)
        for j in range(f // cn):
            cols = pl.ds(j * cn, cn)
            dffv = _dot(dxv, w_ref[cols, :], NT)
            gv = gate_ref[:, cols].astype(F32)
            sig = _sigmoid(gv)
            dgate_ref[:, cols] = (dffv * up_ref[:, cols].astype(F32) * sig * (1.0 + gv * (1.0 - sig))).astype(BF16)
            dup_ref[:, cols] = (dffv * gv * sig).astype(BF16)

    row = lambda width: pl.BlockSpec((tm, width), lambda i: (i, 0))
    out = jax.ShapeDtypeStruct((t, f), BF16)
    return _pcall(body, name=name, out_shape=(out, out), grid=(t // tm,), in_specs=[row(d), _resident(w_down), row(f), row(f)],
                  out_specs=(row(f), row(f)), semantics=("parallel",), vmem_limit=VMEM_LIMIT, after=after)(dx2, w_down, gate, up)


def _ple_fwd(x2, p, w_gate, w_proj, *, name):
    t, d = x2.shape
    nb, pdim, bw = w_proj.shape
    tm = _tile(t, ROWS_PER_STEP, 16)
    cn = _tile(d, COLS_PER_DOT, LANE)

    def body(x_ref, p_ref, wg_ref, wp_ref, x3_ref, pgl_ref, pp_ref):
        xb = x_ref[...].astype(BF16)
        pb = p_ref[...].astype(BF16)
        per = cn // bw
        for c in range(d // cn):
            cols = pl.ds(c * cn, cn)
            pgl = _dot(xb, wg_ref[:, cols], NN)
            pp = jnp.concatenate([_dot(pb, wp_ref[c * per + j], NN) for j in range(per)], axis=1)
            pgl_ref[:, cols] = pgl
            pp_ref[:, cols] = pp
            x3_ref[:, cols] = x_ref[:, cols] + _sigmoid(pgl) * pp

    row = lambda width: pl.BlockSpec((tm, width), lambda i: (i, 0))
    out = jax.ShapeDtypeStruct((t, d), F32)
    return _pcall(body, name=name, out_shape=(out,) * 3, grid=(t // tm,),
                  in_specs=[row(d), row(pdim), _resident(w_gate), _resident(w_proj)], out_specs=(row(d),) * 3,
                  semantics=("parallel",), vmem_limit=VMEM_LIMIT)(x2, p, w_gate, w_proj)


def _ple_bwd(dx3, pgl, pp, *, name, after=None):
    t, d = dx3.shape

    def body(dx_ref, pgl_ref, pp_ref, dpgl_ref, dpp_ref):
        dxv = dx_ref[...]
        sig = _sigmoid(pgl_ref[...])
        dpp_ref[...] = (dxv * sig).astype(BF16)
        dpgl_ref[...] = (dxv * pp_ref[...] * sig * (1.0 - sig)).astype(BF16)

    return _pcall(body, name=name, out_shape=(jax.ShapeDtypeStruct((t, d), BF16),) * 2, grid=(t // ROW_TILE,),
                  in_specs=[_rows(t, d)] * 3, out_specs=(_rows(t, d),) * 2, semantics=("parallel",), after=after)(dx3, pgl, pp)


def _loss_head(x3, g, target, *, name):
    t, d = x3.shape

    def body(x_ref, g_ref, t_ref, dx_ref, dg_ref, loss_ref):
        xv = x_ref[...]
        r = lax.rsqrt(jnp.mean(xv * xv, axis=-1, keepdims=True) + EPS)
        xhat = xv * r
        gv = g_ref[...]
        err = xhat * gv - t_ref[...]
        row_loss = jnp.sum(err * err, axis=-1, keepdims=True) * (0.5 / d)
        lpart = jnp.broadcast_to(jnp.sum(row_loss, axis=0, keepdims=True), (1, LANE))
        dy = err * (1.0 / d)
        dyg = dy * gv
        dx_ref[...] = r * (dyg - xhat * jnp.mean(dyg * xhat, axis=-1, keepdims=True))
        gpart = jnp.sum(dy * xhat, axis=0, keepdims=True)

        @pl.when(pl.program_id(0) == 0)
        def _():
            dg_ref[...] = gpart
            loss_ref[...] = lpart

        @pl.when(pl.program_id(0) > 0)
        def _():
            dg_ref[...] += gpart
            loss_ref[...] += lpart

    return _pcall(body, name=name,
                  out_shape=(jax.ShapeDtypeStruct((t, d), F32), jax.ShapeDtypeStruct((1, d), F32), jax.ShapeDtypeStruct((1, LANE), F32)),
                  grid=(t // ROW_TILE,), in_specs=[_rows(t, d), _vec(d), _rows(t, d)],
                  out_specs=(_rows(t, d), _vec(d), _vec(LANE)), semantics=("arbitrary",))(x3, g, target)


def _shift_down(x, d):
    if d == 0:
        return x
    row = lax.broadcasted_iota(jnp.int32, x.shape, 0)
    return jnp.where(row >= d, pltpu.roll(x, d, 0), 0.0)


def _shift_up(x, d):
    if d == 0:
        return x
    t = x.shape[0]
    row = lax.broadcasted_iota(jnp.int32, x.shape, 0)
    return jnp.where(row < t - d, pltpu.roll(x, t - d, 0), 0.0)


def _colsum(x):
    return jnp.sum(x, axis=0, keepdims=True)


def _col(t, idx_fn):
    return pl.BlockSpec((t, LANE), idx_fn)


def _conv_fwd(x, w_ref, taps):
    acc = None
    for j in range(taps):
        term = w_ref[pl.ds(j, 1), :] * _shift_down(x, taps - 1 - j)
        acc = term if acc is None else acc + term
    return acc


def _conv_bwd(x, dy, w_ref, dw_ref, taps):
    dx = None
    for j in range(taps):
        term = w_ref[pl.ds(j, 1), :] * _shift_up(dy, taps - 1 - j)
        dx = term if dx is None else dx + term
        dw_ref[pl.ds(j, 1), :] = _colsum(dy * _shift_down(x, taps - 1 - j))
    return dx


def _qkv_prep_fwd(proj, conv_w, *, name):
    t = proj.shape[0]
    scale = HEAD_DIM ** -0.5

    def body(x_ref, w_ref, o_ref):
        j = pl.program_id(0)
        c = _conv_fwd(x_ref[...], w_ref, QKV_TAPS)
        s = c * _sigmoid(c)
        r = lax.rsqrt(jnp.sum(s * s, axis=-1, keepdims=True) + EPS)
        f = jnp.where(j < 2 * HEADS, r, 1.0) * jnp.where(j < HEADS, scale, 1.0)
        o_ref[0] = s * f

    return _pcall(body, name=name, out_shape=jax.ShapeDtypeStruct((3 * HEADS, t, LANE), F32), grid=(3 * HEADS,),
                  in_specs=[_col(t, lambda j: (0, j)), pl.BlockSpec((QKV_TAPS, LANE), lambda j: (0, j))],
                  out_specs=pl.BlockSpec((1, t, LANE), lambda j: (j, 0, 0)), semantics=("parallel",),
                  vmem_limit=VMEM_LIMIT)(proj, conv_w)


def _qkv_prep_bwd(proj, conv_w, dqkv, *, name):
    t = proj.shape[0]
    scale = HEAD_DIM ** -0.5

    def body(x_ref, w_ref, d_ref, dx_ref, dw_ref):
        j = pl.program_id(0)
        xv = x_ref[...]
        c = _conv_fwd(xv, w_ref, QKV_TAPS)
        sig = _sigmoid(c)
        s = c * sig
        r = lax.rsqrt(jnp.sum(s * s, axis=-1, keepdims=True) + EPS)
        n0 = s * r
        dv = d_ref[0]
        dn0 = dv * jnp.where(j < HEADS, scale, 1.0)
        ds_norm = r * (dn0 - n0 * jnp.sum(dn0 * n0, axis=-1, keepdims=True))
        ds = jnp.where(j < 2 * HEADS, ds_norm, dv)
        dc = ds * sig * (1.0 + c * (1.0 - sig))
        dx_ref[...] = _conv_bwd(xv, dc, w_ref, dw_ref, QKV_TAPS).astype(BF16)

    return _pcall(body, name=name,
                  out_shape=(jax.ShapeDtypeStruct((t, 3 * A_DIM), BF16), jax.ShapeDtypeStruct((QKV_TAPS, 3 * A_DIM), F32)),
                  grid=(3 * HEADS,),
                  in_specs=[_col(t, lambda j: (0, j)), pl.BlockSpec((QKV_TAPS, LANE), lambda j: (0, j)),
                            pl.BlockSpec((1, t, LANE), lambda j: (j, 0, 0))],
                  out_specs=(_col(t, lambda j: (0, j)), pl.BlockSpec((QKV_TAPS, LANE), lambda j: (0, j))),
                  semantics=("parallel",), vmem_limit=VMEM_LIMIT)(proj, conv_w, dqkv)


def _lane_pick(x, lane_idx, lane):
    return jnp.broadcast_to(jnp.sum(jnp.where(lane == lane_idx, x, 0.0), axis=-1, keepdims=True), x.shape)


def _gates_fwd(proj, alog, dtb, *, name):
    t = proj.shape[0]

    def body(x_ref, alog_ref, dtb_ref, g_ref, b_ref):
        xv = x_ref[...]
        lane = lax.broadcasted_iota(jnp.int32, xv.shape, 1)
        gall = -jnp.exp(alog_ref[...]) * _softplus(xv + dtb_ref[...])
        ball = _sigmoid(xv)
        for h in range(HEADS):
            g_ref[h] = _lane_pick(gall, h, lane)
            b_ref[h] = _lane_pick(ball, HEADS + h, lane)

    out = jax.ShapeDtypeStruct((HEADS, t, LANE), F32)
    whole = pl.BlockSpec((HEADS, t, LANE), lambda i: (0, 0, 0))
    return _pcall(body, name=name, out_shape=(out, out), grid=(1,),
                  in_specs=[_col(t, lambda i: (0, AB_COL // LANE)), _vec(LANE), _vec(LANE)], out_specs=(whole, whole),
                  semantics=("arbitrary",), vmem_limit=VMEM_LIMIT)(proj, alog, dtb)


def _gates_bwd(proj, alog, dtb, dg, dbeta, *, name):
    t = proj.shape[0]

    def body(x_ref, alog_ref, dtb_ref, dg_ref, db_ref, dab_ref, dalog_ref, ddtb_ref):
        xv = x_ref[...]
        lane = lax.broadcasted_iota(jnp.int32, xv.shape, 1)
        lane1 = lax.broadcasted_iota(jnp.int32, (1, LANE), 1)
        z = xv + dtb_ref[...]
        nea = -jnp.exp(alog_ref[...])
        da_f = nea * _sigmoid(z)
        g_f = nea * _softplus(z)
        ball = _sigmoid(xv)
        db_f = ball * (1.0 - ball)
        dab = jnp.zeros_like(xv)
        dalog = jnp.zeros((1, LANE), F32)
        for h in range(HEADS):
            dgh = dg_ref[h]
            dab = dab + jnp.where(lane == h, dgh * da_f, 0.0) + jnp.where(lane == HEADS + h, db_ref[h] * db_f, 0.0)
            dalog = dalog + jnp.where(lane1 == h, _colsum(dgh * g_f), 0.0)
        dab_ref[...] = dab.astype(BF16)
        dalog_ref[...] = dalog
        ddtb_ref[...] = jnp.where(lane1 < HEADS, _colsum(dab), 0.0)

    whole = pl.BlockSpec((HEADS, t, LANE), lambda i: (0, 0, 0))
    vec = jax.ShapeDtypeStruct((1, LANE), F32)
    return _pcall(body, name=name, out_shape=(jax.ShapeDtypeStruct((t, LANE), BF16), vec, vec), grid=(1,),
                  in_specs=[_col(t, lambda i: (0, AB_COL // LANE)), _vec(LANE), _vec(LANE), whole, whole],
                  out_specs=(_col(t, lambda i: (0, 0)), _vec(LANE), _vec(LANE)), semantics=("arbitrary",),
                  vmem_limit=VMEM_LIMIT)(proj, alog, dtb, dg, dbeta)


Z_COL = 3 * A_DIM // LANE


def _apost_fwd(o, proj, gn, *, name):
    t = proj.shape[0]

    def body(o_ref, z_ref, gn_ref, y_ref):
        ov = o_ref[0]
        z = z_ref[...]
        r = lax.rsqrt(jnp.mean(ov * ov, axis=-1, keepdims=True) + EPS)
        y_ref[...] = (ov * r * gn_ref[...] * (z * _sigmoid(z))).astype(BF16)

    return _pcall(body, name=name, out_shape=jax.ShapeDtypeStruct((t, A_DIM), BF16), grid=(HEADS,),
                  in_specs=[pl.BlockSpec((1, t, LANE), lambda h: (h, 0, 0)), _col(t, lambda h: (0, Z_COL + h)),
                            pl.BlockSpec((1, LANE), lambda h: (0, 0))],
                  out_specs=_col(t, lambda h: (0, h)), semantics=("parallel",), vmem_limit=VMEM_LIMIT)(o, proj, gn)


def _apost_bwd(o, proj, gn, dmixed, *, name):
    t = proj.shape[0]

    def body(o_ref, z_ref, gn_ref, d_ref, do_ref, dz_ref, dgn_ref):
        ov = o_ref[0]
        z = z_ref[...]
        gnv = gn_ref[...]
        dv = d_ref[...]
        r = lax.rsqrt(jnp.mean(ov * ov, axis=-1, keepdims=True) + EPS)
        ohat = ov * r
        sig = _sigmoid(z)
        dy = dv * (z * sig)
        dz_ref[...] = (dv * ohat * gnv * sig * (1.0 + z * (1.0 - sig))).astype(BF16)
        dyo = dy * gnv
        do_ref[0] = r * (dyo - ohat * jnp.mean(dyo * ohat, axis=-1, keepdims=True))
        part = _colsum(dy * ohat)

        @pl.when(pl.program_id(0) == 0)
        def _():
            dgn_ref[...] = part

        @pl.when(pl.program_id(0) > 0)
        def _():
            dgn_ref[...] += part

    return _pcall(body, name=name,
                  out_shape=(jax.ShapeDtypeStruct((HEADS, t, LANE), F32), jax.ShapeDtypeStruct((t, A_DIM), BF16),
                             jax.ShapeDtypeStruct((1, LANE), F32)),
                  grid=(HEADS,),
                  in_specs=[pl.BlockSpec((1, t, LANE), lambda h: (h, 0, 0)), _col(t, lambda h: (0, Z_COL + h)),
                            pl.BlockSpec((1, LANE), lambda h: (0, 0)), _col(t, lambda h: (0, h))],
                  out_specs=(pl.BlockSpec((1, t, LANE), lambda h: (h, 0, 0)), _col(t, lambda h: (0, h)),
                             pl.BlockSpec((1, LANE), lambda h: (0, 0))),
                  semantics=("arbitrary",), vmem_limit=VMEM_LIMIT)(o, proj, gn, dmixed)


POOL_COL = (AB_COL + LANE) // LANE
CB_COL = POOL_COL + POOL_DIM // LANE
CC_COL = CB_COL + CONV_DIM // LANE
CH_COL = CC_COL + CONV_DIM // LANE
MAX_WIN_LOG2 = 4


def _window_sums(x, shift):
    sums = []
    cur = x
    for k in range(MAX_WIN_LOG2):
        cur = cur + shift(cur, 1 << k)
        sums.append(cur)
    return sums


def _pick_window(sums, win):
    out = sums[-1]
    for k in range(MAX_WIN_LOG2 - 2, -1, -1):
        out = jnp.where(win == float(2 << k), sums[k], out)
    return out


def _pool_counts(shape, win):
    row = lax.broadcasted_iota(jnp.int32, shape, 0).astype(F32)
    return jnp.minimum(row + 1.0, win)


def _pool_fwd(proj, win, wbd, scale, *, name):
    t = proj.shape[0]

    def body(x_ref, win_ref, w_ref, s_ref, y_ref):
        xv = x_ref[...]
        winv = win_ref[...]
        pooled = _pick_window(_window_sums(xv, _shift_down), winv) / _pool_counts(xv.shape, winv) - xv
        y_ref[...] = (_dot(pooled, w_ref[0], NN) * s_ref[...]).astype(BF16)

    nb = POOL_DIM // LANE
    vec = pl.BlockSpec((1, LANE), lambda b: (0, b))
    return _pcall(body, name=name, out_shape=jax.ShapeDtypeStruct((t, POOL_DIM), BF16), grid=(nb,),
                  in_specs=[_col(t, lambda b: (0, POOL_COL + b)), vec, pl.BlockSpec((1, LANE, LANE), lambda b: (b, 0, 0)), vec],
                  out_specs=_col(t, lambda b: (0, b)), semantics=("parallel",), vmem_limit=VMEM_LIMIT)(proj, win, wbd, scale)


def _pool_bwd(proj, win, wbd, scale, dmixed, *, name):
    t = proj.shape[0]

    def body(x_ref, win_ref, w_ref, s_ref, d_ref, dx_ref, dw_ref, ds_ref):
        xv = x_ref[...]
        winv = win_ref[...]
        cnt = _pool_counts(xv.shape, winv)
        pooled = _pick_window(_window_sums(xv, _shift_down), winv) / cnt - xv
        dv = d_ref[...]
        ds_ref[...] = _colsum(dv * _dot(pooled, w_ref[0], NN))
        dy0 = dv * s_ref[...]
        dw_ref[0] = _dot(pooled, dy0, TN)
        dpooled = _dot(dy0, w_ref[0], NT)
        dmean = dpooled / cnt
        dx_ref[...] = (_pick_window(_window_sums(dmean, _shift_up), winv) - dpooled).astype(BF16)

    nb = POOL_DIM // LANE
    vec = pl.BlockSpec((1, LANE), lambda b: (0, b))
    mat = pl.BlockSpec((1, LANE, LANE), lambda b: (b, 0, 0))
    first = A_DIM // LANE
    return _pcall(body, name=name,
                  out_shape=(jax.ShapeDtypeStruct((t, POOL_DIM), BF16), jax.ShapeDtypeStruct((nb, LANE, LANE), F32),
                             jax.ShapeDtypeStruct((1, POOL_DIM), F32)),
                  grid=(nb,),
                  in_specs=[_col(t, lambda b: (0, POOL_COL + b)), vec, mat, vec, _col(t, lambda b: (0, first + b))],
                  out_specs=(_col(t, lambda b: (0, b)), mat, vec), semantics=("parallel",),
                  vmem_limit=VMEM_LIMIT)(proj, win, wbd, scale, dmixed)


def _sconv_fwd(proj, w, *, name):
    t = proj.shape[0]

    def body(cb_ref, cc_ref, ch_ref, w_ref, y_ref):
        y_ref[...] = (cb_ref[...] * _conv_fwd(cc_ref[...] * ch_ref[...], w_ref, CONV_TAPS)).astype(BF16)

    nb = CONV_DIM // LANE
    return _pcall(body, name=name, out_shape=jax.ShapeDtypeStruct((t, CONV_DIM), BF16), grid=(nb,),
                  in_specs=[_col(t, lambda b: (0, CB_COL + b)), _col(t, lambda b: (0, CC_COL + b)),
                            _col(t, lambda b: (0, CH_COL + b)), pl.BlockSpec((CONV_TAPS, LANE), lambda b: (0, b))],
                  out_specs=_col(t, lambda b: (0, b)), semantics=("parallel",), vmem_limit=VMEM_LIMIT)(proj, proj, proj, w)


def _sconv_bwd(proj, w, dmixed, *, name):
    t = proj.shape[0]

    def body(cb_ref, cc_ref, ch_ref, w_ref, d_ref, dcb_ref, dcc_ref, dch_ref, dw_ref):
        cc = cc_ref[...]
        ch = ch_ref[...]
        u = cc * ch
        dv = d_ref[...]
        dcb_ref[...] = (dv * _conv_fwd(u, w_ref, CONV_TAPS)).astype(BF16)
        du = _conv_bwd(u, dv * cb_ref[...], w_ref, dw_ref, CONV_TAPS)
        dcc_ref[...] = (du * ch).astype(BF16)
        dch_ref[...] = (du * cc).astype(BF16)

    nb = CONV_DIM // LANE
    first = (A_DIM + POOL_DIM) // LANE
    act = jax.ShapeDtypeStruct((t, CONV_DIM), BF16)
    wspec = pl.BlockSpec((CONV_TAPS, LANE), lambda b: (0, b))
    ospec = _col(t, lambda b: (0, b))
    return _pcall(body, name=name, out_shape=(act, act, act, jax.ShapeDtypeStruct((CONV_TAPS, CONV_DIM), F32)), grid=(nb,),
                  in_specs=[_col(t, lambda b: (0, CB_COL + b)), _col(t, lambda b: (0, CC_COL + b)),
                            _col(t, lambda b: (0, CH_COL + b)), wspec, _col(t, lambda b: (0, first + b))],
                  out_specs=(ospec, ospec, ospec, wspec), semantics=("parallel",),
                  vmem_limit=VMEM_LIMIT)(proj, proj, proj, w, dmixed)


def _chunk_masks():
    r = lax.broadcasted_iota(jnp.int32, (CHUNK, CHUNK), 0)
    c = lax.broadcasted_iota(jnp.int32, (CHUNK, CHUNK), 1)
    return r >= c, r > c, jnp.where(r == c, 1.0, 0.0).astype(F32)


def _split(a):
    hi = a.astype(BF16)
    return hi, (a - hi.astype(F32)).astype(BF16)


def _dot_split(a, b, dims):
    (ah, al), (bh, bl) = a, b
    return _dot(ah, bh, dims) + _dot(ah, bl, dims) + _dot(al, bh, dims)


def _tri_inv(lows, eye):
    xs = [eye - low for low in lows]
    ps = [_split(low) for low in lows]
    ps = [_split(_dot_split(p, p, NN)) for p in ps]
    for i in range(5):
        xs = [x + _dot_split(_split(x), p, NN) for x, p in zip(xs, ps)]
        if i < 4:
            ps = [_split(_dot_split(p, p, NN)) for p in ps]
    return xs


def _prefix_sum_rows(x):
    for k in range(6):
        x = x + _shift_down(x, 1 << k)
    return x


def _suffix_sum_rows(x):
    for k in range(6):
        x = x + _shift_up(x, 1 << k)
    return x


def _chunk_decay(g, incl):
    gcb = _prefix_sum_rows(g)
    gtot = _colsum(g)
    col = gcb[:, :CHUNK]
    row = gcb.T[:CHUNK, :]
    decay = jnp.exp(jnp.where(incl, col - row, -1e30))
    return gcb, gtot, decay


CHUNKS_PER_STEP = 4


def _chunk_rows(j):
    return pl.ds(j * CHUNK, CHUNK)


def _deltanet_prep(qkv, g, beta, *, name):
    t = qkv.shape[1]
    n_chunks = t // CHUNK
    per = CHUNKS_PER_STEP
    probs = [(j, h) for j in range(per) for h in range(HEADS)]

    def body(qkv_ref, g_ref, b_ref, u_ref, w_ref, qg_ref, kg_ref, attn_ref, tm_ref):
        incl, strict, eye = _chunk_masks()
        q = [qkv_ref[h, _chunk_rows(j), :] for j, h in probs]
        k = [qkv_ref[HEADS + h, _chunk_rows(j), :] for j, h in probs]
        v = [qkv_ref[2 * HEADS + h, _chunk_rows(j), :] for j, h in probs]
        bv = [b_ref[h, _chunk_rows(j), :] for j, h in probs]
        dec = [_chunk_decay(g_ref[h, _chunk_rows(j), :], incl) for j, h in probs]
        kb = [a * b for a, b in zip(k, bv)]
        low = [jnp.where(strict, _dot(a, b, NT) * d[2], 0.0) for a, b, d in zip(kb, k, dec)]
        tm = _tri_inv(low, eye)
        egc = [jnp.exp(d[0]) for d in dec]
        u = [_dot(m, a * b, NN) for m, a, b in zip(tm, v, bv)]
        w = [_dot(m, a * e, NN) for m, a, e in zip(tm, kb, egc)]
        attn = [_dot(a, b, NT) * d[2] for a, b, d in zip(q, k, dec)]
        for i, (j, h) in enumerate(probs):
            rows = _chunk_rows(j)
            u_ref[h, rows, :] = u[i]
            w_ref[h, rows, :] = w[i].astype(BF16)
            qg_ref[h, rows, :] = (q[i] * egc[i]).astype(BF16)
            kg_ref[h, rows, :] = (k[i] * jnp.exp(dec[i][1] - dec[i][0])).astype(BF16)
            attn_ref[j, h] = attn[i].astype(BF16)
            tm_ref[j, h] = tm[i]

    act = lambda heads: pl.BlockSpec((heads, per * CHUNK, LANE), lambda n: (0, n, 0))
    mat = pl.BlockSpec((per, HEADS, CHUNK, CHUNK), lambda n: (n, 0, 0, 0))
    return _pcall(
        body, name=name,
        out_shape=(jax.ShapeDtypeStruct((HEADS, t, LANE), F32),) + (jax.ShapeDtypeStruct((HEADS, t, LANE), BF16),) * 3
        + (jax.ShapeDtypeStruct((n_chunks, HEADS, CHUNK, CHUNK), BF16), jax.ShapeDtypeStruct((n_chunks, HEADS, CHUNK, CHUNK), F32)),
        grid=(n_chunks // per,), in_specs=[act(3 * HEADS), act(HEADS), act(HEADS)],
        out_specs=(act(HEADS),) * 4 + (mat, mat), semantics=("parallel",), vmem_limit=VMEM_LIMIT)(qkv, g, beta)


SCAN_CHUNKS_PER_STEP = 8


def _deltanet_scan(u, w, qg, kg, attn, g, *, name):
    t = u.shape[1]
    n_chunks = t // CHUNK
    per = SCAN_CHUNKS_PER_STEP

    def body(u_ref, w_ref, qg_ref, kg_ref, attn_ref, g_ref, o_ref, vn_ref, st_ref, s_ref):
        @pl.when(pl.program_id(0) == 0)
        def _():
            s_ref[...] = jnp.zeros_like(s_ref)

        for j in range(per):
            rows = _chunk_rows(j)
            s = [s_ref[h] for h in range(HEADS)]
            vn = [u_ref[h, rows, :] - _dot(w_ref[h, rows, :], s[h], NN) for h in range(HEADS)]
            o = [_dot(qg_ref[h, rows, :], s[h], NN) + _dot(attn_ref[j, h], vn[h], NN) for h in range(HEADS)]
            eg = [jnp.exp(_colsum(g_ref[h, rows, :])) for h in range(HEADS)]
            for h in range(HEADS):
                st_ref[j, h] = s[h]
                s_ref[h] = s[h] * eg[h] + _dot(kg_ref[h, rows, :], vn[h], TN)
                o_ref[h, rows, :] = o[h]
                vn_ref[h, rows, :] = vn[h]

    act = pl.BlockSpec((HEADS, per * CHUNK, LANE), lambda n: (0, n, 0))
    out = jax.ShapeDtypeStruct((HEADS, t, LANE), F32)
    return _pcall(
        body, name=name, out_shape=(out, out, jax.ShapeDtypeStruct((n_chunks, HEADS, LANE, LANE), F32)), grid=(n_chunks // per,),
        in_specs=[act] * 4 + [pl.BlockSpec((per, HEADS, CHUNK, CHUNK), lambda n: (n, 0, 0, 0)), act],
        out_specs=(act, act, pl.BlockSpec((per, HEADS, LANE, LANE), lambda n: (n, 0, 0, 0))),
        scratch_shapes=[pltpu.VMEM((HEADS, LANE, LANE), F32)], semantics=("arbitrary",))(u, w, qg, kg, attn, g)


def _deltanet_bscan(w, qg, kg, attn, g, do, *, name):
    t = w.shape[1]
    n_chunks = t // CHUNK
    per = SCAN_CHUNKS_PER_STEP
    steps = n_chunks // per

    def body(w_ref, qg_ref, kg_ref, attn_ref, g_ref, do_ref, dvn_ref, dsn_ref, ds_ref):
        @pl.when(pl.program_id(0) == 0)
        def _():
            ds_ref[...] = jnp.zeros_like(ds_ref)

        for j in reversed(range(per)):
            rows = _chunk_rows(j)
            dsn = [ds_ref[h] for h in range(HEADS)]
            dov = [do_ref[h, rows, :] for h in range(HEADS)]
            dvn = [_dot(attn_ref[j, h], dov[h], TN) + _dot(kg_ref[h, rows, :], dsn[h], NN) for h in range(HEADS)]
            eg = [jnp.exp(_colsum(g_ref[h, rows, :])) for h in range(HEADS)]
            for h in range(HEADS):
                dsn_ref[j, h] = dsn[h]
                ds_ref[h] = _dot(qg_ref[h, rows, :], dov[h], TN) + eg[h] * dsn[h] - _dot(w_ref[h, rows, :], dvn[h], TN)
                dvn_ref[h, rows, :] = dvn[h]

    act = pl.BlockSpec((HEADS, per * CHUNK, LANE), lambda n: (0, steps - 1 - n, 0))
    return _pcall(
        body, name=name,
        out_shape=(jax.ShapeDtypeStruct((HEADS, t, LANE), F32), jax.ShapeDtypeStruct((n_chunks, HEADS, LANE, LANE), F32)),
        grid=(steps,),
        in_specs=[act] * 3 + [pl.BlockSpec((per, HEADS, CHUNK, CHUNK), lambda n: (steps - 1 - n, 0, 0, 0)), act, act],
        out_specs=(act, pl.BlockSpec((per, HEADS, LANE, LANE), lambda n: (steps - 1 - n, 0, 0, 0))),
        scratch_shapes=[pltpu.VMEM((HEADS, LANE, LANE), F32)], semantics=("arbitrary",))(w, qg, kg, attn, g, do)


def _sum_all(x):
    return jnp.sum(jnp.sum(x, axis=1, keepdims=True), axis=0, keepdims=True)


def _rowsum(x):
    return jnp.sum(x, axis=1, keepdims=True)


def _deltanet_post(qkv, g, beta, tmats, states, dstates, do, dvn, vn, *, name):
    t = qkv.shape[1]
    n_chunks = t // CHUNK
    per = CHUNKS_PER_STEP
    probs = [(j, h) for j in range(per) for h in range(HEADS)]

    def body(qkv_ref, g_ref, b_ref, tm_ref, st_ref, dsn_ref, do_ref, dvn_ref, vn_ref, dqkv_ref, dg_ref, db_ref):
        incl, strict, _ = _chunk_masks()
        ones = jnp.ones((CHUNK, LANE), BF16)
        last_row = lax.broadcasted_iota(jnp.int32, (CHUNK, LANE), 0) == CHUNK - 1
        z = lambda f, *cols: [f(*a) for a in zip(*cols)]
        q = [qkv_ref[h, _chunk_rows(j), :] for j, h in probs]
        k = [qkv_ref[HEADS + h, _chunk_rows(j), :] for j, h in probs]
        v = [qkv_ref[2 * HEADS + h, _chunk_rows(j), :] for j, h in probs]
        bv = [b_ref[h, _chunk_rows(j), :] for j, h in probs]
        dov = [do_ref[h, _chunk_rows(j), :] for j, h in probs]
        dvn_ = [dvn_ref[h, _chunk_rows(j), :] for j, h in probs]
        vn_ = [vn_ref[h, _chunk_rows(j), :] for j, h in probs]
        tm = [tm_ref[j, h] for j, h in probs]
        s = [st_ref[j, h] for j, h in probs]
        dsn = [dsn_ref[j, h] for j, h in probs]
        dec = [_chunk_decay(g_ref[h, _chunk_rows(j), :], incl) for j, h in probs]
        decay = [d[2] for d in dec]
        egc = [jnp.exp(d[0]) for d in dec]
        ekg = [jnp.exp(d[1] - d[0]) for d in dec]
        kb = z(lambda a, b: a * b, k, bv)
        vb = z(lambda a, b: a * b, v, bv)
        kbg = z(lambda a, b: a * b, kb, egc)
        qg = z(lambda a, b: a * b, q, egc)
        kg = z(lambda a, b: a * b, k, ekg)
        kk = z(lambda a, b: _dot(a, b, NT), kb, k)
        qk = z(lambda a, b: _dot(a, b, NT), q, k)
        dattn = z(lambda a, b: jnp.where(incl, _dot(a, b, NT), 0.0), dov, vn_)
        dqg = z(lambda a, b: _dot(a, b, NT), dov, s)
        dkg = z(lambda a, b: _dot(a, b, NT), vn_, dsn)
        dglast = z(lambda a, b, c, d, e: _sum_all(a * b) * jnp.exp(e[1]) + _sum_all(c * d), s, dsn, dkg, kg, dec)
        dw = z(lambda a, b: -_dot(a, b, NT), dvn_, s)
        dtm = z(lambda a, b, c, d: _dot(a, b, NT) + _dot(c, d, NT), dvn_, vb, dw, kbg)
        dvb = z(lambda a, b: _dot(a, b, TN), tm, dvn_)
        dkbg = z(lambda a, b: _dot(a, b, TN), tm, dw)
        dlow = z(lambda a, b: jnp.where(strict, -_dot(_dot(a, b, TN), a, NT), 0.0), tm, dtm)
        dkk = z(lambda a, b: a * b, dlow, decay)
        dqk = z(lambda a, b: a * b, dattn, decay)
        dkb = z(lambda a, b, c, d: _dot(a, b, NN) + c * d, dkk, k, dkbg, egc)
        dk = z(lambda a, b, c, d, e, f, g_, h_: _dot(a, b, TN) + _dot(c, d, TN) + e * f + g_ * h_, dkk, kb, dqk, q, dkg, ekg, dkb, bv)
        dq = z(lambda a, b, c, d: _dot(a, b, NN) + c * d, dqk, k, dqg, egc)
        m = z(lambda a, b, c, d, e: (a * b + c * d) * e, dlow, kk, dattn, qk, decay)
        mcol = [_dot(mh, ones, TN) + _dot(ml, ones, TN) for mh, ml in (_split(a) for a in m)]
        for i, (j, h) in enumerate(probs):
            rows = _chunk_rows(j)
            dqkv_ref[h, rows, :] = dq[i]
            dqkv_ref[HEADS + h, rows, :] = dk[i]
            dqkv_ref[2 * HEADS + h, rows, :] = dvb[i] * bv[i]
            db_ref[h, rows, :] = jnp.broadcast_to(_rowsum(dkb[i] * k[i] + dvb[i] * v[i]), (CHUNK, LANE))
            dgc = (_rowsum(dqg[i] * qg[i] + dkbg[i] * kbg[i] - dkg[i] * kg[i]) + _rowsum(m[i]) - mcol[i]
                   + jnp.where(last_row, dglast[i], 0.0))
            dg_ref[h, rows, :] = _suffix_sum_rows(dgc)

    act = lambda heads: pl.BlockSpec((heads, per * CHUNK, LANE), lambda n: (0, n, 0))
    mat = lambda d: pl.BlockSpec((per, HEADS, d, d), lambda n: (n, 0, 0, 0))
    out = jax.ShapeDtypeStruct((HEADS, t, LANE), F32)
    return _pcall(
        body, name=name, out_shape=(jax.ShapeDtypeStruct((3 * HEADS, t, LANE), F32), out, out), grid=(n_chunks // per,),
        in_specs=[act(3 * HEADS), act(HEADS), act(HEADS), mat(CHUNK), mat(LANE), mat(LANE), act(HEADS), act(HEADS), act(HEADS)],
        out_specs=(act(3 * HEADS), act(HEADS), act(HEADS)), semantics=("parallel",),
        vmem_limit=VMEM_LIMIT)(qkv, g, beta, tmats, states, dstates, do, dvn, vn)


ANY = pl.BlockSpec(memory_space=pl.ANY)
PEERS = N_DEV - 1


def _all_gather(arrays, *, name):
    n = len(arrays)

    def body(*refs):
        ins, outs = refs[:n], refs[n:2 * n]
        send_sems, recv_sems, local_sems = refs[2 * n:]
        x, y, c = lax.axis_index("x"), lax.axis_index("y"), lax.axis_index("c")
        me, sibling = (x, y, c), (x, y, 1 - c)
        chips = [(1 - x, y), (x, 1 - y), (1 - x, 1 - y)]

        def copy(a, k, block, to, src=None):
            dst = outs[a].at[4 * block[0] + 2 * block[1] + block[2]]
            return pltpu.make_async_remote_copy(src_ref=dst if src is None else src, dst_ref=dst, send_sem=send_sems.at[a * PEERS + k],
                                                recv_sem=recv_sems.at[a * PEERS + k], device_id=to, device_id_type=MESH)

        local = [pltpu.make_async_copy(ins[a], outs[a].at[4 * x + 2 * y + c], local_sems.at[a]) for a in range(n)]
        for cp in local:
            cp.start()
        first = []
        for a in range(n):
            first += [copy(a, 1 + j, me, (*chip, c), src=ins[a]) for j, chip in enumerate(chips)]
            first.append(copy(a, 0, me, sibling, src=ins[a]))
        for cp in first:
            cp.start()
        passed = []
        for a in range(n):
            for j, chip in enumerate(chips):
                copy(a, 1 + j, (*chip, c), me).wait_recv()
                fwd = copy(a, 4 + j, (*chip, c), sibling)
                fwd.start()
                passed.append(fwd)
        for a in range(n):
            copy(a, 0, sibling, me).wait_recv()
            for j, chip in enumerate(chips):
                copy(a, 4 + j, (*chip, 1 - c), me).wait_recv()
        for cp in first + passed:
            cp.wait_send()
        for cp in local:
            cp.wait()

    return _pcall(body, name=name, out_shape=tuple(jax.ShapeDtypeStruct((N_DEV,) + a.shape, a.dtype) for a in arrays),
                  in_specs=[ANY] * n, out_specs=(ANY,) * n,
                  scratch_shapes=[pltpu.SemaphoreType.DMA((n * PEERS,)), pltpu.SemaphoreType.DMA((n * PEERS,)),
                                  pltpu.SemaphoreType.DMA((n,))])(*arrays)


CHIPS = 4


def _pair_exchange(arrays, *, name, after=None):
    n = len(arrays)

    def body(*refs):
        ins, outs = refs[:n], refs[n:2 * n]
        send_sems, recv_sems = refs[2 * n:]
        x, y, c = lax.axis_index("x"), lax.axis_index("y"), lax.axis_index("c")
        copies = []
        for a in range(n):
            for q in range(CHIPS):
                cp = pltpu.make_async_remote_copy(src_ref=ins[a].at[2 * q + 1 - c], dst_ref=outs[a].at[q],
                                                  send_sem=send_sems.at[a * CHIPS + q], recv_sem=recv_sems.at[a * CHIPS + q],
                                                  device_id=(x, y, 1 - c), device_id_type=MESH)
                cp.start()
                copies.append(cp)
        for cp in copies:
            cp.wait()

    return _pcall(body, name=name, out_shape=tuple(jax.ShapeDtypeStruct((CHIPS,) + a.shape[1:], a.dtype) for a in arrays),
                  in_specs=[ANY] * n, out_specs=(ANY,) * n,
                  scratch_shapes=[pltpu.SemaphoreType.DMA((n * CHIPS,)), pltpu.SemaphoreType.DMA((n * CHIPS,))],
                  after=after)(*arrays)


def _pair_add(blocks, theirs, *, name):
    _, r, c_ = blocks.shape
    tr = _tile(r, 512, 16)

    def body(mine_ref, theirs_ref, o_ref):
        core = lax.axis_index("c")
        own = jnp.where(core == 0, mine_ref[0, 0].astype(F32), mine_ref[0, 1].astype(F32))
        o_ref[0] = (own + theirs_ref[0].astype(F32)).astype(o_ref.dtype)

    spec = pl.BlockSpec((1, tr, c_), lambda q, i: (q, i, 0))
    return _pcall(body, name=name, out_shape=jax.ShapeDtypeStruct(theirs.shape, theirs.dtype), grid=(CHIPS, r // tr),
                  in_specs=[pl.BlockSpec((1, 2, tr, c_), lambda q, i: (q, 0, i, 0)), spec], out_specs=spec,
                  semantics=("parallel", "parallel"), vmem_limit=VMEM_LIMIT)(blocks.reshape(CHIPS, 2, r, c_), theirs)


HBM = pl.BlockSpec(memory_space=pltpu.HBM)
SEM = pl.BlockSpec(memory_space=pltpu.SEMAPHORE)
EFFECT = pltpu.SideEffectType.DATAFLOW_SIDE_EFFECTING


GATHER, CHIP_GATHER, CHIP_SCATTER, PAIR_SCATTER = "gather", "chip_gather", "chip_scatter", "pair_scatter"
PEERS_OF = {GATHER: N_DEV - 1, CHIP_GATHER: CHIPS - 1, CHIP_SCATTER: CHIPS - 1, PAIR_SCATTER: CHIPS}


def _direct_copies(srcs, lands, send_sems, recv_sems, local_sems, kind):
    x, y, c = lax.axis_index("x"), lax.axis_index("y"), lax.axis_index("c")
    peers = PEERS_OF[kind]
    mine = 2 * x + y if kind == CHIP_SCATTER else 4 * x + 2 * y + c
    copies = []
    for a, (src, land) in enumerate(zip(srcs, lands)):
        if kind == PAIR_SCATTER:
            copies += [pltpu.make_async_remote_copy(
                src_ref=src.at[2 * q + 1 - c], dst_ref=land.at[q], send_sem=send_sems.at[a * CHIPS + q],
                recv_sem=recv_sems.at[a * CHIPS + q], device_id=(x, y, 1 - c), device_id_type=MESH) for q in range(CHIPS)]
            continue
        copies.append(pltpu.make_async_copy(src.at[mine] if kind == CHIP_SCATTER else src, land.at[mine], local_sems.at[a]))
        for k in range(1, peers + 1):
            bits = k if kind == GATHER else 2 * k
            px = 1 - x if bits & 4 else x
            py = 1 - y if bits & 2 else y
            pc = 1 - c if bits & 1 else c
            copies.append(pltpu.make_async_remote_copy(
                src_ref=src.at[2 * px + py] if kind == CHIP_SCATTER else src, dst_ref=land.at[mine],
                send_sem=send_sems.at[a * peers + k - 1], recv_sem=recv_sems.at[a * peers + k - 1],
                device_id=(px, py, pc), device_id_type=MESH))
    return copies


def _pair_swap(arrays, *, name):
    n = len(arrays)

    def body(*refs):
        mine, zones = refs[:n], refs[n:2 * n]
        send_sems, recv_sems = refs[2 * n:]
        x, y, c = lax.axis_index("x"), lax.axis_index("y"), lax.axis_index("c")
        copies = []
        for a in range(n):
            for q in range(CHIPS):
                copies.append(pltpu.make_async_remote_copy(
                    src_ref=mine[a].at[2 * q + c], dst_ref=zones[a].at[2 * q + c], send_sem=send_sems.at[a * CHIPS + q],
                    recv_sem=recv_sems.at[a * CHIPS + q], device_id=(x, y, 1 - c), device_id_type=MESH))
        for cp in copies:
            cp.start()
        for cp in copies:
            cp.wait()

    return _pcall(body, name=name, out_shape=tuple(jax.ShapeDtypeStruct(a.shape, a.dtype) for a in arrays),
                  in_specs=[ANY] * n, out_specs=(ANY,) * n, input_output_aliases={i: i for i in range(n)},
                  scratch_shapes=[pltpu.SemaphoreType.DMA((n * CHIPS,)), pltpu.SemaphoreType.DMA((n * CHIPS,))])(*arrays)


def _exchange_start(groups, kind, *, name, after=None):
    srcs = [s for group in groups for s in group]
    n = len(srcs)
    sizes = [len(group) for group in groups]
    starts = [sum(sizes[:g]) for g in range(len(groups))]
    land_shapes = [{CHIP_SCATTER: s.shape, PAIR_SCATTER: (CHIPS,) + s.shape[1:]}.get(kind, (N_DEV,) + s.shape) for s in srcs]
    peers = PEERS_OF[kind]
    extra = [] if after is None else [after]

    def body(*refs):
        srcs_, lands = refs[:n], refs[n:2 * n]
        token = refs[-1]
        sem_refs = refs[2 * n + len(extra):]
        for g, (at, size) in enumerate(zip(starts, sizes)):
            send_sems, recv_sems, local_sems = sem_refs[3 * g:3 * g + 3]
            for cp in _direct_copies(srcs_[at:at + size], lands[at:at + size], send_sems, recv_sems, local_sems, kind):
                cp.start()
        token[...] = jnp.zeros_like(token)

    sems = tuple(t for size in sizes for t in (pltpu.SemaphoreType.DMA((size * peers,)), pltpu.SemaphoreType.DMA((size * peers,)),
                                               pltpu.SemaphoreType.DMA((size,))))
    thru = tuple(pltpu.HBM(s.shape, s.dtype) for s in srcs) + tuple(pltpu.HBM(shp, s.dtype) for shp, s in zip(land_shapes, srcs))
    ins = [pltpu.with_memory_space_constraint(s, pltpu.HBM) for s in srcs]
    ins += [pltpu.with_memory_space_constraint(lax.empty(shp, s.dtype), pltpu.HBM) for shp, s in zip(land_shapes, srcs)]
    out = pl.pallas_call(
        body, name=name, out_shape=sems + thru + (jax.ShapeDtypeStruct((SUBLANE, LANE), F32),),
        in_specs=[HBM] * (2 * n) + [ANY] * len(extra),
        out_specs=(SEM,) * len(sems) + (HBM,) * (2 * n) + (pl.BlockSpec(memory_space=pltpu.VMEM),),
        input_output_aliases={i: len(sems) + i for i in range(2 * n)},
        compiler_params=pltpu.CompilerParams(has_side_effects=EFFECT))(*ins, *extra)
    arrays = out[len(sems):-1]
    started = [tuple(out[3 * g:3 * g + 3]) + tuple(arrays[at:at + size]) + tuple(arrays[n + at:n + at + size])
               for g, (at, size) in enumerate(zip(starts, sizes))]
    return started, out[-1]


def _exchange_wait(started, after, kind, *, name):
    n = (len(started) - 3) // 2
    sems, arrays = started[:3], started[3:]

    def body(*refs):
        srcs_, lands = refs[:n], refs[n:2 * n]
        send_sems, recv_sems, local_sems = refs[2 * n:2 * n + 3]
        for cp in _direct_copies(srcs_, lands, send_sems, recv_sems, local_sems, kind):
            cp.wait()

    out = pl.pallas_call(
        body, name=name, out_shape=tuple(pltpu.HBM(a.shape, a.dtype) for a in arrays),
        in_specs=[HBM] * (2 * n) + [SEM] * 3 + [ANY], out_specs=(HBM,) * (2 * n),
        input_output_aliases={i: i for i in range(2 * n)},
        compiler_params=pltpu.CompilerParams(has_side_effects=EFFECT))(*arrays, *sems, after)
    return out[n:]


def _adamw_reduce(w, parts, m, v, *, name, after=None):
    layers, r, c = w.shape
    assert len(parts) == layers
    senders = parts[0].shape[0]
    tr = _tile(r, 512, 16)
    tiles = r // tr
    bc1 = 1.0 - ADAM_B1 ** ADAM_STEP
    bc2 = 1.0 - ADAM_B2 ** ADAM_STEP

    def body(w_ref, *rest):
        p_refs = rest[:layers]
        m_ref, v_ref, g_ref, d_ref, nm_ref, nv_ref = rest[layers:]

        def update(p_ref):
            g = p_ref[0, :, pl.ds(0, c)].astype(F32)
            for s in range(1, senders):
                g = g + p_ref[s, :, pl.ds(0, c)].astype(F32)
            nm = ADAM_B1 * m_ref[0] + (1.0 - ADAM_B1) * g
            nv = ADAM_B2 * v_ref[0] + (1.0 - ADAM_B2) * (g * g)
            g_ref[0] = g
            nm_ref[0] = nm
            nv_ref[0] = nv
            d_ref[0] = -ADAM_LR * ((nm / bc1) / (jnp.sqrt(nv / bc2) + ADAM_EPS) + ADAM_WD * w_ref[0])

        for layer in range(layers):
            pl.when(pl.program_id(0) == layer)(functools.partial(update, p_refs[layer]))

    def part_spec(layer, shape):
        rest = 0 if layer > 0 else tiles - 1
        return pl.BlockSpec((senders, tr, shape[2]), lambda l, i: (0, jnp.where(l == layer, i, rest), 0))

    spec = pl.BlockSpec((1, tr, c), lambda l, i: (l, i, 0))
    out = jax.ShapeDtypeStruct((layers, r, c), F32)
    return _pcall(body, name=name, out_shape=(out,) * 4, grid=(layers, tiles),
                  in_specs=[spec] + [part_spec(layer, p.shape) for layer, p in enumerate(parts)] + [spec, spec],
                  out_specs=(spec,) * 4, semantics=("arbitrary", "arbitrary"), vmem_limit=VMEM_LIMIT, after=after)(w, *parts, m, v)


def _pool_windows():
    return jnp.repeat(jnp.asarray(POOL_WINDOWS, F32), POOL_DIM // len(POOL_WINDOWS))[None, :]


def _block_diag_pairs(pool_w):
    z = jnp.zeros_like(pool_w[0])
    return jnp.stack([jnp.block([[pool_w[2 * b], z], [z, pool_w[2 * b + 1]]]) for b in range(2)])


def _pad_lanes(vec):
    return jnp.zeros((1, LANE), F32).at[0, :vec.shape[0]].set(vec)


FF_SHARD = D_FF // N_DEV
FF_BLOCK = 384


def _layer_fwd(x, p_i, wt, fetch):
    wt = {**wt, **fetch(0, x)}
    proj, h1 = _matmul(x, wt["w_in"], "nt", norm_g=wt["norm1_g"], name="mm_in")
    qkv = _qkv_prep_fwd(proj, wt["conv_qkv"], name="qkv_prep_fwd")
    g, beta = _gates_fwd(proj, wt["a_log"], wt["dt_bias"], name="gates_fwd")
    u, w, qg, kg, attn, tmats = _deltanet_prep(qkv, g, beta, name="deltanet_prep")
    o, vn, states = _deltanet_scan(u, w, qg, kg, attn, g, name="deltanet_scan")
    o_a = _apost_fwd(o, proj, wt["onorm_g"], name="apost_fwd")
    o_b = _pool_fwd(proj, wt["pool_win"], wt["pool_wbd"], wt["pool_scale"], name="pool_fwd")
    o_c = _sconv_fwd(proj, wt["sconv_w"], name="sconv_fwd")
    mixed = jnp.concatenate([o_a, o_b, o_c], axis=1)
    wt.update(fetch(1, mixed))
    x1 = _matmul(mixed, wt["w_out"], "nn", res=x, name="mm_out", after=wt.get("behind"))
    wt.update(fetch(2, x1))
    ff, gate, up, h2 = _swiglu_fwd(x1, wt["norm2_g"], wt["w_gate"], wt["w_up"], name="swiglu_fwd")
    wt.update(fetch(3, ff))
    x2 = _matmul(ff, wt["w_down"], "nn", res=x1, name="mm_down")
    wt.update(fetch(4, x2))
    x3, pgl, pp = _ple_fwd(x2, p_i, wt["ple_gate"], wt["ple_proj"], name="ple_fwd")
    saved = dict(x=x, h1=h1, proj=proj, qkv=qkv, g=g, beta=beta, o=o, states=states, tmats=tmats, mixed=mixed, x1=x1, h2=h2,
                 gate=gate, up=up, ff=ff, x2=x2, pgl=pgl, pp=pp, p=p_i, w=w, qg=qg, kg=kg, attn=attn, vn=vn, wt=wt)
    return x3, saved


def _col_blocks(g):
    a = g.shape[0]
    return jnp.transpose(g.reshape(a, N_DEV, -1), (1, 0, 2))


def _cols_joined(blocks):
    return jnp.transpose(blocks, (1, 0, 2)).reshape(blocks.shape[1], -1)


def _layer_bwd(dx3, sv, emit, after=None):
    gr, big = {}, {}
    wt = sv["wt"]
    rows = D_MODEL // N_DEV
    dpgl, dpp = _ple_bwd(dx3, sv["pgl"], sv["pp"], name="ple_bwd", after=after)
    big["ple_proj"] = _matmul(sv["p"], dpp, "tn", out_blocked=(N_DEV, rows), out_dtype=BF16, name="mm_dplep")
    big["ple_gate"] = _matmul(sv["x2"], dpgl, "tn", out_dtype=BF16, name="mm_dpleg").reshape(N_DEV, rows, D_MODEL)
    dx2 = _matmul(dpgl, wt["ple_gate"], "nt", res=dx3, name="mm_dx2")
    big["w_down"] = _matmul(sv["ff"], dx2, "tn", out_dtype=BF16, name="mm_ddown").reshape(N_DEV, FF_BLOCK, D_MODEL)
    dgate, dup = _swiglu_bwd(dx2, wt["w_down"], sv["gate"], sv["up"], name="swiglu_bwd", after=emit(0, big))
    big["w_gate"] = _matmul(dgate, sv["h2"], "tn", out_dtype=BF16, name="mm_dgate").reshape(N_DEV, FF_BLOCK, D_MODEL)
    big["w_up"] = _matmul(dup, sv["h2"], "tn", out_dtype=BF16, name="mm_dup").reshape(N_DEV, FF_BLOCK, D_MODEL)
    dx1, gr["norm2_g"] = _matmul_norm_bwd(dgate, wt["w_gate"], sv["x1"], wt["norm2_g"], dx2, more=(dup, wt["w_up"]), name="mm_dh2")
    big["w_out"] = _matmul(sv["mixed"], dx1, "tn", out_dtype=BF16, name="mm_dout").reshape(N_DEV, rows, D_MODEL)
    dmixed = _matmul(dx1, wt["w_out"], "nt", name="mm_dmixed", after=emit(1, big))
    proj = sv["proj"]
    dcb, dcc, dch, dsconv = _sconv_bwd(proj, wt["sconv_w"], dmixed, name="sconv_bwd")
    big["sconv_w"] = _col_blocks(dsconv)
    dhp, dwbd, gr["pool_scale"] = _pool_bwd(proj, wt["pool_win"], wt["pool_wbd"], wt["pool_scale"], dmixed, name="pool_bwd")
    half = LANE // 2
    gr["pool_w"] = jnp.stack([dwbd[0, :half, :half], dwbd[0, half:, half:], dwbd[1, :half, :half], dwbd[1, half:, half:]])
    do, dz, gr["onorm_g"] = _apost_bwd(sv["o"], proj, wt["onorm_g"], dmixed, name="apost_bwd")
    dvn, dstates = _deltanet_bscan(sv["w"], sv["qg"], sv["kg"], sv["attn"], sv["g"], do, name="deltanet_bscan")
    dqkv_h, dg, dbeta = _deltanet_post(sv["qkv"], sv["g"], sv["beta"], sv["tmats"], sv["states"], dstates, do, dvn, sv["vn"],
                                       name="deltanet_post")
    dab, dalog, ddtb = _gates_bwd(proj, wt["a_log"], wt["dt_bias"], dg, dbeta, name="gates_bwd")
    gr["a_log"], gr["dt_bias"] = dalog[0, :HEADS], ddtb[0, :HEADS]
    dqkv, dconv = _qkv_prep_bwd(proj, wt["conv_qkv"], dqkv_h, name="qkv_prep_bwd")
    big["conv_qkv"] = _col_blocks(dconv)
    dproj = jnp.concatenate([dqkv, dz, dab, dhp, dcb, dcc, dch], axis=1)
    dwin = _matmul(dproj, sv["h1"], "tn", out_dtype=BF16, name="mm_din")
    big["w_in"] = jnp.concatenate([dwin[:AB_COL + 2 * HEADS], dwin[AB_COL + LANE:]], axis=0).reshape(N_DEV, -1, D_MODEL)
    dx, gr["norm1_g"] = _matmul_norm_bwd(dproj, wt["w_in"], sv["x"], wt["norm1_g"], dx1, name="mm_dh1", after=emit(2, big))
    return dx, gr


FETCH_GROUPS = (("w_in", "conv_qkv", "sconv_w"), ("w_out",), ("w_gate", "w_up"), ("w_down",), ("ple_gate", "ple_proj"))
EMIT_GROUPS = (("ple_proj", "ple_gate", "w_down"), ("w_gate", "w_up", "w_out"), ("w_in", "conv_qkv", "sconv_w"))


def _small_weights(w, i):
    return dict(
        norm1_g=w["norm1_g"][i][None], norm2_g=w["norm2_g"][i][None], onorm_g=w["onorm_g"][i][None],
        a_log=_pad_lanes(w["a_log"][i]), dt_bias=_pad_lanes(w["dt_bias"][i]),
        pool_scale=w["pool_scale"][i][None], pool_win=_pool_windows(), pool_wbd=_block_diag_pairs(w["pool_w"][i]))


def _as_read(name, gathered):
    if name == "w_in":
        rows = gathered[:, :D_IN // N_DEV].reshape(-1, D_MODEL)
        return jnp.concatenate([rows[:AB_COL + 2 * HEADS], jnp.zeros((LANE - 2 * HEADS, D_MODEL), BF16),
                                rows[AB_COL + 2 * HEADS:]], axis=0)
    if name in ("conv_qkv", "sconv_w"):
        return _cols_joined(gathered)
    if name == "ple_proj":
        return gathered
    return gathered.reshape(-1, D_MODEL)


SHARDED = ("w_in", "w_gate", "w_up", "w_down", "w_out", "ple_gate", "ple_proj", "conv_qkv", "sconv_w")
SMALL = ("norm1_g", "a_log", "dt_bias", "onorm_g", "pool_w", "pool_scale", "norm2_g", "final_g")
SLAB_COLS = 1024


def _payload(name, shard):
    if name in ("conv_qkv", "sconv_w"):
        return shard
    out = shard.astype(BF16)
    if name in ("w_gate", "w_up", "w_down"):
        out = jnp.pad(out, ((0, FF_BLOCK - FF_SHARD), (0, 0)))
    if name == "w_in":
        out = jnp.pad(out, ((0, -out.shape[0] % (2 * SUBLANE)), (0, 0)))
    return out


TRANSPOSED = ("w_in", "w_gate", "w_up")


def _ff_rows(t):
    return jnp.transpose(t, (0, 2, 1))


def _slab_rows(shape):
    size = 1
    for s in shape:
        size *= s
    return SUBLANE * -(-size // (SUBLANE * SLAB_COLS))


def _pack_slab(parts, extra_row):
    rows = []
    for name in SMALL:
        flat = parts[name].reshape(-1)
        nrow = _slab_rows(parts[name].shape)
        rows.append(jnp.pad(flat, (0, nrow * SLAB_COLS - flat.shape[0])).reshape(nrow, SLAB_COLS))
    rows.append(jnp.pad(extra_row, ((0, SUBLANE - 1), (0, 0))))
    return jnp.concatenate(rows, axis=0)


def _unpack_slab(slab, shapes):
    out, row = {}, 0
    for name in SMALL:
        size = 1
        for s in shapes[name]:
            size *= s
        out[name] = slab[row:row + _slab_rows(shapes[name])].reshape(-1)[:size].reshape(shapes[name])
        row += _slab_rows(shapes[name])
    return out, row


def kernel(x, p, norm1_g, w_in, conv_qkv, a_log, dt_bias, onorm_g, pool_w, pool_scale, sconv_w, w_out, norm2_g, w_gate, w_up, w_down, ple_proj, ple_gate, final_g, loss_target, m_norm1_g, m_w_in, m_conv_qkv, m_a_log, m_dt_bias, m_onorm_g, m_pool_w, m_pool_scale, m_sconv_w, m_w_out, m_norm2_g, m_w_gate, m_w_up, m_w_down, m_ple_proj, m_ple_gate, m_final_g, v_norm1_g, v_w_in, v_conv_qkv, v_a_log, v_dt_bias, v_onorm_g, v_pool_w, v_pool_scale, v_sconv_w, v_w_out, v_norm2_g, v_w_gate, v_w_up, v_w_down, v_ple_proj, v_ple_gate, v_final_g):
    names = ["norm1_g", "w_in", "conv_qkv", "a_log", "dt_bias", "onorm_g", "pool_w", "pool_scale", "sconv_w", "w_out", "norm2_g",
             "w_gate", "w_up", "w_down", "ple_proj", "ple_gate", "final_g"]
    w = dict(zip(names, [norm1_g, w_in, conv_qkv, a_log, dt_bias, onorm_g, pool_w, pool_scale, sconv_w, w_out, norm2_g, w_gate, w_up,
                         w_down, ple_proj, ple_gate, final_g]))
    m = dict(zip(names, [m_norm1_g, m_w_in, m_conv_qkv, m_a_log, m_dt_bias, m_onorm_g, m_pool_w, m_pool_scale, m_sconv_w, m_w_out,
                         m_norm2_g, m_w_gate, m_w_up, m_w_down, m_ple_proj, m_ple_gate, m_final_g]))
    v = dict(zip(names, [v_norm1_g, v_w_in, v_conv_qkv, v_a_log, v_dt_bias, v_onorm_g, v_pool_w, v_pool_scale, v_sconv_w, v_w_out,
                         v_norm2_g, v_w_gate, v_w_up, v_w_down, v_ple_proj, v_ple_gate, v_final_g]))
    w.update({k: _ff_rows(w[k]) for k in TRANSPOSED})

    first, rest = FETCH_GROUPS[0], tuple(k for members in FETCH_GROUPS[1:] for k in members)
    gathered = dict(zip(first, _all_gather([_payload(k, w[k][0]) for k in first], name="all_gather_weights")))
    (flying0,), token = _exchange_start([[_payload(k, w[k][0]) for k in rest]], CHIP_GATHER, name="gather_start_0",
                                        after=gathered[first[0]])
    replicated = [_small_weights(w, i) for i in range(DEPTH)]
    replicated[0]["norm1_g"] = replicated[0]["norm1_g"] + token[0, 0]
    for group in (m, v):
        group.update({k: _ff_rows(group[k] + token[0, 0]) for k in TRANSPOSED})
    flying = {}

    def fetch(i, group, after):
        extra = {}
        if i == 0 and group == 1:
            landed = _exchange_wait(flying0, after, CHIP_GATHER, name="gather_wait_0")
            gathered.update(zip(rest, _pair_swap(landed, name="pair_swap")))
            (flying["first"],), token = _exchange_start([[_payload(k, w[k][1]) for k in first]], CHIP_GATHER,
                                                        name="gather_start_1_first", after=gathered[rest[0]])
            extra = {"behind": token}
        if i == 1 and group == 0:
            landed = _exchange_wait(flying["first"], after, CHIP_GATHER, name="gather_wait_1_first")
            gathered.update(zip(first, _pair_swap(landed, name="pair_swap")))
            (flying["rest"],), token = _exchange_start([[_payload(k, w[k][1]) for k in rest]], CHIP_GATHER,
                                                       name="gather_start_1_rest", after=gathered[first[0]])
            extra = {"norm1_g": replicated[1]["norm1_g"] + token[0, 0]}
        if i == 1 and group == 1:
            landed = _exchange_wait(flying["rest"], after, CHIP_GATHER, name="gather_wait_1_rest")
            gathered.update(zip(rest, _pair_swap(landed, name="pair_swap")))
        return {**{k: _as_read(k, gathered[k]) for k in FETCH_GROUPS[group]}, **extra}

    def reduce_scatter_start(members, blocks, tag, after=None):
        mine = [blocks[k] for k in members]
        theirs = _pair_exchange(mine, name="pair_exchange", after=after)
        sums = [_pair_add(a, b, name="pair_add") for a, b in zip(mine, theirs)]
        (started,), token = _exchange_start([sums], CHIP_SCATTER, name="exchange_start_" + tag)
        return started, token

    h, saved0 = _layer_fwd(x[0], p[0, 0], replicated[0], functools.partial(fetch, 0))
    h, saved1 = _layer_fwd(h, p[1, 0], replicated[1], functools.partial(fetch, 1))
    dx, dgf, loss_part = _loss_head(h, final_g[None], loss_target[0], name="loss_head")
    small, big1, flying0 = [None] * DEPTH, {}, []
    dx, small[1] = _layer_bwd(dx, saved1, lambda group, blocks: big1.update({k: blocks[k] for k in EMIT_GROUPS[group]}))
    mine1 = [big1[k] for k in SHARDED]
    (pair_flying,), token = _exchange_start([mine1], PAIR_SCATTER, name="pair_start_1")
    flying1 = []

    def emit(group, blocks):
        if group == 0:
            theirs = _exchange_wait(pair_flying, blocks[EMIT_GROUPS[0][0]], PAIR_SCATTER, name="pair_wait_1")
            sums = [_pair_add(a, b, name="pair_add") for a, b in zip(mine1, theirs)]
            started1, behind = _exchange_start([sums], CHIP_SCATTER, name="exchange_start_1")
            flying1.extend(started1)
        started, token = reduce_scatter_start(EMIT_GROUPS[group], blocks, f"0_{group}", after=behind if group == 0 else None)
        flying0.append(started)
        return token

    dx, small[0] = _layer_bwd(dx, saved0, emit, after=token)
    received = [{}, dict(zip(SHARDED, _exchange_wait(flying1[0], dx, CHIP_SCATTER, name="exchange_wait_1")))]
    for group, members in enumerate(EMIT_GROUPS):
        received[0].update(zip(members, _exchange_wait(flying0[group], dx, CHIP_SCATTER, name=f"exchange_wait_0_{group}")))

    grads = {k: jnp.stack([small[i][k] for i in range(DEPTH)]) for k in small[0]}
    grads = {k: g[:, 0] if k in ("norm1_g", "norm2_g", "onorm_g", "pool_scale") else g for k, g in grads.items()}
    grads["final_g"] = dgf[0]
    loss_row = jnp.pad(loss_part, ((0, 0), (0, SLAB_COLS - LANE)))
    (small_flying,), token = _exchange_start([[_pack_slab(grads, loss_row)]], GATHER, name="small_gather_start")

    out_g, out_d, out_m, out_v = {}, {}, {}, {}
    for k in SHARDED:
        out_g[k], out_d[k], out_m[k], out_v[k] = _adamw_reduce(w[k], [received[i][k] for i in range(DEPTH)], m[k], v[k],
                                                                name="adamw_" + k, after=token)
    behind_all = jnp.stack([out_v[k][0, 0, 0] for k in SHARDED])
    (small_parts,) = _exchange_wait(small_flying, behind_all, GATHER, name="small_gather_wait")
    zero_row = jnp.zeros((1, SLAB_COLS), F32)
    slabs = _adamw_reduce(_pack_slab(w, zero_row)[None], [small_parts], _pack_slab(m, zero_row)[None],
                          _pack_slab(v, zero_row)[None], name="adamw_small")
    slabs = [s[0] for s in slabs]
    shapes = {k: w[k].shape for k in SMALL}
    for dst, slab in zip((out_g, out_d, out_m, out_v), slabs):
        vals, _ = _unpack_slab(slab, shapes)
        dst.update(vals)
    _, loss_at = _unpack_slab(slabs[0], shapes)
    loss = slabs[0][loss_at, 0]
    for group in (out_g, out_d, out_m, out_v):
        group.update({k: _ff_rows(group[k]) for k in TRANSPOSED})

    return (loss, dx[None], *[out_g[k] for k in names], *[out_d[k] for k in names], *[out_m[k] for k in names],
            *[out_v[k] for k in names])
```
